```python
import jax, jax.numpy as jnp
from jax import lax
import numpy as np

D_MODEL = 1024
BATCH = 8
SEQ = 8192
DEPTH = 2

D_MIX = D_MODEL
HEAD_DIM = 64
N_ATTN_HEADS = 8
ATTN_DIM = N_ATTN_HEADS * HEAD_DIM
N_CONV_GROUPS = 8
CONV_DIM = D_MIX - ATTN_DIM
CONV_WIDTH = 3
IN_COLS = 3 * ATTN_DIM + 3 * CONV_DIM
D_FF = ((8 * D_MODEL // 3 + 255) // 256) * 256
Q_BLOCK = 128
EPS = 1e-6

kernel_name = "hymba_stickbreak_shortconv_swiglu"


def _rms_norm(x, gain):
    x32 = x.astype(jnp.float32)
    y = x32 * lax.rsqrt(jnp.mean(x32 * x32, axis=-1, keepdims=True) + EPS)
    return (y * gain.astype(jnp.float32)).astype(x.dtype)


def _stick_breaking_attention(q, k, v):
    b, h, s, dh = q.shape
    nb = s // Q_BLOCK
    scale = dh ** -0.5
    k32 = k.astype(jnp.float32)
    v32 = v.astype(jnp.float32)
    q_blocks = q.reshape(b, h, nb, Q_BLOCK, dh).transpose(2, 0, 1, 3, 4)
    starts = jnp.arange(nb, dtype=jnp.int32) * Q_BLOCK
    key_pos = jnp.arange(s, dtype=jnp.int32)

    def one_block(args):
        q_blk, start = args
        z = jnp.einsum('bhqd,bhkd->bhqk', q_blk.astype(jnp.float32), k32) * scale
        q_pos = start + jnp.arange(Q_BLOCK, dtype=jnp.int32)
        causal = key_pos[None, :] < q_pos[:, None]
        sp = jax.nn.softplus(z)
        neg_log_rem = jnp.where(causal, sp, 0.0)
        after = lax.cumsum(neg_log_rem, axis=3, reverse=True) - neg_log_rem
        log_a = (z - sp) - after
        a = jnp.where(causal, jnp.exp(log_a), 0.0)
        return jnp.einsum('bhqk,bhkd->bhqd', a, v32)

    out = lax.map(one_block, (q_blocks, starts))
    return out.transpose(1, 2, 0, 3, 4).reshape(b, h, s, dh).astype(q.dtype)


def _short_conv_mixer(b_gate, c_gate, u, conv_w):
    h = c_gate * u
    rhs = conv_w.astype(h.dtype)[:, None, :]
    y = lax.conv_general_dilated(h, rhs, window_strides=(1,), padding=[(CONV_WIDTH - 1, 0)],
                                 dimension_numbers=('NWC', 'WIO', 'NWC'),
                                 feature_group_count=CONV_DIM)
    return b_gate * y


def _fwd_setup_inputs(seed: int = 0) -> dict:
    key = jax.random.key(seed)
    ks = jax.random.split(key, 12)
    f32 = jnp.float32

    def gain(k, shape):
        return jnp.ones(shape, f32) + 0.01 * jax.random.normal(k, shape, f32)

    x = jax.random.normal(ks[0], (BATCH, SEQ, D_MODEL), f32)
    norm_mix = gain(ks[1], (DEPTH, D_MODEL))
    w_in = jax.random.normal(ks[2], (DEPTH, D_MODEL, IN_COLS), f32) * D_MODEL ** -0.5
    q_norm = gain(ks[3], (DEPTH, HEAD_DIM))
    k_norm = gain(ks[4], (DEPTH, HEAD_DIM))
    conv_w = jax.random.normal(ks[5], (DEPTH, CONV_WIDTH, CONV_DIM), f32) * CONV_WIDTH ** -0.5
    w_out = jax.random.normal(ks[6], (DEPTH, D_MIX, D_MODEL), f32) * D_MIX ** -0.5
    norm_ffn = gain(ks[7], (DEPTH, D_MODEL))
    w_gate = jax.random.normal(ks[8], (DEPTH, D_MODEL, D_FF), f32) * D_MODEL ** -0.5
    w_up = jax.random.normal(ks[9], (DEPTH, D_MODEL, D_FF), f32) * D_MODEL ** -0.5
    w_down = jax.random.normal(ks[10], (DEPTH, D_FF, D_MODEL), f32) * D_FF ** -0.5
    return {"x": x, "norm_mix": norm_mix, "w_in": w_in, "q_norm": q_norm, "k_norm": k_norm,
            "conv_w": conv_w, "w_out": w_out, "norm_ffn": norm_ffn, "w_gate": w_gate,
            "w_up": w_up, "w_down": w_down}


def _fwd_reference(x, norm_mix, w_in, q_norm, k_norm, conv_w, w_out, norm_ffn, w_gate, w_up, w_down):
    b, s, _ = x.shape
    splits = [ATTN_DIM, 2 * ATTN_DIM, 3 * ATTN_DIM,
              3 * ATTN_DIM + CONV_DIM, 3 * ATTN_DIM + 2 * CONV_DIM]
    for l in range(DEPTH):
        h = _rms_norm(x, norm_mix[l])
        proj = h @ w_in[l].astype(h.dtype)
        q, k, v, cb, cc, cu = jnp.split(proj, splits, axis=-1)
        q = _rms_norm(q.reshape(b, s, N_ATTN_HEADS, HEAD_DIM), q_norm[l])
        k = _rms_norm(k.reshape(b, s, N_ATTN_HEADS, HEAD_DIM), k_norm[l])
        v = v.reshape(b, s, N_ATTN_HEADS, HEAD_DIM)
        attn = _stick_breaking_attention(q.transpose(0, 2, 1, 3), k.transpose(0, 2, 1, 3),
                                         v.transpose(0, 2, 1, 3))
        attn = attn.transpose(0, 2, 1, 3).reshape(b, s, ATTN_DIM)
        conv = _short_conv_mixer(cb, cc, cu, conv_w[l])
        mix = jnp.concatenate([attn, conv], axis=-1)
        x = x + mix @ w_out[l].astype(mix.dtype)
        h = _rms_norm(x, norm_ffn[l])
        g = h @ w_gate[l].astype(h.dtype)
        u = h @ w_up[l].astype(h.dtype)
        x = x + (jax.nn.silu(g) * u) @ w_down[l].astype(h.dtype)
    return x


import jax as _jax
import jax.numpy as _jnp

TWIN_FORMAT = 'train_step'
FWD_PARAMS = ['x', 'norm_mix', 'w_in', 'q_norm', 'k_norm', 'conv_w', 'w_out', 'norm_ffn', 'w_gate', 'w_up', 'w_down']
TWIN_WEIGHTS = ['norm_mix', 'w_in', 'q_norm', 'k_norm', 'conv_w', 'w_out', 'norm_ffn', 'w_gate', 'w_up', 'w_down']
TWIN_DIFF_INPUT = 'x'
TWIN_INPUTS = ['x', 'norm_mix', 'w_in', 'q_norm', 'k_norm', 'conv_w', 'w_out', 'norm_ffn', 'w_gate', 'w_up', 'w_down', 'loss_target', 'm_norm_mix', 'm_w_in', 'm_q_norm', 'm_k_norm', 'm_conv_w', 'm_w_out', 'm_norm_ffn', 'm_w_gate', 'm_w_up', 'm_w_down', 'v_norm_mix', 'v_w_in', 'v_q_norm', 'v_k_norm', 'v_conv_w', 'v_w_out', 'v_norm_ffn', 'v_w_gate', 'v_w_up', 'v_w_down']
TWIN_OUTPUTS = ['loss', 'grad_x', 'grad_norm_mix', 'grad_w_in', 'grad_q_norm', 'grad_k_norm', 'grad_conv_w', 'grad_w_out', 'grad_norm_ffn', 'grad_w_gate', 'grad_w_up', 'grad_w_down', 'delta_norm_mix', 'delta_w_in', 'delta_q_norm', 'delta_k_norm', 'delta_conv_w', 'delta_w_out', 'delta_norm_ffn', 'delta_w_gate', 'delta_w_up', 'delta_w_down', 'new_m_norm_mix', 'new_m_w_in', 'new_m_q_norm', 'new_m_k_norm', 'new_m_conv_w', 'new_m_w_out', 'new_m_norm_ffn', 'new_m_w_gate', 'new_m_w_up', 'new_m_w_down', 'new_v_norm_mix', 'new_v_w_in', 'new_v_q_norm', 'new_v_k_norm', 'new_v_conv_w', 'new_v_w_out', 'new_v_norm_ffn', 'new_v_w_gate', 'new_v_w_up', 'new_v_w_down']
TWIN_LEAF_KINDS = {'loss': 'loss', 'grad_x': 'grad_x', 'grad_norm_mix': 'grad_w', 'grad_w_in': 'grad_w', 'grad_q_norm': 'grad_w', 'grad_k_norm': 'grad_w', 'grad_conv_w': 'grad_w', 'grad_w_out': 'grad_w', 'grad_norm_ffn': 'grad_w', 'grad_w_gate': 'grad_w', 'grad_w_up': 'grad_w', 'grad_w_down': 'grad_w', 'delta_norm_mix': 'delta_w', 'delta_w_in': 'delta_w', 'delta_q_norm': 'delta_w', 'delta_k_norm': 'delta_w', 'delta_conv_w': 'delta_w', 'delta_w_out': 'delta_w', 'delta_norm_ffn': 'delta_w', 'delta_w_gate': 'delta_w', 'delta_w_up': 'delta_w', 'delta_w_down': 'delta_w', 'new_m_norm_mix': 'new_m', 'new_m_w_in': 'new_m', 'new_m_q_norm': 'new_m', 'new_m_k_norm': 'new_m', 'new_m_conv_w': 'new_m', 'new_m_w_out': 'new_m', 'new_m_norm_ffn': 'new_m', 'new_m_w_gate': 'new_m', 'new_m_w_up': 'new_m', 'new_m_w_down': 'new_m', 'new_v_norm_mix': 'new_v', 'new_v_w_in': 'new_v', 'new_v_q_norm': 'new_v', 'new_v_k_norm': 'new_v', 'new_v_conv_w': 'new_v', 'new_v_w_out': 'new_v', 'new_v_norm_ffn': 'new_v', 'new_v_w_gate': 'new_v', 'new_v_w_up': 'new_v', 'new_v_w_down': 'new_v'}


def _forward(args):
    return _fwd_reference(*[args[k] for k in FWD_PARAMS])


def _output_shape():
    def fwd():
        inp = _fwd_setup_inputs(0)
        return _fwd_reference(*[inp[k] for k in FWD_PARAMS])
    out = _jax.eval_shape(fwd)
    return out.shape, out.dtype

N_MICROBATCH = 1
ADAM_LR = 0.001
ADAM_B1 = 0.9
ADAM_B2 = 0.999
ADAM_EPS = 1e-08
ADAM_WD = 0.01
ADAM_STEP = 10
PER_EXAMPLE_BATCH_AXIS = {'x': 0, 'loss_target': 0}
SHARED_INPUTS = []
_WEIGHT_DTYPES = {'norm_mix': _jnp.float32, 'w_in': _jnp.float32, 'q_norm': _jnp.float32, 'k_norm': _jnp.float32, 'conv_w': _jnp.float32, 'w_out': _jnp.float32, 'norm_ffn': _jnp.float32, 'w_gate': _jnp.float32, 'w_up': _jnp.float32, 'w_down': _jnp.float32}
MOMENT_SCALE = {'norm_mix': 1.083079e+02, 'w_in': 1.528743e+00, 'q_norm': 3.035600e+01, 'k_norm': 3.035876e+01, 'conv_w': 3.627062e+01, 'w_out': 1.661131e+00, 'norm_ffn': 4.938567e+01, 'w_gate': 3.802314e-01, 'w_up': 4.505359e-01, 'w_down': 7.330935e-01}


def _to_microbatches(a, axis):
    t = _jnp.moveaxis(a, axis, 0)
    t = t.reshape((N_MICROBATCH, t.shape[0] // N_MICROBATCH) + t.shape[1:])
    return _jnp.moveaxis(t, 1, axis + 1)


def setup_inputs(seed: int = 0) -> dict:
    inp = _fwd_setup_inputs(seed)
    key = _jax.random.fold_in(_jax.random.key(seed), 7919)
    shape, _ = _output_shape()
    out = dict(inp)
    out["loss_target"] = _jax.random.normal(_jax.random.fold_in(key, 0), shape, _jnp.float32)
    for i, name in enumerate(TWIN_WEIGHTS):
        w = inp[name].astype(_jnp.float32)
        if MOMENT_SCALE is None:
            s = _jnp.sqrt(_jnp.mean(_jnp.square(w)) + 1e-30)
        else:
            s = MOMENT_SCALE[name]
        km, kv = _jax.random.split(_jax.random.fold_in(key, i + 1))
        out[name] = w
        out["m_" + name] = s * _jax.random.normal(km, w.shape, _jnp.float32)
        out["v_" + name] = (s * s) * _jax.random.uniform(kv, w.shape, _jnp.float32, 0.5, 1.5)
    if N_MICROBATCH > 1:
        for name, axis in PER_EXAMPLE_BATCH_AXIS.items():
            out[name] = _to_microbatches(out[name], axis)
    return {'x': out['x'], 'norm_mix': out['norm_mix'], 'w_in': out['w_in'], 'q_norm': out['q_norm'], 'k_norm': out['k_norm'], 'conv_w': out['conv_w'], 'w_out': out['w_out'], 'norm_ffn': out['norm_ffn'], 'w_gate': out['w_gate'], 'w_up': out['w_up'], 'w_down': out['w_down'], 'loss_target': out['loss_target'], 'm_norm_mix': out['m_norm_mix'], 'm_w_in': out['m_w_in'], 'm_q_norm': out['m_q_norm'], 'm_k_norm': out['m_k_norm'], 'm_conv_w': out['m_conv_w'], 'm_w_out': out['m_w_out'], 'm_norm_ffn': out['m_norm_ffn'], 'm_w_gate': out['m_w_gate'], 'm_w_up': out['m_w_up'], 'm_w_down': out['m_w_down'], 'v_norm_mix': out['v_norm_mix'], 'v_w_in': out['v_w_in'], 'v_q_norm': out['v_q_norm'], 'v_k_norm': out['v_k_norm'], 'v_conv_w': out['v_conv_w'], 'v_w_out': out['v_w_out'], 'v_norm_ffn': out['v_norm_ffn'], 'v_w_gate': out['v_w_gate'], 'v_w_up': out['v_w_up'], 'v_w_down': out['v_w_down']}


def _loss(weights, diff, rest, loss_target):
    with _jax.named_scope("forward"):
        args = {**rest, TWIN_DIFF_INPUT: diff, **{k: w.astype(_WEIGHT_DTYPES[k]) for k, w in weights.items()}}
        y = _forward(args)
    with _jax.named_scope("loss_head"):
        err = _jnp.square(y.astype(_jnp.float32) - loss_target)
        return 0.5 * _jnp.sum(_jnp.mean(err, axis=-1)) if err.ndim else 0.5 * err


def _adamw(w, g, m, v):
    m = ADAM_B1 * m + (1.0 - ADAM_B1) * g
    v = ADAM_B2 * v + (1.0 - ADAM_B2) * _jnp.square(g)
    m_hat = m / (1.0 - ADAM_B1 ** ADAM_STEP)
    v_hat = v / (1.0 - ADAM_B2 ** ADAM_STEP)
    delta = -ADAM_LR * (m_hat / (_jnp.sqrt(v_hat) + ADAM_EPS) + ADAM_WD * w)
    return delta, m, v


def reference(x, norm_mix, w_in, q_norm, k_norm, conv_w, w_out, norm_ffn, w_gate, w_up, w_down, loss_target, m_norm_mix, m_w_in, m_q_norm, m_k_norm, m_conv_w, m_w_out, m_norm_ffn, m_w_gate, m_w_up, m_w_down, v_norm_mix, v_w_in, v_q_norm, v_k_norm, v_conv_w, v_w_out, v_norm_ffn, v_w_gate, v_w_up, v_w_down):
    given = dict(x=x, norm_mix=norm_mix, w_in=w_in, q_norm=q_norm, k_norm=k_norm, conv_w=conv_w, w_out=w_out, norm_ffn=norm_ffn, w_gate=w_gate, w_up=w_up, w_down=w_down, loss_target=loss_target, m_norm_mix=m_norm_mix, m_w_in=m_w_in, m_q_norm=m_q_norm, m_k_norm=m_k_norm, m_conv_w=m_conv_w, m_w_out=m_w_out, m_norm_ffn=m_norm_ffn, m_w_gate=m_w_gate, m_w_up=m_w_up, m_w_down=m_w_down, v_norm_mix=v_norm_mix, v_w_in=v_w_in, v_q_norm=v_q_norm, v_k_norm=v_k_norm, v_conv_w=v_conv_w, v_w_out=v_w_out, v_norm_ffn=v_norm_ffn, v_w_gate=v_w_gate, v_w_up=v_w_up, v_w_down=v_w_down)
    weights = {n: given[n] for n in TWIN_WEIGHTS}
    shared = {n: given[n] for n in SHARED_INPUTS}
    per_example = {n: given[n] for n in ['x']}
    grad_fn = _jax.value_and_grad(_loss, argnums=(0, 1))

    def one_microbatch(ex, loss_target):
        ex = dict(ex)
        diff = ex.pop(TWIN_DIFF_INPUT)
        return grad_fn(weights, diff, {**shared, **ex}, loss_target)

    if N_MICROBATCH == 1:
        loss, (grad_w, grad_x) = one_microbatch(per_example, given["loss_target"])
    else:
        def body(carry, xs):
            loss_sum, grad_sum = carry
            l_k, (gw_k, gx_k) = one_microbatch(xs[0], xs[1])
            with _jax.named_scope("update"):
                return (loss_sum + l_k, _jax.tree.map(_jnp.add, grad_sum, gw_k)), gx_k

        init = (_jnp.zeros((), _jnp.float32), _jax.tree.map(_jnp.zeros_like, weights))
        (loss, grad_w), grad_x = _jax.lax.scan(body, init, (per_example, given["loss_target"]))
    with _jax.named_scope("update"):
        delta_w, new_m, new_v = {}, {}, {}
        for n in TWIN_WEIGHTS:
            delta_w[n], new_m[n], new_v[n] = _adamw(weights[n], grad_w[n], given["m_" + n], given["v_" + n])
    return (loss, grad_x, *[grad_w[n] for n in TWIN_WEIGHTS], *[delta_w[n] for n in TWIN_WEIGHTS],
            *[new_m[n] for n in TWIN_WEIGHTS], *[new_v[n] for n in TWIN_WEIGHTS])
```

```python
import jax
import jax.numpy as jnp
from jax import lax
from jax.experimental import pallas as pl
from jax.experimental.pallas import tpu as pltpu

F32 = jnp.float32
BF16 = jnp.bfloat16
MESH = pl.DeviceIdType.MESH

N_DEV = 8
LANES = 128
HEAD_DIM = 64
KEY_CHUNK = 128
EPS = 1e-6
VMEM_LIMIT = 48 * 1024 * 1024

ADAM_LR = 0.001
ADAM_B1 = 0.9
ADAM_B2 = 0.999
ADAM_EPS = 1e-08
ADAM_WD = 0.01
ADAM_STEP = 10

NN = (((1,), (0,)), ((), ()))
NT = (((1,), (1,)), ((), ()))
TN = (((0,), (0,)), ((), ()))


def _dot(a, b, dims):
    return lax.dot_general(a.astype(BF16), b.astype(BF16), dims, preferred_element_type=F32)


def _cparams(*sem):
    return pltpu.CompilerParams(dimension_semantics=sem, vmem_limit_bytes=VMEM_LIMIT)


def _split_hi_lo(v):
    hi = v.astype(BF16)
    lo = (v - hi.astype(F32)).astype(BF16)
    return jnp.concatenate([hi, lo], axis=1)


def _rmsnorm_fwd(x, gain, tm, name):
    s, d = x.shape

    def body(x_ref, g_ref, o_ref):
        xv = x_ref[...]
        r = lax.rsqrt(jnp.mean(xv * xv, axis=-1, keepdims=True) + EPS)
        o_ref[...] = ((xv * r) * g_ref[...]).astype(o_ref.dtype)

    return pl.pallas_call(
        body, name=name, grid=(s // tm,),
        in_specs=[pl.BlockSpec((tm, d), lambda i: (i, 0)), pl.BlockSpec((1, d), lambda i: (0, 0))],
        out_specs=pl.BlockSpec((tm, d), lambda i: (i, 0)),
        out_shape=jax.ShapeDtypeStruct((s, d), BF16),
        compiler_params=_cparams("parallel"),
    )(x, gain)


def _rmsnorm_bwd(dh, x, gain, dres, tm, name):
    s, d = x.shape
    nsteps = s // tm

    def body(dh_ref, x_ref, g_ref, dres_ref, dx_ref, dg_ref):
        i = pl.program_id(0)
        xv = x_ref[...]
        r = lax.rsqrt(jnp.mean(xv * xv, axis=-1, keepdims=True) + EPS)
        xhat = xv * r
        dhv = dh_ref[...]
        dxh = dhv * g_ref[...]
        proj = jnp.mean(dxh * xhat, axis=-1, keepdims=True)
        dx_ref[...] = dres_ref[...] + r * (dxh - xhat * proj)
        part = jnp.sum((dhv * xhat).reshape(tm // 8, 8, d), axis=0)

        @pl.when(i == 0)
        def _():
            dg_ref[...] = part

        @pl.when(i > 0)
        def _():
            dg_ref[...] += part

        @pl.when(i == nsteps - 1)
        def _():
            dg_ref[...] = jnp.broadcast_to(jnp.sum(dg_ref[...], axis=0, keepdims=True), (8, d))

    row = pl.BlockSpec((tm, d), lambda i: (i, 0))
    return pl.pallas_call(
        body, name=name, grid=(nsteps,),
        in_specs=[row, row, pl.BlockSpec((1, d), lambda i: (0, 0)), row],
        out_specs=[row, pl.BlockSpec((8, d), lambda i: (0, 0))],
        out_shape=[jax.ShapeDtypeStruct((s, d), F32), jax.ShapeDtypeStruct((8, d), F32)],
        compiler_params=_cparams("arbitrary"),
    )(dh, x, gain, dres)


def _group_mean_matrix():
    r = lax.broadcasted_iota(jnp.int32, (LANES, LANES), 0) // HEAD_DIM
    c = lax.broadcasted_iota(jnp.int32, (LANES, LANES), 1) // HEAD_DIM
    return jnp.where(r == c, 1.0 / HEAD_DIM, 0.0).astype(BF16)


def _group_mean(v, gm):
    hi = v.astype(BF16)
    lo = (v - hi.astype(F32)).astype(BF16)
    return _dot(hi, gm, NN) + _dot(lo, gm, NN)


def _qknorm_fwd(proj, gains, tm, name):
    s = proj.shape[0]
    ncol = gains.shape[1] // LANES

    def body(p_ref, g_ref, gm_ref, o_ref):
        xv = p_ref[...].astype(F32)
        r = lax.rsqrt(_group_mean(xv * xv, gm_ref[...]) + EPS)
        o_ref[...] = ((xv * r) * g_ref[...]).astype(o_ref.dtype)

    blk = pl.BlockSpec((tm, LANES), lambda i, j: (i, j))
    return pl.pallas_call(
        body, name=name, grid=(s // tm, ncol),
        in_specs=[blk, pl.BlockSpec((1, LANES), lambda i, j: (0, j)),
                  pl.BlockSpec((LANES, LANES), lambda i, j: (0, 0))],
        out_specs=blk,
        out_shape=jax.ShapeDtypeStruct((s, ncol * LANES), BF16),
        compiler_params=_cparams("parallel", "parallel"),
    )(proj, gains, _group_mean_matrix())


def _qknorm_bwd(dqk, proj, gains, tm, name):
    s = proj.shape[0]
    ncol = gains.shape[1] // LANES
    nsteps = s // tm

    def body(dy_ref, p_ref, g_ref, gm_ref, dx_ref, dg_ref):
        i = pl.program_id(1)
        gm = gm_ref[...]
        xv = p_ref[...].astype(F32)
        r = lax.rsqrt(_group_mean(xv * xv, gm) + EPS)
        xhat = xv * r
        dy = dy_ref[...]
        dxh = dy * g_ref[...]
        proj_ = _group_mean(dxh * xhat, gm)
        dx_ref[...] = (r * (dxh - xhat * proj_)).astype(dx_ref.dtype)
        part = jnp.sum((dy * xhat).reshape(tm // 8, 8, LANES), axis=0)

        @pl.when(i == 0)
        def _():
            dg_ref[...] = part

        @pl.when(i > 0)
        def _():
            dg_ref[...] += part

        @pl.when(i == nsteps - 1)
        def _():
            dg_ref[...] = jnp.broadcast_to(jnp.sum(dg_ref[...], axis=0, keepdims=True), (8, LANES))

    blk = pl.BlockSpec((tm, LANES), lambda j, i: (i, j))
    return pl.pallas_call(
        body, name=name, grid=(ncol, nsteps),
        in_specs=[blk, blk, pl.BlockSpec((1, LANES), lambda j, i: (0, j)),
                  pl.BlockSpec((LANES, LANES), lambda j, i: (0, 0))],
        out_specs=[blk, pl.BlockSpec((8, LANES), lambda j, i: (0, j))],
        out_shape=[jax.ShapeDtypeStruct((s, ncol * LANES), BF16),
                   jax.ShapeDtypeStruct((8, ncol * LANES), F32)],
        compiler_params=_cparams("parallel", "arbitrary"),
    )(dqk, proj, gains, _group_mean_matrix())


CONV_ROWS = 256
HALO = 8


def _conv_fwd(proj, conv_w8, name):
    s = proj.shape[0]
    nblk = conv_w8.shape[1] // LANES
    first = 3 * nblk
    nchunk = s // CONV_ROWS

    def body(cb_ref, cc_ref, cu_ref, w_ref, y_ref, hpad):
        hpad[pl.ds(0, 2 * HALO), :] = jnp.zeros((2 * HALO, LANES), F32)

        def fill(i, _):
            r0 = pl.multiple_of(i * CONV_ROWS, CONV_ROWS)
            hpad[pl.ds(r0 + 2 * HALO, CONV_ROWS), :] = (
                cc_ref[pl.ds(r0, CONV_ROWS), :].astype(F32) * cu_ref[pl.ds(r0, CONV_ROWS), :].astype(F32))
            return 0

        lax.fori_loop(0, nchunk, fill, 0)
        w0, w1, w2 = w_ref[0:1, :], w_ref[1:2, :], w_ref[2:3, :]

        def conv(i, _):
            r0 = pl.multiple_of(i * CONV_ROWS, CONV_ROWS)
            win = hpad[pl.ds(r0 + HALO, CONV_ROWS + HALO), :]
            c = (w2 * win[HALO:] + w1 * pltpu.roll(win, 1, 0)[HALO:] + w0 * pltpu.roll(win, 2, 0)[HALO:])
            y_ref[pl.ds(r0, CONV_ROWS), :] = (cb_ref[pl.ds(r0, CONV_ROWS), :].astype(F32) * c).astype(y_ref.dtype)
            return 0

        lax.fori_loop(0, nchunk, conv, 0)

    def col(off):
        return pl.BlockSpec((s, LANES), lambda j: (0, off + j))

    return pl.pallas_call(
        body, name=name, grid=(nblk,),
        in_specs=[col(first), col(first + nblk), col(first + 2 * nblk), pl.BlockSpec((8, LANES), lambda j: (0, j))],
        out_specs=pl.BlockSpec((s, LANES), lambda j: (0, j)),
        out_shape=jax.ShapeDtypeStruct((s, nblk * LANES), BF16),
        scratch_shapes=[pltpu.VMEM((s + 2 * HALO, LANES), F32)],
        compiler_params=_cparams("parallel"),
    )(proj, proj, proj, conv_w8)


def _conv_bwd(dmix, proj, conv_w8, name):
    s = proj.shape[0]
    nblk = conv_w8.shape[1] // LANES
    first = 3 * nblk
    nchunk = s // CONV_ROWS

    def body(dy_ref, cb_ref, cc_ref, cu_ref, w_ref, dcb_ref, dcc_ref, dcu_ref, dw_ref, hpad, dcpad):
        hpad[pl.ds(0, 2 * HALO), :] = jnp.zeros((2 * HALO, LANES), F32)
        dcpad[pl.ds(s, 2 * HALO), :] = jnp.zeros((2 * HALO, LANES), F32)

        def fill(i, _):
            r0 = pl.multiple_of(i * CONV_ROWS, CONV_ROWS)
            hpad[pl.ds(r0 + 2 * HALO, CONV_ROWS), :] = (
                cc_ref[pl.ds(r0, CONV_ROWS), :].astype(F32) * cu_ref[pl.ds(r0, CONV_ROWS), :].astype(F32))
            return 0

        lax.fori_loop(0, nchunk, fill, 0)
        w0, w1, w2 = w_ref[0:1, :], w_ref[1:2, :], w_ref[2:3, :]

        def fold(v):
            return jnp.sum(v.reshape(CONV_ROWS // 8, 8, LANES), axis=0)

        def first_pass(i, acc):
            a0, a1, a2 = acc
            r0 = pl.multiple_of(i * CONV_ROWS, CONV_ROWS)
            win = hpad[pl.ds(r0 + HALO, CONV_ROWS + HALO), :]
            h0 = win[HALO:]
            h1 = pltpu.roll(win, 1, 0)[HALO:]
            h2 = pltpu.roll(win, 2, 0)[HALO:]
            c = w2 * h0 + w1 * h1 + w0 * h2
            dy = dy_ref[pl.ds(r0, CONV_ROWS), :]
            dcb_ref[pl.ds(r0, CONV_ROWS), :] = (dy * c).astype(dcb_ref.dtype)
            dc = dy * cb_ref[pl.ds(r0, CONV_ROWS), :].astype(F32)
            dcpad[pl.ds(r0, CONV_ROWS), :] = dc
            return a0 + fold(dc * h2), a1 + fold(dc * h1), a2 + fold(dc * h0)

        z8 = jnp.zeros((8, LANES), F32)
        a0, a1, a2 = lax.fori_loop(0, nchunk, first_pass, (z8, z8, z8))
        dw_ref[...] = jnp.concatenate(
            [jnp.sum(a0, axis=0, keepdims=True), jnp.sum(a1, axis=0, keepdims=True),
             jnp.sum(a2, axis=0, keepdims=True), jnp.zeros((5, LANES), F32)], axis=0)

        def second_pass(i, _):
            r0 = pl.multiple_of(i * CONV_ROWS, CONV_ROWS)
            win = dcpad[pl.ds(r0, CONV_ROWS + HALO), :]
            n = CONV_ROWS + HALO
            dh = (w2 * win[:CONV_ROWS] + w1 * pltpu.roll(win, n - 1, 0)[:CONV_ROWS]
                  + w0 * pltpu.roll(win, n - 2, 0)[:CONV_ROWS])
            dcc_ref[pl.ds(r0, CONV_ROWS), :] = (dh * cu_ref[pl.ds(r0, CONV_ROWS), :].astype(F32)).astype(dcc_ref.dtype)
            dcu_ref[pl.ds(r0, CONV_ROWS), :] = (dh * cc_ref[pl.ds(r0, CONV_ROWS), :].astype(F32)).astype(dcu_ref.dtype)
            return 0

        lax.fori_loop(0, nchunk, second_pass, 0)

    def col(off):
        return pl.BlockSpec((s, LANES), lambda j: (0, off + j))

    out = pl.BlockSpec((s, LANES), lambda j: (0, j))
    return pl.pallas_call(
        body, name=name, grid=(nblk,),
        in_specs=[col(nblk), col(first), col(first + nblk), col(first + 2 * nblk),
                  pl.BlockSpec((8, LANES), lambda j: (0, j))],
        out_specs=[out, out, out, pl.BlockSpec((8, LANES), lambda j: (0, j))],
        out_shape=[jax.ShapeDtypeStruct((s, nblk * LANES), BF16)] * 3 + [jax.ShapeDtypeStruct((8, nblk * LANES), F32)],
        scratch_shapes=[pltpu.VMEM((s + 2 * HALO, LANES), F32), pltpu.VMEM((s + 2 * HALO, LANES), F32)],
        compiler_params=_cparams("parallel"),
    )(dmix, proj, proj, proj, conv_w8)


def _cumsum_matrix(kind):
    j = lax.broadcasted_iota(jnp.int32, (2 * KEY_CHUNK, 2 * KEY_CHUNK), 0) % KEY_CHUNK
    c = lax.broadcasted_iota(jnp.int32, (2 * KEY_CHUNK, 2 * KEY_CHUNK), 1)
    tri = {"after": j > c, "upto": j <= c, "before": j < c}[kind]
    return jnp.where((c >= KEY_CHUNK) | tri, 1.0, 0.0).astype(BF16)


def _softplus_parts(z):
    relu = jnp.maximum(z, 0.0)
    sp = relu + jnp.log(1.0 + jnp.exp(-jnp.abs(z)))
    return sp


def _attn_fwd(qk, proj, tq, name):
    s = qk.shape[0]
    nhp = qk.shape[1] // (2 * LANES)
    nc = tq // KEY_CHUNK

    def body(q_ref, k_ref, v_ref, cm_ref, o_ref, r_ref):
        qi = pl.program_id(1)
        q = q_ref[...]
        m0 = lax.broadcasted_iota(jnp.int32, (1, LANES), 1) < HEAD_DIM
        zq = jnp.zeros_like(q)
        qh = (jnp.where(m0, q, zq), jnp.where(m0, zq, q))
        cm = cm_ref[...]
        qpos = qi * tq + lax.broadcasted_iota(jnp.int32, (tq, KEY_CHUNK), 0)
        kcol = lax.broadcasted_iota(jnp.int32, (tq, KEY_CHUNK), 1)

        def chunk(kc, carry, masked):
            rs, acc = list(carry[:2]), carry[2]
            ks = pl.multiple_of(kc * KEY_CHUNK, KEY_CHUNK)
            kb = k_ref[pl.ds(ks, KEY_CHUNK), :]
            vb = v_ref[pl.ds(ks, KEY_CHUNK), :]
            mask = (kcol + kc * KEY_CHUNK) < qpos if masked else None
            pv = []
            for h in range(2):
                z = _dot(qh[h], kb, NT)
                sp = _softplus_parts(z)
                ls = z - sp
                if masked:
                    sp = jnp.where(mask, sp, 0.0)
                cs = _dot(_split_hi_lo(sp), cm, NN)
                a = jnp.exp(ls - cs[:, :KEY_CHUNK] - rs[h])
                if masked:
                    a = jnp.where(mask, a, 0.0)
                pv.append(_dot(a, vb, NN))
                rs[h] = rs[h] + cs[:, KEY_CHUNK:]
            return rs[0], rs[1], acc + jnp.where(m0, pv[0], pv[1])

        zero = jnp.zeros((tq, LANES), F32)
        carry = (zero, zero, zero)
        for d in range(nc):
            carry = chunk(qi * nc + (nc - 1 - d), carry, True)
        carry = lax.fori_loop(0, qi * nc, lambda it, cr: chunk(qi * nc - 1 - it, cr, False), carry)
        o_ref[...] = carry[2].astype(o_ref.dtype)
        r_ref[:, :LANES] = carry[0]
        r_ref[:, LANES:] = carry[1]

    return pl.pallas_call(
        body, name=name, grid=(nhp, s // tq),
        in_specs=[pl.BlockSpec((tq, LANES), lambda p, i: (i, p)),
                  pl.BlockSpec((s, LANES), lambda p, i: (0, nhp + p)),
                  pl.BlockSpec((s, LANES), lambda p, i: (0, 2 * nhp + p)),
                  pl.BlockSpec((2 * KEY_CHUNK, 2 * KEY_CHUNK), lambda p, i: (0, 0))],
        out_specs=[pl.BlockSpec((tq, LANES), lambda p, i: (i, p)),
                   pl.BlockSpec((tq, 2 * LANES), lambda p, i: (i, p))],
        out_shape=[jax.ShapeDtypeStruct((s, nhp * LANES), BF16),
                   jax.ShapeDtypeStruct((s, nhp * 2 * LANES), F32)],
        compiler_params=_cparams("parallel", "parallel"),
    )(qk, qk, proj, _cumsum_matrix("after"))


def _attn_bwd(qk, proj, dmix, rtot, tq, name):
    s = qk.shape[0]
    nhp = qk.shape[1] // (2 * LANES)
    nc = tq // KEY_CHUNK

    def body(q_ref, k_ref, v_ref, do_ref, r_ref, cmi_ref, cme_ref, dq_ref, dk_ref, dv_ref):
        qi = pl.program_id(1)

        @pl.when(qi == 0)
        def _():
            dk_ref[...] = jnp.zeros_like(dk_ref)
            dv_ref[...] = jnp.zeros_like(dv_ref)

        q = q_ref[...]
        do = do_ref[...].astype(BF16)
        m0 = lax.broadcasted_iota(jnp.int32, (1, LANES), 1) < HEAD_DIM
        zq = jnp.zeros_like(q)
        qh = (jnp.where(m0, q, zq), jnp.where(m0, zq, q))
        doh = (jnp.where(m0, do, zq), jnp.where(m0, zq, do))
        rt = (r_ref[:, :LANES], r_ref[:, LANES:])
        cmi = cmi_ref[...]
        cme = cme_ref[...]
        qpos = qi * tq + lax.broadcasted_iota(jnp.int32, (tq, KEY_CHUNK), 0)
        kcol = lax.broadcasted_iota(jnp.int32, (tq, KEY_CHUNK), 1)

        def chunk(kc, carry, masked):
            ps, gs, dq = list(carry[0:2]), list(carry[2:4]), carry[4]
            ks = pl.multiple_of(kc * KEY_CHUNK, KEY_CHUNK)
            kb = k_ref[pl.ds(ks, KEY_CHUNK), :]
            vb = v_ref[pl.ds(ks, KEY_CHUNK), :]
            mask = (kcol + kc * KEY_CHUNK) < qpos if masked else None
            dqs, dks, dvs = [], [], []
            for h in range(2):
                z = _dot(qh[h], kb, NT)
                sp = _softplus_parts(z)
                ls = z - sp
                if masked:
                    sp = jnp.where(mask, sp, 0.0)
                cs = _dot(_split_hi_lo(sp), cmi, NN)
                rem = rt[h] - ps[h] - cs[:, :KEY_CHUNK]
                a = jnp.exp(ls - rem)
                if masked:
                    a = jnp.where(mask, a, 0.0)
                g = a * _dot(doh[h], vb, NT)
                cg = _dot(_split_hi_lo(g), cme, NN)
                sig = jnp.exp(ls)
                dz = g * (1.0 - sig) - sig * (gs[h] + cg[:, :KEY_CHUNK])
                if masked:
                    dz = jnp.where(mask, dz, 0.0)
                dzb = dz.astype(BF16)
                dqs.append(_dot(dzb, kb, NN))
                dks.append(_dot(dzb, q, TN))
                dvs.append(_dot(a, do, TN))
                ps[h] = ps[h] + cs[:, KEY_CHUNK:]
                gs[h] = gs[h] + cg[:, KEY_CHUNK:]
            dk_ref[pl.ds(ks, KEY_CHUNK), :] += jnp.where(m0, dks[0], dks[1])
            dv_ref[pl.ds(ks, KEY_CHUNK), :] += jnp.where(m0, dvs[0], dvs[1])
            return ps[0], ps[1], gs[0], gs[1], dq + jnp.where(m0, dqs[0], dqs[1])

        zero = jnp.zeros((tq, LANES), F32)
        carry = (zero, zero, zero, zero, zero)
        carry = lax.fori_loop(0, qi * nc, lambda kc, cr: chunk(kc, cr, False), carry)
        for d in range(nc):
            carry = chunk(qi * nc + d, carry, True)
        dq_ref[...] = carry[4]

    qblk = pl.BlockSpec((tq, LANES), lambda p, i: (i, p))
    full = pl.BlockSpec((s, LANES), lambda p, i: (0, p))
    cmspec = pl.BlockSpec((2 * KEY_CHUNK, 2 * KEY_CHUNK), lambda p, i: (0, 0))
    shape = jax.ShapeDtypeStruct((s, nhp * LANES), F32)
    return pl.pallas_call(
        body, name=name, grid=(nhp, s // tq),
        in_specs=[qblk,
                  pl.BlockSpec((s, LANES), lambda p, i: (0, nhp + p)),
                  pl.BlockSpec((s, LANES), lambda p, i: (0, 2 * nhp + p)),
                  qblk,
                  pl.BlockSpec((tq, 2 * LANES), lambda p, i: (i, p)),
                  cmspec, cmspec],
        out_specs=[qblk, full, full],
        out_shape=[shape, shape, shape],
        compiler_params=_cparams("parallel", "arbitrary"),
    )(qk, qk, proj, dmix, rtot, _cumsum_matrix("upto"), _cumsum_matrix("before"))


def _mm_blocks(h, ga, widx, tm, name):
    s, d = h.shape
    nb, cols = ga.shape[1], ga.shape[3]

    def body(a_ref, b_ref, o_ref):
        o_ref[...] = _dot(a_ref[...], b_ref[...], NN).astype(o_ref.dtype)

    return pl.pallas_call(
        body, name=name, grid=(s // tm, nb),
        in_specs=[pl.BlockSpec((tm, d), lambda i, j: (i, 0)),
                  pl.BlockSpec((None, None, d, cols), lambda i, j: (widx, j, 0, 0))],
        out_specs=pl.BlockSpec((tm, cols), lambda i, j: (i, j)),
        out_shape=jax.ShapeDtypeStruct((s, nb * cols), BF16),
        compiler_params=_cparams("parallel", "arbitrary"),
    )(h, ga)


def _mm_swiglu(h, ga, gidx, uidx, tm, name):
    s, d = h.shape
    nb, cols = ga.shape[1], ga.shape[3]

    def body(a_ref, bg_ref, bu_ref, g_ref, u_ref, act_ref):
        a = a_ref[...]
        g = _dot(a, bg_ref[...], NN)
        u = _dot(a, bu_ref[...], NN)
        g_ref[...] = g.astype(g_ref.dtype)
        u_ref[...] = u.astype(u_ref.dtype)
        act_ref[...] = (g * (1.0 / (1.0 + jnp.exp(-g))) * u).astype(act_ref.dtype)

    def wspec(idx):
        return pl.BlockSpec((None, None, d, cols), lambda i, j: (idx, j, 0, 0))

    out = pl.BlockSpec((tm, cols), lambda i, j: (i, j))
    shape = jax.ShapeDtypeStruct((s, nb * cols), BF16)
    return pl.pallas_call(
        body, name=name, grid=(s // tm, nb),
        in_specs=[pl.BlockSpec((tm, d), lambda i, j: (i, 0)), wspec(gidx), wspec(uidx)],
        out_specs=[out, out, out], out_shape=[shape, shape, shape],
        compiler_params=_cparams("parallel", "arbitrary"),
    )(h, ga, ga)


def _mm_residual(a, w3, lidx, res, tm, tn, name):
    s, k = a.shape
    n = w3.shape[2]

    def body(a_ref, b_ref, r_ref, o_ref):
        o_ref[...] = r_ref[...] + _dot(a_ref[...], b_ref[...], NN)

    return pl.pallas_call(
        body, name=name, grid=(s // tm, n // tn),
        in_specs=[pl.BlockSpec((tm, k), lambda i, j: (i, 0)),
                  pl.BlockSpec((None, k, tn), lambda i, j: (lidx, 0, j)),
                  pl.BlockSpec((tm, tn), lambda i, j: (i, j))],
        out_specs=pl.BlockSpec((tm, tn), lambda i, j: (i, j)),
        out_shape=jax.ShapeDtypeStruct((s, n), F32),
        compiler_params=_cparams("parallel", "arbitrary"),
    )(a, w3, res)


def _mm_nt(a, w3, lidx, tm, tn, name):
    s, k = a.shape
    n = w3.shape[1]

    def body(a_ref, b_ref, o_ref):
        o_ref[...] = _dot(a_ref[...], b_ref[...], NT)

    return pl.pallas_call(
        body, name=name, grid=(s // tm, n // tn),
        in_specs=[pl.BlockSpec((tm, k), lambda i, j: (i, 0)),
                  pl.BlockSpec((None, tn, k), lambda i, j: (lidx, j, 0))],
        out_specs=pl.BlockSpec((tm, tn), lambda i, j: (i, j)),
        out_shape=jax.ShapeDtypeStruct((s, n), F32),
        compiler_params=_cparams("parallel", "arbitrary"),
    )(a, w3)


def _mm_nt_swiglu_bwd(dx, wd3, lidx, g, u, tm, name):
    s, d = dx.shape
    cols = g.shape[1] // N_DEV

    def body(a_ref, b_ref, g_ref, u_ref, dg_ref, du_ref):
        dact = _dot(a_ref[...], b_ref[...], NT)
        gv = g_ref[...].astype(F32)
        sig = 1.0 / (1.0 + jnp.exp(-gv))
        du_ref[...] = (dact * (gv * sig)).astype(du_ref.dtype)
        dg_ref[...] = (dact * u_ref[...].astype(F32) * (sig * (1.0 + gv * (1.0 - sig)))).astype(dg_ref.dtype)

    blk = pl.BlockSpec((tm, cols), lambda i, j: (i, j))
    shape = jax.ShapeDtypeStruct(g.shape, BF16)
    return pl.pallas_call(
        body, name=name, grid=(s // tm, N_DEV),
        in_specs=[pl.BlockSpec((tm, d), lambda i, j: (i, 0)),
                  pl.BlockSpec((None, cols, d), lambda i, j: (lidx, j, 0)), blk, blk],
        out_specs=[blk, blk], out_shape=[shape, shape],
        compiler_params=_cparams("parallel", "arbitrary"),
    )(dx, wd3, g, u)


def _mm_nt_blocks(das, ga, widxs, tm, name):
    s = das[0].shape[0]
    nb, d, cols = ga.shape[1], ga.shape[2], ga.shape[3]
    nw = len(das)

    def body(*refs):
        a_refs, b_refs, o_ref = refs[:nw], refs[nw:2 * nw], refs[2 * nw]
        k = pl.program_id(1)
        part = _dot(a_refs[0][...], b_refs[0][...], NT)
        for w in range(1, nw):
            part = part + _dot(a_refs[w][...], b_refs[w][...], NT)

        @pl.when(k == 0)
        def _():
            o_ref[...] = part

        @pl.when(k > 0)
        def _():
            o_ref[...] += part

    def wspec(idx):
        return pl.BlockSpec((None, None, d, cols), lambda i, k: (idx, k, 0, 0))

    return pl.pallas_call(
        body, name=name, grid=(s // tm, nb),
        in_specs=[pl.BlockSpec((tm, cols), lambda i, k: (i, k))] * nw + [wspec(i) for i in widxs],
        out_specs=pl.BlockSpec((tm, d), lambda i, k: (i, 0)),
        out_shape=jax.ShapeDtypeStruct((s, d), F32),
        compiler_params=_cparams("parallel", "arbitrary"),
    )(*das, *([ga] * nw))


def _mm_tn(a, b, ta, tb, tk, out_blocks, name):
    s, ka = a.shape
    nb = b.shape[1]
    nk = s // tk

    def body(a_ref, b_ref, o_ref):
        k = pl.program_id(2)
        part = _dot(a_ref[...], b_ref[...], TN)

        @pl.when(k == 0)
        def _():
            o_ref[...] = part

        @pl.when(k > 0)
        def _():
            o_ref[...] += part

    if out_blocks:
        out_spec = pl.BlockSpec((None, ta, tb), lambda i, j, k: (j, i, 0))
        out_shape = jax.ShapeDtypeStruct((nb // tb, ka, tb), F32)
    else:
        out_spec = pl.BlockSpec((ta, tb), lambda i, j, k: (i, j))
        out_shape = jax.ShapeDtypeStruct((ka, nb), F32)
    return pl.pallas_call(
        body, name=name, grid=(ka // ta, nb // tb, nk),
        in_specs=[pl.BlockSpec((tk, ta), lambda i, j, k: (k, i)),
                  pl.BlockSpec((tk, tb), lambda i, j, k: (k, j))],
        out_specs=out_spec, out_shape=out_shape,
        compiler_params=_cparams("parallel", "parallel", "arbitrary"),
    )(a, b)


def _loss_head(y, target, tm, name):
    s, d = y.shape
    nsteps = s // tm

    def body(y_ref, t_ref, dy_ref, l_ref, acc):
        i = pl.program_id(0)
        diff = y_ref[...] - t_ref[...]
        dy_ref[...] = diff * (1.0 / d)
        part = jnp.sum((diff * diff).reshape(tm // 8, 8, d), axis=0)

        @pl.when(i == 0)
        def _():
            acc[...] = part

        @pl.when(i > 0)
        def _():
            acc[...] += part

        @pl.when(i == nsteps - 1)
        def _():
            tot = jnp.sum(jnp.sum(acc[...], axis=1, keepdims=True), axis=0, keepdims=True)
            l_ref[...] = jnp.broadcast_to(tot * (0.5 / d), (8, LANES))

    row = pl.BlockSpec((tm, d), lambda i: (i, 0))
    return pl.pallas_call(
        body, name=name, grid=(nsteps,),
        in_specs=[row, row],
        out_specs=[row, pl.BlockSpec((8, LANES), lambda i: (0, 0))],
        out_shape=[jax.ShapeDtypeStruct((s, d), F32), jax.ShapeDtypeStruct((8, LANES), F32)],
        scratch_shapes=[pltpu.VMEM((8, d), F32)],
        compiler_params=_cparams("arbitrary"),
    )(y, target)


def _adamw(parts, w, m, v, tr, name):
    p, rows, cols = parts.shape
    c1 = 1.0 / (1.0 - ADAM_B1 ** ADAM_STEP)
    c2 = 1.0 / (1.0 - ADAM_B2 ** ADAM_STEP)

    def body(p_ref, w_ref, m_ref, v_ref, g_ref, d_ref, nm_ref, nv_ref):
        g = p_ref[0]
        for k in range(1, p):
            g = g + p_ref[k]
        nm = ADAM_B1 * m_ref[...] + (1.0 - ADAM_B1) * g
        nv = ADAM_B2 * v_ref[...] + (1.0 - ADAM_B2) * (g * g)
        g_ref[...] = g
        nm_ref[...] = nm
        nv_ref[...] = nv
        d_ref[...] = -ADAM_LR * ((nm * c1) / (jnp.sqrt(nv * c2) + ADAM_EPS) + ADAM_WD * w_ref[...])

    blk = pl.BlockSpec((tr, cols), lambda i: (i, 0))
    shape = jax.ShapeDtypeStruct((rows, cols), F32)
    return pl.pallas_call(
        body, name=name, grid=(rows // tr,),
        in_specs=[pl.BlockSpec((p, tr, cols), lambda i: (0, i, 0)), blk, blk, blk],
        out_specs=[blk] * 4, out_shape=[shape] * 4,
        compiler_params=_cparams("parallel"),
    )(parts, w, m, v)


def _place():
    x, y, c = lax.axis_index("x"), lax.axis_index("y"), lax.axis_index("c")
    return x, y, c


def _all_gather(shards, name):
    na = len(shards)

    def body(*refs):
        srcs, dsts = refs[:na], refs[na:2 * na]
        send_sems, recv_sems, local_sems = refs[2 * na:]
        x, y, c = _place()
        me, sibling = (x, y, c), (x, y, 1 - c)
        chips = [(1 - x, y), (x, 1 - y), (1 - x, 1 - y)]

        def slot(a, dev):
            return dsts[a].at[:, pl.ds(4 * dev[0] + 2 * dev[1] + dev[2], 1)]

        def copy(k, a, block, to, from_shard=False):
            return pltpu.make_async_remote_copy(
                src_ref=srcs[a] if from_shard else slot(a, block), dst_ref=slot(a, block),
                send_sem=send_sems.at[k, a], recv_sem=recv_sems.at[k, a], device_id=to, device_id_type=MESH)

        mine = [pltpu.make_async_copy(srcs[a], slot(a, me), local_sems.at[a]) for a in range(na)]
        for cp in mine:
            cp.start()
        first = [copy(0, a, me, sibling, True) for a in range(na)]
        first += [copy(1 + j, a, me, (*chip, c), True) for j, chip in enumerate(chips) for a in range(na)]
        for cp in first:
            cp.start()
        passed = []
        for j, chip in enumerate(chips):
            for a in range(na):
                copy(1 + j, a, (*chip, c), me).wait_recv()
                fwd = copy(4 + j, a, (*chip, c), sibling)
                fwd.start()
                passed.append(fwd)
        for a in range(na):
            copy(0, a, sibling, me).wait_recv()
        for j, chip in enumerate(chips):
            for a in range(na):
                copy(4 + j, a, (*chip, 1 - c), me).wait_recv()
        for cp in first + passed:
            cp.wait_send()
        for cp in mine:
            cp.wait()

    anyspec = pl.BlockSpec(memory_space=pl.ANY)
    return pl.pallas_call(
        body, name=name,
        in_specs=[anyspec] * na, out_specs=[anyspec] * na,
        out_shape=[jax.ShapeDtypeStruct((a.shape[0], N_DEV) + a.shape[2:], a.dtype) for a in shards],
        scratch_shapes=[pltpu.SemaphoreType.DMA((7, na)), pltpu.SemaphoreType.DMA((7, na)),
                        pltpu.SemaphoreType.DMA((na,))],
    )(*shards)


_RELATIONS = [(dx, dy, dc) for dx in (0, 1) for dy in (0, 1) for dc in (0, 1)][1:]


def _flip(v, d):
    return 1 - v if d else v


def _exchange_blocks(grads, name):
    na = len(grads)

    def body(*refs):
        srcs, dsts = refs[:na], refs[na:2 * na]
        send_sems, recv_sems, local_sems = refs[2 * na:]
        x, y, c = _place()
        my = 4 * x + 2 * y + c
        mine = [pltpu.make_async_copy(srcs[a].at[pl.ds(my, 1)], dsts[a].at[pl.ds(my, 1)], local_sems.at[a])
                for a in range(na)]
        for cp in mine:
            cp.start()
        sends = []
        for k, (dx, dy, dc) in enumerate(_RELATIONS):
            peer = (_flip(x, dx), _flip(y, dy), _flip(c, dc))
            pidx = 4 * peer[0] + 2 * peer[1] + peer[2]
            for a in range(na):
                cp = pltpu.make_async_remote_copy(
                    src_ref=srcs[a].at[pl.ds(pidx, 1)], dst_ref=dsts[a].at[pl.ds(my, 1)],
                    send_sem=send_sems.at[k, a], recv_sem=recv_sems.at[k, a], device_id=peer, device_id_type=MESH)
                cp.start()
                sends.append((cp, pidx, k, a, peer))
        for cp, pidx, k, a, peer in sends:
            pltpu.make_async_remote_copy(
                src_ref=srcs[a].at[pl.ds(pidx, 1)], dst_ref=dsts[a].at[pl.ds(pidx, 1)],
                send_sem=send_sems.at[k, a], recv_sem=recv_sems.at[k, a], device_id=peer,
                device_id_type=MESH).wait_recv()
        for cp, *_ in sends:
            cp.wait_send()
        for cp in mine:
            cp.wait()

    anyspec = pl.BlockSpec(memory_space=pl.ANY)
    return pl.pallas_call(
        body, name=name,
        in_specs=[anyspec] * na, out_specs=[anyspec] * na,
        out_shape=[jax.ShapeDtypeStruct(a.shape, a.dtype) for a in grads],
        scratch_shapes=[pltpu.SemaphoreType.DMA((7, na)), pltpu.SemaphoreType.DMA((7, na)),
                        pltpu.SemaphoreType.DMA((na,))],
    )(*grads)


def _all_reduce_small(v, name):
    r, c_ = v.shape

    def body(v_ref, o_ref, gath, send_sems, recv_sems):
        x, y, c = _place()
        my = 4 * x + 2 * y + c
        gath[my] = v_ref[...]
        sends = []
        for k, (dx, dy, dc) in enumerate(_RELATIONS):
            peer = (_flip(x, dx), _flip(y, dy), _flip(c, dc))
            cp = pltpu.make_async_remote_copy(
                src_ref=v_ref, dst_ref=gath.at[my], send_sem=send_sems.at[k], recv_sem=recv_sems.at[k],
                device_id=peer, device_id_type=MESH)
            cp.start()
            sends.append((cp, 4 * peer[0] + 2 * peer[1] + peer[2], k, peer))
        for cp, pidx, k, peer in sends:
            pltpu.make_async_remote_copy(
                src_ref=v_ref, dst_ref=gath.at[pidx], send_sem=send_sems.at[k], recv_sem=recv_sems.at[k],
                device_id=peer, device_id_type=MESH).wait_recv()
        for cp, *_ in sends:
            cp.wait_send()
        tot = gath[0]
        for k in range(1, N_DEV):
            tot = tot + gath[k]
        o_ref[...] = tot

    vm = pl.BlockSpec(memory_space=pltpu.VMEM)
    return pl.pallas_call(
        body, name=name, in_specs=[vm], out_specs=vm,
        out_shape=jax.ShapeDtypeStruct((r, c_), F32),
        scratch_shapes=[pltpu.VMEM((N_DEV, r, c_), F32), pltpu.SemaphoreType.DMA((7,)),
                        pltpu.SemaphoreType.DMA((7,))],
    )(v)


TM = 512
TQ = 256


def _pad_to(a, axis, size):
    pad = [(0, 0)] * a.ndim
    pad[axis] = (0, size - a.shape[axis])
    return jnp.pad(a, pad)


def _local_step(x, target, ga, gb, gc, conv_full, norm_mix, q_norm, k_norm, norm_ffn):
    depth = gb.shape[0]
    tm, tq = min(TM, x.shape[0]), min(TQ, x.shape[0])
    attn = gb.shape[1] // 2
    nheads = attn // HEAD_DIM
    scale = HEAD_DIM ** -0.5
    saved = []
    for l in range(depth):
        h1 = _rmsnorm_fwd(x, norm_mix[l][None], tm,f"norm_mix_fwd_{l}")
        proj = _mm_blocks(h1, ga, 3 * l, tm,f"proj_in_{l}")
        qk_gain = jnp.concatenate([jnp.tile(q_norm[l], nheads) * scale, jnp.tile(k_norm[l], nheads)])[None]
        qk = _qknorm_fwd(proj, qk_gain, tm,f"qknorm_fwd_{l}")
        o, rtot = _attn_fwd(qk, proj, tq,f"attn_fwd_{l}")
        conv_w8 = _pad_to(conv_full[l], 0, 8)
        cv = _conv_fwd(proj, conv_w8, f"conv_fwd_{l}")
        mix = jnp.concatenate([o, cv], axis=1)
        x1 = _mm_residual(mix, gb, l, x, tm,512, f"proj_out_{l}")
        h2 = _rmsnorm_fwd(x1, norm_ffn[l][None], tm,f"norm_ffn_fwd_{l}")
        g, u, act = _mm_swiglu(h2, ga, 3 * l + 1, 3 * l + 2, tm,f"ffn_up_{l}")
        x2 = _mm_residual(act, gc, l, x1, tm,512, f"ffn_down_{l}")
        saved.append((x, h1, proj, qk_gain, qk, rtot, conv_w8, mix, x1, h2, g, u, act))
        x = x2

    dx, loss = _loss_head(x, target, tm,"loss_head")

    grads = [None] * depth
    small = [None] * depth
    for l in reversed(range(depth)):
        x0, h1, proj, qk_gain, qk, rtot, conv_w8, mix, x1, h2, g, u, act = saved[l]
        d = x0.shape[1]
        dg, du = _mm_nt_swiglu_bwd(dx, gc, l, g, u, tm,f"ffn_down_bwd_{l}")
        d_wdown = _mm_tn(act, dx, 768, d, tm,False, f"dw_down_{l}")
        d_wgate = _mm_tn(h2, dg, d, ga.shape[3], tm,True, f"dw_gate_{l}")
        d_wup = _mm_tn(h2, du, d, ga.shape[3], tm,True, f"dw_up_{l}")
        dh2 = _mm_nt_blocks([dg, du], ga, [3 * l + 1, 3 * l + 2], tm,f"ffn_up_bwd_{l}")
        dx1, dg_ffn = _rmsnorm_bwd(dh2, x1, norm_ffn[l][None], dx, tm,f"norm_ffn_bwd_{l}")
        dmix = _mm_nt(dx1, gb, l, tm,512, f"proj_out_bwd_{l}")
        d_wout = _mm_tn(mix, dx1, 512, d, tm,False, f"dw_out_{l}")
        dcb, dcc, dcu, dconv = _conv_bwd(dmix, proj, conv_w8, f"conv_bwd_{l}")
        dq, dk, dv = _attn_bwd(qk, proj, dmix, rtot, tq,f"attn_bwd_{l}")
        dqk, dg_qk = _qknorm_bwd(jnp.concatenate([dq, dk], axis=1), proj, qk_gain, tm,f"qknorm_bwd_{l}")
        dproj = jnp.concatenate([dqk, dv.astype(BF16), dcb, dcc, dcu], axis=1)
        d_win = _mm_tn(h1, dproj, d, ga.shape[3], tm,True, f"dw_in_{l}")
        dh1 = _mm_nt_blocks([dproj], ga, [3 * l], tm,f"proj_in_bwd_{l}")
        dx, dg_mix = _rmsnorm_bwd(dh1, x0, norm_mix[l][None], dx1, tm,f"norm_mix_bwd_{l}")
        grads[l] = (d_win, d_wgate, d_wup, d_wout, d_wdown)
        dq_gain = jnp.sum(dg_qk[0, :attn].reshape(nheads, HEAD_DIM), axis=0) * scale
        dk_gain = jnp.sum(dg_qk[0, attn:].reshape(nheads, HEAD_DIM), axis=0)
        small[l] = (dg_mix[0], dg_ffn[0], dq_gain, dk_gain, dconv[:3])
    return loss, dx, grads, small


def kernel(x, norm_mix, w_in, q_norm, k_norm, conv_w, w_out, norm_ffn, w_gate, w_up, w_down, loss_target, m_norm_mix, m_w_in, m_q_norm, m_k_norm, m_conv_w, m_w_out, m_norm_ffn, m_w_gate, m_w_up, m_w_down, v_norm_mix, v_w_in, v_q_norm, v_k_norm, v_conv_w, v_w_out, v_norm_ffn, v_w_gate, v_w_up, v_w_down):
    depth, d, in_shard = w_in.shape
    ff_shard = w_gate.shape[2]
    ff_pad = in_shard
    conv_shard = conv_w.shape[2]
    xs = x.reshape(x.shape[-2], d)
    target = loss_target.reshape(xs.shape)

    pa = jnp.stack([w_in, _pad_to(w_gate, 2, ff_pad), _pad_to(w_up, 2, ff_pad)], axis=1)
    pa = pa.reshape(3 * depth, 1, d, in_shard).astype(BF16)
    pb = w_out.astype(BF16)[:, None]
    pc = _pad_to(w_down, 1, ff_pad).astype(BF16)[:, None]
    pd = _pad_to(_pad_to(conv_w.reshape(depth * 3, conv_shard), 0, 8), 1, LANES)[None, None]
    ga, gb, gc, gd = _all_gather([pa, pb, pc, pd], "gather_weights")
    gb = gb.reshape(depth, N_DEV * gb.shape[2], d)
    gc = gc.reshape(depth, N_DEV * ff_pad, d)
    conv_full = gd[0, :, :depth * 3, :conv_shard].transpose(1, 0, 2).reshape(depth, 3, N_DEV * conv_shard)

    loss, grad_x, grads, small = _local_step(xs, target, ga, gb, gc, conv_full, norm_mix, q_norm, k_norm, norm_ffn)

    send = []
    for l in range(depth):
        d_win, d_wgate, d_wup, d_wout, d_wdown = grads[l]
        send += [d_win, d_wgate, d_wup, d_wout.reshape(N_DEV, -1, d), d_wdown.reshape(N_DEV, ff_pad, d)]
    landed = _exchange_blocks(send, "exchange_grads")

    nconv = N_DEV * conv_shard
    rows = []
    for l in range(depth):
        g_mix, g_ffn, g_q, g_k, g_conv = small[l]
        qkrow = _pad_to(jnp.concatenate([g_q, g_k]), 0, d)
        rows += [g_mix[None], g_ffn[None], qkrow[None], _pad_to(g_conv, 1, d)]
    nrow = 6 * depth
    packed = jnp.concatenate(rows + [_pad_to(loss[:1], 1, d)], axis=0)
    packed = _pad_to(packed, 0, ((nrow + 1 + 7) // 8) * 8)
    summed = _all_reduce_small(packed, "reduce_small")
    loss_out = summed[nrow, 0]

    def big(parts, w, m, v, tr, name, rows_=None, cols_=None):
        pr, pcn = parts.shape[1], parts.shape[2]
        w, m, v = [_pad_to(_pad_to(t, 0, pr), 1, pcn) for t in (w, m, v)]
        outs = _adamw(parts, w, m, v, tr, name)
        return [o[:rows_ or pr, :cols_ or pcn] for o in outs]

    res = {}
    for l in range(depth):
        la = landed[5 * l:5 * l + 5]
        res[("w_in", l)] = big(la[0], w_in[l], m_w_in[l], v_w_in[l], 256, f"adamw_in_{l}")
        res[("w_gate", l)] = big(la[1], w_gate[l], m_w_gate[l], v_w_gate[l], 256, f"adamw_gate_{l}", cols_=ff_shard)
        res[("w_up", l)] = big(la[2], w_up[l], m_w_up[l], v_w_up[l], 256, f"adamw_up_{l}", cols_=ff_shard)
        res[("w_out", l)] = big(la[3], w_out[l], m_w_out[l], v_w_out[l], la[3].shape[1], f"adamw_out_{l}")
        res[("w_down", l)] = big(la[4], w_down[l], m_w_down[l], v_w_down[l], 128, f"adamw_down_{l}",
                                 rows_=ff_shard)

    x_, y_, c_ = _place()
    my = 4 * x_ + 2 * y_ + c_
    g_rows, w_rows, m_rows, v_rows = [], [], [], []
    for l in range(depth):
        base = l * 6
        conv_g = lax.dynamic_slice(summed[base + 3:base + 6], (0, my * conv_shard), (3, conv_shard))
        g_rows += [summed[base:base + 3], _pad_to(conv_g, 1, d)]
        for dst, (nm, qn, kn, nf, cw) in ((w_rows, (norm_mix, q_norm, k_norm, norm_ffn, conv_w)),
                                          (m_rows, (m_norm_mix, m_q_norm, m_k_norm, m_norm_ffn, m_conv_w)),
                                          (v_rows, (v_norm_mix, v_q_norm, v_k_norm, v_norm_ffn, v_conv_w))):
            dst += [nm[l][None], nf[l][None], _pad_to(jnp.concatenate([qn[l], kn[l]]), 0, d)[None],
                    _pad_to(cw[l], 1, d)]
    prow = ((nrow + 7) // 8) * 8
    gs, ws, ms, vs = [_pad_to(jnp.concatenate(t, axis=0), 0, prow) for t in (g_rows, w_rows, m_rows, v_rows)]
    sm = _adamw(gs[None], ws, ms, vs, prow, "adamw_small")

    hd = q_norm.shape[1]

    def small_out(t, kind):
        per_layer = []
        for l in range(depth):
            base = l * 6
            per_layer.append({"norm_mix": t[base], "norm_ffn": t[base + 1], "q_norm": t[base + 2, :hd],
                              "k_norm": t[base + 2, hd:2 * hd], "conv_w": t[base + 3:base + 6, :conv_shard]}[kind])
        return jnp.stack(per_layer)

    def big_out(name, i):
        return jnp.stack([res[(name, l)][i] for l in range(depth)])

    outs = [loss_out, grad_x.reshape(x.shape)]
    for i in range(4):
        outs += [small_out(sm[i], "norm_mix"), big_out("w_in", i), small_out(sm[i], "q_norm"),
                 small_out(sm[i], "k_norm"), small_out(sm[i], "conv_w"), big_out("w_out", i),
                 small_out(sm[i], "norm_ffn"), big_out("w_gate", i), big_out("w_up", i), big_out("w_down", i)]
    return tuple(outs)
```

```python
import jax
import jax.numpy as jnp
from jax import lax
from jax.experimental import pallas as pl
from jax.experimental.pallas import tpu as pltpu

F32 = jnp.float32
BF16 = jnp.bfloat16
MESH = pl.DeviceIdType.MESH

N_DEV = 8
LANES = 128
HEAD_DIM = 64
KEY_CHUNK = 128
EPS = 1e-6
VMEM_LIMIT = 48 * 1024 * 1024

ADAM_LR = 0.001
ADAM_B1 = 0.9
ADAM_B2 = 0.999
ADAM_EPS = 1e-08
ADAM_WD = 0.01
ADAM_STEP = 10

NN = (((1,), (0,)), ((), ()))
NT = (((1,), (1,)), ((), ()))
TN = (((0,), (0,)), ((), ()))


def _dot(a, b, dims):
    return lax.dot_general(a.astype(BF16), b.astype(BF16), dims, preferred_element_type=F32)


def _cparams(*sem):
    return pltpu.CompilerParams(dimension_semantics=sem, vmem_limit_bytes=VMEM_LIMIT)


def _split_hi_lo(v):
    hi = v.astype(BF16)
    lo = (v - hi.astype(F32)).astype(BF16)
    return jnp.concatenate([hi, lo], axis=1)


def _rmsnorm_fwd(x, gain, tm, name):
    s, d = x.shape

    def body(x_ref, g_ref, o_ref):
        xv = x_ref[...]
        r = lax.rsqrt(jnp.mean(xv * xv, axis=-1, keepdims=True) + EPS)
        o_ref[...] = ((xv * r) * g_ref[...]).astype(o_ref.dtype)

    return pl.pallas_call(
        body, name=name, grid=(s // tm,),
        in_specs=[pl.BlockSpec((tm, d), lambda i: (i, 0)), pl.BlockSpec((1, d), lambda i: (0, 0))],
        out_specs=pl.BlockSpec((tm, d), lambda i: (i, 0)),
        out_shape=jax.ShapeDtypeStruct((s, d), BF16),
        compiler_params=_cparams("parallel"),
    )(x, gain)


def _rmsnorm_bwd(dh, x, gain, dres, tm, name):
    s, d = x.shape
    nsteps = s // tm

    def body(dh_ref, x_ref, g_ref, dres_ref, dx_ref, dg_ref):
        i = pl.program_id(0)
        xv = x_ref[...]
        r = lax.rsqrt(jnp.mean(xv * xv, axis=-1, keepdims=True) + EPS)
        xhat = xv * r
        dhv = dh_ref[...]
        dxh = dhv * g_ref[...]
        proj = jnp.mean(dxh * xhat, axis=-1, keepdims=True)
        dx_ref[...] = dres_ref[...] + r * (dxh - xhat * proj)
        part = jnp.sum((dhv * xhat).reshape(tm // 8, 8, d), axis=0)

        @pl.when(i == 0)
        def _():
            dg_ref[...] = part

        @pl.when(i > 0)
        def _():
            dg_ref[...] += part

        @pl.when(i == nsteps - 1)
        def _():
            dg_ref[...] = jnp.broadcast_to(jnp.sum(dg_ref[...], axis=0, keepdims=True), (8, d))

    row = pl.BlockSpec((tm, d), lambda i: (i, 0))
    return pl.pallas_call(
        body, name=name, grid=(nsteps,),
        in_specs=[row, row, pl.BlockSpec((1, d), lambda i: (0, 0)), row],
        out_specs=[row, pl.BlockSpec((8, d), lambda i: (0, 0))],
        out_shape=[jax.ShapeDtypeStruct((s, d), F32), jax.ShapeDtypeStruct((8, d), F32)],
        compiler_params=_cparams("arbitrary"),
    )(dh, x, gain, dres)


def _group_mean_matrix():
    r = lax.broadcasted_iota(jnp.int32, (LANES, LANES), 0) // HEAD_DIM
    c = lax.broadcasted_iota(jnp.int32, (LANES, LANES), 1) // HEAD_DIM
    return jnp.where(r == c, 1.0 / HEAD_DIM, 0.0).astype(BF16)


def _group_mean(v, gm):
    hi = v.astype(BF16)
    lo = (v - hi.astype(F32)).astype(BF16)
    return _dot(hi, gm, NN) + _dot(lo, gm, NN)


def _qknorm_fwd(proj, gains, tm, name):
    s = proj.shape[0]
    ncol = gains.shape[1] // LANES

    def body(p_ref, g_ref, gm_ref, o_ref):
        xv = p_ref[...].astype(F32)
        r = lax.rsqrt(_group_mean(xv * xv, gm_ref[...]) + EPS)
        o_ref[...] = ((xv * r) * g_ref[...]).astype(o_ref.dtype)

    blk = pl.BlockSpec((tm, LANES), lambda i, j: (i, j))
    return pl.pallas_call(
        body, name=name, grid=(s // tm, ncol),
        in_specs=[blk, pl.BlockSpec((1, LANES), lambda i, j: (0, j)),
                  pl.BlockSpec((LANES, LANES), lambda i, j: (0, 0))],
        out_specs=blk,
        out_shape=jax.ShapeDtypeStruct((s, ncol * LANES), BF16),
        compiler_params=_cparams("parallel", "parallel"),
    )(proj, gains, _group_mean_matrix())


def _qknorm_bwd(dqk, proj, gains, tm, name):
    s = proj.shape[0]
    ncol = gains.shape[1] // LANES
    nsteps = s // tm

    def body(dy_ref, p_ref, g_ref, gm_ref, dx_ref, dg_ref):
        i = pl.program_id(1)
        gm = gm_ref[...]
        xv = p_ref[...].astype(F32)
        r = lax.rsqrt(_group_mean(xv * xv, gm) + EPS)
        xhat = xv * r
        dy = dy_ref[...]
        dxh = dy * g_ref[...]
        proj_ = _group_mean(dxh * xhat, gm)
        dx_ref[...] = (r * (dxh - xhat * proj_)).astype(dx_ref.dtype)
        part = jnp.sum((dy * xhat).reshape(tm // 8, 8, LANES), axis=0)

        @pl.when(i == 0)
        def _():
            dg_ref[...] = part

        @pl.when(i > 0)
        def _():
            dg_ref[...] += part

        @pl.when(i == nsteps - 1)
        def _():
            dg_ref[...] = jnp.broadcast_to(jnp.sum(dg_ref[...], axis=0, keepdims=True), (8, LANES))

    blk = pl.BlockSpec((tm, LANES), lambda j, i: (i, j))
    return pl.pallas_call(
        body, name=name, grid=(ncol, nsteps),
        in_specs=[blk, blk, pl.BlockSpec((1, LANES), lambda j, i: (0, j)),
                  pl.BlockSpec((LANES, LANES), lambda j, i: (0, 0))],
        out_specs=[blk, pl.BlockSpec((8, LANES), lambda j, i: (0, j))],
        out_shape=[jax.ShapeDtypeStruct((s, ncol * LANES), BF16),
                   jax.ShapeDtypeStruct((8, ncol * LANES), F32)],
        compiler_params=_cparams("parallel", "arbitrary"),
    )(dqk, proj, gains, _group_mean_matrix())


CONV_ROWS = 256
HALO = 8


def _conv_fwd(proj, conv_w8, name):
    s = proj.shape[0]
    nblk = conv_w8.shape[1] // LANES
    first = 3 * nblk
    nchunk = s // CONV_ROWS

    def body(cb_ref, cc_ref, cu_ref, w_ref, y_ref, hpad):
        hpad[pl.ds(0, 2 * HALO), :] = jnp.zeros((2 * HALO, LANES), F32)

        def fill(i, _):
            r0 = pl.multiple_of(i * CONV_ROWS, CONV_ROWS)
            hpad[pl.ds(r0 + 2 * HALO, CONV_ROWS), :] = (
                cc_ref[pl.ds(r0, CONV_ROWS), :].astype(F32) * cu_ref[pl.ds(r0, CONV_ROWS), :].astype(F32))
            return 0

        lax.fori_loop(0, nchunk, fill, 0)
        w0, w1, w2 = w_ref[0:1, :], w_ref[1:2, :], w_ref[2:3, :]

        def conv(i, _):
            r0 = pl.multiple_of(i * CONV_ROWS, CONV_ROWS)
            win = hpad[pl.ds(r0 + HALO, CONV_ROWS + HALO), :]
            c = (w2 * win[HALO:] + w1 * pltpu.roll(win, 1, 0)[HALO:] + w0 * pltpu.roll(win, 2, 0)[HALO:])
            y_ref[pl.ds(r0, CONV_ROWS), :] = (cb_ref[pl.ds(r0, CONV_ROWS), :].astype(F32) * c).astype(y_ref.dtype)
            return 0

        lax.fori_loop(0, nchunk, conv, 0)

    def col(off):
        return pl.BlockSpec((s, LANES), lambda j: (0, off + j))

    return pl.pallas_call(
        body, name=name, grid=(nblk,),
        in_specs=[col(first), col(first + nblk), col(first + 2 * nblk), pl.BlockSpec((8, LANES), lambda j: (0, j))],
        out_specs=pl.BlockSpec((s, LANES), lambda j: (0, j)),
        out_shape=jax.ShapeDtypeStruct((s, nblk * LANES), BF16),
        scratch_shapes=[pltpu.VMEM((s + 2 * HALO, LANES), F32)],
        compiler_params=_cparams("parallel"),
    )(proj, proj, proj, conv_w8)


def _conv_bwd(dmix, proj, conv_w8, name):
    s = proj.shape[0]
    nblk = conv_w8.shape[1] // LANES
    first = 3 * nblk
    nchunk = s // CONV_ROWS

    def body(dy_ref, cb_ref, cc_ref, cu_ref, w_ref, dcb_ref, dcc_ref, dcu_ref, dw_ref, hpad, dcpad):
        hpad[pl.ds(0, 2 * HALO), :] = jnp.zeros((2 * HALO, LANES), F32)
        dcpad[pl.ds(s, 2 * HALO), :] = jnp.zeros((2 * HALO, LANES), F32)

        def fill(i, _):
            r0 = pl.multiple_of(i * CONV_ROWS, CONV_ROWS)
            hpad[pl.ds(r0 + 2 * HALO, CONV_ROWS), :] = (
                cc_ref[pl.ds(r0, CONV_ROWS), :].astype(F32) * cu_ref[pl.ds(r0, CONV_ROWS), :].astype(F32))
            return 0

        lax.fori_loop(0, nchunk, fill, 0)
        w0, w1, w2 = w_ref[0:1, :], w_ref[1:2, :], w_ref[2:3, :]

        def fold(v):
            return jnp.sum(v.reshape(CONV_ROWS // 8, 8, LANES), axis=0)

        def first_pass(i, acc):
            a0, a1, a2 = acc
            r0 = pl.multiple_of(i * CONV_ROWS, CONV_ROWS)
            win = hpad[pl.ds(r0 + HALO, CONV_ROWS + HALO), :]
            h0 = win[HALO:]
            h1 = pltpu.roll(win, 1, 0)[HALO:]
            h2 = pltpu.roll(win, 2, 0)[HALO:]
            c = w2 * h0 + w1 * h1 + w0 * h2
            dy = dy_ref[pl.ds(r0, CONV_ROWS), :]
            dcb_ref[pl.ds(r0, CONV_ROWS), :] = (dy * c).astype(dcb_ref.dtype)
            dc = dy * cb_ref[pl.ds(r0, CONV_ROWS), :].astype(F32)
            dcpad[pl.ds(r0, CONV_ROWS), :] = dc
            return a0 + fold(dc * h2), a1 + fold(dc * h1), a2 + fold(dc * h0)

        z8 = jnp.zeros((8, LANES), F32)
        a0, a1, a2 = lax.fori_loop(0, nchunk, first_pass, (z8, z8, z8))
        dw_ref[...] = jnp.concatenate(
            [jnp.sum(a0, axis=0, keepdims=True), jnp.sum(a1, axis=0, keepdims=True),
             jnp.sum(a2, axis=0, keepdims=True), jnp.zeros((5, LANES), F32)], axis=0)

        def second_pass(i, _):
            r0 = pl.multiple_of(i * CONV_ROWS, CONV_ROWS)
            win = dcpad[pl.ds(r0, CONV_ROWS + HALO), :]
            n = CONV_ROWS + HALO
            dh = (w2 * win[:CONV_ROWS] + w1 * pltpu.roll(win, n - 1, 0)[:CONV_ROWS]
                  + w0 * pltpu.roll(win, n - 2, 0)[:CONV_ROWS])
            dcc_ref[pl.ds(r0, CONV_ROWS), :] = (dh * cu_ref[pl.ds(r0, CONV_ROWS), :].astype(F32)).astype(dcc_ref.dtype)
            dcu_ref[pl.ds(r0, CONV_ROWS), :] = (dh * cc_ref[pl.ds(r0, CONV_ROWS), :].astype(F32)).astype(dcu_ref.dtype)
            return 0

        lax.fori_loop(0, nchunk, second_pass, 0)

    def col(off):
        return pl.BlockSpec((s, LANES), lambda j: (0, off + j))

    out = pl.BlockSpec((s, LANES), lambda j: (0, j))
    return pl.pallas_call(
        body, name=name, grid=(nblk,),
        in_specs=[col(nblk), col(first), col(first + nblk), col(first + 2 * nblk),
                  pl.BlockSpec((8, LANES), lambda j: (0, j))],
        out_specs=[out, out, out, pl.BlockSpec((8, LANES), lambda j: (0, j))],
        out_shape=[jax.ShapeDtypeStruct((s, nblk * LANES), BF16)] * 3 + [jax.ShapeDtypeStruct((8, nblk * LANES), F32)],
        scratch_shapes=[pltpu.VMEM((s + 2 * HALO, LANES), F32), pltpu.VMEM((s + 2 * HALO, LANES), F32)],
        compiler_params=_cparams("parallel"),
    )(dmix, proj, proj, proj, conv_w8)


LOG2E = 1.4426950408889634
LN2 = 0.6931471805599453
NEG_BIG = -1e30


def _cumsum_matrix(kind):
    j = lax.broadcasted_iota(jnp.int32, (KEY_CHUNK, 2 * KEY_CHUNK), 0)
    c = lax.broadcasted_iota(jnp.int32, (KEY_CHUNK, 2 * KEY_CHUNK), 1)
    tri = {"after": j > c, "upto": j <= c, "before": j < c}[kind]
    return jnp.where((c >= KEY_CHUNK) | tri, 1.0, 0.0).astype(BF16)


def _stack_heads(t, m0):
    zero = jnp.zeros_like(t)
    return jnp.concatenate([jnp.where(m0, t, zero), jnp.where(m0, zero, t)], axis=0)


def _softplus2(z):
    sp = jnp.maximum(z, 0.0) + jnp.log2(1.0 + jnp.exp2(-jnp.abs(z)))
    return sp, z - sp


def _key_chunk(ref, kc):
    return ref[pl.ds(pl.multiple_of(kc * KEY_CHUNK, KEY_CHUNK), KEY_CHUNK), :]


def _attn_fwd(qk, proj, tq, name):
    s = qk.shape[0]
    nhp = qk.shape[1] // (2 * LANES)
    nc = tq // KEY_CHUNK

    def body(q_ref, k_ref, v_ref, cm_ref, o_ref, r_ref):
        qi = pl.program_id(1)
        n = qi * nc
        last = jnp.maximum(n - 1, 0)
        m0 = lax.broadcasted_iota(jnp.int32, (1, LANES), 1) < HEAD_DIM
        qs = _stack_heads(q_ref[...], m0)
        cm = cm_ref[...]
        qpos = qi * tq + (lax.broadcasted_iota(jnp.int32, (2 * tq, KEY_CHUNK), 0) & (tq - 1))
        kcol = lax.broadcasted_iota(jnp.int32, (2 * tq, KEY_CHUNK), 1)

        def chunk_at(i):
            return jnp.clip(n - 1 - i, 0, last)

        def scores(kc):
            return _dot(qs, _key_chunk(k_ref, kc), NT)

        def soft(z, mask):
            sp, ls = _softplus2(z)
            if mask is not None:
                sp = jnp.where(mask, sp, 0.0)
            return sp.astype(BF16), ls

        def weights(ls, cs, rs, mask):
            a = jnp.exp2(ls - cs[:, :KEY_CHUNK] - rs)
            if mask is not None:
                a = jnp.where(mask, a, 0.0)
            return a.astype(BF16), rs + cs[:, KEY_CHUNK:]

        def values(ab, kc):
            return _dot(jnp.concatenate([ab[:tq], ab[tq:]], axis=1), _stack_heads(_key_chunk(v_ref, kc), m0), NN)

        rs = jnp.zeros((2 * tq, LANES), F32)
        acc = jnp.zeros((tq, LANES), F32)
        for d in range(nc):
            kc = n + nc - 1 - d
            mask = (kcol + kc * KEY_CHUNK) < qpos
            spb, ls = soft(scores(kc), mask)
            ab, rs = weights(ls, _dot(spb, cm, NN), rs, mask)
            acc = acc + values(ab, kc)

        def step(i, st):
            z, spb, ls, ab, rs, acc = st
            z_next = scores(chunk_at(i + 1))
            cs = _dot(spb, cm, NN)
            pv = values(ab, chunk_at(i - 2))
            spb_new, ls_new = soft(z, None)
            ab_new, rs = weights(ls, cs, rs, None)
            return z_next, spb_new, ls_new, ab_new, rs, acc + pv

        zero = rs * 0.0
        empty = zero.astype(BF16)
        st = (scores(chunk_at(0)), empty, zero + NEG_BIG, empty, rs, acc)
        _, spb, ls, ab, rs, acc = lax.fori_loop(0, n, step, st)
        pv = values(ab, chunk_at(n - 2))
        ab, rs = weights(ls, _dot(spb, cm, NN), rs, None)
        acc = acc + pv + values(ab, chunk_at(n - 1))
        o_ref[...] = acc.astype(o_ref.dtype)
        r_ref[:, :LANES] = rs[:tq]
        r_ref[:, LANES:] = rs[tq:]

    return pl.pallas_call(
        body, name=name, grid=(nhp, s // tq),
        in_specs=[pl.BlockSpec((tq, LANES), lambda p, i: (i, p)),
                  pl.BlockSpec((s, LANES), lambda p, i: (0, nhp + p)),
                  pl.BlockSpec((s, LANES), lambda p, i: (0, 2 * nhp + p)),
                  pl.BlockSpec((KEY_CHUNK, 2 * KEY_CHUNK), lambda p, i: (0, 0))],
        out_specs=[pl.BlockSpec((tq, LANES), lambda p, i: (i, p)),
                   pl.BlockSpec((tq, 2 * LANES), lambda p, i: (i, p))],
        out_shape=[jax.ShapeDtypeStruct((s, nhp * LANES), BF16),
                   jax.ShapeDtypeStruct((s, nhp * 2 * LANES), F32)],
        compiler_params=_cparams("parallel", "parallel"),
    )(qk, qk, proj, _cumsum_matrix("after"))


def _attn_bwd(qk, proj, dmix, rtot, tq, name):
    s = qk.shape[0]
    nhp = qk.shape[1] // (2 * LANES)
    nc = tq // KEY_CHUNK

    def body(q_ref, k_ref, v_ref, do_ref, r_ref, cmi_ref, cme_ref, dq_ref, dk_ref, dv_ref):
        qi = pl.program_id(1)

        @pl.when(qi == 0)
        def _():
            dk_ref[...] = jnp.zeros_like(dk_ref)
            dv_ref[...] = jnp.zeros_like(dv_ref)

        n = qi * nc
        last = jnp.maximum(n - 1, 0)
        m0 = lax.broadcasted_iota(jnp.int32, (1, LANES), 1) < HEAD_DIM
        qs = _stack_heads(q_ref[...], m0)
        do = do_ref[...]
        dos = _stack_heads(do.astype(BF16), m0)
        dosl = _stack_heads((do * LN2).astype(BF16), m0)
        cmi = cmi_ref[...]
        cme = cme_ref[...]
        qpos = qi * tq + (lax.broadcasted_iota(jnp.int32, (2 * tq, KEY_CHUNK), 0) & (tq - 1))
        kcol = lax.broadcasted_iota(jnp.int32, (2 * tq, KEY_CHUNK), 1)

        def chunk_at(i):
            return jnp.clip(i, 0, last)

        def scores(kc):
            return _dot(qs, _key_chunk(k_ref, kc), NT)

        def soft(z, mask):
            sp, ls = _softplus2(z)
            if mask is not None:
                sp = jnp.where(mask, sp, 0.0)
            return sp.astype(BF16), ls

        def weights(ls, cs, da, pr, kc, mask):
            a = jnp.exp2(ls - (pr - cs[:, :KEY_CHUNK]))
            if mask is not None:
                a = jnp.where(mask, a, 0.0)
            gb = (a * da).astype(BF16)
            ks = pl.multiple_of(kc * KEY_CHUNK, KEY_CHUNK)
            dv_ref[pl.ds(ks, KEY_CHUNK), :] += _dot(a, dos, TN)
            return gb, jnp.exp2(ls), pr - cs[:, KEY_CHUNK:]

        def score_grads(gb, sig, cg, gs, dq, kc, mask):
            dz = gb.astype(F32) * (1.0 - sig) - sig * (gs + cg[:, :KEY_CHUNK])
            if mask is not None:
                dz = jnp.where(mask, dz, 0.0)
            dzb = dz.astype(BF16)
            ks = pl.multiple_of(kc * KEY_CHUNK, KEY_CHUNK)
            dk_ref[pl.ds(ks, KEY_CHUNK), :] += _dot(dzb, qs, TN)
            dq = dq + _dot(jnp.concatenate([dzb[:tq], dzb[tq:]], axis=1), _stack_heads(_key_chunk(k_ref, kc), m0), NN)
            return gs + cg[:, KEY_CHUNK:], dq

        def step(i, st):
            z, spb, ls, gb, sig, pr, gs, dq = st
            k1, k2 = chunk_at(i - 1), chunk_at(i - 2)
            z_next = scores(chunk_at(i + 1))
            cs = _dot(spb, cmi, NN)
            da = _dot(dosl, _key_chunk(v_ref, k1), NT)
            cg = _dot(gb, cme, NN)
            spb_new, ls_new = soft(z, None)
            gb_new, sig_new, pr = weights(ls, cs, da, pr, k1, None)
            gs, dq = score_grads(gb, sig, cg, gs, dq, k2, None)
            return z_next, spb_new, ls_new, gb_new, sig_new, pr, gs, dq

        pr = jnp.concatenate([r_ref[:, :LANES], r_ref[:, LANES:]], axis=0)
        zero = pr * 0.0
        empty = zero.astype(BF16)
        st = (scores(chunk_at(0)), empty, zero + NEG_BIG, empty, zero, pr, zero, zero[:tq])
        st = lax.fori_loop(0, n, step, st)
        z, spb, ls, gb, sig, pr, gs, dq = st
        k1, k2 = chunk_at(n - 1), chunk_at(n - 2)
        gb_new, sig_new, pr = weights(ls, _dot(spb, cmi, NN), _dot(dosl, _key_chunk(v_ref, k1), NT), pr, k1, None)
        gs, dq = score_grads(gb, sig, _dot(gb, cme, NN), gs, dq, k2, None)
        gs, dq = score_grads(gb_new, sig_new, _dot(gb_new, cme, NN), gs, dq, k1, None)
        for d in range(nc):
            kc = n + d
            mask = (kcol + kc * KEY_CHUNK) < qpos
            spb, ls = soft(scores(kc), mask)
            gb, sig, pr = weights(ls, _dot(spb, cmi, NN), _dot(dosl, _key_chunk(v_ref, kc), NT), pr, kc, mask)
            gs, dq = score_grads(gb, sig, _dot(gb, cme, NN), gs, dq, kc, mask)
        dq_ref[...] = dq

    qblk = pl.BlockSpec((tq, LANES), lambda p, i: (i, p))
    full = pl.BlockSpec((s, LANES), lambda p, i: (0, p))
    cmspec = pl.BlockSpec((KEY_CHUNK, 2 * KEY_CHUNK), lambda p, i: (0, 0))
    shape = jax.ShapeDtypeStruct((s, nhp * LANES), F32)
    return pl.pallas_call(
        body, name=name, grid=(nhp, s // tq),
        in_specs=[qblk,
                  pl.BlockSpec((s, LANES), lambda p, i: (0, nhp + p)),
                  pl.BlockSpec((s, LANES), lambda p, i: (0, 2 * nhp + p)),
                  qblk,
                  pl.BlockSpec((tq, 2 * LANES), lambda p, i: (i, p)),
                  cmspec, cmspec],
        out_specs=[qblk, full, full],
        out_shape=[shape, shape, shape],
        compiler_params=_cparams("parallel", "arbitrary"),
    )(qk, qk, proj, dmix, rtot, _cumsum_matrix("upto"), _cumsum_matrix("before"))


def _mm_blocks(h, ga, widx, tm, name):
    s, d = h.shape
    nb, cols = ga.shape[1], ga.shape[3]

    def body(a_ref, b_ref, o_ref):
        o_ref[...] = _dot(a_ref[...], b_ref[...], NN).astype(o_ref.dtype)

    return pl.pallas_call(
        body, name=name, grid=(s // tm, nb),
        in_specs=[pl.BlockSpec((tm, d), lambda i, j: (i, 0)),
                  pl.BlockSpec((None, None, d, cols), lambda i, j: (widx, j, 0, 0))],
        out_specs=pl.BlockSpec((tm, cols), lambda i, j: (i, j)),
        out_shape=jax.ShapeDtypeStruct((s, nb * cols), BF16),
        compiler_params=_cparams("parallel", "arbitrary"),
    )(h, ga)


def _mm_swiglu(h, ga, gidx, uidx, tm, name):
    s, d = h.shape
    nb, cols = ga.shape[1], ga.shape[3]

    def body(a_ref, bg_ref, bu_ref, g_ref, u_ref, act_ref):
        a = a_ref[...]
        g = _dot(a, bg_ref[...], NN)
        u = _dot(a, bu_ref[...], NN)
        g_ref[...] = g.astype(g_ref.dtype)
        u_ref[...] = u.astype(u_ref.dtype)
        act_ref[...] = (g * (1.0 / (1.0 + jnp.exp(-g))) * u).astype(act_ref.dtype)

    def wspec(idx):
        return pl.BlockSpec((None, None, d, cols), lambda i, j: (idx, j, 0, 0))

    out = pl.BlockSpec((tm, cols), lambda i, j: (i, j))
    shape = jax.ShapeDtypeStruct((s, nb * cols), BF16)
    return pl.pallas_call(
        body, name=name, grid=(s // tm, nb),
        in_specs=[pl.BlockSpec((tm, d), lambda i, j: (i, 0)), wspec(gidx), wspec(uidx)],
        out_specs=[out, out, out], out_shape=[shape, shape, shape],
        compiler_params=_cparams("parallel", "arbitrary"),
    )(h, ga, ga)


def _mm_residual(a, w3, lidx, res, tm, tn, name):
    s, k = a.shape
    n = w3.shape[2]

    def body(a_ref, b_ref, r_ref, o_ref):
        o_ref[...] = r_ref[...] + _dot(a_ref[...], b_ref[...], NN)

    return pl.pallas_call(
        body, name=name, grid=(s // tm, n // tn),
        in_specs=[pl.BlockSpec((tm, k), lambda i, j: (i, 0)),
                  pl.BlockSpec((None, k, tn), lambda i, j: (lidx, 0, j)),
                  pl.BlockSpec((tm, tn), lambda i, j: (i, j))],
        out_specs=pl.BlockSpec((tm, tn), lambda i, j: (i, j)),
        out_shape=jax.ShapeDtypeStruct((s, n), F32),
        compiler_params=_cparams("parallel", "arbitrary"),
    )(a, w3, res)


def _mm_nt(a, w3, lidx, tm, tn, name):
    s, k = a.shape
    n = w3.shape[1]

    def body(a_ref, b_ref, o_ref):
        o_ref[...] = _dot(a_ref[...], b_ref[...], NT)

    return pl.pallas_call(
        body, name=name, grid=(s // tm, n // tn),
        in_specs=[pl.BlockSpec((tm, k), lambda i, j: (i, 0)),
                  pl.BlockSpec((None, tn, k), lambda i, j: (lidx, j, 0))],
        out_specs=pl.BlockSpec((tm, tn), lambda i, j: (i, j)),
        out_shape=jax.ShapeDtypeStruct((s, n), F32),
        compiler_params=_cparams("parallel", "arbitrary"),
    )(a, w3)


def _mm_nt_swiglu_bwd(dx, wd3, lidx, g, u, tm, name):
    s, d = dx.shape
    cols = g.shape[1] // N_DEV

    def body(a_ref, b_ref, g_ref, u_ref, dg_ref, du_ref):
        dact = _dot(a_ref[...], b_ref[...], NT)
        gv = g_ref[...].astype(F32)
        sig = 1.0 / (1.0 + jnp.exp(-gv))
        du_ref[...] = (dact * (gv * sig)).astype(du_ref.dtype)
        dg_ref[...] = (dact * u_ref[...].astype(F32) * (sig * (1.0 + gv * (1.0 - sig)))).astype(dg_ref.dtype)

    blk = pl.BlockSpec((tm, cols), lambda i, j: (i, j))
    shape = jax.ShapeDtypeStruct(g.shape, BF16)
    return pl.pallas_call(
        body, name=name, grid=(s // tm, N_DEV),
        in_specs=[pl.BlockSpec((tm, d), lambda i, j: (i, 0)),
                  pl.BlockSpec((None, cols, d), lambda i, j: (lidx, j, 0)), blk, blk],
        out_specs=[blk, blk], out_shape=[shape, shape],
        compiler_params=_cparams("parallel", "arbitrary"),
    )(dx, wd3, g, u)


def _mm_nt_blocks(das, ga, widxs, tm, name):
    s = das[0].shape[0]
    nb, d, cols = ga.shape[1], ga.shape[2], ga.shape[3]
    nw = len(das)

    def body(*refs):
        a_refs, b_refs, o_ref = refs[:nw], refs[nw:2 * nw], refs[2 * nw]
        k = pl.program_id(1)
        part = _dot(a_refs[0][...], b_refs[0][...], NT)
        for w in range(1, nw):
            part = part + _dot(a_refs[w][...], b_refs[w][...], NT)

        @pl.when(k == 0)
        def _():
            o_ref[...] = part

        @pl.when(k > 0)
        def _():
            o_ref[...] += part

    def wspec(idx):
        return pl.BlockSpec((None, None, d, cols), lambda i, k: (idx, k, 0, 0))

    return pl.pallas_call(
        body, name=name, grid=(s // tm, nb),
        in_specs=[pl.BlockSpec((tm, cols), lambda i, k: (i, k))] * nw + [wspec(i) for i in widxs],
        out_specs=pl.BlockSpec((tm, d), lambda i, k: (i, 0)),
        out_shape=jax.ShapeDtypeStruct((s, d), F32),
        compiler_params=_cparams("parallel", "arbitrary"),
    )(*das, *([ga] * nw))


def _mm_tn(a, b, ta, tb, tk, out_blocks, name):
    s, ka = a.shape
    nb = b.shape[1]
    nk = s // tk

    def body(a_ref, b_ref, o_ref):
        k = pl.program_id(2)
        part = _dot(a_ref[...], b_ref[...], TN)

        @pl.when(k == 0)
        def _():
            o_ref[...] = part

        @pl.when(k > 0)
        def _():
            o_ref[...] += part

    if out_blocks:
        out_spec = pl.BlockSpec((None, ta, tb), lambda i, j, k: (j, i, 0))
        out_shape = jax.ShapeDtypeStruct((nb // tb, ka, tb), F32)
    else:
        out_spec = pl.BlockSpec((ta, tb), lambda i, j, k: (i, j))
        out_shape = jax.ShapeDtypeStruct((ka, nb), F32)
    return pl.pallas_call(
        body, name=name, grid=(ka // ta, nb // tb, nk),
        in_specs=[pl.BlockSpec((tk, ta), lambda i, j, k: (k, i)),
                  pl.BlockSpec((tk, tb), lambda i, j, k: (k, j))],
        out_specs=out_spec, out_shape=out_shape,
        compiler_params=_cparams("parallel", "parallel", "arbitrary"),
    )(a, b)


def _loss_head(y, target, tm, name):
    s, d = y.shape
    nsteps = s // tm

    def body(y_ref, t_ref, dy_ref, l_ref, acc):
        i = pl.program_id(0)
        diff = y_ref[...] - t_ref[...]
        dy_ref[...] = diff * (1.0 / d)
        part = jnp.sum((diff * diff).reshape(tm // 8, 8, d), axis=0)

        @pl.when(i == 0)
        def _():
            acc[...] = part

        @pl.when(i > 0)
        def _():
            acc[...] += part

        @pl.when(i == nsteps - 1)
        def _():
            tot = jnp.sum(jnp.sum(acc[...], axis=1, keepdims=True), axis=0, keepdims=True)
            l_ref[...] = jnp.broadcast_to(tot * (0.5 / d), (8, LANES))

    row = pl.BlockSpec((tm, d), lambda i: (i, 0))
    return pl.pallas_call(
        body, name=name, grid=(nsteps,),
        in_specs=[row, row],
        out_specs=[row, pl.BlockSpec((8, LANES), lambda i: (0, 0))],
        out_shape=[jax.ShapeDtypeStruct((s, d), F32), jax.ShapeDtypeStruct((8, LANES), F32)],
        scratch_shapes=[pltpu.VMEM((8, d), F32)],
        compiler_params=_cparams("arbitrary"),
    )(y, target)


def _adamw(parts, w, m, v, tr, name):
    p, rows, cols = parts.shape
    c1 = 1.0 / (1.0 - ADAM_B1 ** ADAM_STEP)
    c2 = 1.0 / (1.0 - ADAM_B2 ** ADAM_STEP)

    def body(p_ref, w_ref, m_ref, v_ref, g_ref, d_ref, nm_ref, nv_ref):
        g = p_ref[0]
        for k in range(1, p):
            g = g + p_ref[k]
        nm = ADAM_B1 * m_ref[...] + (1.0 - ADAM_B1) * g
        nv = ADAM_B2 * v_ref[...] + (1.0 - ADAM_B2) * (g * g)
        g_ref[...] = g
        nm_ref[...] = nm
        nv_ref[...] = nv
        d_ref[...] = -ADAM_LR * ((nm * c1) / (jnp.sqrt(nv * c2) + ADAM_EPS) + ADAM_WD * w_ref[...])

    blk = pl.BlockSpec((tr, cols), lambda i: (i, 0))
    shape = jax.ShapeDtypeStruct((rows, cols), F32)
    return pl.pallas_call(
        body, name=name, grid=(rows // tr,),
        in_specs=[pl.BlockSpec((p, tr, cols), lambda i: (0, i, 0)), blk, blk, blk],
        out_specs=[blk] * 4, out_shape=[shape] * 4,
        compiler_params=_cparams("parallel"),
    )(parts, w, m, v)


def _place():
    x, y, c = lax.axis_index("x"), lax.axis_index("y"), lax.axis_index("c")
    return x, y, c


def _all_gather(shards, name):
    na = len(shards)

    def body(*refs):
        srcs, dsts = refs[:na], refs[na:2 * na]
        send_sems, recv_sems, local_sems = refs[2 * na:]
        x, y, c = _place()
        me, sibling = (x, y, c), (x, y, 1 - c)
        chips = [(1 - x, y), (x, 1 - y), (1 - x, 1 - y)]

        def slot(a, dev):
            return dsts[a].at[:, pl.ds(4 * dev[0] + 2 * dev[1] + dev[2], 1)]

        def copy(k, a, block, to, from_shard=False):
            return pltpu.make_async_remote_copy(
                src_ref=srcs[a] if from_shard else slot(a, block), dst_ref=slot(a, block),
                send_sem=send_sems.at[k, a], recv_sem=recv_sems.at[k, a], device_id=to, device_id_type=MESH)

        mine = [pltpu.make_async_copy(srcs[a], slot(a, me), local_sems.at[a]) for a in range(na)]
        for cp in mine:
            cp.start()
        first = [copy(0, a, me, sibling, True) for a in range(na)]
        first += [copy(1 + j, a, me, (*chip, c), True) for j, chip in enumerate(chips) for a in range(na)]
        for cp in first:
            cp.start()
        passed = []
        for j, chip in enumerate(chips):
            for a in range(na):
                copy(1 + j, a, (*chip, c), me).wait_recv()
                fwd = copy(4 + j, a, (*chip, c), sibling)
                fwd.start()
                passed.append(fwd)
        for a in range(na):
            copy(0, a, sibling, me).wait_recv()
        for j, chip in enumerate(chips):
            for a in range(na):
                copy(4 + j, a, (*chip, 1 - c), me).wait_recv()
        for cp in first + passed:
            cp.wait_send()
        for cp in mine:
            cp.wait()

    anyspec = pl.BlockSpec(memory_space=pl.ANY)
    return pl.pallas_call(
        body, name=name,
        in_specs=[anyspec] * na, out_specs=[anyspec] * na,
        out_shape=[jax.ShapeDtypeStruct((a.shape[0], N_DEV) + a.shape[2:], a.dtype) for a in shards],
        scratch_shapes=[pltpu.SemaphoreType.DMA((7, na)), pltpu.SemaphoreType.DMA((7, na)),
                        pltpu.SemaphoreType.DMA((na,))],
    )(*shards)


_RELATIONS = [(dx, dy, dc) for dx in (0, 1) for dy in (0, 1) for dc in (0, 1)][1:]


def _flip(v, d):
    return 1 - v if d else v


def _exchange_blocks(grads, name):
    na = len(grads)

    def body(*refs):
        srcs, dsts = refs[:na], refs[na:2 * na]
        send_sems, recv_sems, local_sems = refs[2 * na:]
        x, y, c = _place()
        my = 4 * x + 2 * y + c
        mine = [pltpu.make_async_copy(srcs[a].at[pl.ds(my, 1)], dsts[a].at[pl.ds(my, 1)], local_sems.at[a])
                for a in range(na)]
        for cp in mine:
            cp.start()
        sends = []
        for k, (dx, dy, dc) in enumerate(_RELATIONS):
            peer = (_flip(x, dx), _flip(y, dy), _flip(c, dc))
            pidx = 4 * peer[0] + 2 * peer[1] + peer[2]
            for a in range(na):
                cp = pltpu.make_async_remote_copy(
                    src_ref=srcs[a].at[pl.ds(pidx, 1)], dst_ref=dsts[a].at[pl.ds(my, 1)],
                    send_sem=send_sems.at[k, a], recv_sem=recv_sems.at[k, a], device_id=peer, device_id_type=MESH)
                cp.start()
                sends.append((cp, pidx, k, a, peer))
        for cp, pidx, k, a, peer in sends:
            pltpu.make_async_remote_copy(
                src_ref=srcs[a].at[pl.ds(pidx, 1)], dst_ref=dsts[a].at[pl.ds(pidx, 1)],
                send_sem=send_sems.at[k, a], recv_sem=recv_sems.at[k, a], device_id=peer,
                device_id_type=MESH).wait_recv()
        for cp, *_ in sends:
            cp.wait_send()
        for cp in mine:
            cp.wait()

    anyspec = pl.BlockSpec(memory_space=pl.ANY)
    return pl.pallas_call(
        body, name=name,
        in_specs=[anyspec] * na, out_specs=[anyspec] * na,
        out_shape=[jax.ShapeDtypeStruct(a.shape, a.dtype) for a in grads],
        scratch_shapes=[pltpu.SemaphoreType.DMA((7, na)), pltpu.SemaphoreType.DMA((7, na)),
                        pltpu.SemaphoreType.DMA((na,))],
    )(*grads)


def _all_reduce_small(v, name):
    r, c_ = v.shape

    def body(v_ref, o_ref, gath, send_sems, recv_sems):
        x, y, c = _place()
        my = 4 * x + 2 * y + c
        gath[my] = v_ref[...]
        sends = []
        for k, (dx, dy, dc) in enumerate(_RELATIONS):
            peer = (_flip(x, dx), _flip(y, dy), _flip(c, dc))
            cp = pltpu.make_async_remote_copy(
                src_ref=v_ref, dst_ref=gath.at[my], send_sem=send_sems.at[k], recv_sem=recv_sems.at[k],
                device_id=peer, device_id_type=MESH)
            cp.start()
            sends.append((cp, 4 * peer[0] + 2 * peer[1] + peer[2], k, peer))
        for cp, pidx, k, peer in sends:
            pltpu.make_async_remote_copy(
                src_ref=v_ref, dst_ref=gath.at[pidx], send_sem=send_sems.at[k], recv_sem=recv_sems.at[k],
                device_id=peer, device_id_type=MESH).wait_recv()
        for cp, *_ in sends:
            cp.wait_send()
        tot = gath[0]
        for k in range(1, N_DEV):
            tot = tot + gath[k]
        o_ref[...] = tot

    vm = pl.BlockSpec(memory_space=pltpu.VMEM)
    return pl.pallas_call(
        body, name=name, in_specs=[vm], out_specs=vm,
        out_shape=jax.ShapeDtypeStruct((r, c_), F32),
        scratch_shapes=[pltpu.VMEM((N_DEV, r, c_), F32), pltpu.SemaphoreType.DMA((7,)),
                        pltpu.SemaphoreType.DMA((7,))],
    )(v)


TM = 512
TQ = 256


def _pad_to(a, axis, size):
    pad = [(0, 0)] * a.ndim
    pad[axis] = (0, size - a.shape[axis])
    return jnp.pad(a, pad)


def _local_step(x, target, ga, gb, gc, conv_full, norm_mix, q_norm, k_norm, norm_ffn):
    depth = gb.shape[0]
    tm, tq = min(TM, x.shape[0]), min(TQ, x.shape[0])
    attn = gb.shape[1] // 2
    nheads = attn // HEAD_DIM
    scale = HEAD_DIM ** -0.5 * LOG2E
    saved = []
    for l in range(depth):
        h1 = _rmsnorm_fwd(x, norm_mix[l][None], tm,f"norm_mix_fwd_{l}")
        proj = _mm_blocks(h1, ga, 3 * l, tm,f"proj_in_{l}")
        qk_gain = jnp.concatenate([jnp.tile(q_norm[l], nheads) * scale, jnp.tile(k_norm[l], nheads)])[None]
        qk = _qknorm_fwd(proj, qk_gain, tm,f"qknorm_fwd_{l}")
        o, rtot = _attn_fwd(qk, proj, tq,f"attn_fwd_{l}")
        conv_w8 = _pad_to(conv_full[l], 0, 8)
        cv = _conv_fwd(proj, conv_w8, f"conv_fwd_{l}")
        mix = jnp.concatenate([o, cv], axis=1)
        x1 = _mm_residual(mix, gb, l, x, tm,512, f"proj_out_{l}")
        h2 = _rmsnorm_fwd(x1, norm_ffn[l][None], tm,f"norm_ffn_fwd_{l}")
        g, u, act = _mm_swiglu(h2, ga, 3 * l + 1, 3 * l + 2, tm,f"ffn_up_{l}")
        x2 = _mm_residual(act, gc, l, x1, tm,512, f"ffn_down_{l}")
        saved.append((x, h1, proj, qk_gain, qk, rtot, conv_w8, mix, x1, h2, g, u, act))
        x = x2

    dx, loss = _loss_head(x, target, tm,"loss_head")

    grads = [None] * depth
    small = [None] * depth
    for l in reversed(range(depth)):
        x0, h1, proj, qk_gain, qk, rtot, conv_w8, mix, x1, h2, g, u, act = saved[l]
        d = x0.shape[1]
        dg, du = _mm_nt_swiglu_bwd(dx, gc, l, g, u, tm,f"ffn_down_bwd_{l}")
        d_wdown = _mm_tn(act, dx, 768, d, tm,False, f"dw_down_{l}")
        d_wgate = _mm_tn(h2, dg, d, ga.shape[3], tm,True, f"dw_gate_{l}")
        d_wup = _mm_tn(h2, du, d, ga.shape[3], tm,True, f"dw_up_{l}")
        dh2 = _mm_nt_blocks([dg, du], ga, [3 * l + 1, 3 * l + 2], tm,f"ffn_up_bwd_{l}")
        dx1, dg_ffn = _rmsnorm_bwd(dh2, x1, norm_ffn[l][None], dx, tm,f"norm_ffn_bwd_{l}")
        dmix = _mm_nt(dx1, gb, l, tm,512, f"proj_out_bwd_{l}")
        d_wout = _mm_tn(mix, dx1, 512, d, tm,False, f"dw_out_{l}")
        dcb, dcc, dcu, dconv = _conv_bwd(dmix, proj, conv_w8, f"conv_bwd_{l}")
        dq, dk, dv = _attn_bwd(qk, proj, dmix, rtot, tq,f"attn_bwd_{l}")
        dqk, dg_qk = _qknorm_bwd(jnp.concatenate([dq, dk], axis=1), proj, qk_gain, tm,f"qknorm_bwd_{l}")
        dproj = jnp.concatenate([dqk, dv.astype(BF16), dcb, dcc, dcu], axis=1)
        d_win = _mm_tn(h1, dproj, d, ga.shape[3], tm,True, f"dw_in_{l}")
        dh1 = _mm_nt_blocks([dproj], ga, [3 * l], tm,f"proj_in_bwd_{l}")
        dx, dg_mix = _rmsnorm_bwd(dh1, x0, norm_mix[l][None], dx1, tm,f"norm_mix_bwd_{l}")
        grads[l] = (d_win, d_wgate, d_wup, d_wout, d_wdown)
        dq_gain = jnp.sum(dg_qk[0, :attn].reshape(nheads, HEAD_DIM), axis=0) * scale
        dk_gain = jnp.sum(dg_qk[0, attn:].reshape(nheads, HEAD_DIM), axis=0)
        small[l] = (dg_mix[0], dg_ffn[0], dq_gain, dk_gain, dconv[:3])
    return loss, dx, grads, small


def kernel(x, norm_mix, w_in, q_norm, k_norm, conv_w, w_out, norm_ffn, w_gate, w_up, w_down, loss_target, m_norm_mix, m_w_in, m_q_norm, m_k_norm, m_conv_w, m_w_out, m_norm_ffn, m_w_gate, m_w_up, m_w_down, v_norm_mix, v_w_in, v_q_norm, v_k_norm, v_conv_w, v_w_out, v_norm_ffn, v_w_gate, v_w_up, v_w_down):
    depth, d, in_shard = w_in.shape
    ff_shard = w_gate.shape[2]
    ff_pad = in_shard
    conv_shard = conv_w.shape[2]
    xs = x.reshape(x.shape[-2], d)
    target = loss_target.reshape(xs.shape)

    pa = jnp.stack([w_in, _pad_to(w_gate, 2, ff_pad), _pad_to(w_up, 2, ff_pad)], axis=1)
    pa = pa.reshape(3 * depth, 1, d, in_shard).astype(BF16)
    pb = w_out.astype(BF16)[:, None]
    pc = _pad_to(w_down, 1, ff_pad).astype(BF16)[:, None]
    pd = _pad_to(_pad_to(conv_w.reshape(depth * 3, conv_shard), 0, 8), 1, LANES)[None, None]
    ga, gb, gc, gd = _all_gather([pa, pb, pc, pd], "gather_weights")
    gb = gb.reshape(depth, N_DEV * gb.shape[2], d)
    gc = gc.reshape(depth, N_DEV * ff_pad, d)
    conv_full = gd[0, :, :depth * 3, :conv_shard].transpose(1, 0, 2).reshape(depth, 3, N_DEV * conv_shard)

    loss, grad_x, grads, small = _local_step(xs, target, ga, gb, gc, conv_full, norm_mix, q_norm, k_norm, norm_ffn)

    send = []
    for l in range(depth):
        d_win, d_wgate, d_wup, d_wout, d_wdown = grads[l]
        send += [d_win, d_wgate, d_wup, d_wout.reshape(N_DEV, -1, d), d_wdown.reshape(N_DEV, ff_pad, d)]
    landed = _exchange_blocks(send, "exchange_grads")

    nconv = N_DEV * conv_shard
    rows = []
    for l in range(depth):
        g_mix, g_ffn, g_q, g_k, g_conv = small[l]
        qkrow = _pad_to(jnp.concatenate([g_q, g_k]), 0, d)
        rows += [g_mix[None], g_ffn[None], qkrow[None], _pad_to(g_conv, 1, d)]
    nrow = 6 * depth
    packed = jnp.concatenate(rows + [_pad_to(loss[:1], 1, d)], axis=0)
    packed = _pad_to(packed, 0, ((nrow + 1 + 7) // 8) * 8)
    summed = _all_reduce_small(packed, "reduce_small")
    loss_out = summed[nrow, 0]

    def big(parts, w, m, v, tr, name, rows_=None, cols_=None):
        pr, pcn = parts.shape[1], parts.shape[2]
        w, m, v = [_pad_to(_pad_to(t, 0, pr), 1, pcn) for t in (w, m, v)]
        outs = _adamw(parts, w, m, v, tr, name)
        return [o[:rows_ or pr, :cols_ or pcn] for o in outs]

    res = {}
    for l in range(depth):
        la = landed[5 * l:5 * l + 5]
        res[("w_in", l)] = big(la[0], w_in[l], m_w_in[l], v_w_in[l], 256, f"adamw_in_{l}")
        res[("w_gate", l)] = big(la[1], w_gate[l], m_w_gate[l], v_w_gate[l], 256, f"adamw_gate_{l}", cols_=ff_shard)
        res[("w_up", l)] = big(la[2], w_up[l], m_w_up[l], v_w_up[l], 256, f"adamw_up_{l}", cols_=ff_shard)
        res[("w_out", l)] = big(la[3], w_out[l], m_w_out[l], v_w_out[l], la[3].shape[1], f"adamw_out_{l}")
        res[("w_down", l)] = big(la[4], w_down[l], m_w_down[l], v_w_down[l], 128, f"adamw_down_{l}",
                                 rows_=ff_shard)

    x_, y_, c_ = _place()
    my = 4 * x_ + 2 * y_ + c_
    g_rows, w_rows, m_rows, v_rows = [], [], [], []
    for l in range(depth):
        base = l * 6
        conv_g = lax.dynamic_slice(summed[base + 3:base + 6], (0, my * conv_shard), (3, conv_shard))
        g_rows += [summed[base:base + 3], _pad_to(conv_g, 1, d)]
        for dst, (nm, qn, kn, nf, cw) in ((w_rows, (norm_mix, q_norm, k_norm, norm_ffn, conv_w)),
                                          (m_rows, (m_norm_mix, m_q_norm, m_k_norm, m_norm_ffn, m_conv_w)),
                                          (v_rows, (v_norm_mix, v_q_norm, v_k_norm, v_norm_ffn, v_conv_w))):
            dst += [nm[l][None], nf[l][None], _pad_to(jnp.concatenate([qn[l], kn[l]]), 0, d)[None],
                    _pad_to(cw[l], 1, d)]
    prow = ((nrow + 7) // 8) * 8
    gs, ws, ms, vs = [_pad_to(jnp.concatenate(t, axis=0), 0, prow) for t in (g_rows, w_rows, m_rows, v_rows)]
    sm = _adamw(gs[None], ws, ms, vs, prow, "adamw_small")

    hd = q_norm.shape[1]

    def small_out(t, kind):
        per_layer = []
        for l in range(depth):
            base = l * 6
            per_layer.append({"norm_mix": t[base], "norm_ffn": t[base + 1], "q_norm": t[base + 2, :hd],
                              "k_norm": t[base + 2, hd:2 * hd], "conv_w": t[base + 3:base + 6, :conv_shard]}[kind])
        return jnp.stack(per_layer)

    def big_out(name, i):
        return jnp.stack([res[(name, l)][i] for l in range(depth)])

    outs = [loss_out, grad_x.reshape(x.shape)]
    for i in range(4):
        outs += [small_out(sm[i], "norm_mix"), big_out("w_in", i), small_out(sm[i], "q_norm"),
                 small_out(sm[i], "k_norm"), small_out(sm[i], "conv_w"), big_out("w_out", i),
                 small_out(sm[i], "norm_ffn"), big_out("w_gate", i), big_out("w_up", i), big_out("w_down", i)]
    return tuple(outs)
```

```python
import jax
import jax.numpy as jnp
from jax import lax
from jax.experimental import pallas as pl
from jax.experimental.pallas import tpu as pltpu

F32 = jnp.float32
BF16 = jnp.bfloat16
MESH = pl.DeviceIdType.MESH

N_DEV = 8
LANES = 128
HEAD_DIM = 64
KEY_CHUNK = 128
EPS = 1e-6
VMEM_LIMIT = 48 * 1024 * 1024

ADAM_LR = 0.001
ADAM_B1 = 0.9
ADAM_B2 = 0.999
ADAM_EPS = 1e-08
ADAM_WD = 0.01
ADAM_STEP = 10

NN = (((1,), (0,)), ((), ()))
NT = (((1,), (1,)), ((), ()))
TN = (((0,), (0,)), ((), ()))


def _dot(a, b, dims):
    return lax.dot_general(a.astype(BF16), b.astype(BF16), dims, preferred_element_type=F32)


def _cparams(*sem):
    return pltpu.CompilerParams(dimension_semantics=sem, vmem_limit_bytes=VMEM_LIMIT)


def _split_hi_lo(v):
    hi = v.astype(BF16)
    lo = (v - hi.astype(F32)).astype(BF16)
    return jnp.concatenate([hi, lo], axis=1)


def _rmsnorm_fwd(x, gain, tm, name):
    s, d = x.shape

    def body(x_ref, g_ref, o_ref):
        xv = x_ref[...]
        r = lax.rsqrt(jnp.mean(xv * xv, axis=-1, keepdims=True) + EPS)
        o_ref[...] = ((xv * r) * g_ref[...]).astype(o_ref.dtype)

    return pl.pallas_call(
        body, name=name, grid=(s // tm,),
        in_specs=[pl.BlockSpec((tm, d), lambda i: (i, 0)), pl.BlockSpec((1, d), lambda i: (0, 0))],
        out_specs=pl.BlockSpec((tm, d), lambda i: (i, 0)),
        out_shape=jax.ShapeDtypeStruct((s, d), BF16),
        compiler_params=_cparams("parallel"),
    )(x, gain)


def _rmsnorm_bwd(dh, x, gain, dres, tm, name):
    s, d = x.shape
    nsteps = s // tm

    def body(dh_ref, x_ref, g_ref, dres_ref, dx_ref, dg_ref):
        i = pl.program_id(0)
        xv = x_ref[...]
        r = lax.rsqrt(jnp.mean(xv * xv, axis=-1, keepdims=True) + EPS)
        xhat = xv * r
        dhv = dh_ref[...]
        dxh = dhv * g_ref[...]
        proj = jnp.mean(dxh * xhat, axis=-1, keepdims=True)
        dx_ref[...] = dres_ref[...] + r * (dxh - xhat * proj)
        part = jnp.sum((dhv * xhat).reshape(tm // 8, 8, d), axis=0)

        @pl.when(i == 0)
        def _():
            dg_ref[...] = part

        @pl.when(i > 0)
        def _():
            dg_ref[...] += part

        @pl.when(i == nsteps - 1)
        def _():
            dg_ref[...] = jnp.broadcast_to(jnp.sum(dg_ref[...], axis=0, keepdims=True), (8, d))

    row = pl.BlockSpec((tm, d), lambda i: (i, 0))
    return pl.pallas_call(
        body, name=name, grid=(nsteps,),
        in_specs=[row, row, pl.BlockSpec((1, d), lambda i: (0, 0)), row],
        out_specs=[row, pl.BlockSpec((8, d), lambda i: (0, 0))],
        out_shape=[jax.ShapeDtypeStruct((s, d), F32), jax.ShapeDtypeStruct((8, d), F32)],
        compiler_params=_cparams("arbitrary"),
    )(dh, x, gain, dres)


def _group_mean_matrix():
    r = lax.broadcasted_iota(jnp.int32, (LANES, LANES), 0) // HEAD_DIM
    c = lax.broadcasted_iota(jnp.int32, (LANES, LANES), 1) // HEAD_DIM
    return jnp.where(r == c, 1.0 / HEAD_DIM, 0.0).astype(BF16)


def _group_mean(v, gm):
    hi = v.astype(BF16)
    lo = (v - hi.astype(F32)).astype(BF16)
    return _dot(hi, gm, NN) + _dot(lo, gm, NN)


def _qknorm_fwd(proj, gains, tm, name):
    s = proj.shape[0]
    ncol = gains.shape[1] // LANES

    def body(p_ref, g_ref, gm_ref, o_ref):
        xv = p_ref[...].astype(F32)
        r = lax.rsqrt(_group_mean(xv * xv, gm_ref[...]) + EPS)
        o_ref[...] = ((xv * r) * g_ref[...]).astype(o_ref.dtype)

    blk = pl.BlockSpec((tm, LANES), lambda i, j: (i, j))
    return pl.pallas_call(
        body, name=name, grid=(s // tm, ncol),
        in_specs=[blk, pl.BlockSpec((1, LANES), lambda i, j: (0, j)),
                  pl.BlockSpec((LANES, LANES), lambda i, j: (0, 0))],
        out_specs=blk,
        out_shape=jax.ShapeDtypeStruct((s, ncol * LANES), BF16),
        compiler_params=_cparams("parallel", "parallel"),
    )(proj, gains, _group_mean_matrix())


def _qknorm_bwd(dqk, proj, gains, tm, name):
    s = proj.shape[0]
    ncol = gains.shape[1] // LANES
    nsteps = s // tm

    def body(dy_ref, p_ref, g_ref, gm_ref, dx_ref, dg_ref):
        i = pl.program_id(1)
        gm = gm_ref[...]
        xv = p_ref[...].astype(F32)
        r = lax.rsqrt(_group_mean(xv * xv, gm) + EPS)
        xhat = xv * r
        dy = dy_ref[...]
        dxh = dy * g_ref[...]
        proj_ = _group_mean(dxh * xhat, gm)
        dx_ref[...] = (r * (dxh - xhat * proj_)).astype(dx_ref.dtype)
        part = jnp.sum((dy * xhat).reshape(tm // 8, 8, LANES), axis=0)

        @pl.when(i == 0)
        def _():
            dg_ref[...] = part

        @pl.when(i > 0)
        def _():
            dg_ref[...] += part

        @pl.when(i == nsteps - 1)
        def _():
            dg_ref[...] = jnp.broadcast_to(jnp.sum(dg_ref[...], axis=0, keepdims=True), (8, LANES))

    blk = pl.BlockSpec((tm, LANES), lambda j, i: (i, j))
    return pl.pallas_call(
        body, name=name, grid=(ncol, nsteps),
        in_specs=[blk, blk, pl.BlockSpec((1, LANES), lambda j, i: (0, j)),
                  pl.BlockSpec((LANES, LANES), lambda j, i: (0, 0))],
        out_specs=[blk, pl.BlockSpec((8, LANES), lambda j, i: (0, j))],
        out_shape=[jax.ShapeDtypeStruct((s, ncol * LANES), BF16),
                   jax.ShapeDtypeStruct((8, ncol * LANES), F32)],
        compiler_params=_cparams("parallel", "arbitrary"),
    )(dqk, proj, gains, _group_mean_matrix())


CONV_ROWS = 256
HALO = 8


def _conv_fwd(proj, conv_w8, name):
    s = proj.shape[0]
    nblk = conv_w8.shape[1] // LANES
    first = 3 * nblk
    nchunk = s // CONV_ROWS

    def body(cb_ref, cc_ref, cu_ref, w_ref, y_ref, hpad):
        hpad[pl.ds(0, 2 * HALO), :] = jnp.zeros((2 * HALO, LANES), F32)

        def fill(i, _):
            r0 = pl.multiple_of(i * CONV_ROWS, CONV_ROWS)
            hpad[pl.ds(r0 + 2 * HALO, CONV_ROWS), :] = (
                cc_ref[pl.ds(r0, CONV_ROWS), :].astype(F32) * cu_ref[pl.ds(r0, CONV_ROWS), :].astype(F32))
            return 0

        lax.fori_loop(0, nchunk, fill, 0)
        w0, w1, w2 = w_ref[0:1, :], w_ref[1:2, :], w_ref[2:3, :]

        def conv(i, _):
            r0 = pl.multiple_of(i * CONV_ROWS, CONV_ROWS)
            win = hpad[pl.ds(r0 + HALO, CONV_ROWS + HALO), :]
            c = (w2 * win[HALO:] + w1 * pltpu.roll(win, 1, 0)[HALO:] + w0 * pltpu.roll(win, 2, 0)[HALO:])
            y_ref[pl.ds(r0, CONV_ROWS), :] = (cb_ref[pl.ds(r0, CONV_ROWS), :].astype(F32) * c).astype(y_ref.dtype)
            return 0

        lax.fori_loop(0, nchunk, conv, 0)

    def col(off):
        return pl.BlockSpec((s, LANES), lambda j: (0, off + j))

    return pl.pallas_call(
        body, name=name, grid=(nblk,),
        in_specs=[col(first), col(first + nblk), col(first + 2 * nblk), pl.BlockSpec((8, LANES), lambda j: (0, j))],
        out_specs=pl.BlockSpec((s, LANES), lambda j: (0, j)),
        out_shape=jax.ShapeDtypeStruct((s, nblk * LANES), BF16),
        scratch_shapes=[pltpu.VMEM((s + 2 * HALO, LANES), F32)],
        compiler_params=_cparams("parallel"),
    )(proj, proj, proj, conv_w8)


def _conv_bwd(dmix, proj, conv_w8, name):
    s = proj.shape[0]
    nblk = conv_w8.shape[1] // LANES
    first = 3 * nblk
    nchunk = s // CONV_ROWS

    def body(dy_ref, cb_ref, cc_ref, cu_ref, w_ref, dcb_ref, dcc_ref, dcu_ref, dw_ref, hpad, dcpad):
        hpad[pl.ds(0, 2 * HALO), :] = jnp.zeros((2 * HALO, LANES), F32)
        dcpad[pl.ds(s, 2 * HALO), :] = jnp.zeros((2 * HALO, LANES), F32)

        def fill(i, _):
            r0 = pl.multiple_of(i * CONV_ROWS, CONV_ROWS)
            hpad[pl.ds(r0 + 2 * HALO, CONV_ROWS), :] = (
                cc_ref[pl.ds(r0, CONV_ROWS), :].astype(F32) * cu_ref[pl.ds(r0, CONV_ROWS), :].astype(F32))
            return 0

        lax.fori_loop(0, nchunk, fill, 0)
        w0, w1, w2 = w_ref[0:1, :], w_ref[1:2, :], w_ref[2:3, :]

        def fold(v):
            return jnp.sum(v.reshape(CONV_ROWS // 8, 8, LANES), axis=0)

        def first_pass(i, acc):
            a0, a1, a2 = acc
            r0 = pl.multiple_of(i * CONV_ROWS, CONV_ROWS)
            win = hpad[pl.ds(r0 + HALO, CONV_ROWS + HALO), :]
            h0 = win[HALO:]
            h1 = pltpu.roll(win, 1, 0)[HALO:]
            h2 = pltpu.roll(win, 2, 0)[HALO:]
            c = w2 * h0 + w1 * h1 + w0 * h2
            dy = dy_ref[pl.ds(r0, CONV_ROWS), :]
            dcb_ref[pl.ds(r0, CONV_ROWS), :] = (dy * c).astype(dcb_ref.dtype)
            dc = dy * cb_ref[pl.ds(r0, CONV_ROWS), :].astype(F32)
            dcpad[pl.ds(r0, CONV_ROWS), :] = dc
            return a0 + fold(dc * h2), a1 + fold(dc * h1), a2 + fold(dc * h0)

        z8 = jnp.zeros((8, LANES), F32)
        a0, a1, a2 = lax.fori_loop(0, nchunk, first_pass, (z8, z8, z8))
        dw_ref[...] = jnp.concatenate(
            [jnp.sum(a0, axis=0, keepdims=True), jnp.sum(a1, axis=0, keepdims=True),
             jnp.sum(a2, axis=0, keepdims=True), jnp.zeros((5, LANES), F32)], axis=0)

        def second_pass(i, _):
            r0 = pl.multiple_of(i * CONV_ROWS, CONV_ROWS)
            win = dcpad[pl.ds(r0, CONV_ROWS + HALO), :]
            n = CONV_ROWS + HALO
            dh = (w2 * win[:CONV_ROWS] + w1 * pltpu.roll(win, n - 1, 0)[:CONV_ROWS]
                  + w0 * pltpu.roll(win, n - 2, 0)[:CONV_ROWS])
            dcc_ref[pl.ds(r0, CONV_ROWS), :] = (dh * cu_ref[pl.ds(r0, CONV_ROWS), :].astype(F32)).astype(dcc_ref.dtype)
            dcu_ref[pl.ds(r0, CONV_ROWS), :] = (dh * cc_ref[pl.ds(r0, CONV_ROWS), :].astype(F32)).astype(dcu_ref.dtype)
            return 0

        lax.fori_loop(0, nchunk, second_pass, 0)

    def col(off):
        return pl.BlockSpec((s, LANES), lambda j: (0, off + j))

    out = pl.BlockSpec((s, LANES), lambda j: (0, j))
    return pl.pallas_call(
        body, name=name, grid=(nblk,),
        in_specs=[col(nblk), col(first), col(first + nblk), col(first + 2 * nblk),
                  pl.BlockSpec((8, LANES), lambda j: (0, j))],
        out_specs=[out, out, out, pl.BlockSpec((8, LANES), lambda j: (0, j))],
        out_shape=[jax.ShapeDtypeStruct((s, nblk * LANES), BF16)] * 3 + [jax.ShapeDtypeStruct((8, nblk * LANES), F32)],
        scratch_shapes=[pltpu.VMEM((s + 2 * HALO, LANES), F32), pltpu.VMEM((s + 2 * HALO, LANES), F32)],
        compiler_params=_cparams("parallel"),
    )(dmix, proj, proj, proj, conv_w8)


LOG2E = 1.4426950408889634
LN2 = 0.6931471805599453
NEG_BIG = -1e30


def _cumsum_matrix(kind):
    j = lax.broadcasted_iota(jnp.int32, (KEY_CHUNK, 2 * KEY_CHUNK), 0)
    c = lax.broadcasted_iota(jnp.int32, (KEY_CHUNK, 2 * KEY_CHUNK), 1)
    tri = {"after": j > c, "upto": j <= c, "before": j < c}[kind]
    return jnp.where((c >= KEY_CHUNK) | tri, 1.0, 0.0).astype(BF16)


def _stack_heads(t, m0):
    zero = jnp.zeros_like(t)
    return jnp.concatenate([jnp.where(m0, t, zero), jnp.where(m0, zero, t)], axis=0)


def _softplus2(z):
    sp = jnp.maximum(z, 0.0) + jnp.log2(1.0 + jnp.exp2(-jnp.abs(z)))
    return sp, z - sp


def _key_chunk(ref, kc):
    return ref[pl.ds(pl.multiple_of(kc * KEY_CHUNK, KEY_CHUNK), KEY_CHUNK), :]


def _attn_fwd(qk, proj, tq, name):
    s = qk.shape[0]
    nhp = qk.shape[1] // (2 * LANES)
    nc = tq // KEY_CHUNK

    def body(q_ref, k_ref, v_ref, cm_ref, o_ref, r_ref, z_refs, ls_refs, sp_refs, ab_refs, rs_ref, acc_ref):
        qi = pl.program_id(1)
        n = qi * nc
        last = jnp.maximum(n - 1, 0)
        m0 = lax.broadcasted_iota(jnp.int32, (1, LANES), 1) < HEAD_DIM
        qs = _stack_heads(q_ref[...], m0)
        cm = cm_ref[...]
        qpos = qi * tq + (lax.broadcasted_iota(jnp.int32, (2 * tq, KEY_CHUNK), 0) & (tq - 1))
        kcol = lax.broadcasted_iota(jnp.int32, (2 * tq, KEY_CHUNK), 1)

        def chunk_at(i):
            return jnp.clip(n - 1 - i, 0, last)

        def scores(kc):
            return _dot(qs, _key_chunk(k_ref, kc), NT)

        def soft(z, mask):
            sp, ls = _softplus2(z)
            if mask is not None:
                sp = jnp.where(mask, sp, 0.0)
            return sp.astype(BF16), ls

        def weights(ls, cs, rs, mask):
            a = jnp.exp2(ls - cs[:, :KEY_CHUNK] - rs)
            if mask is not None:
                a = jnp.where(mask, a, 0.0)
            return a.astype(BF16), rs + cs[:, KEY_CHUNK:]

        def values(ab, kc):
            return _dot(jnp.concatenate([ab[:tq], ab[tq:]], axis=1), _stack_heads(_key_chunk(v_ref, kc), m0), NN)

        rs = jnp.zeros((2 * tq, LANES), F32)
        acc = jnp.zeros((tq, LANES), F32)
        for d in range(nc):
            kc = n + nc - 1 - d
            mask = (kcol + kc * KEY_CHUNK) < qpos
            spb, ls = soft(scores(kc), mask)
            ab, rs = weights(ls, _dot(spb, cm, NN), rs, mask)
            acc = acc + values(ab, kc)
        rs_ref[...] = rs
        acc_ref[...] = acc

        def step(i, par):
            cur, prv = par, 1 - par
            z_next = scores(chunk_at(i + 1))
            cs = _dot(sp_refs[prv][...], cm, NN)
            pv = values(ab_refs[cur][...], chunk_at(i - 2))
            spb, ls = soft(z_refs[cur][...], None)
            sp_refs[cur][...] = spb
            ls_refs[cur][...] = ls
            ab, rs = weights(ls_refs[prv][...], cs, rs_ref[...], None)
            ab_refs[prv][...] = ab
            rs_ref[...] = rs
            acc_ref[...] += pv
            z_refs[prv][...] = z_next

        z_refs[0][...] = scores(chunk_at(0))
        sp_refs[1][...] = jnp.zeros((2 * tq, LANES), BF16)
        ls_refs[1][...] = jnp.full((2 * tq, LANES), NEG_BIG, F32)
        ab_refs[0][...] = jnp.zeros((2 * tq, LANES), BF16)

        def two_steps(j, _):
            step(2 * j, 0)
            step(2 * j + 1, 1)
            return 0

        lax.fori_loop(0, n // 2, two_steps, 0)
        pv = values(ab_refs[0][...], chunk_at(n - 2))
        ab, rs = weights(ls_refs[1][...], _dot(sp_refs[1][...], cm, NN), rs_ref[...], None)
        o_ref[...] = (acc_ref[...] + pv + values(ab, chunk_at(n - 1))).astype(o_ref.dtype)
        r_ref[:, :LANES] = rs[:tq]
        r_ref[:, LANES:] = rs[tq:]

    def wrapped(q_ref, k_ref, v_ref, cm_ref, o_ref, r_ref, z0, z1, ls0, ls1, sp0, sp1, ab0, ab1, rs_ref, acc_ref):
        body(q_ref, k_ref, v_ref, cm_ref, o_ref, r_ref, (z0, z1), (ls0, ls1), (sp0, sp1), (ab0, ab1), rs_ref, acc_ref)

    assert nc % 2 == 0
    f32buf = pltpu.VMEM((2 * tq, LANES), F32)
    bf16buf = pltpu.VMEM((2 * tq, LANES), BF16)
    return pl.pallas_call(
        wrapped, name=name, grid=(nhp, s // tq),
        in_specs=[pl.BlockSpec((tq, LANES), lambda p, i: (i, p)),
                  pl.BlockSpec((s, LANES), lambda p, i: (0, nhp + p)),
                  pl.BlockSpec((s, LANES), lambda p, i: (0, 2 * nhp + p)),
                  pl.BlockSpec((KEY_CHUNK, 2 * KEY_CHUNK), lambda p, i: (0, 0))],
        out_specs=[pl.BlockSpec((tq, LANES), lambda p, i: (i, p)),
                   pl.BlockSpec((tq, 2 * LANES), lambda p, i: (i, p))],
        out_shape=[jax.ShapeDtypeStruct((s, nhp * LANES), BF16),
                   jax.ShapeDtypeStruct((s, nhp * 2 * LANES), F32)],
        scratch_shapes=[f32buf, f32buf, f32buf, f32buf, bf16buf, bf16buf, bf16buf, bf16buf, f32buf,
                        pltpu.VMEM((tq, LANES), F32)],
        compiler_params=_cparams("parallel", "parallel"),
    )(qk, qk, proj, _cumsum_matrix("after"))


def _attn_bwd(qk, proj, dmix, rtot, tq, name):
    s = qk.shape[0]
    nhp = qk.shape[1] // (2 * LANES)
    nc = tq // KEY_CHUNK

    def body(q_ref, k_ref, v_ref, do_ref, r_ref, cmi_ref, cme_ref, dq_ref, dk_ref, dv_ref,
             z_refs, ls_refs, sig_refs, sp_refs, gb_refs, pr_ref, gs_ref):
        qi = pl.program_id(1)

        @pl.when(qi == 0)
        def _():
            dk_ref[...] = jnp.zeros_like(dk_ref)
            dv_ref[...] = jnp.zeros_like(dv_ref)

        n = qi * nc
        last = jnp.maximum(n - 1, 0)
        m0 = lax.broadcasted_iota(jnp.int32, (1, LANES), 1) < HEAD_DIM
        qs = _stack_heads(q_ref[...], m0)
        do = do_ref[...]
        dos = _stack_heads(do.astype(BF16), m0)
        dosl = _stack_heads((do * LN2).astype(BF16), m0)
        cmi = cmi_ref[...]
        cme = cme_ref[...]
        qpos = qi * tq + (lax.broadcasted_iota(jnp.int32, (2 * tq, KEY_CHUNK), 0) & (tq - 1))
        kcol = lax.broadcasted_iota(jnp.int32, (2 * tq, KEY_CHUNK), 1)

        def chunk_at(i):
            return jnp.clip(i, 0, last)

        def scores(kc):
            return _dot(qs, _key_chunk(k_ref, kc), NT)

        def soft(z, mask):
            sp, ls = _softplus2(z)
            if mask is not None:
                sp = jnp.where(mask, sp, 0.0)
            return sp.astype(BF16), ls

        def weights(ls, cs, da, pr, kc, mask):
            a = jnp.exp2(ls - (pr - cs[:, :KEY_CHUNK]))
            if mask is not None:
                a = jnp.where(mask, a, 0.0)
            gb = (a * da).astype(BF16)
            ks = pl.multiple_of(kc * KEY_CHUNK, KEY_CHUNK)
            dv_ref[pl.ds(ks, KEY_CHUNK), :] += _dot(a, dos, TN)
            return gb, jnp.exp2(ls), pr - cs[:, KEY_CHUNK:]

        def score_grads(gb, sig, cg, gs, dq, kc, mask):
            dz = gb.astype(F32) * (1.0 - sig) - sig * (gs + cg[:, :KEY_CHUNK])
            if mask is not None:
                dz = jnp.where(mask, dz, 0.0)
            dzb = dz.astype(BF16)
            ks = pl.multiple_of(kc * KEY_CHUNK, KEY_CHUNK)
            dk_ref[pl.ds(ks, KEY_CHUNK), :] += _dot(dzb, qs, TN)
            dq = dq + _dot(jnp.concatenate([dzb[:tq], dzb[tq:]], axis=1), _stack_heads(_key_chunk(k_ref, kc), m0), NN)
            return gs + cg[:, KEY_CHUNK:], dq

        def step(i, par):
            cur, prv = par, 1 - par
            k1, k2 = chunk_at(i - 1), chunk_at(i - 2)
            z_next = scores(chunk_at(i + 1))
            cs = _dot(sp_refs[prv][...], cmi, NN)
            da = _dot(dosl, _key_chunk(v_ref, k1), NT)
            cg = _dot(gb_refs[cur][...], cme, NN)
            spb, ls = soft(z_refs[cur][...], None)
            sp_refs[cur][...] = spb
            ls_refs[cur][...] = ls
            gs, dq = score_grads(gb_refs[cur][...], sig_refs[cur][...], cg, gs_ref[...], dq_ref[...], k2, None)
            gs_ref[...] = gs
            dq_ref[...] = dq
            gb, sig, pr = weights(ls_refs[prv][...], cs, da, pr_ref[...], k1, None)
            gb_refs[prv][...] = gb
            sig_refs[prv][...] = sig
            pr_ref[...] = pr
            z_refs[prv][...] = z_next

        pr_ref[...] = jnp.concatenate([r_ref[:, :LANES], r_ref[:, LANES:]], axis=0)
        gs_ref[...] = jnp.zeros((2 * tq, LANES), F32)
        dq_ref[...] = jnp.zeros((tq, LANES), F32)
        z_refs[0][...] = scores(chunk_at(0))
        sp_refs[1][...] = jnp.zeros((2 * tq, LANES), BF16)
        ls_refs[1][...] = jnp.full((2 * tq, LANES), NEG_BIG, F32)
        gb_refs[0][...] = jnp.zeros((2 * tq, LANES), BF16)
        sig_refs[0][...] = jnp.zeros((2 * tq, LANES), F32)

        def two_steps(j, _):
            step(2 * j, 0)
            step(2 * j + 1, 1)
            return 0

        lax.fori_loop(0, n // 2, two_steps, 0)
        k1, k2 = chunk_at(n - 1), chunk_at(n - 2)
        gb, sig, pr = weights(ls_refs[1][...], _dot(sp_refs[1][...], cmi, NN),
                              _dot(dosl, _key_chunk(v_ref, k1), NT), pr_ref[...], k1, None)
        gb2 = gb_refs[0][...]
        gs, dq = score_grads(gb2, sig_refs[0][...], _dot(gb2, cme, NN), gs_ref[...], dq_ref[...], k2, None)
        gs, dq = score_grads(gb, sig, _dot(gb, cme, NN), gs, dq, k1, None)
        for d in range(nc):
            kc = n + d
            mask = (kcol + kc * KEY_CHUNK) < qpos
            spb, ls = soft(scores(kc), mask)
            gb, sig, pr = weights(ls, _dot(spb, cmi, NN), _dot(dosl, _key_chunk(v_ref, kc), NT), pr, kc, mask)
            gs, dq = score_grads(gb, sig, _dot(gb, cme, NN), gs, dq, kc, mask)
        dq_ref[...] = dq

    def wrapped(q_ref, k_ref, v_ref, do_ref, r_ref, cmi_ref, cme_ref, dq_ref, dk_ref, dv_ref,
                z0, z1, ls0, ls1, sg0, sg1, sp0, sp1, gb0, gb1, pr_ref, gs_ref):
        body(q_ref, k_ref, v_ref, do_ref, r_ref, cmi_ref, cme_ref, dq_ref, dk_ref, dv_ref,
             (z0, z1), (ls0, ls1), (sg0, sg1), (sp0, sp1), (gb0, gb1), pr_ref, gs_ref)

    assert nc % 2 == 0
    qblk = pl.BlockSpec((tq, LANES), lambda p, i: (i, p))
    full = pl.BlockSpec((s, LANES), lambda p, i: (0, p))
    cmspec = pl.BlockSpec((KEY_CHUNK, 2 * KEY_CHUNK), lambda p, i: (0, 0))
    shape = jax.ShapeDtypeStruct((s, nhp * LANES), F32)
    f32buf = pltpu.VMEM((2 * tq, LANES), F32)
    bf16buf = pltpu.VMEM((2 * tq, LANES), BF16)
    return pl.pallas_call(
        wrapped, name=name, grid=(nhp, s // tq),
        in_specs=[qblk,
                  pl.BlockSpec((s, LANES), lambda p, i: (0, nhp + p)),
                  pl.BlockSpec((s, LANES), lambda p, i: (0, 2 * nhp + p)),
                  qblk,
                  pl.BlockSpec((tq, 2 * LANES), lambda p, i: (i, p)),
                  cmspec, cmspec],
        out_specs=[qblk, full, full],
        out_shape=[shape, shape, shape],
        scratch_shapes=[f32buf] * 6 + [bf16buf] * 4 + [f32buf] * 2,
        compiler_params=_cparams("parallel", "arbitrary"),
    )(qk, qk, proj, dmix, rtot, _cumsum_matrix("upto"), _cumsum_matrix("before"))


def _mm_blocks(h, ga, widx, tm, name):
    s, d = h.shape
    nb, cols = ga.shape[1], ga.shape[3]

    def body(a_ref, b_ref, o_ref):
        o_ref[...] = _dot(a_ref[...], b_ref[...], NN).astype(o_ref.dtype)

    return pl.pallas_call(
        body, name=name, grid=(s // tm, nb),
        in_specs=[pl.BlockSpec((tm, d), lambda i, j: (i, 0)),
                  pl.BlockSpec((None, None, d, cols), lambda i, j: (widx, j, 0, 0))],
        out_specs=pl.BlockSpec((tm, cols), lambda i, j: (i, j)),
        out_shape=jax.ShapeDtypeStruct((s, nb * cols), BF16),
        compiler_params=_cparams("parallel", "arbitrary"),
    )(h, ga)


def _mm_swiglu(h, ga, gidx, uidx, tm, name):
    s, d = h.shape
    nb, cols = ga.shape[1], ga.shape[3]

    def body(a_ref, bg_ref, bu_ref, g_ref, u_ref, act_ref):
        a = a_ref[...]
        g = _dot(a, bg_ref[...], NN)
        u = _dot(a, bu_ref[...], NN)
        g_ref[...] = g.astype(g_ref.dtype)
        u_ref[...] = u.astype(u_ref.dtype)
        act_ref[...] = (g * (1.0 / (1.0 + jnp.exp(-g))) * u).astype(act_ref.dtype)

    def wspec(idx):
        return pl.BlockSpec((None, None, d, cols), lambda i, j: (idx, j, 0, 0))

    out = pl.BlockSpec((tm, cols), lambda i, j: (i, j))
    shape = jax.ShapeDtypeStruct((s, nb * cols), BF16)
    return pl.pallas_call(
        body, name=name, grid=(s // tm, nb),
        in_specs=[pl.BlockSpec((tm, d), lambda i, j: (i, 0)), wspec(gidx), wspec(uidx)],
        out_specs=[out, out, out], out_shape=[shape, shape, shape],
        compiler_params=_cparams("parallel", "arbitrary"),
    )(h, ga, ga)


def _mm_residual(a, w3, lidx, res, tm, tn, name):
    s, k = a.shape
    n = w3.shape[2]

    def body(a_ref, b_ref, r_ref, o_ref):
        o_ref[...] = r_ref[...] + _dot(a_ref[...], b_ref[...], NN)

    return pl.pallas_call(
        body, name=name, grid=(s // tm, n // tn),
        in_specs=[pl.BlockSpec((tm, k), lambda i, j: (i, 0)),
                  pl.BlockSpec((None, k, tn), lambda i, j: (lidx, 0, j)),
                  pl.BlockSpec((tm, tn), lambda i, j: (i, j))],
        out_specs=pl.BlockSpec((tm, tn), lambda i, j: (i, j)),
        out_shape=jax.ShapeDtypeStruct((s, n), F32),
        compiler_params=_cparams("parallel", "arbitrary"),
    )(a, w3, res)


def _mm_nt(a, w3, lidx, tm, tn, name):
    s, k = a.shape
    n = w3.shape[1]

    def body(a_ref, b_ref, o_ref):
        o_ref[...] = _dot(a_ref[...], b_ref[...], NT)

    return pl.pallas_call(
        body, name=name, grid=(s // tm, n // tn),
        in_specs=[pl.BlockSpec((tm, k), lambda i, j: (i, 0)),
                  pl.BlockSpec((None, tn, k), lambda i, j: (lidx, j, 0))],
        out_specs=pl.BlockSpec((tm, tn), lambda i, j: (i, j)),
        out_shape=jax.ShapeDtypeStruct((s, n), F32),
        compiler_params=_cparams("parallel", "arbitrary"),
    )(a, w3)


def _mm_nt_swiglu_bwd(dx, wd3, lidx, g, u, tm, name):
    s, d = dx.shape
    cols = g.shape[1] // N_DEV

    def body(a_ref, b_ref, g_ref, u_ref, dg_ref, du_ref):
        dact = _dot(a_ref[...], b_ref[...], NT)
        gv = g_ref[...].astype(F32)
        sig = 1.0 / (1.0 + jnp.exp(-gv))
        du_ref[...] = (dact * (gv * sig)).astype(du_ref.dtype)
        dg_ref[...] = (dact * u_ref[...].astype(F32) * (sig * (1.0 + gv * (1.0 - sig)))).astype(dg_ref.dtype)

    blk = pl.BlockSpec((tm, cols), lambda i, j: (i, j))
    shape = jax.ShapeDtypeStruct(g.shape, BF16)
    return pl.pallas_call(
        body, name=name, grid=(s // tm, N_DEV),
        in_specs=[pl.BlockSpec((tm, d), lambda i, j: (i, 0)),
                  pl.BlockSpec((None, cols, d), lambda i, j: (lidx, j, 0)), blk, blk],
        out_specs=[blk, blk], out_shape=[shape, shape],
        compiler_params=_cparams("parallel", "arbitrary"),
    )(dx, wd3, g, u)


def _mm_nt_blocks(das, ga, widxs, tm, name):
    s = das[0].shape[0]
    nb, d, cols = ga.shape[1], ga.shape[2], ga.shape[3]
    nw = len(das)

    def body(*refs):
        a_refs, b_refs, o_ref = refs[:nw], refs[nw:2 * nw], refs[2 * nw]
        k = pl.program_id(1)
        part = _dot(a_refs[0][...], b_refs[0][...], NT)
        for w in range(1, nw):
            part = part + _dot(a_refs[w][...], b_refs[w][...], NT)

        @pl.when(k == 0)
        def _():
            o_ref[...] = part

        @pl.when(k > 0)
        def _():
            o_ref[...] += part

    def wspec(idx):
        return pl.BlockSpec((None, None, d, cols), lambda i, k: (idx, k, 0, 0))

    return pl.pallas_call(
        body, name=name, grid=(s // tm, nb),
        in_specs=[pl.BlockSpec((tm, cols), lambda i, k: (i, k))] * nw + [wspec(i) for i in widxs],
        out_specs=pl.BlockSpec((tm, d), lambda i, k: (i, 0)),
        out_shape=jax.ShapeDtypeStruct((s, d), F32),
        compiler_params=_cparams("parallel", "arbitrary"),
    )(*das, *([ga] * nw))


def _mm_tn(a, b, ta, tb, tk, out_blocks, name):
    s, ka = a.shape
    nb = b.shape[1]
    nk = s // tk

    def body(a_ref, b_ref, o_ref):
        k = pl.program_id(2)
        part = _dot(a_ref[...], b_ref[...], TN)

        @pl.when(k == 0)
        def _():
            o_ref[...] = part

        @pl.when(k > 0)
        def _():
            o_ref[...] += part

    if out_blocks:
        out_spec = pl.BlockSpec((None, ta, tb), lambda i, j, k: (j, i, 0))
        out_shape = jax.ShapeDtypeStruct((nb // tb, ka, tb), F32)
    else:
        out_spec = pl.BlockSpec((ta, tb), lambda i, j, k: (i, j))
        out_shape = jax.ShapeDtypeStruct((ka, nb), F32)
    return pl.pallas_call(
        body, name=name, grid=(ka // ta, nb // tb, nk),
        in_specs=[pl.BlockSpec((tk, ta), lambda i, j, k: (k, i)),
                  pl.BlockSpec((tk, tb), lambda i, j, k: (k, j))],
        out_specs=out_spec, out_shape=out_shape,
        compiler_params=_cparams("parallel", "parallel", "arbitrary"),
    )(a, b)


def _loss_head(y, target, tm, name):
    s, d = y.shape
    nsteps = s // tm

    def body(y_ref, t_ref, dy_ref, l_ref, acc):
        i = pl.program_id(0)
        diff = y_ref[...] - t_ref[...]
        dy_ref[...] = diff * (1.0 / d)
        part = jnp.sum((diff * diff).reshape(tm // 8, 8, d), axis=0)

        @pl.when(i == 0)
        def _():
            acc[...] = part

        @pl.when(i > 0)
        def _():
            acc[...] += part

        @pl.when(i == nsteps - 1)
        def _():
            tot = jnp.sum(jnp.sum(acc[...], axis=1, keepdims=True), axis=0, keepdims=True)
            l_ref[...] = jnp.broadcast_to(tot * (0.5 / d), (8, LANES))

    row = pl.BlockSpec((tm, d), lambda i: (i, 0))
    return pl.pallas_call(
        body, name=name, grid=(nsteps,),
        in_specs=[row, row],
        out_specs=[row, pl.BlockSpec((8, LANES), lambda i: (0, 0))],
        out_shape=[jax.ShapeDtypeStruct((s, d), F32), jax.ShapeDtypeStruct((8, LANES), F32)],
        scratch_shapes=[pltpu.VMEM((8, d), F32)],
        compiler_params=_cparams("arbitrary"),
    )(y, target)


def _adamw(parts, w, m, v, tr, name):
    p, rows, cols = parts.shape
    c1 = 1.0 / (1.0 - ADAM_B1 ** ADAM_STEP)
    c2 = 1.0 / (1.0 - ADAM_B2 ** ADAM_STEP)

    def body(p_ref, w_ref, m_ref, v_ref, g_ref, d_ref, nm_ref, nv_ref):
        g = p_ref[0]
        for k in range(1, p):
            g = g + p_ref[k]
        nm = ADAM_B1 * m_ref[...] + (1.0 - ADAM_B1) * g
        nv = ADAM_B2 * v_ref[...] + (1.0 - ADAM_B2) * (g * g)
        g_ref[...] = g
        nm_ref[...] = nm
        nv_ref[...] = nv
        d_ref[...] = -ADAM_LR * ((nm * c1) / (jnp.sqrt(nv * c2) + ADAM_EPS) + ADAM_WD * w_ref[...])

    blk = pl.BlockSpec((tr, cols), lambda i: (i, 0))
    shape = jax.ShapeDtypeStruct((rows, cols), F32)
    return pl.pallas_call(
        body, name=name, grid=(rows // tr,),
        in_specs=[pl.BlockSpec((p, tr, cols), lambda i: (0, i, 0)), blk, blk, blk],
        out_specs=[blk] * 4, out_shape=[shape] * 4,
        compiler_params=_cparams("parallel"),
    )(parts, w, m, v)


def _place():
    x, y, c = lax.axis_index("x"), lax.axis_index("y"), lax.axis_index("c")
    return x, y, c


def _all_gather(shards, name):
    na = len(shards)

    def body(*refs):
        srcs, dsts = refs[:na], refs[na:2 * na]
        send_sems, recv_sems, local_sems = refs[2 * na:]
        x, y, c = _place()
        me, sibling = (x, y, c), (x, y, 1 - c)
        chips = [(1 - x, y), (x, 1 - y), (1 - x, 1 - y)]

        def slot(a, dev):
            return dsts[a].at[:, pl.ds(4 * dev[0] + 2 * dev[1] + dev[2], 1)]

        def copy(k, a, block, to, from_shard=False):
            return pltpu.make_async_remote_copy(
                src_ref=srcs[a] if from_shard else slot(a, block), dst_ref=slot(a, block),
                send_sem=send_sems.at[k, a], recv_sem=recv_sems.at[k, a], device_id=to, device_id_type=MESH)

        mine = [pltpu.make_async_copy(srcs[a], slot(a, me), local_sems.at[a]) for a in range(na)]
        for cp in mine:
            cp.start()
        first = [copy(0, a, me, sibling, True) for a in range(na)]
        first += [copy(1 + j, a, me, (*chip, c), True) for j, chip in enumerate(chips) for a in range(na)]
        for cp in first:
            cp.start()
        passed = []
        for j, chip in enumerate(chips):
            for a in range(na):
                copy(1 + j, a, (*chip, c), me).wait_recv()
                fwd = copy(4 + j, a, (*chip, c), sibling)
                fwd.start()
                passed.append(fwd)
        for a in range(na):
            copy(0, a, sibling, me).wait_recv()
        for j, chip in enumerate(chips):
            for a in range(na):
                copy(4 + j, a, (*chip, 1 - c), me).wait_recv()
        for cp in first + passed:
            cp.wait_send()
        for cp in mine:
            cp.wait()

    anyspec = pl.BlockSpec(memory_space=pl.ANY)
    return pl.pallas_call(
        body, name=name,
        in_specs=[anyspec] * na, out_specs=[anyspec] * na,
        out_shape=[jax.ShapeDtypeStruct((a.shape[0], N_DEV) + a.shape[2:], a.dtype) for a in shards],
        scratch_shapes=[pltpu.SemaphoreType.DMA((7, na)), pltpu.SemaphoreType.DMA((7, na)),
                        pltpu.SemaphoreType.DMA((na,))],
    )(*shards)


_RELATIONS = [(dx, dy, dc) for dx in (0, 1) for dy in (0, 1) for dc in (0, 1)][1:]


def _flip(v, d):
    return 1 - v if d else v


def _exchange_blocks(grads, name):
    na = len(grads)

    def body(*refs):
        srcs, dsts = refs[:na], refs[na:2 * na]
        send_sems, recv_sems, local_sems = refs[2 * na:]
        x, y, c = _place()
        my = 4 * x + 2 * y + c
        mine = [pltpu.make_async_copy(srcs[a].at[pl.ds(my, 1)], dsts[a].at[pl.ds(my, 1)], local_sems.at[a])
                for a in range(na)]
        for cp in mine:
            cp.start()
        sends = []
        for k, (dx, dy, dc) in enumerate(_RELATIONS):
            peer = (_flip(x, dx), _flip(y, dy), _flip(c, dc))
            pidx = 4 * peer[0] + 2 * peer[1] + peer[2]
            for a in range(na):
                cp = pltpu.make_async_remote_copy(
                    src_ref=srcs[a].at[pl.ds(pidx, 1)], dst_ref=dsts[a].at[pl.ds(my, 1)],
                    send_sem=send_sems.at[k, a], recv_sem=recv_sems.at[k, a], device_id=peer, device_id_type=MESH)
                cp.start()
                sends.append((cp, pidx, k, a, peer))
        for cp, pidx, k, a, peer in sends:
            pltpu.make_async_remote_copy(
                src_ref=srcs[a].at[pl.ds(pidx, 1)], dst_ref=dsts[a].at[pl.ds(pidx, 1)],
                send_sem=send_sems.at[k, a], recv_sem=recv_sems.at[k, a], device_id=peer,
                device_id_type=MESH).wait_recv()
        for cp, *_ in sends:
            cp.wait_send()
        for cp in mine:
            cp.wait()

    anyspec = pl.BlockSpec(memory_space=pl.ANY)
    return pl.pallas_call(
        body, name=name,
        in_specs=[anyspec] * na, out_specs=[anyspec] * na,
        out_shape=[jax.ShapeDtypeStruct(a.shape, a.dtype) for a in grads],
        scratch_shapes=[pltpu.SemaphoreType.DMA((7, na)), pltpu.SemaphoreType.DMA((7, na)),
                        pltpu.SemaphoreType.DMA((na,))],
    )(*grads)


def _all_reduce_small(v, name):
    r, c_ = v.shape

    def body(v_ref, o_ref, gath, send_sems, recv_sems):
        x, y, c = _place()
        my = 4 * x + 2 * y + c
        gath[my] = v_ref[...]
        sends = []
        for k, (dx, dy, dc) in enumerate(_RELATIONS):
            peer = (_flip(x, dx), _flip(y, dy), _flip(c, dc))
            cp = pltpu.make_async_remote_copy(
                src_ref=v_ref, dst_ref=gath.at[my], send_sem=send_sems.at[k], recv_sem=recv_sems.at[k],
                device_id=peer, device_id_type=MESH)
            cp.start()
            sends.append((cp, 4 * peer[0] + 2 * peer[1] + peer[2], k, peer))
        for cp, pidx, k, peer in sends:
            pltpu.make_async_remote_copy(
                src_ref=v_ref, dst_ref=gath.at[pidx], send_sem=send_sems.at[k], recv_sem=recv_sems.at[k],
                device_id=peer, device_id_type=MESH).wait_recv()
        for cp, *_ in sends:
            cp.wait_send()
        tot = gath[0]
        for k in range(1, N_DEV):
            tot = tot + gath[k]
        o_ref[...] = tot

    vm = pl.BlockSpec(memory_space=pltpu.VMEM)
    return pl.pallas_call(
        body, name=name, in_specs=[vm], out_specs=vm,
        out_shape=jax.ShapeDtypeStruct((r, c_), F32),
        scratch_shapes=[pltpu.VMEM((N_DEV, r, c_), F32), pltpu.SemaphoreType.DMA((7,)),
                        pltpu.SemaphoreType.DMA((7,))],
    )(v)


TM = 512
TQ = 256


def _pad_to(a, axis, size):
    pad = [(0, 0)] * a.ndim
    pad[axis] = (0, size - a.shape[axis])
    return jnp.pad(a, pad)


def _local_step(x, target, ga, gb, gc, conv_full, norm_mix, q_norm, k_norm, norm_ffn):
    depth = gb.shape[0]
    tm, tq = min(TM, x.shape[0]), min(TQ, x.shape[0])
    attn = gb.shape[1] // 2
    nheads = attn // HEAD_DIM
    scale = HEAD_DIM ** -0.5 * LOG2E
    saved = []
    for l in range(depth):
        h1 = _rmsnorm_fwd(x, norm_mix[l][None], tm,f"norm_mix_fwd_{l}")
        proj = _mm_blocks(h1, ga, 3 * l, tm,f"proj_in_{l}")
        qk_gain = jnp.concatenate([jnp.tile(q_norm[l], nheads) * scale, jnp.tile(k_norm[l], nheads)])[None]
        qk = _qknorm_fwd(proj, qk_gain, tm,f"qknorm_fwd_{l}")
        o, rtot = _attn_fwd(qk, proj, tq,f"attn_fwd_{l}")
        conv_w8 = _pad_to(conv_full[l], 0, 8)
        cv = _conv_fwd(proj, conv_w8, f"conv_fwd_{l}")
        mix = jnp.concatenate([o, cv], axis=1)
        x1 = _mm_residual(mix, gb, l, x, tm,512, f"proj_out_{l}")
        h2 = _rmsnorm_fwd(x1, norm_ffn[l][None], tm,f"norm_ffn_fwd_{l}")
        g, u, act = _mm_swiglu(h2, ga, 3 * l + 1, 3 * l + 2, tm,f"ffn_up_{l}")
        x2 = _mm_residual(act, gc, l, x1, tm,512, f"ffn_down_{l}")
        saved.append((x, h1, proj, qk_gain, qk, rtot, conv_w8, mix, x1, h2, g, u, act))
        x = x2

    dx, loss = _loss_head(x, target, tm,"loss_head")

    grads = [None] * depth
    small = [None] * depth
    for l in reversed(range(depth)):
        x0, h1, proj, qk_gain, qk, rtot, conv_w8, mix, x1, h2, g, u, act = saved[l]
        d = x0.shape[1]
        dg, du = _mm_nt_swiglu_bwd(dx, gc, l, g, u, tm,f"ffn_down_bwd_{l}")
        d_wdown = _mm_tn(act, dx, 768, d, tm,False, f"dw_down_{l}")
        d_wgate = _mm_tn(h2, dg, d, ga.shape[3], tm,True, f"dw_gate_{l}")
        d_wup = _mm_tn(h2, du, d, ga.shape[3], tm,True, f"dw_up_{l}")
        dh2 = _mm_nt_blocks([dg, du], ga, [3 * l + 1, 3 * l + 2], tm,f"ffn_up_bwd_{l}")
        dx1, dg_ffn = _rmsnorm_bwd(dh2, x1, norm_ffn[l][None], dx, tm,f"norm_ffn_bwd_{l}")
        dmix = _mm_nt(dx1, gb, l, tm,512, f"proj_out_bwd_{l}")
        d_wout = _mm_tn(mix, dx1, 512, d, tm,False, f"dw_out_{l}")
        dcb, dcc, dcu, dconv = _conv_bwd(dmix, proj, conv_w8, f"conv_bwd_{l}")
        dq, dk, dv = _attn_bwd(qk, proj, dmix, rtot, tq,f"attn_bwd_{l}")
        dqk, dg_qk = _qknorm_bwd(jnp.concatenate([dq, dk], axis=1), proj, qk_gain, tm,f"qknorm_bwd_{l}")
        dproj = jnp.concatenate([dqk, dv.astype(BF16), dcb, dcc, dcu], axis=1)
        d_win = _mm_tn(h1, dproj, d, ga.shape[3], tm,True, f"dw_in_{l}")
        dh1 = _mm_nt_blocks([dproj], ga, [3 * l], tm,f"proj_in_bwd_{l}")
        dx, dg_mix = _rmsnorm_bwd(dh1, x0, norm_mix[l][None], dx1, tm,f"norm_mix_bwd_{l}")
        grads[l] = (d_win, d_wgate, d_wup, d_wout, d_wdown)
        dq_gain = jnp.sum(dg_qk[0, :attn].reshape(nheads, HEAD_DIM), axis=0) * scale
        dk_gain = jnp.sum(dg_qk[0, attn:].reshape(nheads, HEAD_DIM), axis=0)
        small[l] = (dg_mix[0], dg_ffn[0], dq_gain, dk_gain, dconv[:3])
    return loss, dx, grads, small


def kernel(x, norm_mix, w_in, q_norm, k_norm, conv_w, w_out, norm_ffn, w_gate, w_up, w_down, loss_target, m_norm_mix, m_w_in, m_q_norm, m_k_norm, m_conv_w, m_w_out, m_norm_ffn, m_w_gate, m_w_up, m_w_down, v_norm_mix, v_w_in, v_q_norm, v_k_norm, v_conv_w, v_w_out, v_norm_ffn, v_w_gate, v_w_up, v_w_down):
    depth, d, in_shard = w_in.shape
    ff_shard = w_gate.shape[2]
    ff_pad = in_shard
    conv_shard = conv_w.shape[2]
    xs = x.reshape(x.shape[-2], d)
    target = loss_target.reshape(xs.shape)

    pa = jnp.stack([w_in, _pad_to(w_gate, 2, ff_pad), _pad_to(w_up, 2, ff_pad)], axis=1)
    pa = pa.reshape(3 * depth, 1, d, in_shard).astype(BF16)
    pb = w_out.astype(BF16)[:, None]
    pc = _pad_to(w_down, 1, ff_pad).astype(BF16)[:, None]
    pd = _pad_to(_pad_to(conv_w.reshape(depth * 3, conv_shard), 0, 8), 1, LANES)[None, None]
    ga, gb, gc, gd = _all_gather([pa, pb, pc, pd], "gather_weights")
    gb = gb.reshape(depth, N_DEV * gb.shape[2], d)
    gc = gc.reshape(depth, N_DEV * ff_pad, d)
    conv_full = gd[0, :, :depth * 3, :conv_shard].transpose(1, 0, 2).reshape(depth, 3, N_DEV * conv_shard)

    loss, grad_x, grads, small = _local_step(xs, target, ga, gb, gc, conv_full, norm_mix, q_norm, k_norm, norm_ffn)

    send = []
    for l in range(depth):
        d_win, d_wgate, d_wup, d_wout, d_wdown = grads[l]
        send += [d_win, d_wgate, d_wup, d_wout.reshape(N_DEV, -1, d), d_wdown.reshape(N_DEV, ff_pad, d)]
    landed = _exchange_blocks(send, "exchange_grads")

    nconv = N_DEV * conv_shard
    rows = []
    for l in range(depth):
        g_mix, g_ffn, g_q, g_k, g_conv = small[l]
        qkrow = _pad_to(jnp.concatenate([g_q, g_k]), 0, d)
        rows += [g_mix[None], g_ffn[None], qkrow[None], _pad_to(g_conv, 1, d)]
    nrow = 6 * depth
    packed = jnp.concatenate(rows + [_pad_to(loss[:1], 1, d)], axis=0)
    packed = _pad_to(packed, 0, ((nrow + 1 + 7) // 8) * 8)
    summed = _all_reduce_small(packed, "reduce_small")
    loss_out = summed[nrow, 0]

    def big(parts, w, m, v, tr, name, rows_=None, cols_=None):
        pr, pcn = parts.shape[1], parts.shape[2]
        w, m, v = [_pad_to(_pad_to(t, 0, pr), 1, pcn) for t in (w, m, v)]
        outs = _adamw(parts, w, m, v, tr, name)
        return [o[:rows_ or pr, :cols_ or pcn] for o in outs]

    res = {}
    for l in range(depth):
        la = landed[5 * l:5 * l + 5]
        res[("w_in", l)] = big(la[0], w_in[l], m_w_in[l], v_w_in[l], 256, f"adamw_in_{l}")
        res[("w_gate", l)] = big(la[1], w_gate[l], m_w_gate[l], v_w_gate[l], 256, f"adamw_gate_{l}", cols_=ff_shard)
        res[("w_up", l)] = big(la[2], w_up[l], m_w_up[l], v_w_up[l], 256, f"adamw_up_{l}", cols_=ff_shard)
        res[("w_out", l)] = big(la[3], w_out[l], m_w_out[l], v_w_out[l], la[3].shape[1], f"adamw_out_{l}")
        res[("w_down", l)] = big(la[4], w_down[l], m_w_down[l], v_w_down[l], 128, f"adamw_down_{l}",
                                 rows_=ff_shard)

    x_, y_, c_ = _place()
    my = 4 * x_ + 2 * y_ + c_
    g_rows, w_rows, m_rows, v_rows = [], [], [], []
    for l in range(depth):
        base = l * 6
        conv_g = lax.dynamic_slice(summed[base + 3:base + 6], (0, my * conv_shard), (3, conv_shard))
        g_rows += [summed[base:base + 3], _pad_to(conv_g, 1, d)]
        for dst, (nm, qn, kn, nf, cw) in ((w_rows, (norm_mix, q_norm, k_norm, norm_ffn, conv_w)),
                                          (m_rows, (m_norm_mix, m_q_norm, m_k_norm, m_norm_ffn, m_conv_w)),
                                          (v_rows, (v_norm_mix, v_q_norm, v_k_norm, v_norm_ffn, v_conv_w))):
            dst += [nm[l][None], nf[l][None], _pad_to(jnp.concatenate([qn[l], kn[l]]), 0, d)[None],
                    _pad_to(cw[l], 1, d)]
    prow = ((nrow + 7) // 8) * 8
    gs, ws, ms, vs = [_pad_to(jnp.concatenate(t, axis=0), 0, prow) for t in (g_rows, w_rows, m_rows, v_rows)]
    sm = _adamw(gs[None], ws, ms, vs, prow, "adamw_small")

    hd = q_norm.shape[1]

    def small_out(t, kind):
        per_layer = []
        for l in range(depth):
            base = l * 6
            per_layer.append({"norm_mix": t[base], "norm_ffn": t[base + 1], "q_norm": t[base + 2, :hd],
                              "k_norm": t[base + 2, hd:2 * hd], "conv_w": t[base + 3:base + 6, :conv_shard]}[kind])
        return jnp.stack(per_layer)

    def big_out(name, i):
        return jnp.stack([res[(name, l)][i] for l in range(depth)])

    outs = [loss_out, grad_x.reshape(x.shape)]
    for i in range(4):
        outs += [small_out(sm[i], "norm_mix"), big_out("w_in", i), small_out(sm[i], "q_norm"),
                 small_out(sm[i], "k_norm"), small_out(sm[i], "conv_w"), big_out("w_out", i),
                 small_out(sm[i], "norm_ffn"), big_out("w_gate", i), big_out("w_up", i), big_out("w_down", i)]
    return tuple(outs)
```

```python
import jax
import jax.numpy as jnp
from jax import lax
from jax.experimental import pallas as pl
from jax.experimental.pallas import tpu as pltpu

F32 = jnp.float32
BF16 = jnp.bfloat16
MESH = pl.DeviceIdType.MESH

N_DEV = 8
LANES = 128
HEAD_DIM = 64
KEY_CHUNK = 128
EPS = 1e-6
VMEM_LIMIT = 48 * 1024 * 1024

ADAM_LR = 0.001
ADAM_B1 = 0.9
ADAM_B2 = 0.999
ADAM_EPS = 1e-08
ADAM_WD = 0.01
ADAM_STEP = 10

NN = (((1,), (0,)), ((), ()))
NT = (((1,), (1,)), ((), ()))
TN = (((0,), (0,)), ((), ()))


def _dot(a, b, dims):
    return lax.dot_general(a.astype(BF16), b.astype(BF16), dims, preferred_element_type=F32)


def _cparams(*sem):
    return pltpu.CompilerParams(dimension_semantics=sem, vmem_limit_bytes=VMEM_LIMIT)


def _split_hi_lo(v):
    hi = v.astype(BF16)
    lo = (v - hi.astype(F32)).astype(BF16)
    return jnp.concatenate([hi, lo], axis=1)


def _rmsnorm_fwd(x, gain, tm, name):
    s, d = x.shape

    def body(x_ref, g_ref, o_ref):
        xv = x_ref[...]
        r = lax.rsqrt(jnp.mean(xv * xv, axis=-1, keepdims=True) + EPS)
        o_ref[...] = ((xv * r) * g_ref[...]).astype(o_ref.dtype)

    return pl.pallas_call(
        body, name=name, grid=(s // tm,),
        in_specs=[pl.BlockSpec((tm, d), lambda i: (i, 0)), pl.BlockSpec((1, d), lambda i: (0, 0))],
        out_specs=pl.BlockSpec((tm, d), lambda i: (i, 0)),
        out_shape=jax.ShapeDtypeStruct((s, d), BF16),
        compiler_params=_cparams("parallel"),
    )(x, gain)


def _rmsnorm_bwd(dh, x, gain, dres, tm, name):
    s, d = x.shape
    nsteps = s // tm

    def body(dh_ref, x_ref, g_ref, dres_ref, dx_ref, dg_ref):
        i = pl.program_id(0)
        xv = x_ref[...]
        r = lax.rsqrt(jnp.mean(xv * xv, axis=-1, keepdims=True) + EPS)
        xhat = xv * r
        dhv = dh_ref[...]
        dxh = dhv * g_ref[...]
        proj = jnp.mean(dxh * xhat, axis=-1, keepdims=True)
        dx_ref[...] = dres_ref[...] + r * (dxh - xhat * proj)
        part = jnp.sum((dhv * xhat).reshape(tm // 8, 8, d), axis=0)

        @pl.when(i == 0)
        def _():
            dg_ref[...] = part

        @pl.when(i > 0)
        def _():
            dg_ref[...] += part

        @pl.when(i == nsteps - 1)
        def _():
            dg_ref[...] = jnp.broadcast_to(jnp.sum(dg_ref[...], axis=0, keepdims=True), (8, d))

    row = pl.BlockSpec((tm, d), lambda i: (i, 0))
    return pl.pallas_call(
        body, name=name, grid=(nsteps,),
        in_specs=[row, row, pl.BlockSpec((1, d), lambda i: (0, 0)), row],
        out_specs=[row, pl.BlockSpec((8, d), lambda i: (0, 0))],
        out_shape=[jax.ShapeDtypeStruct((s, d), F32), jax.ShapeDtypeStruct((8, d), F32)],
        compiler_params=_cparams("arbitrary"),
    )(dh, x, gain, dres)


def _group_mean_matrix():
    r = lax.broadcasted_iota(jnp.int32, (LANES, LANES), 0) // HEAD_DIM
    c = lax.broadcasted_iota(jnp.int32, (LANES, LANES), 1) // HEAD_DIM
    return jnp.where(r == c, 1.0 / HEAD_DIM, 0.0).astype(BF16)


def _group_mean(v, gm):
    hi = v.astype(BF16)
    lo = (v - hi.astype(F32)).astype(BF16)
    return _dot(hi, gm, NN) + _dot(lo, gm, NN)


def _qknorm_fwd(proj, gains, tm, name):
    s = proj.shape[0]
    ncol = gains.shape[1] // LANES

    def body(p_ref, g_ref, gm_ref, o_ref):
        xv = p_ref[...].astype(F32)
        r = lax.rsqrt(_group_mean(xv * xv, gm_ref[...]) + EPS)
        o_ref[...] = ((xv * r) * g_ref[...]).astype(o_ref.dtype)

    blk = pl.BlockSpec((tm, LANES), lambda i, j: (i, j))
    return pl.pallas_call(
        body, name=name, grid=(s // tm, ncol),
        in_specs=[blk, pl.BlockSpec((1, LANES), lambda i, j: (0, j)),
                  pl.BlockSpec((LANES, LANES), lambda i, j: (0, 0))],
        out_specs=blk,
        out_shape=jax.ShapeDtypeStruct((s, ncol * LANES), BF16),
        compiler_params=_cparams("parallel", "parallel"),
    )(proj, gains, _group_mean_matrix())


def _qknorm_bwd(dqk, proj, gains, tm, name):
    s = proj.shape[0]
    ncol = gains.shape[1] // LANES
    nsteps = s // tm

    def body(dy_ref, p_ref, g_ref, gm_ref, dx_ref, dg_ref):
        i = pl.program_id(1)
        gm = gm_ref[...]
        xv = p_ref[...].astype(F32)
        r = lax.rsqrt(_group_mean(xv * xv, gm) + EPS)
        xhat = xv * r
        dy = dy_ref[...]
        dxh = dy * g_ref[...]
        proj_ = _group_mean(dxh * xhat, gm)
        dx_ref[...] = (r * (dxh - xhat * proj_)).astype(dx_ref.dtype)
        part = jnp.sum((dy * xhat).reshape(tm // 8, 8, LANES), axis=0)

        @pl.when(i == 0)
        def _():
            dg_ref[...] = part

        @pl.when(i > 0)
        def _():
            dg_ref[...] += part

        @pl.when(i == nsteps - 1)
        def _():
            dg_ref[...] = jnp.broadcast_to(jnp.sum(dg_ref[...], axis=0, keepdims=True), (8, LANES))

    blk = pl.BlockSpec((tm, LANES), lambda j, i: (i, j))
    return pl.pallas_call(
        body, name=name, grid=(ncol, nsteps),
        in_specs=[blk, blk, pl.BlockSpec((1, LANES), lambda j, i: (0, j)),
                  pl.BlockSpec((LANES, LANES), lambda j, i: (0, 0))],
        out_specs=[blk, pl.BlockSpec((8, LANES), lambda j, i: (0, j))],
        out_shape=[jax.ShapeDtypeStruct((s, ncol * LANES), BF16),
                   jax.ShapeDtypeStruct((8, ncol * LANES), F32)],
        compiler_params=_cparams("parallel", "arbitrary"),
    )(dqk, proj, gains, _group_mean_matrix())


CONV_ROWS = 256
HALO = 8


def _conv_fwd(proj, conv_w8, name):
    s = proj.shape[0]
    nblk = conv_w8.shape[1] // LANES
    first = 3 * nblk
    nchunk = s // CONV_ROWS

    def body(cb_ref, cc_ref, cu_ref, w_ref, y_ref, hpad):
        hpad[pl.ds(0, 2 * HALO), :] = jnp.zeros((2 * HALO, LANES), F32)

        def fill(i, _):
            r0 = pl.multiple_of(i * CONV_ROWS, CONV_ROWS)
            hpad[pl.ds(r0 + 2 * HALO, CONV_ROWS), :] = (
                cc_ref[pl.ds(r0, CONV_ROWS), :].astype(F32) * cu_ref[pl.ds(r0, CONV_ROWS), :].astype(F32))
            return 0

        lax.fori_loop(0, nchunk, fill, 0)
        w0, w1, w2 = w_ref[0:1, :], w_ref[1:2, :], w_ref[2:3, :]

        def conv(i, _):
            r0 = pl.multiple_of(i * CONV_ROWS, CONV_ROWS)
            win = hpad[pl.ds(r0 + HALO, CONV_ROWS + HALO), :]
            c = (w2 * win[HALO:] + w1 * pltpu.roll(win, 1, 0)[HALO:] + w0 * pltpu.roll(win, 2, 0)[HALO:])
            y_ref[pl.ds(r0, CONV_ROWS), :] = (cb_ref[pl.ds(r0, CONV_ROWS), :].astype(F32) * c).astype(y_ref.dtype)
            return 0

        lax.fori_loop(0, nchunk, conv, 0)

    def col(off):
        return pl.BlockSpec((s, LANES), lambda j: (0, off + j))

    return pl.pallas_call(
        body, name=name, grid=(nblk,),
        in_specs=[col(first), col(first + nblk), col(first + 2 * nblk), pl.BlockSpec((8, LANES), lambda j: (0, j))],
        out_specs=pl.BlockSpec((s, LANES), lambda j: (0, j)),
        out_shape=jax.ShapeDtypeStruct((s, nblk * LANES), BF16),
        scratch_shapes=[pltpu.VMEM((s + 2 * HALO, LANES), F32)],
        compiler_params=_cparams("parallel"),
    )(proj, proj, proj, conv_w8)


def _conv_bwd(dmix, proj, conv_w8, name):
    s = proj.shape[0]
    nblk = conv_w8.shape[1] // LANES
    first = 3 * nblk
    nchunk = s // CONV_ROWS

    def body(dy_ref, cb_ref, cc_ref, cu_ref, w_ref, dcb_ref, dcc_ref, dcu_ref, dw_ref, hpad, dcpad):
        hpad[pl.ds(0, 2 * HALO), :] = jnp.zeros((2 * HALO, LANES), F32)
        dcpad[pl.ds(s, 2 * HALO), :] = jnp.zeros((2 * HALO, LANES), F32)

        def fill(i, _):
            r0 = pl.multiple_of(i * CONV_ROWS, CONV_ROWS)
            hpad[pl.ds(r0 + 2 * HALO, CONV_ROWS), :] = (
                cc_ref[pl.ds(r0, CONV_ROWS), :].astype(F32) * cu_ref[pl.ds(r0, CONV_ROWS), :].astype(F32))
            return 0

        lax.fori_loop(0, nchunk, fill, 0)
        w0, w1, w2 = w_ref[0:1, :], w_ref[1:2, :], w_ref[2:3, :]

        def fold(v):
            return jnp.sum(v.reshape(CONV_ROWS // 8, 8, LANES), axis=0)

        def first_pass(i, acc):
            a0, a1, a2 = acc
            r0 = pl.multiple_of(i * CONV_ROWS, CONV_ROWS)
            win = hpad[pl.ds(r0 + HALO, CONV_ROWS + HALO), :]
            h0 = win[HALO:]
            h1 = pltpu.roll(win, 1, 0)[HALO:]
            h2 = pltpu.roll(win, 2, 0)[HALO:]
            c = w2 * h0 + w1 * h1 + w0 * h2
            dy = dy_ref[pl.ds(r0, CONV_ROWS), :]
            dcb_ref[pl.ds(r0, CONV_ROWS), :] = (dy * c).astype(dcb_ref.dtype)
            dc = dy * cb_ref[pl.ds(r0, CONV_ROWS), :].astype(F32)
            dcpad[pl.ds(r0, CONV_ROWS), :] = dc
            return a0 + fold(dc * h2), a1 + fold(dc * h1), a2 + fold(dc * h0)

        z8 = jnp.zeros((8, LANES), F32)
        a0, a1, a2 = lax.fori_loop(0, nchunk, first_pass, (z8, z8, z8))
        dw_ref[...] = jnp.concatenate(
            [jnp.sum(a0, axis=0, keepdims=True), jnp.sum(a1, axis=0, keepdims=True),
             jnp.sum(a2, axis=0, keepdims=True), jnp.zeros((5, LANES), F32)], axis=0)

        def second_pass(i, _):
            r0 = pl.multiple_of(i * CONV_ROWS, CONV_ROWS)
            win = dcpad[pl.ds(r0, CONV_ROWS + HALO), :]
            n = CONV_ROWS + HALO
            dh = (w2 * win[:CONV_ROWS] + w1 * pltpu.roll(win, n - 1, 0)[:CONV_ROWS]
                  + w0 * pltpu.roll(win, n - 2, 0)[:CONV_ROWS])
            dcc_ref[pl.ds(r0, CONV_ROWS), :] = (dh * cu_ref[pl.ds(r0, CONV_ROWS), :].astype(F32)).astype(dcc_ref.dtype)
            dcu_ref[pl.ds(r0, CONV_ROWS), :] = (dh * cc_ref[pl.ds(r0, CONV_ROWS), :].astype(F32)).astype(dcu_ref.dtype)
            return 0

        lax.fori_loop(0, nchunk, second_pass, 0)

    def col(off):
        return pl.BlockSpec((s, LANES), lambda j: (0, off + j))

    out = pl.BlockSpec((s, LANES), lambda j: (0, j))
    return pl.pallas_call(
        body, name=name, grid=(nblk,),
        in_specs=[col(nblk), col(first), col(first + nblk), col(first + 2 * nblk),
                  pl.BlockSpec((8, LANES), lambda j: (0, j))],
        out_specs=[out, out, out, pl.BlockSpec((8, LANES), lambda j: (0, j))],
        out_shape=[jax.ShapeDtypeStruct((s, nblk * LANES), BF16)] * 3 + [jax.ShapeDtypeStruct((8, nblk * LANES), F32)],
        scratch_shapes=[pltpu.VMEM((s + 2 * HALO, LANES), F32), pltpu.VMEM((s + 2 * HALO, LANES), F32)],
        compiler_params=_cparams("parallel"),
    )(dmix, proj, proj, proj, conv_w8)


LOG2E = 1.4426950408889634
LN2 = 0.6931471805599453
NEG_BIG = -1e30


def _cumsum_matrix(kind):
    j = lax.broadcasted_iota(jnp.int32, (KEY_CHUNK, 2 * KEY_CHUNK), 0)
    c = lax.broadcasted_iota(jnp.int32, (KEY_CHUNK, 2 * KEY_CHUNK), 1)
    tri = {"after": j > c, "upto": j <= c, "before": j < c}[kind]
    return jnp.where((c >= KEY_CHUNK) | tri, 1.0, 0.0).astype(BF16)


def _stack_heads(t, m0):
    zero = jnp.zeros_like(t)
    return jnp.concatenate([jnp.where(m0, t, zero), jnp.where(m0, zero, t)], axis=0)


def _softplus2(z):
    sp = jnp.maximum(z, 0.0) + jnp.log2(1.0 + jnp.exp2(-jnp.abs(z)))
    return sp, z - sp


def _key_chunk(ref, kc):
    return ref[pl.ds(pl.multiple_of(kc * KEY_CHUNK, KEY_CHUNK), KEY_CHUNK), :]


def _attn_fwd(qk, proj, tq, name):
    s = qk.shape[0]
    nhp = qk.shape[1] // (2 * LANES)
    nc = tq // KEY_CHUNK

    def body(q_ref, k_ref, v_ref, cm_ref, o_ref, r_ref, z_refs, ls_refs, sp_refs, ab_refs, rs_ref, acc_ref):
        qi = pl.program_id(1)
        n = qi * nc
        last = jnp.maximum(n - 1, 0)
        m0 = lax.broadcasted_iota(jnp.int32, (1, LANES), 1) < HEAD_DIM
        qs = _stack_heads(q_ref[...], m0)
        cm = cm_ref[...]
        qpos = qi * tq + (lax.broadcasted_iota(jnp.int32, (2 * tq, KEY_CHUNK), 0) & (tq - 1))
        kcol = lax.broadcasted_iota(jnp.int32, (2 * tq, KEY_CHUNK), 1)

        def chunk_at(i):
            return jnp.clip(n - 1 - i, 0, last)

        def scores(kc):
            return _dot(qs, _key_chunk(k_ref, kc), NT)

        def soft(z, mask):
            sp, ls = _softplus2(z)
            if mask is not None:
                sp = jnp.where(mask, sp, 0.0)
            return sp.astype(BF16), ls

        def weights(ls, cs, rs, mask):
            a = jnp.exp2(ls - cs[:, :KEY_CHUNK] - rs)
            if mask is not None:
                a = jnp.where(mask, a, 0.0)
            return a.astype(BF16), rs + cs[:, KEY_CHUNK:]

        def values(ab, kc):
            return _dot(jnp.concatenate([ab[:tq], ab[tq:]], axis=1), _stack_heads(_key_chunk(v_ref, kc), m0), NN)

        rs = jnp.zeros((2 * tq, LANES), F32)
        acc = jnp.zeros((tq, LANES), F32)
        for d in range(nc):
            kc = n + nc - 1 - d
            mask = (kcol + kc * KEY_CHUNK) < qpos
            spb, ls = soft(scores(kc), mask)
            ab, rs = weights(ls, _dot(spb, cm, NN), rs, mask)
            acc = acc + values(ab, kc)
        rs_ref[...] = rs
        acc_ref[...] = acc

        def step(i, par):
            cur, prv = par, 1 - par
            z_next = scores(chunk_at(i + 1))
            cs = _dot(sp_refs[prv][...], cm, NN)
            pv = values(ab_refs[cur][...], chunk_at(i - 2))
            spb, ls = soft(z_refs[cur][...], None)
            sp_refs[cur][...] = spb
            ls_refs[cur][...] = ls
            ab, rs = weights(ls_refs[prv][...], cs, rs_ref[...], None)
            ab_refs[prv][...] = ab
            rs_ref[...] = rs
            acc_ref[...] += pv
            z_refs[prv][...] = z_next

        z_refs[0][...] = scores(chunk_at(0))
        sp_refs[1][...] = jnp.zeros((2 * tq, LANES), BF16)
        ls_refs[1][...] = jnp.full((2 * tq, LANES), NEG_BIG, F32)
        ab_refs[0][...] = jnp.zeros((2 * tq, LANES), BF16)

        def two_steps(j, _):
            step(2 * j, 0)
            step(2 * j + 1, 1)
            return 0

        lax.fori_loop(0, n // 2, two_steps, 0)
        pv = values(ab_refs[0][...], chunk_at(n - 2))
        ab, rs = weights(ls_refs[1][...], _dot(sp_refs[1][...], cm, NN), rs_ref[...], None)
        o_ref[...] = (acc_ref[...] + pv + values(ab, chunk_at(n - 1))).astype(o_ref.dtype)
        r_ref[:, :LANES] = rs[:tq]
        r_ref[:, LANES:] = rs[tq:]

    def wrapped(q_ref, k_ref, v_ref, cm_ref, o_ref, r_ref, z0, z1, ls0, ls1, sp0, sp1, ab0, ab1, rs_ref, acc_ref):
        body(q_ref, k_ref, v_ref, cm_ref, o_ref, r_ref, (z0, z1), (ls0, ls1), (sp0, sp1), (ab0, ab1), rs_ref, acc_ref)

    assert nc % 2 == 0
    f32buf = pltpu.VMEM((2 * tq, LANES), F32)
    bf16buf = pltpu.VMEM((2 * tq, LANES), BF16)
    return pl.pallas_call(
        wrapped, name=name, grid=(nhp, s // tq),
        in_specs=[pl.BlockSpec((tq, LANES), lambda p, i: (i, p)),
                  pl.BlockSpec((s, LANES), lambda p, i: (0, nhp + p)),
                  pl.BlockSpec((s, LANES), lambda p, i: (0, 2 * nhp + p)),
                  pl.BlockSpec((KEY_CHUNK, 2 * KEY_CHUNK), lambda p, i: (0, 0))],
        out_specs=[pl.BlockSpec((tq, LANES), lambda p, i: (i, p)),
                   pl.BlockSpec((tq, 2 * LANES), lambda p, i: (i, p))],
        out_shape=[jax.ShapeDtypeStruct((s, nhp * LANES), BF16),
                   jax.ShapeDtypeStruct((s, nhp * 2 * LANES), F32)],
        scratch_shapes=[f32buf, f32buf, f32buf, f32buf, bf16buf, bf16buf, bf16buf, bf16buf, f32buf,
                        pltpu.VMEM((tq, LANES), F32)],
        compiler_params=_cparams("parallel", "parallel"),
    )(qk, qk, proj, _cumsum_matrix("after"))


def _attn_bwd(qk, proj, dmix, rtot, tq, name):
    s = qk.shape[0]
    nhp = qk.shape[1] // (2 * LANES)
    nc = tq // KEY_CHUNK

    def body(q_ref, k_ref, v_ref, do_ref, r_ref, cmi_ref, cme_ref, dq_ref, dk_ref, dv_ref,
             z_refs, ls_refs, sig_refs, sp_refs, gb_refs, pr_ref, gs_ref):
        qi = pl.program_id(1)

        @pl.when(qi == 0)
        def _():
            dk_ref[...] = jnp.zeros_like(dk_ref)
            dv_ref[...] = jnp.zeros_like(dv_ref)

        n = qi * nc
        last = jnp.maximum(n - 1, 0)
        m0 = lax.broadcasted_iota(jnp.int32, (1, LANES), 1) < HEAD_DIM
        qs = _stack_heads(q_ref[...], m0)
        do = do_ref[...]
        dos = _stack_heads(do.astype(BF16), m0)
        dosl = _stack_heads((do * LN2).astype(BF16), m0)
        cmi = cmi_ref[...]
        cme = cme_ref[...]
        qpos = qi * tq + (lax.broadcasted_iota(jnp.int32, (2 * tq, KEY_CHUNK), 0) & (tq - 1))
        kcol = lax.broadcasted_iota(jnp.int32, (2 * tq, KEY_CHUNK), 1)

        def chunk_at(i):
            return jnp.clip(i, 0, last)

        def scores(kc):
            return _dot(qs, _key_chunk(k_ref, kc), NT)

        def soft(z, mask):
            sp, ls = _softplus2(z)
            if mask is not None:
                sp = jnp.where(mask, sp, 0.0)
            return sp.astype(BF16), ls

        def weights(ls, cs, da, pr, kc, mask):
            a = jnp.exp2(ls - (pr - cs[:, :KEY_CHUNK]))
            if mask is not None:
                a = jnp.where(mask, a, 0.0)
            gb = (a * da).astype(BF16)
            ks = pl.multiple_of(kc * KEY_CHUNK, KEY_CHUNK)
            dv_ref[pl.ds(ks, KEY_CHUNK), :] += _dot(a, dos, TN)
            return gb, jnp.exp2(ls), pr - cs[:, KEY_CHUNK:]

        def score_grads(gb, sig, cg, gs, dq, kc, mask):
            dz = gb.astype(F32) * (1.0 - sig) - sig * (gs + cg[:, :KEY_CHUNK])
            if mask is not None:
                dz = jnp.where(mask, dz, 0.0)
            dzb = dz.astype(BF16)
            ks = pl.multiple_of(kc * KEY_CHUNK, KEY_CHUNK)
            dk_ref[pl.ds(ks, KEY_CHUNK), :] += _dot(dzb, qs, TN)
            dq = dq + _dot(jnp.concatenate([dzb[:tq], dzb[tq:]], axis=1), _stack_heads(_key_chunk(k_ref, kc), m0), NN)
            return gs + cg[:, KEY_CHUNK:], dq

        def step(i, par):
            cur, prv = par, 1 - par
            k1, k2 = chunk_at(i - 1), chunk_at(i - 2)
            z_next = scores(chunk_at(i + 1))
            cs = _dot(sp_refs[prv][...], cmi, NN)
            da = _dot(dosl, _key_chunk(v_ref, k1), NT)
            cg = _dot(gb_refs[cur][...], cme, NN)
            spb, ls = soft(z_refs[cur][...], None)
            sp_refs[cur][...] = spb
            ls_refs[cur][...] = ls
            gs, dq = score_grads(gb_refs[cur][...], sig_refs[cur][...], cg, gs_ref[...], dq_ref[...], k2, None)
            gs_ref[...] = gs
            dq_ref[...] = dq
            gb, sig, pr = weights(ls_refs[prv][...], cs, da, pr_ref[...], k1, None)
            gb_refs[prv][...] = gb
            sig_refs[prv][...] = sig
            pr_ref[...] = pr
            z_refs[prv][...] = z_next

        pr_ref[...] = jnp.concatenate([r_ref[:, :LANES], r_ref[:, LANES:]], axis=0)
        gs_ref[...] = jnp.zeros((2 * tq, LANES), F32)
        dq_ref[...] = jnp.zeros((tq, LANES), F32)
        z_refs[0][...] = scores(chunk_at(0))
        sp_refs[1][...] = jnp.zeros((2 * tq, LANES), BF16)
        ls_refs[1][...] = jnp.full((2 * tq, LANES), NEG_BIG, F32)
        gb_refs[0][...] = jnp.zeros((2 * tq, LANES), BF16)
        sig_refs[0][...] = jnp.zeros((2 * tq, LANES), F32)

        def two_steps(j, _):
            step(2 * j, 0)
            step(2 * j + 1, 1)
            return 0

        lax.fori_loop(0, n // 2, two_steps, 0)
        k1, k2 = chunk_at(n - 1), chunk_at(n - 2)
        gb, sig, pr = weights(ls_refs[1][...], _dot(sp_refs[1][...], cmi, NN),
                              _dot(dosl, _key_chunk(v_ref, k1), NT), pr_ref[...], k1, None)
        gb2 = gb_refs[0][...]
        gs, dq = score_grads(gb2, sig_refs[0][...], _dot(gb2, cme, NN), gs_ref[...], dq_ref[...], k2, None)
        gs, dq = score_grads(gb, sig, _dot(gb, cme, NN), gs, dq, k1, None)
        for d in range(nc):
            kc = n + d
            mask = (kcol + kc * KEY_CHUNK) < qpos
            spb, ls = soft(scores(kc), mask)
            gb, sig, pr = weights(ls, _dot(spb, cmi, NN), _dot(dosl, _key_chunk(v_ref, kc), NT), pr, kc, mask)
            gs, dq = score_grads(gb, sig, _dot(gb, cme, NN), gs, dq, kc, mask)
        dq_ref[...] = dq

    def wrapped(q_ref, k_ref, v_ref, do_ref, r_ref, cmi_ref, cme_ref, dq_ref, dk_ref, dv_ref,
                z0, z1, ls0, ls1, sg0, sg1, sp0, sp1, gb0, gb1, pr_ref, gs_ref):
        body(q_ref, k_ref, v_ref, do_ref, r_ref, cmi_ref, cme_ref, dq_ref, dk_ref, dv_ref,
             (z0, z1), (ls0, ls1), (sg0, sg1), (sp0, sp1), (gb0, gb1), pr_ref, gs_ref)

    assert nc % 2 == 0
    qblk = pl.BlockSpec((tq, LANES), lambda p, i: (i, p))
    full = pl.BlockSpec((s, LANES), lambda p, i: (0, p))
    cmspec = pl.BlockSpec((KEY_CHUNK, 2 * KEY_CHUNK), lambda p, i: (0, 0))
    shape = jax.ShapeDtypeStruct((s, nhp * LANES), F32)
    f32buf = pltpu.VMEM((2 * tq, LANES), F32)
    bf16buf = pltpu.VMEM((2 * tq, LANES), BF16)
    return pl.pallas_call(
        wrapped, name=name, grid=(nhp, s // tq),
        in_specs=[qblk,
                  pl.BlockSpec((s, LANES), lambda p, i: (0, nhp + p)),
                  pl.BlockSpec((s, LANES), lambda p, i: (0, 2 * nhp + p)),
                  qblk,
                  pl.BlockSpec((tq, 2 * LANES), lambda p, i: (i, p)),
                  cmspec, cmspec],
        out_specs=[qblk, full, full],
        out_shape=[shape, shape, shape],
        scratch_shapes=[f32buf] * 6 + [bf16buf] * 4 + [f32buf] * 2,
        compiler_params=_cparams("parallel", "arbitrary"),
    )(qk, qk, proj, dmix, rtot, _cumsum_matrix("upto"), _cumsum_matrix("before"))


def _mm_blocks(h, ga, widx, tm, name):
    s, d = h.shape
    nb, cols = ga.shape[1], ga.shape[3]

    def body(a_ref, b_ref, o_ref):
        o_ref[...] = _dot(a_ref[...], b_ref[...], NN).astype(o_ref.dtype)

    return pl.pallas_call(
        body, name=name, grid=(s // tm, nb),
        in_specs=[pl.BlockSpec((tm, d), lambda i, j: (i, 0)),
                  pl.BlockSpec((None, None, d, cols), lambda i, j: (widx, j, 0, 0))],
        out_specs=pl.BlockSpec((tm, cols), lambda i, j: (i, j)),
        out_shape=jax.ShapeDtypeStruct((s, nb * cols), BF16),
        compiler_params=_cparams("parallel", "arbitrary"),
    )(h, ga)


def _mm_swiglu(h, ga, gidx, uidx, tm, name):
    s, d = h.shape
    nb, cols = ga.shape[1], ga.shape[3]

    def body(a_ref, bg_ref, bu_ref, g_ref, u_ref, act_ref):
        a = a_ref[...]
        g = _dot(a, bg_ref[...], NN)
        u = _dot(a, bu_ref[...], NN)
        g_ref[...] = g.astype(g_ref.dtype)
        u_ref[...] = u.astype(u_ref.dtype)
        act_ref[...] = (g * (1.0 / (1.0 + jnp.exp(-g))) * u).astype(act_ref.dtype)

    def wspec(idx):
        return pl.BlockSpec((None, None, d, cols), lambda i, j: (idx, j, 0, 0))

    out = pl.BlockSpec((tm, cols), lambda i, j: (i, j))
    shape = jax.ShapeDtypeStruct((s, nb * cols), BF16)
    return pl.pallas_call(
        body, name=name, grid=(s // tm, nb),
        in_specs=[pl.BlockSpec((tm, d), lambda i, j: (i, 0)), wspec(gidx), wspec(uidx)],
        out_specs=[out, out, out], out_shape=[shape, shape, shape],
        compiler_params=_cparams("parallel", "arbitrary"),
    )(h, ga, ga)


def _mm_residual(a, w3, lidx, res, tm, tn, name):
    s, k = a.shape
    n = w3.shape[2]

    def body(a_ref, b_ref, r_ref, o_ref):
        o_ref[...] = r_ref[...] + _dot(a_ref[...], b_ref[...], NN)

    return pl.pallas_call(
        body, name=name, grid=(s // tm, n // tn),
        in_specs=[pl.BlockSpec((tm, k), lambda i, j: (i, 0)),
                  pl.BlockSpec((None, k, tn), lambda i, j: (lidx, 0, j)),
                  pl.BlockSpec((tm, tn), lambda i, j: (i, j))],
        out_specs=pl.BlockSpec((tm, tn), lambda i, j: (i, j)),
        out_shape=jax.ShapeDtypeStruct((s, n), F32),
        compiler_params=_cparams("parallel", "arbitrary"),
    )(a, w3, res)


def _mm_nt(a, w3, lidx, tm, tn, name):
    s, k = a.shape
    n = w3.shape[1]

    def body(a_ref, b_ref, o_ref):
        o_ref[...] = _dot(a_ref[...], b_ref[...], NT)

    return pl.pallas_call(
        body, name=name, grid=(s // tm, n // tn),
        in_specs=[pl.BlockSpec((tm, k), lambda i, j: (i, 0)),
                  pl.BlockSpec((None, tn, k), lambda i, j: (lidx, j, 0))],
        out_specs=pl.BlockSpec((tm, tn), lambda i, j: (i, j)),
        out_shape=jax.ShapeDtypeStruct((s, n), F32),
        compiler_params=_cparams("parallel", "arbitrary"),
    )(a, w3)


def _mm_nt_swiglu_bwd(dx, wd3, lidx, g, u, tm, name):
    s, d = dx.shape
    cols = g.shape[1] // N_DEV

    def body(a_ref, b_ref, g_ref, u_ref, dg_ref, du_ref):
        dact = _dot(a_ref[...], b_ref[...], NT)
        gv = g_ref[...].astype(F32)
        sig = 1.0 / (1.0 + jnp.exp(-gv))
        du_ref[...] = (dact * (gv * sig)).astype(du_ref.dtype)
        dg_ref[...] = (dact * u_ref[...].astype(F32) * (sig * (1.0 + gv * (1.0 - sig)))).astype(dg_ref.dtype)

    blk = pl.BlockSpec((tm, cols), lambda i, j: (i, j))
    shape = jax.ShapeDtypeStruct(g.shape, BF16)
    return pl.pallas_call(
        body, name=name, grid=(s // tm, N_DEV),
        in_specs=[pl.BlockSpec((tm, d), lambda i, j: (i, 0)),
                  pl.BlockSpec((None, cols, d), lambda i, j: (lidx, j, 0)), blk, blk],
        out_specs=[blk, blk], out_shape=[shape, shape],
        compiler_params=_cparams("parallel", "arbitrary"),
    )(dx, wd3, g, u)


def _mm_nt_blocks(das, ga, widxs, tm, name):
    s = das[0].shape[0]
    nb, d, cols = ga.shape[1], ga.shape[2], ga.shape[3]
    nw = len(das)

    def body(*refs):
        a_refs, b_refs, o_ref = refs[:nw], refs[nw:2 * nw], refs[2 * nw]
        k = pl.program_id(1)
        part = _dot(a_refs[0][...], b_refs[0][...], NT)
        for w in range(1, nw):
            part = part + _dot(a_refs[w][...], b_refs[w][...], NT)

        @pl.when(k == 0)
        def _():
            o_ref[...] = part

        @pl.when(k > 0)
        def _():
            o_ref[...] += part

    def wspec(idx):
        return pl.BlockSpec((None, None, d, cols), lambda i, k: (idx, k, 0, 0))

    return pl.pallas_call(
        body, name=name, grid=(s // tm, nb),
        in_specs=[pl.BlockSpec((tm, cols), lambda i, k: (i, k))] * nw + [wspec(i) for i in widxs],
        out_specs=pl.BlockSpec((tm, d), lambda i, k: (i, 0)),
        out_shape=jax.ShapeDtypeStruct((s, d), F32),
        compiler_params=_cparams("parallel", "arbitrary"),
    )(*das, *([ga] * nw))


def _mm_tn(a, b, ta, tb, tk, out_blocks, name):
    s, ka = a.shape
    nb = b.shape[1]
    nk = s // tk

    def body(a_ref, b_ref, o_ref, ob_ref):
        k = pl.program_id(2)
        part = _dot(a_ref[...], b_ref[...], TN)

        @pl.when(k == 0)
        def _():
            o_ref[...] = part

        @pl.when(k > 0)
        def _():
            o_ref[...] += part

        @pl.when(k == nk - 1)
        def _():
            ob_ref[...] = o_ref[...].astype(ob_ref.dtype)

    if out_blocks:
        out_spec = pl.BlockSpec((None, ta, tb), lambda i, j, k: (j, i, 0))
        shape = (nb // tb, ka, tb)
    else:
        out_spec = pl.BlockSpec((ta, tb), lambda i, j, k: (i, j))
        shape = (ka, nb)
    return pl.pallas_call(
        body, name=name, grid=(ka // ta, nb // tb, nk),
        in_specs=[pl.BlockSpec((tk, ta), lambda i, j, k: (k, i)),
                  pl.BlockSpec((tk, tb), lambda i, j, k: (k, j))],
        out_specs=[out_spec, out_spec],
        out_shape=[jax.ShapeDtypeStruct(shape, F32), jax.ShapeDtypeStruct(shape, BF16)],
        compiler_params=_cparams("parallel", "parallel", "arbitrary"),
    )(a, b)


def _loss_head(y, target, tm, name):
    s, d = y.shape
    nsteps = s // tm

    def body(y_ref, t_ref, dy_ref, l_ref, acc):
        i = pl.program_id(0)
        diff = y_ref[...] - t_ref[...]
        dy_ref[...] = diff * (1.0 / d)
        part = jnp.sum((diff * diff).reshape(tm // 8, 8, d), axis=0)

        @pl.when(i == 0)
        def _():
            acc[...] = part

        @pl.when(i > 0)
        def _():
            acc[...] += part

        @pl.when(i == nsteps - 1)
        def _():
            tot = jnp.sum(jnp.sum(acc[...], axis=1, keepdims=True), axis=0, keepdims=True)
            l_ref[...] = jnp.broadcast_to(tot * (0.5 / d), (8, LANES))

    row = pl.BlockSpec((tm, d), lambda i: (i, 0))
    return pl.pallas_call(
        body, name=name, grid=(nsteps,),
        in_specs=[row, row],
        out_specs=[row, pl.BlockSpec((8, LANES), lambda i: (0, 0))],
        out_shape=[jax.ShapeDtypeStruct((s, d), F32), jax.ShapeDtypeStruct((8, LANES), F32)],
        scratch_shapes=[pltpu.VMEM((8, d), F32)],
        compiler_params=_cparams("arbitrary"),
    )(y, target)


def _adamw(parts, own, w, m, v, tr, name):
    p, rows, cols = parts.shape
    c1 = 1.0 / (1.0 - ADAM_B1 ** ADAM_STEP)
    c2 = 1.0 / (1.0 - ADAM_B2 ** ADAM_STEP)

    def body(*refs):
        if own is None:
            p_ref, w_ref, m_ref, v_ref, g_ref, d_ref, nm_ref, nv_ref = refs
            g = p_ref[0]
            for k in range(1, p):
                g = g + p_ref[k]
        else:
            p_ref, own_ref, w_ref, m_ref, v_ref, g_ref, d_ref, nm_ref, nv_ref = refs
            x, y, c = _place()
            my = 4 * x + 2 * y + c
            mine = own_ref[...]
            g = jnp.where(my == 0, mine, p_ref[0].astype(F32))
            for k in range(1, p):
                g = g + jnp.where(my == k, mine, p_ref[k].astype(F32))
        nm = ADAM_B1 * m_ref[...] + (1.0 - ADAM_B1) * g
        nv = ADAM_B2 * v_ref[...] + (1.0 - ADAM_B2) * (g * g)
        g_ref[...] = g
        nm_ref[...] = nm
        nv_ref[...] = nv
        d_ref[...] = -ADAM_LR * ((nm * c1) / (jnp.sqrt(nv * c2) + ADAM_EPS) + ADAM_WD * w_ref[...])

    blk = pl.BlockSpec((tr, cols), lambda i: (i, 0))
    shape = jax.ShapeDtypeStruct((rows, cols), F32)
    return pl.pallas_call(
        body, name=name, grid=(rows // tr,),
        in_specs=[pl.BlockSpec((p, tr, cols), lambda i: (0, i, 0))] + [blk] * (3 if own is None else 4),
        out_specs=[blk] * 4, out_shape=[shape] * 4,
        compiler_params=_cparams("parallel"),
    )(*([parts] + ([] if own is None else [own]) + [w, m, v]))


def _place():
    x, y, c = lax.axis_index("x"), lax.axis_index("y"), lax.axis_index("c")
    return x, y, c


def _all_gather(shards, name):
    na = len(shards)

    def body(*refs):
        srcs, dsts = refs[:na], refs[na:2 * na]
        send_sems, recv_sems, local_sems = refs[2 * na:]
        x, y, c = _place()
        me, sibling = (x, y, c), (x, y, 1 - c)
        chips = [(1 - x, y), (x, 1 - y), (1 - x, 1 - y)]

        def slot(a, dev):
            return dsts[a].at[:, pl.ds(4 * dev[0] + 2 * dev[1] + dev[2], 1)]

        def copy(k, a, block, to, from_shard=False):
            return pltpu.make_async_remote_copy(
                src_ref=srcs[a] if from_shard else slot(a, block), dst_ref=slot(a, block),
                send_sem=send_sems.at[k, a], recv_sem=recv_sems.at[k, a], device_id=to, device_id_type=MESH)

        mine = [pltpu.make_async_copy(srcs[a], slot(a, me), local_sems.at[a]) for a in range(na)]
        for cp in mine:
            cp.start()
        first = [copy(0, a, me, sibling, True) for a in range(na)]
        first += [copy(1 + j, a, me, (*chip, c), True) for j, chip in enumerate(chips) for a in range(na)]
        for cp in first:
            cp.start()
        passed = []
        for j, chip in enumerate(chips):
            for a in range(na):
                copy(1 + j, a, (*chip, c), me).wait_recv()
                fwd = copy(4 + j, a, (*chip, c), sibling)
                fwd.start()
                passed.append(fwd)
        for a in range(na):
            copy(0, a, sibling, me).wait_recv()
        for j, chip in enumerate(chips):
            for a in range(na):
                copy(4 + j, a, (*chip, 1 - c), me).wait_recv()
        for cp in first + passed:
            cp.wait_send()
        for cp in mine:
            cp.wait()

    anyspec = pl.BlockSpec(memory_space=pl.ANY)
    return pl.pallas_call(
        body, name=name,
        in_specs=[anyspec] * na, out_specs=[anyspec] * na,
        out_shape=[jax.ShapeDtypeStruct((a.shape[0], N_DEV) + a.shape[2:], a.dtype) for a in shards],
        scratch_shapes=[pltpu.SemaphoreType.DMA((7, na)), pltpu.SemaphoreType.DMA((7, na)),
                        pltpu.SemaphoreType.DMA((na,))],
    )(*shards)


_RELATIONS = [(dx, dy, dc) for dx in (0, 1) for dy in (0, 1) for dc in (0, 1)][1:]


def _flip(v, d):
    return 1 - v if d else v


def _exchange_blocks(grads, name):
    na = len(grads)

    def body(*refs):
        srcs, dsts = refs[:na], refs[na:2 * na]
        send_sems, recv_sems, local_sems = refs[2 * na:]
        x, y, c = _place()
        my = 4 * x + 2 * y + c
        mine = [pltpu.make_async_copy(srcs[a].at[pl.ds(my, 1)], dsts[a].at[pl.ds(my, 1)], local_sems.at[a])
                for a in range(na)]
        for cp in mine:
            cp.start()
        sends = []
        for k, (dx, dy, dc) in enumerate(_RELATIONS):
            peer = (_flip(x, dx), _flip(y, dy), _flip(c, dc))
            pidx = 4 * peer[0] + 2 * peer[1] + peer[2]
            for a in range(na):
                cp = pltpu.make_async_remote_copy(
                    src_ref=srcs[a].at[pl.ds(pidx, 1)], dst_ref=dsts[a].at[pl.ds(my, 1)],
                    send_sem=send_sems.at[k, a], recv_sem=recv_sems.at[k, a], device_id=peer, device_id_type=MESH)
                cp.start()
                sends.append((cp, pidx, k, a, peer))
        for cp, pidx, k, a, peer in sends:
            pltpu.make_async_remote_copy(
                src_ref=srcs[a].at[pl.ds(pidx, 1)], dst_ref=dsts[a].at[pl.ds(pidx, 1)],
                send_sem=send_sems.at[k, a], recv_sem=recv_sems.at[k, a], device_id=peer,
                device_id_type=MESH).wait_recv()
        for cp, *_ in sends:
            cp.wait_send()
        for cp in mine:
            cp.wait()

    anyspec = pl.BlockSpec(memory_space=pl.ANY)
    return pl.pallas_call(
        body, name=name,
        in_specs=[anyspec] * na, out_specs=[anyspec] * na,
        out_shape=[jax.ShapeDtypeStruct(a.shape, a.dtype) for a in grads],
        scratch_shapes=[pltpu.SemaphoreType.DMA((7, na)), pltpu.SemaphoreType.DMA((7, na)),
                        pltpu.SemaphoreType.DMA((na,))],
    )(*grads)


def _all_reduce_small(v, name):
    r, c_ = v.shape

    def body(v_ref, o_ref, gath, send_sems, recv_sems):
        x, y, c = _place()
        my = 4 * x + 2 * y + c
        gath[my] = v_ref[...]
        sends = []
        for k, (dx, dy, dc) in enumerate(_RELATIONS):
            peer = (_flip(x, dx), _flip(y, dy), _flip(c, dc))
            cp = pltpu.make_async_remote_copy(
                src_ref=v_ref, dst_ref=gath.at[my], send_sem=send_sems.at[k], recv_sem=recv_sems.at[k],
                device_id=peer, device_id_type=MESH)
            cp.start()
            sends.append((cp, 4 * peer[0] + 2 * peer[1] + peer[2], k, peer))
        for cp, pidx, k, peer in sends:
            pltpu.make_async_remote_copy(
                src_ref=v_ref, dst_ref=gath.at[pidx], send_sem=send_sems.at[k], recv_sem=recv_sems.at[k],
                device_id=peer, device_id_type=MESH).wait_recv()
        for cp, *_ in sends:
            cp.wait_send()
        tot = gath[0]
        for k in range(1, N_DEV):
            tot = tot + gath[k]
        o_ref[...] = tot

    vm = pl.BlockSpec(memory_space=pltpu.VMEM)
    return pl.pallas_call(
        body, name=name, in_specs=[vm], out_specs=vm,
        out_shape=jax.ShapeDtypeStruct((r, c_), F32),
        scratch_shapes=[pltpu.VMEM((N_DEV, r, c_), F32), pltpu.SemaphoreType.DMA((7,)),
                        pltpu.SemaphoreType.DMA((7,))],
    )(v)


TM = 512
TM_MATMUL = 2048
TM_RESIDUAL = 1024
TQ = 256


def _pad_to(a, axis, size):
    pad = [(0, 0)] * a.ndim
    pad[axis] = (0, size - a.shape[axis])
    return jnp.pad(a, pad)


def _local_step(x, target, ga, gb, gc, conv_full, norm_mix, q_norm, k_norm, norm_ffn):
    depth = gb.shape[0]
    tm, tq = min(TM, x.shape[0]), min(TQ, x.shape[0])
    tmm, tmr = min(TM_MATMUL, x.shape[0]), min(TM_RESIDUAL, x.shape[0])
    attn = gb.shape[1] // 2
    nheads = attn // HEAD_DIM
    scale = HEAD_DIM ** -0.5 * LOG2E
    saved = []
    for l in range(depth):
        h1 = _rmsnorm_fwd(x, norm_mix[l][None], tm,f"norm_mix_fwd_{l}")
        proj = _mm_blocks(h1, ga, 3 * l, tmm, f"proj_in_{l}")
        qk_gain = jnp.concatenate([jnp.tile(q_norm[l], nheads) * scale, jnp.tile(k_norm[l], nheads)])[None]
        qk = _qknorm_fwd(proj, qk_gain, tm,f"qknorm_fwd_{l}")
        o, rtot = _attn_fwd(qk, proj, tq,f"attn_fwd_{l}")
        conv_w8 = _pad_to(conv_full[l], 0, 8)
        cv = _conv_fwd(proj, conv_w8, f"conv_fwd_{l}")
        mix = jnp.concatenate([o, cv], axis=1)
        x1 = _mm_residual(mix, gb, l, x, tmr, 512, f"proj_out_{l}")
        h2 = _rmsnorm_fwd(x1, norm_ffn[l][None], tm,f"norm_ffn_fwd_{l}")
        g, u, act = _mm_swiglu(h2, ga, 3 * l + 1, 3 * l + 2, tmm, f"ffn_up_{l}")
        x2 = _mm_residual(act, gc, l, x1, tmr, 512, f"ffn_down_{l}")
        saved.append((x, h1, proj, qk_gain, qk, rtot, conv_w8, mix, x1, h2, g, u, act))
        x = x2

    dx, loss = _loss_head(x, target, tm,"loss_head")

    grads = [None] * depth
    small = [None] * depth
    for l in reversed(range(depth)):
        x0, h1, proj, qk_gain, qk, rtot, conv_w8, mix, x1, h2, g, u, act = saved[l]
        d = x0.shape[1]
        dg, du = _mm_nt_swiglu_bwd(dx, gc, l, g, u, tmm, f"ffn_down_bwd_{l}")
        d_wdown = _mm_tn(act, dx, 768, d, tmm, False, f"dw_down_{l}")
        d_wgate = _mm_tn(h2, dg, d, ga.shape[3], tmm, True, f"dw_gate_{l}")
        d_wup = _mm_tn(h2, du, d, ga.shape[3], tmm, True, f"dw_up_{l}")
        dh2 = _mm_nt_blocks([dg, du], ga, [3 * l + 1, 3 * l + 2], tmm, f"ffn_up_bwd_{l}")
        dx1, dg_ffn = _rmsnorm_bwd(dh2, x1, norm_ffn[l][None], dx, tm,f"norm_ffn_bwd_{l}")
        dmix = _mm_nt(dx1, gb, l, tmr, 512, f"proj_out_bwd_{l}")
        d_wout = _mm_tn(mix, dx1, 512, d, tmm, False, f"dw_out_{l}")
        dcb, dcc, dcu, dconv = _conv_bwd(dmix, proj, conv_w8, f"conv_bwd_{l}")
        dq, dk, dv = _attn_bwd(qk, proj, dmix, rtot, tq,f"attn_bwd_{l}")
        dqk, dg_qk = _qknorm_bwd(jnp.concatenate([dq, dk], axis=1), proj, qk_gain, tm,f"qknorm_bwd_{l}")
        dproj = jnp.concatenate([dqk, dv.astype(BF16), dcb, dcc, dcu], axis=1)
        d_win = _mm_tn(h1, dproj, d, ga.shape[3], tmm, True, f"dw_in_{l}")
        dh1 = _mm_nt_blocks([dproj], ga, [3 * l], tmm, f"proj_in_bwd_{l}")
        dx, dg_mix = _rmsnorm_bwd(dh1, x0, norm_mix[l][None], dx1, tm,f"norm_mix_bwd_{l}")
        grads[l] = (d_win, d_wgate, d_wup, d_wout, d_wdown)
        dq_gain = jnp.sum(dg_qk[0, :attn].reshape(nheads, HEAD_DIM), axis=0) * scale
        dk_gain = jnp.sum(dg_qk[0, attn:].reshape(nheads, HEAD_DIM), axis=0)
        small[l] = (dg_mix[0], dg_ffn[0], dq_gain, dk_gain, dconv[:3])
    return loss, dx, grads, small


def kernel(x, norm_mix, w_in, q_norm, k_norm, conv_w, w_out, norm_ffn, w_gate, w_up, w_down, loss_target, m_norm_mix, m_w_in, m_q_norm, m_k_norm, m_conv_w, m_w_out, m_norm_ffn, m_w_gate, m_w_up, m_w_down, v_norm_mix, v_w_in, v_q_norm, v_k_norm, v_conv_w, v_w_out, v_norm_ffn, v_w_gate, v_w_up, v_w_down):
    depth, d, in_shard = w_in.shape
    ff_shard = w_gate.shape[2]
    ff_pad = in_shard
    conv_shard = conv_w.shape[2]
    xs = x.reshape(x.shape[-2], d)
    target = loss_target.reshape(xs.shape)

    pa = jnp.stack([w_in, _pad_to(w_gate, 2, ff_pad), _pad_to(w_up, 2, ff_pad)], axis=1)
    pa = pa.reshape(3 * depth, 1, d, in_shard).astype(BF16)
    pb = w_out.astype(BF16)[:, None]
    pc = _pad_to(w_down, 1, ff_pad).astype(BF16)[:, None]
    pd = _pad_to(_pad_to(conv_w.reshape(depth * 3, conv_shard), 0, 8), 1, LANES)[None, None]
    ga, gb, gc, gd = _all_gather([pa, pb, pc, pd], "gather_weights")
    gb = gb.reshape(depth, N_DEV * gb.shape[2], d)
    gc = gc.reshape(depth, N_DEV * ff_pad, d)
    conv_full = gd[0, :, :depth * 3, :conv_shard].transpose(1, 0, 2).reshape(depth, 3, N_DEV * conv_shard)

    loss, grad_x, grads, small = _local_step(xs, target, ga, gb, gc, conv_full, norm_mix, q_norm, k_norm, norm_ffn)

    x_, y_, c_ = _place()
    my = 4 * x_ + 2 * y_ + c_
    send, own = [], []
    for l in range(depth):
        for (g32, g16), rows_ in zip(grads[l], (d, d, d, w_out.shape[1], ff_pad)):
            send.append(g16.reshape(N_DEV, rows_, -1))
            own.append(lax.dynamic_index_in_dim(g32.reshape(N_DEV, rows_, -1), my, 0, keepdims=False))
    landed = _exchange_blocks(send, "exchange_grads")

    nconv = N_DEV * conv_shard
    rows = []
    for l in range(depth):
        g_mix, g_ffn, g_q, g_k, g_conv = small[l]
        qkrow = _pad_to(jnp.concatenate([g_q, g_k]), 0, d)
        rows += [g_mix[None], g_ffn[None], qkrow[None], _pad_to(g_conv, 1, d)]
    nrow = 6 * depth
    packed = jnp.concatenate(rows + [_pad_to(loss[:1], 1, d)], axis=0)
    packed = _pad_to(packed, 0, ((nrow + 1 + 7) // 8) * 8)
    summed = _all_reduce_small(packed, "reduce_small")
    loss_out = summed[nrow, 0]

    def big(i, w, m, v, tr, name, rows_=None, cols_=None):
        parts = landed[i]
        pr, pcn = parts.shape[1], parts.shape[2]
        w, m, v = [_pad_to(_pad_to(t, 0, pr), 1, pcn) for t in (w, m, v)]
        outs = _adamw(parts, own[i], w, m, v, tr, name)
        return [o[:rows_ or pr, :cols_ or pcn] for o in outs]

    res = {}
    for l in range(depth):
        i = 5 * l
        res[("w_in", l)] = big(i, w_in[l], m_w_in[l], v_w_in[l], 256, f"adamw_in_{l}")
        res[("w_gate", l)] = big(i + 1, w_gate[l], m_w_gate[l], v_w_gate[l], 256, f"adamw_gate_{l}", cols_=ff_shard)
        res[("w_up", l)] = big(i + 2, w_up[l], m_w_up[l], v_w_up[l], 256, f"adamw_up_{l}", cols_=ff_shard)
        res[("w_out", l)] = big(i + 3, w_out[l], m_w_out[l], v_w_out[l], w_out.shape[1], f"adamw_out_{l}")
        res[("w_down", l)] = big(i + 4, w_down[l], m_w_down[l], v_w_down[l], 128, f"adamw_down_{l}",
                                 rows_=ff_shard)

    g_rows, w_rows, m_rows, v_rows = [], [], [], []
    for l in range(depth):
        base = l * 6
        conv_g = lax.dynamic_slice(summed[base + 3:base + 6], (0, my * conv_shard), (3, conv_shard))
        g_rows += [summed[base:base + 3], _pad_to(conv_g, 1, d)]
        for dst, (nm, qn, kn, nf, cw) in ((w_rows, (norm_mix, q_norm, k_norm, norm_ffn, conv_w)),
                                          (m_rows, (m_norm_mix, m_q_norm, m_k_norm, m_norm_ffn, m_conv_w)),
                                          (v_rows, (v_norm_mix, v_q_norm, v_k_norm, v_norm_ffn, v_conv_w))):
            dst += [nm[l][None], nf[l][None], _pad_to(jnp.concatenate([qn[l], kn[l]]), 0, d)[None],
                    _pad_to(cw[l], 1, d)]
    prow = ((nrow + 7) // 8) * 8
    gs, ws, ms, vs = [_pad_to(jnp.concatenate(t, axis=0), 0, prow) for t in (g_rows, w_rows, m_rows, v_rows)]
    sm = _adamw(gs[None], None, ws, ms, vs, prow, "adamw_small")

    hd = q_norm.shape[1]

    def small_out(t, kind):
        per_layer = []
        for l in range(depth):
            base = l * 6
            per_layer.append({"norm_mix": t[base], "norm_ffn": t[base + 1], "q_norm": t[base + 2, :hd],
                              "k_norm": t[base + 2, hd:2 * hd], "conv_w": t[base + 3:base + 6, :conv_shard]}[kind])
        return jnp.stack(per_layer)

    def big_out(name, i):
        return jnp.stack([res[(name, l)][i] for l in range(depth)])

    outs = [loss_out, grad_x.reshape(x.shape)]
    for i in range(4):
        outs += [small_out(sm[i], "norm_mix"), big_out("w_in", i), small_out(sm[i], "q_norm"),
                 small_out(sm[i], "k_norm"), small_out(sm[i], "conv_w"), big_out("w_out", i),
                 small_out(sm[i], "norm_ffn"), big_out("w_gate", i), big_out("w_up", i), big_out("w_down", i)]
    return tuple(outs)
```

```python
import jax
import jax.numpy as jnp
from jax import lax
from jax.experimental import pallas as pl
from jax.experimental.pallas import tpu as pltpu

F32 = jnp.float32
BF16 = jnp.bfloat16
MESH = pl.DeviceIdType.MESH

N_DEV = 8
LANES = 128
HEAD_DIM = 64
KEY_CHUNK = 128
EPS = 1e-6
VMEM_LIMIT = 48 * 1024 * 1024

ADAM_LR = 0.001
ADAM_B1 = 0.9
ADAM_B2 = 0.999
ADAM_EPS = 1e-08
ADAM_WD = 0.01
ADAM_STEP = 10

NN = (((1,), (0,)), ((), ()))
NT = (((1,), (1,)), ((), ()))
TN = (((0,), (0,)), ((), ()))


def _dot(a, b, dims):
    return lax.dot_general(a.astype(BF16), b.astype(BF16), dims, preferred_element_type=F32)


def _cparams(*sem):
    return pltpu.CompilerParams(dimension_semantics=sem, vmem_limit_bytes=VMEM_LIMIT)


def _split_hi_lo(v):
    hi = v.astype(BF16)
    lo = (v - hi.astype(F32)).astype(BF16)
    return jnp.concatenate([hi, lo], axis=1)


def _rmsnorm_fwd(x, gain, tm, name):
    s, d = x.shape

    def body(x_ref, g_ref, o_ref):
        xv = x_ref[...]
        r = lax.rsqrt(jnp.mean(xv * xv, axis=-1, keepdims=True) + EPS)
        o_ref[...] = ((xv * r) * g_ref[...]).astype(o_ref.dtype)

    return pl.pallas_call(
        body, name=name, grid=(s // tm,),
        in_specs=[pl.BlockSpec((tm, d), lambda i: (i, 0)), pl.BlockSpec((1, d), lambda i: (0, 0))],
        out_specs=pl.BlockSpec((tm, d), lambda i: (i, 0)),
        out_shape=jax.ShapeDtypeStruct((s, d), BF16),
        compiler_params=_cparams("parallel"),
    )(x, gain)


def _rmsnorm_bwd(dh, x, gain, dres, tm, name):
    s, d = x.shape
    nsteps = s // tm

    def body(dh_ref, x_ref, g_ref, dres_ref, dx_ref, dg_ref):
        i = pl.program_id(0)
        xv = x_ref[...]
        r = lax.rsqrt(jnp.mean(xv * xv, axis=-1, keepdims=True) + EPS)
        xhat = xv * r
        dhv = dh_ref[...]
        dxh = dhv * g_ref[...]
        proj = jnp.mean(dxh * xhat, axis=-1, keepdims=True)
        dx_ref[...] = dres_ref[...] + r * (dxh - xhat * proj)
        part = jnp.sum((dhv * xhat).reshape(tm // 8, 8, d), axis=0)

        @pl.when(i == 0)
        def _():
            dg_ref[...] = part

        @pl.when(i > 0)
        def _():
            dg_ref[...] += part

        @pl.when(i == nsteps - 1)
        def _():
            dg_ref[...] = jnp.broadcast_to(jnp.sum(dg_ref[...], axis=0, keepdims=True), (8, d))

    row = pl.BlockSpec((tm, d), lambda i: (i, 0))
    return pl.pallas_call(
        body, name=name, grid=(nsteps,),
        in_specs=[row, row, pl.BlockSpec((1, d), lambda i: (0, 0)), row],
        out_specs=[row, pl.BlockSpec((8, d), lambda i: (0, 0))],
        out_shape=[jax.ShapeDtypeStruct((s, d), F32), jax.ShapeDtypeStruct((8, d), F32)],
        compiler_params=_cparams("arbitrary"),
    )(dh, x, gain, dres)


def _group_mean_matrix():
    r = lax.broadcasted_iota(jnp.int32, (LANES, LANES), 0) // HEAD_DIM
    c = lax.broadcasted_iota(jnp.int32, (LANES, LANES), 1) // HEAD_DIM
    return jnp.where(r == c, 1.0 / HEAD_DIM, 0.0).astype(BF16)


def _group_mean(v, gm):
    hi = v.astype(BF16)
    lo = (v - hi.astype(F32)).astype(BF16)
    return _dot(hi, gm, NN) + _dot(lo, gm, NN)


def _qknorm_fwd(proj, gains, tm, name):
    s = proj.shape[0]
    ncol = gains.shape[1] // LANES

    def body(p_ref, g_ref, gm_ref, o_ref):
        xv = p_ref[...].astype(F32)
        r = lax.rsqrt(_group_mean(xv * xv, gm_ref[...]) + EPS)
        o_ref[...] = ((xv * r) * g_ref[...]).astype(o_ref.dtype)

    blk = pl.BlockSpec((tm, LANES), lambda i, j: (i, j))
    return pl.pallas_call(
        body, name=name, grid=(s // tm, ncol),
        in_specs=[blk, pl.BlockSpec((1, LANES), lambda i, j: (0, j)),
                  pl.BlockSpec((LANES, LANES), lambda i, j: (0, 0))],
        out_specs=blk,
        out_shape=jax.ShapeDtypeStruct((s, ncol * LANES), BF16),
        compiler_params=_cparams("parallel", "parallel"),
    )(proj, gains, _group_mean_matrix())


def _qknorm_bwd(dqk, proj, gains, tm, name):
    s = proj.shape[0]
    ncol = gains.shape[1] // LANES
    nsteps = s // tm

    def body(dy_ref, p_ref, g_ref, gm_ref, dx_ref, dg_ref):
        i = pl.program_id(1)
        gm = gm_ref[...]
        xv = p_ref[...].astype(F32)
        r = lax.rsqrt(_group_mean(xv * xv, gm) + EPS)
        xhat = xv * r
        dy = dy_ref[...]
        dxh = dy * g_ref[...]
        proj_ = _group_mean(dxh * xhat, gm)
        dx_ref[...] = (r * (dxh - xhat * proj_)).astype(dx_ref.dtype)
        part = jnp.sum((dy * xhat).reshape(tm // 8, 8, LANES), axis=0)

        @pl.when(i == 0)
        def _():
            dg_ref[...] = part

        @pl.when(i > 0)
        def _():
            dg_ref[...] += part

        @pl.when(i == nsteps - 1)
        def _():
            dg_ref[...] = jnp.broadcast_to(jnp.sum(dg_ref[...], axis=0, keepdims=True), (8, LANES))

    blk = pl.BlockSpec((tm, LANES), lambda j, i: (i, j))
    return pl.pallas_call(
        body, name=name, grid=(ncol, nsteps),
        in_specs=[blk, blk, pl.BlockSpec((1, LANES), lambda j, i: (0, j)),
                  pl.BlockSpec((LANES, LANES), lambda j, i: (0, 0))],
        out_specs=[blk, pl.BlockSpec((8, LANES), lambda j, i: (0, j))],
        out_shape=[jax.ShapeDtypeStruct((s, ncol * LANES), BF16),
                   jax.ShapeDtypeStruct((8, ncol * LANES), F32)],
        compiler_params=_cparams("parallel", "arbitrary"),
    )(dqk, proj, gains, _group_mean_matrix())


CONV_ROWS = 256
HALO = 8


def _conv_fwd(proj, conv_w8, name):
    s = proj.shape[0]
    nblk = conv_w8.shape[1] // LANES
    first = 3 * nblk
    nchunk = s // CONV_ROWS

    def body(cb_ref, cc_ref, cu_ref, w_ref, y_ref, hpad):
        hpad[pl.ds(0, 2 * HALO), :] = jnp.zeros((2 * HALO, LANES), F32)

        def fill(i, _):
            r0 = pl.multiple_of(i * CONV_ROWS, CONV_ROWS)
            hpad[pl.ds(r0 + 2 * HALO, CONV_ROWS), :] = (
                cc_ref[pl.ds(r0, CONV_ROWS), :].astype(F32) * cu_ref[pl.ds(r0, CONV_ROWS), :].astype(F32))
            return 0

        lax.fori_loop(0, nchunk, fill, 0)
        w0, w1, w2 = w_ref[0:1, :], w_ref[1:2, :], w_ref[2:3, :]

        def conv(i, _):
            r0 = pl.multiple_of(i * CONV_ROWS, CONV_ROWS)
            win = hpad[pl.ds(r0 + HALO, CONV_ROWS + HALO), :]
            c = (w2 * win[HALO:] + w1 * pltpu.roll(win, 1, 0)[HALO:] + w0 * pltpu.roll(win, 2, 0)[HALO:])
            y_ref[pl.ds(r0, CONV_ROWS), :] = (cb_ref[pl.ds(r0, CONV_ROWS), :].astype(F32) * c).astype(y_ref.dtype)
            return 0

        lax.fori_loop(0, nchunk, conv, 0)

    def col(off):
        return pl.BlockSpec((s, LANES), lambda j: (0, off + j))

    return pl.pallas_call(
        body, name=name, grid=(nblk,),
        in_specs=[col(first), col(first + nblk), col(first + 2 * nblk), pl.BlockSpec((8, LANES), lambda j: (0, j))],
        out_specs=pl.BlockSpec((s, LANES), lambda j: (0, j)),
        out_shape=jax.ShapeDtypeStruct((s, nblk * LANES), BF16),
        scratch_shapes=[pltpu.VMEM((s + 2 * HALO, LANES), F32)],
        compiler_params=_cparams("parallel"),
    )(proj, proj, proj, conv_w8)


def _conv_bwd(dmix, proj, conv_w8, name):
    s = proj.shape[0]
    nblk = conv_w8.shape[1] // LANES
    first = 3 * nblk
    nchunk = s // CONV_ROWS

    def body(dy_ref, cb_ref, cc_ref, cu_ref, w_ref, dcb_ref, dcc_ref, dcu_ref, dw_ref, hpad, dcpad):
        hpad[pl.ds(0, 2 * HALO), :] = jnp.zeros((2 * HALO, LANES), F32)
        dcpad[pl.ds(s, 2 * HALO), :] = jnp.zeros((2 * HALO, LANES), F32)

        def fill(i, _):
            r0 = pl.multiple_of(i * CONV_ROWS, CONV_ROWS)
            hpad[pl.ds(r0 + 2 * HALO, CONV_ROWS), :] = (
                cc_ref[pl.ds(r0, CONV_ROWS), :].astype(F32) * cu_ref[pl.ds(r0, CONV_ROWS), :].astype(F32))
            return 0

        lax.fori_loop(0, nchunk, fill, 0)
        w0, w1, w2 = w_ref[0:1, :], w_ref[1:2, :], w_ref[2:3, :]

        def fold(v):
            return jnp.sum(v.reshape(CONV_ROWS // 8, 8, LANES), axis=0)

        def first_pass(i, acc):
            a0, a1, a2 = acc
            r0 = pl.multiple_of(i * CONV_ROWS, CONV_ROWS)
            win = hpad[pl.ds(r0 + HALO, CONV_ROWS + HALO), :]
            h0 = win[HALO:]
            h1 = pltpu.roll(win, 1, 0)[HALO:]
            h2 = pltpu.roll(win, 2, 0)[HALO:]
            c = w2 * h0 + w1 * h1 + w0 * h2
            dy = dy_ref[pl.ds(r0, CONV_ROWS), :]
            dcb_ref[pl.ds(r0, CONV_ROWS), :] = (dy * c).astype(dcb_ref.dtype)
            dc = dy * cb_ref[pl.ds(r0, CONV_ROWS), :].astype(F32)
            dcpad[pl.ds(r0, CONV_ROWS), :] = dc
            return a0 + fold(dc * h2), a1 + fold(dc * h1), a2 + fold(dc * h0)

        z8 = jnp.zeros((8, LANES), F32)
        a0, a1, a2 = lax.fori_loop(0, nchunk, first_pass, (z8, z8, z8))
        dw_ref[...] = jnp.concatenate(
            [jnp.sum(a0, axis=0, keepdims=True), jnp.sum(a1, axis=0, keepdims=True),
             jnp.sum(a2, axis=0, keepdims=True), jnp.zeros((5, LANES), F32)], axis=0)

        def second_pass(i, _):
            r0 = pl.multiple_of(i * CONV_ROWS, CONV_ROWS)
            win = dcpad[pl.ds(r0, CONV_ROWS + HALO), :]
            n = CONV_ROWS + HALO
            dh = (w2 * win[:CONV_ROWS] + w1 * pltpu.roll(win, n - 1, 0)[:CONV_ROWS]
                  + w0 * pltpu.roll(win, n - 2, 0)[:CONV_ROWS])
            dcc_ref[pl.ds(r0, CONV_ROWS), :] = (dh * cu_ref[pl.ds(r0, CONV_ROWS), :].astype(F32)).astype(dcc_ref.dtype)
            dcu_ref[pl.ds(r0, CONV_ROWS), :] = (dh * cc_ref[pl.ds(r0, CONV_ROWS), :].astype(F32)).astype(dcu_ref.dtype)
            return 0

        lax.fori_loop(0, nchunk, second_pass, 0)

    def col(off):
        return pl.BlockSpec((s, LANES), lambda j: (0, off + j))

    out = pl.BlockSpec((s, LANES), lambda j: (0, j))
    return pl.pallas_call(
        body, name=name, grid=(nblk,),
        in_specs=[col(nblk), col(first), col(first + nblk), col(first + 2 * nblk),
                  pl.BlockSpec((8, LANES), lambda j: (0, j))],
        out_specs=[out, out, out, pl.BlockSpec((8, LANES), lambda j: (0, j))],
        out_shape=[jax.ShapeDtypeStruct((s, nblk * LANES), BF16)] * 3 + [jax.ShapeDtypeStruct((8, nblk * LANES), F32)],
        scratch_shapes=[pltpu.VMEM((s + 2 * HALO, LANES), F32), pltpu.VMEM((s + 2 * HALO, LANES), F32)],
        compiler_params=_cparams("parallel"),
    )(dmix, proj, proj, proj, conv_w8)


LOG2E = 1.4426950408889634
LN2 = 0.6931471805599453
NEG_BIG = -1e30


def _cumsum_matrix(kind):
    j = lax.broadcasted_iota(jnp.int32, (KEY_CHUNK, 2 * KEY_CHUNK), 0)
    c = lax.broadcasted_iota(jnp.int32, (KEY_CHUNK, 2 * KEY_CHUNK), 1)
    tri = {"after": j > c, "upto": j <= c, "before": j < c}[kind]
    return jnp.where((c >= KEY_CHUNK) | tri, 1.0, 0.0).astype(BF16)


def _stack_heads(t, m0):
    zero = jnp.zeros_like(t)
    return jnp.concatenate([jnp.where(m0, t, zero), jnp.where(m0, zero, t)], axis=0)


def _softplus2(z):
    sp = jnp.maximum(z, 0.0) + jnp.log2(1.0 + jnp.exp2(-jnp.abs(z)))
    return sp, z - sp


def _key_chunk(ref, kc):
    return ref[pl.ds(pl.multiple_of(kc * KEY_CHUNK, KEY_CHUNK), KEY_CHUNK), :]


def _attn_fwd_old(qk, proj, tq, name):
    s = qk.shape[0]
    nhp = qk.shape[1] // (2 * LANES)
    nc = tq // KEY_CHUNK

    def body(q_ref, k_ref, v_ref, cm_ref, o_ref, r_ref, z_refs, ls_refs, sp_refs, ab_refs, rs_ref, acc_ref):
        qi = pl.program_id(1)
        n = qi * nc
        last = jnp.maximum(n - 1, 0)
        m0 = lax.broadcasted_iota(jnp.int32, (1, LANES), 1) < HEAD_DIM
        qs = _stack_heads(q_ref[...], m0)
        cm = cm_ref[...]
        qpos = qi * tq + (lax.broadcasted_iota(jnp.int32, (2 * tq, KEY_CHUNK), 0) & (tq - 1))
        kcol = lax.broadcasted_iota(jnp.int32, (2 * tq, KEY_CHUNK), 1)

        def chunk_at(i):
            return jnp.clip(n - 1 - i, 0, last)

        def scores(kc):
            return _dot(qs, _key_chunk(k_ref, kc), NT)

        def soft(z, mask):
            sp, ls = _softplus2(z)
            if mask is not None:
                sp = jnp.where(mask, sp, 0.0)
            return sp.astype(BF16), ls

        def weights(ls, cs, rs, mask):
            a = jnp.exp2(ls - cs[:, :KEY_CHUNK] - rs)
            if mask is not None:
                a = jnp.where(mask, a, 0.0)
            return a.astype(BF16), rs + cs[:, KEY_CHUNK:]

        def values(ab, kc):
            return _dot(jnp.concatenate([ab[:tq], ab[tq:]], axis=1), _stack_heads(_key_chunk(v_ref, kc), m0), NN)

        rs = jnp.zeros((2 * tq, LANES), F32)
        acc = jnp.zeros((tq, LANES), F32)
        for d in range(nc):
            kc = n + nc - 1 - d
            mask = (kcol + kc * KEY_CHUNK) < qpos
            spb, ls = soft(scores(kc), mask)
            ab, rs = weights(ls, _dot(spb, cm, NN), rs, mask)
            acc = acc + values(ab, kc)
        rs_ref[...] = rs
        acc_ref[...] = acc

        def step(i, par):
            cur, prv = par, 1 - par
            z_next = scores(chunk_at(i + 1))
            cs = _dot(sp_refs[prv][...], cm, NN)
            pv = values(ab_refs[cur][...], chunk_at(i - 2))
            spb, ls = soft(z_refs[cur][...], None)
            sp_refs[cur][...] = spb
            ls_refs[cur][...] = ls
            ab, rs = weights(ls_refs[prv][...], cs, rs_ref[...], None)
            ab_refs[prv][...] = ab
            rs_ref[...] = rs
            acc_ref[...] += pv
            z_refs[prv][...] = z_next

        z_refs[0][...] = scores(chunk_at(0))
        sp_refs[1][...] = jnp.zeros((2 * tq, LANES), BF16)
        ls_refs[1][...] = jnp.full((2 * tq, LANES), NEG_BIG, F32)
        ab_refs[0][...] = jnp.zeros((2 * tq, LANES), BF16)

        def two_steps(j, _):
            step(2 * j, 0)
            step(2 * j + 1, 1)
            return 0

        lax.fori_loop(0, n // 2, two_steps, 0)
        pv = values(ab_refs[0][...], chunk_at(n - 2))
        ab, rs = weights(ls_refs[1][...], _dot(sp_refs[1][...], cm, NN), rs_ref[...], None)
        o_ref[...] = (acc_ref[...] + pv + values(ab, chunk_at(n - 1))).astype(o_ref.dtype)
        r_ref[:, :LANES] = rs[:tq]
        r_ref[:, LANES:] = rs[tq:]

    def wrapped(q_ref, k_ref, v_ref, cm_ref, o_ref, r_ref, z0, z1, ls0, ls1, sp0, sp1, ab0, ab1, rs_ref, acc_ref):
        body(q_ref, k_ref, v_ref, cm_ref, o_ref, r_ref, (z0, z1), (ls0, ls1), (sp0, sp1), (ab0, ab1), rs_ref, acc_ref)

    assert nc % 2 == 0
    f32buf = pltpu.VMEM((2 * tq, LANES), F32)
    bf16buf = pltpu.VMEM((2 * tq, LANES), BF16)
    return pl.pallas_call(
        wrapped, name=name, grid=(nhp, s // tq),
        in_specs=[pl.BlockSpec((tq, LANES), lambda p, i: (i, p)),
                  pl.BlockSpec((s, LANES), lambda p, i: (0, nhp + p)),
                  pl.BlockSpec((s, LANES), lambda p, i: (0, 2 * nhp + p)),
                  pl.BlockSpec((KEY_CHUNK, 2 * KEY_CHUNK), lambda p, i: (0, 0))],
        out_specs=[pl.BlockSpec((tq, LANES), lambda p, i: (i, p)),
                   pl.BlockSpec((tq, 2 * LANES), lambda p, i: (i, p))],
        out_shape=[jax.ShapeDtypeStruct((s, nhp * LANES), BF16),
                   jax.ShapeDtypeStruct((s, nhp * 2 * LANES), F32)],
        scratch_shapes=[f32buf, f32buf, f32buf, f32buf, bf16buf, bf16buf, bf16buf, bf16buf, f32buf,
                        pltpu.VMEM((tq, LANES), F32)],
        compiler_params=_cparams("parallel", "parallel"),
    )(qk, qk, proj, _cumsum_matrix("after"))


def _attn_bwd_old(qk, proj, dmix, rtot, tq, name):
    s = qk.shape[0]
    nhp = qk.shape[1] // (2 * LANES)
    nc = tq // KEY_CHUNK

    def body(q_ref, k_ref, v_ref, do_ref, r_ref, cmi_ref, cme_ref, dq_ref, dk_ref, dv_ref,
             z_refs, ls_refs, sig_refs, sp_refs, gb_refs, pr_ref, gs_ref):
        qi = pl.program_id(1)

        @pl.when(qi == 0)
        def _():
            dk_ref[...] = jnp.zeros_like(dk_ref)
            dv_ref[...] = jnp.zeros_like(dv_ref)

        n = qi * nc
        last = jnp.maximum(n - 1, 0)
        m0 = lax.broadcasted_iota(jnp.int32, (1, LANES), 1) < HEAD_DIM
        qs = _stack_heads(q_ref[...], m0)
        do = do_ref[...]
        dos = _stack_heads(do.astype(BF16), m0)
        dosl = _stack_heads((do * LN2).astype(BF16), m0)
        cmi = cmi_ref[...]
        cme = cme_ref[...]
        qpos = qi * tq + (lax.broadcasted_iota(jnp.int32, (2 * tq, KEY_CHUNK), 0) & (tq - 1))
        kcol = lax.broadcasted_iota(jnp.int32, (2 * tq, KEY_CHUNK), 1)

        def chunk_at(i):
            return jnp.clip(i, 0, last)

        def scores(kc):
            return _dot(qs, _key_chunk(k_ref, kc), NT)

        def soft(z, mask):
            sp, ls = _softplus2(z)
            if mask is not None:
                sp = jnp.where(mask, sp, 0.0)
            return sp.astype(BF16), ls

        def weights(ls, cs, da, pr, kc, mask):
            a = jnp.exp2(ls - (pr - cs[:, :KEY_CHUNK]))
            if mask is not None:
                a = jnp.where(mask, a, 0.0)
            gb = (a * da).astype(BF16)
            ks = pl.multiple_of(kc * KEY_CHUNK, KEY_CHUNK)
            dv_ref[pl.ds(ks, KEY_CHUNK), :] += _dot(a, dos, TN)
            return gb, jnp.exp2(ls), pr - cs[:, KEY_CHUNK:]

        def score_grads(gb, sig, cg, gs, dq, kc, mask):
            dz = gb.astype(F32) * (1.0 - sig) - sig * (gs + cg[:, :KEY_CHUNK])
            if mask is not None:
                dz = jnp.where(mask, dz, 0.0)
            dzb = dz.astype(BF16)
            ks = pl.multiple_of(kc * KEY_CHUNK, KEY_CHUNK)
            dk_ref[pl.ds(ks, KEY_CHUNK), :] += _dot(dzb, qs, TN)
            dq = dq + _dot(jnp.concatenate([dzb[:tq], dzb[tq:]], axis=1), _stack_heads(_key_chunk(k_ref, kc), m0), NN)
            return gs + cg[:, KEY_CHUNK:], dq

        def step(i, par):
            cur, prv = par, 1 - par
            k1, k2 = chunk_at(i - 1), chunk_at(i - 2)
            z_next = scores(chunk_at(i + 1))
            cs = _dot(sp_refs[prv][...], cmi, NN)
            da = _dot(dosl, _key_chunk(v_ref, k1), NT)
            cg = _dot(gb_refs[cur][...], cme, NN)
            spb, ls = soft(z_refs[cur][...], None)
            sp_refs[cur][...] = spb
            ls_refs[cur][...] = ls
            gs, dq = score_grads(gb_refs[cur][...], sig_refs[cur][...], cg, gs_ref[...], dq_ref[...], k2, None)
            gs_ref[...] = gs
            dq_ref[...] = dq
            gb, sig, pr = weights(ls_refs[prv][...], cs, da, pr_ref[...], k1, None)
            gb_refs[prv][...] = gb
            sig_refs[prv][...] = sig
            pr_ref[...] = pr
            z_refs[prv][...] = z_next

        pr_ref[...] = jnp.concatenate([r_ref[:, :LANES], r_ref[:, LANES:]], axis=0)
        gs_ref[...] = jnp.zeros((2 * tq, LANES), F32)
        dq_ref[...] = jnp.zeros((tq, LANES), F32)
        z_refs[0][...] = scores(chunk_at(0))
        sp_refs[1][...] = jnp.zeros((2 * tq, LANES), BF16)
        ls_refs[1][...] = jnp.full((2 * tq, LANES), NEG_BIG, F32)
        gb_refs[0][...] = jnp.zeros((2 * tq, LANES), BF16)
        sig_refs[0][...] = jnp.zeros((2 * tq, LANES), F32)

        def two_steps(j, _):
            step(2 * j, 0)
            step(2 * j + 1, 1)
            return 0

        lax.fori_loop(0, n // 2, two_steps, 0)
        k1, k2 = chunk_at(n - 1), chunk_at(n - 2)
        gb, sig, pr = weights(ls_refs[1][...], _dot(sp_refs[1][...], cmi, NN),
                              _dot(dosl, _key_chunk(v_ref, k1), NT), pr_ref[...], k1, None)
        gb2 = gb_refs[0][...]
        gs, dq = score_grads(gb2, sig_refs[0][...], _dot(gb2, cme, NN), gs_ref[...], dq_ref[...], k2, None)
        gs, dq = score_grads(gb, sig, _dot(gb, cme, NN), gs, dq, k1, None)
        for d in range(nc):
            kc = n + d
            mask = (kcol + kc * KEY_CHUNK) < qpos
            spb, ls = soft(scores(kc), mask)
            gb, sig, pr = weights(ls, _dot(spb, cmi, NN), _dot(dosl, _key_chunk(v_ref, kc), NT), pr, kc, mask)
            gs, dq = score_grads(gb, sig, _dot(gb, cme, NN), gs, dq, kc, mask)
        dq_ref[...] = dq

    def wrapped(q_ref, k_ref, v_ref, do_ref, r_ref, cmi_ref, cme_ref, dq_ref, dk_ref, dv_ref,
                z0, z1, ls0, ls1, sg0, sg1, sp0, sp1, gb0, gb1, pr_ref, gs_ref):
        body(q_ref, k_ref, v_ref, do_ref, r_ref, cmi_ref, cme_ref, dq_ref, dk_ref, dv_ref,
             (z0, z1), (ls0, ls1), (sg0, sg1), (sp0, sp1), (gb0, gb1), pr_ref, gs_ref)

    assert nc % 2 == 0
    qblk = pl.BlockSpec((tq, LANES), lambda p, i: (i, p))
    full = pl.BlockSpec((s, LANES), lambda p, i: (0, p))
    cmspec = pl.BlockSpec((KEY_CHUNK, 2 * KEY_CHUNK), lambda p, i: (0, 0))
    shape = jax.ShapeDtypeStruct((s, nhp * LANES), F32)
    f32buf = pltpu.VMEM((2 * tq, LANES), F32)
    bf16buf = pltpu.VMEM((2 * tq, LANES), BF16)
    return pl.pallas_call(
        wrapped, name=name, grid=(nhp, s // tq),
        in_specs=[qblk,
                  pl.BlockSpec((s, LANES), lambda p, i: (0, nhp + p)),
                  pl.BlockSpec((s, LANES), lambda p, i: (0, 2 * nhp + p)),
                  qblk,
                  pl.BlockSpec((tq, 2 * LANES), lambda p, i: (i, p)),
                  cmspec, cmspec],
        out_specs=[qblk, full, full],
        out_shape=[shape, shape, shape],
        scratch_shapes=[f32buf] * 6 + [bf16buf] * 4 + [f32buf] * 2,
        compiler_params=_cparams("parallel", "arbitrary"),
    )(qk, qk, proj, dmix, rtot, _cumsum_matrix("upto"), _cumsum_matrix("before"))


def _pair_cumsum_matrix(kind):
    j = lax.broadcasted_iota(jnp.int32, (2 * KEY_CHUNK, 4 * KEY_CHUNK), 0)
    c = lax.broadcasted_iota(jnp.int32, (2 * KEY_CHUNK, 4 * KEY_CHUNK), 1)
    same_head = (j // KEY_CHUNK) == ((c // KEY_CHUNK) % 2)
    jj, cc = j % KEY_CHUNK, c % KEY_CHUNK
    tri = {"after": jj > cc, "upto": jj <= cc, "before": jj < cc}[kind]
    return jnp.where(same_head & ((c >= 2 * KEY_CHUNK) | tri), 1.0, 0.0).astype(BF16)


def _diag_bias(tq, ascending):
    nc = tq // KEY_CHUNK
    shape = (nc, tq, 2 * KEY_CHUNK)
    d = lax.broadcasted_iota(jnp.int32, shape, 0)
    r = lax.broadcasted_iota(jnp.int32, shape, 1)
    c = lax.broadcasted_iota(jnp.int32, shape, 2) % KEY_CHUNK
    chunk = d if ascending else nc - 1 - d
    return jnp.where(chunk * KEY_CHUNK + c < r, 0.0, NEG_BIG).astype(F32)


def _attn_fwd(qk, proj, tq, name):
    s = qk.shape[0]
    nhp = qk.shape[1] // (2 * LANES)
    nc = tq // KEY_CHUNK
    assert nc == 2
    w = 2 * KEY_CHUNK

    def body(q_ref, k_ref, v_ref, cm_ref, bias_ref, o_ref, r_ref, z_refs, ls_refs, cs_refs, ct_refs, sp_refs,
             ab_refs, acc_ref):
        qi = pl.program_id(1)
        nslots = (qi + 1) * nc
        m0 = lax.broadcasted_iota(jnp.int32, (1, LANES), 1) < HEAD_DIM
        q = q_ref[...]
        cm = cm_ref[...]

        def chunk_at(i):
            return jnp.clip(nslots - 1 - i, 0, nslots - 1)

        def scores(kc):
            return _dot(q, _stack_heads(_key_chunk(k_ref, kc), m0), NT)

        def values(ab, kc):
            return _dot(ab, _stack_heads(_key_chunk(v_ref, kc), m0), NN)

        def step(i, par, bias=None, stages="zscwv"):
            cur, prv = par, 1 - par
            if "z" in stages:
                z_next = scores(chunk_at(i + 1))
            if "c" in stages:
                cs = _dot(sp_refs[prv][...], cm, NN)
            if "v" in stages:
                pv = values(ab_refs[prv][...], chunk_at(i - 3))
            if "w" in stages:
                rs = r_ref[...]
                r_ref[...] = rs + ct_refs[cur][...]
                ab_refs[cur][...] = jnp.exp2(ls_refs[cur][...] - cs_refs[cur][...] - rs).astype(BF16)
            if "s" in stages:
                z = z_refs[cur][...]
                if bias is not None:
                    z = z + bias
                sp, ls = _softplus2(z)
                sp_refs[cur][...] = sp.astype(BF16)
                ls_refs[cur][...] = ls
            if "v" in stages:
                acc_ref[...] += pv
            if "c" in stages:
                cs_refs[prv][...] = cs[:, :w]
                ct_refs[prv][...] = cs[:, w:]
            if "z" in stages:
                z_refs[prv][...] = z_next

        z_refs[0][...] = scores(chunk_at(0))
        for p in range(2):
            sp_refs[p][...] = jnp.zeros((tq, w), BF16)
            ls_refs[p][...] = jnp.full((tq, w), NEG_BIG, F32)
            cs_refs[p][...] = jnp.zeros((tq, w), F32)
            ct_refs[p][...] = jnp.zeros((tq, w), F32)
            ab_refs[p][...] = jnp.zeros((tq, w), BF16)
        r_ref[...] = jnp.zeros((tq, w), F32)
        acc_ref[...] = jnp.zeros((tq, LANES), F32)
        step(0, 0, bias_ref[0])
        step(1, 1, bias_ref[1])

        def two_steps(j, _):
            step(2 * j, 0)
            step(2 * j + 1, 1)
            return 0

        lax.fori_loop(1, nslots // 2, two_steps, 0)
        step(nslots, 0, stages="cwv")
        step(nslots + 1, 1, stages="wv")
        step(nslots + 2, 0, stages="v")
        o_ref[...] = acc_ref[...].astype(o_ref.dtype)

    def wrapped(q_ref, k_ref, v_ref, cm_ref, bias_ref, o_ref, r_ref, *scratch):
        z, ls, cs, ct, sp, ab = [scratch[2 * j:2 * j + 2] for j in range(6)]
        body(q_ref, k_ref, v_ref, cm_ref, bias_ref, o_ref, r_ref, z, ls, cs, ct, sp, ab, scratch[12])

    f32buf = pltpu.VMEM((tq, w), F32)
    bf16buf = pltpu.VMEM((tq, w), BF16)
    return pl.pallas_call(
        wrapped, name=name, grid=(nhp, s // tq),
        in_specs=[pl.BlockSpec((tq, LANES), lambda p, i: (i, p)),
                  pl.BlockSpec((s, LANES), lambda p, i: (0, nhp + p)),
                  pl.BlockSpec((s, LANES), lambda p, i: (0, 2 * nhp + p)),
                  pl.BlockSpec((w, 2 * w), lambda p, i: (0, 0)),
                  pl.BlockSpec((nc, tq, w), lambda p, i: (0, 0, 0))],
        out_specs=[pl.BlockSpec((tq, LANES), lambda p, i: (i, p)),
                   pl.BlockSpec((tq, w), lambda p, i: (i, p))],
        out_shape=[jax.ShapeDtypeStruct((s, nhp * LANES), BF16),
                   jax.ShapeDtypeStruct((s, nhp * w), F32)],
        scratch_shapes=[f32buf] * 8 + [bf16buf] * 4 + [pltpu.VMEM((tq, LANES), F32)],
        compiler_params=_cparams("parallel", "parallel"),
    )(qk, qk, proj, _pair_cumsum_matrix("after"), _diag_bias(tq, False))


def _attn_bwd(qk, proj, dmix, rtot, tq, name):
    s = qk.shape[0]
    nhp = qk.shape[1] // (2 * LANES)
    nc = tq // KEY_CHUNK
    assert nc == 2
    w = 2 * KEY_CHUNK

    def body(q_ref, k_ref, v_ref, do_ref, r_ref, cmi_ref, cme_ref, bias_ref, dq_ref, dk_ref, dv_ref,
             z_refs, ls_refs, cs_refs, ct_refs, da_refs, sig_refs, cg_refs, cgt_refs, sp_refs, gb_refs,
             pr_ref, gs_ref):
        qi = pl.program_id(1)

        @pl.when(qi == 0)
        def _():
            dk_ref[...] = jnp.zeros_like(dk_ref)
            dv_ref[...] = jnp.zeros_like(dv_ref)

        nslots = (qi + 1) * nc
        m0 = lax.broadcasted_iota(jnp.int32, (1, LANES), 1) < HEAD_DIM
        q = q_ref[...]
        do = do_ref[...]
        dob = do.astype(BF16)
        dobl = (do * LN2).astype(BF16)
        cmi = cmi_ref[...]
        cme = cme_ref[...]

        def chunk_at(i):
            return jnp.clip(i, 0, nslots - 1)

        def scores(kc):
            return _dot(q, _stack_heads(_key_chunk(k_ref, kc), m0), NT)

        def value_grads(kc):
            return _dot(dobl, _stack_heads(_key_chunk(v_ref, kc), m0), NT)

        def add_key_grad(ref, kc, t):
            ks = pl.multiple_of(kc * KEY_CHUNK, KEY_CHUNK)
            ref[pl.ds(ks, KEY_CHUNK), :] += jnp.where(m0, t[:KEY_CHUNK], t[KEY_CHUNK:])

        def step(i, par, bias=None, stages="zscwgd"):
            cur, prv = par, 1 - par
            if "z" in stages:
                z_next = scores(chunk_at(i + 1))
            if "c" in stages:
                cs = _dot(sp_refs[prv][...], cmi, NN)
                da = value_grads(chunk_at(i - 1))
            if "g" in stages:
                cg = _dot(gb_refs[prv][...], cme, NN)
            if "d" in stages:
                kc = chunk_at(i - 4)
                gs = gs_ref[...]
                gs_ref[...] = gs + cgt_refs[cur][...]
                sig = sig_refs[cur][...]
                dzb = (gb_refs[cur][...].astype(F32) * (1.0 - sig) - sig * (gs + cg_refs[cur][...])).astype(BF16)
                add_key_grad(dk_ref, kc, _dot(dzb, q, TN))
                dq_ref[...] += _dot(dzb, _stack_heads(_key_chunk(k_ref, kc), m0), NN)
            if "w" in stages:
                ls = ls_refs[cur][...]
                pr = pr_ref[...]
                pr_ref[...] = pr - ct_refs[cur][...]
                a = jnp.exp2(ls - (pr - cs_refs[cur][...]))
                gb_refs[cur][...] = (a * da_refs[cur][...]).astype(BF16)
                sig_refs[cur][...] = jnp.exp2(ls)
                add_key_grad(dv_ref, chunk_at(i - 2), _dot(a, dob, TN))
            if "s" in stages:
                z = z_refs[cur][...]
                if bias is not None:
                    z = z + bias
                sp, ls = _softplus2(z)
                sp_refs[cur][...] = sp.astype(BF16)
                ls_refs[cur][...] = ls
            if "c" in stages:
                cs_refs[prv][...] = cs[:, :w]
                ct_refs[prv][...] = cs[:, w:]
                da_refs[prv][...] = da
            if "g" in stages:
                cg_refs[prv][...] = cg[:, :w]
                cgt_refs[prv][...] = cg[:, w:]
            if "z" in stages:
                z_refs[prv][...] = z_next

        for p in range(2):
            sp_refs[p][...] = jnp.zeros((tq, w), BF16)
            gb_refs[p][...] = jnp.zeros((tq, w), BF16)
            ls_refs[p][...] = jnp.full((tq, w), NEG_BIG, F32)
            for refs in (cs_refs, ct_refs, da_refs, sig_refs, cg_refs, cgt_refs):
                refs[p][...] = jnp.zeros((tq, w), F32)
        pr_ref[...] = r_ref[...]
        gs_ref[...] = jnp.zeros((tq, w), F32)
        dq_ref[...] = jnp.zeros((tq, LANES), F32)
        z_refs[0][...] = scores(chunk_at(0))

        def two_steps(j, _):
            step(2 * j, 0)
            step(2 * j + 1, 1)
            return 0

        lax.fori_loop(0, nslots // 2 - 1, two_steps, 0)
        step(nslots - 2, 0, bias_ref[0])
        step(nslots - 1, 1, bias_ref[1])
        step(nslots, 0, stages="cwgd")
        step(nslots + 1, 1, stages="wgd")
        step(nslots + 2, 0, stages="gd")
        step(nslots + 3, 1, stages="d")

    def wrapped(q_ref, k_ref, v_ref, do_ref, r_ref, cmi_ref, cme_ref, bias_ref, dq_ref, dk_ref, dv_ref, *scratch):
        pairs = [scratch[2 * j:2 * j + 2] for j in range(10)]
        body(q_ref, k_ref, v_ref, do_ref, r_ref, cmi_ref, cme_ref, bias_ref, dq_ref, dk_ref, dv_ref,
             *pairs, scratch[20], scratch[21])

    qblk = pl.BlockSpec((tq, LANES), lambda p, i: (i, p))
    full = pl.BlockSpec((s, LANES), lambda p, i: (0, p))
    cmspec = pl.BlockSpec((w, 2 * w), lambda p, i: (0, 0))
    shape = jax.ShapeDtypeStruct((s, nhp * LANES), F32)
    f32buf = pltpu.VMEM((tq, w), F32)
    bf16buf = pltpu.VMEM((tq, w), BF16)
    return pl.pallas_call(
        wrapped, name=name, grid=(nhp, s // tq),
        in_specs=[qblk,
                  pl.BlockSpec((s, LANES), lambda p, i: (0, nhp + p)),
                  pl.BlockSpec((s, LANES), lambda p, i: (0, 2 * nhp + p)),
                  qblk,
                  pl.BlockSpec((tq, w), lambda p, i: (i, p)),
                  cmspec, cmspec,
                  pl.BlockSpec((nc, tq, w), lambda p, i: (0, 0, 0))],
        out_specs=[qblk, full, full],
        out_shape=[shape, shape, shape],
        scratch_shapes=[f32buf] * 16 + [bf16buf] * 4 + [f32buf] * 2,
        compiler_params=_cparams("parallel", "arbitrary"),
    )(qk, qk, proj, dmix, rtot, _pair_cumsum_matrix("upto"), _pair_cumsum_matrix("before"), _diag_bias(tq, True))


def _mm_blocks(h, ga, widx, tm, name):
    s, d = h.shape
    nb, cols = ga.shape[1], ga.shape[3]

    def body(a_ref, b_ref, o_ref):
        o_ref[...] = _dot(a_ref[...], b_ref[...], NN).astype(o_ref.dtype)

    return pl.pallas_call(
        body, name=name, grid=(s // tm, nb),
        in_specs=[pl.BlockSpec((tm, d), lambda i, j: (i, 0)),
                  pl.BlockSpec((None, None, d, cols), lambda i, j: (widx, j, 0, 0))],
        out_specs=pl.BlockSpec((tm, cols), lambda i, j: (i, j)),
        out_shape=jax.ShapeDtypeStruct((s, nb * cols), BF16),
        compiler_params=_cparams("parallel", "arbitrary"),
    )(h, ga)


def _mm_swiglu(h, ga, gidx, uidx, tm, name):
    s, d = h.shape
    nb, cols = ga.shape[1], ga.shape[3]

    def body(a_ref, bg_ref, bu_ref, g_ref, u_ref, act_ref):
        a = a_ref[...]
        g = _dot(a, bg_ref[...], NN)
        u = _dot(a, bu_ref[...], NN)
        g_ref[...] = g.astype(g_ref.dtype)
        u_ref[...] = u.astype(u_ref.dtype)
        act_ref[...] = (g * (1.0 / (1.0 + jnp.exp(-g))) * u).astype(act_ref.dtype)

    def wspec(idx):
        return pl.BlockSpec((None, None, d, cols), lambda i, j: (idx, j, 0, 0))

    out = pl.BlockSpec((tm, cols), lambda i, j: (i, j))
    shape = jax.ShapeDtypeStruct((s, nb * cols), BF16)
    return pl.pallas_call(
        body, name=name, grid=(s // tm, nb),
        in_specs=[pl.BlockSpec((tm, d), lambda i, j: (i, 0)), wspec(gidx), wspec(uidx)],
        out_specs=[out, out, out], out_shape=[shape, shape, shape],
        compiler_params=_cparams("parallel", "arbitrary"),
    )(h, ga, ga)


def _mm_residual(a, w3, lidx, res, tm, tn, name):
    s, k = a.shape
    n = w3.shape[2]

    def body(a_ref, b_ref, r_ref, o_ref):
        o_ref[...] = r_ref[...] + _dot(a_ref[...], b_ref[...], NN)

    return pl.pallas_call(
        body, name=name, grid=(s // tm, n // tn),
        in_specs=[pl.BlockSpec((tm, k), lambda i, j: (i, 0)),
                  pl.BlockSpec((None, k, tn), lambda i, j: (lidx, 0, j)),
                  pl.BlockSpec((tm, tn), lambda i, j: (i, j))],
        out_specs=pl.BlockSpec((tm, tn), lambda i, j: (i, j)),
        out_shape=jax.ShapeDtypeStruct((s, n), F32),
        compiler_params=_cparams("parallel", "arbitrary"),
    )(a, w3, res)


def _mm_nt(a, w3, lidx, tm, tn, name):
    s, k = a.shape
    n = w3.shape[1]

    def body(a_ref, b_ref, o_ref):
        o_ref[...] = _dot(a_ref[...], b_ref[...], NT)

    return pl.pallas_call(
        body, name=name, grid=(s // tm, n // tn),
        in_specs=[pl.BlockSpec((tm, k), lambda i, j: (i, 0)),
                  pl.BlockSpec((None, tn, k), lambda i, j: (lidx, j, 0))],
        out_specs=pl.BlockSpec((tm, tn), lambda i, j: (i, j)),
        out_shape=jax.ShapeDtypeStruct((s, n), F32),
        compiler_params=_cparams("parallel", "arbitrary"),
    )(a, w3)


def _mm_nt_swiglu_bwd(dx, wd3, lidx, g, u, tm, name):
    s, d = dx.shape
    cols = g.shape[1] // N_DEV

    def body(a_ref, b_ref, g_ref, u_ref, dg_ref, du_ref):
        dact = _dot(a_ref[...], b_ref[...], NT)
        gv = g_ref[...].astype(F32)
        sig = 1.0 / (1.0 + jnp.exp(-gv))
        du_ref[...] = (dact * (gv * sig)).astype(du_ref.dtype)
        dg_ref[...] = (dact * u_ref[...].astype(F32) * (sig * (1.0 + gv * (1.0 - sig)))).astype(dg_ref.dtype)

    blk = pl.BlockSpec((tm, cols), lambda i, j: (i, j))
    shape = jax.ShapeDtypeStruct(g.shape, BF16)
    return pl.pallas_call(
        body, name=name, grid=(s // tm, N_DEV),
        in_specs=[pl.BlockSpec((tm, d), lambda i, j: (i, 0)),
                  pl.BlockSpec((None, cols, d), lambda i, j: (lidx, j, 0)), blk, blk],
        out_specs=[blk, blk], out_shape=[shape, shape],
        compiler_params=_cparams("parallel", "arbitrary"),
    )(dx, wd3, g, u)


def _mm_nt_blocks(das, ga, widxs, tm, name):
    s = das[0].shape[0]
    nb, d, cols = ga.shape[1], ga.shape[2], ga.shape[3]
    nw = len(das)

    def body(*refs):
        a_refs, b_refs, o_ref = refs[:nw], refs[nw:2 * nw], refs[2 * nw]
        k = pl.program_id(1)
        part = _dot(a_refs[0][...], b_refs[0][...], NT)
        for w in range(1, nw):
            part = part + _dot(a_refs[w][...], b_refs[w][...], NT)

        @pl.when(k == 0)
        def _():
            o_ref[...] = part

        @pl.when(k > 0)
        def _():
            o_ref[...] += part

    def wspec(idx):
        return pl.BlockSpec((None, None, d, cols), lambda i, k: (idx, k, 0, 0))

    return pl.pallas_call(
        body, name=name, grid=(s // tm, nb),
        in_specs=[pl.BlockSpec((tm, cols), lambda i, k: (i, k))] * nw + [wspec(i) for i in widxs],
        out_specs=pl.BlockSpec((tm, d), lambda i, k: (i, 0)),
        out_shape=jax.ShapeDtypeStruct((s, d), F32),
        compiler_params=_cparams("parallel", "arbitrary"),
    )(*das, *([ga] * nw))


def _mm_tn(a, b, ta, tb, tk, out_blocks, name):
    s, ka = a.shape
    nb = b.shape[1]
    nk = s // tk

    def body(a_ref, b_ref, o_ref, ob_ref):
        k = pl.program_id(2)
        part = _dot(a_ref[...], b_ref[...], TN)

        @pl.when(k == 0)
        def _():
            o_ref[...] = part

        @pl.when(k > 0)
        def _():
            o_ref[...] += part

        @pl.when(k == nk - 1)
        def _():
            ob_ref[...] = o_ref[...].astype(ob_ref.dtype)

    if out_blocks:
        out_spec = pl.BlockSpec((None, ta, tb), lambda i, j, k: (j, i, 0))
        shape = (nb // tb, ka, tb)
    else:
        out_spec = pl.BlockSpec((ta, tb), lambda i, j, k: (i, j))
        shape = (ka, nb)
    return pl.pallas_call(
        body, name=name, grid=(ka // ta, nb // tb, nk),
        in_specs=[pl.BlockSpec((tk, ta), lambda i, j, k: (k, i)),
                  pl.BlockSpec((tk, tb), lambda i, j, k: (k, j))],
        out_specs=[out_spec, out_spec],
        out_shape=[jax.ShapeDtypeStruct(shape, F32), jax.ShapeDtypeStruct(shape, BF16)],
        compiler_params=_cparams("parallel", "parallel", "arbitrary"),
    )(a, b)


def _loss_head(y, target, tm, name):
    s, d = y.shape
    nsteps = s // tm

    def body(y_ref, t_ref, dy_ref, l_ref, acc):
        i = pl.program_id(0)
        diff = y_ref[...] - t_ref[...]
        dy_ref[...] = diff * (1.0 / d)
        part = jnp.sum((diff * diff).reshape(tm // 8, 8, d), axis=0)

        @pl.when(i == 0)
        def _():
            acc[...] = part

        @pl.when(i > 0)
        def _():
            acc[...] += part

        @pl.when(i == nsteps - 1)
        def _():
            tot = jnp.sum(jnp.sum(acc[...], axis=1, keepdims=True), axis=0, keepdims=True)
            l_ref[...] = jnp.broadcast_to(tot * (0.5 / d), (8, LANES))

    row = pl.BlockSpec((tm, d), lambda i: (i, 0))
    return pl.pallas_call(
        body, name=name, grid=(nsteps,),
        in_specs=[row, row],
        out_specs=[row, pl.BlockSpec((8, LANES), lambda i: (0, 0))],
        out_shape=[jax.ShapeDtypeStruct((s, d), F32), jax.ShapeDtypeStruct((8, LANES), F32)],
        scratch_shapes=[pltpu.VMEM((8, d), F32)],
        compiler_params=_cparams("arbitrary"),
    )(y, target)


def _adamw(parts, own, w, m, v, tr, name):
    p, rows, cols = parts.shape
    c1 = 1.0 / (1.0 - ADAM_B1 ** ADAM_STEP)
    c2 = 1.0 / (1.0 - ADAM_B2 ** ADAM_STEP)

    def body(*refs):
        if own is None:
            p_ref, w_ref, m_ref, v_ref, g_ref, d_ref, nm_ref, nv_ref = refs
            g = p_ref[0]
            for k in range(1, p):
                g = g + p_ref[k]
        else:
            p_ref, own_ref, w_ref, m_ref, v_ref, g_ref, d_ref, nm_ref, nv_ref = refs
            x, y, c = _place()
            my = 4 * x + 2 * y + c
            mine = own_ref[...]
            g = jnp.where(my == 0, mine, p_ref[0].astype(F32))
            for k in range(1, p):
                g = g + jnp.where(my == k, mine, p_ref[k].astype(F32))
        nm = ADAM_B1 * m_ref[...] + (1.0 - ADAM_B1) * g
        nv = ADAM_B2 * v_ref[...] + (1.0 - ADAM_B2) * (g * g)
        g_ref[...] = g
        nm_ref[...] = nm
        nv_ref[...] = nv
        d_ref[...] = -ADAM_LR * ((nm * c1) / (jnp.sqrt(nv * c2) + ADAM_EPS) + ADAM_WD * w_ref[...])

    blk = pl.BlockSpec((tr, cols), lambda i: (i, 0))
    shape = jax.ShapeDtypeStruct((rows, cols), F32)
    return pl.pallas_call(
        body, name=name, grid=(rows // tr,),
        in_specs=[pl.BlockSpec((p, tr, cols), lambda i: (0, i, 0))] + [blk] * (3 if own is None else 4),
        out_specs=[blk] * 4, out_shape=[shape] * 4,
        compiler_params=_cparams("parallel"),
    )(*([parts] + ([] if own is None else [own]) + [w, m, v]))


def _place():
    x, y, c = lax.axis_index("x"), lax.axis_index("y"), lax.axis_index("c")
    return x, y, c


def _all_gather(shards, name):
    na = len(shards)

    def body(*refs):
        srcs, dsts = refs[:na], refs[na:2 * na]
        send_sems, recv_sems, local_sems = refs[2 * na:]
        x, y, c = _place()
        me, sibling = (x, y, c), (x, y, 1 - c)
        chips = [(1 - x, y), (x, 1 - y), (1 - x, 1 - y)]

        def slot(a, dev):
            return dsts[a].at[:, pl.ds(4 * dev[0] + 2 * dev[1] + dev[2], 1)]

        def copy(k, a, block, to, from_shard=False):
            return pltpu.make_async_remote_copy(
                src_ref=srcs[a] if from_shard else slot(a, block), dst_ref=slot(a, block),
                send_sem=send_sems.at[k, a], recv_sem=recv_sems.at[k, a], device_id=to, device_id_type=MESH)

        mine = [pltpu.make_async_copy(srcs[a], slot(a, me), local_sems.at[a]) for a in range(na)]
        for cp in mine:
            cp.start()
        first = [copy(0, a, me, sibling, True) for a in range(na)]
        first += [copy(1 + j, a, me, (*chip, c), True) for j, chip in enumerate(chips) for a in range(na)]
        for cp in first:
            cp.start()
        passed = []
        for j, chip in enumerate(chips):
            for a in range(na):
                copy(1 + j, a, (*chip, c), me).wait_recv()
                fwd = copy(4 + j, a, (*chip, c), sibling)
                fwd.start()
                passed.append(fwd)
        for a in range(na):
            copy(0, a, sibling, me).wait_recv()
        for j, chip in enumerate(chips):
            for a in range(na):
                copy(4 + j, a, (*chip, 1 - c), me).wait_recv()
        for cp in first + passed:
            cp.wait_send()
        for cp in mine:
            cp.wait()

    anyspec = pl.BlockSpec(memory_space=pl.ANY)
    return pl.pallas_call(
        body, name=name,
        in_specs=[anyspec] * na, out_specs=[anyspec] * na,
        out_shape=[jax.ShapeDtypeStruct((a.shape[0], N_DEV) + a.shape[2:], a.dtype) for a in shards],
        scratch_shapes=[pltpu.SemaphoreType.DMA((7, na)), pltpu.SemaphoreType.DMA((7, na)),
                        pltpu.SemaphoreType.DMA((na,))],
    )(*shards)


_RELATIONS = [(dx, dy, dc) for dx in (0, 1) for dy in (0, 1) for dc in (0, 1)][1:]


def _flip(v, d):
    return 1 - v if d else v


def _exchange_blocks(grads, name):
    na = len(grads)

    def body(*refs):
        srcs, dsts = refs[:na], refs[na:2 * na]
        send_sems, recv_sems, local_sems = refs[2 * na:]
        x, y, c = _place()
        my = 4 * x + 2 * y + c
        mine = [pltpu.make_async_copy(srcs[a].at[pl.ds(my, 1)], dsts[a].at[pl.ds(my, 1)], local_sems.at[a])
                for a in range(na)]
        for cp in mine:
            cp.start()
        sends = []
        for k, (dx, dy, dc) in enumerate(_RELATIONS):
            peer = (_flip(x, dx), _flip(y, dy), _flip(c, dc))
            pidx = 4 * peer[0] + 2 * peer[1] + peer[2]
            for a in range(na):
                cp = pltpu.make_async_remote_copy(
                    src_ref=srcs[a].at[pl.ds(pidx, 1)], dst_ref=dsts[a].at[pl.ds(my, 1)],
                    send_sem=send_sems.at[k, a], recv_sem=recv_sems.at[k, a], device_id=peer, device_id_type=MESH)
                cp.start()
                sends.append((cp, pidx, k, a, peer))
        for cp, pidx, k, a, peer in sends:
            pltpu.make_async_remote_copy(
                src_ref=srcs[a].at[pl.ds(pidx, 1)], dst_ref=dsts[a].at[pl.ds(pidx, 1)],
                send_sem=send_sems.at[k, a], recv_sem=recv_sems.at[k, a], device_id=peer,
                device_id_type=MESH).wait_recv()
        for cp, *_ in sends:
            cp.wait_send()
        for cp in mine:
            cp.wait()

    anyspec = pl.BlockSpec(memory_space=pl.ANY)
    return pl.pallas_call(
        body, name=name,
        in_specs=[anyspec] * na, out_specs=[anyspec] * na,
        out_shape=[jax.ShapeDtypeStruct(a.shape, a.dtype) for a in grads],
        scratch_shapes=[pltpu.SemaphoreType.DMA((7, na)), pltpu.SemaphoreType.DMA((7, na)),
                        pltpu.SemaphoreType.DMA((na,))],
    )(*grads)


def _all_reduce_small(v, name):
    r, c_ = v.shape

    def body(v_ref, o_ref, gath, send_sems, recv_sems):
        x, y, c = _place()
        my = 4 * x + 2 * y + c
        gath[my] = v_ref[...]
        sends = []
        for k, (dx, dy, dc) in enumerate(_RELATIONS):
            peer = (_flip(x, dx), _flip(y, dy), _flip(c, dc))
            cp = pltpu.make_async_remote_copy(
                src_ref=v_ref, dst_ref=gath.at[my], send_sem=send_sems.at[k], recv_sem=recv_sems.at[k],
                device_id=peer, device_id_type=MESH)
            cp.start()
            sends.append((cp, 4 * peer[0] + 2 * peer[1] + peer[2], k, peer))
        for cp, pidx, k, peer in sends:
            pltpu.make_async_remote_copy(
                src_ref=v_ref, dst_ref=gath.at[pidx], send_sem=send_sems.at[k], recv_sem=recv_sems.at[k],
                device_id=peer, device_id_type=MESH).wait_recv()
        for cp, *_ in sends:
            cp.wait_send()
        tot = gath[0]
        for k in range(1, N_DEV):
            tot = tot + gath[k]
        o_ref[...] = tot

    vm = pl.BlockSpec(memory_space=pltpu.VMEM)
    return pl.pallas_call(
        body, name=name, in_specs=[vm], out_specs=vm,
        out_shape=jax.ShapeDtypeStruct((r, c_), F32),
        scratch_shapes=[pltpu.VMEM((N_DEV, r, c_), F32), pltpu.SemaphoreType.DMA((7,)),
                        pltpu.SemaphoreType.DMA((7,))],
    )(v)


TM = 512
TM_MATMUL = 2048
TM_RESIDUAL = 1024
TQ = 256


def _pad_to(a, axis, size):
    pad = [(0, 0)] * a.ndim
    pad[axis] = (0, size - a.shape[axis])
    return jnp.pad(a, pad)


def _local_step(x, target, ga, gb, gc, conv_full, norm_mix, q_norm, k_norm, norm_ffn):
    depth = gb.shape[0]
    tm, tq = min(TM, x.shape[0]), min(TQ, x.shape[0])
    tmm, tmr = min(TM_MATMUL, x.shape[0]), min(TM_RESIDUAL, x.shape[0])
    attn = gb.shape[1] // 2
    nheads = attn // HEAD_DIM
    scale = HEAD_DIM ** -0.5 * LOG2E
    saved = []
    for l in range(depth):
        h1 = _rmsnorm_fwd(x, norm_mix[l][None], tm,f"norm_mix_fwd_{l}")
        proj = _mm_blocks(h1, ga, 3 * l, tmm, f"proj_in_{l}")
        qk_gain = jnp.concatenate([jnp.tile(q_norm[l], nheads) * scale, jnp.tile(k_norm[l], nheads)])[None]
        qk = _qknorm_fwd(proj, qk_gain, tm,f"qknorm_fwd_{l}")
        o, rtot = _attn_fwd(qk, proj, tq,f"attn_fwd_{l}")
        conv_w8 = _pad_to(conv_full[l], 0, 8)
        cv = _conv_fwd(proj, conv_w8, f"conv_fwd_{l}")
        mix = jnp.concatenate([o, cv], axis=1)
        x1 = _mm_residual(mix, gb, l, x, tmr, 512, f"proj_out_{l}")
        h2 = _rmsnorm_fwd(x1, norm_ffn[l][None], tm,f"norm_ffn_fwd_{l}")
        g, u, act = _mm_swiglu(h2, ga, 3 * l + 1, 3 * l + 2, tmm, f"ffn_up_{l}")
        x2 = _mm_residual(act, gc, l, x1, tmr, 512, f"ffn_down_{l}")
        saved.append((x, h1, proj, qk_gain, qk, rtot, conv_w8, mix, x1, h2, g, u, act))
        x = x2

    dx, loss = _loss_head(x, target, tm,"loss_head")

    grads = [None] * depth
    small = [None] * depth
    for l in reversed(range(depth)):
        x0, h1, proj, qk_gain, qk, rtot, conv_w8, mix, x1, h2, g, u, act = saved[l]
        d = x0.shape[1]
        dg, du = _mm_nt_swiglu_bwd(dx, gc, l, g, u, tmm, f"ffn_down_bwd_{l}")
        d_wdown = _mm_tn(act, dx, 768, d, tmm, False, f"dw_down_{l}")
        d_wgate = _mm_tn(h2, dg, d, ga.shape[3], tmm, True, f"dw_gate_{l}")
        d_wup = _mm_tn(h2, du, d, ga.shape[3], tmm, True, f"dw_up_{l}")
        dh2 = _mm_nt_blocks([dg, du], ga, [3 * l + 1, 3 * l + 2], tmm, f"ffn_up_bwd_{l}")
        dx1, dg_ffn = _rmsnorm_bwd(dh2, x1, norm_ffn[l][None], dx, tm,f"norm_ffn_bwd_{l}")
        dmix = _mm_nt(dx1, gb, l, tmr, 512, f"proj_out_bwd_{l}")
        d_wout = _mm_tn(mix, dx1, 512, d, tmm, False, f"dw_out_{l}")
        dcb, dcc, dcu, dconv = _conv_bwd(dmix, proj, conv_w8, f"conv_bwd_{l}")
        dq, dk, dv = _attn_bwd(qk, proj, dmix, rtot, tq,f"attn_bwd_{l}")
        dqk, dg_qk = _qknorm_bwd(jnp.concatenate([dq, dk], axis=1), proj, qk_gain, tm,f"qknorm_bwd_{l}")
        dproj = jnp.concatenate([dqk, dv.astype(BF16), dcb, dcc, dcu], axis=1)
        d_win = _mm_tn(h1, dproj, d, ga.shape[3], tmm, True, f"dw_in_{l}")
        dh1 = _mm_nt_blocks([dproj], ga, [3 * l], tmm, f"proj_in_bwd_{l}")
        dx, dg_mix = _rmsnorm_bwd(dh1, x0, norm_mix[l][None], dx1, tm,f"norm_mix_bwd_{l}")
        grads[l] = (d_win, d_wgate, d_wup, d_wout, d_wdown)
        dq_gain = jnp.sum(dg_qk[0, :attn].reshape(nheads, HEAD_DIM), axis=0) * scale
        dk_gain = jnp.sum(dg_qk[0, attn:].reshape(nheads, HEAD_DIM), axis=0)
        small[l] = (dg_mix[0], dg_ffn[0], dq_gain, dk_gain, dconv[:3])
    return loss, dx, grads, small


def kernel(x, norm_mix, w_in, q_norm, k_norm, conv_w, w_out, norm_ffn, w_gate, w_up, w_down, loss_target, m_norm_mix, m_w_in, m_q_norm, m_k_norm, m_conv_w, m_w_out, m_norm_ffn, m_w_gate, m_w_up, m_w_down, v_norm_mix, v_w_in, v_q_norm, v_k_norm, v_conv_w, v_w_out, v_norm_ffn, v_w_gate, v_w_up, v_w_down):
    depth, d, in_shard = w_in.shape
    ff_shard = w_gate.shape[2]
    ff_pad = in_shard
    conv_shard = conv_w.shape[2]
    xs = x.reshape(x.shape[-2], d)
    target = loss_target.reshape(xs.shape)

    pa = jnp.stack([w_in, _pad_to(w_gate, 2, ff_pad), _pad_to(w_up, 2, ff_pad)], axis=1)
    pa = pa.reshape(3 * depth, 1, d, in_shard).astype(BF16)
    pb = w_out.astype(BF16)[:, None]
    pc = _pad_to(w_down, 1, ff_pad).astype(BF16)[:, None]
    pd = _pad_to(_pad_to(conv_w.reshape(depth * 3, conv_shard), 0, 8), 1, LANES)[None, None]
    ga, gb, gc, gd = _all_gather([pa, pb, pc, pd], "gather_weights")
    gb = gb.reshape(depth, N_DEV * gb.shape[2], d)
    gc = gc.reshape(depth, N_DEV * ff_pad, d)
    conv_full = gd[0, :, :depth * 3, :conv_shard].transpose(1, 0, 2).reshape(depth, 3, N_DEV * conv_shard)

    loss, grad_x, grads, small = _local_step(xs, target, ga, gb, gc, conv_full, norm_mix, q_norm, k_norm, norm_ffn)

    x_, y_, c_ = _place()
    my = 4 * x_ + 2 * y_ + c_
    send, own = [], []
    for l in range(depth):
        for (g32, g16), rows_ in zip(grads[l], (d, d, d, w_out.shape[1], ff_pad)):
            send.append(g16.reshape(N_DEV, rows_, -1))
            own.append(lax.dynamic_index_in_dim(g32.reshape(N_DEV, rows_, -1), my, 0, keepdims=False))
    landed = _exchange_blocks(send, "exchange_grads")

    nconv = N_DEV * conv_shard
    rows = []
    for l in range(depth):
        g_mix, g_ffn, g_q, g_k, g_conv = small[l]
        qkrow = _pad_to(jnp.concatenate([g_q, g_k]), 0, d)
        rows += [g_mix[None], g_ffn[None], qkrow[None], _pad_to(g_conv, 1, d)]
    nrow = 6 * depth
    packed = jnp.concatenate(rows + [_pad_to(loss[:1], 1, d)], axis=0)
    packed = _pad_to(packed, 0, ((nrow + 1 + 7) // 8) * 8)
    summed = _all_reduce_small(packed, "reduce_small")
    loss_out = summed[nrow, 0]

    def big(i, w, m, v, tr, name, rows_=None, cols_=None):
        parts = landed[i]
        pr, pcn = parts.shape[1], parts.shape[2]
        w, m, v = [_pad_to(_pad_to(t, 0, pr), 1, pcn) for t in (w, m, v)]
        outs = _adamw(parts, own[i], w, m, v, tr, name)
        return [o[:rows_ or pr, :cols_ or pcn] for o in outs]

    res = {}
    for l in range(depth):
        i = 5 * l
        res[("w_in", l)] = big(i, w_in[l], m_w_in[l], v_w_in[l], 256, f"adamw_in_{l}")
        res[("w_gate", l)] = big(i + 1, w_gate[l], m_w_gate[l], v_w_gate[l], 256, f"adamw_gate_{l}", cols_=ff_shard)
        res[("w_up", l)] = big(i + 2, w_up[l], m_w_up[l], v_w_up[l], 256, f"adamw_up_{l}", cols_=ff_shard)
        res[("w_out", l)] = big(i + 3, w_out[l], m_w_out[l], v_w_out[l], w_out.shape[1], f"adamw_out_{l}")
        res[("w_down", l)] = big(i + 4, w_down[l], m_w_down[l], v_w_down[l], 128, f"adamw_down_{l}",
                                 rows_=ff_shard)

    g_rows, w_rows, m_rows, v_rows = [], [], [], []
    for l in range(depth):
        base = l * 6
        conv_g = lax.dynamic_slice(summed[base + 3:base + 6], (0, my * conv_shard), (3, conv_shard))
        g_rows += [summed[base:base + 3], _pad_to(conv_g, 1, d)]
        for dst, (nm, qn, kn, nf, cw) in ((w_rows, (norm_mix, q_norm, k_norm, norm_ffn, conv_w)),
                                          (m_rows, (m_norm_mix, m_q_norm, m_k_norm, m_norm_ffn, m_conv_w)),
                                          (v_rows, (v_norm_mix, v_q_norm, v_k_norm, v_norm_ffn, v_conv_w))):
            dst += [nm[l][None], nf[l][None], _pad_to(jnp.concatenate([qn[l], kn[l]]), 0, d)[None],
                    _pad_to(cw[l], 1, d)]
    prow = ((nrow + 7) // 8) * 8
    gs, ws, ms, vs = [_pad_to(jnp.concatenate(t, axis=0), 0, prow) for t in (g_rows, w_rows, m_rows, v_rows)]
    sm = _adamw(gs[None], None, ws, ms, vs, prow, "adamw_small")

    hd = q_norm.shape[1]

    def small_out(t, kind):
        per_layer = []
        for l in range(depth):
            base = l * 6
            per_layer.append({"norm_mix": t[base], "norm_ffn": t[base + 1], "q_norm": t[base + 2, :hd],
                              "k_norm": t[base + 2, hd:2 * hd], "conv_w": t[base + 3:base + 6, :conv_shard]}[kind])
        return jnp.stack(per_layer)

    def big_out(name, i):
        return jnp.stack([res[(name, l)][i] for l in range(depth)])

    outs = [loss_out, grad_x.reshape(x.shape)]
    for i in range(4):
        outs += [small_out(sm[i], "norm_mix"), big_out("w_in", i), small_out(sm[i], "q_norm"),
                 small_out(sm[i], "k_norm"), small_out(sm[i], "conv_w"), big_out("w_out", i),
                 small_out(sm[i], "norm_ffn"), big_out("w_gate", i), big_out("w_up", i), big_out("w_down", i)]
    return tuple(outs)
```

```python
import jax
import jax.numpy as jnp
from jax import lax
from jax.experimental import pallas as pl
from jax.experimental.pallas import tpu as pltpu

F32 = jnp.float32
BF16 = jnp.bfloat16
MESH = pl.DeviceIdType.MESH

N_DEV = 8
LANES = 128
HEAD_DIM = 64
KEY_CHUNK = 128
EPS = 1e-6
VMEM_LIMIT = 48 * 1024 * 1024

ADAM_LR = 0.001
ADAM_B1 = 0.9
ADAM_B2 = 0.999
ADAM_EPS = 1e-08
ADAM_WD = 0.01
ADAM_STEP = 10

NN = (((1,), (0,)), ((), ()))
NT = (((1,), (1,)), ((), ()))
TN = (((0,), (0,)), ((), ()))


def _dot(a, b, dims):
    return lax.dot_general(a.astype(BF16), b.astype(BF16), dims, preferred_element_type=F32)


def _cparams(*sem):
    return pltpu.CompilerParams(dimension_semantics=sem, vmem_limit_bytes=VMEM_LIMIT)


def _split_hi_lo(v):
    hi = v.astype(BF16)
    lo = (v - hi.astype(F32)).astype(BF16)
    return jnp.concatenate([hi, lo], axis=1)


def _rmsnorm_fwd(x, gain, tm, name):
    s, d = x.shape

    def body(x_ref, g_ref, o_ref):
        xv = x_ref[...]
        r = lax.rsqrt(jnp.mean(xv * xv, axis=-1, keepdims=True) + EPS)
        o_ref[...] = ((xv * r) * g_ref[...]).astype(o_ref.dtype)

    return pl.pallas_call(
        body, name=name, grid=(s // tm,),
        in_specs=[pl.BlockSpec((tm, d), lambda i: (i, 0)), pl.BlockSpec((1, d), lambda i: (0, 0))],
        out_specs=pl.BlockSpec((tm, d), lambda i: (i, 0)),
        out_shape=jax.ShapeDtypeStruct((s, d), BF16),
        compiler_params=_cparams("parallel"),
    )(x, gain)


def _rmsnorm_bwd(dh, x, gain, dres, tm, name):
    s, d = x.shape
    nsteps = s // tm

    def body(dh_ref, x_ref, g_ref, dres_ref, dx_ref, dg_ref):
        i = pl.program_id(0)
        xv = x_ref[...]
        r = lax.rsqrt(jnp.mean(xv * xv, axis=-1, keepdims=True) + EPS)
        xhat = xv * r
        dhv = dh_ref[...]
        dxh = dhv * g_ref[...]
        proj = jnp.mean(dxh * xhat, axis=-1, keepdims=True)
        dx_ref[...] = dres_ref[...] + r * (dxh - xhat * proj)
        part = jnp.sum((dhv * xhat).reshape(tm // 8, 8, d), axis=0)

        @pl.when(i == 0)
        def _():
            dg_ref[...] = part

        @pl.when(i > 0)
        def _():
            dg_ref[...] += part

        @pl.when(i == nsteps - 1)
        def _():
            dg_ref[...] = jnp.broadcast_to(jnp.sum(dg_ref[...], axis=0, keepdims=True), (8, d))

    row = pl.BlockSpec((tm, d), lambda i: (i, 0))
    return pl.pallas_call(
        body, name=name, grid=(nsteps,),
        in_specs=[row, row, pl.BlockSpec((1, d), lambda i: (0, 0)), row],
        out_specs=[row, pl.BlockSpec((8, d), lambda i: (0, 0))],
        out_shape=[jax.ShapeDtypeStruct((s, d), F32), jax.ShapeDtypeStruct((8, d), F32)],
        compiler_params=_cparams("arbitrary"),
    )(dh, x, gain, dres)


def _group_mean_matrix():
    r = lax.broadcasted_iota(jnp.int32, (LANES, LANES), 0) // HEAD_DIM
    c = lax.broadcasted_iota(jnp.int32, (LANES, LANES), 1) // HEAD_DIM
    return jnp.where(r == c, 1.0 / HEAD_DIM, 0.0).astype(BF16)


def _group_mean(v, gm):
    hi = v.astype(BF16)
    lo = (v - hi.astype(F32)).astype(BF16)
    return _dot(hi, gm, NN) + _dot(lo, gm, NN)


def _qknorm_fwd(proj, gains, tm, name):
    s = proj.shape[0]
    ncol = gains.shape[1] // LANES

    def body(p_ref, g_ref, gm_ref, o_ref):
        xv = p_ref[...].astype(F32)
        r = lax.rsqrt(_group_mean(xv * xv, gm_ref[...]) + EPS)
        o_ref[...] = ((xv * r) * g_ref[...]).astype(o_ref.dtype)

    blk = pl.BlockSpec((tm, LANES), lambda i, j: (i, j))
    return pl.pallas_call(
        body, name=name, grid=(s // tm, ncol),
        in_specs=[blk, pl.BlockSpec((1, LANES), lambda i, j: (0, j)),
                  pl.BlockSpec((LANES, LANES), lambda i, j: (0, 0))],
        out_specs=blk,
        out_shape=jax.ShapeDtypeStruct((s, ncol * LANES), BF16),
        compiler_params=_cparams("parallel", "parallel"),
    )(proj, gains, _group_mean_matrix())


def _qknorm_bwd(dqk, proj, gains, tm, name):
    s = proj.shape[0]
    ncol = gains.shape[1] // LANES
    nsteps = s // tm

    def body(dy_ref, p_ref, g_ref, gm_ref, dx_ref, dg_ref):
        i = pl.program_id(1)
        gm = gm_ref[...]
        xv = p_ref[...].astype(F32)
        r = lax.rsqrt(_group_mean(xv * xv, gm) + EPS)
        xhat = xv * r
        dy = dy_ref[...]
        dxh = dy * g_ref[...]
        proj_ = _group_mean(dxh * xhat, gm)
        dx_ref[...] = (r * (dxh - xhat * proj_)).astype(dx_ref.dtype)
        part = jnp.sum((dy * xhat).reshape(tm // 8, 8, LANES), axis=0)

        @pl.when(i == 0)
        def _():
            dg_ref[...] = part

        @pl.when(i > 0)
        def _():
            dg_ref[...] += part

        @pl.when(i == nsteps - 1)
        def _():
            dg_ref[...] = jnp.broadcast_to(jnp.sum(dg_ref[...], axis=0, keepdims=True), (8, LANES))

    blk = pl.BlockSpec((tm, LANES), lambda j, i: (i, j))
    return pl.pallas_call(
        body, name=name, grid=(ncol, nsteps),
        in_specs=[blk, blk, pl.BlockSpec((1, LANES), lambda j, i: (0, j)),
                  pl.BlockSpec((LANES, LANES), lambda j, i: (0, 0))],
        out_specs=[blk, pl.BlockSpec((8, LANES), lambda j, i: (0, j))],
        out_shape=[jax.ShapeDtypeStruct((s, ncol * LANES), BF16),
                   jax.ShapeDtypeStruct((8, ncol * LANES), F32)],
        compiler_params=_cparams("parallel", "arbitrary"),
    )(dqk, proj, gains, _group_mean_matrix())


CONV_ROWS = 256
HALO = 8


def _conv_fwd(proj, conv_w8, name):
    s = proj.shape[0]
    nblk = conv_w8.shape[1] // LANES
    first = 3 * nblk
    nchunk = s // CONV_ROWS

    def body(cb_ref, cc_ref, cu_ref, w_ref, y_ref, hpad):
        hpad[pl.ds(0, 2 * HALO), :] = jnp.zeros((2 * HALO, LANES), F32)

        def fill(i, _):
            r0 = pl.multiple_of(i * CONV_ROWS, CONV_ROWS)
            hpad[pl.ds(r0 + 2 * HALO, CONV_ROWS), :] = (
                cc_ref[pl.ds(r0, CONV_ROWS), :].astype(F32) * cu_ref[pl.ds(r0, CONV_ROWS), :].astype(F32))
            return 0

        lax.fori_loop(0, nchunk, fill, 0)
        w0, w1, w2 = w_ref[0:1, :], w_ref[1:2, :], w_ref[2:3, :]

        def conv(i, _):
            r0 = pl.multiple_of(i * CONV_ROWS, CONV_ROWS)
            win = hpad[pl.ds(r0 + HALO, CONV_ROWS + HALO), :]
            c = (w2 * win[HALO:] + w1 * pltpu.roll(win, 1, 0)[HALO:] + w0 * pltpu.roll(win, 2, 0)[HALO:])
            y_ref[pl.ds(r0, CONV_ROWS), :] = (cb_ref[pl.ds(r0, CONV_ROWS), :].astype(F32) * c).astype(y_ref.dtype)
            return 0

        lax.fori_loop(0, nchunk, conv, 0)

    def col(off):
        return pl.BlockSpec((s, LANES), lambda j: (0, off + j))

    return pl.pallas_call(
        body, name=name, grid=(nblk,),
        in_specs=[col(first), col(first + nblk), col(first + 2 * nblk), pl.BlockSpec((8, LANES), lambda j: (0, j))],
        out_specs=pl.BlockSpec((s, LANES), lambda j: (0, j)),
        out_shape=jax.ShapeDtypeStruct((s, nblk * LANES), BF16),
        scratch_shapes=[pltpu.VMEM((s + 2 * HALO, LANES), F32)],
        compiler_params=_cparams("parallel"),
    )(proj, proj, proj, conv_w8)


def _conv_bwd(dmix, proj, conv_w8, name):
    s = proj.shape[0]
    nblk = conv_w8.shape[1] // LANES
    first = 3 * nblk
    nchunk = s // CONV_ROWS

    def body(dy_ref, cb_ref, cc_ref, cu_ref, w_ref, dcb_ref, dcc_ref, dcu_ref, dw_ref, hpad, dcpad):
        hpad[pl.ds(0, 2 * HALO), :] = jnp.zeros((2 * HALO, LANES), F32)
        dcpad[pl.ds(s, 2 * HALO), :] = jnp.zeros((2 * HALO, LANES), F32)

        def fill(i, _):
            r0 = pl.multiple_of(i * CONV_ROWS, CONV_ROWS)
            hpad[pl.ds(r0 + 2 * HALO, CONV_ROWS), :] = (
                cc_ref[pl.ds(r0, CONV_ROWS), :].astype(F32) * cu_ref[pl.ds(r0, CONV_ROWS), :].astype(F32))
            return 0

        lax.fori_loop(0, nchunk, fill, 0)
        w0, w1, w2 = w_ref[0:1, :], w_ref[1:2, :], w_ref[2:3, :]

        def fold(v):
            return jnp.sum(v.reshape(CONV_ROWS // 8, 8, LANES), axis=0)

        def first_pass(i, acc):
            a0, a1, a2 = acc
            r0 = pl.multiple_of(i * CONV_ROWS, CONV_ROWS)
            win = hpad[pl.ds(r0 + HALO, CONV_ROWS + HALO), :]
            h0 = win[HALO:]
            h1 = pltpu.roll(win, 1, 0)[HALO:]
            h2 = pltpu.roll(win, 2, 0)[HALO:]
            c = w2 * h0 + w1 * h1 + w0 * h2
            dy = dy_ref[pl.ds(r0, CONV_ROWS), :]
            dcb_ref[pl.ds(r0, CONV_ROWS), :] = (dy * c).astype(dcb_ref.dtype)
            dc = dy * cb_ref[pl.ds(r0, CONV_ROWS), :].astype(F32)
            dcpad[pl.ds(r0, CONV_ROWS), :] = dc
            return a0 + fold(dc * h2), a1 + fold(dc * h1), a2 + fold(dc * h0)

        z8 = jnp.zeros((8, LANES), F32)
        a0, a1, a2 = lax.fori_loop(0, nchunk, first_pass, (z8, z8, z8))
        dw_ref[...] = jnp.concatenate(
            [jnp.sum(a0, axis=0, keepdims=True), jnp.sum(a1, axis=0, keepdims=True),
             jnp.sum(a2, axis=0, keepdims=True), jnp.zeros((5, LANES), F32)], axis=0)

        def second_pass(i, _):
            r0 = pl.multiple_of(i * CONV_ROWS, CONV_ROWS)
            win = dcpad[pl.ds(r0, CONV_ROWS + HALO), :]
            n = CONV_ROWS + HALO
            dh = (w2 * win[:CONV_ROWS] + w1 * pltpu.roll(win, n - 1, 0)[:CONV_ROWS]
                  + w0 * pltpu.roll(win, n - 2, 0)[:CONV_ROWS])
            dcc_ref[pl.ds(r0, CONV_ROWS), :] = (dh * cu_ref[pl.ds(r0, CONV_ROWS), :].astype(F32)).astype(dcc_ref.dtype)
            dcu_ref[pl.ds(r0, CONV_ROWS), :] = (dh * cc_ref[pl.ds(r0, CONV_ROWS), :].astype(F32)).astype(dcu_ref.dtype)
            return 0

        lax.fori_loop(0, nchunk, second_pass, 0)

    def col(off):
        return pl.BlockSpec((s, LANES), lambda j: (0, off + j))

    out = pl.BlockSpec((s, LANES), lambda j: (0, j))
    return pl.pallas_call(
        body, name=name, grid=(nblk,),
        in_specs=[col(nblk), col(first), col(first + nblk), col(first + 2 * nblk),
                  pl.BlockSpec((8, LANES), lambda j: (0, j))],
        out_specs=[out, out, out, pl.BlockSpec((8, LANES), lambda j: (0, j))],
        out_shape=[jax.ShapeDtypeStruct((s, nblk * LANES), BF16)] * 3 + [jax.ShapeDtypeStruct((8, nblk * LANES), F32)],
        scratch_shapes=[pltpu.VMEM((s + 2 * HALO, LANES), F32), pltpu.VMEM((s + 2 * HALO, LANES), F32)],
        compiler_params=_cparams("parallel"),
    )(dmix, proj, proj, proj, conv_w8)


LOG2E = 1.4426950408889634
LN2 = 0.6931471805599453
NEG_BIG = -1e30


def _cumsum_matrix(kind):
    j = lax.broadcasted_iota(jnp.int32, (KEY_CHUNK, 2 * KEY_CHUNK), 0)
    c = lax.broadcasted_iota(jnp.int32, (KEY_CHUNK, 2 * KEY_CHUNK), 1)
    tri = {"after": j > c, "upto": j <= c, "before": j < c}[kind]
    return jnp.where((c >= KEY_CHUNK) | tri, 1.0, 0.0).astype(BF16)


def _stack_heads(t, m0):
    zero = jnp.zeros_like(t)
    return jnp.concatenate([jnp.where(m0, t, zero), jnp.where(m0, zero, t)], axis=0)


def _softplus2(z):
    sp = jnp.maximum(z, 0.0) + jnp.log2(1.0 + jnp.exp2(-jnp.abs(z)))
    return sp, z - sp


def _key_chunk(ref, kc):
    return ref[pl.ds(pl.multiple_of(kc * KEY_CHUNK, KEY_CHUNK), KEY_CHUNK), :]


def _attn_bwd(qk, proj, dmix, rtot, tq, name):
    s = qk.shape[0]
    nhp = qk.shape[1] // (2 * LANES)
    nc = tq // KEY_CHUNK

    def body(q_ref, k_ref, v_ref, do_ref, r_ref, cmi_ref, cme_ref, bias_ref, dq_ref, dk_ref, dv_ref,
             z_refs, ls_refs, sig_refs, sp_refs, gb_refs, pr_ref, gs_ref):
        qi = pl.program_id(1)

        @pl.when(qi == 0)
        def _():
            dk_ref[...] = jnp.zeros_like(dk_ref)
            dv_ref[...] = jnp.zeros_like(dv_ref)

        nslots = (qi + 1) * nc
        m0 = lax.broadcasted_iota(jnp.int32, (1, LANES), 1) < HEAD_DIM
        qs = _stack_heads(q_ref[...], m0)
        do = do_ref[...]
        dos = _stack_heads(do.astype(BF16), m0)
        dosl = _stack_heads((do * LN2).astype(BF16), m0)
        cmi = cmi_ref[...]
        cme = cme_ref[...]

        def chunk_at(i):
            return jnp.clip(i, 0, nslots - 1)

        def scores(kc):
            return _dot(qs, _key_chunk(k_ref, kc), NT)

        def weights(ls, cs, da, pr, kc):
            a = jnp.exp2(ls - (pr - cs[:, :KEY_CHUNK]))
            gb = (a * da).astype(BF16)
            ks = pl.multiple_of(kc * KEY_CHUNK, KEY_CHUNK)
            dv_ref[pl.ds(ks, KEY_CHUNK), :] += _dot(a, dos, TN)
            return gb, jnp.exp2(ls), pr - cs[:, KEY_CHUNK:]

        def score_grads(gb, sig, cg, gs, dq, kc):
            dzb = (gb.astype(F32) * (1.0 - sig) - sig * (gs + cg[:, :KEY_CHUNK])).astype(BF16)
            ks = pl.multiple_of(kc * KEY_CHUNK, KEY_CHUNK)
            dk_ref[pl.ds(ks, KEY_CHUNK), :] += _dot(dzb, qs, TN)
            dq = dq + _dot(jnp.concatenate([dzb[:tq], dzb[tq:]], axis=1), _stack_heads(_key_chunk(k_ref, kc), m0), NN)
            return gs + cg[:, KEY_CHUNK:], dq

        def step(i, par, bias=None):
            cur, prv = par, 1 - par
            k1, k2 = chunk_at(i - 1), chunk_at(i - 2)
            z_next = scores(chunk_at(i + 1))
            cs = _dot(sp_refs[prv][...], cmi, NN)
            da = _dot(dosl, _key_chunk(v_ref, k1), NT)
            cg = _dot(gb_refs[cur][...], cme, NN)
            z = z_refs[cur][...]
            if bias is not None:
                z = z + bias
            sp, ls = _softplus2(z)
            sp_refs[cur][...] = sp.astype(BF16)
            ls_refs[cur][...] = ls
            gs, dq = score_grads(gb_refs[cur][...], sig_refs[cur][...], cg, gs_ref[...], dq_ref[...], k2)
            gs_ref[...] = gs
            dq_ref[...] = dq
            gb, sig, pr = weights(ls_refs[prv][...], cs, da, pr_ref[...], k1)
            gb_refs[prv][...] = gb
            sig_refs[prv][...] = sig
            pr_ref[...] = pr
            z_refs[prv][...] = z_next

        pr_ref[...] = jnp.concatenate([r_ref[:, :LANES], r_ref[:, LANES:]], axis=0)
        gs_ref[...] = jnp.zeros((2 * tq, LANES), F32)
        dq_ref[...] = jnp.zeros((tq, LANES), F32)
        z_refs[0][...] = scores(chunk_at(0))
        sp_refs[1][...] = jnp.zeros((2 * tq, LANES), BF16)
        ls_refs[1][...] = jnp.full((2 * tq, LANES), NEG_BIG, F32)
        gb_refs[0][...] = jnp.zeros((2 * tq, LANES), BF16)
        sig_refs[0][...] = jnp.zeros((2 * tq, LANES), F32)

        def two_steps(j, _):
            step(2 * j, 0)
            step(2 * j + 1, 1)
            return 0

        lax.fori_loop(0, nslots // 2 - 1, two_steps, 0)
        step(nslots - 2, 0, bias_ref[0])
        step(nslots - 1, 1, bias_ref[1])
        k1, k2 = chunk_at(nslots - 1), chunk_at(nslots - 2)
        gb, sig, _ = weights(ls_refs[1][...], _dot(sp_refs[1][...], cmi, NN),
                             _dot(dosl, _key_chunk(v_ref, k1), NT), pr_ref[...], k1)
        gb2 = gb_refs[0][...]
        gs, dq = score_grads(gb2, sig_refs[0][...], _dot(gb2, cme, NN), gs_ref[...], dq_ref[...], k2)
        _, dq = score_grads(gb, sig, _dot(gb, cme, NN), gs, dq, k1)
        dq_ref[...] = dq

    def wrapped(q_ref, k_ref, v_ref, do_ref, r_ref, cmi_ref, cme_ref, bias_ref, dq_ref, dk_ref, dv_ref,
                z0, z1, ls0, ls1, sg0, sg1, sp0, sp1, gb0, gb1, pr_ref, gs_ref):
        body(q_ref, k_ref, v_ref, do_ref, r_ref, cmi_ref, cme_ref, bias_ref, dq_ref, dk_ref, dv_ref,
             (z0, z1), (ls0, ls1), (sg0, sg1), (sp0, sp1), (gb0, gb1), pr_ref, gs_ref)

    assert nc == 2
    bias = _diag_bias(tq, True)
    bias = jnp.concatenate([bias[:, :, :KEY_CHUNK], bias[:, :, KEY_CHUNK:]], axis=1)
    qblk = pl.BlockSpec((tq, LANES), lambda p, i: (i, p))
    full = pl.BlockSpec((s, LANES), lambda p, i: (0, p))
    cmspec = pl.BlockSpec((KEY_CHUNK, 2 * KEY_CHUNK), lambda p, i: (0, 0))
    shape = jax.ShapeDtypeStruct((s, nhp * LANES), F32)
    f32buf = pltpu.VMEM((2 * tq, LANES), F32)
    bf16buf = pltpu.VMEM((2 * tq, LANES), BF16)
    return pl.pallas_call(
        wrapped, name=name, grid=(nhp, s // tq),
        in_specs=[qblk,
                  pl.BlockSpec((s, LANES), lambda p, i: (0, nhp + p)),
                  pl.BlockSpec((s, LANES), lambda p, i: (0, 2 * nhp + p)),
                  qblk,
                  pl.BlockSpec((tq, 2 * LANES), lambda p, i: (i, p)),
                  cmspec, cmspec,
                  pl.BlockSpec((nc, 2 * tq, LANES), lambda p, i: (0, 0, 0))],
        out_specs=[qblk, full, full],
        out_shape=[shape, shape, shape],
        scratch_shapes=[f32buf] * 6 + [bf16buf] * 4 + [f32buf] * 2,
        compiler_params=_cparams("parallel", "arbitrary"),
    )(qk, qk, proj, dmix, rtot, _cumsum_matrix("upto"), _cumsum_matrix("before"), bias)


def _pair_cumsum_matrix(kind):
    j = lax.broadcasted_iota(jnp.int32, (2 * KEY_CHUNK, 4 * KEY_CHUNK), 0)
    c = lax.broadcasted_iota(jnp.int32, (2 * KEY_CHUNK, 4 * KEY_CHUNK), 1)
    same_head = (j // KEY_CHUNK) == ((c // KEY_CHUNK) % 2)
    jj, cc = j % KEY_CHUNK, c % KEY_CHUNK
    tri = {"after": jj > cc, "upto": jj <= cc, "before": jj < cc}[kind]
    return jnp.where(same_head & ((c >= 2 * KEY_CHUNK) | tri), 1.0, 0.0).astype(BF16)


def _diag_bias(tq, ascending):
    nc = tq // KEY_CHUNK
    shape = (nc, tq, 2 * KEY_CHUNK)
    d = lax.broadcasted_iota(jnp.int32, shape, 0)
    r = lax.broadcasted_iota(jnp.int32, shape, 1)
    c = lax.broadcasted_iota(jnp.int32, shape, 2) % KEY_CHUNK
    chunk = d if ascending else nc - 1 - d
    return jnp.where(chunk * KEY_CHUNK + c < r, 0.0, NEG_BIG).astype(F32)


def _attn_fwd(qk, proj, tq, name):
    s = qk.shape[0]
    nhp = qk.shape[1] // (2 * LANES)
    nc = tq // KEY_CHUNK
    assert nc == 2
    w = 2 * KEY_CHUNK

    def body(q_ref, k_ref, v_ref, cm_ref, bias_ref, o_ref, r_ref, z_refs, ls_refs, cs_refs, ct_refs, sp_refs,
             ab_refs, acc_ref):
        qi = pl.program_id(1)
        nslots = (qi + 1) * nc
        m0 = lax.broadcasted_iota(jnp.int32, (1, LANES), 1) < HEAD_DIM
        q = q_ref[...]
        cm = cm_ref[...]

        def chunk_at(i):
            return jnp.clip(nslots - 1 - i, 0, nslots - 1)

        def scores(kc):
            return _dot(q, _stack_heads(_key_chunk(k_ref, kc), m0), NT)

        def values(ab, kc):
            return _dot(ab, _stack_heads(_key_chunk(v_ref, kc), m0), NN)

        def step(i, par, bias=None, stages="zscwv"):
            cur, prv = par, 1 - par
            if "z" in stages:
                z_next = scores(chunk_at(i + 1))
            if "c" in stages:
                cs = _dot(sp_refs[prv][...], cm, NN)
            if "v" in stages:
                pv = values(ab_refs[prv][...], chunk_at(i - 3))
            if "w" in stages:
                rs = r_ref[...]
                r_ref[...] = rs + ct_refs[cur][...]
                ab_refs[cur][...] = jnp.exp2(ls_refs[cur][...] - cs_refs[cur][...] - rs).astype(BF16)
            if "s" in stages:
                z = z_refs[cur][...]
                if bias is not None:
                    z = z + bias
                sp, ls = _softplus2(z)
                sp_refs[cur][...] = sp.astype(BF16)
                ls_refs[cur][...] = ls
            if "v" in stages:
                acc_ref[...] += pv
            if "c" in stages:
                cs_refs[prv][...] = cs[:, :w]
                ct_refs[prv][...] = cs[:, w:]
            if "z" in stages:
                z_refs[prv][...] = z_next

        z_refs[0][...] = scores(chunk_at(0))
        for p in range(2):
            sp_refs[p][...] = jnp.zeros((tq, w), BF16)
            ls_refs[p][...] = jnp.full((tq, w), NEG_BIG, F32)
            cs_refs[p][...] = jnp.zeros((tq, w), F32)
            ct_refs[p][...] = jnp.zeros((tq, w), F32)
            ab_refs[p][...] = jnp.zeros((tq, w), BF16)
        r_ref[...] = jnp.zeros((tq, w), F32)
        acc_ref[...] = jnp.zeros((tq, LANES), F32)
        step(0, 0, bias_ref[0])
        step(1, 1, bias_ref[1])

        def two_steps(j, _):
            step(2 * j, 0)
            step(2 * j + 1, 1)
            return 0

        lax.fori_loop(1, nslots // 2, two_steps, 0)
        step(nslots, 0, stages="cwv")
        step(nslots + 1, 1, stages="wv")
        step(nslots + 2, 0, stages="v")
        o_ref[...] = acc_ref[...].astype(o_ref.dtype)

    def wrapped(q_ref, k_ref, v_ref, cm_ref, bias_ref, o_ref, r_ref, *scratch):
        z, ls, cs, ct, sp, ab = [scratch[2 * j:2 * j + 2] for j in range(6)]
        body(q_ref, k_ref, v_ref, cm_ref, bias_ref, o_ref, r_ref, z, ls, cs, ct, sp, ab, scratch[12])

    f32buf = pltpu.VMEM((tq, w), F32)
    bf16buf = pltpu.VMEM((tq, w), BF16)
    return pl.pallas_call(
        wrapped, name=name, grid=(nhp, s // tq),
        in_specs=[pl.BlockSpec((tq, LANES), lambda p, i: (i, p)),
                  pl.BlockSpec((s, LANES), lambda p, i: (0, nhp + p)),
                  pl.BlockSpec((s, LANES), lambda p, i: (0, 2 * nhp + p)),
                  pl.BlockSpec((w, 2 * w), lambda p, i: (0, 0)),
                  pl.BlockSpec((nc, tq, w), lambda p, i: (0, 0, 0))],
        out_specs=[pl.BlockSpec((tq, LANES), lambda p, i: (i, p)),
                   pl.BlockSpec((tq, w), lambda p, i: (i, p))],
        out_shape=[jax.ShapeDtypeStruct((s, nhp * LANES), BF16),
                   jax.ShapeDtypeStruct((s, nhp * w), F32)],
        scratch_shapes=[f32buf] * 8 + [bf16buf] * 4 + [pltpu.VMEM((tq, LANES), F32)],
        compiler_params=_cparams("parallel", "parallel"),
    )(qk, qk, proj, _pair_cumsum_matrix("after"), _diag_bias(tq, False))


def _mm_blocks(h, ga, widx, tm, name):
    s, d = h.shape
    nb, cols = ga.shape[1], ga.shape[3]

    def body(a_ref, b_ref, o_ref):
        o_ref[...] = _dot(a_ref[...], b_ref[...], NN).astype(o_ref.dtype)

    return pl.pallas_call(
        body, name=name, grid=(s // tm, nb),
        in_specs=[pl.BlockSpec((tm, d), lambda i, j: (i, 0)),
                  pl.BlockSpec((None, None, d, cols), lambda i, j: (widx, j, 0, 0))],
        out_specs=pl.BlockSpec((tm, cols), lambda i, j: (i, j)),
        out_shape=jax.ShapeDtypeStruct((s, nb * cols), BF16),
        compiler_params=_cparams("parallel", "arbitrary"),
    )(h, ga)


def _mm_swiglu(h, ga, gidx, uidx, tm, name):
    s, d = h.shape
    nb, cols = ga.shape[1], ga.shape[3]

    def body(a_ref, bg_ref, bu_ref, g_ref, u_ref, act_ref):
        a = a_ref[...]
        g = _dot(a, bg_ref[...], NN)
        u = _dot(a, bu_ref[...], NN)
        g_ref[...] = g.astype(g_ref.dtype)
        u_ref[...] = u.astype(u_ref.dtype)
        act_ref[...] = (g * (1.0 / (1.0 + jnp.exp(-g))) * u).astype(act_ref.dtype)

    def wspec(idx):
        return pl.BlockSpec((None, None, d, cols), lambda i, j: (idx, j, 0, 0))

    out = pl.BlockSpec((tm, cols), lambda i, j: (i, j))
    shape = jax.ShapeDtypeStruct((s, nb * cols), BF16)
    return pl.pallas_call(
        body, name=name, grid=(s // tm, nb),
        in_specs=[pl.BlockSpec((tm, d), lambda i, j: (i, 0)), wspec(gidx), wspec(uidx)],
        out_specs=[out, out, out], out_shape=[shape, shape, shape],
        compiler_params=_cparams("parallel", "arbitrary"),
    )(h, ga, ga)


def _mm_residual(a, w3, lidx, res, tm, tn, name):
    s, k = a.shape
    n = w3.shape[2]

    def body(a_ref, b_ref, r_ref, o_ref):
        o_ref[...] = r_ref[...] + _dot(a_ref[...], b_ref[...], NN)

    return pl.pallas_call(
        body, name=name, grid=(s // tm, n // tn),
        in_specs=[pl.BlockSpec((tm, k), lambda i, j: (i, 0)),
                  pl.BlockSpec((None, k, tn), lambda i, j: (lidx, 0, j)),
                  pl.BlockSpec((tm, tn), lambda i, j: (i, j))],
        out_specs=pl.BlockSpec((tm, tn), lambda i, j: (i, j)),
        out_shape=jax.ShapeDtypeStruct((s, n), F32),
        compiler_params=_cparams("parallel", "arbitrary"),
    )(a, w3, res)


def _mm_nt(a, w3, lidx, tm, tn, name):
    s, k = a.shape
    n = w3.shape[1]

    def body(a_ref, b_ref, o_ref):
        o_ref[...] = _dot(a_ref[...], b_ref[...], NT)

    return pl.pallas_call(
        body, name=name, grid=(s // tm, n // tn),
        in_specs=[pl.BlockSpec((tm, k), lambda i, j: (i, 0)),
                  pl.BlockSpec((None, tn, k), lambda i, j: (lidx, j, 0))],
        out_specs=pl.BlockSpec((tm, tn), lambda i, j: (i, j)),
        out_shape=jax.ShapeDtypeStruct((s, n), F32),
        compiler_params=_cparams("parallel", "arbitrary"),
    )(a, w3)


def _mm_nt_swiglu_bwd(dx, wd3, lidx, g, u, tm, name):
    s, d = dx.shape
    cols = g.shape[1] // N_DEV

    def body(a_ref, b_ref, g_ref, u_ref, dg_ref, du_ref):
        dact = _dot(a_ref[...], b_ref[...], NT)
        gv = g_ref[...].astype(F32)
        sig = 1.0 / (1.0 + jnp.exp(-gv))
        du_ref[...] = (dact * (gv * sig)).astype(du_ref.dtype)
        dg_ref[...] = (dact * u_ref[...].astype(F32) * (sig * (1.0 + gv * (1.0 - sig)))).astype(dg_ref.dtype)

    blk = pl.BlockSpec((tm, cols), lambda i, j: (i, j))
    shape = jax.ShapeDtypeStruct(g.shape, BF16)
    return pl.pallas_call(
        body, name=name, grid=(s // tm, N_DEV),
        in_specs=[pl.BlockSpec((tm, d), lambda i, j: (i, 0)),
                  pl.BlockSpec((None, cols, d), lambda i, j: (lidx, j, 0)), blk, blk],
        out_specs=[blk, blk], out_shape=[shape, shape],
        compiler_params=_cparams("parallel", "arbitrary"),
    )(dx, wd3, g, u)


def _mm_nt_blocks(das, ga, widxs, tm, name):
    s = das[0].shape[0]
    nb, d, cols = ga.shape[1], ga.shape[2], ga.shape[3]
    nw = len(das)

    def body(*refs):
        a_refs, b_refs, o_ref = refs[:nw], refs[nw:2 * nw], refs[2 * nw]
        k = pl.program_id(1)
        part = _dot(a_refs[0][...], b_refs[0][...], NT)
        for w in range(1, nw):
            part = part + _dot(a_refs[w][...], b_refs[w][...], NT)

        @pl.when(k == 0)
        def _():
            o_ref[...] = part

        @pl.when(k > 0)
        def _():
            o_ref[...] += part

    def wspec(idx):
        return pl.BlockSpec((None, None, d, cols), lambda i, k: (idx, k, 0, 0))

    return pl.pallas_call(
        body, name=name, grid=(s // tm, nb),
        in_specs=[pl.BlockSpec((tm, cols), lambda i, k: (i, k))] * nw + [wspec(i) for i in widxs],
        out_specs=pl.BlockSpec((tm, d), lambda i, k: (i, 0)),
        out_shape=jax.ShapeDtypeStruct((s, d), F32),
        compiler_params=_cparams("parallel", "arbitrary"),
    )(*das, *([ga] * nw))


def _mm_tn(a, b, ta, tb, tk, out_blocks, name):
    s, ka = a.shape
    nb = b.shape[1]
    nk = s // tk

    def body(a_ref, b_ref, o_ref, ob_ref):
        k = pl.program_id(2)
        part = _dot(a_ref[...], b_ref[...], TN)

        @pl.when(k == 0)
        def _():
            o_ref[...] = part

        @pl.when(k > 0)
        def _():
            o_ref[...] += part

        @pl.when(k == nk - 1)
        def _():
            ob_ref[...] = o_ref[...].astype(ob_ref.dtype)

    if out_blocks:
        out_spec = pl.BlockSpec((None, ta, tb), lambda i, j, k: (j, i, 0))
        shape = (nb // tb, ka, tb)
    else:
        out_spec = pl.BlockSpec((ta, tb), lambda i, j, k: (i, j))
        shape = (ka, nb)
    return pl.pallas_call(
        body, name=name, grid=(ka // ta, nb // tb, nk),
        in_specs=[pl.BlockSpec((tk, ta), lambda i, j, k: (k, i)),
                  pl.BlockSpec((tk, tb), lambda i, j, k: (k, j))],
        out_specs=[out_spec, out_spec],
        out_shape=[jax.ShapeDtypeStruct(shape, F32), jax.ShapeDtypeStruct(shape, BF16)],
        compiler_params=_cparams("parallel", "parallel", "arbitrary"),
    )(a, b)


def _loss_head(y, target, tm, name):
    s, d = y.shape
    nsteps = s // tm

    def body(y_ref, t_ref, dy_ref, l_ref, acc):
        i = pl.program_id(0)
        diff = y_ref[...] - t_ref[...]
        dy_ref[...] = diff * (1.0 / d)
        part = jnp.sum((diff * diff).reshape(tm // 8, 8, d), axis=0)

        @pl.when(i == 0)
        def _():
            acc[...] = part

        @pl.when(i > 0)
        def _():
            acc[...] += part

        @pl.when(i == nsteps - 1)
        def _():
            tot = jnp.sum(jnp.sum(acc[...], axis=1, keepdims=True), axis=0, keepdims=True)
            l_ref[...] = jnp.broadcast_to(tot * (0.5 / d), (8, LANES))

    row = pl.BlockSpec((tm, d), lambda i: (i, 0))
    return pl.pallas_call(
        body, name=name, grid=(nsteps,),
        in_specs=[row, row],
        out_specs=[row, pl.BlockSpec((8, LANES), lambda i: (0, 0))],
        out_shape=[jax.ShapeDtypeStruct((s, d), F32), jax.ShapeDtypeStruct((8, LANES), F32)],
        scratch_shapes=[pltpu.VMEM((8, d), F32)],
        compiler_params=_cparams("arbitrary"),
    )(y, target)


def _adamw(parts, own, w, m, v, tr, name):
    p, rows, cols = parts.shape
    c1 = 1.0 / (1.0 - ADAM_B1 ** ADAM_STEP)
    c2 = 1.0 / (1.0 - ADAM_B2 ** ADAM_STEP)

    def body(*refs):
        if own is None:
            p_ref, w_ref, m_ref, v_ref, g_ref, d_ref, nm_ref, nv_ref = refs
            g = p_ref[0]
            for k in range(1, p):
                g = g + p_ref[k]
        else:
            p_ref, own_ref, w_ref, m_ref, v_ref, g_ref, d_ref, nm_ref, nv_ref = refs
            x, y, c = _place()
            my = 4 * x + 2 * y + c
            mine = own_ref[...]
            g = jnp.where(my == 0, mine, p_ref[0].astype(F32))
            for k in range(1, p):
                g = g + jnp.where(my == k, mine, p_ref[k].astype(F32))
        nm = ADAM_B1 * m_ref[...] + (1.0 - ADAM_B1) * g
        nv = ADAM_B2 * v_ref[...] + (1.0 - ADAM_B2) * (g * g)
        g_ref[...] = g
        nm_ref[...] = nm
        nv_ref[...] = nv
        d_ref[...] = -ADAM_LR * ((nm * c1) / (jnp.sqrt(nv * c2) + ADAM_EPS) + ADAM_WD * w_ref[...])

    blk = pl.BlockSpec((tr, cols), lambda i: (i, 0))
    shape = jax.ShapeDtypeStruct((rows, cols), F32)
    return pl.pallas_call(
        body, name=name, grid=(rows // tr,),
        in_specs=[pl.BlockSpec((p, tr, cols), lambda i: (0, i, 0))] + [blk] * (3 if own is None else 4),
        out_specs=[blk] * 4, out_shape=[shape] * 4,
        compiler_params=_cparams("parallel"),
    )(*([parts] + ([] if own is None else [own]) + [w, m, v]))


def _place():
    x, y, c = lax.axis_index("x"), lax.axis_index("y"), lax.axis_index("c")
    return x, y, c


def _all_gather(shards, name):
    na = len(shards)

    def body(*refs):
        srcs, dsts = refs[:na], refs[na:2 * na]
        send_sems, recv_sems, local_sems = refs[2 * na:]
        x, y, c = _place()
        me, sibling = (x, y, c), (x, y, 1 - c)
        chips = [(1 - x, y), (x, 1 - y), (1 - x, 1 - y)]

        def slot(a, dev):
            return dsts[a].at[:, pl.ds(4 * dev[0] + 2 * dev[1] + dev[2], 1)]

        def copy(k, a, block, to, from_shard=False):
            return pltpu.make_async_remote_copy(
                src_ref=srcs[a] if from_shard else slot(a, block), dst_ref=slot(a, block),
                send_sem=send_sems.at[k, a], recv_sem=recv_sems.at[k, a], device_id=to, device_id_type=MESH)

        mine = [pltpu.make_async_copy(srcs[a], slot(a, me), local_sems.at[a]) for a in range(na)]
        for cp in mine:
            cp.start()
        first = [copy(0, a, me, sibling, True) for a in range(na)]
        first += [copy(1 + j, a, me, (*chip, c), True) for j, chip in enumerate(chips) for a in range(na)]
        for cp in first:
            cp.start()
        passed = []
        for j, chip in enumerate(chips):
            for a in range(na):
                copy(1 + j, a, (*chip, c), me).wait_recv()
                fwd = copy(4 + j, a, (*chip, c), sibling)
                fwd.start()
                passed.append(fwd)
        for a in range(na):
            copy(0, a, sibling, me).wait_recv()
        for j, chip in enumerate(chips):
            for a in range(na):
                copy(4 + j, a, (*chip, 1 - c), me).wait_recv()
        for cp in first + passed:
            cp.wait_send()
        for cp in mine:
            cp.wait()

    anyspec = pl.BlockSpec(memory_space=pl.ANY)
    return pl.pallas_call(
        body, name=name,
        in_specs=[anyspec] * na, out_specs=[anyspec] * na,
        out_shape=[jax.ShapeDtypeStruct((a.shape[0], N_DEV) + a.shape[2:], a.dtype) for a in shards],
        scratch_shapes=[pltpu.SemaphoreType.DMA((7, na)), pltpu.SemaphoreType.DMA((7, na)),
                        pltpu.SemaphoreType.DMA((na,))],
    )(*shards)


_RELATIONS = [(dx, dy, dc) for dx in (0, 1) for dy in (0, 1) for dc in (0, 1)][1:]


def _flip(v, d):
    return 1 - v if d else v


def _exchange_blocks(grads, name):
    na = len(grads)

    def body(*refs):
        srcs, dsts = refs[:na], refs[na:2 * na]
        send_sems, recv_sems, local_sems = refs[2 * na:]
        x, y, c = _place()
        my = 4 * x + 2 * y + c
        mine = [pltpu.make_async_copy(srcs[a].at[pl.ds(my, 1)], dsts[a].at[pl.ds(my, 1)], local_sems.at[a])
                for a in range(na)]
        for cp in mine:
            cp.start()
        sends = []
        for k, (dx, dy, dc) in enumerate(_RELATIONS):
            peer = (_flip(x, dx), _flip(y, dy), _flip(c, dc))
            pidx = 4 * peer[0] + 2 * peer[1] + peer[2]
            for a in range(na):
                cp = pltpu.make_async_remote_copy(
                    src_ref=srcs[a].at[pl.ds(pidx, 1)], dst_ref=dsts[a].at[pl.ds(my, 1)],
                    send_sem=send_sems.at[k, a], recv_sem=recv_sems.at[k, a], device_id=peer, device_id_type=MESH)
                cp.start()
                sends.append((cp, pidx, k, a, peer))
        for cp, pidx, k, a, peer in sends:
            pltpu.make_async_remote_copy(
                src_ref=srcs[a].at[pl.ds(pidx, 1)], dst_ref=dsts[a].at[pl.ds(pidx, 1)],
                send_sem=send_sems.at[k, a], recv_sem=recv_sems.at[k, a], device_id=peer,
                device_id_type=MESH).wait_recv()
        for cp, *_ in sends:
            cp.wait_send()
        for cp in mine:
            cp.wait()

    anyspec = pl.BlockSpec(memory_space=pl.ANY)
    return pl.pallas_call(
        body, name=name,
        in_specs=[anyspec] * na, out_specs=[anyspec] * na,
        out_shape=[jax.ShapeDtypeStruct(a.shape, a.dtype) for a in grads],
        scratch_shapes=[pltpu.SemaphoreType.DMA((7, na)), pltpu.SemaphoreType.DMA((7, na)),
                        pltpu.SemaphoreType.DMA((na,))],
    )(*grads)


def _all_reduce_small(v, name):
    r, c_ = v.shape

    def body(v_ref, o_ref, gath, send_sems, recv_sems):
        x, y, c = _place()
        my = 4 * x + 2 * y + c
        gath[my] = v_ref[...]
        sends = []
        for k, (dx, dy, dc) in enumerate(_RELATIONS):
            peer = (_flip(x, dx), _flip(y, dy), _flip(c, dc))
            cp = pltpu.make_async_remote_copy(
                src_ref=v_ref, dst_ref=gath.at[my], send_sem=send_sems.at[k], recv_sem=recv_sems.at[k],
                device_id=peer, device_id_type=MESH)
            cp.start()
            sends.append((cp, 4 * peer[0] + 2 * peer[1] + peer[2], k, peer))
        for cp, pidx, k, peer in sends:
            pltpu.make_async_remote_copy(
                src_ref=v_ref, dst_ref=gath.at[pidx], send_sem=send_sems.at[k], recv_sem=recv_sems.at[k],
                device_id=peer, device_id_type=MESH).wait_recv()
        for cp, *_ in sends:
            cp.wait_send()
        tot = gath[0]
        for k in range(1, N_DEV):
            tot = tot + gath[k]
        o_ref[...] = tot

    vm = pl.BlockSpec(memory_space=pltpu.VMEM)
    return pl.pallas_call(
        body, name=name, in_specs=[vm], out_specs=vm,
        out_shape=jax.ShapeDtypeStruct((r, c_), F32),
        scratch_shapes=[pltpu.VMEM((N_DEV, r, c_), F32), pltpu.SemaphoreType.DMA((7,)),
                        pltpu.SemaphoreType.DMA((7,))],
    )(v)


TM = 512
TM_MATMUL = 2048
TM_RESIDUAL = 1024
TQ = 256


def _pad_to(a, axis, size):
    pad = [(0, 0)] * a.ndim
    pad[axis] = (0, size - a.shape[axis])
    return jnp.pad(a, pad)


def _local_step(x, target, ga, gb, gc, conv_full, norm_mix, q_norm, k_norm, norm_ffn):
    depth = gb.shape[0]
    tm, tq = min(TM, x.shape[0]), min(TQ, x.shape[0])
    tmm, tmr = min(TM_MATMUL, x.shape[0]), min(TM_RESIDUAL, x.shape[0])
    attn = gb.shape[1] // 2
    nheads = attn // HEAD_DIM
    scale = HEAD_DIM ** -0.5 * LOG2E
    saved = []
    for l in range(depth):
        h1 = _rmsnorm_fwd(x, norm_mix[l][None], tm,f"norm_mix_fwd_{l}")
        proj = _mm_blocks(h1, ga, 3 * l, tmm, f"proj_in_{l}")
        qk_gain = jnp.concatenate([jnp.tile(q_norm[l], nheads) * scale, jnp.tile(k_norm[l], nheads)])[None]
        qk = _qknorm_fwd(proj, qk_gain, tmm, f"qknorm_fwd_{l}")
        o, rtot = _attn_fwd(qk, proj, tq,f"attn_fwd_{l}")
        conv_w8 = _pad_to(conv_full[l], 0, 8)
        cv = _conv_fwd(proj, conv_w8, f"conv_fwd_{l}")
        mix = jnp.concatenate([o, cv], axis=1)
        x1 = _mm_residual(mix, gb, l, x, tmr, 512, f"proj_out_{l}")
        h2 = _rmsnorm_fwd(x1, norm_ffn[l][None], tm,f"norm_ffn_fwd_{l}")
        g, u, act = _mm_swiglu(h2, ga, 3 * l + 1, 3 * l + 2, tmm, f"ffn_up_{l}")
        x2 = _mm_residual(act, gc, l, x1, tmr, 512, f"ffn_down_{l}")
        saved.append((x, h1, proj, qk_gain, qk, rtot, conv_w8, mix, x1, h2, g, u, act))
        x = x2

    dx, loss = _loss_head(x, target, tm,"loss_head")

    grads = [None] * depth
    small = [None] * depth
    for l in reversed(range(depth)):
        x0, h1, proj, qk_gain, qk, rtot, conv_w8, mix, x1, h2, g, u, act = saved[l]
        d = x0.shape[1]
        dg, du = _mm_nt_swiglu_bwd(dx, gc, l, g, u, tmm, f"ffn_down_bwd_{l}")
        d_wdown = _mm_tn(act, dx, 768, d, tmm, False, f"dw_down_{l}")
        d_wgate = _mm_tn(h2, dg, d, ga.shape[3], tmm, True, f"dw_gate_{l}")
        d_wup = _mm_tn(h2, du, d, ga.shape[3], tmm, True, f"dw_up_{l}")
        dh2 = _mm_nt_blocks([dg, du], ga, [3 * l + 1, 3 * l + 2], tmm, f"ffn_up_bwd_{l}")
        dx1, dg_ffn = _rmsnorm_bwd(dh2, x1, norm_ffn[l][None], dx, tm,f"norm_ffn_bwd_{l}")
        dmix = _mm_nt(dx1, gb, l, tmr, 512, f"proj_out_bwd_{l}")
        d_wout = _mm_tn(mix, dx1, 512, d, tmm, False, f"dw_out_{l}")
        dcb, dcc, dcu, dconv = _conv_bwd(dmix, proj, conv_w8, f"conv_bwd_{l}")
        dq, dk, dv = _attn_bwd(qk, proj, dmix, rtot, tq,f"attn_bwd_{l}")
        dqk, dg_qk = _qknorm_bwd(jnp.concatenate([dq, dk], axis=1), proj, qk_gain, tmm, f"qknorm_bwd_{l}")
        dproj = jnp.concatenate([dqk, dv.astype(BF16), dcb, dcc, dcu], axis=1)
        d_win = _mm_tn(h1, dproj, d, ga.shape[3], tmm, True, f"dw_in_{l}")
        dh1 = _mm_nt_blocks([dproj], ga, [3 * l], tmm, f"proj_in_bwd_{l}")
        dx, dg_mix = _rmsnorm_bwd(dh1, x0, norm_mix[l][None], dx1, tm,f"norm_mix_bwd_{l}")
        grads[l] = (d_win, d_wgate, d_wup, d_wout, d_wdown)
        dq_gain = jnp.sum(dg_qk[0, :attn].reshape(nheads, HEAD_DIM), axis=0) * scale
        dk_gain = jnp.sum(dg_qk[0, attn:].reshape(nheads, HEAD_DIM), axis=0)
        small[l] = (dg_mix[0], dg_ffn[0], dq_gain, dk_gain, dconv[:3])
    return loss, dx, grads, small


def kernel(x, norm_mix, w_in, q_norm, k_norm, conv_w, w_out, norm_ffn, w_gate, w_up, w_down, loss_target, m_norm_mix, m_w_in, m_q_norm, m_k_norm, m_conv_w, m_w_out, m_norm_ffn, m_w_gate, m_w_up, m_w_down, v_norm_mix, v_w_in, v_q_norm, v_k_norm, v_conv_w, v_w_out, v_norm_ffn, v_w_gate, v_w_up, v_w_down):
    depth, d, in_shard = w_in.shape
    ff_shard = w_gate.shape[2]
    ff_pad = in_shard
    conv_shard = conv_w.shape[2]
    xs = x.reshape(x.shape[-2], d)
    target = loss_target.reshape(xs.shape)

    pa = jnp.stack([w_in, _pad_to(w_gate, 2, ff_pad), _pad_to(w_up, 2, ff_pad)], axis=1)
    pa = pa.reshape(3 * depth, 1, d, in_shard).astype(BF16)
    pb = w_out.astype(BF16)[:, None]
    pc = _pad_to(w_down, 1, ff_pad).astype(BF16)[:, None]
    pd = _pad_to(_pad_to(conv_w.reshape(depth * 3, conv_shard), 0, 8), 1, LANES)[None, None]
    ga, gb, gc, gd = _all_gather([pa, pb, pc, pd], "gather_weights")
    gb = gb.reshape(depth, N_DEV * gb.shape[2], d)
    gc = gc.reshape(depth, N_DEV * ff_pad, d)
    conv_full = gd[0, :, :depth * 3, :conv_shard].transpose(1, 0, 2).reshape(depth, 3, N_DEV * conv_shard)

    loss, grad_x, grads, small = _local_step(xs, target, ga, gb, gc, conv_full, norm_mix, q_norm, k_norm, norm_ffn)

    x_, y_, c_ = _place()
    my = 4 * x_ + 2 * y_ + c_
    send, own = [], []
    for l in range(depth):
        for (g32, g16), rows_ in zip(grads[l], (d, d, d, w_out.shape[1], ff_pad)):
            send.append(g16.reshape(N_DEV, rows_, -1))
            own.append(lax.dynamic_index_in_dim(g32.reshape(N_DEV, rows_, -1), my, 0, keepdims=False))
    landed = _exchange_blocks(send, "exchange_grads")

    nconv = N_DEV * conv_shard
    rows = []
    for l in range(depth):
        g_mix, g_ffn, g_q, g_k, g_conv = small[l]
        qkrow = _pad_to(jnp.concatenate([g_q, g_k]), 0, d)
        rows += [g_mix[None], g_ffn[None], qkrow[None], _pad_to(g_conv, 1, d)]
    nrow = 6 * depth
    packed = jnp.concatenate(rows + [_pad_to(loss[:1], 1, d)], axis=0)
    packed = _pad_to(packed, 0, ((nrow + 1 + 7) // 8) * 8)
    summed = _all_reduce_small(packed, "reduce_small")
    loss_out = summed[nrow, 0]

    def big(i, w, m, v, tr, name, rows_=None, cols_=None):
        parts = landed[i]
        pr, pcn = parts.shape[1], parts.shape[2]
        w, m, v = [_pad_to(_pad_to(t, 0, pr), 1, pcn) for t in (w, m, v)]
        outs = _adamw(parts, own[i], w, m, v, tr, name)
        return [o[:rows_ or pr, :cols_ or pcn] for o in outs]

    res = {}
    for l in range(depth):
        i = 5 * l
        res[("w_in", l)] = big(i, w_in[l], m_w_in[l], v_w_in[l], 256, f"adamw_in_{l}")
        res[("w_gate", l)] = big(i + 1, w_gate[l], m_w_gate[l], v_w_gate[l], 256, f"adamw_gate_{l}", cols_=ff_shard)
        res[("w_up", l)] = big(i + 2, w_up[l], m_w_up[l], v_w_up[l], 256, f"adamw_up_{l}", cols_=ff_shard)
        res[("w_out", l)] = big(i + 3, w_out[l], m_w_out[l], v_w_out[l], w_out.shape[1], f"adamw_out_{l}")
        res[("w_down", l)] = big(i + 4, w_down[l], m_w_down[l], v_w_down[l], 128, f"adamw_down_{l}",
                                 rows_=ff_shard)

    g_rows, w_rows, m_rows, v_rows = [], [], [], []
    for l in range(depth):
        base = l * 6
        conv_g = lax.dynamic_slice(summed[base + 3:base + 6], (0, my * conv_shard), (3, conv_shard))
        g_rows += [summed[base:base + 3], _pad_to(conv_g, 1, d)]
        for dst, (nm, qn, kn, nf, cw) in ((w_rows, (norm_mix, q_norm, k_norm, norm_ffn, conv_w)),
                                          (m_rows, (m_norm_mix, m_q_norm, m_k_norm, m_norm_ffn, m_conv_w)),
                                          (v_rows, (v_norm_mix, v_q_norm, v_k_norm, v_norm_ffn, v_conv_w))):
            dst += [nm[l][None], nf[l][None], _pad_to(jnp.concatenate([qn[l], kn[l]]), 0, d)[None],
                    _pad_to(cw[l], 1, d)]
    prow = ((nrow + 7) // 8) * 8
    gs, ws, ms, vs = [_pad_to(jnp.concatenate(t, axis=0), 0, prow) for t in (g_rows, w_rows, m_rows, v_rows)]
    sm = _adamw(gs[None], None, ws, ms, vs, prow, "adamw_small")

    hd = q_norm.shape[1]

    def small_out(t, kind):
        per_layer = []
        for l in range(depth):
            base = l * 6
            per_layer.append({"norm_mix": t[base], "norm_ffn": t[base + 1], "q_norm": t[base + 2, :hd],
                              "k_norm": t[base + 2, hd:2 * hd], "conv_w": t[base + 3:base + 6, :conv_shard]}[kind])
        return jnp.stack(per_layer)

    def big_out(name, i):
        return jnp.stack([res[(name, l)][i] for l in range(depth)])

    outs = [loss_out, grad_x.reshape(x.shape)]
    for i in range(4):
        outs += [small_out(sm[i], "norm_mix"), big_out("w_in", i), small_out(sm[i], "q_norm"),
                 small_out(sm[i], "k_norm"), small_out(sm[i], "conv_w"), big_out("w_out", i),
                 small_out(sm[i], "norm_ffn"), big_out("w_gate", i), big_out("w_up", i), big_out("w_down", i)]
    return tuple(outs)
```

```python
import jax
import jax.numpy as jnp
from jax import lax
from jax.experimental import pallas as pl
from jax.experimental.pallas import tpu as pltpu

F32 = jnp.float32
BF16 = jnp.bfloat16
MESH = pl.DeviceIdType.MESH

N_DEV = 8
LANES = 128
HEAD_DIM = 64
KEY_CHUNK = 128
EPS = 1e-6
VMEM_LIMIT = 48 * 1024 * 1024

ADAM_LR = 0.001
ADAM_B1 = 0.9
ADAM_B2 = 0.999
ADAM_EPS = 1e-08
ADAM_WD = 0.01
ADAM_STEP = 10

NN = (((1,), (0,)), ((), ()))
NT = (((1,), (1,)), ((), ()))
TN = (((0,), (0,)), ((), ()))


def _dot(a, b, dims):
    return lax.dot_general(a.astype(BF16), b.astype(BF16), dims, preferred_element_type=F32)


def _cparams(*sem):
    return pltpu.CompilerParams(dimension_semantics=sem, vmem_limit_bytes=VMEM_LIMIT)


def _split_hi_lo(v):
    hi = v.astype(BF16)
    lo = (v - hi.astype(F32)).astype(BF16)
    return jnp.concatenate([hi, lo], axis=1)


def _rmsnorm_fwd(x, gain, tm, name):
    s, d = x.shape

    def body(x_ref, g_ref, o_ref):
        xv = x_ref[...]
        r = lax.rsqrt(jnp.mean(xv * xv, axis=-1, keepdims=True) + EPS)
        o_ref[...] = ((xv * r) * g_ref[...]).astype(o_ref.dtype)

    return pl.pallas_call(
        body, name=name, grid=(s // tm,),
        in_specs=[pl.BlockSpec((tm, d), lambda i: (i, 0)), pl.BlockSpec((1, d), lambda i: (0, 0))],
        out_specs=pl.BlockSpec((tm, d), lambda i: (i, 0)),
        out_shape=jax.ShapeDtypeStruct((s, d), BF16),
        compiler_params=_cparams("parallel"),
    )(x, gain)


def _rmsnorm_bwd(dh, x, gain, dres, tm, name):
    s, d = x.shape
    nsteps = s // tm

    def body(dh_ref, x_ref, g_ref, dres_ref, dx_ref, dg_ref):
        i = pl.program_id(0)
        xv = x_ref[...]
        r = lax.rsqrt(jnp.mean(xv * xv, axis=-1, keepdims=True) + EPS)
        xhat = xv * r
        dhv = dh_ref[...]
        dxh = dhv * g_ref[...]
        proj = jnp.mean(dxh * xhat, axis=-1, keepdims=True)
        dx_ref[...] = dres_ref[...] + r * (dxh - xhat * proj)
        part = jnp.sum((dhv * xhat).reshape(tm // 8, 8, d), axis=0)

        @pl.when(i == 0)
        def _():
            dg_ref[...] = part

        @pl.when(i > 0)
        def _():
            dg_ref[...] += part

        @pl.when(i == nsteps - 1)
        def _():
            dg_ref[...] = jnp.broadcast_to(jnp.sum(dg_ref[...], axis=0, keepdims=True), (8, d))

    row = pl.BlockSpec((tm, d), lambda i: (i, 0))
    return pl.pallas_call(
        body, name=name, grid=(nsteps,),
        in_specs=[row, row, pl.BlockSpec((1, d), lambda i: (0, 0)), row],
        out_specs=[row, pl.BlockSpec((8, d), lambda i: (0, 0))],
        out_shape=[jax.ShapeDtypeStruct((s, d), F32), jax.ShapeDtypeStruct((8, d), F32)],
        compiler_params=_cparams("arbitrary"),
    )(dh, x, gain, dres)


def _group_mean_matrix():
    r = lax.broadcasted_iota(jnp.int32, (LANES, LANES), 0) // HEAD_DIM
    c = lax.broadcasted_iota(jnp.int32, (LANES, LANES), 1) // HEAD_DIM
    return jnp.where(r == c, 1.0 / HEAD_DIM, 0.0).astype(BF16)


def _group_mean(v, gm):
    hi = v.astype(BF16)
    lo = (v - hi.astype(F32)).astype(BF16)
    return _dot(hi, gm, NN) + _dot(lo, gm, NN)


def _qknorm_fwd(proj, gains, tm, name):
    s = proj.shape[0]
    ncol = gains.shape[1] // LANES

    def body(p_ref, g_ref, gm_ref, o_ref):
        xv = p_ref[...].astype(F32)
        r = lax.rsqrt(_group_mean(xv * xv, gm_ref[...]) + EPS)
        o_ref[...] = ((xv * r) * g_ref[...]).astype(o_ref.dtype)

    blk = pl.BlockSpec((tm, LANES), lambda i, j: (i, j))
    return pl.pallas_call(
        body, name=name, grid=(s // tm, ncol),
        in_specs=[blk, pl.BlockSpec((1, LANES), lambda i, j: (0, j)),
                  pl.BlockSpec((LANES, LANES), lambda i, j: (0, 0))],
        out_specs=blk,
        out_shape=jax.ShapeDtypeStruct((s, ncol * LANES), BF16),
        compiler_params=_cparams("parallel", "parallel"),
    )(proj, gains, _group_mean_matrix())


def _qknorm_bwd(dqk, proj, gains, tm, name):
    s = proj.shape[0]
    ncol = gains.shape[1] // LANES
    nsteps = s // tm

    def body(dy_ref, p_ref, g_ref, gm_ref, dx_ref, dg_ref):
        i = pl.program_id(1)
        gm = gm_ref[...]
        xv = p_ref[...].astype(F32)
        r = lax.rsqrt(_group_mean(xv * xv, gm) + EPS)
        xhat = xv * r
        dy = dy_ref[...]
        dxh = dy * g_ref[...]
        proj_ = _group_mean(dxh * xhat, gm)
        dx_ref[...] = (r * (dxh - xhat * proj_)).astype(dx_ref.dtype)
        part = jnp.sum((dy * xhat).reshape(tm // 8, 8, LANES), axis=0)

        @pl.when(i == 0)
        def _():
            dg_ref[...] = part

        @pl.when(i > 0)
        def _():
            dg_ref[...] += part

        @pl.when(i == nsteps - 1)
        def _():
            dg_ref[...] = jnp.broadcast_to(jnp.sum(dg_ref[...], axis=0, keepdims=True), (8, LANES))

    blk = pl.BlockSpec((tm, LANES), lambda j, i: (i, j))
    return pl.pallas_call(
        body, name=name, grid=(ncol, nsteps),
        in_specs=[blk, blk, pl.BlockSpec((1, LANES), lambda j, i: (0, j)),
                  pl.BlockSpec((LANES, LANES), lambda j, i: (0, 0))],
        out_specs=[blk, pl.BlockSpec((8, LANES), lambda j, i: (0, j))],
        out_shape=[jax.ShapeDtypeStruct((s, ncol * LANES), BF16),
                   jax.ShapeDtypeStruct((8, ncol * LANES), F32)],
        compiler_params=_cparams("parallel", "arbitrary"),
    )(dqk, proj, gains, _group_mean_matrix())


CONV_ROWS = 256
HALO = 8


def _conv_fwd(proj, conv_w8, name):
    s = proj.shape[0]
    nblk = conv_w8.shape[1] // LANES
    first = 3 * nblk
    nchunk = s // CONV_ROWS

    def body(cb_ref, cc_ref, cu_ref, w_ref, y_ref, hpad):
        hpad[pl.ds(0, 2 * HALO), :] = jnp.zeros((2 * HALO, LANES), F32)

        def fill(i, _):
            r0 = pl.multiple_of(i * CONV_ROWS, CONV_ROWS)
            hpad[pl.ds(r0 + 2 * HALO, CONV_ROWS), :] = (
                cc_ref[pl.ds(r0, CONV_ROWS), :].astype(F32) * cu_ref[pl.ds(r0, CONV_ROWS), :].astype(F32))
            return 0

        lax.fori_loop(0, nchunk, fill, 0)
        w0, w1, w2 = w_ref[0:1, :], w_ref[1:2, :], w_ref[2:3, :]

        def conv(i, _):
            r0 = pl.multiple_of(i * CONV_ROWS, CONV_ROWS)
            win = hpad[pl.ds(r0 + HALO, CONV_ROWS + HALO), :]
            c = (w2 * win[HALO:] + w1 * pltpu.roll(win, 1, 0)[HALO:] + w0 * pltpu.roll(win, 2, 0)[HALO:])
            y_ref[pl.ds(r0, CONV_ROWS), :] = (cb_ref[pl.ds(r0, CONV_ROWS), :].astype(F32) * c).astype(y_ref.dtype)
            return 0

        lax.fori_loop(0, nchunk, conv, 0)

    def col(off):
        return pl.BlockSpec((s, LANES), lambda j: (0, off + j))

    return pl.pallas_call(
        body, name=name, grid=(nblk,),
        in_specs=[col(first), col(first + nblk), col(first + 2 * nblk), pl.BlockSpec((8, LANES), lambda j: (0, j))],
        out_specs=pl.BlockSpec((s, LANES), lambda j: (0, j)),
        out_shape=jax.ShapeDtypeStruct((s, nblk * LANES), BF16),
        scratch_shapes=[pltpu.VMEM((s + 2 * HALO, LANES), F32)],
        compiler_params=_cparams("parallel"),
    )(proj, proj, proj, conv_w8)


def _conv_bwd(dmix, proj, conv_w8, name):
    s = proj.shape[0]
    nblk = conv_w8.shape[1] // LANES
    first = 3 * nblk
    nchunk = s // CONV_ROWS

    def body(dy_ref, cb_ref, cc_ref, cu_ref, w_ref, dcb_ref, dcc_ref, dcu_ref, dw_ref, hpad, dcpad):
        hpad[pl.ds(0, 2 * HALO), :] = jnp.zeros((2 * HALO, LANES), F32)
        dcpad[pl.ds(s, 2 * HALO), :] = jnp.zeros((2 * HALO, LANES), F32)

        def fill(i, _):
            r0 = pl.multiple_of(i * CONV_ROWS, CONV_ROWS)
            hpad[pl.ds(r0 + 2 * HALO, CONV_ROWS), :] = (
                cc_ref[pl.ds(r0, CONV_ROWS), :].astype(F32) * cu_ref[pl.ds(r0, CONV_ROWS), :].astype(F32))
            return 0

        lax.fori_loop(0, nchunk, fill, 0)
        w0, w1, w2 = w_ref[0:1, :], w_ref[1:2, :], w_ref[2:3, :]

        def fold(v):
            return jnp.sum(v.reshape(CONV_ROWS // 8, 8, LANES), axis=0)

        def first_pass(i, acc):
            a0, a1, a2 = acc
            r0 = pl.multiple_of(i * CONV_ROWS, CONV_ROWS)
            win = hpad[pl.ds(r0 + HALO, CONV_ROWS + HALO), :]
            h0 = win[HALO:]
            h1 = pltpu.roll(win, 1, 0)[HALO:]
            h2 = pltpu.roll(win, 2, 0)[HALO:]
            c = w2 * h0 + w1 * h1 + w0 * h2
            dy = dy_ref[pl.ds(r0, CONV_ROWS), :]
            dcb_ref[pl.ds(r0, CONV_ROWS), :] = (dy * c).astype(dcb_ref.dtype)
            dc = dy * cb_ref[pl.ds(r0, CONV_ROWS), :].astype(F32)
            dcpad[pl.ds(r0, CONV_ROWS), :] = dc
            return a0 + fold(dc * h2), a1 + fold(dc * h1), a2 + fold(dc * h0)

        z8 = jnp.zeros((8, LANES), F32)
        a0, a1, a2 = lax.fori_loop(0, nchunk, first_pass, (z8, z8, z8))
        dw_ref[...] = jnp.concatenate(
            [jnp.sum(a0, axis=0, keepdims=True), jnp.sum(a1, axis=0, keepdims=True),
             jnp.sum(a2, axis=0, keepdims=True), jnp.zeros((5, LANES), F32)], axis=0)

        def second_pass(i, _):
            r0 = pl.multiple_of(i * CONV_ROWS, CONV_ROWS)
            win = dcpad[pl.ds(r0, CONV_ROWS + HALO), :]
            n = CONV_ROWS + HALO
            dh = (w2 * win[:CONV_ROWS] + w1 * pltpu.roll(win, n - 1, 0)[:CONV_ROWS]
                  + w0 * pltpu.roll(win, n - 2, 0)[:CONV_ROWS])
            dcc_ref[pl.ds(r0, CONV_ROWS), :] = (dh * cu_ref[pl.ds(r0, CONV_ROWS), :].astype(F32)).astype(dcc_ref.dtype)
            dcu_ref[pl.ds(r0, CONV_ROWS), :] = (dh * cc_ref[pl.ds(r0, CONV_ROWS), :].astype(F32)).astype(dcu_ref.dtype)
            return 0

        lax.fori_loop(0, nchunk, second_pass, 0)

    def col(off):
        return pl.BlockSpec((s, LANES), lambda j: (0, off + j))

    out = pl.BlockSpec((s, LANES), lambda j: (0, j))
    return pl.pallas_call(
        body, name=name, grid=(nblk,),
        in_specs=[col(nblk), col(first), col(first + nblk), col(first + 2 * nblk),
                  pl.BlockSpec((8, LANES), lambda j: (0, j))],
        out_specs=[out, out, out, pl.BlockSpec((8, LANES), lambda j: (0, j))],
        out_shape=[jax.ShapeDtypeStruct((s, nblk * LANES), BF16)] * 3 + [jax.ShapeDtypeStruct((8, nblk * LANES), F32)],
        scratch_shapes=[pltpu.VMEM((s + 2 * HALO, LANES), F32), pltpu.VMEM((s + 2 * HALO, LANES), F32)],
        compiler_params=_cparams("parallel"),
    )(dmix, proj, proj, proj, conv_w8)


LOG2E = 1.4426950408889634
LN2 = 0.6931471805599453
NEG_BIG = -1e30


def _cumsum_matrix(kind):
    j = lax.broadcasted_iota(jnp.int32, (KEY_CHUNK, 2 * KEY_CHUNK), 0)
    c = lax.broadcasted_iota(jnp.int32, (KEY_CHUNK, 2 * KEY_CHUNK), 1)
    tri = {"after": j > c, "upto": j <= c, "before": j < c}[kind]
    return jnp.where((c >= KEY_CHUNK) | tri, 1.0, 0.0).astype(BF16)


def _stack_heads(t, m0):
    zero = jnp.zeros_like(t)
    return jnp.concatenate([jnp.where(m0, t, zero), jnp.where(m0, zero, t)], axis=0)


def _softplus2(z):
    sp = jnp.maximum(z, 0.0) + jnp.log2(1.0 + jnp.exp2(-jnp.abs(z)))
    return sp, z - sp


def _key_chunk(ref, kc):
    return ref[pl.ds(pl.multiple_of(kc * KEY_CHUNK, KEY_CHUNK), KEY_CHUNK), :]


def _attn_bwd(qk, proj, dmix, rtot, tq, name):
    s = qk.shape[0]
    nhp = qk.shape[1] // (2 * LANES)
    nc = tq // KEY_CHUNK

    def body(q_ref, k_ref, v_ref, do_ref, r_ref, cmi_ref, cme_ref, bias_ref, dq_ref, dk_ref, dv_ref,
             z_refs, ls_refs, sig_refs, sp_refs, gb_refs, pr_ref, gs_ref):
        qi = pl.program_id(1)

        @pl.when(qi == 0)
        def _():
            dk_ref[...] = jnp.zeros_like(dk_ref)
            dv_ref[...] = jnp.zeros_like(dv_ref)

        nslots = (qi + 1) * nc
        m0 = lax.broadcasted_iota(jnp.int32, (1, LANES), 1) < HEAD_DIM
        qs = _stack_heads(q_ref[...], m0)
        do = do_ref[...]
        dos = _stack_heads(do.astype(BF16), m0)
        dosl = _stack_heads((do * LN2).astype(BF16), m0)
        cmi = cmi_ref[...]
        cme = cme_ref[...]

        def chunk_at(i):
            return jnp.clip(i, 0, nslots - 1)

        def scores(kc):
            return _dot(qs, _key_chunk(k_ref, kc), NT)

        def weights(ls, cs, da, pr, kc):
            a = jnp.exp2(ls - (pr - cs[:, :KEY_CHUNK]))
            gb = (a * da).astype(BF16)
            ks = pl.multiple_of(kc * KEY_CHUNK, KEY_CHUNK)
            dv_ref[pl.ds(ks, KEY_CHUNK), :] += _dot(a, dos, TN)
            return gb, jnp.exp2(ls), pr - cs[:, KEY_CHUNK:]

        def score_grads(gb, sig, cg, gs, dq, kc):
            dzb = (gb.astype(F32) * (1.0 - sig) - sig * (gs + cg[:, :KEY_CHUNK])).astype(BF16)
            ks = pl.multiple_of(kc * KEY_CHUNK, KEY_CHUNK)
            dk_ref[pl.ds(ks, KEY_CHUNK), :] += _dot(dzb, qs, TN)
            dq = dq + _dot(jnp.concatenate([dzb[:tq], dzb[tq:]], axis=1), _stack_heads(_key_chunk(k_ref, kc), m0), NN)
            return gs + cg[:, KEY_CHUNK:], dq

        def step(i, par, bias=None):
            cur, prv = par, 1 - par
            k1, k2 = chunk_at(i - 1), chunk_at(i - 2)
            z_next = scores(chunk_at(i + 1))
            cs = _dot(sp_refs[prv][...], cmi, NN)
            da = _dot(dosl, _key_chunk(v_ref, k1), NT)
            cg = _dot(gb_refs[cur][...], cme, NN)
            z = z_refs[cur][...]
            if bias is not None:
                z = z + bias
            sp, ls = _softplus2(z)
            sp_refs[cur][...] = sp.astype(BF16)
            ls_refs[cur][...] = ls
            gs, dq = score_grads(gb_refs[cur][...], sig_refs[cur][...], cg, gs_ref[...], dq_ref[...], k2)
            gs_ref[...] = gs
            dq_ref[...] = dq
            gb, sig, pr = weights(ls_refs[prv][...], cs, da, pr_ref[...], k1)
            gb_refs[prv][...] = gb
            sig_refs[prv][...] = sig
            pr_ref[...] = pr
            z_refs[prv][...] = z_next

        pr_ref[...] = jnp.concatenate([r_ref[:, :LANES], r_ref[:, LANES:]], axis=0)
        gs_ref[...] = jnp.zeros((2 * tq, LANES), F32)
        dq_ref[...] = jnp.zeros((tq, LANES), F32)
        z_refs[0][...] = scores(chunk_at(0))
        sp_refs[1][...] = jnp.zeros((2 * tq, LANES), BF16)
        ls_refs[1][...] = jnp.full((2 * tq, LANES), NEG_BIG, F32)
        gb_refs[0][...] = jnp.zeros((2 * tq, LANES), BF16)
        sig_refs[0][...] = jnp.zeros((2 * tq, LANES), F32)

        def two_steps(j, _):
            step(2 * j, 0)
            step(2 * j + 1, 1)
            return 0

        lax.fori_loop(0, nslots // 2 - 1, two_steps, 0)
        step(nslots - 2, 0, bias_ref[0])
        step(nslots - 1, 1, bias_ref[1])
        k1, k2 = chunk_at(nslots - 1), chunk_at(nslots - 2)
        gb, sig, _ = weights(ls_refs[1][...], _dot(sp_refs[1][...], cmi, NN),
                             _dot(dosl, _key_chunk(v_ref, k1), NT), pr_ref[...], k1)
        gb2 = gb_refs[0][...]
        gs, dq = score_grads(gb2, sig_refs[0][...], _dot(gb2, cme, NN), gs_ref[...], dq_ref[...], k2)
        _, dq = score_grads(gb, sig, _dot(gb, cme, NN), gs, dq, k1)
        dq_ref[...] = dq

    def wrapped(q_ref, k_ref, v_ref, do_ref, r_ref, cmi_ref, cme_ref, bias_ref, dq_ref, dk_ref, dv_ref,
                z0, z1, ls0, ls1, sg0, sg1, sp0, sp1, gb0, gb1, pr_ref, gs_ref):
        body(q_ref, k_ref, v_ref, do_ref, r_ref, cmi_ref, cme_ref, bias_ref, dq_ref, dk_ref, dv_ref,
             (z0, z1), (ls0, ls1), (sg0, sg1), (sp0, sp1), (gb0, gb1), pr_ref, gs_ref)

    assert nc == 2
    bias = _diag_bias(tq, True)
    bias = jnp.concatenate([bias[:, :, :KEY_CHUNK], bias[:, :, KEY_CHUNK:]], axis=1)
    qblk = pl.BlockSpec((tq, LANES), lambda p, i: (i, p))
    full = pl.BlockSpec((s, LANES), lambda p, i: (0, p))
    cmspec = pl.BlockSpec((KEY_CHUNK, 2 * KEY_CHUNK), lambda p, i: (0, 0))
    shape = jax.ShapeDtypeStruct((s, nhp * LANES), F32)
    f32buf = pltpu.VMEM((2 * tq, LANES), F32)
    bf16buf = pltpu.VMEM((2 * tq, LANES), BF16)
    return pl.pallas_call(
        wrapped, name=name, grid=(nhp, s // tq),
        in_specs=[qblk,
                  pl.BlockSpec((s, LANES), lambda p, i: (0, nhp + p)),
                  pl.BlockSpec((s, LANES), lambda p, i: (0, 2 * nhp + p)),
                  qblk,
                  pl.BlockSpec((tq, 2 * LANES), lambda p, i: (i, p)),
                  cmspec, cmspec,
                  pl.BlockSpec((nc, 2 * tq, LANES), lambda p, i: (0, 0, 0))],
        out_specs=[qblk, full, full],
        out_shape=[shape, shape, shape],
        scratch_shapes=[f32buf] * 6 + [bf16buf] * 4 + [f32buf] * 2,
        compiler_params=_cparams("parallel", "arbitrary"),
    )(qk, qk, proj, dmix, rtot, _cumsum_matrix("upto"), _cumsum_matrix("before"), bias)


def _pair_cumsum_matrix(kind):
    j = lax.broadcasted_iota(jnp.int32, (2 * KEY_CHUNK, 4 * KEY_CHUNK), 0)
    c = lax.broadcasted_iota(jnp.int32, (2 * KEY_CHUNK, 4 * KEY_CHUNK), 1)
    same_head = (j // KEY_CHUNK) == ((c // KEY_CHUNK) % 2)
    jj, cc = j % KEY_CHUNK, c % KEY_CHUNK
    tri = {"after": jj > cc, "upto": jj <= cc, "before": jj < cc}[kind]
    return jnp.where(same_head & ((c >= 2 * KEY_CHUNK) | tri), 1.0, 0.0).astype(BF16)


def _diag_bias(tq, ascending):
    nc = tq // KEY_CHUNK
    shape = (nc, tq, 2 * KEY_CHUNK)
    d = lax.broadcasted_iota(jnp.int32, shape, 0)
    r = lax.broadcasted_iota(jnp.int32, shape, 1)
    c = lax.broadcasted_iota(jnp.int32, shape, 2) % KEY_CHUNK
    chunk = d if ascending else nc - 1 - d
    return jnp.where(chunk * KEY_CHUNK + c < r, 0.0, NEG_BIG).astype(F32)


def _attn_fwd(qk, proj, tq, name):
    s = qk.shape[0]
    nhp = qk.shape[1] // (2 * LANES)
    nc = tq // KEY_CHUNK
    assert nc == 2
    w = 2 * KEY_CHUNK

    def body(q_ref, k_ref, v_ref, cm_ref, bias_ref, o_ref, r_ref, z_refs, ls_refs, cs_refs, ct_refs, sp_refs,
             ab_refs, acc_ref):
        qi = pl.program_id(1)
        nslots = (qi + 1) * nc
        m0 = lax.broadcasted_iota(jnp.int32, (1, LANES), 1) < HEAD_DIM
        q = q_ref[...]
        cm = cm_ref[...]

        def chunk_at(i):
            return jnp.clip(nslots - 1 - i, 0, nslots - 1)

        def scores(kc):
            return _dot(q, _stack_heads(_key_chunk(k_ref, kc), m0), NT)

        def values(ab, kc):
            return _dot(ab, _stack_heads(_key_chunk(v_ref, kc), m0), NN)

        def step(i, par, bias=None, stages="zscwv"):
            cur, prv = par, 1 - par
            if "z" in stages:
                z_next = scores(chunk_at(i + 1))
            if "c" in stages:
                cs = _dot(sp_refs[prv][...], cm, NN)
            if "v" in stages:
                pv = values(ab_refs[prv][...], chunk_at(i - 3))
            if "w" in stages:
                rs = r_ref[...]
                r_ref[...] = rs + ct_refs[cur][...]
                ab_refs[cur][...] = jnp.exp2(ls_refs[cur][...] - cs_refs[cur][...] - rs).astype(BF16)
            if "s" in stages:
                z = z_refs[cur][...]
                if bias is not None:
                    z = z + bias
                sp, ls = _softplus2(z)
                sp_refs[cur][...] = sp.astype(BF16)
                ls_refs[cur][...] = ls
            if "v" in stages:
                acc_ref[...] += pv
            if "c" in stages:
                cs_refs[prv][...] = cs[:, :w]
                ct_refs[prv][...] = cs[:, w:]
            if "z" in stages:
                z_refs[prv][...] = z_next

        z_refs[0][...] = scores(chunk_at(0))
        for p in range(2):
            sp_refs[p][...] = jnp.zeros((tq, w), BF16)
            ls_refs[p][...] = jnp.full((tq, w), NEG_BIG, F32)
            cs_refs[p][...] = jnp.zeros((tq, w), F32)
            ct_refs[p][...] = jnp.zeros((tq, w), F32)
            ab_refs[p][...] = jnp.zeros((tq, w), BF16)
        r_ref[...] = jnp.zeros((tq, w), F32)
        acc_ref[...] = jnp.zeros((tq, LANES), F32)
        step(0, 0, bias_ref[0])
        step(1, 1, bias_ref[1])

        def two_steps(j, _):
            step(2 * j, 0)
            step(2 * j + 1, 1)
            return 0

        lax.fori_loop(1, nslots // 2, two_steps, 0)
        step(nslots, 0, stages="cwv")
        step(nslots + 1, 1, stages="wv")
        step(nslots + 2, 0, stages="v")
        o_ref[...] = acc_ref[...].astype(o_ref.dtype)

    def wrapped(q_ref, k_ref, v_ref, cm_ref, bias_ref, o_ref, r_ref, *scratch):
        z, ls, cs, ct, sp, ab = [scratch[2 * j:2 * j + 2] for j in range(6)]
        body(q_ref, k_ref, v_ref, cm_ref, bias_ref, o_ref, r_ref, z, ls, cs, ct, sp, ab, scratch[12])

    f32buf = pltpu.VMEM((tq, w), F32)
    bf16buf = pltpu.VMEM((tq, w), BF16)
    return pl.pallas_call(
        wrapped, name=name, grid=(nhp, s // tq),
        in_specs=[pl.BlockSpec((tq, LANES), lambda p, i: (i, p)),
                  pl.BlockSpec((s, LANES), lambda p, i: (0, nhp + p)),
                  pl.BlockSpec((s, LANES), lambda p, i: (0, 2 * nhp + p)),
                  pl.BlockSpec((w, 2 * w), lambda p, i: (0, 0)),
                  pl.BlockSpec((nc, tq, w), lambda p, i: (0, 0, 0))],
        out_specs=[pl.BlockSpec((tq, LANES), lambda p, i: (i, p)),
                   pl.BlockSpec((tq, w), lambda p, i: (i, p))],
        out_shape=[jax.ShapeDtypeStruct((s, nhp * LANES), BF16),
                   jax.ShapeDtypeStruct((s, nhp * w), F32)],
        scratch_shapes=[f32buf] * 8 + [bf16buf] * 4 + [pltpu.VMEM((tq, LANES), F32)],
        compiler_params=_cparams("parallel", "parallel"),
    )(qk, qk, proj, _pair_cumsum_matrix("after"), _diag_bias(tq, False))


def _mm_blocks(h, ga, widx, tm, name):
    s, d = h.shape
    nb, cols = ga.shape[1], ga.shape[3]

    def body(a_ref, b_ref, o_ref):
        o_ref[...] = _dot(a_ref[...], b_ref[...], NN).astype(o_ref.dtype)

    return pl.pallas_call(
        body, name=name, grid=(s // tm, nb),
        in_specs=[pl.BlockSpec((tm, d), lambda i, j: (i, 0)),
                  pl.BlockSpec((None, None, d, cols), lambda i, j: (widx, j, 0, 0))],
        out_specs=pl.BlockSpec((tm, cols), lambda i, j: (i, j)),
        out_shape=jax.ShapeDtypeStruct((s, nb * cols), BF16),
        compiler_params=_cparams("parallel", "arbitrary"),
    )(h, ga)


def _mm_swiglu(h, ga, gidx, uidx, tm, name):
    s, d = h.shape
    nb, cols = ga.shape[1], ga.shape[3]

    def body(a_ref, bg_ref, bu_ref, g_ref, u_ref, act_ref):
        a = a_ref[...]
        g = _dot(a, bg_ref[...], NN)
        u = _dot(a, bu_ref[...], NN)
        g_ref[...] = g.astype(g_ref.dtype)
        u_ref[...] = u.astype(u_ref.dtype)
        act_ref[...] = (g * (1.0 / (1.0 + jnp.exp(-g))) * u).astype(act_ref.dtype)

    def wspec(idx):
        return pl.BlockSpec((None, None, d, cols), lambda i, j: (idx, j, 0, 0))

    out = pl.BlockSpec((tm, cols), lambda i, j: (i, j))
    shape = jax.ShapeDtypeStruct((s, nb * cols), BF16)
    return pl.pallas_call(
        body, name=name, grid=(s // tm, nb),
        in_specs=[pl.BlockSpec((tm, d), lambda i, j: (i, 0)), wspec(gidx), wspec(uidx)],
        out_specs=[out, out, out], out_shape=[shape, shape, shape],
        compiler_params=_cparams("parallel", "arbitrary"),
    )(h, ga, ga)


def _mm_residual(a, w3, lidx, res, tm, tn, name):
    s, k = a.shape
    n = w3.shape[2]

    def body(a_ref, b_ref, r_ref, o_ref):
        o_ref[...] = r_ref[...] + _dot(a_ref[...], b_ref[...], NN)

    return pl.pallas_call(
        body, name=name, grid=(s // tm, n // tn),
        in_specs=[pl.BlockSpec((tm, k), lambda i, j: (i, 0)),
                  pl.BlockSpec((None, k, tn), lambda i, j: (lidx, 0, j)),
                  pl.BlockSpec((tm, tn), lambda i, j: (i, j))],
        out_specs=pl.BlockSpec((tm, tn), lambda i, j: (i, j)),
        out_shape=jax.ShapeDtypeStruct((s, n), F32),
        compiler_params=_cparams("parallel", "arbitrary"),
    )(a, w3, res)


def _mm_nt(a, w3, lidx, tm, tn, name):
    s, k = a.shape
    n = w3.shape[1]

    def body(a_ref, b_ref, o_ref):
        o_ref[...] = _dot(a_ref[...], b_ref[...], NT)

    return pl.pallas_call(
        body, name=name, grid=(s // tm, n // tn),
        in_specs=[pl.BlockSpec((tm, k), lambda i, j: (i, 0)),
                  pl.BlockSpec((None, tn, k), lambda i, j: (lidx, j, 0))],
        out_specs=pl.BlockSpec((tm, tn), lambda i, j: (i, j)),
        out_shape=jax.ShapeDtypeStruct((s, n), F32),
        compiler_params=_cparams("parallel", "arbitrary"),
    )(a, w3)


def _mm_nt_swiglu_bwd(dx, wd3, lidx, g, u, tm, name):
    s, d = dx.shape
    cols = g.shape[1] // N_DEV

    def body(a_ref, b_ref, g_ref, u_ref, dg_ref, du_ref):
        dact = _dot(a_ref[...], b_ref[...], NT)
        gv = g_ref[...].astype(F32)
        sig = 1.0 / (1.0 + jnp.exp(-gv))
        du_ref[...] = (dact * (gv * sig)).astype(du_ref.dtype)
        dg_ref[...] = (dact * u_ref[...].astype(F32) * (sig * (1.0 + gv * (1.0 - sig)))).astype(dg_ref.dtype)

    blk = pl.BlockSpec((tm, cols), lambda i, j: (i, j))
    shape = jax.ShapeDtypeStruct(g.shape, BF16)
    return pl.pallas_call(
        body, name=name, grid=(s // tm, N_DEV),
        in_specs=[pl.BlockSpec((tm, d), lambda i, j: (i, 0)),
                  pl.BlockSpec((None, cols, d), lambda i, j: (lidx, j, 0)), blk, blk],
        out_specs=[blk, blk], out_shape=[shape, shape],
        compiler_params=_cparams("parallel", "arbitrary"),
    )(dx, wd3, g, u)


def _mm_nt_blocks(das, ga, widxs, tm, name):
    s = das[0].shape[0]
    nb, d, cols = ga.shape[1], ga.shape[2], ga.shape[3]
    nw = len(das)

    def body(*refs):
        a_refs, b_refs, o_ref = refs[:nw], refs[nw:2 * nw], refs[2 * nw]
        k = pl.program_id(1)
        part = _dot(a_refs[0][...], b_refs[0][...], NT)
        for w in range(1, nw):
            part = part + _dot(a_refs[w][...], b_refs[w][...], NT)

        @pl.when(k == 0)
        def _():
            o_ref[...] = part

        @pl.when(k > 0)
        def _():
            o_ref[...] += part

    def wspec(idx):
        return pl.BlockSpec((None, None, d, cols), lambda i, k: (idx, k, 0, 0))

    return pl.pallas_call(
        body, name=name, grid=(s // tm, nb),
        in_specs=[pl.BlockSpec((tm, cols), lambda i, k: (i, k))] * nw + [wspec(i) for i in widxs],
        out_specs=pl.BlockSpec((tm, d), lambda i, k: (i, 0)),
        out_shape=jax.ShapeDtypeStruct((s, d), F32),
        compiler_params=_cparams("parallel", "arbitrary"),
    )(*das, *([ga] * nw))


def _mm_tn(a, b, ta, tb, tk, out_blocks, name):
    s, ka = a.shape
    nb = b.shape[1]
    nk = s // tk

    def body(a_ref, b_ref, o_ref, ob_ref):
        k = pl.program_id(2)
        part = _dot(a_ref[...], b_ref[...], TN)

        @pl.when(k == 0)
        def _():
            o_ref[...] = part

        @pl.when(k > 0)
        def _():
            o_ref[...] += part

        @pl.when(k == nk - 1)
        def _():
            ob_ref[...] = o_ref[...].astype(ob_ref.dtype)

    if out_blocks:
        out_spec = pl.BlockSpec((None, ta, tb), lambda i, j, k: (j, i, 0))
        shape = (nb // tb, ka, tb)
    else:
        out_spec = pl.BlockSpec((ta, tb), lambda i, j, k: (i, j))
        shape = (ka, nb)
    return pl.pallas_call(
        body, name=name, grid=(ka // ta, nb // tb, nk),
        in_specs=[pl.BlockSpec((tk, ta), lambda i, j, k: (k, i)),
                  pl.BlockSpec((tk, tb), lambda i, j, k: (k, j))],
        out_specs=[out_spec, out_spec],
        out_shape=[jax.ShapeDtypeStruct(shape, F32), jax.ShapeDtypeStruct(shape, BF16)],
        compiler_params=_cparams("parallel", "parallel", "arbitrary"),
    )(a, b)


def _loss_head(y, target, tm, name):
    s, d = y.shape
    nsteps = s // tm

    def body(y_ref, t_ref, dy_ref, l_ref, acc):
        i = pl.program_id(0)
        diff = y_ref[...] - t_ref[...]
        dy_ref[...] = diff * (1.0 / d)
        part = jnp.sum((diff * diff).reshape(tm // 8, 8, d), axis=0)

        @pl.when(i == 0)
        def _():
            acc[...] = part

        @pl.when(i > 0)
        def _():
            acc[...] += part

        @pl.when(i == nsteps - 1)
        def _():
            tot = jnp.sum(jnp.sum(acc[...], axis=1, keepdims=True), axis=0, keepdims=True)
            l_ref[...] = jnp.broadcast_to(tot * (0.5 / d), (8, LANES))

    row = pl.BlockSpec((tm, d), lambda i: (i, 0))
    return pl.pallas_call(
        body, name=name, grid=(nsteps,),
        in_specs=[row, row],
        out_specs=[row, pl.BlockSpec((8, LANES), lambda i: (0, 0))],
        out_shape=[jax.ShapeDtypeStruct((s, d), F32), jax.ShapeDtypeStruct((8, LANES), F32)],
        scratch_shapes=[pltpu.VMEM((8, d), F32)],
        compiler_params=_cparams("arbitrary"),
    )(y, target)


def _adamw(parts, own, w, m, v, tr, name):
    p, rows, cols = parts.shape
    c1 = 1.0 / (1.0 - ADAM_B1 ** ADAM_STEP)
    c2 = 1.0 / (1.0 - ADAM_B2 ** ADAM_STEP)

    def body(*refs):
        if own is None:
            p_ref, w_ref, m_ref, v_ref, g_ref, d_ref, nm_ref, nv_ref = refs
            g = p_ref[0]
            for k in range(1, p):
                g = g + p_ref[k]
        else:
            p_ref, own_ref, w_ref, m_ref, v_ref, g_ref, d_ref, nm_ref, nv_ref = refs
            x, y, c = _place()
            my = 4 * x + 2 * y + c
            mine = own_ref[...]
            g = jnp.where(my == 0, mine, p_ref[0].astype(F32))
            for k in range(1, p):
                g = g + jnp.where(my == k, mine, p_ref[k].astype(F32))
        nm = ADAM_B1 * m_ref[...] + (1.0 - ADAM_B1) * g
        nv = ADAM_B2 * v_ref[...] + (1.0 - ADAM_B2) * (g * g)
        g_ref[...] = g
        nm_ref[...] = nm
        nv_ref[...] = nv
        d_ref[...] = -ADAM_LR * ((nm * c1) / (jnp.sqrt(nv * c2) + ADAM_EPS) + ADAM_WD * w_ref[...])

    blk = pl.BlockSpec((tr, cols), lambda i: (i, 0))
    shape = jax.ShapeDtypeStruct((rows, cols), F32)
    return pl.pallas_call(
        body, name=name, grid=(rows // tr,),
        in_specs=[pl.BlockSpec((p, tr, cols), lambda i: (0, i, 0))] + [blk] * (3 if own is None else 4),
        out_specs=[blk] * 4, out_shape=[shape] * 4,
        compiler_params=_cparams("parallel"),
    )(*([parts] + ([] if own is None else [own]) + [w, m, v]))


def _place():
    x, y, c = lax.axis_index("x"), lax.axis_index("y"), lax.axis_index("c")
    return x, y, c


def _all_gather(shards, name):
    na = len(shards)

    def body(*refs):
        srcs, dsts = refs[:na], refs[na:2 * na]
        send_sems, recv_sems, local_sems = refs[2 * na:]
        x, y, c = _place()
        me, sibling = (x, y, c), (x, y, 1 - c)
        chips = [(1 - x, y), (x, 1 - y), (1 - x, 1 - y)]

        def slot(a, dev):
            return dsts[a].at[:, pl.ds(4 * dev[0] + 2 * dev[1] + dev[2], 1)]

        def copy(k, a, block, to, from_shard=False):
            return pltpu.make_async_remote_copy(
                src_ref=srcs[a] if from_shard else slot(a, block), dst_ref=slot(a, block),
                send_sem=send_sems.at[k, a], recv_sem=recv_sems.at[k, a], device_id=to, device_id_type=MESH)

        mine = [pltpu.make_async_copy(srcs[a], slot(a, me), local_sems.at[a]) for a in range(na)]
        for cp in mine:
            cp.start()
        first = [copy(0, a, me, sibling, True) for a in range(na)]
        first += [copy(1 + j, a, me, (*chip, c), True) for j, chip in enumerate(chips) for a in range(na)]
        for cp in first:
            cp.start()
        passed = []
        for j, chip in enumerate(chips):
            for a in range(na):
                copy(1 + j, a, (*chip, c), me).wait_recv()
                fwd = copy(4 + j, a, (*chip, c), sibling)
                fwd.start()
                passed.append(fwd)
        for a in range(na):
            copy(0, a, sibling, me).wait_recv()
        for j, chip in enumerate(chips):
            for a in range(na):
                copy(4 + j, a, (*chip, 1 - c), me).wait_recv()
        for cp in first + passed:
            cp.wait_send()
        for cp in mine:
            cp.wait()

    anyspec = pl.BlockSpec(memory_space=pl.ANY)
    return pl.pallas_call(
        body, name=name,
        in_specs=[anyspec] * na, out_specs=[anyspec] * na,
        out_shape=[jax.ShapeDtypeStruct((a.shape[0], N_DEV) + a.shape[2:], a.dtype) for a in shards],
        scratch_shapes=[pltpu.SemaphoreType.DMA((7, na)), pltpu.SemaphoreType.DMA((7, na)),
                        pltpu.SemaphoreType.DMA((na,))],
    )(*shards)


_RELATIONS = [(dx, dy, dc) for dx in (0, 1) for dy in (0, 1) for dc in (0, 1)][1:]


def _flip(v, d):
    return 1 - v if d else v


def _exchange_blocks(grads, name):
    na = len(grads)

    def body(*refs):
        srcs, dsts = refs[:na], refs[na:2 * na]
        send_sems, recv_sems, local_sems = refs[2 * na:]
        x, y, c = _place()
        my = 4 * x + 2 * y + c
        mine = [pltpu.make_async_copy(srcs[a].at[pl.ds(my, 1)], dsts[a].at[pl.ds(my, 1)], local_sems.at[a])
                for a in range(na)]
        for cp in mine:
            cp.start()
        sends = []
        for k, (dx, dy, dc) in enumerate(_RELATIONS):
            peer = (_flip(x, dx), _flip(y, dy), _flip(c, dc))
            pidx = 4 * peer[0] + 2 * peer[1] + peer[2]
            for a in range(na):
                cp = pltpu.make_async_remote_copy(
                    src_ref=srcs[a].at[pl.ds(pidx, 1)], dst_ref=dsts[a].at[pl.ds(my, 1)],
                    send_sem=send_sems.at[k, a], recv_sem=recv_sems.at[k, a], device_id=peer, device_id_type=MESH)
                cp.start()
                sends.append((cp, pidx, k, a, peer))
        for cp, pidx, k, a, peer in sends:
            pltpu.make_async_remote_copy(
                src_ref=srcs[a].at[pl.ds(pidx, 1)], dst_ref=dsts[a].at[pl.ds(pidx, 1)],
                send_sem=send_sems.at[k, a], recv_sem=recv_sems.at[k, a], device_id=peer,
                device_id_type=MESH).wait_recv()
        for cp, *_ in sends:
            cp.wait_send()
        for cp in mine:
            cp.wait()

    anyspec = pl.BlockSpec(memory_space=pl.ANY)
    return pl.pallas_call(
        body, name=name,
        in_specs=[anyspec] * na, out_specs=[anyspec] * na,
        out_shape=[jax.ShapeDtypeStruct(a.shape, a.dtype) for a in grads],
        scratch_shapes=[pltpu.SemaphoreType.DMA((7, na)), pltpu.SemaphoreType.DMA((7, na)),
                        pltpu.SemaphoreType.DMA((na,))],
    )(*grads)


def _peers():
    x, y, c = _place()
    out = []
    for dx, dy, dc in _RELATIONS:
        peer = (_flip(x, dx), _flip(y, dy), _flip(c, dc))
        out.append((peer, 4 * peer[0] + 2 * peer[1] + peer[2]))
    return 4 * x + 2 * y + c, out


def _exchange_start(grads, name):
    na = len(grads)

    def body(*refs):
        srcs, lands = refs[:na], refs[na:2 * na]
        send_sems, recv_sems, token = refs[2 * na], refs[2 * na + 1], refs[-1]
        my, peers = _peers()
        for k, (peer, pidx) in enumerate(peers):
            for a in range(na):
                pltpu.make_async_remote_copy(
                    src_ref=srcs[a].at[pl.ds(pidx, 1)], dst_ref=lands[a].at[pl.ds(my, 1)],
                    send_sem=send_sems.at[k * na + a], recv_sem=recv_sems.at[k * na + a], device_id=peer,
                    device_id_type=MESH).start()
        token[...] = jnp.zeros_like(token)

    hbm = pl.BlockSpec(memory_space=pltpu.HBM)
    sem = pl.BlockSpec(memory_space=pltpu.SEMAPHORE)
    thru = [pltpu.HBM(g.shape, g.dtype) for g in grads]
    operands = [pltpu.with_memory_space_constraint(g, pltpu.HBM) for g in grads]
    operands += [pltpu.with_memory_space_constraint(jnp.zeros(g.shape, g.dtype), pltpu.HBM) for g in grads]
    outs = pl.pallas_call(
        body, name=name,
        out_shape=(pltpu.SemaphoreType.DMA((7 * na,)), pltpu.SemaphoreType.DMA((7 * na,)), *thru, *thru,
                   jax.ShapeDtypeStruct((8, LANES), F32)),
        in_specs=[hbm] * (2 * na),
        out_specs=(sem, sem, *([hbm] * (2 * na)), pl.BlockSpec(memory_space=pltpu.VMEM)),
        input_output_aliases={i: 2 + i for i in range(2 * na)},
        compiler_params=pltpu.CompilerParams(has_side_effects=pltpu.SideEffectType.DATAFLOW_SIDE_EFFECTING),
    )(*operands)
    return outs[0], outs[1], list(outs[2:2 + na]), list(outs[2 + na:2 + 2 * na]), outs[-1]


def _exchange_wait(send_sems, recv_sems, grads, lands, after, name):
    na = len(grads)

    def body(*refs):
        srcs, lands_ = refs[:na], refs[na:2 * na]
        ssem, rsem = refs[2 * na], refs[2 * na + 1]
        _, peers = _peers()
        for k, (peer, pidx) in enumerate(peers):
            for a in range(na):
                cp = pltpu.make_async_remote_copy(
                    src_ref=srcs[a].at[pl.ds(pidx, 1)], dst_ref=lands_[a].at[pl.ds(pidx, 1)],
                    send_sem=ssem.at[k * na + a], recv_sem=rsem.at[k * na + a], device_id=peer, device_id_type=MESH)
                cp.wait_send()
                cp.wait_recv()

    hbm = pl.BlockSpec(memory_space=pltpu.HBM)
    sem = pl.BlockSpec(memory_space=pltpu.SEMAPHORE)
    thru = [pltpu.HBM(g.shape, g.dtype) for g in grads]
    outs = pl.pallas_call(
        body, name=name,
        out_shape=(*thru, *thru),
        in_specs=[hbm] * (2 * na) + [sem, sem, pl.BlockSpec(memory_space=pl.ANY)],
        out_specs=tuple([hbm] * (2 * na)),
        input_output_aliases={i: i for i in range(2 * na)},
        compiler_params=pltpu.CompilerParams(has_side_effects=pltpu.SideEffectType.DATAFLOW_SIDE_EFFECTING),
    )(*grads, *lands, send_sems, recv_sems, after)
    return list(outs[na:])


def _all_reduce_small(v, name):
    r, c_ = v.shape

    def body(v_ref, o_ref, gath, send_sems, recv_sems):
        x, y, c = _place()
        my = 4 * x + 2 * y + c
        gath[my] = v_ref[...]
        sends = []
        for k, (dx, dy, dc) in enumerate(_RELATIONS):
            peer = (_flip(x, dx), _flip(y, dy), _flip(c, dc))
            cp = pltpu.make_async_remote_copy(
                src_ref=v_ref, dst_ref=gath.at[my], send_sem=send_sems.at[k], recv_sem=recv_sems.at[k],
                device_id=peer, device_id_type=MESH)
            cp.start()
            sends.append((cp, 4 * peer[0] + 2 * peer[1] + peer[2], k, peer))
        for cp, pidx, k, peer in sends:
            pltpu.make_async_remote_copy(
                src_ref=v_ref, dst_ref=gath.at[pidx], send_sem=send_sems.at[k], recv_sem=recv_sems.at[k],
                device_id=peer, device_id_type=MESH).wait_recv()
        for cp, *_ in sends:
            cp.wait_send()
        tot = gath[0]
        for k in range(1, N_DEV):
            tot = tot + gath[k]
        o_ref[...] = tot

    vm = pl.BlockSpec(memory_space=pltpu.VMEM)
    return pl.pallas_call(
        body, name=name, in_specs=[vm], out_specs=vm,
        out_shape=jax.ShapeDtypeStruct((r, c_), F32),
        scratch_shapes=[pltpu.VMEM((N_DEV, r, c_), F32), pltpu.SemaphoreType.DMA((7,)),
                        pltpu.SemaphoreType.DMA((7,))],
    )(v)


TM = 512
TM_MATMUL = 2048
TM_RESIDUAL = 1024
TQ = 256


def _pad_to(a, axis, size):
    pad = [(0, 0)] * a.ndim
    pad[axis] = (0, size - a.shape[axis])
    return jnp.pad(a, pad)


def _local_step(x, target, ga, gb, gc, conv_full, norm_mix, q_norm, k_norm, norm_ffn, start_exchange=None):
    depth = gb.shape[0]
    tm, tq = min(TM, x.shape[0]), min(TQ, x.shape[0])
    tmm, tmr = min(TM_MATMUL, x.shape[0]), min(TM_RESIDUAL, x.shape[0])
    attn = gb.shape[1] // 2
    nheads = attn // HEAD_DIM
    scale = HEAD_DIM ** -0.5 * LOG2E
    saved = []
    for l in range(depth):
        h1 = _rmsnorm_fwd(x, norm_mix[l][None], tm,f"norm_mix_fwd_{l}")
        proj = _mm_blocks(h1, ga, 3 * l, tmm, f"proj_in_{l}")
        qk_gain = jnp.concatenate([jnp.tile(q_norm[l], nheads) * scale, jnp.tile(k_norm[l], nheads)])[None]
        qk = _qknorm_fwd(proj, qk_gain, tmm, f"qknorm_fwd_{l}")
        o, rtot = _attn_fwd(qk, proj, tq,f"attn_fwd_{l}")
        conv_w8 = _pad_to(conv_full[l], 0, 8)
        cv = _conv_fwd(proj, conv_w8, f"conv_fwd_{l}")
        mix = jnp.concatenate([o, cv], axis=1)
        x1 = _mm_residual(mix, gb, l, x, tmr, 512, f"proj_out_{l}")
        h2 = _rmsnorm_fwd(x1, norm_ffn[l][None], tm,f"norm_ffn_fwd_{l}")
        g, u, act = _mm_swiglu(h2, ga, 3 * l + 1, 3 * l + 2, tmm, f"ffn_up_{l}")
        x2 = _mm_residual(act, gc, l, x1, tmr, 512, f"ffn_down_{l}")
        saved.append((x, h1, proj, qk_gain, qk, rtot, conv_w8, mix, x1, h2, g, u, act))
        x = x2

    dx, loss = _loss_head(x, target, tm,"loss_head")

    grads = [None] * depth
    small = [None] * depth
    tie = 0.0
    for l in reversed(range(depth)):
        x0, h1, proj, qk_gain, qk, rtot, conv_w8, mix, x1, h2, g, u, act = saved[l]
        d = x0.shape[1]
        dg, du = _mm_nt_swiglu_bwd(dx, gc, l, g, u, tmm, f"ffn_down_bwd_{l}")
        d_wdown = _mm_tn(act, dx, 768, d, tmm, False, f"dw_down_{l}")
        d_wgate = _mm_tn(h2, dg, d, ga.shape[3], tmm, True, f"dw_gate_{l}")
        d_wup = _mm_tn(h2, du, d, ga.shape[3], tmm, True, f"dw_up_{l}")
        dh2 = _mm_nt_blocks([dg, du], ga, [3 * l + 1, 3 * l + 2], tmm, f"ffn_up_bwd_{l}")
        dx1, dg_ffn = _rmsnorm_bwd(dh2, x1, norm_ffn[l][None] + tie, dx, tm,f"norm_ffn_bwd_{l}")
        if start_exchange is not None and l == 0:
            conv_w8 = conv_w8 + start_exchange(l, (1, 2, 4), (d_wgate, d_wup, d_wdown))
        dmix = _mm_nt(dx1, gb, l, tmr, 512, f"proj_out_bwd_{l}")
        d_wout = _mm_tn(mix, dx1, 512, d, tmm, False, f"dw_out_{l}")
        dcb, dcc, dcu, dconv = _conv_bwd(dmix, proj, conv_w8, f"conv_bwd_{l}")
        dq, dk, dv = _attn_bwd(qk, proj, dmix, rtot, tq,f"attn_bwd_{l}")
        dqk, dg_qk = _qknorm_bwd(jnp.concatenate([dq, dk], axis=1), proj, qk_gain, tmm, f"qknorm_bwd_{l}")
        dproj = jnp.concatenate([dqk, dv.astype(BF16), dcb, dcc, dcu], axis=1)
        d_win = _mm_tn(h1, dproj, d, ga.shape[3], tmm, True, f"dw_in_{l}")
        dh1 = _mm_nt_blocks([dproj], ga, [3 * l], tmm, f"proj_in_bwd_{l}")
        dx, dg_mix = _rmsnorm_bwd(dh1, x0, norm_mix[l][None], dx1, tm,f"norm_mix_bwd_{l}")
        grads[l] = (d_win, d_wgate, d_wup, d_wout, d_wdown)
        if start_exchange is not None and l > 0:
            tie = start_exchange(l, (0, 1, 2, 3, 4), grads[l])
        dq_gain = jnp.sum(dg_qk[0, :attn].reshape(nheads, HEAD_DIM), axis=0) * scale
        dk_gain = jnp.sum(dg_qk[0, attn:].reshape(nheads, HEAD_DIM), axis=0)
        small[l] = (dg_mix[0], dg_ffn[0], dq_gain, dk_gain, dconv[:3])
    return loss, dx, grads, small


def kernel(x, norm_mix, w_in, q_norm, k_norm, conv_w, w_out, norm_ffn, w_gate, w_up, w_down, loss_target, m_norm_mix, m_w_in, m_q_norm, m_k_norm, m_conv_w, m_w_out, m_norm_ffn, m_w_gate, m_w_up, m_w_down, v_norm_mix, v_w_in, v_q_norm, v_k_norm, v_conv_w, v_w_out, v_norm_ffn, v_w_gate, v_w_up, v_w_down):
    depth, d, in_shard = w_in.shape
    ff_shard = w_gate.shape[2]
    ff_pad = in_shard
    conv_shard = conv_w.shape[2]
    xs = x.reshape(x.shape[-2], d)
    target = loss_target.reshape(xs.shape)

    pa = jnp.stack([w_in, _pad_to(w_gate, 2, ff_pad), _pad_to(w_up, 2, ff_pad)], axis=1)
    pa = pa.reshape(3 * depth, 1, d, in_shard).astype(BF16)
    pb = w_out.astype(BF16)[:, None]
    pc = _pad_to(w_down, 1, ff_pad).astype(BF16)[:, None]
    pd = _pad_to(_pad_to(conv_w.reshape(depth * 3, conv_shard), 0, 8), 1, LANES)[None, None]
    ga, gb, gc, gd = _all_gather([pa, pb, pc, pd], "gather_weights")
    gb = gb.reshape(depth, N_DEV * gb.shape[2], d)
    gc = gc.reshape(depth, N_DEV * ff_pad, d)
    conv_full = gd[0, :, :depth * 3, :conv_shard].transpose(1, 0, 2).reshape(depth, 3, N_DEV * conv_shard)

    block_rows = (d, d, d, w_out.shape[1], ff_pad)
    in_flight = []

    def start_exchange(l, which, pairs):
        send = [g16.reshape(N_DEV, block_rows[j], -1) for j, (_, g16) in zip(which, pairs)]
        send_sems, recv_sems, thru, lands, token = _exchange_start(send, f"exchange_start_{l}_{which[0]}")
        in_flight.append(([5 * l + j for j in which], send_sems, recv_sems, thru, lands))
        return token[0, 0]

    loss, grad_x, grads, small = _local_step(xs, target, ga, gb, gc, conv_full, norm_mix, q_norm, k_norm, norm_ffn,
                                             start_exchange)

    x_, y_, c_ = _place()
    my = 4 * x_ + 2 * y_ + c_
    landed = [None] * (5 * depth)
    for n_, (idx, send_sems, recv_sems, thru, lands) in enumerate(in_flight):
        for i, arr in zip(idx, _exchange_wait(send_sems, recv_sems, thru, lands, grad_x, f"exchange_wait_{n_}")):
            landed[i] = arr
    rest = [i for i in range(5 * depth) if landed[i] is None]
    send = [grads[i // 5][i % 5][1].reshape(N_DEV, block_rows[i % 5], -1) for i in rest]
    for i, arr in zip(rest, _exchange_blocks(send, "exchange_grads")):
        landed[i] = arr
    own = [lax.dynamic_index_in_dim(grads[i // 5][i % 5][0].reshape(N_DEV, block_rows[i % 5], -1), my, 0,
                                    keepdims=False) for i in range(5 * depth)]

    nconv = N_DEV * conv_shard
    rows = []
    for l in range(depth):
        g_mix, g_ffn, g_q, g_k, g_conv = small[l]
        qkrow = _pad_to(jnp.concatenate([g_q, g_k]), 0, d)
        rows += [g_mix[None], g_ffn[None], qkrow[None], _pad_to(g_conv, 1, d)]
    nrow = 6 * depth
    packed = jnp.concatenate(rows + [_pad_to(loss[:1], 1, d)], axis=0)
    packed = _pad_to(packed, 0, ((nrow + 1 + 7) // 8) * 8)
    summed = _all_reduce_small(packed, "reduce_small")
    loss_out = summed[nrow, 0]

    def big(i, w, m, v, tr, name, rows_=None, cols_=None):
        parts = landed[i]
        pr, pcn = parts.shape[1], parts.shape[2]
        w, m, v = [_pad_to(_pad_to(t, 0, pr), 1, pcn) for t in (w, m, v)]
        outs = _adamw(parts, own[i], w, m, v, tr, name)
        return [o[:rows_ or pr, :cols_ or pcn] for o in outs]

    res = {}
    for l in range(depth):
        i = 5 * l
        res[("w_in", l)] = big(i, w_in[l], m_w_in[l], v_w_in[l], 256, f"adamw_in_{l}")
        res[("w_gate", l)] = big(i + 1, w_gate[l], m_w_gate[l], v_w_gate[l], 256, f"adamw_gate_{l}", cols_=ff_shard)
        res[("w_up", l)] = big(i + 2, w_up[l], m_w_up[l], v_w_up[l], 256, f"adamw_up_{l}", cols_=ff_shard)
        res[("w_out", l)] = big(i + 3, w_out[l], m_w_out[l], v_w_out[l], w_out.shape[1], f"adamw_out_{l}")
        res[("w_down", l)] = big(i + 4, w_down[l], m_w_down[l], v_w_down[l], 128, f"adamw_down_{l}",
                                 rows_=ff_shard)

    g_rows, w_rows, m_rows, v_rows = [], [], [], []
    for l in range(depth):
        base = l * 6
        conv_g = lax.dynamic_slice(summed[base + 3:base + 6], (0, my * conv_shard), (3, conv_shard))
        g_rows += [summed[base:base + 3], _pad_to(conv_g, 1, d)]
        for dst, (nm, qn, kn, nf, cw) in ((w_rows, (norm_mix, q_norm, k_norm, norm_ffn, conv_w)),
                                          (m_rows, (m_norm_mix, m_q_norm, m_k_norm, m_norm_ffn, m_conv_w)),
                                          (v_rows, (v_norm_mix, v_q_norm, v_k_norm, v_norm_ffn, v_conv_w))):
            dst += [nm[l][None], nf[l][None], _pad_to(jnp.concatenate([qn[l], kn[l]]), 0, d)[None],
                    _pad_to(cw[l], 1, d)]
    prow = ((nrow + 7) // 8) * 8
    gs, ws, ms, vs = [_pad_to(jnp.concatenate(t, axis=0), 0, prow) for t in (g_rows, w_rows, m_rows, v_rows)]
    sm = _adamw(gs[None], None, ws, ms, vs, prow, "adamw_small")

    hd = q_norm.shape[1]

    def small_out(t, kind):
        per_layer = []
        for l in range(depth):
            base = l * 6
            per_layer.append({"norm_mix": t[base], "norm_ffn": t[base + 1], "q_norm": t[base + 2, :hd],
                              "k_norm": t[base + 2, hd:2 * hd], "conv_w": t[base + 3:base + 6, :conv_shard]}[kind])
        return jnp.stack(per_layer)

    def big_out(name, i):
        return jnp.stack([res[(name, l)][i] for l in range(depth)])

    outs = [loss_out, grad_x.reshape(x.shape)]
    for i in range(4):
        outs += [small_out(sm[i], "norm_mix"), big_out("w_in", i), small_out(sm[i], "q_norm"),
                 small_out(sm[i], "k_norm"), small_out(sm[i], "conv_w"), big_out("w_out", i),
                 small_out(sm[i], "norm_ffn"), big_out("w_gate", i), big_out("w_up", i), big_out("w_down", i)]
    return tuple(outs)
```

```python
import jax
import jax.numpy as jnp
from jax import lax
from jax.experimental import pallas as pl
from jax.experimental.pallas import tpu as pltpu

F32 = jnp.float32
BF16 = jnp.bfloat16
MESH = pl.DeviceIdType.MESH

N_DEV = 8
LANES = 128
HEAD_DIM = 64
KEY_CHUNK = 128
EPS = 1e-6
VMEM_LIMIT = 48 * 1024 * 1024

ADAM_LR = 0.001
ADAM_B1 = 0.9
ADAM_B2 = 0.999
ADAM_EPS = 1e-08
ADAM_WD = 0.01
ADAM_STEP = 10

NN = (((1,), (0,)), ((), ()))
NT = (((1,), (1,)), ((), ()))
TN = (((0,), (0,)), ((), ()))


def _dot(a, b, dims):
    return lax.dot_general(a.astype(BF16), b.astype(BF16), dims, preferred_element_type=F32)


def _cparams(*sem):
    return pltpu.CompilerParams(dimension_semantics=sem, vmem_limit_bytes=VMEM_LIMIT)


def _split_hi_lo(v):
    hi = v.astype(BF16)
    lo = (v - hi.astype(F32)).astype(BF16)
    return jnp.concatenate([hi, lo], axis=1)


def _rmsnorm_fwd(x, gain, tm, name):
    s, d = x.shape

    def body(x_ref, g_ref, o_ref):
        xv = x_ref[...]
        r = lax.rsqrt(jnp.mean(xv * xv, axis=-1, keepdims=True) + EPS)
        o_ref[...] = ((xv * r) * g_ref[...]).astype(o_ref.dtype)

    return pl.pallas_call(
        body, name=name, grid=(s // tm,),
        in_specs=[pl.BlockSpec((tm, d), lambda i: (i, 0)), pl.BlockSpec((1, d), lambda i: (0, 0))],
        out_specs=pl.BlockSpec((tm, d), lambda i: (i, 0)),
        out_shape=jax.ShapeDtypeStruct((s, d), BF16),
        compiler_params=_cparams("parallel"),
    )(x, gain)


def _rmsnorm_bwd(dh, x, gain, dres, tm, name):
    s, d = x.shape
    nsteps = s // tm

    def body(dh_ref, x_ref, g_ref, dres_ref, dx_ref, dxb_ref, dg_ref):
        i = pl.program_id(0)
        xv = x_ref[...]
        r = lax.rsqrt(jnp.mean(xv * xv, axis=-1, keepdims=True) + EPS)
        xhat = xv * r
        dhv = dh_ref[...]
        dxh = dhv * g_ref[...]
        proj = jnp.mean(dxh * xhat, axis=-1, keepdims=True)
        dxv = dres_ref[...] + r * (dxh - xhat * proj)
        dx_ref[...] = dxv
        dxb_ref[...] = dxv.astype(dxb_ref.dtype)
        part = jnp.sum((dhv * xhat).reshape(tm // 8, 8, d), axis=0)

        @pl.when(i == 0)
        def _():
            dg_ref[...] = part

        @pl.when(i > 0)
        def _():
            dg_ref[...] += part

        @pl.when(i == nsteps - 1)
        def _():
            dg_ref[...] = jnp.broadcast_to(jnp.sum(dg_ref[...], axis=0, keepdims=True), (8, d))

    row = pl.BlockSpec((tm, d), lambda i: (i, 0))
    return pl.pallas_call(
        body, name=name, grid=(nsteps,),
        in_specs=[row, row, pl.BlockSpec((1, d), lambda i: (0, 0)), row],
        out_specs=[row, row, pl.BlockSpec((8, d), lambda i: (0, 0))],
        out_shape=[jax.ShapeDtypeStruct((s, d), F32), jax.ShapeDtypeStruct((s, d), BF16),
                   jax.ShapeDtypeStruct((8, d), F32)],
        compiler_params=_cparams("arbitrary"),
    )(dh, x, gain, dres)


def _group_mean_matrix():
    r = lax.broadcasted_iota(jnp.int32, (LANES, LANES), 0) // HEAD_DIM
    c = lax.broadcasted_iota(jnp.int32, (LANES, LANES), 1) // HEAD_DIM
    return jnp.where(r == c, 1.0 / HEAD_DIM, 0.0).astype(BF16)


def _group_mean(v, gm):
    hi = v.astype(BF16)
    lo = (v - hi.astype(F32)).astype(BF16)
    return _dot(hi, gm, NN) + _dot(lo, gm, NN)


def _qknorm_fwd(proj, gains, tm, name):
    s = proj.shape[0]
    ncol = gains.shape[1] // LANES

    def body(p_ref, g_ref, gm_ref, o_ref):
        xv = p_ref[...].astype(F32)
        r = lax.rsqrt(_group_mean(xv * xv, gm_ref[...]) + EPS)
        o_ref[...] = ((xv * r) * g_ref[...]).astype(o_ref.dtype)

    blk = pl.BlockSpec((tm, LANES), lambda i, j: (i, j))
    return pl.pallas_call(
        body, name=name, grid=(s // tm, ncol),
        in_specs=[blk, pl.BlockSpec((1, LANES), lambda i, j: (0, j)),
                  pl.BlockSpec((LANES, LANES), lambda i, j: (0, 0))],
        out_specs=blk,
        out_shape=jax.ShapeDtypeStruct((s, ncol * LANES), BF16),
        compiler_params=_cparams("parallel", "parallel"),
    )(proj, gains, _group_mean_matrix())


def _qknorm_bwd(dqk, proj, gains, tm, name):
    s = proj.shape[0]
    ncol = gains.shape[1] // LANES
    nsteps = s // tm

    def body(dy_ref, p_ref, g_ref, gm_ref, dx_ref, dg_ref):
        i = pl.program_id(1)
        gm = gm_ref[...]
        xv = p_ref[...].astype(F32)
        r = lax.rsqrt(_group_mean(xv * xv, gm) + EPS)
        xhat = xv * r
        dy = dy_ref[...]
        dxh = dy * g_ref[...]
        proj_ = _group_mean(dxh * xhat, gm)
        dx_ref[...] = (r * (dxh - xhat * proj_)).astype(dx_ref.dtype)
        part = jnp.sum((dy * xhat).reshape(tm // 8, 8, LANES), axis=0)

        @pl.when(i == 0)
        def _():
            dg_ref[...] = part

        @pl.when(i > 0)
        def _():
            dg_ref[...] += part

        @pl.when(i == nsteps - 1)
        def _():
            dg_ref[...] = jnp.broadcast_to(jnp.sum(dg_ref[...], axis=0, keepdims=True), (8, LANES))

    blk = pl.BlockSpec((tm, LANES), lambda j, i: (i, j))
    return pl.pallas_call(
        body, name=name, grid=(ncol, nsteps),
        in_specs=[blk, blk, pl.BlockSpec((1, LANES), lambda j, i: (0, j)),
                  pl.BlockSpec((LANES, LANES), lambda j, i: (0, 0))],
        out_specs=[blk, pl.BlockSpec((8, LANES), lambda j, i: (0, j))],
        out_shape=[jax.ShapeDtypeStruct((s, ncol * LANES), BF16),
                   jax.ShapeDtypeStruct((8, ncol * LANES), F32)],
        compiler_params=_cparams("parallel", "arbitrary"),
    )(dqk, proj, gains, _group_mean_matrix())


CONV_ROWS = 256
HALO = 8


def _conv_fwd(proj, conv_w8, name):
    s = proj.shape[0]
    nblk = conv_w8.shape[1] // LANES
    first = 3 * nblk
    nchunk = s // CONV_ROWS

    def body(cb_ref, cc_ref, cu_ref, w_ref, y_ref, hpad):
        hpad[pl.ds(0, 2 * HALO), :] = jnp.zeros((2 * HALO, LANES), F32)

        def fill(i, _):
            r0 = pl.multiple_of(i * CONV_ROWS, CONV_ROWS)
            hpad[pl.ds(r0 + 2 * HALO, CONV_ROWS), :] = (
                cc_ref[pl.ds(r0, CONV_ROWS), :].astype(F32) * cu_ref[pl.ds(r0, CONV_ROWS), :].astype(F32))
            return 0

        lax.fori_loop(0, nchunk, fill, 0)
        w0, w1, w2 = w_ref[0:1, :], w_ref[1:2, :], w_ref[2:3, :]

        def conv(i, _):
            r0 = pl.multiple_of(i * CONV_ROWS, CONV_ROWS)
            win = hpad[pl.ds(r0 + HALO, CONV_ROWS + HALO), :]
            c = (w2 * win[HALO:] + w1 * pltpu.roll(win, 1, 0)[HALO:] + w0 * pltpu.roll(win, 2, 0)[HALO:])
            y_ref[pl.ds(r0, CONV_ROWS), :] = (cb_ref[pl.ds(r0, CONV_ROWS), :].astype(F32) * c).astype(y_ref.dtype)
            return 0

        lax.fori_loop(0, nchunk, conv, 0)

    def col(off):
        return pl.BlockSpec((s, LANES), lambda j: (0, off + j))

    return pl.pallas_call(
        body, name=name, grid=(nblk,),
        in_specs=[col(first), col(first + nblk), col(first + 2 * nblk), pl.BlockSpec((8, LANES), lambda j: (0, j))],
        out_specs=pl.BlockSpec((s, LANES), lambda j: (0, j)),
        out_shape=jax.ShapeDtypeStruct((s, nblk * LANES), BF16),
        scratch_shapes=[pltpu.VMEM((s + 2 * HALO, LANES), F32)],
        compiler_params=_cparams("parallel"),
    )(proj, proj, proj, conv_w8)


def _conv_bwd(dmix, proj, conv_w8, name):
    s = proj.shape[0]
    nblk = conv_w8.shape[1] // LANES
    first = 3 * nblk
    nchunk = s // CONV_ROWS

    def body(dy_ref, cb_ref, cc_ref, cu_ref, w_ref, dcb_ref, dcc_ref, dcu_ref, dw_ref, hpad, dcpad):
        hpad[pl.ds(0, 2 * HALO), :] = jnp.zeros((2 * HALO, LANES), F32)
        dcpad[pl.ds(s, 2 * HALO), :] = jnp.zeros((2 * HALO, LANES), F32)

        def fill(i, _):
            r0 = pl.multiple_of(i * CONV_ROWS, CONV_ROWS)
            hpad[pl.ds(r0 + 2 * HALO, CONV_ROWS), :] = (
                cc_ref[pl.ds(r0, CONV_ROWS), :].astype(F32) * cu_ref[pl.ds(r0, CONV_ROWS), :].astype(F32))
            return 0

        lax.fori_loop(0, nchunk, fill, 0)
        w0, w1, w2 = w_ref[0:1, :], w_ref[1:2, :], w_ref[2:3, :]

        def fold(v):
            return jnp.sum(v.reshape(CONV_ROWS // 8, 8, LANES), axis=0)

        def first_pass(i, acc):
            a0, a1, a2 = acc
            r0 = pl.multiple_of(i * CONV_ROWS, CONV_ROWS)
            win = hpad[pl.ds(r0 + HALO, CONV_ROWS + HALO), :]
            h0 = win[HALO:]
            h1 = pltpu.roll(win, 1, 0)[HALO:]
            h2 = pltpu.roll(win, 2, 0)[HALO:]
            c = w2 * h0 + w1 * h1 + w0 * h2
            dy = dy_ref[pl.ds(r0, CONV_ROWS), :]
            dcb_ref[pl.ds(r0, CONV_ROWS), :] = (dy * c).astype(dcb_ref.dtype)
            dc = dy * cb_ref[pl.ds(r0, CONV_ROWS), :].astype(F32)
            dcpad[pl.ds(r0, CONV_ROWS), :] = dc
            return a0 + fold(dc * h2), a1 + fold(dc * h1), a2 + fold(dc * h0)

        z8 = jnp.zeros((8, LANES), F32)
        a0, a1, a2 = lax.fori_loop(0, nchunk, first_pass, (z8, z8, z8))
        dw_ref[...] = jnp.concatenate(
            [jnp.sum(a0, axis=0, keepdims=True), jnp.sum(a1, axis=0, keepdims=True),
             jnp.sum(a2, axis=0, keepdims=True), jnp.zeros((5, LANES), F32)], axis=0)

        def second_pass(i, _):
            r0 = pl.multiple_of(i * CONV_ROWS, CONV_ROWS)
            win = dcpad[pl.ds(r0, CONV_ROWS + HALO), :]
            n = CONV_ROWS + HALO
            dh = (w2 * win[:CONV_ROWS] + w1 * pltpu.roll(win, n - 1, 0)[:CONV_ROWS]
                  + w0 * pltpu.roll(win, n - 2, 0)[:CONV_ROWS])
            dcc_ref[pl.ds(r0, CONV_ROWS), :] = (dh * cu_ref[pl.ds(r0, CONV_ROWS), :].astype(F32)).astype(dcc_ref.dtype)
            dcu_ref[pl.ds(r0, CONV_ROWS), :] = (dh * cc_ref[pl.ds(r0, CONV_ROWS), :].astype(F32)).astype(dcu_ref.dtype)
            return 0

        lax.fori_loop(0, nchunk, second_pass, 0)

    def col(off):
        return pl.BlockSpec((s, LANES), lambda j: (0, off + j))

    out = pl.BlockSpec((s, LANES), lambda j: (0, j))
    return pl.pallas_call(
        body, name=name, grid=(nblk,),
        in_specs=[col(nblk), col(first), col(first + nblk), col(first + 2 * nblk),
                  pl.BlockSpec((8, LANES), lambda j: (0, j))],
        out_specs=[out, out, out, pl.BlockSpec((8, LANES), lambda j: (0, j))],
        out_shape=[jax.ShapeDtypeStruct((s, nblk * LANES), BF16)] * 3 + [jax.ShapeDtypeStruct((8, nblk * LANES), F32)],
        scratch_shapes=[pltpu.VMEM((s + 2 * HALO, LANES), F32), pltpu.VMEM((s + 2 * HALO, LANES), F32)],
        compiler_params=_cparams("parallel"),
    )(dmix, proj, proj, proj, conv_w8)


LOG2E = 1.4426950408889634
LN2 = 0.6931471805599453
NEG_BIG = -1e30


def _cumsum_matrix(kind):
    j = lax.broadcasted_iota(jnp.int32, (KEY_CHUNK, 2 * KEY_CHUNK), 0)
    c = lax.broadcasted_iota(jnp.int32, (KEY_CHUNK, 2 * KEY_CHUNK), 1)
    tri = {"after": j > c, "upto": j <= c, "before": j < c}[kind]
    return jnp.where((c >= KEY_CHUNK) | tri, 1.0, 0.0).astype(BF16)


def _stack_heads(t, m0):
    zero = jnp.zeros_like(t)
    return jnp.concatenate([jnp.where(m0, t, zero), jnp.where(m0, zero, t)], axis=0)


def _softplus2(z):
    sp = jnp.maximum(z, 0.0) + jnp.log2(1.0 + jnp.exp2(-jnp.abs(z)))
    return sp, z - sp


def _key_chunk(ref, kc):
    return ref[pl.ds(pl.multiple_of(kc * KEY_CHUNK, KEY_CHUNK), KEY_CHUNK), :]


def _attn_bwd(qk, proj, dmix, rtot, tq, name):
    s = qk.shape[0]
    nhp = qk.shape[1] // (2 * LANES)
    nc = tq // KEY_CHUNK

    def body(q_ref, k_ref, v_ref, do_ref, r_ref, cmi_ref, cme_ref, bias_ref, dq_ref, dk_ref, dv_ref,
             z_refs, ls_refs, sig_refs, sp_refs, gb_refs, pr_ref, gs_ref):
        qi = pl.program_id(1)

        @pl.when(qi == 0)
        def _():
            dk_ref[...] = jnp.zeros_like(dk_ref)
            dv_ref[...] = jnp.zeros_like(dv_ref)

        nslots = (qi + 1) * nc
        m0 = lax.broadcasted_iota(jnp.int32, (1, LANES), 1) < HEAD_DIM
        qs = _stack_heads(q_ref[...], m0)
        do = do_ref[...]
        dos = _stack_heads(do.astype(BF16), m0)
        dosl = _stack_heads((do * LN2).astype(BF16), m0)
        cmi = cmi_ref[...]
        cme = cme_ref[...]

        def chunk_at(i):
            return jnp.clip(i, 0, nslots - 1)

        def scores(kc):
            return _dot(qs, _key_chunk(k_ref, kc), NT)

        def weights(ls, cs, da, pr, kc):
            a = jnp.exp2(ls - (pr - cs[:, :KEY_CHUNK]))
            gb = (a * da).astype(BF16)
            ks = pl.multiple_of(kc * KEY_CHUNK, KEY_CHUNK)
            dv_ref[pl.ds(ks, KEY_CHUNK), :] += _dot(a, dos, TN)
            return gb, jnp.exp2(ls), pr - cs[:, KEY_CHUNK:]

        def score_grads(gb, sig, cg, gs, dq, kc):
            dzb = (gb.astype(F32) * (1.0 - sig) - sig * (gs + cg[:, :KEY_CHUNK])).astype(BF16)
            ks = pl.multiple_of(kc * KEY_CHUNK, KEY_CHUNK)
            dk_ref[pl.ds(ks, KEY_CHUNK), :] += _dot(dzb, qs, TN)
            dq = dq + _dot(jnp.concatenate([dzb[:tq], dzb[tq:]], axis=1), _stack_heads(_key_chunk(k_ref, kc), m0), NN)
            return gs + cg[:, KEY_CHUNK:], dq

        def step(i, par, bias=None):
            cur, prv = par, 1 - par
            k1, k2 = chunk_at(i - 1), chunk_at(i - 2)
            z_next = scores(chunk_at(i + 1))
            cs = _dot(sp_refs[prv][...], cmi, NN)
            da = _dot(dosl, _key_chunk(v_ref, k1), NT)
            cg = _dot(gb_refs[cur][...], cme, NN)
            z = z_refs[cur][...]
            if bias is not None:
                z = z + bias
            sp, ls = _softplus2(z)
            sp_refs[cur][...] = sp.astype(BF16)
            ls_refs[cur][...] = ls
            gs, dq = score_grads(gb_refs[cur][...], sig_refs[cur][...], cg, gs_ref[...], dq_ref[...], k2)
            gs_ref[...] = gs
            dq_ref[...] = dq
            gb, sig, pr = weights(ls_refs[prv][...], cs, da, pr_ref[...], k1)
            gb_refs[prv][...] = gb
            sig_refs[prv][...] = sig
            pr_ref[...] = pr
            z_refs[prv][...] = z_next

        pr_ref[...] = jnp.concatenate([r_ref[:, :LANES], r_ref[:, LANES:]], axis=0)
        gs_ref[...] = jnp.zeros((2 * tq, LANES), F32)
        dq_ref[...] = jnp.zeros((tq, LANES), F32)
        z_refs[0][...] = scores(chunk_at(0))
        sp_refs[1][...] = jnp.zeros((2 * tq, LANES), BF16)
        ls_refs[1][...] = jnp.full((2 * tq, LANES), NEG_BIG, F32)
        gb_refs[0][...] = jnp.zeros((2 * tq, LANES), BF16)
        sig_refs[0][...] = jnp.zeros((2 * tq, LANES), F32)

        def two_steps(j, _):
            step(2 * j, 0)
            step(2 * j + 1, 1)
            return 0

        lax.fori_loop(0, nslots // 2 - 1, two_steps, 0)
        step(nslots - 2, 0, bias_ref[0])
        step(nslots - 1, 1, bias_ref[1])
        k1, k2 = chunk_at(nslots - 1), chunk_at(nslots - 2)
        gb, sig, _ = weights(ls_refs[1][...], _dot(sp_refs[1][...], cmi, NN),
                             _dot(dosl, _key_chunk(v_ref, k1), NT), pr_ref[...], k1)
        gb2 = gb_refs[0][...]
        gs, dq = score_grads(gb2, sig_refs[0][...], _dot(gb2, cme, NN), gs_ref[...], dq_ref[...], k2)
        _, dq = score_grads(gb, sig, _dot(gb, cme, NN), gs, dq, k1)
        dq_ref[...] = dq

    def wrapped(q_ref, k_ref, v_ref, do_ref, r_ref, cmi_ref, cme_ref, bias_ref, dq_ref, dk_ref, dv_ref,
                z0, z1, ls0, ls1, sg0, sg1, sp0, sp1, gb0, gb1, pr_ref, gs_ref):
        body(q_ref, k_ref, v_ref, do_ref, r_ref, cmi_ref, cme_ref, bias_ref, dq_ref, dk_ref, dv_ref,
             (z0, z1), (ls0, ls1), (sg0, sg1), (sp0, sp1), (gb0, gb1), pr_ref, gs_ref)

    assert nc == 2
    bias = _diag_bias(tq, True)
    bias = jnp.concatenate([bias[:, :, :KEY_CHUNK], bias[:, :, KEY_CHUNK:]], axis=1)
    qblk = pl.BlockSpec((tq, LANES), lambda p, i: (i, p))
    full = pl.BlockSpec((s, LANES), lambda p, i: (0, p))
    cmspec = pl.BlockSpec((KEY_CHUNK, 2 * KEY_CHUNK), lambda p, i: (0, 0))
    shape = jax.ShapeDtypeStruct((s, nhp * LANES), F32)
    f32buf = pltpu.VMEM((2 * tq, LANES), F32)
    bf16buf = pltpu.VMEM((2 * tq, LANES), BF16)
    return pl.pallas_call(
        wrapped, name=name, grid=(nhp, s // tq),
        in_specs=[qblk,
                  pl.BlockSpec((s, LANES), lambda p, i: (0, nhp + p)),
                  pl.BlockSpec((s, LANES), lambda p, i: (0, 2 * nhp + p)),
                  qblk,
                  pl.BlockSpec((tq, 2 * LANES), lambda p, i: (i, p)),
                  cmspec, cmspec,
                  pl.BlockSpec((nc, 2 * tq, LANES), lambda p, i: (0, 0, 0))],
        out_specs=[qblk, full, full],
        out_shape=[shape, shape, shape],
        scratch_shapes=[f32buf] * 6 + [bf16buf] * 4 + [f32buf] * 2,
        compiler_params=_cparams("parallel", "arbitrary"),
    )(qk, qk, proj, dmix, rtot, _cumsum_matrix("upto"), _cumsum_matrix("before"), bias)


def _pair_cumsum_matrix(kind):
    j = lax.broadcasted_iota(jnp.int32, (2 * KEY_CHUNK, 4 * KEY_CHUNK), 0)
    c = lax.broadcasted_iota(jnp.int32, (2 * KEY_CHUNK, 4 * KEY_CHUNK), 1)
    same_head = (j // KEY_CHUNK) == ((c // KEY_CHUNK) % 2)
    jj, cc = j % KEY_CHUNK, c % KEY_CHUNK
    tri = {"after": jj > cc, "upto": jj <= cc, "before": jj < cc}[kind]
    return jnp.where(same_head & ((c >= 2 * KEY_CHUNK) | tri), 1.0, 0.0).astype(BF16)


def _diag_bias(tq, ascending):
    nc = tq // KEY_CHUNK
    shape = (nc, tq, 2 * KEY_CHUNK)
    d = lax.broadcasted_iota(jnp.int32, shape, 0)
    r = lax.broadcasted_iota(jnp.int32, shape, 1)
    c = lax.broadcasted_iota(jnp.int32, shape, 2) % KEY_CHUNK
    chunk = d if ascending else nc - 1 - d
    return jnp.where(chunk * KEY_CHUNK + c < r, 0.0, NEG_BIG).astype(F32)


def _attn_fwd(qk, proj, tq, name):
    s = qk.shape[0]
    nhp = qk.shape[1] // (2 * LANES)
    nc = tq // KEY_CHUNK
    assert nc == 2
    w = 2 * KEY_CHUNK

    def body(q_ref, k_ref, v_ref, cm_ref, bias_ref, o_ref, r_ref, z_refs, ls_refs, cs_refs, ct_refs, sp_refs,
             ab_refs, acc_ref):
        qi = pl.program_id(1)
        nslots = (qi + 1) * nc
        m0 = lax.broadcasted_iota(jnp.int32, (1, LANES), 1) < HEAD_DIM
        q = q_ref[...]
        cm = cm_ref[...]

        def chunk_at(i):
            return jnp.clip(nslots - 1 - i, 0, nslots - 1)

        def scores(kc):
            return _dot(q, _stack_heads(_key_chunk(k_ref, kc), m0), NT)

        def values(ab, kc):
            return _dot(ab, _stack_heads(_key_chunk(v_ref, kc), m0), NN)

        def step(i, par, bias=None, stages="zscwv"):
            cur, prv = par, 1 - par
            if "z" in stages:
                z_next = scores(chunk_at(i + 1))
            if "c" in stages:
                cs = _dot(sp_refs[prv][...], cm, NN)
            if "v" in stages:
                pv = values(ab_refs[prv][...], chunk_at(i - 3))
            if "w" in stages:
                rs = r_ref[...]
                r_ref[...] = rs + ct_refs[cur][...]
                ab_refs[cur][...] = jnp.exp2(ls_refs[cur][...] - cs_refs[cur][...] - rs).astype(BF16)
            if "s" in stages:
                z = z_refs[cur][...]
                if bias is not None:
                    z = z + bias
                sp, ls = _softplus2(z)
                sp_refs[cur][...] = sp.astype(BF16)
                ls_refs[cur][...] = ls
            if "v" in stages:
                acc_ref[...] += pv
            if "c" in stages:
                cs_refs[prv][...] = cs[:, :w]
                ct_refs[prv][...] = cs[:, w:]
            if "z" in stages:
                z_refs[prv][...] = z_next

        z_refs[0][...] = scores(chunk_at(0))
        for p in range(2):
            sp_refs[p][...] = jnp.zeros((tq, w), BF16)
            ls_refs[p][...] = jnp.full((tq, w), NEG_BIG, F32)
            cs_refs[p][...] = jnp.zeros((tq, w), F32)
            ct_refs[p][...] = jnp.zeros((tq, w), F32)
            ab_refs[p][...] = jnp.zeros((tq, w), BF16)
        r_ref[...] = jnp.zeros((tq, w), F32)
        acc_ref[...] = jnp.zeros((tq, LANES), F32)
        step(0, 0, bias_ref[0])
        step(1, 1, bias_ref[1])

        def two_steps(j, _):
            step(2 * j, 0)
            step(2 * j + 1, 1)
            return 0

        lax.fori_loop(1, nslots // 2, two_steps, 0)
        step(nslots, 0, stages="cwv")
        step(nslots + 1, 1, stages="wv")
        step(nslots + 2, 0, stages="v")
        o_ref[...] = acc_ref[...].astype(o_ref.dtype)

    def wrapped(q_ref, k_ref, v_ref, cm_ref, bias_ref, o_ref, r_ref, *scratch):
        z, ls, cs, ct, sp, ab = [scratch[2 * j:2 * j + 2] for j in range(6)]
        body(q_ref, k_ref, v_ref, cm_ref, bias_ref, o_ref, r_ref, z, ls, cs, ct, sp, ab, scratch[12])

    f32buf = pltpu.VMEM((tq, w), F32)
    bf16buf = pltpu.VMEM((tq, w), BF16)
    return pl.pallas_call(
        wrapped, name=name, grid=(nhp, s // tq),
        in_specs=[pl.BlockSpec((tq, LANES), lambda p, i: (i, p)),
                  pl.BlockSpec((s, LANES), lambda p, i: (0, nhp + p)),
                  pl.BlockSpec((s, LANES), lambda p, i: (0, 2 * nhp + p)),
                  pl.BlockSpec((w, 2 * w), lambda p, i: (0, 0)),
                  pl.BlockSpec((nc, tq, w), lambda p, i: (0, 0, 0))],
        out_specs=[pl.BlockSpec((tq, LANES), lambda p, i: (i, p)),
                   pl.BlockSpec((tq, w), lambda p, i: (i, p))],
        out_shape=[jax.ShapeDtypeStruct((s, nhp * LANES), BF16),
                   jax.ShapeDtypeStruct((s, nhp * w), F32)],
        scratch_shapes=[f32buf] * 8 + [bf16buf] * 4 + [pltpu.VMEM((tq, LANES), F32)],
        compiler_params=_cparams("parallel", "parallel"),
    )(qk, qk, proj, _pair_cumsum_matrix("after"), _diag_bias(tq, False))


def _mm_blocks(h, ga, widx, tm, name):
    s, d = h.shape
    nb, cols = ga.shape[1], ga.shape[3]

    def body(a_ref, b_ref, o_ref):
        o_ref[...] = _dot(a_ref[...], b_ref[...], NN).astype(o_ref.dtype)

    return pl.pallas_call(
        body, name=name, grid=(s // tm, nb),
        in_specs=[pl.BlockSpec((tm, d), lambda i, j: (i, 0)),
                  pl.BlockSpec((None, None, d, cols), lambda i, j: (widx, j, 0, 0))],
        out_specs=pl.BlockSpec((tm, cols), lambda i, j: (i, j)),
        out_shape=jax.ShapeDtypeStruct((s, nb * cols), BF16),
        compiler_params=_cparams("parallel", "arbitrary"),
    )(h, ga)


def _mm_swiglu(h, ga, gidx, uidx, tm, name):
    s, d = h.shape
    nb, cols = ga.shape[1], ga.shape[3]

    def body(a_ref, bg_ref, bu_ref, g_ref, u_ref, act_ref):
        a = a_ref[...]
        g = _dot(a, bg_ref[...], NN)
        u = _dot(a, bu_ref[...], NN)
        g_ref[...] = g.astype(g_ref.dtype)
        u_ref[...] = u.astype(u_ref.dtype)
        act_ref[...] = (g * (1.0 / (1.0 + jnp.exp(-g))) * u).astype(act_ref.dtype)

    def wspec(idx):
        return pl.BlockSpec((None, None, d, cols), lambda i, j: (idx, j, 0, 0))

    out = pl.BlockSpec((tm, cols), lambda i, j: (i, j))
    shape = jax.ShapeDtypeStruct((s, nb * cols), BF16)
    return pl.pallas_call(
        body, name=name, grid=(s // tm, nb),
        in_specs=[pl.BlockSpec((tm, d), lambda i, j: (i, 0)), wspec(gidx), wspec(uidx)],
        out_specs=[out, out, out], out_shape=[shape, shape, shape],
        compiler_params=_cparams("parallel", "arbitrary"),
    )(h, ga, ga)


def _mm_residual(a, w3, lidx, res, tm, tn, name):
    s, k = a.shape
    n = w3.shape[2]

    def body(a_ref, b_ref, r_ref, o_ref):
        o_ref[...] = r_ref[...] + _dot(a_ref[...], b_ref[...], NN)

    return pl.pallas_call(
        body, name=name, grid=(s // tm, n // tn),
        in_specs=[pl.BlockSpec((tm, k), lambda i, j: (i, 0)),
                  pl.BlockSpec((None, k, tn), lambda i, j: (lidx, 0, j)),
                  pl.BlockSpec((tm, tn), lambda i, j: (i, j))],
        out_specs=pl.BlockSpec((tm, tn), lambda i, j: (i, j)),
        out_shape=jax.ShapeDtypeStruct((s, n), F32),
        compiler_params=_cparams("parallel", "arbitrary"),
    )(a, w3, res)


def _mm_nt(a, w3, lidx, tm, tn, name):
    s, k = a.shape
    n = w3.shape[1]

    def body(a_ref, b_ref, o_ref):
        o_ref[...] = _dot(a_ref[...], b_ref[...], NT)

    return pl.pallas_call(
        body, name=name, grid=(s // tm, n // tn),
        in_specs=[pl.BlockSpec((tm, k), lambda i, j: (i, 0)),
                  pl.BlockSpec((None, tn, k), lambda i, j: (lidx, j, 0))],
        out_specs=pl.BlockSpec((tm, tn), lambda i, j: (i, j)),
        out_shape=jax.ShapeDtypeStruct((s, n), F32),
        compiler_params=_cparams("parallel", "arbitrary"),
    )(a, w3)


def _mm_nt_swiglu_bwd(dx, wd3, lidx, g, u, tm, name):
    s, d = dx.shape
    cols = g.shape[1] // N_DEV

    def body(a_ref, b_ref, g_ref, u_ref, dg_ref, du_ref):
        dact = _dot(a_ref[...], b_ref[...], NT)
        gv = g_ref[...].astype(F32)
        sig = 1.0 / (1.0 + jnp.exp(-gv))
        du_ref[...] = (dact * (gv * sig)).astype(du_ref.dtype)
        dg_ref[...] = (dact * u_ref[...].astype(F32) * (sig * (1.0 + gv * (1.0 - sig)))).astype(dg_ref.dtype)

    blk = pl.BlockSpec((tm, cols), lambda i, j: (i, j))
    shape = jax.ShapeDtypeStruct(g.shape, BF16)
    return pl.pallas_call(
        body, name=name, grid=(s // tm, N_DEV),
        in_specs=[pl.BlockSpec((tm, d), lambda i, j: (i, 0)),
                  pl.BlockSpec((None, cols, d), lambda i, j: (lidx, j, 0)), blk, blk],
        out_specs=[blk, blk], out_shape=[shape, shape],
        compiler_params=_cparams("parallel", "arbitrary"),
    )(dx, wd3, g, u)


def _mm_nt_blocks(das, ga, widxs, tm, name):
    s = das[0].shape[0]
    nb, d, cols = ga.shape[1], ga.shape[2], ga.shape[3]
    nw = len(das)

    def body(*refs):
        a_refs, b_refs, o_ref = refs[:nw], refs[nw:2 * nw], refs[2 * nw]
        acc = None
        for w in range(nw):
            for k in range(nb):
                part = _dot(a_refs[w][:, k * cols:(k + 1) * cols], b_refs[w][k], NT)
                acc = part if acc is None else acc + part
        o_ref[...] = acc

    def wspec(idx):
        return pl.BlockSpec((None, nb, d, cols), lambda i: (idx, 0, 0, 0), pipeline_mode=pl.Buffered(1))

    return pl.pallas_call(
        body, name=name, grid=(s // tm,),
        in_specs=[pl.BlockSpec((tm, nb * cols), lambda i: (i, 0))] * nw + [wspec(i) for i in widxs],
        out_specs=pl.BlockSpec((tm, d), lambda i: (i, 0)),
        out_shape=jax.ShapeDtypeStruct((s, d), F32),
        compiler_params=_cparams("parallel"),
    )(*das, *([ga] * nw))


def _mm_tn(a, b, ta, tb, tk, out_blocks, name):
    s, ka = a.shape
    nb = b.shape[1]
    nk = s // tk

    def body(a_ref, b_ref, o_ref, ob_ref):
        k = pl.program_id(2)
        part = _dot(a_ref[...], b_ref[...], TN)

        @pl.when(k == 0)
        def _():
            o_ref[...] = part

        @pl.when(k > 0)
        def _():
            o_ref[...] += part

        @pl.when(k == nk - 1)
        def _():
            ob_ref[...] = o_ref[...].astype(ob_ref.dtype)

    if out_blocks:
        out_spec = pl.BlockSpec((None, ta, tb), lambda i, j, k: (j, i, 0))
        shape = (nb // tb, ka, tb)
    else:
        out_spec = pl.BlockSpec((ta, tb), lambda i, j, k: (i, j))
        shape = (ka, nb)
    return pl.pallas_call(
        body, name=name, grid=(ka // ta, nb // tb, nk),
        in_specs=[pl.BlockSpec((tk, ta), lambda i, j, k: (k, i)),
                  pl.BlockSpec((tk, tb), lambda i, j, k: (k, j))],
        out_specs=[out_spec, out_spec],
        out_shape=[jax.ShapeDtypeStruct(shape, F32), jax.ShapeDtypeStruct(shape, BF16)],
        compiler_params=_cparams("parallel", "parallel", "arbitrary"),
    )(a, b)


def _loss_head(y, target, tm, name):
    s, d = y.shape
    nsteps = s // tm

    def body(y_ref, t_ref, dy_ref, dyb_ref, l_ref, acc):
        i = pl.program_id(0)
        diff = y_ref[...] - t_ref[...]
        dy_ref[...] = diff * (1.0 / d)
        dyb_ref[...] = (diff * (1.0 / d)).astype(dyb_ref.dtype)
        part = jnp.sum((diff * diff).reshape(tm // 8, 8, d), axis=0)

        @pl.when(i == 0)
        def _():
            acc[...] = part

        @pl.when(i > 0)
        def _():
            acc[...] += part

        @pl.when(i == nsteps - 1)
        def _():
            tot = jnp.sum(jnp.sum(acc[...], axis=1, keepdims=True), axis=0, keepdims=True)
            l_ref[...] = jnp.broadcast_to(tot * (0.5 / d), (8, LANES))

    row = pl.BlockSpec((tm, d), lambda i: (i, 0))
    return pl.pallas_call(
        body, name=name, grid=(nsteps,),
        in_specs=[row, row],
        out_specs=[row, row, pl.BlockSpec((8, LANES), lambda i: (0, 0))],
        out_shape=[jax.ShapeDtypeStruct((s, d), F32), jax.ShapeDtypeStruct((s, d), BF16),
                   jax.ShapeDtypeStruct((8, LANES), F32)],
        scratch_shapes=[pltpu.VMEM((8, d), F32)],
        compiler_params=_cparams("arbitrary"),
    )(y, target)


def _adamw(parts, own, w, m, v, tr, name):
    p, rows, cols = parts.shape
    c1 = 1.0 / (1.0 - ADAM_B1 ** ADAM_STEP)
    c2 = 1.0 / (1.0 - ADAM_B2 ** ADAM_STEP)

    def body(*refs):
        if own is None:
            p_ref, w_ref, m_ref, v_ref, g_ref, d_ref, nm_ref, nv_ref = refs
            g = p_ref[0]
            for k in range(1, p):
                g = g + p_ref[k]
        else:
            p_ref, own_ref, w_ref, m_ref, v_ref, g_ref, d_ref, nm_ref, nv_ref = refs
            x, y, c = _place()
            my = 4 * x + 2 * y + c
            mine = own_ref[...]
            g = jnp.where(my == 0, mine, p_ref[0].astype(F32))
            for k in range(1, p):
                g = g + jnp.where(my == k, mine, p_ref[k].astype(F32))
        nm = ADAM_B1 * m_ref[...] + (1.0 - ADAM_B1) * g
        nv = ADAM_B2 * v_ref[...] + (1.0 - ADAM_B2) * (g * g)
        g_ref[...] = g
        nm_ref[...] = nm
        nv_ref[...] = nv
        d_ref[...] = -ADAM_LR * ((nm * c1) / (jnp.sqrt(nv * c2) + ADAM_EPS) + ADAM_WD * w_ref[...])

    blk = pl.BlockSpec((tr, cols), lambda i: (i, 0))
    shape = jax.ShapeDtypeStruct((rows, cols), F32)
    return pl.pallas_call(
        body, name=name, grid=(rows // tr,),
        in_specs=[pl.BlockSpec((p, tr, cols), lambda i: (0, i, 0))] + [blk] * (3 if own is None else 4),
        out_specs=[blk] * 4, out_shape=[shape] * 4,
        compiler_params=_cparams("parallel"),
    )(*([parts] + ([] if own is None else [own]) + [w, m, v]))


def _place():
    x, y, c = lax.axis_index("x"), lax.axis_index("y"), lax.axis_index("c")
    return x, y, c


def _all_gather(shards, name):
    na = len(shards)

    def body(*refs):
        srcs, dsts = refs[:na], refs[na:2 * na]
        send_sems, recv_sems, local_sems = refs[2 * na:]
        x, y, c = _place()
        me, sibling = (x, y, c), (x, y, 1 - c)
        chips = [(1 - x, y), (x, 1 - y), (1 - x, 1 - y)]

        def slot(a, dev):
            return dsts[a].at[:, pl.ds(4 * dev[0] + 2 * dev[1] + dev[2], 1)]

        def copy(k, a, block, to, from_shard=False):
            return pltpu.make_async_remote_copy(
                src_ref=srcs[a] if from_shard else slot(a, block), dst_ref=slot(a, block),
                send_sem=send_sems.at[k, a], recv_sem=recv_sems.at[k, a], device_id=to, device_id_type=MESH)

        mine = [pltpu.make_async_copy(srcs[a], slot(a, me), local_sems.at[a]) for a in range(na)]
        for cp in mine:
            cp.start()
        first = [copy(0, a, me, sibling, True) for a in range(na)]
        first += [copy(1 + j, a, me, (*chip, c), True) for j, chip in enumerate(chips) for a in range(na)]
        for cp in first:
            cp.start()
        passed = []
        for j, chip in enumerate(chips):
            for a in range(na):
                copy(1 + j, a, (*chip, c), me).wait_recv()
                fwd = copy(4 + j, a, (*chip, c), sibling)
                fwd.start()
                passed.append(fwd)
        for a in range(na):
            copy(0, a, sibling, me).wait_recv()
        for j, chip in enumerate(chips):
            for a in range(na):
                copy(4 + j, a, (*chip, 1 - c), me).wait_recv()
        for cp in first + passed:
            cp.wait_send()
        for cp in mine:
            cp.wait()

    anyspec = pl.BlockSpec(memory_space=pl.ANY)
    return pl.pallas_call(
        body, name=name,
        in_specs=[anyspec] * na, out_specs=[anyspec] * na,
        out_shape=[jax.ShapeDtypeStruct((a.shape[0], N_DEV) + a.shape[2:], a.dtype) for a in shards],
        scratch_shapes=[pltpu.SemaphoreType.DMA((7, na)), pltpu.SemaphoreType.DMA((7, na)),
                        pltpu.SemaphoreType.DMA((na,))],
    )(*shards)


_RELATIONS = [(dx, dy, dc) for dx in (0, 1) for dy in (0, 1) for dc in (0, 1)][1:]


def _flip(v, d):
    return 1 - v if d else v


def _exchange_blocks(grads, name):
    na = len(grads)

    def body(*refs):
        srcs, dsts = refs[:na], refs[na:2 * na]
        send_sems, recv_sems, local_sems = refs[2 * na:]
        x, y, c = _place()
        my = 4 * x + 2 * y + c
        mine = [pltpu.make_async_copy(srcs[a].at[pl.ds(my, 1)], dsts[a].at[pl.ds(my, 1)], local_sems.at[a])
                for a in range(na)]
        for cp in mine:
            cp.start()
        sends = []
        for k, (dx, dy, dc) in enumerate(_RELATIONS):
            peer = (_flip(x, dx), _flip(y, dy), _flip(c, dc))
            pidx = 4 * peer[0] + 2 * peer[1] + peer[2]
            for a in range(na):
                cp = pltpu.make_async_remote_copy(
                    src_ref=srcs[a].at[pl.ds(pidx, 1)], dst_ref=dsts[a].at[pl.ds(my, 1)],
                    send_sem=send_sems.at[k, a], recv_sem=recv_sems.at[k, a], device_id=peer, device_id_type=MESH)
                cp.start()
                sends.append((cp, pidx, k, a, peer))
        for cp, pidx, k, a, peer in sends:
            pltpu.make_async_remote_copy(
                src_ref=srcs[a].at[pl.ds(pidx, 1)], dst_ref=dsts[a].at[pl.ds(pidx, 1)],
                send_sem=send_sems.at[k, a], recv_sem=recv_sems.at[k, a], device_id=peer,
                device_id_type=MESH).wait_recv()
        for cp, *_ in sends:
            cp.wait_send()
        for cp in mine:
            cp.wait()

    anyspec = pl.BlockSpec(memory_space=pl.ANY)
    return pl.pallas_call(
        body, name=name,
        in_specs=[anyspec] * na, out_specs=[anyspec] * na,
        out_shape=[jax.ShapeDtypeStruct(a.shape, a.dtype) for a in grads],
        scratch_shapes=[pltpu.SemaphoreType.DMA((7, na)), pltpu.SemaphoreType.DMA((7, na)),
                        pltpu.SemaphoreType.DMA((na,))],
    )(*grads)


def _peers():
    x, y, c = _place()
    out = []
    for dx, dy, dc in _RELATIONS:
        peer = (_flip(x, dx), _flip(y, dy), _flip(c, dc))
        out.append((peer, 4 * peer[0] + 2 * peer[1] + peer[2]))
    return 4 * x + 2 * y + c, out


def _exchange_start(grads, name):
    na = len(grads)

    def body(*refs):
        srcs, lands = refs[:na], refs[na:2 * na]
        send_sems, recv_sems, token = refs[2 * na], refs[2 * na + 1], refs[-1]
        my, peers = _peers()
        for k, (peer, pidx) in enumerate(peers):
            for a in range(na):
                pltpu.make_async_remote_copy(
                    src_ref=srcs[a].at[pl.ds(pidx, 1)], dst_ref=lands[a].at[pl.ds(my, 1)],
                    send_sem=send_sems.at[k * na + a], recv_sem=recv_sems.at[k * na + a], device_id=peer,
                    device_id_type=MESH).start()
        token[...] = jnp.zeros_like(token)

    hbm = pl.BlockSpec(memory_space=pltpu.HBM)
    sem = pl.BlockSpec(memory_space=pltpu.SEMAPHORE)
    thru = [pltpu.HBM(g.shape, g.dtype) for g in grads]
    operands = [pltpu.with_memory_space_constraint(g, pltpu.HBM) for g in grads]
    operands += [pltpu.with_memory_space_constraint(jnp.zeros(g.shape, g.dtype), pltpu.HBM) for g in grads]
    outs = pl.pallas_call(
        body, name=name,
        out_shape=(pltpu.SemaphoreType.DMA((7 * na,)), pltpu.SemaphoreType.DMA((7 * na,)), *thru, *thru,
                   jax.ShapeDtypeStruct((8, LANES), F32)),
        in_specs=[hbm] * (2 * na),
        out_specs=(sem, sem, *([hbm] * (2 * na)), pl.BlockSpec(memory_space=pltpu.VMEM)),
        input_output_aliases={i: 2 + i for i in range(2 * na)},
        compiler_params=pltpu.CompilerParams(has_side_effects=pltpu.SideEffectType.DATAFLOW_SIDE_EFFECTING),
    )(*operands)
    return outs[0], outs[1], list(outs[2:2 + na]), list(outs[2 + na:2 + 2 * na]), outs[-1]


def _exchange_wait(send_sems, recv_sems, grads, lands, after, name):
    na = len(grads)

    def body(*refs):
        srcs, lands_ = refs[:na], refs[na:2 * na]
        ssem, rsem = refs[2 * na], refs[2 * na + 1]
        _, peers = _peers()
        for k, (peer, pidx) in enumerate(peers):
            for a in range(na):
                cp = pltpu.make_async_remote_copy(
                    src_ref=srcs[a].at[pl.ds(pidx, 1)], dst_ref=lands_[a].at[pl.ds(pidx, 1)],
                    send_sem=ssem.at[k * na + a], recv_sem=rsem.at[k * na + a], device_id=peer, device_id_type=MESH)
                cp.wait_send()
                cp.wait_recv()

    hbm = pl.BlockSpec(memory_space=pltpu.HBM)
    sem = pl.BlockSpec(memory_space=pltpu.SEMAPHORE)
    thru = [pltpu.HBM(g.shape, g.dtype) for g in grads]
    outs = pl.pallas_call(
        body, name=name,
        out_shape=(*thru, *thru),
        in_specs=[hbm] * (2 * na) + [sem, sem, pl.BlockSpec(memory_space=pl.ANY)],
        out_specs=tuple([hbm] * (2 * na)),
        input_output_aliases={i: i for i in range(2 * na)},
        compiler_params=pltpu.CompilerParams(has_side_effects=pltpu.SideEffectType.DATAFLOW_SIDE_EFFECTING),
    )(*grads, *lands, send_sems, recv_sems, after)
    return list(outs[na:])


def _all_reduce_small(v, name):
    r, c_ = v.shape

    def body(v_ref, o_ref, gath, send_sems, recv_sems):
        x, y, c = _place()
        my = 4 * x + 2 * y + c
        gath[my] = v_ref[...]
        sends = []
        for k, (dx, dy, dc) in enumerate(_RELATIONS):
            peer = (_flip(x, dx), _flip(y, dy), _flip(c, dc))
            cp = pltpu.make_async_remote_copy(
                src_ref=v_ref, dst_ref=gath.at[my], send_sem=send_sems.at[k], recv_sem=recv_sems.at[k],
                device_id=peer, device_id_type=MESH)
            cp.start()
            sends.append((cp, 4 * peer[0] + 2 * peer[1] + peer[2], k, peer))
        for cp, pidx, k, peer in sends:
            pltpu.make_async_remote_copy(
                src_ref=v_ref, dst_ref=gath.at[pidx], send_sem=send_sems.at[k], recv_sem=recv_sems.at[k],
                device_id=peer, device_id_type=MESH).wait_recv()
        for cp, *_ in sends:
            cp.wait_send()
        tot = gath[0]
        for k in range(1, N_DEV):
            tot = tot + gath[k]
        o_ref[...] = tot

    vm = pl.BlockSpec(memory_space=pltpu.VMEM)
    return pl.pallas_call(
        body, name=name, in_specs=[vm], out_specs=vm,
        out_shape=jax.ShapeDtypeStruct((r, c_), F32),
        scratch_shapes=[pltpu.VMEM((N_DEV, r, c_), F32), pltpu.SemaphoreType.DMA((7,)),
                        pltpu.SemaphoreType.DMA((7,))],
    )(v)


TM = 512
TM_MATMUL = 2048
TM_RESIDUAL = 1024
TQ = 256


def _pad_to(a, axis, size):
    pad = [(0, 0)] * a.ndim
    pad[axis] = (0, size - a.shape[axis])
    return jnp.pad(a, pad)


def _local_step(x, target, ga, gb, gc, conv_full, norm_mix, q_norm, k_norm, norm_ffn, start_exchange=None):
    depth = gb.shape[0]
    tm, tq = min(TM, x.shape[0]), min(TQ, x.shape[0])
    tmm, tmr = min(TM_MATMUL, x.shape[0]), min(TM_RESIDUAL, x.shape[0])
    attn = gb.shape[1] // 2
    nheads = attn // HEAD_DIM
    scale = HEAD_DIM ** -0.5 * LOG2E
    saved = []
    for l in range(depth):
        h1 = _rmsnorm_fwd(x, norm_mix[l][None], tm,f"norm_mix_fwd_{l}")
        proj = _mm_blocks(h1, ga, 3 * l, tmm, f"proj_in_{l}")
        qk_gain = jnp.concatenate([jnp.tile(q_norm[l], nheads) * scale, jnp.tile(k_norm[l], nheads)])[None]
        qk = _qknorm_fwd(proj, qk_gain, tmm, f"qknorm_fwd_{l}")
        o, rtot = _attn_fwd(qk, proj, tq,f"attn_fwd_{l}")
        conv_w8 = _pad_to(conv_full[l], 0, 8)
        cv = _conv_fwd(proj, conv_w8, f"conv_fwd_{l}")
        mix = jnp.concatenate([o, cv], axis=1)
        x1 = _mm_residual(mix, gb, l, x, tmr, 512, f"proj_out_{l}")
        h2 = _rmsnorm_fwd(x1, norm_ffn[l][None], tm,f"norm_ffn_fwd_{l}")
        g, u, act = _mm_swiglu(h2, ga, 3 * l + 1, 3 * l + 2, tmm, f"ffn_up_{l}")
        x2 = _mm_residual(act, gc, l, x1, tmr, 512, f"ffn_down_{l}")
        saved.append((x, h1, proj, qk_gain, qk, rtot, conv_w8, mix, x1, h2, g, u, act))
        x = x2

    dx, dxb, loss = _loss_head(x, target, tm, "loss_head")

    grads = [None] * depth
    small = [None] * depth
    tie = 0.0
    for l in reversed(range(depth)):
        x0, h1, proj, qk_gain, qk, rtot, conv_w8, mix, x1, h2, g, u, act = saved[l]
        d = x0.shape[1]
        dg, du = _mm_nt_swiglu_bwd(dxb, gc, l, g, u, tmm, f"ffn_down_bwd_{l}")
        d_wdown = _mm_tn(act, dxb, 768, d, tmm, False, f"dw_down_{l}")
        d_wgate = _mm_tn(h2, dg, d, ga.shape[3], tmm, True, f"dw_gate_{l}")
        d_wup = _mm_tn(h2, du, d, ga.shape[3], tmm, True, f"dw_up_{l}")
        dh2 = _mm_nt_blocks([dg, du], ga, [3 * l + 1, 3 * l + 2], tm, f"ffn_up_bwd_{l}")
        dx1, dx1b, dg_ffn = _rmsnorm_bwd(dh2, x1, norm_ffn[l][None] + tie, dx, tm, f"norm_ffn_bwd_{l}")
        dmix = _mm_nt(dx1b, gb, l, tmr, 512, f"proj_out_bwd_{l}")
        d_wout = _mm_tn(mix, dx1b, 512, d, tmm, False, f"dw_out_{l}")
        if start_exchange is not None and l == 0:
            conv_w8 = conv_w8 + start_exchange(l, (1, 2, 3, 4), (d_wgate, d_wup, d_wout, d_wdown))
        dcb, dcc, dcu, dconv = _conv_bwd(dmix, proj, conv_w8, f"conv_bwd_{l}")
        dq, dk, dv = _attn_bwd(qk, proj, dmix, rtot, tq,f"attn_bwd_{l}")
        dqk, dg_qk = _qknorm_bwd(jnp.concatenate([dq, dk], axis=1), proj, qk_gain, tmm, f"qknorm_bwd_{l}")
        dproj = jnp.concatenate([dqk, dv.astype(BF16), dcb, dcc, dcu], axis=1)
        d_win = _mm_tn(h1, dproj, d, ga.shape[3], tmm, True, f"dw_in_{l}")
        dh1 = _mm_nt_blocks([dproj], ga, [3 * l], tmr, f"proj_in_bwd_{l}")
        dx, dxb, dg_mix = _rmsnorm_bwd(dh1, x0, norm_mix[l][None], dx1, tm, f"norm_mix_bwd_{l}")
        grads[l] = (d_win, d_wgate, d_wup, d_wout, d_wdown)
        if start_exchange is not None and l > 0:
            tie = start_exchange(l, (0, 1, 2, 3, 4), grads[l])
        dq_gain = jnp.sum(dg_qk[0, :attn].reshape(nheads, HEAD_DIM), axis=0) * scale
        dk_gain = jnp.sum(dg_qk[0, attn:].reshape(nheads, HEAD_DIM), axis=0)
        small[l] = (dg_mix[0], dg_ffn[0], dq_gain, dk_gain, dconv[:3])
    return loss, dx, grads, small


def kernel(x, norm_mix, w_in, q_norm, k_norm, conv_w, w_out, norm_ffn, w_gate, w_up, w_down, loss_target, m_norm_mix, m_w_in, m_q_norm, m_k_norm, m_conv_w, m_w_out, m_norm_ffn, m_w_gate, m_w_up, m_w_down, v_norm_mix, v_w_in, v_q_norm, v_k_norm, v_conv_w, v_w_out, v_norm_ffn, v_w_gate, v_w_up, v_w_down):
    depth, d, in_shard = w_in.shape
    ff_shard = w_gate.shape[2]
    ff_pad = in_shard
    conv_shard = conv_w.shape[2]
    xs = x.reshape(x.shape[-2], d)
    target = loss_target.reshape(xs.shape)

    pa = jnp.stack([w_in, _pad_to(w_gate, 2, ff_pad), _pad_to(w_up, 2, ff_pad)], axis=1)
    pa = pa.reshape(3 * depth, 1, d, in_shard).astype(BF16)
    pb = w_out.astype(BF16)[:, None]
    pc = _pad_to(w_down, 1, ff_pad).astype(BF16)[:, None]
    pd = _pad_to(_pad_to(conv_w.reshape(depth * 3, conv_shard), 0, 8), 1, LANES)[None, None]
    ga, gb, gc, gd = _all_gather([pa, pb, pc, pd], "gather_weights")
    gb = gb.reshape(depth, N_DEV * gb.shape[2], d)
    gc = gc.reshape(depth, N_DEV * ff_pad, d)
    conv_full = gd[0, :, :depth * 3, :conv_shard].transpose(1, 0, 2).reshape(depth, 3, N_DEV * conv_shard)

    block_rows = (d, d, d, w_out.shape[1], ff_pad)
    in_flight = []

    def start_exchange(l, which, pairs):
        send = [g16.reshape(N_DEV, block_rows[j], -1) for j, (_, g16) in zip(which, pairs)]
        send_sems, recv_sems, thru, lands, token = _exchange_start(send, f"exchange_start_{l}_{which[0]}")
        in_flight.append(([5 * l + j for j in which], send_sems, recv_sems, thru, lands))
        return token[0, 0]

    loss, grad_x, grads, small = _local_step(xs, target, ga, gb, gc, conv_full, norm_mix, q_norm, k_norm, norm_ffn,
                                             start_exchange)

    x_, y_, c_ = _place()
    my = 4 * x_ + 2 * y_ + c_
    landed = [None] * (5 * depth)
    for n_, (idx, send_sems, recv_sems, thru, lands) in enumerate(in_flight):
        for i, arr in zip(idx, _exchange_wait(send_sems, recv_sems, thru, lands, grad_x, f"exchange_wait_{n_}")):
            landed[i] = arr
    rest = [i for i in range(5 * depth) if landed[i] is None]
    send = [grads[i // 5][i % 5][1].reshape(N_DEV, block_rows[i % 5], -1) for i in rest]
    for i, arr in zip(rest, _exchange_blocks(send, "exchange_grads")):
        landed[i] = arr
    own = [lax.dynamic_index_in_dim(grads[i // 5][i % 5][0].reshape(N_DEV, block_rows[i % 5], -1), my, 0,
                                    keepdims=False) for i in range(5 * depth)]

    nconv = N_DEV * conv_shard
    rows = []
    for l in range(depth):
        g_mix, g_ffn, g_q, g_k, g_conv = small[l]
        qkrow = _pad_to(jnp.concatenate([g_q, g_k]), 0, d)
        rows += [g_mix[None], g_ffn[None], qkrow[None], _pad_to(g_conv, 1, d)]
    nrow = 6 * depth
    packed = jnp.concatenate(rows + [_pad_to(loss[:1], 1, d)], axis=0)
    packed = _pad_to(packed, 0, ((nrow + 1 + 7) // 8) * 8)
    summed = _all_reduce_small(packed, "reduce_small")
    loss_out = summed[nrow, 0]

    def big(i, w, m, v, tr, name, rows_=None, cols_=None):
        parts = landed[i]
        pr, pcn = parts.shape[1], parts.shape[2]
        w, m, v = [_pad_to(_pad_to(t, 0, pr), 1, pcn) for t in (w, m, v)]
        outs = _adamw(parts, own[i], w, m, v, tr, name)
        return [o[:rows_ or pr, :cols_ or pcn] for o in outs]

    res = {}
    for l in range(depth):
        i = 5 * l
        res[("w_in", l)] = big(i, w_in[l], m_w_in[l], v_w_in[l], 256, f"adamw_in_{l}")
        res[("w_gate", l)] = big(i + 1, w_gate[l], m_w_gate[l], v_w_gate[l], 256, f"adamw_gate_{l}", cols_=ff_shard)
        res[("w_up", l)] = big(i + 2, w_up[l], m_w_up[l], v_w_up[l], 256, f"adamw_up_{l}", cols_=ff_shard)
        res[("w_out", l)] = big(i + 3, w_out[l], m_w_out[l], v_w_out[l], w_out.shape[1], f"adamw_out_{l}")
        res[("w_down", l)] = big(i + 4, w_down[l], m_w_down[l], v_w_down[l], 128, f"adamw_down_{l}",
                                 rows_=ff_shard)

    g_rows, w_rows, m_rows, v_rows = [], [], [], []
    for l in range(depth):
        base = l * 6
        conv_g = lax.dynamic_slice(summed[base + 3:base + 6], (0, my * conv_shard), (3, conv_shard))
        g_rows += [summed[base:base + 3], _pad_to(conv_g, 1, d)]
        for dst, (nm, qn, kn, nf, cw) in ((w_rows, (norm_mix, q_norm, k_norm, norm_ffn, conv_w)),
                                          (m_rows, (m_norm_mix, m_q_norm, m_k_norm, m_norm_ffn, m_conv_w)),
                                          (v_rows, (v_norm_mix, v_q_norm, v_k_norm, v_norm_ffn, v_conv_w))):
            dst += [nm[l][None], nf[l][None], _pad_to(jnp.concatenate([qn[l], kn[l]]), 0, d)[None],
                    _pad_to(cw[l], 1, d)]
    prow = ((nrow + 7) // 8) * 8
    gs, ws, ms, vs = [_pad_to(jnp.concatenate(t, axis=0), 0, prow) for t in (g_rows, w_rows, m_rows, v_rows)]
    sm = _adamw(gs[None], None, ws, ms, vs, prow, "adamw_small")

    hd = q_norm.shape[1]

    def small_out(t, kind):
        per_layer = []
        for l in range(depth):
            base = l * 6
            per_layer.append({"norm_mix": t[base], "norm_ffn": t[base + 1], "q_norm": t[base + 2, :hd],
                              "k_norm": t[base + 2, hd:2 * hd], "conv_w": t[base + 3:base + 6, :conv_shard]}[kind])
        return jnp.stack(per_layer)

    def big_out(name, i):
        return jnp.stack([res[(name, l)][i] for l in range(depth)])

    outs = [loss_out, grad_x.reshape(x.shape)]
    for i in range(4):
        outs += [small_out(sm[i], "norm_mix"), big_out("w_in", i), small_out(sm[i], "q_norm"),
                 small_out(sm[i], "k_norm"), small_out(sm[i], "conv_w"), big_out("w_out", i),
                 small_out(sm[i], "norm_ffn"), big_out("w_gate", i), big_out("w_up", i), big_out("w_down", i)]
    return tuple(outs)
```

```python
import jax
import jax.numpy as jnp
from jax import lax
from jax.experimental import pallas as pl
from jax.experimental.pallas import tpu as pltpu

F32 = jnp.float32
BF16 = jnp.bfloat16
MESH = pl.DeviceIdType.MESH

N_DEV = 8
LANES = 128
HEAD_DIM = 64
KEY_CHUNK = 128
EPS = 1e-6
VMEM_LIMIT = 48 * 1024 * 1024

ADAM_LR = 0.001
ADAM_B1 = 0.9
ADAM_B2 = 0.999
ADAM_EPS = 1e-08
ADAM_WD = 0.01
ADAM_STEP = 10

NN = (((1,), (0,)), ((), ()))
NT = (((1,), (1,)), ((), ()))
TN = (((0,), (0,)), ((), ()))


def _dot(a, b, dims):
    return lax.dot_general(a.astype(BF16), b.astype(BF16), dims, preferred_element_type=F32)


def _cparams(*sem):
    return pltpu.CompilerParams(dimension_semantics=sem, vmem_limit_bytes=VMEM_LIMIT)


def _split_hi_lo(v):
    hi = v.astype(BF16)
    lo = (v - hi.astype(F32)).astype(BF16)
    return jnp.concatenate([hi, lo], axis=1)


def _rmsnorm_fwd(x, gain, tm, name):
    s, d = x.shape

    def body(x_ref, g_ref, o_ref):
        xv = x_ref[...]
        r = lax.rsqrt(jnp.mean(xv * xv, axis=-1, keepdims=True) + EPS)
        o_ref[...] = ((xv * r) * g_ref[...]).astype(o_ref.dtype)

    return pl.pallas_call(
        body, name=name, grid=(s // tm,),
        in_specs=[pl.BlockSpec((tm, d), lambda i: (i, 0)), pl.BlockSpec((1, d), lambda i: (0, 0))],
        out_specs=pl.BlockSpec((tm, d), lambda i: (i, 0)),
        out_shape=jax.ShapeDtypeStruct((s, d), BF16),
        compiler_params=_cparams("parallel"),
    )(x, gain)


def _rmsnorm_bwd(dh, x, gain, dres, tm, name):
    s, d = x.shape
    nsteps = s // tm

    def body(dh_ref, x_ref, g_ref, dres_ref, dx_ref, dxb_ref, dg_ref):
        i = pl.program_id(0)
        xv = x_ref[...]
        r = lax.rsqrt(jnp.mean(xv * xv, axis=-1, keepdims=True) + EPS)
        xhat = xv * r
        dhv = dh_ref[...]
        dxh = dhv * g_ref[...]
        proj = jnp.mean(dxh * xhat, axis=-1, keepdims=True)
        dxv = dres_ref[...] + r * (dxh - xhat * proj)
        dx_ref[...] = dxv
        dxb_ref[...] = dxv.astype(dxb_ref.dtype)
        part = jnp.sum((dhv * xhat).reshape(tm // 8, 8, d), axis=0)

        @pl.when(i == 0)
        def _():
            dg_ref[...] = part

        @pl.when(i > 0)
        def _():
            dg_ref[...] += part

        @pl.when(i == nsteps - 1)
        def _():
            dg_ref[...] = jnp.broadcast_to(jnp.sum(dg_ref[...], axis=0, keepdims=True), (8, d))

    row = pl.BlockSpec((tm, d), lambda i: (i, 0))
    return pl.pallas_call(
        body, name=name, grid=(nsteps,),
        in_specs=[row, row, pl.BlockSpec((1, d), lambda i: (0, 0)), row],
        out_specs=[row, row, pl.BlockSpec((8, d), lambda i: (0, 0))],
        out_shape=[jax.ShapeDtypeStruct((s, d), F32), jax.ShapeDtypeStruct((s, d), BF16),
                   jax.ShapeDtypeStruct((8, d), F32)],
        compiler_params=_cparams("arbitrary"),
    )(dh, x, gain, dres)


def _group_mean_matrix():
    r = lax.broadcasted_iota(jnp.int32, (LANES, LANES), 0) // HEAD_DIM
    c = lax.broadcasted_iota(jnp.int32, (LANES, LANES), 1) // HEAD_DIM
    return jnp.where(r == c, 1.0 / HEAD_DIM, 0.0).astype(BF16)


def _group_mean(v, gm):
    hi = v.astype(BF16)
    lo = (v - hi.astype(F32)).astype(BF16)
    return _dot(hi, gm, NN) + _dot(lo, gm, NN)


def _qknorm_fwd(proj, gains, tm, name):
    s = proj.shape[0]
    ncol = gains.shape[1] // LANES

    def body(p_ref, g_ref, gm_ref, o_ref):
        xv = p_ref[...].astype(F32)
        r = lax.rsqrt(_group_mean(xv * xv, gm_ref[...]) + EPS)
        o_ref[...] = ((xv * r) * g_ref[...]).astype(o_ref.dtype)

    blk = pl.BlockSpec((tm, LANES), lambda i, j: (i, j))
    return pl.pallas_call(
        body, name=name, grid=(s // tm, ncol),
        in_specs=[blk, pl.BlockSpec((1, LANES), lambda i, j: (0, j)),
                  pl.BlockSpec((LANES, LANES), lambda i, j: (0, 0))],
        out_specs=blk,
        out_shape=jax.ShapeDtypeStruct((s, ncol * LANES), BF16),
        compiler_params=_cparams("parallel", "parallel"),
    )(proj, gains, _group_mean_matrix())


def _qknorm_bwd(dqk, proj, gains, tm, name):
    s = proj.shape[0]
    ncol = gains.shape[1] // LANES
    nsteps = s // tm

    def body(dy_ref, p_ref, g_ref, gm_ref, dx_ref, dg_ref):
        i = pl.program_id(1)
        gm = gm_ref[...]
        xv = p_ref[...].astype(F32)
        r = lax.rsqrt(_group_mean(xv * xv, gm) + EPS)
        xhat = xv * r
        dy = dy_ref[...]
        dxh = dy * g_ref[...]
        proj_ = _group_mean(dxh * xhat, gm)
        dx_ref[...] = (r * (dxh - xhat * proj_)).astype(dx_ref.dtype)
        part = jnp.sum((dy * xhat).reshape(tm // 8, 8, LANES), axis=0)

        @pl.when(i == 0)
        def _():
            dg_ref[...] = part

        @pl.when(i > 0)
        def _():
            dg_ref[...] += part

        @pl.when(i == nsteps - 1)
        def _():
            dg_ref[...] = jnp.broadcast_to(jnp.sum(dg_ref[...], axis=0, keepdims=True), (8, LANES))

    blk = pl.BlockSpec((tm, LANES), lambda j, i: (i, j))
    return pl.pallas_call(
        body, name=name, grid=(ncol, nsteps),
        in_specs=[blk, blk, pl.BlockSpec((1, LANES), lambda j, i: (0, j)),
                  pl.BlockSpec((LANES, LANES), lambda j, i: (0, 0))],
        out_specs=[blk, pl.BlockSpec((8, LANES), lambda j, i: (0, j))],
        out_shape=[jax.ShapeDtypeStruct((s, ncol * LANES), BF16),
                   jax.ShapeDtypeStruct((8, ncol * LANES), F32)],
        compiler_params=_cparams("parallel", "arbitrary"),
    )(dqk, proj, gains, _group_mean_matrix())


CONV_ROWS = 256
HALO = 8


def _conv_fwd(proj, conv_w8, name):
    s = proj.shape[0]
    nblk = conv_w8.shape[1] // LANES
    first = 3 * nblk
    nchunk = s // CONV_ROWS

    def body(cb_ref, cc_ref, cu_ref, w_ref, y_ref, hpad):
        hpad[pl.ds(0, 2 * HALO), :] = jnp.zeros((2 * HALO, LANES), F32)

        def fill(i, _):
            r0 = pl.multiple_of(i * CONV_ROWS, CONV_ROWS)
            hpad[pl.ds(r0 + 2 * HALO, CONV_ROWS), :] = (
                cc_ref[pl.ds(r0, CONV_ROWS), :].astype(F32) * cu_ref[pl.ds(r0, CONV_ROWS), :].astype(F32))
            return 0

        lax.fori_loop(0, nchunk, fill, 0)
        w0, w1, w2 = w_ref[0:1, :], w_ref[1:2, :], w_ref[2:3, :]

        def conv(i, _):
            r0 = pl.multiple_of(i * CONV_ROWS, CONV_ROWS)
            win = hpad[pl.ds(r0 + HALO, CONV_ROWS + HALO), :]
            c = (w2 * win[HALO:] + w1 * pltpu.roll(win, 1, 0)[HALO:] + w0 * pltpu.roll(win, 2, 0)[HALO:])
            y_ref[pl.ds(r0, CONV_ROWS), :] = (cb_ref[pl.ds(r0, CONV_ROWS), :].astype(F32) * c).astype(y_ref.dtype)
            return 0

        lax.fori_loop(0, nchunk, conv, 0)

    def col(off):
        return pl.BlockSpec((s, LANES), lambda j: (0, off + j))

    return pl.pallas_call(
        body, name=name, grid=(nblk,),
        in_specs=[col(first), col(first + nblk), col(first + 2 * nblk), pl.BlockSpec((8, LANES), lambda j: (0, j))],
        out_specs=pl.BlockSpec((s, LANES), lambda j: (0, j)),
        out_shape=jax.ShapeDtypeStruct((s, nblk * LANES), BF16),
        scratch_shapes=[pltpu.VMEM((s + 2 * HALO, LANES), F32)],
        compiler_params=_cparams("parallel"),
    )(proj, proj, proj, conv_w8)


def _conv_bwd(dmix, proj, conv_w8, name):
    s = proj.shape[0]
    nblk = conv_w8.shape[1] // LANES
    first = 3 * nblk
    nchunk = s // CONV_ROWS

    def body(dy_ref, cb_ref, cc_ref, cu_ref, w_ref, dcb_ref, dcc_ref, dcu_ref, dw_ref, hpad, dcpad):
        hpad[pl.ds(0, 2 * HALO), :] = jnp.zeros((2 * HALO, LANES), F32)
        dcpad[pl.ds(s, 2 * HALO), :] = jnp.zeros((2 * HALO, LANES), F32)

        def fill(i, _):
            r0 = pl.multiple_of(i * CONV_ROWS, CONV_ROWS)
            hpad[pl.ds(r0 + 2 * HALO, CONV_ROWS), :] = (
                cc_ref[pl.ds(r0, CONV_ROWS), :].astype(F32) * cu_ref[pl.ds(r0, CONV_ROWS), :].astype(F32))
            return 0

        lax.fori_loop(0, nchunk, fill, 0)
        w0, w1, w2 = w_ref[0:1, :], w_ref[1:2, :], w_ref[2:3, :]

        def fold(v):
            return jnp.sum(v.reshape(CONV_ROWS // 8, 8, LANES), axis=0)

        def first_pass(i, acc):
            a0, a1, a2 = acc
            r0 = pl.multiple_of(i * CONV_ROWS, CONV_ROWS)
            win = hpad[pl.ds(r0 + HALO, CONV_ROWS + HALO), :]
            h0 = win[HALO:]
            h1 = pltpu.roll(win, 1, 0)[HALO:]
            h2 = pltpu.roll(win, 2, 0)[HALO:]
            c = w2 * h0 + w1 * h1 + w0 * h2
            dy = dy_ref[pl.ds(r0, CONV_ROWS), :]
            dcb_ref[pl.ds(r0, CONV_ROWS), :] = (dy * c).astype(dcb_ref.dtype)
            dc = dy * cb_ref[pl.ds(r0, CONV_ROWS), :].astype(F32)
            dcpad[pl.ds(r0, CONV_ROWS), :] = dc
            return a0 + fold(dc * h2), a1 + fold(dc * h1), a2 + fold(dc * h0)

        z8 = jnp.zeros((8, LANES), F32)
        a0, a1, a2 = lax.fori_loop(0, nchunk, first_pass, (z8, z8, z8))
        dw_ref[...] = jnp.concatenate(
            [jnp.sum(a0, axis=0, keepdims=True), jnp.sum(a1, axis=0, keepdims=True),
             jnp.sum(a2, axis=0, keepdims=True), jnp.zeros((5, LANES), F32)], axis=0)

        def second_pass(i, _):
            r0 = pl.multiple_of(i * CONV_ROWS, CONV_ROWS)
            win = dcpad[pl.ds(r0, CONV_ROWS + HALO), :]
            n = CONV_ROWS + HALO
            dh = (w2 * win[:CONV_ROWS] + w1 * pltpu.roll(win, n - 1, 0)[:CONV_ROWS]
                  + w0 * pltpu.roll(win, n - 2, 0)[:CONV_ROWS])
            dcc_ref[pl.ds(r0, CONV_ROWS), :] = (dh * cu_ref[pl.ds(r0, CONV_ROWS), :].astype(F32)).astype(dcc_ref.dtype)
            dcu_ref[pl.ds(r0, CONV_ROWS), :] = (dh * cc_ref[pl.ds(r0, CONV_ROWS), :].astype(F32)).astype(dcu_ref.dtype)
            return 0

        lax.fori_loop(0, nchunk, second_pass, 0)

    def col(off):
        return pl.BlockSpec((s, LANES), lambda j: (0, off + j))

    out = pl.BlockSpec((s, LANES), lambda j: (0, j))
    return pl.pallas_call(
        body, name=name, grid=(nblk,),
        in_specs=[col(nblk), col(first), col(first + nblk), col(first + 2 * nblk),
                  pl.BlockSpec((8, LANES), lambda j: (0, j))],
        out_specs=[out, out, out, pl.BlockSpec((8, LANES), lambda j: (0, j))],
        out_shape=[jax.ShapeDtypeStruct((s, nblk * LANES), BF16)] * 3 + [jax.ShapeDtypeStruct((8, nblk * LANES), F32)],
        scratch_shapes=[pltpu.VMEM((s + 2 * HALO, LANES), F32), pltpu.VMEM((s + 2 * HALO, LANES), F32)],
        compiler_params=_cparams("parallel"),
    )(dmix, proj, proj, proj, conv_w8)


LOG2E = 1.4426950408889634
LN2 = 0.6931471805599453
NEG_BIG = -1e30
SATURATED = 160.0


def _cumsum_matrix(kind):
    j = lax.broadcasted_iota(jnp.int32, (KEY_CHUNK, 2 * KEY_CHUNK), 0)
    c = lax.broadcasted_iota(jnp.int32, (KEY_CHUNK, 2 * KEY_CHUNK), 1)
    tri = {"after": j > c, "upto": j <= c, "before": j < c}[kind]
    return jnp.where((c >= KEY_CHUNK) | tri, 1.0, 0.0).astype(BF16)


def _stack_heads(t, m0):
    zero = jnp.zeros_like(t)
    return jnp.concatenate([jnp.where(m0, t, zero), jnp.where(m0, zero, t)], axis=0)


def _softplus2(z):
    sp = jnp.maximum(z, 0.0) + jnp.log2(1.0 + jnp.exp2(-jnp.abs(z)))
    return sp, z - sp


def _key_chunk(ref, kc):
    return ref[pl.ds(pl.multiple_of(kc * KEY_CHUNK, KEY_CHUNK), KEY_CHUNK), :]


def _attn_bwd(qk, proj, dmix, rtot, used, tq, name):
    s = qk.shape[0]
    nhp = qk.shape[1] // (2 * LANES)
    nc = tq // KEY_CHUNK

    def body(used_ref, q_ref, k_ref, v_ref, do_ref, r_ref, cmi_ref, cme_ref, bias_ref, dq_ref, dk_ref, dv_ref,
             z_refs, ls_refs, sig_refs, sp_refs, gb_refs, pr_ref, gs_ref):
        qi = pl.program_id(1)

        @pl.when(qi == 0)
        def _():
            dk_ref[...] = jnp.zeros_like(dk_ref)
            dv_ref[...] = jnp.zeros_like(dv_ref)

        nslots = (qi + 1) * nc
        first = jnp.clip(nslots - used_ref[pl.program_id(0), qi], 0, nslots - nc)
        m0 = lax.broadcasted_iota(jnp.int32, (1, LANES), 1) < HEAD_DIM
        qs = _stack_heads(q_ref[...], m0)
        do = do_ref[...]
        dos = _stack_heads(do.astype(BF16), m0)
        dosl = _stack_heads((do * LN2).astype(BF16), m0)
        cmi = cmi_ref[...]
        cme = cme_ref[...]

        def chunk_at(i):
            return jnp.clip(i, first, nslots - 1)

        def scores(kc):
            return _dot(qs, _key_chunk(k_ref, kc), NT)

        def weights(ls, cs, da, pr, kc):
            a = jnp.exp2(ls - (pr - cs[:, :KEY_CHUNK]))
            gb = (a * da).astype(BF16)
            ks = pl.multiple_of(kc * KEY_CHUNK, KEY_CHUNK)
            dv_ref[pl.ds(ks, KEY_CHUNK), :] += _dot(a, dos, TN)
            return gb, jnp.exp2(ls), pr - cs[:, KEY_CHUNK:]

        def score_grads(gb, sig, cg, gs, dq, kc):
            dzb = (gb.astype(F32) * (1.0 - sig) - sig * (gs + cg[:, :KEY_CHUNK])).astype(BF16)
            ks = pl.multiple_of(kc * KEY_CHUNK, KEY_CHUNK)
            dk_ref[pl.ds(ks, KEY_CHUNK), :] += _dot(dzb, qs, TN)
            dq = dq + _dot(jnp.concatenate([dzb[:tq], dzb[tq:]], axis=1), _stack_heads(_key_chunk(k_ref, kc), m0), NN)
            return gs + cg[:, KEY_CHUNK:], dq

        def step(i, par, bias=None):
            cur, prv = par, 1 - par
            k1, k2 = chunk_at(i - 1), chunk_at(i - 2)
            z_next = scores(chunk_at(i + 1))
            cs = _dot(sp_refs[prv][...], cmi, NN)
            da = _dot(dosl, _key_chunk(v_ref, k1), NT)
            cg = _dot(gb_refs[cur][...], cme, NN)
            z = z_refs[cur][...]
            if bias is not None:
                z = z + bias
            sp, ls = _softplus2(z)
            sp_refs[cur][...] = sp.astype(BF16)
            ls_refs[cur][...] = ls
            gs, dq = score_grads(gb_refs[cur][...], sig_refs[cur][...], cg, gs_ref[...], dq_ref[...], k2)
            gs_ref[...] = gs
            dq_ref[...] = dq
            gb, sig, pr = weights(ls_refs[prv][...], cs, da, pr_ref[...], k1)
            gb_refs[prv][...] = gb
            sig_refs[prv][...] = sig
            pr_ref[...] = pr
            z_refs[prv][...] = z_next

        pr_ref[...] = jnp.concatenate([r_ref[:, :LANES], r_ref[:, LANES:]], axis=0)
        gs_ref[...] = jnp.zeros((2 * tq, LANES), F32)
        dq_ref[...] = jnp.zeros((tq, LANES), F32)
        z_refs[0][...] = scores(first)
        sp_refs[1][...] = jnp.zeros((2 * tq, LANES), BF16)
        ls_refs[1][...] = jnp.full((2 * tq, LANES), NEG_BIG, F32)
        gb_refs[0][...] = jnp.zeros((2 * tq, LANES), BF16)
        sig_refs[0][...] = jnp.zeros((2 * tq, LANES), F32)

        def two_steps(j, _):
            step(2 * j, 0)
            step(2 * j + 1, 1)
            return 0

        lax.fori_loop(first // 2, nslots // 2 - 1, two_steps, 0)
        step(nslots - 2, 0, bias_ref[0])
        step(nslots - 1, 1, bias_ref[1])
        k1, k2 = chunk_at(nslots - 1), chunk_at(nslots - 2)
        gb, sig, _ = weights(ls_refs[1][...], _dot(sp_refs[1][...], cmi, NN),
                             _dot(dosl, _key_chunk(v_ref, k1), NT), pr_ref[...], k1)
        gb2 = gb_refs[0][...]
        gs, dq = score_grads(gb2, sig_refs[0][...], _dot(gb2, cme, NN), gs_ref[...], dq_ref[...], k2)
        _, dq = score_grads(gb, sig, _dot(gb, cme, NN), gs, dq, k1)
        dq_ref[...] = dq

    def wrapped(used_ref, q_ref, k_ref, v_ref, do_ref, r_ref, cmi_ref, cme_ref, bias_ref, dq_ref, dk_ref, dv_ref,
                z0, z1, ls0, ls1, sg0, sg1, sp0, sp1, gb0, gb1, pr_ref, gs_ref):
        body(used_ref, q_ref, k_ref, v_ref, do_ref, r_ref, cmi_ref, cme_ref, bias_ref, dq_ref, dk_ref, dv_ref,
             (z0, z1), (ls0, ls1), (sg0, sg1), (sp0, sp1), (gb0, gb1), pr_ref, gs_ref)

    assert nc == 2
    bias = _diag_bias(tq, True)
    bias = jnp.concatenate([bias[:, :, :KEY_CHUNK], bias[:, :, KEY_CHUNK:]], axis=1)
    qblk = pl.BlockSpec((tq, LANES), lambda p, i, u: (i, p))
    full = pl.BlockSpec((s, LANES), lambda p, i, u: (0, p))
    cmspec = pl.BlockSpec((KEY_CHUNK, 2 * KEY_CHUNK), lambda p, i, u: (0, 0))
    shape = jax.ShapeDtypeStruct((s, nhp * LANES), F32)
    f32buf = pltpu.VMEM((2 * tq, LANES), F32)
    bf16buf = pltpu.VMEM((2 * tq, LANES), BF16)
    return pl.pallas_call(
        wrapped, name=name,
        grid_spec=pltpu.PrefetchScalarGridSpec(
            num_scalar_prefetch=1, grid=(nhp, s // tq),
            in_specs=[qblk,
                      pl.BlockSpec((s, LANES), lambda p, i, u: (0, nhp + p)),
                      pl.BlockSpec((s, LANES), lambda p, i, u: (0, 2 * nhp + p)),
                      qblk,
                      pl.BlockSpec((tq, 2 * LANES), lambda p, i, u: (i, p)),
                      cmspec, cmspec,
                      pl.BlockSpec((nc, 2 * tq, LANES), lambda p, i, u: (0, 0, 0))],
            out_specs=[qblk, full, full],
            scratch_shapes=[f32buf] * 6 + [bf16buf] * 4 + [f32buf] * 2),
        out_shape=[shape, shape, shape],
        compiler_params=_cparams("parallel", "arbitrary"),
    )(used, qk, qk, proj, dmix, rtot, _cumsum_matrix("upto"), _cumsum_matrix("before"), bias)


def _pair_cumsum_matrix(kind):
    j = lax.broadcasted_iota(jnp.int32, (2 * KEY_CHUNK, 4 * KEY_CHUNK), 0)
    c = lax.broadcasted_iota(jnp.int32, (2 * KEY_CHUNK, 4 * KEY_CHUNK), 1)
    same_head = (j // KEY_CHUNK) == ((c // KEY_CHUNK) % 2)
    jj, cc = j % KEY_CHUNK, c % KEY_CHUNK
    tri = {"after": jj > cc, "upto": jj <= cc, "before": jj < cc}[kind]
    return jnp.where(same_head & ((c >= 2 * KEY_CHUNK) | tri), 1.0, 0.0).astype(BF16)


def _diag_bias(tq, ascending):
    nc = tq // KEY_CHUNK
    shape = (nc, tq, 2 * KEY_CHUNK)
    d = lax.broadcasted_iota(jnp.int32, shape, 0)
    r = lax.broadcasted_iota(jnp.int32, shape, 1)
    c = lax.broadcasted_iota(jnp.int32, shape, 2) % KEY_CHUNK
    chunk = d if ascending else nc - 1 - d
    return jnp.where(chunk * KEY_CHUNK + c < r, 0.0, NEG_BIG).astype(F32)


def _attn_fwd(qk, proj, tq, name):
    s = qk.shape[0]
    nhp = qk.shape[1] // (2 * LANES)
    nc = tq // KEY_CHUNK
    assert nc == 2
    w = 2 * KEY_CHUNK

    def body(q_ref, k_ref, v_ref, cm_ref, bias_ref, o_ref, r_ref, used_ref, z_refs, ls_refs, cs_refs, ct_refs,
             sp_refs, ab_refs, acc_ref):
        qi = pl.program_id(1)
        nslots = (qi + 1) * nc
        m0 = lax.broadcasted_iota(jnp.int32, (1, LANES), 1) < HEAD_DIM
        q = q_ref[...]
        cm = cm_ref[...]

        def chunk_at(i):
            return jnp.clip(nslots - 1 - i, 0, nslots - 1)

        def scores(kc):
            return _dot(q, _stack_heads(_key_chunk(k_ref, kc), m0), NT)

        def values(ab, kc):
            return _dot(ab, _stack_heads(_key_chunk(v_ref, kc), m0), NN)

        def step(i, par, bias=None, stages="zscwv"):
            cur, prv = par, 1 - par
            if "z" in stages:
                z_next = scores(chunk_at(i + 1))
            if "c" in stages:
                cs = _dot(sp_refs[prv][...], cm, NN)
            if "v" in stages:
                pv = values(ab_refs[prv][...], chunk_at(i - 3))
            if "w" in stages:
                rs = r_ref[...]
                r_ref[...] = rs + ct_refs[cur][...]
                ab_refs[cur][...] = jnp.exp2(ls_refs[cur][...] - cs_refs[cur][...] - rs).astype(BF16)
            if "s" in stages:
                z = z_refs[cur][...]
                if bias is not None:
                    z = z + bias
                sp, ls = _softplus2(z)
                sp_refs[cur][...] = sp.astype(BF16)
                ls_refs[cur][...] = ls
            if "v" in stages:
                acc_ref[...] += pv
            if "c" in stages:
                cs_refs[prv][...] = cs[:, :w]
                ct_refs[prv][...] = cs[:, w:]
            if "z" in stages:
                z_refs[prv][...] = z_next

        z_refs[0][...] = scores(chunk_at(0))
        for p in range(2):
            sp_refs[p][...] = jnp.zeros((tq, w), BF16)
            ls_refs[p][...] = jnp.full((tq, w), NEG_BIG, F32)
            cs_refs[p][...] = jnp.zeros((tq, w), F32)
            ct_refs[p][...] = jnp.zeros((tq, w), F32)
            ab_refs[p][...] = jnp.zeros((tq, w), BF16)
        r_ref[...] = jnp.zeros((tq, w), F32)
        acc_ref[...] = jnp.zeros((tq, LANES), F32)
        step(0, 0, bias_ref[0])
        step(1, 1, bias_ref[1])

        def two_steps(carry):
            j, _ = carry
            step(2 * j, 0)
            step(2 * j + 1, 1)
            return j + 1, jnp.min(r_ref[...])

        pairs, _ = lax.while_loop(lambda c: jnp.logical_and(c[0] < nslots // 2, c[1] < SATURATED), two_steps,
                                  (jnp.int32(1), jnp.min(r_ref[...])))
        used = 2 * pairs
        step(used, 0, stages="cwv")
        step(used + 1, 1, stages="wv")
        step(used + 2, 0, stages="v")
        o_ref[...] = acc_ref[...].astype(o_ref.dtype)
        used_ref[pl.program_id(0), qi] = used

    def wrapped(q_ref, k_ref, v_ref, cm_ref, bias_ref, o_ref, r_ref, used_ref, *scratch):
        z, ls, cs, ct, sp, ab = [scratch[2 * j:2 * j + 2] for j in range(6)]
        body(q_ref, k_ref, v_ref, cm_ref, bias_ref, o_ref, r_ref, used_ref, z, ls, cs, ct, sp, ab, scratch[12])

    f32buf = pltpu.VMEM((tq, w), F32)
    bf16buf = pltpu.VMEM((tq, w), BF16)
    return pl.pallas_call(
        wrapped, name=name, grid=(nhp, s // tq),
        in_specs=[pl.BlockSpec((tq, LANES), lambda p, i: (i, p)),
                  pl.BlockSpec((s, LANES), lambda p, i: (0, nhp + p)),
                  pl.BlockSpec((s, LANES), lambda p, i: (0, 2 * nhp + p)),
                  pl.BlockSpec((w, 2 * w), lambda p, i: (0, 0)),
                  pl.BlockSpec((nc, tq, w), lambda p, i: (0, 0, 0))],
        out_specs=[pl.BlockSpec((tq, LANES), lambda p, i: (i, p)),
                   pl.BlockSpec((tq, w), lambda p, i: (i, p)),
                   pl.BlockSpec(memory_space=pltpu.SMEM)],
        out_shape=[jax.ShapeDtypeStruct((s, nhp * LANES), BF16),
                   jax.ShapeDtypeStruct((s, nhp * w), F32),
                   jax.ShapeDtypeStruct((nhp, s // tq), jnp.int32)],
        scratch_shapes=[f32buf] * 8 + [bf16buf] * 4 + [pltpu.VMEM((tq, LANES), F32)],
        compiler_params=_cparams("arbitrary", "arbitrary"),
    )(qk, qk, proj, _pair_cumsum_matrix("after"), _diag_bias(tq, False))


def _mm_blocks(h, ga, widx, tm, name):
    s, d = h.shape
    nb, cols = ga.shape[1], ga.shape[3]

    def body(a_ref, b_ref, o_ref):
        o_ref[...] = _dot(a_ref[...], b_ref[...], NN).astype(o_ref.dtype)

    return pl.pallas_call(
        body, name=name, grid=(s // tm, nb),
        in_specs=[pl.BlockSpec((tm, d), lambda i, j: (i, 0)),
                  pl.BlockSpec((None, None, d, cols), lambda i, j: (widx, j, 0, 0))],
        out_specs=pl.BlockSpec((tm, cols), lambda i, j: (i, j)),
        out_shape=jax.ShapeDtypeStruct((s, nb * cols), BF16),
        compiler_params=_cparams("parallel", "arbitrary"),
    )(h, ga)


def _mm_swiglu(h, ga, gidx, uidx, tm, name):
    s, d = h.shape
    nb, cols = ga.shape[1], ga.shape[3]

    def body(a_ref, bg_ref, bu_ref, g_ref, u_ref, act_ref):
        a = a_ref[...]
        g = _dot(a, bg_ref[...], NN)
        u = _dot(a, bu_ref[...], NN)
        g_ref[...] = g.astype(g_ref.dtype)
        u_ref[...] = u.astype(u_ref.dtype)
        act_ref[...] = (g * (1.0 / (1.0 + jnp.exp(-g))) * u).astype(act_ref.dtype)

    def wspec(idx):
        return pl.BlockSpec((None, None, d, cols), lambda i, j: (idx, j, 0, 0))

    out = pl.BlockSpec((tm, cols), lambda i, j: (i, j))
    shape = jax.ShapeDtypeStruct((s, nb * cols), BF16)
    return pl.pallas_call(
        body, name=name, grid=(s // tm, nb),
        in_specs=[pl.BlockSpec((tm, d), lambda i, j: (i, 0)), wspec(gidx), wspec(uidx)],
        out_specs=[out, out, out], out_shape=[shape, shape, shape],
        compiler_params=_cparams("parallel", "arbitrary"),
    )(h, ga, ga)


def _mm_residual(a, w3, lidx, res, tm, tn, name):
    s, k = a.shape
    n = w3.shape[2]

    def body(a_ref, b_ref, r_ref, o_ref):
        o_ref[...] = r_ref[...] + _dot(a_ref[...], b_ref[...], NN)

    return pl.pallas_call(
        body, name=name, grid=(s // tm, n // tn),
        in_specs=[pl.BlockSpec((tm, k), lambda i, j: (i, 0)),
                  pl.BlockSpec((None, k, tn), lambda i, j: (lidx, 0, j)),
                  pl.BlockSpec((tm, tn), lambda i, j: (i, j))],
        out_specs=pl.BlockSpec((tm, tn), lambda i, j: (i, j)),
        out_shape=jax.ShapeDtypeStruct((s, n), F32),
        compiler_params=_cparams("parallel", "arbitrary"),
    )(a, w3, res)


def _mm_nt(a, w3, lidx, tm, tn, name):
    s, k = a.shape
    n = w3.shape[1]

    def body(a_ref, b_ref, o_ref):
        o_ref[...] = _dot(a_ref[...], b_ref[...], NT)

    return pl.pallas_call(
        body, name=name, grid=(s // tm, n // tn),
        in_specs=[pl.BlockSpec((tm, k), lambda i, j: (i, 0)),
                  pl.BlockSpec((None, tn, k), lambda i, j: (lidx, j, 0))],
        out_specs=pl.BlockSpec((tm, tn), lambda i, j: (i, j)),
        out_shape=jax.ShapeDtypeStruct((s, n), F32),
        compiler_params=_cparams("parallel", "arbitrary"),
    )(a, w3)


def _mm_nt_swiglu_bwd(dx, wd3, lidx, g, u, tm, name):
    s, d = dx.shape
    cols = g.shape[1] // N_DEV

    def body(a_ref, b_ref, g_ref, u_ref, dg_ref, du_ref):
        dact = _dot(a_ref[...], b_ref[...], NT)
        gv = g_ref[...].astype(F32)
        sig = 1.0 / (1.0 + jnp.exp(-gv))
        du_ref[...] = (dact * (gv * sig)).astype(du_ref.dtype)
        dg_ref[...] = (dact * u_ref[...].astype(F32) * (sig * (1.0 + gv * (1.0 - sig)))).astype(dg_ref.dtype)

    blk = pl.BlockSpec((tm, cols), lambda i, j: (i, j))
    shape = jax.ShapeDtypeStruct(g.shape, BF16)
    return pl.pallas_call(
        body, name=name, grid=(s // tm, N_DEV),
        in_specs=[pl.BlockSpec((tm, d), lambda i, j: (i, 0)),
                  pl.BlockSpec((None, cols, d), lambda i, j: (lidx, j, 0)), blk, blk],
        out_specs=[blk, blk], out_shape=[shape, shape],
        compiler_params=_cparams("parallel", "arbitrary"),
    )(dx, wd3, g, u)


def _mm_nt_blocks(das, ga, widxs, tm, name):
    s = das[0].shape[0]
    nb, d, cols = ga.shape[1], ga.shape[2], ga.shape[3]
    nw = len(das)

    def body(*refs):
        a_refs, b_refs, o_ref = refs[:nw], refs[nw:2 * nw], refs[2 * nw]
        acc = None
        for w in range(nw):
            for k in range(nb):
                part = _dot(a_refs[w][:, k * cols:(k + 1) * cols], b_refs[w][k], NT)
                acc = part if acc is None else acc + part
        o_ref[...] = acc

    def wspec(idx):
        return pl.BlockSpec((None, nb, d, cols), lambda i: (idx, 0, 0, 0), pipeline_mode=pl.Buffered(1))

    return pl.pallas_call(
        body, name=name, grid=(s // tm,),
        in_specs=[pl.BlockSpec((tm, nb * cols), lambda i: (i, 0))] * nw + [wspec(i) for i in widxs],
        out_specs=pl.BlockSpec((tm, d), lambda i: (i, 0)),
        out_shape=jax.ShapeDtypeStruct((s, d), F32),
        compiler_params=_cparams("parallel"),
    )(*das, *([ga] * nw))


def _mm_tn(a, b, ta, tb, tk, out_blocks, name):
    s, ka = a.shape
    nb = b.shape[1]
    nk = s // tk

    def body(a_ref, b_ref, o_ref, ob_ref):
        k = pl.program_id(2)
        part = _dot(a_ref[...], b_ref[...], TN)

        @pl.when(k == 0)
        def _():
            o_ref[...] = part

        @pl.when(k > 0)
        def _():
            o_ref[...] += part

        @pl.when(k == nk - 1)
        def _():
            ob_ref[...] = o_ref[...].astype(ob_ref.dtype)

    if out_blocks:
        out_spec = pl.BlockSpec((None, ta, tb), lambda i, j, k: (j, i, 0))
        shape = (nb // tb, ka, tb)
    else:
        out_spec = pl.BlockSpec((ta, tb), lambda i, j, k: (i, j))
        shape = (ka, nb)
    return pl.pallas_call(
        body, name=name, grid=(ka // ta, nb // tb, nk),
        in_specs=[pl.BlockSpec((tk, ta), lambda i, j, k: (k, i)),
                  pl.BlockSpec((tk, tb), lambda i, j, k: (k, j))],
        out_specs=[out_spec, out_spec],
        out_shape=[jax.ShapeDtypeStruct(shape, F32), jax.ShapeDtypeStruct(shape, BF16)],
        compiler_params=_cparams("parallel", "parallel", "arbitrary"),
    )(a, b)


def _loss_head(y, target, tm, name):
    s, d = y.shape
    nsteps = s // tm

    def body(y_ref, t_ref, dy_ref, dyb_ref, l_ref, acc):
        i = pl.program_id(0)
        diff = y_ref[...] - t_ref[...]
        dy_ref[...] = diff * (1.0 / d)
        dyb_ref[...] = (diff * (1.0 / d)).astype(dyb_ref.dtype)
        part = jnp.sum((diff * diff).reshape(tm // 8, 8, d), axis=0)

        @pl.when(i == 0)
        def _():
            acc[...] = part

        @pl.when(i > 0)
        def _():
            acc[...] += part

        @pl.when(i == nsteps - 1)
        def _():
            tot = jnp.sum(jnp.sum(acc[...], axis=1, keepdims=True), axis=0, keepdims=True)
            l_ref[...] = jnp.broadcast_to(tot * (0.5 / d), (8, LANES))

    row = pl.BlockSpec((tm, d), lambda i: (i, 0))
    return pl.pallas_call(
        body, name=name, grid=(nsteps,),
        in_specs=[row, row],
        out_specs=[row, row, pl.BlockSpec((8, LANES), lambda i: (0, 0))],
        out_shape=[jax.ShapeDtypeStruct((s, d), F32), jax.ShapeDtypeStruct((s, d), BF16),
                   jax.ShapeDtypeStruct((8, LANES), F32)],
        scratch_shapes=[pltpu.VMEM((8, d), F32)],
        compiler_params=_cparams("arbitrary"),
    )(y, target)


def _adamw(parts, own, w, m, v, tr, name):
    p, rows, cols = parts.shape
    c1 = 1.0 / (1.0 - ADAM_B1 ** ADAM_STEP)
    c2 = 1.0 / (1.0 - ADAM_B2 ** ADAM_STEP)

    def body(*refs):
        if own is None:
            p_ref, w_ref, m_ref, v_ref, g_ref, d_ref, nm_ref, nv_ref = refs
            g = p_ref[0]
            for k in range(1, p):
                g = g + p_ref[k]
        else:
            p_ref, own_ref, w_ref, m_ref, v_ref, g_ref, d_ref, nm_ref, nv_ref = refs
            x, y, c = _place()
            my = 4 * x + 2 * y + c
            mine = own_ref[...]
            g = jnp.where(my == 0, mine, p_ref[0].astype(F32))
            for k in range(1, p):
                g = g + jnp.where(my == k, mine, p_ref[k].astype(F32))
        nm = ADAM_B1 * m_ref[...] + (1.0 - ADAM_B1) * g
        nv = ADAM_B2 * v_ref[...] + (1.0 - ADAM_B2) * (g * g)
        g_ref[...] = g
        nm_ref[...] = nm
        nv_ref[...] = nv
        d_ref[...] = -ADAM_LR * ((nm * c1) / (jnp.sqrt(nv * c2) + ADAM_EPS) + ADAM_WD * w_ref[...])

    blk = pl.BlockSpec((tr, cols), lambda i: (i, 0))
    shape = jax.ShapeDtypeStruct((rows, cols), F32)
    return pl.pallas_call(
        body, name=name, grid=(rows // tr,),
        in_specs=[pl.BlockSpec((p, tr, cols), lambda i: (0, i, 0))] + [blk] * (3 if own is None else 4),
        out_specs=[blk] * 4, out_shape=[shape] * 4,
        compiler_params=_cparams("parallel"),
    )(*([parts] + ([] if own is None else [own]) + [w, m, v]))


def _place():
    x, y, c = lax.axis_index("x"), lax.axis_index("y"), lax.axis_index("c")
    return x, y, c


def _all_gather(shards, name):
    na = len(shards)

    def body(*refs):
        srcs, dsts = refs[:na], refs[na:2 * na]
        send_sems, recv_sems, local_sems = refs[2 * na:]
        x, y, c = _place()
        me, sibling = (x, y, c), (x, y, 1 - c)
        chips = [(1 - x, y), (x, 1 - y), (1 - x, 1 - y)]

        def slot(a, dev):
            return dsts[a].at[:, pl.ds(4 * dev[0] + 2 * dev[1] + dev[2], 1)]

        def copy(k, a, block, to, from_shard=False):
            return pltpu.make_async_remote_copy(
                src_ref=srcs[a] if from_shard else slot(a, block), dst_ref=slot(a, block),
                send_sem=send_sems.at[k, a], recv_sem=recv_sems.at[k, a], device_id=to, device_id_type=MESH)

        mine = [pltpu.make_async_copy(srcs[a], slot(a, me), local_sems.at[a]) for a in range(na)]
        for cp in mine:
            cp.start()
        first = [copy(0, a, me, sibling, True) for a in range(na)]
        first += [copy(1 + j, a, me, (*chip, c), True) for j, chip in enumerate(chips) for a in range(na)]
        for cp in first:
            cp.start()
        passed = []
        for j, chip in enumerate(chips):
            for a in range(na):
                copy(1 + j, a, (*chip, c), me).wait_recv()
                fwd = copy(4 + j, a, (*chip, c), sibling)
                fwd.start()
                passed.append(fwd)
        for a in range(na):
            copy(0, a, sibling, me).wait_recv()
        for j, chip in enumerate(chips):
            for a in range(na):
                copy(4 + j, a, (*chip, 1 - c), me).wait_recv()
        for cp in first + passed:
            cp.wait_send()
        for cp in mine:
            cp.wait()

    anyspec = pl.BlockSpec(memory_space=pl.ANY)
    return pl.pallas_call(
        body, name=name,
        in_specs=[anyspec] * na, out_specs=[anyspec] * na,
        out_shape=[jax.ShapeDtypeStruct((a.shape[0], N_DEV) + a.shape[2:], a.dtype) for a in shards],
        scratch_shapes=[pltpu.SemaphoreType.DMA((7, na)), pltpu.SemaphoreType.DMA((7, na)),
                        pltpu.SemaphoreType.DMA((na,))],
    )(*shards)


_RELATIONS = [(dx, dy, dc) for dx in (0, 1) for dy in (0, 1) for dc in (0, 1)][1:]


def _flip(v, d):
    return 1 - v if d else v


def _exchange_blocks(grads, name):
    na = len(grads)

    def body(*refs):
        srcs, dsts = refs[:na], refs[na:2 * na]
        send_sems, recv_sems, local_sems = refs[2 * na:]
        x, y, c = _place()
        my = 4 * x + 2 * y + c
        mine = [pltpu.make_async_copy(srcs[a].at[pl.ds(my, 1)], dsts[a].at[pl.ds(my, 1)], local_sems.at[a])
                for a in range(na)]
        for cp in mine:
            cp.start()
        sends = []
        for k, (dx, dy, dc) in enumerate(_RELATIONS):
            peer = (_flip(x, dx), _flip(y, dy), _flip(c, dc))
            pidx = 4 * peer[0] + 2 * peer[1] + peer[2]
            for a in range(na):
                cp = pltpu.make_async_remote_copy(
                    src_ref=srcs[a].at[pl.ds(pidx, 1)], dst_ref=dsts[a].at[pl.ds(my, 1)],
                    send_sem=send_sems.at[k, a], recv_sem=recv_sems.at[k, a], device_id=peer, device_id_type=MESH)
                cp.start()
                sends.append((cp, pidx, k, a, peer))
        for cp, pidx, k, a, peer in sends:
            pltpu.make_async_remote_copy(
                src_ref=srcs[a].at[pl.ds(pidx, 1)], dst_ref=dsts[a].at[pl.ds(pidx, 1)],
                send_sem=send_sems.at[k, a], recv_sem=recv_sems.at[k, a], device_id=peer,
                device_id_type=MESH).wait_recv()
        for cp, *_ in sends:
            cp.wait_send()
        for cp in mine:
            cp.wait()

    anyspec = pl.BlockSpec(memory_space=pl.ANY)
    return pl.pallas_call(
        body, name=name,
        in_specs=[anyspec] * na, out_specs=[anyspec] * na,
        out_shape=[jax.ShapeDtypeStruct(a.shape, a.dtype) for a in grads],
        scratch_shapes=[pltpu.SemaphoreType.DMA((7, na)), pltpu.SemaphoreType.DMA((7, na)),
                        pltpu.SemaphoreType.DMA((na,))],
    )(*grads)


def _all_reduce_small(v, name):
    r, c_ = v.shape

    def body(v_ref, o_ref, gath, send_sems, recv_sems):
        x, y, c = _place()
        my = 4 * x + 2 * y + c
        gath[my] = v_ref[...]
        sends = []
        for k, (dx, dy, dc) in enumerate(_RELATIONS):
            peer = (_flip(x, dx), _flip(y, dy), _flip(c, dc))
            cp = pltpu.make_async_remote_copy(
                src_ref=v_ref, dst_ref=gath.at[my], send_sem=send_sems.at[k], recv_sem=recv_sems.at[k],
                device_id=peer, device_id_type=MESH)
            cp.start()
            sends.append((cp, 4 * peer[0] + 2 * peer[1] + peer[2], k, peer))
        for cp, pidx, k, peer in sends:
            pltpu.make_async_remote_copy(
                src_ref=v_ref, dst_ref=gath.at[pidx], send_sem=send_sems.at[k], recv_sem=recv_sems.at[k],
                device_id=peer, device_id_type=MESH).wait_recv()
        for cp, *_ in sends:
            cp.wait_send()
        tot = gath[0]
        for k in range(1, N_DEV):
            tot = tot + gath[k]
        o_ref[...] = tot

    vm = pl.BlockSpec(memory_space=pltpu.VMEM)
    return pl.pallas_call(
        body, name=name, in_specs=[vm], out_specs=vm,
        out_shape=jax.ShapeDtypeStruct((r, c_), F32),
        scratch_shapes=[pltpu.VMEM((N_DEV, r, c_), F32), pltpu.SemaphoreType.DMA((7,)),
                        pltpu.SemaphoreType.DMA((7,))],
    )(v)


TM = 512
TM_MATMUL = 2048
TM_RESIDUAL = 1024
TQ = 256


def _pad_to(a, axis, size):
    pad = [(0, 0)] * a.ndim
    pad[axis] = (0, size - a.shape[axis])
    return jnp.pad(a, pad)


def _local_step(x, target, ga, gb, gc, conv_full, norm_mix, q_norm, k_norm, norm_ffn):
    depth = gb.shape[0]
    tm, tq = min(TM, x.shape[0]), min(TQ, x.shape[0])
    tmm, tmr = min(TM_MATMUL, x.shape[0]), min(TM_RESIDUAL, x.shape[0])
    attn = gb.shape[1] // 2
    nheads = attn // HEAD_DIM
    scale = HEAD_DIM ** -0.5 * LOG2E
    saved = []
    for l in range(depth):
        h1 = _rmsnorm_fwd(x, norm_mix[l][None], tm,f"norm_mix_fwd_{l}")
        proj = _mm_blocks(h1, ga, 3 * l, tmm, f"proj_in_{l}")
        qk_gain = jnp.concatenate([jnp.tile(q_norm[l], nheads) * scale, jnp.tile(k_norm[l], nheads)])[None]
        qk = _qknorm_fwd(proj, qk_gain, tmm, f"qknorm_fwd_{l}")
        o, rtot, used = _attn_fwd(qk, proj, tq, f"attn_fwd_{l}")
        conv_w8 = _pad_to(conv_full[l], 0, 8)
        cv = _conv_fwd(proj, conv_w8, f"conv_fwd_{l}")
        mix = jnp.concatenate([o, cv], axis=1)
        x1 = _mm_residual(mix, gb, l, x, tmr, 512, f"proj_out_{l}")
        h2 = _rmsnorm_fwd(x1, norm_ffn[l][None], tm,f"norm_ffn_fwd_{l}")
        g, u, act = _mm_swiglu(h2, ga, 3 * l + 1, 3 * l + 2, tmm, f"ffn_up_{l}")
        x2 = _mm_residual(act, gc, l, x1, tmr, 512, f"ffn_down_{l}")
        saved.append((x, h1, proj, qk_gain, qk, rtot, used, conv_w8, mix, x1, h2, g, u, act))
        x = x2

    dx, dxb, loss = _loss_head(x, target, tm, "loss_head")

    grads = [None] * depth
    small = [None] * depth
    for l in reversed(range(depth)):
        x0, h1, proj, qk_gain, qk, rtot, used, conv_w8, mix, x1, h2, g, u, act = saved[l]
        d = x0.shape[1]
        dg, du = _mm_nt_swiglu_bwd(dxb, gc, l, g, u, tmm, f"ffn_down_bwd_{l}")
        d_wdown = _mm_tn(act, dxb, 768, d, tmm, False, f"dw_down_{l}")
        d_wgate = _mm_tn(h2, dg, d, ga.shape[3], tmm, True, f"dw_gate_{l}")
        d_wup = _mm_tn(h2, du, d, ga.shape[3], tmm, True, f"dw_up_{l}")
        dh2 = _mm_nt_blocks([dg, du], ga, [3 * l + 1, 3 * l + 2], tm, f"ffn_up_bwd_{l}")
        dx1, dx1b, dg_ffn = _rmsnorm_bwd(dh2, x1, norm_ffn[l][None], dx, tm, f"norm_ffn_bwd_{l}")
        dmix = _mm_nt(dx1b, gb, l, tmr, 512, f"proj_out_bwd_{l}")
        d_wout = _mm_tn(mix, dx1b, 512, d, tmm, False, f"dw_out_{l}")
        dcb, dcc, dcu, dconv = _conv_bwd(dmix, proj, conv_w8, f"conv_bwd_{l}")
        dq, dk, dv = _attn_bwd(qk, proj, dmix, rtot, used, tq, f"attn_bwd_{l}")
        dqk, dg_qk = _qknorm_bwd(jnp.concatenate([dq, dk], axis=1), proj, qk_gain, tmm, f"qknorm_bwd_{l}")
        dproj = jnp.concatenate([dqk, dv.astype(BF16), dcb, dcc, dcu], axis=1)
        d_win = _mm_tn(h1, dproj, d, ga.shape[3], tmm, True, f"dw_in_{l}")
        dh1 = _mm_nt_blocks([dproj], ga, [3 * l], tmr, f"proj_in_bwd_{l}")
        dx, dxb, dg_mix = _rmsnorm_bwd(dh1, x0, norm_mix[l][None], dx1, tm, f"norm_mix_bwd_{l}")
        grads[l] = (d_win, d_wgate, d_wup, d_wout, d_wdown)
        dq_gain = jnp.sum(dg_qk[0, :attn].reshape(nheads, HEAD_DIM), axis=0) * scale
        dk_gain = jnp.sum(dg_qk[0, attn:].reshape(nheads, HEAD_DIM), axis=0)
        small[l] = (dg_mix[0], dg_ffn[0], dq_gain, dk_gain, dconv[:3])
    return loss, dx, grads, small


def kernel(x, norm_mix, w_in, q_norm, k_norm, conv_w, w_out, norm_ffn, w_gate, w_up, w_down, loss_target, m_norm_mix, m_w_in, m_q_norm, m_k_norm, m_conv_w, m_w_out, m_norm_ffn, m_w_gate, m_w_up, m_w_down, v_norm_mix, v_w_in, v_q_norm, v_k_norm, v_conv_w, v_w_out, v_norm_ffn, v_w_gate, v_w_up, v_w_down):
    depth, d, in_shard = w_in.shape
    ff_shard = w_gate.shape[2]
    ff_pad = in_shard
    conv_shard = conv_w.shape[2]
    xs = x.reshape(x.shape[-2], d)
    target = loss_target.reshape(xs.shape)

    pa = jnp.stack([w_in, _pad_to(w_gate, 2, ff_pad), _pad_to(w_up, 2, ff_pad)], axis=1)
    pa = pa.reshape(3 * depth, 1, d, in_shard).astype(BF16)
    pb = w_out.astype(BF16)[:, None]
    pc = _pad_to(w_down, 1, ff_pad).astype(BF16)[:, None]
    pd = _pad_to(_pad_to(conv_w.reshape(depth * 3, conv_shard), 0, 8), 1, LANES)[None, None]
    ga, gb, gc, gd = _all_gather([pa, pb, pc, pd], "gather_weights")
    gb = gb.reshape(depth, N_DEV * gb.shape[2], d)
    gc = gc.reshape(depth, N_DEV * ff_pad, d)
    conv_full = gd[0, :, :depth * 3, :conv_shard].transpose(1, 0, 2).reshape(depth, 3, N_DEV * conv_shard)

    loss, grad_x, grads, small = _local_step(xs, target, ga, gb, gc, conv_full, norm_mix, q_norm, k_norm, norm_ffn)

    x_, y_, c_ = _place()
    my = 4 * x_ + 2 * y_ + c_
    block_rows = (d, d, d, w_out.shape[1], ff_pad)
    send = [grads[l][j][1].reshape(N_DEV, block_rows[j], -1) for l in range(depth) for j in range(5)]
    landed = _exchange_blocks(send, "exchange_grads")
    own = [lax.dynamic_index_in_dim(grads[l][j][0].reshape(N_DEV, block_rows[j], -1), my, 0, keepdims=False)
           for l in range(depth) for j in range(5)]

    nconv = N_DEV * conv_shard
    rows = []
    for l in range(depth):
        g_mix, g_ffn, g_q, g_k, g_conv = small[l]
        qkrow = _pad_to(jnp.concatenate([g_q, g_k]), 0, d)
        rows += [g_mix[None], g_ffn[None], qkrow[None], _pad_to(g_conv, 1, d)]
    nrow = 6 * depth
    packed = jnp.concatenate(rows + [_pad_to(loss[:1], 1, d)], axis=0)
    packed = _pad_to(packed, 0, ((nrow + 1 + 7) // 8) * 8)
    summed = _all_reduce_small(packed, "reduce_small")
    loss_out = summed[nrow, 0]

    def big(i, w, m, v, tr, name, rows_=None, cols_=None):
        parts = landed[i]
        pr, pcn = parts.shape[1], parts.shape[2]
        w, m, v = [_pad_to(_pad_to(t, 0, pr), 1, pcn) for t in (w, m, v)]
        outs = _adamw(parts, own[i], w, m, v, tr, name)
        return [o[:rows_ or pr, :cols_ or pcn] for o in outs]

    res = {}
    for l in range(depth):
        i = 5 * l
        res[("w_in", l)] = big(i, w_in[l], m_w_in[l], v_w_in[l], 256, f"adamw_in_{l}")
        res[("w_gate", l)] = big(i + 1, w_gate[l], m_w_gate[l], v_w_gate[l], 256, f"adamw_gate_{l}", cols_=ff_shard)
        res[("w_up", l)] = big(i + 2, w_up[l], m_w_up[l], v_w_up[l], 256, f"adamw_up_{l}", cols_=ff_shard)
        res[("w_out", l)] = big(i + 3, w_out[l], m_w_out[l], v_w_out[l], w_out.shape[1], f"adamw_out_{l}")
        res[("w_down", l)] = big(i + 4, w_down[l], m_w_down[l], v_w_down[l], 128, f"adamw_down_{l}",
                                 rows_=ff_shard)

    g_rows, w_rows, m_rows, v_rows = [], [], [], []
    for l in range(depth):
        base = l * 6
        conv_g = lax.dynamic_slice(summed[base + 3:base + 6], (0, my * conv_shard), (3, conv_shard))
        g_rows += [summed[base:base + 3], _pad_to(conv_g, 1, d)]
        for dst, (nm, qn, kn, nf, cw) in ((w_rows, (norm_mix, q_norm, k_norm, norm_ffn, conv_w)),
                                          (m_rows, (m_norm_mix, m_q_norm, m_k_norm, m_norm_ffn, m_conv_w)),
                                          (v_rows, (v_norm_mix, v_q_norm, v_k_norm, v_norm_ffn, v_conv_w))):
            dst += [nm[l][None], nf[l][None], _pad_to(jnp.concatenate([qn[l], kn[l]]), 0, d)[None],
                    _pad_to(cw[l], 1, d)]
    prow = ((nrow + 7) // 8) * 8
    gs, ws, ms, vs = [_pad_to(jnp.concatenate(t, axis=0), 0, prow) for t in (g_rows, w_rows, m_rows, v_rows)]
    sm = _adamw(gs[None], None, ws, ms, vs, prow, "adamw_small")

    hd = q_norm.shape[1]

    def small_out(t, kind):
        per_layer = []
        for l in range(depth):
            base = l * 6
            per_layer.append({"norm_mix": t[base], "norm_ffn": t[base + 1], "q_norm": t[base + 2, :hd],
                              "k_norm": t[base + 2, hd:2 * hd], "conv_w": t[base + 3:base + 6, :conv_shard]}[kind])
        return jnp.stack(per_layer)

    def big_out(name, i):
        return jnp.stack([res[(name, l)][i] for l in range(depth)])

    outs = [loss_out, grad_x.reshape(x.shape)]
    for i in range(4):
        outs += [small_out(sm[i], "norm_mix"), big_out("w_in", i), small_out(sm[i], "q_norm"),
                 small_out(sm[i], "k_norm"), small_out(sm[i], "conv_w"), big_out("w_out", i),
                 small_out(sm[i], "norm_ffn"), big_out("w_gate", i), big_out("w_up", i), big_out("w_down", i)]
    return tuple(outs)
```

```python
import jax
import jax.numpy as jnp
from jax import lax
from jax.experimental import pallas as pl
from jax.experimental.pallas import tpu as pltpu

F32 = jnp.float32
BF16 = jnp.bfloat16
MESH = pl.DeviceIdType.MESH

N_DEV = 8
LANES = 128
HEAD_DIM = 64
KEY_CHUNK = 128
EPS = 1e-6
VMEM_LIMIT = 48 * 1024 * 1024

ADAM_LR = 0.001
ADAM_B1 = 0.9
ADAM_B2 = 0.999
ADAM_EPS = 1e-08
ADAM_WD = 0.01
ADAM_STEP = 10

NN = (((1,), (0,)), ((), ()))
NT = (((1,), (1,)), ((), ()))
TN = (((0,), (0,)), ((), ()))


def _dot(a, b, dims):
    return lax.dot_general(a.astype(BF16), b.astype(BF16), dims, preferred_element_type=F32)


def _cparams(*sem):
    return pltpu.CompilerParams(dimension_semantics=sem, vmem_limit_bytes=VMEM_LIMIT)


def _split_hi_lo(v):
    hi = v.astype(BF16)
    lo = (v - hi.astype(F32)).astype(BF16)
    return jnp.concatenate([hi, lo], axis=1)


def _rmsnorm_fwd(x, gain, tm, name):
    s, d = x.shape

    def body(x_ref, g_ref, o_ref):
        xv = x_ref[...]
        r = lax.rsqrt(jnp.mean(xv * xv, axis=-1, keepdims=True) + EPS)
        o_ref[...] = ((xv * r) * g_ref[...]).astype(o_ref.dtype)

    return pl.pallas_call(
        body, name=name, grid=(s // tm,),
        in_specs=[pl.BlockSpec((tm, d), lambda i: (i, 0)), pl.BlockSpec((1, d), lambda i: (0, 0))],
        out_specs=pl.BlockSpec((tm, d), lambda i: (i, 0)),
        out_shape=jax.ShapeDtypeStruct((s, d), BF16),
        compiler_params=_cparams("parallel"),
    )(x, gain)


def _rmsnorm_bwd(dh, x, gain, dres, tm, name):
    s, d = x.shape
    nsteps = s // tm

    def body(dh_ref, x_ref, g_ref, dres_ref, dx_ref, dxb_ref, dg_ref):
        i = pl.program_id(0)
        xv = x_ref[...]
        r = lax.rsqrt(jnp.mean(xv * xv, axis=-1, keepdims=True) + EPS)
        xhat = xv * r
        dhv = dh_ref[...]
        dxh = dhv * g_ref[...]
        proj = jnp.mean(dxh * xhat, axis=-1, keepdims=True)
        dxv = dres_ref[...] + r * (dxh - xhat * proj)
        dx_ref[...] = dxv
        dxb_ref[...] = dxv.astype(dxb_ref.dtype)
        part = jnp.sum((dhv * xhat).reshape(tm // 8, 8, d), axis=0)

        @pl.when(i == 0)
        def _():
            dg_ref[...] = part

        @pl.when(i > 0)
        def _():
            dg_ref[...] += part

        @pl.when(i == nsteps - 1)
        def _():
            dg_ref[...] = jnp.broadcast_to(jnp.sum(dg_ref[...], axis=0, keepdims=True), (8, d))

    row = pl.BlockSpec((tm, d), lambda i: (i, 0))
    return pl.pallas_call(
        body, name=name, grid=(nsteps,),
        in_specs=[row, row, pl.BlockSpec((1, d), lambda i: (0, 0)), row],
        out_specs=[row, row, pl.BlockSpec((8, d), lambda i: (0, 0))],
        out_shape=[jax.ShapeDtypeStruct((s, d), F32), jax.ShapeDtypeStruct((s, d), BF16),
                   jax.ShapeDtypeStruct((8, d), F32)],
        compiler_params=_cparams("arbitrary"),
    )(dh, x, gain, dres)


def _group_mean_matrix():
    r = lax.broadcasted_iota(jnp.int32, (LANES, LANES), 0) // HEAD_DIM
    c = lax.broadcasted_iota(jnp.int32, (LANES, LANES), 1) // HEAD_DIM
    return jnp.where(r == c, 1.0 / HEAD_DIM, 0.0).astype(BF16)


def _group_mean(v, gm):
    hi = v.astype(BF16)
    lo = (v - hi.astype(F32)).astype(BF16)
    return _dot(hi, gm, NN) + _dot(lo, gm, NN)


def _qknorm_fwd(proj, gains, tm, name):
    s = proj.shape[0]
    ncol = gains.shape[1] // LANES

    def body(p_ref, g_ref, gm_ref, o_ref):
        xv = p_ref[...].astype(F32)
        r = lax.rsqrt(_group_mean(xv * xv, gm_ref[...]) + EPS)
        o_ref[...] = ((xv * r) * g_ref[...]).astype(o_ref.dtype)

    blk = pl.BlockSpec((tm, LANES), lambda i, j: (i, j))
    return pl.pallas_call(
        body, name=name, grid=(s // tm, ncol),
        in_specs=[blk, pl.BlockSpec((1, LANES), lambda i, j: (0, j)),
                  pl.BlockSpec((LANES, LANES), lambda i, j: (0, 0))],
        out_specs=blk,
        out_shape=jax.ShapeDtypeStruct((s, ncol * LANES), BF16),
        compiler_params=_cparams("parallel", "parallel"),
    )(proj, gains, _group_mean_matrix())


def _qknorm_bwd(dqk, proj, gains, tm, name):
    s = proj.shape[0]
    ncol = gains.shape[1] // LANES
    nsteps = s // tm

    def body(dy_ref, p_ref, g_ref, gm_ref, dx_ref, dg_ref):
        i = pl.program_id(1)
        gm = gm_ref[...]
        xv = p_ref[...].astype(F32)
        r = lax.rsqrt(_group_mean(xv * xv, gm) + EPS)
        xhat = xv * r
        dy = dy_ref[...]
        dxh = dy * g_ref[...]
        proj_ = _group_mean(dxh * xhat, gm)
        dx_ref[...] = (r * (dxh - xhat * proj_)).astype(dx_ref.dtype)
        part = jnp.sum((dy * xhat).reshape(tm // 8, 8, LANES), axis=0)

        @pl.when(i == 0)
        def _():
            dg_ref[...] = part

        @pl.when(i > 0)
        def _():
            dg_ref[...] += part

        @pl.when(i == nsteps - 1)
        def _():
            dg_ref[...] = jnp.broadcast_to(jnp.sum(dg_ref[...], axis=0, keepdims=True), (8, LANES))

    blk = pl.BlockSpec((tm, LANES), lambda j, i: (i, j))
    return pl.pallas_call(
        body, name=name, grid=(ncol, nsteps),
        in_specs=[blk, blk, pl.BlockSpec((1, LANES), lambda j, i: (0, j)),
                  pl.BlockSpec((LANES, LANES), lambda j, i: (0, 0))],
        out_specs=[blk, pl.BlockSpec((8, LANES), lambda j, i: (0, j))],
        out_shape=[jax.ShapeDtypeStruct((s, ncol * LANES), BF16),
                   jax.ShapeDtypeStruct((8, ncol * LANES), F32)],
        compiler_params=_cparams("parallel", "arbitrary"),
    )(dqk, proj, gains, _group_mean_matrix())


CONV_ROWS = 256
HALO = 8


def _conv_fwd(proj, conv_w8, name):
    s = proj.shape[0]
    nblk = conv_w8.shape[1] // LANES
    first = 3 * nblk
    nchunk = s // CONV_ROWS

    def body(cb_ref, cc_ref, cu_ref, w_ref, y_ref, hpad):
        hpad[pl.ds(0, 2 * HALO), :] = jnp.zeros((2 * HALO, LANES), F32)

        def fill(i, _):
            r0 = pl.multiple_of(i * CONV_ROWS, CONV_ROWS)
            hpad[pl.ds(r0 + 2 * HALO, CONV_ROWS), :] = (
                cc_ref[pl.ds(r0, CONV_ROWS), :].astype(F32) * cu_ref[pl.ds(r0, CONV_ROWS), :].astype(F32))
            return 0

        lax.fori_loop(0, nchunk, fill, 0)
        w0, w1, w2 = w_ref[0:1, :], w_ref[1:2, :], w_ref[2:3, :]

        def conv(i, _):
            r0 = pl.multiple_of(i * CONV_ROWS, CONV_ROWS)
            win = hpad[pl.ds(r0 + HALO, CONV_ROWS + HALO), :]
            c = (w2 * win[HALO:] + w1 * pltpu.roll(win, 1, 0)[HALO:] + w0 * pltpu.roll(win, 2, 0)[HALO:])
            y_ref[pl.ds(r0, CONV_ROWS), :] = (cb_ref[pl.ds(r0, CONV_ROWS), :].astype(F32) * c).astype(y_ref.dtype)
            return 0

        lax.fori_loop(0, nchunk, conv, 0)

    def col(off):
        return pl.BlockSpec((s, LANES), lambda j: (0, off + j))

    return pl.pallas_call(
        body, name=name, grid=(nblk,),
        in_specs=[col(first), col(first + nblk), col(first + 2 * nblk), pl.BlockSpec((8, LANES), lambda j: (0, j))],
        out_specs=pl.BlockSpec((s, LANES), lambda j: (0, j)),
        out_shape=jax.ShapeDtypeStruct((s, nblk * LANES), BF16),
        scratch_shapes=[pltpu.VMEM((s + 2 * HALO, LANES), F32)],
        compiler_params=_cparams("parallel"),
    )(proj, proj, proj, conv_w8)


def _conv_bwd(dmix, proj, conv_w8, name):
    s = proj.shape[0]
    nblk = conv_w8.shape[1] // LANES
    first = 3 * nblk
    nchunk = s // CONV_ROWS

    def body(dy_ref, cb_ref, cc_ref, cu_ref, w_ref, dcb_ref, dcc_ref, dcu_ref, dw_ref, hpad, dcpad):
        hpad[pl.ds(0, 2 * HALO), :] = jnp.zeros((2 * HALO, LANES), F32)
        dcpad[pl.ds(s, 2 * HALO), :] = jnp.zeros((2 * HALO, LANES), F32)

        def fill(i, _):
            r0 = pl.multiple_of(i * CONV_ROWS, CONV_ROWS)
            hpad[pl.ds(r0 + 2 * HALO, CONV_ROWS), :] = (
                cc_ref[pl.ds(r0, CONV_ROWS), :].astype(F32) * cu_ref[pl.ds(r0, CONV_ROWS), :].astype(F32))
            return 0

        lax.fori_loop(0, nchunk, fill, 0)
        w0, w1, w2 = w_ref[0:1, :], w_ref[1:2, :], w_ref[2:3, :]

        def fold(v):
            return jnp.sum(v.reshape(CONV_ROWS // 8, 8, LANES), axis=0)

        def first_pass(i, acc):
            a0, a1, a2 = acc
            r0 = pl.multiple_of(i * CONV_ROWS, CONV_ROWS)
            win = hpad[pl.ds(r0 + HALO, CONV_ROWS + HALO), :]
            h0 = win[HALO:]
            h1 = pltpu.roll(win, 1, 0)[HALO:]
            h2 = pltpu.roll(win, 2, 0)[HALO:]
            c = w2 * h0 + w1 * h1 + w0 * h2
            dy = dy_ref[pl.ds(r0, CONV_ROWS), :]
            dcb_ref[pl.ds(r0, CONV_ROWS), :] = (dy * c).astype(dcb_ref.dtype)
            dc = dy * cb_ref[pl.ds(r0, CONV_ROWS), :].astype(F32)
            dcpad[pl.ds(r0, CONV_ROWS), :] = dc
            return a0 + fold(dc * h2), a1 + fold(dc * h1), a2 + fold(dc * h0)

        z8 = jnp.zeros((8, LANES), F32)
        a0, a1, a2 = lax.fori_loop(0, nchunk, first_pass, (z8, z8, z8))
        dw_ref[...] = jnp.concatenate(
            [jnp.sum(a0, axis=0, keepdims=True), jnp.sum(a1, axis=0, keepdims=True),
             jnp.sum(a2, axis=0, keepdims=True), jnp.zeros((5, LANES), F32)], axis=0)

        def second_pass(i, _):
            r0 = pl.multiple_of(i * CONV_ROWS, CONV_ROWS)
            win = dcpad[pl.ds(r0, CONV_ROWS + HALO), :]
            n = CONV_ROWS + HALO
            dh = (w2 * win[:CONV_ROWS] + w1 * pltpu.roll(win, n - 1, 0)[:CONV_ROWS]
                  + w0 * pltpu.roll(win, n - 2, 0)[:CONV_ROWS])
            dcc_ref[pl.ds(r0, CONV_ROWS), :] = (dh * cu_ref[pl.ds(r0, CONV_ROWS), :].astype(F32)).astype(dcc_ref.dtype)
            dcu_ref[pl.ds(r0, CONV_ROWS), :] = (dh * cc_ref[pl.ds(r0, CONV_ROWS), :].astype(F32)).astype(dcu_ref.dtype)
            return 0

        lax.fori_loop(0, nchunk, second_pass, 0)

    def col(off):
        return pl.BlockSpec((s, LANES), lambda j: (0, off + j))

    out = pl.BlockSpec((s, LANES), lambda j: (0, j))
    return pl.pallas_call(
        body, name=name, grid=(nblk,),
        in_specs=[col(nblk), col(first), col(first + nblk), col(first + 2 * nblk),
                  pl.BlockSpec((8, LANES), lambda j: (0, j))],
        out_specs=[out, out, out, pl.BlockSpec((8, LANES), lambda j: (0, j))],
        out_shape=[jax.ShapeDtypeStruct((s, nblk * LANES), BF16)] * 3 + [jax.ShapeDtypeStruct((8, nblk * LANES), F32)],
        scratch_shapes=[pltpu.VMEM((s + 2 * HALO, LANES), F32), pltpu.VMEM((s + 2 * HALO, LANES), F32)],
        compiler_params=_cparams("parallel"),
    )(dmix, proj, proj, proj, conv_w8)


LOG2E = 1.4426950408889634
LN2 = 0.6931471805599453
NEG_BIG = -1e30
SATURATED = 160.0


def _cumsum_matrix(kind):
    j = lax.broadcasted_iota(jnp.int32, (KEY_CHUNK, 2 * KEY_CHUNK), 0)
    c = lax.broadcasted_iota(jnp.int32, (KEY_CHUNK, 2 * KEY_CHUNK), 1)
    tri = {"after": j > c, "upto": j <= c, "before": j < c}[kind]
    return jnp.where((c >= KEY_CHUNK) | tri, 1.0, 0.0).astype(BF16)


def _stack_heads(t, m0):
    zero = jnp.zeros_like(t)
    return jnp.concatenate([jnp.where(m0, t, zero), jnp.where(m0, zero, t)], axis=0)


def _softplus2(z):
    sp = jnp.maximum(z, 0.0) + jnp.log2(1.0 + jnp.exp2(-jnp.abs(z)))
    return sp, z - sp


def _key_chunk(ref, kc):
    return ref[pl.ds(pl.multiple_of(kc * KEY_CHUNK, KEY_CHUNK), KEY_CHUNK), :]


def _attn_bwd(qk, proj, dmix, rtot, used, tq, name):
    s = qk.shape[0]
    nhp = qk.shape[1] // (2 * LANES)
    nc = tq // KEY_CHUNK

    def body(used_ref, q_ref, k_ref, v_ref, do_ref, r_ref, cmi_ref, cme_ref, bias_ref, dq_ref, dk_ref, dv_ref,
             z_refs, ls_refs, sig_refs, sp_refs, gb_refs, pr_ref, gs_ref):
        qi = pl.program_id(1)

        @pl.when(qi == 0)
        def _():
            dk_ref[...] = jnp.zeros_like(dk_ref)
            dv_ref[...] = jnp.zeros_like(dv_ref)

        nslots = (qi + 1) * nc
        first = jnp.clip(nslots - used_ref[pl.program_id(0), qi], 0, nslots - nc)
        m0 = lax.broadcasted_iota(jnp.int32, (1, LANES), 1) < HEAD_DIM
        qs = _stack_heads(q_ref[...], m0)
        do = do_ref[...]
        dos = _stack_heads(do.astype(BF16), m0)
        dosl = _stack_heads((do * LN2).astype(BF16), m0)
        cmi = cmi_ref[...]
        cme = cme_ref[...]

        def chunk_at(i):
            return jnp.clip(i, first, nslots - 1)

        def scores(kc):
            return _dot(qs, _key_chunk(k_ref, kc), NT)

        def weights(ls, cs, da, pr, kc):
            a = jnp.exp2(ls - (pr - cs[:, :KEY_CHUNK]))
            gb = (a * da).astype(BF16)
            ks = pl.multiple_of(kc * KEY_CHUNK, KEY_CHUNK)
            dv_ref[pl.ds(ks, KEY_CHUNK), :] += _dot(a, dos, TN)
            return gb, jnp.exp2(ls), pr - cs[:, KEY_CHUNK:]

        def score_grads(gb, sig, cg, gs, dq, kc):
            dzb = (gb.astype(F32) * (1.0 - sig) - sig * (gs + cg[:, :KEY_CHUNK])).astype(BF16)
            ks = pl.multiple_of(kc * KEY_CHUNK, KEY_CHUNK)
            dk_ref[pl.ds(ks, KEY_CHUNK), :] += _dot(dzb, qs, TN)
            dq = dq + _dot(jnp.concatenate([dzb[:tq], dzb[tq:]], axis=1), _stack_heads(_key_chunk(k_ref, kc), m0), NN)
            return gs + cg[:, KEY_CHUNK:], dq

        def step(i, par, bias=None):
            cur, prv = par, 1 - par
            k1, k2 = chunk_at(i - 1), chunk_at(i - 2)
            z_next = scores(chunk_at(i + 1))
            cs = _dot(sp_refs[prv][...], cmi, NN)
            da = _dot(dosl, _key_chunk(v_ref, k1), NT)
            cg = _dot(gb_refs[cur][...], cme, NN)
            z = z_refs[cur][...]
            if bias is not None:
                z = z + bias
            sp, ls = _softplus2(z)
            sp_refs[cur][...] = sp.astype(BF16)
            ls_refs[cur][...] = ls
            gs, dq = score_grads(gb_refs[cur][...], sig_refs[cur][...], cg, gs_ref[...], dq_ref[...], k2)
            gs_ref[...] = gs
            dq_ref[...] = dq
            gb, sig, pr = weights(ls_refs[prv][...], cs, da, pr_ref[...], k1)
            gb_refs[prv][...] = gb
            sig_refs[prv][...] = sig
            pr_ref[...] = pr
            z_refs[prv][...] = z_next

        pr_ref[...] = jnp.concatenate([r_ref[:, :LANES], r_ref[:, LANES:]], axis=0)
        gs_ref[...] = jnp.zeros((2 * tq, LANES), F32)
        dq_ref[...] = jnp.zeros((tq, LANES), F32)
        z_refs[0][...] = scores(first)
        sp_refs[1][...] = jnp.zeros((2 * tq, LANES), BF16)
        ls_refs[1][...] = jnp.full((2 * tq, LANES), NEG_BIG, F32)
        gb_refs[0][...] = jnp.zeros((2 * tq, LANES), BF16)
        sig_refs[0][...] = jnp.zeros((2 * tq, LANES), F32)

        def two_steps(j, _):
            step(2 * j, 0)
            step(2 * j + 1, 1)
            return 0

        lax.fori_loop(first // 2, nslots // 2 - 1, two_steps, 0)
        step(nslots - 2, 0, bias_ref[0])
        step(nslots - 1, 1, bias_ref[1])
        k1, k2 = chunk_at(nslots - 1), chunk_at(nslots - 2)
        gb, sig, _ = weights(ls_refs[1][...], _dot(sp_refs[1][...], cmi, NN),
                             _dot(dosl, _key_chunk(v_ref, k1), NT), pr_ref[...], k1)
        gb2 = gb_refs[0][...]
        gs, dq = score_grads(gb2, sig_refs[0][...], _dot(gb2, cme, NN), gs_ref[...], dq_ref[...], k2)
        _, dq = score_grads(gb, sig, _dot(gb, cme, NN), gs, dq, k1)
        dq_ref[...] = dq

    def wrapped(used_ref, q_ref, k_ref, v_ref, do_ref, r_ref, cmi_ref, cme_ref, bias_ref, dq_ref, dk_ref, dv_ref,
                z0, z1, ls0, ls1, sg0, sg1, sp0, sp1, gb0, gb1, pr_ref, gs_ref):
        body(used_ref, q_ref, k_ref, v_ref, do_ref, r_ref, cmi_ref, cme_ref, bias_ref, dq_ref, dk_ref, dv_ref,
             (z0, z1), (ls0, ls1), (sg0, sg1), (sp0, sp1), (gb0, gb1), pr_ref, gs_ref)

    assert nc == 2
    bias = _diag_bias(tq, True)
    bias = jnp.concatenate([bias[:, :, :KEY_CHUNK], bias[:, :, KEY_CHUNK:]], axis=1)
    qblk = pl.BlockSpec((tq, LANES), lambda p, i, u: (i, p))
    full = pl.BlockSpec((s, LANES), lambda p, i, u: (0, p))
    cmspec = pl.BlockSpec((KEY_CHUNK, 2 * KEY_CHUNK), lambda p, i, u: (0, 0))
    shape = jax.ShapeDtypeStruct((s, nhp * LANES), F32)
    f32buf = pltpu.VMEM((2 * tq, LANES), F32)
    bf16buf = pltpu.VMEM((2 * tq, LANES), BF16)
    return pl.pallas_call(
        wrapped, name=name,
        grid_spec=pltpu.PrefetchScalarGridSpec(
            num_scalar_prefetch=1, grid=(nhp, s // tq),
            in_specs=[qblk,
                      pl.BlockSpec((s, LANES), lambda p, i, u: (0, nhp + p)),
                      pl.BlockSpec((s, LANES), lambda p, i, u: (0, 2 * nhp + p)),
                      qblk,
                      pl.BlockSpec((tq, 2 * LANES), lambda p, i, u: (i, p)),
                      cmspec, cmspec,
                      pl.BlockSpec((nc, 2 * tq, LANES), lambda p, i, u: (0, 0, 0))],
            out_specs=[qblk, full, full],
            scratch_shapes=[f32buf] * 6 + [bf16buf] * 4 + [f32buf] * 2),
        out_shape=[shape, shape, shape],
        compiler_params=_cparams("parallel", "arbitrary"),
    )(used, qk, qk, proj, dmix, rtot, _cumsum_matrix("upto"), _cumsum_matrix("before"), bias)


def _pair_cumsum_matrix(kind):
    j = lax.broadcasted_iota(jnp.int32, (2 * KEY_CHUNK, 4 * KEY_CHUNK), 0)
    c = lax.broadcasted_iota(jnp.int32, (2 * KEY_CHUNK, 4 * KEY_CHUNK), 1)
    same_head = (j // KEY_CHUNK) == ((c // KEY_CHUNK) % 2)
    jj, cc = j % KEY_CHUNK, c % KEY_CHUNK
    tri = {"after": jj > cc, "upto": jj <= cc, "before": jj < cc}[kind]
    return jnp.where(same_head & ((c >= 2 * KEY_CHUNK) | tri), 1.0, 0.0).astype(BF16)


def _diag_bias(tq, ascending):
    nc = tq // KEY_CHUNK
    shape = (nc, tq, 2 * KEY_CHUNK)
    d = lax.broadcasted_iota(jnp.int32, shape, 0)
    r = lax.broadcasted_iota(jnp.int32, shape, 1)
    c = lax.broadcasted_iota(jnp.int32, shape, 2) % KEY_CHUNK
    chunk = d if ascending else nc - 1 - d
    return jnp.where(chunk * KEY_CHUNK + c < r, 0.0, NEG_BIG).astype(F32)


def _attn_fwd(qk, proj, tq, name):
    s = qk.shape[0]
    nhp = qk.shape[1] // (2 * LANES)
    nc = tq // KEY_CHUNK
    assert nc == 2
    w = 2 * KEY_CHUNK

    def body(q_ref, k_ref, v_ref, cm_ref, bias_ref, o_ref, r_ref, used_ref, z_refs, ls_refs, cs_refs, ct_refs,
             sp_refs, ab_refs, acc_ref):
        qi = pl.program_id(1)
        nslots = (qi + 1) * nc
        m0 = lax.broadcasted_iota(jnp.int32, (1, LANES), 1) < HEAD_DIM
        q = q_ref[...]
        cm = cm_ref[...]

        def chunk_at(i):
            return jnp.clip(nslots - 1 - i, 0, nslots - 1)

        def scores(kc):
            return _dot(q, _stack_heads(_key_chunk(k_ref, kc), m0), NT)

        def values(ab, kc):
            return _dot(ab, _stack_heads(_key_chunk(v_ref, kc), m0), NN)

        def step(i, par, bias=None, stages="zscwv"):
            cur, prv = par, 1 - par
            if "z" in stages:
                z_next = scores(chunk_at(i + 1))
            if "c" in stages:
                cs = _dot(sp_refs[prv][...], cm, NN)
            if "v" in stages:
                pv = values(ab_refs[prv][...], chunk_at(i - 3))
            if "w" in stages:
                rs = r_ref[...]
                r_ref[...] = rs + ct_refs[cur][...]
                ab_refs[cur][...] = jnp.exp2(ls_refs[cur][...] - cs_refs[cur][...] - rs).astype(BF16)
            if "s" in stages:
                z = z_refs[cur][...]
                if bias is not None:
                    z = z + bias
                sp, ls = _softplus2(z)
                sp_refs[cur][...] = sp.astype(BF16)
                ls_refs[cur][...] = ls
            if "v" in stages:
                acc_ref[...] += pv
            if "c" in stages:
                cs_refs[prv][...] = cs[:, :w]
                ct_refs[prv][...] = cs[:, w:]
            if "z" in stages:
                z_refs[prv][...] = z_next

        z_refs[0][...] = scores(chunk_at(0))
        for p in range(2):
            sp_refs[p][...] = jnp.zeros((tq, w), BF16)
            ls_refs[p][...] = jnp.full((tq, w), NEG_BIG, F32)
            cs_refs[p][...] = jnp.zeros((tq, w), F32)
            ct_refs[p][...] = jnp.zeros((tq, w), F32)
            ab_refs[p][...] = jnp.zeros((tq, w), BF16)
        r_ref[...] = jnp.zeros((tq, w), F32)
        acc_ref[...] = jnp.zeros((tq, LANES), F32)
        step(0, 0, bias_ref[0])
        step(1, 1, bias_ref[1])

        def two_steps(carry):
            j, _ = carry
            step(2 * j, 0)
            step(2 * j + 1, 1)
            return j + 1, jnp.min(r_ref[...])

        pairs, low = lax.while_loop(lambda c: jnp.logical_and(c[0] < nslots // 2, c[1] < SATURATED), two_steps,
                                    (jnp.int32(1), jnp.min(r_ref[...])))
        entered = 2 * pairs
        saturated = low >= SATURATED

        @pl.when(saturated)
        def _():
            step(entered, 0, stages="v")

        @pl.when(jnp.logical_not(saturated))
        def _():
            step(entered, 0, stages="cwv")
            step(entered + 1, 1, stages="wv")
            step(entered + 2, 0, stages="v")

        o_ref[...] = acc_ref[...].astype(o_ref.dtype)
        used_ref[pl.program_id(0), qi] = jnp.where(saturated, entered - 2, entered)

    def wrapped(q_ref, k_ref, v_ref, cm_ref, bias_ref, o_ref, r_ref, used_ref, *scratch):
        z, ls, cs, ct, sp, ab = [scratch[2 * j:2 * j + 2] for j in range(6)]
        body(q_ref, k_ref, v_ref, cm_ref, bias_ref, o_ref, r_ref, used_ref, z, ls, cs, ct, sp, ab, scratch[12])

    f32buf = pltpu.VMEM((tq, w), F32)
    bf16buf = pltpu.VMEM((tq, w), BF16)
    return pl.pallas_call(
        wrapped, name=name, grid=(nhp, s // tq),
        in_specs=[pl.BlockSpec((tq, LANES), lambda p, i: (i, p)),
                  pl.BlockSpec((s, LANES), lambda p, i: (0, nhp + p)),
                  pl.BlockSpec((s, LANES), lambda p, i: (0, 2 * nhp + p)),
                  pl.BlockSpec((w, 2 * w), lambda p, i: (0, 0)),
                  pl.BlockSpec((nc, tq, w), lambda p, i: (0, 0, 0))],
        out_specs=[pl.BlockSpec((tq, LANES), lambda p, i: (i, p)),
                   pl.BlockSpec((tq, w), lambda p, i: (i, p)),
                   pl.BlockSpec(memory_space=pltpu.SMEM)],
        out_shape=[jax.ShapeDtypeStruct((s, nhp * LANES), BF16),
                   jax.ShapeDtypeStruct((s, nhp * w), F32),
                   jax.ShapeDtypeStruct((nhp, s // tq), jnp.int32)],
        scratch_shapes=[f32buf] * 8 + [bf16buf] * 4 + [pltpu.VMEM((tq, LANES), F32)],
        compiler_params=_cparams("arbitrary", "arbitrary"),
    )(qk, qk, proj, _pair_cumsum_matrix("after"), _diag_bias(tq, False))


def _mm_blocks(h, ga, widx, tm, name):
    s, d = h.shape
    nb, cols = ga.shape[1], ga.shape[3]

    def body(a_ref, b_ref, o_ref):
        o_ref[...] = _dot(a_ref[...], b_ref[...], NN).astype(o_ref.dtype)

    return pl.pallas_call(
        body, name=name, grid=(s // tm, nb),
        in_specs=[pl.BlockSpec((tm, d), lambda i, j: (i, 0)),
                  pl.BlockSpec((None, None, d, cols), lambda i, j: (widx, j, 0, 0))],
        out_specs=pl.BlockSpec((tm, cols), lambda i, j: (i, j)),
        out_shape=jax.ShapeDtypeStruct((s, nb * cols), BF16),
        compiler_params=_cparams("parallel", "arbitrary"),
    )(h, ga)


def _mm_swiglu(h, ga, gidx, uidx, tm, name):
    s, d = h.shape
    nb, cols = ga.shape[1], ga.shape[3]

    def body(a_ref, bg_ref, bu_ref, g_ref, u_ref, act_ref):
        a = a_ref[...]
        g = _dot(a, bg_ref[...], NN)
        u = _dot(a, bu_ref[...], NN)
        g_ref[...] = g.astype(g_ref.dtype)
        u_ref[...] = u.astype(u_ref.dtype)
        act_ref[...] = (g * (1.0 / (1.0 + jnp.exp(-g))) * u).astype(act_ref.dtype)

    def wspec(idx):
        return pl.BlockSpec((None, None, d, cols), lambda i, j: (idx, j, 0, 0))

    out = pl.BlockSpec((tm, cols), lambda i, j: (i, j))
    shape = jax.ShapeDtypeStruct((s, nb * cols), BF16)
    return pl.pallas_call(
        body, name=name, grid=(s // tm, nb),
        in_specs=[pl.BlockSpec((tm, d), lambda i, j: (i, 0)), wspec(gidx), wspec(uidx)],
        out_specs=[out, out, out], out_shape=[shape, shape, shape],
        compiler_params=_cparams("parallel", "arbitrary"),
    )(h, ga, ga)


def _mm_residual(a, w3, lidx, res, tm, tn, name):
    s, k = a.shape
    n = w3.shape[2]

    def body(a_ref, b_ref, r_ref, o_ref):
        o_ref[...] = r_ref[...] + _dot(a_ref[...], b_ref[...], NN)

    return pl.pallas_call(
        body, name=name, grid=(s // tm, n // tn),
        in_specs=[pl.BlockSpec((tm, k), lambda i, j: (i, 0)),
                  pl.BlockSpec((None, k, tn), lambda i, j: (lidx, 0, j)),
                  pl.BlockSpec((tm, tn), lambda i, j: (i, j))],
        out_specs=pl.BlockSpec((tm, tn), lambda i, j: (i, j)),
        out_shape=jax.ShapeDtypeStruct((s, n), F32),
        compiler_params=_cparams("parallel", "arbitrary"),
    )(a, w3, res)


def _mm_nt(a, w3, lidx, tm, tn, name):
    s, k = a.shape
    n = w3.shape[1]

    def body(a_ref, b_ref, o_ref):
        o_ref[...] = _dot(a_ref[...], b_ref[...], NT)

    return pl.pallas_call(
        body, name=name, grid=(s // tm, n // tn),
        in_specs=[pl.BlockSpec((tm, k), lambda i, j: (i, 0)),
                  pl.BlockSpec((None, tn, k), lambda i, j: (lidx, j, 0))],
        out_specs=pl.BlockSpec((tm, tn), lambda i, j: (i, j)),
        out_shape=jax.ShapeDtypeStruct((s, n), F32),
        compiler_params=_cparams("parallel", "arbitrary"),
    )(a, w3)


def _mm_nt_swiglu_bwd(dx, wd3, lidx, g, u, tm, name):
    s, d = dx.shape
    cols = g.shape[1] // N_DEV

    def body(a_ref, b_ref, g_ref, u_ref, dg_ref, du_ref):
        dact = _dot(a_ref[...], b_ref[...], NT)
        gv = g_ref[...].astype(F32)
        sig = 1.0 / (1.0 + jnp.exp(-gv))
        du_ref[...] = (dact * (gv * sig)).astype(du_ref.dtype)
        dg_ref[...] = (dact * u_ref[...].astype(F32) * (sig * (1.0 + gv * (1.0 - sig)))).astype(dg_ref.dtype)

    blk = pl.BlockSpec((tm, cols), lambda i, j: (i, j))
    shape = jax.ShapeDtypeStruct(g.shape, BF16)
    return pl.pallas_call(
        body, name=name, grid=(s // tm, N_DEV),
        in_specs=[pl.BlockSpec((tm, d), lambda i, j: (i, 0)),
                  pl.BlockSpec((None, cols, d), lambda i, j: (lidx, j, 0)), blk, blk],
        out_specs=[blk, blk], out_shape=[shape, shape],
        compiler_params=_cparams("parallel", "arbitrary"),
    )(dx, wd3, g, u)


def _mm_nt_blocks(das, ga, widxs, tm, name):
    s = das[0].shape[0]
    nb, d, cols = ga.shape[1], ga.shape[2], ga.shape[3]
    nw = len(das)

    def body(*refs):
        a_refs, b_refs, o_ref = refs[:nw], refs[nw:2 * nw], refs[2 * nw]
        acc = None
        for w in range(nw):
            for k in range(nb):
                part = _dot(a_refs[w][:, k * cols:(k + 1) * cols], b_refs[w][k], NT)
                acc = part if acc is None else acc + part
        o_ref[...] = acc

    def wspec(idx):
        return pl.BlockSpec((None, nb, d, cols), lambda i: (idx, 0, 0, 0), pipeline_mode=pl.Buffered(1))

    return pl.pallas_call(
        body, name=name, grid=(s // tm,),
        in_specs=[pl.BlockSpec((tm, nb * cols), lambda i: (i, 0))] * nw + [wspec(i) for i in widxs],
        out_specs=pl.BlockSpec((tm, d), lambda i: (i, 0)),
        out_shape=jax.ShapeDtypeStruct((s, d), F32),
        compiler_params=_cparams("parallel"),
    )(*das, *([ga] * nw))


def _mm_tn(a, b, ta, tb, tk, out_blocks, name):
    s, ka = a.shape
    nb = b.shape[1]
    nk = s // tk

    def body(a_ref, b_ref, o_ref, ob_ref):
        k = pl.program_id(2)
        part = _dot(a_ref[...], b_ref[...], TN)

        @pl.when(k == 0)
        def _():
            o_ref[...] = part

        @pl.when(k > 0)
        def _():
            o_ref[...] += part

        @pl.when(k == nk - 1)
        def _():
            ob_ref[...] = o_ref[...].astype(ob_ref.dtype)

    if out_blocks:
        out_spec = pl.BlockSpec((None, ta, tb), lambda i, j, k: (j, i, 0))
        shape = (nb // tb, ka, tb)
    else:
        out_spec = pl.BlockSpec((ta, tb), lambda i, j, k: (i, j))
        shape = (ka, nb)
    return pl.pallas_call(
        body, name=name, grid=(ka // ta, nb // tb, nk),
        in_specs=[pl.BlockSpec((tk, ta), lambda i, j, k: (k, i)),
                  pl.BlockSpec((tk, tb), lambda i, j, k: (k, j))],
        out_specs=[out_spec, out_spec],
        out_shape=[jax.ShapeDtypeStruct(shape, F32), jax.ShapeDtypeStruct(shape, BF16)],
        compiler_params=_cparams("parallel", "parallel", "arbitrary"),
    )(a, b)


def _loss_head(y, target, tm, name):
    s, d = y.shape
    nsteps = s // tm

    def body(y_ref, t_ref, dy_ref, dyb_ref, l_ref, acc):
        i = pl.program_id(0)
        diff = y_ref[...] - t_ref[...]
        dy_ref[...] = diff * (1.0 / d)
        dyb_ref[...] = (diff * (1.0 / d)).astype(dyb_ref.dtype)
        part = jnp.sum((diff * diff).reshape(tm // 8, 8, d), axis=0)

        @pl.when(i == 0)
        def _():
            acc[...] = part

        @pl.when(i > 0)
        def _():
            acc[...] += part

        @pl.when(i == nsteps - 1)
        def _():
            tot = jnp.sum(jnp.sum(acc[...], axis=1, keepdims=True), axis=0, keepdims=True)
            l_ref[...] = jnp.broadcast_to(tot * (0.5 / d), (8, LANES))

    row = pl.BlockSpec((tm, d), lambda i: (i, 0))
    return pl.pallas_call(
        body, name=name, grid=(nsteps,),
        in_specs=[row, row],
        out_specs=[row, row, pl.BlockSpec((8, LANES), lambda i: (0, 0))],
        out_shape=[jax.ShapeDtypeStruct((s, d), F32), jax.ShapeDtypeStruct((s, d), BF16),
                   jax.ShapeDtypeStruct((8, LANES), F32)],
        scratch_shapes=[pltpu.VMEM((8, d), F32)],
        compiler_params=_cparams("arbitrary"),
    )(y, target)


def _adamw(parts, own, w, m, v, tr, name):
    p, rows, cols = parts.shape
    c1 = 1.0 / (1.0 - ADAM_B1 ** ADAM_STEP)
    c2 = 1.0 / (1.0 - ADAM_B2 ** ADAM_STEP)

    def body(*refs):
        if own is None:
            p_ref, w_ref, m_ref, v_ref, g_ref, d_ref, nm_ref, nv_ref = refs
            g = p_ref[0]
            for k in range(1, p):
                g = g + p_ref[k]
        else:
            p_ref, own_ref, w_ref, m_ref, v_ref, g_ref, d_ref, nm_ref, nv_ref = refs
            x, y, c = _place()
            my = 4 * x + 2 * y + c
            mine = own_ref[...]
            g = jnp.where(my == 0, mine, p_ref[0].astype(F32))
            for k in range(1, p):
                g = g + jnp.where(my == k, mine, p_ref[k].astype(F32))
        nm = ADAM_B1 * m_ref[...] + (1.0 - ADAM_B1) * g
        nv = ADAM_B2 * v_ref[...] + (1.0 - ADAM_B2) * (g * g)
        g_ref[...] = g
        nm_ref[...] = nm
        nv_ref[...] = nv
        d_ref[...] = -ADAM_LR * ((nm * c1) / (jnp.sqrt(nv * c2) + ADAM_EPS) + ADAM_WD * w_ref[...])

    blk = pl.BlockSpec((tr, cols), lambda i: (i, 0))
    shape = jax.ShapeDtypeStruct((rows, cols), F32)
    return pl.pallas_call(
        body, name=name, grid=(rows // tr,),
        in_specs=[pl.BlockSpec((p, tr, cols), lambda i: (0, i, 0))] + [blk] * (3 if own is None else 4),
        out_specs=[blk] * 4, out_shape=[shape] * 4,
        compiler_params=_cparams("parallel"),
    )(*([parts] + ([] if own is None else [own]) + [w, m, v]))


def _place():
    x, y, c = lax.axis_index("x"), lax.axis_index("y"), lax.axis_index("c")
    return x, y, c


def _all_gather(shards, name):
    na = len(shards)

    def body(*refs):
        srcs, dsts = refs[:na], refs[na:2 * na]
        send_sems, recv_sems, local_sems = refs[2 * na:]
        x, y, c = _place()
        me, sibling = (x, y, c), (x, y, 1 - c)
        chips = [(1 - x, y), (x, 1 - y), (1 - x, 1 - y)]

        def slot(a, dev):
            return dsts[a].at[:, pl.ds(4 * dev[0] + 2 * dev[1] + dev[2], 1)]

        def copy(k, a, block, to, from_shard=False):
            return pltpu.make_async_remote_copy(
                src_ref=srcs[a] if from_shard else slot(a, block), dst_ref=slot(a, block),
                send_sem=send_sems.at[k, a], recv_sem=recv_sems.at[k, a], device_id=to, device_id_type=MESH)

        mine = [pltpu.make_async_copy(srcs[a], slot(a, me), local_sems.at[a]) for a in range(na)]
        for cp in mine:
            cp.start()
        first = [copy(0, a, me, sibling, True) for a in range(na)]
        first += [copy(1 + j, a, me, (*chip, c), True) for j, chip in enumerate(chips) for a in range(na)]
        for cp in first:
            cp.start()
        passed = []
        for j, chip in enumerate(chips):
            for a in range(na):
                copy(1 + j, a, (*chip, c), me).wait_recv()
                fwd = copy(4 + j, a, (*chip, c), sibling)
                fwd.start()
                passed.append(fwd)
        for a in range(na):
            copy(0, a, sibling, me).wait_recv()
        for j, chip in enumerate(chips):
            for a in range(na):
                copy(4 + j, a, (*chip, 1 - c), me).wait_recv()
        for cp in first + passed:
            cp.wait_send()
        for cp in mine:
            cp.wait()

    anyspec = pl.BlockSpec(memory_space=pl.ANY)
    return pl.pallas_call(
        body, name=name,
        in_specs=[anyspec] * na, out_specs=[anyspec] * na,
        out_shape=[jax.ShapeDtypeStruct((a.shape[0], N_DEV) + a.shape[2:], a.dtype) for a in shards],
        scratch_shapes=[pltpu.SemaphoreType.DMA((7, na)), pltpu.SemaphoreType.DMA((7, na)),
                        pltpu.SemaphoreType.DMA((na,))],
    )(*shards)


_RELATIONS = [(dx, dy, dc) for dx in (0, 1) for dy in (0, 1) for dc in (0, 1)][1:]


def _flip(v, d):
    return 1 - v if d else v


def _exchange_blocks(grads, name):
    na = len(grads)

    def body(*refs):
        srcs, dsts = refs[:na], refs[na:2 * na]
        send_sems, recv_sems, local_sems = refs[2 * na:]
        x, y, c = _place()
        my = 4 * x + 2 * y + c
        mine = [pltpu.make_async_copy(srcs[a].at[pl.ds(my, 1)], dsts[a].at[pl.ds(my, 1)], local_sems.at[a])
                for a in range(na)]
        for cp in mine:
            cp.start()
        sends = []
        for k, (dx, dy, dc) in enumerate(_RELATIONS):
            peer = (_flip(x, dx), _flip(y, dy), _flip(c, dc))
            pidx = 4 * peer[0] + 2 * peer[1] + peer[2]
            for a in range(na):
                cp = pltpu.make_async_remote_copy(
                    src_ref=srcs[a].at[pl.ds(pidx, 1)], dst_ref=dsts[a].at[pl.ds(my, 1)],
                    send_sem=send_sems.at[k, a], recv_sem=recv_sems.at[k, a], device_id=peer, device_id_type=MESH)
                cp.start()
                sends.append((cp, pidx, k, a, peer))
        for cp, pidx, k, a, peer in sends:
            pltpu.make_async_remote_copy(
                src_ref=srcs[a].at[pl.ds(pidx, 1)], dst_ref=dsts[a].at[pl.ds(pidx, 1)],
                send_sem=send_sems.at[k, a], recv_sem=recv_sems.at[k, a], device_id=peer,
                device_id_type=MESH).wait_recv()
        for cp, *_ in sends:
            cp.wait_send()
        for cp in mine:
            cp.wait()

    anyspec = pl.BlockSpec(memory_space=pl.ANY)
    return pl.pallas_call(
        body, name=name,
        in_specs=[anyspec] * na, out_specs=[anyspec] * na,
        out_shape=[jax.ShapeDtypeStruct(a.shape, a.dtype) for a in grads],
        scratch_shapes=[pltpu.SemaphoreType.DMA((7, na)), pltpu.SemaphoreType.DMA((7, na)),
                        pltpu.SemaphoreType.DMA((na,))],
    )(*grads)


def _all_reduce_small(v, name):
    r, c_ = v.shape

    def body(v_ref, o_ref, gath, send_sems, recv_sems):
        x, y, c = _place()
        my = 4 * x + 2 * y + c
        gath[my] = v_ref[...]
        sends = []
        for k, (dx, dy, dc) in enumerate(_RELATIONS):
            peer = (_flip(x, dx), _flip(y, dy), _flip(c, dc))
            cp = pltpu.make_async_remote_copy(
                src_ref=v_ref, dst_ref=gath.at[my], send_sem=send_sems.at[k], recv_sem=recv_sems.at[k],
                device_id=peer, device_id_type=MESH)
            cp.start()
            sends.append((cp, 4 * peer[0] + 2 * peer[1] + peer[2], k, peer))
        for cp, pidx, k, peer in sends:
            pltpu.make_async_remote_copy(
                src_ref=v_ref, dst_ref=gath.at[pidx], send_sem=send_sems.at[k], recv_sem=recv_sems.at[k],
                device_id=peer, device_id_type=MESH).wait_recv()
        for cp, *_ in sends:
            cp.wait_send()
        tot = gath[0]
        for k in range(1, N_DEV):
            tot = tot + gath[k]
        o_ref[...] = tot

    vm = pl.BlockSpec(memory_space=pltpu.VMEM)
    return pl.pallas_call(
        body, name=name, in_specs=[vm], out_specs=vm,
        out_shape=jax.ShapeDtypeStruct((r, c_), F32),
        scratch_shapes=[pltpu.VMEM((N_DEV, r, c_), F32), pltpu.SemaphoreType.DMA((7,)),
                        pltpu.SemaphoreType.DMA((7,))],
    )(v)


TM = 512
TM_MATMUL = 2048
TM_RESIDUAL = 1024
TQ = 256


def _pad_to(a, axis, size):
    pad = [(0, 0)] * a.ndim
    pad[axis] = (0, size - a.shape[axis])
    return jnp.pad(a, pad)


def _local_step(x, target, ga, gb, gc, conv_full, norm_mix, q_norm, k_norm, norm_ffn):
    depth = gb.shape[0]
    tm, tq = min(TM, x.shape[0]), min(TQ, x.shape[0])
    tmm, tmr = min(TM_MATMUL, x.shape[0]), min(TM_RESIDUAL, x.shape[0])
    attn = gb.shape[1] // 2
    nheads = attn // HEAD_DIM
    scale = HEAD_DIM ** -0.5 * LOG2E
    saved = []
    for l in range(depth):
        h1 = _rmsnorm_fwd(x, norm_mix[l][None], tm,f"norm_mix_fwd_{l}")
        proj = _mm_blocks(h1, ga, 3 * l, tmm, f"proj_in_{l}")
        qk_gain = jnp.concatenate([jnp.tile(q_norm[l], nheads) * scale, jnp.tile(k_norm[l], nheads)])[None]
        qk = _qknorm_fwd(proj, qk_gain, tmm, f"qknorm_fwd_{l}")
        o, rtot, used = _attn_fwd(qk, proj, tq, f"attn_fwd_{l}")
        conv_w8 = _pad_to(conv_full[l], 0, 8)
        cv = _conv_fwd(proj, conv_w8, f"conv_fwd_{l}")
        mix = jnp.concatenate([o, cv], axis=1)
        x1 = _mm_residual(mix, gb, l, x, tmr, 512, f"proj_out_{l}")
        h2 = _rmsnorm_fwd(x1, norm_ffn[l][None], tm,f"norm_ffn_fwd_{l}")
        g, u, act = _mm_swiglu(h2, ga, 3 * l + 1, 3 * l + 2, tmm, f"ffn_up_{l}")
        x2 = _mm_residual(act, gc, l, x1, tmr, 512, f"ffn_down_{l}")
        saved.append((x, h1, proj, qk_gain, qk, rtot, used, conv_w8, mix, x1, h2, g, u, act))
        x = x2

    dx, dxb, loss = _loss_head(x, target, tm, "loss_head")

    grads = [None] * depth
    small = [None] * depth
    for l in reversed(range(depth)):
        x0, h1, proj, qk_gain, qk, rtot, used, conv_w8, mix, x1, h2, g, u, act = saved[l]
        d = x0.shape[1]
        dg, du = _mm_nt_swiglu_bwd(dxb, gc, l, g, u, tmm, f"ffn_down_bwd_{l}")
        d_wdown = _mm_tn(act, dxb, 768, d, tmm, False, f"dw_down_{l}")
        d_wgate = _mm_tn(h2, dg, d, ga.shape[3], tmm, True, f"dw_gate_{l}")
        d_wup = _mm_tn(h2, du, d, ga.shape[3], tmm, True, f"dw_up_{l}")
        dh2 = _mm_nt_blocks([dg, du], ga, [3 * l + 1, 3 * l + 2], tm, f"ffn_up_bwd_{l}")
        dx1, dx1b, dg_ffn = _rmsnorm_bwd(dh2, x1, norm_ffn[l][None], dx, tm, f"norm_ffn_bwd_{l}")
        dmix = _mm_nt(dx1b, gb, l, tmr, 512, f"proj_out_bwd_{l}")
        d_wout = _mm_tn(mix, dx1b, 512, d, tmm, False, f"dw_out_{l}")
        dcb, dcc, dcu, dconv = _conv_bwd(dmix, proj, conv_w8, f"conv_bwd_{l}")
        dq, dk, dv = _attn_bwd(qk, proj, dmix, rtot, used, tq, f"attn_bwd_{l}")
        dqk, dg_qk = _qknorm_bwd(jnp.concatenate([dq, dk], axis=1), proj, qk_gain, tmm, f"qknorm_bwd_{l}")
        dproj = jnp.concatenate([dqk, dv.astype(BF16), dcb, dcc, dcu], axis=1)
        d_win = _mm_tn(h1, dproj, d, ga.shape[3], tmm, True, f"dw_in_{l}")
        dh1 = _mm_nt_blocks([dproj], ga, [3 * l], tmr, f"proj_in_bwd_{l}")
        dx, dxb, dg_mix = _rmsnorm_bwd(dh1, x0, norm_mix[l][None], dx1, tm, f"norm_mix_bwd_{l}")
        grads[l] = (d_win, d_wgate, d_wup, d_wout, d_wdown)
        dq_gain = jnp.sum(dg_qk[0, :attn].reshape(nheads, HEAD_DIM), axis=0) * scale
        dk_gain = jnp.sum(dg_qk[0, attn:].reshape(nheads, HEAD_DIM), axis=0)
        small[l] = (dg_mix[0], dg_ffn[0], dq_gain, dk_gain, dconv[:3])
    return loss, dx, grads, small


def kernel(x, norm_mix, w_in, q_norm, k_norm, conv_w, w_out, norm_ffn, w_gate, w_up, w_down, loss_target, m_norm_mix, m_w_in, m_q_norm, m_k_norm, m_conv_w, m_w_out, m_norm_ffn, m_w_gate, m_w_up, m_w_down, v_norm_mix, v_w_in, v_q_norm, v_k_norm, v_conv_w, v_w_out, v_norm_ffn, v_w_gate, v_w_up, v_w_down):
    depth, d, in_shard = w_in.shape
    ff_shard = w_gate.shape[2]
    ff_pad = in_shard
    conv_shard = conv_w.shape[2]
    xs = x.reshape(x.shape[-2], d)
    target = loss_target.reshape(xs.shape)

    pa = jnp.stack([w_in, _pad_to(w_gate, 2, ff_pad), _pad_to(w_up, 2, ff_pad)], axis=1)
    pa = pa.reshape(3 * depth, 1, d, in_shard).astype(BF16)
    pb = w_out.astype(BF16)[:, None]
    pc = _pad_to(w_down, 1, ff_pad).astype(BF16)[:, None]
    pd = _pad_to(_pad_to(conv_w.reshape(depth * 3, conv_shard), 0, 8), 1, LANES)[None, None]
    ga, gb, gc, gd = _all_gather([pa, pb, pc, pd], "gather_weights")
    gb = gb.reshape(depth, N_DEV * gb.shape[2], d)
    gc = gc.reshape(depth, N_DEV * ff_pad, d)
    conv_full = gd[0, :, :depth * 3, :conv_shard].transpose(1, 0, 2).reshape(depth, 3, N_DEV * conv_shard)

    loss, grad_x, grads, small = _local_step(xs, target, ga, gb, gc, conv_full, norm_mix, q_norm, k_norm, norm_ffn)

    x_, y_, c_ = _place()
    my = 4 * x_ + 2 * y_ + c_
    block_rows = (d, d, d, w_out.shape[1], ff_pad)
    send = [grads[l][j][1].reshape(N_DEV, block_rows[j], -1) for l in range(depth) for j in range(5)]
    landed = _exchange_blocks(send, "exchange_grads")
    own = [lax.dynamic_index_in_dim(grads[l][j][0].reshape(N_DEV, block_rows[j], -1), my, 0, keepdims=False)
           for l in range(depth) for j in range(5)]

    nconv = N_DEV * conv_shard
    rows = []
    for l in range(depth):
        g_mix, g_ffn, g_q, g_k, g_conv = small[l]
        qkrow = _pad_to(jnp.concatenate([g_q, g_k]), 0, d)
        rows += [g_mix[None], g_ffn[None], qkrow[None], _pad_to(g_conv, 1, d)]
    nrow = 6 * depth
    packed = jnp.concatenate(rows + [_pad_to(loss[:1], 1, d)], axis=0)
    packed = _pad_to(packed, 0, ((nrow + 1 + 7) // 8) * 8)
    summed = _all_reduce_small(packed, "reduce_small")
    loss_out = summed[nrow, 0]

    def big(i, w, m, v, tr, name, rows_=None, cols_=None):
        parts = landed[i]
        pr, pcn = parts.shape[1], parts.shape[2]
        w, m, v = [_pad_to(_pad_to(t, 0, pr), 1, pcn) for t in (w, m, v)]
        outs = _adamw(parts, own[i], w, m, v, tr, name)
        return [o[:rows_ or pr, :cols_ or pcn] for o in outs]

    res = {}
    for l in range(depth):
        i = 5 * l
        res[("w_in", l)] = big(i, w_in[l], m_w_in[l], v_w_in[l], 256, f"adamw_in_{l}")
        res[("w_gate", l)] = big(i + 1, w_gate[l], m_w_gate[l], v_w_gate[l], 256, f"adamw_gate_{l}", cols_=ff_shard)
        res[("w_up", l)] = big(i + 2, w_up[l], m_w_up[l], v_w_up[l], 256, f"adamw_up_{l}", cols_=ff_shard)
        res[("w_out", l)] = big(i + 3, w_out[l], m_w_out[l], v_w_out[l], w_out.shape[1], f"adamw_out_{l}")
        res[("w_down", l)] = big(i + 4, w_down[l], m_w_down[l], v_w_down[l], 128, f"adamw_down_{l}",
                                 rows_=ff_shard)

    g_rows, w_rows, m_rows, v_rows = [], [], [], []
    for l in range(depth):
        base = l * 6
        conv_g = lax.dynamic_slice(summed[base + 3:base + 6], (0, my * conv_shard), (3, conv_shard))
        g_rows += [summed[base:base + 3], _pad_to(conv_g, 1, d)]
        for dst, (nm, qn, kn, nf, cw) in ((w_rows, (norm_mix, q_norm, k_norm, norm_ffn, conv_w)),
                                          (m_rows, (m_norm_mix, m_q_norm, m_k_norm, m_norm_ffn, m_conv_w)),
                                          (v_rows, (v_norm_mix, v_q_norm, v_k_norm, v_norm_ffn, v_conv_w))):
            dst += [nm[l][None], nf[l][None], _pad_to(jnp.concatenate([qn[l], kn[l]]), 0, d)[None],
                    _pad_to(cw[l], 1, d)]
    prow = ((nrow + 7) // 8) * 8
    gs, ws, ms, vs = [_pad_to(jnp.concatenate(t, axis=0), 0, prow) for t in (g_rows, w_rows, m_rows, v_rows)]
    sm = _adamw(gs[None], None, ws, ms, vs, prow, "adamw_small")

    hd = q_norm.shape[1]

    def small_out(t, kind):
        per_layer = []
        for l in range(depth):
            base = l * 6
            per_layer.append({"norm_mix": t[base], "norm_ffn": t[base + 1], "q_norm": t[base + 2, :hd],
                              "k_norm": t[base + 2, hd:2 * hd], "conv_w": t[base + 3:base + 6, :conv_shard]}[kind])
        return jnp.stack(per_layer)

    def big_out(name, i):
        return jnp.stack([res[(name, l)][i] for l in range(depth)])

    outs = [loss_out, grad_x.reshape(x.shape)]
    for i in range(4):
        outs += [small_out(sm[i], "norm_mix"), big_out("w_in", i), small_out(sm[i], "q_norm"),
                 small_out(sm[i], "k_norm"), small_out(sm[i], "conv_w"), big_out("w_out", i),
                 small_out(sm[i], "norm_ffn"), big_out("w_gate", i), big_out("w_up", i), big_out("w_down", i)]
    return tuple(outs)
```

```python
import jax
import jax.numpy as jnp
from jax import lax
from jax.experimental import pallas as pl
from jax.experimental.pallas import tpu as pltpu

F32 = jnp.float32
BF16 = jnp.bfloat16
MESH = pl.DeviceIdType.MESH

N_DEV = 8
LANES = 128
HEAD_DIM = 64
KEY_CHUNK = 128
EPS = 1e-6
VMEM_LIMIT = 48 * 1024 * 1024

ADAM_LR = 0.001
ADAM_B1 = 0.9
ADAM_B2 = 0.999
ADAM_EPS = 1e-08
ADAM_WD = 0.01
ADAM_STEP = 10

NN = (((1,), (0,)), ((), ()))
NT = (((1,), (1,)), ((), ()))
TN = (((0,), (0,)), ((), ()))


def _dot(a, b, dims):
    return lax.dot_general(a.astype(BF16), b.astype(BF16), dims, preferred_element_type=F32)


def _cparams(*sem):
    return pltpu.CompilerParams(dimension_semantics=sem, vmem_limit_bytes=VMEM_LIMIT)


def _split_hi_lo(v):
    hi = v.astype(BF16)
    lo = (v - hi.astype(F32)).astype(BF16)
    return jnp.concatenate([hi, lo], axis=1)


def _rmsnorm_fwd(x, gain, tm, name):
    s, d = x.shape

    def body(x_ref, g_ref, o_ref):
        xv = x_ref[...]
        r = lax.rsqrt(jnp.mean(xv * xv, axis=-1, keepdims=True) + EPS)
        o_ref[...] = ((xv * r) * g_ref[...]).astype(o_ref.dtype)

    return pl.pallas_call(
        body, name=name, grid=(s // tm,),
        in_specs=[pl.BlockSpec((tm, d), lambda i: (i, 0)), pl.BlockSpec((1, d), lambda i: (0, 0))],
        out_specs=pl.BlockSpec((tm, d), lambda i: (i, 0)),
        out_shape=jax.ShapeDtypeStruct((s, d), BF16),
        compiler_params=_cparams("parallel"),
    )(x, gain)


def _rmsnorm_bwd(dh, x, gain, dres, tm, name):
    s, d = x.shape
    nsteps = s // tm

    def body(dh_ref, x_ref, g_ref, dres_ref, dx_ref, dxb_ref, dg_ref):
        i = pl.program_id(0)
        xv = x_ref[...]
        r = lax.rsqrt(jnp.mean(xv * xv, axis=-1, keepdims=True) + EPS)
        xhat = xv * r
        dhv = dh_ref[...]
        dxh = dhv * g_ref[...]
        proj = jnp.mean(dxh * xhat, axis=-1, keepdims=True)
        dxv = dres_ref[...] + r * (dxh - xhat * proj)
        dx_ref[...] = dxv
        dxb_ref[...] = dxv.astype(dxb_ref.dtype)
        part = jnp.sum((dhv * xhat).reshape(tm // 8, 8, d), axis=0)

        @pl.when(i == 0)
        def _():
            dg_ref[...] = part

        @pl.when(i > 0)
        def _():
            dg_ref[...] += part

        @pl.when(i == nsteps - 1)
        def _():
            dg_ref[...] = jnp.broadcast_to(jnp.sum(dg_ref[...], axis=0, keepdims=True), (8, d))

    row = pl.BlockSpec((tm, d), lambda i: (i, 0))
    return pl.pallas_call(
        body, name=name, grid=(nsteps,),
        in_specs=[row, row, pl.BlockSpec((1, d), lambda i: (0, 0)), row],
        out_specs=[row, row, pl.BlockSpec((8, d), lambda i: (0, 0))],
        out_shape=[jax.ShapeDtypeStruct((s, d), F32), jax.ShapeDtypeStruct((s, d), BF16),
                   jax.ShapeDtypeStruct((8, d), F32)],
        compiler_params=_cparams("arbitrary"),
    )(dh, x, gain, dres)


def _group_mean_matrix():
    r = lax.broadcasted_iota(jnp.int32, (LANES, LANES), 0) // HEAD_DIM
    c = lax.broadcasted_iota(jnp.int32, (LANES, LANES), 1) // HEAD_DIM
    return jnp.where(r == c, 1.0 / HEAD_DIM, 0.0).astype(BF16)


def _group_mean(v, gm):
    hi = v.astype(BF16)
    lo = (v - hi.astype(F32)).astype(BF16)
    return _dot(hi, gm, NN) + _dot(lo, gm, NN)


def _qknorm_fwd(proj, gains, tm, name):
    s = proj.shape[0]
    ncol = gains.shape[1] // LANES

    def body(p_ref, g_ref, gm_ref, o_ref):
        xv = p_ref[...].astype(F32)
        r = lax.rsqrt(_group_mean(xv * xv, gm_ref[...]) + EPS)
        o_ref[...] = ((xv * r) * g_ref[...]).astype(o_ref.dtype)

    blk = pl.BlockSpec((tm, LANES), lambda i, j: (i, j))
    return pl.pallas_call(
        body, name=name, grid=(s // tm, ncol),
        in_specs=[blk, pl.BlockSpec((1, LANES), lambda i, j: (0, j)),
                  pl.BlockSpec((LANES, LANES), lambda i, j: (0, 0))],
        out_specs=blk,
        out_shape=jax.ShapeDtypeStruct((s, ncol * LANES), BF16),
        compiler_params=_cparams("parallel", "parallel"),
    )(proj, gains, _group_mean_matrix())


def _qknorm_bwd(dqk, proj, gains, tm, name):
    s = proj.shape[0]
    ncol = gains.shape[1] // LANES
    nsteps = s // tm

    def body(dy_ref, p_ref, g_ref, gm_ref, dx_ref, dg_ref):
        i = pl.program_id(1)
        gm = gm_ref[...]
        xv = p_ref[...].astype(F32)
        r = lax.rsqrt(_group_mean(xv * xv, gm) + EPS)
        xhat = xv * r
        dy = dy_ref[...]
        dxh = dy * g_ref[...]
        proj_ = _group_mean(dxh * xhat, gm)
        dx_ref[...] = (r * (dxh - xhat * proj_)).astype(dx_ref.dtype)
        part = jnp.sum((dy * xhat).reshape(tm // 8, 8, LANES), axis=0)

        @pl.when(i == 0)
        def _():
            dg_ref[...] = part

        @pl.when(i > 0)
        def _():
            dg_ref[...] += part

        @pl.when(i == nsteps - 1)
        def _():
            dg_ref[...] = jnp.broadcast_to(jnp.sum(dg_ref[...], axis=0, keepdims=True), (8, LANES))

    blk = pl.BlockSpec((tm, LANES), lambda j, i: (i, j))
    return pl.pallas_call(
        body, name=name, grid=(ncol, nsteps),
        in_specs=[blk, blk, pl.BlockSpec((1, LANES), lambda j, i: (0, j)),
                  pl.BlockSpec((LANES, LANES), lambda j, i: (0, 0))],
        out_specs=[blk, pl.BlockSpec((8, LANES), lambda j, i: (0, j))],
        out_shape=[jax.ShapeDtypeStruct((s, ncol * LANES), BF16),
                   jax.ShapeDtypeStruct((8, ncol * LANES), F32)],
        compiler_params=_cparams("parallel", "arbitrary"),
    )(dqk, proj, gains, _group_mean_matrix())


CONV_ROWS = 256
HALO = 8


def _conv_fwd(proj, conv_w8, name):
    s = proj.shape[0]
    nblk = conv_w8.shape[1] // LANES
    first = 3 * nblk
    nchunk = s // CONV_ROWS

    def body(cb_ref, cc_ref, cu_ref, w_ref, y_ref, hpad):
        hpad[pl.ds(0, 2 * HALO), :] = jnp.zeros((2 * HALO, LANES), F32)

        def fill(i, _):
            r0 = pl.multiple_of(i * CONV_ROWS, CONV_ROWS)
            hpad[pl.ds(r0 + 2 * HALO, CONV_ROWS), :] = (
                cc_ref[pl.ds(r0, CONV_ROWS), :].astype(F32) * cu_ref[pl.ds(r0, CONV_ROWS), :].astype(F32))
            return 0

        lax.fori_loop(0, nchunk, fill, 0)
        w0, w1, w2 = w_ref[0:1, :], w_ref[1:2, :], w_ref[2:3, :]

        def conv(i, _):
            r0 = pl.multiple_of(i * CONV_ROWS, CONV_ROWS)
            win = hpad[pl.ds(r0 + HALO, CONV_ROWS + HALO), :]
            c = (w2 * win[HALO:] + w1 * pltpu.roll(win, 1, 0)[HALO:] + w0 * pltpu.roll(win, 2, 0)[HALO:])
            y_ref[pl.ds(r0, CONV_ROWS), :] = (cb_ref[pl.ds(r0, CONV_ROWS), :].astype(F32) * c).astype(y_ref.dtype)
            return 0

        lax.fori_loop(0, nchunk, conv, 0)

    def col(off):
        return pl.BlockSpec((s, LANES), lambda j: (0, off + j))

    return pl.pallas_call(
        body, name=name, grid=(nblk,),
        in_specs=[col(first), col(first + nblk), col(first + 2 * nblk), pl.BlockSpec((8, LANES), lambda j: (0, j))],
        out_specs=pl.BlockSpec((s, LANES), lambda j: (0, j)),
        out_shape=jax.ShapeDtypeStruct((s, nblk * LANES), BF16),
        scratch_shapes=[pltpu.VMEM((s + 2 * HALO, LANES), F32)],
        compiler_params=_cparams("parallel"),
    )(proj, proj, proj, conv_w8)


def _conv_bwd(dmix, proj, conv_w8, name):
    s = proj.shape[0]
    nblk = conv_w8.shape[1] // LANES
    first = 3 * nblk
    nchunk = s // CONV_ROWS

    def body(dy_ref, cb_ref, cc_ref, cu_ref, w_ref, dcb_ref, dcc_ref, dcu_ref, dw_ref, hpad, dcpad):
        hpad[pl.ds(0, 2 * HALO), :] = jnp.zeros((2 * HALO, LANES), F32)
        dcpad[pl.ds(s, 2 * HALO), :] = jnp.zeros((2 * HALO, LANES), F32)

        def fill(i, _):
            r0 = pl.multiple_of(i * CONV_ROWS, CONV_ROWS)
            hpad[pl.ds(r0 + 2 * HALO, CONV_ROWS), :] = (
                cc_ref[pl.ds(r0, CONV_ROWS), :].astype(F32) * cu_ref[pl.ds(r0, CONV_ROWS), :].astype(F32))
            return 0

        lax.fori_loop(0, nchunk, fill, 0)
        w0, w1, w2 = w_ref[0:1, :], w_ref[1:2, :], w_ref[2:3, :]

        def fold(v):
            return jnp.sum(v.reshape(CONV_ROWS // 8, 8, LANES), axis=0)

        def first_pass(i, acc):
            a0, a1, a2 = acc
            r0 = pl.multiple_of(i * CONV_ROWS, CONV_ROWS)
            win = hpad[pl.ds(r0 + HALO, CONV_ROWS + HALO), :]
            h0 = win[HALO:]
            h1 = pltpu.roll(win, 1, 0)[HALO:]
            h2 = pltpu.roll(win, 2, 0)[HALO:]
            c = w2 * h0 + w1 * h1 + w0 * h2
            dy = dy_ref[pl.ds(r0, CONV_ROWS), :]
            dcb_ref[pl.ds(r0, CONV_ROWS), :] = (dy * c).astype(dcb_ref.dtype)
            dc = dy * cb_ref[pl.ds(r0, CONV_ROWS), :].astype(F32)
            dcpad[pl.ds(r0, CONV_ROWS), :] = dc
            return a0 + fold(dc * h2), a1 + fold(dc * h1), a2 + fold(dc * h0)

        z8 = jnp.zeros((8, LANES), F32)
        a0, a1, a2 = lax.fori_loop(0, nchunk, first_pass, (z8, z8, z8))
        dw_ref[...] = jnp.concatenate(
            [jnp.sum(a0, axis=0, keepdims=True), jnp.sum(a1, axis=0, keepdims=True),
             jnp.sum(a2, axis=0, keepdims=True), jnp.zeros((5, LANES), F32)], axis=0)

        def second_pass(i, _):
            r0 = pl.multiple_of(i * CONV_ROWS, CONV_ROWS)
            win = dcpad[pl.ds(r0, CONV_ROWS + HALO), :]
            n = CONV_ROWS + HALO
            dh = (w2 * win[:CONV_ROWS] + w1 * pltpu.roll(win, n - 1, 0)[:CONV_ROWS]
                  + w0 * pltpu.roll(win, n - 2, 0)[:CONV_ROWS])
            dcc_ref[pl.ds(r0, CONV_ROWS), :] = (dh * cu_ref[pl.ds(r0, CONV_ROWS), :].astype(F32)).astype(dcc_ref.dtype)
            dcu_ref[pl.ds(r0, CONV_ROWS), :] = (dh * cc_ref[pl.ds(r0, CONV_ROWS), :].astype(F32)).astype(dcu_ref.dtype)
            return 0

        lax.fori_loop(0, nchunk, second_pass, 0)

    def col(off):
        return pl.BlockSpec((s, LANES), lambda j: (0, off + j))

    out = pl.BlockSpec((s, LANES), lambda j: (0, j))
    return pl.pallas_call(
        body, name=name, grid=(nblk,),
        in_specs=[col(nblk), col(first), col(first + nblk), col(first + 2 * nblk),
                  pl.BlockSpec((8, LANES), lambda j: (0, j))],
        out_specs=[out, out, out, pl.BlockSpec((8, LANES), lambda j: (0, j))],
        out_shape=[jax.ShapeDtypeStruct((s, nblk * LANES), BF16)] * 3 + [jax.ShapeDtypeStruct((8, nblk * LANES), F32)],
        scratch_shapes=[pltpu.VMEM((s + 2 * HALO, LANES), F32), pltpu.VMEM((s + 2 * HALO, LANES), F32)],
        compiler_params=_cparams("parallel"),
    )(dmix, proj, proj, proj, conv_w8)


LOG2E = 1.4426950408889634
LN2 = 0.6931471805599453
NEG_BIG = -1e30
SATURATED = 160.0


def _cumsum_matrix(kind):
    j = lax.broadcasted_iota(jnp.int32, (KEY_CHUNK, 2 * KEY_CHUNK), 0)
    c = lax.broadcasted_iota(jnp.int32, (KEY_CHUNK, 2 * KEY_CHUNK), 1)
    tri = {"after": j > c, "upto": j <= c, "before": j < c}[kind]
    return jnp.where((c >= KEY_CHUNK) | tri, 1.0, 0.0).astype(BF16)


def _stack_heads(t, m0):
    zero = jnp.zeros_like(t)
    return jnp.concatenate([jnp.where(m0, t, zero), jnp.where(m0, zero, t)], axis=0)


def _softplus2(z):
    sp = jnp.maximum(z, 0.0) + jnp.log2(1.0 + jnp.exp2(-jnp.abs(z)))
    return sp, z - sp


def _key_chunk(ref, kc):
    return ref[pl.ds(pl.multiple_of(kc * KEY_CHUNK, KEY_CHUNK), KEY_CHUNK), :]


def _attn_bwd(qk, proj, dmix, rtot, used, tq, name):
    s = qk.shape[0]
    nhp = qk.shape[1] // (2 * LANES)
    nc = tq // KEY_CHUNK

    def body(used_ref, q_ref, k_ref, v_ref, do_ref, r_ref, cmi_ref, cme_ref, bias_ref, dq_ref, dk_ref, dv_ref,
             z_refs, ls_refs, sig_refs, sp_refs, gb_refs, pr_ref, gs_ref):
        qi = pl.program_id(1)

        @pl.when(qi == 0)
        def _():
            dk_ref[...] = jnp.zeros_like(dk_ref)
            dv_ref[...] = jnp.zeros_like(dv_ref)

        nslots = (qi + 1) * nc
        first = jnp.clip(nslots - used_ref[pl.program_id(0), qi], 0, nslots - nc)
        m0 = lax.broadcasted_iota(jnp.int32, (1, LANES), 1) < HEAD_DIM
        qs = _stack_heads(q_ref[...], m0)
        do = do_ref[...]
        dos = _stack_heads(do.astype(BF16), m0)
        dosl = _stack_heads((do * LN2).astype(BF16), m0)
        cmi = cmi_ref[...]
        cme = cme_ref[...]

        def chunk_at(i):
            return jnp.clip(i, first, nslots - 1)

        def scores(kc):
            return _dot(qs, _key_chunk(k_ref, kc), NT)

        def weights(ls, cs, da, pr, kc):
            a = jnp.exp2(ls - (pr - cs[:, :KEY_CHUNK]))
            gb = (a * da).astype(BF16)
            ks = pl.multiple_of(kc * KEY_CHUNK, KEY_CHUNK)
            dv_ref[pl.ds(ks, KEY_CHUNK), :] += _dot(a, dos, TN)
            return gb, jnp.exp2(ls), pr - cs[:, KEY_CHUNK:]

        def score_grads(gb, sig, cg, gs, dq, kc):
            dzb = (gb.astype(F32) * (1.0 - sig) - sig * (gs + cg[:, :KEY_CHUNK])).astype(BF16)
            ks = pl.multiple_of(kc * KEY_CHUNK, KEY_CHUNK)
            dk_ref[pl.ds(ks, KEY_CHUNK), :] += _dot(dzb, qs, TN)
            dq = dq + _dot(jnp.concatenate([dzb[:tq], dzb[tq:]], axis=1), _stack_heads(_key_chunk(k_ref, kc), m0), NN)
            return gs + cg[:, KEY_CHUNK:], dq

        def step(i, par, bias=None):
            cur, prv = par, 1 - par
            k1, k2 = chunk_at(i - 1), chunk_at(i - 2)
            z_next = scores(chunk_at(i + 1))
            cs = _dot(sp_refs[prv][...], cmi, NN)
            da = _dot(dosl, _key_chunk(v_ref, k1), NT)
            cg = _dot(gb_refs[cur][...], cme, NN)
            z = z_refs[cur][...]
            if bias is not None:
                z = z + bias
            sp, ls = _softplus2(z)
            sp_refs[cur][...] = sp.astype(BF16)
            ls_refs[cur][...] = ls
            gs, dq = score_grads(gb_refs[cur][...], sig_refs[cur][...], cg, gs_ref[...], dq_ref[...], k2)
            gs_ref[...] = gs
            dq_ref[...] = dq
            gb, sig, pr = weights(ls_refs[prv][...], cs, da, pr_ref[...], k1)
            gb_refs[prv][...] = gb
            sig_refs[prv][...] = sig
            pr_ref[...] = pr
            z_refs[prv][...] = z_next

        pr_ref[...] = jnp.concatenate([r_ref[:, :LANES], r_ref[:, LANES:]], axis=0)
        gs_ref[...] = jnp.zeros((2 * tq, LANES), F32)
        dq_ref[...] = jnp.zeros((tq, LANES), F32)
        z_refs[0][...] = scores(first)
        sp_refs[1][...] = jnp.zeros((2 * tq, LANES), BF16)
        ls_refs[1][...] = jnp.full((2 * tq, LANES), NEG_BIG, F32)
        gb_refs[0][...] = jnp.zeros((2 * tq, LANES), BF16)
        sig_refs[0][...] = jnp.zeros((2 * tq, LANES), F32)

        def two_steps(j, _):
            step(2 * j, 0)
            step(2 * j + 1, 1)
            return 0

        lax.fori_loop(first // 2, nslots // 2 - 1, two_steps, 0)
        step(nslots - 2, 0, bias_ref[0])
        step(nslots - 1, 1, bias_ref[1])
        k1, k2 = chunk_at(nslots - 1), chunk_at(nslots - 2)
        gb, sig, _ = weights(ls_refs[1][...], _dot(sp_refs[1][...], cmi, NN),
                             _dot(dosl, _key_chunk(v_ref, k1), NT), pr_ref[...], k1)
        gb2 = gb_refs[0][...]
        gs, dq = score_grads(gb2, sig_refs[0][...], _dot(gb2, cme, NN), gs_ref[...], dq_ref[...], k2)
        _, dq = score_grads(gb, sig, _dot(gb, cme, NN), gs, dq, k1)
        dq_ref[...] = dq

    def wrapped(used_ref, q_ref, k_ref, v_ref, do_ref, r_ref, cmi_ref, cme_ref, bias_ref, dq_ref, dk_ref, dv_ref,
                z0, z1, ls0, ls1, sg0, sg1, sp0, sp1, gb0, gb1, pr_ref, gs_ref):
        body(used_ref, q_ref, k_ref, v_ref, do_ref, r_ref, cmi_ref, cme_ref, bias_ref, dq_ref, dk_ref, dv_ref,
             (z0, z1), (ls0, ls1), (sg0, sg1), (sp0, sp1), (gb0, gb1), pr_ref, gs_ref)

    assert nc == 2
    bias = _diag_bias(tq, True)
    bias = jnp.concatenate([bias[:, :, :KEY_CHUNK], bias[:, :, KEY_CHUNK:]], axis=1)
    qblk = pl.BlockSpec((tq, LANES), lambda p, i, u: (i, p))
    full = pl.BlockSpec((s, LANES), lambda p, i, u: (0, p))
    cmspec = pl.BlockSpec((KEY_CHUNK, 2 * KEY_CHUNK), lambda p, i, u: (0, 0))
    shape = jax.ShapeDtypeStruct((s, nhp * LANES), F32)
    f32buf = pltpu.VMEM((2 * tq, LANES), F32)
    bf16buf = pltpu.VMEM((2 * tq, LANES), BF16)
    return pl.pallas_call(
        wrapped, name=name,
        grid_spec=pltpu.PrefetchScalarGridSpec(
            num_scalar_prefetch=1, grid=(nhp, s // tq),
            in_specs=[qblk,
                      pl.BlockSpec((s, LANES), lambda p, i, u: (0, nhp + p)),
                      pl.BlockSpec((s, LANES), lambda p, i, u: (0, 2 * nhp + p)),
                      qblk,
                      pl.BlockSpec((tq, 2 * LANES), lambda p, i, u: (i, p)),
                      cmspec, cmspec,
                      pl.BlockSpec((nc, 2 * tq, LANES), lambda p, i, u: (0, 0, 0))],
            out_specs=[qblk, full, full],
            scratch_shapes=[f32buf] * 6 + [bf16buf] * 4 + [f32buf] * 2),
        out_shape=[shape, shape, shape],
        compiler_params=_cparams("parallel", "arbitrary"),
    )(used, qk, qk, proj, dmix, rtot, _cumsum_matrix("upto"), _cumsum_matrix("before"), bias)


def _pair_cumsum_matrix(kind):
    j = lax.broadcasted_iota(jnp.int32, (2 * KEY_CHUNK, 4 * KEY_CHUNK), 0)
    c = lax.broadcasted_iota(jnp.int32, (2 * KEY_CHUNK, 4 * KEY_CHUNK), 1)
    same_head = (j // KEY_CHUNK) == ((c // KEY_CHUNK) % 2)
    jj, cc = j % KEY_CHUNK, c % KEY_CHUNK
    tri = {"after": jj > cc, "upto": jj <= cc, "before": jj < cc}[kind]
    return jnp.where(same_head & ((c >= 2 * KEY_CHUNK) | tri), 1.0, 0.0).astype(BF16)


def _diag_bias(tq, ascending):
    nc = tq // KEY_CHUNK
    shape = (nc, tq, 2 * KEY_CHUNK)
    d = lax.broadcasted_iota(jnp.int32, shape, 0)
    r = lax.broadcasted_iota(jnp.int32, shape, 1)
    c = lax.broadcasted_iota(jnp.int32, shape, 2) % KEY_CHUNK
    chunk = d if ascending else nc - 1 - d
    return jnp.where(chunk * KEY_CHUNK + c < r, 0.0, NEG_BIG).astype(F32)


def _attn_fwd(qk, proj, tq, name):
    s = qk.shape[0]
    nhp = qk.shape[1] // (2 * LANES)
    nc = tq // KEY_CHUNK
    assert nc == 2
    w = 2 * KEY_CHUNK

    def body(q_ref, k_ref, v_ref, cm_ref, bias_ref, o_ref, r_ref, used_ref, z_refs, ls_refs, cs_refs, ct_refs,
             sp_refs, ab_refs, acc_ref):
        qi = pl.program_id(1)
        nslots = (qi + 1) * nc
        m0 = lax.broadcasted_iota(jnp.int32, (1, LANES), 1) < HEAD_DIM
        q = q_ref[...]
        cm = cm_ref[...]

        def chunk_at(i):
            return jnp.clip(nslots - 1 - i, 0, nslots - 1)

        def scores(kc):
            return _dot(q, _stack_heads(_key_chunk(k_ref, kc), m0), NT)

        def values(ab, kc):
            return _dot(ab, _stack_heads(_key_chunk(v_ref, kc), m0), NN)

        def step(i, par, bias=None, stages="zscwv"):
            cur, prv = par, 1 - par
            if "z" in stages:
                z_next = scores(chunk_at(i + 1))
            if "c" in stages:
                cs = _dot(sp_refs[prv][...], cm, NN)
            if "v" in stages:
                pv = values(ab_refs[prv][...], chunk_at(i - 3))
            if "w" in stages:
                rs = r_ref[...]
                r_ref[...] = rs + ct_refs[cur][...]
                ab_refs[cur][...] = jnp.exp2(ls_refs[cur][...] - cs_refs[cur][...] - rs).astype(BF16)
            if "s" in stages:
                z = z_refs[cur][...]
                if bias is not None:
                    z = z + bias
                sp, ls = _softplus2(z)
                sp_refs[cur][...] = sp.astype(BF16)
                ls_refs[cur][...] = ls
            if "v" in stages:
                acc_ref[...] += pv
            if "c" in stages:
                cs_refs[prv][...] = cs[:, :w]
                ct_refs[prv][...] = cs[:, w:]
            if "z" in stages:
                z_refs[prv][...] = z_next

        z_refs[0][...] = scores(chunk_at(0))
        for p in range(2):
            sp_refs[p][...] = jnp.zeros((tq, w), BF16)
            ls_refs[p][...] = jnp.full((tq, w), NEG_BIG, F32)
            cs_refs[p][...] = jnp.zeros((tq, w), F32)
            ct_refs[p][...] = jnp.zeros((tq, w), F32)
            ab_refs[p][...] = jnp.zeros((tq, w), BF16)
        r_ref[...] = jnp.zeros((tq, w), F32)
        acc_ref[...] = jnp.zeros((tq, LANES), F32)
        step(0, 0, bias_ref[0])
        step(1, 1, bias_ref[1])

        def two_steps(carry):
            j, _ = carry
            step(2 * j, 0)
            step(2 * j + 1, 1)
            return j + 1, jnp.min(r_ref[...])

        pairs, low = lax.while_loop(lambda c: jnp.logical_and(c[0] < nslots // 2, c[1] < SATURATED), two_steps,
                                    (jnp.int32(1), jnp.min(r_ref[...])))
        entered = 2 * pairs
        saturated = low >= SATURATED

        @pl.when(saturated)
        def _():
            step(entered, 0, stages="v")

        @pl.when(jnp.logical_not(saturated))
        def _():
            step(entered, 0, stages="cwv")
            step(entered + 1, 1, stages="wv")
            step(entered + 2, 0, stages="v")

        o_ref[...] = acc_ref[...].astype(o_ref.dtype)
        used_ref[pl.program_id(0), qi] = jnp.where(saturated, entered - 2, entered)

    def wrapped(q_ref, k_ref, v_ref, cm_ref, bias_ref, o_ref, r_ref, used_ref, *scratch):
        z, ls, cs, ct, sp, ab = [scratch[2 * j:2 * j + 2] for j in range(6)]
        body(q_ref, k_ref, v_ref, cm_ref, bias_ref, o_ref, r_ref, used_ref, z, ls, cs, ct, sp, ab, scratch[12])

    f32buf = pltpu.VMEM((tq, w), F32)
    bf16buf = pltpu.VMEM((tq, w), BF16)
    return pl.pallas_call(
        wrapped, name=name, grid=(nhp, s // tq),
        in_specs=[pl.BlockSpec((tq, LANES), lambda p, i: (i, p)),
                  pl.BlockSpec((s, LANES), lambda p, i: (0, nhp + p)),
                  pl.BlockSpec((s, LANES), lambda p, i: (0, 2 * nhp + p)),
                  pl.BlockSpec((w, 2 * w), lambda p, i: (0, 0)),
                  pl.BlockSpec((nc, tq, w), lambda p, i: (0, 0, 0))],
        out_specs=[pl.BlockSpec((tq, LANES), lambda p, i: (i, p)),
                   pl.BlockSpec((tq, w), lambda p, i: (i, p)),
                   pl.BlockSpec(memory_space=pltpu.SMEM)],
        out_shape=[jax.ShapeDtypeStruct((s, nhp * LANES), BF16),
                   jax.ShapeDtypeStruct((s, nhp * w), F32),
                   jax.ShapeDtypeStruct((nhp, s // tq), jnp.int32)],
        scratch_shapes=[f32buf] * 8 + [bf16buf] * 4 + [pltpu.VMEM((tq, LANES), F32)],
        compiler_params=_cparams("arbitrary", "arbitrary"),
    )(qk, qk, proj, _pair_cumsum_matrix("after"), _diag_bias(tq, False))


BLOCK_PAIR = 2


def _side_by_side(b_ref):
    return jnp.concatenate([b_ref[p] for p in range(BLOCK_PAIR)], axis=1)


def _mm_blocks(h, ga, widx, tm, name):
    s, d = h.shape
    nb, cols = ga.shape[1], ga.shape[3]

    def body(a_ref, b_ref, o_ref):
        o_ref[...] = _dot(a_ref[...], _side_by_side(b_ref), NN).astype(o_ref.dtype)

    return pl.pallas_call(
        body, name=name, grid=(s // tm, nb // BLOCK_PAIR),
        in_specs=[pl.BlockSpec((tm, d), lambda i, j: (i, 0)),
                  pl.BlockSpec((None, BLOCK_PAIR, d, cols), lambda i, j: (widx, j, 0, 0))],
        out_specs=pl.BlockSpec((tm, BLOCK_PAIR * cols), lambda i, j: (i, j)),
        out_shape=jax.ShapeDtypeStruct((s, nb * cols), BF16),
        compiler_params=_cparams("parallel", "arbitrary"),
    )(h, ga)


def _mm_swiglu(h, ga, gidx, uidx, tm, name):
    s, d = h.shape
    nb, cols = ga.shape[1], ga.shape[3]

    def body(a_ref, bg_ref, bu_ref, g_ref, u_ref, act_ref):
        a = a_ref[...]
        g = _dot(a, _side_by_side(bg_ref), NN)
        u = _dot(a, _side_by_side(bu_ref), NN)
        g_ref[...] = g.astype(g_ref.dtype)
        u_ref[...] = u.astype(u_ref.dtype)
        act_ref[...] = (g * (1.0 / (1.0 + jnp.exp(-g))) * u).astype(act_ref.dtype)

    def wspec(idx):
        return pl.BlockSpec((None, BLOCK_PAIR, d, cols), lambda i, j: (idx, j, 0, 0))

    out = pl.BlockSpec((tm, BLOCK_PAIR * cols), lambda i, j: (i, j))
    shape = jax.ShapeDtypeStruct((s, nb * cols), BF16)
    return pl.pallas_call(
        body, name=name, grid=(s // tm, nb // BLOCK_PAIR),
        in_specs=[pl.BlockSpec((tm, d), lambda i, j: (i, 0)), wspec(gidx), wspec(uidx)],
        out_specs=[out, out, out], out_shape=[shape, shape, shape],
        compiler_params=_cparams("parallel", "arbitrary"),
    )(h, ga, ga)


def _mm_residual(a, w3, lidx, res, tm, tn, name):
    s, k = a.shape
    n = w3.shape[2]

    def body(a_ref, b_ref, r_ref, o_ref):
        o_ref[...] = r_ref[...] + _dot(a_ref[...], b_ref[...], NN)

    return pl.pallas_call(
        body, name=name, grid=(s // tm, n // tn),
        in_specs=[pl.BlockSpec((tm, k), lambda i, j: (i, 0)),
                  pl.BlockSpec((None, k, tn), lambda i, j: (lidx, 0, j)),
                  pl.BlockSpec((tm, tn), lambda i, j: (i, j))],
        out_specs=pl.BlockSpec((tm, tn), lambda i, j: (i, j)),
        out_shape=jax.ShapeDtypeStruct((s, n), F32),
        compiler_params=_cparams("parallel", "arbitrary"),
    )(a, w3, res)


def _mm_nt(a, w3, lidx, tm, tn, name):
    s, k = a.shape
    n = w3.shape[1]

    def body(a_ref, b_ref, o_ref):
        o_ref[...] = _dot(a_ref[...], b_ref[...], NT)

    return pl.pallas_call(
        body, name=name, grid=(s // tm, n // tn),
        in_specs=[pl.BlockSpec((tm, k), lambda i, j: (i, 0)),
                  pl.BlockSpec((None, tn, k), lambda i, j: (lidx, j, 0))],
        out_specs=pl.BlockSpec((tm, tn), lambda i, j: (i, j)),
        out_shape=jax.ShapeDtypeStruct((s, n), F32),
        compiler_params=_cparams("parallel", "arbitrary"),
    )(a, w3)


def _mm_nt_swiglu_bwd(dx, wd3, lidx, g, u, tm, name):
    s, d = dx.shape
    cols = BLOCK_PAIR * (g.shape[1] // N_DEV)

    def body(a_ref, b_ref, g_ref, u_ref, dg_ref, du_ref):
        dact = _dot(a_ref[...], b_ref[...], NT)
        gv = g_ref[...].astype(F32)
        sig = 1.0 / (1.0 + jnp.exp(-gv))
        du_ref[...] = (dact * (gv * sig)).astype(du_ref.dtype)
        dg_ref[...] = (dact * u_ref[...].astype(F32) * (sig * (1.0 + gv * (1.0 - sig)))).astype(dg_ref.dtype)

    blk = pl.BlockSpec((tm, cols), lambda i, j: (i, j))
    shape = jax.ShapeDtypeStruct(g.shape, BF16)
    return pl.pallas_call(
        body, name=name, grid=(s // tm, N_DEV // BLOCK_PAIR),
        in_specs=[pl.BlockSpec((tm, d), lambda i, j: (i, 0)),
                  pl.BlockSpec((None, cols, d), lambda i, j: (lidx, j, 0)), blk, blk],
        out_specs=[blk, blk], out_shape=[shape, shape],
        compiler_params=_cparams("parallel", "arbitrary"),
    )(dx, wd3, g, u)


def _mm_nt_blocks(das, ga, widxs, tm, name):
    s = das[0].shape[0]
    nb, d, cols = ga.shape[1], ga.shape[2], ga.shape[3]
    nw = len(das)

    def body(*refs):
        a_refs, b_refs, o_ref = refs[:nw], refs[nw:2 * nw], refs[2 * nw]
        acc = None
        wide = BLOCK_PAIR * cols
        for w in range(nw):
            for k in range(nb // BLOCK_PAIR):
                b = jnp.concatenate([b_refs[w][BLOCK_PAIR * k + p] for p in range(BLOCK_PAIR)], axis=1)
                part = _dot(a_refs[w][:, k * wide:(k + 1) * wide], b, NT)
                acc = part if acc is None else acc + part
        o_ref[...] = acc

    def wspec(idx):
        return pl.BlockSpec((None, nb, d, cols), lambda i: (idx, 0, 0, 0), pipeline_mode=pl.Buffered(1))

    return pl.pallas_call(
        body, name=name, grid=(s // tm,),
        in_specs=[pl.BlockSpec((tm, nb * cols), lambda i: (i, 0))] * nw + [wspec(i) for i in widxs],
        out_specs=pl.BlockSpec((tm, d), lambda i: (i, 0)),
        out_shape=jax.ShapeDtypeStruct((s, d), F32),
        compiler_params=_cparams("parallel"),
    )(*das, *([ga] * nw))


def _mm_tn(a, b, ta, tb, tk, out_blocks, name):
    s, ka = a.shape
    nb = b.shape[1]
    nk = s // tk
    cols = tb
    if out_blocks:
        tb = BLOCK_PAIR * cols

    def body(a_ref, b_ref, o_ref, ob_ref):
        k = pl.program_id(2)
        part = _dot(a_ref[...], b_ref[...], TN)

        def put(first):
            if out_blocks:
                for p in range(BLOCK_PAIR):
                    piece = part[:, p * cols:(p + 1) * cols]
                    o_ref[p] = piece if first else o_ref[p] + piece
            else:
                o_ref[...] = part if first else o_ref[...] + part

        @pl.when(k == 0)
        def _():
            put(True)

        @pl.when(k > 0)
        def _():
            put(False)

        @pl.when(k == nk - 1)
        def _():
            ob_ref[...] = o_ref[...].astype(ob_ref.dtype)

    if out_blocks:
        out_spec = pl.BlockSpec((BLOCK_PAIR, ta, cols), lambda i, j, k: (j, i, 0))
        shape = (nb // cols, ka, cols)
    else:
        out_spec = pl.BlockSpec((ta, tb), lambda i, j, k: (i, j))
        shape = (ka, nb)
    return pl.pallas_call(
        body, name=name, grid=(ka // ta, nb // tb, nk),
        in_specs=[pl.BlockSpec((tk, ta), lambda i, j, k: (k, i)),
                  pl.BlockSpec((tk, tb), lambda i, j, k: (k, j))],
        out_specs=[out_spec, out_spec],
        out_shape=[jax.ShapeDtypeStruct(shape, F32), jax.ShapeDtypeStruct(shape, BF16)],
        compiler_params=_cparams("parallel", "parallel", "arbitrary"),
    )(a, b)


def _loss_head(y, target, tm, name):
    s, d = y.shape
    nsteps = s // tm

    def body(y_ref, t_ref, dy_ref, dyb_ref, l_ref, acc):
        i = pl.program_id(0)
        diff = y_ref[...] - t_ref[...]
        dy_ref[...] = diff * (1.0 / d)
        dyb_ref[...] = (diff * (1.0 / d)).astype(dyb_ref.dtype)
        part = jnp.sum((diff * diff).reshape(tm // 8, 8, d), axis=0)

        @pl.when(i == 0)
        def _():
            acc[...] = part

        @pl.when(i > 0)
        def _():
            acc[...] += part

        @pl.when(i == nsteps - 1)
        def _():
            tot = jnp.sum(jnp.sum(acc[...], axis=1, keepdims=True), axis=0, keepdims=True)
            l_ref[...] = jnp.broadcast_to(tot * (0.5 / d), (8, LANES))

    row = pl.BlockSpec((tm, d), lambda i: (i, 0))
    return pl.pallas_call(
        body, name=name, grid=(nsteps,),
        in_specs=[row, row],
        out_specs=[row, row, pl.BlockSpec((8, LANES), lambda i: (0, 0))],
        out_shape=[jax.ShapeDtypeStruct((s, d), F32), jax.ShapeDtypeStruct((s, d), BF16),
                   jax.ShapeDtypeStruct((8, LANES), F32)],
        scratch_shapes=[pltpu.VMEM((8, d), F32)],
        compiler_params=_cparams("arbitrary"),
    )(y, target)


def _adamw(parts, own, w, m, v, tr, name):
    p, rows, cols = parts.shape
    c1 = 1.0 / (1.0 - ADAM_B1 ** ADAM_STEP)
    c2 = 1.0 / (1.0 - ADAM_B2 ** ADAM_STEP)

    def body(*refs):
        if own is None:
            p_ref, w_ref, m_ref, v_ref, g_ref, d_ref, nm_ref, nv_ref = refs
            g = p_ref[0]
            for k in range(1, p):
                g = g + p_ref[k]
        else:
            p_ref, own_ref, w_ref, m_ref, v_ref, g_ref, d_ref, nm_ref, nv_ref = refs
            x, y, c = _place()
            my = 4 * x + 2 * y + c
            mine = own_ref[...]
            g = jnp.where(my == 0, mine, p_ref[0].astype(F32))
            for k in range(1, p):
                g = g + jnp.where(my == k, mine, p_ref[k].astype(F32))
        nm = ADAM_B1 * m_ref[...] + (1.0 - ADAM_B1) * g
        nv = ADAM_B2 * v_ref[...] + (1.0 - ADAM_B2) * (g * g)
        g_ref[...] = g
        nm_ref[...] = nm
        nv_ref[...] = nv
        d_ref[...] = -ADAM_LR * ((nm * c1) / (jnp.sqrt(nv * c2) + ADAM_EPS) + ADAM_WD * w_ref[...])

    blk = pl.BlockSpec((tr, cols), lambda i: (i, 0))
    shape = jax.ShapeDtypeStruct((rows, cols), F32)
    return pl.pallas_call(
        body, name=name, grid=(rows // tr,),
        in_specs=[pl.BlockSpec((p, tr, cols), lambda i: (0, i, 0))] + [blk] * (3 if own is None else 4),
        out_specs=[blk] * 4, out_shape=[shape] * 4,
        compiler_params=_cparams("parallel"),
    )(*([parts] + ([] if own is None else [own]) + [w, m, v]))


def _place():
    x, y, c = lax.axis_index("x"), lax.axis_index("y"), lax.axis_index("c")
    return x, y, c


def _all_gather(shards, name):
    na = len(shards)

    def body(*refs):
        srcs, dsts = refs[:na], refs[na:2 * na]
        send_sems, recv_sems, local_sems = refs[2 * na:]
        x, y, c = _place()
        me, sibling = (x, y, c), (x, y, 1 - c)
        chips = [(1 - x, y), (x, 1 - y), (1 - x, 1 - y)]

        def slot(a, dev):
            return dsts[a].at[:, pl.ds(4 * dev[0] + 2 * dev[1] + dev[2], 1)]

        def copy(k, a, block, to, from_shard=False):
            return pltpu.make_async_remote_copy(
                src_ref=srcs[a] if from_shard else slot(a, block), dst_ref=slot(a, block),
                send_sem=send_sems.at[k, a], recv_sem=recv_sems.at[k, a], device_id=to, device_id_type=MESH)

        mine = [pltpu.make_async_copy(srcs[a], slot(a, me), local_sems.at[a]) for a in range(na)]
        for cp in mine:
            cp.start()
        first = [copy(0, a, me, sibling, True) for a in range(na)]
        first += [copy(1 + j, a, me, (*chip, c), True) for j, chip in enumerate(chips) for a in range(na)]
        for cp in first:
            cp.start()
        passed = []
        for j, chip in enumerate(chips):
            for a in range(na):
                copy(1 + j, a, (*chip, c), me).wait_recv()
                fwd = copy(4 + j, a, (*chip, c), sibling)
                fwd.start()
                passed.append(fwd)
        for a in range(na):
            copy(0, a, sibling, me).wait_recv()
        for j, chip in enumerate(chips):
            for a in range(na):
                copy(4 + j, a, (*chip, 1 - c), me).wait_recv()
        for cp in first + passed:
            cp.wait_send()
        for cp in mine:
            cp.wait()

    anyspec = pl.BlockSpec(memory_space=pl.ANY)
    return pl.pallas_call(
        body, name=name,
        in_specs=[anyspec] * na, out_specs=[anyspec] * na,
        out_shape=[jax.ShapeDtypeStruct((a.shape[0], N_DEV) + a.shape[2:], a.dtype) for a in shards],
        scratch_shapes=[pltpu.SemaphoreType.DMA((7, na)), pltpu.SemaphoreType.DMA((7, na)),
                        pltpu.SemaphoreType.DMA((na,))],
    )(*shards)


_RELATIONS = [(dx, dy, dc) for dx in (0, 1) for dy in (0, 1) for dc in (0, 1)][1:]


def _flip(v, d):
    return 1 - v if d else v


def _exchange_blocks(grads, name):
    na = len(grads)

    def body(*refs):
        srcs, dsts = refs[:na], refs[na:2 * na]
        send_sems, recv_sems, local_sems = refs[2 * na:]
        x, y, c = _place()
        my = 4 * x + 2 * y + c
        mine = [pltpu.make_async_copy(srcs[a].at[pl.ds(my, 1)], dsts[a].at[pl.ds(my, 1)], local_sems.at[a])
                for a in range(na)]
        for cp in mine:
            cp.start()
        sends = []
        for k, (dx, dy, dc) in enumerate(_RELATIONS):
            peer = (_flip(x, dx), _flip(y, dy), _flip(c, dc))
            pidx = 4 * peer[0] + 2 * peer[1] + peer[2]
            for a in range(na):
                cp = pltpu.make_async_remote_copy(
                    src_ref=srcs[a].at[pl.ds(pidx, 1)], dst_ref=dsts[a].at[pl.ds(my, 1)],
                    send_sem=send_sems.at[k, a], recv_sem=recv_sems.at[k, a], device_id=peer, device_id_type=MESH)
                cp.start()
                sends.append((cp, pidx, k, a, peer))
        for cp, pidx, k, a, peer in sends:
            pltpu.make_async_remote_copy(
                src_ref=srcs[a].at[pl.ds(pidx, 1)], dst_ref=dsts[a].at[pl.ds(pidx, 1)],
                send_sem=send_sems.at[k, a], recv_sem=recv_sems.at[k, a], device_id=peer,
                device_id_type=MESH).wait_recv()
        for cp, *_ in sends:
            cp.wait_send()
        for cp in mine:
            cp.wait()

    anyspec = pl.BlockSpec(memory_space=pl.ANY)
    return pl.pallas_call(
        body, name=name,
        in_specs=[anyspec] * na, out_specs=[anyspec] * na,
        out_shape=[jax.ShapeDtypeStruct(a.shape, a.dtype) for a in grads],
        scratch_shapes=[pltpu.SemaphoreType.DMA((7, na)), pltpu.SemaphoreType.DMA((7, na)),
                        pltpu.SemaphoreType.DMA((na,))],
    )(*grads)


def _all_reduce_small(v, name):
    r, c_ = v.shape

    def body(v_ref, o_ref, gath, send_sems, recv_sems):
        x, y, c = _place()
        my = 4 * x + 2 * y + c
        gath[my] = v_ref[...]
        sends = []
        for k, (dx, dy, dc) in enumerate(_RELATIONS):
            peer = (_flip(x, dx), _flip(y, dy), _flip(c, dc))
            cp = pltpu.make_async_remote_copy(
                src_ref=v_ref, dst_ref=gath.at[my], send_sem=send_sems.at[k], recv_sem=recv_sems.at[k],
                device_id=peer, device_id_type=MESH)
            cp.start()
            sends.append((cp, 4 * peer[0] + 2 * peer[1] + peer[2], k, peer))
        for cp, pidx, k, peer in sends:
            pltpu.make_async_remote_copy(
                src_ref=v_ref, dst_ref=gath.at[pidx], send_sem=send_sems.at[k], recv_sem=recv_sems.at[k],
                device_id=peer, device_id_type=MESH).wait_recv()
        for cp, *_ in sends:
            cp.wait_send()
        tot = gath[0]
        for k in range(1, N_DEV):
            tot = tot + gath[k]
        o_ref[...] = tot

    vm = pl.BlockSpec(memory_space=pltpu.VMEM)
    return pl.pallas_call(
        body, name=name, in_specs=[vm], out_specs=vm,
        out_shape=jax.ShapeDtypeStruct((r, c_), F32),
        scratch_shapes=[pltpu.VMEM((N_DEV, r, c_), F32), pltpu.SemaphoreType.DMA((7,)),
                        pltpu.SemaphoreType.DMA((7,))],
    )(v)


TM = 512
TM_MATMUL = 2048
TM_RESIDUAL = 1024
TQ = 256


def _pad_to(a, axis, size):
    pad = [(0, 0)] * a.ndim
    pad[axis] = (0, size - a.shape[axis])
    return jnp.pad(a, pad)


def _local_step(x, target, ga, gb, gc, conv_full, norm_mix, q_norm, k_norm, norm_ffn):
    depth = gb.shape[0]
    tm, tq = min(TM, x.shape[0]), min(TQ, x.shape[0])
    tmm, tmr = min(TM_MATMUL, x.shape[0]), min(TM_RESIDUAL, x.shape[0])
    attn = gb.shape[1] // 2
    nheads = attn // HEAD_DIM
    scale = HEAD_DIM ** -0.5 * LOG2E
    saved = []
    for l in range(depth):
        h1 = _rmsnorm_fwd(x, norm_mix[l][None], tm,f"norm_mix_fwd_{l}")
        proj = _mm_blocks(h1, ga, 3 * l, tmm, f"proj_in_{l}")
        qk_gain = jnp.concatenate([jnp.tile(q_norm[l], nheads) * scale, jnp.tile(k_norm[l], nheads)])[None]
        qk = _qknorm_fwd(proj, qk_gain, tmm, f"qknorm_fwd_{l}")
        o, rtot, used = _attn_fwd(qk, proj, tq, f"attn_fwd_{l}")
        conv_w8 = _pad_to(conv_full[l], 0, 8)
        cv = _conv_fwd(proj, conv_w8, f"conv_fwd_{l}")
        mix = jnp.concatenate([o, cv], axis=1)
        x1 = _mm_residual(mix, gb, l, x, tmr, 512, f"proj_out_{l}")
        h2 = _rmsnorm_fwd(x1, norm_ffn[l][None], tm,f"norm_ffn_fwd_{l}")
        g, u, act = _mm_swiglu(h2, ga, 3 * l + 1, 3 * l + 2, tmr, f"ffn_up_{l}")
        x2 = _mm_residual(act, gc, l, x1, tmr, 512, f"ffn_down_{l}")
        saved.append((x, h1, proj, qk_gain, qk, rtot, used, conv_w8, mix, x1, h2, g, u, act))
        x = x2

    dx, dxb, loss = _loss_head(x, target, tm, "loss_head")

    grads = [None] * depth
    small = [None] * depth
    for l in reversed(range(depth)):
        x0, h1, proj, qk_gain, qk, rtot, used, conv_w8, mix, x1, h2, g, u, act = saved[l]
        d = x0.shape[1]
        dg, du = _mm_nt_swiglu_bwd(dxb, gc, l, g, u, tmr, f"ffn_down_bwd_{l}")
        d_wdown = _mm_tn(act, dxb, 768, d, tmm, False, f"dw_down_{l}")
        d_wgate = _mm_tn(h2, dg, d, ga.shape[3], tmm, True, f"dw_gate_{l}")
        d_wup = _mm_tn(h2, du, d, ga.shape[3], tmm, True, f"dw_up_{l}")
        dh2 = _mm_nt_blocks([dg, du], ga, [3 * l + 1, 3 * l + 2], tm, f"ffn_up_bwd_{l}")
        dx1, dx1b, dg_ffn = _rmsnorm_bwd(dh2, x1, norm_ffn[l][None], dx, tm, f"norm_ffn_bwd_{l}")
        dmix = _mm_nt(dx1b, gb, l, tmr, 512, f"proj_out_bwd_{l}")
        d_wout = _mm_tn(mix, dx1b, 512, d, tmm, False, f"dw_out_{l}")
        dcb, dcc, dcu, dconv = _conv_bwd(dmix, proj, conv_w8, f"conv_bwd_{l}")
        dq, dk, dv = _attn_bwd(qk, proj, dmix, rtot, used, tq, f"attn_bwd_{l}")
        dqk, dg_qk = _qknorm_bwd(jnp.concatenate([dq, dk], axis=1), proj, qk_gain, tmm, f"qknorm_bwd_{l}")
        dproj = jnp.concatenate([dqk, dv.astype(BF16), dcb, dcc, dcu], axis=1)
        d_win = _mm_tn(h1, dproj, d, ga.shape[3], tmm, True, f"dw_in_{l}")
        dh1 = _mm_nt_blocks([dproj], ga, [3 * l], tmr, f"proj_in_bwd_{l}")
        dx, dxb, dg_mix = _rmsnorm_bwd(dh1, x0, norm_mix[l][None], dx1, tm, f"norm_mix_bwd_{l}")
        grads[l] = (d_win, d_wgate, d_wup, d_wout, d_wdown)
        dq_gain = jnp.sum(dg_qk[0, :attn].reshape(nheads, HEAD_DIM), axis=0) * scale
        dk_gain = jnp.sum(dg_qk[0, attn:].reshape(nheads, HEAD_DIM), axis=0)
        small[l] = (dg_mix[0], dg_ffn[0], dq_gain, dk_gain, dconv[:3])
    return loss, dx, grads, small


def kernel(x, norm_mix, w_in, q_norm, k_norm, conv_w, w_out, norm_ffn, w_gate, w_up, w_down, loss_target, m_norm_mix, m_w_in, m_q_norm, m_k_norm, m_conv_w, m_w_out, m_norm_ffn, m_w_gate, m_w_up, m_w_down, v_norm_mix, v_w_in, v_q_norm, v_k_norm, v_conv_w, v_w_out, v_norm_ffn, v_w_gate, v_w_up, v_w_down):
    depth, d, in_shard = w_in.shape
    ff_shard = w_gate.shape[2]
    ff_pad = in_shard
    conv_shard = conv_w.shape[2]
    xs = x.reshape(x.shape[-2], d)
    target = loss_target.reshape(xs.shape)

    pa = jnp.stack([w_in, _pad_to(w_gate, 2, ff_pad), _pad_to(w_up, 2, ff_pad)], axis=1)
    pa = pa.reshape(3 * depth, 1, d, in_shard).astype(BF16)
    pb = w_out.astype(BF16)[:, None]
    pc = _pad_to(w_down, 1, ff_pad).astype(BF16)[:, None]
    pd = _pad_to(_pad_to(conv_w.reshape(depth * 3, conv_shard), 0, 8), 1, LANES)[None, None]
    ga, gb, gc, gd = _all_gather([pa, pb, pc, pd], "gather_weights")
    gb = gb.reshape(depth, N_DEV * gb.shape[2], d)
    gc = gc.reshape(depth, N_DEV * ff_pad, d)
    conv_full = gd[0, :, :depth * 3, :conv_shard].transpose(1, 0, 2).reshape(depth, 3, N_DEV * conv_shard)

    loss, grad_x, grads, small = _local_step(xs, target, ga, gb, gc, conv_full, norm_mix, q_norm, k_norm, norm_ffn)

    x_, y_, c_ = _place()
    my = 4 * x_ + 2 * y_ + c_
    block_rows = (d, d, d, w_out.shape[1], ff_pad)
    send = [grads[l][j][1].reshape(N_DEV, block_rows[j], -1) for l in range(depth) for j in range(5)]
    landed = _exchange_blocks(send, "exchange_grads")
    own = [lax.dynamic_index_in_dim(grads[l][j][0].reshape(N_DEV, block_rows[j], -1), my, 0, keepdims=False)
           for l in range(depth) for j in range(5)]

    nconv = N_DEV * conv_shard
    rows = []
    for l in range(depth):
        g_mix, g_ffn, g_q, g_k, g_conv = small[l]
        qkrow = _pad_to(jnp.concatenate([g_q, g_k]), 0, d)
        rows += [g_mix[None], g_ffn[None], qkrow[None], _pad_to(g_conv, 1, d)]
    nrow = 6 * depth
    packed = jnp.concatenate(rows + [_pad_to(loss[:1], 1, d)], axis=0)
    packed = _pad_to(packed, 0, ((nrow + 1 + 7) // 8) * 8)
    summed = _all_reduce_small(packed, "reduce_small")
    loss_out = summed[nrow, 0]

    def big(i, w, m, v, tr, name, rows_=None, cols_=None):
        parts = landed[i]
        pr, pcn = parts.shape[1], parts.shape[2]
        w, m, v = [_pad_to(_pad_to(t, 0, pr), 1, pcn) for t in (w, m, v)]
        outs = _adamw(parts, own[i], w, m, v, tr, name)
        return [o[:rows_ or pr, :cols_ or pcn] for o in outs]

    res = {}
    for l in range(depth):
        i = 5 * l
        res[("w_in", l)] = big(i, w_in[l], m_w_in[l], v_w_in[l], 256, f"adamw_in_{l}")
        res[("w_gate", l)] = big(i + 1, w_gate[l], m_w_gate[l], v_w_gate[l], 256, f"adamw_gate_{l}", cols_=ff_shard)
        res[("w_up", l)] = big(i + 2, w_up[l], m_w_up[l], v_w_up[l], 256, f"adamw_up_{l}", cols_=ff_shard)
        res[("w_out", l)] = big(i + 3, w_out[l], m_w_out[l], v_w_out[l], w_out.shape[1], f"adamw_out_{l}")
        res[("w_down", l)] = big(i + 4, w_down[l], m_w_down[l], v_w_down[l], 128, f"adamw_down_{l}",
                                 rows_=ff_shard)

    g_rows, w_rows, m_rows, v_rows = [], [], [], []
    for l in range(depth):
        base = l * 6
        conv_g = lax.dynamic_slice(summed[base + 3:base + 6], (0, my * conv_shard), (3, conv_shard))
        g_rows += [summed[base:base + 3], _pad_to(conv_g, 1, d)]
        for dst, (nm, qn, kn, nf, cw) in ((w_rows, (norm_mix, q_norm, k_norm, norm_ffn, conv_w)),
                                          (m_rows, (m_norm_mix, m_q_norm, m_k_norm, m_norm_ffn, m_conv_w)),
                                          (v_rows, (v_norm_mix, v_q_norm, v_k_norm, v_norm_ffn, v_conv_w))):
            dst += [nm[l][None], nf[l][None], _pad_to(jnp.concatenate([qn[l], kn[l]]), 0, d)[None],
                    _pad_to(cw[l], 1, d)]
    prow = ((nrow + 7) // 8) * 8
    gs, ws, ms, vs = [_pad_to(jnp.concatenate(t, axis=0), 0, prow) for t in (g_rows, w_rows, m_rows, v_rows)]
    sm = _adamw(gs[None], None, ws, ms, vs, prow, "adamw_small")

    hd = q_norm.shape[1]

    def small_out(t, kind):
        per_layer = []
        for l in range(depth):
            base = l * 6
            per_layer.append({"norm_mix": t[base], "norm_ffn": t[base + 1], "q_norm": t[base + 2, :hd],
                              "k_norm": t[base + 2, hd:2 * hd], "conv_w": t[base + 3:base + 6, :conv_shard]}[kind])
        return jnp.stack(per_layer)

    def big_out(name, i):
        return jnp.stack([res[(name, l)][i] for l in range(depth)])

    outs = [loss_out, grad_x.reshape(x.shape)]
    for i in range(4):
        outs += [small_out(sm[i], "norm_mix"), big_out("w_in", i), small_out(sm[i], "q_norm"),
                 small_out(sm[i], "k_norm"), small_out(sm[i], "conv_w"), big_out("w_out", i),
                 small_out(sm[i], "norm_ffn"), big_out("w_gate", i), big_out("w_up", i), big_out("w_down", i)]
    return tuple(outs)
```

```python
import jax
import jax.numpy as jnp
from jax import lax
from jax.experimental import pallas as pl
from jax.experimental.pallas import tpu as pltpu

F32 = jnp.float32
BF16 = jnp.bfloat16
MESH = pl.DeviceIdType.MESH

N_DEV = 8
LANES = 128
HEAD_DIM = 64
KEY_CHUNK = 128
EPS = 1e-6
VMEM_LIMIT = 48 * 1024 * 1024

ADAM_LR = 0.001
ADAM_B1 = 0.9
ADAM_B2 = 0.999
ADAM_EPS = 1e-08
ADAM_WD = 0.01
ADAM_STEP = 10

NN = (((1,), (0,)), ((), ()))
NT = (((1,), (1,)), ((), ()))
TN = (((0,), (0,)), ((), ()))


def _dot(a, b, dims):
    return lax.dot_general(a.astype(BF16), b.astype(BF16), dims, preferred_element_type=F32)


def _cparams(*sem):
    return pltpu.CompilerParams(dimension_semantics=sem, vmem_limit_bytes=VMEM_LIMIT)


def _split_hi_lo(v):
    hi = v.astype(BF16)
    lo = (v - hi.astype(F32)).astype(BF16)
    return jnp.concatenate([hi, lo], axis=1)


def _rmsnorm_fwd(x, gain, tm, name):
    s, d = x.shape

    def body(x_ref, g_ref, o_ref):
        xv = x_ref[...]
        r = lax.rsqrt(jnp.mean(xv * xv, axis=-1, keepdims=True) + EPS)
        o_ref[...] = ((xv * r) * g_ref[...]).astype(o_ref.dtype)

    return pl.pallas_call(
        body, name=name, grid=(s // tm,),
        in_specs=[pl.BlockSpec((tm, d), lambda i: (i, 0)), pl.BlockSpec((1, d), lambda i: (0, 0))],
        out_specs=pl.BlockSpec((tm, d), lambda i: (i, 0)),
        out_shape=jax.ShapeDtypeStruct((s, d), BF16),
        compiler_params=_cparams("parallel"),
    )(x, gain)


def _rmsnorm_bwd(dh, x, gain, dres, tm, name):
    s, d = x.shape
    nsteps = s // tm

    def body(dh_ref, x_ref, g_ref, dres_ref, dx_ref, dxb_ref, dg_ref):
        i = pl.program_id(0)
        xv = x_ref[...]
        r = lax.rsqrt(jnp.mean(xv * xv, axis=-1, keepdims=True) + EPS)
        xhat = xv * r
        dhv = dh_ref[...]
        dxh = dhv * g_ref[...]
        proj = jnp.mean(dxh * xhat, axis=-1, keepdims=True)
        dxv = dres_ref[...] + r * (dxh - xhat * proj)
        dx_ref[...] = dxv
        dxb_ref[...] = dxv.astype(dxb_ref.dtype)
        part = jnp.sum((dhv * xhat).reshape(tm // 8, 8, d), axis=0)

        @pl.when(i == 0)
        def _():
            dg_ref[...] = part

        @pl.when(i > 0)
        def _():
            dg_ref[...] += part

        @pl.when(i == nsteps - 1)
        def _():
            dg_ref[...] = jnp.broadcast_to(jnp.sum(dg_ref[...], axis=0, keepdims=True), (8, d))

    row = pl.BlockSpec((tm, d), lambda i: (i, 0))
    return pl.pallas_call(
        body, name=name, grid=(nsteps,),
        in_specs=[row, row, pl.BlockSpec((1, d), lambda i: (0, 0)), row],
        out_specs=[row, row, pl.BlockSpec((8, d), lambda i: (0, 0))],
        out_shape=[jax.ShapeDtypeStruct((s, d), F32), jax.ShapeDtypeStruct((s, d), BF16),
                   jax.ShapeDtypeStruct((8, d), F32)],
        compiler_params=_cparams("arbitrary"),
    )(dh, x, gain, dres)


def _group_mean_matrix():
    r = lax.broadcasted_iota(jnp.int32, (LANES, LANES), 0) // HEAD_DIM
    c = lax.broadcasted_iota(jnp.int32, (LANES, LANES), 1) // HEAD_DIM
    return jnp.where(r == c, 1.0 / HEAD_DIM, 0.0).astype(BF16)


def _group_mean(v, gm):
    hi = v.astype(BF16)
    lo = (v - hi.astype(F32)).astype(BF16)
    return _dot(hi, gm, NN) + _dot(lo, gm, NN)


def _qknorm_fwd(proj, gains, tm, name):
    s = proj.shape[0]
    ncol = gains.shape[1] // LANES

    def body(p_ref, g_ref, gm_ref, o_ref):
        xv = p_ref[...].astype(F32)
        r = lax.rsqrt(_group_mean(xv * xv, gm_ref[...]) + EPS)
        o_ref[...] = ((xv * r) * g_ref[...]).astype(o_ref.dtype)

    blk = pl.BlockSpec((tm, LANES), lambda i, j: (i, j))
    return pl.pallas_call(
        body, name=name, grid=(s // tm, ncol),
        in_specs=[blk, pl.BlockSpec((1, LANES), lambda i, j: (0, j)),
                  pl.BlockSpec((LANES, LANES), lambda i, j: (0, 0))],
        out_specs=blk,
        out_shape=jax.ShapeDtypeStruct((s, ncol * LANES), BF16),
        compiler_params=_cparams("parallel", "parallel"),
    )(proj, gains, _group_mean_matrix())


def _qknorm_bwd(dqk, proj, gains, tm, name):
    s = proj.shape[0]
    ncol = gains.shape[1] // LANES
    nsteps = s // tm

    def body(dy_ref, p_ref, g_ref, gm_ref, dx_ref, dg_ref):
        i = pl.program_id(1)
        gm = gm_ref[...]
        xv = p_ref[...].astype(F32)
        r = lax.rsqrt(_group_mean(xv * xv, gm) + EPS)
        xhat = xv * r
        dy = dy_ref[...]
        dxh = dy * g_ref[...]
        proj_ = _group_mean(dxh * xhat, gm)
        dx_ref[...] = (r * (dxh - xhat * proj_)).astype(dx_ref.dtype)
        part = jnp.sum((dy * xhat).reshape(tm // 8, 8, LANES), axis=0)

        @pl.when(i == 0)
        def _():
            dg_ref[...] = part

        @pl.when(i > 0)
        def _():
            dg_ref[...] += part

        @pl.when(i == nsteps - 1)
        def _():
            dg_ref[...] = jnp.broadcast_to(jnp.sum(dg_ref[...], axis=0, keepdims=True), (8, LANES))

    blk = pl.BlockSpec((tm, LANES), lambda j, i: (i, j))
    return pl.pallas_call(
        body, name=name, grid=(ncol, nsteps),
        in_specs=[blk, blk, pl.BlockSpec((1, LANES), lambda j, i: (0, j)),
                  pl.BlockSpec((LANES, LANES), lambda j, i: (0, 0))],
        out_specs=[blk, pl.BlockSpec((8, LANES), lambda j, i: (0, j))],
        out_shape=[jax.ShapeDtypeStruct((s, ncol * LANES), BF16),
                   jax.ShapeDtypeStruct((8, ncol * LANES), F32)],
        compiler_params=_cparams("parallel", "arbitrary"),
    )(dqk, proj, gains, _group_mean_matrix())


CONV_ROWS = 256
HALO = 8


def _conv_fwd(proj, conv_w8, name):
    s = proj.shape[0]
    nblk = conv_w8.shape[1] // LANES
    first = 3 * nblk
    nchunk = s // CONV_ROWS

    def body(cb_ref, cc_ref, cu_ref, w_ref, y_ref, hpad):
        hpad[pl.ds(0, 2 * HALO), :] = jnp.zeros((2 * HALO, LANES), F32)

        def fill(i, _):
            r0 = pl.multiple_of(i * CONV_ROWS, CONV_ROWS)
            hpad[pl.ds(r0 + 2 * HALO, CONV_ROWS), :] = (
                cc_ref[pl.ds(r0, CONV_ROWS), :].astype(F32) * cu_ref[pl.ds(r0, CONV_ROWS), :].astype(F32))
            return 0

        lax.fori_loop(0, nchunk, fill, 0)
        w0, w1, w2 = w_ref[0:1, :], w_ref[1:2, :], w_ref[2:3, :]

        def conv(i, _):
            r0 = pl.multiple_of(i * CONV_ROWS, CONV_ROWS)
            win = hpad[pl.ds(r0 + HALO, CONV_ROWS + HALO), :]
            c = (w2 * win[HALO:] + w1 * pltpu.roll(win, 1, 0)[HALO:] + w0 * pltpu.roll(win, 2, 0)[HALO:])
            y_ref[pl.ds(r0, CONV_ROWS), :] = (cb_ref[pl.ds(r0, CONV_ROWS), :].astype(F32) * c).astype(y_ref.dtype)
            return 0

        lax.fori_loop(0, nchunk, conv, 0)

    def col(off):
        return pl.BlockSpec((s, LANES), lambda j: (0, off + j))

    return pl.pallas_call(
        body, name=name, grid=(nblk,),
        in_specs=[col(first), col(first + nblk), col(first + 2 * nblk), pl.BlockSpec((8, LANES), lambda j: (0, j))],
        out_specs=pl.BlockSpec((s, LANES), lambda j: (0, j)),
        out_shape=jax.ShapeDtypeStruct((s, nblk * LANES), BF16),
        scratch_shapes=[pltpu.VMEM((s + 2 * HALO, LANES), F32)],
        compiler_params=_cparams("parallel"),
    )(proj, proj, proj, conv_w8)


def _conv_bwd(dmix, proj, conv_w8, name):
    s = proj.shape[0]
    nblk = conv_w8.shape[1] // LANES
    first = 3 * nblk
    nchunk = s // CONV_ROWS

    def body(dy_ref, cb_ref, cc_ref, cu_ref, w_ref, dcb_ref, dcc_ref, dcu_ref, dw_ref, hpad, dcpad):
        hpad[pl.ds(0, 2 * HALO), :] = jnp.zeros((2 * HALO, LANES), F32)
        dcpad[pl.ds(s, 2 * HALO), :] = jnp.zeros((2 * HALO, LANES), F32)

        def fill(i, _):
            r0 = pl.multiple_of(i * CONV_ROWS, CONV_ROWS)
            hpad[pl.ds(r0 + 2 * HALO, CONV_ROWS), :] = (
                cc_ref[pl.ds(r0, CONV_ROWS), :].astype(F32) * cu_ref[pl.ds(r0, CONV_ROWS), :].astype(F32))
            return 0

        lax.fori_loop(0, nchunk, fill, 0)
        w0, w1, w2 = w_ref[0:1, :], w_ref[1:2, :], w_ref[2:3, :]

        def fold(v):
            return jnp.sum(v.reshape(CONV_ROWS // 8, 8, LANES), axis=0)

        def first_pass(i, acc):
            a0, a1, a2 = acc
            r0 = pl.multiple_of(i * CONV_ROWS, CONV_ROWS)
            win = hpad[pl.ds(r0 + HALO, CONV_ROWS + HALO), :]
            h0 = win[HALO:]
            h1 = pltpu.roll(win, 1, 0)[HALO:]
            h2 = pltpu.roll(win, 2, 0)[HALO:]
            c = w2 * h0 + w1 * h1 + w0 * h2
            dy = dy_ref[pl.ds(r0, CONV_ROWS), :]
            dcb_ref[pl.ds(r0, CONV_ROWS), :] = (dy * c).astype(dcb_ref.dtype)
            dc = dy * cb_ref[pl.ds(r0, CONV_ROWS), :].astype(F32)
            dcpad[pl.ds(r0, CONV_ROWS), :] = dc
            return a0 + fold(dc * h2), a1 + fold(dc * h1), a2 + fold(dc * h0)

        z8 = jnp.zeros((8, LANES), F32)
        a0, a1, a2 = lax.fori_loop(0, nchunk, first_pass, (z8, z8, z8))
        dw_ref[...] = jnp.concatenate(
            [jnp.sum(a0, axis=0, keepdims=True), jnp.sum(a1, axis=0, keepdims=True),
             jnp.sum(a2, axis=0, keepdims=True), jnp.zeros((5, LANES), F32)], axis=0)

        def second_pass(i, _):
            r0 = pl.multiple_of(i * CONV_ROWS, CONV_ROWS)
            win = dcpad[pl.ds(r0, CONV_ROWS + HALO), :]
            n = CONV_ROWS + HALO
            dh = (w2 * win[:CONV_ROWS] + w1 * pltpu.roll(win, n - 1, 0)[:CONV_ROWS]
                  + w0 * pltpu.roll(win, n - 2, 0)[:CONV_ROWS])
            dcc_ref[pl.ds(r0, CONV_ROWS), :] = (dh * cu_ref[pl.ds(r0, CONV_ROWS), :].astype(F32)).astype(dcc_ref.dtype)
            dcu_ref[pl.ds(r0, CONV_ROWS), :] = (dh * cc_ref[pl.ds(r0, CONV_ROWS), :].astype(F32)).astype(dcu_ref.dtype)
            return 0

        lax.fori_loop(0, nchunk, second_pass, 0)

    def col(off):
        return pl.BlockSpec((s, LANES), lambda j: (0, off + j))

    out = pl.BlockSpec((s, LANES), lambda j: (0, j))
    return pl.pallas_call(
        body, name=name, grid=(nblk,),
        in_specs=[col(nblk), col(first), col(first + nblk), col(first + 2 * nblk),
                  pl.BlockSpec((8, LANES), lambda j: (0, j))],
        out_specs=[out, out, out, pl.BlockSpec((8, LANES), lambda j: (0, j))],
        out_shape=[jax.ShapeDtypeStruct((s, nblk * LANES), BF16)] * 3 + [jax.ShapeDtypeStruct((8, nblk * LANES), F32)],
        scratch_shapes=[pltpu.VMEM((s + 2 * HALO, LANES), F32), pltpu.VMEM((s + 2 * HALO, LANES), F32)],
        compiler_params=_cparams("parallel"),
    )(dmix, proj, proj, proj, conv_w8)


LOG2E = 1.4426950408889634
LN2 = 0.6931471805599453
NEG_BIG = -1e30
SATURATED = 160.0


def _cumsum_matrix(kind):
    j = lax.broadcasted_iota(jnp.int32, (KEY_CHUNK, 2 * KEY_CHUNK), 0)
    c = lax.broadcasted_iota(jnp.int32, (KEY_CHUNK, 2 * KEY_CHUNK), 1)
    tri = {"after": j > c, "upto": j <= c, "before": j < c}[kind]
    return jnp.where((c >= KEY_CHUNK) | tri, 1.0, 0.0).astype(BF16)


def _stack_heads(t, m0):
    zero = jnp.zeros_like(t)
    return jnp.concatenate([jnp.where(m0, t, zero), jnp.where(m0, zero, t)], axis=0)


def _softplus2(z):
    sp = jnp.maximum(z, 0.0) + jnp.log2(1.0 + jnp.exp2(-jnp.abs(z)))
    return sp, z - sp


def _key_chunk(ref, kc):
    return ref[pl.ds(pl.multiple_of(kc * KEY_CHUNK, KEY_CHUNK), KEY_CHUNK), :]


def _attn_bwd(qk, proj, dmix, rtot, used, tq, name, travel=()):
    s = qk.shape[0]
    nhp = qk.shape[1] // (2 * LANES)
    nc = tq // KEY_CHUNK
    nq = s // tq
    nt = len(travel)

    def body(used_ref, q_ref, k_ref, v_ref, do_ref, r_ref, cmi_ref, cme_ref, bias_ref, dq_ref, dk_ref, dv_ref,
             z_refs, ls_refs, sig_refs, sp_refs, gb_refs, pr_ref, gs_ref, copies):
        qi = pl.program_id(1)

        @pl.when(qi == 0)
        def _():
            dk_ref[...] = jnp.zeros_like(dk_ref)
            dv_ref[...] = jnp.zeros_like(dv_ref)

        if copies is not None:
            @pl.when(jnp.logical_and(pl.program_id(0) == 0, qi == 0))
            def _():
                _exchange_begin(copies)

        nslots = (qi + 1) * nc
        walked = used_ref[pl.program_id(0), qi].astype(jnp.int32)
        first = jnp.clip(nslots - walked, 0, nslots - nc) // nc * nc
        m0 = lax.broadcasted_iota(jnp.int32, (1, LANES), 1) < HEAD_DIM
        qs = _stack_heads(q_ref[...], m0)
        do = do_ref[...]
        dos = _stack_heads(do.astype(BF16), m0)
        dosl = _stack_heads((do * LN2).astype(BF16), m0)
        cmi = cmi_ref[...]
        cme = cme_ref[...]

        def chunk_at(i):
            return jnp.clip(i, first, nslots - 1)

        def scores(kc):
            return _dot(qs, _key_chunk(k_ref, kc), NT)

        def weights(ls, cs, da, pr, kc):
            a = jnp.exp2(ls - (pr - cs[:, :KEY_CHUNK]))
            gb = (a * da).astype(BF16)
            ks = pl.multiple_of(kc * KEY_CHUNK, KEY_CHUNK)
            dv_ref[pl.ds(ks, KEY_CHUNK), :] += _dot(a, dos, TN)
            return gb, jnp.exp2(ls), pr - cs[:, KEY_CHUNK:]

        def score_grads(gb, sig, cg, gs, dq, kc):
            dzb = (gb.astype(F32) * (1.0 - sig) - sig * (gs + cg[:, :KEY_CHUNK])).astype(BF16)
            ks = pl.multiple_of(kc * KEY_CHUNK, KEY_CHUNK)
            dk_ref[pl.ds(ks, KEY_CHUNK), :] += _dot(dzb, qs, TN)
            dq = dq + _dot(jnp.concatenate([dzb[:tq], dzb[tq:]], axis=1), _stack_heads(_key_chunk(k_ref, kc), m0), NN)
            return gs + cg[:, KEY_CHUNK:], dq

        def step(i, par, bias=None):
            cur, prv = par, 1 - par
            k1, k2 = chunk_at(i - 1), chunk_at(i - 2)
            z_next = scores(chunk_at(i + 1))
            cs = _dot(sp_refs[prv][...], cmi, NN)
            da = _dot(dosl, _key_chunk(v_ref, k1), NT)
            cg = _dot(gb_refs[cur][...], cme, NN)
            z = z_refs[cur][...]
            if bias is not None:
                z = z + bias
            sp, ls = _softplus2(z)
            sp_refs[cur][...] = sp.astype(BF16)
            ls_refs[cur][...] = ls
            gs, dq = score_grads(gb_refs[cur][...], sig_refs[cur][...], cg, gs_ref[...], dq_ref[...], k2)
            gs_ref[...] = gs
            dq_ref[...] = dq
            gb, sig, pr = weights(ls_refs[prv][...], cs, da, pr_ref[...], k1)
            gb_refs[prv][...] = gb
            sig_refs[prv][...] = sig
            pr_ref[...] = pr
            z_refs[prv][...] = z_next

        pr_ref[...] = jnp.concatenate([r_ref[:, :LANES], r_ref[:, LANES:]], axis=0)
        gs_ref[...] = jnp.zeros((2 * tq, LANES), F32)
        dq_ref[...] = jnp.zeros((tq, LANES), F32)
        z_refs[0][...] = scores(first)
        sp_refs[1][...] = jnp.zeros((2 * tq, LANES), BF16)
        ls_refs[1][...] = jnp.full((2 * tq, LANES), NEG_BIG, F32)
        gb_refs[0][...] = jnp.zeros((2 * tq, LANES), BF16)
        sig_refs[0][...] = jnp.zeros((2 * tq, LANES), F32)

        def two_steps(j, _):
            step(2 * j, 0)
            step(2 * j + 1, 1)
            return 0

        lax.fori_loop(first // 2, nslots // 2 - 1, two_steps, 0)
        step(nslots - 2, 0, bias_ref[0])
        step(nslots - 1, 1, bias_ref[1])
        k1, k2 = chunk_at(nslots - 1), chunk_at(nslots - 2)
        gb, sig, _ = weights(ls_refs[1][...], _dot(sp_refs[1][...], cmi, NN),
                             _dot(dosl, _key_chunk(v_ref, k1), NT), pr_ref[...], k1)
        gb2 = gb_refs[0][...]
        gs, dq = score_grads(gb2, sig_refs[0][...], _dot(gb2, cme, NN), gs_ref[...], dq_ref[...], k2)
        _, dq = score_grads(gb, sig, _dot(gb, cme, NN), gs, dq, k1)
        dq_ref[...] = dq

        if copies is not None:
            @pl.when(jnp.logical_and(pl.program_id(0) == nhp - 1, qi == nq - 1))
            def _():
                _exchange_finish(copies)

    def wrapped(*refs):
        ins, rest = refs[:9], refs[9:]
        srcs, rest = rest[:nt], rest[nt:]
        outs, rest = rest[:3], rest[3:]
        lands, rest = rest[:nt], rest[nt:]
        z0, z1, ls0, ls1, sg0, sg1, sp0, sp1, gb0, gb1, pr_ref, gs_ref = rest[:12]
        copies = _exchange_copies(srcs, lands, *rest[12:]) if nt else None
        body(*ins, *outs, (z0, z1), (ls0, ls1), (sg0, sg1), (sp0, sp1), (gb0, gb1), pr_ref, gs_ref, copies)

    assert nc == 2
    bias = _diag_bias(tq, True)
    bias = jnp.concatenate([bias[:, :, :KEY_CHUNK], bias[:, :, KEY_CHUNK:]], axis=1)
    qblk = pl.BlockSpec((tq, LANES), lambda p, i: (i, p))
    full = pl.BlockSpec((s, LANES), lambda p, i: (0, p))
    cmspec = pl.BlockSpec((KEY_CHUNK, 2 * KEY_CHUNK), lambda p, i: (0, 0))
    anyspec = pl.BlockSpec(memory_space=pl.ANY)
    shape = jax.ShapeDtypeStruct((s, nhp * LANES), F32)
    f32buf = pltpu.VMEM((2 * tq, LANES), F32)
    bf16buf = pltpu.VMEM((2 * tq, LANES), BF16)
    outs = pl.pallas_call(
        wrapped, name=name, grid=(nhp, nq),
        in_specs=[pl.BlockSpec(memory_space=pltpu.SMEM),
                  qblk,
                  pl.BlockSpec((s, LANES), lambda p, i: (0, nhp + p)),
                  pl.BlockSpec((s, LANES), lambda p, i: (0, 2 * nhp + p)),
                  qblk,
                  pl.BlockSpec((tq, 2 * LANES), lambda p, i: (i, p)),
                  cmspec, cmspec,
                  pl.BlockSpec((nc, 2 * tq, LANES), lambda p, i: (0, 0, 0))] + [anyspec] * nt,
        out_specs=[qblk, full, full] + [anyspec] * nt,
        out_shape=[shape, shape, shape] + [jax.ShapeDtypeStruct(t.shape, t.dtype) for t in travel],
        scratch_shapes=[f32buf] * 6 + [bf16buf] * 4 + [f32buf] * 2 + (_exchange_scratch(nt) if nt else []),
        compiler_params=_cparams("arbitrary", "arbitrary"),
    )(used, qk, qk, proj, dmix, rtot, _cumsum_matrix("upto"), _cumsum_matrix("before"), bias, *travel)
    return outs[0], outs[1], outs[2], list(outs[3:])


def _pair_cumsum_matrix(kind):
    j = lax.broadcasted_iota(jnp.int32, (2 * KEY_CHUNK, 4 * KEY_CHUNK), 0)
    c = lax.broadcasted_iota(jnp.int32, (2 * KEY_CHUNK, 4 * KEY_CHUNK), 1)
    same_head = (j // KEY_CHUNK) == ((c // KEY_CHUNK) % 2)
    jj, cc = j % KEY_CHUNK, c % KEY_CHUNK
    tri = {"after": jj > cc, "upto": jj <= cc, "before": jj < cc}[kind]
    return jnp.where(same_head & ((c >= 2 * KEY_CHUNK) | tri), 1.0, 0.0).astype(BF16)


def _diag_bias(tq, ascending):
    nc = tq // KEY_CHUNK
    shape = (nc, tq, 2 * KEY_CHUNK)
    d = lax.broadcasted_iota(jnp.int32, shape, 0)
    r = lax.broadcasted_iota(jnp.int32, shape, 1)
    c = lax.broadcasted_iota(jnp.int32, shape, 2) % KEY_CHUNK
    chunk = d if ascending else nc - 1 - d
    return jnp.where(chunk * KEY_CHUNK + c < r, 0.0, NEG_BIG).astype(F32)


def _attn_fwd(qk, proj, tq, name):
    s = qk.shape[0]
    nhp = qk.shape[1] // (2 * LANES)
    nc = tq // KEY_CHUNK
    assert nc == 2
    w = 2 * KEY_CHUNK

    def body(q_ref, k_ref, v_ref, cm_ref, bias_ref, o_ref, r_ref, used_ref, z_refs, ls_refs, cs_refs, ct_refs,
             sp_refs, ab_refs, acc_ref):
        qi = pl.program_id(1)
        nslots = (qi + 1) * nc
        m0 = lax.broadcasted_iota(jnp.int32, (1, LANES), 1) < HEAD_DIM
        q = q_ref[...]
        cm = cm_ref[...]

        def chunk_at(i):
            return jnp.clip(nslots - 1 - i, 0, nslots - 1)

        def scores(kc):
            return _dot(q, _stack_heads(_key_chunk(k_ref, kc), m0), NT)

        def values(ab, kc):
            return _dot(ab, _stack_heads(_key_chunk(v_ref, kc), m0), NN)

        def step(i, par, bias=None, stages="zscwv"):
            cur, prv = par, 1 - par
            if "z" in stages:
                z_next = scores(chunk_at(i + 1))
            if "c" in stages:
                cs = _dot(sp_refs[prv][...], cm, NN)
            if "v" in stages:
                pv = values(ab_refs[prv][...], chunk_at(i - 3))
            if "w" in stages:
                rs = r_ref[...]
                r_ref[...] = rs + ct_refs[cur][...]
                ab_refs[cur][...] = jnp.exp2(ls_refs[cur][...] - cs_refs[cur][...] - rs).astype(BF16)
            if "s" in stages:
                z = z_refs[cur][...]
                if bias is not None:
                    z = z + bias
                sp, ls = _softplus2(z)
                sp_refs[cur][...] = sp.astype(BF16)
                ls_refs[cur][...] = ls
            if "v" in stages:
                acc_ref[...] += pv
            if "c" in stages:
                cs_refs[prv][...] = cs[:, :w]
                ct_refs[prv][...] = cs[:, w:]
            if "z" in stages:
                z_refs[prv][...] = z_next

        z_refs[0][...] = scores(chunk_at(0))
        for p in range(2):
            sp_refs[p][...] = jnp.zeros((tq, w), BF16)
            ls_refs[p][...] = jnp.full((tq, w), NEG_BIG, F32)
            cs_refs[p][...] = jnp.zeros((tq, w), F32)
            ct_refs[p][...] = jnp.zeros((tq, w), F32)
            ab_refs[p][...] = jnp.zeros((tq, w), BF16)
        r_ref[...] = jnp.zeros((tq, w), F32)
        acc_ref[...] = jnp.zeros((tq, LANES), F32)
        step(0, 0, bias_ref[0])
        step(1, 1, bias_ref[1])

        def two_steps(carry):
            j, _ = carry
            step(2 * j, 0)
            step(2 * j + 1, 1)
            return j + 1, jnp.min(r_ref[...])

        pairs, low = lax.while_loop(lambda c: jnp.logical_and(c[0] < nslots // 2, c[1] < SATURATED), two_steps,
                                    (jnp.int32(1), jnp.min(r_ref[...])))
        entered = 2 * pairs
        saturated = low >= SATURATED

        @pl.when(saturated)
        def _():
            step(entered, 0, stages="v")

        @pl.when(jnp.logical_not(saturated))
        def _():
            step(entered, 0, stages="cwv")
            step(entered + 1, 1, stages="wv")
            step(entered + 2, 0, stages="v")

        o_ref[...] = acc_ref[...].astype(o_ref.dtype)
        used_ref[pl.program_id(0), qi] = jnp.where(saturated, entered - 2, entered).astype(F32)

    def wrapped(q_ref, k_ref, v_ref, cm_ref, bias_ref, o_ref, r_ref, used_ref, *scratch):
        z, ls, cs, ct, sp, ab = [scratch[2 * j:2 * j + 2] for j in range(6)]
        body(q_ref, k_ref, v_ref, cm_ref, bias_ref, o_ref, r_ref, used_ref, z, ls, cs, ct, sp, ab, scratch[12])

    f32buf = pltpu.VMEM((tq, w), F32)
    bf16buf = pltpu.VMEM((tq, w), BF16)
    return pl.pallas_call(
        wrapped, name=name, grid=(nhp, s // tq),
        in_specs=[pl.BlockSpec((tq, LANES), lambda p, i: (i, p)),
                  pl.BlockSpec((s, LANES), lambda p, i: (0, nhp + p)),
                  pl.BlockSpec((s, LANES), lambda p, i: (0, 2 * nhp + p)),
                  pl.BlockSpec((w, 2 * w), lambda p, i: (0, 0)),
                  pl.BlockSpec((nc, tq, w), lambda p, i: (0, 0, 0))],
        out_specs=[pl.BlockSpec((tq, LANES), lambda p, i: (i, p)),
                   pl.BlockSpec((tq, w), lambda p, i: (i, p)),
                   pl.BlockSpec(memory_space=pltpu.SMEM)],
        out_shape=[jax.ShapeDtypeStruct((s, nhp * LANES), BF16),
                   jax.ShapeDtypeStruct((s, nhp * w), F32),
                   jax.ShapeDtypeStruct((nhp, s // tq), F32)],
        scratch_shapes=[f32buf] * 8 + [bf16buf] * 4 + [pltpu.VMEM((tq, LANES), F32)],
        compiler_params=_cparams("arbitrary", "arbitrary"),
    )(qk, qk, proj, _pair_cumsum_matrix("after"), _diag_bias(tq, False))


BLOCK_PAIR = 2


def _side_by_side(b_ref):
    return jnp.concatenate([b_ref[p] for p in range(BLOCK_PAIR)], axis=1)


def _mm_blocks(h, ga, widx, tm, name):
    s, d = h.shape
    nb, cols = ga.shape[1], ga.shape[3]

    def body(a_ref, b_ref, o_ref):
        o_ref[...] = _dot(a_ref[...], _side_by_side(b_ref), NN).astype(o_ref.dtype)

    return pl.pallas_call(
        body, name=name, grid=(s // tm, nb // BLOCK_PAIR),
        in_specs=[pl.BlockSpec((tm, d), lambda i, j: (i, 0)),
                  pl.BlockSpec((None, BLOCK_PAIR, d, cols), lambda i, j: (widx, j, 0, 0))],
        out_specs=pl.BlockSpec((tm, BLOCK_PAIR * cols), lambda i, j: (i, j)),
        out_shape=jax.ShapeDtypeStruct((s, nb * cols), BF16),
        compiler_params=_cparams("parallel", "arbitrary"),
    )(h, ga)


def _mm_swiglu(h, ga, gidx, uidx, tm, name):
    s, d = h.shape
    nb, cols = ga.shape[1], ga.shape[3]

    def body(a_ref, bg_ref, bu_ref, g_ref, u_ref, act_ref):
        a = a_ref[...]
        g = _dot(a, _side_by_side(bg_ref), NN)
        u = _dot(a, _side_by_side(bu_ref), NN)
        g_ref[...] = g.astype(g_ref.dtype)
        u_ref[...] = u.astype(u_ref.dtype)
        act_ref[...] = (g * (1.0 / (1.0 + jnp.exp(-g))) * u).astype(act_ref.dtype)

    def wspec(idx):
        return pl.BlockSpec((None, BLOCK_PAIR, d, cols), lambda i, j: (idx, j, 0, 0))

    out = pl.BlockSpec((tm, BLOCK_PAIR * cols), lambda i, j: (i, j))
    shape = jax.ShapeDtypeStruct((s, nb * cols), BF16)
    return pl.pallas_call(
        body, name=name, grid=(s // tm, nb // BLOCK_PAIR),
        in_specs=[pl.BlockSpec((tm, d), lambda i, j: (i, 0)), wspec(gidx), wspec(uidx)],
        out_specs=[out, out, out], out_shape=[shape, shape, shape],
        compiler_params=_cparams("parallel", "arbitrary"),
    )(h, ga, ga)


def _mm_residual(a, w3, lidx, res, tm, tn, name):
    s, k = a.shape
    n = w3.shape[2]

    def body(a_ref, b_ref, r_ref, o_ref):
        o_ref[...] = r_ref[...] + _dot(a_ref[...], b_ref[...], NN)

    return pl.pallas_call(
        body, name=name, grid=(s // tm, n // tn),
        in_specs=[pl.BlockSpec((tm, k), lambda i, j: (i, 0)),
                  pl.BlockSpec((None, k, tn), lambda i, j: (lidx, 0, j)),
                  pl.BlockSpec((tm, tn), lambda i, j: (i, j))],
        out_specs=pl.BlockSpec((tm, tn), lambda i, j: (i, j)),
        out_shape=jax.ShapeDtypeStruct((s, n), F32),
        compiler_params=_cparams("parallel", "arbitrary"),
    )(a, w3, res)


def _mm_nt(a, w3, lidx, tm, tn, name):
    s, k = a.shape
    n = w3.shape[1]

    def body(a_ref, b_ref, o_ref):
        o_ref[...] = _dot(a_ref[...], b_ref[...], NT)

    return pl.pallas_call(
        body, name=name, grid=(s // tm, n // tn),
        in_specs=[pl.BlockSpec((tm, k), lambda i, j: (i, 0)),
                  pl.BlockSpec((None, tn, k), lambda i, j: (lidx, j, 0))],
        out_specs=pl.BlockSpec((tm, tn), lambda i, j: (i, j)),
        out_shape=jax.ShapeDtypeStruct((s, n), F32),
        compiler_params=_cparams("parallel", "arbitrary"),
    )(a, w3)


def _mm_nt_swiglu_bwd(dx, wd3, lidx, g, u, tm, name):
    s, d = dx.shape
    cols = BLOCK_PAIR * (g.shape[1] // N_DEV)

    def body(a_ref, b_ref, g_ref, u_ref, dg_ref, du_ref):
        dact = _dot(a_ref[...], b_ref[...], NT)
        gv = g_ref[...].astype(F32)
        sig = 1.0 / (1.0 + jnp.exp(-gv))
        du_ref[...] = (dact * (gv * sig)).astype(du_ref.dtype)
        dg_ref[...] = (dact * u_ref[...].astype(F32) * (sig * (1.0 + gv * (1.0 - sig)))).astype(dg_ref.dtype)

    blk = pl.BlockSpec((tm, cols), lambda i, j: (i, j))
    shape = jax.ShapeDtypeStruct(g.shape, BF16)
    return pl.pallas_call(
        body, name=name, grid=(s // tm, N_DEV // BLOCK_PAIR),
        in_specs=[pl.BlockSpec((tm, d), lambda i, j: (i, 0)),
                  pl.BlockSpec((None, cols, d), lambda i, j: (lidx, j, 0)), blk, blk],
        out_specs=[blk, blk], out_shape=[shape, shape],
        compiler_params=_cparams("parallel", "arbitrary"),
    )(dx, wd3, g, u)


def _mm_nt_blocks(das, ga, widxs, tm, name):
    s = das[0].shape[0]
    nb, d, cols = ga.shape[1], ga.shape[2], ga.shape[3]
    nw = len(das)

    def body(*refs):
        a_refs, b_refs, o_ref = refs[:nw], refs[nw:2 * nw], refs[2 * nw]
        acc = None
        wide = BLOCK_PAIR * cols
        for w in range(nw):
            for k in range(nb // BLOCK_PAIR):
                b = jnp.concatenate([b_refs[w][BLOCK_PAIR * k + p] for p in range(BLOCK_PAIR)], axis=1)
                part = _dot(a_refs[w][:, k * wide:(k + 1) * wide], b, NT)
                acc = part if acc is None else acc + part
        o_ref[...] = acc

    def wspec(idx):
        return pl.BlockSpec((None, nb, d, cols), lambda i: (idx, 0, 0, 0), pipeline_mode=pl.Buffered(1))

    return pl.pallas_call(
        body, name=name, grid=(s // tm,),
        in_specs=[pl.BlockSpec((tm, nb * cols), lambda i: (i, 0))] * nw + [wspec(i) for i in widxs],
        out_specs=pl.BlockSpec((tm, d), lambda i: (i, 0)),
        out_shape=jax.ShapeDtypeStruct((s, d), F32),
        compiler_params=_cparams("parallel"),
    )(*das, *([ga] * nw))


def _mm_tn(a, b, ta, tb, tk, out_blocks, name):
    s, ka = a.shape
    nb = b.shape[1]
    nk = s // tk
    cols = tb
    if out_blocks:
        tb = BLOCK_PAIR * cols

    def body(a_ref, b_ref, o_ref, ob_ref):
        k = pl.program_id(2)
        part = _dot(a_ref[...], b_ref[...], TN)

        def put(first):
            if out_blocks:
                for p in range(BLOCK_PAIR):
                    piece = part[:, p * cols:(p + 1) * cols]
                    o_ref[p] = piece if first else o_ref[p] + piece
            else:
                o_ref[...] = part if first else o_ref[...] + part

        @pl.when(k == 0)
        def _():
            put(True)

        @pl.when(k > 0)
        def _():
            put(False)

        @pl.when(k == nk - 1)
        def _():
            ob_ref[...] = o_ref[...].astype(ob_ref.dtype)

    if out_blocks:
        out_spec = pl.BlockSpec((BLOCK_PAIR, ta, cols), lambda i, j, k: (j, i, 0))
        shape = (nb // cols, ka, cols)
    else:
        out_spec = pl.BlockSpec((ta, tb), lambda i, j, k: (i, j))
        shape = (ka, nb)
    return pl.pallas_call(
        body, name=name, grid=(ka // ta, nb // tb, nk),
        in_specs=[pl.BlockSpec((tk, ta), lambda i, j, k: (k, i)),
                  pl.BlockSpec((tk, tb), lambda i, j, k: (k, j))],
        out_specs=[out_spec, out_spec],
        out_shape=[jax.ShapeDtypeStruct(shape, F32), jax.ShapeDtypeStruct(shape, BF16)],
        compiler_params=_cparams("parallel", "parallel", "arbitrary"),
    )(a, b)


def _loss_head(y, target, tm, name):
    s, d = y.shape
    nsteps = s // tm

    def body(y_ref, t_ref, dy_ref, dyb_ref, l_ref, acc):
        i = pl.program_id(0)
        diff = y_ref[...] - t_ref[...]
        dy_ref[...] = diff * (1.0 / d)
        dyb_ref[...] = (diff * (1.0 / d)).astype(dyb_ref.dtype)
        part = jnp.sum((diff * diff).reshape(tm // 8, 8, d), axis=0)

        @pl.when(i == 0)
        def _():
            acc[...] = part

        @pl.when(i > 0)
        def _():
            acc[...] += part

        @pl.when(i == nsteps - 1)
        def _():
            tot = jnp.sum(jnp.sum(acc[...], axis=1, keepdims=True), axis=0, keepdims=True)
            l_ref[...] = jnp.broadcast_to(tot * (0.5 / d), (8, LANES))

    row = pl.BlockSpec((tm, d), lambda i: (i, 0))
    return pl.pallas_call(
        body, name=name, grid=(nsteps,),
        in_specs=[row, row],
        out_specs=[row, row, pl.BlockSpec((8, LANES), lambda i: (0, 0))],
        out_shape=[jax.ShapeDtypeStruct((s, d), F32), jax.ShapeDtypeStruct((s, d), BF16),
                   jax.ShapeDtypeStruct((8, LANES), F32)],
        scratch_shapes=[pltpu.VMEM((8, d), F32)],
        compiler_params=_cparams("arbitrary"),
    )(y, target)


def _adamw(parts, own, w, m, v, tr, name):
    p, rows, cols = parts.shape
    c1 = 1.0 / (1.0 - ADAM_B1 ** ADAM_STEP)
    c2 = 1.0 / (1.0 - ADAM_B2 ** ADAM_STEP)

    def body(*refs):
        if own is None:
            p_ref, w_ref, m_ref, v_ref, g_ref, d_ref, nm_ref, nv_ref = refs
            g = p_ref[0]
            for k in range(1, p):
                g = g + p_ref[k]
        else:
            p_ref, own_ref, w_ref, m_ref, v_ref, g_ref, d_ref, nm_ref, nv_ref = refs
            x, y, c = _place()
            my = 4 * x + 2 * y + c
            mine = own_ref[...]
            g = jnp.where(my == 0, mine, p_ref[0].astype(F32))
            for k in range(1, p):
                g = g + jnp.where(my == k, mine, p_ref[k].astype(F32))
        nm = ADAM_B1 * m_ref[...] + (1.0 - ADAM_B1) * g
        nv = ADAM_B2 * v_ref[...] + (1.0 - ADAM_B2) * (g * g)
        g_ref[...] = g
        nm_ref[...] = nm
        nv_ref[...] = nv
        d_ref[...] = -ADAM_LR * ((nm * c1) / (jnp.sqrt(nv * c2) + ADAM_EPS) + ADAM_WD * w_ref[...])

    blk = pl.BlockSpec((tr, cols), lambda i: (i, 0))
    shape = jax.ShapeDtypeStruct((rows, cols), F32)
    return pl.pallas_call(
        body, name=name, grid=(rows // tr,),
        in_specs=[pl.BlockSpec((p, tr, cols), lambda i: (0, i, 0))] + [blk] * (3 if own is None else 4),
        out_specs=[blk] * 4, out_shape=[shape] * 4,
        compiler_params=_cparams("parallel"),
    )(*([parts] + ([] if own is None else [own]) + [w, m, v]))


def _place():
    x, y, c = lax.axis_index("x"), lax.axis_index("y"), lax.axis_index("c")
    return x, y, c


def _all_gather(shards, name):
    na = len(shards)

    def body(*refs):
        srcs, dsts = refs[:na], refs[na:2 * na]
        send_sems, recv_sems, local_sems = refs[2 * na:]
        x, y, c = _place()
        me, sibling = (x, y, c), (x, y, 1 - c)
        chips = [(1 - x, y), (x, 1 - y), (1 - x, 1 - y)]

        def slot(a, dev):
            return dsts[a].at[:, pl.ds(4 * dev[0] + 2 * dev[1] + dev[2], 1)]

        def copy(k, a, block, to, from_shard=False):
            return pltpu.make_async_remote_copy(
                src_ref=srcs[a] if from_shard else slot(a, block), dst_ref=slot(a, block),
                send_sem=send_sems.at[k, a], recv_sem=recv_sems.at[k, a], device_id=to, device_id_type=MESH)

        mine = [pltpu.make_async_copy(srcs[a], slot(a, me), local_sems.at[a]) for a in range(na)]
        for cp in mine:
            cp.start()
        first = [copy(0, a, me, sibling, True) for a in range(na)]
        first += [copy(1 + j, a, me, (*chip, c), True) for j, chip in enumerate(chips) for a in range(na)]
        for cp in first:
            cp.start()
        passed = []
        for j, chip in enumerate(chips):
            for a in range(na):
                copy(1 + j, a, (*chip, c), me).wait_recv()
                fwd = copy(4 + j, a, (*chip, c), sibling)
                fwd.start()
                passed.append(fwd)
        for a in range(na):
            copy(0, a, sibling, me).wait_recv()
        for j, chip in enumerate(chips):
            for a in range(na):
                copy(4 + j, a, (*chip, 1 - c), me).wait_recv()
        for cp in first + passed:
            cp.wait_send()
        for cp in mine:
            cp.wait()

    anyspec = pl.BlockSpec(memory_space=pl.ANY)
    return pl.pallas_call(
        body, name=name,
        in_specs=[anyspec] * na, out_specs=[anyspec] * na,
        out_shape=[jax.ShapeDtypeStruct((a.shape[0], N_DEV) + a.shape[2:], a.dtype) for a in shards],
        scratch_shapes=[pltpu.SemaphoreType.DMA((7, na)), pltpu.SemaphoreType.DMA((7, na)),
                        pltpu.SemaphoreType.DMA((na,))],
    )(*shards)


_RELATIONS = [(dx, dy, dc) for dx in (0, 1) for dy in (0, 1) for dc in (0, 1)][1:]


def _flip(v, d):
    return 1 - v if d else v


def _exchange_copies(srcs, dsts, send_sems, recv_sems, local_sems):
    x, y, c = _place()
    my = 4 * x + 2 * y + c
    na = len(srcs)
    mine = [pltpu.make_async_copy(srcs[a].at[pl.ds(my, 1)], dsts[a].at[pl.ds(my, 1)], local_sems.at[a])
            for a in range(na)]
    sends, recvs = [], []
    for k, (dx, dy, dc) in enumerate(_RELATIONS):
        peer = (_flip(x, dx), _flip(y, dy), _flip(c, dc))
        pidx = 4 * peer[0] + 2 * peer[1] + peer[2]
        for a in range(na):
            for into, out in ((my, sends), (pidx, recvs)):
                out.append(pltpu.make_async_remote_copy(
                    src_ref=srcs[a].at[pl.ds(pidx, 1)], dst_ref=dsts[a].at[pl.ds(into, 1)],
                    send_sem=send_sems.at[k, a], recv_sem=recv_sems.at[k, a], device_id=peer, device_id_type=MESH))
    return mine, sends, recvs


def _exchange_begin(copies):
    mine, sends, _ = copies
    for cp in mine + sends:
        cp.start()


def _exchange_finish(copies):
    mine, sends, recvs = copies
    for cp in recvs:
        cp.wait_recv()
    for cp in sends:
        cp.wait_send()
    for cp in mine:
        cp.wait()


def _exchange_scratch(na):
    return [pltpu.SemaphoreType.DMA((7, na)), pltpu.SemaphoreType.DMA((7, na)), pltpu.SemaphoreType.DMA((na,))]


def _exchange_blocks(grads, name):
    na = len(grads)

    def body(*refs):
        copies = _exchange_copies(refs[:na], refs[na:2 * na], *refs[2 * na:])
        _exchange_begin(copies)
        _exchange_finish(copies)

    anyspec = pl.BlockSpec(memory_space=pl.ANY)
    return pl.pallas_call(
        body, name=name,
        in_specs=[anyspec] * na, out_specs=[anyspec] * na,
        out_shape=[jax.ShapeDtypeStruct(a.shape, a.dtype) for a in grads],
        scratch_shapes=_exchange_scratch(na),
    )(*grads)


def _all_reduce_small(v, name):
    r, c_ = v.shape

    def body(v_ref, o_ref, gath, send_sems, recv_sems):
        x, y, c = _place()
        my = 4 * x + 2 * y + c
        gath[my] = v_ref[...]
        sends = []
        for k, (dx, dy, dc) in enumerate(_RELATIONS):
            peer = (_flip(x, dx), _flip(y, dy), _flip(c, dc))
            cp = pltpu.make_async_remote_copy(
                src_ref=v_ref, dst_ref=gath.at[my], send_sem=send_sems.at[k], recv_sem=recv_sems.at[k],
                device_id=peer, device_id_type=MESH)
            cp.start()
            sends.append((cp, 4 * peer[0] + 2 * peer[1] + peer[2], k, peer))
        for cp, pidx, k, peer in sends:
            pltpu.make_async_remote_copy(
                src_ref=v_ref, dst_ref=gath.at[pidx], send_sem=send_sems.at[k], recv_sem=recv_sems.at[k],
                device_id=peer, device_id_type=MESH).wait_recv()
        for cp, *_ in sends:
            cp.wait_send()
        tot = gath[0]
        for k in range(1, N_DEV):
            tot = tot + gath[k]
        o_ref[...] = tot

    vm = pl.BlockSpec(memory_space=pltpu.VMEM)
    return pl.pallas_call(
        body, name=name, in_specs=[vm], out_specs=vm,
        out_shape=jax.ShapeDtypeStruct((r, c_), F32),
        scratch_shapes=[pltpu.VMEM((N_DEV, r, c_), F32), pltpu.SemaphoreType.DMA((7,)),
                        pltpu.SemaphoreType.DMA((7,))],
    )(v)


TM = 512
TM_MATMUL = 2048
TM_RESIDUAL = 1024
TQ = 256


def _pad_to(a, axis, size):
    pad = [(0, 0)] * a.ndim
    pad[axis] = (0, size - a.shape[axis])
    return jnp.pad(a, pad)


def _local_step(x, target, ga, gb, gc, conv_full, norm_mix, q_norm, k_norm, norm_ffn):
    depth = gb.shape[0]
    tm, tq = min(TM, x.shape[0]), min(TQ, x.shape[0])
    tmm, tmr = min(TM_MATMUL, x.shape[0]), min(TM_RESIDUAL, x.shape[0])
    attn = gb.shape[1] // 2
    nheads = attn // HEAD_DIM
    scale = HEAD_DIM ** -0.5 * LOG2E
    saved = []
    for l in range(depth):
        h1 = _rmsnorm_fwd(x, norm_mix[l][None], tm,f"norm_mix_fwd_{l}")
        proj = _mm_blocks(h1, ga, 3 * l, tmm, f"proj_in_{l}")
        qk_gain = jnp.concatenate([jnp.tile(q_norm[l], nheads) * scale, jnp.tile(k_norm[l], nheads)])[None]
        qk = _qknorm_fwd(proj, qk_gain, tmm, f"qknorm_fwd_{l}")
        o, rtot, used = _attn_fwd(qk, proj, tq, f"attn_fwd_{l}")
        conv_w8 = _pad_to(conv_full[l], 0, 8)
        cv = _conv_fwd(proj, conv_w8, f"conv_fwd_{l}")
        mix = jnp.concatenate([o, cv], axis=1)
        x1 = _mm_residual(mix, gb, l, x, tmr, 512, f"proj_out_{l}")
        h2 = _rmsnorm_fwd(x1, norm_ffn[l][None], tm,f"norm_ffn_fwd_{l}")
        g, u, act = _mm_swiglu(h2, ga, 3 * l + 1, 3 * l + 2, tmr, f"ffn_up_{l}")
        x2 = _mm_residual(act, gc, l, x1, tmr, 512, f"ffn_down_{l}")
        saved.append((x, h1, proj, qk_gain, qk, rtot, used, conv_w8, mix, x1, h2, g, u, act))
        x = x2

    dx, dxb, loss = _loss_head(x, target, tm, "loss_head")

    grads = [None] * depth
    small = [None] * depth
    landed = {}
    for l in reversed(range(depth)):
        x0, h1, proj, qk_gain, qk, rtot, used, conv_w8, mix, x1, h2, g, u, act = saved[l]
        d = x0.shape[1]
        dg, du = _mm_nt_swiglu_bwd(dxb, gc, l, g, u, tmr, f"ffn_down_bwd_{l}")
        d_wdown = _mm_tn(act, dxb, 768, d, tmm, False, f"dw_down_{l}")
        d_wgate = _mm_tn(h2, dg, d, ga.shape[3], tmm, True, f"dw_gate_{l}")
        d_wup = _mm_tn(h2, du, d, ga.shape[3], tmm, True, f"dw_up_{l}")
        dh2 = _mm_nt_blocks([dg, du], ga, [3 * l + 1, 3 * l + 2], tm, f"ffn_up_bwd_{l}")
        dx1, dx1b, dg_ffn = _rmsnorm_bwd(dh2, x1, norm_ffn[l][None], dx, tm, f"norm_ffn_bwd_{l}")
        dmix = _mm_nt(dx1b, gb, l, tmr, 512, f"proj_out_bwd_{l}")
        d_wout = _mm_tn(mix, dx1b, 512, d, tmm, False, f"dw_out_{l}")
        dcb, dcc, dcu, dconv = _conv_bwd(dmix, proj, conv_w8, f"conv_bwd_{l}")
        travel = {}
        if l == 0:
            travel = {5 * ll + j: grads[ll][j][1] for ll in range(1, depth) for j in range(5)}
            travel.update({1: d_wgate[1], 2: d_wup[1], 3: d_wout[1], 4: d_wdown[1]})
        dq, dk, dv, arrived = _attn_bwd(qk, proj, dmix, rtot, used, tq, f"attn_bwd_{l}",
                                        [t.reshape(N_DEV, -1, t.shape[-1]) for t in travel.values()])
        landed.update(zip(travel.keys(), arrived))
        dqk, dg_qk = _qknorm_bwd(jnp.concatenate([dq, dk], axis=1), proj, qk_gain, tmm, f"qknorm_bwd_{l}")
        dproj = jnp.concatenate([dqk, dv.astype(BF16), dcb, dcc, dcu], axis=1)
        d_win = _mm_tn(h1, dproj, d, ga.shape[3], tmm, True, f"dw_in_{l}")
        dh1 = _mm_nt_blocks([dproj], ga, [3 * l], tmr, f"proj_in_bwd_{l}")
        dx, dxb, dg_mix = _rmsnorm_bwd(dh1, x0, norm_mix[l][None], dx1, tm, f"norm_mix_bwd_{l}")
        grads[l] = (d_win, d_wgate, d_wup, d_wout, d_wdown)
        dq_gain = jnp.sum(dg_qk[0, :attn].reshape(nheads, HEAD_DIM), axis=0) * scale
        dk_gain = jnp.sum(dg_qk[0, attn:].reshape(nheads, HEAD_DIM), axis=0)
        small[l] = (dg_mix[0], dg_ffn[0], dq_gain, dk_gain, dconv[:3])
    return loss, dx, grads, small, landed


def kernel(x, norm_mix, w_in, q_norm, k_norm, conv_w, w_out, norm_ffn, w_gate, w_up, w_down, loss_target, m_norm_mix, m_w_in, m_q_norm, m_k_norm, m_conv_w, m_w_out, m_norm_ffn, m_w_gate, m_w_up, m_w_down, v_norm_mix, v_w_in, v_q_norm, v_k_norm, v_conv_w, v_w_out, v_norm_ffn, v_w_gate, v_w_up, v_w_down):
    depth, d, in_shard = w_in.shape
    ff_shard = w_gate.shape[2]
    ff_pad = in_shard
    conv_shard = conv_w.shape[2]
    xs = x.reshape(x.shape[-2], d)
    target = loss_target.reshape(xs.shape)

    pa = jnp.stack([w_in, _pad_to(w_gate, 2, ff_pad), _pad_to(w_up, 2, ff_pad)], axis=1)
    pa = pa.reshape(3 * depth, 1, d, in_shard).astype(BF16)
    pb = w_out.astype(BF16)[:, None]
    pc = _pad_to(w_down, 1, ff_pad).astype(BF16)[:, None]
    pd = _pad_to(_pad_to(conv_w.reshape(depth * 3, conv_shard), 0, 8), 1, LANES)[None, None]
    ga, gb, gc, gd = _all_gather([pa, pb, pc, pd], "gather_weights")
    gb = gb.reshape(depth, N_DEV * gb.shape[2], d)
    gc = gc.reshape(depth, N_DEV * ff_pad, d)
    conv_full = gd[0, :, :depth * 3, :conv_shard].transpose(1, 0, 2).reshape(depth, 3, N_DEV * conv_shard)

    loss, grad_x, grads, small, landed = _local_step(xs, target, ga, gb, gc, conv_full, norm_mix, q_norm, k_norm,
                                                     norm_ffn)

    x_, y_, c_ = _place()
    my = 4 * x_ + 2 * y_ + c_

    def blocks(t):
        return t.reshape(N_DEV, -1, t.shape[-1])

    rest = [i for i in range(5 * depth) if i not in landed]
    landed.update(zip(rest, _exchange_blocks([blocks(grads[i // 5][i % 5][1]) for i in rest], "exchange_grads")))
    landed = [landed[i] for i in range(5 * depth)]
    own = [lax.dynamic_index_in_dim(blocks(grads[i // 5][i % 5][0]), my, 0, keepdims=False)
           for i in range(5 * depth)]

    nconv = N_DEV * conv_shard
    rows = []
    for l in range(depth):
        g_mix, g_ffn, g_q, g_k, g_conv = small[l]
        qkrow = _pad_to(jnp.concatenate([g_q, g_k]), 0, d)
        rows += [g_mix[None], g_ffn[None], qkrow[None], _pad_to(g_conv, 1, d)]
    nrow = 6 * depth
    packed = jnp.concatenate(rows + [_pad_to(loss[:1], 1, d)], axis=0)
    packed = _pad_to(packed, 0, ((nrow + 1 + 7) // 8) * 8)
    summed = _all_reduce_small(packed, "reduce_small")
    loss_out = summed[nrow, 0]

    def big(i, w, m, v, tr, name, rows_=None, cols_=None):
        parts = landed[i]
        pr, pcn = parts.shape[1], parts.shape[2]
        w, m, v = [_pad_to(_pad_to(t, 0, pr), 1, pcn) for t in (w, m, v)]
        outs = _adamw(parts, own[i], w, m, v, tr, name)
        return [o[:rows_ or pr, :cols_ or pcn] for o in outs]

    res = {}
    for l in range(depth):
        i = 5 * l
        res[("w_in", l)] = big(i, w_in[l], m_w_in[l], v_w_in[l], 256, f"adamw_in_{l}")
        res[("w_gate", l)] = big(i + 1, w_gate[l], m_w_gate[l], v_w_gate[l], 256, f"adamw_gate_{l}", cols_=ff_shard)
        res[("w_up", l)] = big(i + 2, w_up[l], m_w_up[l], v_w_up[l], 256, f"adamw_up_{l}", cols_=ff_shard)
        res[("w_out", l)] = big(i + 3, w_out[l], m_w_out[l], v_w_out[l], w_out.shape[1], f"adamw_out_{l}")
        res[("w_down", l)] = big(i + 4, w_down[l], m_w_down[l], v_w_down[l], 128, f"adamw_down_{l}",
                                 rows_=ff_shard)

    g_rows, w_rows, m_rows, v_rows = [], [], [], []
    for l in range(depth):
        base = l * 6
        conv_g = lax.dynamic_slice(summed[base + 3:base + 6], (0, my * conv_shard), (3, conv_shard))
        g_rows += [summed[base:base + 3], _pad_to(conv_g, 1, d)]
        for dst, (nm, qn, kn, nf, cw) in ((w_rows, (norm_mix, q_norm, k_norm, norm_ffn, conv_w)),
                                          (m_rows, (m_norm_mix, m_q_norm, m_k_norm, m_norm_ffn, m_conv_w)),
                                          (v_rows, (v_norm_mix, v_q_norm, v_k_norm, v_norm_ffn, v_conv_w))):
            dst += [nm[l][None], nf[l][None], _pad_to(jnp.concatenate([qn[l], kn[l]]), 0, d)[None],
                    _pad_to(cw[l], 1, d)]
    prow = ((nrow + 7) // 8) * 8
    gs, ws, ms, vs = [_pad_to(jnp.concatenate(t, axis=0), 0, prow) for t in (g_rows, w_rows, m_rows, v_rows)]
    sm = _adamw(gs[None], None, ws, ms, vs, prow, "adamw_small")

    hd = q_norm.shape[1]

    def small_out(t, kind):
        per_layer = []
        for l in range(depth):
            base = l * 6
            per_layer.append({"norm_mix": t[base], "norm_ffn": t[base + 1], "q_norm": t[base + 2, :hd],
                              "k_norm": t[base + 2, hd:2 * hd], "conv_w": t[base + 3:base + 6, :conv_shard]}[kind])
        return jnp.stack(per_layer)

    def big_out(name, i):
        return jnp.stack([res[(name, l)][i] for l in range(depth)])

    outs = [loss_out, grad_x.reshape(x.shape)]
    for i in range(4):
        outs += [small_out(sm[i], "norm_mix"), big_out("w_in", i), small_out(sm[i], "q_norm"),
                 small_out(sm[i], "k_norm"), small_out(sm[i], "conv_w"), big_out("w_out", i),
                 small_out(sm[i], "norm_ffn"), big_out("w_gate", i), big_out("w_up", i), big_out("w_down", i)]
    return tuple(outs)
```

```python
import jax
import jax.numpy as jnp
from jax import lax
from jax.experimental import pallas as pl
from jax.experimental.pallas import tpu as pltpu

F32 = jnp.float32
BF16 = jnp.bfloat16
MESH = pl.DeviceIdType.MESH

N_DEV = 8
LANES = 128
HEAD_DIM = 64
KEY_CHUNK = 128
EPS = 1e-6
VMEM_LIMIT = 48 * 1024 * 1024

ADAM_LR = 0.001
ADAM_B1 = 0.9
ADAM_B2 = 0.999
ADAM_EPS = 1e-08
ADAM_WD = 0.01
ADAM_STEP = 10

NN = (((1,), (0,)), ((), ()))
NT = (((1,), (1,)), ((), ()))
TN = (((0,), (0,)), ((), ()))


def _dot(a, b, dims):
    return lax.dot_general(a.astype(BF16), b.astype(BF16), dims, preferred_element_type=F32)


def _cparams(*sem):
    return pltpu.CompilerParams(dimension_semantics=sem, vmem_limit_bytes=VMEM_LIMIT)


def _split_hi_lo(v):
    hi = v.astype(BF16)
    lo = (v - hi.astype(F32)).astype(BF16)
    return jnp.concatenate([hi, lo], axis=1)


def _rmsnorm_fwd(x, gain, tm, name):
    s, d = x.shape

    def body(x_ref, g_ref, o_ref):
        xv = x_ref[...]
        r = lax.rsqrt(jnp.mean(xv * xv, axis=-1, keepdims=True) + EPS)
        o_ref[...] = ((xv * r) * g_ref[...]).astype(o_ref.dtype)

    return pl.pallas_call(
        body, name=name, grid=(s // tm,),
        in_specs=[pl.BlockSpec((tm, d), lambda i: (i, 0)), pl.BlockSpec((1, d), lambda i: (0, 0))],
        out_specs=pl.BlockSpec((tm, d), lambda i: (i, 0)),
        out_shape=jax.ShapeDtypeStruct((s, d), BF16),
        compiler_params=_cparams("parallel"),
    )(x, gain)


def _rmsnorm_bwd(dh, x, gain, dres, tm, name):
    s, d = x.shape
    nsteps = s // tm

    def body(dh_ref, x_ref, g_ref, dres_ref, dx_ref, dxb_ref, dg_ref):
        i = pl.program_id(0)
        xv = x_ref[...]
        r = lax.rsqrt(jnp.mean(xv * xv, axis=-1, keepdims=True) + EPS)
        xhat = xv * r
        dhv = dh_ref[...]
        dxh = dhv * g_ref[...]
        proj = jnp.mean(dxh * xhat, axis=-1, keepdims=True)
        dxv = dres_ref[...] + r * (dxh - xhat * proj)
        dx_ref[...] = dxv
        dxb_ref[...] = dxv.astype(dxb_ref.dtype)
        part = jnp.sum((dhv * xhat).reshape(tm // 8, 8, d), axis=0)

        @pl.when(i == 0)
        def _():
            dg_ref[...] = part

        @pl.when(i > 0)
        def _():
            dg_ref[...] += part

        @pl.when(i == nsteps - 1)
        def _():
            dg_ref[...] = jnp.broadcast_to(jnp.sum(dg_ref[...], axis=0, keepdims=True), (8, d))

    row = pl.BlockSpec((tm, d), lambda i: (i, 0))
    return pl.pallas_call(
        body, name=name, grid=(nsteps,),
        in_specs=[row, row, pl.BlockSpec((1, d), lambda i: (0, 0)), row],
        out_specs=[row, row, pl.BlockSpec((8, d), lambda i: (0, 0))],
        out_shape=[jax.ShapeDtypeStruct((s, d), F32), jax.ShapeDtypeStruct((s, d), BF16),
                   jax.ShapeDtypeStruct((8, d), F32)],
        compiler_params=_cparams("arbitrary"),
    )(dh, x, gain, dres)


def _group_mean_matrix():
    r = lax.broadcasted_iota(jnp.int32, (LANES, LANES), 0) // HEAD_DIM
    c = lax.broadcasted_iota(jnp.int32, (LANES, LANES), 1) // HEAD_DIM
    return jnp.where(r == c, 1.0 / HEAD_DIM, 0.0).astype(BF16)


def _group_mean(v, gm):
    hi = v.astype(BF16)
    lo = (v - hi.astype(F32)).astype(BF16)
    return _dot(hi, gm, NN) + _dot(lo, gm, NN)


def _qknorm_fwd(proj, gains, tm, name):
    s = proj.shape[0]
    ncol = gains.shape[1] // LANES

    def body(p_ref, g_ref, gm_ref, o_ref):
        xv = p_ref[...].astype(F32)
        r = lax.rsqrt(_group_mean(xv * xv, gm_ref[...]) + EPS)
        o_ref[...] = ((xv * r) * g_ref[...]).astype(o_ref.dtype)

    blk = pl.BlockSpec((tm, LANES), lambda i, j: (i, j))
    return pl.pallas_call(
        body, name=name, grid=(s // tm, ncol),
        in_specs=[blk, pl.BlockSpec((1, LANES), lambda i, j: (0, j)),
                  pl.BlockSpec((LANES, LANES), lambda i, j: (0, 0))],
        out_specs=blk,
        out_shape=jax.ShapeDtypeStruct((s, ncol * LANES), BF16),
        compiler_params=_cparams("parallel", "parallel"),
    )(proj, gains, _group_mean_matrix())


def _qknorm_bwd(dqk, proj, gains, tm, name):
    s = proj.shape[0]
    ncol = gains.shape[1] // LANES
    nsteps = s // tm

    def body(dy_ref, p_ref, g_ref, gm_ref, dx_ref, dg_ref):
        i = pl.program_id(1)
        gm = gm_ref[...]
        xv = p_ref[...].astype(F32)
        r = lax.rsqrt(_group_mean(xv * xv, gm) + EPS)
        xhat = xv * r
        dy = dy_ref[...]
        dxh = dy * g_ref[...]
        proj_ = _group_mean(dxh * xhat, gm)
        dx_ref[...] = (r * (dxh - xhat * proj_)).astype(dx_ref.dtype)
        part = jnp.sum((dy * xhat).reshape(tm // 8, 8, LANES), axis=0)

        @pl.when(i == 0)
        def _():
            dg_ref[...] = part

        @pl.when(i > 0)
        def _():
            dg_ref[...] += part

        @pl.when(i == nsteps - 1)
        def _():
            dg_ref[...] = jnp.broadcast_to(jnp.sum(dg_ref[...], axis=0, keepdims=True), (8, LANES))

    blk = pl.BlockSpec((tm, LANES), lambda j, i: (i, j))
    return pl.pallas_call(
        body, name=name, grid=(ncol, nsteps),
        in_specs=[blk, blk, pl.BlockSpec((1, LANES), lambda j, i: (0, j)),
                  pl.BlockSpec((LANES, LANES), lambda j, i: (0, 0))],
        out_specs=[blk, pl.BlockSpec((8, LANES), lambda j, i: (0, j))],
        out_shape=[jax.ShapeDtypeStruct((s, ncol * LANES), BF16),
                   jax.ShapeDtypeStruct((8, ncol * LANES), F32)],
        compiler_params=_cparams("parallel", "arbitrary"),
    )(dqk, proj, gains, _group_mean_matrix())


CONV_ROWS = 256
HALO = 8


def _conv_fwd(proj, conv_w8, name):
    s = proj.shape[0]
    nblk = conv_w8.shape[1] // LANES
    first = 3 * nblk
    nchunk = s // CONV_ROWS

    def body(cb_ref, cc_ref, cu_ref, w_ref, y_ref, hpad):
        hpad[pl.ds(0, 2 * HALO), :] = jnp.zeros((2 * HALO, LANES), F32)

        def fill(i, _):
            r0 = pl.multiple_of(i * CONV_ROWS, CONV_ROWS)
            hpad[pl.ds(r0 + 2 * HALO, CONV_ROWS), :] = (
                cc_ref[pl.ds(r0, CONV_ROWS), :].astype(F32) * cu_ref[pl.ds(r0, CONV_ROWS), :].astype(F32))
            return 0

        lax.fori_loop(0, nchunk, fill, 0)
        w0, w1, w2 = w_ref[0:1, :], w_ref[1:2, :], w_ref[2:3, :]

        def conv(i, _):
            r0 = pl.multiple_of(i * CONV_ROWS, CONV_ROWS)
            win = hpad[pl.ds(r0 + HALO, CONV_ROWS + HALO), :]
            c = (w2 * win[HALO:] + w1 * pltpu.roll(win, 1, 0)[HALO:] + w0 * pltpu.roll(win, 2, 0)[HALO:])
            y_ref[pl.ds(r0, CONV_ROWS), :] = (cb_ref[pl.ds(r0, CONV_ROWS), :].astype(F32) * c).astype(y_ref.dtype)
            return 0

        lax.fori_loop(0, nchunk, conv, 0)

    def col(off):
        return pl.BlockSpec((s, LANES), lambda j: (0, off + j))

    return pl.pallas_call(
        body, name=name, grid=(nblk,),
        in_specs=[col(first), col(first + nblk), col(first + 2 * nblk), pl.BlockSpec((8, LANES), lambda j: (0, j))],
        out_specs=pl.BlockSpec((s, LANES), lambda j: (0, j)),
        out_shape=jax.ShapeDtypeStruct((s, nblk * LANES), BF16),
        scratch_shapes=[pltpu.VMEM((s + 2 * HALO, LANES), F32)],
        compiler_params=_cparams("parallel"),
    )(proj, proj, proj, conv_w8)


def _conv_bwd(dmix, proj, conv_w8, name):
    s = proj.shape[0]
    nblk = conv_w8.shape[1] // LANES
    first = 3 * nblk
    nchunk = s // CONV_ROWS

    def body(dy_ref, cb_ref, cc_ref, cu_ref, w_ref, dcb_ref, dcc_ref, dcu_ref, dw_ref, hpad, dcpad):
        hpad[pl.ds(0, 2 * HALO), :] = jnp.zeros((2 * HALO, LANES), F32)
        dcpad[pl.ds(s, 2 * HALO), :] = jnp.zeros((2 * HALO, LANES), F32)

        def fill(i, _):
            r0 = pl.multiple_of(i * CONV_ROWS, CONV_ROWS)
            hpad[pl.ds(r0 + 2 * HALO, CONV_ROWS), :] = (
                cc_ref[pl.ds(r0, CONV_ROWS), :].astype(F32) * cu_ref[pl.ds(r0, CONV_ROWS), :].astype(F32))
            return 0

        lax.fori_loop(0, nchunk, fill, 0)
        w0, w1, w2 = w_ref[0:1, :], w_ref[1:2, :], w_ref[2:3, :]

        def fold(v):
            return jnp.sum(v.reshape(CONV_ROWS // 8, 8, LANES), axis=0)

        def first_pass(i, acc):
            a0, a1, a2 = acc
            r0 = pl.multiple_of(i * CONV_ROWS, CONV_ROWS)
            win = hpad[pl.ds(r0 + HALO, CONV_ROWS + HALO), :]
            h0 = win[HALO:]
            h1 = pltpu.roll(win, 1, 0)[HALO:]
            h2 = pltpu.roll(win, 2, 0)[HALO:]
            c = w2 * h0 + w1 * h1 + w0 * h2
            dy = dy_ref[pl.ds(r0, CONV_ROWS), :]
            dcb_ref[pl.ds(r0, CONV_ROWS), :] = (dy * c).astype(dcb_ref.dtype)
            dc = dy * cb_ref[pl.ds(r0, CONV_ROWS), :].astype(F32)
            dcpad[pl.ds(r0, CONV_ROWS), :] = dc
            return a0 + fold(dc * h2), a1 + fold(dc * h1), a2 + fold(dc * h0)

        z8 = jnp.zeros((8, LANES), F32)
        a0, a1, a2 = lax.fori_loop(0, nchunk, first_pass, (z8, z8, z8))
        dw_ref[...] = jnp.concatenate(
            [jnp.sum(a0, axis=0, keepdims=True), jnp.sum(a1, axis=0, keepdims=True),
             jnp.sum(a2, axis=0, keepdims=True), jnp.zeros((5, LANES), F32)], axis=0)

        def second_pass(i, _):
            r0 = pl.multiple_of(i * CONV_ROWS, CONV_ROWS)
            win = dcpad[pl.ds(r0, CONV_ROWS + HALO), :]
            n = CONV_ROWS + HALO
            dh = (w2 * win[:CONV_ROWS] + w1 * pltpu.roll(win, n - 1, 0)[:CONV_ROWS]
                  + w0 * pltpu.roll(win, n - 2, 0)[:CONV_ROWS])
            dcc_ref[pl.ds(r0, CONV_ROWS), :] = (dh * cu_ref[pl.ds(r0, CONV_ROWS), :].astype(F32)).astype(dcc_ref.dtype)
            dcu_ref[pl.ds(r0, CONV_ROWS), :] = (dh * cc_ref[pl.ds(r0, CONV_ROWS), :].astype(F32)).astype(dcu_ref.dtype)
            return 0

        lax.fori_loop(0, nchunk, second_pass, 0)

    def col(off):
        return pl.BlockSpec((s, LANES), lambda j: (0, off + j))

    out = pl.BlockSpec((s, LANES), lambda j: (0, j))
    return pl.pallas_call(
        body, name=name, grid=(nblk,),
        in_specs=[col(nblk), col(first), col(first + nblk), col(first + 2 * nblk),
                  pl.BlockSpec((8, LANES), lambda j: (0, j))],
        out_specs=[out, out, out, pl.BlockSpec((8, LANES), lambda j: (0, j))],
        out_shape=[jax.ShapeDtypeStruct((s, nblk * LANES), BF16)] * 3 + [jax.ShapeDtypeStruct((8, nblk * LANES), F32)],
        scratch_shapes=[pltpu.VMEM((s + 2 * HALO, LANES), F32), pltpu.VMEM((s + 2 * HALO, LANES), F32)],
        compiler_params=_cparams("parallel"),
    )(dmix, proj, proj, proj, conv_w8)


LOG2E = 1.4426950408889634
LN2 = 0.6931471805599453
NEG_BIG = -1e30
SATURATED = 160.0


def _cumsum_matrix(kind):
    j = lax.broadcasted_iota(jnp.int32, (KEY_CHUNK, 2 * KEY_CHUNK), 0)
    c = lax.broadcasted_iota(jnp.int32, (KEY_CHUNK, 2 * KEY_CHUNK), 1)
    tri = {"after": j > c, "upto": j <= c, "before": j < c}[kind]
    return jnp.where((c >= KEY_CHUNK) | tri, 1.0, 0.0).astype(BF16)


def _stack_heads(t, m0):
    zero = jnp.zeros_like(t)
    return jnp.concatenate([jnp.where(m0, t, zero), jnp.where(m0, zero, t)], axis=0)


def _softplus2(z):
    sp = jnp.maximum(z, 0.0) + jnp.log2(1.0 + jnp.exp2(-jnp.abs(z)))
    return sp, z - sp


def _key_chunk(ref, kc):
    return ref[pl.ds(pl.multiple_of(kc * KEY_CHUNK, KEY_CHUNK), KEY_CHUNK), :]


def _attn_bwd(qk, proj, dmix, rtot, used, tq, name, travel=()):
    s = qk.shape[0]
    nhp = qk.shape[1] // (2 * LANES)
    nc = tq // KEY_CHUNK
    nq = s // tq
    nt = len(travel)

    def body(used_ref, q_ref, k_ref, v_ref, do_ref, r_ref, cmi_ref, cme_ref, bias_ref, dq_ref, dk_ref, dv_ref,
             z_refs, ls_refs, sig_refs, sp_refs, gb_refs, pr_ref, gs_ref, copies):
        qi = pl.program_id(1)

        @pl.when(qi == 0)
        def _():
            dk_ref[...] = jnp.zeros_like(dk_ref)
            dv_ref[...] = jnp.zeros_like(dv_ref)

        if copies is not None:
            @pl.when(jnp.logical_and(pl.program_id(0) == 0, qi == 0))
            def _():
                _exchange_begin(copies)

        nslots = (qi + 1) * nc
        walked = used_ref[pl.program_id(0), qi].astype(jnp.int32)
        first = jnp.clip(nslots - walked, 0, nslots - nc) // nc * nc
        m0 = lax.broadcasted_iota(jnp.int32, (1, LANES), 1) < HEAD_DIM
        qs = _stack_heads(q_ref[...], m0)
        do = do_ref[...]
        dos = _stack_heads(do.astype(BF16), m0)
        dosl = _stack_heads((do * LN2).astype(BF16), m0)
        cmi = cmi_ref[...]
        cme = cme_ref[...]

        def chunk_at(i):
            return jnp.clip(i, first, nslots - 1)

        def scores(kc):
            return _dot(qs, _key_chunk(k_ref, kc), NT)

        def weights(ls, cs, da, pr, kc):
            a = jnp.exp2(ls - (pr - cs[:, :KEY_CHUNK]))
            gb = (a * da).astype(BF16)
            ks = pl.multiple_of(kc * KEY_CHUNK, KEY_CHUNK)
            dv_ref[pl.ds(ks, KEY_CHUNK), :] += _dot(a, dos, TN)
            return gb, jnp.exp2(ls), pr - cs[:, KEY_CHUNK:]

        def score_grads(gb, sig, cg, gs, dq, kc):
            dzb = (gb.astype(F32) * (1.0 - sig) - sig * (gs + cg[:, :KEY_CHUNK])).astype(BF16)
            ks = pl.multiple_of(kc * KEY_CHUNK, KEY_CHUNK)
            dk_ref[pl.ds(ks, KEY_CHUNK), :] += _dot(dzb, qs, TN)
            dq = dq + _dot(jnp.concatenate([dzb[:tq], dzb[tq:]], axis=1), _stack_heads(_key_chunk(k_ref, kc), m0), NN)
            return gs + cg[:, KEY_CHUNK:], dq

        def step(i, par, bias=None):
            cur, prv = par, 1 - par
            k1, k2 = chunk_at(i - 1), chunk_at(i - 2)
            z_next = scores(chunk_at(i + 1))
            cs = _dot(sp_refs[prv][...], cmi, NN)
            da = _dot(dosl, _key_chunk(v_ref, k1), NT)
            cg = _dot(gb_refs[cur][...], cme, NN)
            z = z_refs[cur][...]
            if bias is not None:
                z = z + bias
            sp, ls = _softplus2(z)
            sp_refs[cur][...] = sp.astype(BF16)
            ls_refs[cur][...] = ls
            gs, dq = score_grads(gb_refs[cur][...], sig_refs[cur][...], cg, gs_ref[...], dq_ref[...], k2)
            gs_ref[...] = gs
            dq_ref[...] = dq
            gb, sig, pr = weights(ls_refs[prv][...], cs, da, pr_ref[...], k1)
            gb_refs[prv][...] = gb
            sig_refs[prv][...] = sig
            pr_ref[...] = pr
            z_refs[prv][...] = z_next

        pr_ref[...] = jnp.concatenate([r_ref[:, :LANES], r_ref[:, LANES:]], axis=0)
        gs_ref[...] = jnp.zeros((2 * tq, LANES), F32)
        dq_ref[...] = jnp.zeros((tq, LANES), F32)
        z_refs[0][...] = scores(first)
        sp_refs[1][...] = jnp.zeros((2 * tq, LANES), BF16)
        ls_refs[1][...] = jnp.full((2 * tq, LANES), NEG_BIG, F32)
        gb_refs[0][...] = jnp.zeros((2 * tq, LANES), BF16)
        sig_refs[0][...] = jnp.zeros((2 * tq, LANES), F32)

        def two_steps(j, _):
            step(2 * j, 0)
            step(2 * j + 1, 1)
            return 0

        lax.fori_loop(first // 2, nslots // 2 - 1, two_steps, 0)
        step(nslots - 2, 0, bias_ref[0])
        step(nslots - 1, 1, bias_ref[1])
        k1, k2 = chunk_at(nslots - 1), chunk_at(nslots - 2)
        gb, sig, _ = weights(ls_refs[1][...], _dot(sp_refs[1][...], cmi, NN),
                             _dot(dosl, _key_chunk(v_ref, k1), NT), pr_ref[...], k1)
        gb2 = gb_refs[0][...]
        gs, dq = score_grads(gb2, sig_refs[0][...], _dot(gb2, cme, NN), gs_ref[...], dq_ref[...], k2)
        _, dq = score_grads(gb, sig, _dot(gb, cme, NN), gs, dq, k1)
        dq_ref[...] = dq

        if copies is not None:
            @pl.when(jnp.logical_and(pl.program_id(0) == nhp - 1, qi == nq - 1))
            def _():
                _exchange_finish(copies)

    def wrapped(*refs):
        ins, rest = refs[:9], refs[9:]
        srcs, rest = rest[:nt], rest[nt:]
        outs, rest = rest[:3], rest[3:]
        lands, rest = rest[:nt], rest[nt:]
        z0, z1, ls0, ls1, sg0, sg1, sp0, sp1, gb0, gb1, pr_ref, gs_ref = rest[:12]
        copies = _exchange_copies(srcs, lands, *rest[12:]) if nt else None
        body(*ins, *outs, (z0, z1), (ls0, ls1), (sg0, sg1), (sp0, sp1), (gb0, gb1), pr_ref, gs_ref, copies)

    assert nc == 2
    bias = _diag_bias(tq, True)
    bias = jnp.concatenate([bias[:, :, :KEY_CHUNK], bias[:, :, KEY_CHUNK:]], axis=1)
    qblk = pl.BlockSpec((tq, LANES), lambda p, i: (i, p))
    full = pl.BlockSpec((s, LANES), lambda p, i: (0, p))
    cmspec = pl.BlockSpec((KEY_CHUNK, 2 * KEY_CHUNK), lambda p, i: (0, 0))
    anyspec = pl.BlockSpec(memory_space=pl.ANY)
    shape = jax.ShapeDtypeStruct((s, nhp * LANES), F32)
    f32buf = pltpu.VMEM((2 * tq, LANES), F32)
    bf16buf = pltpu.VMEM((2 * tq, LANES), BF16)
    outs = pl.pallas_call(
        wrapped, name=name, grid=(nhp, nq),
        in_specs=[pl.BlockSpec(memory_space=pltpu.SMEM),
                  qblk,
                  pl.BlockSpec((s, LANES), lambda p, i: (0, nhp + p)),
                  pl.BlockSpec((s, LANES), lambda p, i: (0, 2 * nhp + p)),
                  qblk,
                  pl.BlockSpec((tq, 2 * LANES), lambda p, i: (i, p)),
                  cmspec, cmspec,
                  pl.BlockSpec((nc, 2 * tq, LANES), lambda p, i: (0, 0, 0))] + [anyspec] * nt,
        out_specs=[qblk, full, full] + [anyspec] * nt,
        out_shape=[shape, shape, shape] + [jax.ShapeDtypeStruct(t.shape, t.dtype) for t in travel],
        scratch_shapes=[f32buf] * 6 + [bf16buf] * 4 + [f32buf] * 2 + (_exchange_scratch(nt) if nt else []),
        compiler_params=_cparams("arbitrary", "arbitrary"),
    )(used, qk, qk, proj, dmix, rtot, _cumsum_matrix("upto"), _cumsum_matrix("before"), bias, *travel)
    return outs[0], outs[1], outs[2], list(outs[3:])


def _pair_cumsum_matrix(kind):
    j = lax.broadcasted_iota(jnp.int32, (2 * KEY_CHUNK, 4 * KEY_CHUNK), 0)
    c = lax.broadcasted_iota(jnp.int32, (2 * KEY_CHUNK, 4 * KEY_CHUNK), 1)
    same_head = (j // KEY_CHUNK) == ((c // KEY_CHUNK) % 2)
    jj, cc = j % KEY_CHUNK, c % KEY_CHUNK
    tri = {"after": jj > cc, "upto": jj <= cc, "before": jj < cc}[kind]
    return jnp.where(same_head & ((c >= 2 * KEY_CHUNK) | tri), 1.0, 0.0).astype(BF16)


def _diag_bias(tq, ascending):
    nc = tq // KEY_CHUNK
    shape = (nc, tq, 2 * KEY_CHUNK)
    d = lax.broadcasted_iota(jnp.int32, shape, 0)
    r = lax.broadcasted_iota(jnp.int32, shape, 1)
    c = lax.broadcasted_iota(jnp.int32, shape, 2) % KEY_CHUNK
    chunk = d if ascending else nc - 1 - d
    return jnp.where(chunk * KEY_CHUNK + c < r, 0.0, NEG_BIG).astype(F32)


def _attn_fwd(qk, proj, tq, name, shards=()):
    s = qk.shape[0]
    nhp = qk.shape[1] // (2 * LANES)
    nc = tq // KEY_CHUNK
    nq = s // tq
    ng = len(shards)
    assert nc == 2
    w = 2 * KEY_CHUNK

    def body(q_ref, k_ref, v_ref, cm_ref, bias_ref, o_ref, r_ref, used_ref, z_refs, ls_refs, cs_refs, ct_refs,
             sp_refs, ab_refs, acc_ref, gather):
        qi = pl.program_id(1)
        if gather is not None:
            @pl.when(jnp.logical_and(pl.program_id(0) == 0, qi == 0))
            def _():
                gather.begin()

        nslots = (qi + 1) * nc
        m0 = lax.broadcasted_iota(jnp.int32, (1, LANES), 1) < HEAD_DIM
        q = q_ref[...]
        cm = cm_ref[...]

        def chunk_at(i):
            return jnp.clip(nslots - 1 - i, 0, nslots - 1)

        def scores(kc):
            return _dot(q, _stack_heads(_key_chunk(k_ref, kc), m0), NT)

        def values(ab, kc):
            return _dot(ab, _stack_heads(_key_chunk(v_ref, kc), m0), NN)

        def step(i, par, bias=None, stages="zscwv"):
            cur, prv = par, 1 - par
            if "z" in stages:
                z_next = scores(chunk_at(i + 1))
            if "c" in stages:
                cs = _dot(sp_refs[prv][...], cm, NN)
            if "v" in stages:
                pv = values(ab_refs[prv][...], chunk_at(i - 3))
            if "w" in stages:
                rs = r_ref[...]
                r_ref[...] = rs + ct_refs[cur][...]
                ab_refs[cur][...] = jnp.exp2(ls_refs[cur][...] - cs_refs[cur][...] - rs).astype(BF16)
            if "s" in stages:
                z = z_refs[cur][...]
                if bias is not None:
                    z = z + bias
                sp, ls = _softplus2(z)
                sp_refs[cur][...] = sp.astype(BF16)
                ls_refs[cur][...] = ls
            if "v" in stages:
                acc_ref[...] += pv
            if "c" in stages:
                cs_refs[prv][...] = cs[:, :w]
                ct_refs[prv][...] = cs[:, w:]
            if "z" in stages:
                z_refs[prv][...] = z_next

        z_refs[0][...] = scores(chunk_at(0))
        for p in range(2):
            sp_refs[p][...] = jnp.zeros((tq, w), BF16)
            ls_refs[p][...] = jnp.full((tq, w), NEG_BIG, F32)
            cs_refs[p][...] = jnp.zeros((tq, w), F32)
            ct_refs[p][...] = jnp.zeros((tq, w), F32)
            ab_refs[p][...] = jnp.zeros((tq, w), BF16)
        r_ref[...] = jnp.zeros((tq, w), F32)
        acc_ref[...] = jnp.zeros((tq, LANES), F32)
        step(0, 0, bias_ref[0])
        step(1, 1, bias_ref[1])

        def two_steps(carry):
            j, _ = carry
            step(2 * j, 0)
            step(2 * j + 1, 1)
            return j + 1, jnp.min(r_ref[...])

        pairs, low = lax.while_loop(lambda c: jnp.logical_and(c[0] < nslots // 2, c[1] < SATURATED), two_steps,
                                    (jnp.int32(1), jnp.min(r_ref[...])))
        entered = 2 * pairs
        saturated = low >= SATURATED

        @pl.when(saturated)
        def _():
            step(entered, 0, stages="v")

        @pl.when(jnp.logical_not(saturated))
        def _():
            step(entered, 0, stages="cwv")
            step(entered + 1, 1, stages="wv")
            step(entered + 2, 0, stages="v")

        o_ref[...] = acc_ref[...].astype(o_ref.dtype)
        used_ref[pl.program_id(0), qi] = jnp.where(saturated, entered - 2, entered).astype(F32)

        if gather is not None:
            @pl.when(jnp.logical_and(pl.program_id(0) == nhp // 2, qi == 0))
            def _():
                gather.relay()

            @pl.when(jnp.logical_and(pl.program_id(0) == nhp - 1, qi == nq - 1))
            def _():
                gather.finish()

    def wrapped(*refs):
        ins, rest = refs[:5], refs[5:]
        srcs, rest = rest[:ng], rest[ng:]
        outs, rest = rest[:3], rest[3:]
        dsts, scratch = rest[:ng], rest[ng:]
        z, ls, cs, ct, sp, ab = [scratch[2 * j:2 * j + 2] for j in range(6)]
        gather = _Gather(srcs, dsts, *scratch[13:]) if ng else None
        body(*ins, *outs, z, ls, cs, ct, sp, ab, scratch[12], gather)

    f32buf = pltpu.VMEM((tq, w), F32)
    bf16buf = pltpu.VMEM((tq, w), BF16)
    anyspec = pl.BlockSpec(memory_space=pl.ANY)
    outs = pl.pallas_call(
        wrapped, name=name, grid=(nhp, nq),
        in_specs=[pl.BlockSpec((tq, LANES), lambda p, i: (i, p)),
                  pl.BlockSpec((s, LANES), lambda p, i: (0, nhp + p)),
                  pl.BlockSpec((s, LANES), lambda p, i: (0, 2 * nhp + p)),
                  pl.BlockSpec((w, 2 * w), lambda p, i: (0, 0)),
                  pl.BlockSpec((nc, tq, w), lambda p, i: (0, 0, 0))] + [anyspec] * ng,
        out_specs=[pl.BlockSpec((tq, LANES), lambda p, i: (i, p)),
                   pl.BlockSpec((tq, w), lambda p, i: (i, p)),
                   pl.BlockSpec(memory_space=pltpu.SMEM)] + [anyspec] * ng,
        out_shape=[jax.ShapeDtypeStruct((s, nhp * LANES), BF16),
                   jax.ShapeDtypeStruct((s, nhp * w), F32),
                   jax.ShapeDtypeStruct((nhp, nq), F32)] + _gathered_shapes(shards),
        scratch_shapes=([f32buf] * 8 + [bf16buf] * 4 + [pltpu.VMEM((tq, LANES), F32)]
                        + (_gather_scratch(ng) if ng else [])),
        compiler_params=_cparams("arbitrary", "arbitrary"),
    )(qk, qk, proj, _pair_cumsum_matrix("after"), _diag_bias(tq, False), *shards)
    return outs[0], outs[1], outs[2], list(outs[3:])


BLOCK_PAIR = 2


def _side_by_side(b_ref):
    return jnp.concatenate([b_ref[p] for p in range(BLOCK_PAIR)], axis=1)


def _mm_blocks(h, ga, widx, tm, name):
    s, d = h.shape
    nb, cols = ga.shape[1], ga.shape[3]

    def body(a_ref, b_ref, o_ref):
        o_ref[...] = _dot(a_ref[...], _side_by_side(b_ref), NN).astype(o_ref.dtype)

    return pl.pallas_call(
        body, name=name, grid=(s // tm, nb // BLOCK_PAIR),
        in_specs=[pl.BlockSpec((tm, d), lambda i, j: (i, 0)),
                  pl.BlockSpec((None, BLOCK_PAIR, d, cols), lambda i, j: (widx, j, 0, 0))],
        out_specs=pl.BlockSpec((tm, BLOCK_PAIR * cols), lambda i, j: (i, j)),
        out_shape=jax.ShapeDtypeStruct((s, nb * cols), BF16),
        compiler_params=_cparams("parallel", "arbitrary"),
    )(h, ga)


def _mm_swiglu(h, ga, gidx, uidx, tm, name):
    s, d = h.shape
    nb, cols = ga.shape[1], ga.shape[3]

    def body(a_ref, bg_ref, bu_ref, g_ref, u_ref, act_ref):
        a = a_ref[...]
        g = _dot(a, _side_by_side(bg_ref), NN)
        u = _dot(a, _side_by_side(bu_ref), NN)
        g_ref[...] = g.astype(g_ref.dtype)
        u_ref[...] = u.astype(u_ref.dtype)
        act_ref[...] = (g * (1.0 / (1.0 + jnp.exp(-g))) * u).astype(act_ref.dtype)

    def wspec(idx):
        return pl.BlockSpec((None, BLOCK_PAIR, d, cols), lambda i, j: (idx, j, 0, 0))

    out = pl.BlockSpec((tm, BLOCK_PAIR * cols), lambda i, j: (i, j))
    shape = jax.ShapeDtypeStruct((s, nb * cols), BF16)
    return pl.pallas_call(
        body, name=name, grid=(s // tm, nb // BLOCK_PAIR),
        in_specs=[pl.BlockSpec((tm, d), lambda i, j: (i, 0)), wspec(gidx), wspec(uidx)],
        out_specs=[out, out, out], out_shape=[shape, shape, shape],
        compiler_params=_cparams("parallel", "arbitrary"),
    )(h, ga, ga)


def _mm_residual(a, w3, lidx, res, tm, tn, name):
    s, k = a.shape
    n = w3.shape[2]

    def body(a_ref, b_ref, r_ref, o_ref):
        o_ref[...] = r_ref[...] + _dot(a_ref[...], b_ref[...], NN)

    return pl.pallas_call(
        body, name=name, grid=(s // tm, n // tn),
        in_specs=[pl.BlockSpec((tm, k), lambda i, j: (i, 0)),
                  pl.BlockSpec((None, k, tn), lambda i, j: (lidx, 0, j)),
                  pl.BlockSpec((tm, tn), lambda i, j: (i, j))],
        out_specs=pl.BlockSpec((tm, tn), lambda i, j: (i, j)),
        out_shape=jax.ShapeDtypeStruct((s, n), F32),
        compiler_params=_cparams("parallel", "arbitrary"),
    )(a, w3, res)


def _mm_nt(a, w3, lidx, tm, tn, name):
    s, k = a.shape
    n = w3.shape[1]

    def body(a_ref, b_ref, o_ref):
        o_ref[...] = _dot(a_ref[...], b_ref[...], NT)

    return pl.pallas_call(
        body, name=name, grid=(s // tm, n // tn),
        in_specs=[pl.BlockSpec((tm, k), lambda i, j: (i, 0)),
                  pl.BlockSpec((None, tn, k), lambda i, j: (lidx, j, 0))],
        out_specs=pl.BlockSpec((tm, tn), lambda i, j: (i, j)),
        out_shape=jax.ShapeDtypeStruct((s, n), F32),
        compiler_params=_cparams("parallel", "arbitrary"),
    )(a, w3)


def _mm_nt_swiglu_bwd(dx, wd3, lidx, g, u, tm, name):
    s, d = dx.shape
    cols = BLOCK_PAIR * (g.shape[1] // N_DEV)

    def body(a_ref, b_ref, g_ref, u_ref, dg_ref, du_ref):
        dact = _dot(a_ref[...], b_ref[...], NT)
        gv = g_ref[...].astype(F32)
        sig = 1.0 / (1.0 + jnp.exp(-gv))
        du_ref[...] = (dact * (gv * sig)).astype(du_ref.dtype)
        dg_ref[...] = (dact * u_ref[...].astype(F32) * (sig * (1.0 + gv * (1.0 - sig)))).astype(dg_ref.dtype)

    blk = pl.BlockSpec((tm, cols), lambda i, j: (i, j))
    shape = jax.ShapeDtypeStruct(g.shape, BF16)
    return pl.pallas_call(
        body, name=name, grid=(s // tm, N_DEV // BLOCK_PAIR),
        in_specs=[pl.BlockSpec((tm, d), lambda i, j: (i, 0)),
                  pl.BlockSpec((None, cols, d), lambda i, j: (lidx, j, 0)), blk, blk],
        out_specs=[blk, blk], out_shape=[shape, shape],
        compiler_params=_cparams("parallel", "arbitrary"),
    )(dx, wd3, g, u)


def _mm_nt_blocks(das, ga, widxs, tm, name):
    s = das[0].shape[0]
    nb, d, cols = ga.shape[1], ga.shape[2], ga.shape[3]
    nw = len(das)

    def body(*refs):
        a_refs, b_refs, o_ref = refs[:nw], refs[nw:2 * nw], refs[2 * nw]
        acc = None
        wide = BLOCK_PAIR * cols
        for w in range(nw):
            for k in range(nb // BLOCK_PAIR):
                b = jnp.concatenate([b_refs[w][BLOCK_PAIR * k + p] for p in range(BLOCK_PAIR)], axis=1)
                part = _dot(a_refs[w][:, k * wide:(k + 1) * wide], b, NT)
                acc = part if acc is None else acc + part
        o_ref[...] = acc

    def wspec(idx):
        return pl.BlockSpec((None, nb, d, cols), lambda i: (idx, 0, 0, 0), pipeline_mode=pl.Buffered(1))

    return pl.pallas_call(
        body, name=name, grid=(s // tm,),
        in_specs=[pl.BlockSpec((tm, nb * cols), lambda i: (i, 0))] * nw + [wspec(i) for i in widxs],
        out_specs=pl.BlockSpec((tm, d), lambda i: (i, 0)),
        out_shape=jax.ShapeDtypeStruct((s, d), F32),
        compiler_params=_cparams("parallel"),
    )(*das, *([ga] * nw))


def _mm_tn(a, b, ta, tb, tk, out_blocks, name):
    s, ka = a.shape
    nb = b.shape[1]
    nk = s // tk
    cols = tb
    if out_blocks:
        tb = BLOCK_PAIR * cols

    def body(a_ref, b_ref, o_ref, ob_ref):
        k = pl.program_id(2)
        part = _dot(a_ref[...], b_ref[...], TN)

        def put(first):
            if out_blocks:
                for p in range(BLOCK_PAIR):
                    piece = part[:, p * cols:(p + 1) * cols]
                    o_ref[p] = piece if first else o_ref[p] + piece
            else:
                o_ref[...] = part if first else o_ref[...] + part

        @pl.when(k == 0)
        def _():
            put(True)

        @pl.when(k > 0)
        def _():
            put(False)

        @pl.when(k == nk - 1)
        def _():
            ob_ref[...] = o_ref[...].astype(ob_ref.dtype)

    if out_blocks:
        out_spec = pl.BlockSpec((BLOCK_PAIR, ta, cols), lambda i, j, k: (j, i, 0))
        shape = (nb // cols, ka, cols)
    else:
        out_spec = pl.BlockSpec((ta, tb), lambda i, j, k: (i, j))
        shape = (ka, nb)
    return pl.pallas_call(
        body, name=name, grid=(ka // ta, nb // tb, nk),
        in_specs=[pl.BlockSpec((tk, ta), lambda i, j, k: (k, i)),
                  pl.BlockSpec((tk, tb), lambda i, j, k: (k, j))],
        out_specs=[out_spec, out_spec],
        out_shape=[jax.ShapeDtypeStruct(shape, F32), jax.ShapeDtypeStruct(shape, BF16)],
        compiler_params=_cparams("parallel", "parallel", "arbitrary"),
    )(a, b)


def _loss_head(y, target, tm, name):
    s, d = y.shape
    nsteps = s // tm

    def body(y_ref, t_ref, dy_ref, dyb_ref, l_ref, acc):
        i = pl.program_id(0)
        diff = y_ref[...] - t_ref[...]
        dy_ref[...] = diff * (1.0 / d)
        dyb_ref[...] = (diff * (1.0 / d)).astype(dyb_ref.dtype)
        part = jnp.sum((diff * diff).reshape(tm // 8, 8, d), axis=0)

        @pl.when(i == 0)
        def _():
            acc[...] = part

        @pl.when(i > 0)
        def _():
            acc[...] += part

        @pl.when(i == nsteps - 1)
        def _():
            tot = jnp.sum(jnp.sum(acc[...], axis=1, keepdims=True), axis=0, keepdims=True)
            l_ref[...] = jnp.broadcast_to(tot * (0.5 / d), (8, LANES))

    row = pl.BlockSpec((tm, d), lambda i: (i, 0))
    return pl.pallas_call(
        body, name=name, grid=(nsteps,),
        in_specs=[row, row],
        out_specs=[row, row, pl.BlockSpec((8, LANES), lambda i: (0, 0))],
        out_shape=[jax.ShapeDtypeStruct((s, d), F32), jax.ShapeDtypeStruct((s, d), BF16),
                   jax.ShapeDtypeStruct((8, LANES), F32)],
        scratch_shapes=[pltpu.VMEM((8, d), F32)],
        compiler_params=_cparams("arbitrary"),
    )(y, target)


def _adamw(parts, own, w, m, v, tr, name):
    p, rows, cols = parts.shape
    c1 = 1.0 / (1.0 - ADAM_B1 ** ADAM_STEP)
    c2 = 1.0 / (1.0 - ADAM_B2 ** ADAM_STEP)

    def body(*refs):
        if own is None:
            p_ref, w_ref, m_ref, v_ref, g_ref, d_ref, nm_ref, nv_ref = refs
            g = p_ref[0]
            for k in range(1, p):
                g = g + p_ref[k]
        else:
            p_ref, own_ref, w_ref, m_ref, v_ref, g_ref, d_ref, nm_ref, nv_ref = refs
            x, y, c = _place()
            my = 4 * x + 2 * y + c
            mine = own_ref[...]
            g = jnp.where(my == 0, mine, p_ref[0].astype(F32))
            for k in range(1, p):
                g = g + jnp.where(my == k, mine, p_ref[k].astype(F32))
        nm = ADAM_B1 * m_ref[...] + (1.0 - ADAM_B1) * g
        nv = ADAM_B2 * v_ref[...] + (1.0 - ADAM_B2) * (g * g)
        g_ref[...] = g
        nm_ref[...] = nm
        nv_ref[...] = nv
        d_ref[...] = -ADAM_LR * ((nm * c1) / (jnp.sqrt(nv * c2) + ADAM_EPS) + ADAM_WD * w_ref[...])

    blk = pl.BlockSpec((tr, cols), lambda i: (i, 0))
    shape = jax.ShapeDtypeStruct((rows, cols), F32)
    return pl.pallas_call(
        body, name=name, grid=(rows // tr,),
        in_specs=[pl.BlockSpec((p, tr, cols), lambda i: (0, i, 0))] + [blk] * (3 if own is None else 4),
        out_specs=[blk] * 4, out_shape=[shape] * 4,
        compiler_params=_cparams("parallel"),
    )(*([parts] + ([] if own is None else [own]) + [w, m, v]))


def _place():
    x, y, c = lax.axis_index("x"), lax.axis_index("y"), lax.axis_index("c")
    return x, y, c


class _Gather:
    def __init__(self, srcs, dsts, send_sems, recv_sems, local_sems):
        na = len(srcs)
        x, y, c = _place()
        me, sibling = (x, y, c), (x, y, 1 - c)
        chips = [(1 - x, y), (x, 1 - y), (1 - x, 1 - y)]

        def slot(a, dev):
            return dsts[a].at[:, pl.ds(4 * dev[0] + 2 * dev[1] + dev[2], 1)]

        def copy(k, a, block, to, from_shard=False):
            return pltpu.make_async_remote_copy(
                src_ref=srcs[a] if from_shard else slot(a, block), dst_ref=slot(a, block),
                send_sem=send_sems.at[k, a], recv_sem=recv_sems.at[k, a], device_id=to, device_id_type=MESH)

        pairs = [(j, chip, a) for j, chip in enumerate(chips) for a in range(na)]
        self.mine = [pltpu.make_async_copy(srcs[a], slot(a, me), local_sems.at[a]) for a in range(na)]
        self.first = [copy(0, a, me, sibling, True) for a in range(na)]
        self.first += [copy(1 + j, a, me, (*chip, c), True) for j, chip, a in pairs]
        self.over_ici = [copy(1 + j, a, (*chip, c), me) for j, chip, a in pairs]
        self.passed = [copy(4 + j, a, (*chip, c), sibling) for j, chip, a in pairs]
        self.from_sibling = [copy(0, a, sibling, me) for a in range(na)]
        self.from_sibling += [copy(4 + j, a, (*chip, 1 - c), me) for j, chip, a in pairs]

    def begin(self):
        for cp in self.mine + self.first:
            cp.start()

    def relay(self):
        for arrived, onward in zip(self.over_ici, self.passed):
            arrived.wait_recv()
            onward.start()

    def finish(self):
        for cp in self.from_sibling:
            cp.wait_recv()
        for cp in self.first + self.passed:
            cp.wait_send()
        for cp in self.mine:
            cp.wait()


def _gather_scratch(na):
    return [pltpu.SemaphoreType.DMA((7, na)), pltpu.SemaphoreType.DMA((7, na)), pltpu.SemaphoreType.DMA((na,))]


def _gathered_shapes(shards):
    return [jax.ShapeDtypeStruct((a.shape[0], N_DEV) + a.shape[2:], a.dtype) for a in shards]


def _all_gather(shards, name):
    na = len(shards)

    def body(*refs):
        gather = _Gather(refs[:na], refs[na:2 * na], *refs[2 * na:])
        gather.begin()
        gather.relay()
        gather.finish()

    anyspec = pl.BlockSpec(memory_space=pl.ANY)
    return pl.pallas_call(
        body, name=name,
        in_specs=[anyspec] * na, out_specs=[anyspec] * na,
        out_shape=_gathered_shapes(shards), scratch_shapes=_gather_scratch(na),
    )(*shards)


_RELATIONS = [(dx, dy, dc) for dx in (0, 1) for dy in (0, 1) for dc in (0, 1)][1:]


def _flip(v, d):
    return 1 - v if d else v


def _exchange_copies(srcs, dsts, send_sems, recv_sems, local_sems):
    x, y, c = _place()
    my = 4 * x + 2 * y + c
    na = len(srcs)
    mine = [pltpu.make_async_copy(srcs[a].at[pl.ds(my, 1)], dsts[a].at[pl.ds(my, 1)], local_sems.at[a])
            for a in range(na)]
    sends, recvs = [], []
    for k, (dx, dy, dc) in enumerate(_RELATIONS):
        peer = (_flip(x, dx), _flip(y, dy), _flip(c, dc))
        pidx = 4 * peer[0] + 2 * peer[1] + peer[2]
        for a in range(na):
            for into, out in ((my, sends), (pidx, recvs)):
                out.append(pltpu.make_async_remote_copy(
                    src_ref=srcs[a].at[pl.ds(pidx, 1)], dst_ref=dsts[a].at[pl.ds(into, 1)],
                    send_sem=send_sems.at[k, a], recv_sem=recv_sems.at[k, a], device_id=peer, device_id_type=MESH))
    return mine, sends, recvs


def _exchange_begin(copies):
    mine, sends, _ = copies
    for cp in mine + sends:
        cp.start()


def _exchange_finish(copies):
    mine, sends, recvs = copies
    for cp in recvs:
        cp.wait_recv()
    for cp in sends:
        cp.wait_send()
    for cp in mine:
        cp.wait()


def _exchange_scratch(na):
    return [pltpu.SemaphoreType.DMA((7, na)), pltpu.SemaphoreType.DMA((7, na)), pltpu.SemaphoreType.DMA((na,))]


def _exchange_blocks(grads, name):
    na = len(grads)

    def body(*refs):
        copies = _exchange_copies(refs[:na], refs[na:2 * na], *refs[2 * na:])
        _exchange_begin(copies)
        _exchange_finish(copies)

    anyspec = pl.BlockSpec(memory_space=pl.ANY)
    return pl.pallas_call(
        body, name=name,
        in_specs=[anyspec] * na, out_specs=[anyspec] * na,
        out_shape=[jax.ShapeDtypeStruct(a.shape, a.dtype) for a in grads],
        scratch_shapes=_exchange_scratch(na),
    )(*grads)


def _all_reduce_small(v, name):
    r, c_ = v.shape

    def body(v_ref, o_ref, gath, send_sems, recv_sems):
        x, y, c = _place()
        my = 4 * x + 2 * y + c
        gath[my] = v_ref[...]
        sends = []
        for k, (dx, dy, dc) in enumerate(_RELATIONS):
            peer = (_flip(x, dx), _flip(y, dy), _flip(c, dc))
            cp = pltpu.make_async_remote_copy(
                src_ref=v_ref, dst_ref=gath.at[my], send_sem=send_sems.at[k], recv_sem=recv_sems.at[k],
                device_id=peer, device_id_type=MESH)
            cp.start()
            sends.append((cp, 4 * peer[0] + 2 * peer[1] + peer[2], k, peer))
        for cp, pidx, k, peer in sends:
            pltpu.make_async_remote_copy(
                src_ref=v_ref, dst_ref=gath.at[pidx], send_sem=send_sems.at[k], recv_sem=recv_sems.at[k],
                device_id=peer, device_id_type=MESH).wait_recv()
        for cp, *_ in sends:
            cp.wait_send()
        tot = gath[0]
        for k in range(1, N_DEV):
            tot = tot + gath[k]
        o_ref[...] = tot

    vm = pl.BlockSpec(memory_space=pltpu.VMEM)
    return pl.pallas_call(
        body, name=name, in_specs=[vm], out_specs=vm,
        out_shape=jax.ShapeDtypeStruct((r, c_), F32),
        scratch_shapes=[pltpu.VMEM((N_DEV, r, c_), F32), pltpu.SemaphoreType.DMA((7,)),
                        pltpu.SemaphoreType.DMA((7,))],
    )(v)


TM = 512
TM_MATMUL = 2048
TM_RESIDUAL = 1024
TQ = 256


def _pad_to(a, axis, size):
    pad = [(0, 0)] * a.ndim
    pad[axis] = (0, size - a.shape[axis])
    return jnp.pad(a, pad)


def _local_step(x, target, g_in0, late_shards, conv_full, norm_mix, q_norm, k_norm, norm_ffn):
    depth, d = norm_mix.shape
    cols = g_in0.shape[3]
    tm, tq = min(TM, x.shape[0]), min(TQ, x.shape[0])
    tmm, tmr = min(TM_MATMUL, x.shape[0]), min(TM_RESIDUAL, x.shape[0])
    attn = d // 2
    nheads = attn // HEAD_DIM
    scale = HEAD_DIM ** -0.5 * LOG2E
    saved = []
    for l in range(depth):
        h1 = _rmsnorm_fwd(x, norm_mix[l][None], tm,f"norm_mix_fwd_{l}")
        w_in = (g_in0, 0) if l == 0 else (g_rest, 3 * (l - 1))
        proj = _mm_blocks(h1, *w_in, tmm, f"proj_in_{l}")
        qk_gain = jnp.concatenate([jnp.tile(q_norm[l], nheads) * scale, jnp.tile(k_norm[l], nheads)])[None]
        qk = _qknorm_fwd(proj, qk_gain, tmm, f"qknorm_fwd_{l}")
        o, rtot, used, gathered = _attn_fwd(qk, proj, tq, f"attn_fwd_{l}", late_shards if l == 0 else ())
        if l == 0:
            g_gu0, g_rest, gb, gc = gathered if depth > 1 else (gathered[0], None, *gathered[1:])
            gb = gb.reshape(depth, -1, d)
            gc = gc.reshape(depth, -1, d)
        w_gu = (g_gu0, 0, 1) if l == 0 else (g_rest, 3 * (l - 1) + 1, 3 * (l - 1) + 2)
        conv_w8 = _pad_to(conv_full[l], 0, 8)
        cv = _conv_fwd(proj, conv_w8, f"conv_fwd_{l}")
        mix = jnp.concatenate([o, cv], axis=1)
        x1 = _mm_residual(mix, gb, l, x, tmr, 512, f"proj_out_{l}")
        h2 = _rmsnorm_fwd(x1, norm_ffn[l][None], tm,f"norm_ffn_fwd_{l}")
        g, u, act = _mm_swiglu(h2, *w_gu, tmr, f"ffn_up_{l}")
        x2 = _mm_residual(act, gc, l, x1, tmr, 512, f"ffn_down_{l}")
        saved.append((x, h1, proj, qk_gain, qk, rtot, used, conv_w8, mix, x1, h2, g, u, act, w_in, w_gu))
        x = x2

    dx, dxb, loss = _loss_head(x, target, tm, "loss_head")

    grads = [None] * depth
    small = [None] * depth
    landed = {}
    for l in reversed(range(depth)):
        x0, h1, proj, qk_gain, qk, rtot, used, conv_w8, mix, x1, h2, g, u, act, w_in, w_gu = saved[l]
        d = x0.shape[1]
        dg, du = _mm_nt_swiglu_bwd(dxb, gc, l, g, u, tmr, f"ffn_down_bwd_{l}")
        d_wdown = _mm_tn(act, dxb, 768, d, tmm, False, f"dw_down_{l}")
        d_wgate = _mm_tn(h2, dg, d, cols, tmm, True, f"dw_gate_{l}")
        d_wup = _mm_tn(h2, du, d, cols, tmm, True, f"dw_up_{l}")
        dh2 = _mm_nt_blocks([dg, du], w_gu[0], list(w_gu[1:]), tm, f"ffn_up_bwd_{l}")
        dx1, dx1b, dg_ffn = _rmsnorm_bwd(dh2, x1, norm_ffn[l][None], dx, tm, f"norm_ffn_bwd_{l}")
        dmix = _mm_nt(dx1b, gb, l, tmr, 512, f"proj_out_bwd_{l}")
        d_wout = _mm_tn(mix, dx1b, 512, d, tmm, False, f"dw_out_{l}")
        dcb, dcc, dcu, dconv = _conv_bwd(dmix, proj, conv_w8, f"conv_bwd_{l}")
        travel = {}
        if l == 0:
            travel = {5 * ll + j: grads[ll][j][1] for ll in range(1, depth) for j in range(5)}
            travel.update({1: d_wgate[1], 2: d_wup[1], 3: d_wout[1], 4: d_wdown[1]})
        dq, dk, dv, arrived = _attn_bwd(qk, proj, dmix, rtot, used, tq, f"attn_bwd_{l}",
                                        [t.reshape(N_DEV, -1, t.shape[-1]) for t in travel.values()])
        landed.update(zip(travel.keys(), arrived))
        dqk, dg_qk = _qknorm_bwd(jnp.concatenate([dq, dk], axis=1), proj, qk_gain, tmm, f"qknorm_bwd_{l}")
        dproj = jnp.concatenate([dqk, dv.astype(BF16), dcb, dcc, dcu], axis=1)
        d_win = _mm_tn(h1, dproj, d, cols, tmm, True, f"dw_in_{l}")
        dh1 = _mm_nt_blocks([dproj], w_in[0], [w_in[1]], tmr, f"proj_in_bwd_{l}")
        dx, dxb, dg_mix = _rmsnorm_bwd(dh1, x0, norm_mix[l][None], dx1, tm, f"norm_mix_bwd_{l}")
        grads[l] = (d_win, d_wgate, d_wup, d_wout, d_wdown)
        dq_gain = jnp.sum(dg_qk[0, :attn].reshape(nheads, HEAD_DIM), axis=0) * scale
        dk_gain = jnp.sum(dg_qk[0, attn:].reshape(nheads, HEAD_DIM), axis=0)
        small[l] = (dg_mix[0], dg_ffn[0], dq_gain, dk_gain, dconv[:3])
    return loss, dx, grads, small, landed


def kernel(x, norm_mix, w_in, q_norm, k_norm, conv_w, w_out, norm_ffn, w_gate, w_up, w_down, loss_target, m_norm_mix, m_w_in, m_q_norm, m_k_norm, m_conv_w, m_w_out, m_norm_ffn, m_w_gate, m_w_up, m_w_down, v_norm_mix, v_w_in, v_q_norm, v_k_norm, v_conv_w, v_w_out, v_norm_ffn, v_w_gate, v_w_up, v_w_down):
    depth, d, in_shard = w_in.shape
    ff_shard = w_gate.shape[2]
    ff_pad = in_shard
    conv_shard = conv_w.shape[2]
    xs = x.reshape(x.shape[-2], d)
    target = loss_target.reshape(xs.shape)

    pa = jnp.stack([w_in, _pad_to(w_gate, 2, ff_pad), _pad_to(w_up, 2, ff_pad)], axis=1)
    pa = pa.reshape(3 * depth, 1, d, in_shard).astype(BF16)
    pd = _pad_to(_pad_to(conv_w.reshape(depth * 3, conv_shard), 0, 8), 1, LANES)[None, None]
    g_in0, gd = _all_gather([pa[:1], pd], "gather_first")
    conv_full = gd[0, :, :depth * 3, :conv_shard].transpose(1, 0, 2).reshape(depth, 3, N_DEV * conv_shard)
    late_shards = [pa[1:3]] + ([pa[3:]] if depth > 1 else [])
    late_shards += [w_out.astype(BF16)[:, None], _pad_to(w_down, 1, ff_pad).astype(BF16)[:, None]]

    loss, grad_x, grads, small, landed = _local_step(xs, target, g_in0, late_shards, conv_full, norm_mix, q_norm,
                                                     k_norm, norm_ffn)

    x_, y_, c_ = _place()
    my = 4 * x_ + 2 * y_ + c_

    def blocks(t):
        return t.reshape(N_DEV, -1, t.shape[-1])

    rest = [i for i in range(5 * depth) if i not in landed]
    landed.update(zip(rest, _exchange_blocks([blocks(grads[i // 5][i % 5][1]) for i in rest], "exchange_grads")))
    landed = [landed[i] for i in range(5 * depth)]
    own = [lax.dynamic_index_in_dim(blocks(grads[i // 5][i % 5][0]), my, 0, keepdims=False)
           for i in range(5 * depth)]

    nconv = N_DEV * conv_shard
    rows = []
    for l in range(depth):
        g_mix, g_ffn, g_q, g_k, g_conv = small[l]
        qkrow = _pad_to(jnp.concatenate([g_q, g_k]), 0, d)
        rows += [g_mix[None], g_ffn[None], qkrow[None], _pad_to(g_conv, 1, d)]
    nrow = 6 * depth
    packed = jnp.concatenate(rows + [_pad_to(loss[:1], 1, d)], axis=0)
    packed = _pad_to(packed, 0, ((nrow + 1 + 7) // 8) * 8)
    summed = _all_reduce_small(packed, "reduce_small")
    loss_out = summed[nrow, 0]

    def big(i, w, m, v, tr, name, rows_=None, cols_=None):
        parts = landed[i]
        pr, pcn = parts.shape[1], parts.shape[2]
        w, m, v = [_pad_to(_pad_to(t, 0, pr), 1, pcn) for t in (w, m, v)]
        outs = _adamw(parts, own[i], w, m, v, tr, name)
        return [o[:rows_ or pr, :cols_ or pcn] for o in outs]

    res = {}
    for l in range(depth):
        i = 5 * l
        res[("w_in", l)] = big(i, w_in[l], m_w_in[l], v_w_in[l], 256, f"adamw_in_{l}")
        res[("w_gate", l)] = big(i + 1, w_gate[l], m_w_gate[l], v_w_gate[l], 256, f"adamw_gate_{l}", cols_=ff_shard)
        res[("w_up", l)] = big(i + 2, w_up[l], m_w_up[l], v_w_up[l], 256, f"adamw_up_{l}", cols_=ff_shard)
        res[("w_out", l)] = big(i + 3, w_out[l], m_w_out[l], v_w_out[l], w_out.shape[1], f"adamw_out_{l}")
        res[("w_down", l)] = big(i + 4, w_down[l], m_w_down[l], v_w_down[l], 128, f"adamw_down_{l}",
                                 rows_=ff_shard)

    g_rows, w_rows, m_rows, v_rows = [], [], [], []
    for l in range(depth):
        base = l * 6
        conv_g = lax.dynamic_slice(summed[base + 3:base + 6], (0, my * conv_shard), (3, conv_shard))
        g_rows += [summed[base:base + 3], _pad_to(conv_g, 1, d)]
        for dst, (nm, qn, kn, nf, cw) in ((w_rows, (norm_mix, q_norm, k_norm, norm_ffn, conv_w)),
                                          (m_rows, (m_norm_mix, m_q_norm, m_k_norm, m_norm_ffn, m_conv_w)),
                                          (v_rows, (v_norm_mix, v_q_norm, v_k_norm, v_norm_ffn, v_conv_w))):
            dst += [nm[l][None], nf[l][None], _pad_to(jnp.concatenate([qn[l], kn[l]]), 0, d)[None],
                    _pad_to(cw[l], 1, d)]
    prow = ((nrow + 7) // 8) * 8
    gs, ws, ms, vs = [_pad_to(jnp.concatenate(t, axis=0), 0, prow) for t in (g_rows, w_rows, m_rows, v_rows)]
    sm = _adamw(gs[None], None, ws, ms, vs, prow, "adamw_small")

    hd = q_norm.shape[1]

    def small_out(t, kind):
        per_layer = []
        for l in range(depth):
            base = l * 6
            per_layer.append({"norm_mix": t[base], "norm_ffn": t[base + 1], "q_norm": t[base + 2, :hd],
                              "k_norm": t[base + 2, hd:2 * hd], "conv_w": t[base + 3:base + 6, :conv_shard]}[kind])
        return jnp.stack(per_layer)

    def big_out(name, i):
        return jnp.stack([res[(name, l)][i] for l in range(depth)])

    outs = [loss_out, grad_x.reshape(x.shape)]
    for i in range(4):
        outs += [small_out(sm[i], "norm_mix"), big_out("w_in", i), small_out(sm[i], "q_norm"),
                 small_out(sm[i], "k_norm"), small_out(sm[i], "conv_w"), big_out("w_out", i),
                 small_out(sm[i], "norm_ffn"), big_out("w_gate", i), big_out("w_up", i), big_out("w_down", i)]
    return tuple(outs)
```

```python
import jax
import jax.numpy as jnp
from jax import lax
from jax.experimental import pallas as pl
from jax.experimental.pallas import tpu as pltpu

F32 = jnp.float32
BF16 = jnp.bfloat16
MESH = pl.DeviceIdType.MESH

N_DEV = 8
LANES = 128
HEAD_DIM = 64
KEY_CHUNK = 128
EPS = 1e-6
VMEM_LIMIT = 48 * 1024 * 1024

ADAM_LR = 0.001
ADAM_B1 = 0.9
ADAM_B2 = 0.999
ADAM_EPS = 1e-08
ADAM_WD = 0.01
ADAM_STEP = 10

NN = (((1,), (0,)), ((), ()))
NT = (((1,), (1,)), ((), ()))
TN = (((0,), (0,)), ((), ()))


def _dot(a, b, dims):
    return lax.dot_general(a.astype(BF16), b.astype(BF16), dims, preferred_element_type=F32)


def _cparams(*sem):
    return pltpu.CompilerParams(dimension_semantics=sem, vmem_limit_bytes=VMEM_LIMIT)


def _split_hi_lo(v):
    hi = v.astype(BF16)
    lo = (v - hi.astype(F32)).astype(BF16)
    return jnp.concatenate([hi, lo], axis=1)


def _rmsnorm_fwd(x, gain, tm, name):
    s, d = x.shape

    def body(x_ref, g_ref, o_ref):
        xv = x_ref[...]
        r = lax.rsqrt(jnp.mean(xv * xv, axis=-1, keepdims=True) + EPS)
        o_ref[...] = ((xv * r) * g_ref[...]).astype(o_ref.dtype)

    return pl.pallas_call(
        body, name=name, grid=(s // tm,),
        in_specs=[pl.BlockSpec((tm, d), lambda i: (i, 0)), pl.BlockSpec((1, d), lambda i: (0, 0))],
        out_specs=pl.BlockSpec((tm, d), lambda i: (i, 0)),
        out_shape=jax.ShapeDtypeStruct((s, d), BF16),
        compiler_params=_cparams("parallel"),
    )(x, gain)


def _rmsnorm_bwd(dh, x, gain, dres, tm, name, travel=()):
    s, d = x.shape
    nsteps = s // tm

    def body(dh_ref, x_ref, g_ref, dres_ref, dx_ref, dxb_ref, dg_ref):
        i = pl.program_id(0)
        xv = x_ref[...]
        r = lax.rsqrt(jnp.mean(xv * xv, axis=-1, keepdims=True) + EPS)
        xhat = xv * r
        dhv = dh_ref[...]
        dxh = dhv * g_ref[...]
        proj = jnp.mean(dxh * xhat, axis=-1, keepdims=True)
        dxv = dres_ref[...] + r * (dxh - xhat * proj)
        dx_ref[...] = dxv
        dxb_ref[...] = dxv.astype(dxb_ref.dtype)
        part = jnp.sum((dhv * xhat).reshape(tm // 8, 8, d), axis=0)

        @pl.when(i == 0)
        def _():
            dg_ref[...] = part

        @pl.when(i > 0)
        def _():
            dg_ref[...] += part

        @pl.when(i == nsteps - 1)
        def _():
            dg_ref[...] = jnp.broadcast_to(jnp.sum(dg_ref[...], axis=0, keepdims=True), (8, d))

    row = pl.BlockSpec((tm, d), lambda i: (i, 0))
    body, more_in, more_out, more_shapes, more_scratch = _host_exchange(body, 4, 3, travel, (nsteps,))
    outs = pl.pallas_call(
        body, name=name, grid=(nsteps,),
        in_specs=[row, row, pl.BlockSpec((1, d), lambda i: (0, 0)), row] + more_in,
        out_specs=[row, row, pl.BlockSpec((8, d), lambda i: (0, 0))] + more_out,
        out_shape=[jax.ShapeDtypeStruct((s, d), F32), jax.ShapeDtypeStruct((s, d), BF16),
                   jax.ShapeDtypeStruct((8, d), F32)] + more_shapes,
        scratch_shapes=more_scratch,
        compiler_params=_cparams("arbitrary"),
    )(dh, x, gain, dres, *travel)
    return outs[0], outs[1], outs[2], list(outs[3:])


def _group_mean_matrix():
    r = lax.broadcasted_iota(jnp.int32, (LANES, LANES), 0) // HEAD_DIM
    c = lax.broadcasted_iota(jnp.int32, (LANES, LANES), 1) // HEAD_DIM
    return jnp.where(r == c, 1.0 / HEAD_DIM, 0.0).astype(BF16)


def _group_mean(v, gm):
    hi = v.astype(BF16)
    lo = (v - hi.astype(F32)).astype(BF16)
    return _dot(hi, gm, NN) + _dot(lo, gm, NN)


def _qknorm_fwd(proj, gains, tm, name):
    s = proj.shape[0]
    ncol = gains.shape[1] // LANES

    def body(p_ref, g_ref, gm_ref, o_ref):
        xv = p_ref[...].astype(F32)
        r = lax.rsqrt(_group_mean(xv * xv, gm_ref[...]) + EPS)
        o_ref[...] = ((xv * r) * g_ref[...]).astype(o_ref.dtype)

    blk = pl.BlockSpec((tm, LANES), lambda i, j: (i, j))
    return pl.pallas_call(
        body, name=name, grid=(s // tm, ncol),
        in_specs=[blk, pl.BlockSpec((1, LANES), lambda i, j: (0, j)),
                  pl.BlockSpec((LANES, LANES), lambda i, j: (0, 0))],
        out_specs=blk,
        out_shape=jax.ShapeDtypeStruct((s, ncol * LANES), BF16),
        compiler_params=_cparams("parallel", "parallel"),
    )(proj, gains, _group_mean_matrix())


def _qknorm_bwd(dqk, proj, gains, tm, name):
    s = proj.shape[0]
    ncol = gains.shape[1] // LANES
    nsteps = s // tm

    def body(dy_ref, p_ref, g_ref, gm_ref, dx_ref, dg_ref):
        i = pl.program_id(1)
        gm = gm_ref[...]
        xv = p_ref[...].astype(F32)
        r = lax.rsqrt(_group_mean(xv * xv, gm) + EPS)
        xhat = xv * r
        dy = dy_ref[...]
        dxh = dy * g_ref[...]
        proj_ = _group_mean(dxh * xhat, gm)
        dx_ref[...] = (r * (dxh - xhat * proj_)).astype(dx_ref.dtype)
        part = jnp.sum((dy * xhat).reshape(tm // 8, 8, LANES), axis=0)

        @pl.when(i == 0)
        def _():
            dg_ref[...] = part

        @pl.when(i > 0)
        def _():
            dg_ref[...] += part

        @pl.when(i == nsteps - 1)
        def _():
            dg_ref[...] = jnp.broadcast_to(jnp.sum(dg_ref[...], axis=0, keepdims=True), (8, LANES))

    blk = pl.BlockSpec((tm, LANES), lambda j, i: (i, j))
    return pl.pallas_call(
        body, name=name, grid=(ncol, nsteps),
        in_specs=[blk, blk, pl.BlockSpec((1, LANES), lambda j, i: (0, j)),
                  pl.BlockSpec((LANES, LANES), lambda j, i: (0, 0))],
        out_specs=[blk, pl.BlockSpec((8, LANES), lambda j, i: (0, j))],
        out_shape=[jax.ShapeDtypeStruct((s, ncol * LANES), BF16),
                   jax.ShapeDtypeStruct((8, ncol * LANES), F32)],
        compiler_params=_cparams("parallel", "arbitrary"),
    )(dqk, proj, gains, _group_mean_matrix())


CONV_ROWS = 256
HALO = 8


def _conv_fwd(proj, conv_w8, name):
    s = proj.shape[0]
    nblk = conv_w8.shape[1] // LANES
    first = 3 * nblk
    nchunk = s // CONV_ROWS

    def body(cb_ref, cc_ref, cu_ref, w_ref, y_ref, hpad):
        hpad[pl.ds(0, 2 * HALO), :] = jnp.zeros((2 * HALO, LANES), F32)

        def fill(i, _):
            r0 = pl.multiple_of(i * CONV_ROWS, CONV_ROWS)
            hpad[pl.ds(r0 + 2 * HALO, CONV_ROWS), :] = (
                cc_ref[pl.ds(r0, CONV_ROWS), :].astype(F32) * cu_ref[pl.ds(r0, CONV_ROWS), :].astype(F32))
            return 0

        lax.fori_loop(0, nchunk, fill, 0)
        w0, w1, w2 = w_ref[0:1, :], w_ref[1:2, :], w_ref[2:3, :]

        def conv(i, _):
            r0 = pl.multiple_of(i * CONV_ROWS, CONV_ROWS)
            win = hpad[pl.ds(r0 + HALO, CONV_ROWS + HALO), :]
            c = (w2 * win[HALO:] + w1 * pltpu.roll(win, 1, 0)[HALO:] + w0 * pltpu.roll(win, 2, 0)[HALO:])
            y_ref[pl.ds(r0, CONV_ROWS), :] = (cb_ref[pl.ds(r0, CONV_ROWS), :].astype(F32) * c).astype(y_ref.dtype)
            return 0

        lax.fori_loop(0, nchunk, conv, 0)

    def col(off):
        return pl.BlockSpec((s, LANES), lambda j: (0, off + j))

    return pl.pallas_call(
        body, name=name, grid=(nblk,),
        in_specs=[col(first), col(first + nblk), col(first + 2 * nblk), pl.BlockSpec((8, LANES), lambda j: (0, j))],
        out_specs=pl.BlockSpec((s, LANES), lambda j: (0, j)),
        out_shape=jax.ShapeDtypeStruct((s, nblk * LANES), BF16),
        scratch_shapes=[pltpu.VMEM((s + 2 * HALO, LANES), F32)],
        compiler_params=_cparams("parallel"),
    )(proj, proj, proj, conv_w8)


def _conv_bwd(dmix, proj, conv_w8, name):
    s = proj.shape[0]
    nblk = conv_w8.shape[1] // LANES
    first = 3 * nblk
    nchunk = s // CONV_ROWS

    def body(dy_ref, cb_ref, cc_ref, cu_ref, w_ref, dcb_ref, dcc_ref, dcu_ref, dw_ref, hpad, dcpad):
        hpad[pl.ds(0, 2 * HALO), :] = jnp.zeros((2 * HALO, LANES), F32)
        dcpad[pl.ds(s, 2 * HALO), :] = jnp.zeros((2 * HALO, LANES), F32)

        def fill(i, _):
            r0 = pl.multiple_of(i * CONV_ROWS, CONV_ROWS)
            hpad[pl.ds(r0 + 2 * HALO, CONV_ROWS), :] = (
                cc_ref[pl.ds(r0, CONV_ROWS), :].astype(F32) * cu_ref[pl.ds(r0, CONV_ROWS), :].astype(F32))
            return 0

        lax.fori_loop(0, nchunk, fill, 0)
        w0, w1, w2 = w_ref[0:1, :], w_ref[1:2, :], w_ref[2:3, :]

        def fold(v):
            return jnp.sum(v.reshape(CONV_ROWS // 8, 8, LANES), axis=0)

        def first_pass(i, acc):
            a0, a1, a2 = acc
            r0 = pl.multiple_of(i * CONV_ROWS, CONV_ROWS)
            win = hpad[pl.ds(r0 + HALO, CONV_ROWS + HALO), :]
            h0 = win[HALO:]
            h1 = pltpu.roll(win, 1, 0)[HALO:]
            h2 = pltpu.roll(win, 2, 0)[HALO:]
            c = w2 * h0 + w1 * h1 + w0 * h2
            dy = dy_ref[pl.ds(r0, CONV_ROWS), :]
            dcb_ref[pl.ds(r0, CONV_ROWS), :] = (dy * c).astype(dcb_ref.dtype)
            dc = dy * cb_ref[pl.ds(r0, CONV_ROWS), :].astype(F32)
            dcpad[pl.ds(r0, CONV_ROWS), :] = dc
            return a0 + fold(dc * h2), a1 + fold(dc * h1), a2 + fold(dc * h0)

        z8 = jnp.zeros((8, LANES), F32)
        a0, a1, a2 = lax.fori_loop(0, nchunk, first_pass, (z8, z8, z8))
        dw_ref[...] = jnp.concatenate(
            [jnp.sum(a0, axis=0, keepdims=True), jnp.sum(a1, axis=0, keepdims=True),
             jnp.sum(a2, axis=0, keepdims=True), jnp.zeros((5, LANES), F32)], axis=0)

        def second_pass(i, _):
            r0 = pl.multiple_of(i * CONV_ROWS, CONV_ROWS)
            win = dcpad[pl.ds(r0, CONV_ROWS + HALO), :]
            n = CONV_ROWS + HALO
            dh = (w2 * win[:CONV_ROWS] + w1 * pltpu.roll(win, n - 1, 0)[:CONV_ROWS]
                  + w0 * pltpu.roll(win, n - 2, 0)[:CONV_ROWS])
            dcc_ref[pl.ds(r0, CONV_ROWS), :] = (dh * cu_ref[pl.ds(r0, CONV_ROWS), :].astype(F32)).astype(dcc_ref.dtype)
            dcu_ref[pl.ds(r0, CONV_ROWS), :] = (dh * cc_ref[pl.ds(r0, CONV_ROWS), :].astype(F32)).astype(dcu_ref.dtype)
            return 0

        lax.fori_loop(0, nchunk, second_pass, 0)

    def col(off):
        return pl.BlockSpec((s, LANES), lambda j: (0, off + j))

    out = pl.BlockSpec((s, LANES), lambda j: (0, j))
    return pl.pallas_call(
        body, name=name, grid=(nblk,),
        in_specs=[col(nblk), col(first), col(first + nblk), col(first + 2 * nblk),
                  pl.BlockSpec((8, LANES), lambda j: (0, j))],
        out_specs=[out, out, out, pl.BlockSpec((8, LANES), lambda j: (0, j))],
        out_shape=[jax.ShapeDtypeStruct((s, nblk * LANES), BF16)] * 3 + [jax.ShapeDtypeStruct((8, nblk * LANES), F32)],
        scratch_shapes=[pltpu.VMEM((s + 2 * HALO, LANES), F32), pltpu.VMEM((s + 2 * HALO, LANES), F32)],
        compiler_params=_cparams("parallel"),
    )(dmix, proj, proj, proj, conv_w8)


LOG2E = 1.4426950408889634
LN2 = 0.6931471805599453
NEG_BIG = -1e30
SATURATED = 160.0


def _cumsum_matrix(kind):
    j = lax.broadcasted_iota(jnp.int32, (KEY_CHUNK, 2 * KEY_CHUNK), 0)
    c = lax.broadcasted_iota(jnp.int32, (KEY_CHUNK, 2 * KEY_CHUNK), 1)
    tri = {"after": j > c, "upto": j <= c, "before": j < c}[kind]
    return jnp.where((c >= KEY_CHUNK) | tri, 1.0, 0.0).astype(BF16)


def _stack_heads(t, m0):
    zero = jnp.zeros_like(t)
    return jnp.concatenate([jnp.where(m0, t, zero), jnp.where(m0, zero, t)], axis=0)


def _softplus2(z):
    sp = jnp.maximum(z, 0.0) + jnp.log2(1.0 + jnp.exp2(-jnp.abs(z)))
    return sp, z - sp


def _key_chunk(ref, kc):
    return ref[pl.ds(pl.multiple_of(kc * KEY_CHUNK, KEY_CHUNK), KEY_CHUNK), :]


def _attn_bwd(qk, proj, dmix, rtot, used, tq, name, travel=()):
    s = qk.shape[0]
    nhp = qk.shape[1] // (2 * LANES)
    nc = tq // KEY_CHUNK
    nq = s // tq
    nt = len(travel)

    def body(used_ref, q_ref, k_ref, v_ref, do_ref, r_ref, cmi_ref, cme_ref, bias_ref, dq_ref, dk_ref, dv_ref,
             z_refs, ls_refs, sig_refs, sp_refs, gb_refs, pr_ref, gs_ref, copies):
        qi = pl.program_id(1)

        @pl.when(qi == 0)
        def _():
            dk_ref[...] = jnp.zeros_like(dk_ref)
            dv_ref[...] = jnp.zeros_like(dv_ref)

        if copies is not None:
            @pl.when(jnp.logical_and(pl.program_id(0) == 0, qi == 0))
            def _():
                _exchange_begin(copies)

        nslots = (qi + 1) * nc
        walked = used_ref[pl.program_id(0), qi].astype(jnp.int32)
        first = jnp.clip(nslots - walked, 0, nslots - nc) // nc * nc
        m0 = lax.broadcasted_iota(jnp.int32, (1, LANES), 1) < HEAD_DIM
        qs = _stack_heads(q_ref[...], m0)
        do = do_ref[...]
        dos = _stack_heads(do.astype(BF16), m0)
        dosl = _stack_heads((do * LN2).astype(BF16), m0)
        cmi = cmi_ref[...]
        cme = cme_ref[...]

        def chunk_at(i):
            return jnp.clip(i, first, nslots - 1)

        def scores(kc):
            return _dot(qs, _key_chunk(k_ref, kc), NT)

        def weights(ls, cs, da, pr, kc):
            a = jnp.exp2(ls - (pr - cs[:, :KEY_CHUNK]))
            gb = (a * da).astype(BF16)
            ks = pl.multiple_of(kc * KEY_CHUNK, KEY_CHUNK)
            dv_ref[pl.ds(ks, KEY_CHUNK), :] += _dot(a, dos, TN)
            return gb, jnp.exp2(ls), pr - cs[:, KEY_CHUNK:]

        def score_grads(gb, sig, cg, gs, dq, kc):
            dzb = (gb.astype(F32) * (1.0 - sig) - sig * (gs + cg[:, :KEY_CHUNK])).astype(BF16)
            ks = pl.multiple_of(kc * KEY_CHUNK, KEY_CHUNK)
            dk_ref[pl.ds(ks, KEY_CHUNK), :] += _dot(dzb, qs, TN)
            dq = dq + _dot(jnp.concatenate([dzb[:tq], dzb[tq:]], axis=1), _stack_heads(_key_chunk(k_ref, kc), m0), NN)
            return gs + cg[:, KEY_CHUNK:], dq

        def step(i, par, bias=None):
            cur, prv = par, 1 - par
            k1, k2 = chunk_at(i - 1), chunk_at(i - 2)
            z_next = scores(chunk_at(i + 1))
            cs = _dot(sp_refs[prv][...], cmi, NN)
            da = _dot(dosl, _key_chunk(v_ref, k1), NT)
            cg = _dot(gb_refs[cur][...], cme, NN)
            z = z_refs[cur][...]
            if bias is not None:
                z = z + bias
            sp, ls = _softplus2(z)
            sp_refs[cur][...] = sp.astype(BF16)
            ls_refs[cur][...] = ls
            gs, dq = score_grads(gb_refs[cur][...], sig_refs[cur][...], cg, gs_ref[...], dq_ref[...], k2)
            gs_ref[...] = gs
            dq_ref[...] = dq
            gb, sig, pr = weights(ls_refs[prv][...], cs, da, pr_ref[...], k1)
            gb_refs[prv][...] = gb
            sig_refs[prv][...] = sig
            pr_ref[...] = pr
            z_refs[prv][...] = z_next

        pr_ref[...] = jnp.concatenate([r_ref[:, :LANES], r_ref[:, LANES:]], axis=0)
        gs_ref[...] = jnp.zeros((2 * tq, LANES), F32)
        dq_ref[...] = jnp.zeros((tq, LANES), F32)
        z_refs[0][...] = scores(first)
        sp_refs[1][...] = jnp.zeros((2 * tq, LANES), BF16)
        ls_refs[1][...] = jnp.full((2 * tq, LANES), NEG_BIG, F32)
        gb_refs[0][...] = jnp.zeros((2 * tq, LANES), BF16)
        sig_refs[0][...] = jnp.zeros((2 * tq, LANES), F32)

        def two_steps(j, _):
            step(2 * j, 0)
            step(2 * j + 1, 1)
            return 0

        lax.fori_loop(first // 2, nslots // 2 - 1, two_steps, 0)
        step(nslots - 2, 0, bias_ref[0])
        step(nslots - 1, 1, bias_ref[1])
        k1, k2 = chunk_at(nslots - 1), chunk_at(nslots - 2)
        gb, sig, _ = weights(ls_refs[1][...], _dot(sp_refs[1][...], cmi, NN),
                             _dot(dosl, _key_chunk(v_ref, k1), NT), pr_ref[...], k1)
        gb2 = gb_refs[0][...]
        gs, dq = score_grads(gb2, sig_refs[0][...], _dot(gb2, cme, NN), gs_ref[...], dq_ref[...], k2)
        _, dq = score_grads(gb, sig, _dot(gb, cme, NN), gs, dq, k1)
        dq_ref[...] = dq

        if copies is not None:
            @pl.when(jnp.logical_and(pl.program_id(0) == nhp - 1, qi == nq - 1))
            def _():
                _exchange_finish(copies)

    def wrapped(*refs):
        ins, rest = refs[:9], refs[9:]
        srcs, rest = rest[:nt], rest[nt:]
        outs, rest = rest[:3], rest[3:]
        lands, rest = rest[:nt], rest[nt:]
        z0, z1, ls0, ls1, sg0, sg1, sp0, sp1, gb0, gb1, pr_ref, gs_ref = rest[:12]
        copies = _exchange_copies(srcs, lands, *rest[12:]) if nt else None
        body(*ins, *outs, (z0, z1), (ls0, ls1), (sg0, sg1), (sp0, sp1), (gb0, gb1), pr_ref, gs_ref, copies)

    assert nc == 2
    bias = _diag_bias(tq, True)
    bias = jnp.concatenate([bias[:, :, :KEY_CHUNK], bias[:, :, KEY_CHUNK:]], axis=1)
    qblk = pl.BlockSpec((tq, LANES), lambda p, i: (i, p))
    full = pl.BlockSpec((s, LANES), lambda p, i: (0, p))
    cmspec = pl.BlockSpec((KEY_CHUNK, 2 * KEY_CHUNK), lambda p, i: (0, 0))
    anyspec = pl.BlockSpec(memory_space=pl.ANY)
    shape = jax.ShapeDtypeStruct((s, nhp * LANES), F32)
    f32buf = pltpu.VMEM((2 * tq, LANES), F32)
    bf16buf = pltpu.VMEM((2 * tq, LANES), BF16)
    outs = pl.pallas_call(
        wrapped, name=name, grid=(nhp, nq),
        in_specs=[pl.BlockSpec(memory_space=pltpu.SMEM),
                  qblk,
                  pl.BlockSpec((s, LANES), lambda p, i: (0, nhp + p)),
                  pl.BlockSpec((s, LANES), lambda p, i: (0, 2 * nhp + p)),
                  qblk,
                  pl.BlockSpec((tq, 2 * LANES), lambda p, i: (i, p)),
                  cmspec, cmspec,
                  pl.BlockSpec((nc, 2 * tq, LANES), lambda p, i: (0, 0, 0))] + [anyspec] * nt,
        out_specs=[qblk, full, full] + [anyspec] * nt,
        out_shape=[shape, shape, shape] + [jax.ShapeDtypeStruct(t.shape, t.dtype) for t in travel],
        scratch_shapes=[f32buf] * 6 + [bf16buf] * 4 + [f32buf] * 2 + (_exchange_scratch(nt) if nt else []),
        compiler_params=_cparams("arbitrary", "arbitrary"),
    )(used, qk, qk, proj, dmix, rtot, _cumsum_matrix("upto"), _cumsum_matrix("before"), bias, *travel)
    return outs[0], outs[1], outs[2], list(outs[3:])


def _pair_cumsum_matrix(kind):
    j = lax.broadcasted_iota(jnp.int32, (2 * KEY_CHUNK, 4 * KEY_CHUNK), 0)
    c = lax.broadcasted_iota(jnp.int32, (2 * KEY_CHUNK, 4 * KEY_CHUNK), 1)
    same_head = (j // KEY_CHUNK) == ((c // KEY_CHUNK) % 2)
    jj, cc = j % KEY_CHUNK, c % KEY_CHUNK
    tri = {"after": jj > cc, "upto": jj <= cc, "before": jj < cc}[kind]
    return jnp.where(same_head & ((c >= 2 * KEY_CHUNK) | tri), 1.0, 0.0).astype(BF16)


def _diag_bias(tq, ascending):
    nc = tq // KEY_CHUNK
    shape = (nc, tq, 2 * KEY_CHUNK)
    d = lax.broadcasted_iota(jnp.int32, shape, 0)
    r = lax.broadcasted_iota(jnp.int32, shape, 1)
    c = lax.broadcasted_iota(jnp.int32, shape, 2) % KEY_CHUNK
    chunk = d if ascending else nc - 1 - d
    return jnp.where(chunk * KEY_CHUNK + c < r, 0.0, NEG_BIG).astype(F32)


def _attn_fwd(qk, proj, tq, name, shards=()):
    s = qk.shape[0]
    nhp = qk.shape[1] // (2 * LANES)
    nc = tq // KEY_CHUNK
    nq = s // tq
    ng = len(shards)
    assert nc == 2
    w = 2 * KEY_CHUNK

    def body(q_ref, k_ref, v_ref, cm_ref, bias_ref, o_ref, r_ref, used_ref, z_refs, ls_refs, cs_refs, ct_refs,
             sp_refs, ab_refs, acc_ref, gather):
        qi = pl.program_id(1)
        if gather is not None:
            @pl.when(jnp.logical_and(pl.program_id(0) == 0, qi == 0))
            def _():
                gather.begin()

        nslots = (qi + 1) * nc
        m0 = lax.broadcasted_iota(jnp.int32, (1, LANES), 1) < HEAD_DIM
        q = q_ref[...]
        cm = cm_ref[...]

        def chunk_at(i):
            return jnp.clip(nslots - 1 - i, 0, nslots - 1)

        def scores(kc):
            return _dot(q, _stack_heads(_key_chunk(k_ref, kc), m0), NT)

        def values(ab, kc):
            return _dot(ab, _stack_heads(_key_chunk(v_ref, kc), m0), NN)

        def step(i, par, bias=None, stages="zscwv"):
            cur, prv = par, 1 - par
            if "z" in stages:
                z_next = scores(chunk_at(i + 1))
            if "c" in stages:
                cs = _dot(sp_refs[prv][...], cm, NN)
            if "v" in stages:
                pv = values(ab_refs[prv][...], chunk_at(i - 3))
            if "w" in stages:
                rs = r_ref[...]
                r_ref[...] = rs + ct_refs[cur][...]
                ab_refs[cur][...] = jnp.exp2(ls_refs[cur][...] - cs_refs[cur][...] - rs).astype(BF16)
            if "s" in stages:
                z = z_refs[cur][...]
                if bias is not None:
                    z = z + bias
                sp, ls = _softplus2(z)
                sp_refs[cur][...] = sp.astype(BF16)
                ls_refs[cur][...] = ls
            if "v" in stages:
                acc_ref[...] += pv
            if "c" in stages:
                cs_refs[prv][...] = cs[:, :w]
                ct_refs[prv][...] = cs[:, w:]
            if "z" in stages:
                z_refs[prv][...] = z_next

        z_refs[0][...] = scores(chunk_at(0))
        for p in range(2):
            sp_refs[p][...] = jnp.zeros((tq, w), BF16)
            ls_refs[p][...] = jnp.full((tq, w), NEG_BIG, F32)
            cs_refs[p][...] = jnp.zeros((tq, w), F32)
            ct_refs[p][...] = jnp.zeros((tq, w), F32)
            ab_refs[p][...] = jnp.zeros((tq, w), BF16)
        r_ref[...] = jnp.zeros((tq, w), F32)
        acc_ref[...] = jnp.zeros((tq, LANES), F32)
        step(0, 0, bias_ref[0])
        step(1, 1, bias_ref[1])

        def two_steps(carry):
            j, _ = carry
            step(2 * j, 0)
            step(2 * j + 1, 1)
            return j + 1, jnp.min(r_ref[...])

        pairs, low = lax.while_loop(lambda c: jnp.logical_and(c[0] < nslots // 2, c[1] < SATURATED), two_steps,
                                    (jnp.int32(1), jnp.min(r_ref[...])))
        entered = 2 * pairs
        saturated = low >= SATURATED

        @pl.when(saturated)
        def _():
            step(entered, 0, stages="v")

        @pl.when(jnp.logical_not(saturated))
        def _():
            step(entered, 0, stages="cwv")
            step(entered + 1, 1, stages="wv")
            step(entered + 2, 0, stages="v")

        o_ref[...] = acc_ref[...].astype(o_ref.dtype)
        used_ref[pl.program_id(0), qi] = jnp.where(saturated, entered - 2, entered).astype(F32)

        if gather is not None:
            @pl.when(jnp.logical_and(pl.program_id(0) == nhp - 1, qi == nq // 2))
            def _():
                gather.relay()

            @pl.when(jnp.logical_and(pl.program_id(0) == nhp - 1, qi == nq - 1))
            def _():
                gather.finish()

    def wrapped(*refs):
        ins, rest = refs[:5], refs[5:]
        srcs, rest = rest[:ng], rest[ng:]
        outs, rest = rest[:3], rest[3:]
        dsts, scratch = rest[:ng], rest[ng:]
        z, ls, cs, ct, sp, ab = [scratch[2 * j:2 * j + 2] for j in range(6)]
        gather = _Gather(srcs, dsts, *scratch[13:]) if ng else None
        body(*ins, *outs, z, ls, cs, ct, sp, ab, scratch[12], gather)

    f32buf = pltpu.VMEM((tq, w), F32)
    bf16buf = pltpu.VMEM((tq, w), BF16)
    anyspec = pl.BlockSpec(memory_space=pl.ANY)
    outs = pl.pallas_call(
        wrapped, name=name, grid=(nhp, nq),
        in_specs=[pl.BlockSpec((tq, LANES), lambda p, i: (i, p)),
                  pl.BlockSpec((s, LANES), lambda p, i: (0, nhp + p)),
                  pl.BlockSpec((s, LANES), lambda p, i: (0, 2 * nhp + p)),
                  pl.BlockSpec((w, 2 * w), lambda p, i: (0, 0)),
                  pl.BlockSpec((nc, tq, w), lambda p, i: (0, 0, 0))] + [anyspec] * ng,
        out_specs=[pl.BlockSpec((tq, LANES), lambda p, i: (i, p)),
                   pl.BlockSpec((tq, w), lambda p, i: (i, p)),
                   pl.BlockSpec(memory_space=pltpu.SMEM)] + [anyspec] * ng,
        out_shape=[jax.ShapeDtypeStruct((s, nhp * LANES), BF16),
                   jax.ShapeDtypeStruct((s, nhp * w), F32),
                   jax.ShapeDtypeStruct((nhp, nq), F32)] + _gathered_shapes(shards),
        scratch_shapes=([f32buf] * 8 + [bf16buf] * 4 + [pltpu.VMEM((tq, LANES), F32)]
                        + (_gather_scratch(ng) if ng else [])),
        compiler_params=_cparams("arbitrary", "arbitrary"),
    )(qk, qk, proj, _pair_cumsum_matrix("after"), _diag_bias(tq, False), *shards)
    return outs[0], outs[1], outs[2], list(outs[3:])


BLOCK_PAIR = 2


def _side_by_side(b_ref):
    return jnp.concatenate([b_ref[p] for p in range(BLOCK_PAIR)], axis=1)


def _mm_blocks(h, ga, widx, tm, name):
    s, d = h.shape
    nb, cols = ga.shape[1], ga.shape[3]

    def body(a_ref, b_ref, o_ref):
        o_ref[...] = _dot(a_ref[...], _side_by_side(b_ref), NN).astype(o_ref.dtype)

    return pl.pallas_call(
        body, name=name, grid=(s // tm, nb // BLOCK_PAIR),
        in_specs=[pl.BlockSpec((tm, d), lambda i, j: (i, 0)),
                  pl.BlockSpec((None, BLOCK_PAIR, d, cols), lambda i, j: (widx, j, 0, 0))],
        out_specs=pl.BlockSpec((tm, BLOCK_PAIR * cols), lambda i, j: (i, j)),
        out_shape=jax.ShapeDtypeStruct((s, nb * cols), BF16),
        compiler_params=_cparams("parallel", "arbitrary"),
    )(h, ga)


def _mm_swiglu(h, ga, gidx, uidx, tm, name):
    s, d = h.shape
    nb, cols = ga.shape[1], ga.shape[3]

    def body(a_ref, bg_ref, bu_ref, g_ref, u_ref, act_ref):
        a = a_ref[...]
        g = _dot(a, _side_by_side(bg_ref), NN)
        u = _dot(a, _side_by_side(bu_ref), NN)
        g_ref[...] = g.astype(g_ref.dtype)
        u_ref[...] = u.astype(u_ref.dtype)
        act_ref[...] = (g * (1.0 / (1.0 + jnp.exp(-g))) * u).astype(act_ref.dtype)

    def wspec(idx):
        return pl.BlockSpec((None, BLOCK_PAIR, d, cols), lambda i, j: (idx, j, 0, 0))

    out = pl.BlockSpec((tm, BLOCK_PAIR * cols), lambda i, j: (i, j))
    shape = jax.ShapeDtypeStruct((s, nb * cols), BF16)
    return pl.pallas_call(
        body, name=name, grid=(s // tm, nb // BLOCK_PAIR),
        in_specs=[pl.BlockSpec((tm, d), lambda i, j: (i, 0)), wspec(gidx), wspec(uidx)],
        out_specs=[out, out, out], out_shape=[shape, shape, shape],
        compiler_params=_cparams("parallel", "arbitrary"),
    )(h, ga, ga)


def _mm_residual(a, w3, lidx, res, tm, tn, name):
    s, k = a.shape
    n = w3.shape[2]

    def body(a_ref, b_ref, r_ref, o_ref):
        o_ref[...] = r_ref[...] + _dot(a_ref[...], b_ref[...], NN)

    return pl.pallas_call(
        body, name=name, grid=(s // tm, n // tn),
        in_specs=[pl.BlockSpec((tm, k), lambda i, j: (i, 0)),
                  pl.BlockSpec((None, k, tn), lambda i, j: (lidx, 0, j)),
                  pl.BlockSpec((tm, tn), lambda i, j: (i, j))],
        out_specs=pl.BlockSpec((tm, tn), lambda i, j: (i, j)),
        out_shape=jax.ShapeDtypeStruct((s, n), F32),
        compiler_params=_cparams("parallel", "arbitrary"),
    )(a, w3, res)


def _mm_nt(a, w3, lidx, tm, tn, name):
    s, k = a.shape
    n = w3.shape[1]

    def body(a_ref, b_ref, o_ref):
        o_ref[...] = _dot(a_ref[...], b_ref[...], NT)

    return pl.pallas_call(
        body, name=name, grid=(s // tm, n // tn),
        in_specs=[pl.BlockSpec((tm, k), lambda i, j: (i, 0)),
                  pl.BlockSpec((None, tn, k), lambda i, j: (lidx, j, 0))],
        out_specs=pl.BlockSpec((tm, tn), lambda i, j: (i, j)),
        out_shape=jax.ShapeDtypeStruct((s, n), F32),
        compiler_params=_cparams("parallel", "arbitrary"),
    )(a, w3)


def _mm_nt_swiglu_bwd(dx, wd3, lidx, g, u, tm, name):
    s, d = dx.shape
    cols = BLOCK_PAIR * (g.shape[1] // N_DEV)

    def body(a_ref, b_ref, g_ref, u_ref, dg_ref, du_ref):
        dact = _dot(a_ref[...], b_ref[...], NT)
        gv = g_ref[...].astype(F32)
        sig = 1.0 / (1.0 + jnp.exp(-gv))
        du_ref[...] = (dact * (gv * sig)).astype(du_ref.dtype)
        dg_ref[...] = (dact * u_ref[...].astype(F32) * (sig * (1.0 + gv * (1.0 - sig)))).astype(dg_ref.dtype)

    blk = pl.BlockSpec((tm, cols), lambda i, j: (i, j))
    shape = jax.ShapeDtypeStruct(g.shape, BF16)
    return pl.pallas_call(
        body, name=name, grid=(s // tm, N_DEV // BLOCK_PAIR),
        in_specs=[pl.BlockSpec((tm, d), lambda i, j: (i, 0)),
                  pl.BlockSpec((None, cols, d), lambda i, j: (lidx, j, 0)), blk, blk],
        out_specs=[blk, blk], out_shape=[shape, shape],
        compiler_params=_cparams("parallel", "arbitrary"),
    )(dx, wd3, g, u)


def _mm_nt_blocks(das, ga, widxs, tm, name, travel=()):
    s = das[0].shape[0]
    nb, d, cols = ga.shape[1], ga.shape[2], ga.shape[3]
    nw = len(das)

    def body(*refs):
        a_refs, b_refs, o_ref = refs[:nw], refs[nw:2 * nw], refs[2 * nw]
        acc = None
        wide = BLOCK_PAIR * cols
        for w in range(nw):
            for k in range(nb // BLOCK_PAIR):
                b = jnp.concatenate([b_refs[w][BLOCK_PAIR * k + p] for p in range(BLOCK_PAIR)], axis=1)
                part = _dot(a_refs[w][:, k * wide:(k + 1) * wide], b, NT)
                acc = part if acc is None else acc + part
        o_ref[...] = acc

    def wspec(idx):
        return pl.BlockSpec((None, nb, d, cols), lambda i: (idx, 0, 0, 0), pipeline_mode=pl.Buffered(1))

    grid = (s // tm,)
    body, more_in, more_out, more_shapes, more_scratch = _host_exchange(body, 2 * nw, 1, travel, grid)
    outs = pl.pallas_call(
        body, name=name, grid=grid,
        in_specs=[pl.BlockSpec((tm, nb * cols), lambda i: (i, 0))] * nw + [wspec(i) for i in widxs] + more_in,
        out_specs=[pl.BlockSpec((tm, d), lambda i: (i, 0))] + more_out,
        out_shape=[jax.ShapeDtypeStruct((s, d), F32)] + more_shapes,
        scratch_shapes=more_scratch,
        compiler_params=_cparams("arbitrary"),
    )(*das, *([ga] * nw), *travel)
    return outs[0], list(outs[1:])


def _mm_tn(a, b, ta, tb, tk, out_blocks, name, travel=()):
    s, ka = a.shape
    nb = b.shape[1]
    nk = s // tk
    cols = tb
    if out_blocks:
        tb = BLOCK_PAIR * cols

    def body(a_ref, b_ref, o_ref, ob_ref):
        k = pl.program_id(2)
        part = _dot(a_ref[...], b_ref[...], TN)

        def put(first):
            if out_blocks:
                for p in range(BLOCK_PAIR):
                    piece = part[:, p * cols:(p + 1) * cols]
                    o_ref[p] = piece if first else o_ref[p] + piece
            else:
                o_ref[...] = part if first else o_ref[...] + part

        @pl.when(k == 0)
        def _():
            put(True)

        @pl.when(k > 0)
        def _():
            put(False)

        @pl.when(k == nk - 1)
        def _():
            ob_ref[...] = o_ref[...].astype(ob_ref.dtype)

    if out_blocks:
        out_spec = pl.BlockSpec((BLOCK_PAIR, ta, cols), lambda i, j, k: (j, i, 0))
        shape = (nb // cols, ka, cols)
    else:
        out_spec = pl.BlockSpec((ta, tb), lambda i, j, k: (i, j))
        shape = (ka, nb)
    grid = (ka // ta, nb // tb, nk)
    body, more_in, more_out, more_shapes, more_scratch = _host_exchange(body, 2, 2, travel, grid)
    outs = pl.pallas_call(
        body, name=name, grid=grid,
        in_specs=[pl.BlockSpec((tk, ta), lambda i, j, k: (k, i)),
                  pl.BlockSpec((tk, tb), lambda i, j, k: (k, j))] + more_in,
        out_specs=[out_spec, out_spec] + more_out,
        out_shape=[jax.ShapeDtypeStruct(shape, F32), jax.ShapeDtypeStruct(shape, BF16)] + more_shapes,
        scratch_shapes=more_scratch,
        compiler_params=_cparams("arbitrary", "arbitrary", "arbitrary"),
    )(a, b, *travel)
    return (outs[0], outs[1]), list(outs[2:])


def _loss_head(y, target, tm, name):
    s, d = y.shape
    nsteps = s // tm

    def body(y_ref, t_ref, dy_ref, dyb_ref, l_ref, acc):
        i = pl.program_id(0)
        diff = y_ref[...] - t_ref[...]
        dy_ref[...] = diff * (1.0 / d)
        dyb_ref[...] = (diff * (1.0 / d)).astype(dyb_ref.dtype)
        part = jnp.sum((diff * diff).reshape(tm // 8, 8, d), axis=0)

        @pl.when(i == 0)
        def _():
            acc[...] = part

        @pl.when(i > 0)
        def _():
            acc[...] += part

        @pl.when(i == nsteps - 1)
        def _():
            tot = jnp.sum(jnp.sum(acc[...], axis=1, keepdims=True), axis=0, keepdims=True)
            l_ref[...] = jnp.broadcast_to(tot * (0.5 / d), (8, LANES))

    row = pl.BlockSpec((tm, d), lambda i: (i, 0))
    return pl.pallas_call(
        body, name=name, grid=(nsteps,),
        in_specs=[row, row],
        out_specs=[row, row, pl.BlockSpec((8, LANES), lambda i: (0, 0))],
        out_shape=[jax.ShapeDtypeStruct((s, d), F32), jax.ShapeDtypeStruct((s, d), BF16),
                   jax.ShapeDtypeStruct((8, LANES), F32)],
        scratch_shapes=[pltpu.VMEM((8, d), F32)],
        compiler_params=_cparams("arbitrary"),
    )(y, target)


def _adamw(parts, own, w, m, v, tr, name):
    p, rows, cols = parts.shape
    c1 = 1.0 / (1.0 - ADAM_B1 ** ADAM_STEP)
    c2 = 1.0 / (1.0 - ADAM_B2 ** ADAM_STEP)

    def body(*refs):
        if own is None:
            p_ref, w_ref, m_ref, v_ref, g_ref, d_ref, nm_ref, nv_ref = refs
            g = p_ref[0]
            for k in range(1, p):
                g = g + p_ref[k]
        else:
            p_ref, own_ref, w_ref, m_ref, v_ref, g_ref, d_ref, nm_ref, nv_ref = refs
            x, y, c = _place()
            my = 4 * x + 2 * y + c
            mine = own_ref[...]
            g = jnp.where(my == 0, mine, p_ref[0].astype(F32))
            for k in range(1, p):
                g = g + jnp.where(my == k, mine, p_ref[k].astype(F32))
        nm = ADAM_B1 * m_ref[...] + (1.0 - ADAM_B1) * g
        nv = ADAM_B2 * v_ref[...] + (1.0 - ADAM_B2) * (g * g)
        g_ref[...] = g
        nm_ref[...] = nm
        nv_ref[...] = nv
        d_ref[...] = -ADAM_LR * ((nm * c1) / (jnp.sqrt(nv * c2) + ADAM_EPS) + ADAM_WD * w_ref[...])

    blk = pl.BlockSpec((tr, cols), lambda i: (i, 0))
    shape = jax.ShapeDtypeStruct((rows, cols), F32)
    return pl.pallas_call(
        body, name=name, grid=(rows // tr,),
        in_specs=[pl.BlockSpec((p, tr, cols), lambda i: (0, i, 0))] + [blk] * (3 if own is None else 4),
        out_specs=[blk] * 4, out_shape=[shape] * 4,
        compiler_params=_cparams("parallel"),
    )(*([parts] + ([] if own is None else [own]) + [w, m, v]))


def _place():
    x, y, c = lax.axis_index("x"), lax.axis_index("y"), lax.axis_index("c")
    return x, y, c


class _Gather:
    def __init__(self, srcs, dsts, send_sems, recv_sems, local_sems):
        na = len(srcs)
        x, y, c = _place()
        me, sibling = (x, y, c), (x, y, 1 - c)
        chips = [(1 - x, y), (x, 1 - y), (1 - x, 1 - y)]

        def slot(a, dev):
            return dsts[a].at[:, pl.ds(4 * dev[0] + 2 * dev[1] + dev[2], 1)]

        def copy(k, a, block, to, from_shard=False):
            return pltpu.make_async_remote_copy(
                src_ref=srcs[a] if from_shard else slot(a, block), dst_ref=slot(a, block),
                send_sem=send_sems.at[k, a], recv_sem=recv_sems.at[k, a], device_id=to, device_id_type=MESH)

        pairs = [(j, chip, a) for j, chip in enumerate(chips) for a in range(na)]
        self.mine = [pltpu.make_async_copy(srcs[a], slot(a, me), local_sems.at[a]) for a in range(na)]
        self.first = [copy(0, a, me, sibling, True) for a in range(na)]
        self.first += [copy(1 + j, a, me, (*chip, c), True) for j, chip, a in pairs]
        self.over_ici = [copy(1 + j, a, (*chip, c), me) for j, chip, a in pairs]
        self.passed = [copy(4 + j, a, (*chip, c), sibling) for j, chip, a in pairs]
        self.from_sibling = [copy(0, a, sibling, me) for a in range(na)]
        self.from_sibling += [copy(4 + j, a, (*chip, 1 - c), me) for j, chip, a in pairs]

    def begin(self):
        for cp in self.mine + self.first:
            cp.start()

    def relay(self):
        for arrived, onward in zip(self.over_ici, self.passed):
            arrived.wait_recv()
            onward.start()

    def finish(self):
        for cp in self.from_sibling:
            cp.wait_recv()
        for cp in self.first + self.passed:
            cp.wait_send()
        for cp in self.mine:
            cp.wait()


def _gather_scratch(na):
    return [pltpu.SemaphoreType.DMA((7, na)), pltpu.SemaphoreType.DMA((7, na)), pltpu.SemaphoreType.DMA((na,))]


def _gathered_shapes(shards):
    return [jax.ShapeDtypeStruct((a.shape[0], N_DEV) + a.shape[2:], a.dtype) for a in shards]


def _all_gather(shards, name):
    na = len(shards)

    def body(*refs):
        gather = _Gather(refs[:na], refs[na:2 * na], *refs[2 * na:])
        gather.begin()
        gather.relay()
        gather.finish()

    anyspec = pl.BlockSpec(memory_space=pl.ANY)
    return pl.pallas_call(
        body, name=name,
        in_specs=[anyspec] * na, out_specs=[anyspec] * na,
        out_shape=_gathered_shapes(shards), scratch_shapes=_gather_scratch(na),
    )(*shards)


_RELATIONS = [(dx, dy, dc) for dx in (0, 1) for dy in (0, 1) for dc in (0, 1)][1:]


def _flip(v, d):
    return 1 - v if d else v


def _exchange_copies(srcs, dsts, send_sems, recv_sems, local_sems):
    x, y, c = _place()
    my = 4 * x + 2 * y + c
    na = len(srcs)
    mine = [pltpu.make_async_copy(srcs[a].at[pl.ds(my, 1)], dsts[a].at[pl.ds(my, 1)], local_sems.at[a])
            for a in range(na)]
    sends, recvs = [], []
    for k, (dx, dy, dc) in enumerate(_RELATIONS):
        peer = (_flip(x, dx), _flip(y, dy), _flip(c, dc))
        pidx = 4 * peer[0] + 2 * peer[1] + peer[2]
        for a in range(na):
            for into, out in ((my, sends), (pidx, recvs)):
                out.append(pltpu.make_async_remote_copy(
                    src_ref=srcs[a].at[pl.ds(pidx, 1)], dst_ref=dsts[a].at[pl.ds(into, 1)],
                    send_sem=send_sems.at[k, a], recv_sem=recv_sems.at[k, a], device_id=peer, device_id_type=MESH))
    return mine, sends, recvs


def _exchange_begin(copies):
    mine, sends, _ = copies
    for cp in mine + sends:
        cp.start()


def _exchange_finish(copies):
    mine, sends, recvs = copies
    for cp in recvs:
        cp.wait_recv()
    for cp in sends:
        cp.wait_send()
    for cp in mine:
        cp.wait()


def _exchange_scratch(na):
    return [pltpu.SemaphoreType.DMA((7, na)), pltpu.SemaphoreType.DMA((7, na)), pltpu.SemaphoreType.DMA((na,))]


def _host_exchange(body, n_in, n_out, travel, grid):
    nt = len(travel)
    if not nt:
        return body, [], [], [], []

    def wrapped(*refs):
        ins, srcs = refs[:n_in], refs[n_in:n_in + nt]
        outs, rest = refs[n_in + nt:n_in + nt + n_out], refs[n_in + nt + n_out:]
        dsts, scratch = rest[:nt], rest[nt:]
        copies = _exchange_copies(srcs, dsts, *scratch[-3:])
        first = last = None
        for axis, size in enumerate(grid):
            at_start, at_end = pl.program_id(axis) == 0, pl.program_id(axis) == size - 1
            first = at_start if first is None else jnp.logical_and(first, at_start)
            last = at_end if last is None else jnp.logical_and(last, at_end)

        @pl.when(first)
        def _():
            _exchange_begin(copies)

        body(*ins, *outs, *scratch[:-3])

        @pl.when(last)
        def _():
            _exchange_finish(copies)

    anyspec = pl.BlockSpec(memory_space=pl.ANY)
    return (wrapped, [anyspec] * nt, [anyspec] * nt, [jax.ShapeDtypeStruct(t.shape, t.dtype) for t in travel],
            _exchange_scratch(nt))


def _all_reduce_small(v, name):
    r, c_ = v.shape

    def body(v_ref, o_ref, gath, send_sems, recv_sems):
        x, y, c = _place()
        my = 4 * x + 2 * y + c
        gath[my] = v_ref[...]
        sends = []
        for k, (dx, dy, dc) in enumerate(_RELATIONS):
            peer = (_flip(x, dx), _flip(y, dy), _flip(c, dc))
            cp = pltpu.make_async_remote_copy(
                src_ref=v_ref, dst_ref=gath.at[my], send_sem=send_sems.at[k], recv_sem=recv_sems.at[k],
                device_id=peer, device_id_type=MESH)
            cp.start()
            sends.append((cp, 4 * peer[0] + 2 * peer[1] + peer[2], k, peer))
        for cp, pidx, k, peer in sends:
            pltpu.make_async_remote_copy(
                src_ref=v_ref, dst_ref=gath.at[pidx], send_sem=send_sems.at[k], recv_sem=recv_sems.at[k],
                device_id=peer, device_id_type=MESH).wait_recv()
        for cp, *_ in sends:
            cp.wait_send()
        tot = gath[0]
        for k in range(1, N_DEV):
            tot = tot + gath[k]
        o_ref[...] = tot

    vm = pl.BlockSpec(memory_space=pltpu.VMEM)
    return pl.pallas_call(
        body, name=name, in_specs=[vm], out_specs=vm,
        out_shape=jax.ShapeDtypeStruct((r, c_), F32),
        scratch_shapes=[pltpu.VMEM((N_DEV, r, c_), F32), pltpu.SemaphoreType.DMA((7,)),
                        pltpu.SemaphoreType.DMA((7,))],
    )(v)


TM = 512
TM_MATMUL = 2048
TM_RESIDUAL = 1024
TQ = 256


def _device_blocks(t):
    return t.reshape(N_DEV, -1, t.shape[-1])


def _pad_to(a, axis, size):
    pad = [(0, 0)] * a.ndim
    pad[axis] = (0, size - a.shape[axis])
    return jnp.pad(a, pad)


def _local_step(x, target, g_in0, late_shards, conv_full, norm_mix, q_norm, k_norm, norm_ffn):
    depth, d = norm_mix.shape
    cols = g_in0.shape[3]
    tm, tq = min(TM, x.shape[0]), min(TQ, x.shape[0])
    tmm, tmr = min(TM_MATMUL, x.shape[0]), min(TM_RESIDUAL, x.shape[0])
    attn = d // 2
    nheads = attn // HEAD_DIM
    scale = HEAD_DIM ** -0.5 * LOG2E
    saved = []
    for l in range(depth):
        h1 = _rmsnorm_fwd(x, norm_mix[l][None], tm,f"norm_mix_fwd_{l}")
        w_in = (g_in0, 0) if l == 0 else (g_rest, 3 * (l - 1))
        proj = _mm_blocks(h1, *w_in, tmm, f"proj_in_{l}")
        qk_gain = jnp.concatenate([jnp.tile(q_norm[l], nheads) * scale, jnp.tile(k_norm[l], nheads)])[None]
        qk = _qknorm_fwd(proj, qk_gain, tmm, f"qknorm_fwd_{l}")
        o, rtot, used, gathered = _attn_fwd(qk, proj, tq, f"attn_fwd_{l}", late_shards if l == 0 else ())
        if l == 0:
            g_gu0, g_rest, gb, gc = gathered if depth > 1 else (gathered[0], None, *gathered[1:])
            gb = gb.reshape(depth, -1, d)
            gc = gc.reshape(depth, -1, d)
        w_gu = (g_gu0, 0, 1) if l == 0 else (g_rest, 3 * (l - 1) + 1, 3 * (l - 1) + 2)
        conv_w8 = _pad_to(conv_full[l], 0, 8)
        cv = _conv_fwd(proj, conv_w8, f"conv_fwd_{l}")
        mix = jnp.concatenate([o, cv], axis=1)
        x1 = _mm_residual(mix, gb, l, x, tmr, 512, f"proj_out_{l}")
        h2 = _rmsnorm_fwd(x1, norm_ffn[l][None], tm,f"norm_ffn_fwd_{l}")
        g, u, act = _mm_swiglu(h2, *w_gu, tmr, f"ffn_up_{l}")
        x2 = _mm_residual(act, gc, l, x1, tmr, 512, f"ffn_down_{l}")
        saved.append((x, h1, proj, qk_gain, qk, rtot, used, conv_w8, mix, x1, h2, g, u, act, w_in, w_gu))
        x = x2

    dx, dxb, loss = _loss_head(x, target, tm, "loss_head")

    grads = [None] * depth
    small = [None] * depth
    landed = {}
    for l in reversed(range(depth)):
        x0, h1, proj, qk_gain, qk, rtot, used, conv_w8, mix, x1, h2, g, u, act, w_in, w_gu = saved[l]
        d = x0.shape[1]
        dg, du = _mm_nt_swiglu_bwd(dxb, gc, l, g, u, tmr, f"ffn_down_bwd_{l}")
        d_wdown, _ = _mm_tn(act, dxb, 768, d, tmm, False, f"dw_down_{l}")
        d_wgate, _ = _mm_tn(h2, dg, d, cols, tmm, True, f"dw_gate_{l}")
        d_wup, _ = _mm_tn(h2, du, d, cols, tmm, True, f"dw_up_{l}")
        dh2, _ = _mm_nt_blocks([dg, du], w_gu[0], list(w_gu[1:]), tm, f"ffn_up_bwd_{l}")
        dx1, dx1b, dg_ffn, _ = _rmsnorm_bwd(dh2, x1, norm_ffn[l][None], dx, tm, f"norm_ffn_bwd_{l}")
        dmix = _mm_nt(dx1b, gb, l, tmr, 512, f"proj_out_bwd_{l}")
        d_wout, _ = _mm_tn(mix, dx1b, 512, d, tmm, False, f"dw_out_{l}")
        dcb, dcc, dcu, dconv = _conv_bwd(dmix, proj, conv_w8, f"conv_bwd_{l}")
        rides = {}
        if l == 0:
            rides = {5 * ll + j: grads[ll][j][1] for ll in range(1, depth) for j in range(5)}
            rides.update({3: d_wout[1], 4: d_wdown[1]})
        dq, dk, dv, arrived = _attn_bwd(qk, proj, dmix, rtot, used, tq, f"attn_bwd_{l}",
                                        [_device_blocks(t) for t in rides.values()])
        landed.update(zip(rides.keys(), arrived))
        dqk, dg_qk = _qknorm_bwd(jnp.concatenate([dq, dk], axis=1), proj, qk_gain, tmm, f"qknorm_bwd_{l}")
        dproj = jnp.concatenate([dqk, dv.astype(BF16), dcb, dcc, dcu], axis=1)
        first = l == 0
        d_win, arrived = _mm_tn(h1, dproj, d, cols, tmm, True, f"dw_in_{l}",
                                [_device_blocks(d_wgate[1])] if first else [])
        landed.update(zip([1], arrived))
        dh1, arrived = _mm_nt_blocks([dproj], w_in[0], [w_in[1]], tmr, f"proj_in_bwd_{l}",
                                     [_device_blocks(d_win[1])] if first else [])
        landed.update(zip([0], arrived))
        dx, dxb, dg_mix, arrived = _rmsnorm_bwd(dh1, x0, norm_mix[l][None], dx1, tm, f"norm_mix_bwd_{l}",
                                                [_device_blocks(d_wup[1])] if first else [])
        landed.update(zip([2], arrived))
        grads[l] = (d_win, d_wgate, d_wup, d_wout, d_wdown)
        dq_gain = jnp.sum(dg_qk[0, :attn].reshape(nheads, HEAD_DIM), axis=0) * scale
        dk_gain = jnp.sum(dg_qk[0, attn:].reshape(nheads, HEAD_DIM), axis=0)
        small[l] = (dg_mix[0], dg_ffn[0], dq_gain, dk_gain, dconv[:3])
    return loss, dx, grads, small, landed


def kernel(x, norm_mix, w_in, q_norm, k_norm, conv_w, w_out, norm_ffn, w_gate, w_up, w_down, loss_target, m_norm_mix, m_w_in, m_q_norm, m_k_norm, m_conv_w, m_w_out, m_norm_ffn, m_w_gate, m_w_up, m_w_down, v_norm_mix, v_w_in, v_q_norm, v_k_norm, v_conv_w, v_w_out, v_norm_ffn, v_w_gate, v_w_up, v_w_down):
    depth, d, in_shard = w_in.shape
    ff_shard = w_gate.shape[2]
    ff_pad = in_shard
    conv_shard = conv_w.shape[2]
    xs = x.reshape(x.shape[-2], d)
    target = loss_target.reshape(xs.shape)

    pa = jnp.stack([w_in, _pad_to(w_gate, 2, ff_pad), _pad_to(w_up, 2, ff_pad)], axis=1)
    pa = pa.reshape(3 * depth, 1, d, in_shard).astype(BF16)
    pd = _pad_to(_pad_to(conv_w.reshape(depth * 3, conv_shard), 0, 8), 1, LANES)[None, None]
    g_in0, gd = _all_gather([pa[:1], pd], "gather_first")
    conv_full = gd[0, :, :depth * 3, :conv_shard].transpose(1, 0, 2).reshape(depth, 3, N_DEV * conv_shard)
    late_shards = [pa[1:3]] + ([pa[3:]] if depth > 1 else [])
    late_shards += [w_out.astype(BF16)[:, None], _pad_to(w_down, 1, ff_pad).astype(BF16)[:, None]]

    loss, grad_x, grads, small, landed = _local_step(xs, target, g_in0, late_shards, conv_full, norm_mix, q_norm,
                                                     k_norm, norm_ffn)

    x_, y_, c_ = _place()
    my = 4 * x_ + 2 * y_ + c_
    landed = [landed[i] for i in range(5 * depth)]
    own = [lax.dynamic_index_in_dim(_device_blocks(grads[i // 5][i % 5][0]), my, 0, keepdims=False)
           for i in range(5 * depth)]

    nconv = N_DEV * conv_shard
    rows = []
    for l in range(depth):
        g_mix, g_ffn, g_q, g_k, g_conv = small[l]
        qkrow = _pad_to(jnp.concatenate([g_q, g_k]), 0, d)
        rows += [g_mix[None], g_ffn[None], qkrow[None], _pad_to(g_conv, 1, d)]
    nrow = 6 * depth
    packed = jnp.concatenate(rows + [_pad_to(loss[:1], 1, d)], axis=0)
    packed = _pad_to(packed, 0, ((nrow + 1 + 7) // 8) * 8)
    summed = _all_reduce_small(packed, "reduce_small")
    loss_out = summed[nrow, 0]

    def big(i, w, m, v, tr, name, rows_=None, cols_=None):
        parts = landed[i]
        pr, pcn = parts.shape[1], parts.shape[2]
        w, m, v = [_pad_to(_pad_to(t, 0, pr), 1, pcn) for t in (w, m, v)]
        outs = _adamw(parts, own[i], w, m, v, tr, name)
        return [o[:rows_ or pr, :cols_ or pcn] for o in outs]

    res = {}
    for l in range(depth):
        i = 5 * l
        res[("w_in", l)] = big(i, w_in[l], m_w_in[l], v_w_in[l], 256, f"adamw_in_{l}")
        res[("w_gate", l)] = big(i + 1, w_gate[l], m_w_gate[l], v_w_gate[l], 256, f"adamw_gate_{l}", cols_=ff_shard)
        res[("w_up", l)] = big(i + 2, w_up[l], m_w_up[l], v_w_up[l], 256, f"adamw_up_{l}", cols_=ff_shard)
        res[("w_out", l)] = big(i + 3, w_out[l], m_w_out[l], v_w_out[l], w_out.shape[1], f"adamw_out_{l}")
        res[("w_down", l)] = big(i + 4, w_down[l], m_w_down[l], v_w_down[l], 128, f"adamw_down_{l}",
                                 rows_=ff_shard)

    g_rows, w_rows, m_rows, v_rows = [], [], [], []
    for l in range(depth):
        base = l * 6
        conv_g = lax.dynamic_slice(summed[base + 3:base + 6], (0, my * conv_shard), (3, conv_shard))
        g_rows += [summed[base:base + 3], _pad_to(conv_g, 1, d)]
        for dst, (nm, qn, kn, nf, cw) in ((w_rows, (norm_mix, q_norm, k_norm, norm_ffn, conv_w)),
                                          (m_rows, (m_norm_mix, m_q_norm, m_k_norm, m_norm_ffn, m_conv_w)),
                                          (v_rows, (v_norm_mix, v_q_norm, v_k_norm, v_norm_ffn, v_conv_w))):
            dst += [nm[l][None], nf[l][None], _pad_to(jnp.concatenate([qn[l], kn[l]]), 0, d)[None],
                    _pad_to(cw[l], 1, d)]
    prow = ((nrow + 7) // 8) * 8
    gs, ws, ms, vs = [_pad_to(jnp.concatenate(t, axis=0), 0, prow) for t in (g_rows, w_rows, m_rows, v_rows)]
    sm = _adamw(gs[None], None, ws, ms, vs, prow, "adamw_small")

    hd = q_norm.shape[1]

    def small_out(t, kind):
        per_layer = []
        for l in range(depth):
            base = l * 6
            per_layer.append({"norm_mix": t[base], "norm_ffn": t[base + 1], "q_norm": t[base + 2, :hd],
                              "k_norm": t[base + 2, hd:2 * hd], "conv_w": t[base + 3:base + 6, :conv_shard]}[kind])
        return jnp.stack(per_layer)

    def big_out(name, i):
        return jnp.stack([res[(name, l)][i] for l in range(depth)])

    outs = [loss_out, grad_x.reshape(x.shape)]
    for i in range(4):
        outs += [small_out(sm[i], "norm_mix"), big_out("w_in", i), small_out(sm[i], "q_norm"),
                 small_out(sm[i], "k_norm"), small_out(sm[i], "conv_w"), big_out("w_out", i),
                 small_out(sm[i], "norm_ffn"), big_out("w_gate", i), big_out("w_up", i), big_out("w_down", i)]
    return tuple(outs)
```

```python
import jax
import jax.numpy as jnp
from jax import lax
from jax.experimental import pallas as pl
from jax.experimental.pallas import tpu as pltpu

F32 = jnp.float32
BF16 = jnp.bfloat16
MESH = pl.DeviceIdType.MESH

N_DEV = 8
LANES = 128
HEAD_DIM = 64
KEY_CHUNK = 128
EPS = 1e-6
VMEM_LIMIT = 48 * 1024 * 1024

ADAM_LR = 0.001
ADAM_B1 = 0.9
ADAM_B2 = 0.999
ADAM_EPS = 1e-08
ADAM_WD = 0.01
ADAM_STEP = 10

NN = (((1,), (0,)), ((), ()))
NT = (((1,), (1,)), ((), ()))
TN = (((0,), (0,)), ((), ()))


def _dot(a, b, dims):
    return lax.dot_general(a.astype(BF16), b.astype(BF16), dims, preferred_element_type=F32)


def _cparams(*sem):
    return pltpu.CompilerParams(dimension_semantics=sem, vmem_limit_bytes=VMEM_LIMIT)


def _split_hi_lo(v):
    hi = v.astype(BF16)
    lo = (v - hi.astype(F32)).astype(BF16)
    return jnp.concatenate([hi, lo], axis=1)


def _rmsnorm_fwd(x, gain, tm, name):
    s, d = x.shape

    def body(x_ref, g_ref, o_ref):
        xv = x_ref[...]
        r = lax.rsqrt(jnp.mean(xv * xv, axis=-1, keepdims=True) + EPS)
        o_ref[...] = ((xv * r) * g_ref[...]).astype(o_ref.dtype)

    return pl.pallas_call(
        body, name=name, grid=(s // tm,),
        in_specs=[pl.BlockSpec((tm, d), lambda i: (i, 0)), pl.BlockSpec((1, d), lambda i: (0, 0))],
        out_specs=pl.BlockSpec((tm, d), lambda i: (i, 0)),
        out_shape=jax.ShapeDtypeStruct((s, d), BF16),
        compiler_params=_cparams("parallel"),
    )(x, gain)


def _rmsnorm_bwd(dh, x, gain, dres, tm, name, travel=()):
    s, d = x.shape
    nsteps = s // tm

    def body(dh_ref, x_ref, g_ref, dres_ref, dx_ref, dxb_ref, dg_ref):
        i = pl.program_id(0)
        xv = x_ref[...]
        r = lax.rsqrt(jnp.mean(xv * xv, axis=-1, keepdims=True) + EPS)
        xhat = xv * r
        dhv = dh_ref[...]
        dxh = dhv * g_ref[...]
        proj = jnp.mean(dxh * xhat, axis=-1, keepdims=True)
        dxv = dres_ref[...] + r * (dxh - xhat * proj)
        dx_ref[...] = dxv
        dxb_ref[...] = dxv.astype(dxb_ref.dtype)
        part = jnp.sum((dhv * xhat).reshape(tm // 8, 8, d), axis=0)

        @pl.when(i == 0)
        def _():
            dg_ref[...] = part

        @pl.when(i > 0)
        def _():
            dg_ref[...] += part

        @pl.when(i == nsteps - 1)
        def _():
            dg_ref[...] = jnp.broadcast_to(jnp.sum(dg_ref[...], axis=0, keepdims=True), (8, d))

    row = pl.BlockSpec((tm, d), lambda i: (i, 0))
    body, more_in, more_out, more_shapes, more_scratch = _host_exchange(body, 4, 3, travel, (nsteps,))
    outs = pl.pallas_call(
        body, name=name, grid=(nsteps,),
        in_specs=[row, row, pl.BlockSpec((1, d), lambda i: (0, 0)), row] + more_in,
        out_specs=[row, row, pl.BlockSpec((8, d), lambda i: (0, 0))] + more_out,
        out_shape=[jax.ShapeDtypeStruct((s, d), F32), jax.ShapeDtypeStruct((s, d), BF16),
                   jax.ShapeDtypeStruct((8, d), F32)] + more_shapes,
        scratch_shapes=more_scratch,
        compiler_params=_cparams("arbitrary"),
    )(dh, x, gain, dres, *travel)
    return outs[0], outs[1], outs[2], list(outs[3:])


def _group_mean_matrix():
    r = lax.broadcasted_iota(jnp.int32, (LANES, LANES), 0) // HEAD_DIM
    c = lax.broadcasted_iota(jnp.int32, (LANES, LANES), 1) // HEAD_DIM
    return jnp.where(r == c, 1.0 / HEAD_DIM, 0.0).astype(BF16)


def _group_mean(v, gm):
    hi = v.astype(BF16)
    lo = (v - hi.astype(F32)).astype(BF16)
    return _dot(hi, gm, NN) + _dot(lo, gm, NN)


def _qknorm_fwd(proj, gains, tm, name):
    s = proj.shape[0]
    ncol = gains.shape[1] // LANES

    def body(p_ref, g_ref, gm_ref, o_ref):
        xv = p_ref[...].astype(F32)
        r = lax.rsqrt(_group_mean(xv * xv, gm_ref[...]) + EPS)
        o_ref[...] = ((xv * r) * g_ref[...]).astype(o_ref.dtype)

    blk = pl.BlockSpec((tm, LANES), lambda i, j: (i, j))
    return pl.pallas_call(
        body, name=name, grid=(s // tm, ncol),
        in_specs=[blk, pl.BlockSpec((1, LANES), lambda i, j: (0, j)),
                  pl.BlockSpec((LANES, LANES), lambda i, j: (0, 0))],
        out_specs=blk,
        out_shape=jax.ShapeDtypeStruct((s, ncol * LANES), BF16),
        compiler_params=_cparams("parallel", "parallel"),
    )(proj, gains, _group_mean_matrix())


def _qknorm_bwd(dqk, proj, gains, tm, name):
    s = proj.shape[0]
    ncol = gains.shape[1] // LANES
    nsteps = s // tm

    def body(dy_ref, p_ref, g_ref, gm_ref, dx_ref, dg_ref):
        i = pl.program_id(1)
        gm = gm_ref[...]
        xv = p_ref[...].astype(F32)
        r = lax.rsqrt(_group_mean(xv * xv, gm) + EPS)
        xhat = xv * r
        dy = dy_ref[...]
        dxh = dy * g_ref[...]
        proj_ = _group_mean(dxh * xhat, gm)
        dx_ref[...] = (r * (dxh - xhat * proj_)).astype(dx_ref.dtype)
        part = jnp.sum((dy * xhat).reshape(tm // 8, 8, LANES), axis=0)

        @pl.when(i == 0)
        def _():
            dg_ref[...] = part

        @pl.when(i > 0)
        def _():
            dg_ref[...] += part

        @pl.when(i == nsteps - 1)
        def _():
            dg_ref[...] = jnp.broadcast_to(jnp.sum(dg_ref[...], axis=0, keepdims=True), (8, LANES))

    blk = pl.BlockSpec((tm, LANES), lambda j, i: (i, j))
    return pl.pallas_call(
        body, name=name, grid=(ncol, nsteps),
        in_specs=[blk, blk, pl.BlockSpec((1, LANES), lambda j, i: (0, j)),
                  pl.BlockSpec((LANES, LANES), lambda j, i: (0, 0))],
        out_specs=[blk, pl.BlockSpec((8, LANES), lambda j, i: (0, j))],
        out_shape=[jax.ShapeDtypeStruct((s, ncol * LANES), BF16),
                   jax.ShapeDtypeStruct((8, ncol * LANES), F32)],
        compiler_params=_cparams("parallel", "arbitrary"),
    )(dqk, proj, gains, _group_mean_matrix())


CONV_ROWS = 256
HALO = 8


def _conv_fwd(proj, conv_w8, name):
    s = proj.shape[0]
    nblk = conv_w8.shape[1] // LANES
    first = 3 * nblk
    nchunk = s // CONV_ROWS

    def body(cb_ref, cc_ref, cu_ref, w_ref, y_ref, hpad):
        hpad[pl.ds(0, 2 * HALO), :] = jnp.zeros((2 * HALO, LANES), F32)

        def fill(i, _):
            r0 = pl.multiple_of(i * CONV_ROWS, CONV_ROWS)
            hpad[pl.ds(r0 + 2 * HALO, CONV_ROWS), :] = (
                cc_ref[pl.ds(r0, CONV_ROWS), :].astype(F32) * cu_ref[pl.ds(r0, CONV_ROWS), :].astype(F32))
            return 0

        lax.fori_loop(0, nchunk, fill, 0)
        w0, w1, w2 = w_ref[0:1, :], w_ref[1:2, :], w_ref[2:3, :]

        def conv(i, _):
            r0 = pl.multiple_of(i * CONV_ROWS, CONV_ROWS)
            win = hpad[pl.ds(r0 + HALO, CONV_ROWS + HALO), :]
            c = (w2 * win[HALO:] + w1 * pltpu.roll(win, 1, 0)[HALO:] + w0 * pltpu.roll(win, 2, 0)[HALO:])
            y_ref[pl.ds(r0, CONV_ROWS), :] = (cb_ref[pl.ds(r0, CONV_ROWS), :].astype(F32) * c).astype(y_ref.dtype)
            return 0

        lax.fori_loop(0, nchunk, conv, 0)

    def col(off):
        return pl.BlockSpec((s, LANES), lambda j: (0, off + j))

    return pl.pallas_call(
        body, name=name, grid=(nblk,),
        in_specs=[col(first), col(first + nblk), col(first + 2 * nblk), pl.BlockSpec((8, LANES), lambda j: (0, j))],
        out_specs=pl.BlockSpec((s, LANES), lambda j: (0, j)),
        out_shape=jax.ShapeDtypeStruct((s, nblk * LANES), BF16),
        scratch_shapes=[pltpu.VMEM((s + 2 * HALO, LANES), F32)],
        compiler_params=_cparams("parallel"),
    )(proj, proj, proj, conv_w8)


def _conv_bwd(dmix, proj, conv_w8, name):
    s = proj.shape[0]
    nblk = conv_w8.shape[1] // LANES
    first = 3 * nblk
    nchunk = s // CONV_ROWS

    def body(dy_ref, cb_ref, cc_ref, cu_ref, w_ref, dcb_ref, dcc_ref, dcu_ref, dw_ref, hpad, dcpad):
        hpad[pl.ds(0, 2 * HALO), :] = jnp.zeros((2 * HALO, LANES), F32)
        dcpad[pl.ds(s, 2 * HALO), :] = jnp.zeros((2 * HALO, LANES), F32)

        def fill(i, _):
            r0 = pl.multiple_of(i * CONV_ROWS, CONV_ROWS)
            hpad[pl.ds(r0 + 2 * HALO, CONV_ROWS), :] = (
                cc_ref[pl.ds(r0, CONV_ROWS), :].astype(F32) * cu_ref[pl.ds(r0, CONV_ROWS), :].astype(F32))
            return 0

        lax.fori_loop(0, nchunk, fill, 0)
        w0, w1, w2 = w_ref[0:1, :], w_ref[1:2, :], w_ref[2:3, :]

        def fold(v):
            return jnp.sum(v.reshape(CONV_ROWS // 8, 8, LANES), axis=0)

        def first_pass(i, acc):
            a0, a1, a2 = acc
            r0 = pl.multiple_of(i * CONV_ROWS, CONV_ROWS)
            win = hpad[pl.ds(r0 + HALO, CONV_ROWS + HALO), :]
            h0 = win[HALO:]
            h1 = pltpu.roll(win, 1, 0)[HALO:]
            h2 = pltpu.roll(win, 2, 0)[HALO:]
            c = w2 * h0 + w1 * h1 + w0 * h2
            dy = dy_ref[pl.ds(r0, CONV_ROWS), :]
            dcb_ref[pl.ds(r0, CONV_ROWS), :] = (dy * c).astype(dcb_ref.dtype)
            dc = dy * cb_ref[pl.ds(r0, CONV_ROWS), :].astype(F32)
            dcpad[pl.ds(r0, CONV_ROWS), :] = dc
            return a0 + fold(dc * h2), a1 + fold(dc * h1), a2 + fold(dc * h0)

        z8 = jnp.zeros((8, LANES), F32)
        a0, a1, a2 = lax.fori_loop(0, nchunk, first_pass, (z8, z8, z8))
        dw_ref[...] = jnp.concatenate(
            [jnp.sum(a0, axis=0, keepdims=True), jnp.sum(a1, axis=0, keepdims=True),
             jnp.sum(a2, axis=0, keepdims=True), jnp.zeros((5, LANES), F32)], axis=0)

        def second_pass(i, _):
            r0 = pl.multiple_of(i * CONV_ROWS, CONV_ROWS)
            win = dcpad[pl.ds(r0, CONV_ROWS + HALO), :]
            n = CONV_ROWS + HALO
            dh = (w2 * win[:CONV_ROWS] + w1 * pltpu.roll(win, n - 1, 0)[:CONV_ROWS]
                  + w0 * pltpu.roll(win, n - 2, 0)[:CONV_ROWS])
            dcc_ref[pl.ds(r0, CONV_ROWS), :] = (dh * cu_ref[pl.ds(r0, CONV_ROWS), :].astype(F32)).astype(dcc_ref.dtype)
            dcu_ref[pl.ds(r0, CONV_ROWS), :] = (dh * cc_ref[pl.ds(r0, CONV_ROWS), :].astype(F32)).astype(dcu_ref.dtype)
            return 0

        lax.fori_loop(0, nchunk, second_pass, 0)

    def col(off):
        return pl.BlockSpec((s, LANES), lambda j: (0, off + j))

    out = pl.BlockSpec((s, LANES), lambda j: (0, j))
    return pl.pallas_call(
        body, name=name, grid=(nblk,),
        in_specs=[col(nblk), col(first), col(first + nblk), col(first + 2 * nblk),
                  pl.BlockSpec((8, LANES), lambda j: (0, j))],
        out_specs=[out, out, out, pl.BlockSpec((8, LANES), lambda j: (0, j))],
        out_shape=[jax.ShapeDtypeStruct((s, nblk * LANES), BF16)] * 3 + [jax.ShapeDtypeStruct((8, nblk * LANES), F32)],
        scratch_shapes=[pltpu.VMEM((s + 2 * HALO, LANES), F32), pltpu.VMEM((s + 2 * HALO, LANES), F32)],
        compiler_params=_cparams("parallel"),
    )(dmix, proj, proj, proj, conv_w8)


LOG2E = 1.4426950408889634
LN2 = 0.6931471805599453
NEG_BIG = -1e30
SATURATED = 160.0


def _cumsum_matrix(kind):
    j = lax.broadcasted_iota(jnp.int32, (KEY_CHUNK, 2 * KEY_CHUNK), 0)
    c = lax.broadcasted_iota(jnp.int32, (KEY_CHUNK, 2 * KEY_CHUNK), 1)
    tri = {"after": j > c, "upto": j <= c, "before": j < c}[kind]
    return jnp.where((c >= KEY_CHUNK) | tri, 1.0, 0.0).astype(BF16)


def _stack_heads(t, m0):
    zero = jnp.zeros_like(t)
    return jnp.concatenate([jnp.where(m0, t, zero), jnp.where(m0, zero, t)], axis=0)


def _softplus2(z):
    sp = jnp.maximum(z, 0.0) + jnp.log2(1.0 + jnp.exp2(-jnp.abs(z)))
    return sp, z - sp


def _key_chunk(ref, kc):
    return ref[pl.ds(pl.multiple_of(kc * KEY_CHUNK, KEY_CHUNK), KEY_CHUNK), :]


def _attn_bwd(qk, proj, dmix, rtot, used, tq, name, travel=()):
    s = qk.shape[0]
    nhp = qk.shape[1] // (2 * LANES)
    nc = tq // KEY_CHUNK
    nq = s // tq
    nt = len(travel)

    def body(used_ref, q_ref, k_ref, v_ref, do_ref, r_ref, cmi_ref, cme_ref, bias_ref, dq_ref, dk_ref, dv_ref,
             z_refs, ls_refs, sig_refs, sp_refs, gb_refs, pr_ref, gs_ref, copies):
        qi = pl.program_id(1)

        @pl.when(qi == 0)
        def _():
            dk_ref[...] = jnp.zeros_like(dk_ref)
            dv_ref[...] = jnp.zeros_like(dv_ref)

        if copies is not None:
            @pl.when(jnp.logical_and(pl.program_id(0) == 0, qi == 0))
            def _():
                _exchange_begin(copies)

        nslots = (qi + 1) * nc
        walked = used_ref[pl.program_id(0), qi].astype(jnp.int32)
        first = jnp.clip(nslots - walked, 0, nslots - nc) // nc * nc
        m0 = lax.broadcasted_iota(jnp.int32, (1, LANES), 1) < HEAD_DIM
        qs = _stack_heads(q_ref[...], m0)
        do = do_ref[...]
        dos = _stack_heads(do.astype(BF16), m0)
        dosl = _stack_heads((do * LN2).astype(BF16), m0)
        cmi = cmi_ref[...]
        cme = cme_ref[...]

        def chunk_at(i):
            return jnp.clip(i, first, nslots - 1)

        def scores(kc):
            return _dot(qs, _key_chunk(k_ref, kc), NT)

        def weights(ls, cs, da, pr, kc):
            a = jnp.exp2(ls - (pr - cs[:, :KEY_CHUNK]))
            gb = (a * da).astype(BF16)
            ks = pl.multiple_of(kc * KEY_CHUNK, KEY_CHUNK)
            dv_ref[pl.ds(ks, KEY_CHUNK), :] += _dot(a, dos, TN)
            return gb, jnp.exp2(ls), pr - cs[:, KEY_CHUNK:]

        def score_grads(gb, sig, cg, gs, dq, kc):
            dzb = (gb.astype(F32) * (1.0 - sig) - sig * (gs + cg[:, :KEY_CHUNK])).astype(BF16)
            ks = pl.multiple_of(kc * KEY_CHUNK, KEY_CHUNK)
            dk_ref[pl.ds(ks, KEY_CHUNK), :] += _dot(dzb, qs, TN)
            dq = dq + _dot(jnp.concatenate([dzb[:tq], dzb[tq:]], axis=1), _stack_heads(_key_chunk(k_ref, kc), m0), NN)
            return gs + cg[:, KEY_CHUNK:], dq

        def step(i, par, bias=None):
            cur, prv = par, 1 - par
            k1, k2 = chunk_at(i - 1), chunk_at(i - 2)
            z_next = scores(chunk_at(i + 1))
            cs = _dot(sp_refs[prv][...], cmi, NN)
            da = _dot(dosl, _key_chunk(v_ref, k1), NT)
            cg = _dot(gb_refs[cur][...], cme, NN)
            z = z_refs[cur][...]
            if bias is not None:
                z = z + bias
            sp, ls = _softplus2(z)
            sp_refs[cur][...] = sp.astype(BF16)
            ls_refs[cur][...] = ls
            gs, dq = score_grads(gb_refs[cur][...], sig_refs[cur][...], cg, gs_ref[...], dq_ref[...], k2)
            gs_ref[...] = gs
            dq_ref[...] = dq
            gb, sig, pr = weights(ls_refs[prv][...], cs, da, pr_ref[...], k1)
            gb_refs[prv][...] = gb
            sig_refs[prv][...] = sig
            pr_ref[...] = pr
            z_refs[prv][...] = z_next

        pr_ref[...] = jnp.concatenate([r_ref[:, :LANES], r_ref[:, LANES:]], axis=0)
        gs_ref[...] = jnp.zeros((2 * tq, LANES), F32)
        dq_ref[...] = jnp.zeros((tq, LANES), F32)
        z_refs[0][...] = scores(first)
        sp_refs[1][...] = jnp.zeros((2 * tq, LANES), BF16)
        ls_refs[1][...] = jnp.full((2 * tq, LANES), NEG_BIG, F32)
        gb_refs[0][...] = jnp.zeros((2 * tq, LANES), BF16)
        sig_refs[0][...] = jnp.zeros((2 * tq, LANES), F32)

        def two_steps(j, _):
            step(2 * j, 0)
            step(2 * j + 1, 1)
            return 0

        lax.fori_loop(first // 2, nslots // 2 - 1, two_steps, 0)
        step(nslots - 2, 0, bias_ref[0])
        step(nslots - 1, 1, bias_ref[1])
        k1, k2 = chunk_at(nslots - 1), chunk_at(nslots - 2)
        gb, sig, _ = weights(ls_refs[1][...], _dot(sp_refs[1][...], cmi, NN),
                             _dot(dosl, _key_chunk(v_ref, k1), NT), pr_ref[...], k1)
        gb2 = gb_refs[0][...]
        gs, dq = score_grads(gb2, sig_refs[0][...], _dot(gb2, cme, NN), gs_ref[...], dq_ref[...], k2)
        _, dq = score_grads(gb, sig, _dot(gb, cme, NN), gs, dq, k1)
        dq_ref[...] = dq

        if copies is not None:
            @pl.when(jnp.logical_and(pl.program_id(0) == nhp - 1, qi == nq - 1))
            def _():
                _exchange_finish(copies)

    def wrapped(*refs):
        ins, rest = refs[:9], refs[9:]
        srcs, rest = rest[:nt], rest[nt:]
        outs, rest = rest[:3], rest[3:]
        lands, rest = rest[:nt], rest[nt:]
        z0, z1, ls0, ls1, sg0, sg1, sp0, sp1, gb0, gb1, pr_ref, gs_ref = rest[:12]
        copies = _exchange_copies(srcs, lands, *rest[12:]) if nt else None
        body(*ins, *outs, (z0, z1), (ls0, ls1), (sg0, sg1), (sp0, sp1), (gb0, gb1), pr_ref, gs_ref, copies)

    assert nc == 2
    bias = _diag_bias(tq, True)
    bias = jnp.concatenate([bias[:, :, :KEY_CHUNK], bias[:, :, KEY_CHUNK:]], axis=1)
    qblk = pl.BlockSpec((tq, LANES), lambda p, i: (i, p))
    full = pl.BlockSpec((s, LANES), lambda p, i: (0, p))
    cmspec = pl.BlockSpec((KEY_CHUNK, 2 * KEY_CHUNK), lambda p, i: (0, 0))
    anyspec = pl.BlockSpec(memory_space=pl.ANY)
    shape = jax.ShapeDtypeStruct((s, nhp * LANES), F32)
    f32buf = pltpu.VMEM((2 * tq, LANES), F32)
    bf16buf = pltpu.VMEM((2 * tq, LANES), BF16)
    outs = pl.pallas_call(
        wrapped, name=name, grid=(nhp, nq),
        in_specs=[pl.BlockSpec(memory_space=pltpu.SMEM),
                  qblk,
                  pl.BlockSpec((s, LANES), lambda p, i: (0, nhp + p)),
                  pl.BlockSpec((s, LANES), lambda p, i: (0, 2 * nhp + p)),
                  qblk,
                  pl.BlockSpec((tq, 2 * LANES), lambda p, i: (i, p)),
                  cmspec, cmspec,
                  pl.BlockSpec((nc, 2 * tq, LANES), lambda p, i: (0, 0, 0))] + [anyspec] * nt,
        out_specs=[qblk, full, full] + [anyspec] * nt,
        out_shape=[shape, shape, shape] + [jax.ShapeDtypeStruct(t.shape, t.dtype) for t in travel],
        scratch_shapes=[f32buf] * 6 + [bf16buf] * 4 + [f32buf] * 2 + (_exchange_scratch(nt) if nt else []),
        compiler_params=_cparams("arbitrary", "arbitrary"),
    )(used, qk, qk, proj, dmix, rtot, _cumsum_matrix("upto"), _cumsum_matrix("before"), bias, *travel)
    return outs[0], outs[1], outs[2], list(outs[3:])


def _pair_cumsum_matrix(kind):
    j = lax.broadcasted_iota(jnp.int32, (2 * KEY_CHUNK, 4 * KEY_CHUNK), 0)
    c = lax.broadcasted_iota(jnp.int32, (2 * KEY_CHUNK, 4 * KEY_CHUNK), 1)
    same_head = (j // KEY_CHUNK) == ((c // KEY_CHUNK) % 2)
    jj, cc = j % KEY_CHUNK, c % KEY_CHUNK
    tri = {"after": jj > cc, "upto": jj <= cc, "before": jj < cc}[kind]
    return jnp.where(same_head & ((c >= 2 * KEY_CHUNK) | tri), 1.0, 0.0).astype(BF16)


def _diag_bias(tq, ascending):
    nc = tq // KEY_CHUNK
    shape = (nc, tq, 2 * KEY_CHUNK)
    d = lax.broadcasted_iota(jnp.int32, shape, 0)
    r = lax.broadcasted_iota(jnp.int32, shape, 1)
    c = lax.broadcasted_iota(jnp.int32, shape, 2) % KEY_CHUNK
    chunk = d if ascending else nc - 1 - d
    return jnp.where(chunk * KEY_CHUNK + c < r, 0.0, NEG_BIG).astype(F32)


def _attn_fwd(qk, proj, tq, name, shards=()):
    s = qk.shape[0]
    nhp = qk.shape[1] // (2 * LANES)
    nc = tq // KEY_CHUNK
    nq = s // tq
    ng = len(shards)
    assert nc == 2
    w = 2 * KEY_CHUNK

    def body(q_ref, k_ref, v_ref, cm_ref, bias_ref, o_ref, r_ref, used_ref, z_refs, ls_refs, cs_refs, ct_refs,
             sp_refs, ab_refs, acc_ref, gather):
        qi = pl.program_id(1)
        if gather is not None:
            @pl.when(jnp.logical_and(pl.program_id(0) == 0, qi == 0))
            def _():
                gather.begin()

        nslots = (qi + 1) * nc
        m0 = lax.broadcasted_iota(jnp.int32, (1, LANES), 1) < HEAD_DIM
        q = q_ref[...]
        cm = cm_ref[...]

        def chunk_at(i):
            return jnp.clip(nslots - 1 - i, 0, nslots - 1)

        def scores(kc):
            return _dot(q, _stack_heads(_key_chunk(k_ref, kc), m0), NT)

        def values(ab, kc):
            return _dot(ab, _stack_heads(_key_chunk(v_ref, kc), m0), NN)

        def step(i, par, bias=None, stages="zscwv"):
            cur, prv = par, 1 - par
            if "z" in stages:
                z_next = scores(chunk_at(i + 1))
            if "c" in stages:
                cs = _dot(sp_refs[prv][...], cm, NN)
            if "v" in stages:
                pv = values(ab_refs[prv][...], chunk_at(i - 3))
            if "w" in stages:
                rs = r_ref[...]
                r_ref[...] = rs + ct_refs[cur][...]
                ab_refs[cur][...] = jnp.exp2(ls_refs[cur][...] - cs_refs[cur][...] - rs).astype(BF16)
            if "s" in stages:
                z = z_refs[cur][...]
                if bias is not None:
                    z = z + bias
                sp, ls = _softplus2(z)
                sp_refs[cur][...] = sp.astype(BF16)
                ls_refs[cur][...] = ls
            if "v" in stages:
                acc_ref[...] += pv
            if "c" in stages:
                cs_refs[prv][...] = cs[:, :w]
                ct_refs[prv][...] = cs[:, w:]
            if "z" in stages:
                z_refs[prv][...] = z_next

        z_refs[0][...] = scores(chunk_at(0))
        for p in range(2):
            sp_refs[p][...] = jnp.zeros((tq, w), BF16)
            ls_refs[p][...] = jnp.full((tq, w), NEG_BIG, F32)
            cs_refs[p][...] = jnp.zeros((tq, w), F32)
            ct_refs[p][...] = jnp.zeros((tq, w), F32)
            ab_refs[p][...] = jnp.zeros((tq, w), BF16)
        r_ref[...] = jnp.zeros((tq, w), F32)
        acc_ref[...] = jnp.zeros((tq, LANES), F32)
        step(0, 0, bias_ref[0])
        step(1, 1, bias_ref[1])

        def two_steps(carry):
            j, _ = carry
            step(2 * j, 0)
            step(2 * j + 1, 1)
            return j + 1, jnp.min(r_ref[...])

        pairs, low = lax.while_loop(lambda c: jnp.logical_and(c[0] < nslots // 2, c[1] < SATURATED), two_steps,
                                    (jnp.int32(1), jnp.min(r_ref[...])))
        entered = 2 * pairs
        saturated = low >= SATURATED

        @pl.when(saturated)
        def _():
            step(entered, 0, stages="v")

        @pl.when(jnp.logical_not(saturated))
        def _():
            step(entered, 0, stages="cwv")
            step(entered + 1, 1, stages="wv")
            step(entered + 2, 0, stages="v")

        o_ref[...] = acc_ref[...].astype(o_ref.dtype)
        used_ref[pl.program_id(0), qi] = jnp.where(saturated, entered - 2, entered).astype(F32)

        if gather is not None:
            @pl.when(jnp.logical_and(pl.program_id(0) == nhp - 1, qi == nq // 2))
            def _():
                gather.relay()

            @pl.when(jnp.logical_and(pl.program_id(0) == nhp - 1, qi == nq - 1))
            def _():
                gather.finish()

    def wrapped(*refs):
        ins, rest = refs[:5], refs[5:]
        srcs, rest = rest[:ng], rest[ng:]
        outs, rest = rest[:3], rest[3:]
        dsts, scratch = rest[:ng], rest[ng:]
        z, ls, cs, ct, sp, ab = [scratch[2 * j:2 * j + 2] for j in range(6)]
        gather = _Gather(srcs, dsts, *scratch[13:]) if ng else None
        body(*ins, *outs, z, ls, cs, ct, sp, ab, scratch[12], gather)

    f32buf = pltpu.VMEM((tq, w), F32)
    bf16buf = pltpu.VMEM((tq, w), BF16)
    anyspec = pl.BlockSpec(memory_space=pl.ANY)
    outs = pl.pallas_call(
        wrapped, name=name, grid=(nhp, nq),
        in_specs=[pl.BlockSpec((tq, LANES), lambda p, i: (i, p)),
                  pl.BlockSpec((s, LANES), lambda p, i: (0, nhp + p)),
                  pl.BlockSpec((s, LANES), lambda p, i: (0, 2 * nhp + p)),
                  pl.BlockSpec((w, 2 * w), lambda p, i: (0, 0)),
                  pl.BlockSpec((nc, tq, w), lambda p, i: (0, 0, 0))] + [anyspec] * ng,
        out_specs=[pl.BlockSpec((tq, LANES), lambda p, i: (i, p)),
                   pl.BlockSpec((tq, w), lambda p, i: (i, p)),
                   pl.BlockSpec(memory_space=pltpu.SMEM)] + [anyspec] * ng,
        out_shape=[jax.ShapeDtypeStruct((s, nhp * LANES), BF16),
                   jax.ShapeDtypeStruct((s, nhp * w), F32),
                   jax.ShapeDtypeStruct((nhp, nq), F32)] + _gathered_shapes(shards),
        scratch_shapes=([f32buf] * 8 + [bf16buf] * 4 + [pltpu.VMEM((tq, LANES), F32)]
                        + (_gather_scratch(ng) if ng else [])),
        compiler_params=_cparams("arbitrary", "arbitrary"),
    )(qk, qk, proj, _pair_cumsum_matrix("after"), _diag_bias(tq, False), *shards)
    return outs[0], outs[1], outs[2], list(outs[3:])


BLOCK_PAIR = 2


def _side_by_side(b_ref):
    return jnp.concatenate([b_ref[p] for p in range(BLOCK_PAIR)], axis=1)


def _mm_blocks(h, ga, widx, tm, name):
    s, d = h.shape
    nb, cols = ga.shape[1], ga.shape[3]

    def body(a_ref, b_ref, o_ref):
        o_ref[...] = _dot(a_ref[...], _side_by_side(b_ref), NN).astype(o_ref.dtype)

    return pl.pallas_call(
        body, name=name, grid=(s // tm, nb // BLOCK_PAIR),
        in_specs=[pl.BlockSpec((tm, d), lambda i, j: (i, 0)),
                  pl.BlockSpec((None, BLOCK_PAIR, d, cols), lambda i, j: (widx, j, 0, 0))],
        out_specs=pl.BlockSpec((tm, BLOCK_PAIR * cols), lambda i, j: (i, j)),
        out_shape=jax.ShapeDtypeStruct((s, nb * cols), BF16),
        compiler_params=_cparams("parallel", "arbitrary"),
    )(h, ga)


def _mm_swiglu(h, ga, gidx, uidx, tm, name):
    s, d = h.shape
    nb, cols = ga.shape[1], ga.shape[3]

    def body(a_ref, bg_ref, bu_ref, g_ref, u_ref, act_ref):
        a = a_ref[...]
        g = _dot(a, _side_by_side(bg_ref), NN)
        u = _dot(a, _side_by_side(bu_ref), NN)
        g_ref[...] = g.astype(g_ref.dtype)
        u_ref[...] = u.astype(u_ref.dtype)
        act_ref[...] = (g * (1.0 / (1.0 + jnp.exp(-g))) * u).astype(act_ref.dtype)

    def wspec(idx):
        return pl.BlockSpec((None, BLOCK_PAIR, d, cols), lambda i, j: (idx, j, 0, 0))

    out = pl.BlockSpec((tm, BLOCK_PAIR * cols), lambda i, j: (i, j))
    shape = jax.ShapeDtypeStruct((s, nb * cols), BF16)
    return pl.pallas_call(
        body, name=name, grid=(s // tm, nb // BLOCK_PAIR),
        in_specs=[pl.BlockSpec((tm, d), lambda i, j: (i, 0)), wspec(gidx), wspec(uidx)],
        out_specs=[out, out, out], out_shape=[shape, shape, shape],
        compiler_params=_cparams("parallel", "arbitrary"),
    )(h, ga, ga)


def _mm_residual(a, w3, lidx, res, tm, tn, name):
    s, k = a.shape
    n = w3.shape[2]

    def body(a_ref, b_ref, r_ref, o_ref):
        o_ref[...] = r_ref[...] + _dot(a_ref[...], b_ref[...], NN)

    return pl.pallas_call(
        body, name=name, grid=(s // tm, n // tn),
        in_specs=[pl.BlockSpec((tm, k), lambda i, j: (i, 0)),
                  pl.BlockSpec((None, k, tn), lambda i, j: (lidx, 0, j)),
                  pl.BlockSpec((tm, tn), lambda i, j: (i, j))],
        out_specs=pl.BlockSpec((tm, tn), lambda i, j: (i, j)),
        out_shape=jax.ShapeDtypeStruct((s, n), F32),
        compiler_params=_cparams("parallel", "arbitrary"),
    )(a, w3, res)


def _mm_nt(a, w3, lidx, tm, tn, name):
    s, k = a.shape
    n = w3.shape[1]

    def body(a_ref, b_ref, o_ref):
        o_ref[...] = _dot(a_ref[...], b_ref[...], NT)

    return pl.pallas_call(
        body, name=name, grid=(s // tm, n // tn),
        in_specs=[pl.BlockSpec((tm, k), lambda i, j: (i, 0)),
                  pl.BlockSpec((None, tn, k), lambda i, j: (lidx, j, 0))],
        out_specs=pl.BlockSpec((tm, tn), lambda i, j: (i, j)),
        out_shape=jax.ShapeDtypeStruct((s, n), F32),
        compiler_params=_cparams("parallel", "arbitrary"),
    )(a, w3)


def _mm_nt_swiglu_bwd(dx, wd3, lidx, g, u, tm, name):
    s, d = dx.shape
    cols = BLOCK_PAIR * (g.shape[1] // N_DEV)

    def body(a_ref, b_ref, g_ref, u_ref, dg_ref, du_ref):
        dact = _dot(a_ref[...], b_ref[...], NT)
        gv = g_ref[...].astype(F32)
        sig = 1.0 / (1.0 + jnp.exp(-gv))
        du_ref[...] = (dact * (gv * sig)).astype(du_ref.dtype)
        dg_ref[...] = (dact * u_ref[...].astype(F32) * (sig * (1.0 + gv * (1.0 - sig)))).astype(dg_ref.dtype)

    blk = pl.BlockSpec((tm, cols), lambda i, j: (i, j))
    shape = jax.ShapeDtypeStruct(g.shape, BF16)
    return pl.pallas_call(
        body, name=name, grid=(s // tm, N_DEV // BLOCK_PAIR),
        in_specs=[pl.BlockSpec((tm, d), lambda i, j: (i, 0)),
                  pl.BlockSpec((None, cols, d), lambda i, j: (lidx, j, 0)), blk, blk],
        out_specs=[blk, blk], out_shape=[shape, shape],
        compiler_params=_cparams("parallel", "arbitrary"),
    )(dx, wd3, g, u)


def _mm_nt_blocks(das, ga, widxs, tm, name, travel=()):
    s = das[0].shape[0]
    nb, d, cols = ga.shape[1], ga.shape[2], ga.shape[3]
    nw = len(das)

    def body(*refs):
        a_refs, b_refs, o_ref = refs[:nw], refs[nw:2 * nw], refs[2 * nw]
        acc = None
        wide = BLOCK_PAIR * cols
        for w in range(nw):
            for k in range(nb // BLOCK_PAIR):
                b = jnp.concatenate([b_refs[w][BLOCK_PAIR * k + p] for p in range(BLOCK_PAIR)], axis=1)
                part = _dot(a_refs[w][:, k * wide:(k + 1) * wide], b, NT)
                acc = part if acc is None else acc + part
        o_ref[...] = acc

    def wspec(idx):
        return pl.BlockSpec((None, nb, d, cols), lambda i: (idx, 0, 0, 0), pipeline_mode=pl.Buffered(1))

    grid = (s // tm,)
    body, more_in, more_out, more_shapes, more_scratch = _host_exchange(body, 2 * nw, 1, travel, grid)
    outs = pl.pallas_call(
        body, name=name, grid=grid,
        in_specs=[pl.BlockSpec((tm, nb * cols), lambda i: (i, 0))] * nw + [wspec(i) for i in widxs] + more_in,
        out_specs=[pl.BlockSpec((tm, d), lambda i: (i, 0))] + more_out,
        out_shape=[jax.ShapeDtypeStruct((s, d), F32)] + more_shapes,
        scratch_shapes=more_scratch,
        compiler_params=_cparams("arbitrary"),
    )(*das, *([ga] * nw), *travel)
    return outs[0], list(outs[1:])


def _mm_tn(a, b, ta, tb, tk, out_blocks, name, travel=()):
    s, ka = a.shape
    nb = b.shape[1]
    nk = s // tk
    cols = tb
    if out_blocks:
        tb = BLOCK_PAIR * cols

    def body(a_ref, b_ref, o_ref, ob_ref):
        k = pl.program_id(2)
        part = _dot(a_ref[...], b_ref[...], TN)

        def put(first):
            if out_blocks:
                for p in range(BLOCK_PAIR):
                    piece = part[:, p * cols:(p + 1) * cols]
                    o_ref[p] = piece if first else o_ref[p] + piece
            else:
                o_ref[...] = part if first else o_ref[...] + part

        @pl.when(k == 0)
        def _():
            put(True)

        @pl.when(k > 0)
        def _():
            put(False)

        @pl.when(k == nk - 1)
        def _():
            ob_ref[...] = o_ref[...].astype(ob_ref.dtype)

    if out_blocks:
        out_spec = pl.BlockSpec((BLOCK_PAIR, ta, cols), lambda i, j, k: (j, i, 0))
        shape = (nb // cols, ka, cols)
    else:
        out_spec = pl.BlockSpec((ta, tb), lambda i, j, k: (i, j))
        shape = (ka, nb)
    grid = (ka // ta, nb // tb, nk)
    body, more_in, more_out, more_shapes, more_scratch = _host_exchange(body, 2, 2, travel, grid)
    outs = pl.pallas_call(
        body, name=name, grid=grid,
        in_specs=[pl.BlockSpec((tk, ta), lambda i, j, k: (k, i)),
                  pl.BlockSpec((tk, tb), lambda i, j, k: (k, j))] + more_in,
        out_specs=[out_spec, out_spec] + more_out,
        out_shape=[jax.ShapeDtypeStruct(shape, F32), jax.ShapeDtypeStruct(shape, BF16)] + more_shapes,
        scratch_shapes=more_scratch,
        compiler_params=_cparams("arbitrary", "arbitrary", "arbitrary"),
    )(a, b, *travel)
    return (outs[0], outs[1]), list(outs[2:])


def _loss_head(y, target, tm, name):
    s, d = y.shape
    nsteps = s // tm

    def body(y_ref, t_ref, dy_ref, dyb_ref, l_ref, acc):
        i = pl.program_id(0)
        diff = y_ref[...] - t_ref[...]
        dy_ref[...] = diff * (1.0 / d)
        dyb_ref[...] = (diff * (1.0 / d)).astype(dyb_ref.dtype)
        part = jnp.sum((diff * diff).reshape(tm // 8, 8, d), axis=0)

        @pl.when(i == 0)
        def _():
            acc[...] = part

        @pl.when(i > 0)
        def _():
            acc[...] += part

        @pl.when(i == nsteps - 1)
        def _():
            tot = jnp.sum(jnp.sum(acc[...], axis=1, keepdims=True), axis=0, keepdims=True)
            l_ref[...] = jnp.broadcast_to(tot * (0.5 / d), (8, LANES))

    row = pl.BlockSpec((tm, d), lambda i: (i, 0))
    return pl.pallas_call(
        body, name=name, grid=(nsteps,),
        in_specs=[row, row],
        out_specs=[row, row, pl.BlockSpec((8, LANES), lambda i: (0, 0))],
        out_shape=[jax.ShapeDtypeStruct((s, d), F32), jax.ShapeDtypeStruct((s, d), BF16),
                   jax.ShapeDtypeStruct((8, LANES), F32)],
        scratch_shapes=[pltpu.VMEM((8, d), F32)],
        compiler_params=_cparams("arbitrary"),
    )(y, target)


def _adamw(parts, own, w, m, v, tr, name):
    p, rows, cols = parts.shape
    c1 = 1.0 / (1.0 - ADAM_B1 ** ADAM_STEP)
    c2 = 1.0 / (1.0 - ADAM_B2 ** ADAM_STEP)

    def body(*refs):
        if own is None:
            p_ref, w_ref, m_ref, v_ref, g_ref, d_ref, nm_ref, nv_ref = refs
            g = p_ref[0]
            for k in range(1, p):
                g = g + p_ref[k]
        else:
            p_ref, own_ref, w_ref, m_ref, v_ref, g_ref, d_ref, nm_ref, nv_ref = refs
            x, y, c = _place()
            my = 4 * x + 2 * y + c
            mine = own_ref[...]
            g = jnp.where(my == 0, mine, p_ref[0].astype(F32))
            for k in range(1, p):
                g = g + jnp.where(my == k, mine, p_ref[k].astype(F32))
        nm = ADAM_B1 * m_ref[...] + (1.0 - ADAM_B1) * g
        nv = ADAM_B2 * v_ref[...] + (1.0 - ADAM_B2) * (g * g)
        g_ref[...] = g
        nm_ref[...] = nm
        nv_ref[...] = nv
        d_ref[...] = -ADAM_LR * ((nm * c1) / (jnp.sqrt(nv * c2) + ADAM_EPS) + ADAM_WD * w_ref[...])

    blk = pl.BlockSpec((tr, cols), lambda i: (i, 0))
    shape = jax.ShapeDtypeStruct((rows, cols), F32)
    return pl.pallas_call(
        body, name=name, grid=(rows // tr,),
        in_specs=[pl.BlockSpec((p, tr, cols), lambda i: (0, i, 0))] + [blk] * (3 if own is None else 4),
        out_specs=[blk] * 4, out_shape=[shape] * 4,
        compiler_params=_cparams("parallel"),
    )(*([parts] + ([] if own is None else [own]) + [w, m, v]))


def _adamw_sharded(parts, grads, my, w, m, v, tr, name):
    depth, rows, cols = w.shape
    p, pr, pc = parts[0].shape
    c1 = 1.0 / (1.0 - ADAM_B1 ** ADAM_STEP)
    c2 = 1.0 / (1.0 - ADAM_B2 ** ADAM_STEP)

    def body(my_ref, *refs):
        p_refs, own_refs = refs[:depth], refs[depth:2 * depth]
        w_ref, m_ref, v_ref, g_ref, d_ref, nm_ref, nv_ref = refs[2 * depth:]
        layer = pl.program_id(0)
        for ll in range(depth):
            @pl.when(layer == ll)
            def _(ll=ll):
                mine = own_refs[ll][...]
                g = jnp.where(my_ref[0] == 0, mine, p_refs[ll][0].astype(F32))
                for k in range(1, p):
                    g = g + jnp.where(my_ref[0] == k, mine, p_refs[ll][k].astype(F32))
                g = g[:, :cols]
                nm = ADAM_B1 * m_ref[...] + (1.0 - ADAM_B1) * g
                nv = ADAM_B2 * v_ref[...] + (1.0 - ADAM_B2) * (g * g)
                g_ref[...] = g
                nm_ref[...] = nm
                nv_ref[...] = nv
                d_ref[...] = -ADAM_LR * ((nm * c1) / (jnp.sqrt(nv * c2) + ADAM_EPS) + ADAM_WD * w_ref[...])

    def row_block(ll, l, i):
        return jnp.where(l == ll, i, 0)

    blk = pl.BlockSpec((None, tr, cols), lambda l, i, my_: (l, i, 0))
    shape = jax.ShapeDtypeStruct((depth, rows, cols), F32)
    return pl.pallas_call(
        body, name=name,
        grid_spec=pltpu.PrefetchScalarGridSpec(
            num_scalar_prefetch=1, grid=(depth, rows // tr),
            in_specs=([pl.BlockSpec((p, tr, pc), lambda l, i, my_, ll=ll: (0, row_block(ll, l, i), 0))
                       for ll in range(depth)]
                      + [pl.BlockSpec((None, tr, pc), lambda l, i, my_, ll=ll: (my_[0], row_block(ll, l, i), 0))
                         for ll in range(depth)]
                      + [blk, blk, blk]),
            out_specs=[blk] * 4),
        out_shape=[shape] * 4,
        compiler_params=_cparams("arbitrary", "arbitrary"),
    )(my, *parts, *grads, w, m, v)


def _place():
    x, y, c = lax.axis_index("x"), lax.axis_index("y"), lax.axis_index("c")
    return x, y, c


class _Gather:
    def __init__(self, srcs, dsts, send_sems, recv_sems, local_sems):
        na = len(srcs)
        x, y, c = _place()
        me, sibling = (x, y, c), (x, y, 1 - c)
        chips = [(1 - x, y), (x, 1 - y), (1 - x, 1 - y)]

        def slot(a, dev):
            return dsts[a].at[:, pl.ds(4 * dev[0] + 2 * dev[1] + dev[2], 1)]

        def copy(k, a, block, to, from_shard=False):
            return pltpu.make_async_remote_copy(
                src_ref=srcs[a] if from_shard else slot(a, block), dst_ref=slot(a, block),
                send_sem=send_sems.at[k, a], recv_sem=recv_sems.at[k, a], device_id=to, device_id_type=MESH)

        pairs = [(j, chip, a) for j, chip in enumerate(chips) for a in range(na)]
        self.mine = [pltpu.make_async_copy(srcs[a], slot(a, me), local_sems.at[a]) for a in range(na)]
        self.first = [copy(0, a, me, sibling, True) for a in range(na)]
        self.first += [copy(1 + j, a, me, (*chip, c), True) for j, chip, a in pairs]
        self.over_ici = [copy(1 + j, a, (*chip, c), me) for j, chip, a in pairs]
        self.passed = [copy(4 + j, a, (*chip, c), sibling) for j, chip, a in pairs]
        self.from_sibling = [copy(0, a, sibling, me) for a in range(na)]
        self.from_sibling += [copy(4 + j, a, (*chip, 1 - c), me) for j, chip, a in pairs]

    def begin(self):
        for cp in self.mine + self.first:
            cp.start()

    def relay(self):
        for arrived, onward in zip(self.over_ici, self.passed):
            arrived.wait_recv()
            onward.start()

    def finish(self):
        for cp in self.from_sibling:
            cp.wait_recv()
        for cp in self.first + self.passed:
            cp.wait_send()
        for cp in self.mine:
            cp.wait()


def _gather_scratch(na):
    return [pltpu.SemaphoreType.DMA((7, na)), pltpu.SemaphoreType.DMA((7, na)), pltpu.SemaphoreType.DMA((na,))]


def _gathered_shapes(shards):
    return [jax.ShapeDtypeStruct((a.shape[0], N_DEV) + a.shape[2:], a.dtype) for a in shards]


def _all_gather(shards, name):
    na = len(shards)

    def body(*refs):
        gather = _Gather(refs[:na], refs[na:2 * na], *refs[2 * na:])
        gather.begin()
        gather.relay()
        gather.finish()

    anyspec = pl.BlockSpec(memory_space=pl.ANY)
    return pl.pallas_call(
        body, name=name,
        in_specs=[anyspec] * na, out_specs=[anyspec] * na,
        out_shape=_gathered_shapes(shards), scratch_shapes=_gather_scratch(na),
    )(*shards)


_RELATIONS = [(dx, dy, dc) for dx in (0, 1) for dy in (0, 1) for dc in (0, 1)][1:]


def _flip(v, d):
    return 1 - v if d else v


def _exchange_copies(srcs, dsts, send_sems, recv_sems, local_sems):
    x, y, c = _place()
    my = 4 * x + 2 * y + c
    na = len(srcs)
    mine = [pltpu.make_async_copy(srcs[a].at[pl.ds(my, 1)], dsts[a].at[pl.ds(my, 1)], local_sems.at[a])
            for a in range(na)]
    sends, recvs = [], []
    for k, (dx, dy, dc) in enumerate(_RELATIONS):
        peer = (_flip(x, dx), _flip(y, dy), _flip(c, dc))
        pidx = 4 * peer[0] + 2 * peer[1] + peer[2]
        for a in range(na):
            for into, out in ((my, sends), (pidx, recvs)):
                out.append(pltpu.make_async_remote_copy(
                    src_ref=srcs[a].at[pl.ds(pidx, 1)], dst_ref=dsts[a].at[pl.ds(into, 1)],
                    send_sem=send_sems.at[k, a], recv_sem=recv_sems.at[k, a], device_id=peer, device_id_type=MESH))
    return mine, sends, recvs


def _exchange_begin(copies):
    mine, sends, _ = copies
    for cp in mine + sends:
        cp.start()


def _exchange_finish(copies):
    mine, sends, recvs = copies
    for cp in recvs:
        cp.wait_recv()
    for cp in sends:
        cp.wait_send()
    for cp in mine:
        cp.wait()


def _exchange_scratch(na):
    return [pltpu.SemaphoreType.DMA((7, na)), pltpu.SemaphoreType.DMA((7, na)), pltpu.SemaphoreType.DMA((na,))]


def _host_exchange(body, n_in, n_out, travel, grid):
    nt = len(travel)
    if not nt:
        return body, [], [], [], []

    def wrapped(*refs):
        ins, srcs = refs[:n_in], refs[n_in:n_in + nt]
        outs, rest = refs[n_in + nt:n_in + nt + n_out], refs[n_in + nt + n_out:]
        dsts, scratch = rest[:nt], rest[nt:]
        copies = _exchange_copies(srcs, dsts, *scratch[-3:])
        first = last = None
        for axis, size in enumerate(grid):
            at_start, at_end = pl.program_id(axis) == 0, pl.program_id(axis) == size - 1
            first = at_start if first is None else jnp.logical_and(first, at_start)
            last = at_end if last is None else jnp.logical_and(last, at_end)

        @pl.when(first)
        def _():
            _exchange_begin(copies)

        body(*ins, *outs, *scratch[:-3])

        @pl.when(last)
        def _():
            _exchange_finish(copies)

    anyspec = pl.BlockSpec(memory_space=pl.ANY)
    return (wrapped, [anyspec] * nt, [anyspec] * nt, [jax.ShapeDtypeStruct(t.shape, t.dtype) for t in travel],
            _exchange_scratch(nt))


def _all_reduce_small(v, name):
    r, c_ = v.shape

    def body(v_ref, o_ref, gath, send_sems, recv_sems):
        x, y, c = _place()
        my = 4 * x + 2 * y + c
        gath[my] = v_ref[...]
        sends = []
        for k, (dx, dy, dc) in enumerate(_RELATIONS):
            peer = (_flip(x, dx), _flip(y, dy), _flip(c, dc))
            cp = pltpu.make_async_remote_copy(
                src_ref=v_ref, dst_ref=gath.at[my], send_sem=send_sems.at[k], recv_sem=recv_sems.at[k],
                device_id=peer, device_id_type=MESH)
            cp.start()
            sends.append((cp, 4 * peer[0] + 2 * peer[1] + peer[2], k, peer))
        for cp, pidx, k, peer in sends:
            pltpu.make_async_remote_copy(
                src_ref=v_ref, dst_ref=gath.at[pidx], send_sem=send_sems.at[k], recv_sem=recv_sems.at[k],
                device_id=peer, device_id_type=MESH).wait_recv()
        for cp, *_ in sends:
            cp.wait_send()
        tot = gath[0]
        for k in range(1, N_DEV):
            tot = tot + gath[k]
        o_ref[...] = tot

    vm = pl.BlockSpec(memory_space=pltpu.VMEM)
    return pl.pallas_call(
        body, name=name, in_specs=[vm], out_specs=vm,
        out_shape=jax.ShapeDtypeStruct((r, c_), F32),
        scratch_shapes=[pltpu.VMEM((N_DEV, r, c_), F32), pltpu.SemaphoreType.DMA((7,)),
                        pltpu.SemaphoreType.DMA((7,))],
    )(v)


TM = 512
TM_MATMUL = 2048
TM_RESIDUAL = 1024
TQ = 256


def _device_blocks(t):
    return t.reshape(N_DEV, -1, t.shape[-1])


def _pad_to(a, axis, size):
    pad = [(0, 0)] * a.ndim
    pad[axis] = (0, size - a.shape[axis])
    return jnp.pad(a, pad)


def _local_step(x, target, g_in0, late_shards, conv_full, norm_mix, q_norm, k_norm, norm_ffn):
    depth, d = norm_mix.shape
    cols = g_in0.shape[3]
    tm, tq = min(TM, x.shape[0]), min(TQ, x.shape[0])
    tmm, tmr = min(TM_MATMUL, x.shape[0]), min(TM_RESIDUAL, x.shape[0])
    attn = d // 2
    nheads = attn // HEAD_DIM
    scale = HEAD_DIM ** -0.5 * LOG2E
    saved = []
    for l in range(depth):
        h1 = _rmsnorm_fwd(x, norm_mix[l][None], tm,f"norm_mix_fwd_{l}")
        w_in = (g_in0, 0) if l == 0 else (g_rest, 3 * (l - 1))
        proj = _mm_blocks(h1, *w_in, tmm, f"proj_in_{l}")
        qk_gain = jnp.concatenate([jnp.tile(q_norm[l], nheads) * scale, jnp.tile(k_norm[l], nheads)])[None]
        qk = _qknorm_fwd(proj, qk_gain, tmm, f"qknorm_fwd_{l}")
        o, rtot, used, gathered = _attn_fwd(qk, proj, tq, f"attn_fwd_{l}", late_shards if l == 0 else ())
        if l == 0:
            g_gu0, g_rest, gb, gc = gathered if depth > 1 else (gathered[0], None, *gathered[1:])
            gb = gb.reshape(depth, -1, d)
            gc = gc.reshape(depth, -1, d)
        w_gu = (g_gu0, 0, 1) if l == 0 else (g_rest, 3 * (l - 1) + 1, 3 * (l - 1) + 2)
        conv_w8 = _pad_to(conv_full[l], 0, 8)
        cv = _conv_fwd(proj, conv_w8, f"conv_fwd_{l}")
        mix = jnp.concatenate([o, cv], axis=1)
        x1 = _mm_residual(mix, gb, l, x, tmr, 512, f"proj_out_{l}")
        h2 = _rmsnorm_fwd(x1, norm_ffn[l][None], tm,f"norm_ffn_fwd_{l}")
        g, u, act = _mm_swiglu(h2, *w_gu, tmr, f"ffn_up_{l}")
        x2 = _mm_residual(act, gc, l, x1, tmr, 512, f"ffn_down_{l}")
        saved.append((x, h1, proj, qk_gain, qk, rtot, used, conv_w8, mix, x1, h2, g, u, act, w_in, w_gu))
        x = x2

    dx, dxb, loss = _loss_head(x, target, tm, "loss_head")

    grads = [None] * depth
    small = [None] * depth
    landed = {}
    for l in reversed(range(depth)):
        x0, h1, proj, qk_gain, qk, rtot, used, conv_w8, mix, x1, h2, g, u, act, w_in, w_gu = saved[l]
        d = x0.shape[1]
        dg, du = _mm_nt_swiglu_bwd(dxb, gc, l, g, u, tmr, f"ffn_down_bwd_{l}")
        d_wdown, _ = _mm_tn(act, dxb, 768, d, tmm, False, f"dw_down_{l}")
        d_wgate, _ = _mm_tn(h2, dg, d, cols, tmm, True, f"dw_gate_{l}")
        d_wup, _ = _mm_tn(h2, du, d, cols, tmm, True, f"dw_up_{l}")
        dh2, _ = _mm_nt_blocks([dg, du], w_gu[0], list(w_gu[1:]), tm, f"ffn_up_bwd_{l}")
        dx1, dx1b, dg_ffn, _ = _rmsnorm_bwd(dh2, x1, norm_ffn[l][None], dx, tm, f"norm_ffn_bwd_{l}")
        dmix = _mm_nt(dx1b, gb, l, tmr, 512, f"proj_out_bwd_{l}")
        d_wout, _ = _mm_tn(mix, dx1b, 512, d, tmm, False, f"dw_out_{l}")
        dcb, dcc, dcu, dconv = _conv_bwd(dmix, proj, conv_w8, f"conv_bwd_{l}")
        rides = {}
        if l == 0:
            rides = {5 * ll + j: grads[ll][j][1] for ll in range(1, depth) for j in range(5)}
            rides.update({3: d_wout[1], 4: d_wdown[1]})
        dq, dk, dv, arrived = _attn_bwd(qk, proj, dmix, rtot, used, tq, f"attn_bwd_{l}",
                                        [_device_blocks(t) for t in rides.values()])
        landed.update(zip(rides.keys(), arrived))
        dqk, dg_qk = _qknorm_bwd(jnp.concatenate([dq, dk], axis=1), proj, qk_gain, tmm, f"qknorm_bwd_{l}")
        dproj = jnp.concatenate([dqk, dv.astype(BF16), dcb, dcc, dcu], axis=1)
        first = l == 0
        d_win, arrived = _mm_tn(h1, dproj, d, cols, tmm, True, f"dw_in_{l}",
                                [_device_blocks(d_wgate[1])] if first else [])
        landed.update(zip([1], arrived))
        dh1, arrived = _mm_nt_blocks([dproj], w_in[0], [w_in[1]], tmr, f"proj_in_bwd_{l}",
                                     [_device_blocks(d_win[1])] if first else [])
        landed.update(zip([0], arrived))
        dx, dxb, dg_mix, arrived = _rmsnorm_bwd(dh1, x0, norm_mix[l][None], dx1, tm, f"norm_mix_bwd_{l}",
                                                [_device_blocks(d_wup[1])] if first else [])
        landed.update(zip([2], arrived))
        grads[l] = (d_win, d_wgate, d_wup, d_wout, d_wdown)
        dq_gain = jnp.sum(dg_qk[0, :attn].reshape(nheads, HEAD_DIM), axis=0) * scale
        dk_gain = jnp.sum(dg_qk[0, attn:].reshape(nheads, HEAD_DIM), axis=0)
        small[l] = (dg_mix[0], dg_ffn[0], dq_gain, dk_gain, dconv[:3])
    return loss, dx, grads, small, landed


def kernel(x, norm_mix, w_in, q_norm, k_norm, conv_w, w_out, norm_ffn, w_gate, w_up, w_down, loss_target, m_norm_mix, m_w_in, m_q_norm, m_k_norm, m_conv_w, m_w_out, m_norm_ffn, m_w_gate, m_w_up, m_w_down, v_norm_mix, v_w_in, v_q_norm, v_k_norm, v_conv_w, v_w_out, v_norm_ffn, v_w_gate, v_w_up, v_w_down):
    depth, d, in_shard = w_in.shape
    ff_shard = w_gate.shape[2]
    ff_pad = in_shard
    conv_shard = conv_w.shape[2]
    xs = x.reshape(x.shape[-2], d)
    target = loss_target.reshape(xs.shape)

    pa = jnp.stack([w_in, _pad_to(w_gate, 2, ff_pad), _pad_to(w_up, 2, ff_pad)], axis=1)
    pa = pa.reshape(3 * depth, 1, d, in_shard).astype(BF16)
    pd = _pad_to(_pad_to(conv_w.reshape(depth * 3, conv_shard), 0, 8), 1, LANES)[None, None]
    g_in0, gd = _all_gather([pa[:1], pd], "gather_first")
    conv_full = gd[0, :, :depth * 3, :conv_shard].transpose(1, 0, 2).reshape(depth, 3, N_DEV * conv_shard)
    late_shards = [pa[1:3]] + ([pa[3:]] if depth > 1 else [])
    late_shards += [w_out.astype(BF16)[:, None], _pad_to(w_down, 1, ff_pad).astype(BF16)[:, None]]

    loss, grad_x, grads, small, landed = _local_step(xs, target, g_in0, late_shards, conv_full, norm_mix, q_norm,
                                                     k_norm, norm_ffn)

    x_, y_, c_ = _place()
    my = 4 * x_ + 2 * y_ + c_

    nconv = N_DEV * conv_shard
    rows = []
    for l in range(depth):
        g_mix, g_ffn, g_q, g_k, g_conv = small[l]
        qkrow = _pad_to(jnp.concatenate([g_q, g_k]), 0, d)
        rows += [g_mix[None], g_ffn[None], qkrow[None], _pad_to(g_conv, 1, d)]
    nrow = 6 * depth
    packed = jnp.concatenate(rows + [_pad_to(loss[:1], 1, d)], axis=0)
    packed = _pad_to(packed, 0, ((nrow + 1 + 7) // 8) * 8)
    summed = _all_reduce_small(packed, "reduce_small")
    loss_out = summed[nrow, 0]

    my1 = my.astype(jnp.int32).reshape(1)

    def big(j, w, m, v, tr, name):
        return _adamw_sharded([landed[5 * l + j] for l in range(depth)],
                              [_device_blocks(grads[l][j][0]) for l in range(depth)], my1, w, m, v, tr, name)

    res = {"w_in": big(0, w_in, m_w_in, v_w_in, 256, "adamw_in"),
           "w_gate": big(1, w_gate, m_w_gate, v_w_gate, 256, "adamw_gate"),
           "w_up": big(2, w_up, m_w_up, v_w_up, 256, "adamw_up"),
           "w_out": big(3, w_out, m_w_out, v_w_out, w_out.shape[1], "adamw_out"),
           "w_down": big(4, w_down, m_w_down, v_w_down, ff_shard // 2, "adamw_down")}

    g_rows, w_rows, m_rows, v_rows = [], [], [], []
    for l in range(depth):
        base = l * 6
        conv_g = lax.dynamic_slice(summed[base + 3:base + 6], (0, my * conv_shard), (3, conv_shard))
        g_rows += [summed[base:base + 3], _pad_to(conv_g, 1, d)]
        for dst, (nm, qn, kn, nf, cw) in ((w_rows, (norm_mix, q_norm, k_norm, norm_ffn, conv_w)),
                                          (m_rows, (m_norm_mix, m_q_norm, m_k_norm, m_norm_ffn, m_conv_w)),
                                          (v_rows, (v_norm_mix, v_q_norm, v_k_norm, v_norm_ffn, v_conv_w))):
            dst += [nm[l][None], nf[l][None], _pad_to(jnp.concatenate([qn[l], kn[l]]), 0, d)[None],
                    _pad_to(cw[l], 1, d)]
    prow = ((nrow + 7) // 8) * 8
    gs, ws, ms, vs = [_pad_to(jnp.concatenate(t, axis=0), 0, prow) for t in (g_rows, w_rows, m_rows, v_rows)]
    sm = _adamw(gs[None], None, ws, ms, vs, prow, "adamw_small")

    hd = q_norm.shape[1]

    def small_out(t, kind):
        per_layer = []
        for l in range(depth):
            base = l * 6
            per_layer.append({"norm_mix": t[base], "norm_ffn": t[base + 1], "q_norm": t[base + 2, :hd],
                              "k_norm": t[base + 2, hd:2 * hd], "conv_w": t[base + 3:base + 6, :conv_shard]}[kind])
        return jnp.stack(per_layer)

    def big_out(name, i):
        return res[name][i]

    outs = [loss_out, grad_x.reshape(x.shape)]
    for i in range(4):
        outs += [small_out(sm[i], "norm_mix"), big_out("w_in", i), small_out(sm[i], "q_norm"),
                 small_out(sm[i], "k_norm"), small_out(sm[i], "conv_w"), big_out("w_out", i),
                 small_out(sm[i], "norm_ffn"), big_out("w_gate", i), big_out("w_up", i), big_out("w_down", i)]
    return tuple(outs)
```

```python
import jax
import jax.numpy as jnp
from jax import lax
from jax.experimental import pallas as pl
from jax.experimental.pallas import tpu as pltpu

F32 = jnp.float32
BF16 = jnp.bfloat16
MESH = pl.DeviceIdType.MESH

N_DEV = 8
LANES = 128
HEAD_DIM = 64
KEY_CHUNK = 128
EPS = 1e-6
VMEM_LIMIT = 48 * 1024 * 1024

ADAM_LR = 0.001
ADAM_B1 = 0.9
ADAM_B2 = 0.999
ADAM_EPS = 1e-08
ADAM_WD = 0.01
ADAM_STEP = 10

NN = (((1,), (0,)), ((), ()))
NT = (((1,), (1,)), ((), ()))
TN = (((0,), (0,)), ((), ()))


def _dot(a, b, dims):
    return lax.dot_general(a.astype(BF16), b.astype(BF16), dims, preferred_element_type=F32)


def _cparams(*sem):
    return pltpu.CompilerParams(dimension_semantics=sem, vmem_limit_bytes=VMEM_LIMIT)


def _split_hi_lo(v):
    hi = v.astype(BF16)
    lo = (v - hi.astype(F32)).astype(BF16)
    return jnp.concatenate([hi, lo], axis=1)


def _rmsnorm_fwd(x, gain, tm, name):
    s, d = x.shape

    def body(x_ref, g_ref, o_ref):
        xv = x_ref[...]
        r = lax.rsqrt(jnp.mean(xv * xv, axis=-1, keepdims=True) + EPS)
        o_ref[...] = ((xv * r) * g_ref[...]).astype(o_ref.dtype)

    return pl.pallas_call(
        body, name=name, grid=(s // tm,),
        in_specs=[pl.BlockSpec((tm, d), lambda i: (i, 0)), pl.BlockSpec((1, d), lambda i: (0, 0))],
        out_specs=pl.BlockSpec((tm, d), lambda i: (i, 0)),
        out_shape=jax.ShapeDtypeStruct((s, d), BF16),
        compiler_params=_cparams("parallel"),
    )(x, gain)


def _rmsnorm_bwd(dh, x, gain, dres, tm, name, travel=()):
    s, d = x.shape
    nsteps = s // tm

    def body(dh_ref, x_ref, g_ref, dres_ref, dx_ref, dxb_ref, dg_ref):
        i = pl.program_id(0)
        xv = x_ref[...]
        r = lax.rsqrt(jnp.mean(xv * xv, axis=-1, keepdims=True) + EPS)
        xhat = xv * r
        dhv = dh_ref[...]
        dxh = dhv * g_ref[...]
        proj = jnp.mean(dxh * xhat, axis=-1, keepdims=True)
        dxv = dres_ref[...] + r * (dxh - xhat * proj)
        dx_ref[...] = dxv
        dxb_ref[...] = dxv.astype(dxb_ref.dtype)
        part = jnp.sum((dhv * xhat).reshape(tm // 8, 8, d), axis=0)

        @pl.when(i == 0)
        def _():
            dg_ref[...] = part

        @pl.when(i > 0)
        def _():
            dg_ref[...] += part

        @pl.when(i == nsteps - 1)
        def _():
            dg_ref[...] = jnp.broadcast_to(jnp.sum(dg_ref[...], axis=0, keepdims=True), (8, d))

    row = pl.BlockSpec((tm, d), lambda i: (i, 0))
    body, more_in, more_out, more_shapes, more_scratch = _host_exchange(body, 4, 3, travel, (nsteps,))
    outs = pl.pallas_call(
        body, name=name, grid=(nsteps,),
        in_specs=[row, row, pl.BlockSpec((1, d), lambda i: (0, 0)), row] + more_in,
        out_specs=[row, row, pl.BlockSpec((8, d), lambda i: (0, 0))] + more_out,
        out_shape=[jax.ShapeDtypeStruct((s, d), F32), jax.ShapeDtypeStruct((s, d), BF16),
                   jax.ShapeDtypeStruct((8, d), F32)] + more_shapes,
        scratch_shapes=more_scratch,
        compiler_params=_cparams("arbitrary"),
    )(dh, x, gain, dres, *travel)
    return outs[0], outs[1], outs[2], list(outs[3:])


def _group_mean_matrix():
    r = lax.broadcasted_iota(jnp.int32, (LANES, LANES), 0) // HEAD_DIM
    c = lax.broadcasted_iota(jnp.int32, (LANES, LANES), 1) // HEAD_DIM
    return jnp.where(r == c, 1.0 / HEAD_DIM, 0.0).astype(BF16)


def _group_mean(v, gm):
    hi = v.astype(BF16)
    lo = (v - hi.astype(F32)).astype(BF16)
    return _dot(hi, gm, NN) + _dot(lo, gm, NN)


def _qknorm_fwd(proj, gains, tm, name):
    s = proj.shape[0]
    ncol = gains.shape[1] // LANES

    def body(p_ref, g_ref, gm_ref, o_ref):
        xv = p_ref[...].astype(F32)
        r = lax.rsqrt(_group_mean(xv * xv, gm_ref[...]) + EPS)
        o_ref[...] = ((xv * r) * g_ref[...]).astype(o_ref.dtype)

    blk = pl.BlockSpec((tm, LANES), lambda i, j: (i, j))
    return pl.pallas_call(
        body, name=name, grid=(s // tm, ncol),
        in_specs=[blk, pl.BlockSpec((1, LANES), lambda i, j: (0, j)),
                  pl.BlockSpec((LANES, LANES), lambda i, j: (0, 0))],
        out_specs=blk,
        out_shape=jax.ShapeDtypeStruct((s, ncol * LANES), BF16),
        compiler_params=_cparams("parallel", "parallel"),
    )(proj, gains, _group_mean_matrix())


def _qknorm_bwd(dqk, proj, gains, tm, name):
    s = proj.shape[0]
    ncol = gains.shape[1] // LANES
    nsteps = s // tm

    def body(dy_ref, p_ref, g_ref, gm_ref, dx_ref, dg_ref):
        i = pl.program_id(1)
        gm = gm_ref[...]
        xv = p_ref[...].astype(F32)
        r = lax.rsqrt(_group_mean(xv * xv, gm) + EPS)
        xhat = xv * r
        dy = dy_ref[...]
        dxh = dy * g_ref[...]
        proj_ = _group_mean(dxh * xhat, gm)
        dx_ref[...] = (r * (dxh - xhat * proj_)).astype(dx_ref.dtype)
        part = jnp.sum((dy * xhat).reshape(tm // 8, 8, LANES), axis=0)

        @pl.when(i == 0)
        def _():
            dg_ref[...] = part

        @pl.when(i > 0)
        def _():
            dg_ref[...] += part

        @pl.when(i == nsteps - 1)
        def _():
            dg_ref[...] = jnp.broadcast_to(jnp.sum(dg_ref[...], axis=0, keepdims=True), (8, LANES))

    blk = pl.BlockSpec((tm, LANES), lambda j, i: (i, j))
    return pl.pallas_call(
        body, name=name, grid=(ncol, nsteps),
        in_specs=[blk, blk, pl.BlockSpec((1, LANES), lambda j, i: (0, j)),
                  pl.BlockSpec((LANES, LANES), lambda j, i: (0, 0))],
        out_specs=[blk, pl.BlockSpec((8, LANES), lambda j, i: (0, j))],
        out_shape=[jax.ShapeDtypeStruct((s, ncol * LANES), BF16),
                   jax.ShapeDtypeStruct((8, ncol * LANES), F32)],
        compiler_params=_cparams("parallel", "arbitrary"),
    )(dqk, proj, gains, _group_mean_matrix())


CONV_ROWS = 256
HALO = 8


def _conv_fwd(proj, conv_w8, name):
    s = proj.shape[0]
    nblk = conv_w8.shape[1] // LANES
    first = 3 * nblk
    nchunk = s // CONV_ROWS

    def body(cb_ref, cc_ref, cu_ref, w_ref, y_ref, hpad):
        hpad[pl.ds(0, 2 * HALO), :] = jnp.zeros((2 * HALO, LANES), F32)

        def fill(i, _):
            r0 = pl.multiple_of(i * CONV_ROWS, CONV_ROWS)
            hpad[pl.ds(r0 + 2 * HALO, CONV_ROWS), :] = (
                cc_ref[pl.ds(r0, CONV_ROWS), :].astype(F32) * cu_ref[pl.ds(r0, CONV_ROWS), :].astype(F32))
            return 0

        lax.fori_loop(0, nchunk, fill, 0)
        w0, w1, w2 = w_ref[0:1, :], w_ref[1:2, :], w_ref[2:3, :]

        def conv(i, _):
            r0 = pl.multiple_of(i * CONV_ROWS, CONV_ROWS)
            win = hpad[pl.ds(r0 + HALO, CONV_ROWS + HALO), :]
            c = (w2 * win[HALO:] + w1 * pltpu.roll(win, 1, 0)[HALO:] + w0 * pltpu.roll(win, 2, 0)[HALO:])
            y_ref[pl.ds(r0, CONV_ROWS), :] = (cb_ref[pl.ds(r0, CONV_ROWS), :].astype(F32) * c).astype(y_ref.dtype)
            return 0

        lax.fori_loop(0, nchunk, conv, 0)

    def col(off):
        return pl.BlockSpec((s, LANES), lambda j: (0, off + j))

    return pl.pallas_call(
        body, name=name, grid=(nblk,),
        in_specs=[col(first), col(first + nblk), col(first + 2 * nblk), pl.BlockSpec((8, LANES), lambda j: (0, j))],
        out_specs=pl.BlockSpec((s, LANES), lambda j: (0, j)),
        out_shape=jax.ShapeDtypeStruct((s, nblk * LANES), BF16),
        scratch_shapes=[pltpu.VMEM((s + 2 * HALO, LANES), F32)],
        compiler_params=_cparams("parallel"),
    )(proj, proj, proj, conv_w8)


def _conv_bwd(dmix, proj, conv_w8, name):
    s = proj.shape[0]
    nblk = conv_w8.shape[1] // LANES
    first = 3 * nblk
    nchunk = s // CONV_ROWS

    def body(dy_ref, cb_ref, cc_ref, cu_ref, w_ref, dcb_ref, dcc_ref, dcu_ref, dw_ref, hpad, dcpad):
        hpad[pl.ds(0, 2 * HALO), :] = jnp.zeros((2 * HALO, LANES), F32)
        dcpad[pl.ds(s, 2 * HALO), :] = jnp.zeros((2 * HALO, LANES), F32)

        def fill(i, _):
            r0 = pl.multiple_of(i * CONV_ROWS, CONV_ROWS)
            hpad[pl.ds(r0 + 2 * HALO, CONV_ROWS), :] = (
                cc_ref[pl.ds(r0, CONV_ROWS), :].astype(F32) * cu_ref[pl.ds(r0, CONV_ROWS), :].astype(F32))
            return 0

        lax.fori_loop(0, nchunk, fill, 0)
        w0, w1, w2 = w_ref[0:1, :], w_ref[1:2, :], w_ref[2:3, :]

        def fold(v):
            return jnp.sum(v.reshape(CONV_ROWS // 8, 8, LANES), axis=0)

        def first_pass(i, acc):
            a0, a1, a2 = acc
            r0 = pl.multiple_of(i * CONV_ROWS, CONV_ROWS)
            win = hpad[pl.ds(r0 + HALO, CONV_ROWS + HALO), :]
            h0 = win[HALO:]
            h1 = pltpu.roll(win, 1, 0)[HALO:]
            h2 = pltpu.roll(win, 2, 0)[HALO:]
            c = w2 * h0 + w1 * h1 + w0 * h2
            dy = dy_ref[pl.ds(r0, CONV_ROWS), :]
            dcb_ref[pl.ds(r0, CONV_ROWS), :] = (dy * c).astype(dcb_ref.dtype)
            dc = dy * cb_ref[pl.ds(r0, CONV_ROWS), :].astype(F32)
            dcpad[pl.ds(r0, CONV_ROWS), :] = dc
            return a0 + fold(dc * h2), a1 + fold(dc * h1), a2 + fold(dc * h0)

        z8 = jnp.zeros((8, LANES), F32)
        a0, a1, a2 = lax.fori_loop(0, nchunk, first_pass, (z8, z8, z8))
        dw_ref[...] = jnp.concatenate(
            [jnp.sum(a0, axis=0, keepdims=True), jnp.sum(a1, axis=0, keepdims=True),
             jnp.sum(a2, axis=0, keepdims=True), jnp.zeros((5, LANES), F32)], axis=0)

        def second_pass(i, _):
            r0 = pl.multiple_of(i * CONV_ROWS, CONV_ROWS)
            win = dcpad[pl.ds(r0, CONV_ROWS + HALO), :]
            n = CONV_ROWS + HALO
            dh = (w2 * win[:CONV_ROWS] + w1 * pltpu.roll(win, n - 1, 0)[:CONV_ROWS]
                  + w0 * pltpu.roll(win, n - 2, 0)[:CONV_ROWS])
            dcc_ref[pl.ds(r0, CONV_ROWS), :] = (dh * cu_ref[pl.ds(r0, CONV_ROWS), :].astype(F32)).astype(dcc_ref.dtype)
            dcu_ref[pl.ds(r0, CONV_ROWS), :] = (dh * cc_ref[pl.ds(r0, CONV_ROWS), :].astype(F32)).astype(dcu_ref.dtype)
            return 0

        lax.fori_loop(0, nchunk, second_pass, 0)

    def col(off):
        return pl.BlockSpec((s, LANES), lambda j: (0, off + j))

    out = pl.BlockSpec((s, LANES), lambda j: (0, j))
    return pl.pallas_call(
        body, name=name, grid=(nblk,),
        in_specs=[col(nblk), col(first), col(first + nblk), col(first + 2 * nblk),
                  pl.BlockSpec((8, LANES), lambda j: (0, j))],
        out_specs=[out, out, out, pl.BlockSpec((8, LANES), lambda j: (0, j))],
        out_shape=[jax.ShapeDtypeStruct((s, nblk * LANES), BF16)] * 3 + [jax.ShapeDtypeStruct((8, nblk * LANES), F32)],
        scratch_shapes=[pltpu.VMEM((s + 2 * HALO, LANES), F32), pltpu.VMEM((s + 2 * HALO, LANES), F32)],
        compiler_params=_cparams("parallel"),
    )(dmix, proj, proj, proj, conv_w8)


LOG2E = 1.4426950408889634
LN2 = 0.6931471805599453
NEG_BIG = -1e30
SATURATED = 160.0


def _cumsum_matrix(kind):
    j = lax.broadcasted_iota(jnp.int32, (KEY_CHUNK, 2 * KEY_CHUNK), 0)
    c = lax.broadcasted_iota(jnp.int32, (KEY_CHUNK, 2 * KEY_CHUNK), 1)
    tri = {"after": j > c, "upto": j <= c, "before": j < c}[kind]
    return jnp.where((c >= KEY_CHUNK) | tri, 1.0, 0.0).astype(BF16)


def _stack_heads(t, m0):
    zero = jnp.zeros_like(t)
    return jnp.concatenate([jnp.where(m0, t, zero), jnp.where(m0, zero, t)], axis=0)


def _softplus2(z):
    sp = jnp.maximum(z, 0.0) + jnp.log2(1.0 + jnp.exp2(-jnp.abs(z)))
    return sp, z - sp


def _key_chunk(ref, kc):
    return ref[pl.ds(pl.multiple_of(kc * KEY_CHUNK, KEY_CHUNK), KEY_CHUNK), :]


def _attn_bwd(qk, proj, dmix, rtot, used, tq, name, travel=()):
    s = qk.shape[0]
    nhp = qk.shape[1] // (2 * LANES)
    nc = tq // KEY_CHUNK
    nq = s // tq
    nt = len(travel)

    def body(used_ref, q_ref, k_ref, v_ref, do_ref, r_ref, cmi_ref, cme_ref, bias_ref, dq_ref, dk_ref, dv_ref,
             z_refs, ls_refs, sig_refs, sp_refs, gb_refs, pr_ref, gs_ref, copies):
        qi = pl.program_id(1)

        @pl.when(qi == 0)
        def _():
            dk_ref[...] = jnp.zeros_like(dk_ref)
            dv_ref[...] = jnp.zeros_like(dv_ref)

        if copies is not None:
            @pl.when(jnp.logical_and(pl.program_id(0) == 0, qi == 0))
            def _():
                _exchange_begin(copies)

        nslots = (qi + 1) * nc
        walked = used_ref[pl.program_id(0), qi].astype(jnp.int32)
        first = jnp.clip(nslots - walked, 0, nslots - nc) // nc * nc
        m0 = lax.broadcasted_iota(jnp.int32, (1, LANES), 1) < HEAD_DIM
        qs = _stack_heads(q_ref[...], m0)
        do = do_ref[...]
        dos = _stack_heads(do.astype(BF16), m0)
        dosl = _stack_heads((do * LN2).astype(BF16), m0)
        cmi = cmi_ref[...]
        cme = cme_ref[...]

        def chunk_at(i):
            return jnp.clip(i, first, nslots - 1)

        def scores(kc):
            return _dot(qs, _key_chunk(k_ref, kc), NT)

        def weights(ls, cs, da, pr, kc):
            a = jnp.exp2(ls - (pr - cs[:, :KEY_CHUNK]))
            gb = (a * da).astype(BF16)
            ks = pl.multiple_of(kc * KEY_CHUNK, KEY_CHUNK)
            dv_ref[pl.ds(ks, KEY_CHUNK), :] += _dot(a, dos, TN)
            return gb, jnp.exp2(ls), pr - cs[:, KEY_CHUNK:]

        def score_grads(gb, sig, cg, gs, dq, kc):
            dzb = (gb.astype(F32) * (1.0 - sig) - sig * (gs + cg[:, :KEY_CHUNK])).astype(BF16)
            ks = pl.multiple_of(kc * KEY_CHUNK, KEY_CHUNK)
            dk_ref[pl.ds(ks, KEY_CHUNK), :] += _dot(dzb, qs, TN)
            dq = dq + _dot(jnp.concatenate([dzb[:tq], dzb[tq:]], axis=1), _stack_heads(_key_chunk(k_ref, kc), m0), NN)
            return gs + cg[:, KEY_CHUNK:], dq

        def step(i, par, bias=None):
            cur, prv = par, 1 - par
            k1, k2 = chunk_at(i - 1), chunk_at(i - 2)
            z_next = scores(chunk_at(i + 1))
            cs = _dot(sp_refs[prv][...], cmi, NN)
            da = _dot(dosl, _key_chunk(v_ref, k1), NT)
            cg = _dot(gb_refs[cur][...], cme, NN)
            z = z_refs[cur][...]
            if bias is not None:
                z = z + bias
            sp, ls = _softplus2(z)
            sp_refs[cur][...] = sp.astype(BF16)
            ls_refs[cur][...] = ls
            gs, dq = score_grads(gb_refs[cur][...], sig_refs[cur][...], cg, gs_ref[...], dq_ref[...], k2)
            gs_ref[...] = gs
            dq_ref[...] = dq
            gb, sig, pr = weights(ls_refs[prv][...], cs, da, pr_ref[...], k1)
            gb_refs[prv][...] = gb
            sig_refs[prv][...] = sig
            pr_ref[...] = pr
            z_refs[prv][...] = z_next

        pr_ref[...] = jnp.concatenate([r_ref[:, :LANES], r_ref[:, LANES:]], axis=0)
        gs_ref[...] = jnp.zeros((2 * tq, LANES), F32)
        dq_ref[...] = jnp.zeros((tq, LANES), F32)
        z_refs[0][...] = scores(first)
        sp_refs[1][...] = jnp.zeros((2 * tq, LANES), BF16)
        ls_refs[1][...] = jnp.full((2 * tq, LANES), NEG_BIG, F32)
        gb_refs[0][...] = jnp.zeros((2 * tq, LANES), BF16)
        sig_refs[0][...] = jnp.zeros((2 * tq, LANES), F32)

        def two_steps(j, _):
            step(2 * j, 0)
            step(2 * j + 1, 1)
            return 0

        lax.fori_loop(first // 2, nslots // 2 - 1, two_steps, 0)
        step(nslots - 2, 0, bias_ref[0])
        step(nslots - 1, 1, bias_ref[1])
        k1, k2 = chunk_at(nslots - 1), chunk_at(nslots - 2)
        gb, sig, _ = weights(ls_refs[1][...], _dot(sp_refs[1][...], cmi, NN),
                             _dot(dosl, _key_chunk(v_ref, k1), NT), pr_ref[...], k1)
        gb2 = gb_refs[0][...]
        gs, dq = score_grads(gb2, sig_refs[0][...], _dot(gb2, cme, NN), gs_ref[...], dq_ref[...], k2)
        _, dq = score_grads(gb, sig, _dot(gb, cme, NN), gs, dq, k1)
        dq_ref[...] = dq

        if copies is not None:
            @pl.when(jnp.logical_and(pl.program_id(0) == nhp - 1, qi == nq - 1))
            def _():
                _exchange_finish(copies)

    def wrapped(*refs):
        ins, rest = refs[:9], refs[9:]
        srcs, rest = rest[:nt], rest[nt:]
        outs, rest = rest[:3], rest[3:]
        lands, rest = rest[:nt], rest[nt:]
        z0, z1, ls0, ls1, sg0, sg1, sp0, sp1, gb0, gb1, pr_ref, gs_ref = rest[:12]
        copies = _exchange_copies(srcs, lands, *rest[12:]) if nt else None
        body(*ins, *outs, (z0, z1), (ls0, ls1), (sg0, sg1), (sp0, sp1), (gb0, gb1), pr_ref, gs_ref, copies)

    assert nc == 2
    bias = _diag_bias(tq, True)
    bias = jnp.concatenate([bias[:, :, :KEY_CHUNK], bias[:, :, KEY_CHUNK:]], axis=1)
    qblk = pl.BlockSpec((tq, LANES), lambda p, i: (i, p))
    full = pl.BlockSpec((s, LANES), lambda p, i: (0, p))
    cmspec = pl.BlockSpec((KEY_CHUNK, 2 * KEY_CHUNK), lambda p, i: (0, 0))
    anyspec = pl.BlockSpec(memory_space=pl.ANY)
    shape = jax.ShapeDtypeStruct((s, nhp * LANES), F32)
    f32buf = pltpu.VMEM((2 * tq, LANES), F32)
    bf16buf = pltpu.VMEM((2 * tq, LANES), BF16)
    outs = pl.pallas_call(
        wrapped, name=name, grid=(nhp, nq),
        in_specs=[pl.BlockSpec(memory_space=pltpu.SMEM),
                  qblk,
                  pl.BlockSpec((s, LANES), lambda p, i: (0, nhp + p)),
                  pl.BlockSpec((s, LANES), lambda p, i: (0, 2 * nhp + p)),
                  qblk,
                  pl.BlockSpec((tq, 2 * LANES), lambda p, i: (i, p)),
                  cmspec, cmspec,
                  pl.BlockSpec((nc, 2 * tq, LANES), lambda p, i: (0, 0, 0))] + [anyspec] * nt,
        out_specs=[qblk, full, full] + [anyspec] * nt,
        out_shape=[shape, shape, shape] + [jax.ShapeDtypeStruct(t.shape, t.dtype) for t in travel],
        scratch_shapes=[f32buf] * 6 + [bf16buf] * 4 + [f32buf] * 2 + (_exchange_scratch(nt) if nt else []),
        compiler_params=_cparams("arbitrary", "arbitrary"),
    )(used, qk, qk, proj, dmix, rtot, _cumsum_matrix("upto"), _cumsum_matrix("before"), bias, *travel)
    return outs[0], outs[1], outs[2], list(outs[3:])


def _pair_cumsum_matrix(kind):
    j = lax.broadcasted_iota(jnp.int32, (2 * KEY_CHUNK, 4 * KEY_CHUNK), 0)
    c = lax.broadcasted_iota(jnp.int32, (2 * KEY_CHUNK, 4 * KEY_CHUNK), 1)
    same_head = (j // KEY_CHUNK) == ((c // KEY_CHUNK) % 2)
    jj, cc = j % KEY_CHUNK, c % KEY_CHUNK
    tri = {"after": jj > cc, "upto": jj <= cc, "before": jj < cc}[kind]
    return jnp.where(same_head & ((c >= 2 * KEY_CHUNK) | tri), 1.0, 0.0).astype(BF16)


def _diag_bias(tq, ascending):
    nc = tq // KEY_CHUNK
    shape = (nc, tq, 2 * KEY_CHUNK)
    d = lax.broadcasted_iota(jnp.int32, shape, 0)
    r = lax.broadcasted_iota(jnp.int32, shape, 1)
    c = lax.broadcasted_iota(jnp.int32, shape, 2) % KEY_CHUNK
    chunk = d if ascending else nc - 1 - d
    return jnp.where(chunk * KEY_CHUNK + c < r, 0.0, NEG_BIG).astype(F32)


def _attn_fwd(qk, proj, tq, name, shards=()):
    s = qk.shape[0]
    nhp = qk.shape[1] // (2 * LANES)
    nc = tq // KEY_CHUNK
    nq = s // tq
    ng = len(shards)
    assert nc == 2
    w = 2 * KEY_CHUNK

    def body(q_ref, k_ref, v_ref, cm_ref, bias_ref, o_ref, r_ref, used_ref, z_refs, ls_refs, cs_refs, ct_refs,
             sp_refs, ab_refs, acc_ref, gather):
        qi = pl.program_id(1)
        if gather is not None:
            @pl.when(jnp.logical_and(pl.program_id(0) == 0, qi == 0))
            def _():
                gather.begin()

        nslots = (qi + 1) * nc
        m0 = lax.broadcasted_iota(jnp.int32, (1, LANES), 1) < HEAD_DIM
        q = q_ref[...]
        cm = cm_ref[...]

        def chunk_at(i):
            return jnp.clip(nslots - 1 - i, 0, nslots - 1)

        def scores(kc):
            return _dot(q, _stack_heads(_key_chunk(k_ref, kc), m0), NT)

        def values(ab, kc):
            return _dot(ab, _stack_heads(_key_chunk(v_ref, kc), m0), NN)

        def step(i, par, bias=None, stages="zscwv"):
            cur, prv = par, 1 - par
            if "z" in stages:
                z_next = scores(chunk_at(i + 1))
            if "c" in stages:
                cs = _dot(sp_refs[prv][...], cm, NN)
            if "v" in stages:
                pv = values(ab_refs[prv][...], chunk_at(i - 3))
            if "w" in stages:
                rs = r_ref[...]
                r_ref[...] = rs + ct_refs[cur][...]
                ab_refs[cur][...] = jnp.exp2(ls_refs[cur][...] - cs_refs[cur][...] - rs).astype(BF16)
            if "s" in stages:
                z = z_refs[cur][...]
                if bias is not None:
                    z = z + bias
                sp, ls = _softplus2(z)
                sp_refs[cur][...] = sp.astype(BF16)
                ls_refs[cur][...] = ls
            if "v" in stages:
                acc_ref[...] += pv
            if "c" in stages:
                cs_refs[prv][...] = cs[:, :w]
                ct_refs[prv][...] = cs[:, w:]
            if "z" in stages:
                z_refs[prv][...] = z_next

        z_refs[0][...] = scores(chunk_at(0))
        ab_refs[1][...] = jnp.zeros((tq, w), BF16)
        r_ref[...] = jnp.zeros((tq, w), F32)
        acc_ref[...] = jnp.zeros((tq, LANES), F32)
        step(0, 0, bias_ref[0], stages="zs")
        step(1, 1, bias_ref[1], stages="zsc")

        def two_steps(carry):
            j, _ = carry
            step(2 * j, 0)
            step(2 * j + 1, 1)
            return j + 1, jnp.min(r_ref[...])

        pairs, low = lax.while_loop(lambda c: jnp.logical_and(c[0] < nslots // 2, c[1] < SATURATED), two_steps,
                                    (jnp.int32(1), jnp.min(r_ref[...])))
        entered = 2 * pairs
        saturated = low >= SATURATED

        @pl.when(saturated)
        def _():
            step(entered, 0, stages="v")

        @pl.when(jnp.logical_not(saturated))
        def _():
            step(entered, 0, stages="cwv")
            step(entered + 1, 1, stages="wv")
            step(entered + 2, 0, stages="v")

        o_ref[...] = acc_ref[...].astype(o_ref.dtype)
        used_ref[pl.program_id(0), qi] = jnp.where(saturated, entered - 2, entered).astype(F32)

        if gather is not None:
            @pl.when(jnp.logical_and(pl.program_id(0) == nhp - 1, qi == nq // 2))
            def _():
                gather.relay()

            @pl.when(jnp.logical_and(pl.program_id(0) == nhp - 1, qi == nq - 1))
            def _():
                gather.finish()

    def wrapped(*refs):
        ins, rest = refs[:5], refs[5:]
        srcs, rest = rest[:ng], rest[ng:]
        outs, rest = rest[:3], rest[3:]
        dsts, scratch = rest[:ng], rest[ng:]
        z, ls, cs, ct, sp, ab = [scratch[2 * j:2 * j + 2] for j in range(6)]
        gather = _Gather(srcs, dsts, *scratch[13:]) if ng else None
        body(*ins, *outs, z, ls, cs, ct, sp, ab, scratch[12], gather)

    f32buf = pltpu.VMEM((tq, w), F32)
    bf16buf = pltpu.VMEM((tq, w), BF16)
    anyspec = pl.BlockSpec(memory_space=pl.ANY)
    outs = pl.pallas_call(
        wrapped, name=name, grid=(nhp, nq),
        in_specs=[pl.BlockSpec((tq, LANES), lambda p, i: (i, p)),
                  pl.BlockSpec((s, LANES), lambda p, i: (0, nhp + p)),
                  pl.BlockSpec((s, LANES), lambda p, i: (0, 2 * nhp + p)),
                  pl.BlockSpec((w, 2 * w), lambda p, i: (0, 0)),
                  pl.BlockSpec((nc, tq, w), lambda p, i: (0, 0, 0))] + [anyspec] * ng,
        out_specs=[pl.BlockSpec((tq, LANES), lambda p, i: (i, p)),
                   pl.BlockSpec((tq, w), lambda p, i: (i, p)),
                   pl.BlockSpec(memory_space=pltpu.SMEM)] + [anyspec] * ng,
        out_shape=[jax.ShapeDtypeStruct((s, nhp * LANES), BF16),
                   jax.ShapeDtypeStruct((s, nhp * w), F32),
                   jax.ShapeDtypeStruct((nhp, nq), F32)] + _gathered_shapes(shards),
        scratch_shapes=([f32buf] * 8 + [bf16buf] * 4 + [pltpu.VMEM((tq, LANES), F32)]
                        + (_gather_scratch(ng) if ng else [])),
        compiler_params=_cparams("arbitrary", "arbitrary"),
    )(qk, qk, proj, _pair_cumsum_matrix("after"), _diag_bias(tq, False), *shards)
    return outs[0], outs[1], outs[2], list(outs[3:])


BLOCK_PAIR = 2


def _side_by_side(b_ref):
    return jnp.concatenate([b_ref[p] for p in range(BLOCK_PAIR)], axis=1)


def _mm_blocks(h, ga, widx, tm, name):
    s, d = h.shape
    nb, cols = ga.shape[1], ga.shape[3]

    def body(a_ref, b_ref, o_ref):
        o_ref[...] = _dot(a_ref[...], _side_by_side(b_ref), NN).astype(o_ref.dtype)

    return pl.pallas_call(
        body, name=name, grid=(s // tm, nb // BLOCK_PAIR),
        in_specs=[pl.BlockSpec((tm, d), lambda i, j: (i, 0)),
                  pl.BlockSpec((None, BLOCK_PAIR, d, cols), lambda i, j: (widx, j, 0, 0))],
        out_specs=pl.BlockSpec((tm, BLOCK_PAIR * cols), lambda i, j: (i, j)),
        out_shape=jax.ShapeDtypeStruct((s, nb * cols), BF16),
        compiler_params=_cparams("parallel", "arbitrary"),
    )(h, ga)


def _mm_swiglu(h, ga, gidx, uidx, tm, name):
    s, d = h.shape
    nb, cols = ga.shape[1], ga.shape[3]

    def body(a_ref, bg_ref, bu_ref, g_ref, u_ref, act_ref):
        a = a_ref[...]
        g = _dot(a, _side_by_side(bg_ref), NN)
        u = _dot(a, _side_by_side(bu_ref), NN)
        g_ref[...] = g.astype(g_ref.dtype)
        u_ref[...] = u.astype(u_ref.dtype)
        act_ref[...] = (g * (1.0 / (1.0 + jnp.exp(-g))) * u).astype(act_ref.dtype)

    def wspec(idx):
        return pl.BlockSpec((None, BLOCK_PAIR, d, cols), lambda i, j: (idx, j, 0, 0))

    out = pl.BlockSpec((tm, BLOCK_PAIR * cols), lambda i, j: (i, j))
    shape = jax.ShapeDtypeStruct((s, nb * cols), BF16)
    return pl.pallas_call(
        body, name=name, grid=(s // tm, nb // BLOCK_PAIR),
        in_specs=[pl.BlockSpec((tm, d), lambda i, j: (i, 0)), wspec(gidx), wspec(uidx)],
        out_specs=[out, out, out], out_shape=[shape, shape, shape],
        compiler_params=_cparams("parallel", "arbitrary"),
    )(h, ga, ga)


def _mm_residual_norm(a, w3, lidx, res, gain, tm, name):
    s, k = a.shape
    n = w3.shape[2]

    def body(a_ref, b_ref, r_ref, g_ref, o_ref, h_ref):
        xv = r_ref[...] + _dot(a_ref[...], b_ref[...], NN)
        o_ref[...] = xv
        r = lax.rsqrt(jnp.mean(xv * xv, axis=-1, keepdims=True) + EPS)
        h_ref[...] = ((xv * r) * g_ref[...]).astype(h_ref.dtype)

    row = pl.BlockSpec((tm, n), lambda i: (i, 0))
    return pl.pallas_call(
        body, name=name, grid=(s // tm,),
        in_specs=[pl.BlockSpec((tm, k), lambda i: (i, 0)),
                  pl.BlockSpec((None, k, n), lambda i: (lidx, 0, 0), pipeline_mode=pl.Buffered(1)),
                  row, pl.BlockSpec((1, n), lambda i: (0, 0))],
        out_specs=[row, row],
        out_shape=[jax.ShapeDtypeStruct((s, n), F32), jax.ShapeDtypeStruct((s, n), BF16)],
        compiler_params=_cparams("parallel"),
    )(a, w3, res, gain)


def _mm_residual_loss(a, w3, lidx, res, target, tm, name):
    s, k = a.shape
    n = w3.shape[2]
    nsteps = s // tm

    def body(a_ref, b_ref, r_ref, t_ref, dy_ref, dyb_ref, l_ref, acc):
        i = pl.program_id(0)
        diff = r_ref[...] + _dot(a_ref[...], b_ref[...], NN) - t_ref[...]
        dy_ref[...] = diff * (1.0 / n)
        dyb_ref[...] = (diff * (1.0 / n)).astype(dyb_ref.dtype)
        part = jnp.sum((diff * diff).reshape(tm // 8, 8, n), axis=0)

        @pl.when(i == 0)
        def _():
            acc[...] = part

        @pl.when(i > 0)
        def _():
            acc[...] += part

        @pl.when(i == nsteps - 1)
        def _():
            tot = jnp.sum(jnp.sum(acc[...], axis=1, keepdims=True), axis=0, keepdims=True)
            l_ref[...] = jnp.broadcast_to(tot * (0.5 / n), (8, LANES))

    row = pl.BlockSpec((tm, n), lambda i: (i, 0))
    return pl.pallas_call(
        body, name=name, grid=(nsteps,),
        in_specs=[pl.BlockSpec((tm, k), lambda i: (i, 0)),
                  pl.BlockSpec((None, k, n), lambda i: (lidx, 0, 0), pipeline_mode=pl.Buffered(1)),
                  row, row],
        out_specs=[row, row, pl.BlockSpec((8, LANES), lambda i: (0, 0))],
        out_shape=[jax.ShapeDtypeStruct((s, n), F32), jax.ShapeDtypeStruct((s, n), BF16),
                   jax.ShapeDtypeStruct((8, LANES), F32)],
        scratch_shapes=[pltpu.VMEM((8, n), F32)],
        compiler_params=_cparams("arbitrary"),
    )(a, w3, res, target)


def _mm_nt(a, w3, lidx, tm, tn, name):
    s, k = a.shape
    n = w3.shape[1]

    def body(a_ref, b_ref, o_ref):
        o_ref[...] = _dot(a_ref[...], b_ref[...], NT)

    return pl.pallas_call(
        body, name=name, grid=(s // tm, n // tn),
        in_specs=[pl.BlockSpec((tm, k), lambda i, j: (i, 0)),
                  pl.BlockSpec((None, tn, k), lambda i, j: (lidx, j, 0))],
        out_specs=pl.BlockSpec((tm, tn), lambda i, j: (i, j)),
        out_shape=jax.ShapeDtypeStruct((s, n), F32),
        compiler_params=_cparams("parallel", "arbitrary"),
    )(a, w3)


def _mm_nt_swiglu_bwd(dx, wd3, lidx, g, u, tm, name):
    s, d = dx.shape
    cols = BLOCK_PAIR * (g.shape[1] // N_DEV)

    def body(a_ref, b_ref, g_ref, u_ref, dg_ref, du_ref):
        dact = _dot(a_ref[...], b_ref[...], NT)
        gv = g_ref[...].astype(F32)
        sig = 1.0 / (1.0 + jnp.exp(-gv))
        du_ref[...] = (dact * (gv * sig)).astype(du_ref.dtype)
        dg_ref[...] = (dact * u_ref[...].astype(F32) * (sig * (1.0 + gv * (1.0 - sig)))).astype(dg_ref.dtype)

    blk = pl.BlockSpec((tm, cols), lambda i, j: (i, j))
    shape = jax.ShapeDtypeStruct(g.shape, BF16)
    return pl.pallas_call(
        body, name=name, grid=(s // tm, N_DEV // BLOCK_PAIR),
        in_specs=[pl.BlockSpec((tm, d), lambda i, j: (i, 0)),
                  pl.BlockSpec((None, cols, d), lambda i, j: (lidx, j, 0)), blk, blk],
        out_specs=[blk, blk], out_shape=[shape, shape],
        compiler_params=_cparams("parallel", "arbitrary"),
    )(dx, wd3, g, u)


def _mm_nt_blocks(das, ga, widxs, tm, name, travel=()):
    s = das[0].shape[0]
    nb, d, cols = ga.shape[1], ga.shape[2], ga.shape[3]
    nw = len(das)

    def body(*refs):
        a_refs, b_refs, o_ref = refs[:nw], refs[nw:2 * nw], refs[2 * nw]
        acc = None
        wide = BLOCK_PAIR * cols
        for w in range(nw):
            for k in range(nb // BLOCK_PAIR):
                b = jnp.concatenate([b_refs[w][BLOCK_PAIR * k + p] for p in range(BLOCK_PAIR)], axis=1)
                part = _dot(a_refs[w][:, k * wide:(k + 1) * wide], b, NT)
                acc = part if acc is None else acc + part
        o_ref[...] = acc

    def wspec(idx):
        return pl.BlockSpec((None, nb, d, cols), lambda i: (idx, 0, 0, 0), pipeline_mode=pl.Buffered(1))

    grid = (s // tm,)
    body, more_in, more_out, more_shapes, more_scratch = _host_exchange(body, 2 * nw, 1, travel, grid)
    outs = pl.pallas_call(
        body, name=name, grid=grid,
        in_specs=[pl.BlockSpec((tm, nb * cols), lambda i: (i, 0))] * nw + [wspec(i) for i in widxs] + more_in,
        out_specs=[pl.BlockSpec((tm, d), lambda i: (i, 0))] + more_out,
        out_shape=[jax.ShapeDtypeStruct((s, d), F32)] + more_shapes,
        scratch_shapes=more_scratch,
        compiler_params=_cparams("arbitrary"),
    )(*das, *([ga] * nw), *travel)
    return outs[0], list(outs[1:])


def _mm_tn(a, b, ta, tb, tk, out_blocks, name, travel=()):
    s, ka = a.shape
    nb = b.shape[1]
    nk = s // tk
    cols = tb
    if out_blocks:
        tb = BLOCK_PAIR * cols

    def body(a_ref, b_ref, o_ref, ob_ref):
        k = pl.program_id(2)
        part = _dot(a_ref[...], b_ref[...], TN)

        def put(first):
            if out_blocks:
                for p in range(BLOCK_PAIR):
                    piece = part[:, p * cols:(p + 1) * cols]
                    o_ref[p] = piece if first else o_ref[p] + piece
            else:
                o_ref[...] = part if first else o_ref[...] + part

        @pl.when(k == 0)
        def _():
            put(True)

        @pl.when(k > 0)
        def _():
            put(False)

        @pl.when(k == nk - 1)
        def _():
            ob_ref[...] = o_ref[...].astype(ob_ref.dtype)

    if out_blocks:
        out_spec = pl.BlockSpec((BLOCK_PAIR, ta, cols), lambda i, j, k: (j, i, 0))
        shape = (nb // cols, ka, cols)
    else:
        out_spec = pl.BlockSpec((ta, tb), lambda i, j, k: (i, j))
        shape = (ka, nb)
    grid = (ka // ta, nb // tb, nk)
    body, more_in, more_out, more_shapes, more_scratch = _host_exchange(body, 2, 2, travel, grid)
    outs = pl.pallas_call(
        body, name=name, grid=grid,
        in_specs=[pl.BlockSpec((tk, ta), lambda i, j, k: (k, i)),
                  pl.BlockSpec((tk, tb), lambda i, j, k: (k, j))] + more_in,
        out_specs=[out_spec, out_spec] + more_out,
        out_shape=[jax.ShapeDtypeStruct(shape, F32), jax.ShapeDtypeStruct(shape, BF16)] + more_shapes,
        scratch_shapes=more_scratch,
        compiler_params=_cparams("arbitrary", "arbitrary", "arbitrary"),
    )(a, b, *travel)
    return (outs[0], outs[1]), list(outs[2:])


def _adamw(parts, own, w, m, v, tr, name):
    p, rows, cols = parts.shape
    c1 = 1.0 / (1.0 - ADAM_B1 ** ADAM_STEP)
    c2 = 1.0 / (1.0 - ADAM_B2 ** ADAM_STEP)

    def body(*refs):
        if own is None:
            p_ref, w_ref, m_ref, v_ref, g_ref, d_ref, nm_ref, nv_ref = refs
            g = p_ref[0]
            for k in range(1, p):
                g = g + p_ref[k]
        else:
            p_ref, own_ref, w_ref, m_ref, v_ref, g_ref, d_ref, nm_ref, nv_ref = refs
            x, y, c = _place()
            my = 4 * x + 2 * y + c
            mine = own_ref[...]
            g = jnp.where(my == 0, mine, p_ref[0].astype(F32))
            for k in range(1, p):
                g = g + jnp.where(my == k, mine, p_ref[k].astype(F32))
        nm = ADAM_B1 * m_ref[...] + (1.0 - ADAM_B1) * g
        nv = ADAM_B2 * v_ref[...] + (1.0 - ADAM_B2) * (g * g)
        g_ref[...] = g
        nm_ref[...] = nm
        nv_ref[...] = nv
        d_ref[...] = -ADAM_LR * ((nm * c1) / (jnp.sqrt(nv * c2) + ADAM_EPS) + ADAM_WD * w_ref[...])

    blk = pl.BlockSpec((tr, cols), lambda i: (i, 0))
    shape = jax.ShapeDtypeStruct((rows, cols), F32)
    return pl.pallas_call(
        body, name=name, grid=(rows // tr,),
        in_specs=[pl.BlockSpec((p, tr, cols), lambda i: (0, i, 0))] + [blk] * (3 if own is None else 4),
        out_specs=[blk] * 4, out_shape=[shape] * 4,
        compiler_params=_cparams("parallel"),
    )(*([parts] + ([] if own is None else [own]) + [w, m, v]))


def _adamw_sharded(parts, grads, my, w, m, v, tr, name):
    depth, rows, cols = w.shape
    p, pr, pc = parts[0].shape
    c1 = 1.0 / (1.0 - ADAM_B1 ** ADAM_STEP)
    c2 = 1.0 / (1.0 - ADAM_B2 ** ADAM_STEP)

    def body(my_ref, *refs):
        p_refs, own_refs = refs[:depth], refs[depth:2 * depth]
        w_ref, m_ref, v_ref, g_ref, d_ref, nm_ref, nv_ref = refs[2 * depth:]
        layer = pl.program_id(0)
        for ll in range(depth):
            @pl.when(layer == ll)
            def _(ll=ll):
                mine = own_refs[ll][...]
                g = jnp.where(my_ref[0] == 0, mine, p_refs[ll][0].astype(F32))
                for k in range(1, p):
                    g = g + jnp.where(my_ref[0] == k, mine, p_refs[ll][k].astype(F32))
                g = g[:, :cols]
                nm = ADAM_B1 * m_ref[...] + (1.0 - ADAM_B1) * g
                nv = ADAM_B2 * v_ref[...] + (1.0 - ADAM_B2) * (g * g)
                g_ref[...] = g
                nm_ref[...] = nm
                nv_ref[...] = nv
                d_ref[...] = -ADAM_LR * ((nm * c1) / (jnp.sqrt(nv * c2) + ADAM_EPS) + ADAM_WD * w_ref[...])

    def row_block(ll, l, i):
        return jnp.where(l == ll, i, 0)

    blk = pl.BlockSpec((None, tr, cols), lambda l, i, my_: (l, i, 0))
    shape = jax.ShapeDtypeStruct((depth, rows, cols), F32)
    return pl.pallas_call(
        body, name=name,
        grid_spec=pltpu.PrefetchScalarGridSpec(
            num_scalar_prefetch=1, grid=(depth, rows // tr),
            in_specs=([pl.BlockSpec((p, tr, pc), lambda l, i, my_, ll=ll: (0, row_block(ll, l, i), 0))
                       for ll in range(depth)]
                      + [pl.BlockSpec((None, tr, pc), lambda l, i, my_, ll=ll: (my_[0], row_block(ll, l, i), 0))
                         for ll in range(depth)]
                      + [blk, blk, blk]),
            out_specs=[blk] * 4),
        out_shape=[shape] * 4,
        compiler_params=_cparams("arbitrary", "arbitrary"),
    )(my, *parts, *grads, w, m, v)


def _place():
    x, y, c = lax.axis_index("x"), lax.axis_index("y"), lax.axis_index("c")
    return x, y, c


class _Gather:
    def __init__(self, srcs, dsts, send_sems, recv_sems, local_sems):
        na = len(srcs)
        x, y, c = _place()
        me, sibling = (x, y, c), (x, y, 1 - c)
        chips = [(1 - x, y), (x, 1 - y), (1 - x, 1 - y)]

        def slot(a, dev):
            return dsts[a].at[:, pl.ds(4 * dev[0] + 2 * dev[1] + dev[2], 1)]

        def copy(k, a, block, to, from_shard=False):
            return pltpu.make_async_remote_copy(
                src_ref=srcs[a] if from_shard else slot(a, block), dst_ref=slot(a, block),
                send_sem=send_sems.at[k, a], recv_sem=recv_sems.at[k, a], device_id=to, device_id_type=MESH)

        pairs = [(j, chip, a) for j, chip in enumerate(chips) for a in range(na)]
        self.mine = [pltpu.make_async_copy(srcs[a], slot(a, me), local_sems.at[a]) for a in range(na)]
        self.first = [copy(0, a, me, sibling, True) for a in range(na)]
        self.first += [copy(1 + j, a, me, (*chip, c), True) for j, chip, a in pairs]
        self.over_ici = [copy(1 + j, a, (*chip, c), me) for j, chip, a in pairs]
        self.passed = [copy(4 + j, a, (*chip, c), sibling) for j, chip, a in pairs]
        self.from_sibling = [copy(0, a, sibling, me) for a in range(na)]
        self.from_sibling += [copy(4 + j, a, (*chip, 1 - c), me) for j, chip, a in pairs]

    def begin(self):
        for cp in self.mine + self.first:
            cp.start()

    def relay(self):
        for arrived, onward in zip(self.over_ici, self.passed):
            arrived.wait_recv()
            onward.start()

    def finish(self):
        for cp in self.from_sibling:
            cp.wait_recv()
        for cp in self.first + self.passed:
            cp.wait_send()
        for cp in self.mine:
            cp.wait()


def _gather_scratch(na):
    return [pltpu.SemaphoreType.DMA((7, na)), pltpu.SemaphoreType.DMA((7, na)), pltpu.SemaphoreType.DMA((na,))]


def _gathered_shapes(shards):
    return [jax.ShapeDtypeStruct((a.shape[0], N_DEV) + a.shape[2:], a.dtype) for a in shards]


def _all_gather(shards, name):
    na = len(shards)

    def body(*refs):
        gather = _Gather(refs[:na], refs[na:2 * na], *refs[2 * na:])
        gather.begin()
        gather.relay()
        gather.finish()

    anyspec = pl.BlockSpec(memory_space=pl.ANY)
    return pl.pallas_call(
        body, name=name,
        in_specs=[anyspec] * na, out_specs=[anyspec] * na,
        out_shape=_gathered_shapes(shards), scratch_shapes=_gather_scratch(na),
    )(*shards)


_RELATIONS = [(dx, dy, dc) for dx in (0, 1) for dy in (0, 1) for dc in (0, 1)][1:]


def _flip(v, d):
    return 1 - v if d else v


def _exchange_copies(srcs, dsts, send_sems, recv_sems, local_sems):
    x, y, c = _place()
    my = 4 * x + 2 * y + c
    na = len(srcs)
    mine = [pltpu.make_async_copy(srcs[a].at[pl.ds(my, 1)], dsts[a].at[pl.ds(my, 1)], local_sems.at[a])
            for a in range(na)]
    sends, recvs = [], []
    for k, (dx, dy, dc) in enumerate(_RELATIONS):
        peer = (_flip(x, dx), _flip(y, dy), _flip(c, dc))
        pidx = 4 * peer[0] + 2 * peer[1] + peer[2]
        for a in range(na):
            for into, out in ((my, sends), (pidx, recvs)):
                out.append(pltpu.make_async_remote_copy(
                    src_ref=srcs[a].at[pl.ds(pidx, 1)], dst_ref=dsts[a].at[pl.ds(into, 1)],
                    send_sem=send_sems.at[k, a], recv_sem=recv_sems.at[k, a], device_id=peer, device_id_type=MESH))
    return mine, sends, recvs


def _exchange_begin(copies):
    mine, sends, _ = copies
    for cp in mine + sends:
        cp.start()


def _exchange_finish(copies):
    mine, sends, recvs = copies
    for cp in recvs:
        cp.wait_recv()
    for cp in sends:
        cp.wait_send()
    for cp in mine:
        cp.wait()


def _exchange_scratch(na):
    return [pltpu.SemaphoreType.DMA((7, na)), pltpu.SemaphoreType.DMA((7, na)), pltpu.SemaphoreType.DMA((na,))]


def _host_exchange(body, n_in, n_out, travel, grid):
    nt = len(travel)
    if not nt:
        return body, [], [], [], []

    def wrapped(*refs):
        ins, srcs = refs[:n_in], refs[n_in:n_in + nt]
        outs, rest = refs[n_in + nt:n_in + nt + n_out], refs[n_in + nt + n_out:]
        dsts, scratch = rest[:nt], rest[nt:]
        copies = _exchange_copies(srcs, dsts, *scratch[-3:])
        first = last = None
        for axis, size in enumerate(grid):
            at_start, at_end = pl.program_id(axis) == 0, pl.program_id(axis) == size - 1
            first = at_start if first is None else jnp.logical_and(first, at_start)
            last = at_end if last is None else jnp.logical_and(last, at_end)

        @pl.when(first)
        def _():
            _exchange_begin(copies)

        body(*ins, *outs, *scratch[:-3])

        @pl.when(last)
        def _():
            _exchange_finish(copies)

    anyspec = pl.BlockSpec(memory_space=pl.ANY)
    return (wrapped, [anyspec] * nt, [anyspec] * nt, [jax.ShapeDtypeStruct(t.shape, t.dtype) for t in travel],
            _exchange_scratch(nt))


def _all_reduce_small(v, name):
    r, c_ = v.shape

    def body(v_ref, o_ref, gath, send_sems, recv_sems):
        x, y, c = _place()
        my = 4 * x + 2 * y + c
        gath[my] = v_ref[...]
        sends = []
        for k, (dx, dy, dc) in enumerate(_RELATIONS):
            peer = (_flip(x, dx), _flip(y, dy), _flip(c, dc))
            cp = pltpu.make_async_remote_copy(
                src_ref=v_ref, dst_ref=gath.at[my], send_sem=send_sems.at[k], recv_sem=recv_sems.at[k],
                device_id=peer, device_id_type=MESH)
            cp.start()
            sends.append((cp, 4 * peer[0] + 2 * peer[1] + peer[2], k, peer))
        for cp, pidx, k, peer in sends:
            pltpu.make_async_remote_copy(
                src_ref=v_ref, dst_ref=gath.at[pidx], send_sem=send_sems.at[k], recv_sem=recv_sems.at[k],
                device_id=peer, device_id_type=MESH).wait_recv()
        for cp, *_ in sends:
            cp.wait_send()
        tot = gath[0]
        for k in range(1, N_DEV):
            tot = tot + gath[k]
        o_ref[...] = tot

    vm = pl.BlockSpec(memory_space=pltpu.VMEM)
    return pl.pallas_call(
        body, name=name, in_specs=[vm], out_specs=vm,
        out_shape=jax.ShapeDtypeStruct((r, c_), F32),
        scratch_shapes=[pltpu.VMEM((N_DEV, r, c_), F32), pltpu.SemaphoreType.DMA((7,)),
                        pltpu.SemaphoreType.DMA((7,))],
    )(v)


TM = 512
TM_MATMUL = 2048
TM_RESIDUAL = 1024
TQ = 256


def _device_blocks(t):
    return t.reshape(N_DEV, -1, t.shape[-1])


def _pad_to(a, axis, size):
    pad = [(0, 0)] * a.ndim
    pad[axis] = (0, size - a.shape[axis])
    return jnp.pad(a, pad)


def _local_step(x, target, g_in0, late_shards, conv_full, norm_mix, q_norm, k_norm, norm_ffn):
    depth, d = norm_mix.shape
    cols = g_in0.shape[3]
    tm, tq = min(TM, x.shape[0]), min(TQ, x.shape[0])
    tmm, tmr = min(TM_MATMUL, x.shape[0]), min(TM_RESIDUAL, x.shape[0])
    attn = d // 2
    nheads = attn // HEAD_DIM
    scale = HEAD_DIM ** -0.5 * LOG2E
    saved = []
    h1 = _rmsnorm_fwd(x, norm_mix[0][None], tm, "norm_mix_fwd_0")
    for l in range(depth):
        w_in = (g_in0, 0) if l == 0 else (g_rest, 3 * (l - 1))
        proj = _mm_blocks(h1, *w_in, tmm, f"proj_in_{l}")
        qk_gain = jnp.concatenate([jnp.tile(q_norm[l], nheads) * scale, jnp.tile(k_norm[l], nheads)])[None]
        qk = _qknorm_fwd(proj, qk_gain, tmm, f"qknorm_fwd_{l}")
        o, rtot, used, gathered = _attn_fwd(qk, proj, tq, f"attn_fwd_{l}", late_shards if l == 0 else ())
        if l == 0:
            g_gu0, g_rest, gb, gc = gathered if depth > 1 else (gathered[0], None, *gathered[1:])
            gb = gb.reshape(depth, -1, d)
            gc = gc.reshape(depth, -1, d)
        w_gu = (g_gu0, 0, 1) if l == 0 else (g_rest, 3 * (l - 1) + 1, 3 * (l - 1) + 2)
        conv_w8 = _pad_to(conv_full[l], 0, 8)
        cv = _conv_fwd(proj, conv_w8, f"conv_fwd_{l}")
        mix = jnp.concatenate([o, cv], axis=1)
        x1, h2 = _mm_residual_norm(mix, gb, l, x, norm_ffn[l][None], tmr, f"proj_out_{l}")
        g, u, act = _mm_swiglu(h2, *w_gu, tmr, f"ffn_up_{l}")
        saved.append((x, h1, proj, qk_gain, qk, rtot, used, conv_w8, mix, x1, h2, g, u, act, w_in, w_gu))
        if l + 1 < depth:
            x, h1 = _mm_residual_norm(act, gc, l, x1, norm_mix[l + 1][None], tm, f"ffn_down_{l}")
        else:
            dx, dxb, loss = _mm_residual_loss(act, gc, l, x1, target, tm, f"ffn_down_{l}")

    grads = [None] * depth
    small = [None] * depth
    landed = {}
    for l in reversed(range(depth)):
        x0, h1, proj, qk_gain, qk, rtot, used, conv_w8, mix, x1, h2, g, u, act, w_in, w_gu = saved[l]
        d = x0.shape[1]
        dg, du = _mm_nt_swiglu_bwd(dxb, gc, l, g, u, tmr, f"ffn_down_bwd_{l}")
        d_wdown, _ = _mm_tn(act, dxb, 768, d, tmm, False, f"dw_down_{l}")
        d_wgate, _ = _mm_tn(h2, dg, d, cols, tmm, True, f"dw_gate_{l}")
        d_wup, _ = _mm_tn(h2, du, d, cols, tmm, True, f"dw_up_{l}")
        dh2, _ = _mm_nt_blocks([dg, du], w_gu[0], list(w_gu[1:]), tm, f"ffn_up_bwd_{l}")
        dx1, dx1b, dg_ffn, _ = _rmsnorm_bwd(dh2, x1, norm_ffn[l][None], dx, tm, f"norm_ffn_bwd_{l}")
        dmix = _mm_nt(dx1b, gb, l, tmr, 512, f"proj_out_bwd_{l}")
        d_wout, _ = _mm_tn(mix, dx1b, 512, d, tmm, False, f"dw_out_{l}")
        dcb, dcc, dcu, dconv = _conv_bwd(dmix, proj, conv_w8, f"conv_bwd_{l}")
        rides = {}
        if l == 0:
            rides = {5 * ll + j: grads[ll][j][1] for ll in range(1, depth) for j in range(5)}
            rides.update({3: d_wout[1], 4: d_wdown[1]})
        dq, dk, dv, arrived = _attn_bwd(qk, proj, dmix, rtot, used, tq, f"attn_bwd_{l}",
                                        [_device_blocks(t) for t in rides.values()])
        landed.update(zip(rides.keys(), arrived))
        dqk, dg_qk = _qknorm_bwd(jnp.concatenate([dq, dk], axis=1), proj, qk_gain, tmm, f"qknorm_bwd_{l}")
        dproj = jnp.concatenate([dqk, dv.astype(BF16), dcb, dcc, dcu], axis=1)
        first = l == 0
        d_win, arrived = _mm_tn(h1, dproj, d, cols, tmm, True, f"dw_in_{l}",
                                [_device_blocks(d_wgate[1])] if first else [])
        landed.update(zip([1], arrived))
        dh1, arrived = _mm_nt_blocks([dproj], w_in[0], [w_in[1]], tmr, f"proj_in_bwd_{l}",
                                     [_device_blocks(d_win[1])] if first else [])
        landed.update(zip([0], arrived))
        dx, dxb, dg_mix, arrived = _rmsnorm_bwd(dh1, x0, norm_mix[l][None], dx1, tm, f"norm_mix_bwd_{l}",
                                                [_device_blocks(d_wup[1])] if first else [])
        landed.update(zip([2], arrived))
        grads[l] = (d_win, d_wgate, d_wup, d_wout, d_wdown)
        dq_gain = jnp.sum(dg_qk[0, :attn].reshape(nheads, HEAD_DIM), axis=0) * scale
        dk_gain = jnp.sum(dg_qk[0, attn:].reshape(nheads, HEAD_DIM), axis=0)
        small[l] = (dg_mix[0], dg_ffn[0], dq_gain, dk_gain, dconv[:3])
    return loss, dx, grads, small, landed


def kernel(x, norm_mix, w_in, q_norm, k_norm, conv_w, w_out, norm_ffn, w_gate, w_up, w_down, loss_target, m_norm_mix, m_w_in, m_q_norm, m_k_norm, m_conv_w, m_w_out, m_norm_ffn, m_w_gate, m_w_up, m_w_down, v_norm_mix, v_w_in, v_q_norm, v_k_norm, v_conv_w, v_w_out, v_norm_ffn, v_w_gate, v_w_up, v_w_down):
    depth, d, in_shard = w_in.shape
    ff_shard = w_gate.shape[2]
    ff_pad = in_shard
    conv_shard = conv_w.shape[2]
    xs = x.reshape(x.shape[-2], d)
    target = loss_target.reshape(xs.shape)

    pa = jnp.stack([w_in, _pad_to(w_gate, 2, ff_pad), _pad_to(w_up, 2, ff_pad)], axis=1)
    pa = pa.reshape(3 * depth, 1, d, in_shard).astype(BF16)
    pd = _pad_to(_pad_to(conv_w.reshape(depth * 3, conv_shard), 0, 8), 1, LANES)[None, None]
    g_in0, gd = _all_gather([pa[:1], pd], "gather_first")
    conv_full = gd[0, :, :depth * 3, :conv_shard].transpose(1, 0, 2).reshape(depth, 3, N_DEV * conv_shard)
    late_shards = [pa[1:3]] + ([pa[3:]] if depth > 1 else [])
    late_shards += [w_out.astype(BF16)[:, None], _pad_to(w_down, 1, ff_pad).astype(BF16)[:, None]]

    loss, grad_x, grads, small, landed = _local_step(xs, target, g_in0, late_shards, conv_full, norm_mix, q_norm,
                                                     k_norm, norm_ffn)

    x_, y_, c_ = _place()
    my = 4 * x_ + 2 * y_ + c_

    nconv = N_DEV * conv_shard
    rows = []
    for l in range(depth):
        g_mix, g_ffn, g_q, g_k, g_conv = small[l]
        qkrow = _pad_to(jnp.concatenate([g_q, g_k]), 0, d)
        rows += [g_mix[None], g_ffn[None], qkrow[None], _pad_to(g_conv, 1, d)]
    nrow = 6 * depth
    packed = jnp.concatenate(rows + [_pad_to(loss[:1], 1, d)], axis=0)
    packed = _pad_to(packed, 0, ((nrow + 1 + 7) // 8) * 8)
    summed = _all_reduce_small(packed, "reduce_small")
    loss_out = summed[nrow, 0]

    my1 = my.astype(jnp.int32).reshape(1)

    def big(j, w, m, v, tr, name):
        return _adamw_sharded([landed[5 * l + j] for l in range(depth)],
                              [_device_blocks(grads[l][j][0]) for l in range(depth)], my1, w, m, v, tr, name)

    res = {"w_in": big(0, w_in, m_w_in, v_w_in, 256, "adamw_in"),
           "w_gate": big(1, w_gate, m_w_gate, v_w_gate, 256, "adamw_gate"),
           "w_up": big(2, w_up, m_w_up, v_w_up, 256, "adamw_up"),
           "w_out": big(3, w_out, m_w_out, v_w_out, w_out.shape[1], "adamw_out"),
           "w_down": big(4, w_down, m_w_down, v_w_down, ff_shard // 2, "adamw_down")}

    g_rows, w_rows, m_rows, v_rows = [], [], [], []
    for l in range(depth):
        base = l * 6
        conv_g = lax.dynamic_slice(summed[base + 3:base + 6], (0, my * conv_shard), (3, conv_shard))
        g_rows += [summed[base:base + 3], _pad_to(conv_g, 1, d)]
        for dst, (nm, qn, kn, nf, cw) in ((w_rows, (norm_mix, q_norm, k_norm, norm_ffn, conv_w)),
                                          (m_rows, (m_norm_mix, m_q_norm, m_k_norm, m_norm_ffn, m_conv_w)),
                                          (v_rows, (v_norm_mix, v_q_norm, v_k_norm, v_norm_ffn, v_conv_w))):
            dst += [nm[l][None], nf[l][None], _pad_to(jnp.concatenate([qn[l], kn[l]]), 0, d)[None],
                    _pad_to(cw[l], 1, d)]
    prow = ((nrow + 7) // 8) * 8
    gs, ws, ms, vs = [_pad_to(jnp.concatenate(t, axis=0), 0, prow) for t in (g_rows, w_rows, m_rows, v_rows)]
    sm = _adamw(gs[None], None, ws, ms, vs, prow, "adamw_small")

    hd = q_norm.shape[1]

    def small_out(t, kind):
        per_layer = []
        for l in range(depth):
            base = l * 6
            per_layer.append({"norm_mix": t[base], "norm_ffn": t[base + 1], "q_norm": t[base + 2, :hd],
                              "k_norm": t[base + 2, hd:2 * hd], "conv_w": t[base + 3:base + 6, :conv_shard]}[kind])
        return jnp.stack(per_layer)

    def big_out(name, i):
        return res[name][i]

    outs = [loss_out, grad_x.reshape(x.shape)]
    for i in range(4):
        outs += [small_out(sm[i], "norm_mix"), big_out("w_in", i), small_out(sm[i], "q_norm"),
                 small_out(sm[i], "k_norm"), small_out(sm[i], "conv_w"), big_out("w_out", i),
                 small_out(sm[i], "norm_ffn"), big_out("w_gate", i), big_out("w_up", i), big_out("w_down", i)]
    return tuple(outs)
```

```python
import jax
import jax.numpy as jnp
from jax import lax
from jax.experimental import pallas as pl
from jax.experimental.pallas import tpu as pltpu

F32 = jnp.float32
BF16 = jnp.bfloat16
MESH = pl.DeviceIdType.MESH

N_DEV = 8
LANES = 128
HEAD_DIM = 64
KEY_CHUNK = 128
EPS = 1e-6
VMEM_LIMIT = 48 * 1024 * 1024

ADAM_LR = 0.001
ADAM_B1 = 0.9
ADAM_B2 = 0.999
ADAM_EPS = 1e-08
ADAM_WD = 0.01
ADAM_STEP = 10

NN = (((1,), (0,)), ((), ()))
NT = (((1,), (1,)), ((), ()))
TN = (((0,), (0,)), ((), ()))


def _dot(a, b, dims):
    return lax.dot_general(a.astype(BF16), b.astype(BF16), dims, preferred_element_type=F32)


def _cparams(*sem):
    return pltpu.CompilerParams(dimension_semantics=sem, vmem_limit_bytes=VMEM_LIMIT)


def _split_hi_lo(v):
    hi = v.astype(BF16)
    lo = (v - hi.astype(F32)).astype(BF16)
    return jnp.concatenate([hi, lo], axis=1)


def _rmsnorm_fwd(x, gain, tm, name):
    s, d = x.shape

    def body(x_ref, g_ref, o_ref):
        xv = x_ref[...]
        r = lax.rsqrt(jnp.mean(xv * xv, axis=-1, keepdims=True) + EPS)
        o_ref[...] = ((xv * r) * g_ref[...]).astype(o_ref.dtype)

    return pl.pallas_call(
        body, name=name, grid=(s // tm,),
        in_specs=[pl.BlockSpec((tm, d), lambda i: (i, 0)), pl.BlockSpec((1, d), lambda i: (0, 0))],
        out_specs=pl.BlockSpec((tm, d), lambda i: (i, 0)),
        out_shape=jax.ShapeDtypeStruct((s, d), BF16),
        compiler_params=_cparams("parallel"),
    )(x, gain)


def _group_mean_matrix():
    r = lax.broadcasted_iota(jnp.int32, (LANES, LANES), 0) // HEAD_DIM
    c = lax.broadcasted_iota(jnp.int32, (LANES, LANES), 1) // HEAD_DIM
    return jnp.where(r == c, 1.0 / HEAD_DIM, 0.0).astype(BF16)


def _group_mean(v, gm):
    hi = v.astype(BF16)
    lo = (v - hi.astype(F32)).astype(BF16)
    return _dot(hi, gm, NN) + _dot(lo, gm, NN)


def _qknorm_fwd(proj, gains, tm, name):
    s = proj.shape[0]
    ncol = gains.shape[1] // LANES

    def body(p_ref, g_ref, gm_ref, o_ref):
        xv = p_ref[...].astype(F32)
        r = lax.rsqrt(_group_mean(xv * xv, gm_ref[...]) + EPS)
        o_ref[...] = ((xv * r) * g_ref[...]).astype(o_ref.dtype)

    blk = pl.BlockSpec((tm, LANES), lambda i, j: (i, j))
    return pl.pallas_call(
        body, name=name, grid=(s // tm, ncol),
        in_specs=[blk, pl.BlockSpec((1, LANES), lambda i, j: (0, j)),
                  pl.BlockSpec((LANES, LANES), lambda i, j: (0, 0))],
        out_specs=blk,
        out_shape=jax.ShapeDtypeStruct((s, ncol * LANES), BF16),
        compiler_params=_cparams("parallel", "parallel"),
    )(proj, gains, _group_mean_matrix())


def _qknorm_bwd(dqk, proj, gains, tm, name):
    s = proj.shape[0]
    ncol = gains.shape[1] // LANES
    nsteps = s // tm

    def body(dy_ref, p_ref, g_ref, gm_ref, dx_ref, dg_ref):
        i = pl.program_id(1)
        gm = gm_ref[...]
        xv = p_ref[...].astype(F32)
        r = lax.rsqrt(_group_mean(xv * xv, gm) + EPS)
        xhat = xv * r
        dy = dy_ref[...]
        dxh = dy * g_ref[...]
        proj_ = _group_mean(dxh * xhat, gm)
        dx_ref[...] = (r * (dxh - xhat * proj_)).astype(dx_ref.dtype)
        part = jnp.sum((dy * xhat).reshape(tm // 8, 8, LANES), axis=0)

        @pl.when(i == 0)
        def _():
            dg_ref[...] = part

        @pl.when(i > 0)
        def _():
            dg_ref[...] += part

        @pl.when(i == nsteps - 1)
        def _():
            dg_ref[...] = jnp.broadcast_to(jnp.sum(dg_ref[...], axis=0, keepdims=True), (8, LANES))

    blk = pl.BlockSpec((tm, LANES), lambda j, i: (i, j))
    return pl.pallas_call(
        body, name=name, grid=(ncol, nsteps),
        in_specs=[blk, blk, pl.BlockSpec((1, LANES), lambda j, i: (0, j)),
                  pl.BlockSpec((LANES, LANES), lambda j, i: (0, 0))],
        out_specs=[blk, pl.BlockSpec((8, LANES), lambda j, i: (0, j))],
        out_shape=[jax.ShapeDtypeStruct((s, ncol * LANES), BF16),
                   jax.ShapeDtypeStruct((8, ncol * LANES), F32)],
        compiler_params=_cparams("parallel", "arbitrary"),
    )(dqk, proj, gains, _group_mean_matrix())


CONV_ROWS = 256
HALO = 8


def _conv_fwd(proj, conv_w8, name):
    s = proj.shape[0]
    nblk = conv_w8.shape[1] // LANES
    first = 3 * nblk
    nchunk = s // CONV_ROWS

    def body(cb_ref, cc_ref, cu_ref, w_ref, y_ref, hpad):
        hpad[pl.ds(0, 2 * HALO), :] = jnp.zeros((2 * HALO, LANES), F32)

        def fill(i, _):
            r0 = pl.multiple_of(i * CONV_ROWS, CONV_ROWS)
            hpad[pl.ds(r0 + 2 * HALO, CONV_ROWS), :] = (
                cc_ref[pl.ds(r0, CONV_ROWS), :].astype(F32) * cu_ref[pl.ds(r0, CONV_ROWS), :].astype(F32))
            return 0

        lax.fori_loop(0, nchunk, fill, 0)
        w0, w1, w2 = w_ref[0:1, :], w_ref[1:2, :], w_ref[2:3, :]

        def conv(i, _):
            r0 = pl.multiple_of(i * CONV_ROWS, CONV_ROWS)
            win = hpad[pl.ds(r0 + HALO, CONV_ROWS + HALO), :]
            c = (w2 * win[HALO:] + w1 * pltpu.roll(win, 1, 0)[HALO:] + w0 * pltpu.roll(win, 2, 0)[HALO:])
            y_ref[pl.ds(r0, CONV_ROWS), :] = (cb_ref[pl.ds(r0, CONV_ROWS), :].astype(F32) * c).astype(y_ref.dtype)
            return 0

        lax.fori_loop(0, nchunk, conv, 0)

    def col(off):
        return pl.BlockSpec((s, LANES), lambda j: (0, off + j))

    return pl.pallas_call(
        body, name=name, grid=(nblk,),
        in_specs=[col(first), col(first + nblk), col(first + 2 * nblk), pl.BlockSpec((8, LANES), lambda j: (0, j))],
        out_specs=pl.BlockSpec((s, LANES), lambda j: (0, j)),
        out_shape=jax.ShapeDtypeStruct((s, nblk * LANES), BF16),
        scratch_shapes=[pltpu.VMEM((s + 2 * HALO, LANES), F32)],
        compiler_params=_cparams("parallel"),
    )(proj, proj, proj, conv_w8)


def _conv_bwd(dmix, proj, conv_w8, name):
    s = proj.shape[0]
    nblk = conv_w8.shape[1] // LANES
    first = 3 * nblk
    nchunk = s // CONV_ROWS

    def body(dy_ref, cb_ref, cc_ref, cu_ref, w_ref, dcb_ref, dcc_ref, dcu_ref, dw_ref, hpad, dcpad):
        hpad[pl.ds(0, 2 * HALO), :] = jnp.zeros((2 * HALO, LANES), F32)
        dcpad[pl.ds(s, 2 * HALO), :] = jnp.zeros((2 * HALO, LANES), F32)

        def fill(i, _):
            r0 = pl.multiple_of(i * CONV_ROWS, CONV_ROWS)
            hpad[pl.ds(r0 + 2 * HALO, CONV_ROWS), :] = (
                cc_ref[pl.ds(r0, CONV_ROWS), :].astype(F32) * cu_ref[pl.ds(r0, CONV_ROWS), :].astype(F32))
            return 0

        lax.fori_loop(0, nchunk, fill, 0)
        w0, w1, w2 = w_ref[0:1, :], w_ref[1:2, :], w_ref[2:3, :]

        def fold(v):
            return jnp.sum(v.reshape(CONV_ROWS // 8, 8, LANES), axis=0)

        def first_pass(i, acc):
            a0, a1, a2 = acc
            r0 = pl.multiple_of(i * CONV_ROWS, CONV_ROWS)
            win = hpad[pl.ds(r0 + HALO, CONV_ROWS + HALO), :]
            h0 = win[HALO:]
            h1 = pltpu.roll(win, 1, 0)[HALO:]
            h2 = pltpu.roll(win, 2, 0)[HALO:]
            c = w2 * h0 + w1 * h1 + w0 * h2
            dy = dy_ref[pl.ds(r0, CONV_ROWS), :]
            dcb_ref[pl.ds(r0, CONV_ROWS), :] = (dy * c).astype(dcb_ref.dtype)
            dc = dy * cb_ref[pl.ds(r0, CONV_ROWS), :].astype(F32)
            dcpad[pl.ds(r0, CONV_ROWS), :] = dc
            return a0 + fold(dc * h2), a1 + fold(dc * h1), a2 + fold(dc * h0)

        z8 = jnp.zeros((8, LANES), F32)
        a0, a1, a2 = lax.fori_loop(0, nchunk, first_pass, (z8, z8, z8))
        dw_ref[...] = jnp.concatenate(
            [jnp.sum(a0, axis=0, keepdims=True), jnp.sum(a1, axis=0, keepdims=True),
             jnp.sum(a2, axis=0, keepdims=True), jnp.zeros((5, LANES), F32)], axis=0)

        def second_pass(i, _):
            r0 = pl.multiple_of(i * CONV_ROWS, CONV_ROWS)
            win = dcpad[pl.ds(r0, CONV_ROWS + HALO), :]
            n = CONV_ROWS + HALO
            dh = (w2 * win[:CONV_ROWS] + w1 * pltpu.roll(win, n - 1, 0)[:CONV_ROWS]
                  + w0 * pltpu.roll(win, n - 2, 0)[:CONV_ROWS])
            dcc_ref[pl.ds(r0, CONV_ROWS), :] = (dh * cu_ref[pl.ds(r0, CONV_ROWS), :].astype(F32)).astype(dcc_ref.dtype)
            dcu_ref[pl.ds(r0, CONV_ROWS), :] = (dh * cc_ref[pl.ds(r0, CONV_ROWS), :].astype(F32)).astype(dcu_ref.dtype)
            return 0

        lax.fori_loop(0, nchunk, second_pass, 0)

    def col(off):
        return pl.BlockSpec((s, LANES), lambda j: (0, off + j))

    out = pl.BlockSpec((s, LANES), lambda j: (0, j))
    return pl.pallas_call(
        body, name=name, grid=(nblk,),
        in_specs=[col(nblk), col(first), col(first + nblk), col(first + 2 * nblk),
                  pl.BlockSpec((8, LANES), lambda j: (0, j))],
        out_specs=[out, out, out, pl.BlockSpec((8, LANES), lambda j: (0, j))],
        out_shape=[jax.ShapeDtypeStruct((s, nblk * LANES), BF16)] * 3 + [jax.ShapeDtypeStruct((8, nblk * LANES), F32)],
        scratch_shapes=[pltpu.VMEM((s + 2 * HALO, LANES), F32), pltpu.VMEM((s + 2 * HALO, LANES), F32)],
        compiler_params=_cparams("parallel"),
    )(dmix, proj, proj, proj, conv_w8)


LOG2E = 1.4426950408889634
LN2 = 0.6931471805599453
NEG_BIG = -1e30
SATURATED = 160.0


def _cumsum_matrix(kind):
    j = lax.broadcasted_iota(jnp.int32, (KEY_CHUNK, 2 * KEY_CHUNK), 0)
    c = lax.broadcasted_iota(jnp.int32, (KEY_CHUNK, 2 * KEY_CHUNK), 1)
    tri = {"after": j > c, "upto": j <= c, "before": j < c}[kind]
    return jnp.where((c >= KEY_CHUNK) | tri, 1.0, 0.0).astype(BF16)


def _stack_heads(t, m0):
    zero = jnp.zeros_like(t)
    return jnp.concatenate([jnp.where(m0, t, zero), jnp.where(m0, zero, t)], axis=0)


def _softplus2(z):
    sp = jnp.maximum(z, 0.0) + jnp.log2(1.0 + jnp.exp2(-jnp.abs(z)))
    return sp, z - sp


def _key_chunk(ref, kc):
    return ref[pl.ds(pl.multiple_of(kc * KEY_CHUNK, KEY_CHUNK), KEY_CHUNK), :]


def _attn_bwd(qk, proj, dmix, rtot, used, tq, name, travel=()):
    s = qk.shape[0]
    nhp = qk.shape[1] // (2 * LANES)
    nc = tq // KEY_CHUNK
    nq = s // tq
    nt = len(travel)

    def body(used_ref, q_ref, k_ref, v_ref, do_ref, r_ref, cmi_ref, cme_ref, bias_ref, dq_ref, dk_ref, dv_ref,
             z_refs, ls_refs, sig_refs, sp_refs, gb_refs, pr_ref, gs_ref, copies):
        qi = pl.program_id(1)

        @pl.when(qi == 0)
        def _():
            dk_ref[...] = jnp.zeros_like(dk_ref)
            dv_ref[...] = jnp.zeros_like(dv_ref)

        if copies is not None:
            @pl.when(jnp.logical_and(pl.program_id(0) == 0, qi == 0))
            def _():
                _exchange_begin(copies)

        nslots = (qi + 1) * nc
        walked = used_ref[pl.program_id(0), qi].astype(jnp.int32)
        first = jnp.clip(nslots - walked, 0, nslots - nc) // nc * nc
        m0 = lax.broadcasted_iota(jnp.int32, (1, LANES), 1) < HEAD_DIM
        qs = _stack_heads(q_ref[...], m0)
        do = do_ref[...]
        dos = _stack_heads(do.astype(BF16), m0)
        dosl = _stack_heads((do * LN2).astype(BF16), m0)
        cmi = cmi_ref[...]
        cme = cme_ref[...]

        def chunk_at(i):
            return jnp.clip(i, first, nslots - 1)

        def scores(kc):
            return _dot(qs, _key_chunk(k_ref, kc), NT)

        def weights(ls, cs, da, pr, kc):
            a = jnp.exp2(ls - (pr - cs[:, :KEY_CHUNK]))
            gb = (a * da).astype(BF16)
            ks = pl.multiple_of(kc * KEY_CHUNK, KEY_CHUNK)
            dv_ref[pl.ds(ks, KEY_CHUNK), :] += _dot(a, dos, TN)
            return gb, jnp.exp2(ls), pr - cs[:, KEY_CHUNK:]

        def score_grads(gb, sig, cg, gs, dq, kc):
            dzb = (gb.astype(F32) * (1.0 - sig) - sig * (gs + cg[:, :KEY_CHUNK])).astype(BF16)
            ks = pl.multiple_of(kc * KEY_CHUNK, KEY_CHUNK)
            dk_ref[pl.ds(ks, KEY_CHUNK), :] += _dot(dzb, qs, TN)
            dq = dq + _dot(jnp.concatenate([dzb[:tq], dzb[tq:]], axis=1), _stack_heads(_key_chunk(k_ref, kc), m0), NN)
            return gs + cg[:, KEY_CHUNK:], dq

        def step(i, par, bias=None):
            cur, prv = par, 1 - par
            k1, k2 = chunk_at(i - 1), chunk_at(i - 2)
            z_next = scores(chunk_at(i + 1))
            cs = _dot(sp_refs[prv][...], cmi, NN)
            da = _dot(dosl, _key_chunk(v_ref, k1), NT)
            cg = _dot(gb_refs[cur][...], cme, NN)
            z = z_refs[cur][...]
            if bias is not None:
                z = z + bias
            sp, ls = _softplus2(z)
            sp_refs[cur][...] = sp.astype(BF16)
            ls_refs[cur][...] = ls
            gs, dq = score_grads(gb_refs[cur][...], sig_refs[cur][...], cg, gs_ref[...], dq_ref[...], k2)
            gs_ref[...] = gs
            dq_ref[...] = dq
            gb, sig, pr = weights(ls_refs[prv][...], cs, da, pr_ref[...], k1)
            gb_refs[prv][...] = gb
            sig_refs[prv][...] = sig
            pr_ref[...] = pr
            z_refs[prv][...] = z_next

        pr_ref[...] = jnp.concatenate([r_ref[:, :LANES], r_ref[:, LANES:]], axis=0)
        gs_ref[...] = jnp.zeros((2 * tq, LANES), F32)
        dq_ref[...] = jnp.zeros((tq, LANES), F32)
        z_refs[0][...] = scores(first)
        sp_refs[1][...] = jnp.zeros((2 * tq, LANES), BF16)
        ls_refs[1][...] = jnp.full((2 * tq, LANES), NEG_BIG, F32)
        gb_refs[0][...] = jnp.zeros((2 * tq, LANES), BF16)
        sig_refs[0][...] = jnp.zeros((2 * tq, LANES), F32)

        def two_steps(j, _):
            step(2 * j, 0)
            step(2 * j + 1, 1)
            return 0

        lax.fori_loop(first // 2, nslots // 2 - 1, two_steps, 0)
        step(nslots - 2, 0, bias_ref[0])
        step(nslots - 1, 1, bias_ref[1])
        k1, k2 = chunk_at(nslots - 1), chunk_at(nslots - 2)
        gb, sig, _ = weights(ls_refs[1][...], _dot(sp_refs[1][...], cmi, NN),
                             _dot(dosl, _key_chunk(v_ref, k1), NT), pr_ref[...], k1)
        gb2 = gb_refs[0][...]
        gs, dq = score_grads(gb2, sig_refs[0][...], _dot(gb2, cme, NN), gs_ref[...], dq_ref[...], k2)
        _, dq = score_grads(gb, sig, _dot(gb, cme, NN), gs, dq, k1)
        dq_ref[...] = dq

        if copies is not None:
            @pl.when(jnp.logical_and(pl.program_id(0) == nhp - 1, qi == nq - 1))
            def _():
                _exchange_finish(copies)

    def wrapped(*refs):
        ins, rest = refs[:9], refs[9:]
        srcs, rest = rest[:nt], rest[nt:]
        outs, rest = rest[:3], rest[3:]
        lands, rest = rest[:nt], rest[nt:]
        z0, z1, ls0, ls1, sg0, sg1, sp0, sp1, gb0, gb1, pr_ref, gs_ref = rest[:12]
        copies = _exchange_copies(srcs, lands, *rest[12:]) if nt else None
        body(*ins, *outs, (z0, z1), (ls0, ls1), (sg0, sg1), (sp0, sp1), (gb0, gb1), pr_ref, gs_ref, copies)

    assert nc == 2
    bias = _diag_bias(tq, True)
    bias = jnp.concatenate([bias[:, :, :KEY_CHUNK], bias[:, :, KEY_CHUNK:]], axis=1)
    qblk = pl.BlockSpec((tq, LANES), lambda p, i: (i, p))
    full = pl.BlockSpec((s, LANES), lambda p, i: (0, p))
    cmspec = pl.BlockSpec((KEY_CHUNK, 2 * KEY_CHUNK), lambda p, i: (0, 0))
    anyspec = pl.BlockSpec(memory_space=pl.ANY)
    shape = jax.ShapeDtypeStruct((s, nhp * LANES), F32)
    f32buf = pltpu.VMEM((2 * tq, LANES), F32)
    bf16buf = pltpu.VMEM((2 * tq, LANES), BF16)
    outs = pl.pallas_call(
        wrapped, name=name, grid=(nhp, nq),
        in_specs=[pl.BlockSpec(memory_space=pltpu.SMEM),
                  qblk,
                  pl.BlockSpec((s, LANES), lambda p, i: (0, nhp + p)),
                  pl.BlockSpec((s, LANES), lambda p, i: (0, 2 * nhp + p)),
                  qblk,
                  pl.BlockSpec((tq, 2 * LANES), lambda p, i: (i, p)),
                  cmspec, cmspec,
                  pl.BlockSpec((nc, 2 * tq, LANES), lambda p, i: (0, 0, 0))] + [anyspec] * nt,
        out_specs=[qblk, full, full] + [anyspec] * nt,
        out_shape=[shape, shape, shape] + [jax.ShapeDtypeStruct(t.shape, t.dtype) for t in travel],
        scratch_shapes=[f32buf] * 6 + [bf16buf] * 4 + [f32buf] * 2 + (_exchange_scratch(nt) if nt else []),
        compiler_params=_cparams("arbitrary", "arbitrary"),
    )(used, qk, qk, proj, dmix, rtot, _cumsum_matrix("upto"), _cumsum_matrix("before"), bias, *travel)
    return outs[0], outs[1], outs[2], list(outs[3:])


def _pair_cumsum_matrix(kind):
    j = lax.broadcasted_iota(jnp.int32, (2 * KEY_CHUNK, 4 * KEY_CHUNK), 0)
    c = lax.broadcasted_iota(jnp.int32, (2 * KEY_CHUNK, 4 * KEY_CHUNK), 1)
    same_head = (j // KEY_CHUNK) == ((c // KEY_CHUNK) % 2)
    jj, cc = j % KEY_CHUNK, c % KEY_CHUNK
    tri = {"after": jj > cc, "upto": jj <= cc, "before": jj < cc}[kind]
    return jnp.where(same_head & ((c >= 2 * KEY_CHUNK) | tri), 1.0, 0.0).astype(BF16)


def _diag_bias(tq, ascending):
    nc = tq // KEY_CHUNK
    shape = (nc, tq, 2 * KEY_CHUNK)
    d = lax.broadcasted_iota(jnp.int32, shape, 0)
    r = lax.broadcasted_iota(jnp.int32, shape, 1)
    c = lax.broadcasted_iota(jnp.int32, shape, 2) % KEY_CHUNK
    chunk = d if ascending else nc - 1 - d
    return jnp.where(chunk * KEY_CHUNK + c < r, 0.0, NEG_BIG).astype(F32)


def _attn_fwd(qk, proj, tq, name, shards=()):
    s = qk.shape[0]
    nhp = qk.shape[1] // (2 * LANES)
    nc = tq // KEY_CHUNK
    nq = s // tq
    ng = len(shards)
    assert nc == 2
    w = 2 * KEY_CHUNK

    def body(q_ref, k_ref, v_ref, cm_ref, bias_ref, o_ref, r_ref, used_ref, z_refs, ls_refs, cs_refs, ct_refs,
             sp_refs, ab_refs, acc_ref, gather):
        qi = pl.program_id(1)
        if gather is not None:
            @pl.when(jnp.logical_and(pl.program_id(0) == 0, qi == 0))
            def _():
                gather.begin()

        nslots = (qi + 1) * nc
        m0 = lax.broadcasted_iota(jnp.int32, (1, LANES), 1) < HEAD_DIM
        q = q_ref[...]
        cm = cm_ref[...]

        def chunk_at(i):
            return jnp.clip(nslots - 1 - i, 0, nslots - 1)

        def scores(kc):
            return _dot(q, _stack_heads(_key_chunk(k_ref, kc), m0), NT)

        def values(ab, kc):
            return _dot(ab, _stack_heads(_key_chunk(v_ref, kc), m0), NN)

        def step(i, par, bias=None, stages="zscwv"):
            cur, prv = par, 1 - par
            if "z" in stages:
                z_next = scores(chunk_at(i + 1))
            if "c" in stages:
                cs = _dot(sp_refs[prv][...], cm, NN)
            if "v" in stages:
                pv = values(ab_refs[prv][...], chunk_at(i - 3))
            if "w" in stages:
                rs = r_ref[...]
                r_ref[...] = rs + ct_refs[cur][...]
                ab_refs[cur][...] = jnp.exp2(ls_refs[cur][...] - cs_refs[cur][...] - rs).astype(BF16)
            if "s" in stages:
                z = z_refs[cur][...]
                if bias is not None:
                    z = z + bias
                sp, ls = _softplus2(z)
                sp_refs[cur][...] = sp.astype(BF16)
                ls_refs[cur][...] = ls
            if "v" in stages:
                acc_ref[...] += pv
            if "c" in stages:
                cs_refs[prv][...] = cs[:, :w]
                ct_refs[prv][...] = cs[:, w:]
            if "z" in stages:
                z_refs[prv][...] = z_next

        z_refs[0][...] = scores(chunk_at(0))
        ab_refs[1][...] = jnp.zeros((tq, w), BF16)
        r_ref[...] = jnp.zeros((tq, w), F32)
        acc_ref[...] = jnp.zeros((tq, LANES), F32)
        step(0, 0, bias_ref[0], stages="zs")
        step(1, 1, bias_ref[1], stages="zsc")

        def two_steps(carry):
            j, _ = carry
            step(2 * j, 0)
            step(2 * j + 1, 1)
            return j + 1, jnp.min(r_ref[...])

        pairs, low = lax.while_loop(lambda c: jnp.logical_and(c[0] < nslots // 2, c[1] < SATURATED), two_steps,
                                    (jnp.int32(1), jnp.min(r_ref[...])))
        entered = 2 * pairs
        saturated = low >= SATURATED

        @pl.when(saturated)
        def _():
            step(entered, 0, stages="v")

        @pl.when(jnp.logical_not(saturated))
        def _():
            step(entered, 0, stages="cwv")
            step(entered + 1, 1, stages="wv")
            step(entered + 2, 0, stages="v")

        o_ref[...] = acc_ref[...].astype(o_ref.dtype)
        used_ref[pl.program_id(0), qi] = jnp.where(saturated, entered - 2, entered).astype(F32)

        if gather is not None:
            @pl.when(jnp.logical_and(pl.program_id(0) == nhp - 1, qi == nq // 2))
            def _():
                gather.relay()

            @pl.when(jnp.logical_and(pl.program_id(0) == nhp - 1, qi == nq - 1))
            def _():
                gather.finish()

    def wrapped(*refs):
        ins, rest = refs[:5], refs[5:]
        srcs, rest = rest[:ng], rest[ng:]
        outs, rest = rest[:3], rest[3:]
        dsts, scratch = rest[:ng], rest[ng:]
        z, ls, cs, ct, sp, ab = [scratch[2 * j:2 * j + 2] for j in range(6)]
        gather = _Gather(srcs, dsts, *scratch[13:]) if ng else None
        body(*ins, *outs, z, ls, cs, ct, sp, ab, scratch[12], gather)

    f32buf = pltpu.VMEM((tq, w), F32)
    bf16buf = pltpu.VMEM((tq, w), BF16)
    anyspec = pl.BlockSpec(memory_space=pl.ANY)
    outs = pl.pallas_call(
        wrapped, name=name, grid=(nhp, nq),
        in_specs=[pl.BlockSpec((tq, LANES), lambda p, i: (i, p)),
                  pl.BlockSpec((s, LANES), lambda p, i: (0, nhp + p)),
                  pl.BlockSpec((s, LANES), lambda p, i: (0, 2 * nhp + p)),
                  pl.BlockSpec((w, 2 * w), lambda p, i: (0, 0)),
                  pl.BlockSpec((nc, tq, w), lambda p, i: (0, 0, 0))] + [anyspec] * ng,
        out_specs=[pl.BlockSpec((tq, LANES), lambda p, i: (i, p)),
                   pl.BlockSpec((tq, w), lambda p, i: (i, p)),
                   pl.BlockSpec(memory_space=pltpu.SMEM)] + [anyspec] * ng,
        out_shape=[jax.ShapeDtypeStruct((s, nhp * LANES), BF16),
                   jax.ShapeDtypeStruct((s, nhp * w), F32),
                   jax.ShapeDtypeStruct((nhp, nq), F32)] + _gathered_shapes(shards),
        scratch_shapes=([f32buf] * 8 + [bf16buf] * 4 + [pltpu.VMEM((tq, LANES), F32)]
                        + (_gather_scratch(ng) if ng else [])),
        compiler_params=_cparams("arbitrary", "arbitrary"),
    )(qk, qk, proj, _pair_cumsum_matrix("after"), _diag_bias(tq, False), *shards)
    return outs[0], outs[1], outs[2], list(outs[3:])


BLOCK_PAIR = 2


def _side_by_side(b_ref):
    return jnp.concatenate([b_ref[p] for p in range(BLOCK_PAIR)], axis=1)


def _mm_blocks(h, ga, widx, tm, name):
    s, d = h.shape
    nb, cols = ga.shape[1], ga.shape[3]

    def body(a_ref, b_ref, o_ref):
        o_ref[...] = _dot(a_ref[...], _side_by_side(b_ref), NN).astype(o_ref.dtype)

    return pl.pallas_call(
        body, name=name, grid=(s // tm, nb // BLOCK_PAIR),
        in_specs=[pl.BlockSpec((tm, d), lambda i, j: (i, 0)),
                  pl.BlockSpec((None, BLOCK_PAIR, d, cols), lambda i, j: (widx, j, 0, 0))],
        out_specs=pl.BlockSpec((tm, BLOCK_PAIR * cols), lambda i, j: (i, j)),
        out_shape=jax.ShapeDtypeStruct((s, nb * cols), BF16),
        compiler_params=_cparams("parallel", "arbitrary"),
    )(h, ga)


def _mm_swiglu(h, ga, gidx, uidx, tm, name):
    s, d = h.shape
    nb, cols = ga.shape[1], ga.shape[3]

    def body(a_ref, bg_ref, bu_ref, g_ref, u_ref, act_ref):
        a = a_ref[...]
        g = _dot(a, _side_by_side(bg_ref), NN)
        u = _dot(a, _side_by_side(bu_ref), NN)
        g_ref[...] = g.astype(g_ref.dtype)
        u_ref[...] = u.astype(u_ref.dtype)
        act_ref[...] = (g * (1.0 / (1.0 + jnp.exp(-g))) * u).astype(act_ref.dtype)

    def wspec(idx):
        return pl.BlockSpec((None, BLOCK_PAIR, d, cols), lambda i, j: (idx, j, 0, 0))

    out = pl.BlockSpec((tm, BLOCK_PAIR * cols), lambda i, j: (i, j))
    shape = jax.ShapeDtypeStruct((s, nb * cols), BF16)
    return pl.pallas_call(
        body, name=name, grid=(s // tm, nb // BLOCK_PAIR),
        in_specs=[pl.BlockSpec((tm, d), lambda i, j: (i, 0)), wspec(gidx), wspec(uidx)],
        out_specs=[out, out, out], out_shape=[shape, shape, shape],
        compiler_params=_cparams("parallel", "arbitrary"),
    )(h, ga, ga)


def _mm_residual_norm(a, w3, lidx, res, gain, tm, name):
    s, k = a.shape
    n = w3.shape[2]

    def body(a_ref, b_ref, r_ref, g_ref, o_ref, h_ref):
        xv = r_ref[...] + _dot(a_ref[...], b_ref[...], NN)
        o_ref[...] = xv
        r = lax.rsqrt(jnp.mean(xv * xv, axis=-1, keepdims=True) + EPS)
        h_ref[...] = ((xv * r) * g_ref[...]).astype(h_ref.dtype)

    row = pl.BlockSpec((tm, n), lambda i: (i, 0))
    return pl.pallas_call(
        body, name=name, grid=(s // tm,),
        in_specs=[pl.BlockSpec((tm, k), lambda i: (i, 0)),
                  pl.BlockSpec((None, k, n), lambda i: (lidx, 0, 0), pipeline_mode=pl.Buffered(1)),
                  row, pl.BlockSpec((1, n), lambda i: (0, 0))],
        out_specs=[row, row],
        out_shape=[jax.ShapeDtypeStruct((s, n), F32), jax.ShapeDtypeStruct((s, n), BF16)],
        compiler_params=_cparams("parallel"),
    )(a, w3, res, gain)


def _mm_residual_loss(a, w3, lidx, res, target, tm, name):
    s, k = a.shape
    n = w3.shape[2]
    nsteps = s // tm

    def body(a_ref, b_ref, r_ref, t_ref, dy_ref, dyb_ref, l_ref, acc):
        i = pl.program_id(0)
        diff = r_ref[...] + _dot(a_ref[...], b_ref[...], NN) - t_ref[...]
        dy_ref[...] = diff * (1.0 / n)
        dyb_ref[...] = (diff * (1.0 / n)).astype(dyb_ref.dtype)
        part = jnp.sum((diff * diff).reshape(tm // 8, 8, n), axis=0)

        @pl.when(i == 0)
        def _():
            acc[...] = part

        @pl.when(i > 0)
        def _():
            acc[...] += part

        @pl.when(i == nsteps - 1)
        def _():
            tot = jnp.sum(jnp.sum(acc[...], axis=1, keepdims=True), axis=0, keepdims=True)
            l_ref[...] = jnp.broadcast_to(tot * (0.5 / n), (8, LANES))

    row = pl.BlockSpec((tm, n), lambda i: (i, 0))
    return pl.pallas_call(
        body, name=name, grid=(nsteps,),
        in_specs=[pl.BlockSpec((tm, k), lambda i: (i, 0)),
                  pl.BlockSpec((None, k, n), lambda i: (lidx, 0, 0), pipeline_mode=pl.Buffered(1)),
                  row, row],
        out_specs=[row, row, pl.BlockSpec((8, LANES), lambda i: (0, 0))],
        out_shape=[jax.ShapeDtypeStruct((s, n), F32), jax.ShapeDtypeStruct((s, n), BF16),
                   jax.ShapeDtypeStruct((8, LANES), F32)],
        scratch_shapes=[pltpu.VMEM((8, n), F32)],
        compiler_params=_cparams("arbitrary"),
    )(a, w3, res, target)


def _mm_nt(a, w3, lidx, tm, tn, name):
    s, k = a.shape
    n = w3.shape[1]

    def body(a_ref, b_ref, o_ref):
        o_ref[...] = _dot(a_ref[...], b_ref[...], NT)

    return pl.pallas_call(
        body, name=name, grid=(s // tm, n // tn),
        in_specs=[pl.BlockSpec((tm, k), lambda i, j: (i, 0)),
                  pl.BlockSpec((None, tn, k), lambda i, j: (lidx, j, 0))],
        out_specs=pl.BlockSpec((tm, tn), lambda i, j: (i, j)),
        out_shape=jax.ShapeDtypeStruct((s, n), F32),
        compiler_params=_cparams("parallel", "arbitrary"),
    )(a, w3)


def _mm_nt_swiglu_bwd(dx, wd3, lidx, g, u, tm, name):
    s, d = dx.shape
    cols = BLOCK_PAIR * (g.shape[1] // N_DEV)

    def body(a_ref, b_ref, g_ref, u_ref, dg_ref, du_ref):
        dact = _dot(a_ref[...], b_ref[...], NT)
        gv = g_ref[...].astype(F32)
        sig = 1.0 / (1.0 + jnp.exp(-gv))
        du_ref[...] = (dact * (gv * sig)).astype(du_ref.dtype)
        dg_ref[...] = (dact * u_ref[...].astype(F32) * (sig * (1.0 + gv * (1.0 - sig)))).astype(dg_ref.dtype)

    blk = pl.BlockSpec((tm, cols), lambda i, j: (i, j))
    shape = jax.ShapeDtypeStruct(g.shape, BF16)
    return pl.pallas_call(
        body, name=name, grid=(s // tm, N_DEV // BLOCK_PAIR),
        in_specs=[pl.BlockSpec((tm, d), lambda i, j: (i, 0)),
                  pl.BlockSpec((None, cols, d), lambda i, j: (lidx, j, 0)), blk, blk],
        out_specs=[blk, blk], out_shape=[shape, shape],
        compiler_params=_cparams("parallel", "arbitrary"),
    )(dx, wd3, g, u)


def _mm_nt_norm_bwd(das, ga, widxs, x, gain, dres, tm, name, travel=()):
    s = das[0].shape[0]
    nb, d, cols = ga.shape[1], ga.shape[2], ga.shape[3]
    nw = len(das)
    nsteps = s // tm

    def body(*refs):
        a_refs, b_refs = refs[:nw], refs[nw:2 * nw]
        x_ref, g_ref, dres_ref, dx_ref, dxb_ref, dg_ref = refs[2 * nw:]
        i = pl.program_id(0)
        dhv = None
        wide = BLOCK_PAIR * cols
        for w in range(nw):
            for k in range(nb // BLOCK_PAIR):
                b = jnp.concatenate([b_refs[w][BLOCK_PAIR * k + p] for p in range(BLOCK_PAIR)], axis=1)
                part = _dot(a_refs[w][:, k * wide:(k + 1) * wide], b, NT)
                dhv = part if dhv is None else dhv + part
        xv = x_ref[...]
        r = lax.rsqrt(jnp.mean(xv * xv, axis=-1, keepdims=True) + EPS)
        xhat = xv * r
        dxh = dhv * g_ref[...]
        dxv = dres_ref[...] + r * (dxh - xhat * jnp.mean(dxh * xhat, axis=-1, keepdims=True))
        dx_ref[...] = dxv
        dxb_ref[...] = dxv.astype(dxb_ref.dtype)
        part = jnp.sum((dhv * xhat).reshape(tm // 8, 8, d), axis=0)

        @pl.when(i == 0)
        def _():
            dg_ref[...] = part

        @pl.when(i > 0)
        def _():
            dg_ref[...] += part

        @pl.when(i == nsteps - 1)
        def _():
            dg_ref[...] = jnp.broadcast_to(jnp.sum(dg_ref[...], axis=0, keepdims=True), (8, d))

    def wspec(idx):
        return pl.BlockSpec((None, nb, d, cols), lambda i: (idx, 0, 0, 0), pipeline_mode=pl.Buffered(1))

    row = pl.BlockSpec((tm, d), lambda i: (i, 0))
    body, more_in, more_out, more_shapes, more_scratch = _host_exchange(body, 2 * nw + 3, 3, travel, (nsteps,))
    outs = pl.pallas_call(
        body, name=name, grid=(nsteps,),
        in_specs=([pl.BlockSpec((tm, nb * cols), lambda i: (i, 0))] * nw + [wspec(i) for i in widxs]
                  + [row, pl.BlockSpec((1, d), lambda i: (0, 0)), row] + more_in),
        out_specs=[row, row, pl.BlockSpec((8, d), lambda i: (0, 0))] + more_out,
        out_shape=[jax.ShapeDtypeStruct((s, d), F32), jax.ShapeDtypeStruct((s, d), BF16),
                   jax.ShapeDtypeStruct((8, d), F32)] + more_shapes,
        scratch_shapes=more_scratch,
        compiler_params=_cparams("arbitrary"),
    )(*das, *([ga] * nw), x, gain, dres, *travel)
    return outs[0], outs[1], outs[2], list(outs[3:])


def _mm_tn(a, b, ta, tb, tk, out_blocks, name, travel=()):
    s, ka = a.shape
    nb = b.shape[1]
    nk = s // tk
    cols = tb
    if out_blocks:
        tb = BLOCK_PAIR * cols

    def body(a_ref, b_ref, o_ref, ob_ref):
        k = pl.program_id(2)
        part = _dot(a_ref[...], b_ref[...], TN)

        def put(first):
            if out_blocks:
                for p in range(BLOCK_PAIR):
                    piece = part[:, p * cols:(p + 1) * cols]
                    o_ref[p] = piece if first else o_ref[p] + piece
            else:
                o_ref[...] = part if first else o_ref[...] + part

        @pl.when(k == 0)
        def _():
            put(True)

        @pl.when(k > 0)
        def _():
            put(False)

        @pl.when(k == nk - 1)
        def _():
            ob_ref[...] = o_ref[...].astype(ob_ref.dtype)

    if out_blocks:
        out_spec = pl.BlockSpec((BLOCK_PAIR, ta, cols), lambda i, j, k: (j, i, 0))
        shape = (nb // cols, ka, cols)
    else:
        out_spec = pl.BlockSpec((ta, tb), lambda i, j, k: (i, j))
        shape = (ka, nb)
    grid = (ka // ta, nb // tb, nk)
    body, more_in, more_out, more_shapes, more_scratch = _host_exchange(body, 2, 2, travel, grid)
    outs = pl.pallas_call(
        body, name=name, grid=grid,
        in_specs=[pl.BlockSpec((tk, ta), lambda i, j, k: (k, i)),
                  pl.BlockSpec((tk, tb), lambda i, j, k: (k, j))] + more_in,
        out_specs=[out_spec, out_spec] + more_out,
        out_shape=[jax.ShapeDtypeStruct(shape, F32), jax.ShapeDtypeStruct(shape, BF16)] + more_shapes,
        scratch_shapes=more_scratch,
        compiler_params=_cparams("arbitrary", "arbitrary", "arbitrary"),
    )(a, b, *travel)
    return (outs[0], outs[1]), list(outs[2:])


def _adamw(parts, own, w, m, v, tr, name):
    p, rows, cols = parts.shape
    c1 = 1.0 / (1.0 - ADAM_B1 ** ADAM_STEP)
    c2 = 1.0 / (1.0 - ADAM_B2 ** ADAM_STEP)

    def body(*refs):
        if own is None:
            p_ref, w_ref, m_ref, v_ref, g_ref, d_ref, nm_ref, nv_ref = refs
            g = p_ref[0]
            for k in range(1, p):
                g = g + p_ref[k]
        else:
            p_ref, own_ref, w_ref, m_ref, v_ref, g_ref, d_ref, nm_ref, nv_ref = refs
            x, y, c = _place()
            my = 4 * x + 2 * y + c
            mine = own_ref[...]
            g = jnp.where(my == 0, mine, p_ref[0].astype(F32))
            for k in range(1, p):
                g = g + jnp.where(my == k, mine, p_ref[k].astype(F32))
        nm = ADAM_B1 * m_ref[...] + (1.0 - ADAM_B1) * g
        nv = ADAM_B2 * v_ref[...] + (1.0 - ADAM_B2) * (g * g)
        g_ref[...] = g
        nm_ref[...] = nm
        nv_ref[...] = nv
        d_ref[...] = -ADAM_LR * ((nm * c1) / (jnp.sqrt(nv * c2) + ADAM_EPS) + ADAM_WD * w_ref[...])

    blk = pl.BlockSpec((tr, cols), lambda i: (i, 0))
    shape = jax.ShapeDtypeStruct((rows, cols), F32)
    return pl.pallas_call(
        body, name=name, grid=(rows // tr,),
        in_specs=[pl.BlockSpec((p, tr, cols), lambda i: (0, i, 0))] + [blk] * (3 if own is None else 4),
        out_specs=[blk] * 4, out_shape=[shape] * 4,
        compiler_params=_cparams("parallel"),
    )(*([parts] + ([] if own is None else [own]) + [w, m, v]))


def _adamw_sharded(parts, grads, my, w, m, v, tr, name):
    depth, rows, cols = w.shape
    p, pr, pc = parts[0].shape
    c1 = 1.0 / (1.0 - ADAM_B1 ** ADAM_STEP)
    c2 = 1.0 / (1.0 - ADAM_B2 ** ADAM_STEP)

    def body(my_ref, *refs):
        p_refs, own_refs = refs[:depth], refs[depth:2 * depth]
        w_ref, m_ref, v_ref, g_ref, d_ref, nm_ref, nv_ref = refs[2 * depth:]
        layer = pl.program_id(0)
        for ll in range(depth):
            @pl.when(layer == ll)
            def _(ll=ll):
                mine = own_refs[ll][...]
                g = jnp.where(my_ref[0] == 0, mine, p_refs[ll][0].astype(F32))
                for k in range(1, p):
                    g = g + jnp.where(my_ref[0] == k, mine, p_refs[ll][k].astype(F32))
                g = g[:, :cols]
                nm = ADAM_B1 * m_ref[...] + (1.0 - ADAM_B1) * g
                nv = ADAM_B2 * v_ref[...] + (1.0 - ADAM_B2) * (g * g)
                g_ref[...] = g
                nm_ref[...] = nm
                nv_ref[...] = nv
                d_ref[...] = -ADAM_LR * ((nm * c1) / (jnp.sqrt(nv * c2) + ADAM_EPS) + ADAM_WD * w_ref[...])

    def row_block(ll, l, i):
        return jnp.where(l == ll, i, 0)

    blk = pl.BlockSpec((None, tr, cols), lambda l, i, my_: (l, i, 0))
    shape = jax.ShapeDtypeStruct((depth, rows, cols), F32)
    return pl.pallas_call(
        body, name=name,
        grid_spec=pltpu.PrefetchScalarGridSpec(
            num_scalar_prefetch=1, grid=(depth, rows // tr),
            in_specs=([pl.BlockSpec((p, tr, pc), lambda l, i, my_, ll=ll: (0, row_block(ll, l, i), 0))
                       for ll in range(depth)]
                      + [pl.BlockSpec((None, tr, pc), lambda l, i, my_, ll=ll: (my_[0], row_block(ll, l, i), 0))
                         for ll in range(depth)]
                      + [blk, blk, blk]),
            out_specs=[blk] * 4),
        out_shape=[shape] * 4,
        compiler_params=_cparams("arbitrary", "arbitrary"),
    )(my, *parts, *grads, w, m, v)


def _place():
    x, y, c = lax.axis_index("x"), lax.axis_index("y"), lax.axis_index("c")
    return x, y, c


class _Gather:
    def __init__(self, srcs, dsts, send_sems, recv_sems, local_sems):
        na = len(srcs)
        x, y, c = _place()
        me, sibling = (x, y, c), (x, y, 1 - c)
        chips = [(1 - x, y), (x, 1 - y), (1 - x, 1 - y)]

        def slot(a, dev):
            return dsts[a].at[:, pl.ds(4 * dev[0] + 2 * dev[1] + dev[2], 1)]

        def copy(k, a, block, to, from_shard=False):
            return pltpu.make_async_remote_copy(
                src_ref=srcs[a] if from_shard else slot(a, block), dst_ref=slot(a, block),
                send_sem=send_sems.at[k, a], recv_sem=recv_sems.at[k, a], device_id=to, device_id_type=MESH)

        pairs = [(j, chip, a) for j, chip in enumerate(chips) for a in range(na)]
        self.mine = [pltpu.make_async_copy(srcs[a], slot(a, me), local_sems.at[a]) for a in range(na)]
        self.first = [copy(0, a, me, sibling, True) for a in range(na)]
        self.first += [copy(1 + j, a, me, (*chip, c), True) for j, chip, a in pairs]
        self.over_ici = [copy(1 + j, a, (*chip, c), me) for j, chip, a in pairs]
        self.passed = [copy(4 + j, a, (*chip, c), sibling) for j, chip, a in pairs]
        self.from_sibling = [copy(0, a, sibling, me) for a in range(na)]
        self.from_sibling += [copy(4 + j, a, (*chip, 1 - c), me) for j, chip, a in pairs]

    def begin(self):
        for cp in self.mine + self.first:
            cp.start()

    def relay(self):
        for arrived, onward in zip(self.over_ici, self.passed):
            arrived.wait_recv()
            onward.start()

    def finish(self):
        for cp in self.from_sibling:
            cp.wait_recv()
        for cp in self.first + self.passed:
            cp.wait_send()
        for cp in self.mine:
            cp.wait()


def _gather_scratch(na):
    return [pltpu.SemaphoreType.DMA((7, na)), pltpu.SemaphoreType.DMA((7, na)), pltpu.SemaphoreType.DMA((na,))]


def _gathered_shapes(shards):
    return [jax.ShapeDtypeStruct((a.shape[0], N_DEV) + a.shape[2:], a.dtype) for a in shards]


def _all_gather(shards, name):
    na = len(shards)

    def body(*refs):
        gather = _Gather(refs[:na], refs[na:2 * na], *refs[2 * na:])
        gather.begin()
        gather.relay()
        gather.finish()

    anyspec = pl.BlockSpec(memory_space=pl.ANY)
    return pl.pallas_call(
        body, name=name,
        in_specs=[anyspec] * na, out_specs=[anyspec] * na,
        out_shape=_gathered_shapes(shards), scratch_shapes=_gather_scratch(na),
    )(*shards)


_RELATIONS = [(dx, dy, dc) for dx in (0, 1) for dy in (0, 1) for dc in (0, 1)][1:]


def _flip(v, d):
    return 1 - v if d else v


def _exchange_copies(srcs, dsts, send_sems, recv_sems, local_sems):
    x, y, c = _place()
    my = 4 * x + 2 * y + c
    na = len(srcs)
    mine = [pltpu.make_async_copy(srcs[a].at[pl.ds(my, 1)], dsts[a].at[pl.ds(my, 1)], local_sems.at[a])
            for a in range(na)]
    sends, recvs = [], []
    for k, (dx, dy, dc) in enumerate(_RELATIONS):
        peer = (_flip(x, dx), _flip(y, dy), _flip(c, dc))
        pidx = 4 * peer[0] + 2 * peer[1] + peer[2]
        for a in range(na):
            for into, out in ((my, sends), (pidx, recvs)):
                out.append(pltpu.make_async_remote_copy(
                    src_ref=srcs[a].at[pl.ds(pidx, 1)], dst_ref=dsts[a].at[pl.ds(into, 1)],
                    send_sem=send_sems.at[k, a], recv_sem=recv_sems.at[k, a], device_id=peer, device_id_type=MESH))
    return mine, sends, recvs


def _exchange_begin(copies):
    mine, sends, _ = copies
    for cp in mine + sends:
        cp.start()


def _exchange_finish(copies):
    mine, sends, recvs = copies
    for cp in recvs:
        cp.wait_recv()
    for cp in sends:
        cp.wait_send()
    for cp in mine:
        cp.wait()


def _exchange_scratch(na):
    return [pltpu.SemaphoreType.DMA((7, na)), pltpu.SemaphoreType.DMA((7, na)), pltpu.SemaphoreType.DMA((na,))]


def _host_exchange(body, n_in, n_out, travel, grid):
    nt = len(travel)
    if not nt:
        return body, [], [], [], []

    def wrapped(*refs):
        ins, srcs = refs[:n_in], refs[n_in:n_in + nt]
        outs, rest = refs[n_in + nt:n_in + nt + n_out], refs[n_in + nt + n_out:]
        dsts, scratch = rest[:nt], rest[nt:]
        copies = _exchange_copies(srcs, dsts, *scratch[-3:])
        first = last = None
        for axis, size in enumerate(grid):
            at_start, at_end = pl.program_id(axis) == 0, pl.program_id(axis) == size - 1
            first = at_start if first is None else jnp.logical_and(first, at_start)
            last = at_end if last is None else jnp.logical_and(last, at_end)

        @pl.when(first)
        def _():
            _exchange_begin(copies)

        body(*ins, *outs, *scratch[:-3])

        @pl.when(last)
        def _():
            _exchange_finish(copies)

    anyspec = pl.BlockSpec(memory_space=pl.ANY)
    return (wrapped, [anyspec] * nt, [anyspec] * nt, [jax.ShapeDtypeStruct(t.shape, t.dtype) for t in travel],
            _exchange_scratch(nt))


def _all_reduce_small(v, name):
    r, c_ = v.shape

    def body(v_ref, o_ref, gath, send_sems, recv_sems):
        x, y, c = _place()
        my = 4 * x + 2 * y + c
        gath[my] = v_ref[...]
        sends = []
        for k, (dx, dy, dc) in enumerate(_RELATIONS):
            peer = (_flip(x, dx), _flip(y, dy), _flip(c, dc))
            cp = pltpu.make_async_remote_copy(
                src_ref=v_ref, dst_ref=gath.at[my], send_sem=send_sems.at[k], recv_sem=recv_sems.at[k],
                device_id=peer, device_id_type=MESH)
            cp.start()
            sends.append((cp, 4 * peer[0] + 2 * peer[1] + peer[2], k, peer))
        for cp, pidx, k, peer in sends:
            pltpu.make_async_remote_copy(
                src_ref=v_ref, dst_ref=gath.at[pidx], send_sem=send_sems.at[k], recv_sem=recv_sems.at[k],
                device_id=peer, device_id_type=MESH).wait_recv()
        for cp, *_ in sends:
            cp.wait_send()
        tot = gath[0]
        for k in range(1, N_DEV):
            tot = tot + gath[k]
        o_ref[...] = tot

    vm = pl.BlockSpec(memory_space=pltpu.VMEM)
    return pl.pallas_call(
        body, name=name, in_specs=[vm], out_specs=vm,
        out_shape=jax.ShapeDtypeStruct((r, c_), F32),
        scratch_shapes=[pltpu.VMEM((N_DEV, r, c_), F32), pltpu.SemaphoreType.DMA((7,)),
                        pltpu.SemaphoreType.DMA((7,))],
    )(v)


TM = 512
TM_MATMUL = 2048
TM_RESIDUAL = 1024
TQ = 256


def _device_blocks(t):
    return t.reshape(N_DEV, -1, t.shape[-1])


def _pad_to(a, axis, size):
    pad = [(0, 0)] * a.ndim
    pad[axis] = (0, size - a.shape[axis])
    return jnp.pad(a, pad)


def _local_step(x, target, g_in0, late_shards, conv_full, norm_mix, q_norm, k_norm, norm_ffn):
    depth, d = norm_mix.shape
    cols = g_in0.shape[3]
    tm, tq = min(TM, x.shape[0]), min(TQ, x.shape[0])
    tmm, tmr = min(TM_MATMUL, x.shape[0]), min(TM_RESIDUAL, x.shape[0])
    attn = d // 2
    nheads = attn // HEAD_DIM
    scale = HEAD_DIM ** -0.5 * LOG2E
    saved = []
    h1 = _rmsnorm_fwd(x, norm_mix[0][None], tm, "norm_mix_fwd_0")
    for l in range(depth):
        w_in = (g_in0, 0) if l == 0 else (g_rest, 3 * (l - 1))
        proj = _mm_blocks(h1, *w_in, tmm, f"proj_in_{l}")
        qk_gain = jnp.concatenate([jnp.tile(q_norm[l], nheads) * scale, jnp.tile(k_norm[l], nheads)])[None]
        qk = _qknorm_fwd(proj, qk_gain, tmm, f"qknorm_fwd_{l}")
        o, rtot, used, gathered = _attn_fwd(qk, proj, tq, f"attn_fwd_{l}", late_shards if l == 0 else ())
        if l == 0:
            g_gu0, g_rest, gb, gc = gathered if depth > 1 else (gathered[0], None, *gathered[1:])
            gb = gb.reshape(depth, -1, d)
            gc = gc.reshape(depth, -1, d)
        w_gu = (g_gu0, 0, 1) if l == 0 else (g_rest, 3 * (l - 1) + 1, 3 * (l - 1) + 2)
        conv_w8 = _pad_to(conv_full[l], 0, 8)
        cv = _conv_fwd(proj, conv_w8, f"conv_fwd_{l}")
        mix = jnp.concatenate([o, cv], axis=1)
        x1, h2 = _mm_residual_norm(mix, gb, l, x, norm_ffn[l][None], tmr, f"proj_out_{l}")
        g, u, act = _mm_swiglu(h2, *w_gu, tmr, f"ffn_up_{l}")
        saved.append((x, h1, proj, qk_gain, qk, rtot, used, conv_w8, mix, x1, h2, g, u, act, w_in, w_gu))
        if l + 1 < depth:
            x, h1 = _mm_residual_norm(act, gc, l, x1, norm_mix[l + 1][None], tm, f"ffn_down_{l}")
        else:
            dx, dxb, loss = _mm_residual_loss(act, gc, l, x1, target, tm, f"ffn_down_{l}")

    grads = [None] * depth
    small = [None] * depth
    landed = {}
    for l in reversed(range(depth)):
        x0, h1, proj, qk_gain, qk, rtot, used, conv_w8, mix, x1, h2, g, u, act, w_in, w_gu = saved[l]
        d = x0.shape[1]
        dg, du = _mm_nt_swiglu_bwd(dxb, gc, l, g, u, tmr, f"ffn_down_bwd_{l}")
        d_wdown, _ = _mm_tn(act, dxb, 768, d, tmm, False, f"dw_down_{l}")
        d_wgate, _ = _mm_tn(h2, dg, d, cols, tmm, True, f"dw_gate_{l}")
        d_wup, _ = _mm_tn(h2, du, d, cols, tmm, True, f"dw_up_{l}")
        dx1, dx1b, dg_ffn, _ = _mm_nt_norm_bwd([dg, du], w_gu[0], list(w_gu[1:]), x1, norm_ffn[l][None], dx, tm,
                                               f"ffn_up_bwd_{l}")
        dmix = _mm_nt(dx1b, gb, l, tmr, 512, f"proj_out_bwd_{l}")
        d_wout, _ = _mm_tn(mix, dx1b, 512, d, tmm, False, f"dw_out_{l}")
        dcb, dcc, dcu, dconv = _conv_bwd(dmix, proj, conv_w8, f"conv_bwd_{l}")
        rides = {}
        if l == 0:
            rides = {5 * ll + j: grads[ll][j][1] for ll in range(1, depth) for j in range(5)}
            rides.update({3: d_wout[1], 4: d_wdown[1]})
        dq, dk, dv, arrived = _attn_bwd(qk, proj, dmix, rtot, used, tq, f"attn_bwd_{l}",
                                        [_device_blocks(t) for t in rides.values()])
        landed.update(zip(rides.keys(), arrived))
        dqk, dg_qk = _qknorm_bwd(jnp.concatenate([dq, dk], axis=1), proj, qk_gain, tmm, f"qknorm_bwd_{l}")
        dproj = jnp.concatenate([dqk, dv.astype(BF16), dcb, dcc, dcu], axis=1)
        first = l == 0
        d_win, arrived = _mm_tn(h1, dproj, d, cols, tmm, True, f"dw_in_{l}",
                                [_device_blocks(d_wgate[1])] if first else [])
        landed.update(zip([1], arrived))
        dx, dxb, dg_mix, arrived = _mm_nt_norm_bwd(
            [dproj], w_in[0], [w_in[1]], x0, norm_mix[l][None], dx1, tm, f"proj_in_bwd_{l}",
            [_device_blocks(d_win[1]), _device_blocks(d_wup[1])] if first else [])
        landed.update(zip([0, 2], arrived))
        grads[l] = (d_win, d_wgate, d_wup, d_wout, d_wdown)
        dq_gain = jnp.sum(dg_qk[0, :attn].reshape(nheads, HEAD_DIM), axis=0) * scale
        dk_gain = jnp.sum(dg_qk[0, attn:].reshape(nheads, HEAD_DIM), axis=0)
        small[l] = (dg_mix[0], dg_ffn[0], dq_gain, dk_gain, dconv[:3])
    return loss, dx, grads, small, landed


def kernel(x, norm_mix, w_in, q_norm, k_norm, conv_w, w_out, norm_ffn, w_gate, w_up, w_down, loss_target, m_norm_mix, m_w_in, m_q_norm, m_k_norm, m_conv_w, m_w_out, m_norm_ffn, m_w_gate, m_w_up, m_w_down, v_norm_mix, v_w_in, v_q_norm, v_k_norm, v_conv_w, v_w_out, v_norm_ffn, v_w_gate, v_w_up, v_w_down):
    depth, d, in_shard = w_in.shape
    ff_shard = w_gate.shape[2]
    ff_pad = in_shard
    conv_shard = conv_w.shape[2]
    xs = x.reshape(x.shape[-2], d)
    target = loss_target.reshape(xs.shape)

    pa = jnp.stack([w_in, _pad_to(w_gate, 2, ff_pad), _pad_to(w_up, 2, ff_pad)], axis=1)
    pa = pa.reshape(3 * depth, 1, d, in_shard).astype(BF16)
    pd = _pad_to(_pad_to(conv_w.reshape(depth * 3, conv_shard), 0, 8), 1, LANES)[None, None]
    g_in0, gd = _all_gather([pa[:1], pd], "gather_first")
    conv_full = gd[0, :, :depth * 3, :conv_shard].transpose(1, 0, 2).reshape(depth, 3, N_DEV * conv_shard)
    late_shards = [pa[1:3]] + ([pa[3:]] if depth > 1 else [])
    late_shards += [w_out.astype(BF16)[:, None], _pad_to(w_down, 1, ff_pad).astype(BF16)[:, None]]

    loss, grad_x, grads, small, landed = _local_step(xs, target, g_in0, late_shards, conv_full, norm_mix, q_norm,
                                                     k_norm, norm_ffn)

    x_, y_, c_ = _place()
    my = 4 * x_ + 2 * y_ + c_

    nconv = N_DEV * conv_shard
    rows = []
    for l in range(depth):
        g_mix, g_ffn, g_q, g_k, g_conv = small[l]
        qkrow = _pad_to(jnp.concatenate([g_q, g_k]), 0, d)
        rows += [g_mix[None], g_ffn[None], qkrow[None], _pad_to(g_conv, 1, d)]
    nrow = 6 * depth
    packed = jnp.concatenate(rows + [_pad_to(loss[:1], 1, d)], axis=0)
    packed = _pad_to(packed, 0, ((nrow + 1 + 7) // 8) * 8)
    summed = _all_reduce_small(packed, "reduce_small")
    loss_out = summed[nrow, 0]

    my1 = my.astype(jnp.int32).reshape(1)

    def big(j, w, m, v, tr, name):
        return _adamw_sharded([landed[5 * l + j] for l in range(depth)],
                              [_device_blocks(grads[l][j][0]) for l in range(depth)], my1, w, m, v, tr, name)

    res = {"w_in": big(0, w_in, m_w_in, v_w_in, 256, "adamw_in"),
           "w_gate": big(1, w_gate, m_w_gate, v_w_gate, 256, "adamw_gate"),
           "w_up": big(2, w_up, m_w_up, v_w_up, 256, "adamw_up"),
           "w_out": big(3, w_out, m_w_out, v_w_out, w_out.shape[1], "adamw_out"),
           "w_down": big(4, w_down, m_w_down, v_w_down, ff_shard // 2, "adamw_down")}

    g_rows, w_rows, m_rows, v_rows = [], [], [], []
    for l in range(depth):
        base = l * 6
        conv_g = lax.dynamic_slice(summed[base + 3:base + 6], (0, my * conv_shard), (3, conv_shard))
        g_rows += [summed[base:base + 3], _pad_to(conv_g, 1, d)]
        for dst, (nm, qn, kn, nf, cw) in ((w_rows, (norm_mix, q_norm, k_norm, norm_ffn, conv_w)),
                                          (m_rows, (m_norm_mix, m_q_norm, m_k_norm, m_norm_ffn, m_conv_w)),
                                          (v_rows, (v_norm_mix, v_q_norm, v_k_norm, v_norm_ffn, v_conv_w))):
            dst += [nm[l][None], nf[l][None], _pad_to(jnp.concatenate([qn[l], kn[l]]), 0, d)[None],
                    _pad_to(cw[l], 1, d)]
    prow = ((nrow + 7) // 8) * 8
    gs, ws, ms, vs = [_pad_to(jnp.concatenate(t, axis=0), 0, prow) for t in (g_rows, w_rows, m_rows, v_rows)]
    sm = _adamw(gs[None], None, ws, ms, vs, prow, "adamw_small")

    hd = q_norm.shape[1]

    def small_out(t, kind):
        per_layer = []
        for l in range(depth):
            base = l * 6
            per_layer.append({"norm_mix": t[base], "norm_ffn": t[base + 1], "q_norm": t[base + 2, :hd],
                              "k_norm": t[base + 2, hd:2 * hd], "conv_w": t[base + 3:base + 6, :conv_shard]}[kind])
        return jnp.stack(per_layer)

    def big_out(name, i):
        return res[name][i]

    outs = [loss_out, grad_x.reshape(x.shape)]
    for i in range(4):
        outs += [small_out(sm[i], "norm_mix"), big_out("w_in", i), small_out(sm[i], "q_norm"),
                 small_out(sm[i], "k_norm"), small_out(sm[i], "conv_w"), big_out("w_out", i),
                 small_out(sm[i], "norm_ffn"), big_out("w_gate", i), big_out("w_up", i), big_out("w_down", i)]
    return tuple(outs)
```

```python
import jax
import jax.numpy as jnp
from jax import lax
from jax.experimental import pallas as pl
from jax.experimental.pallas import tpu as pltpu

F32 = jnp.float32
BF16 = jnp.bfloat16
MESH = pl.DeviceIdType.MESH

N_DEV = 8
LANES = 128
HEAD_DIM = 64
KEY_CHUNK = 128
EPS = 1e-6
VMEM_LIMIT = 48 * 1024 * 1024

ADAM_LR = 0.001
ADAM_B1 = 0.9
ADAM_B2 = 0.999
ADAM_EPS = 1e-08
ADAM_WD = 0.01
ADAM_STEP = 10

NN = (((1,), (0,)), ((), ()))
NT = (((1,), (1,)), ((), ()))
TN = (((0,), (0,)), ((), ()))


def _dot(a, b, dims):
    return lax.dot_general(a.astype(BF16), b.astype(BF16), dims, preferred_element_type=F32)


def _cparams(*sem):
    return pltpu.CompilerParams(dimension_semantics=sem, vmem_limit_bytes=VMEM_LIMIT)


def _split_hi_lo(v):
    hi = v.astype(BF16)
    lo = (v - hi.astype(F32)).astype(BF16)
    return jnp.concatenate([hi, lo], axis=1)


def _rmsnorm_fwd(x, gain, tm, name):
    s, d = x.shape

    def body(x_ref, g_ref, o_ref):
        xv = x_ref[...]
        r = lax.rsqrt(jnp.mean(xv * xv, axis=-1, keepdims=True) + EPS)
        o_ref[...] = ((xv * r) * g_ref[...]).astype(o_ref.dtype)

    return pl.pallas_call(
        body, name=name, grid=(s // tm,),
        in_specs=[pl.BlockSpec((tm, d), lambda i: (i, 0)), pl.BlockSpec((1, d), lambda i: (0, 0))],
        out_specs=pl.BlockSpec((tm, d), lambda i: (i, 0)),
        out_shape=jax.ShapeDtypeStruct((s, d), BF16),
        compiler_params=_cparams("parallel"),
    )(x, gain)


def _group_mean_matrix():
    r = lax.broadcasted_iota(jnp.int32, (LANES, LANES), 0) // HEAD_DIM
    c = lax.broadcasted_iota(jnp.int32, (LANES, LANES), 1) // HEAD_DIM
    return jnp.where(r == c, 1.0 / HEAD_DIM, 0.0).astype(BF16)


def _group_mean(v, gm):
    hi = v.astype(BF16)
    lo = (v - hi.astype(F32)).astype(BF16)
    return _dot(hi, gm, NN) + _dot(lo, gm, NN)


def _qknorm_fwd(proj, gains, tm, name):
    s = proj.shape[0]
    ncol = gains.shape[1] // LANES

    def body(p_ref, g_ref, gm_ref, o_ref):
        xv = p_ref[...].astype(F32)
        r = lax.rsqrt(_group_mean(xv * xv, gm_ref[...]) + EPS)
        o_ref[...] = ((xv * r) * g_ref[...]).astype(o_ref.dtype)

    blk = pl.BlockSpec((tm, LANES), lambda i, j: (i, j))
    return pl.pallas_call(
        body, name=name, grid=(s // tm, ncol),
        in_specs=[blk, pl.BlockSpec((1, LANES), lambda i, j: (0, j)),
                  pl.BlockSpec((LANES, LANES), lambda i, j: (0, 0))],
        out_specs=blk,
        out_shape=jax.ShapeDtypeStruct((s, ncol * LANES), BF16),
        compiler_params=_cparams("parallel", "parallel"),
    )(proj, gains, _group_mean_matrix())


def _qknorm_bwd(dqk, proj, gains, tm, name):
    s = proj.shape[0]
    ncol = gains.shape[1] // LANES
    nsteps = s // tm

    def body(dy_ref, p_ref, g_ref, gm_ref, dx_ref, dg_ref):
        i = pl.program_id(1)
        gm = gm_ref[...]
        xv = p_ref[...].astype(F32)
        r = lax.rsqrt(_group_mean(xv * xv, gm) + EPS)
        xhat = xv * r
        dy = dy_ref[...]
        dxh = dy * g_ref[...]
        proj_ = _group_mean(dxh * xhat, gm)
        dx_ref[...] = (r * (dxh - xhat * proj_)).astype(dx_ref.dtype)
        part = jnp.sum((dy * xhat).reshape(tm // 8, 8, LANES), axis=0)

        @pl.when(i == 0)
        def _():
            dg_ref[...] = part

        @pl.when(i > 0)
        def _():
            dg_ref[...] += part

        @pl.when(i == nsteps - 1)
        def _():
            dg_ref[...] = jnp.broadcast_to(jnp.sum(dg_ref[...], axis=0, keepdims=True), (8, LANES))

    blk = pl.BlockSpec((tm, LANES), lambda j, i: (i, j))
    return pl.pallas_call(
        body, name=name, grid=(ncol, nsteps),
        in_specs=[blk, blk, pl.BlockSpec((1, LANES), lambda j, i: (0, j)),
                  pl.BlockSpec((LANES, LANES), lambda j, i: (0, 0))],
        out_specs=[blk, pl.BlockSpec((8, LANES), lambda j, i: (0, j))],
        out_shape=[jax.ShapeDtypeStruct((s, ncol * LANES), BF16),
                   jax.ShapeDtypeStruct((8, ncol * LANES), F32)],
        compiler_params=_cparams("parallel", "arbitrary"),
    )(dqk, proj, gains, _group_mean_matrix())


CONV_ROWS = 256
HALO = 8


def _conv_fwd(proj, conv_w8, name):
    s = proj.shape[0]
    nblk = conv_w8.shape[1] // LANES
    first = 3 * nblk
    nchunk = s // CONV_ROWS

    def body(cb_ref, cc_ref, cu_ref, w_ref, y_ref, hpad):
        hpad[pl.ds(0, 2 * HALO), :] = jnp.zeros((2 * HALO, LANES), F32)

        def fill(i, _):
            r0 = pl.multiple_of(i * CONV_ROWS, CONV_ROWS)
            hpad[pl.ds(r0 + 2 * HALO, CONV_ROWS), :] = (
                cc_ref[pl.ds(r0, CONV_ROWS), :].astype(F32) * cu_ref[pl.ds(r0, CONV_ROWS), :].astype(F32))
            return 0

        lax.fori_loop(0, nchunk, fill, 0)
        w0, w1, w2 = w_ref[0:1, :], w_ref[1:2, :], w_ref[2:3, :]

        def conv(i, _):
            r0 = pl.multiple_of(i * CONV_ROWS, CONV_ROWS)
            win = hpad[pl.ds(r0 + HALO, CONV_ROWS + HALO), :]
            c = (w2 * win[HALO:] + w1 * pltpu.roll(win, 1, 0)[HALO:] + w0 * pltpu.roll(win, 2, 0)[HALO:])
            y_ref[pl.ds(r0, CONV_ROWS), :] = (cb_ref[pl.ds(r0, CONV_ROWS), :].astype(F32) * c).astype(y_ref.dtype)
            return 0

        lax.fori_loop(0, nchunk, conv, 0)

    def col(off):
        return pl.BlockSpec((s, LANES), lambda j: (0, off + j))

    return pl.pallas_call(
        body, name=name, grid=(nblk,),
        in_specs=[col(first), col(first + nblk), col(first + 2 * nblk), pl.BlockSpec((8, LANES), lambda j: (0, j))],
        out_specs=pl.BlockSpec((s, LANES), lambda j: (0, j)),
        out_shape=jax.ShapeDtypeStruct((s, nblk * LANES), BF16),
        scratch_shapes=[pltpu.VMEM((s + 2 * HALO, LANES), F32)],
        compiler_params=_cparams("parallel"),
    )(proj, proj, proj, conv_w8)


def _conv_bwd(dmix, proj, conv_w8, name):
    s = proj.shape[0]
    nblk = conv_w8.shape[1] // LANES
    first = 3 * nblk
    nchunk = s // CONV_ROWS

    def body(dy_ref, cb_ref, cc_ref, cu_ref, w_ref, dcb_ref, dcc_ref, dcu_ref, dw_ref, hpad, dcpad):
        hpad[pl.ds(0, 2 * HALO), :] = jnp.zeros((2 * HALO, LANES), F32)
        dcpad[pl.ds(s, 2 * HALO), :] = jnp.zeros((2 * HALO, LANES), F32)

        def fill(i, _):
            r0 = pl.multiple_of(i * CONV_ROWS, CONV_ROWS)
            hpad[pl.ds(r0 + 2 * HALO, CONV_ROWS), :] = (
                cc_ref[pl.ds(r0, CONV_ROWS), :].astype(F32) * cu_ref[pl.ds(r0, CONV_ROWS), :].astype(F32))
            return 0

        lax.fori_loop(0, nchunk, fill, 0)
        w0, w1, w2 = w_ref[0:1, :], w_ref[1:2, :], w_ref[2:3, :]

        def fold(v):
            return jnp.sum(v.reshape(CONV_ROWS // 8, 8, LANES), axis=0)

        def first_pass(i, acc):
            a0, a1, a2 = acc
            r0 = pl.multiple_of(i * CONV_ROWS, CONV_ROWS)
            win = hpad[pl.ds(r0 + HALO, CONV_ROWS + HALO), :]
            h0 = win[HALO:]
            h1 = pltpu.roll(win, 1, 0)[HALO:]
            h2 = pltpu.roll(win, 2, 0)[HALO:]
            c = w2 * h0 + w1 * h1 + w0 * h2
            dy = dy_ref[pl.ds(r0, CONV_ROWS), :]
            dcb_ref[pl.ds(r0, CONV_ROWS), :] = (dy * c).astype(dcb_ref.dtype)
            dc = dy * cb_ref[pl.ds(r0, CONV_ROWS), :].astype(F32)
            dcpad[pl.ds(r0, CONV_ROWS), :] = dc
            return a0 + fold(dc * h2), a1 + fold(dc * h1), a2 + fold(dc * h0)

        z8 = jnp.zeros((8, LANES), F32)
        a0, a1, a2 = lax.fori_loop(0, nchunk, first_pass, (z8, z8, z8))
        dw_ref[...] = jnp.concatenate(
            [jnp.sum(a0, axis=0, keepdims=True), jnp.sum(a1, axis=0, keepdims=True),
             jnp.sum(a2, axis=0, keepdims=True), jnp.zeros((5, LANES), F32)], axis=0)

        def second_pass(i, _):
            r0 = pl.multiple_of(i * CONV_ROWS, CONV_ROWS)
            win = dcpad[pl.ds(r0, CONV_ROWS + HALO), :]
            n = CONV_ROWS + HALO
            dh = (w2 * win[:CONV_ROWS] + w1 * pltpu.roll(win, n - 1, 0)[:CONV_ROWS]
                  + w0 * pltpu.roll(win, n - 2, 0)[:CONV_ROWS])
            dcc_ref[pl.ds(r0, CONV_ROWS), :] = (dh * cu_ref[pl.ds(r0, CONV_ROWS), :].astype(F32)).astype(dcc_ref.dtype)
            dcu_ref[pl.ds(r0, CONV_ROWS), :] = (dh * cc_ref[pl.ds(r0, CONV_ROWS), :].astype(F32)).astype(dcu_ref.dtype)
            return 0

        lax.fori_loop(0, nchunk, second_pass, 0)

    def col(off):
        return pl.BlockSpec((s, LANES), lambda j: (0, off + j))

    out = pl.BlockSpec((s, LANES), lambda j: (0, j))
    return pl.pallas_call(
        body, name=name, grid=(nblk,),
        in_specs=[col(nblk), col(first), col(first + nblk), col(first + 2 * nblk),
                  pl.BlockSpec((8, LANES), lambda j: (0, j))],
        out_specs=[out, out, out, pl.BlockSpec((8, LANES), lambda j: (0, j))],
        out_shape=[jax.ShapeDtypeStruct((s, nblk * LANES), BF16)] * 3 + [jax.ShapeDtypeStruct((8, nblk * LANES), F32)],
        scratch_shapes=[pltpu.VMEM((s + 2 * HALO, LANES), F32), pltpu.VMEM((s + 2 * HALO, LANES), F32)],
        compiler_params=_cparams("parallel"),
    )(dmix, proj, proj, proj, conv_w8)


LOG2E = 1.4426950408889634
LN2 = 0.6931471805599453
NEG_BIG = -1e30
SATURATED = 160.0


def _cumsum_matrix(kind):
    j = lax.broadcasted_iota(jnp.int32, (KEY_CHUNK, 2 * KEY_CHUNK), 0)
    c = lax.broadcasted_iota(jnp.int32, (KEY_CHUNK, 2 * KEY_CHUNK), 1)
    tri = {"after": j > c, "upto": j <= c, "before": j < c}[kind]
    return jnp.where((c >= KEY_CHUNK) | tri, 1.0, 0.0).astype(BF16)


def _stack_heads(t, m0):
    zero = jnp.zeros_like(t)
    return jnp.concatenate([jnp.where(m0, t, zero), jnp.where(m0, zero, t)], axis=0)


def _softplus2(z):
    sp = jnp.maximum(z, 0.0) + jnp.log2(1.0 + jnp.exp2(-jnp.abs(z)))
    return sp, z - sp


def _key_chunk(ref, kc):
    return ref[pl.ds(pl.multiple_of(kc * KEY_CHUNK, KEY_CHUNK), KEY_CHUNK), :]


def _attn_bwd(qk, proj, dmix, rtot, used, tq, name, travel=()):
    s = qk.shape[0]
    nhp = qk.shape[1] // (2 * LANES)
    nc = tq // KEY_CHUNK
    nq = s // tq
    nt = len(travel)

    def body(used_ref, q_ref, k_ref, v_ref, do_ref, r_ref, cmi_ref, cme_ref, bias_ref, dq_ref, dk_ref, dv_ref,
             z_refs, ls_refs, sig_refs, sp_refs, gb_refs, pr_ref, gs_ref, copies):
        qi = pl.program_id(1)

        @pl.when(qi == 0)
        def _():
            dk_ref[...] = jnp.zeros_like(dk_ref)
            dv_ref[...] = jnp.zeros_like(dv_ref)

        if copies is not None:
            @pl.when(jnp.logical_and(pl.program_id(0) == 0, qi == 0))
            def _():
                _exchange_begin(copies)

        nslots = (qi + 1) * nc
        walked = used_ref[pl.program_id(0), qi].astype(jnp.int32)
        first = jnp.clip(nslots - walked, 0, nslots - nc) // nc * nc
        m0 = lax.broadcasted_iota(jnp.int32, (1, LANES), 1) < HEAD_DIM
        qs = _stack_heads(q_ref[...], m0)
        do = do_ref[...]
        dos = _stack_heads(do.astype(BF16), m0)
        dosl = _stack_heads((do * LN2).astype(BF16), m0)
        cmi = cmi_ref[...]
        cme = cme_ref[...]

        def chunk_at(i):
            return jnp.clip(i, first, nslots - 1)

        def scores(kc):
            return _dot(qs, _key_chunk(k_ref, kc), NT)

        def weights(ls, cs, da, pr, kc):
            a = jnp.exp2(ls - (pr - cs[:, :KEY_CHUNK]))
            gb = (a * da).astype(BF16)
            ks = pl.multiple_of(kc * KEY_CHUNK, KEY_CHUNK)
            dv_ref[pl.ds(ks, KEY_CHUNK), :] += _dot(a, dos, TN)
            return gb, jnp.exp2(ls), pr - cs[:, KEY_CHUNK:]

        def score_grads(gb, sig, cg, gs, dq, kc):
            dzb = (gb.astype(F32) * (1.0 - sig) - sig * (gs + cg[:, :KEY_CHUNK])).astype(BF16)
            ks = pl.multiple_of(kc * KEY_CHUNK, KEY_CHUNK)
            dk_ref[pl.ds(ks, KEY_CHUNK), :] += _dot(dzb, qs, TN)
            dq = dq + _dot(jnp.concatenate([dzb[:tq], dzb[tq:]], axis=1), _stack_heads(_key_chunk(k_ref, kc), m0), NN)
            return gs + cg[:, KEY_CHUNK:], dq

        def step(i, par, bias=None, stages="zswg"):
            cur, prv = par, 1 - par
            k1, k2 = chunk_at(i - 1), chunk_at(i - 2)
            z_next = scores(chunk_at(i + 1))
            if "w" in stages:
                cs = _dot(sp_refs[prv][...], cmi, NN)
                da = _dot(dosl, _key_chunk(v_ref, k1), NT)
            if "g" in stages:
                cg = _dot(gb_refs[cur][...], cme, NN)
            z = z_refs[cur][...]
            if bias is not None:
                z = z + bias
            sp, ls = _softplus2(z)
            sp_refs[cur][...] = sp.astype(BF16)
            ls_refs[cur][...] = ls
            if "g" in stages:
                gs, dq = score_grads(gb_refs[cur][...], sig_refs[cur][...], cg, gs_ref[...], dq_ref[...], k2)
                gs_ref[...] = gs
                dq_ref[...] = dq
            if "w" in stages:
                gb, sig, pr = weights(ls_refs[prv][...], cs, da, pr_ref[...], k1)
                gb_refs[prv][...] = gb
                sig_refs[prv][...] = sig
                pr_ref[...] = pr
            z_refs[prv][...] = z_next

        pr_ref[...] = jnp.concatenate([r_ref[:, :LANES], r_ref[:, LANES:]], axis=0)
        gs_ref[...] = jnp.zeros((2 * tq, LANES), F32)
        dq_ref[...] = jnp.zeros((tq, LANES), F32)
        z_refs[0][...] = scores(first)
        only_diagonal = first == nslots - nc
        step(first, 0, jnp.where(only_diagonal, bias_ref[0], 0.0), stages="zs")
        step(first + 1, 1, jnp.where(only_diagonal, bias_ref[1], 0.0), stages="zsw")

        def two_steps(j, _):
            step(2 * j, 0)
            step(2 * j + 1, 1)
            return 0

        lax.fori_loop(first // 2 + 1, nslots // 2 - 1, two_steps, 0)

        @pl.when(jnp.logical_not(only_diagonal))
        def _():
            step(nslots - 2, 0, bias_ref[0])
            step(nslots - 1, 1, bias_ref[1])

        k1, k2 = chunk_at(nslots - 1), chunk_at(nslots - 2)
        gb, sig, _ = weights(ls_refs[1][...], _dot(sp_refs[1][...], cmi, NN),
                             _dot(dosl, _key_chunk(v_ref, k1), NT), pr_ref[...], k1)
        gb2 = gb_refs[0][...]
        gs, dq = score_grads(gb2, sig_refs[0][...], _dot(gb2, cme, NN), gs_ref[...], dq_ref[...], k2)
        _, dq = score_grads(gb, sig, _dot(gb, cme, NN), gs, dq, k1)
        dq_ref[...] = dq

        if copies is not None:
            @pl.when(jnp.logical_and(pl.program_id(0) == nhp - 1, qi == nq - 1))
            def _():
                _exchange_finish(copies)

    def wrapped(*refs):
        ins, rest = refs[:9], refs[9:]
        srcs, rest = rest[:nt], rest[nt:]
        outs, rest = rest[:3], rest[3:]
        lands, rest = rest[:nt], rest[nt:]
        z0, z1, ls0, ls1, sg0, sg1, sp0, sp1, gb0, gb1, pr_ref, gs_ref = rest[:12]
        copies = _exchange_copies(srcs, lands, *rest[12:]) if nt else None
        body(*ins, *outs, (z0, z1), (ls0, ls1), (sg0, sg1), (sp0, sp1), (gb0, gb1), pr_ref, gs_ref, copies)

    assert nc == 2
    bias = _diag_bias(tq, True)
    bias = jnp.concatenate([bias[:, :, :KEY_CHUNK], bias[:, :, KEY_CHUNK:]], axis=1)
    qblk = pl.BlockSpec((tq, LANES), lambda p, i: (i, p))
    full = pl.BlockSpec((s, LANES), lambda p, i: (0, p))
    cmspec = pl.BlockSpec((KEY_CHUNK, 2 * KEY_CHUNK), lambda p, i: (0, 0))
    anyspec = pl.BlockSpec(memory_space=pl.ANY)
    shape = jax.ShapeDtypeStruct((s, nhp * LANES), F32)
    f32buf = pltpu.VMEM((2 * tq, LANES), F32)
    bf16buf = pltpu.VMEM((2 * tq, LANES), BF16)
    outs = pl.pallas_call(
        wrapped, name=name, grid=(nhp, nq),
        in_specs=[pl.BlockSpec(memory_space=pltpu.SMEM),
                  qblk,
                  pl.BlockSpec((s, LANES), lambda p, i: (0, nhp + p)),
                  pl.BlockSpec((s, LANES), lambda p, i: (0, 2 * nhp + p)),
                  qblk,
                  pl.BlockSpec((tq, 2 * LANES), lambda p, i: (i, p)),
                  cmspec, cmspec,
                  pl.BlockSpec((nc, 2 * tq, LANES), lambda p, i: (0, 0, 0))] + [anyspec] * nt,
        out_specs=[qblk, full, full] + [anyspec] * nt,
        out_shape=[shape, shape, shape] + [jax.ShapeDtypeStruct(t.shape, t.dtype) for t in travel],
        scratch_shapes=[f32buf] * 6 + [bf16buf] * 4 + [f32buf] * 2 + (_exchange_scratch(nt) if nt else []),
        compiler_params=_cparams("arbitrary", "arbitrary"),
    )(used, qk, qk, proj, dmix, rtot, _cumsum_matrix("upto"), _cumsum_matrix("before"), bias, *travel)
    return outs[0], outs[1], outs[2], list(outs[3:])


def _pair_cumsum_matrix(kind):
    j = lax.broadcasted_iota(jnp.int32, (2 * KEY_CHUNK, 4 * KEY_CHUNK), 0)
    c = lax.broadcasted_iota(jnp.int32, (2 * KEY_CHUNK, 4 * KEY_CHUNK), 1)
    same_head = (j // KEY_CHUNK) == ((c // KEY_CHUNK) % 2)
    jj, cc = j % KEY_CHUNK, c % KEY_CHUNK
    tri = {"after": jj > cc, "upto": jj <= cc, "before": jj < cc}[kind]
    return jnp.where(same_head & ((c >= 2 * KEY_CHUNK) | tri), 1.0, 0.0).astype(BF16)


def _diag_bias(tq, ascending):
    nc = tq // KEY_CHUNK
    shape = (nc, tq, 2 * KEY_CHUNK)
    d = lax.broadcasted_iota(jnp.int32, shape, 0)
    r = lax.broadcasted_iota(jnp.int32, shape, 1)
    c = lax.broadcasted_iota(jnp.int32, shape, 2) % KEY_CHUNK
    chunk = d if ascending else nc - 1 - d
    return jnp.where(chunk * KEY_CHUNK + c < r, 0.0, NEG_BIG).astype(F32)


def _attn_fwd(qk, proj, tq, name, shards=()):
    s = qk.shape[0]
    nhp = qk.shape[1] // (2 * LANES)
    nc = tq // KEY_CHUNK
    nq = s // tq
    ng = len(shards)
    assert nc == 2
    w = 2 * KEY_CHUNK

    def body(q_ref, k_ref, v_ref, cm_ref, bias_ref, o_ref, r_ref, used_ref, z_refs, ls_refs, cs_refs, ct_refs,
             sp_refs, ab_refs, acc_ref, gather):
        qi = pl.program_id(1)
        if gather is not None:
            @pl.when(jnp.logical_and(pl.program_id(0) == 0, qi == 0))
            def _():
                gather.begin()

        nslots = (qi + 1) * nc
        m0 = lax.broadcasted_iota(jnp.int32, (1, LANES), 1) < HEAD_DIM
        q = q_ref[...]
        cm = cm_ref[...]

        def chunk_at(i):
            return jnp.clip(nslots - 1 - i, 0, nslots - 1)

        def scores(kc):
            return _dot(q, _stack_heads(_key_chunk(k_ref, kc), m0), NT)

        def values(ab, kc):
            return _dot(ab, _stack_heads(_key_chunk(v_ref, kc), m0), NN)

        def step(i, par, bias=None, stages="zscwv"):
            cur, prv = par, 1 - par
            if "z" in stages:
                z_next = scores(chunk_at(i + 1))
            if "c" in stages:
                cs = _dot(sp_refs[prv][...], cm, NN)
            if "v" in stages:
                pv = values(ab_refs[prv][...], chunk_at(i - 3))
            if "w" in stages:
                rs = r_ref[...]
                r_ref[...] = rs + ct_refs[cur][...]
                ab_refs[cur][...] = jnp.exp2(ls_refs[cur][...] - cs_refs[cur][...] - rs).astype(BF16)
            if "s" in stages:
                z = z_refs[cur][...]
                if bias is not None:
                    z = z + bias
                sp, ls = _softplus2(z)
                sp_refs[cur][...] = sp.astype(BF16)
                ls_refs[cur][...] = ls
            if "v" in stages:
                acc_ref[...] += pv
            if "c" in stages:
                cs_refs[prv][...] = cs[:, :w]
                ct_refs[prv][...] = cs[:, w:]
            if "z" in stages:
                z_refs[prv][...] = z_next

        z_refs[0][...] = scores(chunk_at(0))
        ab_refs[1][...] = jnp.zeros((tq, w), BF16)
        r_ref[...] = jnp.zeros((tq, w), F32)
        acc_ref[...] = jnp.zeros((tq, LANES), F32)
        step(0, 0, bias_ref[0], stages="zs")
        step(1, 1, bias_ref[1], stages="zsc")

        def two_steps(carry):
            j, _ = carry
            step(2 * j, 0)
            step(2 * j + 1, 1)
            return j + 1, jnp.min(r_ref[...])

        pairs, low = lax.while_loop(lambda c: jnp.logical_and(c[0] < nslots // 2, c[1] < SATURATED), two_steps,
                                    (jnp.int32(1), jnp.min(r_ref[...])))
        entered = 2 * pairs
        saturated = low >= SATURATED

        @pl.when(saturated)
        def _():
            step(entered, 0, stages="v")

        @pl.when(jnp.logical_not(saturated))
        def _():
            step(entered, 0, stages="cwv")
            step(entered + 1, 1, stages="wv")
            step(entered + 2, 0, stages="v")

        o_ref[...] = acc_ref[...].astype(o_ref.dtype)
        used_ref[pl.program_id(0), qi] = jnp.where(saturated, entered - 2, entered).astype(F32)

        if gather is not None:
            @pl.when(jnp.logical_and(pl.program_id(0) == nhp - 1, qi == nq // 2))
            def _():
                gather.relay()

            @pl.when(jnp.logical_and(pl.program_id(0) == nhp - 1, qi == nq - 1))
            def _():
                gather.finish()

    def wrapped(*refs):
        ins, rest = refs[:5], refs[5:]
        srcs, rest = rest[:ng], rest[ng:]
        outs, rest = rest[:3], rest[3:]
        dsts, scratch = rest[:ng], rest[ng:]
        z, ls, cs, ct, sp, ab = [scratch[2 * j:2 * j + 2] for j in range(6)]
        gather = _Gather(srcs, dsts, *scratch[13:]) if ng else None
        body(*ins, *outs, z, ls, cs, ct, sp, ab, scratch[12], gather)

    f32buf = pltpu.VMEM((tq, w), F32)
    bf16buf = pltpu.VMEM((tq, w), BF16)
    anyspec = pl.BlockSpec(memory_space=pl.ANY)
    outs = pl.pallas_call(
        wrapped, name=name, grid=(nhp, nq),
        in_specs=[pl.BlockSpec((tq, LANES), lambda p, i: (i, p)),
                  pl.BlockSpec((s, LANES), lambda p, i: (0, nhp + p)),
                  pl.BlockSpec((s, LANES), lambda p, i: (0, 2 * nhp + p)),
                  pl.BlockSpec((w, 2 * w), lambda p, i: (0, 0)),
                  pl.BlockSpec((nc, tq, w), lambda p, i: (0, 0, 0))] + [anyspec] * ng,
        out_specs=[pl.BlockSpec((tq, LANES), lambda p, i: (i, p)),
                   pl.BlockSpec((tq, w), lambda p, i: (i, p)),
                   pl.BlockSpec(memory_space=pltpu.SMEM)] + [anyspec] * ng,
        out_shape=[jax.ShapeDtypeStruct((s, nhp * LANES), BF16),
                   jax.ShapeDtypeStruct((s, nhp * w), F32),
                   jax.ShapeDtypeStruct((nhp, nq), F32)] + _gathered_shapes(shards),
        scratch_shapes=([f32buf] * 8 + [bf16buf] * 4 + [pltpu.VMEM((tq, LANES), F32)]
                        + (_gather_scratch(ng) if ng else [])),
        compiler_params=_cparams("arbitrary", "arbitrary"),
    )(qk, qk, proj, _pair_cumsum_matrix("after"), _diag_bias(tq, False), *shards)
    return outs[0], outs[1], outs[2], list(outs[3:])


BLOCK_PAIR = 2


def _side_by_side(b_ref):
    return jnp.concatenate([b_ref[p] for p in range(BLOCK_PAIR)], axis=1)


def _mm_blocks(h, ga, widx, tm, name):
    s, d = h.shape
    nb, cols = ga.shape[1], ga.shape[3]

    def body(a_ref, b_ref, o_ref):
        o_ref[...] = _dot(a_ref[...], _side_by_side(b_ref), NN).astype(o_ref.dtype)

    return pl.pallas_call(
        body, name=name, grid=(s // tm, nb // BLOCK_PAIR),
        in_specs=[pl.BlockSpec((tm, d), lambda i, j: (i, 0)),
                  pl.BlockSpec((None, BLOCK_PAIR, d, cols), lambda i, j: (widx, j, 0, 0))],
        out_specs=pl.BlockSpec((tm, BLOCK_PAIR * cols), lambda i, j: (i, j)),
        out_shape=jax.ShapeDtypeStruct((s, nb * cols), BF16),
        compiler_params=_cparams("parallel", "arbitrary"),
    )(h, ga)


def _mm_swiglu(h, ga, gidx, uidx, tm, name):
    s, d = h.shape
    nb, cols = ga.shape[1], ga.shape[3]

    def body(a_ref, bg_ref, bu_ref, g_ref, u_ref, act_ref):
        a = a_ref[...]
        g = _dot(a, _side_by_side(bg_ref), NN)
        u = _dot(a, _side_by_side(bu_ref), NN)
        g_ref[...] = g.astype(g_ref.dtype)
        u_ref[...] = u.astype(u_ref.dtype)
        act_ref[...] = (g * (1.0 / (1.0 + jnp.exp(-g))) * u).astype(act_ref.dtype)

    def wspec(idx):
        return pl.BlockSpec((None, BLOCK_PAIR, d, cols), lambda i, j: (idx, j, 0, 0))

    out = pl.BlockSpec((tm, BLOCK_PAIR * cols), lambda i, j: (i, j))
    shape = jax.ShapeDtypeStruct((s, nb * cols), BF16)
    return pl.pallas_call(
        body, name=name, grid=(s // tm, nb // BLOCK_PAIR),
        in_specs=[pl.BlockSpec((tm, d), lambda i, j: (i, 0)), wspec(gidx), wspec(uidx)],
        out_specs=[out, out, out], out_shape=[shape, shape, shape],
        compiler_params=_cparams("parallel", "arbitrary"),
    )(h, ga, ga)


def _mm_residual_norm(a, w3, lidx, res, gain, tm, name):
    s, k = a.shape
    n = w3.shape[2]

    def body(a_ref, b_ref, r_ref, g_ref, o_ref, h_ref):
        xv = r_ref[...] + _dot(a_ref[...], b_ref[...], NN)
        o_ref[...] = xv
        r = lax.rsqrt(jnp.mean(xv * xv, axis=-1, keepdims=True) + EPS)
        h_ref[...] = ((xv * r) * g_ref[...]).astype(h_ref.dtype)

    row = pl.BlockSpec((tm, n), lambda i: (i, 0))
    return pl.pallas_call(
        body, name=name, grid=(s // tm,),
        in_specs=[pl.BlockSpec((tm, k), lambda i: (i, 0)),
                  pl.BlockSpec((None, k, n), lambda i: (lidx, 0, 0), pipeline_mode=pl.Buffered(1)),
                  row, pl.BlockSpec((1, n), lambda i: (0, 0))],
        out_specs=[row, row],
        out_shape=[jax.ShapeDtypeStruct((s, n), F32), jax.ShapeDtypeStruct((s, n), BF16)],
        compiler_params=_cparams("parallel"),
    )(a, w3, res, gain)


def _mm_residual_loss(a, w3, lidx, res, target, tm, name):
    s, k = a.shape
    n = w3.shape[2]
    nsteps = s // tm

    def body(a_ref, b_ref, r_ref, t_ref, dy_ref, dyb_ref, l_ref, acc):
        i = pl.program_id(0)
        diff = r_ref[...] + _dot(a_ref[...], b_ref[...], NN) - t_ref[...]
        dy_ref[...] = diff * (1.0 / n)
        dyb_ref[...] = (diff * (1.0 / n)).astype(dyb_ref.dtype)
        part = jnp.sum((diff * diff).reshape(tm // 8, 8, n), axis=0)

        @pl.when(i == 0)
        def _():
            acc[...] = part

        @pl.when(i > 0)
        def _():
            acc[...] += part

        @pl.when(i == nsteps - 1)
        def _():
            tot = jnp.sum(jnp.sum(acc[...], axis=1, keepdims=True), axis=0, keepdims=True)
            l_ref[...] = jnp.broadcast_to(tot * (0.5 / n), (8, LANES))

    row = pl.BlockSpec((tm, n), lambda i: (i, 0))
    return pl.pallas_call(
        body, name=name, grid=(nsteps,),
        in_specs=[pl.BlockSpec((tm, k), lambda i: (i, 0)),
                  pl.BlockSpec((None, k, n), lambda i: (lidx, 0, 0), pipeline_mode=pl.Buffered(1)),
                  row, row],
        out_specs=[row, row, pl.BlockSpec((8, LANES), lambda i: (0, 0))],
        out_shape=[jax.ShapeDtypeStruct((s, n), F32), jax.ShapeDtypeStruct((s, n), BF16),
                   jax.ShapeDtypeStruct((8, LANES), F32)],
        scratch_shapes=[pltpu.VMEM((8, n), F32)],
        compiler_params=_cparams("arbitrary"),
    )(a, w3, res, target)


def _mm_nt(a, w3, lidx, tm, tn, name):
    s, k = a.shape
    n = w3.shape[1]

    def body(a_ref, b_ref, o_ref):
        o_ref[...] = _dot(a_ref[...], b_ref[...], NT)

    return pl.pallas_call(
        body, name=name, grid=(s // tm, n // tn),
        in_specs=[pl.BlockSpec((tm, k), lambda i, j: (i, 0)),
                  pl.BlockSpec((None, tn, k), lambda i, j: (lidx, j, 0))],
        out_specs=pl.BlockSpec((tm, tn), lambda i, j: (i, j)),
        out_shape=jax.ShapeDtypeStruct((s, n), F32),
        compiler_params=_cparams("parallel", "arbitrary"),
    )(a, w3)


def _mm_nt_swiglu_bwd(dx, wd3, lidx, g, u, tm, name):
    s, d = dx.shape
    cols = BLOCK_PAIR * (g.shape[1] // N_DEV)

    def body(a_ref, b_ref, g_ref, u_ref, dg_ref, du_ref):
        dact = _dot(a_ref[...], b_ref[...], NT)
        gv = g_ref[...].astype(F32)
        sig = 1.0 / (1.0 + jnp.exp(-gv))
        du_ref[...] = (dact * (gv * sig)).astype(du_ref.dtype)
        dg_ref[...] = (dact * u_ref[...].astype(F32) * (sig * (1.0 + gv * (1.0 - sig)))).astype(dg_ref.dtype)

    blk = pl.BlockSpec((tm, cols), lambda i, j: (i, j))
    shape = jax.ShapeDtypeStruct(g.shape, BF16)
    return pl.pallas_call(
        body, name=name, grid=(s // tm, N_DEV // BLOCK_PAIR),
        in_specs=[pl.BlockSpec((tm, d), lambda i, j: (i, 0)),
                  pl.BlockSpec((None, cols, d), lambda i, j: (lidx, j, 0)), blk, blk],
        out_specs=[blk, blk], out_shape=[shape, shape],
        compiler_params=_cparams("parallel", "arbitrary"),
    )(dx, wd3, g, u)


def _mm_nt_norm_bwd(das, ga, widxs, x, gain, dres, tm, name, travel=()):
    s = das[0].shape[0]
    nb, d, cols = ga.shape[1], ga.shape[2], ga.shape[3]
    nw = len(das)
    nsteps = s // tm

    def body(*refs):
        a_refs, b_refs = refs[:nw], refs[nw:2 * nw]
        x_ref, g_ref, dres_ref, dx_ref, dxb_ref, dg_ref = refs[2 * nw:]
        i = pl.program_id(0)
        dhv = None
        wide = BLOCK_PAIR * cols
        for w in range(nw):
            for k in range(nb // BLOCK_PAIR):
                b = jnp.concatenate([b_refs[w][BLOCK_PAIR * k + p] for p in range(BLOCK_PAIR)], axis=1)
                part = _dot(a_refs[w][:, k * wide:(k + 1) * wide], b, NT)
                dhv = part if dhv is None else dhv + part
        xv = x_ref[...]
        r = lax.rsqrt(jnp.mean(xv * xv, axis=-1, keepdims=True) + EPS)
        xhat = xv * r
        dxh = dhv * g_ref[...]
        dxv = dres_ref[...] + r * (dxh - xhat * jnp.mean(dxh * xhat, axis=-1, keepdims=True))
        dx_ref[...] = dxv
        dxb_ref[...] = dxv.astype(dxb_ref.dtype)
        part = jnp.sum((dhv * xhat).reshape(tm // 8, 8, d), axis=0)

        @pl.when(i == 0)
        def _():
            dg_ref[...] = part

        @pl.when(i > 0)
        def _():
            dg_ref[...] += part

        @pl.when(i == nsteps - 1)
        def _():
            dg_ref[...] = jnp.broadcast_to(jnp.sum(dg_ref[...], axis=0, keepdims=True), (8, d))

    def wspec(idx):
        return pl.BlockSpec((None, nb, d, cols), lambda i: (idx, 0, 0, 0), pipeline_mode=pl.Buffered(1))

    row = pl.BlockSpec((tm, d), lambda i: (i, 0))
    body, more_in, more_out, more_shapes, more_scratch = _host_exchange(body, 2 * nw + 3, 3, travel, (nsteps,))
    outs = pl.pallas_call(
        body, name=name, grid=(nsteps,),
        in_specs=([pl.BlockSpec((tm, nb * cols), lambda i: (i, 0))] * nw + [wspec(i) for i in widxs]
                  + [row, pl.BlockSpec((1, d), lambda i: (0, 0)), row] + more_in),
        out_specs=[row, row, pl.BlockSpec((8, d), lambda i: (0, 0))] + more_out,
        out_shape=[jax.ShapeDtypeStruct((s, d), F32), jax.ShapeDtypeStruct((s, d), BF16),
                   jax.ShapeDtypeStruct((8, d), F32)] + more_shapes,
        scratch_shapes=more_scratch,
        compiler_params=_cparams("arbitrary"),
    )(*das, *([ga] * nw), x, gain, dres, *travel)
    return outs[0], outs[1], outs[2], list(outs[3:])


def _mm_tn(a, b, ta, tb, tk, out_blocks, name, travel=()):
    s, ka = a.shape
    nb = b.shape[1]
    nk = s // tk
    cols = tb
    if out_blocks:
        tb = BLOCK_PAIR * cols

    def body(a_ref, b_ref, o_ref, ob_ref):
        k = pl.program_id(2)
        part = _dot(a_ref[...], b_ref[...], TN)

        def put(first):
            if out_blocks:
                for p in range(BLOCK_PAIR):
                    piece = part[:, p * cols:(p + 1) * cols]
                    o_ref[p] = piece if first else o_ref[p] + piece
            else:
                o_ref[...] = part if first else o_ref[...] + part

        @pl.when(k == 0)
        def _():
            put(True)

        @pl.when(k > 0)
        def _():
            put(False)

        @pl.when(k == nk - 1)
        def _():
            ob_ref[...] = o_ref[...].astype(ob_ref.dtype)

    if out_blocks:
        out_spec = pl.BlockSpec((BLOCK_PAIR, ta, cols), lambda i, j, k: (j, i, 0))
        shape = (nb // cols, ka, cols)
    else:
        out_spec = pl.BlockSpec((ta, tb), lambda i, j, k: (i, j))
        shape = (ka, nb)
    grid = (ka // ta, nb // tb, nk)
    body, more_in, more_out, more_shapes, more_scratch = _host_exchange(body, 2, 2, travel, grid)
    outs = pl.pallas_call(
        body, name=name, grid=grid,
        in_specs=[pl.BlockSpec((tk, ta), lambda i, j, k: (k, i)),
                  pl.BlockSpec((tk, tb), lambda i, j, k: (k, j))] + more_in,
        out_specs=[out_spec, out_spec] + more_out,
        out_shape=[jax.ShapeDtypeStruct(shape, F32), jax.ShapeDtypeStruct(shape, BF16)] + more_shapes,
        scratch_shapes=more_scratch,
        compiler_params=_cparams("arbitrary", "arbitrary", "arbitrary"),
    )(a, b, *travel)
    return (outs[0], outs[1]), list(outs[2:])


def _adamw(parts, own, w, m, v, tr, name):
    p, rows, cols = parts.shape
    c1 = 1.0 / (1.0 - ADAM_B1 ** ADAM_STEP)
    c2 = 1.0 / (1.0 - ADAM_B2 ** ADAM_STEP)

    def body(*refs):
        if own is None:
            p_ref, w_ref, m_ref, v_ref, g_ref, d_ref, nm_ref, nv_ref = refs
            g = p_ref[0]
            for k in range(1, p):
                g = g + p_ref[k]
        else:
            p_ref, own_ref, w_ref, m_ref, v_ref, g_ref, d_ref, nm_ref, nv_ref = refs
            x, y, c = _place()
            my = 4 * x + 2 * y + c
            mine = own_ref[...]
            g = jnp.where(my == 0, mine, p_ref[0].astype(F32))
            for k in range(1, p):
                g = g + jnp.where(my == k, mine, p_ref[k].astype(F32))
        nm = ADAM_B1 * m_ref[...] + (1.0 - ADAM_B1) * g
        nv = ADAM_B2 * v_ref[...] + (1.0 - ADAM_B2) * (g * g)
        g_ref[...] = g
        nm_ref[...] = nm
        nv_ref[...] = nv
        d_ref[...] = -ADAM_LR * ((nm * c1) / (jnp.sqrt(nv * c2) + ADAM_EPS) + ADAM_WD * w_ref[...])

    blk = pl.BlockSpec((tr, cols), lambda i: (i, 0))
    shape = jax.ShapeDtypeStruct((rows, cols), F32)
    return pl.pallas_call(
        body, name=name, grid=(rows // tr,),
        in_specs=[pl.BlockSpec((p, tr, cols), lambda i: (0, i, 0))] + [blk] * (3 if own is None else 4),
        out_specs=[blk] * 4, out_shape=[shape] * 4,
        compiler_params=_cparams("parallel"),
    )(*([parts] + ([] if own is None else [own]) + [w, m, v]))


def _adamw_sharded(parts, grads, my, w, m, v, tr, name):
    depth, rows, cols = w.shape
    p, pr, pc = parts[0].shape
    c1 = 1.0 / (1.0 - ADAM_B1 ** ADAM_STEP)
    c2 = 1.0 / (1.0 - ADAM_B2 ** ADAM_STEP)

    def body(my_ref, *refs):
        p_refs, own_refs = refs[:depth], refs[depth:2 * depth]
        w_ref, m_ref, v_ref, g_ref, d_ref, nm_ref, nv_ref = refs[2 * depth:]
        layer = pl.program_id(0)
        for ll in range(depth):
            @pl.when(layer == ll)
            def _(ll=ll):
                mine = own_refs[ll][...]
                g = jnp.where(my_ref[0] == 0, mine, p_refs[ll][0].astype(F32))
                for k in range(1, p):
                    g = g + jnp.where(my_ref[0] == k, mine, p_refs[ll][k].astype(F32))
                g = g[:, :cols]
                nm = ADAM_B1 * m_ref[...] + (1.0 - ADAM_B1) * g
                nv = ADAM_B2 * v_ref[...] + (1.0 - ADAM_B2) * (g * g)
                g_ref[...] = g
                nm_ref[...] = nm
                nv_ref[...] = nv
                d_ref[...] = -ADAM_LR * ((nm * c1) / (jnp.sqrt(nv * c2) + ADAM_EPS) + ADAM_WD * w_ref[...])

    def row_block(ll, l, i):
        return jnp.where(l == ll, i, 0)

    blk = pl.BlockSpec((None, tr, cols), lambda l, i, my_: (l, i, 0))
    shape = jax.ShapeDtypeStruct((depth, rows, cols), F32)
    return pl.pallas_call(
        body, name=name,
        grid_spec=pltpu.PrefetchScalarGridSpec(
            num_scalar_prefetch=1, grid=(depth, rows // tr),
            in_specs=([pl.BlockSpec((p, tr, pc), lambda l, i, my_, ll=ll: (0, row_block(ll, l, i), 0))
                       for ll in range(depth)]
                      + [pl.BlockSpec((None, tr, pc), lambda l, i, my_, ll=ll: (my_[0], row_block(ll, l, i), 0))
                         for ll in range(depth)]
                      + [blk, blk, blk]),
            out_specs=[blk] * 4),
        out_shape=[shape] * 4,
        compiler_params=_cparams("arbitrary", "arbitrary"),
    )(my, *parts, *grads, w, m, v)


def _place():
    x, y, c = lax.axis_index("x"), lax.axis_index("y"), lax.axis_index("c")
    return x, y, c


class _Gather:
    def __init__(self, srcs, dsts, send_sems, recv_sems, local_sems):
        na = len(srcs)
        x, y, c = _place()
        me, sibling = (x, y, c), (x, y, 1 - c)
        chips = [(1 - x, y), (x, 1 - y), (1 - x, 1 - y)]

        def slot(a, dev):
            return dsts[a].at[:, pl.ds(4 * dev[0] + 2 * dev[1] + dev[2], 1)]

        def copy(k, a, block, to, from_shard=False):
            return pltpu.make_async_remote_copy(
                src_ref=srcs[a] if from_shard else slot(a, block), dst_ref=slot(a, block),
                send_sem=send_sems.at[k, a], recv_sem=recv_sems.at[k, a], device_id=to, device_id_type=MESH)

        pairs = [(j, chip, a) for j, chip in enumerate(chips) for a in range(na)]
        self.mine = [pltpu.make_async_copy(srcs[a], slot(a, me), local_sems.at[a]) for a in range(na)]
        self.first = [copy(0, a, me, sibling, True) for a in range(na)]
        self.first += [copy(1 + j, a, me, (*chip, c), True) for j, chip, a in pairs]
        self.over_ici = [copy(1 + j, a, (*chip, c), me) for j, chip, a in pairs]
        self.passed = [copy(4 + j, a, (*chip, c), sibling) for j, chip, a in pairs]
        self.from_sibling = [copy(0, a, sibling, me) for a in range(na)]
        self.from_sibling += [copy(4 + j, a, (*chip, 1 - c), me) for j, chip, a in pairs]

    def begin(self):
        for cp in self.mine + self.first:
            cp.start()

    def relay(self):
        for arrived, onward in zip(self.over_ici, self.passed):
            arrived.wait_recv()
            onward.start()

    def finish(self):
        for cp in self.from_sibling:
            cp.wait_recv()
        for cp in self.first + self.passed:
            cp.wait_send()
        for cp in self.mine:
            cp.wait()


def _gather_scratch(na):
    return [pltpu.SemaphoreType.DMA((7, na)), pltpu.SemaphoreType.DMA((7, na)), pltpu.SemaphoreType.DMA((na,))]


def _gathered_shapes(shards):
    return [jax.ShapeDtypeStruct((a.shape[0], N_DEV) + a.shape[2:], a.dtype) for a in shards]


def _all_gather(shards, name):
    na = len(shards)

    def body(*refs):
        gather = _Gather(refs[:na], refs[na:2 * na], *refs[2 * na:])
        gather.begin()
        gather.relay()
        gather.finish()

    anyspec = pl.BlockSpec(memory_space=pl.ANY)
    return pl.pallas_call(
        body, name=name,
        in_specs=[anyspec] * na, out_specs=[anyspec] * na,
        out_shape=_gathered_shapes(shards), scratch_shapes=_gather_scratch(na),
    )(*shards)


_RELATIONS = [(dx, dy, dc) for dx in (0, 1) for dy in (0, 1) for dc in (0, 1)][1:]


def _flip(v, d):
    return 1 - v if d else v


def _exchange_copies(srcs, dsts, send_sems, recv_sems, local_sems):
    x, y, c = _place()
    my = 4 * x + 2 * y + c
    na = len(srcs)
    mine = [pltpu.make_async_copy(srcs[a].at[pl.ds(my, 1)], dsts[a].at[pl.ds(my, 1)], local_sems.at[a])
            for a in range(na)]
    sends, recvs = [], []
    for k, (dx, dy, dc) in enumerate(_RELATIONS):
        peer = (_flip(x, dx), _flip(y, dy), _flip(c, dc))
        pidx = 4 * peer[0] + 2 * peer[1] + peer[2]
        for a in range(na):
            for into, out in ((my, sends), (pidx, recvs)):
                out.append(pltpu.make_async_remote_copy(
                    src_ref=srcs[a].at[pl.ds(pidx, 1)], dst_ref=dsts[a].at[pl.ds(into, 1)],
                    send_sem=send_sems.at[k, a], recv_sem=recv_sems.at[k, a], device_id=peer, device_id_type=MESH))
    return mine, sends, recvs


def _exchange_begin(copies):
    mine, sends, _ = copies
    for cp in mine + sends:
        cp.start()


def _exchange_finish(copies):
    mine, sends, recvs = copies
    for cp in recvs:
        cp.wait_recv()
    for cp in sends:
        cp.wait_send()
    for cp in mine:
        cp.wait()


def _exchange_scratch(na):
    return [pltpu.SemaphoreType.DMA((7, na)), pltpu.SemaphoreType.DMA((7, na)), pltpu.SemaphoreType.DMA((na,))]


def _host_exchange(body, n_in, n_out, travel, grid):
    nt = len(travel)
    if not nt:
        return body, [], [], [], []

    def wrapped(*refs):
        ins, srcs = refs[:n_in], refs[n_in:n_in + nt]
        outs, rest = refs[n_in + nt:n_in + nt + n_out], refs[n_in + nt + n_out:]
        dsts, scratch = rest[:nt], rest[nt:]
        copies = _exchange_copies(srcs, dsts, *scratch[-3:])
        first = last = None
        for axis, size in enumerate(grid):
            at_start, at_end = pl.program_id(axis) == 0, pl.program_id(axis) == size - 1
            first = at_start if first is None else jnp.logical_and(first, at_start)
            last = at_end if last is None else jnp.logical_and(last, at_end)

        @pl.when(first)
        def _():
            _exchange_begin(copies)

        body(*ins, *outs, *scratch[:-3])

        @pl.when(last)
        def _():
            _exchange_finish(copies)

    anyspec = pl.BlockSpec(memory_space=pl.ANY)
    return (wrapped, [anyspec] * nt, [anyspec] * nt, [jax.ShapeDtypeStruct(t.shape, t.dtype) for t in travel],
            _exchange_scratch(nt))


def _all_reduce_small(v, name):
    r, c_ = v.shape

    def body(v_ref, o_ref, gath, send_sems, recv_sems):
        x, y, c = _place()
        my = 4 * x + 2 * y + c
        gath[my] = v_ref[...]
        sends = []
        for k, (dx, dy, dc) in enumerate(_RELATIONS):
            peer = (_flip(x, dx), _flip(y, dy), _flip(c, dc))
            cp = pltpu.make_async_remote_copy(
                src_ref=v_ref, dst_ref=gath.at[my], send_sem=send_sems.at[k], recv_sem=recv_sems.at[k],
                device_id=peer, device_id_type=MESH)
            cp.start()
            sends.append((cp, 4 * peer[0] + 2 * peer[1] + peer[2], k, peer))
        for cp, pidx, k, peer in sends:
            pltpu.make_async_remote_copy(
                src_ref=v_ref, dst_ref=gath.at[pidx], send_sem=send_sems.at[k], recv_sem=recv_sems.at[k],
                device_id=peer, device_id_type=MESH).wait_recv()
        for cp, *_ in sends:
            cp.wait_send()
        tot = gath[0]
        for k in range(1, N_DEV):
            tot = tot + gath[k]
        o_ref[...] = tot

    vm = pl.BlockSpec(memory_space=pltpu.VMEM)
    return pl.pallas_call(
        body, name=name, in_specs=[vm], out_specs=vm,
        out_shape=jax.ShapeDtypeStruct((r, c_), F32),
        scratch_shapes=[pltpu.VMEM((N_DEV, r, c_), F32), pltpu.SemaphoreType.DMA((7,)),
                        pltpu.SemaphoreType.DMA((7,))],
    )(v)


TM = 512
TM_MATMUL = 2048
TM_RESIDUAL = 1024
TQ = 256


def _device_blocks(t):
    return t.reshape(N_DEV, -1, t.shape[-1])


def _pad_to(a, axis, size):
    pad = [(0, 0)] * a.ndim
    pad[axis] = (0, size - a.shape[axis])
    return jnp.pad(a, pad)


def _local_step(x, target, g_in0, late_shards, conv_full, norm_mix, q_norm, k_norm, norm_ffn):
    depth, d = norm_mix.shape
    cols = g_in0.shape[3]
    tm, tq = min(TM, x.shape[0]), min(TQ, x.shape[0])
    tmm, tmr = min(TM_MATMUL, x.shape[0]), min(TM_RESIDUAL, x.shape[0])
    attn = d // 2
    nheads = attn // HEAD_DIM
    scale = HEAD_DIM ** -0.5 * LOG2E
    saved = []
    h1 = _rmsnorm_fwd(x, norm_mix[0][None], tm, "norm_mix_fwd_0")
    for l in range(depth):
        w_in = (g_in0, 0) if l == 0 else (g_rest, 3 * (l - 1))
        proj = _mm_blocks(h1, *w_in, tmm, f"proj_in_{l}")
        qk_gain = jnp.concatenate([jnp.tile(q_norm[l], nheads) * scale, jnp.tile(k_norm[l], nheads)])[None]
        qk = _qknorm_fwd(proj, qk_gain, tmm, f"qknorm_fwd_{l}")
        o, rtot, used, gathered = _attn_fwd(qk, proj, tq, f"attn_fwd_{l}", late_shards if l == 0 else ())
        if l == 0:
            g_gu0, g_rest, gb, gc = gathered if depth > 1 else (gathered[0], None, *gathered[1:])
            gb = gb.reshape(depth, -1, d)
            gc = gc.reshape(depth, -1, d)
        w_gu = (g_gu0, 0, 1) if l == 0 else (g_rest, 3 * (l - 1) + 1, 3 * (l - 1) + 2)
        conv_w8 = _pad_to(conv_full[l], 0, 8)
        cv = _conv_fwd(proj, conv_w8, f"conv_fwd_{l}")
        mix = jnp.concatenate([o, cv], axis=1)
        x1, h2 = _mm_residual_norm(mix, gb, l, x, norm_ffn[l][None], tmr, f"proj_out_{l}")
        g, u, act = _mm_swiglu(h2, *w_gu, tmr, f"ffn_up_{l}")
        saved.append((x, h1, proj, qk_gain, qk, rtot, used, conv_w8, mix, x1, h2, g, u, act, w_in, w_gu))
        if l + 1 < depth:
            x, h1 = _mm_residual_norm(act, gc, l, x1, norm_mix[l + 1][None], tm, f"ffn_down_{l}")
        else:
            dx, dxb, loss = _mm_residual_loss(act, gc, l, x1, target, tm, f"ffn_down_{l}")

    grads = [None] * depth
    small = [None] * depth
    landed = {}
    for l in reversed(range(depth)):
        x0, h1, proj, qk_gain, qk, rtot, used, conv_w8, mix, x1, h2, g, u, act, w_in, w_gu = saved[l]
        d = x0.shape[1]
        dg, du = _mm_nt_swiglu_bwd(dxb, gc, l, g, u, tmr, f"ffn_down_bwd_{l}")
        d_wdown, _ = _mm_tn(act, dxb, 768, d, tmm, False, f"dw_down_{l}")
        d_wgate, _ = _mm_tn(h2, dg, d, cols, tmm, True, f"dw_gate_{l}")
        d_wup, _ = _mm_tn(h2, du, d, cols, tmm, True, f"dw_up_{l}")
        dx1, dx1b, dg_ffn, _ = _mm_nt_norm_bwd([dg, du], w_gu[0], list(w_gu[1:]), x1, norm_ffn[l][None], dx, tm,
                                               f"ffn_up_bwd_{l}")
        dmix = _mm_nt(dx1b, gb, l, tmr, 512, f"proj_out_bwd_{l}")
        d_wout, _ = _mm_tn(mix, dx1b, 512, d, tmm, False, f"dw_out_{l}")
        dcb, dcc, dcu, dconv = _conv_bwd(dmix, proj, conv_w8, f"conv_bwd_{l}")
        rides = {}
        if l == 0:
            rides = {5 * ll + j: grads[ll][j][1] for ll in range(1, depth) for j in range(5)}
            rides.update({3: d_wout[1], 4: d_wdown[1]})
        dq, dk, dv, arrived = _attn_bwd(qk, proj, dmix, rtot, used, tq, f"attn_bwd_{l}",
                                        [_device_blocks(t) for t in rides.values()])
        landed.update(zip(rides.keys(), arrived))
        dqk, dg_qk = _qknorm_bwd(jnp.concatenate([dq, dk], axis=1), proj, qk_gain, tmm, f"qknorm_bwd_{l}")
        dproj = jnp.concatenate([dqk, dv.astype(BF16), dcb, dcc, dcu], axis=1)
        first = l == 0
        d_win, arrived = _mm_tn(h1, dproj, d, cols, tmm, True, f"dw_in_{l}",
                                [_device_blocks(d_wgate[1])] if first else [])
        landed.update(zip([1], arrived))
        dx, dxb, dg_mix, arrived = _mm_nt_norm_bwd(
            [dproj], w_in[0], [w_in[1]], x0, norm_mix[l][None], dx1, tm, f"proj_in_bwd_{l}",
            [_device_blocks(d_win[1]), _device_blocks(d_wup[1])] if first else [])
        landed.update(zip([0, 2], arrived))
        grads[l] = (d_win, d_wgate, d_wup, d_wout, d_wdown)
        dq_gain = jnp.sum(dg_qk[0, :attn].reshape(nheads, HEAD_DIM), axis=0) * scale
        dk_gain = jnp.sum(dg_qk[0, attn:].reshape(nheads, HEAD_DIM), axis=0)
        small[l] = (dg_mix[0], dg_ffn[0], dq_gain, dk_gain, dconv[:3])
    return loss, dx, grads, small, landed


def kernel(x, norm_mix, w_in, q_norm, k_norm, conv_w, w_out, norm_ffn, w_gate, w_up, w_down, loss_target, m_norm_mix, m_w_in, m_q_norm, m_k_norm, m_conv_w, m_w_out, m_norm_ffn, m_w_gate, m_w_up, m_w_down, v_norm_mix, v_w_in, v_q_norm, v_k_norm, v_conv_w, v_w_out, v_norm_ffn, v_w_gate, v_w_up, v_w_down):
    depth, d, in_shard = w_in.shape
    ff_shard = w_gate.shape[2]
    ff_pad = in_shard
    conv_shard = conv_w.shape[2]
    xs = x.reshape(x.shape[-2], d)
    target = loss_target.reshape(xs.shape)

    pa = jnp.stack([w_in, _pad_to(w_gate, 2, ff_pad), _pad_to(w_up, 2, ff_pad)], axis=1)
    pa = pa.reshape(3 * depth, 1, d, in_shard).astype(BF16)
    pd = _pad_to(_pad_to(conv_w.reshape(depth * 3, conv_shard), 0, 8), 1, LANES)[None, None]
    g_in0, gd = _all_gather([pa[:1], pd], "gather_first")
    conv_full = gd[0, :, :depth * 3, :conv_shard].transpose(1, 0, 2).reshape(depth, 3, N_DEV * conv_shard)
    late_shards = [pa[1:3]] + ([pa[3:]] if depth > 1 else [])
    late_shards += [w_out.astype(BF16)[:, None], _pad_to(w_down, 1, ff_pad).astype(BF16)[:, None]]

    loss, grad_x, grads, small, landed = _local_step(xs, target, g_in0, late_shards, conv_full, norm_mix, q_norm,
                                                     k_norm, norm_ffn)

    x_, y_, c_ = _place()
    my = 4 * x_ + 2 * y_ + c_

    nconv = N_DEV * conv_shard
    rows = []
    for l in range(depth):
        g_mix, g_ffn, g_q, g_k, g_conv = small[l]
        qkrow = _pad_to(jnp.concatenate([g_q, g_k]), 0, d)
        rows += [g_mix[None], g_ffn[None], qkrow[None], _pad_to(g_conv, 1, d)]
    nrow = 6 * depth
    packed = jnp.concatenate(rows + [_pad_to(loss[:1], 1, d)], axis=0)
    packed = _pad_to(packed, 0, ((nrow + 1 + 7) // 8) * 8)
    summed = _all_reduce_small(packed, "reduce_small")
    loss_out = summed[nrow, 0]

    my1 = my.astype(jnp.int32).reshape(1)

    def big(j, w, m, v, tr, name):
        return _adamw_sharded([landed[5 * l + j] for l in range(depth)],
                              [_device_blocks(grads[l][j][0]) for l in range(depth)], my1, w, m, v, tr, name)

    res = {"w_in": big(0, w_in, m_w_in, v_w_in, 256, "adamw_in"),
           "w_gate": big(1, w_gate, m_w_gate, v_w_gate, 256, "adamw_gate"),
           "w_up": big(2, w_up, m_w_up, v_w_up, 256, "adamw_up"),
           "w_out": big(3, w_out, m_w_out, v_w_out, w_out.shape[1], "adamw_out"),
           "w_down": big(4, w_down, m_w_down, v_w_down, ff_shard // 2, "adamw_down")}

    g_rows, w_rows, m_rows, v_rows = [], [], [], []
    for l in range(depth):
        base = l * 6
        conv_g = lax.dynamic_slice(summed[base + 3:base + 6], (0, my * conv_shard), (3, conv_shard))
        g_rows += [summed[base:base + 3], _pad_to(conv_g, 1, d)]
        for dst, (nm, qn, kn, nf, cw) in ((w_rows, (norm_mix, q_norm, k_norm, norm_ffn, conv_w)),
                                          (m_rows, (m_norm_mix, m_q_norm, m_k_norm, m_norm_ffn, m_conv_w)),
                                          (v_rows, (v_norm_mix, v_q_norm, v_k_norm, v_norm_ffn, v_conv_w))):
            dst += [nm[l][None], nf[l][None], _pad_to(jnp.concatenate([qn[l], kn[l]]), 0, d)[None],
                    _pad_to(cw[l], 1, d)]
    prow = ((nrow + 7) // 8) * 8
    gs, ws, ms, vs = [_pad_to(jnp.concatenate(t, axis=0), 0, prow) for t in (g_rows, w_rows, m_rows, v_rows)]
    sm = _adamw(gs[None], None, ws, ms, vs, prow, "adamw_small")

    hd = q_norm.shape[1]

    def small_out(t, kind):
        per_layer = []
        for l in range(depth):
            base = l * 6
            per_layer.append({"norm_mix": t[base], "norm_ffn": t[base + 1], "q_norm": t[base + 2, :hd],
                              "k_norm": t[base + 2, hd:2 * hd], "conv_w": t[base + 3:base + 6, :conv_shard]}[kind])
        return jnp.stack(per_layer)

    def big_out(name, i):
        return res[name][i]

    outs = [loss_out, grad_x.reshape(x.shape)]
    for i in range(4):
        outs += [small_out(sm[i], "norm_mix"), big_out("w_in", i), small_out(sm[i], "q_norm"),
                 small_out(sm[i], "k_norm"), small_out(sm[i], "conv_w"), big_out("w_out", i),
                 small_out(sm[i], "norm_ffn"), big_out("w_gate", i), big_out("w_up", i), big_out("w_down", i)]
    return tuple(outs)
```

```python
import jax
import jax.numpy as jnp
from jax import lax
from jax.experimental import pallas as pl
from jax.experimental.pallas import tpu as pltpu

F32 = jnp.float32
BF16 = jnp.bfloat16
MESH = pl.DeviceIdType.MESH

N_DEV = 8
LANES = 128
HEAD_DIM = 64
KEY_CHUNK = 128
EPS = 1e-6
VMEM_LIMIT = 48 * 1024 * 1024

ADAM_LR = 0.001
ADAM_B1 = 0.9
ADAM_B2 = 0.999
ADAM_EPS = 1e-08
ADAM_WD = 0.01
ADAM_STEP = 10

NN = (((1,), (0,)), ((), ()))
NT = (((1,), (1,)), ((), ()))
TN = (((0,), (0,)), ((), ()))


def _dot(a, b, dims):
    return lax.dot_general(a.astype(BF16), b.astype(BF16), dims, preferred_element_type=F32)


def _cparams(*sem):
    return pltpu.CompilerParams(dimension_semantics=sem, vmem_limit_bytes=VMEM_LIMIT)


def _rmsnorm_fwd(x, gain, tm, name):
    s, d = x.shape

    def body(x_ref, g_ref, o_ref):
        xv = x_ref[...]
        r = lax.rsqrt(jnp.mean(xv * xv, axis=-1, keepdims=True) + EPS)
        o_ref[...] = ((xv * r) * g_ref[...]).astype(o_ref.dtype)

    return pl.pallas_call(
        body, name=name, grid=(s // tm,),
        in_specs=[pl.BlockSpec((tm, d), lambda i: (i, 0)), pl.BlockSpec((1, d), lambda i: (0, 0))],
        out_specs=pl.BlockSpec((tm, d), lambda i: (i, 0)),
        out_shape=jax.ShapeDtypeStruct((s, d), BF16),
        compiler_params=_cparams("parallel"),
    )(x, gain)


def _group_mean_matrix():
    r = lax.broadcasted_iota(jnp.int32, (LANES, LANES), 0) // HEAD_DIM
    c = lax.broadcasted_iota(jnp.int32, (LANES, LANES), 1) // HEAD_DIM
    return jnp.where(r == c, 1.0 / HEAD_DIM, 0.0).astype(BF16)


def _group_mean(v, gm):
    hi = v.astype(BF16)
    lo = (v - hi.astype(F32)).astype(BF16)
    return _dot(hi, gm, NN) + _dot(lo, gm, NN)


def _qknorm_fwd(proj, gains, tm, name):
    s = proj.shape[0]
    ncol = gains.shape[1] // LANES

    def body(p_ref, g_ref, gm_ref, o_ref):
        xv = p_ref[...].astype(F32)
        r = lax.rsqrt(_group_mean(xv * xv, gm_ref[...]) + EPS)
        o_ref[...] = ((xv * r) * g_ref[...]).astype(o_ref.dtype)

    blk = pl.BlockSpec((tm, LANES), lambda i, j: (i, j))
    return pl.pallas_call(
        body, name=name, grid=(s // tm, ncol),
        in_specs=[blk, pl.BlockSpec((1, LANES), lambda i, j: (0, j)),
                  pl.BlockSpec((LANES, LANES), lambda i, j: (0, 0))],
        out_specs=blk,
        out_shape=jax.ShapeDtypeStruct((s, ncol * LANES), BF16),
        compiler_params=_cparams("parallel", "parallel"),
    )(proj, gains, _group_mean_matrix())


def _qknorm_bwd(dqk, proj, gains, tm, name):
    s = proj.shape[0]
    ncol = gains.shape[1] // LANES
    nsteps = s // tm

    def body(dy_ref, p_ref, g_ref, gm_ref, dx_ref, dg_ref):
        i = pl.program_id(1)
        gm = gm_ref[...]
        xv = p_ref[...].astype(F32)
        r = lax.rsqrt(_group_mean(xv * xv, gm) + EPS)
        xhat = xv * r
        dy = dy_ref[...]
        dxh = dy * g_ref[...]
        proj_ = _group_mean(dxh * xhat, gm)
        dx_ref[...] = (r * (dxh - xhat * proj_)).astype(dx_ref.dtype)
        part = jnp.sum((dy * xhat).reshape(tm // 8, 8, LANES), axis=0)

        @pl.when(i == 0)
        def _():
            dg_ref[...] = part

        @pl.when(i > 0)
        def _():
            dg_ref[...] += part

        @pl.when(i == nsteps - 1)
        def _():
            dg_ref[...] = jnp.broadcast_to(jnp.sum(dg_ref[...], axis=0, keepdims=True), (8, LANES))

    blk = pl.BlockSpec((tm, LANES), lambda j, i: (i, j))
    return pl.pallas_call(
        body, name=name, grid=(ncol, nsteps),
        in_specs=[blk, blk, pl.BlockSpec((1, LANES), lambda j, i: (0, j)),
                  pl.BlockSpec((LANES, LANES), lambda j, i: (0, 0))],
        out_specs=[blk, pl.BlockSpec((8, LANES), lambda j, i: (0, j))],
        out_shape=[jax.ShapeDtypeStruct((s, ncol * LANES), BF16),
                   jax.ShapeDtypeStruct((8, ncol * LANES), F32)],
        compiler_params=_cparams("parallel", "arbitrary"),
    )(dqk, proj, gains, _group_mean_matrix())


CONV_ROWS = 256
HALO = 8


def _conv_fwd(proj, conv_w8, name):
    s = proj.shape[0]
    nblk = conv_w8.shape[1] // LANES
    first = 3 * nblk
    nchunk = s // CONV_ROWS

    def body(cb_ref, cc_ref, cu_ref, w_ref, y_ref, hpad):
        hpad[pl.ds(0, 2 * HALO), :] = jnp.zeros((2 * HALO, LANES), F32)

        def fill(i, _):
            r0 = pl.multiple_of(i * CONV_ROWS, CONV_ROWS)
            hpad[pl.ds(r0 + 2 * HALO, CONV_ROWS), :] = (
                cc_ref[pl.ds(r0, CONV_ROWS), :].astype(F32) * cu_ref[pl.ds(r0, CONV_ROWS), :].astype(F32))
            return 0

        lax.fori_loop(0, nchunk, fill, 0)
        w0, w1, w2 = w_ref[0:1, :], w_ref[1:2, :], w_ref[2:3, :]

        def conv(i, _):
            r0 = pl.multiple_of(i * CONV_ROWS, CONV_ROWS)
            win = hpad[pl.ds(r0 + HALO, CONV_ROWS + HALO), :]
            c = (w2 * win[HALO:] + w1 * pltpu.roll(win, 1, 0)[HALO:] + w0 * pltpu.roll(win, 2, 0)[HALO:])
            y_ref[pl.ds(r0, CONV_ROWS), :] = (cb_ref[pl.ds(r0, CONV_ROWS), :].astype(F32) * c).astype(y_ref.dtype)
            return 0

        lax.fori_loop(0, nchunk, conv, 0)

    def col(off):
        return pl.BlockSpec((s, LANES), lambda j: (0, off + j))

    return pl.pallas_call(
        body, name=name, grid=(nblk,),
        in_specs=[col(first), col(first + nblk), col(first + 2 * nblk), pl.BlockSpec((8, LANES), lambda j: (0, j))],
        out_specs=pl.BlockSpec((s, LANES), lambda j: (0, j)),
        out_shape=jax.ShapeDtypeStruct((s, nblk * LANES), BF16),
        scratch_shapes=[pltpu.VMEM((s + 2 * HALO, LANES), F32)],
        compiler_params=_cparams("parallel"),
    )(proj, proj, proj, conv_w8)


def _conv_bwd(dmix, proj, conv_w8, name):
    s = proj.shape[0]
    nblk = conv_w8.shape[1] // LANES
    first = 3 * nblk
    nchunk = s // CONV_ROWS

    def body(dy_ref, cb_ref, cc_ref, cu_ref, w_ref, dcb_ref, dcc_ref, dcu_ref, dw_ref, hpad, dcpad):
        hpad[pl.ds(0, 2 * HALO), :] = jnp.zeros((2 * HALO, LANES), F32)
        dcpad[pl.ds(s, 2 * HALO), :] = jnp.zeros((2 * HALO, LANES), F32)

        def fill(i, _):
            r0 = pl.multiple_of(i * CONV_ROWS, CONV_ROWS)
            hpad[pl.ds(r0 + 2 * HALO, CONV_ROWS), :] = (
                cc_ref[pl.ds(r0, CONV_ROWS), :].astype(F32) * cu_ref[pl.ds(r0, CONV_ROWS), :].astype(F32))
            return 0

        lax.fori_loop(0, nchunk, fill, 0)
        w0, w1, w2 = w_ref[0:1, :], w_ref[1:2, :], w_ref[2:3, :]

        def fold(v):
            return jnp.sum(v.reshape(CONV_ROWS // 8, 8, LANES), axis=0)

        def first_pass(i, acc):
            a0, a1, a2 = acc
            r0 = pl.multiple_of(i * CONV_ROWS, CONV_ROWS)
            win = hpad[pl.ds(r0 + HALO, CONV_ROWS + HALO), :]
            h0 = win[HALO:]
            h1 = pltpu.roll(win, 1, 0)[HALO:]
            h2 = pltpu.roll(win, 2, 0)[HALO:]
            c = w2 * h0 + w1 * h1 + w0 * h2
            dy = dy_ref[pl.ds(r0, CONV_ROWS), :]
            dcb_ref[pl.ds(r0, CONV_ROWS), :] = (dy * c).astype(dcb_ref.dtype)
            dc = dy * cb_ref[pl.ds(r0, CONV_ROWS), :].astype(F32)
            dcpad[pl.ds(r0, CONV_ROWS), :] = dc
            return a0 + fold(dc * h2), a1 + fold(dc * h1), a2 + fold(dc * h0)

        z8 = jnp.zeros((8, LANES), F32)
        a0, a1, a2 = lax.fori_loop(0, nchunk, first_pass, (z8, z8, z8))
        dw_ref[...] = jnp.concatenate(
            [jnp.sum(a0, axis=0, keepdims=True), jnp.sum(a1, axis=0, keepdims=True),
             jnp.sum(a2, axis=0, keepdims=True), jnp.zeros((5, LANES), F32)], axis=0)

        def second_pass(i, _):
            r0 = pl.multiple_of(i * CONV_ROWS, CONV_ROWS)
            win = dcpad[pl.ds(r0, CONV_ROWS + HALO), :]
            n = CONV_ROWS + HALO
            dh = (w2 * win[:CONV_ROWS] + w1 * pltpu.roll(win, n - 1, 0)[:CONV_ROWS]
                  + w0 * pltpu.roll(win, n - 2, 0)[:CONV_ROWS])
            dcc_ref[pl.ds(r0, CONV_ROWS), :] = (dh * cu_ref[pl.ds(r0, CONV_ROWS), :].astype(F32)).astype(dcc_ref.dtype)
            dcu_ref[pl.ds(r0, CONV_ROWS), :] = (dh * cc_ref[pl.ds(r0, CONV_ROWS), :].astype(F32)).astype(dcu_ref.dtype)
            return 0

        lax.fori_loop(0, nchunk, second_pass, 0)

    def col(off):
        return pl.BlockSpec((s, LANES), lambda j: (0, off + j))

    out = pl.BlockSpec((s, LANES), lambda j: (0, j))
    return pl.pallas_call(
        body, name=name, grid=(nblk,),
        in_specs=[col(nblk), col(first), col(first + nblk), col(first + 2 * nblk),
                  pl.BlockSpec((8, LANES), lambda j: (0, j))],
        out_specs=[out, out, out, pl.BlockSpec((8, LANES), lambda j: (0, j))],
        out_shape=[jax.ShapeDtypeStruct((s, nblk * LANES), BF16)] * 3 + [jax.ShapeDtypeStruct((8, nblk * LANES), F32)],
        scratch_shapes=[pltpu.VMEM((s + 2 * HALO, LANES), F32), pltpu.VMEM((s + 2 * HALO, LANES), F32)],
        compiler_params=_cparams("parallel"),
    )(dmix, proj, proj, proj, conv_w8)


LOG2E = 1.4426950408889634
LN2 = 0.6931471805599453
NEG_BIG = -1e30
SATURATED = 160.0


def _cumsum_matrix(kind):
    j = lax.broadcasted_iota(jnp.int32, (KEY_CHUNK, 2 * KEY_CHUNK), 0)
    c = lax.broadcasted_iota(jnp.int32, (KEY_CHUNK, 2 * KEY_CHUNK), 1)
    tri = {"after": j > c, "upto": j <= c, "before": j < c}[kind]
    return jnp.where((c >= KEY_CHUNK) | tri, 1.0, 0.0).astype(BF16)


def _stack_heads(t, m0):
    zero = jnp.zeros_like(t)
    return jnp.concatenate([jnp.where(m0, t, zero), jnp.where(m0, zero, t)], axis=0)


def _softplus2(z):
    sp = jnp.maximum(z, 0.0) + jnp.log2(1.0 + jnp.exp2(-jnp.abs(z)))
    return sp, z - sp


def _key_chunk(ref, kc):
    return ref[pl.ds(pl.multiple_of(kc * KEY_CHUNK, KEY_CHUNK), KEY_CHUNK), :]


def _attn_bwd(qk, proj, dmix, rtot, used, tq, name, travel=()):
    s = qk.shape[0]
    nhp = qk.shape[1] // (2 * LANES)
    nc = tq // KEY_CHUNK
    nq = s // tq
    nt = len(travel)

    def body(used_ref, q_ref, k_ref, v_ref, do_ref, r_ref, cmi_ref, cme_ref, bias_ref, dq_ref, dk_ref, dv_ref,
             z_refs, ls_refs, sig_refs, sp_refs, gb_refs, pr_ref, gs_ref, copies):
        qi = pl.program_id(1)

        @pl.when(qi == 0)
        def _():
            dk_ref[...] = jnp.zeros_like(dk_ref)
            dv_ref[...] = jnp.zeros_like(dv_ref)

        if copies is not None:
            @pl.when(jnp.logical_and(pl.program_id(0) == 0, qi == 0))
            def _():
                _exchange_begin(copies)

        nslots = (qi + 1) * nc
        walked = used_ref[pl.program_id(0), qi].astype(jnp.int32)
        first = jnp.clip(nslots - walked, 0, nslots - nc) // nc * nc
        m0 = lax.broadcasted_iota(jnp.int32, (1, LANES), 1) < HEAD_DIM
        qs = _stack_heads(q_ref[...], m0)
        do = do_ref[...]
        dos = _stack_heads(do.astype(BF16), m0)
        dosl = _stack_heads((do * LN2).astype(BF16), m0)
        cmi = cmi_ref[...]
        cme = cme_ref[...]

        def chunk_at(i):
            return jnp.clip(i, first, nslots - 1)

        def scores(kc):
            return _dot(qs, _key_chunk(k_ref, kc), NT)

        def weights(ls, cs, da, pr, kc):
            a = jnp.exp2(ls - (pr - cs[:, :KEY_CHUNK]))
            gb = (a * da).astype(BF16)
            ks = pl.multiple_of(kc * KEY_CHUNK, KEY_CHUNK)
            dv_ref[pl.ds(ks, KEY_CHUNK), :] += _dot(a, dos, TN)
            return gb, jnp.exp2(ls), pr - cs[:, KEY_CHUNK:]

        def score_grads(gb, sig, cg, gs, dq, kc):
            dzb = (gb.astype(F32) * (1.0 - sig) - sig * (gs + cg[:, :KEY_CHUNK])).astype(BF16)
            ks = pl.multiple_of(kc * KEY_CHUNK, KEY_CHUNK)
            dk_ref[pl.ds(ks, KEY_CHUNK), :] += _dot(dzb, qs, TN)
            dq = dq + _dot(jnp.concatenate([dzb[:tq], dzb[tq:]], axis=1), _stack_heads(_key_chunk(k_ref, kc), m0), NN)
            return gs + cg[:, KEY_CHUNK:], dq

        def step(i, par, bias=None, stages="zswg"):
            cur, prv = par, 1 - par
            k1, k2 = chunk_at(i - 1), chunk_at(i - 2)
            z_next = scores(chunk_at(i + 1))
            if "w" in stages:
                cs = _dot(sp_refs[prv][...], cmi, NN)
                da = _dot(dosl, _key_chunk(v_ref, k1), NT)
            if "g" in stages:
                cg = _dot(gb_refs[cur][...], cme, NN)
            z = z_refs[cur][...]
            if bias is not None:
                z = z + bias
            sp, ls = _softplus2(z)
            sp_refs[cur][...] = sp.astype(BF16)
            ls_refs[cur][...] = ls
            if "g" in stages:
                gs, dq = score_grads(gb_refs[cur][...], sig_refs[cur][...], cg, gs_ref[...], dq_ref[...], k2)
                gs_ref[...] = gs
                dq_ref[...] = dq
            if "w" in stages:
                gb, sig, pr = weights(ls_refs[prv][...], cs, da, pr_ref[...], k1)
                gb_refs[prv][...] = gb
                sig_refs[prv][...] = sig
                pr_ref[...] = pr
            z_refs[prv][...] = z_next

        pr_ref[...] = jnp.concatenate([r_ref[:, :LANES], r_ref[:, LANES:]], axis=0)
        gs_ref[...] = jnp.zeros((2 * tq, LANES), F32)
        dq_ref[...] = jnp.zeros((tq, LANES), F32)
        z_refs[0][...] = scores(first)
        only_diagonal = first == nslots - nc
        step(first, 0, jnp.where(only_diagonal, bias_ref[0], 0.0), stages="zs")
        step(first + 1, 1, jnp.where(only_diagonal, bias_ref[1], 0.0), stages="zsw")

        def two_steps(j, _):
            step(2 * j, 0)
            step(2 * j + 1, 1)
            return 0

        lax.fori_loop(first // 2 + 1, nslots // 2 - 1, two_steps, 0)

        @pl.when(jnp.logical_not(only_diagonal))
        def _():
            step(nslots - 2, 0, bias_ref[0])
            step(nslots - 1, 1, bias_ref[1])

        k1, k2 = chunk_at(nslots - 1), chunk_at(nslots - 2)
        gb, sig, _ = weights(ls_refs[1][...], _dot(sp_refs[1][...], cmi, NN),
                             _dot(dosl, _key_chunk(v_ref, k1), NT), pr_ref[...], k1)
        gb2 = gb_refs[0][...]
        gs, dq = score_grads(gb2, sig_refs[0][...], _dot(gb2, cme, NN), gs_ref[...], dq_ref[...], k2)
        _, dq = score_grads(gb, sig, _dot(gb, cme, NN), gs, dq, k1)
        dq_ref[...] = dq

        if copies is not None:
            @pl.when(jnp.logical_and(pl.program_id(0) == nhp - 1, qi == nq - 1))
            def _():
                _exchange_finish(copies)

    def wrapped(*refs):
        ins, rest = refs[:9], refs[9:]
        srcs, rest = rest[:nt], rest[nt:]
        outs, rest = rest[:3], rest[3:]
        lands, rest = rest[:nt], rest[nt:]
        z0, z1, ls0, ls1, sg0, sg1, sp0, sp1, gb0, gb1, pr_ref, gs_ref = rest[:12]
        copies = _exchange_copies(srcs, lands, *rest[12:]) if nt else None
        body(*ins, *outs, (z0, z1), (ls0, ls1), (sg0, sg1), (sp0, sp1), (gb0, gb1), pr_ref, gs_ref, copies)

    assert nc == 2
    bias = _diag_bias(tq, True)
    bias = jnp.concatenate([bias[:, :, :KEY_CHUNK], bias[:, :, KEY_CHUNK:]], axis=1)
    qblk = pl.BlockSpec((tq, LANES), lambda p, i: (i, p))
    full = pl.BlockSpec((s, LANES), lambda p, i: (0, p))
    cmspec = pl.BlockSpec((KEY_CHUNK, 2 * KEY_CHUNK), lambda p, i: (0, 0))
    anyspec = pl.BlockSpec(memory_space=pl.ANY)
    shape = jax.ShapeDtypeStruct((s, nhp * LANES), F32)
    f32buf = pltpu.VMEM((2 * tq, LANES), F32)
    bf16buf = pltpu.VMEM((2 * tq, LANES), BF16)
    outs = pl.pallas_call(
        wrapped, name=name, grid=(nhp, nq),
        in_specs=[pl.BlockSpec(memory_space=pltpu.SMEM),
                  qblk,
                  pl.BlockSpec((s, LANES), lambda p, i: (0, nhp + p)),
                  pl.BlockSpec((s, LANES), lambda p, i: (0, 2 * nhp + p)),
                  qblk,
                  pl.BlockSpec((tq, 2 * LANES), lambda p, i: (i, p)),
                  cmspec, cmspec,
                  pl.BlockSpec((nc, 2 * tq, LANES), lambda p, i: (0, 0, 0))] + [anyspec] * nt,
        out_specs=[qblk, full, full] + [anyspec] * nt,
        out_shape=[shape, shape, shape] + [jax.ShapeDtypeStruct(t.shape, t.dtype) for t in travel],
        scratch_shapes=[f32buf] * 6 + [bf16buf] * 4 + [f32buf] * 2 + (_exchange_scratch(nt) if nt else []),
        compiler_params=_cparams("arbitrary", "arbitrary"),
    )(used, qk, qk, proj, dmix, rtot, _cumsum_matrix("upto"), _cumsum_matrix("before"), bias, *travel)
    return outs[0], outs[1], outs[2], list(outs[3:])


def _pair_cumsum_matrix(kind):
    j = lax.broadcasted_iota(jnp.int32, (2 * KEY_CHUNK, 4 * KEY_CHUNK), 0)
    c = lax.broadcasted_iota(jnp.int32, (2 * KEY_CHUNK, 4 * KEY_CHUNK), 1)
    same_head = (j // KEY_CHUNK) == ((c // KEY_CHUNK) % 2)
    jj, cc = j % KEY_CHUNK, c % KEY_CHUNK
    tri = {"after": jj > cc, "upto": jj <= cc, "before": jj < cc}[kind]
    return jnp.where(same_head & ((c >= 2 * KEY_CHUNK) | tri), 1.0, 0.0).astype(BF16)


def _diag_bias(tq, ascending):
    nc = tq // KEY_CHUNK
    shape = (nc, tq, 2 * KEY_CHUNK)
    d = lax.broadcasted_iota(jnp.int32, shape, 0)
    r = lax.broadcasted_iota(jnp.int32, shape, 1)
    c = lax.broadcasted_iota(jnp.int32, shape, 2) % KEY_CHUNK
    chunk = d if ascending else nc - 1 - d
    return jnp.where(chunk * KEY_CHUNK + c < r, 0.0, NEG_BIG).astype(F32)


def _attn_fwd(qk, proj, tq, name, shards=()):
    s = qk.shape[0]
    nhp = qk.shape[1] // (2 * LANES)
    nc = tq // KEY_CHUNK
    nq = s // tq
    ng = len(shards)
    assert nc == 2
    w = 2 * KEY_CHUNK

    def body(q_ref, k_ref, v_ref, cm_ref, bias_ref, o_ref, r_ref, used_ref, z_refs, ls_refs, cs_refs, ct_refs,
             sp_refs, ab_refs, acc_ref, gather):
        qi = pl.program_id(1)
        if gather is not None:
            @pl.when(jnp.logical_and(pl.program_id(0) == 0, qi == 0))
            def _():
                gather.begin()

        nslots = (qi + 1) * nc
        m0 = lax.broadcasted_iota(jnp.int32, (1, LANES), 1) < HEAD_DIM
        q = q_ref[...]
        cm = cm_ref[...]

        def chunk_at(i):
            return jnp.clip(nslots - 1 - i, 0, nslots - 1)

        def scores(kc):
            return _dot(q, _stack_heads(_key_chunk(k_ref, kc), m0), NT)

        def values(ab, kc):
            return _dot(ab, _stack_heads(_key_chunk(v_ref, kc), m0), NN)

        def step(i, par, bias=None, stages="zscwv"):
            cur, prv = par, 1 - par
            if "z" in stages:
                z_next = scores(chunk_at(i + 1))
            if "c" in stages:
                cs = _dot(sp_refs[prv][...], cm, NN)
            if "v" in stages:
                pv = values(ab_refs[prv][...], chunk_at(i - 3))
            if "w" in stages:
                rs = r_ref[...]
                r_ref[...] = rs + ct_refs[cur][...]
                ab_refs[cur][...] = jnp.exp2(ls_refs[cur][...] - cs_refs[cur][...] - rs).astype(BF16)
            if "s" in stages:
                z = z_refs[cur][...]
                if bias is not None:
                    z = z + bias
                sp, ls = _softplus2(z)
                sp_refs[cur][...] = sp.astype(BF16)
                ls_refs[cur][...] = ls
            if "v" in stages:
                acc_ref[...] += pv
            if "c" in stages:
                cs_refs[prv][...] = cs[:, :w]
                ct_refs[prv][...] = cs[:, w:]
            if "z" in stages:
                z_refs[prv][...] = z_next

        z_refs[0][...] = scores(chunk_at(0))
        ab_refs[1][...] = jnp.zeros((tq, w), BF16)
        r_ref[...] = jnp.zeros((tq, w), F32)
        acc_ref[...] = jnp.zeros((tq, LANES), F32)
        step(0, 0, bias_ref[0], stages="zs")
        step(1, 1, bias_ref[1], stages="zsc")

        def two_steps(carry):
            j, _ = carry
            step(2 * j, 0)
            step(2 * j + 1, 1)
            return j + 1, jnp.min(jnp.minimum(r_ref[:, :KEY_CHUNK], r_ref[:, KEY_CHUNK:]))

        pairs, low = lax.while_loop(lambda c: jnp.logical_and(c[0] < nslots // 2, c[1] < SATURATED), two_steps,
                                    (jnp.int32(1), jnp.float32(0.0)))
        entered = 2 * pairs
        saturated = low >= SATURATED

        @pl.when(saturated)
        def _():
            step(entered, 0, stages="v")

        @pl.when(jnp.logical_not(saturated))
        def _():
            step(entered, 0, stages="cwv")
            step(entered + 1, 1, stages="wv")
            step(entered + 2, 0, stages="v")

        o_ref[...] = acc_ref[...].astype(o_ref.dtype)
        used_ref[pl.program_id(0), qi] = jnp.where(saturated, entered - 2, entered).astype(F32)

        if gather is not None:
            @pl.when(jnp.logical_and(pl.program_id(0) == nhp - 1, qi == nq // 2))
            def _():
                gather.relay()

            @pl.when(jnp.logical_and(pl.program_id(0) == nhp - 1, qi == nq - 1))
            def _():
                gather.finish()

    def wrapped(*refs):
        ins, rest = refs[:5], refs[5:]
        srcs, rest = rest[:ng], rest[ng:]
        outs, rest = rest[:3], rest[3:]
        dsts, scratch = rest[:ng], rest[ng:]
        z, ls, cs, ct, sp, ab = [scratch[2 * j:2 * j + 2] for j in range(6)]
        gather = _Gather(srcs, dsts, *scratch[13:]) if ng else None
        body(*ins, *outs, z, ls, cs, ct, sp, ab, scratch[12], gather)

    f32buf = pltpu.VMEM((tq, w), F32)
    bf16buf = pltpu.VMEM((tq, w), BF16)
    anyspec = pl.BlockSpec(memory_space=pl.ANY)
    outs = pl.pallas_call(
        wrapped, name=name, grid=(nhp, nq),
        in_specs=[pl.BlockSpec((tq, LANES), lambda p, i: (i, p)),
                  pl.BlockSpec((s, LANES), lambda p, i: (0, nhp + p)),
                  pl.BlockSpec((s, LANES), lambda p, i: (0, 2 * nhp + p)),
                  pl.BlockSpec((w, 2 * w), lambda p, i: (0, 0)),
                  pl.BlockSpec((nc, tq, w), lambda p, i: (0, 0, 0))] + [anyspec] * ng,
        out_specs=[pl.BlockSpec((tq, LANES), lambda p, i: (i, p)),
                   pl.BlockSpec((tq, w), lambda p, i: (i, p)),
                   pl.BlockSpec(memory_space=pltpu.SMEM)] + [anyspec] * ng,
        out_shape=[jax.ShapeDtypeStruct((s, nhp * LANES), BF16),
                   jax.ShapeDtypeStruct((s, nhp * w), F32),
                   jax.ShapeDtypeStruct((nhp, nq), F32)] + _gathered_shapes(shards),
        scratch_shapes=([f32buf] * 8 + [bf16buf] * 4 + [pltpu.VMEM((tq, LANES), F32)]
                        + (_gather_scratch(ng) if ng else [])),
        compiler_params=_cparams("arbitrary", "arbitrary"),
    )(qk, qk, proj, _pair_cumsum_matrix("after"), _diag_bias(tq, False), *shards)
    return outs[0], outs[1], outs[2], list(outs[3:])


BLOCK_PAIR = 2


def _side_by_side(b_ref):
    return jnp.concatenate([b_ref[p] for p in range(BLOCK_PAIR)], axis=1)


def _mm_blocks(h, ga, widx, tm, name):
    s, d = h.shape
    nb, cols = ga.shape[1], ga.shape[3]

    def body(a_ref, b_ref, o_ref):
        o_ref[...] = _dot(a_ref[...], _side_by_side(b_ref), NN).astype(o_ref.dtype)

    return pl.pallas_call(
        body, name=name, grid=(s // tm, nb // BLOCK_PAIR),
        in_specs=[pl.BlockSpec((tm, d), lambda i, j: (i, 0)),
                  pl.BlockSpec((None, BLOCK_PAIR, d, cols), lambda i, j: (widx, j, 0, 0))],
        out_specs=pl.BlockSpec((tm, BLOCK_PAIR * cols), lambda i, j: (i, j)),
        out_shape=jax.ShapeDtypeStruct((s, nb * cols), BF16),
        compiler_params=_cparams("parallel", "arbitrary"),
    )(h, ga)


def _mm_swiglu(h, ga, gidx, uidx, tm, name):
    s, d = h.shape
    nb, cols = ga.shape[1], ga.shape[3]

    def body(a_ref, bg_ref, bu_ref, g_ref, u_ref, act_ref):
        a = a_ref[...]
        g = _dot(a, _side_by_side(bg_ref), NN)
        u = _dot(a, _side_by_side(bu_ref), NN)
        g_ref[...] = g.astype(g_ref.dtype)
        u_ref[...] = u.astype(u_ref.dtype)
        act_ref[...] = (g * (1.0 / (1.0 + jnp.exp(-g))) * u).astype(act_ref.dtype)

    def wspec(idx):
        return pl.BlockSpec((None, BLOCK_PAIR, d, cols), lambda i, j: (idx, j, 0, 0))

    out = pl.BlockSpec((tm, BLOCK_PAIR * cols), lambda i, j: (i, j))
    shape = jax.ShapeDtypeStruct((s, nb * cols), BF16)
    return pl.pallas_call(
        body, name=name, grid=(s // tm, nb // BLOCK_PAIR),
        in_specs=[pl.BlockSpec((tm, d), lambda i, j: (i, 0)), wspec(gidx), wspec(uidx)],
        out_specs=[out, out, out], out_shape=[shape, shape, shape],
        compiler_params=_cparams("parallel", "arbitrary"),
    )(h, ga, ga)


def _mm_residual_norm(a, w3, lidx, res, gain, tm, name):
    s, k = a.shape
    n = w3.shape[2]

    def body(a_ref, b_ref, r_ref, g_ref, o_ref, h_ref):
        xv = r_ref[...] + _dot(a_ref[...], b_ref[...], NN)
        o_ref[...] = xv
        r = lax.rsqrt(jnp.mean(xv * xv, axis=-1, keepdims=True) + EPS)
        h_ref[...] = ((xv * r) * g_ref[...]).astype(h_ref.dtype)

    row = pl.BlockSpec((tm, n), lambda i: (i, 0))
    return pl.pallas_call(
        body, name=name, grid=(s // tm,),
        in_specs=[pl.BlockSpec((tm, k), lambda i: (i, 0)),
                  pl.BlockSpec((None, k, n), lambda i: (lidx, 0, 0), pipeline_mode=pl.Buffered(1)),
                  row, pl.BlockSpec((1, n), lambda i: (0, 0))],
        out_specs=[row, row],
        out_shape=[jax.ShapeDtypeStruct((s, n), F32), jax.ShapeDtypeStruct((s, n), BF16)],
        compiler_params=_cparams("parallel"),
    )(a, w3, res, gain)


def _mm_residual_loss(a, w3, lidx, res, target, tm, name):
    s, k = a.shape
    n = w3.shape[2]
    nsteps = s // tm

    def body(a_ref, b_ref, r_ref, t_ref, dy_ref, dyb_ref, l_ref, acc):
        i = pl.program_id(0)
        diff = r_ref[...] + _dot(a_ref[...], b_ref[...], NN) - t_ref[...]
        dy_ref[...] = diff * (1.0 / n)
        dyb_ref[...] = (diff * (1.0 / n)).astype(dyb_ref.dtype)
        part = jnp.sum((diff * diff).reshape(tm // 8, 8, n), axis=0)

        @pl.when(i == 0)
        def _():
            acc[...] = part

        @pl.when(i > 0)
        def _():
            acc[...] += part

        @pl.when(i == nsteps - 1)
        def _():
            tot = jnp.sum(jnp.sum(acc[...], axis=1, keepdims=True), axis=0, keepdims=True)
            l_ref[...] = jnp.broadcast_to(tot * (0.5 / n), (8, LANES))

    row = pl.BlockSpec((tm, n), lambda i: (i, 0))
    return pl.pallas_call(
        body, name=name, grid=(nsteps,),
        in_specs=[pl.BlockSpec((tm, k), lambda i: (i, 0)),
                  pl.BlockSpec((None, k, n), lambda i: (lidx, 0, 0), pipeline_mode=pl.Buffered(1)),
                  row, row],
        out_specs=[row, row, pl.BlockSpec((8, LANES), lambda i: (0, 0))],
        out_shape=[jax.ShapeDtypeStruct((s, n), F32), jax.ShapeDtypeStruct((s, n), BF16),
                   jax.ShapeDtypeStruct((8, LANES), F32)],
        scratch_shapes=[pltpu.VMEM((8, n), F32)],
        compiler_params=_cparams("arbitrary"),
    )(a, w3, res, target)


def _mm_nt(a, w3, lidx, tm, tn, name):
    s, k = a.shape
    n = w3.shape[1]

    def body(a_ref, b_ref, o_ref):
        o_ref[...] = _dot(a_ref[...], b_ref[...], NT)

    return pl.pallas_call(
        body, name=name, grid=(s // tm, n // tn),
        in_specs=[pl.BlockSpec((tm, k), lambda i, j: (i, 0)),
                  pl.BlockSpec((None, tn, k), lambda i, j: (lidx, j, 0))],
        out_specs=pl.BlockSpec((tm, tn), lambda i, j: (i, j)),
        out_shape=jax.ShapeDtypeStruct((s, n), F32),
        compiler_params=_cparams("parallel", "arbitrary"),
    )(a, w3)


def _mm_nt_swiglu_bwd(dx, wd3, lidx, g, u, tm, name):
    s, d = dx.shape
    cols = BLOCK_PAIR * (g.shape[1] // N_DEV)

    def body(a_ref, b_ref, g_ref, u_ref, dg_ref, du_ref):
        dact = _dot(a_ref[...], b_ref[...], NT)
        gv = g_ref[...].astype(F32)
        sig = 1.0 / (1.0 + jnp.exp(-gv))
        du_ref[...] = (dact * (gv * sig)).astype(du_ref.dtype)
        dg_ref[...] = (dact * u_ref[...].astype(F32) * (sig * (1.0 + gv * (1.0 - sig)))).astype(dg_ref.dtype)

    blk = pl.BlockSpec((tm, cols), lambda i, j: (i, j))
    shape = jax.ShapeDtypeStruct(g.shape, BF16)
    return pl.pallas_call(
        body, name=name, grid=(s // tm, N_DEV // BLOCK_PAIR),
        in_specs=[pl.BlockSpec((tm, d), lambda i, j: (i, 0)),
                  pl.BlockSpec((None, cols, d), lambda i, j: (lidx, j, 0)), blk, blk],
        out_specs=[blk, blk], out_shape=[shape, shape],
        compiler_params=_cparams("parallel", "arbitrary"),
    )(dx, wd3, g, u)


def _mm_nt_norm_bwd(das, ga, widxs, x, gain, dres, tm, name, travel=()):
    s = das[0].shape[0]
    nb, d, cols = ga.shape[1], ga.shape[2], ga.shape[3]
    nw = len(das)
    nsteps = s // tm

    def body(*refs):
        a_refs, b_refs = refs[:nw], refs[nw:2 * nw]
        x_ref, g_ref, dres_ref, dx_ref, dxb_ref, dg_ref = refs[2 * nw:]
        i = pl.program_id(0)
        dhv = None
        wide = BLOCK_PAIR * cols
        for w in range(nw):
            for k in range(nb // BLOCK_PAIR):
                b = jnp.concatenate([b_refs[w][BLOCK_PAIR * k + p] for p in range(BLOCK_PAIR)], axis=1)
                part = _dot(a_refs[w][:, k * wide:(k + 1) * wide], b, NT)
                dhv = part if dhv is None else dhv + part
        xv = x_ref[...]
        r = lax.rsqrt(jnp.mean(xv * xv, axis=-1, keepdims=True) + EPS)
        xhat = xv * r
        dxh = dhv * g_ref[...]
        dxv = dres_ref[...] + r * (dxh - xhat * jnp.mean(dxh * xhat, axis=-1, keepdims=True))
        dx_ref[...] = dxv
        dxb_ref[...] = dxv.astype(dxb_ref.dtype)
        part = jnp.sum((dhv * xhat).reshape(tm // 8, 8, d), axis=0)

        @pl.when(i == 0)
        def _():
            dg_ref[...] = part

        @pl.when(i > 0)
        def _():
            dg_ref[...] += part

        @pl.when(i == nsteps - 1)
        def _():
            dg_ref[...] = jnp.broadcast_to(jnp.sum(dg_ref[...], axis=0, keepdims=True), (8, d))

    def wspec(idx):
        return pl.BlockSpec((None, nb, d, cols), lambda i: (idx, 0, 0, 0), pipeline_mode=pl.Buffered(1))

    row = pl.BlockSpec((tm, d), lambda i: (i, 0))
    body, more_in, more_out, more_shapes, more_scratch = _host_exchange(body, 2 * nw + 3, 3, travel, (nsteps,))
    outs = pl.pallas_call(
        body, name=name, grid=(nsteps,),
        in_specs=([pl.BlockSpec((tm, nb * cols), lambda i: (i, 0))] * nw + [wspec(i) for i in widxs]
                  + [row, pl.BlockSpec((1, d), lambda i: (0, 0)), row] + more_in),
        out_specs=[row, row, pl.BlockSpec((8, d), lambda i: (0, 0))] + more_out,
        out_shape=[jax.ShapeDtypeStruct((s, d), F32), jax.ShapeDtypeStruct((s, d), BF16),
                   jax.ShapeDtypeStruct((8, d), F32)] + more_shapes,
        scratch_shapes=more_scratch,
        compiler_params=_cparams("arbitrary"),
    )(*das, *([ga] * nw), x, gain, dres, *travel)
    return outs[0], outs[1], outs[2], list(outs[3:])


def _mm_tn(a, b, ta, tb, tk, out_blocks, name, travel=()):
    s, ka = a.shape
    nb = b.shape[1]
    nk = s // tk
    cols = tb
    if out_blocks:
        tb = BLOCK_PAIR * cols

    def body(a_ref, b_ref, o_ref, ob_ref):
        k = pl.program_id(2)
        part = _dot(a_ref[...], b_ref[...], TN)

        def put(first):
            if out_blocks:
                for p in range(BLOCK_PAIR):
                    piece = part[:, p * cols:(p + 1) * cols]
                    o_ref[p] = piece if first else o_ref[p] + piece
            else:
                o_ref[...] = part if first else o_ref[...] + part

        @pl.when(k == 0)
        def _():
            put(True)

        @pl.when(k > 0)
        def _():
            put(False)

        @pl.when(k == nk - 1)
        def _():
            ob_ref[...] = o_ref[...].astype(ob_ref.dtype)

    if out_blocks:
        out_spec = pl.BlockSpec((BLOCK_PAIR, ta, cols), lambda i, j, k: (j, i, 0))
        shape = (nb // cols, ka, cols)
    else:
        out_spec = pl.BlockSpec((ta, tb), lambda i, j, k: (i, j))
        shape = (ka, nb)
    grid = (ka // ta, nb // tb, nk)
    body, more_in, more_out, more_shapes, more_scratch = _host_exchange(body, 2, 2, travel, grid)
    outs = pl.pallas_call(
        body, name=name, grid=grid,
        in_specs=[pl.BlockSpec((tk, ta), lambda i, j, k: (k, i)),
                  pl.BlockSpec((tk, tb), lambda i, j, k: (k, j))] + more_in,
        out_specs=[out_spec, out_spec] + more_out,
        out_shape=[jax.ShapeDtypeStruct(shape, F32), jax.ShapeDtypeStruct(shape, BF16)] + more_shapes,
        scratch_shapes=more_scratch,
        compiler_params=_cparams("arbitrary", "arbitrary", "arbitrary"),
    )(a, b, *travel)
    return (outs[0], outs[1]), list(outs[2:])


def _adamw(g, w, m, v, name):
    rows, cols = g.shape
    c1 = 1.0 / (1.0 - ADAM_B1 ** ADAM_STEP)
    c2 = 1.0 / (1.0 - ADAM_B2 ** ADAM_STEP)

    def body(p_ref, w_ref, m_ref, v_ref, g_ref, d_ref, nm_ref, nv_ref):
        gv = p_ref[...]
        nm = ADAM_B1 * m_ref[...] + (1.0 - ADAM_B1) * gv
        nv = ADAM_B2 * v_ref[...] + (1.0 - ADAM_B2) * (gv * gv)
        g_ref[...] = gv
        nm_ref[...] = nm
        nv_ref[...] = nv
        d_ref[...] = -ADAM_LR * ((nm * c1) / (jnp.sqrt(nv * c2) + ADAM_EPS) + ADAM_WD * w_ref[...])

    blk = pl.BlockSpec((rows, cols), lambda i: (0, 0))
    shape = jax.ShapeDtypeStruct((rows, cols), F32)
    return pl.pallas_call(
        body, name=name, grid=(1,),
        in_specs=[blk] * 4, out_specs=[blk] * 4, out_shape=[shape] * 4,
        compiler_params=_cparams("arbitrary"),
    )(g, w, m, v)


def _adamw_sharded(parts, grads, my, w, m, v, tr, name):
    depth, rows, cols = w.shape
    p, pr, pc = parts[0].shape
    c1 = 1.0 / (1.0 - ADAM_B1 ** ADAM_STEP)
    c2 = 1.0 / (1.0 - ADAM_B2 ** ADAM_STEP)

    def body(my_ref, *refs):
        p_refs, own_refs = refs[:depth], refs[depth:2 * depth]
        w_ref, m_ref, v_ref, g_ref, d_ref, nm_ref, nv_ref = refs[2 * depth:]
        layer = pl.program_id(0)
        for ll in range(depth):
            @pl.when(layer == ll)
            def _(ll=ll):
                mine = own_refs[ll][...]
                g = jnp.where(my_ref[0] == 0, mine, p_refs[ll][0].astype(F32))
                for k in range(1, p):
                    g = g + jnp.where(my_ref[0] == k, mine, p_refs[ll][k].astype(F32))
                g = g[:, :cols]
                nm = ADAM_B1 * m_ref[...] + (1.0 - ADAM_B1) * g
                nv = ADAM_B2 * v_ref[...] + (1.0 - ADAM_B2) * (g * g)
                g_ref[...] = g
                nm_ref[...] = nm
                nv_ref[...] = nv
                d_ref[...] = -ADAM_LR * ((nm * c1) / (jnp.sqrt(nv * c2) + ADAM_EPS) + ADAM_WD * w_ref[...])

    def row_block(ll, l, i):
        return jnp.where(l == ll, i, 0)

    blk = pl.BlockSpec((None, tr, cols), lambda l, i, my_: (l, i, 0))
    shape = jax.ShapeDtypeStruct((depth, rows, cols), F32)
    return pl.pallas_call(
        body, name=name,
        grid_spec=pltpu.PrefetchScalarGridSpec(
            num_scalar_prefetch=1, grid=(depth, rows // tr),
            in_specs=([pl.BlockSpec((p, tr, pc), lambda l, i, my_, ll=ll: (0, row_block(ll, l, i), 0))
                       for ll in range(depth)]
                      + [pl.BlockSpec((None, tr, pc), lambda l, i, my_, ll=ll: (my_[0], row_block(ll, l, i), 0))
                         for ll in range(depth)]
                      + [blk, blk, blk]),
            out_specs=[blk] * 4),
        out_shape=[shape] * 4,
        compiler_params=_cparams("arbitrary", "arbitrary"),
    )(my, *parts, *grads, w, m, v)


def _place():
    x, y, c = lax.axis_index("x"), lax.axis_index("y"), lax.axis_index("c")
    return x, y, c


class _Gather:
    def __init__(self, srcs, dsts, send_sems, recv_sems, local_sems):
        na = len(srcs)
        x, y, c = _place()
        me, sibling = (x, y, c), (x, y, 1 - c)
        chips = [(1 - x, y), (x, 1 - y), (1 - x, 1 - y)]

        def slot(a, dev):
            return dsts[a].at[:, pl.ds(4 * dev[0] + 2 * dev[1] + dev[2], 1)]

        def copy(k, a, block, to, from_shard=False):
            return pltpu.make_async_remote_copy(
                src_ref=srcs[a] if from_shard else slot(a, block), dst_ref=slot(a, block),
                send_sem=send_sems.at[k, a], recv_sem=recv_sems.at[k, a], device_id=to, device_id_type=MESH)

        pairs = [(j, chip, a) for j, chip in enumerate(chips) for a in range(na)]
        self.mine = [pltpu.make_async_copy(srcs[a], slot(a, me), local_sems.at[a]) for a in range(na)]
        self.first = [copy(0, a, me, sibling, True) for a in range(na)]
        self.first += [copy(1 + j, a, me, (*chip, c), True) for j, chip, a in pairs]
        self.over_ici = [copy(1 + j, a, (*chip, c), me) for j, chip, a in pairs]
        self.passed = [copy(4 + j, a, (*chip, c), sibling) for j, chip, a in pairs]
        self.from_sibling = [copy(0, a, sibling, me) for a in range(na)]
        self.from_sibling += [copy(4 + j, a, (*chip, 1 - c), me) for j, chip, a in pairs]

    def begin(self):
        for cp in self.mine + self.first:
            cp.start()

    def relay(self):
        for arrived, onward in zip(self.over_ici, self.passed):
            arrived.wait_recv()
            onward.start()

    def finish(self):
        for cp in self.from_sibling:
            cp.wait_recv()
        for cp in self.first + self.passed:
            cp.wait_send()
        for cp in self.mine:
            cp.wait()


def _gather_scratch(na):
    return [pltpu.SemaphoreType.DMA((7, na)), pltpu.SemaphoreType.DMA((7, na)), pltpu.SemaphoreType.DMA((na,))]


def _gathered_shapes(shards):
    return [jax.ShapeDtypeStruct((a.shape[0], N_DEV) + a.shape[2:], a.dtype) for a in shards]


def _all_gather(shards, name):
    na = len(shards)

    def body(*refs):
        gather = _Gather(refs[:na], refs[na:2 * na], *refs[2 * na:])
        gather.begin()
        gather.relay()
        gather.finish()

    anyspec = pl.BlockSpec(memory_space=pl.ANY)
    return pl.pallas_call(
        body, name=name,
        in_specs=[anyspec] * na, out_specs=[anyspec] * na,
        out_shape=_gathered_shapes(shards), scratch_shapes=_gather_scratch(na),
    )(*shards)


_RELATIONS = [(dx, dy, dc) for dx in (0, 1) for dy in (0, 1) for dc in (0, 1)][1:]


def _flip(v, d):
    return 1 - v if d else v


def _exchange_copies(srcs, dsts, send_sems, recv_sems, local_sems):
    x, y, c = _place()
    my = 4 * x + 2 * y + c
    na = len(srcs)
    mine = [pltpu.make_async_copy(srcs[a].at[pl.ds(my, 1)], dsts[a].at[pl.ds(my, 1)], local_sems.at[a])
            for a in range(na)]
    sends, recvs = [], []
    for k, (dx, dy, dc) in enumerate(_RELATIONS):
        peer = (_flip(x, dx), _flip(y, dy), _flip(c, dc))
        pidx = 4 * peer[0] + 2 * peer[1] + peer[2]
        for a in range(na):
            for into, out in ((my, sends), (pidx, recvs)):
                out.append(pltpu.make_async_remote_copy(
                    src_ref=srcs[a].at[pl.ds(pidx, 1)], dst_ref=dsts[a].at[pl.ds(into, 1)],
                    send_sem=send_sems.at[k, a], recv_sem=recv_sems.at[k, a], device_id=peer, device_id_type=MESH))
    return mine, sends, recvs


def _exchange_begin(copies):
    mine, sends, _ = copies
    for cp in mine + sends:
        cp.start()


def _exchange_finish(copies):
    mine, sends, recvs = copies
    for cp in recvs:
        cp.wait_recv()
    for cp in sends:
        cp.wait_send()
    for cp in mine:
        cp.wait()


def _exchange_scratch(na):
    return [pltpu.SemaphoreType.DMA((7, na)), pltpu.SemaphoreType.DMA((7, na)), pltpu.SemaphoreType.DMA((na,))]


def _host_exchange(body, n_in, n_out, travel, grid):
    nt = len(travel)
    if not nt:
        return body, [], [], [], []

    def wrapped(*refs):
        ins, srcs = refs[:n_in], refs[n_in:n_in + nt]
        outs, rest = refs[n_in + nt:n_in + nt + n_out], refs[n_in + nt + n_out:]
        dsts, scratch = rest[:nt], rest[nt:]
        copies = _exchange_copies(srcs, dsts, *scratch[-3:])
        first = last = None
        for axis, size in enumerate(grid):
            at_start, at_end = pl.program_id(axis) == 0, pl.program_id(axis) == size - 1
            first = at_start if first is None else jnp.logical_and(first, at_start)
            last = at_end if last is None else jnp.logical_and(last, at_end)

        @pl.when(first)
        def _():
            _exchange_begin(copies)

        body(*ins, *outs, *scratch[:-3])

        @pl.when(last)
        def _():
            _exchange_finish(copies)

    anyspec = pl.BlockSpec(memory_space=pl.ANY)
    return (wrapped, [anyspec] * nt, [anyspec] * nt, [jax.ShapeDtypeStruct(t.shape, t.dtype) for t in travel],
            _exchange_scratch(nt))


def _all_reduce_small(v, name):
    r, c_ = v.shape

    def body(v_ref, o_ref, gath, send_sems, recv_sems):
        x, y, c = _place()
        my = 4 * x + 2 * y + c
        gath[my] = v_ref[...]
        sends = []
        for k, (dx, dy, dc) in enumerate(_RELATIONS):
            peer = (_flip(x, dx), _flip(y, dy), _flip(c, dc))
            cp = pltpu.make_async_remote_copy(
                src_ref=v_ref, dst_ref=gath.at[my], send_sem=send_sems.at[k], recv_sem=recv_sems.at[k],
                device_id=peer, device_id_type=MESH)
            cp.start()
            sends.append((cp, 4 * peer[0] + 2 * peer[1] + peer[2], k, peer))
        for cp, pidx, k, peer in sends:
            pltpu.make_async_remote_copy(
                src_ref=v_ref, dst_ref=gath.at[pidx], send_sem=send_sems.at[k], recv_sem=recv_sems.at[k],
                device_id=peer, device_id_type=MESH).wait_recv()
        for cp, *_ in sends:
            cp.wait_send()
        tot = gath[0]
        for k in range(1, N_DEV):
            tot = tot + gath[k]
        o_ref[...] = tot

    vm = pl.BlockSpec(memory_space=pltpu.VMEM)
    return pl.pallas_call(
        body, name=name, in_specs=[vm], out_specs=vm,
        out_shape=jax.ShapeDtypeStruct((r, c_), F32),
        scratch_shapes=[pltpu.VMEM((N_DEV, r, c_), F32), pltpu.SemaphoreType.DMA((7,)),
                        pltpu.SemaphoreType.DMA((7,))],
    )(v)


TM = 512
TM_MATMUL = 2048
TM_RESIDUAL = 1024
TQ = 256


def _device_blocks(t):
    return t.reshape(N_DEV, -1, t.shape[-1])


def _pad_to(a, axis, size):
    pad = [(0, 0)] * a.ndim
    pad[axis] = (0, size - a.shape[axis])
    return jnp.pad(a, pad)


def _local_step(x, target, g_in0, late_shards, conv_full, norm_mix, q_norm, k_norm, norm_ffn):
    depth, d = norm_mix.shape
    cols = g_in0.shape[3]
    tm, tq = min(TM, x.shape[0]), min(TQ, x.shape[0])
    tmm, tmr = min(TM_MATMUL, x.shape[0]), min(TM_RESIDUAL, x.shape[0])
    attn = d // 2
    nheads = attn // HEAD_DIM
    scale = HEAD_DIM ** -0.5 * LOG2E
    saved = []
    h1 = _rmsnorm_fwd(x, norm_mix[0][None], tm, "norm_mix_fwd_0")
    for l in range(depth):
        w_in = (g_in0, 0) if l == 0 else (g_rest, 3 * (l - 1))
        proj = _mm_blocks(h1, *w_in, tmm, f"proj_in_{l}")
        qk_gain = jnp.concatenate([jnp.tile(q_norm[l], nheads) * scale, jnp.tile(k_norm[l], nheads)])[None]
        qk = _qknorm_fwd(proj, qk_gain, tmm, f"qknorm_fwd_{l}")
        o, rtot, used, gathered = _attn_fwd(qk, proj, tq, f"attn_fwd_{l}", late_shards if l == 0 else ())
        if l == 0:
            g_gu0, g_rest, gb, gc = gathered if depth > 1 else (gathered[0], None, *gathered[1:])
            gb = gb.reshape(depth, -1, d)
            gc = gc.reshape(depth, -1, d)
        w_gu = (g_gu0, 0, 1) if l == 0 else (g_rest, 3 * (l - 1) + 1, 3 * (l - 1) + 2)
        conv_w8 = _pad_to(conv_full[l], 0, 8)
        cv = _conv_fwd(proj, conv_w8, f"conv_fwd_{l}")
        mix = jnp.concatenate([o, cv], axis=1)
        x1, h2 = _mm_residual_norm(mix, gb, l, x, norm_ffn[l][None], tmr, f"proj_out_{l}")
        g, u, act = _mm_swiglu(h2, *w_gu, tmr, f"ffn_up_{l}")
        saved.append((x, h1, proj, qk_gain, qk, rtot, used, conv_w8, mix, x1, h2, g, u, act, w_in, w_gu))
        if l + 1 < depth:
            x, h1 = _mm_residual_norm(act, gc, l, x1, norm_mix[l + 1][None], tm, f"ffn_down_{l}")
        else:
            dx, dxb, loss = _mm_residual_loss(act, gc, l, x1, target, tm, f"ffn_down_{l}")

    grads = [None] * depth
    small = [None] * depth
    landed = {}
    for l in reversed(range(depth)):
        x0, h1, proj, qk_gain, qk, rtot, used, conv_w8, mix, x1, h2, g, u, act, w_in, w_gu = saved[l]
        d = x0.shape[1]
        dg, du = _mm_nt_swiglu_bwd(dxb, gc, l, g, u, tmr, f"ffn_down_bwd_{l}")
        d_wdown, _ = _mm_tn(act, dxb, 768, d, tmm, False, f"dw_down_{l}")
        d_wgate, _ = _mm_tn(h2, dg, d, cols, tmm, True, f"dw_gate_{l}")
        d_wup, _ = _mm_tn(h2, du, d, cols, tmm, True, f"dw_up_{l}")
        dx1, dx1b, dg_ffn, _ = _mm_nt_norm_bwd([dg, du], w_gu[0], list(w_gu[1:]), x1, norm_ffn[l][None], dx, tm,
                                               f"ffn_up_bwd_{l}")
        dmix = _mm_nt(dx1b, gb, l, tmr, 512, f"proj_out_bwd_{l}")
        d_wout, _ = _mm_tn(mix, dx1b, 512, d, tmm, False, f"dw_out_{l}")
        dcb, dcc, dcu, dconv = _conv_bwd(dmix, proj, conv_w8, f"conv_bwd_{l}")
        rides = {}
        if l == 0:
            rides = {5 * ll + j: grads[ll][j][1] for ll in range(1, depth) for j in range(5)}
            rides.update({3: d_wout[1], 4: d_wdown[1]})
        dq, dk, dv, arrived = _attn_bwd(qk, proj, dmix, rtot, used, tq, f"attn_bwd_{l}",
                                        [_device_blocks(t) for t in rides.values()])
        landed.update(zip(rides.keys(), arrived))
        dqk, dg_qk = _qknorm_bwd(jnp.concatenate([dq, dk], axis=1), proj, qk_gain, tmm, f"qknorm_bwd_{l}")
        dproj = jnp.concatenate([dqk, dv.astype(BF16), dcb, dcc, dcu], axis=1)
        first = l == 0
        d_win, arrived = _mm_tn(h1, dproj, d, cols, tmm, True, f"dw_in_{l}",
                                [_device_blocks(d_wgate[1])] if first else [])
        landed.update(zip([1], arrived))
        dx, dxb, dg_mix, arrived = _mm_nt_norm_bwd(
            [dproj], w_in[0], [w_in[1]], x0, norm_mix[l][None], dx1, tm, f"proj_in_bwd_{l}",
            [_device_blocks(d_win[1]), _device_blocks(d_wup[1])] if first else [])
        landed.update(zip([0, 2], arrived))
        grads[l] = (d_win, d_wgate, d_wup, d_wout, d_wdown)
        dq_gain = jnp.sum(dg_qk[0, :attn].reshape(nheads, HEAD_DIM), axis=0) * scale
        dk_gain = jnp.sum(dg_qk[0, attn:].reshape(nheads, HEAD_DIM), axis=0)
        small[l] = (dg_mix[0], dg_ffn[0], dq_gain, dk_gain, dconv[:3])
    return loss, dx, grads, small, landed


def kernel(x, norm_mix, w_in, q_norm, k_norm, conv_w, w_out, norm_ffn, w_gate, w_up, w_down, loss_target, m_norm_mix, m_w_in, m_q_norm, m_k_norm, m_conv_w, m_w_out, m_norm_ffn, m_w_gate, m_w_up, m_w_down, v_norm_mix, v_w_in, v_q_norm, v_k_norm, v_conv_w, v_w_out, v_norm_ffn, v_w_gate, v_w_up, v_w_down):
    depth, d, in_shard = w_in.shape
    ff_shard = w_gate.shape[2]
    ff_pad = in_shard
    conv_shard = conv_w.shape[2]
    xs = x.reshape(x.shape[-2], d)
    target = loss_target.reshape(xs.shape)

    pa = jnp.stack([w_in, _pad_to(w_gate, 2, ff_pad), _pad_to(w_up, 2, ff_pad)], axis=1)
    pa = pa.reshape(3 * depth, 1, d, in_shard).astype(BF16)
    pd = _pad_to(_pad_to(conv_w.reshape(depth * 3, conv_shard), 0, 8), 1, LANES)[None, None]
    g_in0, gd = _all_gather([pa[:1], pd], "gather_first")
    conv_full = gd[0, :, :depth * 3, :conv_shard].transpose(1, 0, 2).reshape(depth, 3, N_DEV * conv_shard)
    late_shards = [pa[1:3]] + ([pa[3:]] if depth > 1 else [])
    late_shards += [w_out.astype(BF16)[:, None], _pad_to(w_down, 1, ff_pad).astype(BF16)[:, None]]

    loss, grad_x, grads, small, landed = _local_step(xs, target, g_in0, late_shards, conv_full, norm_mix, q_norm,
                                                     k_norm, norm_ffn)

    x_, y_, c_ = _place()
    my = 4 * x_ + 2 * y_ + c_

    rows = []
    for l in range(depth):
        g_mix, g_ffn, g_q, g_k, g_conv = small[l]
        qkrow = _pad_to(jnp.concatenate([g_q, g_k]), 0, d)
        rows += [g_mix[None], g_ffn[None], qkrow[None], _pad_to(g_conv, 1, d)]
    nrow = 6 * depth
    packed = jnp.concatenate(rows + [_pad_to(loss[:1], 1, d)], axis=0)
    packed = _pad_to(packed, 0, ((nrow + 1 + 7) // 8) * 8)
    summed = _all_reduce_small(packed, "reduce_small")
    loss_out = summed[nrow, 0]

    my1 = my.astype(jnp.int32).reshape(1)

    def big(j, w, m, v, tr, name):
        return _adamw_sharded([landed[5 * l + j] for l in range(depth)],
                              [_device_blocks(grads[l][j][0]) for l in range(depth)], my1, w, m, v, tr, name)

    res = {"w_in": big(0, w_in, m_w_in, v_w_in, 256, "adamw_in"),
           "w_gate": big(1, w_gate, m_w_gate, v_w_gate, 256, "adamw_gate"),
           "w_up": big(2, w_up, m_w_up, v_w_up, 256, "adamw_up"),
           "w_out": big(3, w_out, m_w_out, v_w_out, w_out.shape[1], "adamw_out"),
           "w_down": big(4, w_down, m_w_down, v_w_down, ff_shard // 2, "adamw_down")}

    g_rows, w_rows, m_rows, v_rows = [], [], [], []
    for l in range(depth):
        base = l * 6
        conv_g = lax.dynamic_slice(summed[base + 3:base + 6], (0, my * conv_shard), (3, conv_shard))
        g_rows += [summed[base:base + 3], _pad_to(conv_g, 1, d)]
        for dst, (nm, qn, kn, nf, cw) in ((w_rows, (norm_mix, q_norm, k_norm, norm_ffn, conv_w)),
                                          (m_rows, (m_norm_mix, m_q_norm, m_k_norm, m_norm_ffn, m_conv_w)),
                                          (v_rows, (v_norm_mix, v_q_norm, v_k_norm, v_norm_ffn, v_conv_w))):
            dst += [nm[l][None], nf[l][None], _pad_to(jnp.concatenate([qn[l], kn[l]]), 0, d)[None],
                    _pad_to(cw[l], 1, d)]
    prow = ((nrow + 7) // 8) * 8
    gs, ws, ms, vs = [_pad_to(jnp.concatenate(t, axis=0), 0, prow) for t in (g_rows, w_rows, m_rows, v_rows)]
    sm = _adamw(gs, ws, ms, vs, "adamw_small")

    hd = q_norm.shape[1]

    def small_out(t, kind):
        per_layer = []
        for l in range(depth):
            base = l * 6
            per_layer.append({"norm_mix": t[base], "norm_ffn": t[base + 1], "q_norm": t[base + 2, :hd],
                              "k_norm": t[base + 2, hd:2 * hd], "conv_w": t[base + 3:base + 6, :conv_shard]}[kind])
        return jnp.stack(per_layer)

    def big_out(name, i):
        return res[name][i]

    outs = [loss_out, grad_x.reshape(x.shape)]
    for i in range(4):
        outs += [small_out(sm[i], "norm_mix"), big_out("w_in", i), small_out(sm[i], "q_norm"),
                 small_out(sm[i], "k_norm"), small_out(sm[i], "conv_w"), big_out("w_out", i),
                 small_out(sm[i], "norm_ffn"), big_out("w_gate", i), big_out("w_up", i), big_out("w_down", i)]
    return tuple(outs)
```

```python
import jax
import jax.numpy as jnp
from jax import lax
from jax.experimental import pallas as pl
from jax.experimental.pallas import tpu as pltpu

F32 = jnp.float32
BF16 = jnp.bfloat16
MESH = pl.DeviceIdType.MESH

N_DEV = 8
LANES = 128
HEAD_DIM = 64
KEY_CHUNK = 128
EPS = 1e-6
VMEM_LIMIT = 48 * 1024 * 1024

ADAM_LR = 0.001
ADAM_B1 = 0.9
ADAM_B2 = 0.999
ADAM_EPS = 1e-08
ADAM_WD = 0.01
ADAM_STEP = 10

NN = (((1,), (0,)), ((), ()))
NT = (((1,), (1,)), ((), ()))
TN = (((0,), (0,)), ((), ()))


def _dot(a, b, dims):
    return lax.dot_general(a.astype(BF16), b.astype(BF16), dims, preferred_element_type=F32)


def _cparams(*sem):
    return pltpu.CompilerParams(dimension_semantics=sem, vmem_limit_bytes=VMEM_LIMIT)


def _rmsnorm_fwd(x, gain, tm, name):
    s, d = x.shape

    def body(x_ref, g_ref, o_ref):
        xv = x_ref[...]
        r = lax.rsqrt(jnp.mean(xv * xv, axis=-1, keepdims=True) + EPS)
        o_ref[...] = ((xv * r) * g_ref[...]).astype(o_ref.dtype)

    return pl.pallas_call(
        body, name=name, grid=(s // tm,),
        in_specs=[pl.BlockSpec((tm, d), lambda i: (i, 0)), pl.BlockSpec((1, d), lambda i: (0, 0))],
        out_specs=pl.BlockSpec((tm, d), lambda i: (i, 0)),
        out_shape=jax.ShapeDtypeStruct((s, d), BF16),
        compiler_params=_cparams("parallel"),
    )(x, gain)


def _group_mean_matrix():
    r = lax.broadcasted_iota(jnp.int32, (LANES, LANES), 0) // HEAD_DIM
    c = lax.broadcasted_iota(jnp.int32, (LANES, LANES), 1) // HEAD_DIM
    return jnp.where(r == c, 1.0 / HEAD_DIM, 0.0).astype(BF16)


def _group_mean(v, gm):
    hi = v.astype(BF16)
    lo = (v - hi.astype(F32)).astype(BF16)
    return _dot(hi, gm, NN) + _dot(lo, gm, NN)


def _qknorm_fwd(proj, gains, tm, name):
    s = proj.shape[0]
    ncol = gains.shape[1] // LANES

    def body(p_ref, g_ref, gm_ref, o_ref):
        xv = p_ref[...].astype(F32)
        r = lax.rsqrt(_group_mean(xv * xv, gm_ref[...]) + EPS)
        o_ref[...] = ((xv * r) * g_ref[...]).astype(o_ref.dtype)

    blk = pl.BlockSpec((tm, LANES), lambda i, j: (i, j))
    return pl.pallas_call(
        body, name=name, grid=(s // tm, ncol),
        in_specs=[blk, pl.BlockSpec((1, LANES), lambda i, j: (0, j)),
                  pl.BlockSpec((LANES, LANES), lambda i, j: (0, 0))],
        out_specs=blk,
        out_shape=jax.ShapeDtypeStruct((s, ncol * LANES), BF16),
        compiler_params=_cparams("parallel", "parallel"),
    )(proj, gains, _group_mean_matrix())


def _qknorm_bwd(dq, dk, proj, gains, tm, name):
    s = proj.shape[0]
    ncol = gains.shape[1] // LANES
    half = ncol // 2
    nsteps = s // tm

    def body(dq_ref, dk_ref, p_ref, g_ref, gm_ref, dx_ref, dg_ref):
        i = pl.program_id(1)
        gm = gm_ref[...]
        xv = p_ref[...].astype(F32)
        r = lax.rsqrt(_group_mean(xv * xv, gm) + EPS)
        xhat = xv * r
        dy = jnp.where(pl.program_id(0) < half, dq_ref[...], dk_ref[...])
        dxh = dy * g_ref[...]
        proj_ = _group_mean(dxh * xhat, gm)
        dx_ref[...] = (r * (dxh - xhat * proj_)).astype(dx_ref.dtype)
        part = jnp.sum((dy * xhat).reshape(tm // 8, 8, LANES), axis=0)

        @pl.when(i == 0)
        def _():
            dg_ref[...] = part

        @pl.when(i > 0)
        def _():
            dg_ref[...] += part

        @pl.when(i == nsteps - 1)
        def _():
            dg_ref[...] = jnp.broadcast_to(jnp.sum(dg_ref[...], axis=0, keepdims=True), (8, LANES))

    blk = pl.BlockSpec((tm, LANES), lambda j, i: (i, j))
    return pl.pallas_call(
        body, name=name, grid=(ncol, nsteps),
        in_specs=[pl.BlockSpec((tm, LANES), lambda j, i: (i, jnp.minimum(j, half - 1))),
                  pl.BlockSpec((tm, LANES), lambda j, i: (i, jnp.maximum(j - half, 0))),
                  blk, pl.BlockSpec((1, LANES), lambda j, i: (0, j)),
                  pl.BlockSpec((LANES, LANES), lambda j, i: (0, 0))],
        out_specs=[blk, pl.BlockSpec((8, LANES), lambda j, i: (0, j))],
        out_shape=[jax.ShapeDtypeStruct((s, ncol * LANES), BF16),
                   jax.ShapeDtypeStruct((8, ncol * LANES), F32)],
        compiler_params=_cparams("parallel", "arbitrary"),
    )(dq, dk, proj, gains, _group_mean_matrix())


CONV_ROWS = 256
HALO = 8


def _conv_fwd(proj, conv_w8, name):
    s = proj.shape[0]
    nblk = conv_w8.shape[1] // LANES
    first = 3 * nblk
    nchunk = s // CONV_ROWS

    def body(cb_ref, cc_ref, cu_ref, w_ref, y_ref, hpad):
        hpad[pl.ds(0, 2 * HALO), :] = jnp.zeros((2 * HALO, LANES), F32)

        def fill(i, _):
            r0 = pl.multiple_of(i * CONV_ROWS, CONV_ROWS)
            hpad[pl.ds(r0 + 2 * HALO, CONV_ROWS), :] = (
                cc_ref[pl.ds(r0, CONV_ROWS), :].astype(F32) * cu_ref[pl.ds(r0, CONV_ROWS), :].astype(F32))
            return 0

        lax.fori_loop(0, nchunk, fill, 0)
        w0, w1, w2 = w_ref[0:1, :], w_ref[1:2, :], w_ref[2:3, :]

        def conv(i, _):
            r0 = pl.multiple_of(i * CONV_ROWS, CONV_ROWS)
            win = hpad[pl.ds(r0 + HALO, CONV_ROWS + HALO), :]
            c = (w2 * win[HALO:] + w1 * pltpu.roll(win, 1, 0)[HALO:] + w0 * pltpu.roll(win, 2, 0)[HALO:])
            y_ref[pl.ds(r0, CONV_ROWS), :] = (cb_ref[pl.ds(r0, CONV_ROWS), :].astype(F32) * c).astype(y_ref.dtype)
            return 0

        lax.fori_loop(0, nchunk, conv, 0)

    def col(off):
        return pl.BlockSpec((s, LANES), lambda j: (0, off + j))

    return pl.pallas_call(
        body, name=name, grid=(nblk,),
        in_specs=[col(first), col(first + nblk), col(first + 2 * nblk), pl.BlockSpec((8, LANES), lambda j: (0, j))],
        out_specs=pl.BlockSpec((s, LANES), lambda j: (0, j)),
        out_shape=jax.ShapeDtypeStruct((s, nblk * LANES), BF16),
        scratch_shapes=[pltpu.VMEM((s + 2 * HALO, LANES), F32)],
        compiler_params=_cparams("parallel"),
    )(proj, proj, proj, conv_w8)


def _conv_bwd(dmix, proj, conv_w8, name):
    s = proj.shape[0]
    nblk = conv_w8.shape[1] // LANES
    first = 3 * nblk
    nchunk = s // CONV_ROWS

    def body(dy_ref, cb_ref, cc_ref, cu_ref, w_ref, dcb_ref, dcc_ref, dcu_ref, dw_ref, hpad, dcpad):
        hpad[pl.ds(0, 2 * HALO), :] = jnp.zeros((2 * HALO, LANES), F32)
        dcpad[pl.ds(s, 2 * HALO), :] = jnp.zeros((2 * HALO, LANES), F32)

        def fill(i, _):
            r0 = pl.multiple_of(i * CONV_ROWS, CONV_ROWS)
            hpad[pl.ds(r0 + 2 * HALO, CONV_ROWS), :] = (
                cc_ref[pl.ds(r0, CONV_ROWS), :].astype(F32) * cu_ref[pl.ds(r0, CONV_ROWS), :].astype(F32))
            return 0

        lax.fori_loop(0, nchunk, fill, 0)
        w0, w1, w2 = w_ref[0:1, :], w_ref[1:2, :], w_ref[2:3, :]

        def fold(v):
            return jnp.sum(v.reshape(CONV_ROWS // 8, 8, LANES), axis=0)

        def first_pass(i, acc):
            a0, a1, a2 = acc
            r0 = pl.multiple_of(i * CONV_ROWS, CONV_ROWS)
            win = hpad[pl.ds(r0 + HALO, CONV_ROWS + HALO), :]
            h0 = win[HALO:]
            h1 = pltpu.roll(win, 1, 0)[HALO:]
            h2 = pltpu.roll(win, 2, 0)[HALO:]
            c = w2 * h0 + w1 * h1 + w0 * h2
            dy = dy_ref[pl.ds(r0, CONV_ROWS), :]
            dcb_ref[pl.ds(r0, CONV_ROWS), :] = (dy * c).astype(dcb_ref.dtype)
            dc = dy * cb_ref[pl.ds(r0, CONV_ROWS), :].astype(F32)
            dcpad[pl.ds(r0, CONV_ROWS), :] = dc
            return a0 + fold(dc * h2), a1 + fold(dc * h1), a2 + fold(dc * h0)

        z8 = jnp.zeros((8, LANES), F32)
        a0, a1, a2 = lax.fori_loop(0, nchunk, first_pass, (z8, z8, z8))
        dw_ref[...] = jnp.concatenate(
            [jnp.sum(a0, axis=0, keepdims=True), jnp.sum(a1, axis=0, keepdims=True),
             jnp.sum(a2, axis=0, keepdims=True), jnp.zeros((5, LANES), F32)], axis=0)

        def second_pass(i, _):
            r0 = pl.multiple_of(i * CONV_ROWS, CONV_ROWS)
            win = dcpad[pl.ds(r0, CONV_ROWS + HALO), :]
            n = CONV_ROWS + HALO
            dh = (w2 * win[:CONV_ROWS] + w1 * pltpu.roll(win, n - 1, 0)[:CONV_ROWS]
                  + w0 * pltpu.roll(win, n - 2, 0)[:CONV_ROWS])
            dcc_ref[pl.ds(r0, CONV_ROWS), :] = (dh * cu_ref[pl.ds(r0, CONV_ROWS), :].astype(F32)).astype(dcc_ref.dtype)
            dcu_ref[pl.ds(r0, CONV_ROWS), :] = (dh * cc_ref[pl.ds(r0, CONV_ROWS), :].astype(F32)).astype(dcu_ref.dtype)
            return 0

        lax.fori_loop(0, nchunk, second_pass, 0)

    def col(off):
        return pl.BlockSpec((s, LANES), lambda j: (0, off + j))

    out = pl.BlockSpec((s, LANES), lambda j: (0, j))
    return pl.pallas_call(
        body, name=name, grid=(nblk,),
        in_specs=[col(nblk), col(first), col(first + nblk), col(first + 2 * nblk),
                  pl.BlockSpec((8, LANES), lambda j: (0, j))],
        out_specs=[out, out, out, pl.BlockSpec((8, LANES), lambda j: (0, j))],
        out_shape=[jax.ShapeDtypeStruct((s, nblk * LANES), BF16)] * 3 + [jax.ShapeDtypeStruct((8, nblk * LANES), F32)],
        scratch_shapes=[pltpu.VMEM((s + 2 * HALO, LANES), F32), pltpu.VMEM((s + 2 * HALO, LANES), F32)],
        compiler_params=_cparams("parallel"),
    )(dmix, proj, proj, proj, conv_w8)


LOG2E = 1.4426950408889634
LN2 = 0.6931471805599453
NEG_BIG = -1e30
SATURATED = 160.0


def _cumsum_matrix(kind):
    j = lax.broadcasted_iota(jnp.int32, (KEY_CHUNK, 2 * KEY_CHUNK), 0)
    c = lax.broadcasted_iota(jnp.int32, (KEY_CHUNK, 2 * KEY_CHUNK), 1)
    tri = {"after": j > c, "upto": j <= c, "before": j < c}[kind]
    return jnp.where((c >= KEY_CHUNK) | tri, 1.0, 0.0).astype(BF16)


def _stack_heads(t, m0):
    zero = jnp.zeros_like(t)
    return jnp.concatenate([jnp.where(m0, t, zero), jnp.where(m0, zero, t)], axis=0)


def _softplus2(z):
    sp = jnp.maximum(z, 0.0) + jnp.log2(1.0 + jnp.exp2(-jnp.abs(z)))
    return sp, z - sp


def _key_chunk(ref, kc):
    return ref[pl.ds(pl.multiple_of(kc * KEY_CHUNK, KEY_CHUNK), KEY_CHUNK), :]


def _attn_bwd(qk, proj, dmix, rtot, used, tq, name, travel=()):
    s = qk.shape[0]
    nhp = qk.shape[1] // (2 * LANES)
    nc = tq // KEY_CHUNK
    nq = s // tq
    nt = len(travel)

    def body(used_ref, q_ref, k_ref, v_ref, do_ref, r_ref, cmi_ref, cme_ref, bias_ref, dq_ref, dk_ref, dv_ref,
             z_refs, ls_refs, sig_refs, sp_refs, gb_refs, pr_ref, gs_ref, copies):
        qi = pl.program_id(1)

        @pl.when(qi == 0)
        def _():
            dk_ref[...] = jnp.zeros_like(dk_ref)
            dv_ref[...] = jnp.zeros_like(dv_ref)

        if copies is not None:
            @pl.when(jnp.logical_and(pl.program_id(0) == 0, qi == 0))
            def _():
                _exchange_begin(copies)

        nslots = (qi + 1) * nc
        walked = used_ref[pl.program_id(0), qi].astype(jnp.int32)
        first = jnp.clip(nslots - walked, 0, nslots - nc) // nc * nc
        m0 = lax.broadcasted_iota(jnp.int32, (1, LANES), 1) < HEAD_DIM
        qs = _stack_heads(q_ref[...], m0)
        do = do_ref[...]
        dos = _stack_heads(do.astype(BF16), m0)
        dosl = _stack_heads((do * LN2).astype(BF16), m0)
        cmi = cmi_ref[...]
        cme = cme_ref[...]

        def chunk_at(i):
            return jnp.clip(i, first, nslots - 1)

        def scores(kc):
            return _dot(qs, _key_chunk(k_ref, kc), NT)

        def weights(ls, cs, da, pr, kc):
            a = jnp.exp2(ls - (pr - cs[:, :KEY_CHUNK]))
            gb = (a * da).astype(BF16)
            ks = pl.multiple_of(kc * KEY_CHUNK, KEY_CHUNK)
            dv_ref[pl.ds(ks, KEY_CHUNK), :] += _dot(a, dos, TN)
            return gb, jnp.exp2(ls), pr - cs[:, KEY_CHUNK:]

        def score_grads(gb, sig, cg, gs, dq, kc):
            dzb = (gb.astype(F32) * (1.0 - sig) - sig * (gs + cg[:, :KEY_CHUNK])).astype(BF16)
            ks = pl.multiple_of(kc * KEY_CHUNK, KEY_CHUNK)
            dk_ref[pl.ds(ks, KEY_CHUNK), :] += _dot(dzb, qs, TN)
            dq = dq + _dot(jnp.concatenate([dzb[:tq], dzb[tq:]], axis=1), _stack_heads(_key_chunk(k_ref, kc), m0), NN)
            return gs + cg[:, KEY_CHUNK:], dq

        def step(i, par, bias=None, stages="zswg"):
            cur, prv = par, 1 - par
            k1, k2 = chunk_at(i - 1), chunk_at(i - 2)
            z_next = scores(chunk_at(i + 1))
            if "w" in stages:
                cs = _dot(sp_refs[prv][...], cmi, NN)
                da = _dot(dosl, _key_chunk(v_ref, k1), NT)
            if "g" in stages:
                cg = _dot(gb_refs[cur][...], cme, NN)
            z = z_refs[cur][...]
            if bias is not None:
                z = z + bias
            sp, ls = _softplus2(z)
            sp_refs[cur][...] = sp.astype(BF16)
            ls_refs[cur][...] = ls
            if "g" in stages:
                gs, dq = score_grads(gb_refs[cur][...], sig_refs[cur][...], cg, gs_ref[...], dq_ref[...], k2)
                gs_ref[...] = gs
                dq_ref[...] = dq
            if "w" in stages:
                gb, sig, pr = weights(ls_refs[prv][...], cs, da, pr_ref[...], k1)
                gb_refs[prv][...] = gb
                sig_refs[prv][...] = sig
                pr_ref[...] = pr
            z_refs[prv][...] = z_next

        pr_ref[...] = jnp.concatenate([r_ref[:, :LANES], r_ref[:, LANES:]], axis=0)
        gs_ref[...] = jnp.zeros((2 * tq, LANES), F32)
        dq_ref[...] = jnp.zeros((tq, LANES), F32)
        z_refs[0][...] = scores(first)
        only_diagonal = first == nslots - nc
        step(first, 0, jnp.where(only_diagonal, bias_ref[0], 0.0), stages="zs")
        step(first + 1, 1, jnp.where(only_diagonal, bias_ref[1], 0.0), stages="zsw")

        def two_steps(j, _):
            step(2 * j, 0)
            step(2 * j + 1, 1)
            return 0

        lax.fori_loop(first // 2 + 1, nslots // 2 - 1, two_steps, 0)

        @pl.when(jnp.logical_not(only_diagonal))
        def _():
            step(nslots - 2, 0, bias_ref[0])
            step(nslots - 1, 1, bias_ref[1])

        k1, k2 = chunk_at(nslots - 1), chunk_at(nslots - 2)
        gb, sig, _ = weights(ls_refs[1][...], _dot(sp_refs[1][...], cmi, NN),
                             _dot(dosl, _key_chunk(v_ref, k1), NT), pr_ref[...], k1)
        gb2 = gb_refs[0][...]
        gs, dq = score_grads(gb2, sig_refs[0][...], _dot(gb2, cme, NN), gs_ref[...], dq_ref[...], k2)
        _, dq = score_grads(gb, sig, _dot(gb, cme, NN), gs, dq, k1)
        dq_ref[...] = dq

        if copies is not None:
            @pl.when(jnp.logical_and(pl.program_id(0) == nhp - 1, qi == nq - 1))
            def _():
                _exchange_finish(copies)

    def wrapped(*refs):
        ins, rest = refs[:9], refs[9:]
        srcs, rest = rest[:nt], rest[nt:]
        outs, rest = rest[:3], rest[3:]
        lands, rest = rest[:nt], rest[nt:]
        z0, z1, ls0, ls1, sg0, sg1, sp0, sp1, gb0, gb1, pr_ref, gs_ref = rest[:12]
        copies = _exchange_copies(srcs, lands, *rest[12:]) if nt else None
        body(*ins, *outs, (z0, z1), (ls0, ls1), (sg0, sg1), (sp0, sp1), (gb0, gb1), pr_ref, gs_ref, copies)

    assert nc == 2
    bias = _diag_bias(tq, True)
    bias = jnp.concatenate([bias[:, :, :KEY_CHUNK], bias[:, :, KEY_CHUNK:]], axis=1)
    qblk = pl.BlockSpec((tq, LANES), lambda p, i: (i, p))
    full = pl.BlockSpec((s, LANES), lambda p, i: (0, p))
    cmspec = pl.BlockSpec((KEY_CHUNK, 2 * KEY_CHUNK), lambda p, i: (0, 0))
    anyspec = pl.BlockSpec(memory_space=pl.ANY)
    shape = jax.ShapeDtypeStruct((s, nhp * LANES), F32)
    f32buf = pltpu.VMEM((2 * tq, LANES), F32)
    bf16buf = pltpu.VMEM((2 * tq, LANES), BF16)
    outs = pl.pallas_call(
        wrapped, name=name, grid=(nhp, nq),
        in_specs=[pl.BlockSpec(memory_space=pltpu.SMEM),
                  qblk,
                  pl.BlockSpec((s, LANES), lambda p, i: (0, nhp + p)),
                  pl.BlockSpec((s, LANES), lambda p, i: (0, 2 * nhp + p)),
                  qblk,
                  pl.BlockSpec((tq, 2 * LANES), lambda p, i: (i, p)),
                  cmspec, cmspec,
                  pl.BlockSpec((nc, 2 * tq, LANES), lambda p, i: (0, 0, 0))] + [anyspec] * nt,
        out_specs=[qblk, full, full] + [anyspec] * nt,
        out_shape=[shape, shape, shape] + [jax.ShapeDtypeStruct(t.shape, t.dtype) for t in travel],
        scratch_shapes=[f32buf] * 6 + [bf16buf] * 4 + [f32buf] * 2 + (_exchange_scratch(nt) if nt else []),
        compiler_params=_cparams("arbitrary", "arbitrary"),
    )(used, qk, qk, proj, dmix, rtot, _cumsum_matrix("upto"), _cumsum_matrix("before"), bias, *travel)
    return outs[0], outs[1], outs[2], list(outs[3:])


def _pair_cumsum_matrix(kind):
    j = lax.broadcasted_iota(jnp.int32, (2 * KEY_CHUNK, 4 * KEY_CHUNK), 0)
    c = lax.broadcasted_iota(jnp.int32, (2 * KEY_CHUNK, 4 * KEY_CHUNK), 1)
    same_head = (j // KEY_CHUNK) == ((c // KEY_CHUNK) % 2)
    jj, cc = j % KEY_CHUNK, c % KEY_CHUNK
    tri = {"after": jj > cc, "upto": jj <= cc, "before": jj < cc}[kind]
    return jnp.where(same_head & ((c >= 2 * KEY_CHUNK) | tri), 1.0, 0.0).astype(BF16)


def _diag_bias(tq, ascending):
    nc = tq // KEY_CHUNK
    shape = (nc, tq, 2 * KEY_CHUNK)
    d = lax.broadcasted_iota(jnp.int32, shape, 0)
    r = lax.broadcasted_iota(jnp.int32, shape, 1)
    c = lax.broadcasted_iota(jnp.int32, shape, 2) % KEY_CHUNK
    chunk = d if ascending else nc - 1 - d
    return jnp.where(chunk * KEY_CHUNK + c < r, 0.0, NEG_BIG).astype(F32)


def _attn_fwd(qk, proj, tq, name, shards=()):
    s = qk.shape[0]
    nhp = qk.shape[1] // (2 * LANES)
    nc = tq // KEY_CHUNK
    nq = s // tq
    ng = len(shards)
    assert nc == 2
    w = 2 * KEY_CHUNK

    def body(q_ref, k_ref, v_ref, cm_ref, bias_ref, o_ref, r_ref, used_ref, z_refs, ls_refs, cs_refs, ct_refs,
             sp_refs, ab_refs, acc_ref, gather):
        qi = pl.program_id(1)
        if gather is not None:
            @pl.when(jnp.logical_and(pl.program_id(0) == 0, qi == 0))
            def _():
                gather.begin()

        nslots = (qi + 1) * nc
        m0 = lax.broadcasted_iota(jnp.int32, (1, LANES), 1) < HEAD_DIM
        q = q_ref[...]
        cm = cm_ref[...]

        def chunk_at(i):
            return jnp.clip(nslots - 1 - i, 0, nslots - 1)

        def scores(kc):
            return _dot(q, _stack_heads(_key_chunk(k_ref, kc), m0), NT)

        def values(ab, kc):
            return _dot(ab, _stack_heads(_key_chunk(v_ref, kc), m0), NN)

        def step(i, par, bias=None, stages="zscwv"):
            cur, prv = par, 1 - par
            if "z" in stages:
                z_next = scores(chunk_at(i + 1))
            if "c" in stages:
                cs = _dot(sp_refs[prv][...], cm, NN)
            if "v" in stages:
                pv = values(ab_refs[prv][...], chunk_at(i - 3))
            if "w" in stages:
                rs = r_ref[...]
                r_ref[...] = rs + ct_refs[cur][...]
                ab_refs[cur][...] = jnp.exp2(ls_refs[cur][...] - cs_refs[cur][...] - rs).astype(BF16)
            if "s" in stages:
                z = z_refs[cur][...]
                if bias is not None:
                    z = z + bias
                sp, ls = _softplus2(z)
                sp_refs[cur][...] = sp.astype(BF16)
                ls_refs[cur][...] = ls
            if "v" in stages:
                acc_ref[...] += pv
            if "c" in stages:
                cs_refs[prv][...] = cs[:, :w]
                ct_refs[prv][...] = cs[:, w:]
            if "z" in stages:
                z_refs[prv][...] = z_next

        z_refs[0][...] = scores(chunk_at(0))
        ab_refs[1][...] = jnp.zeros((tq, w), BF16)
        r_ref[...] = jnp.zeros((tq, w), F32)
        acc_ref[...] = jnp.zeros((tq, LANES), F32)
        step(0, 0, bias_ref[0], stages="zs")
        step(1, 1, bias_ref[1], stages="zsc")

        def two_steps(carry):
            j, _ = carry
            step(2 * j, 0)
            step(2 * j + 1, 1)
            return j + 1, jnp.min(jnp.minimum(r_ref[:, :KEY_CHUNK], r_ref[:, KEY_CHUNK:]))

        pairs, low = lax.while_loop(lambda c: jnp.logical_and(c[0] < nslots // 2, c[1] < SATURATED), two_steps,
                                    (jnp.int32(1), jnp.float32(0.0)))
        entered = 2 * pairs
        saturated = low >= SATURATED

        @pl.when(saturated)
        def _():
            step(entered, 0, stages="v")

        @pl.when(jnp.logical_not(saturated))
        def _():
            step(entered, 0, stages="cwv")
            step(entered + 1, 1, stages="wv")
            step(entered + 2, 0, stages="v")

        o_ref[...] = acc_ref[...].astype(o_ref.dtype)
        used_ref[pl.program_id(0), qi] = jnp.where(saturated, entered - 2, entered).astype(F32)

        if gather is not None:
            @pl.when(jnp.logical_and(pl.program_id(0) == nhp - 1, qi == nq // 2))
            def _():
                gather.relay()

            @pl.when(jnp.logical_and(pl.program_id(0) == nhp - 1, qi == nq - 1))
            def _():
                gather.finish()

    def wrapped(*refs):
        ins, rest = refs[:5], refs[5:]
        srcs, rest = rest[:ng], rest[ng:]
        outs, rest = rest[:3], rest[3:]
        dsts, scratch = rest[:ng], rest[ng:]
        z, ls, cs, ct, sp, ab = [scratch[2 * j:2 * j + 2] for j in range(6)]
        gather = _Gather(srcs, dsts, *scratch[13:]) if ng else None
        body(*ins, *outs, z, ls, cs, ct, sp, ab, scratch[12], gather)

    f32buf = pltpu.VMEM((tq, w), F32)
    bf16buf = pltpu.VMEM((tq, w), BF16)
    anyspec = pl.BlockSpec(memory_space=pl.ANY)
    outs = pl.pallas_call(
        wrapped, name=name, grid=(nhp, nq),
        in_specs=[pl.BlockSpec((tq, LANES), lambda p, i: (i, p)),
                  pl.BlockSpec((s, LANES), lambda p, i: (0, nhp + p)),
                  pl.BlockSpec((s, LANES), lambda p, i: (0, 2 * nhp + p)),
                  pl.BlockSpec((w, 2 * w), lambda p, i: (0, 0)),
                  pl.BlockSpec((nc, tq, w), lambda p, i: (0, 0, 0))] + [anyspec] * ng,
        out_specs=[pl.BlockSpec((tq, LANES), lambda p, i: (i, p)),
                   pl.BlockSpec((tq, w), lambda p, i: (i, p)),
                   pl.BlockSpec(memory_space=pltpu.SMEM)] + [anyspec] * ng,
        out_shape=[jax.ShapeDtypeStruct((s, nhp * LANES), BF16),
                   jax.ShapeDtypeStruct((s, nhp * w), F32),
                   jax.ShapeDtypeStruct((nhp, nq), F32)] + _gathered_shapes(shards),
        scratch_shapes=([f32buf] * 8 + [bf16buf] * 4 + [pltpu.VMEM((tq, LANES), F32)]
                        + (_gather_scratch(ng) if ng else [])),
        compiler_params=_cparams("arbitrary", "arbitrary"),
    )(qk, qk, proj, _pair_cumsum_matrix("after"), _diag_bias(tq, False), *shards)
    return outs[0], outs[1], outs[2], list(outs[3:])


BLOCK_PAIR = 2


def _side_by_side(b_ref):
    return jnp.concatenate([b_ref[p] for p in range(BLOCK_PAIR)], axis=1)


def _mm_blocks(h, ga, widx, tm, name):
    s, d = h.shape
    nb, cols = ga.shape[1], ga.shape[3]

    def body(a_ref, b_ref, o_ref):
        o_ref[...] = _dot(a_ref[...], _side_by_side(b_ref), NN).astype(o_ref.dtype)

    return pl.pallas_call(
        body, name=name, grid=(s // tm, nb // BLOCK_PAIR),
        in_specs=[pl.BlockSpec((tm, d), lambda i, j: (i, 0)),
                  pl.BlockSpec((None, BLOCK_PAIR, d, cols), lambda i, j: (widx, j, 0, 0))],
        out_specs=pl.BlockSpec((tm, BLOCK_PAIR * cols), lambda i, j: (i, j)),
        out_shape=jax.ShapeDtypeStruct((s, nb * cols), BF16),
        compiler_params=_cparams("parallel", "arbitrary"),
    )(h, ga)


def _mm_swiglu(h, ga, gidx, uidx, tm, name):
    s, d = h.shape
    nb, cols = ga.shape[1], ga.shape[3]

    def body(a_ref, bg_ref, bu_ref, g_ref, u_ref, act_ref):
        a = a_ref[...]
        g = _dot(a, _side_by_side(bg_ref), NN)
        u = _dot(a, _side_by_side(bu_ref), NN)
        g_ref[...] = g.astype(g_ref.dtype)
        u_ref[...] = u.astype(u_ref.dtype)
        act_ref[...] = (g * (1.0 / (1.0 + jnp.exp(-g))) * u).astype(act_ref.dtype)

    def wspec(idx):
        return pl.BlockSpec((None, BLOCK_PAIR, d, cols), lambda i, j: (idx, j, 0, 0))

    out = pl.BlockSpec((tm, BLOCK_PAIR * cols), lambda i, j: (i, j))
    shape = jax.ShapeDtypeStruct((s, nb * cols), BF16)
    return pl.pallas_call(
        body, name=name, grid=(s // tm, nb // BLOCK_PAIR),
        in_specs=[pl.BlockSpec((tm, d), lambda i, j: (i, 0)), wspec(gidx), wspec(uidx)],
        out_specs=[out, out, out], out_shape=[shape, shape, shape],
        compiler_params=_cparams("parallel", "arbitrary"),
    )(h, ga, ga)


def _mm_residual_norm(a, w3, lidx, res, gain, tm, name):
    s, k = a.shape
    n = w3.shape[2]

    def body(a_ref, b_ref, r_ref, g_ref, o_ref, h_ref):
        xv = r_ref[...] + _dot(a_ref[...], b_ref[...], NN)
        o_ref[...] = xv
        r = lax.rsqrt(jnp.mean(xv * xv, axis=-1, keepdims=True) + EPS)
        h_ref[...] = ((xv * r) * g_ref[...]).astype(h_ref.dtype)

    row = pl.BlockSpec((tm, n), lambda i: (i, 0))
    return pl.pallas_call(
        body, name=name, grid=(s // tm,),
        in_specs=[pl.BlockSpec((tm, k), lambda i: (i, 0)),
                  pl.BlockSpec((None, k, n), lambda i: (lidx, 0, 0), pipeline_mode=pl.Buffered(1)),
                  row, pl.BlockSpec((1, n), lambda i: (0, 0))],
        out_specs=[row, row],
        out_shape=[jax.ShapeDtypeStruct((s, n), F32), jax.ShapeDtypeStruct((s, n), BF16)],
        compiler_params=_cparams("parallel"),
    )(a, w3, res, gain)


def _mm_residual_loss(a, w3, lidx, res, target, tm, name):
    s, k = a.shape
    n = w3.shape[2]
    nsteps = s // tm

    def body(a_ref, b_ref, r_ref, t_ref, dy_ref, dyb_ref, l_ref, acc):
        i = pl.program_id(0)
        diff = r_ref[...] + _dot(a_ref[...], b_ref[...], NN) - t_ref[...]
        dy_ref[...] = diff * (1.0 / n)
        dyb_ref[...] = (diff * (1.0 / n)).astype(dyb_ref.dtype)
        part = jnp.sum((diff * diff).reshape(tm // 8, 8, n), axis=0)

        @pl.when(i == 0)
        def _():
            acc[...] = part

        @pl.when(i > 0)
        def _():
            acc[...] += part

        @pl.when(i == nsteps - 1)
        def _():
            tot = jnp.sum(jnp.sum(acc[...], axis=1, keepdims=True), axis=0, keepdims=True)
            l_ref[...] = jnp.broadcast_to(tot * (0.5 / n), (8, LANES))

    row = pl.BlockSpec((tm, n), lambda i: (i, 0))
    return pl.pallas_call(
        body, name=name, grid=(nsteps,),
        in_specs=[pl.BlockSpec((tm, k), lambda i: (i, 0)),
                  pl.BlockSpec((None, k, n), lambda i: (lidx, 0, 0), pipeline_mode=pl.Buffered(1)),
                  row, row],
        out_specs=[row, row, pl.BlockSpec((8, LANES), lambda i: (0, 0))],
        out_shape=[jax.ShapeDtypeStruct((s, n), F32), jax.ShapeDtypeStruct((s, n), BF16),
                   jax.ShapeDtypeStruct((8, LANES), F32)],
        scratch_shapes=[pltpu.VMEM((8, n), F32)],
        compiler_params=_cparams("arbitrary"),
    )(a, w3, res, target)


def _mm_nt(a, w3, lidx, tm, tn, name):
    s, k = a.shape
    n = w3.shape[1]

    def body(a_ref, b_ref, o_ref):
        o_ref[...] = _dot(a_ref[...], b_ref[...], NT)

    return pl.pallas_call(
        body, name=name, grid=(s // tm, n // tn),
        in_specs=[pl.BlockSpec((tm, k), lambda i, j: (i, 0)),
                  pl.BlockSpec((None, tn, k), lambda i, j: (lidx, j, 0))],
        out_specs=pl.BlockSpec((tm, tn), lambda i, j: (i, j)),
        out_shape=jax.ShapeDtypeStruct((s, n), F32),
        compiler_params=_cparams("parallel", "arbitrary"),
    )(a, w3)


def _mm_nt_swiglu_bwd(dx, wd3, lidx, g, u, tm, name):
    s, d = dx.shape
    cols = BLOCK_PAIR * (g.shape[1] // N_DEV)

    def body(a_ref, b_ref, g_ref, u_ref, dg_ref, du_ref):
        dact = _dot(a_ref[...], b_ref[...], NT)
        gv = g_ref[...].astype(F32)
        sig = 0.5 * jnp.tanh(0.5 * gv) + 0.5
        silu = gv * sig
        du_ref[...] = (dact * silu).astype(du_ref.dtype)
        dsilu = sig + silu * (1.0 - sig)
        dg_ref[...] = (dact * u_ref[...].astype(F32) * dsilu).astype(dg_ref.dtype)

    blk = pl.BlockSpec((tm, cols), lambda i, j: (i, j))
    shape = jax.ShapeDtypeStruct(g.shape, BF16)
    return pl.pallas_call(
        body, name=name, grid=(s // tm, N_DEV // BLOCK_PAIR),
        in_specs=[pl.BlockSpec((tm, d), lambda i, j: (i, 0)),
                  pl.BlockSpec((None, cols, d), lambda i, j: (lidx, j, 0)), blk, blk],
        out_specs=[blk, blk], out_shape=[shape, shape],
        compiler_params=_cparams("parallel", "arbitrary"),
    )(dx, wd3, g, u)


def _mm_nt_norm_bwd(das, ga, widxs, x, gain, dres, tm, name, travel=()):
    s = das[0].shape[0]
    nb, d, cols = ga.shape[1], ga.shape[2], ga.shape[3]
    nw = len(das)
    nsteps = s // tm

    def body(*refs):
        a_refs, b_refs = refs[:nw], refs[nw:2 * nw]
        x_ref, g_ref, dres_ref, dx_ref, dxb_ref, dg_ref = refs[2 * nw:]
        i = pl.program_id(0)
        dhv = None
        wide = BLOCK_PAIR * cols
        for w in range(nw):
            for k in range(nb // BLOCK_PAIR):
                b = jnp.concatenate([b_refs[w][BLOCK_PAIR * k + p] for p in range(BLOCK_PAIR)], axis=1)
                part = _dot(a_refs[w][:, k * wide:(k + 1) * wide], b, NT)
                dhv = part if dhv is None else dhv + part
        xv = x_ref[...]
        r = lax.rsqrt(jnp.mean(xv * xv, axis=-1, keepdims=True) + EPS)
        xhat = xv * r
        dxh = dhv * g_ref[...]
        dxv = dres_ref[...] + r * (dxh - xhat * jnp.mean(dxh * xhat, axis=-1, keepdims=True))
        dx_ref[...] = dxv
        dxb_ref[...] = dxv.astype(dxb_ref.dtype)
        part = jnp.sum((dhv * xhat).reshape(tm // 8, 8, d), axis=0)

        @pl.when(i == 0)
        def _():
            dg_ref[...] = part

        @pl.when(i > 0)
        def _():
            dg_ref[...] += part

        @pl.when(i == nsteps - 1)
        def _():
            dg_ref[...] = jnp.broadcast_to(jnp.sum(dg_ref[...], axis=0, keepdims=True), (8, d))

    def wspec(idx):
        return pl.BlockSpec((None, nb, d, cols), lambda i: (idx, 0, 0, 0), pipeline_mode=pl.Buffered(1))

    row = pl.BlockSpec((tm, d), lambda i: (i, 0))
    body, more_in, more_out, more_shapes, more_scratch = _host_exchange(body, 2 * nw + 3, 3, travel, (nsteps,))
    outs = pl.pallas_call(
        body, name=name, grid=(nsteps,),
        in_specs=([pl.BlockSpec((tm, nb * cols), lambda i: (i, 0))] * nw + [wspec(i) for i in widxs]
                  + [row, pl.BlockSpec((1, d), lambda i: (0, 0)), row] + more_in),
        out_specs=[row, row, pl.BlockSpec((8, d), lambda i: (0, 0))] + more_out,
        out_shape=[jax.ShapeDtypeStruct((s, d), F32), jax.ShapeDtypeStruct((s, d), BF16),
                   jax.ShapeDtypeStruct((8, d), F32)] + more_shapes,
        scratch_shapes=more_scratch,
        compiler_params=_cparams("arbitrary"),
    )(*das, *([ga] * nw), x, gain, dres, *travel)
    return outs[0], outs[1], outs[2], list(outs[3:])


def _mm_tn(a, b, ta, tb, tk, out_blocks, name, travel=()):
    s, ka = a.shape
    nb = b.shape[1]
    nk = s // tk
    cols = tb
    if out_blocks:
        tb = BLOCK_PAIR * cols

    def body(a_ref, b_ref, o_ref, ob_ref):
        k = pl.program_id(2)
        part = _dot(a_ref[...], b_ref[...], TN)

        def put(first):
            if out_blocks:
                for p in range(BLOCK_PAIR):
                    piece = part[:, p * cols:(p + 1) * cols]
                    o_ref[p] = piece if first else o_ref[p] + piece
            else:
                o_ref[...] = part if first else o_ref[...] + part

        @pl.when(k == 0)
        def _():
            put(True)

        @pl.when(k > 0)
        def _():
            put(False)

        @pl.when(k == nk - 1)
        def _():
            ob_ref[...] = o_ref[...].astype(ob_ref.dtype)

    if out_blocks:
        out_spec = pl.BlockSpec((BLOCK_PAIR, ta, cols), lambda i, j, k: (j, i, 0))
        shape = (nb // cols, ka, cols)
    else:
        out_spec = pl.BlockSpec((ta, tb), lambda i, j, k: (i, j))
        shape = (ka, nb)
    grid = (ka // ta, nb // tb, nk)
    body, more_in, more_out, more_shapes, more_scratch = _host_exchange(body, 2, 2, travel, grid)
    outs = pl.pallas_call(
        body, name=name, grid=grid,
        in_specs=[pl.BlockSpec((tk, ta), lambda i, j, k: (k, i)),
                  pl.BlockSpec((tk, tb), lambda i, j, k: (k, j))] + more_in,
        out_specs=[out_spec, out_spec] + more_out,
        out_shape=[jax.ShapeDtypeStruct(shape, F32), jax.ShapeDtypeStruct(shape, BF16)] + more_shapes,
        scratch_shapes=more_scratch,
        compiler_params=_cparams("arbitrary", "arbitrary", "arbitrary"),
    )(a, b, *travel)
    return (outs[0], outs[1]), list(outs[2:])


def _adamw(g, w, m, v, name):
    rows, cols = g.shape
    c1 = 1.0 / (1.0 - ADAM_B1 ** ADAM_STEP)
    c2 = 1.0 / (1.0 - ADAM_B2 ** ADAM_STEP)

    def body(p_ref, w_ref, m_ref, v_ref, g_ref, d_ref, nm_ref, nv_ref):
        gv = p_ref[...]
        nm = ADAM_B1 * m_ref[...] + (1.0 - ADAM_B1) * gv
        nv = ADAM_B2 * v_ref[...] + (1.0 - ADAM_B2) * (gv * gv)
        g_ref[...] = gv
        nm_ref[...] = nm
        nv_ref[...] = nv
        d_ref[...] = -ADAM_LR * ((nm * c1) / (jnp.sqrt(nv * c2) + ADAM_EPS) + ADAM_WD * w_ref[...])

    blk = pl.BlockSpec((rows, cols), lambda i: (0, 0))
    shape = jax.ShapeDtypeStruct((rows, cols), F32)
    return pl.pallas_call(
        body, name=name, grid=(1,),
        in_specs=[blk] * 4, out_specs=[blk] * 4, out_shape=[shape] * 4,
        compiler_params=_cparams("arbitrary"),
    )(g, w, m, v)


def _adamw_sharded(parts, grads, my, w, m, v, tr, name):
    depth, rows, cols = w.shape
    p, pr, pc = parts[0].shape
    c1 = 1.0 / (1.0 - ADAM_B1 ** ADAM_STEP)
    c2 = 1.0 / (1.0 - ADAM_B2 ** ADAM_STEP)

    def body(my_ref, *refs):
        p_refs, own_refs = refs[:depth], refs[depth:2 * depth]
        w_ref, m_ref, v_ref, g_ref, d_ref, nm_ref, nv_ref = refs[2 * depth:]
        layer = pl.program_id(0)
        for ll in range(depth):
            @pl.when(layer == ll)
            def _(ll=ll):
                mine = own_refs[ll][...]
                g = jnp.where(my_ref[0] == 0, mine, p_refs[ll][0].astype(F32))
                for k in range(1, p):
                    g = g + jnp.where(my_ref[0] == k, mine, p_refs[ll][k].astype(F32))
                g = g[:, :cols]
                nm = ADAM_B1 * m_ref[...] + (1.0 - ADAM_B1) * g
                nv = ADAM_B2 * v_ref[...] + (1.0 - ADAM_B2) * (g * g)
                g_ref[...] = g
                nm_ref[...] = nm
                nv_ref[...] = nv
                d_ref[...] = -ADAM_LR * ((nm * c1) / (jnp.sqrt(nv * c2) + ADAM_EPS) + ADAM_WD * w_ref[...])

    def row_block(ll, l, i):
        return jnp.where(l == ll, i, 0)

    blk = pl.BlockSpec((None, tr, cols), lambda l, i, my_: (l, i, 0))
    shape = jax.ShapeDtypeStruct((depth, rows, cols), F32)
    return pl.pallas_call(
        body, name=name,
        grid_spec=pltpu.PrefetchScalarGridSpec(
            num_scalar_prefetch=1, grid=(depth, rows // tr),
            in_specs=([pl.BlockSpec((p, tr, pc), lambda l, i, my_, ll=ll: (0, row_block(ll, l, i), 0))
                       for ll in range(depth)]
                      + [pl.BlockSpec((None, tr, pc), lambda l, i, my_, ll=ll: (my_[0], row_block(ll, l, i), 0))
                         for ll in range(depth)]
                      + [blk, blk, blk]),
            out_specs=[blk] * 4),
        out_shape=[shape] * 4,
        compiler_params=_cparams("arbitrary", "arbitrary"),
    )(my, *parts, *grads, w, m, v)


def _place():
    x, y, c = lax.axis_index("x"), lax.axis_index("y"), lax.axis_index("c")
    return x, y, c


class _Gather:
    def __init__(self, srcs, dsts, send_sems, recv_sems, local_sems):
        na = len(srcs)
        x, y, c = _place()
        me, sibling = (x, y, c), (x, y, 1 - c)
        chips = [(1 - x, y), (x, 1 - y), (1 - x, 1 - y)]

        def slot(a, dev):
            return dsts[a].at[:, pl.ds(4 * dev[0] + 2 * dev[1] + dev[2], 1)]

        def copy(k, a, block, to, from_shard=False):
            return pltpu.make_async_remote_copy(
                src_ref=srcs[a] if from_shard else slot(a, block), dst_ref=slot(a, block),
                send_sem=send_sems.at[k, a], recv_sem=recv_sems.at[k, a], device_id=to, device_id_type=MESH)

        pairs = [(j, chip, a) for j, chip in enumerate(chips) for a in range(na)]
        self.mine = [pltpu.make_async_copy(srcs[a], slot(a, me), local_sems.at[a]) for a in range(na)]
        self.first = [copy(0, a, me, sibling, True) for a in range(na)]
        self.first += [copy(1 + j, a, me, (*chip, c), True) for j, chip, a in pairs]
        self.over_ici = [copy(1 + j, a, (*chip, c), me) for j, chip, a in pairs]
        self.passed = [copy(4 + j, a, (*chip, c), sibling) for j, chip, a in pairs]
        self.from_sibling = [copy(0, a, sibling, me) for a in range(na)]
        self.from_sibling += [copy(4 + j, a, (*chip, 1 - c), me) for j, chip, a in pairs]

    def begin(self):
        for cp in self.mine + self.first:
            cp.start()

    def relay(self):
        for arrived, onward in zip(self.over_ici, self.passed):
            arrived.wait_recv()
            onward.start()

    def finish(self):
        for cp in self.from_sibling:
            cp.wait_recv()
        for cp in self.first + self.passed:
            cp.wait_send()
        for cp in self.mine:
            cp.wait()


def _gather_scratch(na):
    return [pltpu.SemaphoreType.DMA((7, na)), pltpu.SemaphoreType.DMA((7, na)), pltpu.SemaphoreType.DMA((na,))]


def _gathered_shapes(shards):
    return [jax.ShapeDtypeStruct((a.shape[0], N_DEV) + a.shape[2:], a.dtype) for a in shards]


def _all_gather(shards, name):
    na = len(shards)

    def body(*refs):
        gather = _Gather(refs[:na], refs[na:2 * na], *refs[2 * na:])
        gather.begin()
        gather.relay()
        gather.finish()

    anyspec = pl.BlockSpec(memory_space=pl.ANY)
    return pl.pallas_call(
        body, name=name,
        in_specs=[anyspec] * na, out_specs=[anyspec] * na,
        out_shape=_gathered_shapes(shards), scratch_shapes=_gather_scratch(na),
    )(*shards)


_RELATIONS = [(dx, dy, dc) for dx in (0, 1) for dy in (0, 1) for dc in (0, 1)][1:]


def _flip(v, d):
    return 1 - v if d else v


def _exchange_copies(srcs, dsts, send_sems, recv_sems, local_sems):
    x, y, c = _place()
    my = 4 * x + 2 * y + c
    na = len(srcs)
    mine = [pltpu.make_async_copy(srcs[a].at[pl.ds(my, 1)], dsts[a].at[pl.ds(my, 1)], local_sems.at[a])
            for a in range(na)]
    sends, recvs = [], []
    for k, (dx, dy, dc) in enumerate(_RELATIONS):
        peer = (_flip(x, dx), _flip(y, dy), _flip(c, dc))
        pidx = 4 * peer[0] + 2 * peer[1] + peer[2]
        for a in range(na):
            for into, out in ((my, sends), (pidx, recvs)):
                out.append(pltpu.make_async_remote_copy(
                    src_ref=srcs[a].at[pl.ds(pidx, 1)], dst_ref=dsts[a].at[pl.ds(into, 1)],
                    send_sem=send_sems.at[k, a], recv_sem=recv_sems.at[k, a], device_id=peer, device_id_type=MESH))
    return mine, sends, recvs


def _exchange_begin(copies):
    mine, sends, _ = copies
    for cp in mine + sends:
        cp.start()


def _exchange_finish(copies):
    mine, sends, recvs = copies
    for cp in recvs:
        cp.wait_recv()
    for cp in sends:
        cp.wait_send()
    for cp in mine:
        cp.wait()


def _exchange_scratch(na):
    return [pltpu.SemaphoreType.DMA((7, na)), pltpu.SemaphoreType.DMA((7, na)), pltpu.SemaphoreType.DMA((na,))]


def _host_exchange(body, n_in, n_out, travel, grid):
    nt = len(travel)
    if not nt:
        return body, [], [], [], []

    def wrapped(*refs):
        ins, srcs = refs[:n_in], refs[n_in:n_in + nt]
        outs, rest = refs[n_in + nt:n_in + nt + n_out], refs[n_in + nt + n_out:]
        dsts, scratch = rest[:nt], rest[nt:]
        copies = _exchange_copies(srcs, dsts, *scratch[-3:])
        first = last = None
        for axis, size in enumerate(grid):
            at_start, at_end = pl.program_id(axis) == 0, pl.program_id(axis) == size - 1
            first = at_start if first is None else jnp.logical_and(first, at_start)
            last = at_end if last is None else jnp.logical_and(last, at_end)

        @pl.when(first)
        def _():
            _exchange_begin(copies)

        body(*ins, *outs, *scratch[:-3])

        @pl.when(last)
        def _():
            _exchange_finish(copies)

    anyspec = pl.BlockSpec(memory_space=pl.ANY)
    return (wrapped, [anyspec] * nt, [anyspec] * nt, [jax.ShapeDtypeStruct(t.shape, t.dtype) for t in travel],
            _exchange_scratch(nt))


def _all_reduce_small(v, name):
    r, c_ = v.shape

    def body(v_ref, o_ref, gath, send_sems, recv_sems):
        x, y, c = _place()
        my = 4 * x + 2 * y + c
        gath[my] = v_ref[...]
        sends = []
        for k, (dx, dy, dc) in enumerate(_RELATIONS):
            peer = (_flip(x, dx), _flip(y, dy), _flip(c, dc))
            cp = pltpu.make_async_remote_copy(
                src_ref=v_ref, dst_ref=gath.at[my], send_sem=send_sems.at[k], recv_sem=recv_sems.at[k],
                device_id=peer, device_id_type=MESH)
            cp.start()
            sends.append((cp, 4 * peer[0] + 2 * peer[1] + peer[2], k, peer))
        for cp, pidx, k, peer in sends:
            pltpu.make_async_remote_copy(
                src_ref=v_ref, dst_ref=gath.at[pidx], send_sem=send_sems.at[k], recv_sem=recv_sems.at[k],
                device_id=peer, device_id_type=MESH).wait_recv()
        for cp, *_ in sends:
            cp.wait_send()
        tot = gath[0]
        for k in range(1, N_DEV):
            tot = tot + gath[k]
        o_ref[...] = tot

    vm = pl.BlockSpec(memory_space=pltpu.VMEM)
    return pl.pallas_call(
        body, name=name, in_specs=[vm], out_specs=vm,
        out_shape=jax.ShapeDtypeStruct((r, c_), F32),
        scratch_shapes=[pltpu.VMEM((N_DEV, r, c_), F32), pltpu.SemaphoreType.DMA((7,)),
                        pltpu.SemaphoreType.DMA((7,))],
    )(v)


TM = 512
TM_MATMUL = 2048
TM_RESIDUAL = 1024
TQ = 256


def _device_blocks(t):
    return t.reshape(N_DEV, -1, t.shape[-1])


def _pad_to(a, axis, size):
    pad = [(0, 0)] * a.ndim
    pad[axis] = (0, size - a.shape[axis])
    return jnp.pad(a, pad)


def _local_step(x, target, g_in0, late_shards, conv_full, norm_mix, q_norm, k_norm, norm_ffn):
    depth, d = norm_mix.shape
    cols = g_in0.shape[3]
    tm, tq = min(TM, x.shape[0]), min(TQ, x.shape[0])
    tmm, tmr = min(TM_MATMUL, x.shape[0]), min(TM_RESIDUAL, x.shape[0])
    attn = d // 2
    nheads = attn // HEAD_DIM
    scale = HEAD_DIM ** -0.5 * LOG2E
    saved = []
    h1 = _rmsnorm_fwd(x, norm_mix[0][None], tm, "norm_mix_fwd_0")
    for l in range(depth):
        w_in = (g_in0, 0) if l == 0 else (g_rest, 3 * (l - 1))
        proj = _mm_blocks(h1, *w_in, tmm, f"proj_in_{l}")
        qk_gain = jnp.concatenate([jnp.tile(q_norm[l], nheads) * scale, jnp.tile(k_norm[l], nheads)])[None]
        qk = _qknorm_fwd(proj, qk_gain, tmm, f"qknorm_fwd_{l}")
        o, rtot, used, gathered = _attn_fwd(qk, proj, tq, f"attn_fwd_{l}", late_shards if l == 0 else ())
        if l == 0:
            g_gu0, g_rest, gb, gc = gathered if depth > 1 else (gathered[0], None, *gathered[1:])
            gb = gb.reshape(depth, -1, d)
            gc = gc.reshape(depth, -1, d)
        w_gu = (g_gu0, 0, 1) if l == 0 else (g_rest, 3 * (l - 1) + 1, 3 * (l - 1) + 2)
        conv_w8 = _pad_to(conv_full[l], 0, 8)
        cv = _conv_fwd(proj, conv_w8, f"conv_fwd_{l}")
        mix = jnp.concatenate([o, cv], axis=1)
        x1, h2 = _mm_residual_norm(mix, gb, l, x, norm_ffn[l][None], tmr, f"proj_out_{l}")
        g, u, act = _mm_swiglu(h2, *w_gu, tmr, f"ffn_up_{l}")
        saved.append((x, h1, proj, qk_gain, qk, rtot, used, conv_w8, mix, x1, h2, g, u, act, w_in, w_gu))
        if l + 1 < depth:
            x, h1 = _mm_residual_norm(act, gc, l, x1, norm_mix[l + 1][None], tm, f"ffn_down_{l}")
        else:
            dx, dxb, loss = _mm_residual_loss(act, gc, l, x1, target, tm, f"ffn_down_{l}")

    grads = [None] * depth
    small = [None] * depth
    landed = {}
    for l in reversed(range(depth)):
        x0, h1, proj, qk_gain, qk, rtot, used, conv_w8, mix, x1, h2, g, u, act, w_in, w_gu = saved[l]
        d = x0.shape[1]
        dg, du = _mm_nt_swiglu_bwd(dxb, gc, l, g, u, tmr, f"ffn_down_bwd_{l}")
        d_wdown, _ = _mm_tn(act, dxb, 768, d, tmm, False, f"dw_down_{l}")
        d_wgate, _ = _mm_tn(h2, dg, d, cols, tmm, True, f"dw_gate_{l}")
        d_wup, _ = _mm_tn(h2, du, d, cols, tmm, True, f"dw_up_{l}")
        dx1, dx1b, dg_ffn, _ = _mm_nt_norm_bwd([dg, du], w_gu[0], list(w_gu[1:]), x1, norm_ffn[l][None], dx, tm,
                                               f"ffn_up_bwd_{l}")
        dmix = _mm_nt(dx1b, gb, l, tmr, 512, f"proj_out_bwd_{l}")
        d_wout, _ = _mm_tn(mix, dx1b, 512, d, tmm, False, f"dw_out_{l}")
        dcb, dcc, dcu, dconv = _conv_bwd(dmix, proj, conv_w8, f"conv_bwd_{l}")
        rides = {}
        if l == 0:
            rides = {5 * ll + j: grads[ll][j][1] for ll in range(1, depth) for j in range(5)}
            rides.update({3: d_wout[1], 4: d_wdown[1]})
        dq, dk, dv, arrived = _attn_bwd(qk, proj, dmix, rtot, used, tq, f"attn_bwd_{l}",
                                        [_device_blocks(t) for t in rides.values()])
        landed.update(zip(rides.keys(), arrived))
        dqk, dg_qk = _qknorm_bwd(dq, dk, proj, qk_gain, tmm, f"qknorm_bwd_{l}")
        dproj = jnp.concatenate([dqk, dv.astype(BF16), dcb, dcc, dcu], axis=1)
        first = l == 0
        d_win, arrived = _mm_tn(h1, dproj, d, cols, tmm, True, f"dw_in_{l}",
                                [_device_blocks(d_wgate[1])] if first else [])
        landed.update(zip([1], arrived))
        dx, dxb, dg_mix, arrived = _mm_nt_norm_bwd(
            [dproj], w_in[0], [w_in[1]], x0, norm_mix[l][None], dx1, tm, f"proj_in_bwd_{l}",
            [_device_blocks(d_win[1]), _device_blocks(d_wup[1])] if first else [])
        landed.update(zip([0, 2], arrived))
        grads[l] = (d_win, d_wgate, d_wup, d_wout, d_wdown)
        dq_gain = jnp.sum(dg_qk[0, :attn].reshape(nheads, HEAD_DIM), axis=0) * scale
        dk_gain = jnp.sum(dg_qk[0, attn:].reshape(nheads, HEAD_DIM), axis=0)
        small[l] = (dg_mix[0], dg_ffn[0], dq_gain, dk_gain, dconv[:3])
    return loss, dx, grads, small, landed


def kernel(x, norm_mix, w_in, q_norm, k_norm, conv_w, w_out, norm_ffn, w_gate, w_up, w_down, loss_target, m_norm_mix, m_w_in, m_q_norm, m_k_norm, m_conv_w, m_w_out, m_norm_ffn, m_w_gate, m_w_up, m_w_down, v_norm_mix, v_w_in, v_q_norm, v_k_norm, v_conv_w, v_w_out, v_norm_ffn, v_w_gate, v_w_up, v_w_down):
    depth, d, in_shard = w_in.shape
    ff_shard = w_gate.shape[2]
    ff_pad = in_shard
    conv_shard = conv_w.shape[2]
    xs = x.reshape(x.shape[-2], d)
    target = loss_target.reshape(xs.shape)

    pa = jnp.stack([w_in, _pad_to(w_gate, 2, ff_pad), _pad_to(w_up, 2, ff_pad)], axis=1)
    pa = pa.reshape(3 * depth, 1, d, in_shard).astype(BF16)
    pd = _pad_to(_pad_to(conv_w.reshape(depth * 3, conv_shard), 0, 8), 1, LANES)[None, None]
    g_in0, gd = _all_gather([pa[:1], pd], "gather_first")
    conv_full = gd[0, :, :depth * 3, :conv_shard].transpose(1, 0, 2).reshape(depth, 3, N_DEV * conv_shard)
    late_shards = [pa[1:3]] + ([pa[3:]] if depth > 1 else [])
    late_shards += [w_out.astype(BF16)[:, None], _pad_to(w_down, 1, ff_pad).astype(BF16)[:, None]]

    loss, grad_x, grads, small, landed = _local_step(xs, target, g_in0, late_shards, conv_full, norm_mix, q_norm,
                                                     k_norm, norm_ffn)

    x_, y_, c_ = _place()
    my = 4 * x_ + 2 * y_ + c_

    rows = []
    for l in range(depth):
        g_mix, g_ffn, g_q, g_k, g_conv = small[l]
        qkrow = _pad_to(jnp.concatenate([g_q, g_k]), 0, d)
        rows += [g_mix[None], g_ffn[None], qkrow[None], _pad_to(g_conv, 1, d)]
    nrow = 6 * depth
    packed = jnp.concatenate(rows + [_pad_to(loss[:1], 1, d)], axis=0)
    packed = _pad_to(packed, 0, ((nrow + 1 + 7) // 8) * 8)
    summed = _all_reduce_small(packed, "reduce_small")
    loss_out = summed[nrow, 0]

    my1 = my.astype(jnp.int32).reshape(1)

    def big(j, w, m, v, tr, name):
        return _adamw_sharded([landed[5 * l + j] for l in range(depth)],
                              [_device_blocks(grads[l][j][0]) for l in range(depth)], my1, w, m, v, tr, name)

    res = {"w_in": big(0, w_in, m_w_in, v_w_in, 256, "adamw_in"),
           "w_gate": big(1, w_gate, m_w_gate, v_w_gate, 256, "adamw_gate"),
           "w_up": big(2, w_up, m_w_up, v_w_up, 256, "adamw_up"),
           "w_out": big(3, w_out, m_w_out, v_w_out, w_out.shape[1], "adamw_out"),
           "w_down": big(4, w_down, m_w_down, v_w_down, ff_shard // 2, "adamw_down")}

    g_rows, w_rows, m_rows, v_rows = [], [], [], []
    for l in range(depth):
        base = l * 6
        conv_g = lax.dynamic_slice(summed[base + 3:base + 6], (0, my * conv_shard), (3, conv_shard))
        g_rows += [summed[base:base + 3], _pad_to(conv_g, 1, d)]
        for dst, (nm, qn, kn, nf, cw) in ((w_rows, (norm_mix, q_norm, k_norm, norm_ffn, conv_w)),
                                          (m_rows, (m_norm_mix, m_q_norm, m_k_norm, m_norm_ffn, m_conv_w)),
                                          (v_rows, (v_norm_mix, v_q_norm, v_k_norm, v_norm_ffn, v_conv_w))):
            dst += [nm[l][None], nf[l][None], _pad_to(jnp.concatenate([qn[l], kn[l]]), 0, d)[None],
                    _pad_to(cw[l], 1, d)]
    prow = ((nrow + 7) // 8) * 8
    gs, ws, ms, vs = [_pad_to(jnp.concatenate(t, axis=0), 0, prow) for t in (g_rows, w_rows, m_rows, v_rows)]
    sm = _adamw(gs, ws, ms, vs, "adamw_small")

    hd = q_norm.shape[1]

    def small_out(t, kind):
        per_layer = []
        for l in range(depth):
            base = l * 6
            per_layer.append({"norm_mix": t[base], "norm_ffn": t[base + 1], "q_norm": t[base + 2, :hd],
                              "k_norm": t[base + 2, hd:2 * hd], "conv_w": t[base + 3:base + 6, :conv_shard]}[kind])
        return jnp.stack(per_layer)

    def big_out(name, i):
        return res[name][i]

    outs = [loss_out, grad_x.reshape(x.shape)]
    for i in range(4):
        outs += [small_out(sm[i], "norm_mix"), big_out("w_in", i), small_out(sm[i], "q_norm"),
                 small_out(sm[i], "k_norm"), small_out(sm[i], "conv_w"), big_out("w_out", i),
                 small_out(sm[i], "norm_ffn"), big_out("w_gate", i), big_out("w_up", i), big_out("w_down", i)]
    return tuple(outs)
```

```python
import jax
import jax.numpy as jnp
from jax import lax
from jax.experimental import pallas as pl
from jax.experimental.pallas import tpu as pltpu

F32 = jnp.float32
BF16 = jnp.bfloat16
MESH = pl.DeviceIdType.MESH

N_DEV = 8
LANES = 128
HEAD_DIM = 64
KEY_CHUNK = 128
EPS = 1e-6
VMEM_LIMIT = 48 * 1024 * 1024

ADAM_LR = 0.001
ADAM_B1 = 0.9
ADAM_B2 = 0.999
ADAM_EPS = 1e-08
ADAM_WD = 0.01
ADAM_STEP = 10

NN = (((1,), (0,)), ((), ()))
NT = (((1,), (1,)), ((), ()))
TN = (((0,), (0,)), ((), ()))


def _dot(a, b, dims):
    return lax.dot_general(a.astype(BF16), b.astype(BF16), dims, preferred_element_type=F32)


def _cparams(*sem):
    return pltpu.CompilerParams(dimension_semantics=sem, vmem_limit_bytes=VMEM_LIMIT)


def _rmsnorm_fwd(x, gain, tm, name):
    s, d = x.shape

    def body(x_ref, g_ref, o_ref):
        xv = x_ref[...]
        r = lax.rsqrt(jnp.mean(xv * xv, axis=-1, keepdims=True) + EPS)
        o_ref[...] = ((xv * r) * g_ref[...]).astype(o_ref.dtype)

    return pl.pallas_call(
        body, name=name, grid=(s // tm,),
        in_specs=[pl.BlockSpec((tm, d), lambda i: (i, 0)), pl.BlockSpec((1, d), lambda i: (0, 0))],
        out_specs=pl.BlockSpec((tm, d), lambda i: (i, 0)),
        out_shape=jax.ShapeDtypeStruct((s, d), BF16),
        compiler_params=_cparams("parallel"),
    )(x, gain)


def _group_mean_matrix():
    r = lax.broadcasted_iota(jnp.int32, (LANES, LANES), 0) // HEAD_DIM
    c = lax.broadcasted_iota(jnp.int32, (LANES, LANES), 1) // HEAD_DIM
    return jnp.where(r == c, 1.0 / HEAD_DIM, 0.0).astype(BF16)


def _group_mean(v, gm):
    hi = v.astype(BF16)
    lo = (v - hi.astype(F32)).astype(BF16)
    return _dot(hi, gm, NN) + _dot(lo, gm, NN)


def _qknorm_fwd(proj, gains, tm, name):
    s = proj.shape[0]
    ncol = gains.shape[1] // LANES

    def body(p_ref, g_ref, gm_ref, o_ref):
        xv = p_ref[...].astype(F32)
        r = lax.rsqrt(_group_mean(xv * xv, gm_ref[...]) + EPS)
        o_ref[...] = ((xv * r) * g_ref[...]).astype(o_ref.dtype)

    blk = pl.BlockSpec((tm, LANES), lambda i, j: (i, j))
    return pl.pallas_call(
        body, name=name, grid=(s // tm, ncol),
        in_specs=[blk, pl.BlockSpec((1, LANES), lambda i, j: (0, j)),
                  pl.BlockSpec((LANES, LANES), lambda i, j: (0, 0))],
        out_specs=blk,
        out_shape=jax.ShapeDtypeStruct((s, ncol * LANES), BF16),
        compiler_params=_cparams("parallel", "parallel"),
    )(proj, gains, _group_mean_matrix())


def _qknorm_bwd(dq, dk, proj, gains, tm, name):
    s = proj.shape[0]
    ncol = gains.shape[1] // LANES
    half = ncol // 2
    nsteps = s // tm

    def body(dq_ref, dk_ref, p_ref, g_ref, gm_ref, dx_ref, dg_ref):
        i = pl.program_id(1)
        gm = gm_ref[...]
        xv = p_ref[...].astype(F32)
        r = lax.rsqrt(_group_mean(xv * xv, gm) + EPS)
        xhat = xv * r
        dy = jnp.where(pl.program_id(0) < half, dq_ref[...], dk_ref[...])
        dxh = dy * g_ref[...]
        proj_ = _group_mean(dxh * xhat, gm)
        dx_ref[...] = (r * (dxh - xhat * proj_)).astype(dx_ref.dtype)
        part = jnp.sum((dy * xhat).reshape(tm // 8, 8, LANES), axis=0)

        @pl.when(i == 0)
        def _():
            dg_ref[...] = part

        @pl.when(i > 0)
        def _():
            dg_ref[...] += part

        @pl.when(i == nsteps - 1)
        def _():
            dg_ref[...] = jnp.broadcast_to(jnp.sum(dg_ref[...], axis=0, keepdims=True), (8, LANES))

    blk = pl.BlockSpec((tm, LANES), lambda j, i: (i, j))
    return pl.pallas_call(
        body, name=name, grid=(ncol, nsteps),
        in_specs=[pl.BlockSpec((tm, LANES), lambda j, i: (i, jnp.minimum(j, half - 1))),
                  pl.BlockSpec((tm, LANES), lambda j, i: (i, jnp.maximum(j - half, 0))),
                  blk, pl.BlockSpec((1, LANES), lambda j, i: (0, j)),
                  pl.BlockSpec((LANES, LANES), lambda j, i: (0, 0))],
        out_specs=[blk, pl.BlockSpec((8, LANES), lambda j, i: (0, j))],
        out_shape=[jax.ShapeDtypeStruct((s, ncol * LANES), BF16),
                   jax.ShapeDtypeStruct((8, ncol * LANES), F32)],
        compiler_params=_cparams("parallel", "arbitrary"),
    )(dq, dk, proj, gains, _group_mean_matrix())


CONV_ROWS = 256
HALO = 8


def _conv_fwd(proj, conv_w8, name):
    s = proj.shape[0]
    nblk = conv_w8.shape[1] // LANES
    first = 3 * nblk
    nchunk = s // CONV_ROWS

    def body(cb_ref, cc_ref, cu_ref, w_ref, y_ref, hpad):
        hpad[pl.ds(0, 2 * HALO), :] = jnp.zeros((2 * HALO, LANES), F32)

        def fill(i, _):
            r0 = pl.multiple_of(i * CONV_ROWS, CONV_ROWS)
            hpad[pl.ds(r0 + 2 * HALO, CONV_ROWS), :] = (
                cc_ref[pl.ds(r0, CONV_ROWS), :].astype(F32) * cu_ref[pl.ds(r0, CONV_ROWS), :].astype(F32))
            return 0

        lax.fori_loop(0, nchunk, fill, 0)
        w0, w1, w2 = w_ref[0:1, :], w_ref[1:2, :], w_ref[2:3, :]

        def conv(i, _):
            r0 = pl.multiple_of(i * CONV_ROWS, CONV_ROWS)
            win = hpad[pl.ds(r0 + HALO, CONV_ROWS + HALO), :]
            c = (w2 * win[HALO:] + w1 * pltpu.roll(win, 1, 0)[HALO:] + w0 * pltpu.roll(win, 2, 0)[HALO:])
            y_ref[pl.ds(r0, CONV_ROWS), :] = (cb_ref[pl.ds(r0, CONV_ROWS), :].astype(F32) * c).astype(y_ref.dtype)
            return 0

        lax.fori_loop(0, nchunk, conv, 0)

    def col(off):
        return pl.BlockSpec((s, LANES), lambda j: (0, off + j))

    return pl.pallas_call(
        body, name=name, grid=(nblk,),
        in_specs=[col(first), col(first + nblk), col(first + 2 * nblk), pl.BlockSpec((8, LANES), lambda j: (0, j))],
        out_specs=pl.BlockSpec((s, LANES), lambda j: (0, j)),
        out_shape=jax.ShapeDtypeStruct((s, nblk * LANES), BF16),
        scratch_shapes=[pltpu.VMEM((s + 2 * HALO, LANES), F32)],
        compiler_params=_cparams("parallel"),
    )(proj, proj, proj, conv_w8)


def _conv_bwd(dmix, proj, conv_w8, name):
    s = proj.shape[0]
    nblk = conv_w8.shape[1] // LANES
    first = 3 * nblk
    nchunk = s // CONV_ROWS

    def body(dy_ref, cb_ref, cc_ref, cu_ref, w_ref, dcb_ref, dcc_ref, dcu_ref, dw_ref, hpad, dcpad):
        hpad[pl.ds(0, 2 * HALO), :] = jnp.zeros((2 * HALO, LANES), F32)
        dcpad[pl.ds(s, 2 * HALO), :] = jnp.zeros((2 * HALO, LANES), F32)

        def fill(i, _):
            r0 = pl.multiple_of(i * CONV_ROWS, CONV_ROWS)
            hpad[pl.ds(r0 + 2 * HALO, CONV_ROWS), :] = (
                cc_ref[pl.ds(r0, CONV_ROWS), :].astype(F32) * cu_ref[pl.ds(r0, CONV_ROWS), :].astype(F32))
            return 0

        lax.fori_loop(0, nchunk, fill, 0)
        w0, w1, w2 = w_ref[0:1, :], w_ref[1:2, :], w_ref[2:3, :]

        def fold(v):
            return jnp.sum(v.reshape(CONV_ROWS // 8, 8, LANES), axis=0)

        def first_pass(i, acc):
            a0, a1, a2 = acc
            r0 = pl.multiple_of(i * CONV_ROWS, CONV_ROWS)
            win = hpad[pl.ds(r0 + HALO, CONV_ROWS + HALO), :]
            h0 = win[HALO:]
            h1 = pltpu.roll(win, 1, 0)[HALO:]
            h2 = pltpu.roll(win, 2, 0)[HALO:]
            c = w2 * h0 + w1 * h1 + w0 * h2
            dy = dy_ref[pl.ds(r0, CONV_ROWS), :]
            dcb_ref[pl.ds(r0, CONV_ROWS), :] = (dy * c).astype(dcb_ref.dtype)
            dc = dy * cb_ref[pl.ds(r0, CONV_ROWS), :].astype(F32)
            dcpad[pl.ds(r0, CONV_ROWS), :] = dc
            return a0 + fold(dc * h2), a1 + fold(dc * h1), a2 + fold(dc * h0)

        z8 = jnp.zeros((8, LANES), F32)
        a0, a1, a2 = lax.fori_loop(0, nchunk, first_pass, (z8, z8, z8))
        dw_ref[...] = jnp.concatenate(
            [jnp.sum(a0, axis=0, keepdims=True), jnp.sum(a1, axis=0, keepdims=True),
             jnp.sum(a2, axis=0, keepdims=True), jnp.zeros((5, LANES), F32)], axis=0)

        def second_pass(i, _):
            r0 = pl.multiple_of(i * CONV_ROWS, CONV_ROWS)
            win = dcpad[pl.ds(r0, CONV_ROWS + HALO), :]
            n = CONV_ROWS + HALO
            dh = (w2 * win[:CONV_ROWS] + w1 * pltpu.roll(win, n - 1, 0)[:CONV_ROWS]
                  + w0 * pltpu.roll(win, n - 2, 0)[:CONV_ROWS])
            dcc_ref[pl.ds(r0, CONV_ROWS), :] = (dh * cu_ref[pl.ds(r0, CONV_ROWS), :].astype(F32)).astype(dcc_ref.dtype)
            dcu_ref[pl.ds(r0, CONV_ROWS), :] = (dh * cc_ref[pl.ds(r0, CONV_ROWS), :].astype(F32)).astype(dcu_ref.dtype)
            return 0

        lax.fori_loop(0, nchunk, second_pass, 0)

    def col(off):
        return pl.BlockSpec((s, LANES), lambda j: (0, off + j))

    out = pl.BlockSpec((s, LANES), lambda j: (0, j))
    return pl.pallas_call(
        body, name=name, grid=(nblk,),
        in_specs=[col(nblk), col(first), col(first + nblk), col(first + 2 * nblk),
                  pl.BlockSpec((8, LANES), lambda j: (0, j))],
        out_specs=[out, out, out, pl.BlockSpec((8, LANES), lambda j: (0, j))],
        out_shape=[jax.ShapeDtypeStruct((s, nblk * LANES), BF16)] * 3 + [jax.ShapeDtypeStruct((8, nblk * LANES), F32)],
        scratch_shapes=[pltpu.VMEM((s + 2 * HALO, LANES), F32), pltpu.VMEM((s + 2 * HALO, LANES), F32)],
        compiler_params=_cparams("parallel"),
    )(dmix, proj, proj, proj, conv_w8)


LOG2E = 1.4426950408889634
LN2 = 0.6931471805599453
NEG_BIG = -1e30
SATURATED = 160.0


def _cumsum_matrix(kind):
    j = lax.broadcasted_iota(jnp.int32, (KEY_CHUNK, 2 * KEY_CHUNK), 0)
    c = lax.broadcasted_iota(jnp.int32, (KEY_CHUNK, 2 * KEY_CHUNK), 1)
    tri = {"after": j > c, "upto": j <= c, "before": j < c}[kind]
    return jnp.where((c >= KEY_CHUNK) | tri, 1.0, 0.0).astype(BF16)


def _stack_heads(t, m0):
    zero = jnp.zeros_like(t)
    return jnp.concatenate([jnp.where(m0, t, zero), jnp.where(m0, zero, t)], axis=0)


def _softplus2(z):
    sp = jnp.maximum(z, 0.0) + jnp.log2(1.0 + jnp.exp2(-jnp.abs(z)))
    return sp, z - sp


def _key_chunk(ref, kc):
    return ref[pl.ds(pl.multiple_of(kc * KEY_CHUNK, KEY_CHUNK), KEY_CHUNK), :]


def _attn_bwd(qk, proj, dmix, rtot, used, tq, name, travel=()):
    s = qk.shape[0]
    nhp = qk.shape[1] // (2 * LANES)
    nc = tq // KEY_CHUNK
    nq = s // tq
    nt = len(travel)

    def body(used_ref, q_ref, k_ref, v_ref, do_ref, r_ref, cmi_ref, cme_ref, bias_ref, dq_ref, dk_ref, dv_ref,
             z_refs, ls_refs, sig_refs, sp_refs, gb_refs, pr_ref, gs_ref, copies):
        qi = pl.program_id(1)

        @pl.when(qi == 0)
        def _():
            dk_ref[...] = jnp.zeros_like(dk_ref)
            dv_ref[...] = jnp.zeros_like(dv_ref)

        if copies is not None:
            @pl.when(jnp.logical_and(pl.program_id(0) == 0, qi == 0))
            def _():
                _exchange_begin(copies)

        nslots = (qi + 1) * nc
        walked = used_ref[pl.program_id(0), qi].astype(jnp.int32)
        first = jnp.clip(nslots - walked, 0, nslots - nc) // nc * nc
        m0 = lax.broadcasted_iota(jnp.int32, (1, LANES), 1) < HEAD_DIM
        qs = _stack_heads(q_ref[...], m0)
        do = do_ref[...]
        dos = _stack_heads(do.astype(BF16), m0)
        dosl = _stack_heads((do * LN2).astype(BF16), m0)
        cmi = cmi_ref[...]
        cme = cme_ref[...]

        def chunk_at(i):
            return jnp.clip(i, first, nslots - 1)

        def scores(kc):
            return _dot(qs, _key_chunk(k_ref, kc), NT)

        def weights(ls, cs, da, pr, kc):
            a = jnp.exp2(ls - (pr - cs[:, :KEY_CHUNK]))
            gb = (a * da).astype(BF16)
            ks = pl.multiple_of(kc * KEY_CHUNK, KEY_CHUNK)
            dv_ref[pl.ds(ks, KEY_CHUNK), :] += _dot(a, dos, TN)
            return gb, jnp.exp2(ls), pr - cs[:, KEY_CHUNK:]

        def score_grads(gb, sig, cg, gs, dq, kc):
            dzb = (gb.astype(F32) * (1.0 - sig) - sig * (gs + cg[:, :KEY_CHUNK])).astype(BF16)
            ks = pl.multiple_of(kc * KEY_CHUNK, KEY_CHUNK)
            dk_ref[pl.ds(ks, KEY_CHUNK), :] += _dot(dzb, qs, TN)
            dq = dq + _dot(jnp.concatenate([dzb[:tq], dzb[tq:]], axis=1), _stack_heads(_key_chunk(k_ref, kc), m0), NN)
            return gs + cg[:, KEY_CHUNK:], dq

        def step(i, par, bias=None, stages="zswg"):
            cur, prv = par, 1 - par
            k1, k2 = chunk_at(i - 1), chunk_at(i - 2)
            z_next = scores(chunk_at(i + 1))
            if "w" in stages:
                cs = _dot(sp_refs[prv][...], cmi, NN)
                da = _dot(dosl, _key_chunk(v_ref, k1), NT)
            if "g" in stages:
                cg = _dot(gb_refs[cur][...], cme, NN)
            z = z_refs[cur][...]
            if bias is not None:
                z = z + bias
            sp, ls = _softplus2(z)
            sp_refs[cur][...] = sp.astype(BF16)
            ls_refs[cur][...] = ls
            if "g" in stages:
                gs, dq = score_grads(gb_refs[cur][...], sig_refs[cur][...], cg, gs_ref[...], dq_ref[...], k2)
                gs_ref[...] = gs
                dq_ref[...] = dq
            if "w" in stages:
                gb, sig, pr = weights(ls_refs[prv][...], cs, da, pr_ref[...], k1)
                gb_refs[prv][...] = gb
                sig_refs[prv][...] = sig
                pr_ref[...] = pr
            z_refs[prv][...] = z_next

        pr_ref[...] = jnp.concatenate([r_ref[:, :LANES], r_ref[:, LANES:]], axis=0)
        gs_ref[...] = jnp.zeros((2 * tq, LANES), F32)
        dq_ref[...] = jnp.zeros((tq, LANES), F32)
        z_refs[0][...] = scores(first)
        only_diagonal = first == nslots - nc
        step(first, 0, jnp.where(only_diagonal, bias_ref[0], 0.0), stages="zs")
        step(first + 1, 1, jnp.where(only_diagonal, bias_ref[1], 0.0), stages="zsw")

        def two_steps(j, _):
            step(2 * j, 0)
            step(2 * j + 1, 1)
            return 0

        lax.fori_loop(first // 2 + 1, nslots // 2 - 1, two_steps, 0)

        @pl.when(jnp.logical_not(only_diagonal))
        def _():
            step(nslots - 2, 0, bias_ref[0])
            step(nslots - 1, 1, bias_ref[1])

        k1, k2 = chunk_at(nslots - 1), chunk_at(nslots - 2)
        gb, sig, _ = weights(ls_refs[1][...], _dot(sp_refs[1][...], cmi, NN),
                             _dot(dosl, _key_chunk(v_ref, k1), NT), pr_ref[...], k1)
        gb2 = gb_refs[0][...]
        gs, dq = score_grads(gb2, sig_refs[0][...], _dot(gb2, cme, NN), gs_ref[...], dq_ref[...], k2)
        _, dq = score_grads(gb, sig, _dot(gb, cme, NN), gs, dq, k1)
        dq_ref[...] = dq

        if copies is not None:
            @pl.when(jnp.logical_and(pl.program_id(0) == nhp - 1, qi == nq - 1))
            def _():
                _exchange_finish(copies)

    def wrapped(*refs):
        ins, rest = refs[:9], refs[9:]
        srcs, rest = rest[:nt], rest[nt:]
        outs, rest = rest[:3], rest[3:]
        lands, rest = rest[:nt], rest[nt:]
        z0, z1, ls0, ls1, sg0, sg1, sp0, sp1, gb0, gb1, pr_ref, gs_ref = rest[:12]
        copies = _exchange_copies(srcs, lands, *rest[12:]) if nt else None
        body(*ins, *outs, (z0, z1), (ls0, ls1), (sg0, sg1), (sp0, sp1), (gb0, gb1), pr_ref, gs_ref, copies)

    assert nc == 2
    bias = _diag_bias(tq, True)
    bias = jnp.concatenate([bias[:, :, :KEY_CHUNK], bias[:, :, KEY_CHUNK:]], axis=1)
    qblk = pl.BlockSpec((tq, LANES), lambda p, i: (i, p))
    full = pl.BlockSpec((s, LANES), lambda p, i: (0, p))
    cmspec = pl.BlockSpec((KEY_CHUNK, 2 * KEY_CHUNK), lambda p, i: (0, 0))
    anyspec = pl.BlockSpec(memory_space=pl.ANY)
    shape = jax.ShapeDtypeStruct((s, nhp * LANES), F32)
    f32buf = pltpu.VMEM((2 * tq, LANES), F32)
    bf16buf = pltpu.VMEM((2 * tq, LANES), BF16)
    outs = pl.pallas_call(
        wrapped, name=name, grid=(nhp, nq),
        in_specs=[pl.BlockSpec(memory_space=pltpu.SMEM),
                  qblk,
                  pl.BlockSpec((s, LANES), lambda p, i: (0, nhp + p)),
                  pl.BlockSpec((s, LANES), lambda p, i: (0, 2 * nhp + p)),
                  qblk,
                  pl.BlockSpec((tq, 2 * LANES), lambda p, i: (i, p)),
                  cmspec, cmspec,
                  pl.BlockSpec((nc, 2 * tq, LANES), lambda p, i: (0, 0, 0))] + [anyspec] * nt,
        out_specs=[qblk, full, full] + [anyspec] * nt,
        out_shape=[shape, shape, shape] + [jax.ShapeDtypeStruct(t.shape, t.dtype) for t in travel],
        scratch_shapes=[f32buf] * 6 + [bf16buf] * 4 + [f32buf] * 2 + (_exchange_scratch(nt) if nt else []),
        compiler_params=_cparams("arbitrary", "arbitrary"),
    )(used, qk, qk, proj, dmix, rtot, _cumsum_matrix("upto"), _cumsum_matrix("before"), bias, *travel)
    return outs[0], outs[1], outs[2], list(outs[3:])


def _pair_cumsum_matrix(kind):
    j = lax.broadcasted_iota(jnp.int32, (2 * KEY_CHUNK, 4 * KEY_CHUNK), 0)
    c = lax.broadcasted_iota(jnp.int32, (2 * KEY_CHUNK, 4 * KEY_CHUNK), 1)
    same_head = (j // KEY_CHUNK) == ((c // KEY_CHUNK) % 2)
    jj, cc = j % KEY_CHUNK, c % KEY_CHUNK
    tri = {"after": jj > cc, "upto": jj <= cc, "before": jj < cc}[kind]
    return jnp.where(same_head & ((c >= 2 * KEY_CHUNK) | tri), 1.0, 0.0).astype(BF16)


def _diag_bias(tq, ascending):
    nc = tq // KEY_CHUNK
    shape = (nc, tq, 2 * KEY_CHUNK)
    d = lax.broadcasted_iota(jnp.int32, shape, 0)
    r = lax.broadcasted_iota(jnp.int32, shape, 1)
    c = lax.broadcasted_iota(jnp.int32, shape, 2) % KEY_CHUNK
    chunk = d if ascending else nc - 1 - d
    return jnp.where(chunk * KEY_CHUNK + c < r, 0.0, NEG_BIG).astype(F32)


def _attn_fwd(qk, proj, tq, name, shards=()):
    s = qk.shape[0]
    nhp = qk.shape[1] // (2 * LANES)
    nc = tq // KEY_CHUNK
    nq = s // tq
    ng = len(shards)
    assert nc == 2
    w = 2 * KEY_CHUNK

    def body(q_ref, k_ref, v_ref, cm_ref, bias_ref, o_ref, r_ref, used_ref, z_refs, ls_refs, cs_refs, ct_refs,
             sp_refs, ab_refs, acc_ref, gather):
        qi = pl.program_id(1)
        if gather is not None:
            @pl.when(jnp.logical_and(pl.program_id(0) == 0, qi == 0))
            def _():
                gather.begin()

        nslots = (qi + 1) * nc
        m0 = lax.broadcasted_iota(jnp.int32, (1, LANES), 1) < HEAD_DIM
        q = q_ref[...]
        cm = cm_ref[...]

        def chunk_at(i):
            return jnp.clip(nslots - 1 - i, 0, nslots - 1)

        def scores(kc):
            return _dot(q, _stack_heads(_key_chunk(k_ref, kc), m0), NT)

        def values(ab, kc):
            return _dot(ab, _stack_heads(_key_chunk(v_ref, kc), m0), NN)

        def step(i, par, bias=None, stages="zscwv"):
            cur, prv = par, 1 - par
            if "z" in stages:
                z_next = scores(chunk_at(i + 1))
            if "c" in stages:
                cs = _dot(sp_refs[prv][...], cm, NN)
            if "v" in stages:
                pv = values(ab_refs[prv][...], chunk_at(i - 3))
            if "w" in stages:
                rs = r_ref[...]
                r_ref[...] = rs + ct_refs[cur][...]
                ab_refs[cur][...] = jnp.exp2(ls_refs[cur][...] - cs_refs[cur][...] - rs).astype(BF16)
            if "s" in stages:
                z = z_refs[cur][...]
                if bias is not None:
                    z = z + bias
                sp, ls = _softplus2(z)
                sp_refs[cur][...] = sp.astype(BF16)
                ls_refs[cur][...] = ls
            if "v" in stages:
                acc_ref[...] += pv
            if "c" in stages:
                cs_refs[prv][...] = cs[:, :w]
                ct_refs[prv][...] = cs[:, w:]
            if "z" in stages:
                z_refs[prv][...] = z_next

        z_refs[0][...] = scores(chunk_at(0))
        ab_refs[1][...] = jnp.zeros((tq, w), BF16)
        r_ref[...] = jnp.zeros((tq, w), F32)
        acc_ref[...] = jnp.zeros((tq, LANES), F32)
        step(0, 0, bias_ref[0], stages="zs")
        step(1, 1, bias_ref[1], stages="zsc")

        def two_steps(carry):
            j, _ = carry
            step(2 * j, 0)
            step(2 * j + 1, 1)
            return j + 1, jnp.min(jnp.minimum(r_ref[:, :KEY_CHUNK], r_ref[:, KEY_CHUNK:]))

        pairs, low = lax.while_loop(lambda c: jnp.logical_and(c[0] < nslots // 2, c[1] < SATURATED), two_steps,
                                    (jnp.int32(1), jnp.float32(0.0)))
        entered = 2 * pairs
        saturated = low >= SATURATED

        @pl.when(saturated)
        def _():
            step(entered, 0, stages="v")

        @pl.when(jnp.logical_not(saturated))
        def _():
            step(entered, 0, stages="cwv")
            step(entered + 1, 1, stages="wv")
            step(entered + 2, 0, stages="v")

        o_ref[...] = acc_ref[...].astype(o_ref.dtype)
        used_ref[pl.program_id(0), qi] = jnp.where(saturated, entered - 2, entered).astype(F32)

        if gather is not None:
            @pl.when(jnp.logical_and(pl.program_id(0) == nhp - 1, qi == nq // 2))
            def _():
                gather.relay()

            @pl.when(jnp.logical_and(pl.program_id(0) == nhp - 1, qi == nq - 1))
            def _():
                gather.finish()

    def wrapped(*refs):
        ins, rest = refs[:5], refs[5:]
        srcs, rest = rest[:ng], rest[ng:]
        outs, rest = rest[:3], rest[3:]
        dsts, scratch = rest[:ng], rest[ng:]
        z, ls, cs, ct, sp, ab = [scratch[2 * j:2 * j + 2] for j in range(6)]
        gather = _Gather(srcs, dsts, *scratch[13:]) if ng else None
        body(*ins, *outs, z, ls, cs, ct, sp, ab, scratch[12], gather)

    f32buf = pltpu.VMEM((tq, w), F32)
    bf16buf = pltpu.VMEM((tq, w), BF16)
    anyspec = pl.BlockSpec(memory_space=pl.ANY)
    outs = pl.pallas_call(
        wrapped, name=name, grid=(nhp, nq),
        in_specs=[pl.BlockSpec((tq, LANES), lambda p, i: (i, p)),
                  pl.BlockSpec((s, LANES), lambda p, i: (0, nhp + p)),
                  pl.BlockSpec((s, LANES), lambda p, i: (0, 2 * nhp + p)),
                  pl.BlockSpec((w, 2 * w), lambda p, i: (0, 0)),
                  pl.BlockSpec((nc, tq, w), lambda p, i: (0, 0, 0))] + [anyspec] * ng,
        out_specs=[pl.BlockSpec((tq, LANES), lambda p, i: (i, p)),
                   pl.BlockSpec((tq, w), lambda p, i: (i, p)),
                   pl.BlockSpec(memory_space=pltpu.SMEM)] + [anyspec] * ng,
        out_shape=[jax.ShapeDtypeStruct((s, nhp * LANES), BF16),
                   jax.ShapeDtypeStruct((s, nhp * w), F32),
                   jax.ShapeDtypeStruct((nhp, nq), F32)] + _gathered_shapes(shards),
        scratch_shapes=([f32buf] * 8 + [bf16buf] * 4 + [pltpu.VMEM((tq, LANES), F32)]
                        + (_gather_scratch(ng) if ng else [])),
        compiler_params=_cparams("arbitrary", "arbitrary"),
    )(qk, qk, proj, _pair_cumsum_matrix("after"), _diag_bias(tq, False), *shards)
    return outs[0], outs[1], outs[2], list(outs[3:])


BLOCK_PAIR = 2
MXU_WIDTH = 256


def _side_by_side(b_ref):
    return jnp.concatenate([b_ref[p] for p in range(BLOCK_PAIR)], axis=1)


def _mm_blocks(h, ga, widx, tm, name):
    s, d = h.shape
    nb, cols = ga.shape[1], ga.shape[3]

    def body(a_ref, b_ref, o_ref):
        o_ref[...] = _dot(a_ref[...], _side_by_side(b_ref), NN).astype(o_ref.dtype)

    return pl.pallas_call(
        body, name=name, grid=(s // tm, nb // BLOCK_PAIR),
        in_specs=[pl.BlockSpec((tm, d), lambda i, j: (i, 0)),
                  pl.BlockSpec((None, BLOCK_PAIR, d, cols), lambda i, j: (widx, j, 0, 0))],
        out_specs=pl.BlockSpec((tm, BLOCK_PAIR * cols), lambda i, j: (i, j)),
        out_shape=jax.ShapeDtypeStruct((s, nb * cols), BF16),
        compiler_params=_cparams("parallel", "arbitrary"),
    )(h, ga)


def _mm_swiglu(h, ga, gidx, uidx, tm, name):
    s, d = h.shape
    nb, cols = ga.shape[1], ga.shape[3]

    def body(a_ref, bg_ref, bu_ref, g_ref, u_ref, act_ref):
        a = a_ref[...]
        g = _dot(a, _side_by_side(bg_ref), NN)
        u = _dot(a, _side_by_side(bu_ref), NN)
        g_ref[...] = g.astype(g_ref.dtype)
        u_ref[...] = u.astype(u_ref.dtype)
        act_ref[...] = (g * (1.0 / (1.0 + jnp.exp(-g))) * u).astype(act_ref.dtype)

    def wspec(idx):
        return pl.BlockSpec((None, BLOCK_PAIR, d, cols), lambda i, j: (idx, j, 0, 0))

    out = pl.BlockSpec((tm, BLOCK_PAIR * cols), lambda i, j: (i, j))
    shape = jax.ShapeDtypeStruct((s, nb * cols), BF16)
    return pl.pallas_call(
        body, name=name, grid=(s // tm, nb // BLOCK_PAIR),
        in_specs=[pl.BlockSpec((tm, d), lambda i, j: (i, 0)), wspec(gidx), wspec(uidx)],
        out_specs=[out, out, out], out_shape=[shape, shape, shape],
        compiler_params=_cparams("parallel", "arbitrary"),
    )(h, ga, ga)


def _mm_residual_norm(a, w3, lidx, res, gain, tm, name):
    s, k = a.shape
    n = w3.shape[2]

    def body(a_ref, b_ref, r_ref, g_ref, o_ref, h_ref):
        xv = r_ref[...] + _dot(a_ref[...], b_ref[...], NN)
        o_ref[...] = xv
        r = lax.rsqrt(jnp.mean(xv * xv, axis=-1, keepdims=True) + EPS)
        h_ref[...] = ((xv * r) * g_ref[...]).astype(h_ref.dtype)

    row = pl.BlockSpec((tm, n), lambda i: (i, 0))
    return pl.pallas_call(
        body, name=name, grid=(s // tm,),
        in_specs=[pl.BlockSpec((tm, k), lambda i: (i, 0)),
                  pl.BlockSpec((None, k, n), lambda i: (lidx, 0, 0), pipeline_mode=pl.Buffered(1)),
                  row, pl.BlockSpec((1, n), lambda i: (0, 0))],
        out_specs=[row, row],
        out_shape=[jax.ShapeDtypeStruct((s, n), F32), jax.ShapeDtypeStruct((s, n), BF16)],
        compiler_params=_cparams("parallel"),
    )(a, w3, res, gain)


def _mm_residual_loss(a, w3, lidx, res, target, tm, name):
    s, k = a.shape
    n = w3.shape[2]
    nsteps = s // tm

    def body(a_ref, b_ref, r_ref, t_ref, dy_ref, dyb_ref, l_ref, acc):
        i = pl.program_id(0)
        diff = r_ref[...] + _dot(a_ref[...], b_ref[...], NN) - t_ref[...]
        dy_ref[...] = diff * (1.0 / n)
        dyb_ref[...] = (diff * (1.0 / n)).astype(dyb_ref.dtype)
        part = jnp.sum((diff * diff).reshape(tm // 8, 8, n), axis=0)

        @pl.when(i == 0)
        def _():
            acc[...] = part

        @pl.when(i > 0)
        def _():
            acc[...] += part

        @pl.when(i == nsteps - 1)
        def _():
            tot = jnp.sum(jnp.sum(acc[...], axis=1, keepdims=True), axis=0, keepdims=True)
            l_ref[...] = jnp.broadcast_to(tot * (0.5 / n), (8, LANES))

    row = pl.BlockSpec((tm, n), lambda i: (i, 0))
    return pl.pallas_call(
        body, name=name, grid=(nsteps,),
        in_specs=[pl.BlockSpec((tm, k), lambda i: (i, 0)),
                  pl.BlockSpec((None, k, n), lambda i: (lidx, 0, 0), pipeline_mode=pl.Buffered(1)),
                  row, row],
        out_specs=[row, row, pl.BlockSpec((8, LANES), lambda i: (0, 0))],
        out_shape=[jax.ShapeDtypeStruct((s, n), F32), jax.ShapeDtypeStruct((s, n), BF16),
                   jax.ShapeDtypeStruct((8, LANES), F32)],
        scratch_shapes=[pltpu.VMEM((8, n), F32)],
        compiler_params=_cparams("arbitrary"),
    )(a, w3, res, target)


def _mm_nt(a, w3, lidx, tm, tn, name):
    s, k = a.shape
    n = w3.shape[1]

    def body(a_ref, b_ref, o_ref):
        o_ref[...] = _dot(a_ref[...], b_ref[...], NT)

    return pl.pallas_call(
        body, name=name, grid=(s // tm, n // tn),
        in_specs=[pl.BlockSpec((tm, k), lambda i, j: (i, 0)),
                  pl.BlockSpec((None, tn, k), lambda i, j: (lidx, j, 0))],
        out_specs=pl.BlockSpec((tm, tn), lambda i, j: (i, j)),
        out_shape=jax.ShapeDtypeStruct((s, n), F32),
        compiler_params=_cparams("parallel", "arbitrary"),
    )(a, w3)


def _mm_nt_swiglu_bwd(dx, wd3, lidx, g, u, tm, name, travel=()):
    s, d = dx.shape
    cols = BLOCK_PAIR * (g.shape[1] // N_DEV)

    def body(a_ref, b_ref, g_ref, u_ref, dg_ref, du_ref):
        a = a_ref[...]
        for c0 in range(0, cols, MXU_WIDTH):
            sl = slice(c0, c0 + MXU_WIDTH)
            dact = _dot(a, b_ref[sl, :], NT)
            gv = g_ref[:, sl].astype(F32)
            sig = 0.5 * jnp.tanh(0.5 * gv) + 0.5
            silu = gv * sig
            du_ref[:, sl] = (dact * silu).astype(du_ref.dtype)
            dsilu = sig + silu * (1.0 - sig)
            dg_ref[:, sl] = (dact * u_ref[:, sl].astype(F32) * dsilu).astype(dg_ref.dtype)

    blk = pl.BlockSpec((tm, cols), lambda i, j: (i, j))
    shape = jax.ShapeDtypeStruct(g.shape, BF16)
    grid = (s // tm, N_DEV // BLOCK_PAIR)
    body, more_in, more_out, more_shapes, more_scratch = _host_exchange(body, 4, 2, travel, grid)
    outs = pl.pallas_call(
        body, name=name, grid=grid,
        in_specs=[pl.BlockSpec((tm, d), lambda i, j: (i, 0)),
                  pl.BlockSpec((None, cols, d), lambda i, j: (lidx, j, 0)), blk, blk] + more_in,
        out_specs=[blk, blk] + more_out, out_shape=[shape, shape] + more_shapes,
        scratch_shapes=more_scratch,
        compiler_params=_cparams("arbitrary", "arbitrary"),
    )(dx, wd3, g, u, *travel)
    return outs[0], outs[1], list(outs[2:])


def _mm_nt_norm_bwd(das, ga, widxs, x, gain, dres, tm, name, travel=()):
    s = das[0].shape[0]
    nb, d, cols = ga.shape[1], ga.shape[2], ga.shape[3]
    nw = len(das)
    nsteps = s // tm

    def body(*refs):
        a_refs, b_refs = refs[:nw], refs[nw:2 * nw]
        x_ref, g_ref, dres_ref, dx_ref, dxb_ref, dg_ref = refs[2 * nw:]
        i = pl.program_id(0)
        dhv = None
        wide = BLOCK_PAIR * cols
        for w in range(nw):
            for k in range(nb // BLOCK_PAIR):
                b = jnp.concatenate([b_refs[w][BLOCK_PAIR * k + p] for p in range(BLOCK_PAIR)], axis=1)
                part = _dot(a_refs[w][:, k * wide:(k + 1) * wide], b, NT)
                dhv = part if dhv is None else dhv + part
        xv = x_ref[...]
        r = lax.rsqrt(jnp.mean(xv * xv, axis=-1, keepdims=True) + EPS)
        xhat = xv * r
        dxh = dhv * g_ref[...]
        dxv = dres_ref[...] + r * (dxh - xhat * jnp.mean(dxh * xhat, axis=-1, keepdims=True))
        dx_ref[...] = dxv
        dxb_ref[...] = dxv.astype(dxb_ref.dtype)
        part = jnp.sum((dhv * xhat).reshape(tm // 8, 8, d), axis=0)

        @pl.when(i == 0)
        def _():
            dg_ref[...] = part

        @pl.when(i > 0)
        def _():
            dg_ref[...] += part

        @pl.when(i == nsteps - 1)
        def _():
            dg_ref[...] = jnp.broadcast_to(jnp.sum(dg_ref[...], axis=0, keepdims=True), (8, d))

    def wspec(idx):
        return pl.BlockSpec((None, nb, d, cols), lambda i: (idx, 0, 0, 0), pipeline_mode=pl.Buffered(1))

    row = pl.BlockSpec((tm, d), lambda i: (i, 0))
    body, more_in, more_out, more_shapes, more_scratch = _host_exchange(body, 2 * nw + 3, 3, travel, (nsteps,))
    outs = pl.pallas_call(
        body, name=name, grid=(nsteps,),
        in_specs=([pl.BlockSpec((tm, nb * cols), lambda i: (i, 0))] * nw + [wspec(i) for i in widxs]
                  + [row, pl.BlockSpec((1, d), lambda i: (0, 0)), row] + more_in),
        out_specs=[row, row, pl.BlockSpec((8, d), lambda i: (0, 0))] + more_out,
        out_shape=[jax.ShapeDtypeStruct((s, d), F32), jax.ShapeDtypeStruct((s, d), BF16),
                   jax.ShapeDtypeStruct((8, d), F32)] + more_shapes,
        scratch_shapes=more_scratch,
        compiler_params=_cparams("arbitrary"),
    )(*das, *([ga] * nw), x, gain, dres, *travel)
    return outs[0], outs[1], outs[2], list(outs[3:])


def _mm_tn(a, b, ta, tb, tk, out_blocks, name, travel=()):
    s, ka = a.shape
    nb = b.shape[1]
    nk = s // tk
    cols = tb
    if out_blocks:
        tb = BLOCK_PAIR * cols

    def body(a_ref, b_ref, o_ref, ob_ref):
        k = pl.program_id(2)
        part = _dot(a_ref[...], b_ref[...], TN)

        def put(first):
            if out_blocks:
                for p in range(BLOCK_PAIR):
                    piece = part[:, p * cols:(p + 1) * cols]
                    o_ref[p] = piece if first else o_ref[p] + piece
            else:
                o_ref[...] = part if first else o_ref[...] + part

        @pl.when(k == 0)
        def _():
            put(True)

        @pl.when(k > 0)
        def _():
            put(False)

        @pl.when(k == nk - 1)
        def _():
            ob_ref[...] = o_ref[...].astype(ob_ref.dtype)

    if out_blocks:
        out_spec = pl.BlockSpec((BLOCK_PAIR, ta, cols), lambda i, j, k: (j, i, 0))
        shape = (nb // cols, ka, cols)
    else:
        out_spec = pl.BlockSpec((ta, tb), lambda i, j, k: (i, j))
        shape = (ka, nb)
    grid = (ka // ta, nb // tb, nk)
    body, more_in, more_out, more_shapes, more_scratch = _host_exchange(body, 2, 2, travel, grid)
    outs = pl.pallas_call(
        body, name=name, grid=grid,
        in_specs=[pl.BlockSpec((tk, ta), lambda i, j, k: (k, i)),
                  pl.BlockSpec((tk, tb), lambda i, j, k: (k, j))] + more_in,
        out_specs=[out_spec, out_spec] + more_out,
        out_shape=[jax.ShapeDtypeStruct(shape, F32), jax.ShapeDtypeStruct(shape, BF16)] + more_shapes,
        scratch_shapes=more_scratch,
        compiler_params=_cparams("arbitrary", "arbitrary", "arbitrary"),
    )(a, b, *travel)
    return (outs[0], outs[1]), list(outs[2:])


def _adamw(g, w, m, v, name):
    rows, cols = g.shape
    c1 = 1.0 / (1.0 - ADAM_B1 ** ADAM_STEP)
    c2 = 1.0 / (1.0 - ADAM_B2 ** ADAM_STEP)

    def body(p_ref, w_ref, m_ref, v_ref, g_ref, d_ref, nm_ref, nv_ref):
        gv = p_ref[...]
        nm = ADAM_B1 * m_ref[...] + (1.0 - ADAM_B1) * gv
        nv = ADAM_B2 * v_ref[...] + (1.0 - ADAM_B2) * (gv * gv)
        g_ref[...] = gv
        nm_ref[...] = nm
        nv_ref[...] = nv
        d_ref[...] = -ADAM_LR * ((nm * c1) / (jnp.sqrt(nv * c2) + ADAM_EPS) + ADAM_WD * w_ref[...])

    blk = pl.BlockSpec((rows, cols), lambda i: (0, 0))
    shape = jax.ShapeDtypeStruct((rows, cols), F32)
    return pl.pallas_call(
        body, name=name, grid=(1,),
        in_specs=[blk] * 4, out_specs=[blk] * 4, out_shape=[shape] * 4,
        compiler_params=_cparams("arbitrary"),
    )(g, w, m, v)


def _adamw_sharded(parts, grads, my, w, m, v, tr, name):
    depth, rows, cols = w.shape
    p, pr, pc = parts[0].shape
    c1 = 1.0 / (1.0 - ADAM_B1 ** ADAM_STEP)
    c2 = 1.0 / (1.0 - ADAM_B2 ** ADAM_STEP)

    def body(my_ref, *refs):
        p_refs, own_refs = refs[:depth], refs[depth:2 * depth]
        w_ref, m_ref, v_ref, g_ref, d_ref, nm_ref, nv_ref = refs[2 * depth:]
        layer = pl.program_id(0)
        for ll in range(depth):
            @pl.when(layer == ll)
            def _(ll=ll):
                mine = own_refs[ll][...]
                g = jnp.where(my_ref[0] == 0, mine, p_refs[ll][0].astype(F32))
                for k in range(1, p):
                    g = g + jnp.where(my_ref[0] == k, mine, p_refs[ll][k].astype(F32))
                g = g[:, :cols]
                nm = ADAM_B1 * m_ref[...] + (1.0 - ADAM_B1) * g
                nv = ADAM_B2 * v_ref[...] + (1.0 - ADAM_B2) * (g * g)
                g_ref[...] = g
                nm_ref[...] = nm
                nv_ref[...] = nv
                d_ref[...] = -ADAM_LR * ((nm * c1) / (jnp.sqrt(nv * c2) + ADAM_EPS) + ADAM_WD * w_ref[...])

    def row_block(ll, l, i):
        return jnp.where(l == ll, i, 0)

    blk = pl.BlockSpec((None, tr, cols), lambda l, i, my_: (l, i, 0))
    shape = jax.ShapeDtypeStruct((depth, rows, cols), F32)
    return pl.pallas_call(
        body, name=name,
        grid_spec=pltpu.PrefetchScalarGridSpec(
            num_scalar_prefetch=1, grid=(depth, rows // tr),
            in_specs=([pl.BlockSpec((p, tr, pc), lambda l, i, my_, ll=ll: (0, row_block(ll, l, i), 0))
                       for ll in range(depth)]
                      + [pl.BlockSpec((None, tr, pc), lambda l, i, my_, ll=ll: (my_[0], row_block(ll, l, i), 0))
                         for ll in range(depth)]
                      + [blk, blk, blk]),
            out_specs=[blk] * 4),
        out_shape=[shape] * 4,
        compiler_params=_cparams("arbitrary", "arbitrary"),
    )(my, *parts, *grads, w, m, v)


def _place():
    x, y, c = lax.axis_index("x"), lax.axis_index("y"), lax.axis_index("c")
    return x, y, c


class _Gather:
    def __init__(self, srcs, dsts, send_sems, recv_sems, local_sems):
        na = len(srcs)
        x, y, c = _place()
        me, sibling = (x, y, c), (x, y, 1 - c)
        chips = [(1 - x, y), (x, 1 - y), (1 - x, 1 - y)]

        def slot(a, dev):
            return dsts[a].at[:, pl.ds(4 * dev[0] + 2 * dev[1] + dev[2], 1)]

        def copy(k, a, block, to, from_shard=False):
            return pltpu.make_async_remote_copy(
                src_ref=srcs[a] if from_shard else slot(a, block), dst_ref=slot(a, block),
                send_sem=send_sems.at[k, a], recv_sem=recv_sems.at[k, a], device_id=to, device_id_type=MESH)

        pairs = [(j, chip, a) for j, chip in enumerate(chips) for a in range(na)]
        self.mine = [pltpu.make_async_copy(srcs[a], slot(a, me), local_sems.at[a]) for a in range(na)]
        self.first = [copy(0, a, me, sibling, True) for a in range(na)]
        self.first += [copy(1 + j, a, me, (*chip, c), True) for j, chip, a in pairs]
        self.over_ici = [copy(1 + j, a, (*chip, c), me) for j, chip, a in pairs]
        self.passed = [copy(4 + j, a, (*chip, c), sibling) for j, chip, a in pairs]
        self.from_sibling = [copy(0, a, sibling, me) for a in range(na)]
        self.from_sibling += [copy(4 + j, a, (*chip, 1 - c), me) for j, chip, a in pairs]

    def begin(self):
        for cp in self.mine + self.first:
            cp.start()

    def relay(self):
        for arrived, onward in zip(self.over_ici, self.passed):
            arrived.wait_recv()
            onward.start()

    def finish(self):
        for cp in self.from_sibling:
            cp.wait_recv()
        for cp in self.first + self.passed:
            cp.wait_send()
        for cp in self.mine:
            cp.wait()


def _gather_scratch(na):
    return [pltpu.SemaphoreType.DMA((7, na)), pltpu.SemaphoreType.DMA((7, na)), pltpu.SemaphoreType.DMA((na,))]


def _gathered_shapes(shards):
    return [jax.ShapeDtypeStruct((a.shape[0], N_DEV) + a.shape[2:], a.dtype) for a in shards]


def _all_gather(shards, name):
    na = len(shards)

    def body(*refs):
        gather = _Gather(refs[:na], refs[na:2 * na], *refs[2 * na:])
        gather.begin()
        gather.relay()
        gather.finish()

    anyspec = pl.BlockSpec(memory_space=pl.ANY)
    return pl.pallas_call(
        body, name=name,
        in_specs=[anyspec] * na, out_specs=[anyspec] * na,
        out_shape=_gathered_shapes(shards), scratch_shapes=_gather_scratch(na),
    )(*shards)


_RELATIONS = [(dx, dy, dc) for dx in (0, 1) for dy in (0, 1) for dc in (0, 1)][1:]


def _flip(v, d):
    return 1 - v if d else v


def _exchange_copies(srcs, dsts, send_sems, recv_sems, local_sems):
    x, y, c = _place()
    my = 4 * x + 2 * y + c
    na = len(srcs)
    mine = [pltpu.make_async_copy(srcs[a].at[pl.ds(my, 1)], dsts[a].at[pl.ds(my, 1)], local_sems.at[a])
            for a in range(na)]
    sends, recvs = [], []
    for k, (dx, dy, dc) in enumerate(_RELATIONS):
        peer = (_flip(x, dx), _flip(y, dy), _flip(c, dc))
        pidx = 4 * peer[0] + 2 * peer[1] + peer[2]
        for a in range(na):
            for into, out in ((my, sends), (pidx, recvs)):
                out.append(pltpu.make_async_remote_copy(
                    src_ref=srcs[a].at[pl.ds(pidx, 1)], dst_ref=dsts[a].at[pl.ds(into, 1)],
                    send_sem=send_sems.at[k, a], recv_sem=recv_sems.at[k, a], device_id=peer, device_id_type=MESH))
    return mine, sends, recvs


def _exchange_begin(copies):
    mine, sends, _ = copies
    for cp in mine + sends:
        cp.start()


def _exchange_finish(copies):
    mine, sends, recvs = copies
    for cp in recvs:
        cp.wait_recv()
    for cp in sends:
        cp.wait_send()
    for cp in mine:
        cp.wait()


def _exchange_scratch(na):
    return [pltpu.SemaphoreType.DMA((7, na)), pltpu.SemaphoreType.DMA((7, na)), pltpu.SemaphoreType.DMA((na,))]


def _host_exchange(body, n_in, n_out, travel, grid):
    nt = len(travel)
    if not nt:
        return body, [], [], [], []

    def wrapped(*refs):
        ins, srcs = refs[:n_in], refs[n_in:n_in + nt]
        outs, rest = refs[n_in + nt:n_in + nt + n_out], refs[n_in + nt + n_out:]
        dsts, scratch = rest[:nt], rest[nt:]
        copies = _exchange_copies(srcs, dsts, *scratch[-3:])
        first = last = None
        for axis, size in enumerate(grid):
            at_start, at_end = pl.program_id(axis) == 0, pl.program_id(axis) == size - 1
            first = at_start if first is None else jnp.logical_and(first, at_start)
            last = at_end if last is None else jnp.logical_and(last, at_end)

        @pl.when(first)
        def _():
            _exchange_begin(copies)

        body(*ins, *outs, *scratch[:-3])

        @pl.when(last)
        def _():
            _exchange_finish(copies)

    anyspec = pl.BlockSpec(memory_space=pl.ANY)
    return (wrapped, [anyspec] * nt, [anyspec] * nt, [jax.ShapeDtypeStruct(t.shape, t.dtype) for t in travel],
            _exchange_scratch(nt))


def _all_reduce_small(v, name):
    r, c_ = v.shape

    def body(v_ref, o_ref, gath, send_sems, recv_sems):
        x, y, c = _place()
        my = 4 * x + 2 * y + c
        gath[my] = v_ref[...]
        sends = []
        for k, (dx, dy, dc) in enumerate(_RELATIONS):
            peer = (_flip(x, dx), _flip(y, dy), _flip(c, dc))
            cp = pltpu.make_async_remote_copy(
                src_ref=v_ref, dst_ref=gath.at[my], send_sem=send_sems.at[k], recv_sem=recv_sems.at[k],
                device_id=peer, device_id_type=MESH)
            cp.start()
            sends.append((cp, 4 * peer[0] + 2 * peer[1] + peer[2], k, peer))
        for cp, pidx, k, peer in sends:
            pltpu.make_async_remote_copy(
                src_ref=v_ref, dst_ref=gath.at[pidx], send_sem=send_sems.at[k], recv_sem=recv_sems.at[k],
                device_id=peer, device_id_type=MESH).wait_recv()
        for cp, *_ in sends:
            cp.wait_send()
        tot = gath[0]
        for k in range(1, N_DEV):
            tot = tot + gath[k]
        o_ref[...] = tot

    vm = pl.BlockSpec(memory_space=pltpu.VMEM)
    return pl.pallas_call(
        body, name=name, in_specs=[vm], out_specs=vm,
        out_shape=jax.ShapeDtypeStruct((r, c_), F32),
        scratch_shapes=[pltpu.VMEM((N_DEV, r, c_), F32), pltpu.SemaphoreType.DMA((7,)),
                        pltpu.SemaphoreType.DMA((7,))],
    )(v)


TM = 512
TM_MATMUL = 2048
TM_RESIDUAL = 1024
TQ = 256


def _device_blocks(t):
    return t.reshape(N_DEV, -1, t.shape[-1])


def _pad_to(a, axis, size):
    pad = [(0, 0)] * a.ndim
    pad[axis] = (0, size - a.shape[axis])
    return jnp.pad(a, pad)


def _local_step(x, target, g_in0, late_shards, conv_full, norm_mix, q_norm, k_norm, norm_ffn):
    depth, d = norm_mix.shape
    cols = g_in0.shape[3]
    tm, tq = min(TM, x.shape[0]), min(TQ, x.shape[0])
    tmm, tmr = min(TM_MATMUL, x.shape[0]), min(TM_RESIDUAL, x.shape[0])
    attn = d // 2
    nheads = attn // HEAD_DIM
    scale = HEAD_DIM ** -0.5 * LOG2E
    saved = []
    h1 = _rmsnorm_fwd(x, norm_mix[0][None], tm, "norm_mix_fwd_0")
    for l in range(depth):
        w_in = (g_in0, 0) if l == 0 else (g_rest, 3 * (l - 1))
        proj = _mm_blocks(h1, *w_in, tmm, f"proj_in_{l}")
        qk_gain = jnp.concatenate([jnp.tile(q_norm[l], nheads) * scale, jnp.tile(k_norm[l], nheads)])[None]
        qk = _qknorm_fwd(proj, qk_gain, tmm, f"qknorm_fwd_{l}")
        o, rtot, used, gathered = _attn_fwd(qk, proj, tq, f"attn_fwd_{l}", late_shards if l == 0 else ())
        if l == 0:
            g_gu0, g_rest, gb, gc = gathered if depth > 1 else (gathered[0], None, *gathered[1:])
            gb = gb.reshape(depth, -1, d)
            gc = gc.reshape(depth, -1, d)
        w_gu = (g_gu0, 0, 1) if l == 0 else (g_rest, 3 * (l - 1) + 1, 3 * (l - 1) + 2)
        conv_w8 = _pad_to(conv_full[l], 0, 8)
        cv = _conv_fwd(proj, conv_w8, f"conv_fwd_{l}")
        mix = jnp.concatenate([o, cv], axis=1)
        x1, h2 = _mm_residual_norm(mix, gb, l, x, norm_ffn[l][None], tmr, f"proj_out_{l}")
        g, u, act = _mm_swiglu(h2, *w_gu, tmr, f"ffn_up_{l}")
        saved.append((x, h1, proj, qk_gain, qk, rtot, used, conv_w8, mix, x1, h2, g, u, act, w_in, w_gu))
        if l + 1 < depth:
            x, h1 = _mm_residual_norm(act, gc, l, x1, norm_mix[l + 1][None], tm, f"ffn_down_{l}")
        else:
            dx, dxb, loss = _mm_residual_loss(act, gc, l, x1, target, tm, f"ffn_down_{l}")

    grads = [None] * depth
    small = [None] * depth
    landed = {}
    for l in reversed(range(depth)):
        x0, h1, proj, qk_gain, qk, rtot, used, conv_w8, mix, x1, h2, g, u, act, w_in, w_gu = saved[l]
        d = x0.shape[1]
        late = [[], [], [], []]
        if l == 0:
            for n, i in enumerate(5 * ll + j for ll in range(1, depth) for j in (0, 1, 2, 4, 3)):
                late[n % 4].append(i)

        def ride(host):
            return late[host], [_device_blocks(grads[i // 5][i % 5][1]) for i in late[host]]

        idx, travel = ride(0)
        dg, du, arrived = _mm_nt_swiglu_bwd(dxb, gc, l, g, u, tmr, f"ffn_down_bwd_{l}", travel)
        landed.update(zip(idx, arrived))
        idx, travel = ride(1)
        d_wdown, arrived = _mm_tn(act, dxb, 768, d, tmm, False, f"dw_down_{l}", travel)
        landed.update(zip(idx, arrived))
        idx, travel = ride(2)
        d_wgate, arrived = _mm_tn(h2, dg, d, cols, tmm, True, f"dw_gate_{l}", travel)
        landed.update(zip(idx, arrived))
        idx, travel = ride(3)
        d_wup, arrived = _mm_tn(h2, du, d, cols, tmm, True, f"dw_up_{l}", travel)
        landed.update(zip(idx, arrived))
        dx1, dx1b, dg_ffn, _ = _mm_nt_norm_bwd([dg, du], w_gu[0], list(w_gu[1:]), x1, norm_ffn[l][None], dx, tm,
                                               f"ffn_up_bwd_{l}")
        dmix = _mm_nt(dx1b, gb, l, tmr, 512, f"proj_out_bwd_{l}")
        d_wout, _ = _mm_tn(mix, dx1b, 512, d, tmm, False, f"dw_out_{l}")
        dcb, dcc, dcu, dconv = _conv_bwd(dmix, proj, conv_w8, f"conv_bwd_{l}")
        rides = {1: d_wgate[1], 2: d_wup[1], 3: d_wout[1], 4: d_wdown[1]} if l == 0 else {}
        dq, dk, dv, arrived = _attn_bwd(qk, proj, dmix, rtot, used, tq, f"attn_bwd_{l}",
                                        [_device_blocks(t) for t in rides.values()])
        landed.update(zip(rides.keys(), arrived))
        dqk, dg_qk = _qknorm_bwd(dq, dk, proj, qk_gain, tmm, f"qknorm_bwd_{l}")
        dproj = jnp.concatenate([dqk, dv.astype(BF16), dcb, dcc, dcu], axis=1)
        d_win, _ = _mm_tn(h1, dproj, d, cols, tmm, True, f"dw_in_{l}")
        dx, dxb, dg_mix, arrived = _mm_nt_norm_bwd(
            [dproj], w_in[0], [w_in[1]], x0, norm_mix[l][None], dx1, tm, f"proj_in_bwd_{l}",
            [_device_blocks(d_win[1])] if l == 0 else [])
        landed.update(zip([0], arrived))
        grads[l] = (d_win, d_wgate, d_wup, d_wout, d_wdown)
        dq_gain = jnp.sum(dg_qk[0, :attn].reshape(nheads, HEAD_DIM), axis=0) * scale
        dk_gain = jnp.sum(dg_qk[0, attn:].reshape(nheads, HEAD_DIM), axis=0)
        small[l] = (dg_mix[0], dg_ffn[0], dq_gain, dk_gain, dconv[:3])
    return loss, dx, grads, small, landed


def kernel(x, norm_mix, w_in, q_norm, k_norm, conv_w, w_out, norm_ffn, w_gate, w_up, w_down, loss_target, m_norm_mix, m_w_in, m_q_norm, m_k_norm, m_conv_w, m_w_out, m_norm_ffn, m_w_gate, m_w_up, m_w_down, v_norm_mix, v_w_in, v_q_norm, v_k_norm, v_conv_w, v_w_out, v_norm_ffn, v_w_gate, v_w_up, v_w_down):
    depth, d, in_shard = w_in.shape
    ff_shard = w_gate.shape[2]
    ff_pad = in_shard
    conv_shard = conv_w.shape[2]
    xs = x.reshape(x.shape[-2], d)
    target = loss_target.reshape(xs.shape)

    pa = jnp.stack([w_in, _pad_to(w_gate, 2, ff_pad), _pad_to(w_up, 2, ff_pad)], axis=1)
    pa = pa.reshape(3 * depth, 1, d, in_shard).astype(BF16)
    pd = _pad_to(_pad_to(conv_w.reshape(depth * 3, conv_shard), 0, 8), 1, LANES)[None, None]
    g_in0, gd = _all_gather([pa[:1], pd], "gather_first")
    conv_full = gd[0, :, :depth * 3, :conv_shard].transpose(1, 0, 2).reshape(depth, 3, N_DEV * conv_shard)
    late_shards = [pa[1:3]] + ([pa[3:]] if depth > 1 else [])
    late_shards += [w_out.astype(BF16)[:, None], _pad_to(w_down, 1, ff_pad).astype(BF16)[:, None]]

    loss, grad_x, grads, small, landed = _local_step(xs, target, g_in0, late_shards, conv_full, norm_mix, q_norm,
                                                     k_norm, norm_ffn)

    x_, y_, c_ = _place()
    my = 4 * x_ + 2 * y_ + c_

    rows = []
    for l in range(depth):
        g_mix, g_ffn, g_q, g_k, g_conv = small[l]
        qkrow = _pad_to(jnp.concatenate([g_q, g_k]), 0, d)
        rows += [g_mix[None], g_ffn[None], qkrow[None], _pad_to(g_conv, 1, d)]
    nrow = 6 * depth
    packed = jnp.concatenate(rows + [_pad_to(loss[:1], 1, d)], axis=0)
    packed = _pad_to(packed, 0, ((nrow + 1 + 7) // 8) * 8)
    summed = _all_reduce_small(packed, "reduce_small")
    loss_out = summed[nrow, 0]

    my1 = my.astype(jnp.int32).reshape(1)

    def big(j, w, m, v, tr, name):
        return _adamw_sharded([landed[5 * l + j] for l in range(depth)],
                              [_device_blocks(grads[l][j][0]) for l in range(depth)], my1, w, m, v, tr, name)

    res = {"w_in": big(0, w_in, m_w_in, v_w_in, 256, "adamw_in"),
           "w_gate": big(1, w_gate, m_w_gate, v_w_gate, 256, "adamw_gate"),
           "w_up": big(2, w_up, m_w_up, v_w_up, 256, "adamw_up"),
           "w_out": big(3, w_out, m_w_out, v_w_out, w_out.shape[1], "adamw_out"),
           "w_down": big(4, w_down, m_w_down, v_w_down, ff_shard // 2, "adamw_down")}

    g_rows, w_rows, m_rows, v_rows = [], [], [], []
    for l in range(depth):
        base = l * 6
        conv_g = lax.dynamic_slice(summed[base + 3:base + 6], (0, my * conv_shard), (3, conv_shard))
        g_rows += [summed[base:base + 3], _pad_to(conv_g, 1, d)]
        for dst, (nm, qn, kn, nf, cw) in ((w_rows, (norm_mix, q_norm, k_norm, norm_ffn, conv_w)),
                                          (m_rows, (m_norm_mix, m_q_norm, m_k_norm, m_norm_ffn, m_conv_w)),
                                          (v_rows, (v_norm_mix, v_q_norm, v_k_norm, v_norm_ffn, v_conv_w))):
            dst += [nm[l][None], nf[l][None], _pad_to(jnp.concatenate([qn[l], kn[l]]), 0, d)[None],
                    _pad_to(cw[l], 1, d)]
    prow = ((nrow + 7) // 8) * 8
    gs, ws, ms, vs = [_pad_to(jnp.concatenate(t, axis=0), 0, prow) for t in (g_rows, w_rows, m_rows, v_rows)]
    sm = _adamw(gs, ws, ms, vs, "adamw_small")

    hd = q_norm.shape[1]

    def small_out(t, kind):
        per_layer = []
        for l in range(depth):
            base = l * 6
            per_layer.append({"norm_mix": t[base], "norm_ffn": t[base + 1], "q_norm": t[base + 2, :hd],
                              "k_norm": t[base + 2, hd:2 * hd], "conv_w": t[base + 3:base + 6, :conv_shard]}[kind])
        return jnp.stack(per_layer)

    def big_out(name, i):
        return res[name][i]

    outs = [loss_out, grad_x.reshape(x.shape)]
    for i in range(4):
        outs += [small_out(sm[i], "norm_mix"), big_out("w_in", i), small_out(sm[i], "q_norm"),
                 small_out(sm[i], "k_norm"), small_out(sm[i], "conv_w"), big_out("w_out", i),
                 small_out(sm[i], "norm_ffn"), big_out("w_gate", i), big_out("w_up", i), big_out("w_down", i)]
    return tuple(outs)
```

```python
import jax
import jax.numpy as jnp
from jax import lax
from jax.experimental import pallas as pl
from jax.experimental.pallas import tpu as pltpu

F32 = jnp.float32
BF16 = jnp.bfloat16
MESH = pl.DeviceIdType.MESH

N_DEV = 8
LANES = 128
HEAD_DIM = 64
KEY_CHUNK = 128
EPS = 1e-6
VMEM_LIMIT = 48 * 1024 * 1024

ADAM_LR = 0.001
ADAM_B1 = 0.9
ADAM_B2 = 0.999
ADAM_EPS = 1e-08
ADAM_WD = 0.01
ADAM_STEP = 10

NN = (((1,), (0,)), ((), ()))
NT = (((1,), (1,)), ((), ()))
TN = (((0,), (0,)), ((), ()))


def _dot(a, b, dims):
    return lax.dot_general(a.astype(BF16), b.astype(BF16), dims, preferred_element_type=F32)


def _cparams(*sem):
    return pltpu.CompilerParams(dimension_semantics=sem, vmem_limit_bytes=VMEM_LIMIT)


def _rmsnorm_fwd(x, gain, tm, name, shards=()):
    s, d = x.shape
    nsteps = s // tm
    ng = len(shards)

    def body(*refs):
        x_ref, g_ref, srcs = refs[0], refs[1], refs[2:2 + ng]
        o_ref, dsts, sems = refs[2 + ng], refs[3 + ng:3 + 2 * ng], refs[3 + 2 * ng:]
        i = pl.program_id(0)
        gather = _Gather(srcs, dsts, *sems) if ng else None
        if ng:
            @pl.when(i == 0)
            def _():
                gather.begin()

        xv = x_ref[...]
        r = lax.rsqrt(jnp.mean(xv * xv, axis=-1, keepdims=True) + EPS)
        o_ref[...] = ((xv * r) * g_ref[...]).astype(o_ref.dtype)
        if ng:
            @pl.when(i == nsteps - 1)
            def _():
                gather.relay()
                gather.finish()

    anyspec = pl.BlockSpec(memory_space=pl.ANY)
    outs = pl.pallas_call(
        body, name=name, grid=(nsteps,),
        in_specs=[pl.BlockSpec((tm, d), lambda i: (i, 0)), pl.BlockSpec((1, d), lambda i: (0, 0))] + [anyspec] * ng,
        out_specs=[pl.BlockSpec((tm, d), lambda i: (i, 0))] + [anyspec] * ng,
        out_shape=[jax.ShapeDtypeStruct((s, d), BF16)] + _gathered_shapes(shards),
        scratch_shapes=_gather_scratch(ng) if ng else [],
        compiler_params=_cparams("arbitrary"),
    )(x, gain, *shards)
    return outs[0], list(outs[1:])


def _group_mean_matrix():
    r = lax.broadcasted_iota(jnp.int32, (LANES, LANES), 0) // HEAD_DIM
    c = lax.broadcasted_iota(jnp.int32, (LANES, LANES), 1) // HEAD_DIM
    return jnp.where(r == c, 1.0 / HEAD_DIM, 0.0).astype(BF16)


def _group_mean(v, gm):
    hi = v.astype(BF16)
    lo = (v - hi.astype(F32)).astype(BF16)
    return _dot(hi, gm, NN) + _dot(lo, gm, NN)


def _qknorm_fwd(proj, gains, tm, name):
    s = proj.shape[0]
    ncol = gains.shape[1] // LANES

    def body(p_ref, g_ref, gm_ref, o_ref):
        xv = p_ref[...].astype(F32)
        r = lax.rsqrt(_group_mean(xv * xv, gm_ref[...]) + EPS)
        o_ref[...] = ((xv * r) * g_ref[...]).astype(o_ref.dtype)

    blk = pl.BlockSpec((tm, LANES), lambda i, j: (i, j))
    return pl.pallas_call(
        body, name=name, grid=(s // tm, ncol),
        in_specs=[blk, pl.BlockSpec((1, LANES), lambda i, j: (0, j)),
                  pl.BlockSpec((LANES, LANES), lambda i, j: (0, 0))],
        out_specs=blk,
        out_shape=jax.ShapeDtypeStruct((s, ncol * LANES), BF16),
        compiler_params=_cparams("parallel", "parallel"),
    )(proj, gains, _group_mean_matrix())


def _qknorm_bwd(dq, dk, proj, gains, tm, name):
    s = proj.shape[0]
    ncol = gains.shape[1] // LANES
    half = ncol // 2
    nsteps = s // tm

    def body(dq_ref, dk_ref, p_ref, g_ref, gm_ref, dx_ref, dg_ref):
        i = pl.program_id(1)
        gm = gm_ref[...]
        xv = p_ref[...].astype(F32)
        r = lax.rsqrt(_group_mean(xv * xv, gm) + EPS)
        xhat = xv * r
        dy = jnp.where(pl.program_id(0) < half, dq_ref[...], dk_ref[...])
        dxh = dy * g_ref[...]
        proj_ = _group_mean(dxh * xhat, gm)
        dx_ref[...] = (r * (dxh - xhat * proj_)).astype(dx_ref.dtype)
        part = jnp.sum((dy * xhat).reshape(tm // 8, 8, LANES), axis=0)

        @pl.when(i == 0)
        def _():
            dg_ref[...] = part

        @pl.when(i > 0)
        def _():
            dg_ref[...] += part

        @pl.when(i == nsteps - 1)
        def _():
            dg_ref[...] = jnp.broadcast_to(jnp.sum(dg_ref[...], axis=0, keepdims=True), (8, LANES))

    blk = pl.BlockSpec((tm, LANES), lambda j, i: (i, j))
    return pl.pallas_call(
        body, name=name, grid=(ncol, nsteps),
        in_specs=[pl.BlockSpec((tm, LANES), lambda j, i: (i, jnp.minimum(j, half - 1))),
                  pl.BlockSpec((tm, LANES), lambda j, i: (i, jnp.maximum(j - half, 0))),
                  blk, pl.BlockSpec((1, LANES), lambda j, i: (0, j)),
                  pl.BlockSpec((LANES, LANES), lambda j, i: (0, 0))],
        out_specs=[blk, pl.BlockSpec((8, LANES), lambda j, i: (0, j))],
        out_shape=[jax.ShapeDtypeStruct((s, ncol * LANES), BF16),
                   jax.ShapeDtypeStruct((8, ncol * LANES), F32)],
        compiler_params=_cparams("parallel", "arbitrary"),
    )(dq, dk, proj, gains, _group_mean_matrix())


CONV_ROWS = 256
HALO = 8


def _conv_fwd(proj, conv_w8, name):
    s = proj.shape[0]
    nblk = conv_w8.shape[1] // LANES
    first = 3 * nblk
    nchunk = s // CONV_ROWS

    def body(cb_ref, cc_ref, cu_ref, w_ref, y_ref, hpad):
        hpad[pl.ds(0, 2 * HALO), :] = jnp.zeros((2 * HALO, LANES), F32)

        def fill(i, _):
            r0 = pl.multiple_of(i * CONV_ROWS, CONV_ROWS)
            hpad[pl.ds(r0 + 2 * HALO, CONV_ROWS), :] = (
                cc_ref[pl.ds(r0, CONV_ROWS), :].astype(F32) * cu_ref[pl.ds(r0, CONV_ROWS), :].astype(F32))
            return 0

        lax.fori_loop(0, nchunk, fill, 0)
        w0, w1, w2 = w_ref[0:1, :], w_ref[1:2, :], w_ref[2:3, :]

        def conv(i, _):
            r0 = pl.multiple_of(i * CONV_ROWS, CONV_ROWS)
            win = hpad[pl.ds(r0 + HALO, CONV_ROWS + HALO), :]
            c = (w2 * win[HALO:] + w1 * pltpu.roll(win, 1, 0)[HALO:] + w0 * pltpu.roll(win, 2, 0)[HALO:])
            y_ref[pl.ds(r0, CONV_ROWS), :] = (cb_ref[pl.ds(r0, CONV_ROWS), :].astype(F32) * c).astype(y_ref.dtype)
            return 0

        lax.fori_loop(0, nchunk, conv, 0)

    def col(off):
        return pl.BlockSpec((s, LANES), lambda j: (0, off + j))

    return pl.pallas_call(
        body, name=name, grid=(nblk,),
        in_specs=[col(first), col(first + nblk), col(first + 2 * nblk), pl.BlockSpec((8, LANES), lambda j: (0, j))],
        out_specs=pl.BlockSpec((s, LANES), lambda j: (0, j)),
        out_shape=jax.ShapeDtypeStruct((s, nblk * LANES), BF16),
        scratch_shapes=[pltpu.VMEM((s + 2 * HALO, LANES), F32)],
        compiler_params=_cparams("parallel"),
    )(proj, proj, proj, conv_w8)


def _conv_bwd(dmix, proj, conv_w8, name):
    s = proj.shape[0]
    nblk = conv_w8.shape[1] // LANES
    first = 3 * nblk
    nchunk = s // CONV_ROWS

    def body(dy_ref, cb_ref, cc_ref, cu_ref, w_ref, dcb_ref, dcc_ref, dcu_ref, dw_ref, hpad, dcpad):
        hpad[pl.ds(0, 2 * HALO), :] = jnp.zeros((2 * HALO, LANES), F32)
        dcpad[pl.ds(s, 2 * HALO), :] = jnp.zeros((2 * HALO, LANES), F32)

        def fill(i, _):
            r0 = pl.multiple_of(i * CONV_ROWS, CONV_ROWS)
            hpad[pl.ds(r0 + 2 * HALO, CONV_ROWS), :] = (
                cc_ref[pl.ds(r0, CONV_ROWS), :].astype(F32) * cu_ref[pl.ds(r0, CONV_ROWS), :].astype(F32))
            return 0

        lax.fori_loop(0, nchunk, fill, 0)
        w0, w1, w2 = w_ref[0:1, :], w_ref[1:2, :], w_ref[2:3, :]

        def fold(v):
            return jnp.sum(v.reshape(CONV_ROWS // 8, 8, LANES), axis=0)

        def first_pass(i, acc):
            a0, a1, a2 = acc
            r0 = pl.multiple_of(i * CONV_ROWS, CONV_ROWS)
            win = hpad[pl.ds(r0 + HALO, CONV_ROWS + HALO), :]
            h0 = win[HALO:]
            h1 = pltpu.roll(win, 1, 0)[HALO:]
            h2 = pltpu.roll(win, 2, 0)[HALO:]
            c = w2 * h0 + w1 * h1 + w0 * h2
            dy = dy_ref[pl.ds(r0, CONV_ROWS), :]
            dcb_ref[pl.ds(r0, CONV_ROWS), :] = (dy * c).astype(dcb_ref.dtype)
            dc = dy * cb_ref[pl.ds(r0, CONV_ROWS), :].astype(F32)
            dcpad[pl.ds(r0, CONV_ROWS), :] = dc
            return a0 + fold(dc * h2), a1 + fold(dc * h1), a2 + fold(dc * h0)

        z8 = jnp.zeros((8, LANES), F32)
        a0, a1, a2 = lax.fori_loop(0, nchunk, first_pass, (z8, z8, z8))
        dw_ref[...] = jnp.concatenate(
            [jnp.sum(a0, axis=0, keepdims=True), jnp.sum(a1, axis=0, keepdims=True),
             jnp.sum(a2, axis=0, keepdims=True), jnp.zeros((5, LANES), F32)], axis=0)

        def second_pass(i, _):
            r0 = pl.multiple_of(i * CONV_ROWS, CONV_ROWS)
            win = dcpad[pl.ds(r0, CONV_ROWS + HALO), :]
            n = CONV_ROWS + HALO
            dh = (w2 * win[:CONV_ROWS] + w1 * pltpu.roll(win, n - 1, 0)[:CONV_ROWS]
                  + w0 * pltpu.roll(win, n - 2, 0)[:CONV_ROWS])
            dcc_ref[pl.ds(r0, CONV_ROWS), :] = (dh * cu_ref[pl.ds(r0, CONV_ROWS), :].astype(F32)).astype(dcc_ref.dtype)
            dcu_ref[pl.ds(r0, CONV_ROWS), :] = (dh * cc_ref[pl.ds(r0, CONV_ROWS), :].astype(F32)).astype(dcu_ref.dtype)
            return 0

        lax.fori_loop(0, nchunk, second_pass, 0)

    def col(off):
        return pl.BlockSpec((s, LANES), lambda j: (0, off + j))

    out = pl.BlockSpec((s, LANES), lambda j: (0, j))
    return pl.pallas_call(
        body, name=name, grid=(nblk,),
        in_specs=[col(nblk), col(first), col(first + nblk), col(first + 2 * nblk),
                  pl.BlockSpec((8, LANES), lambda j: (0, j))],
        out_specs=[out, out, out, pl.BlockSpec((8, LANES), lambda j: (0, j))],
        out_shape=[jax.ShapeDtypeStruct((s, nblk * LANES), BF16)] * 3 + [jax.ShapeDtypeStruct((8, nblk * LANES), F32)],
        scratch_shapes=[pltpu.VMEM((s + 2 * HALO, LANES), F32), pltpu.VMEM((s + 2 * HALO, LANES), F32)],
        compiler_params=_cparams("parallel"),
    )(dmix, proj, proj, proj, conv_w8)


LOG2E = 1.4426950408889634
LN2 = 0.6931471805599453
NEG_BIG = -1e30
SATURATED = 160.0


def _cumsum_matrix(kind):
    j = lax.broadcasted_iota(jnp.int32, (KEY_CHUNK, 2 * KEY_CHUNK), 0)
    c = lax.broadcasted_iota(jnp.int32, (KEY_CHUNK, 2 * KEY_CHUNK), 1)
    tri = {"after": j > c, "upto": j <= c, "before": j < c}[kind]
    return jnp.where((c >= KEY_CHUNK) | tri, 1.0, 0.0).astype(BF16)


def _stack_heads(t, m0):
    zero = jnp.zeros_like(t)
    return jnp.concatenate([jnp.where(m0, t, zero), jnp.where(m0, zero, t)], axis=0)


def _softplus2(z):
    sp = jnp.maximum(z, 0.0) + jnp.log2(1.0 + jnp.exp2(-jnp.abs(z)))
    return sp, z - sp


def _key_chunk(ref, kc):
    return ref[pl.ds(pl.multiple_of(kc * KEY_CHUNK, KEY_CHUNK), KEY_CHUNK), :]


def _attn_bwd(qk, proj, dmix, rtot, used, tq, name, travel=()):
    s = qk.shape[0]
    nhp = qk.shape[1] // (2 * LANES)
    nc = tq // KEY_CHUNK
    nq = s // tq
    nt = len(travel)

    def body(used_ref, q_ref, k_ref, v_ref, do_ref, r_ref, cmi_ref, cme_ref, bias_ref, dq_ref, dk_ref, dv_ref,
             z_refs, ls_refs, sig_refs, sp_refs, gb_refs, pr_ref, gs_ref, copies):
        qi = pl.program_id(1)

        @pl.when(qi == 0)
        def _():
            dk_ref[...] = jnp.zeros_like(dk_ref)
            dv_ref[...] = jnp.zeros_like(dv_ref)

        if copies is not None:
            @pl.when(jnp.logical_and(pl.program_id(0) == 0, qi == 0))
            def _():
                _exchange_begin(copies)

        nslots = (qi + 1) * nc
        walked = used_ref[pl.program_id(0), qi].astype(jnp.int32)
        first = jnp.clip(nslots - walked, 0, nslots - nc) // nc * nc
        m0 = lax.broadcasted_iota(jnp.int32, (1, LANES), 1) < HEAD_DIM
        qs = _stack_heads(q_ref[...], m0)
        do = do_ref[...]
        dos = _stack_heads(do.astype(BF16), m0)
        dosl = _stack_heads((do * LN2).astype(BF16), m0)
        cmi = cmi_ref[...]
        cme = cme_ref[...]

        def chunk_at(i):
            return jnp.clip(i, first, nslots - 1)

        def scores(kc):
            return _dot(qs, _key_chunk(k_ref, kc), NT)

        def weights(ls, cs, da, pr, kc):
            a = jnp.exp2(ls - (pr - cs[:, :KEY_CHUNK]))
            gb = (a * da).astype(BF16)
            ks = pl.multiple_of(kc * KEY_CHUNK, KEY_CHUNK)
            dv_ref[pl.ds(ks, KEY_CHUNK), :] += _dot(a, dos, TN)
            return gb, jnp.exp2(ls), pr - cs[:, KEY_CHUNK:]

        def score_grads(gb, sig, cg, gs, dq, kc):
            dzb = (gb.astype(F32) * (1.0 - sig) - sig * (gs + cg[:, :KEY_CHUNK])).astype(BF16)
            ks = pl.multiple_of(kc * KEY_CHUNK, KEY_CHUNK)
            dk_ref[pl.ds(ks, KEY_CHUNK), :] += _dot(dzb, qs, TN)
            dq = dq + _dot(jnp.concatenate([dzb[:tq], dzb[tq:]], axis=1), _stack_heads(_key_chunk(k_ref, kc), m0), NN)
            return gs + cg[:, KEY_CHUNK:], dq

        def step(i, par, bias=None, stages="zswg"):
            cur, prv = par, 1 - par
            k1, k2 = chunk_at(i - 1), chunk_at(i - 2)
            z_next = scores(chunk_at(i + 1))
            if "w" in stages:
                cs = _dot(sp_refs[prv][...], cmi, NN)
                da = _dot(dosl, _key_chunk(v_ref, k1), NT)
            if "g" in stages:
                cg = _dot(gb_refs[cur][...], cme, NN)
            z = z_refs[cur][...]
            if bias is not None:
                z = z + bias
            sp, ls = _softplus2(z)
            sp_refs[cur][...] = sp.astype(BF16)
            ls_refs[cur][...] = ls
            if "g" in stages:
                gs, dq = score_grads(gb_refs[cur][...], sig_refs[cur][...], cg, gs_ref[...], dq_ref[...], k2)
                gs_ref[...] = gs
                dq_ref[...] = dq
            if "w" in stages:
                gb, sig, pr = weights(ls_refs[prv][...], cs, da, pr_ref[...], k1)
                gb_refs[prv][...] = gb
                sig_refs[prv][...] = sig
                pr_ref[...] = pr
            z_refs[prv][...] = z_next

        pr_ref[...] = jnp.concatenate([r_ref[:, :LANES], r_ref[:, LANES:]], axis=0)
        gs_ref[...] = jnp.zeros((2 * tq, LANES), F32)
        dq_ref[...] = jnp.zeros((tq, LANES), F32)
        z_refs[0][...] = scores(first)
        only_diagonal = first == nslots - nc
        step(first, 0, jnp.where(only_diagonal, bias_ref[0], 0.0), stages="zs")
        step(first + 1, 1, jnp.where(only_diagonal, bias_ref[1], 0.0), stages="zsw")

        def two_steps(j, _):
            step(2 * j, 0)
            step(2 * j + 1, 1)
            return 0

        lax.fori_loop(first // 2 + 1, nslots // 2 - 1, two_steps, 0)

        @pl.when(jnp.logical_not(only_diagonal))
        def _():
            step(nslots - 2, 0, bias_ref[0])
            step(nslots - 1, 1, bias_ref[1])

        k1, k2 = chunk_at(nslots - 1), chunk_at(nslots - 2)
        gb, sig, _ = weights(ls_refs[1][...], _dot(sp_refs[1][...], cmi, NN),
                             _dot(dosl, _key_chunk(v_ref, k1), NT), pr_ref[...], k1)
        gb2 = gb_refs[0][...]
        gs, dq = score_grads(gb2, sig_refs[0][...], _dot(gb2, cme, NN), gs_ref[...], dq_ref[...], k2)
        _, dq = score_grads(gb, sig, _dot(gb, cme, NN), gs, dq, k1)
        dq_ref[...] = dq

        if copies is not None:
            @pl.when(jnp.logical_and(pl.program_id(0) == nhp - 1, qi == nq - 1))
            def _():
                _exchange_finish(copies)

    def wrapped(*refs):
        ins, rest = refs[:9], refs[9:]
        srcs, rest = rest[:nt], rest[nt:]
        outs, rest = rest[:3], rest[3:]
        lands, rest = rest[:nt], rest[nt:]
        z0, z1, ls0, ls1, sg0, sg1, sp0, sp1, gb0, gb1, pr_ref, gs_ref = rest[:12]
        copies = _exchange_copies(srcs, lands, *rest[12:]) if nt else None
        body(*ins, *outs, (z0, z1), (ls0, ls1), (sg0, sg1), (sp0, sp1), (gb0, gb1), pr_ref, gs_ref, copies)

    assert nc == 2
    bias = _diag_bias(tq, True)
    bias = jnp.concatenate([bias[:, :, :KEY_CHUNK], bias[:, :, KEY_CHUNK:]], axis=1)
    qblk = pl.BlockSpec((tq, LANES), lambda p, i: (i, p))
    full = pl.BlockSpec((s, LANES), lambda p, i: (0, p))
    cmspec = pl.BlockSpec((KEY_CHUNK, 2 * KEY_CHUNK), lambda p, i: (0, 0))
    anyspec = pl.BlockSpec(memory_space=pl.ANY)
    shape = jax.ShapeDtypeStruct((s, nhp * LANES), F32)
    f32buf = pltpu.VMEM((2 * tq, LANES), F32)
    bf16buf = pltpu.VMEM((2 * tq, LANES), BF16)
    outs = pl.pallas_call(
        wrapped, name=name, grid=(nhp, nq),
        in_specs=[pl.BlockSpec(memory_space=pltpu.SMEM),
                  qblk,
                  pl.BlockSpec((s, LANES), lambda p, i: (0, nhp + p)),
                  pl.BlockSpec((s, LANES), lambda p, i: (0, 2 * nhp + p)),
                  qblk,
                  pl.BlockSpec((tq, 2 * LANES), lambda p, i: (i, p)),
                  cmspec, cmspec,
                  pl.BlockSpec((nc, 2 * tq, LANES), lambda p, i: (0, 0, 0))] + [anyspec] * nt,
        out_specs=[qblk, full, full] + [anyspec] * nt,
        out_shape=[shape, shape, shape] + [jax.ShapeDtypeStruct(t.shape, t.dtype) for t in travel],
        scratch_shapes=[f32buf] * 6 + [bf16buf] * 4 + [f32buf] * 2 + (_exchange_scratch(nt) if nt else []),
        compiler_params=_cparams("arbitrary", "arbitrary"),
    )(used, qk, qk, proj, dmix, rtot, _cumsum_matrix("upto"), _cumsum_matrix("before"), bias, *travel)
    return outs[0], outs[1], outs[2], list(outs[3:])


def _pair_cumsum_matrix(kind):
    j = lax.broadcasted_iota(jnp.int32, (2 * KEY_CHUNK, 4 * KEY_CHUNK), 0)
    c = lax.broadcasted_iota(jnp.int32, (2 * KEY_CHUNK, 4 * KEY_CHUNK), 1)
    same_head = (j // KEY_CHUNK) == ((c // KEY_CHUNK) % 2)
    jj, cc = j % KEY_CHUNK, c % KEY_CHUNK
    tri = {"after": jj > cc, "upto": jj <= cc, "before": jj < cc}[kind]
    return jnp.where(same_head & ((c >= 2 * KEY_CHUNK) | tri), 1.0, 0.0).astype(BF16)


def _diag_bias(tq, ascending):
    nc = tq // KEY_CHUNK
    shape = (nc, tq, 2 * KEY_CHUNK)
    d = lax.broadcasted_iota(jnp.int32, shape, 0)
    r = lax.broadcasted_iota(jnp.int32, shape, 1)
    c = lax.broadcasted_iota(jnp.int32, shape, 2) % KEY_CHUNK
    chunk = d if ascending else nc - 1 - d
    return jnp.where(chunk * KEY_CHUNK + c < r, 0.0, NEG_BIG).astype(F32)


def _attn_fwd(qk, proj, tq, name, shards=()):
    s = qk.shape[0]
    nhp = qk.shape[1] // (2 * LANES)
    nc = tq // KEY_CHUNK
    nq = s // tq
    ng = len(shards)
    assert nc == 2
    w = 2 * KEY_CHUNK

    def body(q_ref, k_ref, v_ref, cm_ref, bias_ref, o_ref, r_ref, used_ref, z_refs, ls_refs, cs_refs, ct_refs,
             sp_refs, ab_refs, acc_ref, gather):
        qi = pl.program_id(1)
        if gather is not None:
            @pl.when(jnp.logical_and(pl.program_id(0) == 0, qi == 0))
            def _():
                gather.begin()

        nslots = (qi + 1) * nc
        m0 = lax.broadcasted_iota(jnp.int32, (1, LANES), 1) < HEAD_DIM
        q = q_ref[...]
        cm = cm_ref[...]

        def chunk_at(i):
            return jnp.clip(nslots - 1 - i, 0, nslots - 1)

        def scores(kc):
            return _dot(q, _stack_heads(_key_chunk(k_ref, kc), m0), NT)

        def values(ab, kc):
            return _dot(ab, _stack_heads(_key_chunk(v_ref, kc), m0), NN)

        def step(i, par, bias=None, stages="zscwv"):
            cur, prv = par, 1 - par
            if "z" in stages:
                z_next = scores(chunk_at(i + 1))
            if "c" in stages:
                cs = _dot(sp_refs[prv][...], cm, NN)
            if "v" in stages:
                pv = values(ab_refs[prv][...], chunk_at(i - 3))
            if "w" in stages:
                rs = r_ref[...]
                r_ref[...] = rs + ct_refs[cur][...]
                ab_refs[cur][...] = jnp.exp2(ls_refs[cur][...] - cs_refs[cur][...] - rs).astype(BF16)
            if "s" in stages:
                z = z_refs[cur][...]
                if bias is not None:
                    z = z + bias
                sp, ls = _softplus2(z)
                sp_refs[cur][...] = sp.astype(BF16)
                ls_refs[cur][...] = ls
            if "v" in stages:
                acc_ref[...] += pv
            if "c" in stages:
                cs_refs[prv][...] = cs[:, :w]
                ct_refs[prv][...] = cs[:, w:]
            if "z" in stages:
                z_refs[prv][...] = z_next

        z_refs[0][...] = scores(chunk_at(0))
        ab_refs[1][...] = jnp.zeros((tq, w), BF16)
        r_ref[...] = jnp.zeros((tq, w), F32)
        acc_ref[...] = jnp.zeros((tq, LANES), F32)
        step(0, 0, bias_ref[0], stages="zs")
        step(1, 1, bias_ref[1], stages="zsc")

        def two_steps(carry):
            j, _ = carry
            step(2 * j, 0)
            step(2 * j + 1, 1)
            return j + 1, jnp.min(jnp.minimum(r_ref[:, :KEY_CHUNK], r_ref[:, KEY_CHUNK:]))

        pairs, low = lax.while_loop(lambda c: jnp.logical_and(c[0] < nslots // 2, c[1] < SATURATED), two_steps,
                                    (jnp.int32(1), jnp.float32(0.0)))
        entered = 2 * pairs
        saturated = low >= SATURATED

        @pl.when(saturated)
        def _():
            step(entered, 0, stages="v")

        @pl.when(jnp.logical_not(saturated))
        def _():
            step(entered, 0, stages="cwv")
            step(entered + 1, 1, stages="wv")
            step(entered + 2, 0, stages="v")

        o_ref[...] = acc_ref[...].astype(o_ref.dtype)
        used_ref[pl.program_id(0), qi] = jnp.where(saturated, entered - 2, entered).astype(F32)

        if gather is not None:
            @pl.when(jnp.logical_and(pl.program_id(0) == nhp - 1, qi == nq // 2))
            def _():
                gather.relay()

            @pl.when(jnp.logical_and(pl.program_id(0) == nhp - 1, qi == nq - 1))
            def _():
                gather.finish()

    def wrapped(*refs):
        ins, rest = refs[:5], refs[5:]
        srcs, rest = rest[:ng], rest[ng:]
        outs, rest = rest[:3], rest[3:]
        dsts, scratch = rest[:ng], rest[ng:]
        z, ls, cs, ct, sp, ab = [scratch[2 * j:2 * j + 2] for j in range(6)]
        gather = _Gather(srcs, dsts, *scratch[13:]) if ng else None
        body(*ins, *outs, z, ls, cs, ct, sp, ab, scratch[12], gather)

    f32buf = pltpu.VMEM((tq, w), F32)
    bf16buf = pltpu.VMEM((tq, w), BF16)
    anyspec = pl.BlockSpec(memory_space=pl.ANY)
    outs = pl.pallas_call(
        wrapped, name=name, grid=(nhp, nq),
        in_specs=[pl.BlockSpec((tq, LANES), lambda p, i: (i, p)),
                  pl.BlockSpec((s, LANES), lambda p, i: (0, nhp + p)),
                  pl.BlockSpec((s, LANES), lambda p, i: (0, 2 * nhp + p)),
                  pl.BlockSpec((w, 2 * w), lambda p, i: (0, 0)),
                  pl.BlockSpec((nc, tq, w), lambda p, i: (0, 0, 0))] + [anyspec] * ng,
        out_specs=[pl.BlockSpec((tq, LANES), lambda p, i: (i, p)),
                   pl.BlockSpec((tq, w), lambda p, i: (i, p)),
                   pl.BlockSpec(memory_space=pltpu.SMEM)] + [anyspec] * ng,
        out_shape=[jax.ShapeDtypeStruct((s, nhp * LANES), BF16),
                   jax.ShapeDtypeStruct((s, nhp * w), F32),
                   jax.ShapeDtypeStruct((nhp, nq), F32)] + _gathered_shapes(shards),
        scratch_shapes=([f32buf] * 8 + [bf16buf] * 4 + [pltpu.VMEM((tq, LANES), F32)]
                        + (_gather_scratch(ng) if ng else [])),
        compiler_params=_cparams("arbitrary", "arbitrary"),
    )(qk, qk, proj, _pair_cumsum_matrix("after"), _diag_bias(tq, False), *shards)
    return outs[0], outs[1], outs[2], list(outs[3:])


BLOCK_PAIR = 2
MXU_WIDTH = 256


def _side_by_side(b_ref):
    return jnp.concatenate([b_ref[p] for p in range(BLOCK_PAIR)], axis=1)


def _mm_blocks(h, ga, widx, tm, name):
    s, d = h.shape
    nb, cols = ga.shape[1], ga.shape[3]

    def body(a_ref, b_ref, o_ref):
        o_ref[...] = _dot(a_ref[...], _side_by_side(b_ref), NN).astype(o_ref.dtype)

    return pl.pallas_call(
        body, name=name, grid=(s // tm, nb // BLOCK_PAIR),
        in_specs=[pl.BlockSpec((tm, d), lambda i, j: (i, 0)),
                  pl.BlockSpec((None, BLOCK_PAIR, d, cols), lambda i, j: (widx, j, 0, 0))],
        out_specs=pl.BlockSpec((tm, BLOCK_PAIR * cols), lambda i, j: (i, j)),
        out_shape=jax.ShapeDtypeStruct((s, nb * cols), BF16),
        compiler_params=_cparams("parallel", "arbitrary"),
    )(h, ga)


def _mm_swiglu(h, ga, gidx, uidx, tm, name):
    s, d = h.shape
    nb, cols = ga.shape[1], ga.shape[3]

    def body(a_ref, bg_ref, bu_ref, g_ref, u_ref, act_ref):
        a = a_ref[...]
        g = _dot(a, _side_by_side(bg_ref), NN)
        u = _dot(a, _side_by_side(bu_ref), NN)
        g_ref[...] = g.astype(g_ref.dtype)
        u_ref[...] = u.astype(u_ref.dtype)
        act_ref[...] = (g * (1.0 / (1.0 + jnp.exp(-g))) * u).astype(act_ref.dtype)

    def wspec(idx):
        return pl.BlockSpec((None, BLOCK_PAIR, d, cols), lambda i, j: (idx, j, 0, 0))

    out = pl.BlockSpec((tm, BLOCK_PAIR * cols), lambda i, j: (i, j))
    shape = jax.ShapeDtypeStruct((s, nb * cols), BF16)
    return pl.pallas_call(
        body, name=name, grid=(s // tm, nb // BLOCK_PAIR),
        in_specs=[pl.BlockSpec((tm, d), lambda i, j: (i, 0)), wspec(gidx), wspec(uidx)],
        out_specs=[out, out, out], out_shape=[shape, shape, shape],
        compiler_params=_cparams("parallel", "arbitrary"),
    )(h, ga, ga)


def _mm_residual_norm(a, w3, lidx, res, gain, tm, name):
    s, k = a.shape
    n = w3.shape[2]

    def body(a_ref, b_ref, r_ref, g_ref, o_ref, h_ref):
        xv = r_ref[...] + _dot(a_ref[...], b_ref[...], NN)
        o_ref[...] = xv
        r = lax.rsqrt(jnp.mean(xv * xv, axis=-1, keepdims=True) + EPS)
        h_ref[...] = ((xv * r) * g_ref[...]).astype(h_ref.dtype)

    row = pl.BlockSpec((tm, n), lambda i: (i, 0))
    return pl.pallas_call(
        body, name=name, grid=(s // tm,),
        in_specs=[pl.BlockSpec((tm, k), lambda i: (i, 0)),
                  pl.BlockSpec((None, k, n), lambda i: (lidx, 0, 0), pipeline_mode=pl.Buffered(1)),
                  row, pl.BlockSpec((1, n), lambda i: (0, 0))],
        out_specs=[row, row],
        out_shape=[jax.ShapeDtypeStruct((s, n), F32), jax.ShapeDtypeStruct((s, n), BF16)],
        compiler_params=_cparams("parallel"),
    )(a, w3, res, gain)


def _mm_residual_loss(a, w3, lidx, res, target, tm, name):
    s, k = a.shape
    n = w3.shape[2]
    nsteps = s // tm

    def body(a_ref, b_ref, r_ref, t_ref, dy_ref, dyb_ref, l_ref, acc):
        i = pl.program_id(0)
        diff = r_ref[...] + _dot(a_ref[...], b_ref[...], NN) - t_ref[...]
        dy_ref[...] = diff * (1.0 / n)
        dyb_ref[...] = (diff * (1.0 / n)).astype(dyb_ref.dtype)
        part = jnp.sum((diff * diff).reshape(tm // 8, 8, n), axis=0)

        @pl.when(i == 0)
        def _():
            acc[...] = part

        @pl.when(i > 0)
        def _():
            acc[...] += part

        @pl.when(i == nsteps - 1)
        def _():
            tot = jnp.sum(jnp.sum(acc[...], axis=1, keepdims=True), axis=0, keepdims=True)
            l_ref[...] = jnp.broadcast_to(tot * (0.5 / n), (8, LANES))

    row = pl.BlockSpec((tm, n), lambda i: (i, 0))
    return pl.pallas_call(
        body, name=name, grid=(nsteps,),
        in_specs=[pl.BlockSpec((tm, k), lambda i: (i, 0)),
                  pl.BlockSpec((None, k, n), lambda i: (lidx, 0, 0), pipeline_mode=pl.Buffered(1)),
                  row, row],
        out_specs=[row, row, pl.BlockSpec((8, LANES), lambda i: (0, 0))],
        out_shape=[jax.ShapeDtypeStruct((s, n), F32), jax.ShapeDtypeStruct((s, n), BF16),
                   jax.ShapeDtypeStruct((8, LANES), F32)],
        scratch_shapes=[pltpu.VMEM((8, n), F32)],
        compiler_params=_cparams("arbitrary"),
    )(a, w3, res, target)


def _mm_nt(a, w3, lidx, tm, tn, name):
    s, k = a.shape
    n = w3.shape[1]

    def body(a_ref, b_ref, o_ref):
        o_ref[...] = _dot(a_ref[...], b_ref[...], NT)

    return pl.pallas_call(
        body, name=name, grid=(s // tm, n // tn),
        in_specs=[pl.BlockSpec((tm, k), lambda i, j: (i, 0)),
                  pl.BlockSpec((None, tn, k), lambda i, j: (lidx, j, 0))],
        out_specs=pl.BlockSpec((tm, tn), lambda i, j: (i, j)),
        out_shape=jax.ShapeDtypeStruct((s, n), F32),
        compiler_params=_cparams("parallel", "arbitrary"),
    )(a, w3)


def _mm_nt_swiglu_bwd(dx, wd3, lidx, g, u, tm, name, travel=()):
    s, d = dx.shape
    cols = BLOCK_PAIR * (g.shape[1] // N_DEV)

    def body(a_ref, b_ref, g_ref, u_ref, dg_ref, du_ref):
        a = a_ref[...]
        for c0 in range(0, cols, MXU_WIDTH):
            sl = slice(c0, c0 + MXU_WIDTH)
            dact = _dot(a, b_ref[sl, :], NT)
            gv = g_ref[:, sl].astype(F32)
            sig = 0.5 * jnp.tanh(0.5 * gv) + 0.5
            silu = gv * sig
            du_ref[:, sl] = (dact * silu).astype(du_ref.dtype)
            dsilu = sig + silu * (1.0 - sig)
            dg_ref[:, sl] = (dact * u_ref[:, sl].astype(F32) * dsilu).astype(dg_ref.dtype)

    blk = pl.BlockSpec((tm, cols), lambda i, j: (i, j))
    shape = jax.ShapeDtypeStruct(g.shape, BF16)
    grid = (s // tm, N_DEV // BLOCK_PAIR)
    body, more_in, more_out, more_shapes, more_scratch = _host_exchange(body, 4, 2, travel, grid)
    outs = pl.pallas_call(
        body, name=name, grid=grid,
        in_specs=[pl.BlockSpec((tm, d), lambda i, j: (i, 0)),
                  pl.BlockSpec((None, cols, d), lambda i, j: (lidx, j, 0)), blk, blk] + more_in,
        out_specs=[blk, blk] + more_out, out_shape=[shape, shape] + more_shapes,
        scratch_shapes=more_scratch,
        compiler_params=_cparams("arbitrary", "arbitrary"),
    )(dx, wd3, g, u, *travel)
    return outs[0], outs[1], list(outs[2:])


def _mm_nt_norm_bwd(das, ga, widxs, x, gain, dres, tm, name, travel=()):
    s = das[0].shape[0]
    nb, d, cols = ga.shape[1], ga.shape[2], ga.shape[3]
    nw = len(das)
    nsteps = s // tm

    def body(*refs):
        a_refs, b_refs = refs[:nw], refs[nw:2 * nw]
        x_ref, g_ref, dres_ref, dx_ref, dxb_ref, dg_ref = refs[2 * nw:]
        i = pl.program_id(0)
        dhv = None
        wide = BLOCK_PAIR * cols
        for w in range(nw):
            for k in range(nb // BLOCK_PAIR):
                b = jnp.concatenate([b_refs[w][BLOCK_PAIR * k + p] for p in range(BLOCK_PAIR)], axis=1)
                part = _dot(a_refs[w][:, k * wide:(k + 1) * wide], b, NT)
                dhv = part if dhv is None else dhv + part
        xv = x_ref[...]
        r = lax.rsqrt(jnp.mean(xv * xv, axis=-1, keepdims=True) + EPS)
        xhat = xv * r
        dxh = dhv * g_ref[...]
        dxv = dres_ref[...] + r * (dxh - xhat * jnp.mean(dxh * xhat, axis=-1, keepdims=True))
        dx_ref[...] = dxv
        dxb_ref[...] = dxv.astype(dxb_ref.dtype)
        part = jnp.sum((dhv * xhat).reshape(tm // 8, 8, d), axis=0)

        @pl.when(i == 0)
        def _():
            dg_ref[...] = part

        @pl.when(i > 0)
        def _():
            dg_ref[...] += part

        @pl.when(i == nsteps - 1)
        def _():
            dg_ref[...] = jnp.broadcast_to(jnp.sum(dg_ref[...], axis=0, keepdims=True), (8, d))

    def wspec(idx):
        return pl.BlockSpec((None, nb, d, cols), lambda i: (idx, 0, 0, 0), pipeline_mode=pl.Buffered(1))

    row = pl.BlockSpec((tm, d), lambda i: (i, 0))
    body, more_in, more_out, more_shapes, more_scratch = _host_exchange(body, 2 * nw + 3, 3, travel, (nsteps,))
    outs = pl.pallas_call(
        body, name=name, grid=(nsteps,),
        in_specs=([pl.BlockSpec((tm, nb * cols), lambda i: (i, 0))] * nw + [wspec(i) for i in widxs]
                  + [row, pl.BlockSpec((1, d), lambda i: (0, 0)), row] + more_in),
        out_specs=[row, row, pl.BlockSpec((8, d), lambda i: (0, 0))] + more_out,
        out_shape=[jax.ShapeDtypeStruct((s, d), F32), jax.ShapeDtypeStruct((s, d), BF16),
                   jax.ShapeDtypeStruct((8, d), F32)] + more_shapes,
        scratch_shapes=more_scratch,
        compiler_params=_cparams("arbitrary"),
    )(*das, *([ga] * nw), x, gain, dres, *travel)
    return outs[0], outs[1], outs[2], list(outs[3:])


def _mm_tn(a, b, ta, tb, tk, out_blocks, name, travel=()):
    s, ka = a.shape
    nb = b.shape[1]
    nk = s // tk
    cols = tb
    if out_blocks:
        tb = BLOCK_PAIR * cols

    def body(a_ref, b_ref, o_ref, ob_ref):
        k = pl.program_id(2)
        part = _dot(a_ref[...], b_ref[...], TN)

        def put(first):
            if out_blocks:
                for p in range(BLOCK_PAIR):
                    piece = part[:, p * cols:(p + 1) * cols]
                    o_ref[p] = piece if first else o_ref[p] + piece
            else:
                o_ref[...] = part if first else o_ref[...] + part

        @pl.when(k == 0)
        def _():
            put(True)

        @pl.when(k > 0)
        def _():
            put(False)

        @pl.when(k == nk - 1)
        def _():
            ob_ref[...] = o_ref[...].astype(ob_ref.dtype)

    if out_blocks:
        out_spec = pl.BlockSpec((BLOCK_PAIR, ta, cols), lambda i, j, k: (j, i, 0))
        shape = (nb // cols, ka, cols)
    else:
        out_spec = pl.BlockSpec((ta, tb), lambda i, j, k: (i, j))
        shape = (ka, nb)
    grid = (ka // ta, nb // tb, nk)
    body, more_in, more_out, more_shapes, more_scratch = _host_exchange(body, 2, 2, travel, grid)
    outs = pl.pallas_call(
        body, name=name, grid=grid,
        in_specs=[pl.BlockSpec((tk, ta), lambda i, j, k: (k, i)),
                  pl.BlockSpec((tk, tb), lambda i, j, k: (k, j))] + more_in,
        out_specs=[out_spec, out_spec] + more_out,
        out_shape=[jax.ShapeDtypeStruct(shape, F32), jax.ShapeDtypeStruct(shape, BF16)] + more_shapes,
        scratch_shapes=more_scratch,
        compiler_params=_cparams("arbitrary", "arbitrary", "arbitrary"),
    )(a, b, *travel)
    return (outs[0], outs[1]), list(outs[2:])


def _adamw(g, w, m, v, name):
    rows, cols = g.shape
    c1 = 1.0 / (1.0 - ADAM_B1 ** ADAM_STEP)
    c2 = 1.0 / (1.0 - ADAM_B2 ** ADAM_STEP)

    def body(p_ref, w_ref, m_ref, v_ref, g_ref, d_ref, nm_ref, nv_ref):
        gv = p_ref[...]
        nm = ADAM_B1 * m_ref[...] + (1.0 - ADAM_B1) * gv
        nv = ADAM_B2 * v_ref[...] + (1.0 - ADAM_B2) * (gv * gv)
        g_ref[...] = gv
        nm_ref[...] = nm
        nv_ref[...] = nv
        d_ref[...] = -ADAM_LR * ((nm * c1) / (jnp.sqrt(nv * c2) + ADAM_EPS) + ADAM_WD * w_ref[...])

    blk = pl.BlockSpec((rows, cols), lambda i: (0, 0))
    shape = jax.ShapeDtypeStruct((rows, cols), F32)
    return pl.pallas_call(
        body, name=name, grid=(1,),
        in_specs=[blk] * 4, out_specs=[blk] * 4, out_shape=[shape] * 4,
        compiler_params=_cparams("arbitrary"),
    )(g, w, m, v)


def _adamw_sharded(parts, grads, my, w, m, v, tr, name):
    depth, rows, cols = w.shape
    p, pr, pc = parts[0].shape
    c1 = 1.0 / (1.0 - ADAM_B1 ** ADAM_STEP)
    c2 = 1.0 / (1.0 - ADAM_B2 ** ADAM_STEP)

    def body(my_ref, *refs):
        p_refs, own_refs = refs[:depth], refs[depth:2 * depth]
        w_ref, m_ref, v_ref, g_ref, d_ref, nm_ref, nv_ref = refs[2 * depth:]
        layer = pl.program_id(0)
        for ll in range(depth):
            @pl.when(layer == ll)
            def _(ll=ll):
                mine = own_refs[ll][...]
                g = jnp.where(my_ref[0] == 0, mine, p_refs[ll][0].astype(F32))
                for k in range(1, p):
                    g = g + jnp.where(my_ref[0] == k, mine, p_refs[ll][k].astype(F32))
                g = g[:, :cols]
                nm = ADAM_B1 * m_ref[...] + (1.0 - ADAM_B1) * g
                nv = ADAM_B2 * v_ref[...] + (1.0 - ADAM_B2) * (g * g)
                g_ref[...] = g
                nm_ref[...] = nm
                nv_ref[...] = nv
                d_ref[...] = -ADAM_LR * ((nm * c1) / (jnp.sqrt(nv * c2) + ADAM_EPS) + ADAM_WD * w_ref[...])

    def row_block(ll, l, i):
        return jnp.where(l == ll, i, 0)

    blk = pl.BlockSpec((None, tr, cols), lambda l, i, my_: (l, i, 0))
    shape = jax.ShapeDtypeStruct((depth, rows, cols), F32)
    return pl.pallas_call(
        body, name=name,
        grid_spec=pltpu.PrefetchScalarGridSpec(
            num_scalar_prefetch=1, grid=(depth, rows // tr),
            in_specs=([pl.BlockSpec((p, tr, pc), lambda l, i, my_, ll=ll: (0, row_block(ll, l, i), 0))
                       for ll in range(depth)]
                      + [pl.BlockSpec((None, tr, pc), lambda l, i, my_, ll=ll: (my_[0], row_block(ll, l, i), 0))
                         for ll in range(depth)]
                      + [blk, blk, blk]),
            out_specs=[blk] * 4),
        out_shape=[shape] * 4,
        compiler_params=_cparams("arbitrary", "arbitrary"),
    )(my, *parts, *grads, w, m, v)


def _place():
    x, y, c = lax.axis_index("x"), lax.axis_index("y"), lax.axis_index("c")
    return x, y, c


class _Gather:
    def __init__(self, srcs, dsts, send_sems, recv_sems, local_sems):
        na = len(srcs)
        x, y, c = _place()
        me, sibling = (x, y, c), (x, y, 1 - c)
        chips = [(1 - x, y), (x, 1 - y), (1 - x, 1 - y)]

        def slot(a, dev):
            return dsts[a].at[:, pl.ds(4 * dev[0] + 2 * dev[1] + dev[2], 1)]

        def copy(k, a, block, to, from_shard=False):
            return pltpu.make_async_remote_copy(
                src_ref=srcs[a] if from_shard else slot(a, block), dst_ref=slot(a, block),
                send_sem=send_sems.at[k, a], recv_sem=recv_sems.at[k, a], device_id=to, device_id_type=MESH)

        pairs = [(j, chip, a) for j, chip in enumerate(chips) for a in range(na)]
        self.mine = [pltpu.make_async_copy(srcs[a], slot(a, me), local_sems.at[a]) for a in range(na)]
        self.first = [copy(0, a, me, sibling, True) for a in range(na)]
        self.first += [copy(1 + j, a, me, (*chip, c), True) for j, chip, a in pairs]
        self.over_ici = [copy(1 + j, a, (*chip, c), me) for j, chip, a in pairs]
        self.passed = [copy(4 + j, a, (*chip, c), sibling) for j, chip, a in pairs]
        self.from_sibling = [copy(0, a, sibling, me) for a in range(na)]
        self.from_sibling += [copy(4 + j, a, (*chip, 1 - c), me) for j, chip, a in pairs]

    def begin(self):
        for cp in self.mine + self.first:
            cp.start()

    def relay(self):
        for arrived, onward in zip(self.over_ici, self.passed):
            arrived.wait_recv()
            onward.start()

    def finish(self):
        for cp in self.from_sibling:
            cp.wait_recv()
        for cp in self.first + self.passed:
            cp.wait_send()
        for cp in self.mine:
            cp.wait()


def _gather_scratch(na):
    return [pltpu.SemaphoreType.DMA((7, na)), pltpu.SemaphoreType.DMA((7, na)), pltpu.SemaphoreType.DMA((na,))]


def _gathered_shapes(shards):
    return [jax.ShapeDtypeStruct((a.shape[0], N_DEV) + a.shape[2:], a.dtype) for a in shards]


_RELATIONS = [(dx, dy, dc) for dx in (0, 1) for dy in (0, 1) for dc in (0, 1)][1:]


def _flip(v, d):
    return 1 - v if d else v


def _exchange_copies(srcs, dsts, send_sems, recv_sems, local_sems):
    x, y, c = _place()
    my = 4 * x + 2 * y + c
    na = len(srcs)
    mine = [pltpu.make_async_copy(srcs[a].at[pl.ds(my, 1)], dsts[a].at[pl.ds(my, 1)], local_sems.at[a])
            for a in range(na)]
    sends, recvs = [], []
    for k, (dx, dy, dc) in enumerate(_RELATIONS):
        peer = (_flip(x, dx), _flip(y, dy), _flip(c, dc))
        pidx = 4 * peer[0] + 2 * peer[1] + peer[2]
        for a in range(na):
            for into, out in ((my, sends), (pidx, recvs)):
                out.append(pltpu.make_async_remote_copy(
                    src_ref=srcs[a].at[pl.ds(pidx, 1)], dst_ref=dsts[a].at[pl.ds(into, 1)],
                    send_sem=send_sems.at[k, a], recv_sem=recv_sems.at[k, a], device_id=peer, device_id_type=MESH))
    return mine, sends, recvs


def _exchange_begin(copies):
    mine, sends, _ = copies
    for cp in mine + sends:
        cp.start()


def _exchange_finish(copies):
    mine, sends, recvs = copies
    for cp in recvs:
        cp.wait_recv()
    for cp in sends:
        cp.wait_send()
    for cp in mine:
        cp.wait()


def _exchange_scratch(na):
    return [pltpu.SemaphoreType.DMA((7, na)), pltpu.SemaphoreType.DMA((7, na)), pltpu.SemaphoreType.DMA((na,))]


def _host_exchange(body, n_in, n_out, travel, grid):
    nt = len(travel)
    if not nt:
        return body, [], [], [], []

    def wrapped(*refs):
        ins, srcs = refs[:n_in], refs[n_in:n_in + nt]
        outs, rest = refs[n_in + nt:n_in + nt + n_out], refs[n_in + nt + n_out:]
        dsts, scratch = rest[:nt], rest[nt:]
        copies = _exchange_copies(srcs, dsts, *scratch[-3:])
        first = last = None
        for axis, size in enumerate(grid):
            at_start, at_end = pl.program_id(axis) == 0, pl.program_id(axis) == size - 1
            first = at_start if first is None else jnp.logical_and(first, at_start)
            last = at_end if last is None else jnp.logical_and(last, at_end)

        @pl.when(first)
        def _():
            _exchange_begin(copies)

        body(*ins, *outs, *scratch[:-3])

        @pl.when(last)
        def _():
            _exchange_finish(copies)

    anyspec = pl.BlockSpec(memory_space=pl.ANY)
    return (wrapped, [anyspec] * nt, [anyspec] * nt, [jax.ShapeDtypeStruct(t.shape, t.dtype) for t in travel],
            _exchange_scratch(nt))


def _all_reduce_small(v, name):
    r, c_ = v.shape

    def body(v_ref, o_ref, gath, send_sems, recv_sems):
        x, y, c = _place()
        my = 4 * x + 2 * y + c
        gath[my] = v_ref[...]
        sends = []
        for k, (dx, dy, dc) in enumerate(_RELATIONS):
            peer = (_flip(x, dx), _flip(y, dy), _flip(c, dc))
            cp = pltpu.make_async_remote_copy(
                src_ref=v_ref, dst_ref=gath.at[my], send_sem=send_sems.at[k], recv_sem=recv_sems.at[k],
                device_id=peer, device_id_type=MESH)
            cp.start()
            sends.append((cp, 4 * peer[0] + 2 * peer[1] + peer[2], k, peer))
        for cp, pidx, k, peer in sends:
            pltpu.make_async_remote_copy(
                src_ref=v_ref, dst_ref=gath.at[pidx], send_sem=send_sems.at[k], recv_sem=recv_sems.at[k],
                device_id=peer, device_id_type=MESH).wait_recv()
        for cp, *_ in sends:
            cp.wait_send()
        tot = gath[0]
        for k in range(1, N_DEV):
            tot = tot + gath[k]
        o_ref[...] = tot

    vm = pl.BlockSpec(memory_space=pltpu.VMEM)
    return pl.pallas_call(
        body, name=name, in_specs=[vm], out_specs=vm,
        out_shape=jax.ShapeDtypeStruct((r, c_), F32),
        scratch_shapes=[pltpu.VMEM((N_DEV, r, c_), F32), pltpu.SemaphoreType.DMA((7,)),
                        pltpu.SemaphoreType.DMA((7,))],
    )(v)


TM = 512
TM_MATMUL = 2048
TM_RESIDUAL = 1024
TQ = 256


def _device_blocks(t):
    return t.reshape(N_DEV, -1, t.shape[-1])


def _pad_to(a, axis, size):
    pad = [(0, 0)] * a.ndim
    pad[axis] = (0, size - a.shape[axis])
    return jnp.pad(a, pad)


def _local_step(x, target, first_shards, late_shards, conv_shard, norm_mix, q_norm, k_norm, norm_ffn):
    depth, d = norm_mix.shape
    cols = first_shards[0].shape[3]
    tm, tq = min(TM, x.shape[0]), min(TQ, x.shape[0])
    tmm, tmr = min(TM_MATMUL, x.shape[0]), min(TM_RESIDUAL, x.shape[0])
    attn = d // 2
    nheads = attn // HEAD_DIM
    scale = HEAD_DIM ** -0.5 * LOG2E
    saved = []
    h1, (g_in0, g_conv) = _rmsnorm_fwd(x, norm_mix[0][None], tm, "norm_mix_fwd_0", first_shards)
    conv_full = g_conv[0, :, :depth * 3, :conv_shard].transpose(1, 0, 2).reshape(depth, 3, N_DEV * conv_shard)
    for l in range(depth):
        w_in = (g_in0, 0) if l == 0 else (g_rest, 3 * (l - 1))
        proj = _mm_blocks(h1, *w_in, tmm, f"proj_in_{l}")
        qk_gain = jnp.concatenate([jnp.tile(q_norm[l], nheads) * scale, jnp.tile(k_norm[l], nheads)])[None]
        qk = _qknorm_fwd(proj, qk_gain, tmm, f"qknorm_fwd_{l}")
        o, rtot, used, gathered = _attn_fwd(qk, proj, tq, f"attn_fwd_{l}", late_shards if l == 0 else ())
        if l == 0:
            g_gu0, g_rest, gb, gc = gathered if depth > 1 else (gathered[0], None, *gathered[1:])
            gb = gb.reshape(depth, -1, d)
            gc = gc.reshape(depth, -1, d)
        w_gu = (g_gu0, 0, 1) if l == 0 else (g_rest, 3 * (l - 1) + 1, 3 * (l - 1) + 2)
        conv_w8 = _pad_to(conv_full[l], 0, 8)
        cv = _conv_fwd(proj, conv_w8, f"conv_fwd_{l}")
        mix = jnp.concatenate([o, cv], axis=1)
        x1, h2 = _mm_residual_norm(mix, gb, l, x, norm_ffn[l][None], tmr, f"proj_out_{l}")
        g, u, act = _mm_swiglu(h2, *w_gu, tmr, f"ffn_up_{l}")
        saved.append((x, h1, proj, qk_gain, qk, rtot, used, conv_w8, mix, x1, h2, g, u, act, w_in, w_gu))
        if l + 1 < depth:
            x, h1 = _mm_residual_norm(act, gc, l, x1, norm_mix[l + 1][None], tm, f"ffn_down_{l}")
        else:
            dx, dxb, loss = _mm_residual_loss(act, gc, l, x1, target, tm, f"ffn_down_{l}")

    grads = [None] * depth
    small = [None] * depth
    landed = {}
    for l in reversed(range(depth)):
        x0, h1, proj, qk_gain, qk, rtot, used, conv_w8, mix, x1, h2, g, u, act, w_in, w_gu = saved[l]
        d = x0.shape[1]
        late = [[], [], [], []]
        if l == 0:
            for n, i in enumerate(5 * ll + j for ll in range(1, depth) for j in (0, 1, 2, 4, 3)):
                late[n % 4].append(i)

        def ride(host):
            return late[host], [_device_blocks(grads[i // 5][i % 5][1]) for i in late[host]]

        idx, travel = ride(0)
        dg, du, arrived = _mm_nt_swiglu_bwd(dxb, gc, l, g, u, tmr, f"ffn_down_bwd_{l}", travel)
        landed.update(zip(idx, arrived))
        idx, travel = ride(1)
        d_wdown, arrived = _mm_tn(act, dxb, 768, d, tmm, False, f"dw_down_{l}", travel)
        landed.update(zip(idx, arrived))
        idx, travel = ride(2)
        d_wgate, arrived = _mm_tn(h2, dg, d, cols, tmm, True, f"dw_gate_{l}", travel)
        landed.update(zip(idx, arrived))
        idx, travel = ride(3)
        d_wup, arrived = _mm_tn(h2, du, d, cols, tmm, True, f"dw_up_{l}", travel)
        landed.update(zip(idx, arrived))
        dx1, dx1b, dg_ffn, _ = _mm_nt_norm_bwd([dg, du], w_gu[0], list(w_gu[1:]), x1, norm_ffn[l][None], dx, tm,
                                               f"ffn_up_bwd_{l}")
        dmix = _mm_nt(dx1b, gb, l, tmr, 512, f"proj_out_bwd_{l}")
        d_wout, _ = _mm_tn(mix, dx1b, 512, d, tmm, False, f"dw_out_{l}")
        dcb, dcc, dcu, dconv = _conv_bwd(dmix, proj, conv_w8, f"conv_bwd_{l}")
        rides = {1: d_wgate[1], 2: d_wup[1], 3: d_wout[1], 4: d_wdown[1]} if l == 0 else {}
        dq, dk, dv, arrived = _attn_bwd(qk, proj, dmix, rtot, used, tq, f"attn_bwd_{l}",
                                        [_device_blocks(t) for t in rides.values()])
        landed.update(zip(rides.keys(), arrived))
        dqk, dg_qk = _qknorm_bwd(dq, dk, proj, qk_gain, tmm, f"qknorm_bwd_{l}")
        dproj = jnp.concatenate([dqk, dv.astype(BF16), dcb, dcc, dcu], axis=1)
        d_win, _ = _mm_tn(h1, dproj, d, cols, tmm, True, f"dw_in_{l}")
        dx, dxb, dg_mix, arrived = _mm_nt_norm_bwd(
            [dproj], w_in[0], [w_in[1]], x0, norm_mix[l][None], dx1, tm, f"proj_in_bwd_{l}",
            [_device_blocks(d_win[1])] if l == 0 else [])
        landed.update(zip([0], arrived))
        grads[l] = (d_win, d_wgate, d_wup, d_wout, d_wdown)
        dq_gain = jnp.sum(dg_qk[0, :attn].reshape(nheads, HEAD_DIM), axis=0) * scale
        dk_gain = jnp.sum(dg_qk[0, attn:].reshape(nheads, HEAD_DIM), axis=0)
        small[l] = (dg_mix[0], dg_ffn[0], dq_gain, dk_gain, dconv[:3])
    return loss, dx, grads, small, landed


def kernel(x, norm_mix, w_in, q_norm, k_norm, conv_w, w_out, norm_ffn, w_gate, w_up, w_down, loss_target, m_norm_mix, m_w_in, m_q_norm, m_k_norm, m_conv_w, m_w_out, m_norm_ffn, m_w_gate, m_w_up, m_w_down, v_norm_mix, v_w_in, v_q_norm, v_k_norm, v_conv_w, v_w_out, v_norm_ffn, v_w_gate, v_w_up, v_w_down):
    depth, d, in_shard = w_in.shape
    ff_shard = w_gate.shape[2]
    ff_pad = in_shard
    conv_shard = conv_w.shape[2]
    xs = x.reshape(x.shape[-2], d)
    target = loss_target.reshape(xs.shape)

    pa = jnp.stack([w_in, _pad_to(w_gate, 2, ff_pad), _pad_to(w_up, 2, ff_pad)], axis=1)
    pa = pa.reshape(3 * depth, 1, d, in_shard).astype(BF16)
    pd = _pad_to(_pad_to(conv_w.reshape(depth * 3, conv_shard), 0, 8), 1, LANES)[None, None]
    late_shards = [pa[1:3]] + ([pa[3:]] if depth > 1 else [])
    late_shards += [w_out.astype(BF16)[:, None], _pad_to(w_down, 1, ff_pad).astype(BF16)[:, None]]

    loss, grad_x, grads, small, landed = _local_step(xs, target, [pa[:1], pd], late_shards, conv_shard, norm_mix,
                                                     q_norm, k_norm, norm_ffn)

    x_, y_, c_ = _place()
    my = 4 * x_ + 2 * y_ + c_

    rows = []
    for l in range(depth):
        g_mix, g_ffn, g_q, g_k, g_conv = small[l]
        qkrow = _pad_to(jnp.concatenate([g_q, g_k]), 0, d)
        rows += [g_mix[None], g_ffn[None], qkrow[None], _pad_to(g_conv, 1, d)]
    nrow = 6 * depth
    packed = jnp.concatenate(rows + [_pad_to(loss[:1], 1, d)], axis=0)
    packed = _pad_to(packed, 0, ((nrow + 1 + 7) // 8) * 8)
    summed = _all_reduce_small(packed, "reduce_small")
    loss_out = summed[nrow, 0]

    my1 = my.astype(jnp.int32).reshape(1)

    def big(j, w, m, v, tr, name):
        return _adamw_sharded([landed[5 * l + j] for l in range(depth)],
                              [_device_blocks(grads[l][j][0]) for l in range(depth)], my1, w, m, v, tr, name)

    res = {"w_in": big(0, w_in, m_w_in, v_w_in, 256, "adamw_in"),
           "w_gate": big(1, w_gate, m_w_gate, v_w_gate, 256, "adamw_gate"),
           "w_up": big(2, w_up, m_w_up, v_w_up, 256, "adamw_up"),
           "w_out": big(3, w_out, m_w_out, v_w_out, w_out.shape[1], "adamw_out"),
           "w_down": big(4, w_down, m_w_down, v_w_down, ff_shard // 2, "adamw_down")}

    g_rows, w_rows, m_rows, v_rows = [], [], [], []
    for l in range(depth):
        base = l * 6
        conv_g = lax.dynamic_slice(summed[base + 3:base + 6], (0, my * conv_shard), (3, conv_shard))
        g_rows += [summed[base:base + 3], _pad_to(conv_g, 1, d)]
        for dst, (nm, qn, kn, nf, cw) in ((w_rows, (norm_mix, q_norm, k_norm, norm_ffn, conv_w)),
                                          (m_rows, (m_norm_mix, m_q_norm, m_k_norm, m_norm_ffn, m_conv_w)),
                                          (v_rows, (v_norm_mix, v_q_norm, v_k_norm, v_norm_ffn, v_conv_w))):
            dst += [nm[l][None], nf[l][None], _pad_to(jnp.concatenate([qn[l], kn[l]]), 0, d)[None],
                    _pad_to(cw[l], 1, d)]
    prow = ((nrow + 7) // 8) * 8
    gs, ws, ms, vs = [_pad_to(jnp.concatenate(t, axis=0), 0, prow) for t in (g_rows, w_rows, m_rows, v_rows)]
    sm = _adamw(gs, ws, ms, vs, "adamw_small")

    hd = q_norm.shape[1]

    def small_out(t, kind):
        per_layer = []
        for l in range(depth):
            base = l * 6
            per_layer.append({"norm_mix": t[base], "norm_ffn": t[base + 1], "q_norm": t[base + 2, :hd],
                              "k_norm": t[base + 2, hd:2 * hd], "conv_w": t[base + 3:base + 6, :conv_shard]}[kind])
        return jnp.stack(per_layer)

    def big_out(name, i):
        return res[name][i]

    outs = [loss_out, grad_x.reshape(x.shape)]
    for i in range(4):
        outs += [small_out(sm[i], "norm_mix"), big_out("w_in", i), small_out(sm[i], "q_norm"),
                 small_out(sm[i], "k_norm"), small_out(sm[i], "conv_w"), big_out("w_out", i),
                 small_out(sm[i], "norm_ffn"), big_out("w_gate", i), big_out("w_up", i), big_out("w_down", i)]
    return tuple(outs)
```

```python
import jax
import jax.numpy as jnp
from jax import lax
from jax.experimental import pallas as pl
from jax.experimental.pallas import tpu as pltpu

F32 = jnp.float32
BF16 = jnp.bfloat16
MESH = pl.DeviceIdType.MESH

N_DEV = 8
LANES = 128
HEAD_DIM = 64
KEY_CHUNK = 128
EPS = 1e-6
VMEM_LIMIT = 48 * 1024 * 1024

ADAM_LR = 0.001
ADAM_B1 = 0.9
ADAM_B2 = 0.999
ADAM_EPS = 1e-08
ADAM_WD = 0.01
ADAM_STEP = 10

NN = (((1,), (0,)), ((), ()))
NT = (((1,), (1,)), ((), ()))
TN = (((0,), (0,)), ((), ()))


def _dot(a, b, dims):
    return lax.dot_general(a.astype(BF16), b.astype(BF16), dims, preferred_element_type=F32)


def _cparams(*sem):
    return pltpu.CompilerParams(dimension_semantics=sem, vmem_limit_bytes=VMEM_LIMIT)


def _rmsnorm_fwd(x, gain, tm, name, shards=()):
    s, d = x.shape
    nsteps = s // tm
    ng = len(shards)

    def body(*refs):
        x_ref, g_ref, srcs = refs[0], refs[1], refs[2:2 + ng]
        o_ref, dsts, sems = refs[2 + ng], refs[3 + ng:3 + 2 * ng], refs[3 + 2 * ng:]
        i = pl.program_id(0)
        gather = _Gather(srcs, dsts, *sems) if ng else None
        if ng:
            @pl.when(i == 0)
            def _():
                gather.begin()

        xv = x_ref[...]
        r = lax.rsqrt(jnp.mean(xv * xv, axis=-1, keepdims=True) + EPS)
        o_ref[...] = ((xv * r) * g_ref[...]).astype(o_ref.dtype)
        if ng:
            @pl.when(i == nsteps - 1)
            def _():
                gather.relay()
                gather.finish()

    anyspec = pl.BlockSpec(memory_space=pl.ANY)
    outs = pl.pallas_call(
        body, name=name, grid=(nsteps,),
        in_specs=[pl.BlockSpec((tm, d), lambda i: (i, 0)), pl.BlockSpec((1, d), lambda i: (0, 0))] + [anyspec] * ng,
        out_specs=[pl.BlockSpec((tm, d), lambda i: (i, 0))] + [anyspec] * ng,
        out_shape=[jax.ShapeDtypeStruct((s, d), BF16)] + _gathered_shapes(shards),
        scratch_shapes=_gather_scratch(ng) if ng else [],
        compiler_params=_cparams("arbitrary"),
    )(x, gain, *shards)
    return outs[0], list(outs[1:])


def _group_mean_matrix():
    r = lax.broadcasted_iota(jnp.int32, (LANES, LANES), 0) // HEAD_DIM
    c = lax.broadcasted_iota(jnp.int32, (LANES, LANES), 1) // HEAD_DIM
    return jnp.where(r == c, 1.0 / HEAD_DIM, 0.0).astype(BF16)


def _group_mean(v, gm):
    hi = v.astype(BF16)
    lo = (v - hi.astype(F32)).astype(BF16)
    return _dot(hi, gm, NN) + _dot(lo, gm, NN)


def _qknorm_fwd(proj, gains, tm, name):
    s = proj.shape[0]
    ncol = gains.shape[1] // LANES

    def body(p_ref, g_ref, gm_ref, o_ref):
        xv = p_ref[...].astype(F32)
        r = lax.rsqrt(_group_mean(xv * xv, gm_ref[...]) + EPS)
        o_ref[...] = ((xv * r) * g_ref[...]).astype(o_ref.dtype)

    blk = pl.BlockSpec((tm, LANES), lambda i, j: (i, j))
    return pl.pallas_call(
        body, name=name, grid=(s // tm, ncol),
        in_specs=[blk, pl.BlockSpec((1, LANES), lambda i, j: (0, j)),
                  pl.BlockSpec((LANES, LANES), lambda i, j: (0, 0))],
        out_specs=blk,
        out_shape=jax.ShapeDtypeStruct((s, ncol * LANES), BF16),
        compiler_params=_cparams("parallel", "parallel"),
    )(proj, gains, _group_mean_matrix())


def _qknorm_bwd(dq, dk, proj, gains, tm, name):
    s = proj.shape[0]
    ncol = gains.shape[1] // LANES
    half = ncol // 2
    nsteps = s // tm

    def body(dq_ref, dk_ref, p_ref, g_ref, gm_ref, dx_ref, dg_ref):
        i = pl.program_id(1)
        gm = gm_ref[...]
        xv = p_ref[...].astype(F32)
        r = lax.rsqrt(_group_mean(xv * xv, gm) + EPS)
        xhat = xv * r
        dy = jnp.where(pl.program_id(0) < half, dq_ref[...], dk_ref[...])
        dxh = dy * g_ref[...]
        proj_ = _group_mean(dxh * xhat, gm)
        dx_ref[...] = (r * (dxh - xhat * proj_)).astype(dx_ref.dtype)
        part = jnp.sum((dy * xhat).reshape(tm // 8, 8, LANES), axis=0)

        @pl.when(i == 0)
        def _():
            dg_ref[...] = part

        @pl.when(i > 0)
        def _():
            dg_ref[...] += part

        @pl.when(i == nsteps - 1)
        def _():
            dg_ref[...] = jnp.broadcast_to(jnp.sum(dg_ref[...], axis=0, keepdims=True), (8, LANES))

    blk = pl.BlockSpec((tm, LANES), lambda j, i: (i, j))
    return pl.pallas_call(
        body, name=name, grid=(ncol, nsteps),
        in_specs=[pl.BlockSpec((tm, LANES), lambda j, i: (i, jnp.minimum(j, half - 1))),
                  pl.BlockSpec((tm, LANES), lambda j, i: (i, jnp.maximum(j - half, 0))),
                  blk, pl.BlockSpec((1, LANES), lambda j, i: (0, j)),
                  pl.BlockSpec((LANES, LANES), lambda j, i: (0, 0))],
        out_specs=[blk, pl.BlockSpec((8, LANES), lambda j, i: (0, j))],
        out_shape=[jax.ShapeDtypeStruct((s, ncol * LANES), BF16),
                   jax.ShapeDtypeStruct((8, ncol * LANES), F32)],
        compiler_params=_cparams("parallel", "arbitrary"),
    )(dq, dk, proj, gains, _group_mean_matrix())


CONV_ROWS = 256
HALO = 8


def _conv_fwd(proj, conv_w8, mix, name):
    s = proj.shape[0]
    nblk = conv_w8.shape[1] // LANES
    first = 3 * nblk
    nchunk = s // CONV_ROWS
    before = mix.shape[1] // LANES - nblk

    def body(cb_ref, cc_ref, cu_ref, w_ref, mix_ref, y_ref, hpad):
        del mix_ref
        hpad[pl.ds(0, 2 * HALO), :] = jnp.zeros((2 * HALO, LANES), F32)

        def fill(i, _):
            r0 = pl.multiple_of(i * CONV_ROWS, CONV_ROWS)
            hpad[pl.ds(r0 + 2 * HALO, CONV_ROWS), :] = (
                cc_ref[pl.ds(r0, CONV_ROWS), :].astype(F32) * cu_ref[pl.ds(r0, CONV_ROWS), :].astype(F32))
            return 0

        lax.fori_loop(0, nchunk, fill, 0)
        w0, w1, w2 = w_ref[0:1, :], w_ref[1:2, :], w_ref[2:3, :]

        def conv(i, _):
            r0 = pl.multiple_of(i * CONV_ROWS, CONV_ROWS)
            win = hpad[pl.ds(r0 + HALO, CONV_ROWS + HALO), :]
            c = (w2 * win[HALO:] + w1 * pltpu.roll(win, 1, 0)[HALO:] + w0 * pltpu.roll(win, 2, 0)[HALO:])
            y_ref[pl.ds(r0, CONV_ROWS), :] = (cb_ref[pl.ds(r0, CONV_ROWS), :].astype(F32) * c).astype(y_ref.dtype)
            return 0

        lax.fori_loop(0, nchunk, conv, 0)

    def col(off):
        return pl.BlockSpec((s, LANES), lambda j: (0, off + j))

    return pl.pallas_call(
        body, name=name, grid=(nblk,),
        in_specs=[col(first), col(first + nblk), col(first + 2 * nblk), pl.BlockSpec((8, LANES), lambda j: (0, j)),
                  pl.BlockSpec(memory_space=pl.ANY)],
        out_specs=pl.BlockSpec((s, LANES), lambda j: (0, before + j)),
        out_shape=jax.ShapeDtypeStruct(mix.shape, mix.dtype),
        scratch_shapes=[pltpu.VMEM((s + 2 * HALO, LANES), F32)],
        input_output_aliases={4: 0},
        compiler_params=_cparams("parallel"),
    )(proj, proj, proj, conv_w8, mix)


def _conv_bwd(dmix, proj, conv_w8, name):
    s = proj.shape[0]
    nblk = conv_w8.shape[1] // LANES
    first = 3 * nblk
    nchunk = s // CONV_ROWS

    def body(dy_ref, cb_ref, cc_ref, cu_ref, w_ref, dcb_ref, dcc_ref, dcu_ref, dw_ref, hpad, dcpad):
        hpad[pl.ds(0, 2 * HALO), :] = jnp.zeros((2 * HALO, LANES), F32)
        dcpad[pl.ds(s, 2 * HALO), :] = jnp.zeros((2 * HALO, LANES), F32)

        def fill(i, _):
            r0 = pl.multiple_of(i * CONV_ROWS, CONV_ROWS)
            hpad[pl.ds(r0 + 2 * HALO, CONV_ROWS), :] = (
                cc_ref[pl.ds(r0, CONV_ROWS), :].astype(F32) * cu_ref[pl.ds(r0, CONV_ROWS), :].astype(F32))
            return 0

        lax.fori_loop(0, nchunk, fill, 0)
        w0, w1, w2 = w_ref[0:1, :], w_ref[1:2, :], w_ref[2:3, :]

        def fold(v):
            return jnp.sum(v.reshape(CONV_ROWS // 8, 8, LANES), axis=0)

        def first_pass(i, acc):
            a0, a1, a2 = acc
            r0 = pl.multiple_of(i * CONV_ROWS, CONV_ROWS)
            win = hpad[pl.ds(r0 + HALO, CONV_ROWS + HALO), :]
            h0 = win[HALO:]
            h1 = pltpu.roll(win, 1, 0)[HALO:]
            h2 = pltpu.roll(win, 2, 0)[HALO:]
            c = w2 * h0 + w1 * h1 + w0 * h2
            dy = dy_ref[pl.ds(r0, CONV_ROWS), :]
            dcb_ref[pl.ds(r0, CONV_ROWS), :] = (dy * c).astype(dcb_ref.dtype)
            dc = dy * cb_ref[pl.ds(r0, CONV_ROWS), :].astype(F32)
            dcpad[pl.ds(r0, CONV_ROWS), :] = dc
            return a0 + fold(dc * h2), a1 + fold(dc * h1), a2 + fold(dc * h0)

        z8 = jnp.zeros((8, LANES), F32)
        a0, a1, a2 = lax.fori_loop(0, nchunk, first_pass, (z8, z8, z8))
        dw_ref[...] = jnp.concatenate(
            [jnp.sum(a0, axis=0, keepdims=True), jnp.sum(a1, axis=0, keepdims=True),
             jnp.sum(a2, axis=0, keepdims=True), jnp.zeros((5, LANES), F32)], axis=0)

        def second_pass(i, _):
            r0 = pl.multiple_of(i * CONV_ROWS, CONV_ROWS)
            win = dcpad[pl.ds(r0, CONV_ROWS + HALO), :]
            n = CONV_ROWS + HALO
            dh = (w2 * win[:CONV_ROWS] + w1 * pltpu.roll(win, n - 1, 0)[:CONV_ROWS]
                  + w0 * pltpu.roll(win, n - 2, 0)[:CONV_ROWS])
            dcc_ref[pl.ds(r0, CONV_ROWS), :] = (dh * cu_ref[pl.ds(r0, CONV_ROWS), :].astype(F32)).astype(dcc_ref.dtype)
            dcu_ref[pl.ds(r0, CONV_ROWS), :] = (dh * cc_ref[pl.ds(r0, CONV_ROWS), :].astype(F32)).astype(dcu_ref.dtype)
            return 0

        lax.fori_loop(0, nchunk, second_pass, 0)

    def col(off):
        return pl.BlockSpec((s, LANES), lambda j: (0, off + j))

    out = pl.BlockSpec((s, LANES), lambda j: (0, j))
    return pl.pallas_call(
        body, name=name, grid=(nblk,),
        in_specs=[col(nblk), col(first), col(first + nblk), col(first + 2 * nblk),
                  pl.BlockSpec((8, LANES), lambda j: (0, j))],
        out_specs=[out, out, out, pl.BlockSpec((8, LANES), lambda j: (0, j))],
        out_shape=[jax.ShapeDtypeStruct((s, nblk * LANES), BF16)] * 3 + [jax.ShapeDtypeStruct((8, nblk * LANES), F32)],
        scratch_shapes=[pltpu.VMEM((s + 2 * HALO, LANES), F32), pltpu.VMEM((s + 2 * HALO, LANES), F32)],
        compiler_params=_cparams("parallel"),
    )(dmix, proj, proj, proj, conv_w8)


LOG2E = 1.4426950408889634
LN2 = 0.6931471805599453
NEG_BIG = -1e30
SATURATED = 160.0


def _cumsum_matrix(kind):
    j = lax.broadcasted_iota(jnp.int32, (KEY_CHUNK, 2 * KEY_CHUNK), 0)
    c = lax.broadcasted_iota(jnp.int32, (KEY_CHUNK, 2 * KEY_CHUNK), 1)
    tri = {"after": j > c, "upto": j <= c, "before": j < c}[kind]
    return jnp.where((c >= KEY_CHUNK) | tri, 1.0, 0.0).astype(BF16)


def _stack_heads(t, m0):
    zero = jnp.zeros_like(t)
    return jnp.concatenate([jnp.where(m0, t, zero), jnp.where(m0, zero, t)], axis=0)


def _softplus2(z):
    sp = jnp.maximum(z, 0.0) + jnp.log2(1.0 + jnp.exp2(-jnp.abs(z)))
    return sp, z - sp


def _key_chunk(ref, kc):
    return ref[pl.ds(pl.multiple_of(kc * KEY_CHUNK, KEY_CHUNK), KEY_CHUNK), :]


def _attn_bwd(qk, proj, dmix, rtot, used, tq, name, travel=()):
    s = qk.shape[0]
    nhp = qk.shape[1] // (2 * LANES)
    nc = tq // KEY_CHUNK
    nq = s // tq
    nt = len(travel)

    def body(used_ref, q_ref, k_ref, v_ref, do_ref, r_ref, cmi_ref, cme_ref, bias_ref, dq_ref, dk_ref, dv_ref,
             z_refs, ls_refs, sig_refs, sp_refs, gb_refs, pr_ref, gs_ref, copies):
        qi = pl.program_id(1)

        @pl.when(qi == 0)
        def _():
            dk_ref[...] = jnp.zeros_like(dk_ref)
            dv_ref[...] = jnp.zeros_like(dv_ref)

        if copies is not None:
            @pl.when(jnp.logical_and(pl.program_id(0) == 0, qi == 0))
            def _():
                _exchange_begin(copies)

        nslots = (qi + 1) * nc
        walked = used_ref[pl.program_id(0), qi].astype(jnp.int32)
        first = jnp.clip(nslots - walked, 0, nslots - nc) // nc * nc
        m0 = lax.broadcasted_iota(jnp.int32, (1, LANES), 1) < HEAD_DIM
        qs = _stack_heads(q_ref[...], m0)
        do = do_ref[...]
        dos = _stack_heads(do.astype(BF16), m0)
        dosl = _stack_heads((do * LN2).astype(BF16), m0)
        cmi = cmi_ref[...]
        cme = cme_ref[...]

        def chunk_at(i):
            return jnp.clip(i, first, nslots - 1)

        def scores(kc):
            return _dot(qs, _key_chunk(k_ref, kc), NT)

        def weights(ls, cs, da, pr, kc):
            a = jnp.exp2(ls - (pr - cs[:, :KEY_CHUNK]))
            gb = (a * da).astype(BF16)
            ks = pl.multiple_of(kc * KEY_CHUNK, KEY_CHUNK)
            dv_ref[pl.ds(ks, KEY_CHUNK), :] += _dot(a, dos, TN)
            return gb, jnp.exp2(ls), pr - cs[:, KEY_CHUNK:]

        def score_grads(gb, sig, cg, gs, dq, kc):
            dzb = (gb.astype(F32) * (1.0 - sig) - sig * (gs + cg[:, :KEY_CHUNK])).astype(BF16)
            ks = pl.multiple_of(kc * KEY_CHUNK, KEY_CHUNK)
            dk_ref[pl.ds(ks, KEY_CHUNK), :] += _dot(dzb, qs, TN)
            dq = dq + _dot(jnp.concatenate([dzb[:tq], dzb[tq:]], axis=1), _stack_heads(_key_chunk(k_ref, kc), m0), NN)
            return gs + cg[:, KEY_CHUNK:], dq

        def step(i, par, bias=None, stages="zswg"):
            cur, prv = par, 1 - par
            k1, k2 = chunk_at(i - 1), chunk_at(i - 2)
            z_next = scores(chunk_at(i + 1))
            if "w" in stages:
                cs = _dot(sp_refs[prv][...], cmi, NN)
                da = _dot(dosl, _key_chunk(v_ref, k1), NT)
            if "g" in stages:
                cg = _dot(gb_refs[cur][...], cme, NN)
            z = z_refs[cur][...]
            if bias is not None:
                z = z + bias
            sp, ls = _softplus2(z)
            sp_refs[cur][...] = sp.astype(BF16)
            ls_refs[cur][...] = ls
            if "g" in stages:
                gs, dq = score_grads(gb_refs[cur][...], sig_refs[cur][...], cg, gs_ref[...], dq_ref[...], k2)
                gs_ref[...] = gs
                dq_ref[...] = dq
            if "w" in stages:
                gb, sig, pr = weights(ls_refs[prv][...], cs, da, pr_ref[...], k1)
                gb_refs[prv][...] = gb
                sig_refs[prv][...] = sig
                pr_ref[...] = pr
            z_refs[prv][...] = z_next

        pr_ref[...] = jnp.concatenate([r_ref[:, :LANES], r_ref[:, LANES:]], axis=0)
        gs_ref[...] = jnp.zeros((2 * tq, LANES), F32)
        dq_ref[...] = jnp.zeros((tq, LANES), F32)
        z_refs[0][...] = scores(first)
        only_diagonal = first == nslots - nc
        step(first, 0, jnp.where(only_diagonal, bias_ref[0], 0.0), stages="zs")
        step(first + 1, 1, jnp.where(only_diagonal, bias_ref[1], 0.0), stages="zsw")

        def two_steps(j, _):
            step(2 * j, 0)
            step(2 * j + 1, 1)
            return 0

        lax.fori_loop(first // 2 + 1, nslots // 2 - 1, two_steps, 0)

        @pl.when(jnp.logical_not(only_diagonal))
        def _():
            step(nslots - 2, 0, bias_ref[0])
            step(nslots - 1, 1, bias_ref[1])

        k1, k2 = chunk_at(nslots - 1), chunk_at(nslots - 2)
        gb, sig, _ = weights(ls_refs[1][...], _dot(sp_refs[1][...], cmi, NN),
                             _dot(dosl, _key_chunk(v_ref, k1), NT), pr_ref[...], k1)
        gb2 = gb_refs[0][...]
        gs, dq = score_grads(gb2, sig_refs[0][...], _dot(gb2, cme, NN), gs_ref[...], dq_ref[...], k2)
        _, dq = score_grads(gb, sig, _dot(gb, cme, NN), gs, dq, k1)
        dq_ref[...] = dq

        if copies is not None:
            @pl.when(jnp.logical_and(pl.program_id(0) == nhp - 1, qi == nq - 1))
            def _():
                _exchange_finish(copies)

    def wrapped(*refs):
        ins, rest = refs[:9], refs[9:]
        srcs, rest = rest[:nt], rest[nt:]
        outs, rest = rest[:3], rest[3:]
        lands, rest = rest[:nt], rest[nt:]
        z0, z1, ls0, ls1, sg0, sg1, sp0, sp1, gb0, gb1, pr_ref, gs_ref = rest[:12]
        copies = _exchange_copies(srcs, lands, *rest[12:]) if nt else None
        body(*ins, *outs, (z0, z1), (ls0, ls1), (sg0, sg1), (sp0, sp1), (gb0, gb1), pr_ref, gs_ref, copies)

    assert nc == 2
    bias = _diag_bias(tq, True)
    bias = jnp.concatenate([bias[:, :, :KEY_CHUNK], bias[:, :, KEY_CHUNK:]], axis=1)
    qblk = pl.BlockSpec((tq, LANES), lambda p, i: (i, p))
    full = pl.BlockSpec((s, LANES), lambda p, i: (0, p))
    cmspec = pl.BlockSpec((KEY_CHUNK, 2 * KEY_CHUNK), lambda p, i: (0, 0))
    anyspec = pl.BlockSpec(memory_space=pl.ANY)
    shape = jax.ShapeDtypeStruct((s, nhp * LANES), F32)
    f32buf = pltpu.VMEM((2 * tq, LANES), F32)
    bf16buf = pltpu.VMEM((2 * tq, LANES), BF16)
    outs = pl.pallas_call(
        wrapped, name=name, grid=(nhp, nq),
        in_specs=[pl.BlockSpec(memory_space=pltpu.SMEM),
                  qblk,
                  pl.BlockSpec((s, LANES), lambda p, i: (0, nhp + p)),
                  pl.BlockSpec((s, LANES), lambda p, i: (0, 2 * nhp + p)),
                  qblk,
                  pl.BlockSpec((tq, 2 * LANES), lambda p, i: (i, p)),
                  cmspec, cmspec,
                  pl.BlockSpec((nc, 2 * tq, LANES), lambda p, i: (0, 0, 0))] + [anyspec] * nt,
        out_specs=[qblk, full, full] + [anyspec] * nt,
        out_shape=[shape, shape, shape] + [jax.ShapeDtypeStruct(t.shape, t.dtype) for t in travel],
        scratch_shapes=[f32buf] * 6 + [bf16buf] * 4 + [f32buf] * 2 + (_exchange_scratch(nt) if nt else []),
        compiler_params=_cparams("arbitrary", "arbitrary"),
    )(used, qk, qk, proj, dmix, rtot, _cumsum_matrix("upto"), _cumsum_matrix("before"), bias, *travel)
    return outs[0], outs[1], outs[2], list(outs[3:])


def _pair_cumsum_matrix(kind):
    j = lax.broadcasted_iota(jnp.int32, (2 * KEY_CHUNK, 4 * KEY_CHUNK), 0)
    c = lax.broadcasted_iota(jnp.int32, (2 * KEY_CHUNK, 4 * KEY_CHUNK), 1)
    same_head = (j // KEY_CHUNK) == ((c // KEY_CHUNK) % 2)
    jj, cc = j % KEY_CHUNK, c % KEY_CHUNK
    tri = {"after": jj > cc, "upto": jj <= cc, "before": jj < cc}[kind]
    return jnp.where(same_head & ((c >= 2 * KEY_CHUNK) | tri), 1.0, 0.0).astype(BF16)


def _diag_bias(tq, ascending):
    nc = tq // KEY_CHUNK
    shape = (nc, tq, 2 * KEY_CHUNK)
    d = lax.broadcasted_iota(jnp.int32, shape, 0)
    r = lax.broadcasted_iota(jnp.int32, shape, 1)
    c = lax.broadcasted_iota(jnp.int32, shape, 2) % KEY_CHUNK
    chunk = d if ascending else nc - 1 - d
    return jnp.where(chunk * KEY_CHUNK + c < r, 0.0, NEG_BIG).astype(F32)


def _attn_fwd(qk, proj, tq, name, mix_cols, shards=()):
    s = qk.shape[0]
    nhp = qk.shape[1] // (2 * LANES)
    nc = tq // KEY_CHUNK
    nq = s // tq
    ng = len(shards)
    assert nc == 2
    w = 2 * KEY_CHUNK

    def body(q_ref, k_ref, v_ref, cm_ref, bias_ref, o_ref, r_ref, used_ref, z_refs, ls_refs, cs_refs, ct_refs,
             sp_refs, ab_refs, acc_ref, gather):
        qi = pl.program_id(1)
        if gather is not None:
            @pl.when(jnp.logical_and(pl.program_id(0) == 0, qi == 0))
            def _():
                gather.begin()

        nslots = (qi + 1) * nc
        m0 = lax.broadcasted_iota(jnp.int32, (1, LANES), 1) < HEAD_DIM
        q = q_ref[...]
        cm = cm_ref[...]

        def chunk_at(i):
            return jnp.clip(nslots - 1 - i, 0, nslots - 1)

        def scores(kc):
            return _dot(q, _stack_heads(_key_chunk(k_ref, kc), m0), NT)

        def values(ab, kc):
            return _dot(ab, _stack_heads(_key_chunk(v_ref, kc), m0), NN)

        def step(i, par, bias=None, stages="zscwv"):
            cur, prv = par, 1 - par
            if "z" in stages:
                z_next = scores(chunk_at(i + 1))
            if "c" in stages:
                cs = _dot(sp_refs[prv][...], cm, NN)
            if "v" in stages:
                pv = values(ab_refs[prv][...], chunk_at(i - 3))
            if "w" in stages:
                rs = r_ref[...]
                r_ref[...] = rs + ct_refs[cur][...]
                ab_refs[cur][...] = jnp.exp2(ls_refs[cur][...] - cs_refs[cur][...] - rs).astype(BF16)
            if "s" in stages:
                z = z_refs[cur][...]
                if bias is not None:
                    z = z + bias
                sp, ls = _softplus2(z)
                sp_refs[cur][...] = sp.astype(BF16)
                ls_refs[cur][...] = ls
            if "v" in stages:
                acc_ref[...] += pv
            if "c" in stages:
                cs_refs[prv][...] = cs[:, :w]
                ct_refs[prv][...] = cs[:, w:]
            if "z" in stages:
                z_refs[prv][...] = z_next

        z_refs[0][...] = scores(chunk_at(0))
        ab_refs[1][...] = jnp.zeros((tq, w), BF16)
        r_ref[...] = jnp.zeros((tq, w), F32)
        acc_ref[...] = jnp.zeros((tq, LANES), F32)
        step(0, 0, bias_ref[0], stages="zs")
        step(1, 1, bias_ref[1], stages="zsc")

        def two_steps(carry):
            j, _ = carry
            step(2 * j, 0)
            step(2 * j + 1, 1)
            return j + 1, jnp.min(jnp.minimum(r_ref[:, :KEY_CHUNK], r_ref[:, KEY_CHUNK:]))

        pairs, low = lax.while_loop(lambda c: jnp.logical_and(c[0] < nslots // 2, c[1] < SATURATED), two_steps,
                                    (jnp.int32(1), jnp.float32(0.0)))
        entered = 2 * pairs
        saturated = low >= SATURATED

        @pl.when(saturated)
        def _():
            step(entered, 0, stages="v")

        @pl.when(jnp.logical_not(saturated))
        def _():
            step(entered, 0, stages="cwv")
            step(entered + 1, 1, stages="wv")
            step(entered + 2, 0, stages="v")

        o_ref[...] = acc_ref[...].astype(o_ref.dtype)
        used_ref[pl.program_id(0), qi] = jnp.where(saturated, entered - 2, entered).astype(F32)

        if gather is not None:
            @pl.when(jnp.logical_and(pl.program_id(0) == nhp - 1, qi == nq // 2))
            def _():
                gather.relay()

            @pl.when(jnp.logical_and(pl.program_id(0) == nhp - 1, qi == nq - 1))
            def _():
                gather.finish()

    def wrapped(*refs):
        ins, rest = refs[:5], refs[5:]
        srcs, rest = rest[:ng], rest[ng:]
        outs, rest = rest[:3], rest[3:]
        dsts, scratch = rest[:ng], rest[ng:]
        z, ls, cs, ct, sp, ab = [scratch[2 * j:2 * j + 2] for j in range(6)]
        gather = _Gather(srcs, dsts, *scratch[13:]) if ng else None
        body(*ins, *outs, z, ls, cs, ct, sp, ab, scratch[12], gather)

    f32buf = pltpu.VMEM((tq, w), F32)
    bf16buf = pltpu.VMEM((tq, w), BF16)
    anyspec = pl.BlockSpec(memory_space=pl.ANY)
    outs = pl.pallas_call(
        wrapped, name=name, grid=(nhp, nq),
        in_specs=[pl.BlockSpec((tq, LANES), lambda p, i: (i, p)),
                  pl.BlockSpec((s, LANES), lambda p, i: (0, nhp + p)),
                  pl.BlockSpec((s, LANES), lambda p, i: (0, 2 * nhp + p)),
                  pl.BlockSpec((w, 2 * w), lambda p, i: (0, 0)),
                  pl.BlockSpec((nc, tq, w), lambda p, i: (0, 0, 0))] + [anyspec] * ng,
        out_specs=[pl.BlockSpec((tq, LANES), lambda p, i: (i, p)),
                   pl.BlockSpec((tq, w), lambda p, i: (i, p)),
                   pl.BlockSpec(memory_space=pltpu.SMEM)] + [anyspec] * ng,
        out_shape=[jax.ShapeDtypeStruct((s, mix_cols), BF16),
                   jax.ShapeDtypeStruct((s, nhp * w), F32),
                   jax.ShapeDtypeStruct((nhp, nq), F32)] + _gathered_shapes(shards),
        scratch_shapes=([f32buf] * 8 + [bf16buf] * 4 + [pltpu.VMEM((tq, LANES), F32)]
                        + (_gather_scratch(ng) if ng else [])),
        compiler_params=_cparams("arbitrary", "arbitrary"),
    )(qk, qk, proj, _pair_cumsum_matrix("after"), _diag_bias(tq, False), *shards)
    return outs[0], outs[1], outs[2], list(outs[3:])


BLOCK_PAIR = 2
MXU_WIDTH = 256


def _side_by_side(b_ref):
    return jnp.concatenate([b_ref[p] for p in range(BLOCK_PAIR)], axis=1)


def _mm_blocks(h, ga, widx, tm, name):
    s, d = h.shape
    nb, cols = ga.shape[1], ga.shape[3]

    def body(a_ref, b_ref, o_ref):
        o_ref[...] = _dot(a_ref[...], _side_by_side(b_ref), NN).astype(o_ref.dtype)

    return pl.pallas_call(
        body, name=name, grid=(s // tm, nb // BLOCK_PAIR),
        in_specs=[pl.BlockSpec((tm, d), lambda i, j: (i, 0)),
                  pl.BlockSpec((None, BLOCK_PAIR, d, cols), lambda i, j: (widx, j, 0, 0))],
        out_specs=pl.BlockSpec((tm, BLOCK_PAIR * cols), lambda i, j: (i, j)),
        out_shape=jax.ShapeDtypeStruct((s, nb * cols), BF16),
        compiler_params=_cparams("parallel", "arbitrary"),
    )(h, ga)


def _mm_swiglu(h, ga, gidx, uidx, tm, name):
    s, d = h.shape
    nb, cols = ga.shape[1], ga.shape[3]

    def body(a_ref, bg_ref, bu_ref, g_ref, u_ref, act_ref):
        a = a_ref[...]
        g = _dot(a, _side_by_side(bg_ref), NN)
        u = _dot(a, _side_by_side(bu_ref), NN)
        g_ref[...] = g.astype(g_ref.dtype)
        u_ref[...] = u.astype(u_ref.dtype)
        act_ref[...] = (g * (1.0 / (1.0 + jnp.exp(-g))) * u).astype(act_ref.dtype)

    def wspec(idx):
        return pl.BlockSpec((None, BLOCK_PAIR, d, cols), lambda i, j: (idx, j, 0, 0))

    out = pl.BlockSpec((tm, BLOCK_PAIR * cols), lambda i, j: (i, j))
    shape = jax.ShapeDtypeStruct((s, nb * cols), BF16)
    return pl.pallas_call(
        body, name=name, grid=(s // tm, nb // BLOCK_PAIR),
        in_specs=[pl.BlockSpec((tm, d), lambda i, j: (i, 0)), wspec(gidx), wspec(uidx)],
        out_specs=[out, out, out], out_shape=[shape, shape, shape],
        compiler_params=_cparams("parallel", "arbitrary"),
    )(h, ga, ga)


def _mm_residual_norm(a, w3, lidx, res, gain, tm, name):
    s, k = a.shape
    n = w3.shape[2]

    def body(a_ref, b_ref, r_ref, g_ref, o_ref, h_ref):
        xv = r_ref[...] + _dot(a_ref[...], b_ref[...], NN)
        o_ref[...] = xv
        r = lax.rsqrt(jnp.mean(xv * xv, axis=-1, keepdims=True) + EPS)
        h_ref[...] = ((xv * r) * g_ref[...]).astype(h_ref.dtype)

    row = pl.BlockSpec((tm, n), lambda i: (i, 0))
    return pl.pallas_call(
        body, name=name, grid=(s // tm,),
        in_specs=[pl.BlockSpec((tm, k), lambda i: (i, 0)),
                  pl.BlockSpec((None, k, n), lambda i: (lidx, 0, 0), pipeline_mode=pl.Buffered(1)),
                  row, pl.BlockSpec((1, n), lambda i: (0, 0))],
        out_specs=[row, row],
        out_shape=[jax.ShapeDtypeStruct((s, n), F32), jax.ShapeDtypeStruct((s, n), BF16)],
        compiler_params=_cparams("parallel"),
    )(a, w3, res, gain)


def _mm_residual_loss(a, w3, lidx, res, target, tm, name):
    s, k = a.shape
    n = w3.shape[2]
    nsteps = s // tm

    def body(a_ref, b_ref, r_ref, t_ref, dy_ref, dyb_ref, l_ref, acc):
        i = pl.program_id(0)
        diff = r_ref[...] + _dot(a_ref[...], b_ref[...], NN) - t_ref[...]
        dy_ref[...] = diff * (1.0 / n)
        dyb_ref[...] = (diff * (1.0 / n)).astype(dyb_ref.dtype)
        part = jnp.sum((diff * diff).reshape(tm // 8, 8, n), axis=0)

        @pl.when(i == 0)
        def _():
            acc[...] = part

        @pl.when(i > 0)
        def _():
            acc[...] += part

        @pl.when(i == nsteps - 1)
        def _():
            tot = jnp.sum(jnp.sum(acc[...], axis=1, keepdims=True), axis=0, keepdims=True)
            l_ref[...] = jnp.broadcast_to(tot * (0.5 / n), (8, LANES))

    row = pl.BlockSpec((tm, n), lambda i: (i, 0))
    return pl.pallas_call(
        body, name=name, grid=(nsteps,),
        in_specs=[pl.BlockSpec((tm, k), lambda i: (i, 0)),
                  pl.BlockSpec((None, k, n), lambda i: (lidx, 0, 0), pipeline_mode=pl.Buffered(1)),
                  row, row],
        out_specs=[row, row, pl.BlockSpec((8, LANES), lambda i: (0, 0))],
        out_shape=[jax.ShapeDtypeStruct((s, n), F32), jax.ShapeDtypeStruct((s, n), BF16),
                   jax.ShapeDtypeStruct((8, LANES), F32)],
        scratch_shapes=[pltpu.VMEM((8, n), F32)],
        compiler_params=_cparams("arbitrary"),
    )(a, w3, res, target)


def _mm_nt(a, w3, lidx, tm, tn, name):
    s, k = a.shape
    n = w3.shape[1]

    def body(a_ref, b_ref, o_ref):
        o_ref[...] = _dot(a_ref[...], b_ref[...], NT)

    return pl.pallas_call(
        body, name=name, grid=(s // tm, n // tn),
        in_specs=[pl.BlockSpec((tm, k), lambda i, j: (i, 0)),
                  pl.BlockSpec((None, tn, k), lambda i, j: (lidx, j, 0))],
        out_specs=pl.BlockSpec((tm, tn), lambda i, j: (i, j)),
        out_shape=jax.ShapeDtypeStruct((s, n), F32),
        compiler_params=_cparams("parallel", "arbitrary"),
    )(a, w3)


def _mm_nt_swiglu_bwd(dx, wd3, lidx, g, u, tm, name, travel=()):
    s, d = dx.shape
    cols = BLOCK_PAIR * (g.shape[1] // N_DEV)

    def body(a_ref, b_ref, g_ref, u_ref, dg_ref, du_ref):
        a = a_ref[...]
        for c0 in range(0, cols, MXU_WIDTH):
            sl = slice(c0, c0 + MXU_WIDTH)
            dact = _dot(a, b_ref[sl, :], NT)
            gv = g_ref[:, sl].astype(F32)
            sig = 0.5 * jnp.tanh(0.5 * gv) + 0.5
            silu = gv * sig
            du_ref[:, sl] = (dact * silu).astype(du_ref.dtype)
            dsilu = sig + silu * (1.0 - sig)
            dg_ref[:, sl] = (dact * u_ref[:, sl].astype(F32) * dsilu).astype(dg_ref.dtype)

    blk = pl.BlockSpec((tm, cols), lambda i, j: (i, j))
    shape = jax.ShapeDtypeStruct(g.shape, BF16)
    grid = (s // tm, N_DEV // BLOCK_PAIR)
    body, more_in, more_out, more_shapes, more_scratch = _host_exchange(body, 4, 2, travel, grid)
    outs = pl.pallas_call(
        body, name=name, grid=grid,
        in_specs=[pl.BlockSpec((tm, d), lambda i, j: (i, 0)),
                  pl.BlockSpec((None, cols, d), lambda i, j: (lidx, j, 0)), blk, blk] + more_in,
        out_specs=[blk, blk] + more_out, out_shape=[shape, shape] + more_shapes,
        scratch_shapes=more_scratch,
        compiler_params=_cparams("arbitrary", "arbitrary"),
    )(dx, wd3, g, u, *travel)
    return outs[0], outs[1], list(outs[2:])


def _mm_nt_norm_bwd(das, ga, widxs, x, gain, dres, tm, name, travel=()):
    s = das[0].shape[0]
    nb, d, cols = ga.shape[1], ga.shape[2], ga.shape[3]
    nw = len(das)
    nsteps = s // tm

    def body(*refs):
        a_refs, b_refs = refs[:nw], refs[nw:2 * nw]
        x_ref, g_ref, dres_ref, dx_ref, dxb_ref, dg_ref = refs[2 * nw:]
        i = pl.program_id(0)
        dhv = None
        wide = BLOCK_PAIR * cols
        for w in range(nw):
            for k in range(nb // BLOCK_PAIR):
                b = jnp.concatenate([b_refs[w][BLOCK_PAIR * k + p] for p in range(BLOCK_PAIR)], axis=1)
                part = _dot(a_refs[w][:, k * wide:(k + 1) * wide], b, NT)
                dhv = part if dhv is None else dhv + part
        xv = x_ref[...]
        r = lax.rsqrt(jnp.mean(xv * xv, axis=-1, keepdims=True) + EPS)
        xhat = xv * r
        dxh = dhv * g_ref[...]
        dxv = dres_ref[...] + r * (dxh - xhat * jnp.mean(dxh * xhat, axis=-1, keepdims=True))
        dx_ref[...] = dxv
        dxb_ref[...] = dxv.astype(dxb_ref.dtype)
        part = jnp.sum((dhv * xhat).reshape(tm // 8, 8, d), axis=0)

        @pl.when(i == 0)
        def _():
            dg_ref[...] = part

        @pl.when(i > 0)
        def _():
            dg_ref[...] += part

        @pl.when(i == nsteps - 1)
        def _():
            dg_ref[...] = jnp.broadcast_to(jnp.sum(dg_ref[...], axis=0, keepdims=True), (8, d))

    def wspec(idx):
        return pl.BlockSpec((None, nb, d, cols), lambda i: (idx, 0, 0, 0), pipeline_mode=pl.Buffered(1))

    row = pl.BlockSpec((tm, d), lambda i: (i, 0))
    body, more_in, more_out, more_shapes, more_scratch = _host_exchange(body, 2 * nw + 3, 3, travel, (nsteps,))
    outs = pl.pallas_call(
        body, name=name, grid=(nsteps,),
        in_specs=([pl.BlockSpec((tm, nb * cols), lambda i: (i, 0))] * nw + [wspec(i) for i in widxs]
                  + [row, pl.BlockSpec((1, d), lambda i: (0, 0)), row] + more_in),
        out_specs=[row, row, pl.BlockSpec((8, d), lambda i: (0, 0))] + more_out,
        out_shape=[jax.ShapeDtypeStruct((s, d), F32), jax.ShapeDtypeStruct((s, d), BF16),
                   jax.ShapeDtypeStruct((8, d), F32)] + more_shapes,
        scratch_shapes=more_scratch,
        compiler_params=_cparams("arbitrary"),
    )(*das, *([ga] * nw), x, gain, dres, *travel)
    return outs[0], outs[1], outs[2], list(outs[3:])


def _mm_tn(a, b, ta, tb, tk, out_blocks, name, travel=()):
    s, ka = a.shape
    nb = b.shape[1]
    nk = s // tk
    cols = tb
    if out_blocks:
        tb = BLOCK_PAIR * cols

    def body(a_ref, b_ref, o_ref, ob_ref):
        k = pl.program_id(2)
        part = _dot(a_ref[...], b_ref[...], TN)

        def put(first):
            if out_blocks:
                for p in range(BLOCK_PAIR):
                    piece = part[:, p * cols:(p + 1) * cols]
                    o_ref[p] = piece if first else o_ref[p] + piece
            else:
                o_ref[...] = part if first else o_ref[...] + part

        @pl.when(k == 0)
        def _():
            put(True)

        @pl.when(k > 0)
        def _():
            put(False)

        @pl.when(k == nk - 1)
        def _():
            ob_ref[...] = o_ref[...].astype(ob_ref.dtype)

    if out_blocks:
        out_spec = pl.BlockSpec((BLOCK_PAIR, ta, cols), lambda i, j, k: (j, i, 0))
        shape = (nb // cols, ka, cols)
    else:
        out_spec = pl.BlockSpec((ta, tb), lambda i, j, k: (i, j))
        shape = (ka, nb)
    grid = (ka // ta, nb // tb, nk)
    body, more_in, more_out, more_shapes, more_scratch = _host_exchange(body, 2, 2, travel, grid)
    outs = pl.pallas_call(
        body, name=name, grid=grid,
        in_specs=[pl.BlockSpec((tk, ta), lambda i, j, k: (k, i)),
                  pl.BlockSpec((tk, tb), lambda i, j, k: (k, j))] + more_in,
        out_specs=[out_spec, out_spec] + more_out,
        out_shape=[jax.ShapeDtypeStruct(shape, F32), jax.ShapeDtypeStruct(shape, BF16)] + more_shapes,
        scratch_shapes=more_scratch,
        compiler_params=_cparams("arbitrary", "arbitrary", "arbitrary"),
    )(a, b, *travel)
    return (outs[0], outs[1]), list(outs[2:])


def _adamw(g, w, m, v, name):
    rows, cols = g.shape
    c1 = 1.0 / (1.0 - ADAM_B1 ** ADAM_STEP)
    c2 = 1.0 / (1.0 - ADAM_B2 ** ADAM_STEP)

    def body(p_ref, w_ref, m_ref, v_ref, g_ref, d_ref, nm_ref, nv_ref):
        gv = p_ref[...]
        nm = ADAM_B1 * m_ref[...] + (1.0 - ADAM_B1) * gv
        nv = ADAM_B2 * v_ref[...] + (1.0 - ADAM_B2) * (gv * gv)
        g_ref[...] = gv
        nm_ref[...] = nm
        nv_ref[...] = nv
        d_ref[...] = -ADAM_LR * ((nm * c1) / (jnp.sqrt(nv * c2) + ADAM_EPS) + ADAM_WD * w_ref[...])

    blk = pl.BlockSpec((rows, cols), lambda i: (0, 0))
    shape = jax.ShapeDtypeStruct((rows, cols), F32)
    return pl.pallas_call(
        body, name=name, grid=(1,),
        in_specs=[blk] * 4, out_specs=[blk] * 4, out_shape=[shape] * 4,
        compiler_params=_cparams("arbitrary"),
    )(g, w, m, v)


def _adamw_sharded(parts, grads, my, w, m, v, tr, name):
    depth, rows, cols = w.shape
    p, pr, pc = parts[0].shape
    c1 = 1.0 / (1.0 - ADAM_B1 ** ADAM_STEP)
    c2 = 1.0 / (1.0 - ADAM_B2 ** ADAM_STEP)

    def body(my_ref, *refs):
        p_refs, own_refs = refs[:depth], refs[depth:2 * depth]
        w_ref, m_ref, v_ref, g_ref, d_ref, nm_ref, nv_ref = refs[2 * depth:]
        layer = pl.program_id(0)
        for ll in range(depth):
            @pl.when(layer == ll)
            def _(ll=ll):
                mine = own_refs[ll][...]
                g = jnp.where(my_ref[0] == 0, mine, p_refs[ll][0].astype(F32))
                for k in range(1, p):
                    g = g + jnp.where(my_ref[0] == k, mine, p_refs[ll][k].astype(F32))
                g = g[:, :cols]
                nm = ADAM_B1 * m_ref[...] + (1.0 - ADAM_B1) * g
                nv = ADAM_B2 * v_ref[...] + (1.0 - ADAM_B2) * (g * g)
                g_ref[...] = g
                nm_ref[...] = nm
                nv_ref[...] = nv
                d_ref[...] = -ADAM_LR * ((nm * c1) / (jnp.sqrt(nv * c2) + ADAM_EPS) + ADAM_WD * w_ref[...])

    def row_block(ll, l, i):
        return jnp.where(l == ll, i, 0)

    blk = pl.BlockSpec((None, tr, cols), lambda l, i, my_: (l, i, 0))
    shape = jax.ShapeDtypeStruct((depth, rows, cols), F32)
    return pl.pallas_call(
        body, name=name,
        grid_spec=pltpu.PrefetchScalarGridSpec(
            num_scalar_prefetch=1, grid=(depth, rows // tr),
            in_specs=([pl.BlockSpec((p, tr, pc), lambda l, i, my_, ll=ll: (0, row_block(ll, l, i), 0))
                       for ll in range(depth)]
                      + [pl.BlockSpec((None, tr, pc), lambda l, i, my_, ll=ll: (my_[0], row_block(ll, l, i), 0))
                         for ll in range(depth)]
                      + [blk, blk, blk]),
            out_specs=[blk] * 4),
        out_shape=[shape] * 4,
        compiler_params=_cparams("arbitrary", "arbitrary"),
    )(my, *parts, *grads, w, m, v)


def _place():
    x, y, c = lax.axis_index("x"), lax.axis_index("y"), lax.axis_index("c")
    return x, y, c


class _Gather:
    def __init__(self, srcs, dsts, send_sems, recv_sems, local_sems):
        na = len(srcs)
        x, y, c = _place()
        me, sibling = (x, y, c), (x, y, 1 - c)
        chips = [(1 - x, y), (x, 1 - y), (1 - x, 1 - y)]

        def slot(a, dev):
            return dsts[a].at[:, pl.ds(4 * dev[0] + 2 * dev[1] + dev[2], 1)]

        def copy(k, a, block, to, from_shard=False):
            return pltpu.make_async_remote_copy(
                src_ref=srcs[a] if from_shard else slot(a, block), dst_ref=slot(a, block),
                send_sem=send_sems.at[k, a], recv_sem=recv_sems.at[k, a], device_id=to, device_id_type=MESH)

        pairs = [(j, chip, a) for j, chip in enumerate(chips) for a in range(na)]
        self.mine = [pltpu.make_async_copy(srcs[a], slot(a, me), local_sems.at[a]) for a in range(na)]
        self.first = [copy(0, a, me, sibling, True) for a in range(na)]
        self.first += [copy(1 + j, a, me, (*chip, c), True) for j, chip, a in pairs]
        self.over_ici = [copy(1 + j, a, (*chip, c), me) for j, chip, a in pairs]
        self.passed = [copy(4 + j, a, (*chip, c), sibling) for j, chip, a in pairs]
        self.from_sibling = [copy(0, a, sibling, me) for a in range(na)]
        self.from_sibling += [copy(4 + j, a, (*chip, 1 - c), me) for j, chip, a in pairs]

    def begin(self):
        for cp in self.mine + self.first:
            cp.start()

    def relay(self):
        for arrived, onward in zip(self.over_ici, self.passed):
            arrived.wait_recv()
            onward.start()

    def finish(self):
        for cp in self.from_sibling:
            cp.wait_recv()
        for cp in self.first + self.passed:
            cp.wait_send()
        for cp in self.mine:
            cp.wait()


def _gather_scratch(na):
    return [pltpu.SemaphoreType.DMA((7, na)), pltpu.SemaphoreType.DMA((7, na)), pltpu.SemaphoreType.DMA((na,))]


def _gathered_shapes(shards):
    return [jax.ShapeDtypeStruct((a.shape[0], N_DEV) + a.shape[2:], a.dtype) for a in shards]


_RELATIONS = [(dx, dy, dc) for dx in (0, 1) for dy in (0, 1) for dc in (0, 1)][1:]


def _flip(v, d):
    return 1 - v if d else v


def _exchange_copies(srcs, dsts, send_sems, recv_sems, local_sems):
    x, y, c = _place()
    my = 4 * x + 2 * y + c
    na = len(srcs)
    mine = [pltpu.make_async_copy(srcs[a].at[pl.ds(my, 1)], dsts[a].at[pl.ds(my, 1)], local_sems.at[a])
            for a in range(na)]
    sends, recvs = [], []
    for k, (dx, dy, dc) in enumerate(_RELATIONS):
        peer = (_flip(x, dx), _flip(y, dy), _flip(c, dc))
        pidx = 4 * peer[0] + 2 * peer[1] + peer[2]
        for a in range(na):
            for into, out in ((my, sends), (pidx, recvs)):
                out.append(pltpu.make_async_remote_copy(
                    src_ref=srcs[a].at[pl.ds(pidx, 1)], dst_ref=dsts[a].at[pl.ds(into, 1)],
                    send_sem=send_sems.at[k, a], recv_sem=recv_sems.at[k, a], device_id=peer, device_id_type=MESH))
    return mine, sends, recvs


def _exchange_begin(copies):
    mine, sends, _ = copies
    for cp in mine + sends:
        cp.start()


def _exchange_finish(copies):
    mine, sends, recvs = copies
    for cp in recvs:
        cp.wait_recv()
    for cp in sends:
        cp.wait_send()
    for cp in mine:
        cp.wait()


def _exchange_scratch(na):
    return [pltpu.SemaphoreType.DMA((7, na)), pltpu.SemaphoreType.DMA((7, na)), pltpu.SemaphoreType.DMA((na,))]


def _host_exchange(body, n_in, n_out, travel, grid):
    nt = len(travel)
    if not nt:
        return body, [], [], [], []

    def wrapped(*refs):
        ins, srcs = refs[:n_in], refs[n_in:n_in + nt]
        outs, rest = refs[n_in + nt:n_in + nt + n_out], refs[n_in + nt + n_out:]
        dsts, scratch = rest[:nt], rest[nt:]
        copies = _exchange_copies(srcs, dsts, *scratch[-3:])
        first = last = None
        for axis, size in enumerate(grid):
            at_start, at_end = pl.program_id(axis) == 0, pl.program_id(axis) == size - 1
            first = at_start if first is None else jnp.logical_and(first, at_start)
            last = at_end if last is None else jnp.logical_and(last, at_end)

        @pl.when(first)
        def _():
            _exchange_begin(copies)

        body(*ins, *outs, *scratch[:-3])

        @pl.when(last)
        def _():
            _exchange_finish(copies)

    anyspec = pl.BlockSpec(memory_space=pl.ANY)
    return (wrapped, [anyspec] * nt, [anyspec] * nt, [jax.ShapeDtypeStruct(t.shape, t.dtype) for t in travel],
            _exchange_scratch(nt))


def _all_reduce_small(v, name):
    r, c_ = v.shape

    def body(v_ref, o_ref, gath, send_sems, recv_sems):
        x, y, c = _place()
        my = 4 * x + 2 * y + c
        gath[my] = v_ref[...]
        sends = []
        for k, (dx, dy, dc) in enumerate(_RELATIONS):
            peer = (_flip(x, dx), _flip(y, dy), _flip(c, dc))
            cp = pltpu.make_async_remote_copy(
                src_ref=v_ref, dst_ref=gath.at[my], send_sem=send_sems.at[k], recv_sem=recv_sems.at[k],
                device_id=peer, device_id_type=MESH)
            cp.start()
            sends.append((cp, 4 * peer[0] + 2 * peer[1] + peer[2], k, peer))
        for cp, pidx, k, peer in sends:
            pltpu.make_async_remote_copy(
                src_ref=v_ref, dst_ref=gath.at[pidx], send_sem=send_sems.at[k], recv_sem=recv_sems.at[k],
                device_id=peer, device_id_type=MESH).wait_recv()
        for cp, *_ in sends:
            cp.wait_send()
        tot = gath[0]
        for k in range(1, N_DEV):
            tot = tot + gath[k]
        o_ref[...] = tot

    vm = pl.BlockSpec(memory_space=pltpu.VMEM)
    return pl.pallas_call(
        body, name=name, in_specs=[vm], out_specs=vm,
        out_shape=jax.ShapeDtypeStruct((r, c_), F32),
        scratch_shapes=[pltpu.VMEM((N_DEV, r, c_), F32), pltpu.SemaphoreType.DMA((7,)),
                        pltpu.SemaphoreType.DMA((7,))],
    )(v)


TM = 512
TM_MATMUL = 2048
TM_RESIDUAL = 1024
TQ = 256


def _device_blocks(t):
    return t.reshape(N_DEV, -1, t.shape[-1])


def _pad_to(a, axis, size):
    pad = [(0, 0)] * a.ndim
    pad[axis] = (0, size - a.shape[axis])
    return jnp.pad(a, pad)


def _local_step(x, target, first_shards, late_shards, conv_shard, norm_mix, q_norm, k_norm, norm_ffn):
    depth, d = norm_mix.shape
    cols = first_shards[0].shape[3]
    tm, tq = min(TM, x.shape[0]), min(TQ, x.shape[0])
    tmm, tmr = min(TM_MATMUL, x.shape[0]), min(TM_RESIDUAL, x.shape[0])
    attn = d // 2
    nheads = attn // HEAD_DIM
    scale = HEAD_DIM ** -0.5 * LOG2E
    saved = []
    h1, (g_in0, g_conv) = _rmsnorm_fwd(x, norm_mix[0][None], tm, "norm_mix_fwd_0", first_shards)
    conv_full = g_conv[0, :, :depth * 3, :conv_shard].transpose(1, 0, 2).reshape(depth, 3, N_DEV * conv_shard)
    for l in range(depth):
        w_in = (g_in0, 0) if l == 0 else (g_rest, 3 * (l - 1))
        proj = _mm_blocks(h1, *w_in, tmm, f"proj_in_{l}")
        qk_gain = jnp.concatenate([jnp.tile(q_norm[l], nheads) * scale, jnp.tile(k_norm[l], nheads)])[None]
        qk = _qknorm_fwd(proj, qk_gain, tmm, f"qknorm_fwd_{l}")
        o, rtot, used, gathered = _attn_fwd(qk, proj, tq, f"attn_fwd_{l}", d, late_shards if l == 0 else ())
        if l == 0:
            g_gu0, g_rest, gb, gc = gathered if depth > 1 else (gathered[0], None, *gathered[1:])
            gb = gb.reshape(depth, -1, d)
            gc = gc.reshape(depth, -1, d)
        w_gu = (g_gu0, 0, 1) if l == 0 else (g_rest, 3 * (l - 1) + 1, 3 * (l - 1) + 2)
        conv_w8 = _pad_to(conv_full[l], 0, 8)
        mix = _conv_fwd(proj, conv_w8, o, f"conv_fwd_{l}")
        x1, h2 = _mm_residual_norm(mix, gb, l, x, norm_ffn[l][None], tmr, f"proj_out_{l}")
        g, u, act = _mm_swiglu(h2, *w_gu, tmr, f"ffn_up_{l}")
        saved.append((x, h1, proj, qk_gain, qk, rtot, used, conv_w8, mix, x1, h2, g, u, act, w_in, w_gu))
        if l + 1 < depth:
            x, h1 = _mm_residual_norm(act, gc, l, x1, norm_mix[l + 1][None], tm, f"ffn_down_{l}")
        else:
            dx, dxb, loss = _mm_residual_loss(act, gc, l, x1, target, tm, f"ffn_down_{l}")

    grads = [None] * depth
    small = [None] * depth
    landed = {}
    for l in reversed(range(depth)):
        x0, h1, proj, qk_gain, qk, rtot, used, conv_w8, mix, x1, h2, g, u, act, w_in, w_gu = saved[l]
        d = x0.shape[1]
        late = [[], [], [], []]
        if l == 0:
            for n, i in enumerate(5 * ll + j for ll in range(1, depth) for j in (0, 1, 2, 4, 3)):
                late[n % 4].append(i)

        def ride(host):
            return late[host], [_device_blocks(grads[i // 5][i % 5][1]) for i in late[host]]

        idx, travel = ride(0)
        dg, du, arrived = _mm_nt_swiglu_bwd(dxb, gc, l, g, u, tmr, f"ffn_down_bwd_{l}", travel)
        landed.update(zip(idx, arrived))
        idx, travel = ride(1)
        d_wdown, arrived = _mm_tn(act, dxb, 768, d, tmm, False, f"dw_down_{l}", travel)
        landed.update(zip(idx, arrived))
        idx, travel = ride(2)
        d_wgate, arrived = _mm_tn(h2, dg, d, cols, tmm, True, f"dw_gate_{l}", travel)
        landed.update(zip(idx, arrived))
        idx, travel = ride(3)
        d_wup, arrived = _mm_tn(h2, du, d, cols, tmm, True, f"dw_up_{l}", travel)
        landed.update(zip(idx, arrived))
        dx1, dx1b, dg_ffn, _ = _mm_nt_norm_bwd([dg, du], w_gu[0], list(w_gu[1:]), x1, norm_ffn[l][None], dx, tm,
                                               f"ffn_up_bwd_{l}")
        dmix = _mm_nt(dx1b, gb, l, tmr, 512, f"proj_out_bwd_{l}")
        d_wout, _ = _mm_tn(mix, dx1b, 512, d, tmm, False, f"dw_out_{l}")
        dcb, dcc, dcu, dconv = _conv_bwd(dmix, proj, conv_w8, f"conv_bwd_{l}")
        rides = {1: d_wgate[1], 2: d_wup[1], 3: d_wout[1], 4: d_wdown[1]} if l == 0 else {}
        dq, dk, dv, arrived = _attn_bwd(qk, proj, dmix, rtot, used, tq, f"attn_bwd_{l}",
                                        [_device_blocks(t) for t in rides.values()])
        landed.update(zip(rides.keys(), arrived))
        dqk, dg_qk = _qknorm_bwd(dq, dk, proj, qk_gain, tmm, f"qknorm_bwd_{l}")
        dproj = jnp.concatenate([dqk, dv.astype(BF16), dcb, dcc, dcu], axis=1)
        d_win, _ = _mm_tn(h1, dproj, d, cols, tmm, True, f"dw_in_{l}")
        dx, dxb, dg_mix, arrived = _mm_nt_norm_bwd(
            [dproj], w_in[0], [w_in[1]], x0, norm_mix[l][None], dx1, tm, f"proj_in_bwd_{l}",
            [_device_blocks(d_win[1])] if l == 0 else [])
        landed.update(zip([0], arrived))
        grads[l] = (d_win, d_wgate, d_wup, d_wout, d_wdown)
        dq_gain = jnp.sum(dg_qk[0, :attn].reshape(nheads, HEAD_DIM), axis=0) * scale
        dk_gain = jnp.sum(dg_qk[0, attn:].reshape(nheads, HEAD_DIM), axis=0)
        small[l] = (dg_mix[0], dg_ffn[0], dq_gain, dk_gain, dconv[:3])
    return loss, dx, grads, small, landed


def kernel(x, norm_mix, w_in, q_norm, k_norm, conv_w, w_out, norm_ffn, w_gate, w_up, w_down, loss_target, m_norm_mix, m_w_in, m_q_norm, m_k_norm, m_conv_w, m_w_out, m_norm_ffn, m_w_gate, m_w_up, m_w_down, v_norm_mix, v_w_in, v_q_norm, v_k_norm, v_conv_w, v_w_out, v_norm_ffn, v_w_gate, v_w_up, v_w_down):
    depth, d, in_shard = w_in.shape
    ff_shard = w_gate.shape[2]
    ff_pad = in_shard
    conv_shard = conv_w.shape[2]
    xs = x.reshape(x.shape[-2], d)
    target = loss_target.reshape(xs.shape)

    pa = jnp.stack([w_in, _pad_to(w_gate, 2, ff_pad), _pad_to(w_up, 2, ff_pad)], axis=1)
    pa = pa.reshape(3 * depth, 1, d, in_shard).astype(BF16)
    pd = _pad_to(_pad_to(conv_w.reshape(depth * 3, conv_shard), 0, 8), 1, LANES)[None, None]
    late_shards = [pa[1:3]] + ([pa[3:]] if depth > 1 else [])
    late_shards += [w_out.astype(BF16)[:, None], _pad_to(w_down, 1, ff_pad).astype(BF16)[:, None]]

    loss, grad_x, grads, small, landed = _local_step(xs, target, [pa[:1], pd], late_shards, conv_shard, norm_mix,
                                                     q_norm, k_norm, norm_ffn)

    x_, y_, c_ = _place()
    my = 4 * x_ + 2 * y_ + c_

    rows = []
    for l in range(depth):
        g_mix, g_ffn, g_q, g_k, g_conv = small[l]
        qkrow = _pad_to(jnp.concatenate([g_q, g_k]), 0, d)
        rows += [g_mix[None], g_ffn[None], qkrow[None], _pad_to(g_conv, 1, d)]
    nrow = 6 * depth
    packed = jnp.concatenate(rows + [_pad_to(loss[:1], 1, d)], axis=0)
    packed = _pad_to(packed, 0, ((nrow + 1 + 7) // 8) * 8)
    summed = _all_reduce_small(packed, "reduce_small")
    loss_out = summed[nrow, 0]

    my1 = my.astype(jnp.int32).reshape(1)

    def big(j, w, m, v, tr, name):
        return _adamw_sharded([landed[5 * l + j] for l in range(depth)],
                              [_device_blocks(grads[l][j][0]) for l in range(depth)], my1, w, m, v, tr, name)

    res = {"w_in": big(0, w_in, m_w_in, v_w_in, 256, "adamw_in"),
           "w_gate": big(1, w_gate, m_w_gate, v_w_gate, 256, "adamw_gate"),
           "w_up": big(2, w_up, m_w_up, v_w_up, 256, "adamw_up"),
           "w_out": big(3, w_out, m_w_out, v_w_out, w_out.shape[1], "adamw_out"),
           "w_down": big(4, w_down, m_w_down, v_w_down, ff_shard // 2, "adamw_down")}

    g_rows, w_rows, m_rows, v_rows = [], [], [], []
    for l in range(depth):
        base = l * 6
        conv_g = lax.dynamic_slice(summed[base + 3:base + 6], (0, my * conv_shard), (3, conv_shard))
        g_rows += [summed[base:base + 3], _pad_to(conv_g, 1, d)]
        for dst, (nm, qn, kn, nf, cw) in ((w_rows, (norm_mix, q_norm, k_norm, norm_ffn, conv_w)),
                                          (m_rows, (m_norm_mix, m_q_norm, m_k_norm, m_norm_ffn, m_conv_w)),
                                          (v_rows, (v_norm_mix, v_q_norm, v_k_norm, v_norm_ffn, v_conv_w))):
            dst += [nm[l][None], nf[l][None], _pad_to(jnp.concatenate([qn[l], kn[l]]), 0, d)[None],
                    _pad_to(cw[l], 1, d)]
    prow = ((nrow + 7) // 8) * 8
    gs, ws, ms, vs = [_pad_to(jnp.concatenate(t, axis=0), 0, prow) for t in (g_rows, w_rows, m_rows, v_rows)]
    sm = _adamw(gs, ws, ms, vs, "adamw_small")

    hd = q_norm.shape[1]

    def small_out(t, kind):
        per_layer = []
        for l in range(depth):
            base = l * 6
            per_layer.append({"norm_mix": t[base], "norm_ffn": t[base + 1], "q_norm": t[base + 2, :hd],
                              "k_norm": t[base + 2, hd:2 * hd], "conv_w": t[base + 3:base + 6, :conv_shard]}[kind])
        return jnp.stack(per_layer)

    def big_out(name, i):
        return res[name][i]

    outs = [loss_out, grad_x.reshape(x.shape)]
    for i in range(4):
        outs += [small_out(sm[i], "norm_mix"), big_out("w_in", i), small_out(sm[i], "q_norm"),
                 small_out(sm[i], "k_norm"), small_out(sm[i], "conv_w"), big_out("w_out", i),
                 small_out(sm[i], "norm_ffn"), big_out("w_gate", i), big_out("w_up", i), big_out("w_down", i)]
    return tuple(outs)
```

```python
import jax
import jax.numpy as jnp
from jax import lax
from jax.experimental import pallas as pl
from jax.experimental.pallas import tpu as pltpu

F32 = jnp.float32
BF16 = jnp.bfloat16
MESH = pl.DeviceIdType.MESH

N_DEV = 8
LANES = 128
HEAD_DIM = 64
KEY_CHUNK = 128
EPS = 1e-6
VMEM_LIMIT = 48 * 1024 * 1024

ADAM_LR = 0.001
ADAM_B1 = 0.9
ADAM_B2 = 0.999
ADAM_EPS = 1e-08
ADAM_WD = 0.01
ADAM_STEP = 10

NN = (((1,), (0,)), ((), ()))
NT = (((1,), (1,)), ((), ()))
TN = (((0,), (0,)), ((), ()))


def _dot(a, b, dims):
    return lax.dot_general(a.astype(BF16), b.astype(BF16), dims, preferred_element_type=F32)


def _cparams(*sem):
    return pltpu.CompilerParams(dimension_semantics=sem, vmem_limit_bytes=VMEM_LIMIT)


def _rmsnorm_fwd(x, gain, tm, name, shards=()):
    s, d = x.shape
    nsteps = s // tm
    ng = len(shards)

    def body(*refs):
        x_ref, g_ref, srcs = refs[0], refs[1], refs[2:2 + ng]
        o_ref, dsts, sems = refs[2 + ng], refs[3 + ng:3 + 2 * ng], refs[3 + 2 * ng:]
        i = pl.program_id(0)
        gather = _Gather(srcs, dsts, *sems) if ng else None
        if ng:
            @pl.when(i == 0)
            def _():
                gather.begin()

        xv = x_ref[...]
        r = lax.rsqrt(jnp.mean(xv * xv, axis=-1, keepdims=True) + EPS)
        o_ref[...] = ((xv * r) * g_ref[...]).astype(o_ref.dtype)
        if ng:
            @pl.when(i == nsteps - 1)
            def _():
                gather.relay()
                gather.finish()

    anyspec = pl.BlockSpec(memory_space=pl.ANY)
    outs = pl.pallas_call(
        body, name=name, grid=(nsteps,),
        in_specs=[pl.BlockSpec((tm, d), lambda i: (i, 0)), pl.BlockSpec((1, d), lambda i: (0, 0))] + [anyspec] * ng,
        out_specs=[pl.BlockSpec((tm, d), lambda i: (i, 0))] + [anyspec] * ng,
        out_shape=[jax.ShapeDtypeStruct((s, d), BF16)] + _gathered_shapes(shards),
        scratch_shapes=_gather_scratch(ng) if ng else [],
        compiler_params=_cparams("arbitrary"),
    )(x, gain, *shards)
    return outs[0], list(outs[1:])


def _group_mean_matrix():
    r = lax.broadcasted_iota(jnp.int32, (LANES, LANES), 0) // HEAD_DIM
    c = lax.broadcasted_iota(jnp.int32, (LANES, LANES), 1) // HEAD_DIM
    return jnp.where(r == c, 1.0 / HEAD_DIM, 0.0).astype(BF16)


def _group_mean(v, gm):
    hi = v.astype(BF16)
    lo = (v - hi.astype(F32)).astype(BF16)
    return _dot(hi, gm, NN) + _dot(lo, gm, NN)


def _qknorm_fwd(proj, gains, tm, name):
    s = proj.shape[0]
    ncol = gains.shape[1] // LANES

    def body(p_ref, g_ref, gm_ref, o_ref):
        xv = p_ref[...].astype(F32)
        r = lax.rsqrt(_group_mean(xv * xv, gm_ref[...]) + EPS)
        o_ref[...] = ((xv * r) * g_ref[...]).astype(o_ref.dtype)

    blk = pl.BlockSpec((tm, LANES), lambda i, j: (i, j))
    return pl.pallas_call(
        body, name=name, grid=(s // tm, ncol),
        in_specs=[blk, pl.BlockSpec((1, LANES), lambda i, j: (0, j)),
                  pl.BlockSpec((LANES, LANES), lambda i, j: (0, 0))],
        out_specs=blk,
        out_shape=jax.ShapeDtypeStruct((s, ncol * LANES), BF16),
        compiler_params=_cparams("parallel", "parallel"),
    )(proj, gains, _group_mean_matrix())


def _qknorm_bwd(pieces, proj, gains, tm, name):
    s = proj.shape[0]
    per = pieces[0].shape[1] // LANES
    ncol = gains.shape[1] // LANES
    nsteps = s // tm

    def body(*refs):
        piece_refs = refs[:len(pieces)]
        p_ref, g_ref, gm_ref, dx_ref, dg_ref = refs[len(pieces):]
        j, i = pl.program_id(0), pl.program_id(1)

        for n in range(ncol // per):
            @pl.when(j // per == n)
            def _(n=n):
                gm = gm_ref[...]
                xv = p_ref[...].astype(F32)
                r = lax.rsqrt(_group_mean(xv * xv, gm) + EPS)
                xhat = xv * r
                dy = piece_refs[n][...]
                dxh = dy * g_ref[...]
                proj_ = _group_mean(dxh * xhat, gm)
                dx_ref[...] = (r * (dxh - xhat * proj_)).astype(dx_ref.dtype)
                part = jnp.sum((dy * xhat).reshape(tm // 8, 8, LANES), axis=0)

                @pl.when(i == 0)
                def _():
                    dg_ref[...] = part

                @pl.when(i > 0)
                def _():
                    dg_ref[...] += part

                @pl.when(i == nsteps - 1)
                def _():
                    dg_ref[...] = jnp.broadcast_to(jnp.sum(dg_ref[...], axis=0, keepdims=True), (8, LANES))

        for n in range(ncol // per, len(pieces)):
            @pl.when(j // per == n)
            def _(n=n):
                dx_ref[...] = piece_refs[n][...].astype(dx_ref.dtype)

    def piece_spec(n):
        def index(j, i):
            mine = j // per == n
            return jnp.where(mine, i, 0), jnp.where(mine, j - n * per, 0)
        return pl.BlockSpec((tm, LANES), index)

    last = ncol - 1
    return pl.pallas_call(
        body, name=name, grid=(len(pieces) * per, nsteps),
        in_specs=[piece_spec(n) for n in range(len(pieces))] + [
            pl.BlockSpec((tm, LANES), lambda j, i: (jnp.where(j <= last, i, 0), jnp.minimum(j, last))),
            pl.BlockSpec((1, LANES), lambda j, i: (0, jnp.minimum(j, last))),
            pl.BlockSpec((LANES, LANES), lambda j, i: (0, 0))],
        out_specs=[pl.BlockSpec((tm, LANES), lambda j, i: (i, j)),
                   pl.BlockSpec((8, LANES), lambda j, i: (0, jnp.minimum(j, last)))],
        out_shape=[jax.ShapeDtypeStruct((s, len(pieces) * per * LANES), BF16),
                   jax.ShapeDtypeStruct((8, ncol * LANES), F32)],
        compiler_params=_cparams("arbitrary", "arbitrary"),
    )(*pieces, proj, gains, _group_mean_matrix())


CONV_ROWS = 256
HALO = 8


def _conv_fwd(proj, conv_w8, mix, name):
    s = proj.shape[0]
    nblk = conv_w8.shape[1] // LANES
    first = 3 * nblk
    nchunk = s // CONV_ROWS
    before = mix.shape[1] // LANES - nblk

    def body(cb_ref, cc_ref, cu_ref, w_ref, mix_ref, y_ref, hpad):
        del mix_ref
        hpad[pl.ds(0, 2 * HALO), :] = jnp.zeros((2 * HALO, LANES), F32)

        def fill(i, _):
            r0 = pl.multiple_of(i * CONV_ROWS, CONV_ROWS)
            hpad[pl.ds(r0 + 2 * HALO, CONV_ROWS), :] = (
                cc_ref[pl.ds(r0, CONV_ROWS), :].astype(F32) * cu_ref[pl.ds(r0, CONV_ROWS), :].astype(F32))
            return 0

        lax.fori_loop(0, nchunk, fill, 0)
        w0, w1, w2 = w_ref[0:1, :], w_ref[1:2, :], w_ref[2:3, :]

        def conv(i, _):
            r0 = pl.multiple_of(i * CONV_ROWS, CONV_ROWS)
            win = hpad[pl.ds(r0 + HALO, CONV_ROWS + HALO), :]
            c = (w2 * win[HALO:] + w1 * pltpu.roll(win, 1, 0)[HALO:] + w0 * pltpu.roll(win, 2, 0)[HALO:])
            y_ref[pl.ds(r0, CONV_ROWS), :] = (cb_ref[pl.ds(r0, CONV_ROWS), :].astype(F32) * c).astype(y_ref.dtype)
            return 0

        lax.fori_loop(0, nchunk, conv, 0)

    def col(off):
        return pl.BlockSpec((s, LANES), lambda j: (0, off + j))

    return pl.pallas_call(
        body, name=name, grid=(nblk,),
        in_specs=[col(first), col(first + nblk), col(first + 2 * nblk), pl.BlockSpec((8, LANES), lambda j: (0, j)),
                  pl.BlockSpec(memory_space=pl.ANY)],
        out_specs=pl.BlockSpec((s, LANES), lambda j: (0, before + j)),
        out_shape=jax.ShapeDtypeStruct(mix.shape, mix.dtype),
        scratch_shapes=[pltpu.VMEM((s + 2 * HALO, LANES), F32)],
        input_output_aliases={4: 0},
        compiler_params=_cparams("parallel"),
    )(proj, proj, proj, conv_w8, mix)


def _conv_bwd(dmix, proj, conv_w8, name):
    s = proj.shape[0]
    nblk = conv_w8.shape[1] // LANES
    first = 3 * nblk
    nchunk = s // CONV_ROWS

    def body(dy_ref, cb_ref, cc_ref, cu_ref, w_ref, dcb_ref, dcc_ref, dcu_ref, dw_ref, hpad, dcpad):
        hpad[pl.ds(0, 2 * HALO), :] = jnp.zeros((2 * HALO, LANES), F32)
        dcpad[pl.ds(s, 2 * HALO), :] = jnp.zeros((2 * HALO, LANES), F32)

        def fill(i, _):
            r0 = pl.multiple_of(i * CONV_ROWS, CONV_ROWS)
            hpad[pl.ds(r0 + 2 * HALO, CONV_ROWS), :] = (
                cc_ref[pl.ds(r0, CONV_ROWS), :].astype(F32) * cu_ref[pl.ds(r0, CONV_ROWS), :].astype(F32))
            return 0

        lax.fori_loop(0, nchunk, fill, 0)
        w0, w1, w2 = w_ref[0:1, :], w_ref[1:2, :], w_ref[2:3, :]

        def fold(v):
            return jnp.sum(v.reshape(CONV_ROWS // 8, 8, LANES), axis=0)

        def first_pass(i, acc):
            a0, a1, a2 = acc
            r0 = pl.multiple_of(i * CONV_ROWS, CONV_ROWS)
            win = hpad[pl.ds(r0 + HALO, CONV_ROWS + HALO), :]
            h0 = win[HALO:]
            h1 = pltpu.roll(win, 1, 0)[HALO:]
            h2 = pltpu.roll(win, 2, 0)[HALO:]
            c = w2 * h0 + w1 * h1 + w0 * h2
            dy = dy_ref[pl.ds(r0, CONV_ROWS), :]
            dcb_ref[pl.ds(r0, CONV_ROWS), :] = (dy * c).astype(dcb_ref.dtype)
            dc = dy * cb_ref[pl.ds(r0, CONV_ROWS), :].astype(F32)
            dcpad[pl.ds(r0, CONV_ROWS), :] = dc
            return a0 + fold(dc * h2), a1 + fold(dc * h1), a2 + fold(dc * h0)

        z8 = jnp.zeros((8, LANES), F32)
        a0, a1, a2 = lax.fori_loop(0, nchunk, first_pass, (z8, z8, z8))
        dw_ref[...] = jnp.concatenate(
            [jnp.sum(a0, axis=0, keepdims=True), jnp.sum(a1, axis=0, keepdims=True),
             jnp.sum(a2, axis=0, keepdims=True), jnp.zeros((5, LANES), F32)], axis=0)

        def second_pass(i, _):
            r0 = pl.multiple_of(i * CONV_ROWS, CONV_ROWS)
            win = dcpad[pl.ds(r0, CONV_ROWS + HALO), :]
            n = CONV_ROWS + HALO
            dh = (w2 * win[:CONV_ROWS] + w1 * pltpu.roll(win, n - 1, 0)[:CONV_ROWS]
                  + w0 * pltpu.roll(win, n - 2, 0)[:CONV_ROWS])
            dcc_ref[pl.ds(r0, CONV_ROWS), :] = (dh * cu_ref[pl.ds(r0, CONV_ROWS), :].astype(F32)).astype(dcc_ref.dtype)
            dcu_ref[pl.ds(r0, CONV_ROWS), :] = (dh * cc_ref[pl.ds(r0, CONV_ROWS), :].astype(F32)).astype(dcu_ref.dtype)
            return 0

        lax.fori_loop(0, nchunk, second_pass, 0)

    def col(off):
        return pl.BlockSpec((s, LANES), lambda j: (0, off + j))

    out = pl.BlockSpec((s, LANES), lambda j: (0, j))
    return pl.pallas_call(
        body, name=name, grid=(nblk,),
        in_specs=[col(nblk), col(first), col(first + nblk), col(first + 2 * nblk),
                  pl.BlockSpec((8, LANES), lambda j: (0, j))],
        out_specs=[out, out, out, pl.BlockSpec((8, LANES), lambda j: (0, j))],
        out_shape=[jax.ShapeDtypeStruct((s, nblk * LANES), BF16)] * 3 + [jax.ShapeDtypeStruct((8, nblk * LANES), F32)],
        scratch_shapes=[pltpu.VMEM((s + 2 * HALO, LANES), F32), pltpu.VMEM((s + 2 * HALO, LANES), F32)],
        compiler_params=_cparams("parallel"),
    )(dmix, proj, proj, proj, conv_w8)


LOG2E = 1.4426950408889634
LN2 = 0.6931471805599453
NEG_BIG = -1e30
SATURATED = 160.0


def _cumsum_matrix(kind):
    j = lax.broadcasted_iota(jnp.int32, (KEY_CHUNK, 2 * KEY_CHUNK), 0)
    c = lax.broadcasted_iota(jnp.int32, (KEY_CHUNK, 2 * KEY_CHUNK), 1)
    tri = {"after": j > c, "upto": j <= c, "before": j < c}[kind]
    return jnp.where((c >= KEY_CHUNK) | tri, 1.0, 0.0).astype(BF16)


def _stack_heads(t, m0):
    zero = jnp.zeros_like(t)
    return jnp.concatenate([jnp.where(m0, t, zero), jnp.where(m0, zero, t)], axis=0)


def _softplus2(z):
    sp = jnp.maximum(z, 0.0) + jnp.log2(1.0 + jnp.exp2(-jnp.abs(z)))
    return sp, z - sp


def _key_chunk(ref, kc):
    return ref[pl.ds(pl.multiple_of(kc * KEY_CHUNK, KEY_CHUNK), KEY_CHUNK), :]


def _attn_bwd(qk, proj, dmix, rtot, used, tq, name, travel=()):
    s = qk.shape[0]
    nhp = qk.shape[1] // (2 * LANES)
    nc = tq // KEY_CHUNK
    nq = s // tq
    nt = len(travel)

    def body(used_ref, q_ref, k_ref, v_ref, do_ref, r_ref, cmi_ref, cme_ref, bias_ref, dq_ref, dk_ref, dv_ref,
             z_refs, ls_refs, sig_refs, sp_refs, gb_refs, pr_ref, gs_ref, copies):
        qi = pl.program_id(1)

        @pl.when(qi == 0)
        def _():
            dk_ref[...] = jnp.zeros_like(dk_ref)
            dv_ref[...] = jnp.zeros_like(dv_ref)

        if copies is not None:
            @pl.when(jnp.logical_and(pl.program_id(0) == 0, qi == 0))
            def _():
                _exchange_begin(copies)

        nslots = (qi + 1) * nc
        walked = used_ref[pl.program_id(0), qi].astype(jnp.int32)
        first = jnp.clip(nslots - walked, 0, nslots - nc) // nc * nc
        m0 = lax.broadcasted_iota(jnp.int32, (1, LANES), 1) < HEAD_DIM
        qs = _stack_heads(q_ref[...], m0)
        do = do_ref[...]
        dos = _stack_heads(do.astype(BF16), m0)
        dosl = _stack_heads((do * LN2).astype(BF16), m0)
        cmi = cmi_ref[...]
        cme = cme_ref[...]

        def chunk_at(i):
            return jnp.clip(i, first, nslots - 1)

        def scores(kc):
            return _dot(qs, _key_chunk(k_ref, kc), NT)

        def weights(ls, cs, da, pr, kc):
            a = jnp.exp2(ls - (pr - cs[:, :KEY_CHUNK]))
            gb = (a * da).astype(BF16)
            ks = pl.multiple_of(kc * KEY_CHUNK, KEY_CHUNK)
            dv_ref[pl.ds(ks, KEY_CHUNK), :] += _dot(a, dos, TN)
            return gb, jnp.exp2(ls), pr - cs[:, KEY_CHUNK:]

        def score_grads(gb, sig, cg, gs, dq, kc):
            dzb = (gb.astype(F32) * (1.0 - sig) - sig * (gs + cg[:, :KEY_CHUNK])).astype(BF16)
            ks = pl.multiple_of(kc * KEY_CHUNK, KEY_CHUNK)
            dk_ref[pl.ds(ks, KEY_CHUNK), :] += _dot(dzb, qs, TN)
            dq = dq + _dot(jnp.concatenate([dzb[:tq], dzb[tq:]], axis=1), _stack_heads(_key_chunk(k_ref, kc), m0), NN)
            return gs + cg[:, KEY_CHUNK:], dq

        def step(i, par, bias=None, stages="zswg"):
            cur, prv = par, 1 - par
            k1, k2 = chunk_at(i - 1), chunk_at(i - 2)
            z_next = scores(chunk_at(i + 1))
            if "w" in stages:
                cs = _dot(sp_refs[prv][...], cmi, NN)
                da = _dot(dosl, _key_chunk(v_ref, k1), NT)
            if "g" in stages:
                cg = _dot(gb_refs[cur][...], cme, NN)
            z = z_refs[cur][...]
            if bias is not None:
                z = z + bias
            sp, ls = _softplus2(z)
            sp_refs[cur][...] = sp.astype(BF16)
            ls_refs[cur][...] = ls
            if "g" in stages:
                gs, dq = score_grads(gb_refs[cur][...], sig_refs[cur][...], cg, gs_ref[...], dq_ref[...], k2)
                gs_ref[...] = gs
                dq_ref[...] = dq
            if "w" in stages:
                gb, sig, pr = weights(ls_refs[prv][...], cs, da, pr_ref[...], k1)
                gb_refs[prv][...] = gb
                sig_refs[prv][...] = sig
                pr_ref[...] = pr
            z_refs[prv][...] = z_next

        pr_ref[...] = jnp.concatenate([r_ref[:, :LANES], r_ref[:, LANES:]], axis=0)
        gs_ref[...] = jnp.zeros((2 * tq, LANES), F32)
        dq_ref[...] = jnp.zeros((tq, LANES), F32)
        z_refs[0][...] = scores(first)
        only_diagonal = first == nslots - nc
        step(first, 0, jnp.where(only_diagonal, bias_ref[0], 0.0), stages="zs")
        step(first + 1, 1, jnp.where(only_diagonal, bias_ref[1], 0.0), stages="zsw")

        def two_steps(j, _):
            step(2 * j, 0)
            step(2 * j + 1, 1)
            return 0

        lax.fori_loop(first // 2 + 1, nslots // 2 - 1, two_steps, 0)

        @pl.when(jnp.logical_not(only_diagonal))
        def _():
            step(nslots - 2, 0, bias_ref[0])
            step(nslots - 1, 1, bias_ref[1])

        k1, k2 = chunk_at(nslots - 1), chunk_at(nslots - 2)
        gb, sig, _ = weights(ls_refs[1][...], _dot(sp_refs[1][...], cmi, NN),
                             _dot(dosl, _key_chunk(v_ref, k1), NT), pr_ref[...], k1)
        gb2 = gb_refs[0][...]
        gs, dq = score_grads(gb2, sig_refs[0][...], _dot(gb2, cme, NN), gs_ref[...], dq_ref[...], k2)
        _, dq = score_grads(gb, sig, _dot(gb, cme, NN), gs, dq, k1)
        dq_ref[...] = dq

        if copies is not None:
            @pl.when(jnp.logical_and(pl.program_id(0) == nhp - 1, qi == nq - 1))
            def _():
                _exchange_finish(copies)

    def wrapped(*refs):
        ins, rest = refs[:9], refs[9:]
        srcs, rest = rest[:nt], rest[nt:]
        outs, rest = rest[:3], rest[3:]
        lands, rest = rest[:nt], rest[nt:]
        z0, z1, ls0, ls1, sg0, sg1, sp0, sp1, gb0, gb1, pr_ref, gs_ref = rest[:12]
        copies = _exchange_copies(srcs, lands, *rest[12:]) if nt else None
        body(*ins, *outs, (z0, z1), (ls0, ls1), (sg0, sg1), (sp0, sp1), (gb0, gb1), pr_ref, gs_ref, copies)

    assert nc == 2
    bias = _diag_bias(tq, True)
    bias = jnp.concatenate([bias[:, :, :KEY_CHUNK], bias[:, :, KEY_CHUNK:]], axis=1)
    qblk = pl.BlockSpec((tq, LANES), lambda p, i: (i, p))
    full = pl.BlockSpec((s, LANES), lambda p, i: (0, p))
    cmspec = pl.BlockSpec((KEY_CHUNK, 2 * KEY_CHUNK), lambda p, i: (0, 0))
    anyspec = pl.BlockSpec(memory_space=pl.ANY)
    shape = jax.ShapeDtypeStruct((s, nhp * LANES), F32)
    f32buf = pltpu.VMEM((2 * tq, LANES), F32)
    bf16buf = pltpu.VMEM((2 * tq, LANES), BF16)
    outs = pl.pallas_call(
        wrapped, name=name, grid=(nhp, nq),
        in_specs=[pl.BlockSpec(memory_space=pltpu.SMEM),
                  qblk,
                  pl.BlockSpec((s, LANES), lambda p, i: (0, nhp + p)),
                  pl.BlockSpec((s, LANES), lambda p, i: (0, 2 * nhp + p)),
                  qblk,
                  pl.BlockSpec((tq, 2 * LANES), lambda p, i: (i, p)),
                  cmspec, cmspec,
                  pl.BlockSpec((nc, 2 * tq, LANES), lambda p, i: (0, 0, 0))] + [anyspec] * nt,
        out_specs=[qblk, full, full] + [anyspec] * nt,
        out_shape=[shape, shape, shape] + [jax.ShapeDtypeStruct(t.shape, t.dtype) for t in travel],
        scratch_shapes=[f32buf] * 6 + [bf16buf] * 4 + [f32buf] * 2 + (_exchange_scratch(nt) if nt else []),
        compiler_params=_cparams("arbitrary", "arbitrary"),
    )(used, qk, qk, proj, dmix, rtot, _cumsum_matrix("upto"), _cumsum_matrix("before"), bias, *travel)
    return outs[0], outs[1], outs[2], list(outs[3:])


def _pair_cumsum_matrix(kind):
    j = lax.broadcasted_iota(jnp.int32, (2 * KEY_CHUNK, 4 * KEY_CHUNK), 0)
    c = lax.broadcasted_iota(jnp.int32, (2 * KEY_CHUNK, 4 * KEY_CHUNK), 1)
    same_head = (j // KEY_CHUNK) == ((c // KEY_CHUNK) % 2)
    jj, cc = j % KEY_CHUNK, c % KEY_CHUNK
    tri = {"after": jj > cc, "upto": jj <= cc, "before": jj < cc}[kind]
    return jnp.where(same_head & ((c >= 2 * KEY_CHUNK) | tri), 1.0, 0.0).astype(BF16)


def _diag_bias(tq, ascending):
    nc = tq // KEY_CHUNK
    shape = (nc, tq, 2 * KEY_CHUNK)
    d = lax.broadcasted_iota(jnp.int32, shape, 0)
    r = lax.broadcasted_iota(jnp.int32, shape, 1)
    c = lax.broadcasted_iota(jnp.int32, shape, 2) % KEY_CHUNK
    chunk = d if ascending else nc - 1 - d
    return jnp.where(chunk * KEY_CHUNK + c < r, 0.0, NEG_BIG).astype(F32)


def _attn_fwd(qk, proj, tq, name, mix_cols, shards=()):
    s = qk.shape[0]
    nhp = qk.shape[1] // (2 * LANES)
    nc = tq // KEY_CHUNK
    nq = s // tq
    ng = len(shards)
    assert nc == 2
    w = 2 * KEY_CHUNK

    def body(q_ref, k_ref, v_ref, cm_ref, bias_ref, o_ref, r_ref, used_ref, z_refs, ls_refs, cs_refs, ct_refs,
             sp_refs, ab_refs, acc_ref, gather):
        qi = pl.program_id(1)
        if gather is not None:
            @pl.when(jnp.logical_and(pl.program_id(0) == 0, qi == 0))
            def _():
                gather.begin()

        nslots = (qi + 1) * nc
        m0 = lax.broadcasted_iota(jnp.int32, (1, LANES), 1) < HEAD_DIM
        q = q_ref[...]
        cm = cm_ref[...]

        def chunk_at(i):
            return jnp.clip(nslots - 1 - i, 0, nslots - 1)

        def scores(kc):
            return _dot(q, _stack_heads(_key_chunk(k_ref, kc), m0), NT)

        def values(ab, kc):
            return _dot(ab, _stack_heads(_key_chunk(v_ref, kc), m0), NN)

        def step(i, par, bias=None, stages="zscwv"):
            cur, prv = par, 1 - par
            if "z" in stages:
                z_next = scores(chunk_at(i + 1))
            if "c" in stages:
                cs = _dot(sp_refs[prv][...], cm, NN)
            if "v" in stages:
                pv = values(ab_refs[prv][...], chunk_at(i - 3))
            if "w" in stages:
                rs = r_ref[...]
                r_ref[...] = rs + ct_refs[cur][...]
                ab_refs[cur][...] = jnp.exp2(ls_refs[cur][...] - cs_refs[cur][...] - rs).astype(BF16)
            if "s" in stages:
                z = z_refs[cur][...]
                if bias is not None:
                    z = z + bias
                sp, ls = _softplus2(z)
                sp_refs[cur][...] = sp.astype(BF16)
                ls_refs[cur][...] = ls
            if "v" in stages:
                acc_ref[...] += pv
            if "c" in stages:
                cs_refs[prv][...] = cs[:, :w]
                ct_refs[prv][...] = cs[:, w:]
            if "z" in stages:
                z_refs[prv][...] = z_next

        z_refs[0][...] = scores(chunk_at(0))
        ab_refs[1][...] = jnp.zeros((tq, w), BF16)
        r_ref[...] = jnp.zeros((tq, w), F32)
        acc_ref[...] = jnp.zeros((tq, LANES), F32)
        step(0, 0, bias_ref[0], stages="zs")
        step(1, 1, bias_ref[1], stages="zsc")

        def two_steps(carry):
            j, _ = carry
            step(2 * j, 0)
            step(2 * j + 1, 1)
            return j + 1, jnp.min(jnp.minimum(r_ref[:, :KEY_CHUNK], r_ref[:, KEY_CHUNK:]))

        pairs, low = lax.while_loop(lambda c: jnp.logical_and(c[0] < nslots // 2, c[1] < SATURATED), two_steps,
                                    (jnp.int32(1), jnp.float32(0.0)))
        entered = 2 * pairs
        saturated = low >= SATURATED

        @pl.when(saturated)
        def _():
            step(entered, 0, stages="v")

        @pl.when(jnp.logical_not(saturated))
        def _():
            step(entered, 0, stages="cwv")
            step(entered + 1, 1, stages="wv")
            step(entered + 2, 0, stages="v")

        o_ref[...] = acc_ref[...].astype(o_ref.dtype)
        used_ref[pl.program_id(0), qi] = jnp.where(saturated, entered - 2, entered).astype(F32)

        if gather is not None:
            @pl.when(jnp.logical_and(pl.program_id(0) == nhp - 1, qi == nq // 2))
            def _():
                gather.relay()

            @pl.when(jnp.logical_and(pl.program_id(0) == nhp - 1, qi == nq - 1))
            def _():
                gather.finish()

    def wrapped(*refs):
        ins, rest = refs[:5], refs[5:]
        srcs, rest = rest[:ng], rest[ng:]
        outs, rest = rest[:3], rest[3:]
        dsts, scratch = rest[:ng], rest[ng:]
        z, ls, cs, ct, sp, ab = [scratch[2 * j:2 * j + 2] for j in range(6)]
        gather = _Gather(srcs, dsts, *scratch[13:]) if ng else None
        body(*ins, *outs, z, ls, cs, ct, sp, ab, scratch[12], gather)

    f32buf = pltpu.VMEM((tq, w), F32)
    bf16buf = pltpu.VMEM((tq, w), BF16)
    anyspec = pl.BlockSpec(memory_space=pl.ANY)
    outs = pl.pallas_call(
        wrapped, name=name, grid=(nhp, nq),
        in_specs=[pl.BlockSpec((tq, LANES), lambda p, i: (i, p)),
                  pl.BlockSpec((s, LANES), lambda p, i: (0, nhp + p)),
                  pl.BlockSpec((s, LANES), lambda p, i: (0, 2 * nhp + p)),
                  pl.BlockSpec((w, 2 * w), lambda p, i: (0, 0)),
                  pl.BlockSpec((nc, tq, w), lambda p, i: (0, 0, 0))] + [anyspec] * ng,
        out_specs=[pl.BlockSpec((tq, LANES), lambda p, i: (i, p)),
                   pl.BlockSpec((tq, w), lambda p, i: (i, p)),
                   pl.BlockSpec(memory_space=pltpu.SMEM)] + [anyspec] * ng,
        out_shape=[jax.ShapeDtypeStruct((s, mix_cols), BF16),
                   jax.ShapeDtypeStruct((s, nhp * w), F32),
                   jax.ShapeDtypeStruct((nhp, nq), F32)] + _gathered_shapes(shards),
        scratch_shapes=([f32buf] * 8 + [bf16buf] * 4 + [pltpu.VMEM((tq, LANES), F32)]
                        + (_gather_scratch(ng) if ng else [])),
        compiler_params=_cparams("arbitrary", "arbitrary"),
    )(qk, qk, proj, _pair_cumsum_matrix("after"), _diag_bias(tq, False), *shards)
    return outs[0], outs[1], outs[2], list(outs[3:])


BLOCK_PAIR = 2
MXU_WIDTH = 256


def _side_by_side(b_ref):
    return jnp.concatenate([b_ref[p] for p in range(BLOCK_PAIR)], axis=1)


def _mm_blocks(h, ga, widx, tm, name):
    s, d = h.shape
    nb, cols = ga.shape[1], ga.shape[3]

    def body(a_ref, b_ref, o_ref):
        o_ref[...] = _dot(a_ref[...], _side_by_side(b_ref), NN).astype(o_ref.dtype)

    return pl.pallas_call(
        body, name=name, grid=(s // tm, nb // BLOCK_PAIR),
        in_specs=[pl.BlockSpec((tm, d), lambda i, j: (i, 0)),
                  pl.BlockSpec((None, BLOCK_PAIR, d, cols), lambda i, j: (widx, j, 0, 0))],
        out_specs=pl.BlockSpec((tm, BLOCK_PAIR * cols), lambda i, j: (i, j)),
        out_shape=jax.ShapeDtypeStruct((s, nb * cols), BF16),
        compiler_params=_cparams("parallel", "arbitrary"),
    )(h, ga)


def _mm_swiglu(h, ga, gidx, uidx, tm, name):
    s, d = h.shape
    nb, cols = ga.shape[1], ga.shape[3]

    def body(a_ref, bg_ref, bu_ref, g_ref, u_ref, act_ref):
        a = a_ref[...]
        g = _dot(a, _side_by_side(bg_ref), NN)
        u = _dot(a, _side_by_side(bu_ref), NN)
        g_ref[...] = g.astype(g_ref.dtype)
        u_ref[...] = u.astype(u_ref.dtype)
        act_ref[...] = (g * (1.0 / (1.0 + jnp.exp(-g))) * u).astype(act_ref.dtype)

    def wspec(idx):
        return pl.BlockSpec((None, BLOCK_PAIR, d, cols), lambda i, j: (idx, j, 0, 0))

    out = pl.BlockSpec((tm, BLOCK_PAIR * cols), lambda i, j: (i, j))
    shape = jax.ShapeDtypeStruct((s, nb * cols), BF16)
    return pl.pallas_call(
        body, name=name, grid=(s // tm, nb // BLOCK_PAIR),
        in_specs=[pl.BlockSpec((tm, d), lambda i, j: (i, 0)), wspec(gidx), wspec(uidx)],
        out_specs=[out, out, out], out_shape=[shape, shape, shape],
        compiler_params=_cparams("parallel", "arbitrary"),
    )(h, ga, ga)


def _mm_residual_norm(a, w3, lidx, res, gain, tm, name):
    s, k = a.shape
    n = w3.shape[2]

    def body(a_ref, b_ref, r_ref, g_ref, o_ref, h_ref):
        xv = r_ref[...] + _dot(a_ref[...], b_ref[...], NN)
        o_ref[...] = xv
        r = lax.rsqrt(jnp.mean(xv * xv, axis=-1, keepdims=True) + EPS)
        h_ref[...] = ((xv * r) * g_ref[...]).astype(h_ref.dtype)

    row = pl.BlockSpec((tm, n), lambda i: (i, 0))
    return pl.pallas_call(
        body, name=name, grid=(s // tm,),
        in_specs=[pl.BlockSpec((tm, k), lambda i: (i, 0)),
                  pl.BlockSpec((None, k, n), lambda i: (lidx, 0, 0), pipeline_mode=pl.Buffered(1)),
                  row, pl.BlockSpec((1, n), lambda i: (0, 0))],
        out_specs=[row, row],
        out_shape=[jax.ShapeDtypeStruct((s, n), F32), jax.ShapeDtypeStruct((s, n), BF16)],
        compiler_params=_cparams("parallel"),
    )(a, w3, res, gain)


def _mm_residual_loss(a, w3, lidx, res, target, tm, name):
    s, k = a.shape
    n = w3.shape[2]
    nsteps = s // tm

    def body(a_ref, b_ref, r_ref, t_ref, dy_ref, dyb_ref, l_ref, acc):
        i = pl.program_id(0)
        diff = r_ref[...] + _dot(a_ref[...], b_ref[...], NN) - t_ref[...]
        dy_ref[...] = diff * (1.0 / n)
        dyb_ref[...] = (diff * (1.0 / n)).astype(dyb_ref.dtype)
        part = jnp.sum((diff * diff).reshape(tm // 8, 8, n), axis=0)

        @pl.when(i == 0)
        def _():
            acc[...] = part

        @pl.when(i > 0)
        def _():
            acc[...] += part

        @pl.when(i == nsteps - 1)
        def _():
            tot = jnp.sum(jnp.sum(acc[...], axis=1, keepdims=True), axis=0, keepdims=True)
            l_ref[...] = jnp.broadcast_to(tot * (0.5 / n), (8, LANES))

    row = pl.BlockSpec((tm, n), lambda i: (i, 0))
    return pl.pallas_call(
        body, name=name, grid=(nsteps,),
        in_specs=[pl.BlockSpec((tm, k), lambda i: (i, 0)),
                  pl.BlockSpec((None, k, n), lambda i: (lidx, 0, 0), pipeline_mode=pl.Buffered(1)),
                  row, row],
        out_specs=[row, row, pl.BlockSpec((8, LANES), lambda i: (0, 0))],
        out_shape=[jax.ShapeDtypeStruct((s, n), F32), jax.ShapeDtypeStruct((s, n), BF16),
                   jax.ShapeDtypeStruct((8, LANES), F32)],
        scratch_shapes=[pltpu.VMEM((8, n), F32)],
        compiler_params=_cparams("arbitrary"),
    )(a, w3, res, target)


def _mm_nt(a, w3, lidx, tm, tn, name):
    s, k = a.shape
    n = w3.shape[1]

    def body(a_ref, b_ref, o_ref):
        o_ref[...] = _dot(a_ref[...], b_ref[...], NT)

    return pl.pallas_call(
        body, name=name, grid=(s // tm, n // tn),
        in_specs=[pl.BlockSpec((tm, k), lambda i, j: (i, 0)),
                  pl.BlockSpec((None, tn, k), lambda i, j: (lidx, j, 0))],
        out_specs=pl.BlockSpec((tm, tn), lambda i, j: (i, j)),
        out_shape=jax.ShapeDtypeStruct((s, n), F32),
        compiler_params=_cparams("parallel", "arbitrary"),
    )(a, w3)


def _mm_nt_swiglu_bwd(dx, wd3, lidx, g, u, tm, name, travel=()):
    s, d = dx.shape
    cols = BLOCK_PAIR * (g.shape[1] // N_DEV)

    def body(a_ref, b_ref, g_ref, u_ref, dg_ref, du_ref):
        a = a_ref[...]
        for c0 in range(0, cols, MXU_WIDTH):
            sl = slice(c0, c0 + MXU_WIDTH)
            dact = _dot(a, b_ref[sl, :], NT)
            gv = g_ref[:, sl].astype(F32)
            sig = 0.5 * jnp.tanh(0.5 * gv) + 0.5
            silu = gv * sig
            du_ref[:, sl] = (dact * silu).astype(du_ref.dtype)
            dsilu = sig + silu * (1.0 - sig)
            dg_ref[:, sl] = (dact * u_ref[:, sl].astype(F32) * dsilu).astype(dg_ref.dtype)

    blk = pl.BlockSpec((tm, cols), lambda i, j: (i, j))
    shape = jax.ShapeDtypeStruct(g.shape, BF16)
    grid = (s // tm, N_DEV // BLOCK_PAIR)
    body, more_in, more_out, more_shapes, more_scratch = _host_exchange(body, 4, 2, travel, grid)
    outs = pl.pallas_call(
        body, name=name, grid=grid,
        in_specs=[pl.BlockSpec((tm, d), lambda i, j: (i, 0)),
                  pl.BlockSpec((None, cols, d), lambda i, j: (lidx, j, 0)), blk, blk] + more_in,
        out_specs=[blk, blk] + more_out, out_shape=[shape, shape] + more_shapes,
        scratch_shapes=more_scratch,
        compiler_params=_cparams("arbitrary", "arbitrary"),
    )(dx, wd3, g, u, *travel)
    return outs[0], outs[1], list(outs[2:])


def _mm_nt_norm_bwd(das, ga, widxs, x, gain, dres, tm, name, travel=()):
    s = das[0].shape[0]
    nb, d, cols = ga.shape[1], ga.shape[2], ga.shape[3]
    nw = len(das)
    nsteps = s // tm

    def body(*refs):
        a_refs, b_refs = refs[:nw], refs[nw:2 * nw]
        x_ref, g_ref, dres_ref, dx_ref, dxb_ref, dg_ref = refs[2 * nw:]
        i = pl.program_id(0)
        dhv = None
        wide = BLOCK_PAIR * cols
        for w in range(nw):
            for k in range(nb // BLOCK_PAIR):
                b = jnp.concatenate([b_refs[w][BLOCK_PAIR * k + p] for p in range(BLOCK_PAIR)], axis=1)
                part = _dot(a_refs[w][:, k * wide:(k + 1) * wide], b, NT)
                dhv = part if dhv is None else dhv + part
        xv = x_ref[...]
        r = lax.rsqrt(jnp.mean(xv * xv, axis=-1, keepdims=True) + EPS)
        xhat = xv * r
        dxh = dhv * g_ref[...]
        dxv = dres_ref[...] + r * (dxh - xhat * jnp.mean(dxh * xhat, axis=-1, keepdims=True))
        dx_ref[...] = dxv
        dxb_ref[...] = dxv.astype(dxb_ref.dtype)
        part = jnp.sum((dhv * xhat).reshape(tm // 8, 8, d), axis=0)

        @pl.when(i == 0)
        def _():
            dg_ref[...] = part

        @pl.when(i > 0)
        def _():
            dg_ref[...] += part

        @pl.when(i == nsteps - 1)
        def _():
            dg_ref[...] = jnp.broadcast_to(jnp.sum(dg_ref[...], axis=0, keepdims=True), (8, d))

    def wspec(idx):
        return pl.BlockSpec((None, nb, d, cols), lambda i: (idx, 0, 0, 0), pipeline_mode=pl.Buffered(1))

    row = pl.BlockSpec((tm, d), lambda i: (i, 0))
    body, more_in, more_out, more_shapes, more_scratch = _host_exchange(body, 2 * nw + 3, 3, travel, (nsteps,))
    outs = pl.pallas_call(
        body, name=name, grid=(nsteps,),
        in_specs=([pl.BlockSpec((tm, nb * cols), lambda i: (i, 0))] * nw + [wspec(i) for i in widxs]
                  + [row, pl.BlockSpec((1, d), lambda i: (0, 0)), row] + more_in),
        out_specs=[row, row, pl.BlockSpec((8, d), lambda i: (0, 0))] + more_out,
        out_shape=[jax.ShapeDtypeStruct((s, d), F32), jax.ShapeDtypeStruct((s, d), BF16),
                   jax.ShapeDtypeStruct((8, d), F32)] + more_shapes,
        scratch_shapes=more_scratch,
        compiler_params=_cparams("arbitrary"),
    )(*das, *([ga] * nw), x, gain, dres, *travel)
    return outs[0], outs[1], outs[2], list(outs[3:])


def _mm_tn(a, b, ta, tb, tk, out_blocks, name, travel=()):
    s, ka = a.shape
    nb = b.shape[1]
    nk = s // tk
    cols = tb
    if out_blocks:
        tb = BLOCK_PAIR * cols

    def body(a_ref, b_ref, o_ref, ob_ref):
        k = pl.program_id(2)
        part = _dot(a_ref[...], b_ref[...], TN)

        def put(first):
            if out_blocks:
                for p in range(BLOCK_PAIR):
                    piece = part[:, p * cols:(p + 1) * cols]
                    o_ref[p] = piece if first else o_ref[p] + piece
            else:
                o_ref[...] = part if first else o_ref[...] + part

        @pl.when(k == 0)
        def _():
            put(True)

        @pl.when(k > 0)
        def _():
            put(False)

        @pl.when(k == nk - 1)
        def _():
            ob_ref[...] = o_ref[...].astype(ob_ref.dtype)

    if out_blocks:
        out_spec = pl.BlockSpec((BLOCK_PAIR, ta, cols), lambda i, j, k: (j, i, 0))
        shape = (nb // cols, ka, cols)
    else:
        out_spec = pl.BlockSpec((ta, tb), lambda i, j, k: (i, j))
        shape = (ka, nb)
    grid = (ka // ta, nb // tb, nk)
    body, more_in, more_out, more_shapes, more_scratch = _host_exchange(body, 2, 2, travel, grid)
    outs = pl.pallas_call(
        body, name=name, grid=grid,
        in_specs=[pl.BlockSpec((tk, ta), lambda i, j, k: (k, i)),
                  pl.BlockSpec((tk, tb), lambda i, j, k: (k, j))] + more_in,
        out_specs=[out_spec, out_spec] + more_out,
        out_shape=[jax.ShapeDtypeStruct(shape, F32), jax.ShapeDtypeStruct(shape, BF16)] + more_shapes,
        scratch_shapes=more_scratch,
        compiler_params=_cparams("arbitrary", "arbitrary", "arbitrary"),
    )(a, b, *travel)
    return (outs[0], outs[1]), list(outs[2:])


def _adamw(g, w, m, v, name):
    rows, cols = g.shape
    c1 = 1.0 / (1.0 - ADAM_B1 ** ADAM_STEP)
    c2 = 1.0 / (1.0 - ADAM_B2 ** ADAM_STEP)

    def body(p_ref, w_ref, m_ref, v_ref, g_ref, d_ref, nm_ref, nv_ref):
        gv = p_ref[...]
        nm = ADAM_B1 * m_ref[...] + (1.0 - ADAM_B1) * gv
        nv = ADAM_B2 * v_ref[...] + (1.0 - ADAM_B2) * (gv * gv)
        g_ref[...] = gv
        nm_ref[...] = nm
        nv_ref[...] = nv
        d_ref[...] = -ADAM_LR * ((nm * c1) / (jnp.sqrt(nv * c2) + ADAM_EPS) + ADAM_WD * w_ref[...])

    blk = pl.BlockSpec((rows, cols), lambda i: (0, 0))
    shape = jax.ShapeDtypeStruct((rows, cols), F32)
    return pl.pallas_call(
        body, name=name, grid=(1,),
        in_specs=[blk] * 4, out_specs=[blk] * 4, out_shape=[shape] * 4,
        compiler_params=_cparams("arbitrary"),
    )(g, w, m, v)


def _adamw_sharded(parts, grads, my, w, m, v, tr, name):
    depth, rows, cols = w.shape
    p, pr, pc = parts[0].shape
    c1 = 1.0 / (1.0 - ADAM_B1 ** ADAM_STEP)
    c2 = 1.0 / (1.0 - ADAM_B2 ** ADAM_STEP)

    def body(my_ref, *refs):
        p_refs, own_refs = refs[:depth], refs[depth:2 * depth]
        w_ref, m_ref, v_ref, g_ref, d_ref, nm_ref, nv_ref = refs[2 * depth:]
        layer = pl.program_id(0)
        for ll in range(depth):
            @pl.when(layer == ll)
            def _(ll=ll):
                mine = own_refs[ll][...]
                g = jnp.where(my_ref[0] == 0, mine, p_refs[ll][0].astype(F32))
                for k in range(1, p):
                    g = g + jnp.where(my_ref[0] == k, mine, p_refs[ll][k].astype(F32))
                g = g[:, :cols]
                nm = ADAM_B1 * m_ref[...] + (1.0 - ADAM_B1) * g
                nv = ADAM_B2 * v_ref[...] + (1.0 - ADAM_B2) * (g * g)
                g_ref[...] = g
                nm_ref[...] = nm
                nv_ref[...] = nv
                d_ref[...] = -ADAM_LR * ((nm * c1) / (jnp.sqrt(nv * c2) + ADAM_EPS) + ADAM_WD * w_ref[...])

    def row_block(ll, l, i):
        return jnp.where(l == ll, i, 0)

    blk = pl.BlockSpec((None, tr, cols), lambda l, i, my_: (l, i, 0))
    shape = jax.ShapeDtypeStruct((depth, rows, cols), F32)
    return pl.pallas_call(
        body, name=name,
        grid_spec=pltpu.PrefetchScalarGridSpec(
            num_scalar_prefetch=1, grid=(depth, rows // tr),
            in_specs=([pl.BlockSpec((p, tr, pc), lambda l, i, my_, ll=ll: (0, row_block(ll, l, i), 0))
                       for ll in range(depth)]
                      + [pl.BlockSpec((None, tr, pc), lambda l, i, my_, ll=ll: (my_[0], row_block(ll, l, i), 0))
                         for ll in range(depth)]
                      + [blk, blk, blk]),
            out_specs=[blk] * 4),
        out_shape=[shape] * 4,
        compiler_params=_cparams("arbitrary", "arbitrary"),
    )(my, *parts, *grads, w, m, v)


def _place():
    x, y, c = lax.axis_index("x"), lax.axis_index("y"), lax.axis_index("c")
    return x, y, c


class _Gather:
    def __init__(self, srcs, dsts, send_sems, recv_sems, local_sems):
        na = len(srcs)
        x, y, c = _place()
        me, sibling = (x, y, c), (x, y, 1 - c)
        chips = [(1 - x, y), (x, 1 - y), (1 - x, 1 - y)]

        def slot(a, dev):
            return dsts[a].at[:, pl.ds(4 * dev[0] + 2 * dev[1] + dev[2], 1)]

        def copy(k, a, block, to, from_shard=False):
            return pltpu.make_async_remote_copy(
                src_ref=srcs[a] if from_shard else slot(a, block), dst_ref=slot(a, block),
                send_sem=send_sems.at[k, a], recv_sem=recv_sems.at[k, a], device_id=to, device_id_type=MESH)

        pairs = [(j, chip, a) for j, chip in enumerate(chips) for a in range(na)]
        self.mine = [pltpu.make_async_copy(srcs[a], slot(a, me), local_sems.at[a]) for a in range(na)]
        self.first = [copy(0, a, me, sibling, True) for a in range(na)]
        self.first += [copy(1 + j, a, me, (*chip, c), True) for j, chip, a in pairs]
        self.over_ici = [copy(1 + j, a, (*chip, c), me) for j, chip, a in pairs]
        self.passed = [copy(4 + j, a, (*chip, c), sibling) for j, chip, a in pairs]
        self.from_sibling = [copy(0, a, sibling, me) for a in range(na)]
        self.from_sibling += [copy(4 + j, a, (*chip, 1 - c), me) for j, chip, a in pairs]

    def begin(self):
        for cp in self.mine + self.first:
            cp.start()

    def relay(self):
        for arrived, onward in zip(self.over_ici, self.passed):
            arrived.wait_recv()
            onward.start()

    def finish(self):
        for cp in self.from_sibling:
            cp.wait_recv()
        for cp in self.first + self.passed:
            cp.wait_send()
        for cp in self.mine:
            cp.wait()


def _gather_scratch(na):
    return [pltpu.SemaphoreType.DMA((7, na)), pltpu.SemaphoreType.DMA((7, na)), pltpu.SemaphoreType.DMA((na,))]


def _gathered_shapes(shards):
    return [jax.ShapeDtypeStruct((a.shape[0], N_DEV) + a.shape[2:], a.dtype) for a in shards]


_RELATIONS = [(dx, dy, dc) for dx in (0, 1) for dy in (0, 1) for dc in (0, 1)][1:]


def _flip(v, d):
    return 1 - v if d else v


def _exchange_copies(srcs, dsts, send_sems, recv_sems, local_sems):
    x, y, c = _place()
    my = 4 * x + 2 * y + c
    na = len(srcs)
    mine = [pltpu.make_async_copy(srcs[a].at[pl.ds(my, 1)], dsts[a].at[pl.ds(my, 1)], local_sems.at[a])
            for a in range(na)]
    sends, recvs = [], []
    for k, (dx, dy, dc) in enumerate(_RELATIONS):
        peer = (_flip(x, dx), _flip(y, dy), _flip(c, dc))
        pidx = 4 * peer[0] + 2 * peer[1] + peer[2]
        for a in range(na):
            for into, out in ((my, sends), (pidx, recvs)):
                out.append(pltpu.make_async_remote_copy(
                    src_ref=srcs[a].at[pl.ds(pidx, 1)], dst_ref=dsts[a].at[pl.ds(into, 1)],
                    send_sem=send_sems.at[k, a], recv_sem=recv_sems.at[k, a], device_id=peer, device_id_type=MESH))
    return mine, sends, recvs


def _exchange_begin(copies):
    mine, sends, _ = copies
    for cp in mine + sends:
        cp.start()


def _exchange_finish(copies):
    mine, sends, recvs = copies
    for cp in recvs:
        cp.wait_recv()
    for cp in sends:
        cp.wait_send()
    for cp in mine:
        cp.wait()


def _exchange_scratch(na):
    return [pltpu.SemaphoreType.DMA((7, na)), pltpu.SemaphoreType.DMA((7, na)), pltpu.SemaphoreType.DMA((na,))]


def _host_exchange(body, n_in, n_out, travel, grid):
    nt = len(travel)
    if not nt:
        return body, [], [], [], []

    def wrapped(*refs):
        ins, srcs = refs[:n_in], refs[n_in:n_in + nt]
        outs, rest = refs[n_in + nt:n_in + nt + n_out], refs[n_in + nt + n_out:]
        dsts, scratch = rest[:nt], rest[nt:]
        copies = _exchange_copies(srcs, dsts, *scratch[-3:])
        first = last = None
        for axis, size in enumerate(grid):
            at_start, at_end = pl.program_id(axis) == 0, pl.program_id(axis) == size - 1
            first = at_start if first is None else jnp.logical_and(first, at_start)
            last = at_end if last is None else jnp.logical_and(last, at_end)

        @pl.when(first)
        def _():
            _exchange_begin(copies)

        body(*ins, *outs, *scratch[:-3])

        @pl.when(last)
        def _():
            _exchange_finish(copies)

    anyspec = pl.BlockSpec(memory_space=pl.ANY)
    return (wrapped, [anyspec] * nt, [anyspec] * nt, [jax.ShapeDtypeStruct(t.shape, t.dtype) for t in travel],
            _exchange_scratch(nt))


def _all_reduce_small(v, name):
    r, c_ = v.shape

    def body(v_ref, o_ref, gath, send_sems, recv_sems):
        x, y, c = _place()
        my = 4 * x + 2 * y + c
        gath[my] = v_ref[...]
        sends = []
        for k, (dx, dy, dc) in enumerate(_RELATIONS):
            peer = (_flip(x, dx), _flip(y, dy), _flip(c, dc))
            cp = pltpu.make_async_remote_copy(
                src_ref=v_ref, dst_ref=gath.at[my], send_sem=send_sems.at[k], recv_sem=recv_sems.at[k],
                device_id=peer, device_id_type=MESH)
            cp.start()
            sends.append((cp, 4 * peer[0] + 2 * peer[1] + peer[2], k, peer))
        for cp, pidx, k, peer in sends:
            pltpu.make_async_remote_copy(
                src_ref=v_ref, dst_ref=gath.at[pidx], send_sem=send_sems.at[k], recv_sem=recv_sems.at[k],
                device_id=peer, device_id_type=MESH).wait_recv()
        for cp, *_ in sends:
            cp.wait_send()
        tot = gath[0]
        for k in range(1, N_DEV):
            tot = tot + gath[k]
        o_ref[...] = tot

    vm = pl.BlockSpec(memory_space=pltpu.VMEM)
    return pl.pallas_call(
        body, name=name, in_specs=[vm], out_specs=vm,
        out_shape=jax.ShapeDtypeStruct((r, c_), F32),
        scratch_shapes=[pltpu.VMEM((N_DEV, r, c_), F32), pltpu.SemaphoreType.DMA((7,)),
                        pltpu.SemaphoreType.DMA((7,))],
    )(v)


TM = 512
TM_MATMUL = 2048
TM_RESIDUAL = 1024
TQ = 256


def _device_blocks(t):
    return t.reshape(N_DEV, -1, t.shape[-1])


def _pad_to(a, axis, size):
    pad = [(0, 0)] * a.ndim
    pad[axis] = (0, size - a.shape[axis])
    return jnp.pad(a, pad)


def _local_step(x, target, first_shards, late_shards, conv_shard, norm_mix, q_norm, k_norm, norm_ffn):
    depth, d = norm_mix.shape
    cols = first_shards[0].shape[3]
    tm, tq = min(TM, x.shape[0]), min(TQ, x.shape[0])
    tmm, tmr = min(TM_MATMUL, x.shape[0]), min(TM_RESIDUAL, x.shape[0])
    attn = d // 2
    nheads = attn // HEAD_DIM
    scale = HEAD_DIM ** -0.5 * LOG2E
    saved = []
    h1, (g_in0, g_conv) = _rmsnorm_fwd(x, norm_mix[0][None], tm, "norm_mix_fwd_0", first_shards)
    conv_full = g_conv[0, :, :depth * 3, :conv_shard].transpose(1, 0, 2).reshape(depth, 3, N_DEV * conv_shard)
    for l in range(depth):
        w_in = (g_in0, 0) if l == 0 else (g_rest, 3 * (l - 1))
        proj = _mm_blocks(h1, *w_in, tmm, f"proj_in_{l}")
        qk_gain = jnp.concatenate([jnp.tile(q_norm[l], nheads) * scale, jnp.tile(k_norm[l], nheads)])[None]
        qk = _qknorm_fwd(proj, qk_gain, tmm, f"qknorm_fwd_{l}")
        o, rtot, used, gathered = _attn_fwd(qk, proj, tq, f"attn_fwd_{l}", d, late_shards if l == 0 else ())
        if l == 0:
            g_gu0, g_rest, gb, gc = gathered if depth > 1 else (gathered[0], None, *gathered[1:])
            gb = gb.reshape(depth, -1, d)
            gc = gc.reshape(depth, -1, d)
        w_gu = (g_gu0, 0, 1) if l == 0 else (g_rest, 3 * (l - 1) + 1, 3 * (l - 1) + 2)
        conv_w8 = _pad_to(conv_full[l], 0, 8)
        mix = _conv_fwd(proj, conv_w8, o, f"conv_fwd_{l}")
        x1, h2 = _mm_residual_norm(mix, gb, l, x, norm_ffn[l][None], tmr, f"proj_out_{l}")
        g, u, act = _mm_swiglu(h2, *w_gu, tmr, f"ffn_up_{l}")
        saved.append((x, h1, proj, qk_gain, qk, rtot, used, conv_w8, mix, x1, h2, g, u, act, w_in, w_gu))
        if l + 1 < depth:
            x, h1 = _mm_residual_norm(act, gc, l, x1, norm_mix[l + 1][None], tm, f"ffn_down_{l}")
        else:
            dx, dxb, loss = _mm_residual_loss(act, gc, l, x1, target, tm, f"ffn_down_{l}")

    grads = [None] * depth
    small = [None] * depth
    landed = {}
    for l in reversed(range(depth)):
        x0, h1, proj, qk_gain, qk, rtot, used, conv_w8, mix, x1, h2, g, u, act, w_in, w_gu = saved[l]
        d = x0.shape[1]
        late = [[], [], [], []]
        if l == 0:
            for n, i in enumerate(5 * ll + j for ll in range(1, depth) for j in (0, 1, 2, 4, 3)):
                late[n % 4].append(i)

        def ride(host):
            return late[host], [_device_blocks(grads[i // 5][i % 5][1]) for i in late[host]]

        idx, travel = ride(0)
        dg, du, arrived = _mm_nt_swiglu_bwd(dxb, gc, l, g, u, tmr, f"ffn_down_bwd_{l}", travel)
        landed.update(zip(idx, arrived))
        idx, travel = ride(1)
        d_wdown, arrived = _mm_tn(act, dxb, 768, d, tmm, False, f"dw_down_{l}", travel)
        landed.update(zip(idx, arrived))
        idx, travel = ride(2)
        d_wgate, arrived = _mm_tn(h2, dg, d, cols, tmm, True, f"dw_gate_{l}", travel)
        landed.update(zip(idx, arrived))
        idx, travel = ride(3)
        d_wup, arrived = _mm_tn(h2, du, d, cols, tmm, True, f"dw_up_{l}", travel)
        landed.update(zip(idx, arrived))
        dx1, dx1b, dg_ffn, _ = _mm_nt_norm_bwd([dg, du], w_gu[0], list(w_gu[1:]), x1, norm_ffn[l][None], dx, tm,
                                               f"ffn_up_bwd_{l}")
        dmix = _mm_nt(dx1b, gb, l, tmr, 512, f"proj_out_bwd_{l}")
        d_wout, _ = _mm_tn(mix, dx1b, 512, d, tmm, False, f"dw_out_{l}")
        dcb, dcc, dcu, dconv = _conv_bwd(dmix, proj, conv_w8, f"conv_bwd_{l}")
        rides = {1: d_wgate[1], 2: d_wup[1], 3: d_wout[1], 4: d_wdown[1]} if l == 0 else {}
        dq, dk, dv, arrived = _attn_bwd(qk, proj, dmix, rtot, used, tq, f"attn_bwd_{l}",
                                        [_device_blocks(t) for t in rides.values()])
        landed.update(zip(rides.keys(), arrived))
        dproj, dg_qk = _qknorm_bwd((dq, dk, dv, dcb, dcc, dcu), proj, qk_gain, tmm, f"qknorm_bwd_{l}")
        d_win, _ = _mm_tn(h1, dproj, d, cols, tmm, True, f"dw_in_{l}")
        dx, dxb, dg_mix, arrived = _mm_nt_norm_bwd(
            [dproj], w_in[0], [w_in[1]], x0, norm_mix[l][None], dx1, tm, f"proj_in_bwd_{l}",
            [_device_blocks(d_win[1])] if l == 0 else [])
        landed.update(zip([0], arrived))
        grads[l] = (d_win, d_wgate, d_wup, d_wout, d_wdown)
        dq_gain = jnp.sum(dg_qk[0, :attn].reshape(nheads, HEAD_DIM), axis=0) * scale
        dk_gain = jnp.sum(dg_qk[0, attn:].reshape(nheads, HEAD_DIM), axis=0)
        small[l] = (dg_mix[0], dg_ffn[0], dq_gain, dk_gain, dconv[:3])
    return loss, dx, grads, small, landed


def kernel(x, norm_mix, w_in, q_norm, k_norm, conv_w, w_out, norm_ffn, w_gate, w_up, w_down, loss_target, m_norm_mix, m_w_in, m_q_norm, m_k_norm, m_conv_w, m_w_out, m_norm_ffn, m_w_gate, m_w_up, m_w_down, v_norm_mix, v_w_in, v_q_norm, v_k_norm, v_conv_w, v_w_out, v_norm_ffn, v_w_gate, v_w_up, v_w_down):
    depth, d, in_shard = w_in.shape
    ff_shard = w_gate.shape[2]
    ff_pad = in_shard
    conv_shard = conv_w.shape[2]
    xs = x.reshape(x.shape[-2], d)
    target = loss_target.reshape(xs.shape)

    pa = jnp.stack([w_in, _pad_to(w_gate, 2, ff_pad), _pad_to(w_up, 2, ff_pad)], axis=1)
    pa = pa.reshape(3 * depth, 1, d, in_shard).astype(BF16)
    pd = _pad_to(_pad_to(conv_w.reshape(depth * 3, conv_shard), 0, 8), 1, LANES)[None, None]
    late_shards = [pa[1:3]] + ([pa[3:]] if depth > 1 else [])
    late_shards += [w_out.astype(BF16)[:, None], _pad_to(w_down, 1, ff_pad).astype(BF16)[:, None]]

    loss, grad_x, grads, small, landed = _local_step(xs, target, [pa[:1], pd], late_shards, conv_shard, norm_mix,
                                                     q_norm, k_norm, norm_ffn)

    x_, y_, c_ = _place()
    my = 4 * x_ + 2 * y_ + c_

    rows = []
    for l in range(depth):
        g_mix, g_ffn, g_q, g_k, g_conv = small[l]
        qkrow = _pad_to(jnp.concatenate([g_q, g_k]), 0, d)
        rows += [g_mix[None], g_ffn[None], qkrow[None], _pad_to(g_conv, 1, d)]
    nrow = 6 * depth
    packed = jnp.concatenate(rows + [_pad_to(loss[:1], 1, d)], axis=0)
    packed = _pad_to(packed, 0, ((nrow + 1 + 7) // 8) * 8)
    summed = _all_reduce_small(packed, "reduce_small")
    loss_out = summed[nrow, 0]

    my1 = my.astype(jnp.int32).reshape(1)

    def big(j, w, m, v, tr, name):
        return _adamw_sharded([landed[5 * l + j] for l in range(depth)],
                              [_device_blocks(grads[l][j][0]) for l in range(depth)], my1, w, m, v, tr, name)

    res = {"w_in": big(0, w_in, m_w_in, v_w_in, 256, "adamw_in"),
           "w_gate": big(1, w_gate, m_w_gate, v_w_gate, 256, "adamw_gate"),
           "w_up": big(2, w_up, m_w_up, v_w_up, 256, "adamw_up"),
           "w_out": big(3, w_out, m_w_out, v_w_out, w_out.shape[1], "adamw_out"),
           "w_down": big(4, w_down, m_w_down, v_w_down, ff_shard // 2, "adamw_down")}

    g_rows, w_rows, m_rows, v_rows = [], [], [], []
    for l in range(depth):
        base = l * 6
        conv_g = lax.dynamic_slice(summed[base + 3:base + 6], (0, my * conv_shard), (3, conv_shard))
        g_rows += [summed[base:base + 3], _pad_to(conv_g, 1, d)]
        for dst, (nm, qn, kn, nf, cw) in ((w_rows, (norm_mix, q_norm, k_norm, norm_ffn, conv_w)),
                                          (m_rows, (m_norm_mix, m_q_norm, m_k_norm, m_norm_ffn, m_conv_w)),
                                          (v_rows, (v_norm_mix, v_q_norm, v_k_norm, v_norm_ffn, v_conv_w))):
            dst += [nm[l][None], nf[l][None], _pad_to(jnp.concatenate([qn[l], kn[l]]), 0, d)[None],
                    _pad_to(cw[l], 1, d)]
    prow = ((nrow + 7) // 8) * 8
    gs, ws, ms, vs = [_pad_to(jnp.concatenate(t, axis=0), 0, prow) for t in (g_rows, w_rows, m_rows, v_rows)]
    sm = _adamw(gs, ws, ms, vs, "adamw_small")

    hd = q_norm.shape[1]

    def small_out(t, kind):
        per_layer = []
        for l in range(depth):
            base = l * 6
            per_layer.append({"norm_mix": t[base], "norm_ffn": t[base + 1], "q_norm": t[base + 2, :hd],
                              "k_norm": t[base + 2, hd:2 * hd], "conv_w": t[base + 3:base + 6, :conv_shard]}[kind])
        return jnp.stack(per_layer)

    def big_out(name, i):
        return res[name][i]

    outs = [loss_out, grad_x.reshape(x.shape)]
    for i in range(4):
        outs += [small_out(sm[i], "norm_mix"), big_out("w_in", i), small_out(sm[i], "q_norm"),
                 small_out(sm[i], "k_norm"), small_out(sm[i], "conv_w"), big_out("w_out", i),
                 small_out(sm[i], "norm_ffn"), big_out("w_gate", i), big_out("w_up", i), big_out("w_down", i)]
    return tuple(outs)
```

```python
import jax
import jax.numpy as jnp
from jax import lax
from jax.experimental import pallas as pl
from jax.experimental.pallas import tpu as pltpu

F32 = jnp.float32
BF16 = jnp.bfloat16
MESH = pl.DeviceIdType.MESH

N_DEV = 8
LANES = 128
HEAD_DIM = 64
KEY_CHUNK = 128
EPS = 1e-6
VMEM_LIMIT = 48 * 1024 * 1024

ADAM_LR = 0.001
ADAM_B1 = 0.9
ADAM_B2 = 0.999
ADAM_EPS = 1e-08
ADAM_WD = 0.01
ADAM_STEP = 10

NN = (((1,), (0,)), ((), ()))
NT = (((1,), (1,)), ((), ()))
TN = (((0,), (0,)), ((), ()))


def _dot(a, b, dims):
    return lax.dot_general(a.astype(BF16), b.astype(BF16), dims, preferred_element_type=F32)


def _cparams(*sem):
    return pltpu.CompilerParams(dimension_semantics=sem, vmem_limit_bytes=VMEM_LIMIT)


def _rmsnorm_fwd(x, gain, tm, name, shards=()):
    s, d = x.shape
    nsteps = s // tm
    ng = len(shards)

    def body(*refs):
        x_ref, g_ref, srcs = refs[0], refs[1], refs[2:2 + ng]
        o_ref, dsts, sems = refs[2 + ng], refs[3 + ng:3 + 2 * ng], refs[3 + 2 * ng:]
        i = pl.program_id(0)
        gather = _Gather(srcs, dsts, *sems) if ng else None
        if ng:
            @pl.when(i == 0)
            def _():
                gather.begin()

        xv = x_ref[...]
        r = lax.rsqrt(jnp.mean(xv * xv, axis=-1, keepdims=True) + EPS)
        o_ref[...] = ((xv * r) * g_ref[...]).astype(o_ref.dtype)
        if ng:
            @pl.when(i == nsteps - 1)
            def _():
                gather.relay()
                gather.finish()

    anyspec = pl.BlockSpec(memory_space=pl.ANY)
    outs = pl.pallas_call(
        body, name=name, grid=(nsteps,),
        in_specs=[pl.BlockSpec((tm, d), lambda i: (i, 0)), pl.BlockSpec((1, d), lambda i: (0, 0))] + [anyspec] * ng,
        out_specs=[pl.BlockSpec((tm, d), lambda i: (i, 0))] + [anyspec] * ng,
        out_shape=[jax.ShapeDtypeStruct((s, d), BF16)] + _gathered_shapes(shards),
        scratch_shapes=_gather_scratch(ng) if ng else [],
        compiler_params=_cparams("arbitrary"),
    )(x, gain, *shards)
    return outs[0], list(outs[1:])


def _group_mean_matrix():
    r = lax.broadcasted_iota(jnp.int32, (LANES, LANES), 0) // HEAD_DIM
    c = lax.broadcasted_iota(jnp.int32, (LANES, LANES), 1) // HEAD_DIM
    return jnp.where(r == c, 1.0 / HEAD_DIM, 0.0).astype(BF16)


def _group_mean(v, gm):
    return _dot(v, gm, NN)


def _qknorm_fwd(proj, gains, tm, name):
    s = proj.shape[0]
    ncol = gains.shape[1] // LANES

    def body(p_ref, g_ref, gm_ref, o_ref):
        xv = p_ref[...].astype(F32)
        r = lax.rsqrt(_group_mean(xv * xv, gm_ref[...]) + EPS)
        o_ref[...] = ((xv * r) * g_ref[...]).astype(o_ref.dtype)

    blk = pl.BlockSpec((tm, LANES), lambda i, j: (i, j))
    return pl.pallas_call(
        body, name=name, grid=(s // tm, ncol),
        in_specs=[blk, pl.BlockSpec((1, LANES), lambda i, j: (0, j)),
                  pl.BlockSpec((LANES, LANES), lambda i, j: (0, 0))],
        out_specs=blk,
        out_shape=jax.ShapeDtypeStruct((s, ncol * LANES), BF16),
        compiler_params=_cparams("parallel", "parallel"),
    )(proj, gains, _group_mean_matrix())


def _qknorm_bwd(dq, dk, proj, gains, tm, name):
    s = proj.shape[0]
    ncol = gains.shape[1] // LANES
    half = ncol // 2
    nsteps = s // tm

    def body(dq_ref, dk_ref, p_ref, g_ref, gm_ref, dx_ref, dg_ref):
        i = pl.program_id(1)
        gm = gm_ref[...]
        xv = p_ref[...].astype(F32)
        r = lax.rsqrt(_group_mean(xv * xv, gm) + EPS)
        xhat = xv * r
        dy = jnp.where(pl.program_id(0) < half, dq_ref[...], dk_ref[...])
        dxh = dy * g_ref[...]
        proj_ = _group_mean(dxh * xhat, gm)
        dx_ref[...] = (r * (dxh - xhat * proj_)).astype(dx_ref.dtype)
        part = jnp.sum((dy * xhat).reshape(tm // 8, 8, LANES), axis=0)

        @pl.when(i == 0)
        def _():
            dg_ref[...] = part

        @pl.when(i > 0)
        def _():
            dg_ref[...] += part

        @pl.when(i == nsteps - 1)
        def _():
            dg_ref[...] = jnp.broadcast_to(jnp.sum(dg_ref[...], axis=0, keepdims=True), (8, LANES))

    blk = pl.BlockSpec((tm, LANES), lambda j, i: (i, j))
    return pl.pallas_call(
        body, name=name, grid=(ncol, nsteps),
        in_specs=[pl.BlockSpec((tm, LANES), lambda j, i: (i, jnp.minimum(j, half - 1))),
                  pl.BlockSpec((tm, LANES), lambda j, i: (i, jnp.maximum(j - half, 0))),
                  blk, pl.BlockSpec((1, LANES), lambda j, i: (0, j)),
                  pl.BlockSpec((LANES, LANES), lambda j, i: (0, 0))],
        out_specs=[blk, pl.BlockSpec((8, LANES), lambda j, i: (0, j))],
        out_shape=[jax.ShapeDtypeStruct((s, ncol * LANES), BF16),
                   jax.ShapeDtypeStruct((8, ncol * LANES), F32)],
        compiler_params=_cparams("parallel", "arbitrary"),
    )(dq, dk, proj, gains, _group_mean_matrix())


CONV_ROWS = 256
HALO = 8


def _conv_fwd(proj, conv_w8, mix, name):
    s = proj.shape[0]
    nblk = conv_w8.shape[1] // LANES
    first = 3 * nblk
    nchunk = s // CONV_ROWS
    before = mix.shape[1] // LANES - nblk

    def body(cb_ref, cc_ref, cu_ref, w_ref, mix_ref, y_ref, hpad):
        del mix_ref
        hpad[pl.ds(0, 2 * HALO), :] = jnp.zeros((2 * HALO, LANES), F32)

        def fill(i, _):
            r0 = pl.multiple_of(i * CONV_ROWS, CONV_ROWS)
            hpad[pl.ds(r0 + 2 * HALO, CONV_ROWS), :] = (
                cc_ref[pl.ds(r0, CONV_ROWS), :].astype(F32) * cu_ref[pl.ds(r0, CONV_ROWS), :].astype(F32))
            return 0

        lax.fori_loop(0, nchunk, fill, 0)
        w0, w1, w2 = w_ref[0:1, :], w_ref[1:2, :], w_ref[2:3, :]

        def conv(i, _):
            r0 = pl.multiple_of(i * CONV_ROWS, CONV_ROWS)
            win = hpad[pl.ds(r0 + HALO, CONV_ROWS + HALO), :]
            c = (w2 * win[HALO:] + w1 * pltpu.roll(win, 1, 0)[HALO:] + w0 * pltpu.roll(win, 2, 0)[HALO:])
            y_ref[pl.ds(r0, CONV_ROWS), :] = (cb_ref[pl.ds(r0, CONV_ROWS), :].astype(F32) * c).astype(y_ref.dtype)
            return 0

        lax.fori_loop(0, nchunk, conv, 0)

    def col(off):
        return pl.BlockSpec((s, LANES), lambda j: (0, off + j))

    return pl.pallas_call(
        body, name=name, grid=(nblk,),
        in_specs=[col(first), col(first + nblk), col(first + 2 * nblk), pl.BlockSpec((8, LANES), lambda j: (0, j)),
                  pl.BlockSpec(memory_space=pl.ANY)],
        out_specs=pl.BlockSpec((s, LANES), lambda j: (0, before + j)),
        out_shape=jax.ShapeDtypeStruct(mix.shape, mix.dtype),
        scratch_shapes=[pltpu.VMEM((s + 2 * HALO, LANES), F32)],
        input_output_aliases={4: 0},
        compiler_params=_cparams("parallel"),
    )(proj, proj, proj, conv_w8, mix)


def _conv_bwd(dmix, proj, conv_w8, name):
    s = proj.shape[0]
    nblk = conv_w8.shape[1] // LANES
    first = 3 * nblk
    nchunk = s // CONV_ROWS

    def body(dy_ref, cb_ref, cc_ref, cu_ref, w_ref, dcb_ref, dcc_ref, dcu_ref, dw_ref, hpad, dcpad):
        hpad[pl.ds(0, 2 * HALO), :] = jnp.zeros((2 * HALO, LANES), F32)
        dcpad[pl.ds(s, 2 * HALO), :] = jnp.zeros((2 * HALO, LANES), F32)

        def fill(i, _):
            r0 = pl.multiple_of(i * CONV_ROWS, CONV_ROWS)
            hpad[pl.ds(r0 + 2 * HALO, CONV_ROWS), :] = (
                cc_ref[pl.ds(r0, CONV_ROWS), :].astype(F32) * cu_ref[pl.ds(r0, CONV_ROWS), :].astype(F32))
            return 0

        lax.fori_loop(0, nchunk, fill, 0)
        w0, w1, w2 = w_ref[0:1, :], w_ref[1:2, :], w_ref[2:3, :]

        def fold(v):
            return jnp.sum(v.reshape(CONV_ROWS // 8, 8, LANES), axis=0)

        def first_pass(i, acc):
            a0, a1, a2 = acc
            r0 = pl.multiple_of(i * CONV_ROWS, CONV_ROWS)
            win = hpad[pl.ds(r0 + HALO, CONV_ROWS + HALO), :]
            h0 = win[HALO:]
            h1 = pltpu.roll(win, 1, 0)[HALO:]
            h2 = pltpu.roll(win, 2, 0)[HALO:]
            c = w2 * h0 + w1 * h1 + w0 * h2
            dy = dy_ref[pl.ds(r0, CONV_ROWS), :]
            dcb_ref[pl.ds(r0, CONV_ROWS), :] = (dy * c).astype(dcb_ref.dtype)
            dc = dy * cb_ref[pl.ds(r0, CONV_ROWS), :].astype(F32)
            dcpad[pl.ds(r0, CONV_ROWS), :] = dc
            return a0 + fold(dc * h2), a1 + fold(dc * h1), a2 + fold(dc * h0)

        z8 = jnp.zeros((8, LANES), F32)
        a0, a1, a2 = lax.fori_loop(0, nchunk, first_pass, (z8, z8, z8))
        dw_ref[...] = jnp.concatenate(
            [jnp.sum(a0, axis=0, keepdims=True), jnp.sum(a1, axis=0, keepdims=True),
             jnp.sum(a2, axis=0, keepdims=True), jnp.zeros((5, LANES), F32)], axis=0)

        def second_pass(i, _):
            r0 = pl.multiple_of(i * CONV_ROWS, CONV_ROWS)
            win = dcpad[pl.ds(r0, CONV_ROWS + HALO), :]
            n = CONV_ROWS + HALO
            dh = (w2 * win[:CONV_ROWS] + w1 * pltpu.roll(win, n - 1, 0)[:CONV_ROWS]
                  + w0 * pltpu.roll(win, n - 2, 0)[:CONV_ROWS])
            dcc_ref[pl.ds(r0, CONV_ROWS), :] = (dh * cu_ref[pl.ds(r0, CONV_ROWS), :].astype(F32)).astype(dcc_ref.dtype)
            dcu_ref[pl.ds(r0, CONV_ROWS), :] = (dh * cc_ref[pl.ds(r0, CONV_ROWS), :].astype(F32)).astype(dcu_ref.dtype)
            return 0

        lax.fori_loop(0, nchunk, second_pass, 0)

    def col(off):
        return pl.BlockSpec((s, LANES), lambda j: (0, off + j))

    out = pl.BlockSpec((s, LANES), lambda j: (0, j))
    return pl.pallas_call(
        body, name=name, grid=(nblk,),
        in_specs=[col(nblk), col(first), col(first + nblk), col(first + 2 * nblk),
                  pl.BlockSpec((8, LANES), lambda j: (0, j))],
        out_specs=[out, out, out, pl.BlockSpec((8, LANES), lambda j: (0, j))],
        out_shape=[jax.ShapeDtypeStruct((s, nblk * LANES), BF16)] * 3 + [jax.ShapeDtypeStruct((8, nblk * LANES), F32)],
        scratch_shapes=[pltpu.VMEM((s + 2 * HALO, LANES), F32), pltpu.VMEM((s + 2 * HALO, LANES), F32)],
        compiler_params=_cparams("parallel"),
    )(dmix, proj, proj, proj, conv_w8)


LOG2E = 1.4426950408889634
LN2 = 0.6931471805599453
NEG_BIG = -1e30
SATURATED = 160.0


def _cumsum_matrix(kind):
    j = lax.broadcasted_iota(jnp.int32, (KEY_CHUNK, 2 * KEY_CHUNK), 0)
    c = lax.broadcasted_iota(jnp.int32, (KEY_CHUNK, 2 * KEY_CHUNK), 1)
    tri = {"after": j > c, "upto": j <= c, "before": j < c}[kind]
    return jnp.where((c >= KEY_CHUNK) | tri, 1.0, 0.0).astype(BF16)


def _stack_heads(t, m0):
    zero = jnp.zeros_like(t)
    return jnp.concatenate([jnp.where(m0, t, zero), jnp.where(m0, zero, t)], axis=0)


def _softplus2(z):
    sp = jnp.maximum(z, 0.0) + jnp.log2(1.0 + jnp.exp2(-jnp.abs(z)))
    return sp, z - sp


def _key_chunk(ref, kc):
    return ref[pl.ds(pl.multiple_of(kc * KEY_CHUNK, KEY_CHUNK), KEY_CHUNK), :]


def _attn_bwd(qk, proj, dmix, rtot, used, tq, name, travel=()):
    s = qk.shape[0]
    nhp = qk.shape[1] // (2 * LANES)
    nc = tq // KEY_CHUNK
    nq = s // tq
    nt = len(travel)

    def body(used_ref, q_ref, k_ref, v_ref, do_ref, r_ref, cmi_ref, cme_ref, bias_ref, dq_ref, dk_ref, dv_ref,
             z_refs, ls_refs, sig_refs, sp_refs, gb_refs, pr_ref, gs_ref, copies):
        qi = pl.program_id(1)

        @pl.when(qi == 0)
        def _():
            dk_ref[...] = jnp.zeros_like(dk_ref)
            dv_ref[...] = jnp.zeros_like(dv_ref)

        if copies is not None:
            @pl.when(jnp.logical_and(pl.program_id(0) == 0, qi == 0))
            def _():
                _exchange_begin(copies)

        nslots = (qi + 1) * nc
        walked = used_ref[pl.program_id(0), qi].astype(jnp.int32)
        first = jnp.clip(nslots - walked, 0, nslots - nc) // nc * nc
        m0 = lax.broadcasted_iota(jnp.int32, (1, LANES), 1) < HEAD_DIM
        qs = _stack_heads(q_ref[...], m0)
        do = do_ref[...]
        dos = _stack_heads(do.astype(BF16), m0)
        dosl = _stack_heads((do * LN2).astype(BF16), m0)
        cmi = cmi_ref[...]
        cme = cme_ref[...]

        def chunk_at(i):
            return jnp.clip(i, first, nslots - 1)

        def scores(kc):
            return _dot(qs, _key_chunk(k_ref, kc), NT)

        def weights(ls, cs, da, pr, kc):
            a = jnp.exp2(ls - (pr - cs[:, :KEY_CHUNK]))
            gb = (a * da).astype(BF16)
            ks = pl.multiple_of(kc * KEY_CHUNK, KEY_CHUNK)
            dv_ref[pl.ds(ks, KEY_CHUNK), :] += _dot(a, dos, TN)
            return gb, jnp.exp2(ls), pr - cs[:, KEY_CHUNK:]

        def score_grads(gb, sig, cg, gs, dq, kc):
            dzb = (gb.astype(F32) * (1.0 - sig) - sig * (gs + cg[:, :KEY_CHUNK])).astype(BF16)
            ks = pl.multiple_of(kc * KEY_CHUNK, KEY_CHUNK)
            dk_ref[pl.ds(ks, KEY_CHUNK), :] += _dot(dzb, qs, TN)
            dq = dq + _dot(jnp.concatenate([dzb[:tq], dzb[tq:]], axis=1), _stack_heads(_key_chunk(k_ref, kc), m0), NN)
            return gs + cg[:, KEY_CHUNK:], dq

        def step(i, par, bias=None, stages="zswg"):
            cur, prv = par, 1 - par
            k1, k2 = chunk_at(i - 1), chunk_at(i - 2)
            z_next = scores(chunk_at(i + 1))
            if "w" in stages:
                cs = _dot(sp_refs[prv][...], cmi, NN)
                da = _dot(dosl, _key_chunk(v_ref, k1), NT)
            if "g" in stages:
                cg = _dot(gb_refs[cur][...], cme, NN)
            z = z_refs[cur][...]
            if bias is not None:
                z = z + bias
            sp, ls = _softplus2(z)
            sp_refs[cur][...] = sp.astype(BF16)
            ls_refs[cur][...] = ls
            if "g" in stages:
                gs, dq = score_grads(gb_refs[cur][...], sig_refs[cur][...], cg, gs_ref[...], dq_ref[...], k2)
                gs_ref[...] = gs
                dq_ref[...] = dq
            if "w" in stages:
                gb, sig, pr = weights(ls_refs[prv][...], cs, da, pr_ref[...], k1)
                gb_refs[prv][...] = gb
                sig_refs[prv][...] = sig
                pr_ref[...] = pr
            z_refs[prv][...] = z_next

        pr_ref[...] = jnp.concatenate([r_ref[:, :LANES], r_ref[:, LANES:]], axis=0)
        gs_ref[...] = jnp.zeros((2 * tq, LANES), F32)
        dq_ref[...] = jnp.zeros((tq, LANES), F32)
        z_refs[0][...] = scores(first)
        only_diagonal = first == nslots - nc
        step(first, 0, jnp.where(only_diagonal, bias_ref[0], 0.0), stages="zs")
        step(first + 1, 1, jnp.where(only_diagonal, bias_ref[1], 0.0), stages="zsw")

        def two_steps(j, _):
            step(2 * j, 0)
            step(2 * j + 1, 1)
            return 0

        lax.fori_loop(first // 2 + 1, nslots // 2 - 1, two_steps, 0)

        @pl.when(jnp.logical_not(only_diagonal))
        def _():
            step(nslots - 2, 0, bias_ref[0])
            step(nslots - 1, 1, bias_ref[1])

        k1, k2 = chunk_at(nslots - 1), chunk_at(nslots - 2)
        gb, sig, _ = weights(ls_refs[1][...], _dot(sp_refs[1][...], cmi, NN),
                             _dot(dosl, _key_chunk(v_ref, k1), NT), pr_ref[...], k1)
        gb2 = gb_refs[0][...]
        gs, dq = score_grads(gb2, sig_refs[0][...], _dot(gb2, cme, NN), gs_ref[...], dq_ref[...], k2)
        _, dq = score_grads(gb, sig, _dot(gb, cme, NN), gs, dq, k1)
        dq_ref[...] = dq

        if copies is not None:
            @pl.when(jnp.logical_and(pl.program_id(0) == nhp - 1, qi == nq - 1))
            def _():
                _exchange_finish(copies)

    def wrapped(*refs):
        ins, rest = refs[:9], refs[9:]
        srcs, rest = rest[:nt], rest[nt:]
        outs, rest = rest[:3], rest[3:]
        lands, rest = rest[:nt], rest[nt:]
        z0, z1, ls0, ls1, sg0, sg1, sp0, sp1, gb0, gb1, pr_ref, gs_ref = rest[:12]
        copies = _exchange_copies(srcs, lands, *rest[12:]) if nt else None
        body(*ins, *outs, (z0, z1), (ls0, ls1), (sg0, sg1), (sp0, sp1), (gb0, gb1), pr_ref, gs_ref, copies)

    assert nc == 2
    bias = _diag_bias(tq, True)
    bias = jnp.concatenate([bias[:, :, :KEY_CHUNK], bias[:, :, KEY_CHUNK:]], axis=1)
    qblk = pl.BlockSpec((tq, LANES), lambda p, i: (i, p))
    full = pl.BlockSpec((s, LANES), lambda p, i: (0, p))
    cmspec = pl.BlockSpec((KEY_CHUNK, 2 * KEY_CHUNK), lambda p, i: (0, 0))
    anyspec = pl.BlockSpec(memory_space=pl.ANY)
    shape = jax.ShapeDtypeStruct((s, nhp * LANES), F32)
    f32buf = pltpu.VMEM((2 * tq, LANES), F32)
    bf16buf = pltpu.VMEM((2 * tq, LANES), BF16)
    outs = pl.pallas_call(
        wrapped, name=name, grid=(nhp, nq),
        in_specs=[pl.BlockSpec(memory_space=pltpu.SMEM),
                  qblk,
                  pl.BlockSpec((s, LANES), lambda p, i: (0, nhp + p)),
                  pl.BlockSpec((s, LANES), lambda p, i: (0, 2 * nhp + p)),
                  qblk,
                  pl.BlockSpec((tq, 2 * LANES), lambda p, i: (i, p)),
                  cmspec, cmspec,
                  pl.BlockSpec((nc, 2 * tq, LANES), lambda p, i: (0, 0, 0))] + [anyspec] * nt,
        out_specs=[qblk, full, full] + [anyspec] * nt,
        out_shape=[shape, shape, shape] + [jax.ShapeDtypeStruct(t.shape, t.dtype) for t in travel],
        scratch_shapes=[f32buf] * 6 + [bf16buf] * 4 + [f32buf] * 2 + (_exchange_scratch(nt) if nt else []),
        compiler_params=_cparams("arbitrary", "arbitrary"),
    )(used, qk, qk, proj, dmix, rtot, _cumsum_matrix("upto"), _cumsum_matrix("before"), bias, *travel)
    return outs[0], outs[1], outs[2], list(outs[3:])


def _pair_cumsum_matrix(kind):
    j = lax.broadcasted_iota(jnp.int32, (2 * KEY_CHUNK, 4 * KEY_CHUNK), 0)
    c = lax.broadcasted_iota(jnp.int32, (2 * KEY_CHUNK, 4 * KEY_CHUNK), 1)
    same_head = (j // KEY_CHUNK) == ((c // KEY_CHUNK) % 2)
    jj, cc = j % KEY_CHUNK, c % KEY_CHUNK
    tri = {"after": jj > cc, "upto": jj <= cc, "before": jj < cc}[kind]
    return jnp.where(same_head & ((c >= 2 * KEY_CHUNK) | tri), 1.0, 0.0).astype(BF16)


def _diag_bias(tq, ascending):
    nc = tq // KEY_CHUNK
    shape = (nc, tq, 2 * KEY_CHUNK)
    d = lax.broadcasted_iota(jnp.int32, shape, 0)
    r = lax.broadcasted_iota(jnp.int32, shape, 1)
    c = lax.broadcasted_iota(jnp.int32, shape, 2) % KEY_CHUNK
    chunk = d if ascending else nc - 1 - d
    return jnp.where(chunk * KEY_CHUNK + c < r, 0.0, NEG_BIG).astype(F32)


def _attn_fwd(qk, proj, tq, name, mix_cols, shards=()):
    s = qk.shape[0]
    nhp = qk.shape[1] // (2 * LANES)
    nc = tq // KEY_CHUNK
    nq = s // tq
    ng = len(shards)
    assert nc == 2
    w = 2 * KEY_CHUNK

    def body(q_ref, k_ref, v_ref, cm_ref, bias_ref, o_ref, r_ref, used_ref, z_refs, ls_refs, cs_refs, ct_refs,
             sp_refs, ab_refs, acc_ref, gather):
        qi = pl.program_id(1)
        if gather is not None:
            @pl.when(jnp.logical_and(pl.program_id(0) == 0, qi == 0))
            def _():
                gather.begin()

        nslots = (qi + 1) * nc
        m0 = lax.broadcasted_iota(jnp.int32, (1, LANES), 1) < HEAD_DIM
        q = q_ref[...]
        cm = cm_ref[...]

        def chunk_at(i):
            return jnp.clip(nslots - 1 - i, 0, nslots - 1)

        def scores(kc):
            return _dot(q, _stack_heads(_key_chunk(k_ref, kc), m0), NT)

        def values(ab, kc):
            return _dot(ab, _stack_heads(_key_chunk(v_ref, kc), m0), NN)

        def step(i, par, bias=None, stages="zscwv"):
            cur, prv = par, 1 - par
            if "z" in stages:
                z_next = scores(chunk_at(i + 1))
            if "c" in stages:
                cs = _dot(sp_refs[prv][...], cm, NN)
            if "v" in stages:
                pv = values(ab_refs[prv][...], chunk_at(i - 3))
            if "w" in stages:
                rs = r_ref[...]
                r_ref[...] = rs + ct_refs[cur][...]
                ab_refs[cur][...] = jnp.exp2(ls_refs[cur][...] - cs_refs[cur][...] - rs).astype(BF16)
            if "s" in stages:
                z = z_refs[cur][...]
                if bias is not None:
                    z = z + bias
                sp, ls = _softplus2(z)
                sp_refs[cur][...] = sp.astype(BF16)
                ls_refs[cur][...] = ls
            if "v" in stages:
                acc_ref[...] += pv
            if "c" in stages:
                cs_refs[prv][...] = cs[:, :w]
                ct_refs[prv][...] = cs[:, w:]
            if "z" in stages:
                z_refs[prv][...] = z_next

        z_refs[0][...] = scores(chunk_at(0))
        ab_refs[1][...] = jnp.zeros((tq, w), BF16)
        r_ref[...] = jnp.zeros((tq, w), F32)
        acc_ref[...] = jnp.zeros((tq, LANES), F32)
        step(0, 0, bias_ref[0], stages="zs")
        step(1, 1, bias_ref[1], stages="zsc")

        def two_steps(carry):
            j, _ = carry
            step(2 * j, 0)
            step(2 * j + 1, 1)
            return j + 1, jnp.min(jnp.minimum(r_ref[:, :KEY_CHUNK], r_ref[:, KEY_CHUNK:]))

        pairs, low = lax.while_loop(lambda c: jnp.logical_and(c[0] < nslots // 2, c[1] < SATURATED), two_steps,
                                    (jnp.int32(1), jnp.float32(0.0)))
        entered = 2 * pairs
        saturated = low >= SATURATED

        @pl.when(saturated)
        def _():
            step(entered, 0, stages="v")

        @pl.when(jnp.logical_not(saturated))
        def _():
            step(entered, 0, stages="cwv")
            step(entered + 1, 1, stages="wv")
            step(entered + 2, 0, stages="v")

        o_ref[...] = acc_ref[...].astype(o_ref.dtype)
        used_ref[pl.program_id(0), qi] = jnp.where(saturated, entered - 2, entered).astype(F32)

        if gather is not None:
            @pl.when(jnp.logical_and(pl.program_id(0) == nhp - 1, qi == nq // 2))
            def _():
                gather.relay()

            @pl.when(jnp.logical_and(pl.program_id(0) == nhp - 1, qi == nq - 1))
            def _():
                gather.finish()

    def wrapped(*refs):
        ins, rest = refs[:5], refs[5:]
        srcs, rest = rest[:ng], rest[ng:]
        outs, rest = rest[:3], rest[3:]
        dsts, scratch = rest[:ng], rest[ng:]
        z, ls, cs, ct, sp, ab = [scratch[2 * j:2 * j + 2] for j in range(6)]
        gather = _Gather(srcs, dsts, *scratch[13:]) if ng else None
        body(*ins, *outs, z, ls, cs, ct, sp, ab, scratch[12], gather)

    f32buf = pltpu.VMEM((tq, w), F32)
    bf16buf = pltpu.VMEM((tq, w), BF16)
    anyspec = pl.BlockSpec(memory_space=pl.ANY)
    outs = pl.pallas_call(
        wrapped, name=name, grid=(nhp, nq),
        in_specs=[pl.BlockSpec((tq, LANES), lambda p, i: (i, p)),
                  pl.BlockSpec((s, LANES), lambda p, i: (0, nhp + p)),
                  pl.BlockSpec((s, LANES), lambda p, i: (0, 2 * nhp + p)),
                  pl.BlockSpec((w, 2 * w), lambda p, i: (0, 0)),
                  pl.BlockSpec((nc, tq, w), lambda p, i: (0, 0, 0))] + [anyspec] * ng,
        out_specs=[pl.BlockSpec((tq, LANES), lambda p, i: (i, p)),
                   pl.BlockSpec((tq, w), lambda p, i: (i, p)),
                   pl.BlockSpec(memory_space=pltpu.SMEM)] + [anyspec] * ng,
        out_shape=[jax.ShapeDtypeStruct((s, mix_cols), BF16),
                   jax.ShapeDtypeStruct((s, nhp * w), F32),
                   jax.ShapeDtypeStruct((nhp, nq), F32)] + _gathered_shapes(shards),
        scratch_shapes=([f32buf] * 8 + [bf16buf] * 4 + [pltpu.VMEM((tq, LANES), F32)]
                        + (_gather_scratch(ng) if ng else [])),
        compiler_params=_cparams("arbitrary", "arbitrary"),
    )(qk, qk, proj, _pair_cumsum_matrix("after"), _diag_bias(tq, False), *shards)
    return outs[0], outs[1], outs[2], list(outs[3:])


BLOCK_PAIR = 2
MXU_WIDTH = 256


def _side_by_side(b_ref):
    return jnp.concatenate([b_ref[p] for p in range(BLOCK_PAIR)], axis=1)


def _mm_blocks(h, ga, widx, tm, name):
    s, d = h.shape
    nb, cols = ga.shape[1], ga.shape[3]

    def body(a_ref, b_ref, o_ref):
        o_ref[...] = _dot(a_ref[...], _side_by_side(b_ref), NN).astype(o_ref.dtype)

    return pl.pallas_call(
        body, name=name, grid=(s // tm, nb // BLOCK_PAIR),
        in_specs=[pl.BlockSpec((tm, d), lambda i, j: (i, 0)),
                  pl.BlockSpec((None, BLOCK_PAIR, d, cols), lambda i, j: (widx, j, 0, 0))],
        out_specs=pl.BlockSpec((tm, BLOCK_PAIR * cols), lambda i, j: (i, j)),
        out_shape=jax.ShapeDtypeStruct((s, nb * cols), BF16),
        compiler_params=_cparams("parallel", "arbitrary"),
    )(h, ga)


def _mm_swiglu(h, ga, gidx, uidx, tm, name):
    s, d = h.shape
    nb, cols = ga.shape[1], ga.shape[3]

    def body(a_ref, bg_ref, bu_ref, g_ref, u_ref, act_ref):
        a = a_ref[...]
        g = _dot(a, _side_by_side(bg_ref), NN)
        u = _dot(a, _side_by_side(bu_ref), NN)
        g_ref[...] = g.astype(g_ref.dtype)
        u_ref[...] = u.astype(u_ref.dtype)
        act_ref[...] = (g * (1.0 / (1.0 + jnp.exp(-g))) * u).astype(act_ref.dtype)

    def wspec(idx):
        return pl.BlockSpec((None, BLOCK_PAIR, d, cols), lambda i, j: (idx, j, 0, 0))

    out = pl.BlockSpec((tm, BLOCK_PAIR * cols), lambda i, j: (i, j))
    shape = jax.ShapeDtypeStruct((s, nb * cols), BF16)
    return pl.pallas_call(
        body, name=name, grid=(s // tm, nb // BLOCK_PAIR),
        in_specs=[pl.BlockSpec((tm, d), lambda i, j: (i, 0)), wspec(gidx), wspec(uidx)],
        out_specs=[out, out, out], out_shape=[shape, shape, shape],
        compiler_params=_cparams("parallel", "arbitrary"),
    )(h, ga, ga)


def _mm_residual_norm(a, w3, lidx, res, gain, tm, name):
    s, k = a.shape
    n = w3.shape[2]

    def body(a_ref, b_ref, r_ref, g_ref, o_ref, h_ref):
        xv = r_ref[...] + _dot(a_ref[...], b_ref[...], NN)
        o_ref[...] = xv
        r = lax.rsqrt(jnp.mean(xv * xv, axis=-1, keepdims=True) + EPS)
        h_ref[...] = ((xv * r) * g_ref[...]).astype(h_ref.dtype)

    row = pl.BlockSpec((tm, n), lambda i: (i, 0))
    return pl.pallas_call(
        body, name=name, grid=(s // tm,),
        in_specs=[pl.BlockSpec((tm, k), lambda i: (i, 0)),
                  pl.BlockSpec((None, k, n), lambda i: (lidx, 0, 0), pipeline_mode=pl.Buffered(1)),
                  row, pl.BlockSpec((1, n), lambda i: (0, 0))],
        out_specs=[row, row],
        out_shape=[jax.ShapeDtypeStruct((s, n), F32), jax.ShapeDtypeStruct((s, n), BF16)],
        compiler_params=_cparams("parallel"),
    )(a, w3, res, gain)


def _mm_residual_loss(a, w3, lidx, res, target, tm, name):
    s, k = a.shape
    n = w3.shape[2]
    nsteps = s // tm

    def body(a_ref, b_ref, r_ref, t_ref, dy_ref, dyb_ref, l_ref, acc):
        i = pl.program_id(0)
        diff = r_ref[...] + _dot(a_ref[...], b_ref[...], NN) - t_ref[...]
        dy_ref[...] = diff * (1.0 / n)
        dyb_ref[...] = (diff * (1.0 / n)).astype(dyb_ref.dtype)
        part = jnp.sum((diff * diff).reshape(tm // 8, 8, n), axis=0)

        @pl.when(i == 0)
        def _():
            acc[...] = part

        @pl.when(i > 0)
        def _():
            acc[...] += part

        @pl.when(i == nsteps - 1)
        def _():
            tot = jnp.sum(jnp.sum(acc[...], axis=1, keepdims=True), axis=0, keepdims=True)
            l_ref[...] = jnp.broadcast_to(tot * (0.5 / n), (8, LANES))

    row = pl.BlockSpec((tm, n), lambda i: (i, 0))
    return pl.pallas_call(
        body, name=name, grid=(nsteps,),
        in_specs=[pl.BlockSpec((tm, k), lambda i: (i, 0)),
                  pl.BlockSpec((None, k, n), lambda i: (lidx, 0, 0), pipeline_mode=pl.Buffered(1)),
                  row, row],
        out_specs=[row, row, pl.BlockSpec((8, LANES), lambda i: (0, 0))],
        out_shape=[jax.ShapeDtypeStruct((s, n), F32), jax.ShapeDtypeStruct((s, n), BF16),
                   jax.ShapeDtypeStruct((8, LANES), F32)],
        scratch_shapes=[pltpu.VMEM((8, n), F32)],
        compiler_params=_cparams("arbitrary"),
    )(a, w3, res, target)


def _mm_nt(a, w3, lidx, tm, tn, name):
    s, k = a.shape
    n = w3.shape[1]

    def body(a_ref, b_ref, o_ref):
        o_ref[...] = _dot(a_ref[...], b_ref[...], NT)

    return pl.pallas_call(
        body, name=name, grid=(s // tm, n // tn),
        in_specs=[pl.BlockSpec((tm, k), lambda i, j: (i, 0)),
                  pl.BlockSpec((None, tn, k), lambda i, j: (lidx, j, 0))],
        out_specs=pl.BlockSpec((tm, tn), lambda i, j: (i, j)),
        out_shape=jax.ShapeDtypeStruct((s, n), F32),
        compiler_params=_cparams("parallel", "arbitrary"),
    )(a, w3)


def _mm_nt_swiglu_bwd(dx, wd3, lidx, g, u, tm, name, travel=()):
    s, d = dx.shape
    cols = BLOCK_PAIR * (g.shape[1] // N_DEV)

    def body(a_ref, b_ref, g_ref, u_ref, dg_ref, du_ref):
        a = a_ref[...]
        for c0 in range(0, cols, MXU_WIDTH):
            sl = slice(c0, c0 + MXU_WIDTH)
            dact = _dot(a, b_ref[sl, :], NT)
            gv = g_ref[:, sl].astype(F32)
            sig = 0.5 * jnp.tanh(0.5 * gv) + 0.5
            silu = gv * sig
            du_ref[:, sl] = (dact * silu).astype(du_ref.dtype)
            dsilu = sig + silu * (1.0 - sig)
            dg_ref[:, sl] = (dact * u_ref[:, sl].astype(F32) * dsilu).astype(dg_ref.dtype)

    blk = pl.BlockSpec((tm, cols), lambda i, j: (i, j))
    shape = jax.ShapeDtypeStruct(g.shape, BF16)
    grid = (s // tm, N_DEV // BLOCK_PAIR)
    body, more_in, more_out, more_shapes, more_scratch = _host_exchange(body, 4, 2, travel, grid)
    outs = pl.pallas_call(
        body, name=name, grid=grid,
        in_specs=[pl.BlockSpec((tm, d), lambda i, j: (i, 0)),
                  pl.BlockSpec((None, cols, d), lambda i, j: (lidx, j, 0)), blk, blk] + more_in,
        out_specs=[blk, blk] + more_out, out_shape=[shape, shape] + more_shapes,
        scratch_shapes=more_scratch,
        compiler_params=_cparams("arbitrary", "arbitrary"),
    )(dx, wd3, g, u, *travel)
    return outs[0], outs[1], list(outs[2:])


def _mm_nt_norm_bwd(das, ga, widxs, x, gain, dres, tm, name, travel=()):
    s = das[0].shape[0]
    nb, d, cols = ga.shape[1], ga.shape[2], ga.shape[3]
    nw = len(das)
    nsteps = s // tm

    def body(*refs):
        a_refs, b_refs = refs[:nw], refs[nw:2 * nw]
        x_ref, g_ref, dres_ref, dx_ref, dxb_ref, dg_ref = refs[2 * nw:]
        i = pl.program_id(0)
        dhv = None
        wide = BLOCK_PAIR * cols
        for w in range(nw):
            for k in range(nb // BLOCK_PAIR):
                b = jnp.concatenate([b_refs[w][BLOCK_PAIR * k + p] for p in range(BLOCK_PAIR)], axis=1)
                part = _dot(a_refs[w][:, k * wide:(k + 1) * wide], b, NT)
                dhv = part if dhv is None else dhv + part
        xv = x_ref[...]
        r = lax.rsqrt(jnp.mean(xv * xv, axis=-1, keepdims=True) + EPS)
        xhat = xv * r
        dxh = dhv * g_ref[...]
        dxv = dres_ref[...] + r * (dxh - xhat * jnp.mean(dxh * xhat, axis=-1, keepdims=True))
        dx_ref[...] = dxv
        dxb_ref[...] = dxv.astype(dxb_ref.dtype)
        part = jnp.sum((dhv * xhat).reshape(tm // 8, 8, d), axis=0)

        @pl.when(i == 0)
        def _():
            dg_ref[...] = part

        @pl.when(i > 0)
        def _():
            dg_ref[...] += part

        @pl.when(i == nsteps - 1)
        def _():
            dg_ref[...] = jnp.broadcast_to(jnp.sum(dg_ref[...], axis=0, keepdims=True), (8, d))

    def wspec(idx):
        return pl.BlockSpec((None, nb, d, cols), lambda i: (idx, 0, 0, 0), pipeline_mode=pl.Buffered(1))

    row = pl.BlockSpec((tm, d), lambda i: (i, 0))
    body, more_in, more_out, more_shapes, more_scratch = _host_exchange(body, 2 * nw + 3, 3, travel, (nsteps,))
    outs = pl.pallas_call(
        body, name=name, grid=(nsteps,),
        in_specs=([pl.BlockSpec((tm, nb * cols), lambda i: (i, 0))] * nw + [wspec(i) for i in widxs]
                  + [row, pl.BlockSpec((1, d), lambda i: (0, 0)), row] + more_in),
        out_specs=[row, row, pl.BlockSpec((8, d), lambda i: (0, 0))] + more_out,
        out_shape=[jax.ShapeDtypeStruct((s, d), F32), jax.ShapeDtypeStruct((s, d), BF16),
                   jax.ShapeDtypeStruct((8, d), F32)] + more_shapes,
        scratch_shapes=more_scratch,
        compiler_params=_cparams("arbitrary"),
    )(*das, *([ga] * nw), x, gain, dres, *travel)
    return outs[0], outs[1], outs[2], list(outs[3:])


def _mm_tn(a, b, ta, tb, tk, out_blocks, name, travel=()):
    s, ka = a.shape
    nb = b.shape[1]
    nk = s // tk
    cols = tb
    if out_blocks:
        tb = BLOCK_PAIR * cols

    def body(a_ref, b_ref, o_ref, ob_ref):
        k = pl.program_id(2)
        part = _dot(a_ref[...], b_ref[...], TN)

        def put(first):
            if out_blocks:
                for p in range(BLOCK_PAIR):
                    piece = part[:, p * cols:(p + 1) * cols]
                    o_ref[p] = piece if first else o_ref[p] + piece
            else:
                o_ref[...] = part if first else o_ref[...] + part

        @pl.when(k == 0)
        def _():
            put(True)

        @pl.when(k > 0)
        def _():
            put(False)

        @pl.when(k == nk - 1)
        def _():
            ob_ref[...] = o_ref[...].astype(ob_ref.dtype)

    if out_blocks:
        out_spec = pl.BlockSpec((BLOCK_PAIR, ta, cols), lambda i, j, k: (j, i, 0))
        shape = (nb // cols, ka, cols)
    else:
        out_spec = pl.BlockSpec((ta, tb), lambda i, j, k: (i, j))
        shape = (ka, nb)
    grid = (ka // ta, nb // tb, nk)
    body, more_in, more_out, more_shapes, more_scratch = _host_exchange(body, 2, 2, travel, grid)
    outs = pl.pallas_call(
        body, name=name, grid=grid,
        in_specs=[pl.BlockSpec((tk, ta), lambda i, j, k: (k, i)),
                  pl.BlockSpec((tk, tb), lambda i, j, k: (k, j))] + more_in,
        out_specs=[out_spec, out_spec] + more_out,
        out_shape=[jax.ShapeDtypeStruct(shape, F32), jax.ShapeDtypeStruct(shape, BF16)] + more_shapes,
        scratch_shapes=more_scratch,
        compiler_params=_cparams("arbitrary", "arbitrary", "arbitrary"),
    )(a, b, *travel)
    return (outs[0], outs[1]), list(outs[2:])


def _adamw(g, w, m, v, name):
    rows, cols = g.shape
    c1 = 1.0 / (1.0 - ADAM_B1 ** ADAM_STEP)
    c2 = 1.0 / (1.0 - ADAM_B2 ** ADAM_STEP)

    def body(p_ref, w_ref, m_ref, v_ref, g_ref, d_ref, nm_ref, nv_ref):
        gv = p_ref[...]
        nm = ADAM_B1 * m_ref[...] + (1.0 - ADAM_B1) * gv
        nv = ADAM_B2 * v_ref[...] + (1.0 - ADAM_B2) * (gv * gv)
        g_ref[...] = gv
        nm_ref[...] = nm
        nv_ref[...] = nv
        d_ref[...] = -ADAM_LR * ((nm * c1) / (jnp.sqrt(nv * c2) + ADAM_EPS) + ADAM_WD * w_ref[...])

    blk = pl.BlockSpec((rows, cols), lambda i: (0, 0))
    shape = jax.ShapeDtypeStruct((rows, cols), F32)
    return pl.pallas_call(
        body, name=name, grid=(1,),
        in_specs=[blk] * 4, out_specs=[blk] * 4, out_shape=[shape] * 4,
        compiler_params=_cparams("arbitrary"),
    )(g, w, m, v)


def _adamw_sharded(parts, grads, my, w, m, v, tr, name):
    depth, rows, cols = w.shape
    p, pr, pc = parts[0].shape
    c1 = 1.0 / (1.0 - ADAM_B1 ** ADAM_STEP)
    c2 = 1.0 / (1.0 - ADAM_B2 ** ADAM_STEP)

    def body(my_ref, *refs):
        p_refs, own_refs = refs[:depth], refs[depth:2 * depth]
        w_ref, m_ref, v_ref, g_ref, d_ref, nm_ref, nv_ref = refs[2 * depth:]
        layer = pl.program_id(0)
        for ll in range(depth):
            @pl.when(layer == ll)
            def _(ll=ll):
                mine = own_refs[ll][...]
                g = jnp.where(my_ref[0] == 0, mine, p_refs[ll][0].astype(F32))
                for k in range(1, p):
                    g = g + jnp.where(my_ref[0] == k, mine, p_refs[ll][k].astype(F32))
                g = g[:, :cols]
                nm = ADAM_B1 * m_ref[...] + (1.0 - ADAM_B1) * g
                nv = ADAM_B2 * v_ref[...] + (1.0 - ADAM_B2) * (g * g)
                g_ref[...] = g
                nm_ref[...] = nm
                nv_ref[...] = nv
                d_ref[...] = -ADAM_LR * ((nm * c1) / (jnp.sqrt(nv * c2) + ADAM_EPS) + ADAM_WD * w_ref[...])

    def row_block(ll, l, i):
        return jnp.where(l == ll, i, 0)

    blk = pl.BlockSpec((None, tr, cols), lambda l, i, my_: (l, i, 0))
    shape = jax.ShapeDtypeStruct((depth, rows, cols), F32)
    return pl.pallas_call(
        body, name=name,
        grid_spec=pltpu.PrefetchScalarGridSpec(
            num_scalar_prefetch=1, grid=(depth, rows // tr),
            in_specs=([pl.BlockSpec((p, tr, pc), lambda l, i, my_, ll=ll: (0, row_block(ll, l, i), 0))
                       for ll in range(depth)]
                      + [pl.BlockSpec((None, tr, pc), lambda l, i, my_, ll=ll: (my_[0], row_block(ll, l, i), 0))
                         for ll in range(depth)]
                      + [blk, blk, blk]),
            out_specs=[blk] * 4),
        out_shape=[shape] * 4,
        compiler_params=_cparams("arbitrary", "arbitrary"),
    )(my, *parts, *grads, w, m, v)


def _place():
    x, y, c = lax.axis_index("x"), lax.axis_index("y"), lax.axis_index("c")
    return x, y, c


class _Gather:
    def __init__(self, srcs, dsts, send_sems, recv_sems, local_sems):
        na = len(srcs)
        x, y, c = _place()
        me, sibling = (x, y, c), (x, y, 1 - c)
        chips = [(1 - x, y), (x, 1 - y), (1 - x, 1 - y)]

        def slot(a, dev):
            return dsts[a].at[:, pl.ds(4 * dev[0] + 2 * dev[1] + dev[2], 1)]

        def copy(k, a, block, to, from_shard=False):
            return pltpu.make_async_remote_copy(
                src_ref=srcs[a] if from_shard else slot(a, block), dst_ref=slot(a, block),
                send_sem=send_sems.at[k, a], recv_sem=recv_sems.at[k, a], device_id=to, device_id_type=MESH)

        pairs = [(j, chip, a) for j, chip in enumerate(chips) for a in range(na)]
        self.mine = [pltpu.make_async_copy(srcs[a], slot(a, me), local_sems.at[a]) for a in range(na)]
        self.first = [copy(0, a, me, sibling, True) for a in range(na)]
        self.first += [copy(1 + j, a, me, (*chip, c), True) for j, chip, a in pairs]
        self.over_ici = [copy(1 + j, a, (*chip, c), me) for j, chip, a in pairs]
        self.passed = [copy(4 + j, a, (*chip, c), sibling) for j, chip, a in pairs]
        self.from_sibling = [copy(0, a, sibling, me) for a in range(na)]
        self.from_sibling += [copy(4 + j, a, (*chip, 1 - c), me) for j, chip, a in pairs]

    def begin(self):
        for cp in self.mine + self.first:
            cp.start()

    def relay(self):
        for arrived, onward in zip(self.over_ici, self.passed):
            arrived.wait_recv()
            onward.start()

    def finish(self):
        for cp in self.from_sibling:
            cp.wait_recv()
        for cp in self.first + self.passed:
            cp.wait_send()
        for cp in self.mine:
            cp.wait()


def _gather_scratch(na):
    return [pltpu.SemaphoreType.DMA((7, na)), pltpu.SemaphoreType.DMA((7, na)), pltpu.SemaphoreType.DMA((na,))]


def _gathered_shapes(shards):
    return [jax.ShapeDtypeStruct((a.shape[0], N_DEV) + a.shape[2:], a.dtype) for a in shards]


_RELATIONS = [(dx, dy, dc) for dx in (0, 1) for dy in (0, 1) for dc in (0, 1)][1:]


def _flip(v, d):
    return 1 - v if d else v


def _exchange_copies(srcs, dsts, send_sems, recv_sems, local_sems):
    x, y, c = _place()
    my = 4 * x + 2 * y + c
    na = len(srcs)
    mine = [pltpu.make_async_copy(srcs[a].at[pl.ds(my, 1)], dsts[a].at[pl.ds(my, 1)], local_sems.at[a])
            for a in range(na)]
    sends, recvs = [], []
    for k, (dx, dy, dc) in enumerate(_RELATIONS):
        peer = (_flip(x, dx), _flip(y, dy), _flip(c, dc))
        pidx = 4 * peer[0] + 2 * peer[1] + peer[2]
        for a in range(na):
            for into, out in ((my, sends), (pidx, recvs)):
                out.append(pltpu.make_async_remote_copy(
                    src_ref=srcs[a].at[pl.ds(pidx, 1)], dst_ref=dsts[a].at[pl.ds(into, 1)],
                    send_sem=send_sems.at[k, a], recv_sem=recv_sems.at[k, a], device_id=peer, device_id_type=MESH))
    return mine, sends, recvs


def _exchange_begin(copies):
    mine, sends, _ = copies
    for cp in mine + sends:
        cp.start()


def _exchange_finish(copies):
    mine, sends, recvs = copies
    for cp in recvs:
        cp.wait_recv()
    for cp in sends:
        cp.wait_send()
    for cp in mine:
        cp.wait()


def _exchange_scratch(na):
    return [pltpu.SemaphoreType.DMA((7, na)), pltpu.SemaphoreType.DMA((7, na)), pltpu.SemaphoreType.DMA((na,))]


def _host_exchange(body, n_in, n_out, travel, grid):
    nt = len(travel)
    if not nt:
        return body, [], [], [], []

    def wrapped(*refs):
        ins, srcs = refs[:n_in], refs[n_in:n_in + nt]
        outs, rest = refs[n_in + nt:n_in + nt + n_out], refs[n_in + nt + n_out:]
        dsts, scratch = rest[:nt], rest[nt:]
        copies = _exchange_copies(srcs, dsts, *scratch[-3:])
        first = last = None
        for axis, size in enumerate(grid):
            at_start, at_end = pl.program_id(axis) == 0, pl.program_id(axis) == size - 1
            first = at_start if first is None else jnp.logical_and(first, at_start)
            last = at_end if last is None else jnp.logical_and(last, at_end)

        @pl.when(first)
        def _():
            _exchange_begin(copies)

        body(*ins, *outs, *scratch[:-3])

        @pl.when(last)
        def _():
            _exchange_finish(copies)

    anyspec = pl.BlockSpec(memory_space=pl.ANY)
    return (wrapped, [anyspec] * nt, [anyspec] * nt, [jax.ShapeDtypeStruct(t.shape, t.dtype) for t in travel],
            _exchange_scratch(nt))


def _all_reduce_small(v, name):
    r, c_ = v.shape

    def body(v_ref, o_ref, gath, send_sems, recv_sems):
        x, y, c = _place()
        my = 4 * x + 2 * y + c
        gath[my] = v_ref[...]
        sends = []
        for k, (dx, dy, dc) in enumerate(_RELATIONS):
            peer = (_flip(x, dx), _flip(y, dy), _flip(c, dc))
            cp = pltpu.make_async_remote_copy(
                src_ref=v_ref, dst_ref=gath.at[my], send_sem=send_sems.at[k], recv_sem=recv_sems.at[k],
                device_id=peer, device_id_type=MESH)
            cp.start()
            sends.append((cp, 4 * peer[0] + 2 * peer[1] + peer[2], k, peer))
        for cp, pidx, k, peer in sends:
            pltpu.make_async_remote_copy(
                src_ref=v_ref, dst_ref=gath.at[pidx], send_sem=send_sems.at[k], recv_sem=recv_sems.at[k],
                device_id=peer, device_id_type=MESH).wait_recv()
        for cp, *_ in sends:
            cp.wait_send()
        tot = gath[0]
        for k in range(1, N_DEV):
            tot = tot + gath[k]
        o_ref[...] = tot

    vm = pl.BlockSpec(memory_space=pltpu.VMEM)
    return pl.pallas_call(
        body, name=name, in_specs=[vm], out_specs=vm,
        out_shape=jax.ShapeDtypeStruct((r, c_), F32),
        scratch_shapes=[pltpu.VMEM((N_DEV, r, c_), F32), pltpu.SemaphoreType.DMA((7,)),
                        pltpu.SemaphoreType.DMA((7,))],
    )(v)


TM = 512
TM_MATMUL = 2048
TM_RESIDUAL = 1024
TQ = 256


def _device_blocks(t):
    return t.reshape(N_DEV, -1, t.shape[-1])


def _pad_to(a, axis, size):
    pad = [(0, 0)] * a.ndim
    pad[axis] = (0, size - a.shape[axis])
    return jnp.pad(a, pad)


def _local_step(x, target, first_shards, late_shards, conv_shard, norm_mix, q_norm, k_norm, norm_ffn):
    depth, d = norm_mix.shape
    cols = first_shards[0].shape[3]
    tm, tq = min(TM, x.shape[0]), min(TQ, x.shape[0])
    tmm, tmr = min(TM_MATMUL, x.shape[0]), min(TM_RESIDUAL, x.shape[0])
    attn = d // 2
    nheads = attn // HEAD_DIM
    scale = HEAD_DIM ** -0.5 * LOG2E
    saved = []
    h1, (g_in0, g_conv) = _rmsnorm_fwd(x, norm_mix[0][None], tm, "norm_mix_fwd_0", first_shards)
    conv_full = g_conv[0, :, :depth * 3, :conv_shard].transpose(1, 0, 2).reshape(depth, 3, N_DEV * conv_shard)
    for l in range(depth):
        w_in = (g_in0, 0) if l == 0 else (g_rest, 3 * (l - 1))
        proj = _mm_blocks(h1, *w_in, tmm, f"proj_in_{l}")
        qk_gain = jnp.concatenate([jnp.tile(q_norm[l], nheads) * scale, jnp.tile(k_norm[l], nheads)])[None]
        qk = _qknorm_fwd(proj, qk_gain, tmm, f"qknorm_fwd_{l}")
        o, rtot, used, gathered = _attn_fwd(qk, proj, tq, f"attn_fwd_{l}", d, late_shards if l == 0 else ())
        if l == 0:
            g_gu0, g_rest, gb, gc = gathered if depth > 1 else (gathered[0], None, *gathered[1:])
            gb = gb.reshape(depth, -1, d)
            gc = gc.reshape(depth, -1, d)
        w_gu = (g_gu0, 0, 1) if l == 0 else (g_rest, 3 * (l - 1) + 1, 3 * (l - 1) + 2)
        conv_w8 = _pad_to(conv_full[l], 0, 8)
        mix = _conv_fwd(proj, conv_w8, o, f"conv_fwd_{l}")
        x1, h2 = _mm_residual_norm(mix, gb, l, x, norm_ffn[l][None], tmr, f"proj_out_{l}")
        g, u, act = _mm_swiglu(h2, *w_gu, tmr, f"ffn_up_{l}")
        saved.append((x, h1, proj, qk_gain, qk, rtot, used, conv_w8, mix, x1, h2, g, u, act, w_in, w_gu))
        if l + 1 < depth:
            x, h1 = _mm_residual_norm(act, gc, l, x1, norm_mix[l + 1][None], tm, f"ffn_down_{l}")
        else:
            dx, dxb, loss = _mm_residual_loss(act, gc, l, x1, target, tm, f"ffn_down_{l}")

    grads = [None] * depth
    small = [None] * depth
    landed = {}
    for l in reversed(range(depth)):
        x0, h1, proj, qk_gain, qk, rtot, used, conv_w8, mix, x1, h2, g, u, act, w_in, w_gu = saved[l]
        d = x0.shape[1]
        late = [[], [], [], []]
        if l == 0:
            for n, i in enumerate(5 * ll + j for ll in range(1, depth) for j in (0, 1, 2, 4, 3)):
                late[n % 4].append(i)

        def ride(host):
            return late[host], [_device_blocks(grads[i // 5][i % 5][1]) for i in late[host]]

        idx, travel = ride(0)
        dg, du, arrived = _mm_nt_swiglu_bwd(dxb, gc, l, g, u, tmr, f"ffn_down_bwd_{l}", travel)
        landed.update(zip(idx, arrived))
        idx, travel = ride(1)
        d_wdown, arrived = _mm_tn(act, dxb, 768, d, tmm, False, f"dw_down_{l}", travel)
        landed.update(zip(idx, arrived))
        idx, travel = ride(2)
        d_wgate, arrived = _mm_tn(h2, dg, d, cols, tmm, True, f"dw_gate_{l}", travel)
        landed.update(zip(idx, arrived))
        idx, travel = ride(3)
        d_wup, arrived = _mm_tn(h2, du, d, cols, tmm, True, f"dw_up_{l}", travel)
        landed.update(zip(idx, arrived))
        dx1, dx1b, dg_ffn, _ = _mm_nt_norm_bwd([dg, du], w_gu[0], list(w_gu[1:]), x1, norm_ffn[l][None], dx, tm,
                                               f"ffn_up_bwd_{l}")
        dmix = _mm_nt(dx1b, gb, l, tmr, 512, f"proj_out_bwd_{l}")
        d_wout, _ = _mm_tn(mix, dx1b, 512, d, tmm, False, f"dw_out_{l}")
        dcb, dcc, dcu, dconv = _conv_bwd(dmix, proj, conv_w8, f"conv_bwd_{l}")
        rides = {1: d_wgate[1], 2: d_wup[1], 3: d_wout[1], 4: d_wdown[1]} if l == 0 else {}
        dq, dk, dv, arrived = _attn_bwd(qk, proj, dmix, rtot, used, tq, f"attn_bwd_{l}",
                                        [_device_blocks(t) for t in rides.values()])
        landed.update(zip(rides.keys(), arrived))
        dqk, dg_qk = _qknorm_bwd(dq, dk, proj, qk_gain, tmm, f"qknorm_bwd_{l}")
        dproj = jnp.concatenate([dqk, dv.astype(BF16), dcb, dcc, dcu], axis=1)
        d_win, _ = _mm_tn(h1, dproj, d, cols, tmm, True, f"dw_in_{l}")
        dx, dxb, dg_mix, arrived = _mm_nt_norm_bwd(
            [dproj], w_in[0], [w_in[1]], x0, norm_mix[l][None], dx1, tm, f"proj_in_bwd_{l}",
            [_device_blocks(d_win[1])] if l == 0 else [])
        landed.update(zip([0], arrived))
        grads[l] = (d_win, d_wgate, d_wup, d_wout, d_wdown)
        dq_gain = jnp.sum(dg_qk[0, :attn].reshape(nheads, HEAD_DIM), axis=0) * scale
        dk_gain = jnp.sum(dg_qk[0, attn:].reshape(nheads, HEAD_DIM), axis=0)
        small[l] = (dg_mix[0], dg_ffn[0], dq_gain, dk_gain, dconv[:3])
    return loss, dx, grads, small, landed


def kernel(x, norm_mix, w_in, q_norm, k_norm, conv_w, w_out, norm_ffn, w_gate, w_up, w_down, loss_target, m_norm_mix, m_w_in, m_q_norm, m_k_norm, m_conv_w, m_w_out, m_norm_ffn, m_w_gate, m_w_up, m_w_down, v_norm_mix, v_w_in, v_q_norm, v_k_norm, v_conv_w, v_w_out, v_norm_ffn, v_w_gate, v_w_up, v_w_down):
    depth, d, in_shard = w_in.shape
    ff_shard = w_gate.shape[2]
    ff_pad = in_shard
    conv_shard = conv_w.shape[2]
    xs = x.reshape(x.shape[-2], d)
    target = loss_target.reshape(xs.shape)

    pa = jnp.stack([w_in, _pad_to(w_gate, 2, ff_pad), _pad_to(w_up, 2, ff_pad)], axis=1)
    pa = pa.reshape(3 * depth, 1, d, in_shard).astype(BF16)
    pd = _pad_to(_pad_to(conv_w.reshape(depth * 3, conv_shard), 0, 8), 1, LANES)[None, None]
    late_shards = [pa[1:3]] + ([pa[3:]] if depth > 1 else [])
    late_shards += [w_out.astype(BF16)[:, None], _pad_to(w_down, 1, ff_pad).astype(BF16)[:, None]]

    loss, grad_x, grads, small, landed = _local_step(xs, target, [pa[:1], pd], late_shards, conv_shard, norm_mix,
                                                     q_norm, k_norm, norm_ffn)

    x_, y_, c_ = _place()
    my = 4 * x_ + 2 * y_ + c_

    rows = []
    for l in range(depth):
        g_mix, g_ffn, g_q, g_k, g_conv = small[l]
        qkrow = _pad_to(jnp.concatenate([g_q, g_k]), 0, d)
        rows += [g_mix[None], g_ffn[None], qkrow[None], _pad_to(g_conv, 1, d)]
    nrow = 6 * depth
    packed = jnp.concatenate(rows + [_pad_to(loss[:1], 1, d)], axis=0)
    packed = _pad_to(packed, 0, ((nrow + 1 + 7) // 8) * 8)
    summed = _all_reduce_small(packed, "reduce_small")
    loss_out = summed[nrow, 0]

    my1 = my.astype(jnp.int32).reshape(1)

    def big(j, w, m, v, tr, name):
        return _adamw_sharded([landed[5 * l + j] for l in range(depth)],
                              [_device_blocks(grads[l][j][0]) for l in range(depth)], my1, w, m, v, tr, name)

    res = {"w_in": big(0, w_in, m_w_in, v_w_in, 256, "adamw_in"),
           "w_gate": big(1, w_gate, m_w_gate, v_w_gate, 256, "adamw_gate"),
           "w_up": big(2, w_up, m_w_up, v_w_up, 256, "adamw_up"),
           "w_out": big(3, w_out, m_w_out, v_w_out, w_out.shape[1], "adamw_out"),
           "w_down": big(4, w_down, m_w_down, v_w_down, ff_shard // 2, "adamw_down")}

    g_rows, w_rows, m_rows, v_rows = [], [], [], []
    for l in range(depth):
        base = l * 6
        conv_g = lax.dynamic_slice(summed[base + 3:base + 6], (0, my * conv_shard), (3, conv_shard))
        g_rows += [summed[base:base + 3], _pad_to(conv_g, 1, d)]
        for dst, (nm, qn, kn, nf, cw) in ((w_rows, (norm_mix, q_norm, k_norm, norm_ffn, conv_w)),
                                          (m_rows, (m_norm_mix, m_q_norm, m_k_norm, m_norm_ffn, m_conv_w)),
                                          (v_rows, (v_norm_mix, v_q_norm, v_k_norm, v_norm_ffn, v_conv_w))):
            dst += [nm[l][None], nf[l][None], _pad_to(jnp.concatenate([qn[l], kn[l]]), 0, d)[None],
                    _pad_to(cw[l], 1, d)]
    prow = ((nrow + 7) // 8) * 8
    gs, ws, ms, vs = [_pad_to(jnp.concatenate(t, axis=0), 0, prow) for t in (g_rows, w_rows, m_rows, v_rows)]
    sm = _adamw(gs, ws, ms, vs, "adamw_small")

    hd = q_norm.shape[1]

    def small_out(t, kind):
        per_layer = []
        for l in range(depth):
            base = l * 6
            per_layer.append({"norm_mix": t[base], "norm_ffn": t[base + 1], "q_norm": t[base + 2, :hd],
                              "k_norm": t[base + 2, hd:2 * hd], "conv_w": t[base + 3:base + 6, :conv_shard]}[kind])
        return jnp.stack(per_layer)

    def big_out(name, i):
        return res[name][i]

    outs = [loss_out, grad_x.reshape(x.shape)]
    for i in range(4):
        outs += [small_out(sm[i], "norm_mix"), big_out("w_in", i), small_out(sm[i], "q_norm"),
                 small_out(sm[i], "k_norm"), small_out(sm[i], "conv_w"), big_out("w_out", i),
                 small_out(sm[i], "norm_ffn"), big_out("w_gate", i), big_out("w_up", i), big_out("w_down", i)]
    return tuple(outs)
```

```python
import jax
import jax.numpy as jnp
from jax import lax
from jax.experimental import pallas as pl
from jax.experimental.pallas import tpu as pltpu

F32 = jnp.float32
BF16 = jnp.bfloat16
MESH = pl.DeviceIdType.MESH

N_DEV = 8
LANES = 128
HEAD_DIM = 64
KEY_CHUNK = 128
EPS = 1e-6
VMEM_LIMIT = 48 * 1024 * 1024

ADAM_LR = 0.001
ADAM_B1 = 0.9
ADAM_B2 = 0.999
ADAM_EPS = 1e-08
ADAM_WD = 0.01
ADAM_STEP = 10

NN = (((1,), (0,)), ((), ()))
NT = (((1,), (1,)), ((), ()))
TN = (((0,), (0,)), ((), ()))


def _dot(a, b, dims):
    return lax.dot_general(a.astype(BF16), b.astype(BF16), dims, preferred_element_type=F32)


def _cparams(*sem):
    return pltpu.CompilerParams(dimension_semantics=sem, vmem_limit_bytes=VMEM_LIMIT)


def _rmsnorm_fwd(x, gain, tm, name, shards=()):
    s, d = x.shape
    nsteps = s // tm
    ng = len(shards)

    def body(*refs):
        x_ref, g_ref, srcs = refs[0], refs[1], refs[2:2 + ng]
        o_ref, dsts, sems = refs[2 + ng], refs[3 + ng:3 + 2 * ng], refs[3 + 2 * ng:]
        i = pl.program_id(0)
        gather = _Gather(srcs, dsts, *sems) if ng else None
        if ng:
            @pl.when(i == 0)
            def _():
                gather.begin()

        xv = x_ref[...]
        r = lax.rsqrt(jnp.mean(xv * xv, axis=-1, keepdims=True) + EPS)
        o_ref[...] = ((xv * r) * g_ref[...]).astype(o_ref.dtype)
        if ng:
            @pl.when(i == nsteps - 1)
            def _():
                gather.relay()
                gather.finish()

    anyspec = pl.BlockSpec(memory_space=pl.ANY)
    outs = pl.pallas_call(
        body, name=name, grid=(nsteps,),
        in_specs=[pl.BlockSpec((tm, d), lambda i: (i, 0)), pl.BlockSpec((1, d), lambda i: (0, 0))] + [anyspec] * ng,
        out_specs=[pl.BlockSpec((tm, d), lambda i: (i, 0))] + [anyspec] * ng,
        out_shape=[jax.ShapeDtypeStruct((s, d), BF16)] + _gathered_shapes(shards),
        scratch_shapes=_gather_scratch(ng) if ng else [],
        compiler_params=_cparams("arbitrary"),
    )(x, gain, *shards)
    return outs[0], list(outs[1:])


def _group_mean_matrix():
    r = lax.broadcasted_iota(jnp.int32, (LANES, LANES), 0) // HEAD_DIM
    c = lax.broadcasted_iota(jnp.int32, (LANES, LANES), 1) // HEAD_DIM
    return jnp.where(r == c, 1.0 / HEAD_DIM, 0.0).astype(BF16)


def _group_mean(v, gm):
    hi = v.astype(BF16)
    lo = (v - hi.astype(F32)).astype(BF16)
    return _dot(hi, gm, NN) + _dot(lo, gm, NN)


def _qknorm_fwd(proj, gains, tm, name):
    s = proj.shape[0]
    ncol = gains.shape[1] // LANES

    def body(p_ref, g_ref, gm_ref, o_ref):
        xv = p_ref[...].astype(F32)
        r = lax.rsqrt(_group_mean(xv * xv, gm_ref[...]) + EPS)
        o_ref[...] = ((xv * r) * g_ref[...]).astype(o_ref.dtype)

    blk = pl.BlockSpec((tm, LANES), lambda i, j: (i, j))
    return pl.pallas_call(
        body, name=name, grid=(s // tm, ncol),
        in_specs=[blk, pl.BlockSpec((1, LANES), lambda i, j: (0, j)),
                  pl.BlockSpec((LANES, LANES), lambda i, j: (0, 0))],
        out_specs=blk,
        out_shape=jax.ShapeDtypeStruct((s, ncol * LANES), BF16),
        compiler_params=_cparams("parallel", "parallel"),
    )(proj, gains, _group_mean_matrix())


def _qknorm_bwd(dq, dk, proj, gains, tm, name):
    s = proj.shape[0]
    ncol = gains.shape[1] // LANES
    half = ncol // 2
    nsteps = s // tm

    def body(dq_ref, dk_ref, p_ref, g_ref, gm_ref, dx_ref, dg_ref):
        i = pl.program_id(1)
        gm = gm_ref[...]
        xv = p_ref[...].astype(F32)
        r = lax.rsqrt(_group_mean(xv * xv, gm) + EPS)
        xhat = xv * r
        dy = jnp.where(pl.program_id(0) < half, dq_ref[...], dk_ref[...])
        dxh = dy * g_ref[...]
        proj_ = _group_mean(dxh * xhat, gm)
        dx_ref[...] = (r * (dxh - xhat * proj_)).astype(dx_ref.dtype)
        part = jnp.sum((dy * xhat).reshape(tm // 8, 8, LANES), axis=0)

        @pl.when(i == 0)
        def _():
            dg_ref[...] = part

        @pl.when(i > 0)
        def _():
            dg_ref[...] += part

        @pl.when(i == nsteps - 1)
        def _():
            dg_ref[...] = jnp.broadcast_to(jnp.sum(dg_ref[...], axis=0, keepdims=True), (8, LANES))

    blk = pl.BlockSpec((tm, LANES), lambda j, i: (i, j))
    return pl.pallas_call(
        body, name=name, grid=(ncol, nsteps),
        in_specs=[pl.BlockSpec((tm, LANES), lambda j, i: (i, jnp.minimum(j, half - 1))),
                  pl.BlockSpec((tm, LANES), lambda j, i: (i, jnp.maximum(j - half, 0))),
                  blk, pl.BlockSpec((1, LANES), lambda j, i: (0, j)),
                  pl.BlockSpec((LANES, LANES), lambda j, i: (0, 0))],
        out_specs=[blk, pl.BlockSpec((8, LANES), lambda j, i: (0, j))],
        out_shape=[jax.ShapeDtypeStruct((s, ncol * LANES), BF16),
                   jax.ShapeDtypeStruct((8, ncol * LANES), F32)],
        compiler_params=_cparams("parallel", "arbitrary"),
    )(dq, dk, proj, gains, _group_mean_matrix())


CONV_ROWS = 256
HALO = 8


def _conv_fwd(proj, conv_w8, mix, name):
    s = proj.shape[0]
    nblk = conv_w8.shape[1] // LANES
    first = 3 * nblk
    nchunk = s // CONV_ROWS
    before = mix.shape[1] // LANES - nblk

    def body(cb_ref, cc_ref, cu_ref, w_ref, mix_ref, y_ref, hpad):
        del mix_ref
        hpad[pl.ds(0, 2 * HALO), :] = jnp.zeros((2 * HALO, LANES), F32)

        def fill(i, _):
            r0 = pl.multiple_of(i * CONV_ROWS, CONV_ROWS)
            hpad[pl.ds(r0 + 2 * HALO, CONV_ROWS), :] = (
                cc_ref[pl.ds(r0, CONV_ROWS), :].astype(F32) * cu_ref[pl.ds(r0, CONV_ROWS), :].astype(F32))
            return 0

        lax.fori_loop(0, nchunk, fill, 0)
        w0, w1, w2 = w_ref[0:1, :], w_ref[1:2, :], w_ref[2:3, :]

        def conv(i, _):
            r0 = pl.multiple_of(i * CONV_ROWS, CONV_ROWS)
            win = hpad[pl.ds(r0 + HALO, CONV_ROWS + HALO), :]
            c = (w2 * win[HALO:] + w1 * pltpu.roll(win, 1, 0)[HALO:] + w0 * pltpu.roll(win, 2, 0)[HALO:])
            y_ref[pl.ds(r0, CONV_ROWS), :] = (cb_ref[pl.ds(r0, CONV_ROWS), :].astype(F32) * c).astype(y_ref.dtype)
            return 0

        lax.fori_loop(0, nchunk, conv, 0)

    def col(off):
        return pl.BlockSpec((s, LANES), lambda j: (0, off + j))

    return pl.pallas_call(
        body, name=name, grid=(nblk,),
        in_specs=[col(first), col(first + nblk), col(first + 2 * nblk), pl.BlockSpec((8, LANES), lambda j: (0, j)),
                  pl.BlockSpec(memory_space=pl.ANY)],
        out_specs=pl.BlockSpec((s, LANES), lambda j: (0, before + j)),
        out_shape=jax.ShapeDtypeStruct(mix.shape, mix.dtype),
        scratch_shapes=[pltpu.VMEM((s + 2 * HALO, LANES), F32)],
        input_output_aliases={4: 0},
        compiler_params=_cparams("parallel"),
    )(proj, proj, proj, conv_w8, mix)


def _conv_bwd(dmix, proj, conv_w8, name):
    s = proj.shape[0]
    nblk = conv_w8.shape[1] // LANES
    first = 3 * nblk
    nchunk = s // CONV_ROWS

    def body(dy_ref, cb_ref, cc_ref, cu_ref, w_ref, dcb_ref, dcc_ref, dcu_ref, dw_ref, hpad, dcpad):
        hpad[pl.ds(0, 2 * HALO), :] = jnp.zeros((2 * HALO, LANES), F32)
        dcpad[pl.ds(s, 2 * HALO), :] = jnp.zeros((2 * HALO, LANES), F32)

        def fill(i, _):
            r0 = pl.multiple_of(i * CONV_ROWS, CONV_ROWS)
            hpad[pl.ds(r0 + 2 * HALO, CONV_ROWS), :] = (
                cc_ref[pl.ds(r0, CONV_ROWS), :].astype(F32) * cu_ref[pl.ds(r0, CONV_ROWS), :].astype(F32))
            return 0

        lax.fori_loop(0, nchunk, fill, 0)
        w0, w1, w2 = w_ref[0:1, :], w_ref[1:2, :], w_ref[2:3, :]

        def fold(v):
            return jnp.sum(v.reshape(CONV_ROWS // 8, 8, LANES), axis=0)

        def first_pass(i, acc):
            a0, a1, a2 = acc
            r0 = pl.multiple_of(i * CONV_ROWS, CONV_ROWS)
            win = hpad[pl.ds(r0 + HALO, CONV_ROWS + HALO), :]
            h0 = win[HALO:]
            h1 = pltpu.roll(win, 1, 0)[HALO:]
            h2 = pltpu.roll(win, 2, 0)[HALO:]
            c = w2 * h0 + w1 * h1 + w0 * h2
            dy = dy_ref[pl.ds(r0, CONV_ROWS), :]
            dcb_ref[pl.ds(r0, CONV_ROWS), :] = (dy * c).astype(dcb_ref.dtype)
            dc = dy * cb_ref[pl.ds(r0, CONV_ROWS), :].astype(F32)
            dcpad[pl.ds(r0, CONV_ROWS), :] = dc
            return a0 + fold(dc * h2), a1 + fold(dc * h1), a2 + fold(dc * h0)

        z8 = jnp.zeros((8, LANES), F32)
        a0, a1, a2 = lax.fori_loop(0, nchunk, first_pass, (z8, z8, z8))
        dw_ref[...] = jnp.concatenate(
            [jnp.sum(a0, axis=0, keepdims=True), jnp.sum(a1, axis=0, keepdims=True),
             jnp.sum(a2, axis=0, keepdims=True), jnp.zeros((5, LANES), F32)], axis=0)

        def second_pass(i, _):
            r0 = pl.multiple_of(i * CONV_ROWS, CONV_ROWS)
            win = dcpad[pl.ds(r0, CONV_ROWS + HALO), :]
            n = CONV_ROWS + HALO
            dh = (w2 * win[:CONV_ROWS] + w1 * pltpu.roll(win, n - 1, 0)[:CONV_ROWS]
                  + w0 * pltpu.roll(win, n - 2, 0)[:CONV_ROWS])
            dcc_ref[pl.ds(r0, CONV_ROWS), :] = (dh * cu_ref[pl.ds(r0, CONV_ROWS), :].astype(F32)).astype(dcc_ref.dtype)
            dcu_ref[pl.ds(r0, CONV_ROWS), :] = (dh * cc_ref[pl.ds(r0, CONV_ROWS), :].astype(F32)).astype(dcu_ref.dtype)
            return 0

        lax.fori_loop(0, nchunk, second_pass, 0)

    def col(off):
        return pl.BlockSpec((s, LANES), lambda j: (0, off + j))

    out = pl.BlockSpec((s, LANES), lambda j: (0, j))
    return pl.pallas_call(
        body, name=name, grid=(nblk,),
        in_specs=[col(nblk), col(first), col(first + nblk), col(first + 2 * nblk),
                  pl.BlockSpec((8, LANES), lambda j: (0, j))],
        out_specs=[out, out, out, pl.BlockSpec((8, LANES), lambda j: (0, j))],
        out_shape=[jax.ShapeDtypeStruct((s, nblk * LANES), BF16)] * 3 + [jax.ShapeDtypeStruct((8, nblk * LANES), F32)],
        scratch_shapes=[pltpu.VMEM((s + 2 * HALO, LANES), F32), pltpu.VMEM((s + 2 * HALO, LANES), F32)],
        compiler_params=_cparams("parallel"),
    )(dmix, proj, proj, proj, conv_w8)


LOG2E = 1.4426950408889634
LN2 = 0.6931471805599453
NEG_BIG = -1e30
SATURATED = 160.0


def _cumsum_matrix(kind):
    j = lax.broadcasted_iota(jnp.int32, (KEY_CHUNK, 2 * KEY_CHUNK), 0)
    c = lax.broadcasted_iota(jnp.int32, (KEY_CHUNK, 2 * KEY_CHUNK), 1)
    tri = {"after": j > c, "upto": j <= c, "before": j < c}[kind]
    return jnp.where((c >= KEY_CHUNK) | tri, 1.0, 0.0).astype(BF16)


def _stack_heads(t, m0):
    zero = jnp.zeros_like(t)
    return jnp.concatenate([jnp.where(m0, t, zero), jnp.where(m0, zero, t)], axis=0)


def _softplus2(z):
    sp = jnp.maximum(z, 0.0) + jnp.log2(1.0 + jnp.exp2(-jnp.abs(z)))
    return sp, z - sp


def _key_chunk(ref, kc):
    return ref[pl.ds(pl.multiple_of(kc * KEY_CHUNK, KEY_CHUNK), KEY_CHUNK), :]


def _attn_bwd(qk, proj, dmix, rtot, used, tq, name, travel=()):
    s = qk.shape[0]
    nhp = qk.shape[1] // (2 * LANES)
    nc = tq // KEY_CHUNK
    nq = s // tq
    nt = len(travel)

    def body(used_ref, q_ref, k_ref, v_ref, do_ref, r_ref, cmi_ref, cme_ref, bias_ref, dq_ref, dk_ref, dv_ref,
             z_refs, ls_refs, sig_refs, sp_refs, gb_refs, pr_ref, gs_ref, copies):
        qi = pl.program_id(1)

        @pl.when(qi == 0)
        def _():
            dk_ref[...] = jnp.zeros_like(dk_ref)
            dv_ref[...] = jnp.zeros_like(dv_ref)

        if copies is not None:
            @pl.when(jnp.logical_and(pl.program_id(0) == 0, qi == 0))
            def _():
                _exchange_begin(copies)

        nslots = (qi + 1) * nc
        walked = used_ref[pl.program_id(0), qi].astype(jnp.int32)
        first = jnp.clip(nslots - walked, 0, nslots - nc) // nc * nc
        m0 = lax.broadcasted_iota(jnp.int32, (1, LANES), 1) < HEAD_DIM
        qs = _stack_heads(q_ref[...], m0)
        do = do_ref[...]
        dos = _stack_heads(do.astype(BF16), m0)
        dosl = _stack_heads((do * LN2).astype(BF16), m0)
        cmi = cmi_ref[...]
        cme = cme_ref[...]

        def chunk_at(i):
            return jnp.clip(i, first, nslots - 1)

        def scores(kc):
            return _dot(qs, _key_chunk(k_ref, kc), NT)

        def weights(ls, cs, da, pr, kc):
            a = jnp.exp2(ls - (pr - cs[:, :KEY_CHUNK]))
            gb = (a * da).astype(BF16)
            ks = pl.multiple_of(kc * KEY_CHUNK, KEY_CHUNK)
            dv_ref[pl.ds(ks, KEY_CHUNK), :] += _dot(a, dos, TN)
            return gb, jnp.exp2(ls), pr - cs[:, KEY_CHUNK:]

        def score_grads(gb, sig, cg, gs, dq, kc):
            dzb = (gb.astype(F32) * (1.0 - sig) - sig * (gs + cg[:, :KEY_CHUNK])).astype(BF16)
            ks = pl.multiple_of(kc * KEY_CHUNK, KEY_CHUNK)
            dk_ref[pl.ds(ks, KEY_CHUNK), :] += _dot(dzb, qs, TN)
            dq = dq + _dot(jnp.concatenate([dzb[:tq], dzb[tq:]], axis=1), _stack_heads(_key_chunk(k_ref, kc), m0), NN)
            return gs + cg[:, KEY_CHUNK:], dq

        def step(i, par, bias=None, stages="zswg"):
            cur, prv = par, 1 - par
            k1, k2 = chunk_at(i - 1), chunk_at(i - 2)
            z_next = scores(chunk_at(i + 1))
            if "w" in stages:
                cs = _dot(sp_refs[prv][...], cmi, NN)
                da = _dot(dosl, _key_chunk(v_ref, k1), NT)
            if "g" in stages:
                cg = _dot(gb_refs[cur][...], cme, NN)
            z = z_refs[cur][...]
            if bias is not None:
                z = z + bias
            sp, ls = _softplus2(z)
            sp_refs[cur][...] = sp.astype(BF16)
            ls_refs[cur][...] = ls
            if "g" in stages:
                gs, dq = score_grads(gb_refs[cur][...], sig_refs[cur][...], cg, gs_ref[...], dq_ref[...], k2)
                gs_ref[...] = gs
                dq_ref[...] = dq
            if "w" in stages:
                gb, sig, pr = weights(ls_refs[prv][...], cs, da, pr_ref[...], k1)
                gb_refs[prv][...] = gb
                sig_refs[prv][...] = sig
                pr_ref[...] = pr
            z_refs[prv][...] = z_next

        pr_ref[...] = jnp.concatenate([r_ref[:, :LANES], r_ref[:, LANES:]], axis=0)
        gs_ref[...] = jnp.zeros((2 * tq, LANES), F32)
        dq_ref[...] = jnp.zeros((tq, LANES), F32)
        z_refs[0][...] = scores(first)
        only_diagonal = first == nslots - nc
        step(first, 0, jnp.where(only_diagonal, bias_ref[0], 0.0), stages="zs")
        step(first + 1, 1, jnp.where(only_diagonal, bias_ref[1], 0.0), stages="zsw")

        def two_steps(j, _):
            step(2 * j, 0)
            step(2 * j + 1, 1)
            return 0

        lax.fori_loop(first // 2 + 1, nslots // 2 - 1, two_steps, 0)

        @pl.when(jnp.logical_not(only_diagonal))
        def _():
            step(nslots - 2, 0, bias_ref[0])
            step(nslots - 1, 1, bias_ref[1])

        k1, k2 = chunk_at(nslots - 1), chunk_at(nslots - 2)
        gb, sig, _ = weights(ls_refs[1][...], _dot(sp_refs[1][...], cmi, NN),
                             _dot(dosl, _key_chunk(v_ref, k1), NT), pr_ref[...], k1)
        gb2 = gb_refs[0][...]
        gs, dq = score_grads(gb2, sig_refs[0][...], _dot(gb2, cme, NN), gs_ref[...], dq_ref[...], k2)
        _, dq = score_grads(gb, sig, _dot(gb, cme, NN), gs, dq, k1)
        dq_ref[...] = dq

        if copies is not None:
            @pl.when(jnp.logical_and(pl.program_id(0) == nhp - 1, qi == nq - 1))
            def _():
                _exchange_finish(copies)

    def wrapped(*refs):
        ins, rest = refs[:9], refs[9:]
        srcs, rest = rest[:nt], rest[nt:]
        outs, rest = rest[:3], rest[3:]
        lands, rest = rest[:nt], rest[nt:]
        z0, z1, ls0, ls1, sg0, sg1, sp0, sp1, gb0, gb1, pr_ref, gs_ref = rest[:12]
        copies = _exchange_copies(srcs, lands, *rest[12:]) if nt else None
        body(*ins, *outs, (z0, z1), (ls0, ls1), (sg0, sg1), (sp0, sp1), (gb0, gb1), pr_ref, gs_ref, copies)

    assert nc == 2
    bias = _diag_bias(tq, True)
    bias = jnp.concatenate([bias[:, :, :KEY_CHUNK], bias[:, :, KEY_CHUNK:]], axis=1)
    qblk = pl.BlockSpec((tq, LANES), lambda p, i: (i, p))
    full = pl.BlockSpec((s, LANES), lambda p, i: (0, p))
    cmspec = pl.BlockSpec((KEY_CHUNK, 2 * KEY_CHUNK), lambda p, i: (0, 0))
    anyspec = pl.BlockSpec(memory_space=pl.ANY)
    shape = jax.ShapeDtypeStruct((s, nhp * LANES), F32)
    f32buf = pltpu.VMEM((2 * tq, LANES), F32)
    bf16buf = pltpu.VMEM((2 * tq, LANES), BF16)
    outs = pl.pallas_call(
        wrapped, name=name, grid=(nhp, nq),
        in_specs=[pl.BlockSpec(memory_space=pltpu.SMEM),
                  qblk,
                  pl.BlockSpec((s, LANES), lambda p, i: (0, nhp + p)),
                  pl.BlockSpec((s, LANES), lambda p, i: (0, 2 * nhp + p)),
                  qblk,
                  pl.BlockSpec((tq, 2 * LANES), lambda p, i: (i, p)),
                  cmspec, cmspec,
                  pl.BlockSpec((nc, 2 * tq, LANES), lambda p, i: (0, 0, 0))] + [anyspec] * nt,
        out_specs=[qblk, full, full] + [anyspec] * nt,
        out_shape=[shape, shape, shape] + [jax.ShapeDtypeStruct(t.shape, t.dtype) for t in travel],
        scratch_shapes=[f32buf] * 6 + [bf16buf] * 4 + [f32buf] * 2 + (_exchange_scratch(nt) if nt else []),
        compiler_params=_cparams("arbitrary", "arbitrary"),
    )(used, qk, qk, proj, dmix, rtot, _cumsum_matrix("upto"), _cumsum_matrix("before"), bias, *travel)
    return outs[0], outs[1], outs[2], list(outs[3:])


def _pair_cumsum_matrix(kind):
    j = lax.broadcasted_iota(jnp.int32, (2 * KEY_CHUNK, 4 * KEY_CHUNK), 0)
    c = lax.broadcasted_iota(jnp.int32, (2 * KEY_CHUNK, 4 * KEY_CHUNK), 1)
    same_head = (j // KEY_CHUNK) == ((c // KEY_CHUNK) % 2)
    jj, cc = j % KEY_CHUNK, c % KEY_CHUNK
    tri = {"after": jj > cc, "upto": jj <= cc, "before": jj < cc}[kind]
    return jnp.where(same_head & ((c >= 2 * KEY_CHUNK) | tri), 1.0, 0.0).astype(BF16)


def _diag_bias(tq, ascending):
    nc = tq // KEY_CHUNK
    shape = (nc, tq, 2 * KEY_CHUNK)
    d = lax.broadcasted_iota(jnp.int32, shape, 0)
    r = lax.broadcasted_iota(jnp.int32, shape, 1)
    c = lax.broadcasted_iota(jnp.int32, shape, 2) % KEY_CHUNK
    chunk = d if ascending else nc - 1 - d
    return jnp.where(chunk * KEY_CHUNK + c < r, 0.0, NEG_BIG).astype(F32)


def _attn_fwd(qk, proj, tq, name, mix_cols, shards=()):
    s = qk.shape[0]
    nhp = qk.shape[1] // (2 * LANES)
    nc = tq // KEY_CHUNK
    nq = s // tq
    ng = len(shards)
    assert nc == 2
    w = 2 * KEY_CHUNK

    def body(q_ref, k_ref, v_ref, cm_ref, bias_ref, o_ref, r_ref, used_ref, z_refs, ls_refs, cs_refs, ct_refs,
             sp_refs, ab_refs, acc_ref, gather):
        qi = pl.program_id(1)
        if gather is not None:
            @pl.when(jnp.logical_and(pl.program_id(0) == 0, qi == 0))
            def _():
                gather.begin()

        nslots = (qi + 1) * nc
        m0 = lax.broadcasted_iota(jnp.int32, (1, LANES), 1) < HEAD_DIM
        q = q_ref[...]
        cm = cm_ref[...]

        def chunk_at(i):
            return jnp.clip(nslots - 1 - i, 0, nslots - 1)

        def scores(kc):
            return _dot(q, _stack_heads(_key_chunk(k_ref, kc), m0), NT)

        def values(ab, kc):
            return _dot(ab, _stack_heads(_key_chunk(v_ref, kc), m0), NN)

        def step(i, par, bias=None, stages="zscwv"):
            cur, prv = par, 1 - par
            if "z" in stages:
                z_next = scores(chunk_at(i + 1))
            if "c" in stages:
                cs = _dot(sp_refs[prv][...], cm, NN)
            if "v" in stages:
                pv = values(ab_refs[prv][...], chunk_at(i - 3))
            if "w" in stages:
                rs = r_ref[...]
                r_ref[...] = rs + ct_refs[cur][...]
                ab_refs[cur][...] = jnp.exp2(ls_refs[cur][...] - cs_refs[cur][...] - rs).astype(BF16)
            if "s" in stages:
                z = z_refs[cur][...]
                if bias is not None:
                    z = z + bias
                sp, ls = _softplus2(z)
                sp_refs[cur][...] = sp.astype(BF16)
                ls_refs[cur][...] = ls
            if "v" in stages:
                acc_ref[...] += pv
            if "c" in stages:
                cs_refs[prv][...] = cs[:, :w]
                ct_refs[prv][...] = cs[:, w:]
            if "z" in stages:
                z_refs[prv][...] = z_next

        z_refs[0][...] = scores(chunk_at(0))
        ab_refs[1][...] = jnp.zeros((tq, w), BF16)
        r_ref[...] = jnp.zeros((tq, w), F32)
        acc_ref[...] = jnp.zeros((tq, LANES), F32)
        step(0, 0, bias_ref[0], stages="zs")
        step(1, 1, bias_ref[1], stages="zsc")

        def two_steps(carry):
            j, _ = carry
            step(2 * j, 0)
            step(2 * j + 1, 1)
            return j + 1, jnp.min(jnp.minimum(r_ref[:, :KEY_CHUNK], r_ref[:, KEY_CHUNK:]))

        pairs, low = lax.while_loop(lambda c: jnp.logical_and(c[0] < nslots // 2, c[1] < SATURATED), two_steps,
                                    (jnp.int32(1), jnp.float32(0.0)))
        entered = 2 * pairs
        saturated = low >= SATURATED

        @pl.when(saturated)
        def _():
            step(entered, 0, stages="v")

        @pl.when(jnp.logical_not(saturated))
        def _():
            step(entered, 0, stages="cwv")
            step(entered + 1, 1, stages="wv")
            step(entered + 2, 0, stages="v")

        o_ref[...] = acc_ref[...].astype(o_ref.dtype)
        used_ref[pl.program_id(0), qi] = jnp.where(saturated, entered - 2, entered).astype(F32)

        if gather is not None:
            @pl.when(jnp.logical_and(pl.program_id(0) == nhp - 1, qi == nq // 2))
            def _():
                gather.relay()

            @pl.when(jnp.logical_and(pl.program_id(0) == nhp - 1, qi == nq - 1))
            def _():
                gather.finish()

    def wrapped(*refs):
        ins, rest = refs[:5], refs[5:]
        srcs, rest = rest[:ng], rest[ng:]
        outs, rest = rest[:3], rest[3:]
        dsts, scratch = rest[:ng], rest[ng:]
        z, ls, cs, ct, sp, ab = [scratch[2 * j:2 * j + 2] for j in range(6)]
        gather = _Gather(srcs, dsts, *scratch[13:]) if ng else None
        body(*ins, *outs, z, ls, cs, ct, sp, ab, scratch[12], gather)

    f32buf = pltpu.VMEM((tq, w), F32)
    bf16buf = pltpu.VMEM((tq, w), BF16)
    anyspec = pl.BlockSpec(memory_space=pl.ANY)
    outs = pl.pallas_call(
        wrapped, name=name, grid=(nhp, nq),
        in_specs=[pl.BlockSpec((tq, LANES), lambda p, i: (i, p)),
                  pl.BlockSpec((s, LANES), lambda p, i: (0, nhp + p)),
                  pl.BlockSpec((s, LANES), lambda p, i: (0, 2 * nhp + p)),
                  pl.BlockSpec((w, 2 * w), lambda p, i: (0, 0)),
                  pl.BlockSpec((nc, tq, w), lambda p, i: (0, 0, 0))] + [anyspec] * ng,
        out_specs=[pl.BlockSpec((tq, LANES), lambda p, i: (i, p)),
                   pl.BlockSpec((tq, w), lambda p, i: (i, p)),
                   pl.BlockSpec(memory_space=pltpu.SMEM)] + [anyspec] * ng,
        out_shape=[jax.ShapeDtypeStruct((s, mix_cols), BF16),
                   jax.ShapeDtypeStruct((s, nhp * w), F32),
                   jax.ShapeDtypeStruct((nhp, nq), F32)] + _gathered_shapes(shards),
        scratch_shapes=([f32buf] * 8 + [bf16buf] * 4 + [pltpu.VMEM((tq, LANES), F32)]
                        + (_gather_scratch(ng) if ng else [])),
        compiler_params=_cparams("arbitrary", "arbitrary"),
    )(qk, qk, proj, _pair_cumsum_matrix("after"), _diag_bias(tq, False), *shards)
    return outs[0], outs[1], outs[2], list(outs[3:])


BLOCK_PAIR = 2
MXU_WIDTH = 256


def _side_by_side(b_ref):
    return jnp.concatenate([b_ref[p] for p in range(BLOCK_PAIR)], axis=1)


def _mm_blocks(h, ga, widx, tm, name):
    s, d = h.shape
    nb, cols = ga.shape[1], ga.shape[3]

    def body(a_ref, b_ref, o_ref):
        o_ref[...] = _dot(a_ref[...], _side_by_side(b_ref), NN).astype(o_ref.dtype)

    return pl.pallas_call(
        body, name=name, grid=(s // tm, nb // BLOCK_PAIR),
        in_specs=[pl.BlockSpec((tm, d), lambda i, j: (i, 0)),
                  pl.BlockSpec((None, BLOCK_PAIR, d, cols), lambda i, j: (widx, j, 0, 0))],
        out_specs=pl.BlockSpec((tm, BLOCK_PAIR * cols), lambda i, j: (i, j)),
        out_shape=jax.ShapeDtypeStruct((s, nb * cols), BF16),
        compiler_params=_cparams("parallel", "arbitrary"),
    )(h, ga)


def _mm_swiglu(h, ga, gidx, uidx, tm, name):
    s, d = h.shape
    nb, cols = ga.shape[1], ga.shape[3]

    def body(a_ref, bg_ref, bu_ref, g_ref, u_ref, act_ref):
        a = a_ref[...]
        g = _dot(a, _side_by_side(bg_ref), NN)
        u = _dot(a, _side_by_side(bu_ref), NN)
        g_ref[...] = g.astype(g_ref.dtype)
        u_ref[...] = u.astype(u_ref.dtype)
        act_ref[...] = (g * (1.0 / (1.0 + jnp.exp(-g))) * u).astype(act_ref.dtype)

    def wspec(idx):
        return pl.BlockSpec((None, BLOCK_PAIR, d, cols), lambda i, j: (idx, j, 0, 0))

    out = pl.BlockSpec((tm, BLOCK_PAIR * cols), lambda i, j: (i, j))
    shape = jax.ShapeDtypeStruct((s, nb * cols), BF16)
    return pl.pallas_call(
        body, name=name, grid=(s // tm, nb // BLOCK_PAIR),
        in_specs=[pl.BlockSpec((tm, d), lambda i, j: (i, 0)), wspec(gidx), wspec(uidx)],
        out_specs=[out, out, out], out_shape=[shape, shape, shape],
        compiler_params=_cparams("parallel", "arbitrary"),
    )(h, ga, ga)


def _mm_residual_norm(a, w3, lidx, res, gain, tm, name):
    s, k = a.shape
    n = w3.shape[2]

    def body(a_ref, b_ref, r_ref, g_ref, o_ref, h_ref):
        xv = r_ref[...] + _dot(a_ref[...], b_ref[...], NN)
        o_ref[...] = xv
        r = lax.rsqrt(jnp.mean(xv * xv, axis=-1, keepdims=True) + EPS)
        h_ref[...] = ((xv * r) * g_ref[...]).astype(h_ref.dtype)

    row = pl.BlockSpec((tm, n), lambda i: (i, 0))
    return pl.pallas_call(
        body, name=name, grid=(s // tm,),
        in_specs=[pl.BlockSpec((tm, k), lambda i: (i, 0)),
                  pl.BlockSpec((None, k, n), lambda i: (lidx, 0, 0), pipeline_mode=pl.Buffered(1)),
                  row, pl.BlockSpec((1, n), lambda i: (0, 0))],
        out_specs=[row, row],
        out_shape=[jax.ShapeDtypeStruct((s, n), F32), jax.ShapeDtypeStruct((s, n), BF16)],
        compiler_params=_cparams("parallel"),
    )(a, w3, res, gain)


def _mm_residual_loss(a, w3, lidx, res, target, tm, name):
    s, k = a.shape
    n = w3.shape[2]
    nsteps = s // tm

    def body(a_ref, b_ref, r_ref, t_ref, dy_ref, dyb_ref, l_ref, acc):
        i = pl.program_id(0)
        diff = r_ref[...] + _dot(a_ref[...], b_ref[...], NN) - t_ref[...]
        dy_ref[...] = diff * (1.0 / n)
        dyb_ref[...] = (diff * (1.0 / n)).astype(dyb_ref.dtype)
        part = jnp.sum((diff * diff).reshape(tm // 8, 8, n), axis=0)

        @pl.when(i == 0)
        def _():
            acc[...] = part

        @pl.when(i > 0)
        def _():
            acc[...] += part

        @pl.when(i == nsteps - 1)
        def _():
            tot = jnp.sum(jnp.sum(acc[...], axis=1, keepdims=True), axis=0, keepdims=True)
            l_ref[...] = jnp.broadcast_to(tot * (0.5 / n), (8, LANES))

    row = pl.BlockSpec((tm, n), lambda i: (i, 0))
    return pl.pallas_call(
        body, name=name, grid=(nsteps,),
        in_specs=[pl.BlockSpec((tm, k), lambda i: (i, 0)),
                  pl.BlockSpec((None, k, n), lambda i: (lidx, 0, 0), pipeline_mode=pl.Buffered(1)),
                  row, row],
        out_specs=[row, row, pl.BlockSpec((8, LANES), lambda i: (0, 0))],
        out_shape=[jax.ShapeDtypeStruct((s, n), F32), jax.ShapeDtypeStruct((s, n), BF16),
                   jax.ShapeDtypeStruct((8, LANES), F32)],
        scratch_shapes=[pltpu.VMEM((8, n), F32)],
        compiler_params=_cparams("arbitrary"),
    )(a, w3, res, target)


def _mm_nt(a, w3, lidx, tm, tn, name):
    s, k = a.shape
    n = w3.shape[1]

    def body(a_ref, b_ref, o_ref):
        o_ref[...] = _dot(a_ref[...], b_ref[...], NT)

    return pl.pallas_call(
        body, name=name, grid=(s // tm, n // tn),
        in_specs=[pl.BlockSpec((tm, k), lambda i, j: (i, 0)),
                  pl.BlockSpec((None, tn, k), lambda i, j: (lidx, j, 0))],
        out_specs=pl.BlockSpec((tm, tn), lambda i, j: (i, j)),
        out_shape=jax.ShapeDtypeStruct((s, n), F32),
        compiler_params=_cparams("parallel", "arbitrary"),
    )(a, w3)


def _mm_nt_swiglu_bwd(dx, wd3, lidx, g, u, tm, name, travel=()):
    s, d = dx.shape
    cols = BLOCK_PAIR * (g.shape[1] // N_DEV)

    def body(a_ref, b_ref, g_ref, u_ref, dg_ref, du_ref):
        a = a_ref[...]
        for c0 in range(0, cols, MXU_WIDTH):
            sl = slice(c0, c0 + MXU_WIDTH)
            dact = _dot(a, b_ref[sl, :], NT)
            gv = g_ref[:, sl]
            sig = 0.5 * jnp.tanh(0.5 * gv) + 0.5
            silu = gv * sig
            du_ref[:, sl] = (dact * silu.astype(F32)).astype(du_ref.dtype)
            dsilu = sig + silu * (1.0 - sig)
            dg_ref[:, sl] = (dact * (u_ref[:, sl] * dsilu).astype(F32)).astype(dg_ref.dtype)

    blk = pl.BlockSpec((tm, cols), lambda i, j: (i, j))
    shape = jax.ShapeDtypeStruct(g.shape, BF16)
    grid = (s // tm, N_DEV // BLOCK_PAIR)
    body, more_in, more_out, more_shapes, more_scratch = _host_exchange(body, 4, 2, travel, grid)
    outs = pl.pallas_call(
        body, name=name, grid=grid,
        in_specs=[pl.BlockSpec((tm, d), lambda i, j: (i, 0)),
                  pl.BlockSpec((None, cols, d), lambda i, j: (lidx, j, 0)), blk, blk] + more_in,
        out_specs=[blk, blk] + more_out, out_shape=[shape, shape] + more_shapes,
        scratch_shapes=more_scratch,
        compiler_params=_cparams("arbitrary", "arbitrary"),
    )(dx, wd3, g, u, *travel)
    return outs[0], outs[1], list(outs[2:])


def _mm_nt_norm_bwd(das, ga, widxs, x, gain, dres, tm, name, travel=()):
    s = das[0].shape[0]
    nb, d, cols = ga.shape[1], ga.shape[2], ga.shape[3]
    nw = len(das)
    nsteps = s // tm

    def body(*refs):
        a_refs, b_refs = refs[:nw], refs[nw:2 * nw]
        x_ref, g_ref, dres_ref, dx_ref, dxb_ref, dg_ref = refs[2 * nw:]
        i = pl.program_id(0)
        dhv = None
        wide = BLOCK_PAIR * cols
        for w in range(nw):
            for k in range(nb // BLOCK_PAIR):
                b = jnp.concatenate([b_refs[w][BLOCK_PAIR * k + p] for p in range(BLOCK_PAIR)], axis=1)
                part = _dot(a_refs[w][:, k * wide:(k + 1) * wide], b, NT)
                dhv = part if dhv is None else dhv + part
        xv = x_ref[...]
        r = lax.rsqrt(jnp.mean(xv * xv, axis=-1, keepdims=True) + EPS)
        xhat = xv * r
        dxh = dhv * g_ref[...]
        dxv = dres_ref[...] + r * (dxh - xhat * jnp.mean(dxh * xhat, axis=-1, keepdims=True))
        dx_ref[...] = dxv
        dxb_ref[...] = dxv.astype(dxb_ref.dtype)
        part = jnp.sum((dhv * xhat).reshape(tm // 8, 8, d), axis=0)

        @pl.when(i == 0)
        def _():
            dg_ref[...] = part

        @pl.when(i > 0)
        def _():
            dg_ref[...] += part

        @pl.when(i == nsteps - 1)
        def _():
            dg_ref[...] = jnp.broadcast_to(jnp.sum(dg_ref[...], axis=0, keepdims=True), (8, d))

    def wspec(idx):
        return pl.BlockSpec((None, nb, d, cols), lambda i: (idx, 0, 0, 0), pipeline_mode=pl.Buffered(1))

    row = pl.BlockSpec((tm, d), lambda i: (i, 0))
    body, more_in, more_out, more_shapes, more_scratch = _host_exchange(body, 2 * nw + 3, 3, travel, (nsteps,))
    outs = pl.pallas_call(
        body, name=name, grid=(nsteps,),
        in_specs=([pl.BlockSpec((tm, nb * cols), lambda i: (i, 0))] * nw + [wspec(i) for i in widxs]
                  + [row, pl.BlockSpec((1, d), lambda i: (0, 0)), row] + more_in),
        out_specs=[row, row, pl.BlockSpec((8, d), lambda i: (0, 0))] + more_out,
        out_shape=[jax.ShapeDtypeStruct((s, d), F32), jax.ShapeDtypeStruct((s, d), BF16),
                   jax.ShapeDtypeStruct((8, d), F32)] + more_shapes,
        scratch_shapes=more_scratch,
        compiler_params=_cparams("arbitrary"),
    )(*das, *([ga] * nw), x, gain, dres, *travel)
    return outs[0], outs[1], outs[2], list(outs[3:])


def _mm_tn(a, b, ta, tb, tk, out_blocks, name, travel=()):
    s, ka = a.shape
    nb = b.shape[1]
    nk = s // tk
    cols = tb
    if out_blocks:
        tb = BLOCK_PAIR * cols

    def body(a_ref, b_ref, o_ref, ob_ref):
        k = pl.program_id(2)
        part = _dot(a_ref[...], b_ref[...], TN)

        def put(first):
            if out_blocks:
                for p in range(BLOCK_PAIR):
                    piece = part[:, p * cols:(p + 1) * cols]
                    o_ref[p] = piece if first else o_ref[p] + piece
            else:
                o_ref[...] = part if first else o_ref[...] + part

        @pl.when(k == 0)
        def _():
            put(True)

        @pl.when(k > 0)
        def _():
            put(False)

        @pl.when(k == nk - 1)
        def _():
            ob_ref[...] = o_ref[...].astype(ob_ref.dtype)

    if out_blocks:
        out_spec = pl.BlockSpec((BLOCK_PAIR, ta, cols), lambda i, j, k: (j, i, 0))
        shape = (nb // cols, ka, cols)
    else:
        out_spec = pl.BlockSpec((ta, tb), lambda i, j, k: (i, j))
        shape = (ka, nb)
    grid = (ka // ta, nb // tb, nk)
    body, more_in, more_out, more_shapes, more_scratch = _host_exchange(body, 2, 2, travel, grid)
    outs = pl.pallas_call(
        body, name=name, grid=grid,
        in_specs=[pl.BlockSpec((tk, ta), lambda i, j, k: (k, i)),
                  pl.BlockSpec((tk, tb), lambda i, j, k: (k, j))] + more_in,
        out_specs=[out_spec, out_spec] + more_out,
        out_shape=[jax.ShapeDtypeStruct(shape, F32), jax.ShapeDtypeStruct(shape, BF16)] + more_shapes,
        scratch_shapes=more_scratch,
        compiler_params=_cparams("arbitrary", "arbitrary", "arbitrary"),
    )(a, b, *travel)
    return (outs[0], outs[1]), list(outs[2:])


def _adamw(g, w, m, v, name):
    rows, cols = g.shape
    c1 = 1.0 / (1.0 - ADAM_B1 ** ADAM_STEP)
    c2 = 1.0 / (1.0 - ADAM_B2 ** ADAM_STEP)

    def body(p_ref, w_ref, m_ref, v_ref, g_ref, d_ref, nm_ref, nv_ref):
        gv = p_ref[...]
        nm = ADAM_B1 * m_ref[...] + (1.0 - ADAM_B1) * gv
        nv = ADAM_B2 * v_ref[...] + (1.0 - ADAM_B2) * (gv * gv)
        g_ref[...] = gv
        nm_ref[...] = nm
        nv_ref[...] = nv
        d_ref[...] = -ADAM_LR * ((nm * c1) / (jnp.sqrt(nv * c2) + ADAM_EPS) + ADAM_WD * w_ref[...])

    blk = pl.BlockSpec((rows, cols), lambda i: (0, 0))
    shape = jax.ShapeDtypeStruct((rows, cols), F32)
    return pl.pallas_call(
        body, name=name, grid=(1,),
        in_specs=[blk] * 4, out_specs=[blk] * 4, out_shape=[shape] * 4,
        compiler_params=_cparams("arbitrary"),
    )(g, w, m, v)


def _adamw_sharded(parts, grads, my, w, m, v, tr, name):
    depth, rows, cols = w.shape
    p, pr, pc = parts[0].shape
    c1 = 1.0 / (1.0 - ADAM_B1 ** ADAM_STEP)
    c2 = 1.0 / (1.0 - ADAM_B2 ** ADAM_STEP)

    def body(my_ref, *refs):
        p_refs, own_refs = refs[:depth], refs[depth:2 * depth]
        w_ref, m_ref, v_ref, g_ref, d_ref, nm_ref, nv_ref = refs[2 * depth:]
        layer = pl.program_id(0)
        for ll in range(depth):
            @pl.when(layer == ll)
            def _(ll=ll):
                mine = own_refs[ll][...]
                g = jnp.where(my_ref[0] == 0, mine, p_refs[ll][0].astype(F32))
                for k in range(1, p):
                    g = g + jnp.where(my_ref[0] == k, mine, p_refs[ll][k].astype(F32))
                g = g[:, :cols]
                nm = ADAM_B1 * m_ref[...] + (1.0 - ADAM_B1) * g
                nv = ADAM_B2 * v_ref[...] + (1.0 - ADAM_B2) * (g * g)
                g_ref[...] = g
                nm_ref[...] = nm
                nv_ref[...] = nv
                d_ref[...] = -ADAM_LR * ((nm * c1) / (jnp.sqrt(nv * c2) + ADAM_EPS) + ADAM_WD * w_ref[...])

    def row_block(ll, l, i):
        return jnp.where(l == ll, i, 0)

    blk = pl.BlockSpec((None, tr, cols), lambda l, i, my_: (l, i, 0))
    shape = jax.ShapeDtypeStruct((depth, rows, cols), F32)
    return pl.pallas_call(
        body, name=name,
        grid_spec=pltpu.PrefetchScalarGridSpec(
            num_scalar_prefetch=1, grid=(depth, rows // tr),
            in_specs=([pl.BlockSpec((p, tr, pc), lambda l, i, my_, ll=ll: (0, row_block(ll, l, i), 0))
                       for ll in range(depth)]
                      + [pl.BlockSpec((None, tr, pc), lambda l, i, my_, ll=ll: (my_[0], row_block(ll, l, i), 0))
                         for ll in range(depth)]
                      + [blk, blk, blk]),
            out_specs=[blk] * 4),
        out_shape=[shape] * 4,
        compiler_params=_cparams("arbitrary", "arbitrary"),
    )(my, *parts, *grads, w, m, v)


def _place():
    x, y, c = lax.axis_index("x"), lax.axis_index("y"), lax.axis_index("c")
    return x, y, c


class _Gather:
    def __init__(self, srcs, dsts, send_sems, recv_sems, local_sems):
        na = len(srcs)
        x, y, c = _place()
        me, sibling = (x, y, c), (x, y, 1 - c)
        chips = [(1 - x, y), (x, 1 - y), (1 - x, 1 - y)]

        def slot(a, dev):
            return dsts[a].at[:, pl.ds(4 * dev[0] + 2 * dev[1] + dev[2], 1)]

        def copy(k, a, block, to, from_shard=False):
            return pltpu.make_async_remote_copy(
                src_ref=srcs[a] if from_shard else slot(a, block), dst_ref=slot(a, block),
                send_sem=send_sems.at[k, a], recv_sem=recv_sems.at[k, a], device_id=to, device_id_type=MESH)

        pairs = [(j, chip, a) for j, chip in enumerate(chips) for a in range(na)]
        self.mine = [pltpu.make_async_copy(srcs[a], slot(a, me), local_sems.at[a]) for a in range(na)]
        self.first = [copy(0, a, me, sibling, True) for a in range(na)]
        self.first += [copy(1 + j, a, me, (*chip, c), True) for j, chip, a in pairs]
        self.over_ici = [copy(1 + j, a, (*chip, c), me) for j, chip, a in pairs]
        self.passed = [copy(4 + j, a, (*chip, c), sibling) for j, chip, a in pairs]
        self.from_sibling = [copy(0, a, sibling, me) for a in range(na)]
        self.from_sibling += [copy(4 + j, a, (*chip, 1 - c), me) for j, chip, a in pairs]

    def begin(self):
        for cp in self.mine + self.first:
            cp.start()

    def relay(self):
        for arrived, onward in zip(self.over_ici, self.passed):
            arrived.wait_recv()
            onward.start()

    def finish(self):
        for cp in self.from_sibling:
            cp.wait_recv()
        for cp in self.first + self.passed:
            cp.wait_send()
        for cp in self.mine:
            cp.wait()


def _gather_scratch(na):
    return [pltpu.SemaphoreType.DMA((7, na)), pltpu.SemaphoreType.DMA((7, na)), pltpu.SemaphoreType.DMA((na,))]


def _gathered_shapes(shards):
    return [jax.ShapeDtypeStruct((a.shape[0], N_DEV) + a.shape[2:], a.dtype) for a in shards]


_RELATIONS = [(dx, dy, dc) for dx in (0, 1) for dy in (0, 1) for dc in (0, 1)][1:]


def _flip(v, d):
    return 1 - v if d else v


def _exchange_copies(srcs, dsts, send_sems, recv_sems, local_sems):
    x, y, c = _place()
    my = 4 * x + 2 * y + c
    na = len(srcs)
    mine = [pltpu.make_async_copy(srcs[a].at[pl.ds(my, 1)], dsts[a].at[pl.ds(my, 1)], local_sems.at[a])
            for a in range(na)]
    sends, recvs = [], []
    for k, (dx, dy, dc) in enumerate(_RELATIONS):
        peer = (_flip(x, dx), _flip(y, dy), _flip(c, dc))
        pidx = 4 * peer[0] + 2 * peer[1] + peer[2]
        for a in range(na):
            for into, out in ((my, sends), (pidx, recvs)):
                out.append(pltpu.make_async_remote_copy(
                    src_ref=srcs[a].at[pl.ds(pidx, 1)], dst_ref=dsts[a].at[pl.ds(into, 1)],
                    send_sem=send_sems.at[k, a], recv_sem=recv_sems.at[k, a], device_id=peer, device_id_type=MESH))
    return mine, sends, recvs


def _exchange_begin(copies):
    mine, sends, _ = copies
    for cp in mine + sends:
        cp.start()


def _exchange_finish(copies):
    mine, sends, recvs = copies
    for cp in recvs:
        cp.wait_recv()
    for cp in sends:
        cp.wait_send()
    for cp in mine:
        cp.wait()


def _exchange_scratch(na):
    return [pltpu.SemaphoreType.DMA((7, na)), pltpu.SemaphoreType.DMA((7, na)), pltpu.SemaphoreType.DMA((na,))]


def _host_exchange(body, n_in, n_out, travel, grid):
    nt = len(travel)
    if not nt:
        return body, [], [], [], []

    def wrapped(*refs):
        ins, srcs = refs[:n_in], refs[n_in:n_in + nt]
        outs, rest = refs[n_in + nt:n_in + nt + n_out], refs[n_in + nt + n_out:]
        dsts, scratch = rest[:nt], rest[nt:]
        copies = _exchange_copies(srcs, dsts, *scratch[-3:])
        first = last = None
        for axis, size in enumerate(grid):
            at_start, at_end = pl.program_id(axis) == 0, pl.program_id(axis) == size - 1
            first = at_start if first is None else jnp.logical_and(first, at_start)
            last = at_end if last is None else jnp.logical_and(last, at_end)

        @pl.when(first)
        def _():
            _exchange_begin(copies)

        body(*ins, *outs, *scratch[:-3])

        @pl.when(last)
        def _():
            _exchange_finish(copies)

    anyspec = pl.BlockSpec(memory_space=pl.ANY)
    return (wrapped, [anyspec] * nt, [anyspec] * nt, [jax.ShapeDtypeStruct(t.shape, t.dtype) for t in travel],
            _exchange_scratch(nt))


def _all_reduce_small(v, name):
    r, c_ = v.shape

    def body(v_ref, o_ref, gath, send_sems, recv_sems):
        x, y, c = _place()
        my = 4 * x + 2 * y + c
        gath[my] = v_ref[...]
        sends = []
        for k, (dx, dy, dc) in enumerate(_RELATIONS):
            peer = (_flip(x, dx), _flip(y, dy), _flip(c, dc))
            cp = pltpu.make_async_remote_copy(
                src_ref=v_ref, dst_ref=gath.at[my], send_sem=send_sems.at[k], recv_sem=recv_sems.at[k],
                device_id=peer, device_id_type=MESH)
            cp.start()
            sends.append((cp, 4 * peer[0] + 2 * peer[1] + peer[2], k, peer))
        for cp, pidx, k, peer in sends:
            pltpu.make_async_remote_copy(
                src_ref=v_ref, dst_ref=gath.at[pidx], send_sem=send_sems.at[k], recv_sem=recv_sems.at[k],
                device_id=peer, device_id_type=MESH).wait_recv()
        for cp, *_ in sends:
            cp.wait_send()
        tot = gath[0]
        for k in range(1, N_DEV):
            tot = tot + gath[k]
        o_ref[...] = tot

    vm = pl.BlockSpec(memory_space=pltpu.VMEM)
    return pl.pallas_call(
        body, name=name, in_specs=[vm], out_specs=vm,
        out_shape=jax.ShapeDtypeStruct((r, c_), F32),
        scratch_shapes=[pltpu.VMEM((N_DEV, r, c_), F32), pltpu.SemaphoreType.DMA((7,)),
                        pltpu.SemaphoreType.DMA((7,))],
    )(v)


TM = 512
TM_MATMUL = 2048
TM_RESIDUAL = 1024
TQ = 256


def _device_blocks(t):
    return t.reshape(N_DEV, -1, t.shape[-1])


def _pad_to(a, axis, size):
    pad = [(0, 0)] * a.ndim
    pad[axis] = (0, size - a.shape[axis])
    return jnp.pad(a, pad)


def _local_step(x, target, first_shards, late_shards, conv_shard, norm_mix, q_norm, k_norm, norm_ffn):
    depth, d = norm_mix.shape
    cols = first_shards[0].shape[3]
    tm, tq = min(TM, x.shape[0]), min(TQ, x.shape[0])
    tmm, tmr = min(TM_MATMUL, x.shape[0]), min(TM_RESIDUAL, x.shape[0])
    attn = d // 2
    nheads = attn // HEAD_DIM
    scale = HEAD_DIM ** -0.5 * LOG2E
    saved = []
    h1, (g_in0, g_conv) = _rmsnorm_fwd(x, norm_mix[0][None], tm, "norm_mix_fwd_0", first_shards)
    conv_full = g_conv[0, :, :depth * 3, :conv_shard].transpose(1, 0, 2).reshape(depth, 3, N_DEV * conv_shard)
    for l in range(depth):
        w_in = (g_in0, 0) if l == 0 else (g_rest, 3 * (l - 1))
        proj = _mm_blocks(h1, *w_in, tmm, f"proj_in_{l}")
        qk_gain = jnp.concatenate([jnp.tile(q_norm[l], nheads) * scale, jnp.tile(k_norm[l], nheads)])[None]
        qk = _qknorm_fwd(proj, qk_gain, tmm, f"qknorm_fwd_{l}")
        o, rtot, used, gathered = _attn_fwd(qk, proj, tq, f"attn_fwd_{l}", d, late_shards if l == 0 else ())
        if l == 0:
            g_gu0, g_rest, gb, gc = gathered if depth > 1 else (gathered[0], None, *gathered[1:])
            gb = gb.reshape(depth, -1, d)
            gc = gc.reshape(depth, -1, d)
        w_gu = (g_gu0, 0, 1) if l == 0 else (g_rest, 3 * (l - 1) + 1, 3 * (l - 1) + 2)
        conv_w8 = _pad_to(conv_full[l], 0, 8)
        mix = _conv_fwd(proj, conv_w8, o, f"conv_fwd_{l}")
        x1, h2 = _mm_residual_norm(mix, gb, l, x, norm_ffn[l][None], tmr, f"proj_out_{l}")
        g, u, act = _mm_swiglu(h2, *w_gu, tmr, f"ffn_up_{l}")
        saved.append((x, h1, proj, qk_gain, qk, rtot, used, conv_w8, mix, x1, h2, g, u, act, w_in, w_gu))
        if l + 1 < depth:
            x, h1 = _mm_residual_norm(act, gc, l, x1, norm_mix[l + 1][None], tm, f"ffn_down_{l}")
        else:
            dx, dxb, loss = _mm_residual_loss(act, gc, l, x1, target, tm, f"ffn_down_{l}")

    grads = [None] * depth
    small = [None] * depth
    landed = {}
    for l in reversed(range(depth)):
        x0, h1, proj, qk_gain, qk, rtot, used, conv_w8, mix, x1, h2, g, u, act, w_in, w_gu = saved[l]
        d = x0.shape[1]
        late = [[], [], [], []]
        if l == 0:
            for n, i in enumerate(5 * ll + j for ll in range(1, depth) for j in (0, 1, 2, 4, 3)):
                late[n % 4].append(i)

        def ride(host):
            return late[host], [_device_blocks(grads[i // 5][i % 5][1]) for i in late[host]]

        idx, travel = ride(0)
        dg, du, arrived = _mm_nt_swiglu_bwd(dxb, gc, l, g, u, tmr, f"ffn_down_bwd_{l}", travel)
        landed.update(zip(idx, arrived))
        idx, travel = ride(1)
        d_wdown, arrived = _mm_tn(act, dxb, 768, d, tmm, False, f"dw_down_{l}", travel)
        landed.update(zip(idx, arrived))
        idx, travel = ride(2)
        d_wgate, arrived = _mm_tn(h2, dg, d, cols, tmm, True, f"dw_gate_{l}", travel)
        landed.update(zip(idx, arrived))
        idx, travel = ride(3)
        d_wup, arrived = _mm_tn(h2, du, d, cols, tmm, True, f"dw_up_{l}", travel)
        landed.update(zip(idx, arrived))
        dx1, dx1b, dg_ffn, _ = _mm_nt_norm_bwd([dg, du], w_gu[0], list(w_gu[1:]), x1, norm_ffn[l][None], dx, tm,
                                               f"ffn_up_bwd_{l}")
        dmix = _mm_nt(dx1b, gb, l, tmr, 512, f"proj_out_bwd_{l}")
        d_wout, _ = _mm_tn(mix, dx1b, 512, d, tmm, False, f"dw_out_{l}")
        dcb, dcc, dcu, dconv = _conv_bwd(dmix, proj, conv_w8, f"conv_bwd_{l}")
        rides = {1: d_wgate[1], 2: d_wup[1], 3: d_wout[1], 4: d_wdown[1]} if l == 0 else {}
        dq, dk, dv, arrived = _attn_bwd(qk, proj, dmix, rtot, used, tq, f"attn_bwd_{l}",
                                        [_device_blocks(t) for t in rides.values()])
        landed.update(zip(rides.keys(), arrived))
        dqk, dg_qk = _qknorm_bwd(dq, dk, proj, qk_gain, tmm, f"qknorm_bwd_{l}")
        dproj = jnp.concatenate([dqk, dv.astype(BF16), dcb, dcc, dcu], axis=1)
        d_win, _ = _mm_tn(h1, dproj, d, cols, tmm, True, f"dw_in_{l}")
        dx, dxb, dg_mix, arrived = _mm_nt_norm_bwd(
            [dproj], w_in[0], [w_in[1]], x0, norm_mix[l][None], dx1, tm, f"proj_in_bwd_{l}",
            [_device_blocks(d_win[1])] if l == 0 else [])
        landed.update(zip([0], arrived))
        grads[l] = (d_win, d_wgate, d_wup, d_wout, d_wdown)
        dq_gain = jnp.sum(dg_qk[0, :attn].reshape(nheads, HEAD_DIM), axis=0) * scale
        dk_gain = jnp.sum(dg_qk[0, attn:].reshape(nheads, HEAD_DIM), axis=0)
        small[l] = (dg_mix[0], dg_ffn[0], dq_gain, dk_gain, dconv[:3])
    return loss, dx, grads, small, landed


def kernel(x, norm_mix, w_in, q_norm, k_norm, conv_w, w_out, norm_ffn, w_gate, w_up, w_down, loss_target, m_norm_mix, m_w_in, m_q_norm, m_k_norm, m_conv_w, m_w_out, m_norm_ffn, m_w_gate, m_w_up, m_w_down, v_norm_mix, v_w_in, v_q_norm, v_k_norm, v_conv_w, v_w_out, v_norm_ffn, v_w_gate, v_w_up, v_w_down):
    depth, d, in_shard = w_in.shape
    ff_shard = w_gate.shape[2]
    ff_pad = in_shard
    conv_shard = conv_w.shape[2]
    xs = x.reshape(x.shape[-2], d)
    target = loss_target.reshape(xs.shape)

    pa = jnp.stack([w_in, _pad_to(w_gate, 2, ff_pad), _pad_to(w_up, 2, ff_pad)], axis=1)
    pa = pa.reshape(3 * depth, 1, d, in_shard).astype(BF16)
    pd = _pad_to(_pad_to(conv_w.reshape(depth * 3, conv_shard), 0, 8), 1, LANES)[None, None]
    late_shards = [pa[1:3]] + ([pa[3:]] if depth > 1 else [])
    late_shards += [w_out.astype(BF16)[:, None], _pad_to(w_down, 1, ff_pad).astype(BF16)[:, None]]

    loss, grad_x, grads, small, landed = _local_step(xs, target, [pa[:1], pd], late_shards, conv_shard, norm_mix,
                                                     q_norm, k_norm, norm_ffn)

    x_, y_, c_ = _place()
    my = 4 * x_ + 2 * y_ + c_

    rows = []
    for l in range(depth):
        g_mix, g_ffn, g_q, g_k, g_conv = small[l]
        qkrow = _pad_to(jnp.concatenate([g_q, g_k]), 0, d)
        rows += [g_mix[None], g_ffn[None], qkrow[None], _pad_to(g_conv, 1, d)]
    nrow = 6 * depth
    packed = jnp.concatenate(rows + [_pad_to(loss[:1], 1, d)], axis=0)
    packed = _pad_to(packed, 0, ((nrow + 1 + 7) // 8) * 8)
    summed = _all_reduce_small(packed, "reduce_small")
    loss_out = summed[nrow, 0]

    my1 = my.astype(jnp.int32).reshape(1)

    def big(j, w, m, v, tr, name):
        return _adamw_sharded([landed[5 * l + j] for l in range(depth)],
                              [_device_blocks(grads[l][j][0]) for l in range(depth)], my1, w, m, v, tr, name)

    res = {"w_in": big(0, w_in, m_w_in, v_w_in, 256, "adamw_in"),
           "w_gate": big(1, w_gate, m_w_gate, v_w_gate, 256, "adamw_gate"),
           "w_up": big(2, w_up, m_w_up, v_w_up, 256, "adamw_up"),
           "w_out": big(3, w_out, m_w_out, v_w_out, w_out.shape[1], "adamw_out"),
           "w_down": big(4, w_down, m_w_down, v_w_down, ff_shard // 2, "adamw_down")}

    g_rows, w_rows, m_rows, v_rows = [], [], [], []
    for l in range(depth):
        base = l * 6
        conv_g = lax.dynamic_slice(summed[base + 3:base + 6], (0, my * conv_shard), (3, conv_shard))
        g_rows += [summed[base:base + 3], _pad_to(conv_g, 1, d)]
        for dst, (nm, qn, kn, nf, cw) in ((w_rows, (norm_mix, q_norm, k_norm, norm_ffn, conv_w)),
                                          (m_rows, (m_norm_mix, m_q_norm, m_k_norm, m_norm_ffn, m_conv_w)),
                                          (v_rows, (v_norm_mix, v_q_norm, v_k_norm, v_norm_ffn, v_conv_w))):
            dst += [nm[l][None], nf[l][None], _pad_to(jnp.concatenate([qn[l], kn[l]]), 0, d)[None],
                    _pad_to(cw[l], 1, d)]
    prow = ((nrow + 7) // 8) * 8
    gs, ws, ms, vs = [_pad_to(jnp.concatenate(t, axis=0), 0, prow) for t in (g_rows, w_rows, m_rows, v_rows)]
    sm = _adamw(gs, ws, ms, vs, "adamw_small")

    hd = q_norm.shape[1]

    def small_out(t, kind):
        per_layer = []
        for l in range(depth):
            base = l * 6
            per_layer.append({"norm_mix": t[base], "norm_ffn": t[base + 1], "q_norm": t[base + 2, :hd],
                              "k_norm": t[base + 2, hd:2 * hd], "conv_w": t[base + 3:base + 6, :conv_shard]}[kind])
        return jnp.stack(per_layer)

    def big_out(name, i):
        return res[name][i]

    outs = [loss_out, grad_x.reshape(x.shape)]
    for i in range(4):
        outs += [small_out(sm[i], "norm_mix"), big_out("w_in", i), small_out(sm[i], "q_norm"),
                 small_out(sm[i], "k_norm"), small_out(sm[i], "conv_w"), big_out("w_out", i),
                 small_out(sm[i], "norm_ffn"), big_out("w_gate", i), big_out("w_up", i), big_out("w_down", i)]
    return tuple(outs)
```

```python
import jax
import jax.numpy as jnp
from jax import lax
from jax.experimental import pallas as pl
from jax.experimental.pallas import tpu as pltpu

F32 = jnp.float32
BF16 = jnp.bfloat16
MESH = pl.DeviceIdType.MESH

N_DEV = 8
LANES = 128
HEAD_DIM = 64
KEY_CHUNK = 128
EPS = 1e-6
VMEM_LIMIT = 48 * 1024 * 1024

ADAM_LR = 0.001
ADAM_B1 = 0.9
ADAM_B2 = 0.999
ADAM_EPS = 1e-08
ADAM_WD = 0.01
ADAM_STEP = 10

NN = (((1,), (0,)), ((), ()))
NT = (((1,), (1,)), ((), ()))
TN = (((0,), (0,)), ((), ()))


def _dot(a, b, dims):
    return lax.dot_general(a.astype(BF16), b.astype(BF16), dims, preferred_element_type=F32)


def _cparams(*sem):
    return pltpu.CompilerParams(dimension_semantics=sem, vmem_limit_bytes=VMEM_LIMIT)


def _rmsnorm_fwd(x, gain, tm, name, shards=()):
    s, d = x.shape
    nsteps = s // tm
    ng = len(shards)

    def body(*refs):
        x_ref, g_ref, srcs = refs[0], refs[1], refs[2:2 + ng]
        o_ref, dsts, sems = refs[2 + ng], refs[3 + ng:3 + 2 * ng], refs[3 + 2 * ng:]
        i = pl.program_id(0)
        gather = _Gather(srcs, dsts, *sems) if ng else None
        if ng:
            @pl.when(i == 0)
            def _():
                gather.begin()

        xv = x_ref[...]
        r = lax.rsqrt(jnp.mean(xv * xv, axis=-1, keepdims=True) + EPS)
        o_ref[...] = ((xv * r) * g_ref[...]).astype(o_ref.dtype)
        if ng:
            @pl.when(i == nsteps - 1)
            def _():
                gather.relay()
                gather.finish()

    anyspec = pl.BlockSpec(memory_space=pl.ANY)
    outs = pl.pallas_call(
        body, name=name, grid=(nsteps,),
        in_specs=[pl.BlockSpec((tm, d), lambda i: (i, 0)), pl.BlockSpec((1, d), lambda i: (0, 0))] + [anyspec] * ng,
        out_specs=[pl.BlockSpec((tm, d), lambda i: (i, 0))] + [anyspec] * ng,
        out_shape=[jax.ShapeDtypeStruct((s, d), BF16)] + _gathered_shapes(shards),
        scratch_shapes=_gather_scratch(ng) if ng else [],
        compiler_params=_cparams("arbitrary"),
    )(x, gain, *shards)
    return outs[0], list(outs[1:])


def _group_mean_matrix():
    r = lax.broadcasted_iota(jnp.int32, (LANES, LANES), 0) // HEAD_DIM
    c = lax.broadcasted_iota(jnp.int32, (LANES, LANES), 1) // HEAD_DIM
    return jnp.where(r == c, 1.0 / HEAD_DIM, 0.0).astype(BF16)


def _group_mean(v, gm):
    hi = v.astype(BF16)
    lo = (v - hi.astype(F32)).astype(BF16)
    return _dot(hi, gm, NN) + _dot(lo, gm, NN)


def _qknorm_fwd(proj, gains, tm, name):
    s = proj.shape[0]
    ncol = gains.shape[1] // LANES

    def body(p_ref, g_ref, gm_ref, o_ref):
        xv = p_ref[...].astype(F32)
        r = lax.rsqrt(_group_mean(xv * xv, gm_ref[...]) + EPS)
        o_ref[...] = ((xv * r) * g_ref[...]).astype(o_ref.dtype)

    blk = pl.BlockSpec((tm, LANES), lambda i, j: (i, j))
    return pl.pallas_call(
        body, name=name, grid=(s // tm, ncol),
        in_specs=[blk, pl.BlockSpec((1, LANES), lambda i, j: (0, j)),
                  pl.BlockSpec((LANES, LANES), lambda i, j: (0, 0))],
        out_specs=blk,
        out_shape=jax.ShapeDtypeStruct((s, ncol * LANES), BF16),
        compiler_params=_cparams("parallel", "parallel"),
    )(proj, gains, _group_mean_matrix())


def _qknorm_bwd(dq, dk, dv, proj, gains, dproj, tm, name):
    s = proj.shape[0]
    ncol = gains.shape[1] // LANES
    half = ncol // 2
    nv = dv.shape[1] // LANES
    nsteps = s // tm

    def body(dq_ref, dk_ref, p_ref, g_ref, gm_ref, dv_ref, dproj_ref, dx_ref, dg_ref):
        del dproj_ref
        i = pl.program_id(1)
        j = pl.program_id(0)

        @pl.when(j >= ncol)
        def _():
            dx_ref[...] = dv_ref[...].astype(dx_ref.dtype)

        @pl.when(j < ncol)
        def _():
            gm = gm_ref[...]
            xv = p_ref[...].astype(F32)
            r = lax.rsqrt(_group_mean(xv * xv, gm) + EPS)
            xhat = xv * r
            dy = jnp.where(j < half, dq_ref[...], dk_ref[...])
            dxh = dy * g_ref[...]
            proj_ = _group_mean(dxh * xhat, gm)
            dx_ref[...] = (r * (dxh - xhat * proj_)).astype(dx_ref.dtype)
            part = jnp.sum((dy * xhat).reshape(tm // 8, 8, LANES), axis=0)

            @pl.when(i == 0)
            def _():
                dg_ref[...] = part

            @pl.when(i > 0)
            def _():
                dg_ref[...] += part

            @pl.when(i == nsteps - 1)
            def _():
                dg_ref[...] = jnp.broadcast_to(jnp.sum(dg_ref[...], axis=0, keepdims=True), (8, LANES))

    def norm_rows(j, i):
        return jnp.where(j < ncol, i, nsteps - 1)

    def norm_block(col):
        return pl.BlockSpec((tm, LANES), lambda j, i: (norm_rows(j, i), col(j)))

    last = ncol - 1
    return pl.pallas_call(
        body, name=name, grid=(ncol + nv, nsteps),
        in_specs=[norm_block(lambda j: jnp.minimum(j, half - 1)),
                  norm_block(lambda j: jnp.clip(j - half, 0, half - 1)),
                  norm_block(lambda j: jnp.minimum(j, last)),
                  pl.BlockSpec((1, LANES), lambda j, i: (0, jnp.minimum(j, last))),
                  pl.BlockSpec((LANES, LANES), lambda j, i: (0, 0)),
                  pl.BlockSpec((tm, LANES), lambda j, i: (jnp.where(j < ncol, 0, i), jnp.maximum(j - ncol, 0))),
                  pl.BlockSpec(memory_space=pl.ANY)],
        out_specs=[pl.BlockSpec((tm, LANES), lambda j, i: (i, j)),
                   pl.BlockSpec((8, LANES), lambda j, i: (0, jnp.minimum(j, last)))],
        out_shape=[jax.ShapeDtypeStruct(dproj.shape, BF16),
                   jax.ShapeDtypeStruct((8, ncol * LANES), F32)],
        input_output_aliases={6: 0},
        compiler_params=_cparams("arbitrary", "arbitrary"),
    )(dq, dk, proj, gains, _group_mean_matrix(), dv, dproj)


CONV_ROWS = 256
HALO = 8


def _conv_fwd(proj, conv_w8, mix, name):
    s = proj.shape[0]
    nblk = conv_w8.shape[1] // LANES
    first = 3 * nblk
    nchunk = s // CONV_ROWS
    before = mix.shape[1] // LANES - nblk

    def body(cb_ref, cc_ref, cu_ref, w_ref, mix_ref, y_ref, hpad):
        del mix_ref
        hpad[pl.ds(0, 2 * HALO), :] = jnp.zeros((2 * HALO, LANES), F32)

        def fill(i, _):
            r0 = pl.multiple_of(i * CONV_ROWS, CONV_ROWS)
            hpad[pl.ds(r0 + 2 * HALO, CONV_ROWS), :] = (
                cc_ref[pl.ds(r0, CONV_ROWS), :].astype(F32) * cu_ref[pl.ds(r0, CONV_ROWS), :].astype(F32))
            return 0

        lax.fori_loop(0, nchunk, fill, 0)
        w0, w1, w2 = w_ref[0:1, :], w_ref[1:2, :], w_ref[2:3, :]

        def conv(i, _):
            r0 = pl.multiple_of(i * CONV_ROWS, CONV_ROWS)
            win = hpad[pl.ds(r0 + HALO, CONV_ROWS + HALO), :]
            c = (w2 * win[HALO:] + w1 * pltpu.roll(win, 1, 0)[HALO:] + w0 * pltpu.roll(win, 2, 0)[HALO:])
            y_ref[pl.ds(r0, CONV_ROWS), :] = (cb_ref[pl.ds(r0, CONV_ROWS), :].astype(F32) * c).astype(y_ref.dtype)
            return 0

        lax.fori_loop(0, nchunk, conv, 0)

    def col(off):
        return pl.BlockSpec((s, LANES), lambda j: (0, off + j))

    return pl.pallas_call(
        body, name=name, grid=(nblk,),
        in_specs=[col(first), col(first + nblk), col(first + 2 * nblk), pl.BlockSpec((8, LANES), lambda j: (0, j)),
                  pl.BlockSpec(memory_space=pl.ANY)],
        out_specs=pl.BlockSpec((s, LANES), lambda j: (0, before + j)),
        out_shape=jax.ShapeDtypeStruct(mix.shape, mix.dtype),
        scratch_shapes=[pltpu.VMEM((s + 2 * HALO, LANES), F32)],
        input_output_aliases={4: 0},
        compiler_params=_cparams("parallel"),
    )(proj, proj, proj, conv_w8, mix)


def _conv_bwd(dmix, proj, conv_w8, name):
    s = proj.shape[0]
    nblk = conv_w8.shape[1] // LANES
    first = 3 * nblk
    nchunk = s // CONV_ROWS

    def body(dy_ref, cb_ref, cc_ref, cu_ref, w_ref, dproj_ref, dw_ref, hpad, dcpad, dcb_ref, dcc_ref, dcu_ref, sems):
        j = pl.program_id(0)

        def writes(jj):
            return [pltpu.make_async_copy(
                buf, dproj_ref.at[:, pl.ds(pl.multiple_of((first + k * nblk + jj) * LANES, LANES), LANES)], sems.at[k])
                for k, buf in enumerate((dcb_ref, dcc_ref, dcu_ref))]

        @pl.when(j > 0)
        def _():
            for w in writes(j - 1):
                w.wait()

        hpad[pl.ds(0, 2 * HALO), :] = jnp.zeros((2 * HALO, LANES), F32)
        dcpad[pl.ds(s, 2 * HALO), :] = jnp.zeros((2 * HALO, LANES), F32)

        def fill(i, _):
            r0 = pl.multiple_of(i * CONV_ROWS, CONV_ROWS)
            hpad[pl.ds(r0 + 2 * HALO, CONV_ROWS), :] = (
                cc_ref[pl.ds(r0, CONV_ROWS), :].astype(F32) * cu_ref[pl.ds(r0, CONV_ROWS), :].astype(F32))
            return 0

        lax.fori_loop(0, nchunk, fill, 0)
        w0, w1, w2 = w_ref[0:1, :], w_ref[1:2, :], w_ref[2:3, :]

        def fold(v):
            return jnp.sum(v.reshape(CONV_ROWS // 8, 8, LANES), axis=0)

        def first_pass(i, acc):
            a0, a1, a2 = acc
            r0 = pl.multiple_of(i * CONV_ROWS, CONV_ROWS)
            win = hpad[pl.ds(r0 + HALO, CONV_ROWS + HALO), :]
            h0 = win[HALO:]
            h1 = pltpu.roll(win, 1, 0)[HALO:]
            h2 = pltpu.roll(win, 2, 0)[HALO:]
            c = w2 * h0 + w1 * h1 + w0 * h2
            dy = dy_ref[pl.ds(r0, CONV_ROWS), :]
            dcb_ref[pl.ds(r0, CONV_ROWS), :] = (dy * c).astype(dcb_ref.dtype)
            dc = dy * cb_ref[pl.ds(r0, CONV_ROWS), :].astype(F32)
            dcpad[pl.ds(r0, CONV_ROWS), :] = dc
            return a0 + fold(dc * h2), a1 + fold(dc * h1), a2 + fold(dc * h0)

        z8 = jnp.zeros((8, LANES), F32)
        a0, a1, a2 = lax.fori_loop(0, nchunk, first_pass, (z8, z8, z8))
        dw_ref[...] = jnp.concatenate(
            [jnp.sum(a0, axis=0, keepdims=True), jnp.sum(a1, axis=0, keepdims=True),
             jnp.sum(a2, axis=0, keepdims=True), jnp.zeros((5, LANES), F32)], axis=0)

        def second_pass(i, _):
            r0 = pl.multiple_of(i * CONV_ROWS, CONV_ROWS)
            win = dcpad[pl.ds(r0, CONV_ROWS + HALO), :]
            n = CONV_ROWS + HALO
            dh = (w2 * win[:CONV_ROWS] + w1 * pltpu.roll(win, n - 1, 0)[:CONV_ROWS]
                  + w0 * pltpu.roll(win, n - 2, 0)[:CONV_ROWS])
            dcc_ref[pl.ds(r0, CONV_ROWS), :] = (dh * cu_ref[pl.ds(r0, CONV_ROWS), :].astype(F32)).astype(dcc_ref.dtype)
            dcu_ref[pl.ds(r0, CONV_ROWS), :] = (dh * cc_ref[pl.ds(r0, CONV_ROWS), :].astype(F32)).astype(dcu_ref.dtype)
            return 0

        lax.fori_loop(0, nchunk, second_pass, 0)
        for w in writes(j):
            w.start()

        @pl.when(j == nblk - 1)
        def _():
            for w in writes(j):
                w.wait()

    def col(off):
        return pl.BlockSpec((s, LANES), lambda j: (0, off + j))

    stage = pltpu.VMEM((s, LANES), BF16)
    return pl.pallas_call(
        body, name=name, grid=(nblk,),
        in_specs=[col(nblk), col(first), col(first + nblk), col(first + 2 * nblk),
                  pl.BlockSpec((8, LANES), lambda j: (0, j))],
        out_specs=[pl.BlockSpec(memory_space=pl.ANY), pl.BlockSpec((8, LANES), lambda j: (0, j))],
        out_shape=[jax.ShapeDtypeStruct(proj.shape, BF16), jax.ShapeDtypeStruct((8, nblk * LANES), F32)],
        scratch_shapes=[pltpu.VMEM((s + 2 * HALO, LANES), F32), pltpu.VMEM((s + 2 * HALO, LANES), F32),
                        stage, stage, stage, pltpu.SemaphoreType.DMA((3,))],
        compiler_params=_cparams("arbitrary"),
    )(dmix, proj, proj, proj, conv_w8)


LOG2E = 1.4426950408889634
LN2 = 0.6931471805599453
NEG_BIG = -1e30
SATURATED = 160.0


def _cumsum_matrix(kind):
    j = lax.broadcasted_iota(jnp.int32, (KEY_CHUNK, 2 * KEY_CHUNK), 0)
    c = lax.broadcasted_iota(jnp.int32, (KEY_CHUNK, 2 * KEY_CHUNK), 1)
    tri = {"after": j > c, "upto": j <= c, "before": j < c}[kind]
    return jnp.where((c >= KEY_CHUNK) | tri, 1.0, 0.0).astype(BF16)


def _stack_heads(t, m0):
    zero = jnp.zeros_like(t)
    return jnp.concatenate([jnp.where(m0, t, zero), jnp.where(m0, zero, t)], axis=0)


def _softplus2(z):
    sp = jnp.maximum(z, 0.0) + jnp.log2(1.0 + jnp.exp2(-jnp.abs(z)))
    return sp, z - sp


def _key_chunk(ref, kc):
    return ref[pl.ds(pl.multiple_of(kc * KEY_CHUNK, KEY_CHUNK), KEY_CHUNK), :]


def _attn_bwd(qk, proj, dmix, rtot, used, tq, name, travel=()):
    s = qk.shape[0]
    nhp = qk.shape[1] // (2 * LANES)
    nc = tq // KEY_CHUNK
    nq = s // tq
    nt = len(travel)

    def body(used_ref, q_ref, k_ref, v_ref, do_ref, r_ref, cmi_ref, cme_ref, bias_ref, dq_ref, dk_ref, dv_ref,
             z_refs, ls_refs, sig_refs, sp_refs, gb_refs, pr_ref, gs_ref, copies):
        qi = pl.program_id(1)

        @pl.when(qi == 0)
        def _():
            dk_ref[...] = jnp.zeros_like(dk_ref)
            dv_ref[...] = jnp.zeros_like(dv_ref)

        if copies is not None:
            @pl.when(jnp.logical_and(pl.program_id(0) == 0, qi == 0))
            def _():
                _exchange_begin(copies)

        nslots = (qi + 1) * nc
        walked = used_ref[pl.program_id(0), qi].astype(jnp.int32)
        first = jnp.clip(nslots - walked, 0, nslots - nc) // nc * nc
        m0 = lax.broadcasted_iota(jnp.int32, (1, LANES), 1) < HEAD_DIM
        qs = _stack_heads(q_ref[...], m0)
        do = do_ref[...]
        dos = _stack_heads(do.astype(BF16), m0)
        dosl = _stack_heads((do * LN2).astype(BF16), m0)
        cmi = cmi_ref[...]
        cme = cme_ref[...]

        def chunk_at(i):
            return jnp.clip(i, first, nslots - 1)

        def scores(kc):
            return _dot(qs, _key_chunk(k_ref, kc), NT)

        def weights(ls, cs, da, pr, kc):
            a = jnp.exp2(ls - (pr - cs[:, :KEY_CHUNK]))
            gb = (a * da).astype(BF16)
            ks = pl.multiple_of(kc * KEY_CHUNK, KEY_CHUNK)
            dv_ref[pl.ds(ks, KEY_CHUNK), :] += _dot(a, dos, TN)
            return gb, jnp.exp2(ls), pr - cs[:, KEY_CHUNK:]

        def score_grads(gb, sig, cg, gs, dq, kc):
            dzb = (gb.astype(F32) * (1.0 - sig) - sig * (gs + cg[:, :KEY_CHUNK])).astype(BF16)
            ks = pl.multiple_of(kc * KEY_CHUNK, KEY_CHUNK)
            dk_ref[pl.ds(ks, KEY_CHUNK), :] += _dot(dzb, qs, TN)
            dq = dq + _dot(jnp.concatenate([dzb[:tq], dzb[tq:]], axis=1), _stack_heads(_key_chunk(k_ref, kc), m0), NN)
            return gs + cg[:, KEY_CHUNK:], dq

        def step(i, par, bias=None, stages="zswg"):
            cur, prv = par, 1 - par
            k1, k2 = chunk_at(i - 1), chunk_at(i - 2)
            z_next = scores(chunk_at(i + 1))
            if "w" in stages:
                cs = _dot(sp_refs[prv][...], cmi, NN)
                da = _dot(dosl, _key_chunk(v_ref, k1), NT)
            if "g" in stages:
                cg = _dot(gb_refs[cur][...], cme, NN)
            z = z_refs[cur][...]
            if bias is not None:
                z = z + bias
            sp, ls = _softplus2(z)
            sp_refs[cur][...] = sp.astype(BF16)
            ls_refs[cur][...] = ls
            if "g" in stages:
                gs, dq = score_grads(gb_refs[cur][...], sig_refs[cur][...], cg, gs_ref[...], dq_ref[...], k2)
                gs_ref[...] = gs
                dq_ref[...] = dq
            if "w" in stages:
                gb, sig, pr = weights(ls_refs[prv][...], cs, da, pr_ref[...], k1)
                gb_refs[prv][...] = gb
                sig_refs[prv][...] = sig
                pr_ref[...] = pr
            z_refs[prv][...] = z_next

        pr_ref[...] = jnp.concatenate([r_ref[:, :LANES], r_ref[:, LANES:]], axis=0)
        gs_ref[...] = jnp.zeros((2 * tq, LANES), F32)
        dq_ref[...] = jnp.zeros((tq, LANES), F32)
        z_refs[0][...] = scores(first)
        only_diagonal = first == nslots - nc
        step(first, 0, jnp.where(only_diagonal, bias_ref[0], 0.0), stages="zs")
        step(first + 1, 1, jnp.where(only_diagonal, bias_ref[1], 0.0), stages="zsw")

        def two_steps(j, _):
            step(2 * j, 0)
            step(2 * j + 1, 1)
            return 0

        lax.fori_loop(first // 2 + 1, nslots // 2 - 1, two_steps, 0)

        @pl.when(jnp.logical_not(only_diagonal))
        def _():
            step(nslots - 2, 0, bias_ref[0])
            step(nslots - 1, 1, bias_ref[1])

        k1, k2 = chunk_at(nslots - 1), chunk_at(nslots - 2)
        gb, sig, _ = weights(ls_refs[1][...], _dot(sp_refs[1][...], cmi, NN),
                             _dot(dosl, _key_chunk(v_ref, k1), NT), pr_ref[...], k1)
        gb2 = gb_refs[0][...]
        gs, dq = score_grads(gb2, sig_refs[0][...], _dot(gb2, cme, NN), gs_ref[...], dq_ref[...], k2)
        _, dq = score_grads(gb, sig, _dot(gb, cme, NN), gs, dq, k1)
        dq_ref[...] = dq

        if copies is not None:
            @pl.when(jnp.logical_and(pl.program_id(0) == nhp - 1, qi == nq - 1))
            def _():
                _exchange_finish(copies)

    def wrapped(*refs):
        ins, rest = refs[:9], refs[9:]
        srcs, rest = rest[:nt], rest[nt:]
        outs, rest = rest[:3], rest[3:]
        lands, rest = rest[:nt], rest[nt:]
        z0, z1, ls0, ls1, sg0, sg1, sp0, sp1, gb0, gb1, pr_ref, gs_ref = rest[:12]
        copies = _exchange_copies(srcs, lands, *rest[12:]) if nt else None
        body(*ins, *outs, (z0, z1), (ls0, ls1), (sg0, sg1), (sp0, sp1), (gb0, gb1), pr_ref, gs_ref, copies)

    assert nc == 2
    bias = _diag_bias(tq, True)
    bias = jnp.concatenate([bias[:, :, :KEY_CHUNK], bias[:, :, KEY_CHUNK:]], axis=1)
    qblk = pl.BlockSpec((tq, LANES), lambda p, i: (i, p))
    full = pl.BlockSpec((s, LANES), lambda p, i: (0, p))
    cmspec = pl.BlockSpec((KEY_CHUNK, 2 * KEY_CHUNK), lambda p, i: (0, 0))
    anyspec = pl.BlockSpec(memory_space=pl.ANY)
    shape = jax.ShapeDtypeStruct((s, nhp * LANES), F32)
    f32buf = pltpu.VMEM((2 * tq, LANES), F32)
    bf16buf = pltpu.VMEM((2 * tq, LANES), BF16)
    outs = pl.pallas_call(
        wrapped, name=name, grid=(nhp, nq),
        in_specs=[pl.BlockSpec(memory_space=pltpu.SMEM),
                  qblk,
                  pl.BlockSpec((s, LANES), lambda p, i: (0, nhp + p)),
                  pl.BlockSpec((s, LANES), lambda p, i: (0, 2 * nhp + p)),
                  qblk,
                  pl.BlockSpec((tq, 2 * LANES), lambda p, i: (i, p)),
                  cmspec, cmspec,
                  pl.BlockSpec((nc, 2 * tq, LANES), lambda p, i: (0, 0, 0))] + [anyspec] * nt,
        out_specs=[qblk, full, full] + [anyspec] * nt,
        out_shape=[shape, shape, shape] + [jax.ShapeDtypeStruct(t.shape, t.dtype) for t in travel],
        scratch_shapes=[f32buf] * 6 + [bf16buf] * 4 + [f32buf] * 2 + (_exchange_scratch(nt) if nt else []),
        compiler_params=_cparams("arbitrary", "arbitrary"),
    )(used, qk, qk, proj, dmix, rtot, _cumsum_matrix("upto"), _cumsum_matrix("before"), bias, *travel)
    return outs[0], outs[1], outs[2], list(outs[3:])


def _pair_cumsum_matrix(kind):
    j = lax.broadcasted_iota(jnp.int32, (2 * KEY_CHUNK, 4 * KEY_CHUNK), 0)
    c = lax.broadcasted_iota(jnp.int32, (2 * KEY_CHUNK, 4 * KEY_CHUNK), 1)
    same_head = (j // KEY_CHUNK) == ((c // KEY_CHUNK) % 2)
    jj, cc = j % KEY_CHUNK, c % KEY_CHUNK
    tri = {"after": jj > cc, "upto": jj <= cc, "before": jj < cc}[kind]
    return jnp.where(same_head & ((c >= 2 * KEY_CHUNK) | tri), 1.0, 0.0).astype(BF16)


def _diag_bias(tq, ascending):
    nc = tq // KEY_CHUNK
    shape = (nc, tq, 2 * KEY_CHUNK)
    d = lax.broadcasted_iota(jnp.int32, shape, 0)
    r = lax.broadcasted_iota(jnp.int32, shape, 1)
    c = lax.broadcasted_iota(jnp.int32, shape, 2) % KEY_CHUNK
    chunk = d if ascending else nc - 1 - d
    return jnp.where(chunk * KEY_CHUNK + c < r, 0.0, NEG_BIG).astype(F32)


def _attn_fwd(qk, proj, tq, name, mix_cols, shards=()):
    s = qk.shape[0]
    nhp = qk.shape[1] // (2 * LANES)
    nc = tq // KEY_CHUNK
    nq = s // tq
    ng = len(shards)
    assert nc == 2
    w = 2 * KEY_CHUNK

    def body(q_ref, k_ref, v_ref, cm_ref, bias_ref, o_ref, r_ref, used_ref, z_refs, ls_refs, cs_refs, ct_refs,
             sp_refs, ab_refs, acc_ref, gather):
        qi = pl.program_id(1)
        if gather is not None:
            @pl.when(jnp.logical_and(pl.program_id(0) == 0, qi == 0))
            def _():
                gather.begin()

        nslots = (qi + 1) * nc
        m0 = lax.broadcasted_iota(jnp.int32, (1, LANES), 1) < HEAD_DIM
        q = q_ref[...]
        cm = cm_ref[...]

        def chunk_at(i):
            return jnp.clip(nslots - 1 - i, 0, nslots - 1)

        def scores(kc):
            return _dot(q, _stack_heads(_key_chunk(k_ref, kc), m0), NT)

        def values(ab, kc):
            return _dot(ab, _stack_heads(_key_chunk(v_ref, kc), m0), NN)

        def step(i, par, bias=None, stages="zscwv"):
            cur, prv = par, 1 - par
            if "z" in stages:
                z_next = scores(chunk_at(i + 1))
            if "c" in stages:
                cs = _dot(sp_refs[prv][...], cm, NN)
            if "v" in stages:
                pv = values(ab_refs[prv][...], chunk_at(i - 3))
            if "w" in stages:
                rs = r_ref[...]
                r_ref[...] = rs + ct_refs[cur][...]
                ab_refs[cur][...] = jnp.exp2(ls_refs[cur][...] - cs_refs[cur][...] - rs).astype(BF16)
            if "s" in stages:
                z = z_refs[cur][...]
                if bias is not None:
                    z = z + bias
                sp, ls = _softplus2(z)
                sp_refs[cur][...] = sp.astype(BF16)
                ls_refs[cur][...] = ls
            if "v" in stages:
                acc_ref[...] += pv
            if "c" in stages:
                cs_refs[prv][...] = cs[:, :w]
                ct_refs[prv][...] = cs[:, w:]
            if "z" in stages:
                z_refs[prv][...] = z_next

        z_refs[0][...] = scores(chunk_at(0))
        ab_refs[1][...] = jnp.zeros((tq, w), BF16)
        r_ref[...] = jnp.zeros((tq, w), F32)
        acc_ref[...] = jnp.zeros((tq, LANES), F32)
        step(0, 0, bias_ref[0], stages="zs")
        step(1, 1, bias_ref[1], stages="zsc")

        def two_steps(carry):
            j, _ = carry
            step(2 * j, 0)
            step(2 * j + 1, 1)
            return j + 1, jnp.min(jnp.minimum(r_ref[:, :KEY_CHUNK], r_ref[:, KEY_CHUNK:]))

        pairs, low = lax.while_loop(lambda c: jnp.logical_and(c[0] < nslots // 2, c[1] < SATURATED), two_steps,
                                    (jnp.int32(1), jnp.float32(0.0)))
        entered = 2 * pairs
        saturated = low >= SATURATED

        @pl.when(saturated)
        def _():
            step(entered, 0, stages="v")

        @pl.when(jnp.logical_not(saturated))
        def _():
            step(entered, 0, stages="cwv")
            step(entered + 1, 1, stages="wv")
            step(entered + 2, 0, stages="v")

        o_ref[...] = acc_ref[...].astype(o_ref.dtype)
        used_ref[pl.program_id(0), qi] = jnp.where(saturated, entered - 2, entered).astype(F32)

        if gather is not None:
            @pl.when(jnp.logical_and(pl.program_id(0) == nhp - 1, qi == nq // 2))
            def _():
                gather.relay()

            @pl.when(jnp.logical_and(pl.program_id(0) == nhp - 1, qi == nq - 1))
            def _():
                gather.finish()

    def wrapped(*refs):
        ins, rest = refs[:5], refs[5:]
        srcs, rest = rest[:ng], rest[ng:]
        outs, rest = rest[:3], rest[3:]
        dsts, scratch = rest[:ng], rest[ng:]
        z, ls, cs, ct, sp, ab = [scratch[2 * j:2 * j + 2] for j in range(6)]
        gather = _Gather(srcs, dsts, *scratch[13:]) if ng else None
        body(*ins, *outs, z, ls, cs, ct, sp, ab, scratch[12], gather)

    f32buf = pltpu.VMEM((tq, w), F32)
    bf16buf = pltpu.VMEM((tq, w), BF16)
    anyspec = pl.BlockSpec(memory_space=pl.ANY)
    outs = pl.pallas_call(
        wrapped, name=name, grid=(nhp, nq),
        in_specs=[pl.BlockSpec((tq, LANES), lambda p, i: (i, p)),
                  pl.BlockSpec((s, LANES), lambda p, i: (0, nhp + p)),
                  pl.BlockSpec((s, LANES), lambda p, i: (0, 2 * nhp + p)),
                  pl.BlockSpec((w, 2 * w), lambda p, i: (0, 0)),
                  pl.BlockSpec((nc, tq, w), lambda p, i: (0, 0, 0))] + [anyspec] * ng,
        out_specs=[pl.BlockSpec((tq, LANES), lambda p, i: (i, p)),
                   pl.BlockSpec((tq, w), lambda p, i: (i, p)),
                   pl.BlockSpec(memory_space=pltpu.SMEM)] + [anyspec] * ng,
        out_shape=[jax.ShapeDtypeStruct((s, mix_cols), BF16),
                   jax.ShapeDtypeStruct((s, nhp * w), F32),
                   jax.ShapeDtypeStruct((nhp, nq), F32)] + _gathered_shapes(shards),
        scratch_shapes=([f32buf] * 8 + [bf16buf] * 4 + [pltpu.VMEM((tq, LANES), F32)]
                        + (_gather_scratch(ng) if ng else [])),
        compiler_params=_cparams("arbitrary", "arbitrary"),
    )(qk, qk, proj, _pair_cumsum_matrix("after"), _diag_bias(tq, False), *shards)
    return outs[0], outs[1], outs[2], list(outs[3:])


BLOCK_PAIR = 2
MXU_WIDTH = 256


def _side_by_side(b_ref):
    return jnp.concatenate([b_ref[p] for p in range(BLOCK_PAIR)], axis=1)


def _mm_blocks(h, ga, widx, tm, name):
    s, d = h.shape
    nb, cols = ga.shape[1], ga.shape[3]

    def body(a_ref, b_ref, o_ref):
        o_ref[...] = _dot(a_ref[...], _side_by_side(b_ref), NN).astype(o_ref.dtype)

    return pl.pallas_call(
        body, name=name, grid=(s // tm, nb // BLOCK_PAIR),
        in_specs=[pl.BlockSpec((tm, d), lambda i, j: (i, 0)),
                  pl.BlockSpec((None, BLOCK_PAIR, d, cols), lambda i, j: (widx, j, 0, 0))],
        out_specs=pl.BlockSpec((tm, BLOCK_PAIR * cols), lambda i, j: (i, j)),
        out_shape=jax.ShapeDtypeStruct((s, nb * cols), BF16),
        compiler_params=_cparams("parallel", "arbitrary"),
    )(h, ga)


def _mm_swiglu(h, ga, gidx, uidx, tm, name):
    s, d = h.shape
    nb, cols = ga.shape[1], ga.shape[3]

    def body(a_ref, bg_ref, bu_ref, g_ref, u_ref, act_ref):
        a = a_ref[...]
        g = _dot(a, _side_by_side(bg_ref), NN)
        u = _dot(a, _side_by_side(bu_ref), NN)
        g_ref[...] = g.astype(g_ref.dtype)
        u_ref[...] = u.astype(u_ref.dtype)
        act_ref[...] = (g * (1.0 / (1.0 + jnp.exp(-g))) * u).astype(act_ref.dtype)

    def wspec(idx):
        return pl.BlockSpec((None, BLOCK_PAIR, d, cols), lambda i, j: (idx, j, 0, 0))

    out = pl.BlockSpec((tm, BLOCK_PAIR * cols), lambda i, j: (i, j))
    shape = jax.ShapeDtypeStruct((s, nb * cols), BF16)
    return pl.pallas_call(
        body, name=name, grid=(s // tm, nb // BLOCK_PAIR),
        in_specs=[pl.BlockSpec((tm, d), lambda i, j: (i, 0)), wspec(gidx), wspec(uidx)],
        out_specs=[out, out, out], out_shape=[shape, shape, shape],
        compiler_params=_cparams("parallel", "arbitrary"),
    )(h, ga, ga)


def _mm_residual_norm(a, w3, lidx, res, gain, tm, name):
    s, k = a.shape
    n = w3.shape[2]

    def body(a_ref, b_ref, r_ref, g_ref, o_ref, h_ref):
        xv = r_ref[...] + _dot(a_ref[...], b_ref[...], NN)
        o_ref[...] = xv
        r = lax.rsqrt(jnp.mean(xv * xv, axis=-1, keepdims=True) + EPS)
        h_ref[...] = ((xv * r) * g_ref[...]).astype(h_ref.dtype)

    row = pl.BlockSpec((tm, n), lambda i: (i, 0))
    return pl.pallas_call(
        body, name=name, grid=(s // tm,),
        in_specs=[pl.BlockSpec((tm, k), lambda i: (i, 0)),
                  pl.BlockSpec((None, k, n), lambda i: (lidx, 0, 0), pipeline_mode=pl.Buffered(1)),
                  row, pl.BlockSpec((1, n), lambda i: (0, 0))],
        out_specs=[row, row],
        out_shape=[jax.ShapeDtypeStruct((s, n), F32), jax.ShapeDtypeStruct((s, n), BF16)],
        compiler_params=_cparams("parallel"),
    )(a, w3, res, gain)


def _mm_residual_loss(a, w3, lidx, res, target, tm, name):
    s, k = a.shape
    n = w3.shape[2]
    nsteps = s // tm

    def body(a_ref, b_ref, r_ref, t_ref, dy_ref, dyb_ref, l_ref, acc):
        i = pl.program_id(0)
        diff = r_ref[...] + _dot(a_ref[...], b_ref[...], NN) - t_ref[...]
        dy_ref[...] = diff * (1.0 / n)
        dyb_ref[...] = (diff * (1.0 / n)).astype(dyb_ref.dtype)
        part = jnp.sum((diff * diff).reshape(tm // 8, 8, n), axis=0)

        @pl.when(i == 0)
        def _():
            acc[...] = part

        @pl.when(i > 0)
        def _():
            acc[...] += part

        @pl.when(i == nsteps - 1)
        def _():
            tot = jnp.sum(jnp.sum(acc[...], axis=1, keepdims=True), axis=0, keepdims=True)
            l_ref[...] = jnp.broadcast_to(tot * (0.5 / n), (8, LANES))

    row = pl.BlockSpec((tm, n), lambda i: (i, 0))
    return pl.pallas_call(
        body, name=name, grid=(nsteps,),
        in_specs=[pl.BlockSpec((tm, k), lambda i: (i, 0)),
                  pl.BlockSpec((None, k, n), lambda i: (lidx, 0, 0), pipeline_mode=pl.Buffered(1)),
                  row, row],
        out_specs=[row, row, pl.BlockSpec((8, LANES), lambda i: (0, 0))],
        out_shape=[jax.ShapeDtypeStruct((s, n), F32), jax.ShapeDtypeStruct((s, n), BF16),
                   jax.ShapeDtypeStruct((8, LANES), F32)],
        scratch_shapes=[pltpu.VMEM((8, n), F32)],
        compiler_params=_cparams("arbitrary"),
    )(a, w3, res, target)


def _mm_nt(a, w3, lidx, tm, tn, name):
    s, k = a.shape
    n = w3.shape[1]

    def body(a_ref, b_ref, o_ref):
        o_ref[...] = _dot(a_ref[...], b_ref[...], NT)

    return pl.pallas_call(
        body, name=name, grid=(s // tm, n // tn),
        in_specs=[pl.BlockSpec((tm, k), lambda i, j: (i, 0)),
                  pl.BlockSpec((None, tn, k), lambda i, j: (lidx, j, 0))],
        out_specs=pl.BlockSpec((tm, tn), lambda i, j: (i, j)),
        out_shape=jax.ShapeDtypeStruct((s, n), F32),
        compiler_params=_cparams("parallel", "arbitrary"),
    )(a, w3)


def _mm_nt_swiglu_bwd(dx, wd3, lidx, g, u, tm, name, travel=()):
    s, d = dx.shape
    cols = BLOCK_PAIR * (g.shape[1] // N_DEV)

    def body(a_ref, b_ref, g_ref, u_ref, dg_ref, du_ref):
        a = a_ref[...]
        for c0 in range(0, cols, MXU_WIDTH):
            sl = slice(c0, c0 + MXU_WIDTH)
            dact = _dot(a, b_ref[sl, :], NT)
            gv = g_ref[:, sl].astype(F32)
            sig = 0.5 * jnp.tanh(0.5 * gv) + 0.5
            silu = gv * sig
            du_ref[:, sl] = (dact * silu).astype(du_ref.dtype)
            dsilu = sig + silu * (1.0 - sig)
            dg_ref[:, sl] = (dact * u_ref[:, sl].astype(F32) * dsilu).astype(dg_ref.dtype)

    blk = pl.BlockSpec((tm, cols), lambda i, j: (i, j))
    shape = jax.ShapeDtypeStruct(g.shape, BF16)
    grid = (s // tm, N_DEV // BLOCK_PAIR)
    body, more_in, more_out, more_shapes, more_scratch = _host_exchange(body, 4, 2, travel, grid)
    outs = pl.pallas_call(
        body, name=name, grid=grid,
        in_specs=[pl.BlockSpec((tm, d), lambda i, j: (i, 0)),
                  pl.BlockSpec((None, cols, d), lambda i, j: (lidx, j, 0)), blk, blk] + more_in,
        out_specs=[blk, blk] + more_out, out_shape=[shape, shape] + more_shapes,
        scratch_shapes=more_scratch,
        compiler_params=_cparams("arbitrary", "arbitrary"),
    )(dx, wd3, g, u, *travel)
    return outs[0], outs[1], list(outs[2:])


def _mm_nt_norm_bwd(das, ga, widxs, x, gain, dres, tm, name, travel=()):
    s = das[0].shape[0]
    nb, d, cols = ga.shape[1], ga.shape[2], ga.shape[3]
    nw = len(das)
    nsteps = s // tm

    def body(*refs):
        a_refs, b_refs = refs[:nw], refs[nw:2 * nw]
        x_ref, g_ref, dres_ref, dx_ref, dxb_ref, dg_ref = refs[2 * nw:]
        i = pl.program_id(0)
        dhv = None
        wide = BLOCK_PAIR * cols
        for w in range(nw):
            for k in range(nb // BLOCK_PAIR):
                b = jnp.concatenate([b_refs[w][BLOCK_PAIR * k + p] for p in range(BLOCK_PAIR)], axis=1)
                part = _dot(a_refs[w][:, k * wide:(k + 1) * wide], b, NT)
                dhv = part if dhv is None else dhv + part
        xv = x_ref[...]
        r = lax.rsqrt(jnp.mean(xv * xv, axis=-1, keepdims=True) + EPS)
        xhat = xv * r
        dxh = dhv * g_ref[...]
        dxv = dres_ref[...] + r * (dxh - xhat * jnp.mean(dxh * xhat, axis=-1, keepdims=True))
        dx_ref[...] = dxv
        dxb_ref[...] = dxv.astype(dxb_ref.dtype)
        part = jnp.sum((dhv * xhat).reshape(tm // 8, 8, d), axis=0)

        @pl.when(i == 0)
        def _():
            dg_ref[...] = part

        @pl.when(i > 0)
        def _():
            dg_ref[...] += part

        @pl.when(i == nsteps - 1)
        def _():
            dg_ref[...] = jnp.broadcast_to(jnp.sum(dg_ref[...], axis=0, keepdims=True), (8, d))

    def wspec(idx):
        return pl.BlockSpec((None, nb, d, cols), lambda i: (idx, 0, 0, 0), pipeline_mode=pl.Buffered(1))

    row = pl.BlockSpec((tm, d), lambda i: (i, 0))
    body, more_in, more_out, more_shapes, more_scratch = _host_exchange(body, 2 * nw + 3, 3, travel, (nsteps,))
    outs = pl.pallas_call(
        body, name=name, grid=(nsteps,),
        in_specs=([pl.BlockSpec((tm, nb * cols), lambda i: (i, 0))] * nw + [wspec(i) for i in widxs]
                  + [row, pl.BlockSpec((1, d), lambda i: (0, 0)), row] + more_in),
        out_specs=[row, row, pl.BlockSpec((8, d), lambda i: (0, 0))] + more_out,
        out_shape=[jax.ShapeDtypeStruct((s, d), F32), jax.ShapeDtypeStruct((s, d), BF16),
                   jax.ShapeDtypeStruct((8, d), F32)] + more_shapes,
        scratch_shapes=more_scratch,
        compiler_params=_cparams("arbitrary"),
    )(*das, *([ga] * nw), x, gain, dres, *travel)
    return outs[0], outs[1], outs[2], list(outs[3:])


def _mm_tn(a, b, ta, tb, tk, out_blocks, name, travel=()):
    s, ka = a.shape
    nb = b.shape[1]
    nk = s // tk
    cols = tb
    if out_blocks:
        tb = BLOCK_PAIR * cols

    def body(a_ref, b_ref, o_ref, ob_ref):
        k = pl.program_id(2)
        part = _dot(a_ref[...], b_ref[...], TN)

        def put(first):
            if out_blocks:
                for p in range(BLOCK_PAIR):
                    piece = part[:, p * cols:(p + 1) * cols]
                    o_ref[p] = piece if first else o_ref[p] + piece
            else:
                o_ref[...] = part if first else o_ref[...] + part

        @pl.when(k == 0)
        def _():
            put(True)

        @pl.when(k > 0)
        def _():
            put(False)

        @pl.when(k == nk - 1)
        def _():
            ob_ref[...] = o_ref[...].astype(ob_ref.dtype)

    if out_blocks:
        out_spec = pl.BlockSpec((BLOCK_PAIR, ta, cols), lambda i, j, k: (j, i, 0))
        shape = (nb // cols, ka, cols)
    else:
        out_spec = pl.BlockSpec((ta, tb), lambda i, j, k: (i, j))
        shape = (ka, nb)
    grid = (ka // ta, nb // tb, nk)
    body, more_in, more_out, more_shapes, more_scratch = _host_exchange(body, 2, 2, travel, grid)
    outs = pl.pallas_call(
        body, name=name, grid=grid,
        in_specs=[pl.BlockSpec((tk, ta), lambda i, j, k: (k, i)),
                  pl.BlockSpec((tk, tb), lambda i, j, k: (k, j))] + more_in,
        out_specs=[out_spec, out_spec] + more_out,
        out_shape=[jax.ShapeDtypeStruct(shape, F32), jax.ShapeDtypeStruct(shape, BF16)] + more_shapes,
        scratch_shapes=more_scratch,
        compiler_params=_cparams("arbitrary", "arbitrary", "arbitrary"),
    )(a, b, *travel)
    return (outs[0], outs[1]), list(outs[2:])


def _adamw(g, w, m, v, name):
    rows, cols = g.shape
    c1 = 1.0 / (1.0 - ADAM_B1 ** ADAM_STEP)
    c2 = 1.0 / (1.0 - ADAM_B2 ** ADAM_STEP)

    def body(p_ref, w_ref, m_ref, v_ref, g_ref, d_ref, nm_ref, nv_ref):
        gv = p_ref[...]
        nm = ADAM_B1 * m_ref[...] + (1.0 - ADAM_B1) * gv
        nv = ADAM_B2 * v_ref[...] + (1.0 - ADAM_B2) * (gv * gv)
        g_ref[...] = gv
        nm_ref[...] = nm
        nv_ref[...] = nv
        d_ref[...] = -ADAM_LR * ((nm * c1) / (jnp.sqrt(nv * c2) + ADAM_EPS) + ADAM_WD * w_ref[...])

    blk = pl.BlockSpec((rows, cols), lambda i: (0, 0))
    shape = jax.ShapeDtypeStruct((rows, cols), F32)
    return pl.pallas_call(
        body, name=name, grid=(1,),
        in_specs=[blk] * 4, out_specs=[blk] * 4, out_shape=[shape] * 4,
        compiler_params=_cparams("arbitrary"),
    )(g, w, m, v)


def _adamw_sharded(parts, grads, my, w, m, v, tr, name):
    depth, rows, cols = w.shape
    p, pr, pc = parts[0].shape
    c1 = 1.0 / (1.0 - ADAM_B1 ** ADAM_STEP)
    c2 = 1.0 / (1.0 - ADAM_B2 ** ADAM_STEP)

    def body(my_ref, *refs):
        p_refs, own_refs = refs[:depth], refs[depth:2 * depth]
        w_ref, m_ref, v_ref, g_ref, d_ref, nm_ref, nv_ref = refs[2 * depth:]
        layer = pl.program_id(0)
        for ll in range(depth):
            @pl.when(layer == ll)
            def _(ll=ll):
                mine = own_refs[ll][...]
                g = jnp.where(my_ref[0] == 0, mine, p_refs[ll][0].astype(F32))
                for k in range(1, p):
                    g = g + jnp.where(my_ref[0] == k, mine, p_refs[ll][k].astype(F32))
                g = g[:, :cols]
                nm = ADAM_B1 * m_ref[...] + (1.0 - ADAM_B1) * g
                nv = ADAM_B2 * v_ref[...] + (1.0 - ADAM_B2) * (g * g)
                g_ref[...] = g
                nm_ref[...] = nm
                nv_ref[...] = nv
                d_ref[...] = -ADAM_LR * ((nm * c1) / (jnp.sqrt(nv * c2) + ADAM_EPS) + ADAM_WD * w_ref[...])

    def row_block(ll, l, i):
        return jnp.where(l == ll, i, 0)

    blk = pl.BlockSpec((None, tr, cols), lambda l, i, my_: (l, i, 0))
    shape = jax.ShapeDtypeStruct((depth, rows, cols), F32)
    return pl.pallas_call(
        body, name=name,
        grid_spec=pltpu.PrefetchScalarGridSpec(
            num_scalar_prefetch=1, grid=(depth, rows // tr),
            in_specs=([pl.BlockSpec((p, tr, pc), lambda l, i, my_, ll=ll: (0, row_block(ll, l, i), 0))
                       for ll in range(depth)]
                      + [pl.BlockSpec((None, tr, pc), lambda l, i, my_, ll=ll: (my_[0], row_block(ll, l, i), 0))
                         for ll in range(depth)]
                      + [blk, blk, blk]),
            out_specs=[blk] * 4),
        out_shape=[shape] * 4,
        compiler_params=_cparams("arbitrary", "arbitrary"),
    )(my, *parts, *grads, w, m, v)


def _place():
    x, y, c = lax.axis_index("x"), lax.axis_index("y"), lax.axis_index("c")
    return x, y, c


class _Gather:
    def __init__(self, srcs, dsts, send_sems, recv_sems, local_sems):
        na = len(srcs)
        x, y, c = _place()
        me, sibling = (x, y, c), (x, y, 1 - c)
        chips = [(1 - x, y), (x, 1 - y), (1 - x, 1 - y)]

        def slot(a, dev):
            return dsts[a].at[:, pl.ds(4 * dev[0] + 2 * dev[1] + dev[2], 1)]

        def copy(k, a, block, to, from_shard=False):
            return pltpu.make_async_remote_copy(
                src_ref=srcs[a] if from_shard else slot(a, block), dst_ref=slot(a, block),
                send_sem=send_sems.at[k, a], recv_sem=recv_sems.at[k, a], device_id=to, device_id_type=MESH)

        pairs = [(j, chip, a) for j, chip in enumerate(chips) for a in range(na)]
        self.mine = [pltpu.make_async_copy(srcs[a], slot(a, me), local_sems.at[a]) for a in range(na)]
        self.first = [copy(0, a, me, sibling, True) for a in range(na)]
        self.first += [copy(1 + j, a, me, (*chip, c), True) for j, chip, a in pairs]
        self.over_ici = [copy(1 + j, a, (*chip, c), me) for j, chip, a in pairs]
        self.passed = [copy(4 + j, a, (*chip, c), sibling) for j, chip, a in pairs]
        self.from_sibling = [copy(0, a, sibling, me) for a in range(na)]
        self.from_sibling += [copy(4 + j, a, (*chip, 1 - c), me) for j, chip, a in pairs]

    def begin(self):
        for cp in self.mine + self.first:
            cp.start()

    def relay(self):
        for arrived, onward in zip(self.over_ici, self.passed):
            arrived.wait_recv()
            onward.start()

    def finish(self):
        for cp in self.from_sibling:
            cp.wait_recv()
        for cp in self.first + self.passed:
            cp.wait_send()
        for cp in self.mine:
            cp.wait()


def _gather_scratch(na):
    return [pltpu.SemaphoreType.DMA((7, na)), pltpu.SemaphoreType.DMA((7, na)), pltpu.SemaphoreType.DMA((na,))]


def _gathered_shapes(shards):
    return [jax.ShapeDtypeStruct((a.shape[0], N_DEV) + a.shape[2:], a.dtype) for a in shards]


_RELATIONS = [(dx, dy, dc) for dx in (0, 1) for dy in (0, 1) for dc in (0, 1)][1:]


def _flip(v, d):
    return 1 - v if d else v


def _exchange_copies(srcs, dsts, send_sems, recv_sems, local_sems):
    x, y, c = _place()
    my = 4 * x + 2 * y + c
    na = len(srcs)
    mine = [pltpu.make_async_copy(srcs[a].at[pl.ds(my, 1)], dsts[a].at[pl.ds(my, 1)], local_sems.at[a])
            for a in range(na)]
    sends, recvs = [], []
    for k, (dx, dy, dc) in enumerate(_RELATIONS):
        peer = (_flip(x, dx), _flip(y, dy), _flip(c, dc))
        pidx = 4 * peer[0] + 2 * peer[1] + peer[2]
        for a in range(na):
            for into, out in ((my, sends), (pidx, recvs)):
                out.append(pltpu.make_async_remote_copy(
                    src_ref=srcs[a].at[pl.ds(pidx, 1)], dst_ref=dsts[a].at[pl.ds(into, 1)],
                    send_sem=send_sems.at[k, a], recv_sem=recv_sems.at[k, a], device_id=peer, device_id_type=MESH))
    return mine, sends, recvs


def _exchange_begin(copies):
    mine, sends, _ = copies
    for cp in mine + sends:
        cp.start()


def _exchange_finish(copies):
    mine, sends, recvs = copies
    for cp in recvs:
        cp.wait_recv()
    for cp in sends:
        cp.wait_send()
    for cp in mine:
        cp.wait()


def _exchange_scratch(na):
    return [pltpu.SemaphoreType.DMA((7, na)), pltpu.SemaphoreType.DMA((7, na)), pltpu.SemaphoreType.DMA((na,))]


def _host_exchange(body, n_in, n_out, travel, grid):
    nt = len(travel)
    if not nt:
        return body, [], [], [], []

    def wrapped(*refs):
        ins, srcs = refs[:n_in], refs[n_in:n_in + nt]
        outs, rest = refs[n_in + nt:n_in + nt + n_out], refs[n_in + nt + n_out:]
        dsts, scratch = rest[:nt], rest[nt:]
        copies = _exchange_copies(srcs, dsts, *scratch[-3:])
        first = last = None
        for axis, size in enumerate(grid):
            at_start, at_end = pl.program_id(axis) == 0, pl.program_id(axis) == size - 1
            first = at_start if first is None else jnp.logical_and(first, at_start)
            last = at_end if last is None else jnp.logical_and(last, at_end)

        @pl.when(first)
        def _():
            _exchange_begin(copies)

        body(*ins, *outs, *scratch[:-3])

        @pl.when(last)
        def _():
            _exchange_finish(copies)

    anyspec = pl.BlockSpec(memory_space=pl.ANY)
    return (wrapped, [anyspec] * nt, [anyspec] * nt, [jax.ShapeDtypeStruct(t.shape, t.dtype) for t in travel],
            _exchange_scratch(nt))


def _all_reduce_small(v, name):
    r, c_ = v.shape

    def body(v_ref, o_ref, gath, send_sems, recv_sems):
        x, y, c = _place()
        my = 4 * x + 2 * y + c
        gath[my] = v_ref[...]
        sends = []
        for k, (dx, dy, dc) in enumerate(_RELATIONS):
            peer = (_flip(x, dx), _flip(y, dy), _flip(c, dc))
            cp = pltpu.make_async_remote_copy(
                src_ref=v_ref, dst_ref=gath.at[my], send_sem=send_sems.at[k], recv_sem=recv_sems.at[k],
                device_id=peer, device_id_type=MESH)
            cp.start()
            sends.append((cp, 4 * peer[0] + 2 * peer[1] + peer[2], k, peer))
        for cp, pidx, k, peer in sends:
            pltpu.make_async_remote_copy(
                src_ref=v_ref, dst_ref=gath.at[pidx], send_sem=send_sems.at[k], recv_sem=recv_sems.at[k],
                device_id=peer, device_id_type=MESH).wait_recv()
        for cp, *_ in sends:
            cp.wait_send()
        tot = gath[0]
        for k in range(1, N_DEV):
            tot = tot + gath[k]
        o_ref[...] = tot

    vm = pl.BlockSpec(memory_space=pltpu.VMEM)
    return pl.pallas_call(
        body, name=name, in_specs=[vm], out_specs=vm,
        out_shape=jax.ShapeDtypeStruct((r, c_), F32),
        scratch_shapes=[pltpu.VMEM((N_DEV, r, c_), F32), pltpu.SemaphoreType.DMA((7,)),
                        pltpu.SemaphoreType.DMA((7,))],
    )(v)


TM = 512
TM_MATMUL = 2048
TM_RESIDUAL = 1024
TQ = 256


def _device_blocks(t):
    return t.reshape(N_DEV, -1, t.shape[-1])


def _pad_to(a, axis, size):
    pad = [(0, 0)] * a.ndim
    pad[axis] = (0, size - a.shape[axis])
    return jnp.pad(a, pad)


def _local_step(x, target, first_shards, late_shards, conv_shard, norm_mix, q_norm, k_norm, norm_ffn):
    depth, d = norm_mix.shape
    cols = first_shards[0].shape[3]
    tm, tq = min(TM, x.shape[0]), min(TQ, x.shape[0])
    tmm, tmr = min(TM_MATMUL, x.shape[0]), min(TM_RESIDUAL, x.shape[0])
    attn = d // 2
    nheads = attn // HEAD_DIM
    scale = HEAD_DIM ** -0.5 * LOG2E
    saved = []
    h1, (g_in0, g_conv) = _rmsnorm_fwd(x, norm_mix[0][None], tm, "norm_mix_fwd_0", first_shards)
    conv_full = g_conv[0, :, :depth * 3, :conv_shard].transpose(1, 0, 2).reshape(depth, 3, N_DEV * conv_shard)
    for l in range(depth):
        w_in = (g_in0, 0) if l == 0 else (g_rest, 3 * (l - 1))
        proj = _mm_blocks(h1, *w_in, tmm, f"proj_in_{l}")
        qk_gain = jnp.concatenate([jnp.tile(q_norm[l], nheads) * scale, jnp.tile(k_norm[l], nheads)])[None]
        qk = _qknorm_fwd(proj, qk_gain, tmm, f"qknorm_fwd_{l}")
        o, rtot, used, gathered = _attn_fwd(qk, proj, tq, f"attn_fwd_{l}", d, late_shards if l == 0 else ())
        if l == 0:
            g_gu0, g_rest, gb, gc = gathered if depth > 1 else (gathered[0], None, *gathered[1:])
            gb = gb.reshape(depth, -1, d)
            gc = gc.reshape(depth, -1, d)
        w_gu = (g_gu0, 0, 1) if l == 0 else (g_rest, 3 * (l - 1) + 1, 3 * (l - 1) + 2)
        conv_w8 = _pad_to(conv_full[l], 0, 8)
        mix = _conv_fwd(proj, conv_w8, o, f"conv_fwd_{l}")
        x1, h2 = _mm_residual_norm(mix, gb, l, x, norm_ffn[l][None], tmr, f"proj_out_{l}")
        g, u, act = _mm_swiglu(h2, *w_gu, tmr, f"ffn_up_{l}")
        saved.append((x, h1, proj, qk_gain, qk, rtot, used, conv_w8, mix, x1, h2, g, u, act, w_in, w_gu))
        if l + 1 < depth:
            x, h1 = _mm_residual_norm(act, gc, l, x1, norm_mix[l + 1][None], tm, f"ffn_down_{l}")
        else:
            dx, dxb, loss = _mm_residual_loss(act, gc, l, x1, target, tm, f"ffn_down_{l}")

    grads = [None] * depth
    small = [None] * depth
    landed = {}
    for l in reversed(range(depth)):
        x0, h1, proj, qk_gain, qk, rtot, used, conv_w8, mix, x1, h2, g, u, act, w_in, w_gu = saved[l]
        d = x0.shape[1]
        late = [[], [], [], []]
        if l == 0:
            for n, i in enumerate(5 * ll + j for ll in range(1, depth) for j in (0, 1, 2, 4, 3)):
                late[n % 4].append(i)

        def ride(host):
            return late[host], [_device_blocks(grads[i // 5][i % 5][1]) for i in late[host]]

        idx, travel = ride(0)
        dg, du, arrived = _mm_nt_swiglu_bwd(dxb, gc, l, g, u, tmr, f"ffn_down_bwd_{l}", travel)
        landed.update(zip(idx, arrived))
        idx, travel = ride(1)
        d_wdown, arrived = _mm_tn(act, dxb, 768, d, tmm, False, f"dw_down_{l}", travel)
        landed.update(zip(idx, arrived))
        idx, travel = ride(2)
        d_wgate, arrived = _mm_tn(h2, dg, d, cols, tmm, True, f"dw_gate_{l}", travel)
        landed.update(zip(idx, arrived))
        idx, travel = ride(3)
        d_wup, arrived = _mm_tn(h2, du, d, cols, tmm, True, f"dw_up_{l}", travel)
        landed.update(zip(idx, arrived))
        dx1, dx1b, dg_ffn, _ = _mm_nt_norm_bwd([dg, du], w_gu[0], list(w_gu[1:]), x1, norm_ffn[l][None], dx, tm,
                                               f"ffn_up_bwd_{l}")
        dmix = _mm_nt(dx1b, gb, l, tmr, 512, f"proj_out_bwd_{l}")
        d_wout, _ = _mm_tn(mix, dx1b, 512, d, tmm, False, f"dw_out_{l}")
        dproj, dconv = _conv_bwd(dmix, proj, conv_w8, f"conv_bwd_{l}")
        rides = {1: d_wgate[1], 2: d_wup[1], 3: d_wout[1], 4: d_wdown[1]} if l == 0 else {}
        dq, dk, dv, arrived = _attn_bwd(qk, proj, dmix, rtot, used, tq, f"attn_bwd_{l}",
                                        [_device_blocks(t) for t in rides.values()])
        landed.update(zip(rides.keys(), arrived))
        dproj, dg_qk = _qknorm_bwd(dq, dk, dv, proj, qk_gain, dproj, tmm, f"qknorm_bwd_{l}")
        d_win, _ = _mm_tn(h1, dproj, d, cols, tmm, True, f"dw_in_{l}")
        dx, dxb, dg_mix, arrived = _mm_nt_norm_bwd(
            [dproj], w_in[0], [w_in[1]], x0, norm_mix[l][None], dx1, tm, f"proj_in_bwd_{l}",
            [_device_blocks(d_win[1])] if l == 0 else [])
        landed.update(zip([0], arrived))
        grads[l] = (d_win, d_wgate, d_wup, d_wout, d_wdown)
        dq_gain = jnp.sum(dg_qk[0, :attn].reshape(nheads, HEAD_DIM), axis=0) * scale
        dk_gain = jnp.sum(dg_qk[0, attn:].reshape(nheads, HEAD_DIM), axis=0)
        small[l] = (dg_mix[0], dg_ffn[0], dq_gain, dk_gain, dconv[:3])
    return loss, dx, grads, small, landed


def kernel(x, norm_mix, w_in, q_norm, k_norm, conv_w, w_out, norm_ffn, w_gate, w_up, w_down, loss_target, m_norm_mix, m_w_in, m_q_norm, m_k_norm, m_conv_w, m_w_out, m_norm_ffn, m_w_gate, m_w_up, m_w_down, v_norm_mix, v_w_in, v_q_norm, v_k_norm, v_conv_w, v_w_out, v_norm_ffn, v_w_gate, v_w_up, v_w_down):
    depth, d, in_shard = w_in.shape
    ff_shard = w_gate.shape[2]
    ff_pad = in_shard
    conv_shard = conv_w.shape[2]
    xs = x.reshape(x.shape[-2], d)
    target = loss_target.reshape(xs.shape)

    pa = jnp.stack([w_in, _pad_to(w_gate, 2, ff_pad), _pad_to(w_up, 2, ff_pad)], axis=1)
    pa = pa.reshape(3 * depth, 1, d, in_shard).astype(BF16)
    pd = _pad_to(_pad_to(conv_w.reshape(depth * 3, conv_shard), 0, 8), 1, LANES)[None, None]
    late_shards = [pa[1:3]] + ([pa[3:]] if depth > 1 else [])
    late_shards += [w_out.astype(BF16)[:, None], _pad_to(w_down, 1, ff_pad).astype(BF16)[:, None]]

    loss, grad_x, grads, small, landed = _local_step(xs, target, [pa[:1], pd], late_shards, conv_shard, norm_mix,
                                                     q_norm, k_norm, norm_ffn)

    x_, y_, c_ = _place()
    my = 4 * x_ + 2 * y_ + c_

    rows = []
    for l in range(depth):
        g_mix, g_ffn, g_q, g_k, g_conv = small[l]
        qkrow = _pad_to(jnp.concatenate([g_q, g_k]), 0, d)
        rows += [g_mix[None], g_ffn[None], qkrow[None], _pad_to(g_conv, 1, d)]
    nrow = 6 * depth
    packed = jnp.concatenate(rows + [_pad_to(loss[:1], 1, d)], axis=0)
    packed = _pad_to(packed, 0, ((nrow + 1 + 7) // 8) * 8)
    summed = _all_reduce_small(packed, "reduce_small")
    loss_out = summed[nrow, 0]

    my1 = my.astype(jnp.int32).reshape(1)

    def big(j, w, m, v, tr, name):
        return _adamw_sharded([landed[5 * l + j] for l in range(depth)],
                              [_device_blocks(grads[l][j][0]) for l in range(depth)], my1, w, m, v, tr, name)

    res = {"w_in": big(0, w_in, m_w_in, v_w_in, 256, "adamw_in"),
           "w_gate": big(1, w_gate, m_w_gate, v_w_gate, 256, "adamw_gate"),
           "w_up": big(2, w_up, m_w_up, v_w_up, 256, "adamw_up"),
           "w_out": big(3, w_out, m_w_out, v_w_out, w_out.shape[1], "adamw_out"),
           "w_down": big(4, w_down, m_w_down, v_w_down, ff_shard // 2, "adamw_down")}

    g_rows, w_rows, m_rows, v_rows = [], [], [], []
    for l in range(depth):
        base = l * 6
        conv_g = lax.dynamic_slice(summed[base + 3:base + 6], (0, my * conv_shard), (3, conv_shard))
        g_rows += [summed[base:base + 3], _pad_to(conv_g, 1, d)]
        for dst, (nm, qn, kn, nf, cw) in ((w_rows, (norm_mix, q_norm, k_norm, norm_ffn, conv_w)),
                                          (m_rows, (m_norm_mix, m_q_norm, m_k_norm, m_norm_ffn, m_conv_w)),
                                          (v_rows, (v_norm_mix, v_q_norm, v_k_norm, v_norm_ffn, v_conv_w))):
            dst += [nm[l][None], nf[l][None], _pad_to(jnp.concatenate([qn[l], kn[l]]), 0, d)[None],
                    _pad_to(cw[l], 1, d)]
    prow = ((nrow + 7) // 8) * 8
    gs, ws, ms, vs = [_pad_to(jnp.concatenate(t, axis=0), 0, prow) for t in (g_rows, w_rows, m_rows, v_rows)]
    sm = _adamw(gs, ws, ms, vs, "adamw_small")

    hd = q_norm.shape[1]

    def small_out(t, kind):
        per_layer = []
        for l in range(depth):
            base = l * 6
            per_layer.append({"norm_mix": t[base], "norm_ffn": t[base + 1], "q_norm": t[base + 2, :hd],
                              "k_norm": t[base + 2, hd:2 * hd], "conv_w": t[base + 3:base + 6, :conv_shard]}[kind])
        return jnp.stack(per_layer)

    def big_out(name, i):
        return res[name][i]

    outs = [loss_out, grad_x.reshape(x.shape)]
    for i in range(4):
        outs += [small_out(sm[i], "norm_mix"), big_out("w_in", i), small_out(sm[i], "q_norm"),
                 small_out(sm[i], "k_norm"), small_out(sm[i], "conv_w"), big_out("w_out", i),
                 small_out(sm[i], "norm_ffn"), big_out("w_gate", i), big_out("w_up", i), big_out("w_down", i)]
    return tuple(outs)
```

```python
import jax
import jax.numpy as jnp
from jax import lax
from jax.experimental import pallas as pl
from jax.experimental.pallas import tpu as pltpu

F32 = jnp.float32
BF16 = jnp.bfloat16
MESH = pl.DeviceIdType.MESH

N_DEV = 8
LANES = 128
HEAD_DIM = 64
KEY_CHUNK = 128
EPS = 1e-6
VMEM_LIMIT = 48 * 1024 * 1024

ADAM_LR = 0.001
ADAM_B1 = 0.9
ADAM_B2 = 0.999
ADAM_EPS = 1e-08
ADAM_WD = 0.01
ADAM_STEP = 10

NN = (((1,), (0,)), ((), ()))
NT = (((1,), (1,)), ((), ()))
TN = (((0,), (0,)), ((), ()))


def _dot(a, b, dims):
    return lax.dot_general(a.astype(BF16), b.astype(BF16), dims, preferred_element_type=F32)


def _cparams(*sem):
    return pltpu.CompilerParams(dimension_semantics=sem, vmem_limit_bytes=VMEM_LIMIT)


def _rmsnorm_fwd(x, gain, tm, name, shards=()):
    s, d = x.shape
    nsteps = s // tm
    ng = len(shards)

    def body(*refs):
        x_ref, g_ref, srcs = refs[0], refs[1], refs[2:2 + ng]
        o_ref, dsts, sems = refs[2 + ng], refs[3 + ng:3 + 2 * ng], refs[3 + 2 * ng:]
        i = pl.program_id(0)
        gather = _Gather(srcs, dsts, *sems) if ng else None
        if ng:
            @pl.when(i == 0)
            def _():
                gather.begin()

        xv = x_ref[...]
        r = lax.rsqrt(jnp.mean(xv * xv, axis=-1, keepdims=True) + EPS)
        o_ref[...] = ((xv * r) * g_ref[...]).astype(o_ref.dtype)
        if ng:
            @pl.when(i == nsteps - 1)
            def _():
                gather.relay()
                gather.finish()

    anyspec = pl.BlockSpec(memory_space=pl.ANY)
    outs = pl.pallas_call(
        body, name=name, grid=(nsteps,),
        in_specs=[pl.BlockSpec((tm, d), lambda i: (i, 0)), pl.BlockSpec((1, d), lambda i: (0, 0))] + [anyspec] * ng,
        out_specs=[pl.BlockSpec((tm, d), lambda i: (i, 0))] + [anyspec] * ng,
        out_shape=[jax.ShapeDtypeStruct((s, d), BF16)] + _gathered_shapes(shards),
        scratch_shapes=_gather_scratch(ng) if ng else [],
        compiler_params=_cparams("arbitrary"),
    )(x, gain, *shards)
    return outs[0], list(outs[1:])


def _group_mean_matrix():
    r = lax.broadcasted_iota(jnp.int32, (LANES, LANES), 0) // HEAD_DIM
    c = lax.broadcasted_iota(jnp.int32, (LANES, LANES), 1) // HEAD_DIM
    return jnp.where(r == c, 1.0 / HEAD_DIM, 0.0).astype(BF16)


def _group_mean(v, gm):
    hi = v.astype(BF16)
    lo = (v - hi.astype(F32)).astype(BF16)
    return _dot(hi, gm, NN) + _dot(lo, gm, NN)


def _qknorm_fwd(proj, gains, tm, name):
    s = proj.shape[0]
    ncol = gains.shape[1] // LANES

    def body(p_ref, g_ref, gm_ref, o_ref):
        xv = p_ref[...].astype(F32)
        r = lax.rsqrt(_group_mean(xv * xv, gm_ref[...]) + EPS)
        o_ref[...] = ((xv * r) * g_ref[...]).astype(o_ref.dtype)

    blk = pl.BlockSpec((tm, LANES), lambda i, j: (i, j))
    return pl.pallas_call(
        body, name=name, grid=(s // tm, ncol),
        in_specs=[blk, pl.BlockSpec((1, LANES), lambda i, j: (0, j)),
                  pl.BlockSpec((LANES, LANES), lambda i, j: (0, 0))],
        out_specs=blk,
        out_shape=jax.ShapeDtypeStruct((s, ncol * LANES), BF16),
        compiler_params=_cparams("parallel", "parallel"),
    )(proj, gains, _group_mean_matrix())


def _qknorm_bwd(dq, dk, proj, gains, dproj, tm, name):
    s = proj.shape[0]
    ncol = gains.shape[1] // LANES
    half = ncol // 2
    nsteps = s // tm

    def body(dq_ref, dk_ref, p_ref, g_ref, gm_ref, dproj_ref, dx_ref, dg_ref):
        del dproj_ref
        i = pl.program_id(1)
        gm = gm_ref[...]
        xv = p_ref[...].astype(F32)
        r = lax.rsqrt(_group_mean(xv * xv, gm) + EPS)
        xhat = xv * r
        dy = jnp.where(pl.program_id(0) < half, dq_ref[...], dk_ref[...])
        dxh = dy * g_ref[...]
        proj_ = _group_mean(dxh * xhat, gm)
        dx_ref[...] = (r * (dxh - xhat * proj_)).astype(dx_ref.dtype)
        part = jnp.sum((dy * xhat).reshape(tm // 8, 8, LANES), axis=0)

        @pl.when(i == 0)
        def _():
            dg_ref[...] = part

        @pl.when(i > 0)
        def _():
            dg_ref[...] += part

        @pl.when(i == nsteps - 1)
        def _():
            dg_ref[...] = jnp.broadcast_to(jnp.sum(dg_ref[...], axis=0, keepdims=True), (8, LANES))

    blk = pl.BlockSpec((tm, LANES), lambda j, i: (i, j))
    return pl.pallas_call(
        body, name=name, grid=(ncol, nsteps),
        in_specs=[pl.BlockSpec((tm, LANES), lambda j, i: (i, jnp.minimum(j, half - 1))),
                  pl.BlockSpec((tm, LANES), lambda j, i: (i, jnp.maximum(j - half, 0))),
                  blk, pl.BlockSpec((1, LANES), lambda j, i: (0, j)),
                  pl.BlockSpec((LANES, LANES), lambda j, i: (0, 0)),
                  pl.BlockSpec(memory_space=pl.ANY)],
        out_specs=[blk, pl.BlockSpec((8, LANES), lambda j, i: (0, j))],
        out_shape=[jax.ShapeDtypeStruct(dproj.shape, BF16),
                   jax.ShapeDtypeStruct((8, ncol * LANES), F32)],
        input_output_aliases={5: 0},
        compiler_params=_cparams("parallel", "arbitrary"),
    )(dq, dk, proj, gains, _group_mean_matrix(), dproj)


CONV_ROWS = 256
HALO = 8


def _conv_fwd(proj, conv_w8, mix, name):
    s = proj.shape[0]
    nblk = conv_w8.shape[1] // LANES
    first = 3 * nblk
    nchunk = s // CONV_ROWS
    before = mix.shape[1] // LANES - nblk

    def body(cb_ref, cc_ref, cu_ref, w_ref, mix_ref, y_ref, hpad):
        del mix_ref
        hpad[pl.ds(0, 2 * HALO), :] = jnp.zeros((2 * HALO, LANES), F32)

        def fill(i, _):
            r0 = pl.multiple_of(i * CONV_ROWS, CONV_ROWS)
            hpad[pl.ds(r0 + 2 * HALO, CONV_ROWS), :] = (
                cc_ref[pl.ds(r0, CONV_ROWS), :].astype(F32) * cu_ref[pl.ds(r0, CONV_ROWS), :].astype(F32))
            return 0

        lax.fori_loop(0, nchunk, fill, 0)
        w0, w1, w2 = w_ref[0:1, :], w_ref[1:2, :], w_ref[2:3, :]

        def conv(i, _):
            r0 = pl.multiple_of(i * CONV_ROWS, CONV_ROWS)
            win = hpad[pl.ds(r0 + HALO, CONV_ROWS + HALO), :]
            c = (w2 * win[HALO:] + w1 * pltpu.roll(win, 1, 0)[HALO:] + w0 * pltpu.roll(win, 2, 0)[HALO:])
            y_ref[pl.ds(r0, CONV_ROWS), :] = (cb_ref[pl.ds(r0, CONV_ROWS), :].astype(F32) * c).astype(y_ref.dtype)
            return 0

        lax.fori_loop(0, nchunk, conv, 0)

    def col(off):
        return pl.BlockSpec((s, LANES), lambda j: (0, off + j))

    return pl.pallas_call(
        body, name=name, grid=(nblk,),
        in_specs=[col(first), col(first + nblk), col(first + 2 * nblk), pl.BlockSpec((8, LANES), lambda j: (0, j)),
                  pl.BlockSpec(memory_space=pl.ANY)],
        out_specs=pl.BlockSpec((s, LANES), lambda j: (0, before + j)),
        out_shape=jax.ShapeDtypeStruct(mix.shape, mix.dtype),
        scratch_shapes=[pltpu.VMEM((s + 2 * HALO, LANES), F32)],
        input_output_aliases={4: 0},
        compiler_params=_cparams("parallel"),
    )(proj, proj, proj, conv_w8, mix)


def _conv_bwd(dmix, proj, conv_w8, name):
    s = proj.shape[0]
    nblk = conv_w8.shape[1] // LANES
    first = 3 * nblk
    nchunk = s // CONV_ROWS

    def body(dy_ref, cb_ref, cc_ref, cu_ref, w_ref, dproj_ref, dw_ref, hpad, dcpad, dcb_ref, dcc_ref, dcu_ref, sems):
        j = pl.program_id(0)

        def writes(jj):
            return [pltpu.make_async_copy(
                buf, dproj_ref.at[:, pl.ds(pl.multiple_of((first + k * nblk + jj) * LANES, LANES), LANES)], sems.at[k])
                for k, buf in enumerate((dcb_ref, dcc_ref, dcu_ref))]

        @pl.when(j > 0)
        def _():
            for w in writes(j - 1):
                w.wait()

        hpad[pl.ds(0, 2 * HALO), :] = jnp.zeros((2 * HALO, LANES), F32)
        dcpad[pl.ds(s, 2 * HALO), :] = jnp.zeros((2 * HALO, LANES), F32)

        def fill(i, _):
            r0 = pl.multiple_of(i * CONV_ROWS, CONV_ROWS)
            hpad[pl.ds(r0 + 2 * HALO, CONV_ROWS), :] = (
                cc_ref[pl.ds(r0, CONV_ROWS), :].astype(F32) * cu_ref[pl.ds(r0, CONV_ROWS), :].astype(F32))
            return 0

        lax.fori_loop(0, nchunk, fill, 0)
        w0, w1, w2 = w_ref[0:1, :], w_ref[1:2, :], w_ref[2:3, :]

        def fold(v):
            return jnp.sum(v.reshape(CONV_ROWS // 8, 8, LANES), axis=0)

        def first_pass(i, acc):
            a0, a1, a2 = acc
            r0 = pl.multiple_of(i * CONV_ROWS, CONV_ROWS)
            win = hpad[pl.ds(r0 + HALO, CONV_ROWS + HALO), :]
            h0 = win[HALO:]
            h1 = pltpu.roll(win, 1, 0)[HALO:]
            h2 = pltpu.roll(win, 2, 0)[HALO:]
            c = w2 * h0 + w1 * h1 + w0 * h2
            dy = dy_ref[pl.ds(r0, CONV_ROWS), :]
            dcb_ref[pl.ds(r0, CONV_ROWS), :] = (dy * c).astype(dcb_ref.dtype)
            dc = dy * cb_ref[pl.ds(r0, CONV_ROWS), :].astype(F32)
            dcpad[pl.ds(r0, CONV_ROWS), :] = dc
            return a0 + fold(dc * h2), a1 + fold(dc * h1), a2 + fold(dc * h0)

        z8 = jnp.zeros((8, LANES), F32)
        a0, a1, a2 = lax.fori_loop(0, nchunk, first_pass, (z8, z8, z8))
        dw_ref[...] = jnp.concatenate(
            [jnp.sum(a0, axis=0, keepdims=True), jnp.sum(a1, axis=0, keepdims=True),
             jnp.sum(a2, axis=0, keepdims=True), jnp.zeros((5, LANES), F32)], axis=0)

        def second_pass(i, _):
            r0 = pl.multiple_of(i * CONV_ROWS, CONV_ROWS)
            win = dcpad[pl.ds(r0, CONV_ROWS + HALO), :]
            n = CONV_ROWS + HALO
            dh = (w2 * win[:CONV_ROWS] + w1 * pltpu.roll(win, n - 1, 0)[:CONV_ROWS]
                  + w0 * pltpu.roll(win, n - 2, 0)[:CONV_ROWS])
            dcc_ref[pl.ds(r0, CONV_ROWS), :] = (dh * cu_ref[pl.ds(r0, CONV_ROWS), :].astype(F32)).astype(dcc_ref.dtype)
            dcu_ref[pl.ds(r0, CONV_ROWS), :] = (dh * cc_ref[pl.ds(r0, CONV_ROWS), :].astype(F32)).astype(dcu_ref.dtype)
            return 0

        lax.fori_loop(0, nchunk, second_pass, 0)
        for w in writes(j):
            w.start()

        @pl.when(j == nblk - 1)
        def _():
            for w in writes(j):
                w.wait()

    def col(off):
        return pl.BlockSpec((s, LANES), lambda j: (0, off + j))

    stage = pltpu.VMEM((s, LANES), BF16)
    return pl.pallas_call(
        body, name=name, grid=(nblk,),
        in_specs=[col(nblk), col(first), col(first + nblk), col(first + 2 * nblk),
                  pl.BlockSpec((8, LANES), lambda j: (0, j))],
        out_specs=[pl.BlockSpec(memory_space=pl.ANY), pl.BlockSpec((8, LANES), lambda j: (0, j))],
        out_shape=[jax.ShapeDtypeStruct(proj.shape, BF16), jax.ShapeDtypeStruct((8, nblk * LANES), F32)],
        scratch_shapes=[pltpu.VMEM((s + 2 * HALO, LANES), F32), pltpu.VMEM((s + 2 * HALO, LANES), F32),
                        stage, stage, stage, pltpu.SemaphoreType.DMA((3,))],
        compiler_params=_cparams("arbitrary"),
    )(dmix, proj, proj, proj, conv_w8)


LOG2E = 1.4426950408889634
LN2 = 0.6931471805599453
NEG_BIG = -1e30
SATURATED = 160.0


def _cumsum_matrix(kind):
    j = lax.broadcasted_iota(jnp.int32, (KEY_CHUNK, 2 * KEY_CHUNK), 0)
    c = lax.broadcasted_iota(jnp.int32, (KEY_CHUNK, 2 * KEY_CHUNK), 1)
    tri = {"after": j > c, "upto": j <= c, "before": j < c}[kind]
    return jnp.where((c >= KEY_CHUNK) | tri, 1.0, 0.0).astype(BF16)


def _stack_heads(t, m0):
    zero = jnp.zeros_like(t)
    return jnp.concatenate([jnp.where(m0, t, zero), jnp.where(m0, zero, t)], axis=0)


def _softplus2(z):
    sp = jnp.maximum(z, 0.0) + jnp.log2(1.0 + jnp.exp2(-jnp.abs(z)))
    return sp, z - sp


def _key_chunk(ref, kc):
    return ref[pl.ds(pl.multiple_of(kc * KEY_CHUNK, KEY_CHUNK), KEY_CHUNK), :]


def _attn_bwd(qk, proj, dmix, rtot, used, dproj, tq, name, travel=()):
    s = qk.shape[0]
    nhp = qk.shape[1] // (2 * LANES)
    nc = tq // KEY_CHUNK
    nq = s // tq
    nt = len(travel)

    def body(used_ref, q_ref, k_ref, v_ref, do_ref, r_ref, cmi_ref, cme_ref, bias_ref, dq_ref, dk_ref, dv_ref,
             z_refs, ls_refs, sig_refs, sp_refs, gb_refs, pr_ref, gs_ref, dv_acc, copies):
        qi = pl.program_id(1)

        @pl.when(qi == 0)
        def _():
            dk_ref[...] = jnp.zeros_like(dk_ref)
            dv_acc[...] = jnp.zeros_like(dv_acc)

        if copies is not None:
            @pl.when(jnp.logical_and(pl.program_id(0) == 0, qi == 0))
            def _():
                _exchange_begin(copies)

        nslots = (qi + 1) * nc
        walked = used_ref[pl.program_id(0), qi].astype(jnp.int32)
        first = jnp.clip(nslots - walked, 0, nslots - nc) // nc * nc
        m0 = lax.broadcasted_iota(jnp.int32, (1, LANES), 1) < HEAD_DIM
        qs = _stack_heads(q_ref[...], m0)
        do = do_ref[...]
        dos = _stack_heads(do.astype(BF16), m0)
        dosl = _stack_heads((do * LN2).astype(BF16), m0)
        cmi = cmi_ref[...]
        cme = cme_ref[...]

        def chunk_at(i):
            return jnp.clip(i, first, nslots - 1)

        def scores(kc):
            return _dot(qs, _key_chunk(k_ref, kc), NT)

        def weights(ls, cs, da, pr, kc):
            a = jnp.exp2(ls - (pr - cs[:, :KEY_CHUNK]))
            gb = (a * da).astype(BF16)
            ks = pl.multiple_of(kc * KEY_CHUNK, KEY_CHUNK)
            dv_acc[pl.ds(ks, KEY_CHUNK), :] += _dot(a, dos, TN)
            return gb, jnp.exp2(ls), pr - cs[:, KEY_CHUNK:]

        def score_grads(gb, sig, cg, gs, dq, kc):
            dzb = (gb.astype(F32) * (1.0 - sig) - sig * (gs + cg[:, :KEY_CHUNK])).astype(BF16)
            ks = pl.multiple_of(kc * KEY_CHUNK, KEY_CHUNK)
            dk_ref[pl.ds(ks, KEY_CHUNK), :] += _dot(dzb, qs, TN)
            dq = dq + _dot(jnp.concatenate([dzb[:tq], dzb[tq:]], axis=1), _stack_heads(_key_chunk(k_ref, kc), m0), NN)
            return gs + cg[:, KEY_CHUNK:], dq

        def step(i, par, bias=None, stages="zswg"):
            cur, prv = par, 1 - par
            k1, k2 = chunk_at(i - 1), chunk_at(i - 2)
            z_next = scores(chunk_at(i + 1))
            if "w" in stages:
                cs = _dot(sp_refs[prv][...], cmi, NN)
                da = _dot(dosl, _key_chunk(v_ref, k1), NT)
            if "g" in stages:
                cg = _dot(gb_refs[cur][...], cme, NN)
            z = z_refs[cur][...]
            if bias is not None:
                z = z + bias
            sp, ls = _softplus2(z)
            sp_refs[cur][...] = sp.astype(BF16)
            ls_refs[cur][...] = ls
            if "g" in stages:
                gs, dq = score_grads(gb_refs[cur][...], sig_refs[cur][...], cg, gs_ref[...], dq_ref[...], k2)
                gs_ref[...] = gs
                dq_ref[...] = dq
            if "w" in stages:
                gb, sig, pr = weights(ls_refs[prv][...], cs, da, pr_ref[...], k1)
                gb_refs[prv][...] = gb
                sig_refs[prv][...] = sig
                pr_ref[...] = pr
            z_refs[prv][...] = z_next

        pr_ref[...] = jnp.concatenate([r_ref[:, :LANES], r_ref[:, LANES:]], axis=0)
        gs_ref[...] = jnp.zeros((2 * tq, LANES), F32)
        dq_ref[...] = jnp.zeros((tq, LANES), F32)
        z_refs[0][...] = scores(first)
        only_diagonal = first == nslots - nc
        step(first, 0, jnp.where(only_diagonal, bias_ref[0], 0.0), stages="zs")
        step(first + 1, 1, jnp.where(only_diagonal, bias_ref[1], 0.0), stages="zsw")

        def two_steps(j, _):
            step(2 * j, 0)
            step(2 * j + 1, 1)
            return 0

        lax.fori_loop(first // 2 + 1, nslots // 2 - 1, two_steps, 0)

        @pl.when(jnp.logical_not(only_diagonal))
        def _():
            step(nslots - 2, 0, bias_ref[0])
            step(nslots - 1, 1, bias_ref[1])

        k1, k2 = chunk_at(nslots - 1), chunk_at(nslots - 2)
        gb, sig, _ = weights(ls_refs[1][...], _dot(sp_refs[1][...], cmi, NN),
                             _dot(dosl, _key_chunk(v_ref, k1), NT), pr_ref[...], k1)
        gb2 = gb_refs[0][...]
        gs, dq = score_grads(gb2, sig_refs[0][...], _dot(gb2, cme, NN), gs_ref[...], dq_ref[...], k2)
        _, dq = score_grads(gb, sig, _dot(gb, cme, NN), gs, dq, k1)
        dq_ref[...] = dq

        @pl.when(qi == nq - 1)
        def _():
            dv_ref[...] = dv_acc[...].astype(dv_ref.dtype)

        if copies is not None:
            @pl.when(jnp.logical_and(pl.program_id(0) == nhp - 1, qi == nq - 1))
            def _():
                _exchange_finish(copies)

    def wrapped(*refs):
        ins, rest = refs[:9], refs[9:]
        srcs, rest = rest[:nt], rest[nt + 1:]
        outs, rest = rest[:3], rest[3:]
        lands, rest = rest[:nt], rest[nt:]
        z0, z1, ls0, ls1, sg0, sg1, sp0, sp1, gb0, gb1, pr_ref, gs_ref, dv_acc = rest[:13]
        copies = _exchange_copies(srcs, lands, *rest[13:]) if nt else None
        body(*ins, *outs, (z0, z1), (ls0, ls1), (sg0, sg1), (sp0, sp1), (gb0, gb1), pr_ref, gs_ref, dv_acc, copies)

    assert nc == 2
    bias = _diag_bias(tq, True)
    bias = jnp.concatenate([bias[:, :, :KEY_CHUNK], bias[:, :, KEY_CHUNK:]], axis=1)
    qblk = pl.BlockSpec((tq, LANES), lambda p, i: (i, p))
    full = pl.BlockSpec((s, LANES), lambda p, i: (0, p))
    cmspec = pl.BlockSpec((KEY_CHUNK, 2 * KEY_CHUNK), lambda p, i: (0, 0))
    anyspec = pl.BlockSpec(memory_space=pl.ANY)
    shape = jax.ShapeDtypeStruct((s, nhp * LANES), F32)
    f32buf = pltpu.VMEM((2 * tq, LANES), F32)
    bf16buf = pltpu.VMEM((2 * tq, LANES), BF16)
    outs = pl.pallas_call(
        wrapped, name=name, grid=(nhp, nq),
        in_specs=[pl.BlockSpec(memory_space=pltpu.SMEM),
                  qblk,
                  pl.BlockSpec((s, LANES), lambda p, i: (0, nhp + p)),
                  pl.BlockSpec((s, LANES), lambda p, i: (0, 2 * nhp + p)),
                  qblk,
                  pl.BlockSpec((tq, 2 * LANES), lambda p, i: (i, p)),
                  cmspec, cmspec,
                  pl.BlockSpec((nc, 2 * tq, LANES), lambda p, i: (0, 0, 0))] + [anyspec] * (nt + 1),
        out_specs=[qblk, full, pl.BlockSpec((s, LANES), lambda p, i: (0, 2 * nhp + p))] + [anyspec] * nt,
        out_shape=[shape, shape, jax.ShapeDtypeStruct(dproj.shape, dproj.dtype)]
        + [jax.ShapeDtypeStruct(t.shape, t.dtype) for t in travel],
        scratch_shapes=[f32buf] * 6 + [bf16buf] * 4 + [f32buf] * 2 + [pltpu.VMEM((s, LANES), F32)]
        + (_exchange_scratch(nt) if nt else []),
        input_output_aliases={9 + nt: 2},
        compiler_params=_cparams("arbitrary", "arbitrary"),
    )(used, qk, qk, proj, dmix, rtot, _cumsum_matrix("upto"), _cumsum_matrix("before"), bias, *travel, dproj)
    return outs[0], outs[1], outs[2], list(outs[3:])


def _pair_cumsum_matrix(kind):
    j = lax.broadcasted_iota(jnp.int32, (2 * KEY_CHUNK, 4 * KEY_CHUNK), 0)
    c = lax.broadcasted_iota(jnp.int32, (2 * KEY_CHUNK, 4 * KEY_CHUNK), 1)
    same_head = (j // KEY_CHUNK) == ((c // KEY_CHUNK) % 2)
    jj, cc = j % KEY_CHUNK, c % KEY_CHUNK
    tri = {"after": jj > cc, "upto": jj <= cc, "before": jj < cc}[kind]
    return jnp.where(same_head & ((c >= 2 * KEY_CHUNK) | tri), 1.0, 0.0).astype(BF16)


def _diag_bias(tq, ascending):
    nc = tq // KEY_CHUNK
    shape = (nc, tq, 2 * KEY_CHUNK)
    d = lax.broadcasted_iota(jnp.int32, shape, 0)
    r = lax.broadcasted_iota(jnp.int32, shape, 1)
    c = lax.broadcasted_iota(jnp.int32, shape, 2) % KEY_CHUNK
    chunk = d if ascending else nc - 1 - d
    return jnp.where(chunk * KEY_CHUNK + c < r, 0.0, NEG_BIG).astype(F32)


def _attn_fwd(qk, proj, tq, name, mix_cols, shards=()):
    s = qk.shape[0]
    nhp = qk.shape[1] // (2 * LANES)
    nc = tq // KEY_CHUNK
    nq = s // tq
    ng = len(shards)
    assert nc == 2
    w = 2 * KEY_CHUNK

    def body(q_ref, k_ref, v_ref, cm_ref, bias_ref, o_ref, r_ref, used_ref, z_refs, ls_refs, cs_refs, ct_refs,
             sp_refs, ab_refs, acc_ref, gather):
        qi = pl.program_id(1)
        if gather is not None:
            @pl.when(jnp.logical_and(pl.program_id(0) == 0, qi == 0))
            def _():
                gather.begin()

        nslots = (qi + 1) * nc
        m0 = lax.broadcasted_iota(jnp.int32, (1, LANES), 1) < HEAD_DIM
        q = q_ref[...]
        cm = cm_ref[...]

        def chunk_at(i):
            return jnp.clip(nslots - 1 - i, 0, nslots - 1)

        def scores(kc):
            return _dot(q, _stack_heads(_key_chunk(k_ref, kc), m0), NT)

        def values(ab, kc):
            return _dot(ab, _stack_heads(_key_chunk(v_ref, kc), m0), NN)

        def step(i, par, bias=None, stages="zscwv"):
            cur, prv = par, 1 - par
            if "z" in stages:
                z_next = scores(chunk_at(i + 1))
            if "c" in stages:
                cs = _dot(sp_refs[prv][...], cm, NN)
            if "v" in stages:
                pv = values(ab_refs[prv][...], chunk_at(i - 3))
            if "w" in stages:
                rs = r_ref[...]
                r_ref[...] = rs + ct_refs[cur][...]
                ab_refs[cur][...] = jnp.exp2(ls_refs[cur][...] - cs_refs[cur][...] - rs).astype(BF16)
            if "s" in stages:
                z = z_refs[cur][...]
                if bias is not None:
                    z = z + bias
                sp, ls = _softplus2(z)
                sp_refs[cur][...] = sp.astype(BF16)
                ls_refs[cur][...] = ls
            if "v" in stages:
                acc_ref[...] += pv
            if "c" in stages:
                cs_refs[prv][...] = cs[:, :w]
                ct_refs[prv][...] = cs[:, w:]
            if "z" in stages:
                z_refs[prv][...] = z_next

        z_refs[0][...] = scores(chunk_at(0))
        ab_refs[1][...] = jnp.zeros((tq, w), BF16)
        r_ref[...] = jnp.zeros((tq, w), F32)
        acc_ref[...] = jnp.zeros((tq, LANES), F32)
        step(0, 0, bias_ref[0], stages="zs")
        step(1, 1, bias_ref[1], stages="zsc")

        def two_steps(carry):
            j, _ = carry
            step(2 * j, 0)
            step(2 * j + 1, 1)
            return j + 1, jnp.min(jnp.minimum(r_ref[:, :KEY_CHUNK], r_ref[:, KEY_CHUNK:]))

        pairs, low = lax.while_loop(lambda c: jnp.logical_and(c[0] < nslots // 2, c[1] < SATURATED), two_steps,
                                    (jnp.int32(1), jnp.float32(0.0)))
        entered = 2 * pairs
        saturated = low >= SATURATED

        @pl.when(saturated)
        def _():
            step(entered, 0, stages="v")

        @pl.when(jnp.logical_not(saturated))
        def _():
            step(entered, 0, stages="cwv")
            step(entered + 1, 1, stages="wv")
            step(entered + 2, 0, stages="v")

        o_ref[...] = acc_ref[...].astype(o_ref.dtype)
        used_ref[pl.program_id(0), qi] = jnp.where(saturated, entered - 2, entered).astype(F32)

        if gather is not None:
            @pl.when(jnp.logical_and(pl.program_id(0) == nhp - 1, qi == nq // 2))
            def _():
                gather.relay()

            @pl.when(jnp.logical_and(pl.program_id(0) == nhp - 1, qi == nq - 1))
            def _():
                gather.finish()

    def wrapped(*refs):
        ins, rest = refs[:5], refs[5:]
        srcs, rest = rest[:ng], rest[ng:]
        outs, rest = rest[:3], rest[3:]
        dsts, scratch = rest[:ng], rest[ng:]
        z, ls, cs, ct, sp, ab = [scratch[2 * j:2 * j + 2] for j in range(6)]
        gather = _Gather(srcs, dsts, *scratch[13:]) if ng else None
        body(*ins, *outs, z, ls, cs, ct, sp, ab, scratch[12], gather)

    f32buf = pltpu.VMEM((tq, w), F32)
    bf16buf = pltpu.VMEM((tq, w), BF16)
    anyspec = pl.BlockSpec(memory_space=pl.ANY)
    outs = pl.pallas_call(
        wrapped, name=name, grid=(nhp, nq),
        in_specs=[pl.BlockSpec((tq, LANES), lambda p, i: (i, p)),
                  pl.BlockSpec((s, LANES), lambda p, i: (0, nhp + p)),
                  pl.BlockSpec((s, LANES), lambda p, i: (0, 2 * nhp + p)),
                  pl.BlockSpec((w, 2 * w), lambda p, i: (0, 0)),
                  pl.BlockSpec((nc, tq, w), lambda p, i: (0, 0, 0))] + [anyspec] * ng,
        out_specs=[pl.BlockSpec((tq, LANES), lambda p, i: (i, p)),
                   pl.BlockSpec((tq, w), lambda p, i: (i, p)),
                   pl.BlockSpec(memory_space=pltpu.SMEM)] + [anyspec] * ng,
        out_shape=[jax.ShapeDtypeStruct((s, mix_cols), BF16),
                   jax.ShapeDtypeStruct((s, nhp * w), F32),
                   jax.ShapeDtypeStruct((nhp, nq), F32)] + _gathered_shapes(shards),
        scratch_shapes=([f32buf] * 8 + [bf16buf] * 4 + [pltpu.VMEM((tq, LANES), F32)]
                        + (_gather_scratch(ng) if ng else [])),
        compiler_params=_cparams("arbitrary", "arbitrary"),
    )(qk, qk, proj, _pair_cumsum_matrix("after"), _diag_bias(tq, False), *shards)
    return outs[0], outs[1], outs[2], list(outs[3:])


BLOCK_PAIR = 2
MXU_WIDTH = 256


def _side_by_side(b_ref):
    return jnp.concatenate([b_ref[p] for p in range(BLOCK_PAIR)], axis=1)


def _mm_blocks(h, ga, widx, tm, name):
    s, d = h.shape
    nb, cols = ga.shape[1], ga.shape[3]

    def body(a_ref, b_ref, o_ref):
        o_ref[...] = _dot(a_ref[...], _side_by_side(b_ref), NN).astype(o_ref.dtype)

    return pl.pallas_call(
        body, name=name, grid=(s // tm, nb // BLOCK_PAIR),
        in_specs=[pl.BlockSpec((tm, d), lambda i, j: (i, 0)),
                  pl.BlockSpec((None, BLOCK_PAIR, d, cols), lambda i, j: (widx, j, 0, 0))],
        out_specs=pl.BlockSpec((tm, BLOCK_PAIR * cols), lambda i, j: (i, j)),
        out_shape=jax.ShapeDtypeStruct((s, nb * cols), BF16),
        compiler_params=_cparams("parallel", "arbitrary"),
    )(h, ga)


def _mm_swiglu(h, ga, gidx, uidx, tm, name):
    s, d = h.shape
    nb, cols = ga.shape[1], ga.shape[3]

    def body(a_ref, bg_ref, bu_ref, g_ref, u_ref, act_ref):
        a = a_ref[...]
        g = _dot(a, _side_by_side(bg_ref), NN)
        u = _dot(a, _side_by_side(bu_ref), NN)
        g_ref[...] = g.astype(g_ref.dtype)
        u_ref[...] = u.astype(u_ref.dtype)
        act_ref[...] = (g * (1.0 / (1.0 + jnp.exp(-g))) * u).astype(act_ref.dtype)

    def wspec(idx):
        return pl.BlockSpec((None, BLOCK_PAIR, d, cols), lambda i, j: (idx, j, 0, 0))

    out = pl.BlockSpec((tm, BLOCK_PAIR * cols), lambda i, j: (i, j))
    shape = jax.ShapeDtypeStruct((s, nb * cols), BF16)
    return pl.pallas_call(
        body, name=name, grid=(s // tm, nb // BLOCK_PAIR),
        in_specs=[pl.BlockSpec((tm, d), lambda i, j: (i, 0)), wspec(gidx), wspec(uidx)],
        out_specs=[out, out, out], out_shape=[shape, shape, shape],
        compiler_params=_cparams("parallel", "arbitrary"),
    )(h, ga, ga)


def _mm_residual_norm(a, w3, lidx, res, gain, tm, name):
    s, k = a.shape
    n = w3.shape[2]

    def body(a_ref, b_ref, r_ref, g_ref, o_ref, h_ref):
        xv = r_ref[...] + _dot(a_ref[...], b_ref[...], NN)
        o_ref[...] = xv
        r = lax.rsqrt(jnp.mean(xv * xv, axis=-1, keepdims=True) + EPS)
        h_ref[...] = ((xv * r) * g_ref[...]).astype(h_ref.dtype)

    row = pl.BlockSpec((tm, n), lambda i: (i, 0))
    return pl.pallas_call(
        body, name=name, grid=(s // tm,),
        in_specs=[pl.BlockSpec((tm, k), lambda i: (i, 0)),
                  pl.BlockSpec((None, k, n), lambda i: (lidx, 0, 0), pipeline_mode=pl.Buffered(1)),
                  row, pl.BlockSpec((1, n), lambda i: (0, 0))],
        out_specs=[row, row],
        out_shape=[jax.ShapeDtypeStruct((s, n), F32), jax.ShapeDtypeStruct((s, n), BF16)],
        compiler_params=_cparams("parallel"),
    )(a, w3, res, gain)


def _mm_residual_loss(a, w3, lidx, res, target, tm, name):
    s, k = a.shape
    n = w3.shape[2]
    nsteps = s // tm

    def body(a_ref, b_ref, r_ref, t_ref, dy_ref, dyb_ref, l_ref, acc):
        i = pl.program_id(0)
        diff = r_ref[...] + _dot(a_ref[...], b_ref[...], NN) - t_ref[...]
        dy_ref[...] = diff * (1.0 / n)
        dyb_ref[...] = (diff * (1.0 / n)).astype(dyb_ref.dtype)
        part = jnp.sum((diff * diff).reshape(tm // 8, 8, n), axis=0)

        @pl.when(i == 0)
        def _():
            acc[...] = part

        @pl.when(i > 0)
        def _():
            acc[...] += part

        @pl.when(i == nsteps - 1)
        def _():
            tot = jnp.sum(jnp.sum(acc[...], axis=1, keepdims=True), axis=0, keepdims=True)
            l_ref[...] = jnp.broadcast_to(tot * (0.5 / n), (8, LANES))

    row = pl.BlockSpec((tm, n), lambda i: (i, 0))
    return pl.pallas_call(
        body, name=name, grid=(nsteps,),
        in_specs=[pl.BlockSpec((tm, k), lambda i: (i, 0)),
                  pl.BlockSpec((None, k, n), lambda i: (lidx, 0, 0), pipeline_mode=pl.Buffered(1)),
                  row, row],
        out_specs=[row, row, pl.BlockSpec((8, LANES), lambda i: (0, 0))],
        out_shape=[jax.ShapeDtypeStruct((s, n), F32), jax.ShapeDtypeStruct((s, n), BF16),
                   jax.ShapeDtypeStruct((8, LANES), F32)],
        scratch_shapes=[pltpu.VMEM((8, n), F32)],
        compiler_params=_cparams("arbitrary"),
    )(a, w3, res, target)


def _mm_nt(a, w3, lidx, tm, tn, name):
    s, k = a.shape
    n = w3.shape[1]

    def body(a_ref, b_ref, o_ref):
        o_ref[...] = _dot(a_ref[...], b_ref[...], NT)

    return pl.pallas_call(
        body, name=name, grid=(s // tm, n // tn),
        in_specs=[pl.BlockSpec((tm, k), lambda i, j: (i, 0)),
                  pl.BlockSpec((None, tn, k), lambda i, j: (lidx, j, 0))],
        out_specs=pl.BlockSpec((tm, tn), lambda i, j: (i, j)),
        out_shape=jax.ShapeDtypeStruct((s, n), F32),
        compiler_params=_cparams("parallel", "arbitrary"),
    )(a, w3)


def _mm_nt_swiglu_bwd(dx, wd3, lidx, g, u, tm, name, travel=()):
    s, d = dx.shape
    cols = BLOCK_PAIR * (g.shape[1] // N_DEV)

    def body(a_ref, b_ref, g_ref, u_ref, dg_ref, du_ref):
        a = a_ref[...]
        for c0 in range(0, cols, MXU_WIDTH):
            sl = slice(c0, c0 + MXU_WIDTH)
            dact = _dot(a, b_ref[sl, :], NT)
            gv = g_ref[:, sl].astype(F32)
            sig = 0.5 * jnp.tanh(0.5 * gv) + 0.5
            silu = gv * sig
            du_ref[:, sl] = (dact * silu).astype(du_ref.dtype)
            dsilu = sig + silu * (1.0 - sig)
            dg_ref[:, sl] = (dact * u_ref[:, sl].astype(F32) * dsilu).astype(dg_ref.dtype)

    blk = pl.BlockSpec((tm, cols), lambda i, j: (i, j))
    shape = jax.ShapeDtypeStruct(g.shape, BF16)
    grid = (s // tm, N_DEV // BLOCK_PAIR)
    body, more_in, more_out, more_shapes, more_scratch = _host_exchange(body, 4, 2, travel, grid)
    outs = pl.pallas_call(
        body, name=name, grid=grid,
        in_specs=[pl.BlockSpec((tm, d), lambda i, j: (i, 0)),
                  pl.BlockSpec((None, cols, d), lambda i, j: (lidx, j, 0)), blk, blk] + more_in,
        out_specs=[blk, blk] + more_out, out_shape=[shape, shape] + more_shapes,
        scratch_shapes=more_scratch,
        compiler_params=_cparams("arbitrary", "arbitrary"),
    )(dx, wd3, g, u, *travel)
    return outs[0], outs[1], list(outs[2:])


def _mm_nt_norm_bwd(das, ga, widxs, x, gain, dres, tm, name, travel=()):
    s = das[0].shape[0]
    nb, d, cols = ga.shape[1], ga.shape[2], ga.shape[3]
    nw = len(das)
    nsteps = s // tm

    def body(*refs):
        a_refs, b_refs = refs[:nw], refs[nw:2 * nw]
        x_ref, g_ref, dres_ref, dx_ref, dxb_ref, dg_ref = refs[2 * nw:]
        i = pl.program_id(0)
        dhv = None
        wide = BLOCK_PAIR * cols
        for w in range(nw):
            for k in range(nb // BLOCK_PAIR):
                b = jnp.concatenate([b_refs[w][BLOCK_PAIR * k + p] for p in range(BLOCK_PAIR)], axis=1)
                part = _dot(a_refs[w][:, k * wide:(k + 1) * wide], b, NT)
                dhv = part if dhv is None else dhv + part
        xv = x_ref[...]
        r = lax.rsqrt(jnp.mean(xv * xv, axis=-1, keepdims=True) + EPS)
        xhat = xv * r
        dxh = dhv * g_ref[...]
        dxv = dres_ref[...] + r * (dxh - xhat * jnp.mean(dxh * xhat, axis=-1, keepdims=True))
        dx_ref[...] = dxv
        dxb_ref[...] = dxv.astype(dxb_ref.dtype)
        part = jnp.sum((dhv * xhat).reshape(tm // 8, 8, d), axis=0)

        @pl.when(i == 0)
        def _():
            dg_ref[...] = part

        @pl.when(i > 0)
        def _():
            dg_ref[...] += part

        @pl.when(i == nsteps - 1)
        def _():
            dg_ref[...] = jnp.broadcast_to(jnp.sum(dg_ref[...], axis=0, keepdims=True), (8, d))

    def wspec(idx):
        return pl.BlockSpec((None, nb, d, cols), lambda i: (idx, 0, 0, 0), pipeline_mode=pl.Buffered(1))

    row = pl.BlockSpec((tm, d), lambda i: (i, 0))
    body, more_in, more_out, more_shapes, more_scratch = _host_exchange(body, 2 * nw + 3, 3, travel, (nsteps,))
    outs = pl.pallas_call(
        body, name=name, grid=(nsteps,),
        in_specs=([pl.BlockSpec((tm, nb * cols), lambda i: (i, 0))] * nw + [wspec(i) for i in widxs]
                  + [row, pl.BlockSpec((1, d), lambda i: (0, 0)), row] + more_in),
        out_specs=[row, row, pl.BlockSpec((8, d), lambda i: (0, 0))] + more_out,
        out_shape=[jax.ShapeDtypeStruct((s, d), F32), jax.ShapeDtypeStruct((s, d), BF16),
                   jax.ShapeDtypeStruct((8, d), F32)] + more_shapes,
        scratch_shapes=more_scratch,
        compiler_params=_cparams("arbitrary"),
    )(*das, *([ga] * nw), x, gain, dres, *travel)
    return outs[0], outs[1], outs[2], list(outs[3:])


def _mm_tn(a, b, ta, tb, tk, out_blocks, name, travel=()):
    s, ka = a.shape
    nb = b.shape[1]
    nk = s // tk
    cols = tb
    if out_blocks:
        tb = BLOCK_PAIR * cols

    def body(a_ref, b_ref, o_ref, ob_ref):
        k = pl.program_id(2)
        part = _dot(a_ref[...], b_ref[...], TN)

        def put(first):
            if out_blocks:
                for p in range(BLOCK_PAIR):
                    piece = part[:, p * cols:(p + 1) * cols]
                    o_ref[p] = piece if first else o_ref[p] + piece
            else:
                o_ref[...] = part if first else o_ref[...] + part

        @pl.when(k == 0)
        def _():
            put(True)

        @pl.when(k > 0)
        def _():
            put(False)

        @pl.when(k == nk - 1)
        def _():
            ob_ref[...] = o_ref[...].astype(ob_ref.dtype)

    if out_blocks:
        out_spec = pl.BlockSpec((BLOCK_PAIR, ta, cols), lambda i, j, k: (j, i, 0))
        shape = (nb // cols, ka, cols)
    else:
        out_spec = pl.BlockSpec((ta, tb), lambda i, j, k: (i, j))
        shape = (ka, nb)
    grid = (ka // ta, nb // tb, nk)
    body, more_in, more_out, more_shapes, more_scratch = _host_exchange(body, 2, 2, travel, grid)
    outs = pl.pallas_call(
        body, name=name, grid=grid,
        in_specs=[pl.BlockSpec((tk, ta), lambda i, j, k: (k, i)),
                  pl.BlockSpec((tk, tb), lambda i, j, k: (k, j))] + more_in,
        out_specs=[out_spec, out_spec] + more_out,
        out_shape=[jax.ShapeDtypeStruct(shape, F32), jax.ShapeDtypeStruct(shape, BF16)] + more_shapes,
        scratch_shapes=more_scratch,
        compiler_params=_cparams("arbitrary", "arbitrary", "arbitrary"),
    )(a, b, *travel)
    return (outs[0], outs[1]), list(outs[2:])


def _adamw(g, w, m, v, name):
    rows, cols = g.shape
    c1 = 1.0 / (1.0 - ADAM_B1 ** ADAM_STEP)
    c2 = 1.0 / (1.0 - ADAM_B2 ** ADAM_STEP)

    def body(p_ref, w_ref, m_ref, v_ref, g_ref, d_ref, nm_ref, nv_ref):
        gv = p_ref[...]
        nm = ADAM_B1 * m_ref[...] + (1.0 - ADAM_B1) * gv
        nv = ADAM_B2 * v_ref[...] + (1.0 - ADAM_B2) * (gv * gv)
        g_ref[...] = gv
        nm_ref[...] = nm
        nv_ref[...] = nv
        d_ref[...] = -ADAM_LR * ((nm * c1) / (jnp.sqrt(nv * c2) + ADAM_EPS) + ADAM_WD * w_ref[...])

    blk = pl.BlockSpec((rows, cols), lambda i: (0, 0))
    shape = jax.ShapeDtypeStruct((rows, cols), F32)
    return pl.pallas_call(
        body, name=name, grid=(1,),
        in_specs=[blk] * 4, out_specs=[blk] * 4, out_shape=[shape] * 4,
        compiler_params=_cparams("arbitrary"),
    )(g, w, m, v)


def _adamw_sharded(parts, grads, my, w, m, v, tr, name):
    depth, rows, cols = w.shape
    p, pr, pc = parts[0].shape
    c1 = 1.0 / (1.0 - ADAM_B1 ** ADAM_STEP)
    c2 = 1.0 / (1.0 - ADAM_B2 ** ADAM_STEP)

    def body(my_ref, *refs):
        p_refs, own_refs = refs[:depth], refs[depth:2 * depth]
        w_ref, m_ref, v_ref, g_ref, d_ref, nm_ref, nv_ref = refs[2 * depth:]
        layer = pl.program_id(0)
        for ll in range(depth):
            @pl.when(layer == ll)
            def _(ll=ll):
                mine = own_refs[ll][...]
                g = jnp.where(my_ref[0] == 0, mine, p_refs[ll][0].astype(F32))
                for k in range(1, p):
                    g = g + jnp.where(my_ref[0] == k, mine, p_refs[ll][k].astype(F32))
                g = g[:, :cols]
                nm = ADAM_B1 * m_ref[...] + (1.0 - ADAM_B1) * g
                nv = ADAM_B2 * v_ref[...] + (1.0 - ADAM_B2) * (g * g)
                g_ref[...] = g
                nm_ref[...] = nm
                nv_ref[...] = nv
                d_ref[...] = -ADAM_LR * ((nm * c1) / (jnp.sqrt(nv * c2) + ADAM_EPS) + ADAM_WD * w_ref[...])

    def row_block(ll, l, i):
        return jnp.where(l == ll, i, 0)

    blk = pl.BlockSpec((None, tr, cols), lambda l, i, my_: (l, i, 0))
    shape = jax.ShapeDtypeStruct((depth, rows, cols), F32)
    return pl.pallas_call(
        body, name=name,
        grid_spec=pltpu.PrefetchScalarGridSpec(
            num_scalar_prefetch=1, grid=(depth, rows // tr),
            in_specs=([pl.BlockSpec((p, tr, pc), lambda l, i, my_, ll=ll: (0, row_block(ll, l, i), 0))
                       for ll in range(depth)]
                      + [pl.BlockSpec((None, tr, pc), lambda l, i, my_, ll=ll: (my_[0], row_block(ll, l, i), 0))
                         for ll in range(depth)]
                      + [blk, blk, blk]),
            out_specs=[blk] * 4),
        out_shape=[shape] * 4,
        compiler_params=_cparams("arbitrary", "arbitrary"),
    )(my, *parts, *grads, w, m, v)


def _place():
    x, y, c = lax.axis_index("x"), lax.axis_index("y"), lax.axis_index("c")
    return x, y, c


class _Gather:
    def __init__(self, srcs, dsts, send_sems, recv_sems, local_sems):
        na = len(srcs)
        x, y, c = _place()
        me, sibling = (x, y, c), (x, y, 1 - c)
        chips = [(1 - x, y), (x, 1 - y), (1 - x, 1 - y)]

        def slot(a, dev):
            return dsts[a].at[:, pl.ds(4 * dev[0] + 2 * dev[1] + dev[2], 1)]

        def copy(k, a, block, to, from_shard=False):
            return pltpu.make_async_remote_copy(
                src_ref=srcs[a] if from_shard else slot(a, block), dst_ref=slot(a, block),
                send_sem=send_sems.at[k, a], recv_sem=recv_sems.at[k, a], device_id=to, device_id_type=MESH)

        pairs = [(j, chip, a) for j, chip in enumerate(chips) for a in range(na)]
        self.mine = [pltpu.make_async_copy(srcs[a], slot(a, me), local_sems.at[a]) for a in range(na)]
        self.first = [copy(0, a, me, sibling, True) for a in range(na)]
        self.first += [copy(1 + j, a, me, (*chip, c), True) for j, chip, a in pairs]
        self.over_ici = [copy(1 + j, a, (*chip, c), me) for j, chip, a in pairs]
        self.passed = [copy(4 + j, a, (*chip, c), sibling) for j, chip, a in pairs]
        self.from_sibling = [copy(0, a, sibling, me) for a in range(na)]
        self.from_sibling += [copy(4 + j, a, (*chip, 1 - c), me) for j, chip, a in pairs]

    def begin(self):
        for cp in self.mine + self.first:
            cp.start()

    def relay(self):
        for arrived, onward in zip(self.over_ici, self.passed):
            arrived.wait_recv()
            onward.start()

    def finish(self):
        for cp in self.from_sibling:
            cp.wait_recv()
        for cp in self.first + self.passed:
            cp.wait_send()
        for cp in self.mine:
            cp.wait()


def _gather_scratch(na):
    return [pltpu.SemaphoreType.DMA((7, na)), pltpu.SemaphoreType.DMA((7, na)), pltpu.SemaphoreType.DMA((na,))]


def _gathered_shapes(shards):
    return [jax.ShapeDtypeStruct((a.shape[0], N_DEV) + a.shape[2:], a.dtype) for a in shards]


_RELATIONS = [(dx, dy, dc) for dx in (0, 1) for dy in (0, 1) for dc in (0, 1)][1:]


def _flip(v, d):
    return 1 - v if d else v


def _exchange_copies(srcs, dsts, send_sems, recv_sems, local_sems):
    x, y, c = _place()
    my = 4 * x + 2 * y + c
    na = len(srcs)
    mine = [pltpu.make_async_copy(srcs[a].at[pl.ds(my, 1)], dsts[a].at[pl.ds(my, 1)], local_sems.at[a])
            for a in range(na)]
    sends, recvs = [], []
    for k, (dx, dy, dc) in enumerate(_RELATIONS):
        peer = (_flip(x, dx), _flip(y, dy), _flip(c, dc))
        pidx = 4 * peer[0] + 2 * peer[1] + peer[2]
        for a in range(na):
            for into, out in ((my, sends), (pidx, recvs)):
                out.append(pltpu.make_async_remote_copy(
                    src_ref=srcs[a].at[pl.ds(pidx, 1)], dst_ref=dsts[a].at[pl.ds(into, 1)],
                    send_sem=send_sems.at[k, a], recv_sem=recv_sems.at[k, a], device_id=peer, device_id_type=MESH))
    return mine, sends, recvs


def _exchange_begin(copies):
    mine, sends, _ = copies
    for cp in mine + sends:
        cp.start()


def _exchange_finish(copies):
    mine, sends, recvs = copies
    for cp in recvs:
        cp.wait_recv()
    for cp in sends:
        cp.wait_send()
    for cp in mine:
        cp.wait()


def _exchange_scratch(na):
    return [pltpu.SemaphoreType.DMA((7, na)), pltpu.SemaphoreType.DMA((7, na)), pltpu.SemaphoreType.DMA((na,))]


def _host_exchange(body, n_in, n_out, travel, grid):
    nt = len(travel)
    if not nt:
        return body, [], [], [], []

    def wrapped(*refs):
        ins, srcs = refs[:n_in], refs[n_in:n_in + nt]
        outs, rest = refs[n_in + nt:n_in + nt + n_out], refs[n_in + nt + n_out:]
        dsts, scratch = rest[:nt], rest[nt:]
        copies = _exchange_copies(srcs, dsts, *scratch[-3:])
        first = last = None
        for axis, size in enumerate(grid):
            at_start, at_end = pl.program_id(axis) == 0, pl.program_id(axis) == size - 1
            first = at_start if first is None else jnp.logical_and(first, at_start)
            last = at_end if last is None else jnp.logical_and(last, at_end)

        @pl.when(first)
        def _():
            _exchange_begin(copies)

        body(*ins, *outs, *scratch[:-3])

        @pl.when(last)
        def _():
            _exchange_finish(copies)

    anyspec = pl.BlockSpec(memory_space=pl.ANY)
    return (wrapped, [anyspec] * nt, [anyspec] * nt, [jax.ShapeDtypeStruct(t.shape, t.dtype) for t in travel],
            _exchange_scratch(nt))


def _all_reduce_small(v, name):
    r, c_ = v.shape

    def body(v_ref, o_ref, gath, send_sems, recv_sems):
        x, y, c = _place()
        my = 4 * x + 2 * y + c
        gath[my] = v_ref[...]
        sends = []
        for k, (dx, dy, dc) in enumerate(_RELATIONS):
            peer = (_flip(x, dx), _flip(y, dy), _flip(c, dc))
            cp = pltpu.make_async_remote_copy(
                src_ref=v_ref, dst_ref=gath.at[my], send_sem=send_sems.at[k], recv_sem=recv_sems.at[k],
                device_id=peer, device_id_type=MESH)
            cp.start()
            sends.append((cp, 4 * peer[0] + 2 * peer[1] + peer[2], k, peer))
        for cp, pidx, k, peer in sends:
            pltpu.make_async_remote_copy(
                src_ref=v_ref, dst_ref=gath.at[pidx], send_sem=send_sems.at[k], recv_sem=recv_sems.at[k],
                device_id=peer, device_id_type=MESH).wait_recv()
        for cp, *_ in sends:
            cp.wait_send()
        tot = gath[0]
        for k in range(1, N_DEV):
            tot = tot + gath[k]
        o_ref[...] = tot

    vm = pl.BlockSpec(memory_space=pltpu.VMEM)
    return pl.pallas_call(
        body, name=name, in_specs=[vm], out_specs=vm,
        out_shape=jax.ShapeDtypeStruct((r, c_), F32),
        scratch_shapes=[pltpu.VMEM((N_DEV, r, c_), F32), pltpu.SemaphoreType.DMA((7,)),
                        pltpu.SemaphoreType.DMA((7,))],
    )(v)


TM = 512
TM_MATMUL = 2048
TM_RESIDUAL = 1024
TQ = 256


def _device_blocks(t):
    return t.reshape(N_DEV, -1, t.shape[-1])


def _pad_to(a, axis, size):
    pad = [(0, 0)] * a.ndim
    pad[axis] = (0, size - a.shape[axis])
    return jnp.pad(a, pad)


def _local_step(x, target, first_shards, late_shards, conv_shard, norm_mix, q_norm, k_norm, norm_ffn):
    depth, d = norm_mix.shape
    cols = first_shards[0].shape[3]
    tm, tq = min(TM, x.shape[0]), min(TQ, x.shape[0])
    tmm, tmr = min(TM_MATMUL, x.shape[0]), min(TM_RESIDUAL, x.shape[0])
    attn = d // 2
    nheads = attn // HEAD_DIM
    scale = HEAD_DIM ** -0.5 * LOG2E
    saved = []
    h1, (g_in0, g_conv) = _rmsnorm_fwd(x, norm_mix[0][None], tm, "norm_mix_fwd_0", first_shards)
    conv_full = g_conv[0, :, :depth * 3, :conv_shard].transpose(1, 0, 2).reshape(depth, 3, N_DEV * conv_shard)
    for l in range(depth):
        w_in = (g_in0, 0) if l == 0 else (g_rest, 3 * (l - 1))
        proj = _mm_blocks(h1, *w_in, tmm, f"proj_in_{l}")
        qk_gain = jnp.concatenate([jnp.tile(q_norm[l], nheads) * scale, jnp.tile(k_norm[l], nheads)])[None]
        qk = _qknorm_fwd(proj, qk_gain, tmm, f"qknorm_fwd_{l}")
        o, rtot, used, gathered = _attn_fwd(qk, proj, tq, f"attn_fwd_{l}", d, late_shards if l == 0 else ())
        if l == 0:
            g_gu0, g_rest, gb, gc = gathered if depth > 1 else (gathered[0], None, *gathered[1:])
            gb = gb.reshape(depth, -1, d)
            gc = gc.reshape(depth, -1, d)
        w_gu = (g_gu0, 0, 1) if l == 0 else (g_rest, 3 * (l - 1) + 1, 3 * (l - 1) + 2)
        conv_w8 = _pad_to(conv_full[l], 0, 8)
        mix = _conv_fwd(proj, conv_w8, o, f"conv_fwd_{l}")
        x1, h2 = _mm_residual_norm(mix, gb, l, x, norm_ffn[l][None], tmr, f"proj_out_{l}")
        g, u, act = _mm_swiglu(h2, *w_gu, tmr, f"ffn_up_{l}")
        saved.append((x, h1, proj, qk_gain, qk, rtot, used, conv_w8, mix, x1, h2, g, u, act, w_in, w_gu))
        if l + 1 < depth:
            x, h1 = _mm_residual_norm(act, gc, l, x1, norm_mix[l + 1][None], tm, f"ffn_down_{l}")
        else:
            dx, dxb, loss = _mm_residual_loss(act, gc, l, x1, target, tm, f"ffn_down_{l}")

    grads = [None] * depth
    small = [None] * depth
    landed = {}
    for l in reversed(range(depth)):
        x0, h1, proj, qk_gain, qk, rtot, used, conv_w8, mix, x1, h2, g, u, act, w_in, w_gu = saved[l]
        d = x0.shape[1]
        late = [[], [], [], []]
        if l == 0:
            for n, i in enumerate(5 * ll + j for ll in range(1, depth) for j in (0, 1, 2, 4, 3)):
                late[n % 4].append(i)

        def ride(host):
            return late[host], [_device_blocks(grads[i // 5][i % 5][1]) for i in late[host]]

        idx, travel = ride(0)
        dg, du, arrived = _mm_nt_swiglu_bwd(dxb, gc, l, g, u, tmr, f"ffn_down_bwd_{l}", travel)
        landed.update(zip(idx, arrived))
        idx, travel = ride(1)
        d_wdown, arrived = _mm_tn(act, dxb, 768, d, tmm, False, f"dw_down_{l}", travel)
        landed.update(zip(idx, arrived))
        idx, travel = ride(2)
        d_wgate, arrived = _mm_tn(h2, dg, d, cols, tmm, True, f"dw_gate_{l}", travel)
        landed.update(zip(idx, arrived))
        idx, travel = ride(3)
        d_wup, arrived = _mm_tn(h2, du, d, cols, tmm, True, f"dw_up_{l}", travel)
        landed.update(zip(idx, arrived))
        dx1, dx1b, dg_ffn, _ = _mm_nt_norm_bwd([dg, du], w_gu[0], list(w_gu[1:]), x1, norm_ffn[l][None], dx, tm,
                                               f"ffn_up_bwd_{l}")
        dmix = _mm_nt(dx1b, gb, l, tmr, 512, f"proj_out_bwd_{l}")
        d_wout, _ = _mm_tn(mix, dx1b, 512, d, tmm, False, f"dw_out_{l}")
        dproj, dconv = _conv_bwd(dmix, proj, conv_w8, f"conv_bwd_{l}")
        rides = {1: d_wgate[1], 2: d_wup[1], 3: d_wout[1], 4: d_wdown[1]} if l == 0 else {}
        dq, dk, dproj, arrived = _attn_bwd(qk, proj, dmix, rtot, used, dproj, tq, f"attn_bwd_{l}",
                                           [_device_blocks(t) for t in rides.values()])
        landed.update(zip(rides.keys(), arrived))
        dproj, dg_qk = _qknorm_bwd(dq, dk, proj, qk_gain, dproj, tmm, f"qknorm_bwd_{l}")
        d_win, _ = _mm_tn(h1, dproj, d, cols, tmm, True, f"dw_in_{l}")
        dx, dxb, dg_mix, arrived = _mm_nt_norm_bwd(
            [dproj], w_in[0], [w_in[1]], x0, norm_mix[l][None], dx1, tm, f"proj_in_bwd_{l}",
            [_device_blocks(d_win[1])] if l == 0 else [])
        landed.update(zip([0], arrived))
        grads[l] = (d_win, d_wgate, d_wup, d_wout, d_wdown)
        dq_gain = jnp.sum(dg_qk[0, :attn].reshape(nheads, HEAD_DIM), axis=0) * scale
        dk_gain = jnp.sum(dg_qk[0, attn:].reshape(nheads, HEAD_DIM), axis=0)
        small[l] = (dg_mix[0], dg_ffn[0], dq_gain, dk_gain, dconv[:3])
    return loss, dx, grads, small, landed


def kernel(x, norm_mix, w_in, q_norm, k_norm, conv_w, w_out, norm_ffn, w_gate, w_up, w_down, loss_target, m_norm_mix, m_w_in, m_q_norm, m_k_norm, m_conv_w, m_w_out, m_norm_ffn, m_w_gate, m_w_up, m_w_down, v_norm_mix, v_w_in, v_q_norm, v_k_norm, v_conv_w, v_w_out, v_norm_ffn, v_w_gate, v_w_up, v_w_down):
    depth, d, in_shard = w_in.shape
    ff_shard = w_gate.shape[2]
    ff_pad = in_shard
    conv_shard = conv_w.shape[2]
    xs = x.reshape(x.shape[-2], d)
    target = loss_target.reshape(xs.shape)

    pa = jnp.stack([w_in, _pad_to(w_gate, 2, ff_pad), _pad_to(w_up, 2, ff_pad)], axis=1)
    pa = pa.reshape(3 * depth, 1, d, in_shard).astype(BF16)
    pd = _pad_to(_pad_to(conv_w.reshape(depth * 3, conv_shard), 0, 8), 1, LANES)[None, None]
    late_shards = [pa[1:3]] + ([pa[3:]] if depth > 1 else [])
    late_shards += [w_out.astype(BF16)[:, None], _pad_to(w_down, 1, ff_pad).astype(BF16)[:, None]]

    loss, grad_x, grads, small, landed = _local_step(xs, target, [pa[:1], pd], late_shards, conv_shard, norm_mix,
                                                     q_norm, k_norm, norm_ffn)

    x_, y_, c_ = _place()
    my = 4 * x_ + 2 * y_ + c_

    rows = []
    for l in range(depth):
        g_mix, g_ffn, g_q, g_k, g_conv = small[l]
        qkrow = _pad_to(jnp.concatenate([g_q, g_k]), 0, d)
        rows += [g_mix[None], g_ffn[None], qkrow[None], _pad_to(g_conv, 1, d)]
    nrow = 6 * depth
    packed = jnp.concatenate(rows + [_pad_to(loss[:1], 1, d)], axis=0)
    packed = _pad_to(packed, 0, ((nrow + 1 + 7) // 8) * 8)
    summed = _all_reduce_small(packed, "reduce_small")
    loss_out = summed[nrow, 0]

    my1 = my.astype(jnp.int32).reshape(1)

    def big(j, w, m, v, tr, name):
        return _adamw_sharded([landed[5 * l + j] for l in range(depth)],
                              [_device_blocks(grads[l][j][0]) for l in range(depth)], my1, w, m, v, tr, name)

    res = {"w_in": big(0, w_in, m_w_in, v_w_in, 256, "adamw_in"),
           "w_gate": big(1, w_gate, m_w_gate, v_w_gate, 256, "adamw_gate"),
           "w_up": big(2, w_up, m_w_up, v_w_up, 256, "adamw_up"),
           "w_out": big(3, w_out, m_w_out, v_w_out, w_out.shape[1], "adamw_out"),
           "w_down": big(4, w_down, m_w_down, v_w_down, ff_shard // 2, "adamw_down")}

    g_rows, w_rows, m_rows, v_rows = [], [], [], []
    for l in range(depth):
        base = l * 6
        conv_g = lax.dynamic_slice(summed[base + 3:base + 6], (0, my * conv_shard), (3, conv_shard))
        g_rows += [summed[base:base + 3], _pad_to(conv_g, 1, d)]
        for dst, (nm, qn, kn, nf, cw) in ((w_rows, (norm_mix, q_norm, k_norm, norm_ffn, conv_w)),
                                          (m_rows, (m_norm_mix, m_q_norm, m_k_norm, m_norm_ffn, m_conv_w)),
                                          (v_rows, (v_norm_mix, v_q_norm, v_k_norm, v_norm_ffn, v_conv_w))):
            dst += [nm[l][None], nf[l][None], _pad_to(jnp.concatenate([qn[l], kn[l]]), 0, d)[None],
                    _pad_to(cw[l], 1, d)]
    prow = ((nrow + 7) // 8) * 8
    gs, ws, ms, vs = [_pad_to(jnp.concatenate(t, axis=0), 0, prow) for t in (g_rows, w_rows, m_rows, v_rows)]
    sm = _adamw(gs, ws, ms, vs, "adamw_small")

    hd = q_norm.shape[1]

    def small_out(t, kind):
        per_layer = []
        for l in range(depth):
            base = l * 6
            per_layer.append({"norm_mix": t[base], "norm_ffn": t[base + 1], "q_norm": t[base + 2, :hd],
                              "k_norm": t[base + 2, hd:2 * hd], "conv_w": t[base + 3:base + 6, :conv_shard]}[kind])
        return jnp.stack(per_layer)

    def big_out(name, i):
        return res[name][i]

    outs = [loss_out, grad_x.reshape(x.shape)]
    for i in range(4):
        outs += [small_out(sm[i], "norm_mix"), big_out("w_in", i), small_out(sm[i], "q_norm"),
                 small_out(sm[i], "k_norm"), small_out(sm[i], "conv_w"), big_out("w_out", i),
                 small_out(sm[i], "norm_ffn"), big_out("w_gate", i), big_out("w_up", i), big_out("w_down", i)]
    return tuple(outs)
```
